```python
import math
import jax, jax.numpy as jnp
from jax import lax
import numpy as np

D_MODEL = 1024
BATCH = 8
SEQ = 2048
DEPTH = 2

N_MIXERS = 2
CHUNK = 128
N_MEM = 256
D_INNER = 2 * D_MODEL
A_GROUPS = 8
SSM_HEAD_DIM = 64
SSM_HEADS = D_INNER // SSM_HEAD_DIM
SSM_GROUPS = 4
SSM_HPG = SSM_HEADS // SSM_GROUPS
SSM_STATE = 128
CONV_K = 4
CONV_DIM = D_INNER + 2 * SSM_GROUPS * SSM_STATE
X_HEADS = 4
X_HEAD_DIM = 256
X_WIDTH = X_HEADS * X_HEAD_DIM
MIX_OUT = D_INNER + X_WIDTH
D_FF = 4 * D_MODEL
A_IN = 2 * D_INNER + X_WIDTH
B_IN = D_INNER + CONV_DIM + SSM_HEADS + X_WIDTH
EPS = 1e-6

kernel_name = "hybrid_gmlp_ssd_memxattn"


def rms_norm(x, g):
    xf = x.astype(jnp.float32)
    y = xf * lax.rsqrt(jnp.mean(xf * xf, axis=-1, keepdims=True) + EPS)
    return (y * g.astype(jnp.float32)).astype(x.dtype)


def layer_norm(x, g, b):
    xf = x.astype(jnp.float32)
    mu = jnp.mean(xf, axis=-1, keepdims=True)
    xc = xf - mu
    y = xc * lax.rsqrt(jnp.mean(xc * xc, axis=-1, keepdims=True) + EPS)
    return (y * g.astype(jnp.float32) + b.astype(jnp.float32)).astype(x.dtype)


def gmlp_spatial_gating(u, v, ln_g, ln_b, ws, bs):
    bn, s, _ = u.shape
    v = layer_norm(v, ln_g, ln_b)
    v = v.reshape(bn, s // CHUNK, CHUNK, A_GROUPS, D_INNER // A_GROUPS)
    causal = jnp.tril(jnp.ones((CHUNK, CHUNK), dtype=bool))
    w = jnp.where(causal[None], ws, jnp.zeros_like(ws))
    sv = jnp.einsum('gts,bcsgd->bctgd', w, v) + bs.T[:, :, None]
    return u * sv.reshape(bn, s, D_INNER)


def causal_dwconv(x, w, b):
    y = lax.conv_general_dilated(
        x, w[:, None, :], window_strides=(1,), padding=[(CONV_K - 1, 0)],
        dimension_numbers=('NWC', 'WIO', 'NWC'), feature_group_count=x.shape[-1])
    return y + b


def ssd_mixer(zxbcdt, conv_w, conv_b, dt_bias, a_log, d_skip, gnorm):
    bn, s, _ = zxbcdt.shape
    nc = s // CHUNK
    z = zxbcdt[..., :D_INNER]
    xbc = zxbcdt[..., D_INNER:D_INNER + CONV_DIM]
    dt = zxbcdt[..., D_INNER + CONV_DIM:]
    xbc = jax.nn.silu(causal_dwconv(xbc, conv_w, conv_b))
    gn = SSM_GROUPS * SSM_STATE
    xs = xbc[..., :D_INNER].astype(jnp.float32)
    bm = xbc[..., D_INNER:D_INNER + gn].astype(jnp.float32)
    cm = xbc[..., D_INNER + gn:].astype(jnp.float32)

    x = xs.reshape(bn, nc, CHUNK, SSM_GROUPS, SSM_HPG, SSM_HEAD_DIM)
    bm = bm.reshape(bn, nc, CHUNK, SSM_GROUPS, SSM_STATE)
    cm = cm.reshape(bn, nc, CHUNK, SSM_GROUPS, SSM_STATE)
    dt = jax.nn.softplus(dt.astype(jnp.float32) + dt_bias.astype(jnp.float32))
    dt = dt.reshape(bn, nc, CHUNK, SSM_GROUPS, SSM_HPG)
    a = -jnp.exp(a_log.astype(jnp.float32)).reshape(SSM_GROUPS, SSM_HPG)
    da = jnp.transpose(dt * a, (0, 3, 4, 1, 2))
    xdt = x * dt[..., None]

    cs = jnp.cumsum(da, axis=-1)
    causal = jnp.tril(jnp.ones((CHUNK, CHUNK), dtype=bool))
    seg = cs[..., :, None] - cs[..., None, :]
    lmat = jnp.exp(jnp.where(causal, seg, -jnp.inf))

    cb = jnp.einsum('bclgn,bcsgn->bcgls', cm, bm)
    y_diag = jnp.einsum('bcgls,bgrcls,bcsgrp->bclgrp', cb, lmat, xdt)

    decay_states = jnp.exp(cs[..., -1:] - cs)
    states = jnp.einsum('bclgn,bgrcl,bclgrp->bcgrpn', bm, decay_states, xdt)
    chunk_decay = jnp.exp(cs[..., -1])

    def step(h, inp):
        st, dec = inp
        return h * dec[..., None, None] + st, h

    h0 = jnp.zeros((bn, SSM_GROUPS, SSM_HPG, SSM_HEAD_DIM, SSM_STATE), jnp.float32)
    _, prev = lax.scan(step, h0, (jnp.moveaxis(states, 1, 0), jnp.moveaxis(chunk_decay, 3, 0)))
    prev = jnp.moveaxis(prev, 0, 1)

    y_off = jnp.einsum('bclgn,bcgrpn,bgrcl->bclgrp', cm, prev, jnp.exp(cs))
    y = y_diag + y_off + x * d_skip.astype(jnp.float32).reshape(SSM_GROUPS, SSM_HPG)[..., None]
    y = y.reshape(bn, s, D_INNER)

    yg = (y * jax.nn.silu(z.astype(jnp.float32))).reshape(bn, s, SSM_GROUPS, D_INNER // SSM_GROUPS)
    yg = yg * lax.rsqrt(jnp.mean(yg * yg, axis=-1, keepdims=True) + EPS)
    y = yg.reshape(bn, s, D_INNER) * gnorm.astype(jnp.float32)
    return y.astype(zxbcdt.dtype)


def memory_attention(q, mem, mem_g, w_kv):
    bn, s, _ = q.shape
    m = rms_norm(mem, mem_g)
    kv = m @ w_kv
    k = kv[..., :X_WIDTH].reshape(bn, N_MEM, X_HEADS, X_HEAD_DIM)
    v = kv[..., X_WIDTH:].reshape(bn, N_MEM, X_HEADS, X_HEAD_DIM)
    qh = q.reshape(bn, s, X_HEADS, X_HEAD_DIM)
    sc = jnp.einsum('bshd,bmhd->bhsm', qh, k).astype(jnp.float32) * (1.0 / math.sqrt(X_HEAD_DIM))
    p = jax.nn.softmax(sc, axis=-1).astype(v.dtype)
    o = jnp.einsum('bhsm,bmhd->bshd', p, v)
    return o.reshape(bn, s, X_WIDTH)


def _fwd_setup_inputs(seed: int = 0) -> dict:
    key = jax.random.key(seed)
    ks = jax.random.split(key, 32)
    na = (DEPTH + 1) // 2
    nb = DEPTH // 2
    f32 = jnp.float32

    def nrm(k, shape, scale):
        return jax.random.normal(k, shape, f32) * scale

    def gain(k, shape):
        return 1.0 + 0.02 * jax.random.normal(k, shape, f32)

    dt0 = jnp.exp(jax.random.uniform(ks[20], (nb, SSM_HEADS), f32, math.log(1e-3), math.log(1e-1)))
    return {
        "x": jax.random.normal(ks[0], (BATCH, SEQ, D_MODEL), f32),
        "mem": jax.random.normal(ks[1], (BATCH, N_MEM, D_MODEL), f32),
        "norm_mix": gain(ks[2], (DEPTH, D_MODEL)),
        "norm_ffn": gain(ks[3], (DEPTH, D_MODEL)),
        "mem_norm": gain(ks[4], (DEPTH, D_MODEL)),
        "w_kv": nrm(ks[5], (DEPTH, D_MODEL, 2 * X_WIDTH), D_MODEL ** -0.5),
        "w_out": nrm(ks[6], (DEPTH, MIX_OUT, D_MODEL), MIX_OUT ** -0.5),
        "w_ffn1": nrm(ks[7], (DEPTH, D_MODEL, D_FF), D_MODEL ** -0.5),
        "w_ffn2": nrm(ks[8], (DEPTH, D_FF, D_MODEL), D_FF ** -0.5),
        "a_in": nrm(ks[9], (na, D_MODEL, A_IN), D_MODEL ** -0.5),
        "a_ln_g": gain(ks[10], (na, D_INNER)),
        "a_ln_b": nrm(ks[11], (na, D_INNER), 0.02),
        "a_ws": nrm(ks[12], (na, A_GROUPS, CHUNK, CHUNK), 0.5 * CHUNK ** -0.5),
        "a_bs": gain(ks[13], (na, A_GROUPS, CHUNK)),
        "b_in": nrm(ks[14], (nb, D_MODEL, B_IN), D_MODEL ** -0.5),
        "b_conv_w": nrm(ks[15], (nb, CONV_K, CONV_DIM), CONV_K ** -0.5),
        "b_conv_b": nrm(ks[16], (nb, CONV_DIM), 0.02),
        "b_dt_bias": dt0 + jnp.log(-jnp.expm1(-dt0)),
        "b_a_log": jnp.log(jax.random.uniform(ks[17], (nb, SSM_HEADS), f32, 1.0, 16.0)),
        "b_d": gain(ks[18], (nb, SSM_HEADS)),
        "b_gnorm": gain(ks[19], (nb, D_INNER)),
        "final_norm": gain(ks[21], (D_MODEL,)),
    }


def _fwd_reference(x, mem, norm_mix, norm_ffn, mem_norm, w_kv, w_out, w_ffn1, w_ffn2,
              a_in, a_ln_g, a_ln_b, a_ws, a_bs,
              b_in, b_conv_w, b_conv_b, b_dt_bias, b_a_log, b_d, b_gnorm,
              final_norm):
    h = x
    for i in range(DEPTH):
        j = i // N_MIXERS
        a = rms_norm(h, norm_mix[i])
        if i % N_MIXERS == 0:
            proj = a @ a_in[j]
            u = jax.nn.gelu(proj[..., :D_INNER], approximate=False)
            v = jax.nn.gelu(proj[..., D_INNER:2 * D_INNER], approximate=False)
            q = proj[..., 2 * D_INNER:]
            mix = gmlp_spatial_gating(u, v, a_ln_g[j], a_ln_b[j], a_ws[j], a_bs[j])
        else:
            proj = a @ b_in[j]
            ssm_cols = D_INNER + CONV_DIM + SSM_HEADS
            mix = ssd_mixer(proj[..., :ssm_cols], b_conv_w[j], b_conv_b[j], b_dt_bias[j],
                            b_a_log[j], b_d[j], b_gnorm[j])
            q = proj[..., ssm_cols:]
        mo = memory_attention(q, mem, mem_norm[i], w_kv[i])
        h = h + jnp.concatenate([mix, mo], axis=-1) @ w_out[i]
        f = rms_norm(h, norm_ffn[i])
        h = h + jnp.square(jax.nn.relu(f @ w_ffn1[i])) @ w_ffn2[i]
    return rms_norm(h, final_norm)


import jax as _jax
import jax.numpy as _jnp

TWIN_FORMAT = 'train_step'
FWD_PARAMS = ['x', 'mem', 'norm_mix', 'norm_ffn', 'mem_norm', 'w_kv', 'w_out', 'w_ffn1', 'w_ffn2', 'a_in', 'a_ln_g', 'a_ln_b', 'a_ws', 'a_bs', 'b_in', 'b_conv_w', 'b_conv_b', 'b_dt_bias', 'b_a_log', 'b_d', 'b_gnorm', 'final_norm']
TWIN_WEIGHTS = ['norm_mix', 'norm_ffn', 'mem_norm', 'w_kv', 'w_out', 'w_ffn1', 'w_ffn2', 'a_in', 'a_ln_g', 'a_ln_b', 'a_ws', 'a_bs', 'b_in', 'b_conv_w', 'b_conv_b', 'b_dt_bias', 'b_a_log', 'b_d', 'b_gnorm', 'final_norm']
TWIN_DIFF_INPUT = 'x'
TWIN_INPUTS = ['x', 'mem', 'norm_mix', 'norm_ffn', 'mem_norm', 'w_kv', 'w_out', 'w_ffn1', 'w_ffn2', 'a_in', 'a_ln_g', 'a_ln_b', 'a_ws', 'a_bs', 'b_in', 'b_conv_w', 'b_conv_b', 'b_dt_bias', 'b_a_log', 'b_d', 'b_gnorm', 'final_norm', 'loss_target', 'm_norm_mix', 'm_norm_ffn', 'm_mem_norm', 'm_w_kv', 'm_w_out', 'm_w_ffn1', 'm_w_ffn2', 'm_a_in', 'm_a_ln_g', 'm_a_ln_b', 'm_a_ws', 'm_a_bs', 'm_b_in', 'm_b_conv_w', 'm_b_conv_b', 'm_b_dt_bias', 'm_b_a_log', 'm_b_d', 'm_b_gnorm', 'm_final_norm', 'v_norm_mix', 'v_norm_ffn', 'v_mem_norm', 'v_w_kv', 'v_w_out', 'v_w_ffn1', 'v_w_ffn2', 'v_a_in', 'v_a_ln_g', 'v_a_ln_b', 'v_a_ws', 'v_a_bs', 'v_b_in', 'v_b_conv_w', 'v_b_conv_b', 'v_b_dt_bias', 'v_b_a_log', 'v_b_d', 'v_b_gnorm', 'v_final_norm']
TWIN_OUTPUTS = ['loss', 'grad_x', 'grad_norm_mix', 'grad_norm_ffn', 'grad_mem_norm', 'grad_w_kv', 'grad_w_out', 'grad_w_ffn1', 'grad_w_ffn2', 'grad_a_in', 'grad_a_ln_g', 'grad_a_ln_b', 'grad_a_ws', 'grad_a_bs', 'grad_b_in', 'grad_b_conv_w', 'grad_b_conv_b', 'grad_b_dt_bias', 'grad_b_a_log', 'grad_b_d', 'grad_b_gnorm', 'grad_final_norm', 'delta_norm_mix', 'delta_norm_ffn', 'delta_mem_norm', 'delta_w_kv', 'delta_w_out', 'delta_w_ffn1', 'delta_w_ffn2', 'delta_a_in', 'delta_a_ln_g', 'delta_a_ln_b', 'delta_a_ws', 'delta_a_bs', 'delta_b_in', 'delta_b_conv_w', 'delta_b_conv_b', 'delta_b_dt_bias', 'delta_b_a_log', 'delta_b_d', 'delta_b_gnorm', 'delta_final_norm', 'new_m_norm_mix', 'new_m_norm_ffn', 'new_m_mem_norm', 'new_m_w_kv', 'new_m_w_out', 'new_m_w_ffn1', 'new_m_w_ffn2', 'new_m_a_in', 'new_m_a_ln_g', 'new_m_a_ln_b', 'new_m_a_ws', 'new_m_a_bs', 'new_m_b_in', 'new_m_b_conv_w', 'new_m_b_conv_b', 'new_m_b_dt_bias', 'new_m_b_a_log', 'new_m_b_d', 'new_m_b_gnorm', 'new_m_final_norm', 'new_v_norm_mix', 'new_v_norm_ffn', 'new_v_mem_norm', 'new_v_w_kv', 'new_v_w_out', 'new_v_w_ffn1', 'new_v_w_ffn2', 'new_v_a_in', 'new_v_a_ln_g', 'new_v_a_ln_b', 'new_v_a_ws', 'new_v_a_bs', 'new_v_b_in', 'new_v_b_conv_w', 'new_v_b_conv_b', 'new_v_b_dt_bias', 'new_v_b_a_log', 'new_v_b_d', 'new_v_b_gnorm', 'new_v_final_norm']
TWIN_LEAF_KINDS = {'loss': 'loss', 'grad_x': 'grad_x', 'grad_norm_mix': 'grad_w', 'grad_norm_ffn': 'grad_w', 'grad_mem_norm': 'grad_w', 'grad_w_kv': 'grad_w', 'grad_w_out': 'grad_w', 'grad_w_ffn1': 'grad_w', 'grad_w_ffn2': 'grad_w', 'grad_a_in': 'grad_w', 'grad_a_ln_g': 'grad_w', 'grad_a_ln_b': 'grad_w', 'grad_a_ws': 'grad_w', 'grad_a_bs': 'grad_w', 'grad_b_in': 'grad_w', 'grad_b_conv_w': 'grad_w', 'grad_b_conv_b': 'grad_w', 'grad_b_dt_bias': 'grad_w', 'grad_b_a_log': 'grad_w', 'grad_b_d': 'grad_w', 'grad_b_gnorm': 'grad_w', 'grad_final_norm': 'grad_w', 'delta_norm_mix': 'delta_w', 'delta_norm_ffn': 'delta_w', 'delta_mem_norm': 'delta_w', 'delta_w_kv': 'delta_w', 'delta_w_out': 'delta_w', 'delta_w_ffn1': 'delta_w', 'delta_w_ffn2': 'delta_w', 'delta_a_in': 'delta_w', 'delta_a_ln_g': 'delta_w', 'delta_a_ln_b': 'delta_w', 'delta_a_ws': 'delta_w', 'delta_a_bs': 'delta_w', 'delta_b_in': 'delta_w', 'delta_b_conv_w': 'delta_w', 'delta_b_conv_b': 'delta_w', 'delta_b_dt_bias': 'delta_w', 'delta_b_a_log': 'delta_w', 'delta_b_d': 'delta_w', 'delta_b_gnorm': 'delta_w', 'delta_final_norm': 'delta_w', 'new_m_norm_mix': 'new_m', 'new_m_norm_ffn': 'new_m', 'new_m_mem_norm': 'new_m', 'new_m_w_kv': 'new_m', 'new_m_w_out': 'new_m', 'new_m_w_ffn1': 'new_m', 'new_m_w_ffn2': 'new_m', 'new_m_a_in': 'new_m', 'new_m_a_ln_g': 'new_m', 'new_m_a_ln_b': 'new_m', 'new_m_a_ws': 'new_m', 'new_m_a_bs': 'new_m', 'new_m_b_in': 'new_m', 'new_m_b_conv_w': 'new_m', 'new_m_b_conv_b': 'new_m', 'new_m_b_dt_bias': 'new_m', 'new_m_b_a_log': 'new_m', 'new_m_b_d': 'new_m', 'new_m_b_gnorm': 'new_m', 'new_m_final_norm': 'new_m', 'new_v_norm_mix': 'new_v', 'new_v_norm_ffn': 'new_v', 'new_v_mem_norm': 'new_v', 'new_v_w_kv': 'new_v', 'new_v_w_out': 'new_v', 'new_v_w_ffn1': 'new_v', 'new_v_w_ffn2': 'new_v', 'new_v_a_in': 'new_v', 'new_v_a_ln_g': 'new_v', 'new_v_a_ln_b': 'new_v', 'new_v_a_ws': 'new_v', 'new_v_a_bs': 'new_v', 'new_v_b_in': 'new_v', 'new_v_b_conv_w': 'new_v', 'new_v_b_conv_b': 'new_v', 'new_v_b_dt_bias': 'new_v', 'new_v_b_a_log': 'new_v', 'new_v_b_d': 'new_v', 'new_v_b_gnorm': 'new_v', 'new_v_final_norm': 'new_v'}


def _forward(args):
    return _fwd_reference(*[args[k] for k in FWD_PARAMS])


def _output_shape():
    out = _jax.eval_shape(lambda: _forward(_fwd_setup_inputs(0)))
    return out.shape, out.dtype

N_MICROBATCH = 1
ADAM_LR = 0.001
ADAM_B1 = 0.9
ADAM_B2 = 0.999
ADAM_EPS = 1e-08
ADAM_WD = 0.01
ADAM_STEP = 10
PER_EXAMPLE_BATCH_AXIS = {'x': 0, 'mem': 0, 'loss_target': 0}
SHARED_INPUTS = []
_WEIGHT_DTYPES = {'norm_mix': _jnp.float32, 'norm_ffn': _jnp.float32, 'mem_norm': _jnp.float32, 'w_kv': _jnp.float32, 'w_out': _jnp.float32, 'w_ffn1': _jnp.float32, 'w_ffn2': _jnp.float32, 'a_in': _jnp.float32, 'a_ln_g': _jnp.float32, 'a_ln_b': _jnp.float32, 'a_ws': _jnp.float32, 'a_bs': _jnp.float32, 'b_in': _jnp.float32, 'b_conv_w': _jnp.float32, 'b_conv_b': _jnp.float32, 'b_dt_bias': _jnp.float32, 'b_a_log': _jnp.float32, 'b_d': _jnp.float32, 'b_gnorm': _jnp.float32, 'final_norm': _jnp.float32}
MOMENT_SCALE = {'norm_mix': 8.646236e-02, 'norm_ffn': 1.079170e-01, 'mem_norm': 9.964151e-03, 'w_kv': 6.804433e-03, 'w_out': 7.238538e-02, 'w_ffn1': 5.228283e-02, 'w_ffn2': 1.000377e-01, 'a_in': 3.862253e-02, 'a_ln_g': 1.910331e-02, 'a_ln_b': 1.896489e-02, 'a_ws': 5.188797e-02, 'a_bs': 7.183682e-02, 'b_in': 3.547144e-02, 'b_conv_w': 3.584637e-02, 'b_conv_b': 4.765793e-02, 'b_dt_bias': 8.043986e-02, 'b_a_log': 1.304628e-01, 'b_d': 2.079064e-01, 'b_gnorm': 4.080385e-02, 'final_norm': 1.631370e+01}


def _to_microbatches(a, axis):
    t = _jnp.moveaxis(a, axis, 0)
    t = t.reshape((N_MICROBATCH, t.shape[0] // N_MICROBATCH) + t.shape[1:])
    return _jnp.moveaxis(t, 1, axis + 1)


def setup_inputs(seed: int = 0) -> dict:
    inp = _fwd_setup_inputs(seed)
    key = _jax.random.fold_in(_jax.random.key(seed), 7919)
    shape, _ = _output_shape()
    out = dict(inp)
    out["loss_target"] = _jax.random.normal(_jax.random.fold_in(key, 0), shape, _jnp.float32)
    for i, name in enumerate(TWIN_WEIGHTS):
        w = inp[name].astype(_jnp.float32)
        if MOMENT_SCALE is None:
            s = _jnp.sqrt(_jnp.mean(_jnp.square(w)) + 1e-30)
        else:
            s = MOMENT_SCALE[name]
        km, kv = _jax.random.split(_jax.random.fold_in(key, i + 1))
        out[name] = w
        out["m_" + name] = s * _jax.random.normal(km, w.shape, _jnp.float32)
        out["v_" + name] = (s * s) * _jax.random.uniform(kv, w.shape, _jnp.float32, 0.5, 1.5)
    if N_MICROBATCH > 1:
        for name, axis in PER_EXAMPLE_BATCH_AXIS.items():
            out[name] = _to_microbatches(out[name], axis)
    return {'x': out['x'], 'mem': out['mem'], 'norm_mix': out['norm_mix'], 'norm_ffn': out['norm_ffn'], 'mem_norm': out['mem_norm'], 'w_kv': out['w_kv'], 'w_out': out['w_out'], 'w_ffn1': out['w_ffn1'], 'w_ffn2': out['w_ffn2'], 'a_in': out['a_in'], 'a_ln_g': out['a_ln_g'], 'a_ln_b': out['a_ln_b'], 'a_ws': out['a_ws'], 'a_bs': out['a_bs'], 'b_in': out['b_in'], 'b_conv_w': out['b_conv_w'], 'b_conv_b': out['b_conv_b'], 'b_dt_bias': out['b_dt_bias'], 'b_a_log': out['b_a_log'], 'b_d': out['b_d'], 'b_gnorm': out['b_gnorm'], 'final_norm': out['final_norm'], 'loss_target': out['loss_target'], 'm_norm_mix': out['m_norm_mix'], 'm_norm_ffn': out['m_norm_ffn'], 'm_mem_norm': out['m_mem_norm'], 'm_w_kv': out['m_w_kv'], 'm_w_out': out['m_w_out'], 'm_w_ffn1': out['m_w_ffn1'], 'm_w_ffn2': out['m_w_ffn2'], 'm_a_in': out['m_a_in'], 'm_a_ln_g': out['m_a_ln_g'], 'm_a_ln_b': out['m_a_ln_b'], 'm_a_ws': out['m_a_ws'], 'm_a_bs': out['m_a_bs'], 'm_b_in': out['m_b_in'], 'm_b_conv_w': out['m_b_conv_w'], 'm_b_conv_b': out['m_b_conv_b'], 'm_b_dt_bias': out['m_b_dt_bias'], 'm_b_a_log': out['m_b_a_log'], 'm_b_d': out['m_b_d'], 'm_b_gnorm': out['m_b_gnorm'], 'm_final_norm': out['m_final_norm'], 'v_norm_mix': out['v_norm_mix'], 'v_norm_ffn': out['v_norm_ffn'], 'v_mem_norm': out['v_mem_norm'], 'v_w_kv': out['v_w_kv'], 'v_w_out': out['v_w_out'], 'v_w_ffn1': out['v_w_ffn1'], 'v_w_ffn2': out['v_w_ffn2'], 'v_a_in': out['v_a_in'], 'v_a_ln_g': out['v_a_ln_g'], 'v_a_ln_b': out['v_a_ln_b'], 'v_a_ws': out['v_a_ws'], 'v_a_bs': out['v_a_bs'], 'v_b_in': out['v_b_in'], 'v_b_conv_w': out['v_b_conv_w'], 'v_b_conv_b': out['v_b_conv_b'], 'v_b_dt_bias': out['v_b_dt_bias'], 'v_b_a_log': out['v_b_a_log'], 'v_b_d': out['v_b_d'], 'v_b_gnorm': out['v_b_gnorm'], 'v_final_norm': out['v_final_norm']}


def _loss(weights, diff, rest, loss_target):
    with _jax.named_scope("forward"):
        args = {**rest, TWIN_DIFF_INPUT: diff, **{k: w.astype(_WEIGHT_DTYPES[k]) for k, w in weights.items()}}
        y = _forward(args)
    with _jax.named_scope("loss_head"):
        err = _jnp.square(y.astype(_jnp.float32) - loss_target)
        return 0.5 * _jnp.sum(_jnp.mean(err, axis=-1)) if err.ndim else 0.5 * err


def _adamw(w, g, m, v):
    m = ADAM_B1 * m + (1.0 - ADAM_B1) * g
    v = ADAM_B2 * v + (1.0 - ADAM_B2) * _jnp.square(g)
    m_hat = m / (1.0 - ADAM_B1 ** ADAM_STEP)
    v_hat = v / (1.0 - ADAM_B2 ** ADAM_STEP)
    delta = -ADAM_LR * (m_hat / (_jnp.sqrt(v_hat) + ADAM_EPS) + ADAM_WD * w)
    return delta, m, v


def reference(x, mem, norm_mix, norm_ffn, mem_norm, w_kv, w_out, w_ffn1, w_ffn2, a_in, a_ln_g, a_ln_b, a_ws, a_bs, b_in, b_conv_w, b_conv_b, b_dt_bias, b_a_log, b_d, b_gnorm, final_norm, loss_target, m_norm_mix, m_norm_ffn, m_mem_norm, m_w_kv, m_w_out, m_w_ffn1, m_w_ffn2, m_a_in, m_a_ln_g, m_a_ln_b, m_a_ws, m_a_bs, m_b_in, m_b_conv_w, m_b_conv_b, m_b_dt_bias, m_b_a_log, m_b_d, m_b_gnorm, m_final_norm, v_norm_mix, v_norm_ffn, v_mem_norm, v_w_kv, v_w_out, v_w_ffn1, v_w_ffn2, v_a_in, v_a_ln_g, v_a_ln_b, v_a_ws, v_a_bs, v_b_in, v_b_conv_w, v_b_conv_b, v_b_dt_bias, v_b_a_log, v_b_d, v_b_gnorm, v_final_norm):
    given = dict(x=x, mem=mem, norm_mix=norm_mix, norm_ffn=norm_ffn, mem_norm=mem_norm, w_kv=w_kv, w_out=w_out, w_ffn1=w_ffn1, w_ffn2=w_ffn2, a_in=a_in, a_ln_g=a_ln_g, a_ln_b=a_ln_b, a_ws=a_ws, a_bs=a_bs, b_in=b_in, b_conv_w=b_conv_w, b_conv_b=b_conv_b, b_dt_bias=b_dt_bias, b_a_log=b_a_log, b_d=b_d, b_gnorm=b_gnorm, final_norm=final_norm, loss_target=loss_target, m_norm_mix=m_norm_mix, m_norm_ffn=m_norm_ffn, m_mem_norm=m_mem_norm, m_w_kv=m_w_kv, m_w_out=m_w_out, m_w_ffn1=m_w_ffn1, m_w_ffn2=m_w_ffn2, m_a_in=m_a_in, m_a_ln_g=m_a_ln_g, m_a_ln_b=m_a_ln_b, m_a_ws=m_a_ws, m_a_bs=m_a_bs, m_b_in=m_b_in, m_b_conv_w=m_b_conv_w, m_b_conv_b=m_b_conv_b, m_b_dt_bias=m_b_dt_bias, m_b_a_log=m_b_a_log, m_b_d=m_b_d, m_b_gnorm=m_b_gnorm, m_final_norm=m_final_norm, v_norm_mix=v_norm_mix, v_norm_ffn=v_norm_ffn, v_mem_norm=v_mem_norm, v_w_kv=v_w_kv, v_w_out=v_w_out, v_w_ffn1=v_w_ffn1, v_w_ffn2=v_w_ffn2, v_a_in=v_a_in, v_a_ln_g=v_a_ln_g, v_a_ln_b=v_a_ln_b, v_a_ws=v_a_ws, v_a_bs=v_a_bs, v_b_in=v_b_in, v_b_conv_w=v_b_conv_w, v_b_conv_b=v_b_conv_b, v_b_dt_bias=v_b_dt_bias, v_b_a_log=v_b_a_log, v_b_d=v_b_d, v_b_gnorm=v_b_gnorm, v_final_norm=v_final_norm)
    weights = {n: given[n] for n in TWIN_WEIGHTS}
    shared = {n: given[n] for n in SHARED_INPUTS}
    per_example = {n: given[n] for n in ['x', 'mem']}
    grad_fn = _jax.value_and_grad(_loss, argnums=(0, 1))

    def one_microbatch(ex, loss_target):
        ex = dict(ex)
        diff = ex.pop(TWIN_DIFF_INPUT)
        return grad_fn(weights, diff, {**shared, **ex}, loss_target)

    if N_MICROBATCH == 1:
        loss, (grad_w, grad_x) = one_microbatch(per_example, given["loss_target"])
    else:
        def body(carry, xs):
            loss_sum, grad_sum = carry
            l_k, (gw_k, gx_k) = one_microbatch(xs[0], xs[1])
            with _jax.named_scope("update"):
                return (loss_sum + l_k, _jax.tree.map(_jnp.add, grad_sum, gw_k)), gx_k

        init = (_jnp.zeros((), _jnp.float32), _jax.tree.map(_jnp.zeros_like, weights))
        (loss, grad_w), grad_x = _jax.lax.scan(body, init, (per_example, given["loss_target"]))
    with _jax.named_scope("update"):
        delta_w, new_m, new_v = {}, {}, {}
        for n in TWIN_WEIGHTS:
            delta_w[n], new_m[n], new_v[n] = _adamw(weights[n], grad_w[n], given["m_" + n], given["v_" + n])
    return (loss, grad_x, *[grad_w[n] for n in TWIN_WEIGHTS], *[delta_w[n] for n in TWIN_WEIGHTS],
            *[new_m[n] for n in TWIN_WEIGHTS], *[new_v[n] for n in TWIN_WEIGHTS])
```

```python
import functools
import math

import jax
import jax.numpy as jnp
from jax import lax
from jax.experimental import pallas as pl
from jax.experimental.pallas import tpu as pltpu

F32 = jnp.float32
_MXU = jnp.bfloat16
_ACT = jnp.bfloat16
_HI = lax.Precision.HIGHEST

D_MODEL = 1024
CHUNK = 128
N_MEM = 256
D_INNER = 2048
A_GROUPS = 8
A_GW = D_INNER // A_GROUPS
SSM_HEADS = 32
SSM_P = 64
SSM_GROUPS = 4
SSM_GW = D_INNER // SSM_GROUPS
SSM_N = 128
CONV_K = 4
CONV_DIM = 3072
X_HEADS = 4
X_HD = 256
X_WIDTH = 1024
D_FF = 4096
EPS = 1e-6
HPAD = 128
N_DEV = 8

ADAM_LR = 0.001
ADAM_B1 = 0.9
ADAM_B2 = 0.999
ADAM_EPS = 1e-08
ADAM_WD = 0.01
ADAM_STEP = 10

VMEM_BIG = 56 * 1024 * 1024
MESH = pl.DeviceIdType.MESH


def _cp(vmem=None):
    if vmem is None:
        return pltpu.CompilerParams()
    return pltpu.CompilerParams(vmem_limit_bytes=vmem)


def _dot(a, b, dims=((1,), (0,))):
    return lax.dot_general(a.astype(_MXU), b.astype(_MXU), (dims, ((), ())), preferred_element_type=F32)


def _dot_nt(a, b):
    return _dot(a, b, ((1,), (1,)))


def _dot_tn(a, b):
    return _dot(a, b, ((0,), (0,)))


def _dot_hi(a, b, dims=((1,), (0,))):
    return lax.dot_general(a.astype(F32), b.astype(F32), (dims, ((), ())), precision=_HI, preferred_element_type=F32)


def _sigmoid(x):
    return 1.0 / (1.0 + jnp.exp(-x))


def _gelu(x):
    return 0.5 * x * (1.0 + lax.erf(x * (1.0 / math.sqrt(2.0))))


def _gelu_grad(x):
    return 0.5 * (1.0 + lax.erf(x * (1.0 / math.sqrt(2.0)))) + x * jnp.exp(-0.5 * x * x) * (1.0 / math.sqrt(2.0 * math.pi))


def _softplus(x):
    return jnp.maximum(x, 0.0) + jnp.log1p(jnp.exp(-jnp.abs(x)))


def _iota(shape, dim):
    return lax.broadcasted_iota(jnp.int32, shape, dim)


def _mm(a, b, *, m, n, k, name, ta=False, tb=False, a_at=(None, 0, 0), b_at=(None, 0, 0),
        out_dtype=F32, add=None, epi_p=None, epi_at=(None, 0, 0), out=None, out_at=(None, 0, 0),
        out_full=None, a_pro=None, tm=512, tn=512, tk=512):
    tm, tn, tk = min(tm, m), min(tn, n), min(tk, k)
    assert m % tm == 0 and n % tn == 0 and k % tk == 0, (name, m, n, k)
    nk = k // tk

    def spec(at, tr, tc, rsel, csel):
        lead, r0, c0 = at
        assert r0 % tr == 0 and c0 % tc == 0, (name, at, tr, tc)
        rb, cb = r0 // tr, c0 // tc
        if lead is None:
            return pl.BlockSpec((tr, tc), lambda i, j, kk: (rb + rsel(i, j, kk), cb + csel(i, j, kk)))
        return pl.BlockSpec((None, tr, tc), lambda i, j, kk: (lead, rb + rsel(i, j, kk), cb + csel(i, j, kk)))

    gi = lambda i, j, kk: i
    gj = lambda i, j, kk: j
    gk = lambda i, j, kk: kk
    a_spec = spec(a_at, tk, tm, gk, gi) if ta else spec(a_at, tm, tk, gi, gk)
    b_spec = spec(b_at, tn, tk, gj, gk) if tb else spec(b_at, tk, tn, gk, gj)
    dims = ((0,), (0,)) if ta else (((1,), (1,)) if tb else ((1,), (0,)))
    assert not (ta and tb)

    operands, in_specs = [a, b], [a_spec, b_spec]
    if add is not None:
        operands.append(add)
        in_specs.append(spec((None, 0, 0), tm, tn, gi, gj))
    if epi_p is not None:
        operands.append(epi_p)
        in_specs.append(spec(epi_at, tm, tn, gi, gj))
    aliases = {}
    if out is not None:
        aliases = {len(operands): 0}
        operands.append(out)
        in_specs.append(pl.BlockSpec(memory_space=pl.ANY))
        out_struct = jax.ShapeDtypeStruct(out.shape, out.dtype)
        out_dtype = out.dtype
    else:
        out_struct = jax.ShapeDtypeStruct(out_full if out_full is not None else (m, n), out_dtype)
    has_add, has_epi, has_alias = add is not None, epi_p is not None, out is not None

    def body(*refs):
        a_ref, b_ref = refs[0], refs[1]
        pos = 2
        add_ref = epi_ref = None
        if has_add:
            add_ref = refs[pos]
            pos += 1
        if has_epi:
            epi_ref = refs[pos]
            pos += 1
        if has_alias:
            pos += 1
        o_ref, acc_ref = refs[pos], refs[pos + 1]
        kk = pl.program_id(2)

        @pl.when(kk == 0)
        def _():
            acc_ref[...] = jnp.zeros_like(acc_ref)

        av = a_ref[...]
        if a_pro == "relu2":
            av = jnp.square(jnp.maximum(av.astype(F32), 0.0))
        acc_ref[...] += _dot(av, b_ref[...], dims)

        @pl.when(kk == nk - 1)
        def _():
            r = acc_ref[...]
            if has_add:
                r = r + add_ref[...].astype(F32)
            if has_epi:
                r = r * (2.0 * jnp.maximum(epi_ref[...].astype(F32), 0.0))
            o_ref[...] = r.astype(o_ref.dtype)

    return pl.pallas_call(
        body, name=name, grid=(m // tm, n // tn, nk), in_specs=in_specs,
        out_specs=spec(out_at, tm, tn, gi, gj), out_shape=out_struct,
        scratch_shapes=[pltpu.VMEM((tm, tn), F32)], input_output_aliases=aliases,
        compiler_params=_cp(VMEM_BIG))(*operands)


def _rms_fwd(x, g, name, tm=256):
    s, d = x.shape
    tm = min(tm, s)

    def body(x_ref, g_ref, o_ref):
        xv = x_ref[...]
        r = lax.rsqrt(jnp.mean(xv * xv, axis=-1, keepdims=True) + EPS)
        o_ref[...] = (xv * r * g_ref[...]).astype(o_ref.dtype)

    return pl.pallas_call(
        body, name=name, grid=(s // tm,),
        in_specs=[pl.BlockSpec((tm, d), lambda i: (i, 0)), pl.BlockSpec((1, d), lambda i: (0, 0))],
        out_specs=pl.BlockSpec((tm, d), lambda i: (i, 0)),
        out_shape=jax.ShapeDtypeStruct((s, d), _ACT))(x, g)


def _rms_bwd(x, g, dy, dres, name, tm=256):
    s, d = x.shape
    tm = min(tm, s)
    has_res = dres is not None

    def body(*refs):
        if has_res:
            x_ref, g_ref, dy_ref, dres_ref, dx_ref, dg_ref = refs
        else:
            x_ref, g_ref, dy_ref, dx_ref, dg_ref = refs

        @pl.when(pl.program_id(0) == 0)
        def _():
            dg_ref[...] = jnp.zeros_like(dg_ref)

        xv = x_ref[...]
        dyv = dy_ref[...].astype(F32)
        r = lax.rsqrt(jnp.mean(xv * xv, axis=-1, keepdims=True) + EPS)
        xh = xv * r
        dyg = dyv * g_ref[...]
        dx = r * (dyg - xh * jnp.mean(dyg * xh, axis=-1, keepdims=True))
        if has_res:
            dx = dx + dres_ref[...]
        dx_ref[...] = dx
        dg_ref[...] += jnp.sum(dyv * xh, axis=0, keepdims=True)

    row = pl.BlockSpec((tm, d), lambda i: (i, 0))
    vec = pl.BlockSpec((1, d), lambda i: (0, 0))
    in_specs = [row, vec, row] + ([row] if has_res else [])
    operands = [x, g, dy] + ([dres] if has_res else [])
    return pl.pallas_call(
        body, name=name, grid=(s // tm,), in_specs=in_specs, out_specs=[row, vec],
        out_shape=[jax.ShapeDtypeStruct((s, d), F32), jax.ShapeDtypeStruct((1, d), F32)])(*operands)


def _loss_head(h, g, target, name, tm=256):
    s, d = h.shape
    tm = min(tm, s)

    def body(h_ref, g_ref, t_ref, loss_ref, dh_ref, dg_ref):
        @pl.when(pl.program_id(0) == 0)
        def _():
            dg_ref[...] = jnp.zeros_like(dg_ref)
            loss_ref[...] = jnp.zeros_like(loss_ref)

        xv = h_ref[...]
        r = lax.rsqrt(jnp.mean(xv * xv, axis=-1, keepdims=True) + EPS)
        xh = xv * r
        err = xh * g_ref[...] - t_ref[...]
        loss_ref[...] += jnp.full(loss_ref.shape, 0.5 * jnp.sum(jnp.mean(err * err, axis=-1, keepdims=True)), F32)
        dyv = err * (1.0 / d)
        dyg = dyv * g_ref[...]
        dh_ref[...] = r * (dyg - xh * jnp.mean(dyg * xh, axis=-1, keepdims=True))
        dg_ref[...] += jnp.sum(dyv * xh, axis=0, keepdims=True)

    row = pl.BlockSpec((tm, d), lambda i: (i, 0))
    vec = pl.BlockSpec((1, d), lambda i: (0, 0))
    return pl.pallas_call(
        body, name=name, grid=(s // tm,), in_specs=[row, vec, row],
        out_specs=[pl.BlockSpec((1, 128), lambda i: (0, 0)), row, vec],
        out_shape=[jax.ShapeDtypeStruct((1, 128), F32), jax.ShapeDtypeStruct((s, d), F32),
                   jax.ShapeDtypeStruct((1, d), F32)])(h, g, target)


def _gmlp_parts(pu, pv, lng, lnb):
    u = _gelu(pu)
    v = _gelu(pv)
    mu = jnp.mean(v, axis=-1, keepdims=True)
    vc = v - mu
    rstd = lax.rsqrt(jnp.mean(vc * vc, axis=-1, keepdims=True) + EPS)
    xhat = vc * rstd
    vn = xhat * lng + lnb
    return u, xhat, rstd, vn


def _gmlp_fwd(proj, lng, lnb, ws, bs3, name):
    s = proj.shape[0]

    def body(pu_ref, pv_ref, lng_ref, lnb_ref, ws_ref, bs_ref, o_ref):
        u, _, _, vn = _gmlp_parts(pu_ref[...], pv_ref[...], lng_ref[...], lnb_ref[...])
        causal = _iota((CHUNK, CHUNK), 0) >= _iota((CHUNK, CHUNK), 1)
        for g in range(A_GROUPS):
            sl = slice(g * A_GW, (g + 1) * A_GW)
            w = jnp.where(causal, ws_ref[g], 0.0)
            sv = _dot(w, vn[:, sl]) + bs_ref[g]
            o_ref[:, sl] = (u[:, sl] * sv).astype(o_ref.dtype)

    full = lambda shape: pl.BlockSpec(shape, lambda c: (0,) * len(shape))
    return pl.pallas_call(
        body, name=name, grid=(s // CHUNK,),
        in_specs=[pl.BlockSpec((CHUNK, D_INNER), lambda c: (c, 0)), pl.BlockSpec((CHUNK, D_INNER), lambda c: (c, 1)),
                  full((1, D_INNER)), full((1, D_INNER)), full((A_GROUPS, CHUNK, CHUNK)), full((A_GROUPS, CHUNK, 1))],
        out_specs=pl.BlockSpec((CHUNK, D_INNER), lambda c: (c, 0)),
        out_shape=jax.ShapeDtypeStruct((s, D_INNER), _ACT), compiler_params=_cp(VMEM_BIG))(proj, proj, lng, lnb, ws, bs3)


def _gmlp_bwd(proj, dcat, lng, lnb, ws, bs3, name):
    s = proj.shape[0]

    def body(pu_ref, pv_ref, dm_ref, lng_ref, lnb_ref, ws_ref, bs_ref, dp_ref, dws_ref, dbs_ref, dlng_ref, dlnb_ref, dvn_ref):
        @pl.when(pl.program_id(0) == 0)
        def _():
            dws_ref[...] = jnp.zeros_like(dws_ref)
            dbs_ref[...] = jnp.zeros_like(dbs_ref)
            dlng_ref[...] = jnp.zeros_like(dlng_ref)
            dlnb_ref[...] = jnp.zeros_like(dlnb_ref)

        pu, pv = pu_ref[...], pv_ref[...]
        lng = lng_ref[...]
        u, xhat, rstd, vn = _gmlp_parts(pu, pv, lng, lnb_ref[...])
        dm = dm_ref[...].astype(F32)
        causal = _iota((CHUNK, CHUNK), 0) >= _iota((CHUNK, CHUNK), 1)
        for g in range(A_GROUPS):
            sl = slice(g * A_GW, (g + 1) * A_GW)
            w = jnp.where(causal, ws_ref[g], 0.0)
            sv = _dot(w, vn[:, sl]) + bs_ref[g]
            dsv = dm[:, sl] * u[:, sl]
            dp_ref[:, sl] = (dm[:, sl] * sv * _gelu_grad(pu[:, sl])).astype(dp_ref.dtype)
            dvn_ref[:, sl] = _dot_tn(w, dsv)
            dws_ref[g] += jnp.where(causal, _dot_nt(dsv, vn[:, sl]), 0.0)
            dbs_ref[g] += jnp.sum(dsv, axis=-1, keepdims=True)
        dvn = dvn_ref[...]
        dlng_ref[...] += jnp.sum(dvn * xhat, axis=0, keepdims=True)
        dlnb_ref[...] += jnp.sum(dvn, axis=0, keepdims=True)
        dxh = dvn * lng
        dv = rstd * (dxh - jnp.mean(dxh, axis=-1, keepdims=True) - xhat * jnp.mean(dxh * xhat, axis=-1, keepdims=True))
        dp_ref[:, D_INNER:] = (dv * _gelu_grad(pv)).astype(dp_ref.dtype)

    full = lambda shape: pl.BlockSpec(shape, lambda c: (0,) * len(shape))
    return pl.pallas_call(
        body, name=name, grid=(s // CHUNK,),
        in_specs=[pl.BlockSpec((CHUNK, D_INNER), lambda c: (c, 0)), pl.BlockSpec((CHUNK, D_INNER), lambda c: (c, 1)),
                  pl.BlockSpec((CHUNK, D_INNER), lambda c: (c, 0)),
                  full((1, D_INNER)), full((1, D_INNER)), full((A_GROUPS, CHUNK, CHUNK)), full((A_GROUPS, CHUNK, 1))],
        out_specs=[pl.BlockSpec((CHUNK, 2 * D_INNER), lambda c: (c, 0)), full((A_GROUPS, CHUNK, CHUNK)),
                   full((A_GROUPS, CHUNK, 1)), full((1, D_INNER)), full((1, D_INNER))],
        out_shape=[jax.ShapeDtypeStruct((s, 2 * D_INNER), _ACT), jax.ShapeDtypeStruct((A_GROUPS, CHUNK, CHUNK), F32),
                   jax.ShapeDtypeStruct((A_GROUPS, CHUNK, 1), F32), jax.ShapeDtypeStruct((1, D_INNER), F32),
                   jax.ShapeDtypeStruct((1, D_INNER), F32)],
        scratch_shapes=[pltpu.VMEM((CHUNK, D_INNER), F32)],
        compiler_params=_cp(VMEM_BIG))(proj, proj, dcat, lng, lnb, ws, bs3)


_X_SCALE = 1.0 / math.sqrt(X_HD)


def _attn_fwd(proj, qblk, kv, name, tm=256):
    s = proj.shape[0]
    tm = min(tm, s)

    def body(q_ref, kv_ref, o_ref):
        for h in range(X_HEADS):
            sl = slice(h * X_HD, (h + 1) * X_HD)
            k = kv_ref[:, sl]
            v = kv_ref[:, X_WIDTH + h * X_HD:X_WIDTH + (h + 1) * X_HD]
            sc = _dot_nt(q_ref[:, sl], k) * _X_SCALE
            e = jnp.exp(sc - jnp.max(sc, axis=-1, keepdims=True))
            p = e / jnp.sum(e, axis=-1, keepdims=True)
            o_ref[:, sl] = _dot(p, v).astype(o_ref.dtype)

    return pl.pallas_call(
        body, name=name, grid=(s // tm,),
        in_specs=[pl.BlockSpec((tm, X_WIDTH), lambda i: (i, qblk)), pl.BlockSpec((N_MEM, 2 * X_WIDTH), lambda i: (0, 0))],
        out_specs=pl.BlockSpec((tm, X_WIDTH), lambda i: (i, 0)),
        out_shape=jax.ShapeDtypeStruct((s, X_WIDTH), _ACT))(proj, kv)


def _attn_bwd(proj, qblk, kv, dcat, name, tm=256):
    s = proj.shape[0]
    tm = min(tm, s)

    def body(q_ref, kv_ref, do_ref, dq_ref, dkv_ref):
        @pl.when(pl.program_id(0) == 0)
        def _():
            dkv_ref[...] = jnp.zeros_like(dkv_ref)

        for h in range(X_HEADS):
            sl = slice(h * X_HD, (h + 1) * X_HD)
            slv = slice(X_WIDTH + h * X_HD, X_WIDTH + (h + 1) * X_HD)
            q = q_ref[:, sl]
            k = kv_ref[:, sl]
            v = kv_ref[:, slv]
            do = do_ref[:, sl].astype(F32)
            sc = _dot_nt(q, k) * _X_SCALE
            e = jnp.exp(sc - jnp.max(sc, axis=-1, keepdims=True))
            p = e / jnp.sum(e, axis=-1, keepdims=True)
            dp = _dot_nt(do, v)
            ds = p * (dp - jnp.sum(dp * p, axis=-1, keepdims=True)) * _X_SCALE
            dq_ref[:, sl] = _dot(ds, k).astype(dq_ref.dtype)
            dkv_ref[:, sl] += _dot_tn(ds, q)
            dkv_ref[:, slv] += _dot_tn(p, do)

    return pl.pallas_call(
        body, name=name, grid=(s // tm,),
        in_specs=[pl.BlockSpec((tm, X_WIDTH), lambda i: (i, qblk)), pl.BlockSpec((N_MEM, 2 * X_WIDTH), lambda i: (0, 0)),
                  pl.BlockSpec((tm, X_WIDTH), lambda i: (i, 2))],
        out_specs=[pl.BlockSpec((tm, X_WIDTH), lambda i: (i, 0)), pl.BlockSpec((N_MEM, 2 * X_WIDTH), lambda i: (0, 0))],
        out_shape=[jax.ShapeDtypeStruct((s, X_WIDTH), _ACT), jax.ShapeDtypeStruct((N_MEM, 2 * X_WIDTH), F32)])(proj, kv, dcat)


CONV_TC = 256
_XBC_BLK0 = D_INNER // CONV_TC


def _shift_down(x, j):
    if j == 0:
        return x
    return jnp.where(_iota(x.shape, 0) >= j, pltpu.roll(x, j, 0), 0.0)


def _shift_up(x, j):
    if j == 0:
        return x
    n = x.shape[0]
    return jnp.where(_iota(x.shape, 0) < n - j, pltpu.roll(x, n - j, 0), 0.0)


def _conv_fwd(proj, w, b, name):
    s = proj.shape[0]

    def body(x_ref, w_ref, b_ref, o_ref):
        xv = x_ref[...]
        pre = b_ref[...] + w_ref[CONV_K - 1:CONV_K, :] * xv
        for kk in range(CONV_K - 1):
            pre = pre + w_ref[kk:kk + 1, :] * _shift_down(xv, CONV_K - 1 - kk)
        o_ref[...] = pre * _sigmoid(pre)

    return pl.pallas_call(
        body, name=name, grid=(CONV_DIM // CONV_TC,),
        in_specs=[pl.BlockSpec((s, CONV_TC), lambda j: (0, _XBC_BLK0 + j)), pl.BlockSpec((CONV_K, CONV_TC), lambda j: (0, j)),
                  pl.BlockSpec((1, CONV_TC), lambda j: (0, j))],
        out_specs=pl.BlockSpec((s, CONV_TC), lambda j: (0, j)),
        out_shape=jax.ShapeDtypeStruct((s, CONV_DIM), F32), compiler_params=_cp(VMEM_BIG))(proj, w, b)


def _conv_bwd(proj, w, b, dxbc, name):
    s = proj.shape[0]

    def body(x_ref, w_ref, b_ref, d_ref, dx_ref, dw_ref, db_ref):
        xv = x_ref[...]
        pre = b_ref[...] + w_ref[CONV_K - 1:CONV_K, :] * xv
        for kk in range(CONV_K - 1):
            pre = pre + w_ref[kk:kk + 1, :] * _shift_down(xv, CONV_K - 1 - kk)
        sig = _sigmoid(pre)
        dpre = d_ref[...] * (sig * (1.0 + pre * (1.0 - sig)))
        dx = w_ref[CONV_K - 1:CONV_K, :] * dpre
        dw_ref[CONV_K - 1:CONV_K, :] = jnp.sum(dpre * xv, axis=0, keepdims=True)
        for kk in range(CONV_K - 1):
            j = CONV_K - 1 - kk
            dx = dx + w_ref[kk:kk + 1, :] * _shift_up(dpre, j)
            dw_ref[kk:kk + 1, :] = jnp.sum(dpre * _shift_down(xv, j), axis=0, keepdims=True)
        dx_ref[...] = dx.astype(dx_ref.dtype)
        db_ref[...] = jnp.sum(dpre, axis=0, keepdims=True)

    return pl.pallas_call(
        body, name=name, grid=(CONV_DIM // CONV_TC,),
        in_specs=[pl.BlockSpec((s, CONV_TC), lambda j: (0, _XBC_BLK0 + j)), pl.BlockSpec((CONV_K, CONV_TC), lambda j: (0, j)),
                  pl.BlockSpec((1, CONV_TC), lambda j: (0, j)), pl.BlockSpec((s, CONV_TC), lambda j: (0, j))],
        out_specs=[pl.BlockSpec((s, CONV_TC), lambda j: (0, j)), pl.BlockSpec((CONV_K, CONV_TC), lambda j: (0, j)),
                   pl.BlockSpec((1, CONV_TC), lambda j: (0, j))],
        out_shape=[jax.ShapeDtypeStruct((s, CONV_DIM), _ACT), jax.ShapeDtypeStruct((CONV_K, CONV_DIM), F32),
                   jax.ShapeDtypeStruct((1, CONV_DIM), F32)], compiler_params=_cp(VMEM_BIG))(proj, w, b, dxbc)


def _ssd_common(dtc_ref, dtr_ref, br_ref, bc_ref, ar_ref, ac_ref, csb_ref, cst_ref, csf_ref):
    a_row = -jnp.exp(ar_ref[...])
    dt_c = _softplus(dtc_ref[...] + br_ref[...])
    a_col = -jnp.exp(ac_ref[...])
    dt_r = _softplus(dtr_ref[...] + bc_ref[...])
    row = _iota((CHUNK, CHUNK), 0)
    col = _iota((CHUNK, CHUNK), 1)
    tril = (row >= col).astype(F32)
    triu = (row <= col).astype(F32)
    cs = _dot_hi(tril, dt_c * a_row)
    cst_ref[...] = _dot_hi(dt_r * a_col, triu)
    e64 = (jnp.right_shift(_iota((HPAD, D_INNER), 1), 6) == _iota((HPAD, D_INNER), 0)).astype(F32)
    e128 = (jnp.right_shift(_iota((HPAD, SSM_HEADS * CHUNK), 1), 7) == _iota((HPAD, SSM_HEADS * CHUNK), 0)).astype(F32)
    csb_ref[...] = _dot_hi(cs, e128)
    dt_full = _dot_hi(dt_c, e64)
    csf_ref[...] = _dot_hi(cs, e64)
    cs_full = csf_ref[...]
    cs_last = csf_ref[CHUNK - 1:CHUNK, :]
    e_full = jnp.exp(cs_full)
    f_full = jnp.exp(cs_last - cs_full)
    gamma = jnp.exp(cs_last)
    return a_row, dt_c, cs, dt_full, e_full, f_full, gamma, e64


def _ssd_lambda(csb_ref, cst_ref, h, causal):
    diff = csb_ref[:, h * CHUNK:(h + 1) * CHUNK] - cst_ref[h:h + 1, :]
    return jnp.exp(jnp.where(causal, diff, -1e30))


_SSD_VEC_SPECS = lambda: [pl.BlockSpec((1, HPAD), lambda c: (0, 0)), pl.BlockSpec((HPAD, 1), lambda c: (0, 0)),
                          pl.BlockSpec((1, HPAD), lambda c: (0, 0)), pl.BlockSpec((HPAD, 1), lambda c: (0, 0)),
                          pl.BlockSpec((1, D_INNER), lambda c: (0, 0))]


def _ssd_fwd(xbc, dtc, dtr, bias_row, bias_col, alog_row, alog_col, dfull, name):
    s = xbc.shape[0]
    nc = s // CHUNK

    def body(xbc_ref, dtc_ref, dtr_ref, br_ref, bc_ref, ar_ref, ac_ref, df_ref, y_ref, st_ref,
             ht_ref, csb_ref, cst_ref, csf_ref):
        @pl.when(pl.program_id(0) == 0)
        def _():
            ht_ref[...] = jnp.zeros_like(ht_ref)

        _, _, _, dt_full, e_full, f_full, gamma, _ = _ssd_common(
            dtc_ref, dtr_ref, br_ref, bc_ref, ar_ref, ac_ref, csb_ref, cst_ref, csf_ref)
        x = xbc_ref[:, :D_INNER]
        xdt = x * dt_full
        st_ref[...] = ht_ref[...]
        causal = _iota((CHUNK, CHUNK), 0) >= _iota((CHUNK, CHUNK), 1)
        lo = _iota((CHUNK, CHUNK), 1) < SSM_P
        for g in range(SSM_GROUPS):
            gs = slice(g * SSM_GW, (g + 1) * SSM_GW)
            bg = xbc_ref[:, D_INNER + g * SSM_N:D_INNER + (g + 1) * SSM_N]
            cg = xbc_ref[:, D_INNER + SSM_GROUPS * SSM_N + g * SSM_N:D_INNER + SSM_GROUPS * SSM_N + (g + 1) * SSM_N]
            ht = ht_ref[:, gs]
            cb = _dot_nt(cg, bg)
            yoff = e_full[:, gs] * _dot(cg, ht)
            for jp in range(SSM_GW // CHUNK):
                j = g * (SSM_GW // CHUNK) + jp
                ps = slice(j * CHUNK, (j + 1) * CHUNK)
                x2 = xdt[:, ps]
                y0 = _dot(cb * _ssd_lambda(csb_ref, cst_ref, 2 * j, causal), x2)
                y1 = _dot(cb * _ssd_lambda(csb_ref, cst_ref, 2 * j + 1, causal), x2)
                y_ref[:, ps] = (jnp.where(lo, y0, y1) + yoff[:, jp * CHUNK:(jp + 1) * CHUNK]
                                + x[:, ps] * df_ref[:, ps])
            ht_ref[:, gs] = gamma[:, gs] * ht + _dot_tn(bg, xdt[:, gs] * f_full[:, gs])

    return pl.pallas_call(
        body, name=name, grid=(nc,),
        in_specs=[pl.BlockSpec((CHUNK, CONV_DIM), lambda c: (c, 0)), pl.BlockSpec((CHUNK, HPAD), lambda c: (c, 0)),
                  pl.BlockSpec((HPAD, CHUNK), lambda c: (0, c))] + _SSD_VEC_SPECS(),
        out_specs=[pl.BlockSpec((CHUNK, D_INNER), lambda c: (c, 0)), pl.BlockSpec((None, SSM_N, D_INNER), lambda c: (c, 0, 0))],
        out_shape=[jax.ShapeDtypeStruct((s, D_INNER), F32), jax.ShapeDtypeStruct((nc, SSM_N, D_INNER), F32)],
        scratch_shapes=[pltpu.VMEM((SSM_N, D_INNER), F32), pltpu.VMEM((CHUNK, SSM_HEADS * CHUNK), F32),
                        pltpu.VMEM((HPAD, CHUNK), F32), pltpu.VMEM((CHUNK, D_INNER), F32)],
        compiler_params=_cp(VMEM_BIG))(xbc, dtc, dtr, bias_row, bias_col, alog_row, alog_col, dfull)


def _ssd_bwd(xbc, dtc, dtr, bias_row, bias_col, alog_row, alog_col, dfull, dy, states, name):
    s = xbc.shape[0]
    nc = s // CHUNK
    rev = lambda c: nc - 1 - c

    def body(xbc_ref, dtc_ref, dtr_ref, br_ref, bc_ref, ar_ref, ac_ref, df_ref, dy_ref, st_ref,
             dxbc_ref, ddt_ref, dalog_ref, dd_ref, dbias_ref,
             dht_ref, csb_ref, cst_ref, csf_ref, ddf_ref, dxs_ref, dcsf_ref, dcsl_ref):
        step = pl.program_id(0)

        @pl.when(step == 0)
        def _():
            dht_ref[...] = jnp.zeros_like(dht_ref)
            ddf_ref[...] = jnp.zeros_like(ddf_ref)
            dalog_ref[...] = jnp.zeros_like(dalog_ref)
            dbias_ref[...] = jnp.zeros_like(dbias_ref)
            dd_ref[...] = jnp.zeros_like(dd_ref)

        a_row, dt_c, _, dt_full, e_full, f_full, gamma, e64 = _ssd_common(
            dtc_ref, dtr_ref, br_ref, bc_ref, ar_ref, ac_ref, csb_ref, cst_ref, csf_ref)
        x = xbc_ref[:, :D_INNER]
        xdt = x * dt_full
        dy_all = dy_ref[...]
        ddf_ref[...] += jnp.broadcast_to(jnp.sum(dy_all * x, axis=0, keepdims=True), ddf_ref.shape)
        causal = _iota((CHUNK, CHUNK), 0) >= _iota((CHUNK, CHUNK), 1)
        lo = _iota((CHUNK, CHUNK), 1) < SSM_P
        ones = jnp.ones((CHUNK, HPAD), F32)
        head_lane = _iota((CHUNK, HPAD), 1)
        dcs_heads = jnp.zeros((CHUNK, HPAD), F32)
        for g in range(SSM_GROUPS):
            gs = slice(g * SSM_GW, (g + 1) * SSM_GW)
            b0 = D_INNER + g * SSM_N
            c0 = D_INNER + SSM_GROUPS * SSM_N + g * SSM_N
            bg = xbc_ref[:, b0:b0 + SSM_N]
            cg = xbc_ref[:, c0:c0 + SSM_N]
            ht = st_ref[:, gs]
            dht = dht_ref[:, gs]
            dyg = dy_all[:, gs]
            eg, fg, gg = e_full[:, gs], f_full[:, gs], gamma[:, gs]
            z = _dot(cg, ht)
            dz = dyg * eg
            dcg = _dot_nt(dz, ht)
            dht_new = _dot_tn(cg, dz) + gg * dht
            xf = xdt[:, gs] * fg
            dxf = _dot(bg, dht)
            dbg = _dot_nt(xf, dht)
            dff = dxf * xf
            dcsf_ref[:, gs] = dyg * eg * z - dff
            dcsl_ref[:, gs] = jnp.broadcast_to(
                jnp.sum(dff, axis=0, keepdims=True) + jnp.sum(dht * ht, axis=0, keepdims=True) * gg, (8, SSM_GW))
            cb = _dot_nt(cg, bg)
            dcb = jnp.zeros((CHUNK, CHUNK), F32)
            for jp in range(SSM_GW // CHUNK):
                j = g * (SSM_GW // CHUNK) + jp
                ps = slice(j * CHUNK, (j + 1) * CHUNK)
                x2 = xdt[:, ps]
                dy2 = dy_all[:, ps]
                dxh = []
                for hh in range(2):
                    h = 2 * j + hh
                    lam = _ssd_lambda(csb_ref, cst_ref, h, causal)
                    mh = cb * lam
                    dyh = jnp.where(lo, dy2, 0.0) if hh == 0 else jnp.where(lo, 0.0, dy2)
                    dm = _dot_nt(dyh, x2)
                    dcb = dcb + dm * lam
                    gm = dm * mh
                    rs = jnp.sum(gm, axis=1, keepdims=True)
                    csum = _dot_hi(gm, ones, ((0,), (0,)))
                    dcs_heads = dcs_heads + jnp.where(head_lane == h, rs - csum, 0.0)
                    dxh.append(_dot_tn(mh, dy2))
                dxs_ref[:, ps] = jnp.where(lo, dxh[0], dxh[1]) + dxf[:, jp * CHUNK:(jp + 1) * CHUNK] * fg[:, jp * CHUNK:(jp + 1) * CHUNK]
            dxbc_ref[:, b0:b0 + SSM_N] = (dbg + _dot_tn(dcb, cg)).astype(dxbc_ref.dtype)
            dxbc_ref[:, c0:c0 + SSM_N] = (dcg + _dot(dcb, bg)).astype(dxbc_ref.dtype)
            dht_ref[:, gs] = dht_new
        dxs = dxs_ref[...]
        dcs_heads = dcs_heads + _dot_hi(dcsf_ref[...], e64, ((1,), (1,)))
        dcs_last = _dot_hi(dcsl_ref[...], e64, ((1,), (1,)))
        dcs_heads = dcs_heads + jnp.where(_iota((CHUNK, HPAD), 0) == CHUNK - 1, dcs_last[0:1, :], 0.0)
        triu = (_iota((CHUNK, CHUNK), 0) <= _iota((CHUNK, CHUNK), 1)).astype(F32)
        dda = _dot_hi(triu, dcs_heads)
        ddt = dda * a_row + _dot_hi(dxs * x, e64, ((1,), (1,)))
        dxbc_ref[:, :D_INNER] = (dxs * dt_full + dy_all * df_ref[...]).astype(dxbc_ref.dtype)
        dalog_ref[...] += jnp.sum(dda * dt_c, axis=0, keepdims=True) * a_row
        ddt_raw = ddt * _sigmoid(dtc_ref[...] + br_ref[...])
        ddt_ref[...] = ddt_raw.astype(ddt_ref.dtype)
        dbias_ref[...] += jnp.sum(ddt_raw, axis=0, keepdims=True)

        @pl.when(step == nc - 1)
        def _():
            dd_ref[...] = _dot_hi(ddf_ref[...], e64, ((1,), (1,)))[0:1, :]

    vec = pl.BlockSpec((1, HPAD), lambda c: (0, 0))
    return pl.pallas_call(
        body, name=name, grid=(nc,),
        in_specs=[pl.BlockSpec((CHUNK, CONV_DIM), lambda c: (rev(c), 0)), pl.BlockSpec((CHUNK, HPAD), lambda c: (rev(c), 0)),
                  pl.BlockSpec((HPAD, CHUNK), lambda c: (0, rev(c)))] + _SSD_VEC_SPECS()
                 + [pl.BlockSpec((CHUNK, D_INNER), lambda c: (rev(c), 0)),
                    pl.BlockSpec((None, SSM_N, D_INNER), lambda c: (rev(c), 0, 0))],
        out_specs=[pl.BlockSpec((CHUNK, CONV_DIM), lambda c: (rev(c), 0)), pl.BlockSpec((CHUNK, HPAD), lambda c: (rev(c), 0)),
                   vec, vec, vec],
        out_shape=[jax.ShapeDtypeStruct((s, CONV_DIM), F32), jax.ShapeDtypeStruct((s, HPAD), _ACT),
                   jax.ShapeDtypeStruct((1, HPAD), F32), jax.ShapeDtypeStruct((1, HPAD), F32),
                   jax.ShapeDtypeStruct((1, HPAD), F32)],
        scratch_shapes=[pltpu.VMEM((SSM_N, D_INNER), F32), pltpu.VMEM((CHUNK, SSM_HEADS * CHUNK), F32),
                        pltpu.VMEM((HPAD, CHUNK), F32), pltpu.VMEM((CHUNK, D_INNER), F32),
                        pltpu.VMEM((8, D_INNER), F32), pltpu.VMEM((CHUNK, D_INNER), F32),
                        pltpu.VMEM((CHUNK, D_INNER), F32), pltpu.VMEM((8, D_INNER), F32)],
        compiler_params=_cp(VMEM_BIG))(xbc, dtc, dtr, bias_row, bias_col, alog_row, alog_col, dfull, dy, states)


def _gate_fwd(y, proj, gn, name, tm=256):
    s = y.shape[0]
    tm = min(tm, s)

    def body(y_ref, z_ref, gn_ref, o_ref):
        for g in range(SSM_GROUPS):
            gs = slice(g * SSM_GW, (g + 1) * SSM_GW)
            z = z_ref[:, gs]
            t = y_ref[:, gs] * (z * _sigmoid(z))
            r = lax.rsqrt(jnp.mean(t * t, axis=-1, keepdims=True) + EPS)
            o_ref[:, gs] = (t * r * gn_ref[:, gs]).astype(o_ref.dtype)

    row = pl.BlockSpec((tm, D_INNER), lambda i: (i, 0))
    return pl.pallas_call(
        body, name=name, grid=(s // tm,), in_specs=[row, row, pl.BlockSpec((1, D_INNER), lambda i: (0, 0))],
        out_specs=row, out_shape=jax.ShapeDtypeStruct((s, D_INNER), _ACT))(y, proj, gn)


def _gate_bwd(y, proj, gn, dcat, name, tm=256):
    s = y.shape[0]
    tm = min(tm, s)

    def body(y_ref, z_ref, gn_ref, dm_ref, dy_ref, dz_ref, dgn_ref):
        @pl.when(pl.program_id(0) == 0)
        def _():
            dgn_ref[...] = jnp.zeros_like(dgn_ref)

        for g in range(SSM_GROUPS):
            gs = slice(g * SSM_GW, (g + 1) * SSM_GW)
            z = z_ref[:, gs]
            yv = y_ref[:, gs]
            sig = _sigmoid(z)
            sz = z * sig
            t = yv * sz
            r = lax.rsqrt(jnp.mean(t * t, axis=-1, keepdims=True) + EPS)
            th = t * r
            dm = dm_ref[:, gs].astype(F32)
            dmg = dm * gn_ref[:, gs]
            dt_ = r * (dmg - th * jnp.mean(dmg * th, axis=-1, keepdims=True))
            dgn_ref[:, gs] += jnp.sum(dm * th, axis=0, keepdims=True)
            dy_ref[:, gs] = dt_ * sz
            dz_ref[:, gs] = (dt_ * yv * (sig * (1.0 + z * (1.0 - sig)))).astype(dz_ref.dtype)

    row = pl.BlockSpec((tm, D_INNER), lambda i: (i, 0))
    vec = pl.BlockSpec((1, D_INNER), lambda i: (0, 0))
    return pl.pallas_call(
        body, name=name, grid=(s // tm,), in_specs=[row, row, vec, row], out_specs=[row, row, vec],
        out_shape=[jax.ShapeDtypeStruct((s, D_INNER), F32), jax.ShapeDtypeStruct((s, D_INNER), _ACT),
                   jax.ShapeDtypeStruct((1, D_INNER), F32)])(y, proj, gn, dcat)


def _block_of(kind, width):
    if kind == "col":
        return lambda ref, j: ref.at[:, :, pl.ds(pl.multiple_of(j * width, 128), width)]
    if kind == "row":
        return lambda ref, j: ref.at[:, pl.ds(pl.multiple_of(j * width, 8), width), :]
    return lambda ref, j: ref.at[j]


def _coords():
    return lax.axis_index("x"), lax.axis_index("y"), lax.axis_index("c")


def _rel_chip(x, y, k):
    return (1 - x if k & 1 else x), (1 - y if k & 2 else y)


_HBM = lambda: pl.BlockSpec(memory_space=pltpu.HBM)


def _all_gather(shards, layouts, name):
    n = len(shards)
    blocks = [_block_of(kind, width) for kind, width, _ in layouts]

    def body(*refs):
        ins, outs = refs[:n], refs[n:2 * n]
        send_sems, recv_sems, local_sems = refs[2 * n:]
        x, y, c = _coords()
        sibling = (x, y, 1 - c)

        def copy(t, k, chip, core, to, src=None):
            dst = blocks[t](outs[t], 4 * chip[0] + 2 * chip[1] + core)
            return pltpu.make_async_remote_copy(
                src_ref=dst if src is None else src, dst_ref=dst, send_sem=send_sems.at[t, k],
                recv_sem=recv_sems.at[t, k], device_id=to, device_id_type=MESH)

        started = []
        for t in range(n):
            mine = pltpu.make_async_copy(ins[t], blocks[t](outs[t], 4 * x + 2 * y + c), local_sems.at[t])
            mine.start()
            started.append(mine)
        sends = []
        for t in range(n):
            for k in range(4):
                px, py = _rel_chip(x, y, k)
                cp = copy(t, k, (x, y), c, (px, py, 1 - c if k == 0 else c), src=ins[t])
                cp.start()
                sends.append(cp)
        for t in range(n):
            for k in range(1, 4):
                chip = _rel_chip(x, y, k)
                copy(t, k, chip, c, sibling).wait_recv()
                fwd = copy(t, 3 + k, chip, c, sibling)
                fwd.start()
                sends.append(fwd)
        for t in range(n):
            copy(t, 0, (x, y), 1 - c, sibling).wait_recv()
            for k in range(1, 4):
                copy(t, 3 + k, _rel_chip(x, y, k), 1 - c, sibling).wait_recv()
        for cp in sends:
            cp.wait_send()
        for mine in started:
            mine.wait()

    return pl.pallas_call(
        body, name=name, in_specs=[_HBM()] * n, out_specs=[_HBM()] * n,
        out_shape=[jax.ShapeDtypeStruct(shape, sh.dtype) for sh, (_, _, shape) in zip(shards, layouts)],
        scratch_shapes=[pltpu.SemaphoreType.DMA((n, 7)), pltpu.SemaphoreType.DMA((n, 7)), pltpu.SemaphoreType.DMA((n,))])(*shards)


def _rs_to_sibling(grads, layouts, name):
    n = len(grads)
    blocks = [_block_of(kind, width) for kind, width, _ in layouts]

    def body(*refs):
        ins, outs = refs[:n], refs[n:2 * n]
        send_sems, recv_sems = refs[2 * n:]
        x, y, c = _coords()
        sibling = (x, y, 1 - c)
        cps = []
        for t in range(n):
            for k in range(4):
                px, py = _rel_chip(x, y, k)
                cp = pltpu.make_async_remote_copy(
                    src_ref=blocks[t](ins[t], 4 * px + 2 * py + (1 - c)), dst_ref=outs[t].at[k],
                    send_sem=send_sems.at[t, k], recv_sem=recv_sems.at[t, k], device_id=sibling, device_id_type=MESH)
                cp.start()
                cps.append(cp)
        for cp in cps:
            cp.wait_recv()
        for cp in cps:
            cp.wait_send()

    return pl.pallas_call(
        body, name=name, in_specs=[_HBM()] * n, out_specs=[_HBM()] * n,
        out_shape=[jax.ShapeDtypeStruct((4,) + shape, g.dtype) for g, (_, _, shape) in zip(grads, layouts)],
        scratch_shapes=[pltpu.SemaphoreType.DMA((n, 4)), pltpu.SemaphoreType.DMA((n, 4))])(*grads)


def _rs_chip_sum(grad, recv, layout, xyc, name):
    kind, width, shape = layout
    r, ccols = shape

    def src_index(k, xyc_ref):
        px = jnp.where(k % 2 == 1, 1 - xyc_ref[0], xyc_ref[0])
        py = jnp.where(k // 2 == 1, 1 - xyc_ref[1], xyc_ref[1])
        return 4 * px + 2 * py + xyc_ref[2]

    if kind == "col":
        g_spec = pl.BlockSpec((r, ccols), lambda k, s_: (0, src_index(k, s_)))
    elif kind == "row":
        g_spec = pl.BlockSpec((r, ccols), lambda k, s_: (src_index(k, s_), 0))
    else:
        g_spec = pl.BlockSpec((None, r, ccols), lambda k, s_: (src_index(k, s_), 0, 0))

    def body(xyc_ref, g_ref, r_ref, o_ref):
        o_ref[...] = (g_ref[...].astype(F32) + r_ref[...].astype(F32)).astype(o_ref.dtype)

    slot = pl.BlockSpec((None, r, ccols), lambda k, s_: (k, 0, 0))
    return pl.pallas_call(
        body, name=name,
        grid_spec=pltpu.PrefetchScalarGridSpec(num_scalar_prefetch=1, grid=(4,), in_specs=[g_spec, slot], out_specs=slot),
        out_shape=jax.ShapeDtypeStruct((4, r, ccols), grad.dtype), compiler_params=_cp(VMEM_BIG))(xyc, grad, recv)


def _rs_across_chips(parts, name):
    n = len(parts)

    def body(*refs):
        ins, outs = refs[:n], refs[n:2 * n]
        send_sems, recv_sems = refs[2 * n:]
        x, y, c = _coords()
        cps = []
        for t in range(n):
            for k in range(1, 4):
                px, py = _rel_chip(x, y, k)
                cp = pltpu.make_async_remote_copy(
                    src_ref=ins[t].at[k], dst_ref=outs[t].at[k - 1], send_sem=send_sems.at[t, k - 1],
                    recv_sem=recv_sems.at[t, k - 1], device_id=(px, py, c), device_id_type=MESH)
                cp.start()
                cps.append(cp)
        for cp in cps:
            cp.wait_recv()
        for cp in cps:
            cp.wait_send()

    return pl.pallas_call(
        body, name=name, in_specs=[_HBM()] * n, out_specs=[_HBM()] * n,
        out_shape=[jax.ShapeDtypeStruct((3,) + p.shape[1:], p.dtype) for p in parts],
        scratch_shapes=[pltpu.SemaphoreType.DMA((n, 3)), pltpu.SemaphoreType.DMA((n, 3))])(*parts)


def _adamw_math(w, g, m, v):
    m = ADAM_B1 * m + (1.0 - ADAM_B1) * g
    v = ADAM_B2 * v + (1.0 - ADAM_B2) * jnp.square(g)
    m_hat = m / (1.0 - ADAM_B1 ** ADAM_STEP)
    v_hat = v / (1.0 - ADAM_B2 ** ADAM_STEP)
    delta = -ADAM_LR * (m_hat / (jnp.sqrt(v_hat) + ADAM_EPS) + ADAM_WD * w)
    return delta, m, v


def _row_tile(rows, cap):
    best = None
    for cand in range(8, min(rows, cap) + 1, 8):
        if rows % cand == 0:
            best = cand
    assert best is not None, rows
    return best


def _adamw(w, m, v, parts, name, layer=None, prev=None, tr=256):
    r, ccols = w.shape[-2:]
    tr = _row_tile(r, tr)
    npart = len(parts)

    def wspec():
        if layer is None:
            return pl.BlockSpec((tr, ccols), lambda i: (i, 0))
        return pl.BlockSpec((None, tr, ccols), lambda i: (layer, i, 0))

    def pspec(lead):
        if lead is None:
            return pl.BlockSpec((tr, ccols), lambda i: (i, 0))
        return pl.BlockSpec((None, tr, ccols), lambda i: (lead, i, 0))

    def body(*refs):
        w_ref, m_ref, v_ref = refs[:3]
        p_refs = refs[3:3 + npart]
        outs = refs[len(refs) - 4:]
        g = p_refs[0][...].astype(F32)
        for p_ref in p_refs[1:]:
            g = g + p_ref[...].astype(F32)
        delta, mn, vn = _adamw_math(w_ref[...], g, m_ref[...], v_ref[...])
        outs[0][...] = g
        outs[1][...] = delta
        outs[2][...] = mn
        outs[3][...] = vn

    operands = [w, m, v] + [p for p, _ in parts]
    in_specs = [wspec(), wspec(), wspec()] + [pspec(lead) for _, lead in parts]
    aliases = {}
    if prev is not None:
        for i, p in enumerate(prev):
            aliases[len(operands)] = i
            operands.append(p)
            in_specs.append(pl.BlockSpec(memory_space=pl.ANY))
    return pl.pallas_call(
        body, name=name, grid=(r // tr,), in_specs=in_specs, out_specs=[wspec()] * 4,
        out_shape=[jax.ShapeDtypeStruct(w.shape, F32)] * 4, input_output_aliases=aliases)(*operands)


def _sum8(buf, name):
    _, r, ccols = buf.shape

    def body(b_ref, o_ref):
        acc = b_ref[0]
        for j in range(1, N_DEV):
            acc = acc + b_ref[j]
        o_ref[...] = acc

    tr = _row_tile(r, 256)
    return pl.pallas_call(
        body, name=name, grid=(r // tr,), in_specs=[pl.BlockSpec((N_DEV, tr, ccols), lambda i: (0, i, 0))],
        out_specs=pl.BlockSpec((tr, ccols), lambda i: (i, 0)), out_shape=jax.ShapeDtypeStruct((r, ccols), F32))(buf)


def _pack(arrays):
    pieces, layout, off = [], [], 0
    for a in arrays:
        n = a.size
        padded = -(-n // 1024) * 1024
        flat = a.reshape(-1).astype(F32)
        if padded != n:
            flat = jnp.pad(flat, (0, padded - n))
        pieces.append(flat.reshape(padded // 128, 128))
        layout.append((off, n, a.shape))
        off += padded // 128
    return jnp.concatenate(pieces, axis=0), layout


def _unpack(packed, layout):
    out = []
    for off, n, shape in layout:
        rows = -(-n // 1024) * 8
        out.append(packed[off:off + rows].reshape(-1)[:n].reshape(shape))
    return out


def kernel(x, mem, norm_mix, norm_ffn, mem_norm, w_kv, w_out, w_ffn1, w_ffn2, a_in, a_ln_g, a_ln_b, a_ws, a_bs, b_in, b_conv_w, b_conv_b, b_dt_bias, b_a_log, b_d, b_gnorm, final_norm, loss_target, m_norm_mix, m_norm_ffn, m_mem_norm, m_w_kv, m_w_out, m_w_ffn1, m_w_ffn2, m_a_in, m_a_ln_g, m_a_ln_b, m_a_ws, m_a_bs, m_b_in, m_b_conv_w, m_b_conv_b, m_b_dt_bias, m_b_a_log, m_b_d, m_b_gnorm, m_final_norm, v_norm_mix, v_norm_ffn, v_mem_norm, v_w_kv, v_w_out, v_w_ffn1, v_w_ffn2, v_a_in, v_a_ln_g, v_a_ln_b, v_a_ws, v_a_bs, v_b_in, v_b_conv_w, v_b_conv_b, v_b_dt_bias, v_b_a_log, v_b_d, v_b_gnorm, v_final_norm):
    s = x.shape[1]
    xs = x.reshape(s, D_MODEL)
    mems = mem.reshape(N_MEM, D_MODEL)
    target = loss_target.reshape(s, D_MODEL)
    ax, ay, ac = lax.axis_index("x"), lax.axis_index("y"), lax.axis_index("c")
    me = 4 * ax + 2 * ay + ac
    xyc = jnp.stack([ax, ay, ac]).astype(jnp.int32)

    b_cols = b_in.shape[2]
    lay_w = [("col", 512, (2, D_MODEL, D_FF)), ("row", 512, (2, D_FF, D_MODEL)), ("row", 384, (2, 3 * D_MODEL, D_MODEL)),
             ("col", 256, (2, D_MODEL, 2 * X_WIDTH)), ("col", 640, (1, D_MODEL, 5 * D_MODEL)),
             ("blk", 0, (N_DEV, D_MODEL, b_cols))]
    shards = [w_ffn1.astype(_ACT), w_ffn2.astype(_ACT), w_out.astype(_ACT), w_kv.astype(_ACT), a_in.astype(_ACT),
              b_in[0].astype(_ACT)]
    W1, W2, WO, WKV, WA, wb_blk = _all_gather(shards, lay_w, "ag_weights")
    wb_full = jnp.transpose(wb_blk, (1, 0, 2)).reshape(D_MODEL, N_DEV * b_cols)
    dt0 = D_INNER + CONV_DIM
    WB = jnp.concatenate([wb_full[:, :dt0], wb_full[:, dt0 + SSM_HEADS:]], axis=1)
    WBDT = jnp.pad(wb_full[:, dt0:dt0 + SSM_HEADS], ((0, 0), (0, HPAD - SSM_HEADS)))

    row = lambda a: a.reshape(1, -1)
    nmix = [row(norm_mix[0]), row(norm_mix[1])]
    nffn = [row(norm_ffn[0]), row(norm_ffn[1])]
    nmem = [row(mem_norm[0]), row(mem_norm[1])]
    fin = row(final_norm)
    lng, lnb = a_ln_g.reshape(1, D_INNER), a_ln_b.reshape(1, D_INNER)
    ws = a_ws[0]
    bs3 = a_bs[0].reshape(A_GROUPS, CHUNK, 1)
    pad_h = lambda a: jnp.pad(a.reshape(-1), (0, HPAD - SSM_HEADS))
    bias_row, bias_col = pad_h(b_dt_bias).reshape(1, HPAD), pad_h(b_dt_bias).reshape(HPAD, 1)
    alog_row, alog_col = pad_h(b_a_log).reshape(1, HPAD), pad_h(b_a_log).reshape(HPAD, 1)
    dfull = jnp.repeat(b_d.reshape(-1), SSM_P).reshape(1, D_INNER)

    (small_w,) = _all_gather([_pack([b_conv_w[0], b_conv_b[0], b_gnorm[0]])[0]],
                             [("blk", 0, (N_DEV, 32, 128))], "ag_small_w")
    cw_sh, cb_sh, gn_sh = 4 * 384, 384, 256
    sw = small_w.reshape(N_DEV, 32 * 128)
    conv_w = jnp.transpose(sw[:, :cw_sh].reshape(N_DEV, CONV_K, 384), (1, 0, 2)).reshape(CONV_K, CONV_DIM)
    conv_b = sw[:, 2048:2048 + cb_sh].reshape(1, CONV_DIM)
    gnorm = sw[:, 3072:3072 + gn_sh].reshape(1, D_INNER)

    kvs, mns = [], []
    for i in range(2):
        mn = _rms_fwd(mems, nmem[i], f"mem_norm{i}")
        mns.append(mn)
        kvs.append(_mm(mn, WKV, m=N_MEM, n=2 * X_WIDTH, k=D_MODEL, b_at=(i, 0, 0), name=f"kv{i}"))

    def ffn_fwd(h, i):
        f = _rms_fwd(h, nffn[i], f"ffn_norm{i}")
        p = _mm(f, W1, m=s, n=D_FF, k=D_MODEL, b_at=(i, 0, 0), out_dtype=_ACT, name=f"ffn_up{i}")
        hn = _mm(p, W2, m=s, n=D_MODEL, k=D_FF, b_at=(i, 0, 0), a_pro="relu2", add=h, name=f"ffn_down{i}")
        return f, p, hn

    def out_proj(h, mix, mo, i):
        t = _mm(mix, WO, m=s, n=D_MODEL, k=D_INNER, b_at=(i, 0, 0), add=h, name=f"out_mix{i}")
        return _mm(mo, WO, m=s, n=D_MODEL, k=X_WIDTH, b_at=(i, D_INNER, 0), add=t, name=f"out_mem{i}")

    a0 = _rms_fwd(xs, nmix[0], "mix_norm0")
    proj_a = _mm(a0, WA, m=s, n=5 * D_MODEL, k=D_MODEL, b_at=(0, 0, 0), name="proj_a")
    mix_a = _gmlp_fwd(proj_a, lng, lnb, ws, bs3, "gmlp_fwd")
    mo_a = _attn_fwd(proj_a, 4, kvs[0], "attn_fwd0")
    h1 = out_proj(xs, mix_a, mo_a, 0)
    f0, p0, h2 = ffn_fwd(h1, 0)

    a1 = _rms_fwd(h2, nmix[1], "mix_norm1")
    proj_b = _mm(a1, WB, m=s, n=6 * D_MODEL, k=D_MODEL, name="proj_b")
    dt_raw = _mm(a1, WBDT, m=s, n=HPAD, k=D_MODEL, name="proj_dt")
    dt_raw_t = dt_raw.T
    xbc = _conv_fwd(proj_b, conv_w, conv_b, "conv_fwd")
    y_ssd, states = _ssd_fwd(xbc, dt_raw, dt_raw_t, bias_row, bias_col, alog_row, alog_col, dfull, "ssd_fwd")
    mix_b = _gate_fwd(y_ssd, proj_b, gnorm, "gate_fwd")
    mo_b = _attn_fwd(proj_b, 5, kvs[1], "attn_fwd1")
    h3 = out_proj(h2, mix_b, mo_b, 1)
    f1, p1, h4 = ffn_fwd(h3, 1)

    loss_part, dh, d_fin = _loss_head(h4, fin, target, "loss_head")
    loss = lax.psum(loss_part[0, 0], ("x", "y", "c"))

    g_f1, g_f2, g_out, g_kv = [None, None], [None, None], [None, None], [None, None]
    d_nffn, d_nmix, d_nmem = [None, None], [None, None], [None, None]

    def ffn_bwd(dh, h_in, f, p, i):
        dp = _mm(dh, W2, m=s, n=D_FF, k=D_MODEL, tb=True, b_at=(i, 0, 0), epi_p=p, out_dtype=_ACT, name=f"ffn_down_dx{i}")
        g_f2[i] = _mm(p, dh, m=D_FF, n=D_MODEL, k=s, ta=True, a_pro="relu2", out_dtype=_ACT, name=f"ffn_down_dw{i}")
        g_f1[i] = _mm(f, dp, m=D_MODEL, n=D_FF, k=s, ta=True, out_dtype=_ACT, name=f"ffn_up_dw{i}")
        df = _mm(dp, W1, m=s, n=D_MODEL, k=D_FF, tb=True, b_at=(i, 0, 0), name=f"ffn_up_dx{i}")
        dh_in, d_nffn[i] = _rms_bwd(h_in, nffn[i], df, dh, f"ffn_norm_bwd{i}")
        return dh_in

    def out_bwd(dh, mix, mo, i):
        dcat = _mm(dh, WO, m=s, n=3 * D_MODEL, k=D_MODEL, tb=True, b_at=(i, 0, 0), out_dtype=_ACT, name=f"out_dx{i}")
        g = _mm(mix, dh, m=D_INNER, n=D_MODEL, k=s, ta=True, out_dtype=_ACT, out_full=(3 * D_MODEL, D_MODEL),
                name=f"out_mix_dw{i}")
        g_out[i] = _mm(mo, dh, m=X_WIDTH, n=D_MODEL, k=s, ta=True, out=g, out_at=(None, D_INNER, 0), name=f"out_mem_dw{i}")
        return dcat

    def mem_bwd(dkv, i):
        g_kv[i] = _mm(mns[i], dkv, m=D_MODEL, n=2 * X_WIDTH, k=N_MEM, ta=True, out_dtype=_ACT, name=f"kv_dw{i}")
        dmn = _mm(dkv, WKV, m=N_MEM, n=D_MODEL, k=2 * X_WIDTH, tb=True, b_at=(i, 0, 0), name=f"kv_dx{i}")
        _, d_nmem[i] = _rms_bwd(mems, nmem[i], dmn, None, f"mem_norm_bwd{i}")

    dh3 = ffn_bwd(dh, h3, f1, p1, 1)
    dcat_b = out_bwd(dh3, mix_b, mo_b, 1)
    dy_ssd, dz, d_gnorm = _gate_bwd(y_ssd, proj_b, gnorm, dcat_b, "gate_bwd")
    dq_b, dkv_b = _attn_bwd(proj_b, 5, kvs[1], dcat_b, "attn_bwd1")
    mem_bwd(dkv_b, 1)
    dxbc, ddt_raw, d_alog, d_dskip, d_dtbias = _ssd_bwd(
        xbc, dt_raw, dt_raw_t, bias_row, bias_col, alog_row, alog_col, dfull, dy_ssd, states, "ssd_bwd")
    dxbc_raw, d_convw, d_convb = _conv_bwd(proj_b, conv_w, conv_b, dxbc, "conv_bwd")
    gb = _mm(a1, dz, m=D_MODEL, n=D_INNER, k=s, ta=True, out_dtype=_ACT, out_full=(D_MODEL, 6 * D_MODEL), name="proj_b_dw_z")
    gb = _mm(a1, dxbc_raw, m=D_MODEL, n=CONV_DIM, k=s, ta=True, out=gb, out_at=(None, 0, D_INNER), name="proj_b_dw_xbc")
    gb = _mm(a1, dq_b, m=D_MODEL, n=X_WIDTH, k=s, ta=True, out=gb, out_at=(None, 0, D_INNER + CONV_DIM), name="proj_b_dw_q")
    gb_dt = _mm(a1, ddt_raw, m=D_MODEL, n=HPAD, k=s, ta=True, out_dtype=_ACT, name="proj_b_dw_dt")
    da1 = _mm(dz, WB, m=s, n=D_MODEL, k=D_INNER, tb=True, name="proj_b_dx_z")
    da1 = _mm(dxbc_raw, WB, m=s, n=D_MODEL, k=CONV_DIM, tb=True, b_at=(None, 0, D_INNER), add=da1, name="proj_b_dx_xbc")
    da1 = _mm(dq_b, WB, m=s, n=D_MODEL, k=X_WIDTH, tb=True, b_at=(None, 0, D_INNER + CONV_DIM), add=da1, name="proj_b_dx_q")
    da1 = _mm(ddt_raw, WBDT, m=s, n=D_MODEL, k=HPAD, tb=True, add=da1, name="proj_b_dx_dt")
    dh2, d_nmix[1] = _rms_bwd(h2, nmix[1], da1, dh3, "mix_norm_bwd1")

    dh1 = ffn_bwd(dh2, h1, f0, p0, 0)
    dcat_a = out_bwd(dh1, mix_a, mo_a, 0)
    dp_uv, d_ws, d_bs3, d_lng, d_lnb = _gmlp_bwd(proj_a, dcat_a, lng, lnb, ws, bs3, "gmlp_bwd")
    dq_a, dkv_a = _attn_bwd(proj_a, 4, kvs[0], dcat_a, "attn_bwd0")
    mem_bwd(dkv_a, 0)
    ga = _mm(a0, dp_uv, m=D_MODEL, n=2 * D_INNER, k=s, ta=True, out_dtype=_ACT, out_full=(D_MODEL, 5 * D_MODEL), name="proj_a_dw_uv")
    ga = _mm(a0, dq_a, m=D_MODEL, n=X_WIDTH, k=s, ta=True, out=ga, out_at=(None, 0, 2 * D_INNER), name="proj_a_dw_q")
    da0 = _mm(dp_uv, WA, m=s, n=D_MODEL, k=2 * D_INNER, tb=True, b_at=(0, 0, 0), name="proj_a_dx_uv")
    da0 = _mm(dq_a, WA, m=s, n=D_MODEL, k=X_WIDTH, tb=True, b_at=(0, 0, 2 * D_INNER), add=da0, name="proj_a_dx_q")
    grad_x, d_nmix[0] = _rms_bwd(xs, nmix[0], da0, dh1, "mix_norm_bwd0")

    gb_full = jnp.concatenate([gb[:, :dt0], gb_dt[:, :SSM_HEADS], gb[:, dt0:]], axis=1)
    gb_blk = jnp.transpose(gb_full.reshape(D_MODEL, N_DEV, b_cols), (1, 0, 2))
    fams = []
    for i in range(2):
        fams.append((g_f1[i], ("col", 512, (D_MODEL, 512))))
        fams.append((g_f2[i], ("row", 512, (512, D_MODEL))))
        fams.append((g_out[i], ("row", 384, (384, D_MODEL))))
        fams.append((g_kv[i], ("col", 256, (D_MODEL, 256))))
    fams.append((ga, ("col", 640, (D_MODEL, 640))))
    fams.append((gb_blk, ("blk", 0, (D_MODEL, b_cols))))

    grads3, lays3 = [], []
    for g, (kind, width, shape) in fams:
        if kind == "blk":
            grads3.append(g)
            lays3.append((kind, width, shape))
        else:
            grads3.append(g.reshape((1,) + g.shape))
            lays3.append((kind, width, (1,) + shape))
    recv1 = _rs_to_sibling(grads3, lays3, "rs_sibling")
    parts = []
    for t, (g, lay) in enumerate(fams):
        r1 = recv1[t].reshape((4,) + lay[2])
        parts.append(_rs_chip_sum(g, r1, lay, xyc, f"rs_chip_sum{t}"))
    recv2 = _rs_across_chips(parts, "rs_chips")

    def big_update(w, m, v, idx, nlayer):
        res = None
        for i in range(nlayer):
            t = idx[i]
            plist = [(parts[t], 0), (recv2[t], 0), (recv2[t], 1), (recv2[t], 2)]
            res = _adamw(w, m, v, plist, f"adamw_{t}", layer=i, prev=res)
        return res

    r_f1 = big_update(w_ffn1, m_w_ffn1, v_w_ffn1, [0, 4], 2)
    r_f2 = big_update(w_ffn2, m_w_ffn2, v_w_ffn2, [1, 5], 2)
    r_out = big_update(w_out, m_w_out, v_w_out, [2, 6], 2)
    r_kv = big_update(w_kv, m_w_kv, v_w_kv, [3, 7], 2)
    r_a = big_update(a_in, m_a_in, v_a_in, [8], 1)
    r_b = big_update(b_in, m_b_in, v_b_in, [9], 1)

    rep_names = ["norm_mix", "norm_ffn", "mem_norm", "a_ln_g", "a_ln_b", "a_ws", "a_bs", "b_dt_bias", "b_a_log", "b_d",
                 "final_norm"]
    rep_grads = [jnp.concatenate(d_nmix, axis=0), jnp.concatenate(d_nffn, axis=0), jnp.concatenate(d_nmem, axis=0),
                 d_lng, d_lnb, d_ws.reshape(1, A_GROUPS, CHUNK, CHUNK), d_bs3.reshape(1, A_GROUPS, CHUNK),
                 d_dtbias[:, :SSM_HEADS], d_alog[:, :SSM_HEADS], d_dskip[:, :SSM_HEADS], d_fin.reshape(D_MODEL)]
    rep_w = [norm_mix, norm_ffn, mem_norm, a_ln_g, a_ln_b, a_ws, a_bs, b_dt_bias, b_a_log, b_d, final_norm]
    rep_m = [m_norm_mix, m_norm_ffn, m_mem_norm, m_a_ln_g, m_a_ln_b, m_a_ws, m_a_bs, m_b_dt_bias, m_b_a_log, m_b_d, m_final_norm]
    rep_v = [v_norm_mix, v_norm_ffn, v_mem_norm, v_a_ln_g, v_a_ln_b, v_a_ws, v_a_bs, v_b_dt_bias, v_b_a_log, v_b_d, v_final_norm]
    rep_grads = [g.reshape(w.shape) for g, w in zip(rep_grads, rep_w)]
    sh_grads = [d_convw, d_convb, d_gnorm]
    g_pack, g_layout = _pack(rep_grads + sh_grads)
    n_rep = len(rep_grads)
    (g_all,) = _all_gather([g_pack], [("blk", 0, (N_DEV,) + g_pack.shape)], "ag_small_grads")
    g_small = _sum8(g_all, "sum_small_grads")
    g_list = _unpack(g_small, g_layout)
    wp, w_layout = _pack(rep_w)
    mp, _ = _pack(rep_m)
    vp, _ = _pack(rep_v)
    gp, _ = _pack(g_list[:n_rep])
    rep_res = [_unpack(o, w_layout) for o in _adamw(wp, mp, vp, [(gp, None)], "adamw_replicated", tr=88)]

    gcw = lax.dynamic_slice_in_dim(g_list[n_rep], me * 384, 384, axis=1).reshape(1, CONV_K, 384)
    gcb = lax.dynamic_slice_in_dim(g_list[n_rep + 1], me * 384, 384, axis=1)
    ggn = lax.dynamic_slice_in_dim(g_list[n_rep + 2], me * 256, 256, axis=1)
    sh_w = [b_conv_w, b_conv_b, b_gnorm]
    sh_m = [m_b_conv_w, m_b_conv_b, m_b_gnorm]
    sh_v = [v_b_conv_w, v_b_conv_b, v_b_gnorm]
    swp, sw_layout = _pack(sh_w)
    smp, _ = _pack(sh_m)
    svp, _ = _pack(sh_v)
    sgp, _ = _pack([gcw, gcb, ggn])
    sh_res = [_unpack(o, sw_layout) for o in _adamw(swp, smp, svp, [(sgp, None)], "adamw_sharded_small", tr=8)]

    names = ["norm_mix", "norm_ffn", "mem_norm", "w_kv", "w_out", "w_ffn1", "w_ffn2", "a_in", "a_ln_g", "a_ln_b", "a_ws",
             "a_bs", "b_in", "b_conv_w", "b_conv_b", "b_dt_bias", "b_a_log", "b_d", "b_gnorm", "final_norm"]
    big = {"w_kv": r_kv, "w_out": r_out, "w_ffn1": r_f1, "w_ffn2": r_f2, "a_in": r_a, "b_in": r_b}
    sh_names = ["b_conv_w", "b_conv_b", "b_gnorm"]
    outs = [loss, grad_x.reshape(x.shape)]
    for kind in range(4):
        for nm in names:
            if nm in big:
                outs.append(big[nm][kind])
            elif nm in sh_names:
                outs.append(sh_res[kind][sh_names.index(nm)])
            else:
                outs.append(rep_res[kind][rep_names.index(nm)])
    return tuple(outs)
```

```python
import functools
import math

import jax
import jax.numpy as jnp
from jax import lax
from jax.experimental import pallas as pl
from jax.experimental.pallas import tpu as pltpu

F32 = jnp.float32
_MXU = jnp.bfloat16
_ACT = jnp.bfloat16
_HI = lax.Precision.HIGHEST

D_MODEL = 1024
CHUNK = 128
N_MEM = 256
D_INNER = 2048
A_GROUPS = 8
A_GW = D_INNER // A_GROUPS
SSM_HEADS = 32
SSM_P = 64
SSM_GROUPS = 4
SSM_GW = D_INNER // SSM_GROUPS
SSM_N = 128
CONV_K = 4
CONV_DIM = 3072
X_HEADS = 4
X_HD = 256
X_WIDTH = 1024
D_FF = 4096
EPS = 1e-6
HPAD = 128
N_DEV = 8

ADAM_LR = 0.001
ADAM_B1 = 0.9
ADAM_B2 = 0.999
ADAM_EPS = 1e-08
ADAM_WD = 0.01
ADAM_STEP = 10

VMEM_BIG = 56 * 1024 * 1024
MESH = pl.DeviceIdType.MESH


def _cp(vmem=None):
    if vmem is None:
        return pltpu.CompilerParams()
    return pltpu.CompilerParams(vmem_limit_bytes=vmem)


def _dot(a, b, dims=((1,), (0,))):
    return lax.dot_general(a.astype(_MXU), b.astype(_MXU), (dims, ((), ())), preferred_element_type=F32)


def _dot_nt(a, b):
    return _dot(a, b, ((1,), (1,)))


def _dot_tn(a, b):
    return _dot(a, b, ((0,), (0,)))


def _dot_hi(a, b, dims=((1,), (0,))):
    return lax.dot_general(a.astype(F32), b.astype(F32), (dims, ((), ())), precision=_HI, preferred_element_type=F32)


def _sigmoid(x):
    return 1.0 / (1.0 + jnp.exp(-x))


def _gelu(x):
    return 0.5 * x * (1.0 + lax.erf(x * (1.0 / math.sqrt(2.0))))


def _gelu_grad(x):
    return 0.5 * (1.0 + lax.erf(x * (1.0 / math.sqrt(2.0)))) + x * jnp.exp(-0.5 * x * x) * (1.0 / math.sqrt(2.0 * math.pi))


def _softplus(x):
    return jnp.maximum(x, 0.0) + jnp.log1p(jnp.exp(-jnp.abs(x)))


def _iota(shape, dim):
    return lax.broadcasted_iota(jnp.int32, shape, dim)


MM_VMEM_BUDGET = 40 * 1024 * 1024
HBM_BYTES_PER_S = 2.5e12
GRID_STEP_S = 0.35e-6
VMEM_ACC_BYTES_PER_S = 6e12


def _divisors(dim, unit):
    out = [d for d in range(unit, min(dim, 2048) + 1, unit) if dim % d == 0]
    return out if out else [dim]


def _mm_tiles(m, n, k, sa, sb, s_mn, a_pro, offsets):
    best = None
    (a_r0, a_c0, ta), (b_r0, b_c0, tb), (o_r0, o_c0) = offsets
    for tm in _divisors(m, 128):
        for tn in _divisors(n, 128):
            for tk in [k // d for d in (1, 2, 3, 4, 6, 8) if k % d == 0 and (k // d) % 128 == 0]:
                a_t = (tk, tm) if ta else (tm, tk)
                b_t = (tn, tk) if tb else (tk, tn)
                if a_r0 % a_t[0] or a_c0 % a_t[1] or b_r0 % b_t[0] or b_c0 % b_t[1] or o_r0 % tm or o_c0 % tn:
                    continue
                nk = k // tk
                vmem = 2 * (tm * tk * sa + tk * tn * sb + tm * tn * s_mn) + tm * tn * 4 * (2 if nk > 1 else 1)
                if a_pro or sa == 4:
                    vmem += tm * tk * 6
                if sb == 4:
                    vmem += tk * tn * 2
                if vmem > MM_VMEM_BUDGET:
                    continue
                gi, gj = m // tm, n // tn
                for j_inner in (True, False):
                    if nk > 1:
                        traffic = gj * m * k * sa + gi * k * n * sb
                    elif j_inner:
                        traffic = m * k * sa + gi * k * n * sb
                    else:
                        traffic = gj * m * k * sa + k * n * sb
                    traffic += m * n * s_mn + (tm * tk * sa + tk * tn * sb)
                    cost = traffic / HBM_BYTES_PER_S + gi * gj * nk * GRID_STEP_S
                    if nk > 1:
                        cost += m * n * 8 * nk / VMEM_ACC_BYTES_PER_S
                    if best is None or cost < best[0]:
                        best = (cost, tm, tn, tk, j_inner)
    assert best is not None, (m, n, k)
    return best[1:]


def _mm(a, b, *, m, n, k, name, ta=False, tb=False, a_at=(None, 0, 0), b_at=(None, 0, 0),
        out_dtype=F32, add=None, epi_p=None, epi_at=(None, 0, 0), out=None, out_at=(None, 0, 0),
        out_full=None, a_pro=None):
    s_mn = jnp.dtype(out.dtype if out is not None else out_dtype).itemsize
    s_mn += add.dtype.itemsize if add is not None else 0
    s_mn += epi_p.dtype.itemsize if epi_p is not None else 0
    tm, tn, tk, j_inner = _mm_tiles(m, n, k, a.dtype.itemsize, b.dtype.itemsize, s_mn, a_pro is not None,
                                    ((a_at[1], a_at[2], ta), (b_at[1], b_at[2], tb), (out_at[1], out_at[2])))
    nk = k // tk

    def spec(at, tr, tc, rsel, csel):
        lead, r0, c0 = at
        assert r0 % tr == 0 and c0 % tc == 0, (name, at, tr, tc)
        rb, cb = r0 // tr, c0 // tc
        if lead is None:
            return pl.BlockSpec((tr, tc), lambda g0, g1, kk: (rb + rsel(g0, g1, kk), cb + csel(g0, g1, kk)))
        return pl.BlockSpec((None, tr, tc), lambda g0, g1, kk: (lead, rb + rsel(g0, g1, kk), cb + csel(g0, g1, kk)))

    gi = (lambda g0, g1, kk: g0) if j_inner else (lambda g0, g1, kk: g1)
    gj = (lambda g0, g1, kk: g1) if j_inner else (lambda g0, g1, kk: g0)
    gk = lambda g0, g1, kk: kk
    a_spec = spec(a_at, tk, tm, gk, gi) if ta else spec(a_at, tm, tk, gi, gk)
    b_spec = spec(b_at, tn, tk, gj, gk) if tb else spec(b_at, tk, tn, gk, gj)
    dims = ((0,), (0,)) if ta else (((1,), (1,)) if tb else ((1,), (0,)))
    assert not (ta and tb)

    operands, in_specs = [a, b], [a_spec, b_spec]
    if add is not None:
        operands.append(add)
        in_specs.append(spec((None, 0, 0), tm, tn, gi, gj))
    if epi_p is not None:
        operands.append(epi_p)
        in_specs.append(spec(epi_at, tm, tn, gi, gj))
    aliases = {}
    if out is not None:
        aliases = {len(operands): 0}
        operands.append(out)
        in_specs.append(pl.BlockSpec(memory_space=pl.ANY))
        out_struct = jax.ShapeDtypeStruct(out.shape, out.dtype)
        out_dtype = out.dtype
    else:
        out_struct = jax.ShapeDtypeStruct(out_full if out_full is not None else (m, n), out_dtype)
    has_add, has_epi, has_alias = add is not None, epi_p is not None, out is not None

    def body(*refs):
        a_ref, b_ref = refs[0], refs[1]
        pos = 2
        add_ref = epi_ref = None
        if has_add:
            add_ref = refs[pos]
            pos += 1
        if has_epi:
            epi_ref = refs[pos]
            pos += 1
        if has_alias:
            pos += 1
        o_ref = refs[pos]

        def finish(r):
            if has_add:
                r = r + add_ref[...].astype(F32)
            if has_epi:
                r = r * (2.0 * jnp.maximum(epi_ref[...].astype(F32), 0.0))
            o_ref[...] = r.astype(o_ref.dtype)

        av = a_ref[...]
        if a_pro == "relu2":
            av = jnp.square(jnp.maximum(av.astype(F32), 0.0))
        part = _dot(av, b_ref[...], dims)
        if nk == 1:
            finish(part)
        else:
            acc_ref = refs[pos + 1]
            kk = pl.program_id(2)

            @pl.when(kk == 0)
            def _():
                acc_ref[...] = part

            @pl.when(kk > 0)
            def _():
                acc_ref[...] += part

            @pl.when(kk == nk - 1)
            def _():
                finish(acc_ref[...])

    grid = (m // tm, n // tn, nk) if j_inner else (n // tn, m // tm, nk)
    return pl.pallas_call(
        body, name=name, grid=grid, in_specs=in_specs,
        out_specs=spec(out_at, tm, tn, gi, gj), out_shape=out_struct,
        scratch_shapes=[pltpu.VMEM((tm, tn), F32)] if nk > 1 else [], input_output_aliases=aliases,
        compiler_params=_cp(VMEM_BIG))(*operands)


def _rms_fwd(x, g, name, tm=256):
    s, d = x.shape
    tm = min(tm, s)

    def body(x_ref, g_ref, o_ref):
        xv = x_ref[...]
        r = lax.rsqrt(jnp.mean(xv * xv, axis=-1, keepdims=True) + EPS)
        o_ref[...] = (xv * r * g_ref[...]).astype(o_ref.dtype)

    return pl.pallas_call(
        body, name=name, grid=(s // tm,),
        in_specs=[pl.BlockSpec((tm, d), lambda i: (i, 0)), pl.BlockSpec((1, d), lambda i: (0, 0))],
        out_specs=pl.BlockSpec((tm, d), lambda i: (i, 0)),
        out_shape=jax.ShapeDtypeStruct((s, d), _ACT))(x, g)


def _rms_bwd(x, g, dy, dres, name, tm=256):
    s, d = x.shape
    tm = min(tm, s)
    has_res = dres is not None

    def body(*refs):
        if has_res:
            x_ref, g_ref, dy_ref, dres_ref, dx_ref, dxa_ref, dg_ref = refs
        else:
            x_ref, g_ref, dy_ref, dx_ref, dxa_ref, dg_ref = refs

        @pl.when(pl.program_id(0) == 0)
        def _():
            dg_ref[...] = jnp.zeros_like(dg_ref)

        xv = x_ref[...]
        dyv = dy_ref[...].astype(F32)
        r = lax.rsqrt(jnp.mean(xv * xv, axis=-1, keepdims=True) + EPS)
        xh = xv * r
        dyg = dyv * g_ref[...]
        dx = r * (dyg - xh * jnp.mean(dyg * xh, axis=-1, keepdims=True))
        if has_res:
            dx = dx + dres_ref[...]
        dx_ref[...] = dx
        dxa_ref[...] = dx.astype(dxa_ref.dtype)
        dg_ref[...] += jnp.sum(dyv * xh, axis=0, keepdims=True)

    row = pl.BlockSpec((tm, d), lambda i: (i, 0))
    vec = pl.BlockSpec((1, d), lambda i: (0, 0))
    in_specs = [row, vec, row] + ([row] if has_res else [])
    operands = [x, g, dy] + ([dres] if has_res else [])
    return pl.pallas_call(
        body, name=name, grid=(s // tm,), in_specs=in_specs, out_specs=[row, row, vec],
        out_shape=[jax.ShapeDtypeStruct((s, d), F32), jax.ShapeDtypeStruct((s, d), _ACT),
                   jax.ShapeDtypeStruct((1, d), F32)])(*operands)


def _loss_head(h, g, target, name, tm=256):
    s, d = h.shape
    tm = min(tm, s)

    def body(h_ref, g_ref, t_ref, loss_ref, dh_ref, dha_ref, dg_ref):
        @pl.when(pl.program_id(0) == 0)
        def _():
            dg_ref[...] = jnp.zeros_like(dg_ref)
            loss_ref[...] = jnp.zeros_like(loss_ref)

        xv = h_ref[...]
        r = lax.rsqrt(jnp.mean(xv * xv, axis=-1, keepdims=True) + EPS)
        xh = xv * r
        err = xh * g_ref[...] - t_ref[...]
        loss_ref[...] += jnp.full(loss_ref.shape, 0.5 * jnp.sum(jnp.mean(err * err, axis=-1, keepdims=True)), F32)
        dyv = err * (1.0 / d)
        dyg = dyv * g_ref[...]
        dh = r * (dyg - xh * jnp.mean(dyg * xh, axis=-1, keepdims=True))
        dh_ref[...] = dh
        dha_ref[...] = dh.astype(dha_ref.dtype)
        dg_ref[...] += jnp.sum(dyv * xh, axis=0, keepdims=True)

    row = pl.BlockSpec((tm, d), lambda i: (i, 0))
    vec = pl.BlockSpec((1, d), lambda i: (0, 0))
    return pl.pallas_call(
        body, name=name, grid=(s // tm,), in_specs=[row, vec, row],
        out_specs=[pl.BlockSpec((1, 128), lambda i: (0, 0)), row, row, vec],
        out_shape=[jax.ShapeDtypeStruct((1, 128), F32), jax.ShapeDtypeStruct((s, d), F32),
                   jax.ShapeDtypeStruct((s, d), _ACT), jax.ShapeDtypeStruct((1, d), F32)])(h, g, target)


def _gmlp_parts(pu, pv, lng, lnb):
    u = _gelu(pu)
    v = _gelu(pv)
    mu = jnp.mean(v, axis=-1, keepdims=True)
    vc = v - mu
    rstd = lax.rsqrt(jnp.mean(vc * vc, axis=-1, keepdims=True) + EPS)
    xhat = vc * rstd
    vn = xhat * lng + lnb
    return u, xhat, rstd, vn


def _gmlp_fwd(proj, lng, lnb, ws, bs3, name):
    s = proj.shape[0]

    def body(pu_ref, pv_ref, lng_ref, lnb_ref, ws_ref, bs_ref, o_ref):
        u, _, _, vn = _gmlp_parts(pu_ref[...], pv_ref[...], lng_ref[...], lnb_ref[...])
        causal = _iota((CHUNK, CHUNK), 0) >= _iota((CHUNK, CHUNK), 1)
        for g in range(A_GROUPS):
            sl = slice(g * A_GW, (g + 1) * A_GW)
            w = jnp.where(causal, ws_ref[g], 0.0)
            sv = _dot(w, vn[:, sl]) + bs_ref[g]
            o_ref[:, sl] = (u[:, sl] * sv).astype(o_ref.dtype)

    full = lambda shape: pl.BlockSpec(shape, lambda c: (0,) * len(shape))
    return pl.pallas_call(
        body, name=name, grid=(s // CHUNK,),
        in_specs=[pl.BlockSpec((CHUNK, D_INNER), lambda c: (c, 0)), pl.BlockSpec((CHUNK, D_INNER), lambda c: (c, 1)),
                  full((1, D_INNER)), full((1, D_INNER)), full((A_GROUPS, CHUNK, CHUNK)), full((A_GROUPS, CHUNK, 1))],
        out_specs=pl.BlockSpec((CHUNK, D_INNER), lambda c: (c, 0)),
        out_shape=jax.ShapeDtypeStruct((s, D_INNER + X_WIDTH), _ACT), compiler_params=_cp(VMEM_BIG))(proj, proj, lng, lnb, ws, bs3)


def _gmlp_bwd(proj, dcat, lng, lnb, ws, bs3, name):
    s = proj.shape[0]

    def body(pu_ref, pv_ref, dm_ref, lng_ref, lnb_ref, ws_ref, bs_ref, dp_ref, dws_ref, dbs_ref, dlng_ref, dlnb_ref, dvn_ref):
        @pl.when(pl.program_id(0) == 0)
        def _():
            dws_ref[...] = jnp.zeros_like(dws_ref)
            dbs_ref[...] = jnp.zeros_like(dbs_ref)
            dlng_ref[...] = jnp.zeros_like(dlng_ref)
            dlnb_ref[...] = jnp.zeros_like(dlnb_ref)

        pu, pv = pu_ref[...], pv_ref[...]
        lng = lng_ref[...]
        u, xhat, rstd, vn = _gmlp_parts(pu, pv, lng, lnb_ref[...])
        dm = dm_ref[...].astype(F32)
        causal = _iota((CHUNK, CHUNK), 0) >= _iota((CHUNK, CHUNK), 1)
        for g in range(A_GROUPS):
            sl = slice(g * A_GW, (g + 1) * A_GW)
            w = jnp.where(causal, ws_ref[g], 0.0)
            sv = _dot(w, vn[:, sl]) + bs_ref[g]
            dsv = dm[:, sl] * u[:, sl]
            dp_ref[:, sl] = (dm[:, sl] * sv * _gelu_grad(pu[:, sl])).astype(dp_ref.dtype)
            dvn_ref[:, sl] = _dot_tn(w, dsv)
            dws_ref[g] += jnp.where(causal, _dot_nt(dsv, vn[:, sl]), 0.0)
            dbs_ref[g] += jnp.sum(dsv, axis=-1, keepdims=True)
        dvn = dvn_ref[...]
        dlng_ref[...] += jnp.sum(dvn * xhat, axis=0, keepdims=True)
        dlnb_ref[...] += jnp.sum(dvn, axis=0, keepdims=True)
        dxh = dvn * lng
        dv = rstd * (dxh - jnp.mean(dxh, axis=-1, keepdims=True) - xhat * jnp.mean(dxh * xhat, axis=-1, keepdims=True))
        dp_ref[:, D_INNER:] = (dv * _gelu_grad(pv)).astype(dp_ref.dtype)

    full = lambda shape: pl.BlockSpec(shape, lambda c: (0,) * len(shape))
    return pl.pallas_call(
        body, name=name, grid=(s // CHUNK,),
        in_specs=[pl.BlockSpec((CHUNK, D_INNER), lambda c: (c, 0)), pl.BlockSpec((CHUNK, D_INNER), lambda c: (c, 1)),
                  pl.BlockSpec((CHUNK, D_INNER), lambda c: (c, 0)),
                  full((1, D_INNER)), full((1, D_INNER)), full((A_GROUPS, CHUNK, CHUNK)), full((A_GROUPS, CHUNK, 1))],
        out_specs=[pl.BlockSpec((CHUNK, 2 * D_INNER), lambda c: (c, 0)), full((A_GROUPS, CHUNK, CHUNK)),
                   full((A_GROUPS, CHUNK, 1)), full((1, D_INNER)), full((1, D_INNER))],
        out_shape=[jax.ShapeDtypeStruct((s, 2 * D_INNER + X_WIDTH), _ACT), jax.ShapeDtypeStruct((A_GROUPS, CHUNK, CHUNK), F32),
                   jax.ShapeDtypeStruct((A_GROUPS, CHUNK, 1), F32), jax.ShapeDtypeStruct((1, D_INNER), F32),
                   jax.ShapeDtypeStruct((1, D_INNER), F32)],
        scratch_shapes=[pltpu.VMEM((CHUNK, D_INNER), F32)],
        compiler_params=_cp(VMEM_BIG))(proj, proj, dcat, lng, lnb, ws, bs3)


_X_SCALE = 1.0 / math.sqrt(X_HD)


def _attn_fwd(proj, qblk, kv, cat, name, tm=256):
    s = proj.shape[0]
    tm = min(tm, s)

    def body(q_ref, kv_ref, cat_ref, o_ref):
        for h in range(X_HEADS):
            sl = slice(h * X_HD, (h + 1) * X_HD)
            k = kv_ref[:, sl]
            v = kv_ref[:, X_WIDTH + h * X_HD:X_WIDTH + (h + 1) * X_HD]
            sc = _dot_nt(q_ref[:, sl], k) * _X_SCALE
            e = jnp.exp(sc - jnp.max(sc, axis=-1, keepdims=True))
            p = e / jnp.sum(e, axis=-1, keepdims=True)
            o_ref[:, sl] = _dot(p, v).astype(o_ref.dtype)

    return pl.pallas_call(
        body, name=name, grid=(s // tm,),
        in_specs=[pl.BlockSpec((tm, X_WIDTH), lambda i: (i, qblk)), pl.BlockSpec((N_MEM, 2 * X_WIDTH), lambda i: (0, 0)),
                  pl.BlockSpec(memory_space=pl.ANY)],
        out_specs=pl.BlockSpec((tm, X_WIDTH), lambda i: (i, D_INNER // X_WIDTH)),
        out_shape=jax.ShapeDtypeStruct(cat.shape, cat.dtype), input_output_aliases={2: 0})(proj, kv, cat)


def _attn_bwd(proj, qblk, kv, dcat, dproj, name, tm=256):
    s = proj.shape[0]
    tm = min(tm, s)

    def body(q_ref, kv_ref, do_ref, dproj_ref, dq_ref, dkv_ref):
        @pl.when(pl.program_id(0) == 0)
        def _():
            dkv_ref[...] = jnp.zeros_like(dkv_ref)

        for h in range(X_HEADS):
            sl = slice(h * X_HD, (h + 1) * X_HD)
            slv = slice(X_WIDTH + h * X_HD, X_WIDTH + (h + 1) * X_HD)
            q = q_ref[:, sl]
            k = kv_ref[:, sl]
            v = kv_ref[:, slv]
            do = do_ref[:, sl].astype(F32)
            sc = _dot_nt(q, k) * _X_SCALE
            e = jnp.exp(sc - jnp.max(sc, axis=-1, keepdims=True))
            p = e / jnp.sum(e, axis=-1, keepdims=True)
            dp = _dot_nt(do, v)
            ds = p * (dp - jnp.sum(dp * p, axis=-1, keepdims=True)) * _X_SCALE
            dq_ref[:, sl] = _dot(ds, k).astype(dq_ref.dtype)
            dkv_ref[:, sl] += _dot_tn(ds, q)
            dkv_ref[:, slv] += _dot_tn(p, do)

    return pl.pallas_call(
        body, name=name, grid=(s // tm,),
        in_specs=[pl.BlockSpec((tm, X_WIDTH), lambda i: (i, qblk)), pl.BlockSpec((N_MEM, 2 * X_WIDTH), lambda i: (0, 0)),
                  pl.BlockSpec((tm, X_WIDTH), lambda i: (i, 2)), pl.BlockSpec(memory_space=pl.ANY)],
        out_specs=[pl.BlockSpec((tm, X_WIDTH), lambda i: (i, qblk)), pl.BlockSpec((N_MEM, 2 * X_WIDTH), lambda i: (0, 0))],
        out_shape=[jax.ShapeDtypeStruct(dproj.shape, dproj.dtype), jax.ShapeDtypeStruct((N_MEM, 2 * X_WIDTH), F32)],
        input_output_aliases={3: 0})(proj, kv, dcat, dproj)


CONV_TC = 256
_XBC_BLK0 = D_INNER // CONV_TC


def _shift_down(x, j):
    if j == 0:
        return x
    return jnp.where(_iota(x.shape, 0) >= j, pltpu.roll(x, j, 0), 0.0)


def _shift_up(x, j):
    if j == 0:
        return x
    n = x.shape[0]
    return jnp.where(_iota(x.shape, 0) < n - j, pltpu.roll(x, n - j, 0), 0.0)


def _conv_fwd(proj, w, b, name):
    s = proj.shape[0]

    def body(x_ref, w_ref, b_ref, o_ref):
        xv = x_ref[...]
        pre = b_ref[...] + w_ref[CONV_K - 1:CONV_K, :] * xv
        for kk in range(CONV_K - 1):
            pre = pre + w_ref[kk:kk + 1, :] * _shift_down(xv, CONV_K - 1 - kk)
        o_ref[...] = pre * _sigmoid(pre)

    return pl.pallas_call(
        body, name=name, grid=(CONV_DIM // CONV_TC,),
        in_specs=[pl.BlockSpec((s, CONV_TC), lambda j: (0, _XBC_BLK0 + j)), pl.BlockSpec((CONV_K, CONV_TC), lambda j: (0, j)),
                  pl.BlockSpec((1, CONV_TC), lambda j: (0, j))],
        out_specs=pl.BlockSpec((s, CONV_TC), lambda j: (0, j)),
        out_shape=jax.ShapeDtypeStruct((s, CONV_DIM), F32), compiler_params=_cp(VMEM_BIG))(proj, w, b)


def _conv_bwd(proj, w, b, dxbc, dproj, name):
    s = proj.shape[0]

    def body(x_ref, w_ref, b_ref, d_ref, dproj_ref, dx_ref, dw_ref, db_ref):
        xv = x_ref[...]
        pre = b_ref[...] + w_ref[CONV_K - 1:CONV_K, :] * xv
        for kk in range(CONV_K - 1):
            pre = pre + w_ref[kk:kk + 1, :] * _shift_down(xv, CONV_K - 1 - kk)
        sig = _sigmoid(pre)
        dpre = d_ref[...] * (sig * (1.0 + pre * (1.0 - sig)))
        dx = w_ref[CONV_K - 1:CONV_K, :] * dpre
        dw_ref[CONV_K - 1:CONV_K, :] = jnp.sum(dpre * xv, axis=0, keepdims=True)
        for kk in range(CONV_K - 1):
            j = CONV_K - 1 - kk
            dx = dx + w_ref[kk:kk + 1, :] * _shift_up(dpre, j)
            dw_ref[kk:kk + 1, :] = jnp.sum(dpre * _shift_down(xv, j), axis=0, keepdims=True)
        dx_ref[...] = dx.astype(dx_ref.dtype)
        db_ref[...] = jnp.sum(dpre, axis=0, keepdims=True)

    return pl.pallas_call(
        body, name=name, grid=(CONV_DIM // CONV_TC,),
        in_specs=[pl.BlockSpec((s, CONV_TC), lambda j: (0, _XBC_BLK0 + j)), pl.BlockSpec((CONV_K, CONV_TC), lambda j: (0, j)),
                  pl.BlockSpec((1, CONV_TC), lambda j: (0, j)), pl.BlockSpec((s, CONV_TC), lambda j: (0, j)),
                  pl.BlockSpec(memory_space=pl.ANY)],
        out_specs=[pl.BlockSpec((s, CONV_TC), lambda j: (0, _XBC_BLK0 + j)), pl.BlockSpec((CONV_K, CONV_TC), lambda j: (0, j)),
                   pl.BlockSpec((1, CONV_TC), lambda j: (0, j))],
        out_shape=[jax.ShapeDtypeStruct(dproj.shape, dproj.dtype), jax.ShapeDtypeStruct((CONV_K, CONV_DIM), F32),
                   jax.ShapeDtypeStruct((1, CONV_DIM), F32)], input_output_aliases={4: 0},
        compiler_params=_cp(VMEM_BIG))(proj, w, b, dxbc, dproj)


def _ssd_common(dtc_ref, dtr_ref, br_ref, bc_ref, ar_ref, ac_ref, csb_ref, cst_ref, csf_ref):
    a_row = -jnp.exp(ar_ref[...])
    dt_c = _softplus(dtc_ref[...] + br_ref[...])
    a_col = -jnp.exp(ac_ref[...])
    dt_r = _softplus(dtr_ref[...] + bc_ref[...])
    row = _iota((CHUNK, CHUNK), 0)
    col = _iota((CHUNK, CHUNK), 1)
    tril = (row >= col).astype(F32)
    triu = (row <= col).astype(F32)
    cs = _dot_hi(tril, dt_c * a_row)
    cst_ref[...] = _dot_hi(dt_r * a_col, triu)
    e64 = (jnp.right_shift(_iota((HPAD, D_INNER), 1), 6) == _iota((HPAD, D_INNER), 0)).astype(F32)
    e128 = (jnp.right_shift(_iota((HPAD, SSM_HEADS * CHUNK), 1), 7) == _iota((HPAD, SSM_HEADS * CHUNK), 0)).astype(F32)
    csb_ref[...] = _dot_hi(cs, e128)
    dt_full = _dot_hi(dt_c, e64)
    csf_ref[...] = _dot_hi(cs, e64)
    cs_full = csf_ref[...]
    cs_last = csf_ref[CHUNK - 1:CHUNK, :]
    e_full = jnp.exp(cs_full)
    f_full = jnp.exp(cs_last - cs_full)
    gamma = jnp.exp(cs_last)
    return a_row, dt_c, cs, dt_full, e_full, f_full, gamma, e64


def _ssd_lambda(csb_ref, cst_ref, h, causal):
    diff = csb_ref[:, h * CHUNK:(h + 1) * CHUNK] - cst_ref[h:h + 1, :]
    return jnp.exp(jnp.where(causal, diff, -1e30))


_SSD_VEC_SPECS = lambda: [pl.BlockSpec((1, HPAD), lambda c: (0, 0)), pl.BlockSpec((HPAD, 1), lambda c: (0, 0)),
                          pl.BlockSpec((1, HPAD), lambda c: (0, 0)), pl.BlockSpec((HPAD, 1), lambda c: (0, 0)),
                          pl.BlockSpec((1, D_INNER), lambda c: (0, 0))]


def _ssd_fwd(xbc, dtc, dtr, bias_row, bias_col, alog_row, alog_col, dfull, name):
    s = xbc.shape[0]
    nc = s // CHUNK

    def body(xbc_ref, dtc_ref, dtr_ref, br_ref, bc_ref, ar_ref, ac_ref, df_ref, y_ref, st_ref,
             ht_ref, csb_ref, cst_ref, csf_ref):
        @pl.when(pl.program_id(0) == 0)
        def _():
            ht_ref[...] = jnp.zeros_like(ht_ref)

        _, _, _, dt_full, e_full, f_full, gamma, _ = _ssd_common(
            dtc_ref, dtr_ref, br_ref, bc_ref, ar_ref, ac_ref, csb_ref, cst_ref, csf_ref)
        x = xbc_ref[:, :D_INNER]
        xdt = x * dt_full
        st_ref[...] = ht_ref[...]
        causal = _iota((CHUNK, CHUNK), 0) >= _iota((CHUNK, CHUNK), 1)
        lo = _iota((CHUNK, CHUNK), 1) < SSM_P
        for g in range(SSM_GROUPS):
            gs = slice(g * SSM_GW, (g + 1) * SSM_GW)
            bg = xbc_ref[:, D_INNER + g * SSM_N:D_INNER + (g + 1) * SSM_N]
            cg = xbc_ref[:, D_INNER + SSM_GROUPS * SSM_N + g * SSM_N:D_INNER + SSM_GROUPS * SSM_N + (g + 1) * SSM_N]
            ht = ht_ref[:, gs]
            cb = _dot_nt(cg, bg)
            yoff = e_full[:, gs] * _dot(cg, ht)
            for jp in range(SSM_GW // CHUNK):
                j = g * (SSM_GW // CHUNK) + jp
                ps = slice(j * CHUNK, (j + 1) * CHUNK)
                x2 = xdt[:, ps]
                y0 = _dot(cb * _ssd_lambda(csb_ref, cst_ref, 2 * j, causal), x2)
                y1 = _dot(cb * _ssd_lambda(csb_ref, cst_ref, 2 * j + 1, causal), x2)
                y_ref[:, ps] = (jnp.where(lo, y0, y1) + yoff[:, jp * CHUNK:(jp + 1) * CHUNK]
                                + x[:, ps] * df_ref[:, ps])
            ht_ref[:, gs] = gamma[:, gs] * ht + _dot_tn(bg, xdt[:, gs] * f_full[:, gs])

    return pl.pallas_call(
        body, name=name, grid=(nc,),
        in_specs=[pl.BlockSpec((CHUNK, CONV_DIM), lambda c: (c, 0)), pl.BlockSpec((CHUNK, HPAD), lambda c: (c, 0)),
                  pl.BlockSpec((HPAD, CHUNK), lambda c: (0, c))] + _SSD_VEC_SPECS(),
        out_specs=[pl.BlockSpec((CHUNK, D_INNER), lambda c: (c, 0)), pl.BlockSpec((None, SSM_N, D_INNER), lambda c: (c, 0, 0))],
        out_shape=[jax.ShapeDtypeStruct((s, D_INNER), F32), jax.ShapeDtypeStruct((nc, SSM_N, D_INNER), F32)],
        scratch_shapes=[pltpu.VMEM((SSM_N, D_INNER), F32), pltpu.VMEM((CHUNK, SSM_HEADS * CHUNK), F32),
                        pltpu.VMEM((HPAD, CHUNK), F32), pltpu.VMEM((CHUNK, D_INNER), F32)],
        compiler_params=_cp(VMEM_BIG))(xbc, dtc, dtr, bias_row, bias_col, alog_row, alog_col, dfull)


def _ssd_bwd(xbc, dtc, dtr, bias_row, bias_col, alog_row, alog_col, dfull, dy, states, name):
    s = xbc.shape[0]
    nc = s // CHUNK
    rev = lambda c: nc - 1 - c

    def body(xbc_ref, dtc_ref, dtr_ref, br_ref, bc_ref, ar_ref, ac_ref, df_ref, dy_ref, st_ref,
             dxbc_ref, ddt_ref, dalog_ref, dd_ref, dbias_ref,
             dht_ref, csb_ref, cst_ref, csf_ref, ddf_ref, dxs_ref, dcsf_ref, dcsl_ref):
        step = pl.program_id(0)

        @pl.when(step == 0)
        def _():
            dht_ref[...] = jnp.zeros_like(dht_ref)
            ddf_ref[...] = jnp.zeros_like(ddf_ref)
            dalog_ref[...] = jnp.zeros_like(dalog_ref)
            dbias_ref[...] = jnp.zeros_like(dbias_ref)
            dd_ref[...] = jnp.zeros_like(dd_ref)

        a_row, dt_c, _, dt_full, e_full, f_full, gamma, e64 = _ssd_common(
            dtc_ref, dtr_ref, br_ref, bc_ref, ar_ref, ac_ref, csb_ref, cst_ref, csf_ref)
        x = xbc_ref[:, :D_INNER]
        xdt = x * dt_full
        dy_all = dy_ref[...]
        ddf_ref[...] += jnp.broadcast_to(jnp.sum(dy_all * x, axis=0, keepdims=True), ddf_ref.shape)
        causal = _iota((CHUNK, CHUNK), 0) >= _iota((CHUNK, CHUNK), 1)
        lo = _iota((CHUNK, CHUNK), 1) < SSM_P
        ones = jnp.ones((CHUNK, HPAD), F32)
        head_lane = _iota((CHUNK, HPAD), 1)
        dcs_heads = jnp.zeros((CHUNK, HPAD), F32)
        for g in range(SSM_GROUPS):
            gs = slice(g * SSM_GW, (g + 1) * SSM_GW)
            b0 = D_INNER + g * SSM_N
            c0 = D_INNER + SSM_GROUPS * SSM_N + g * SSM_N
            bg = xbc_ref[:, b0:b0 + SSM_N]
            cg = xbc_ref[:, c0:c0 + SSM_N]
            ht = st_ref[:, gs]
            dht = dht_ref[:, gs]
            dyg = dy_all[:, gs]
            eg, fg, gg = e_full[:, gs], f_full[:, gs], gamma[:, gs]
            z = _dot(cg, ht)
            dz = dyg * eg
            dcg = _dot_nt(dz, ht)
            dht_new = _dot_tn(cg, dz) + gg * dht
            xf = xdt[:, gs] * fg
            dxf = _dot(bg, dht)
            dbg = _dot_nt(xf, dht)
            dff = dxf * xf
            dcsf_ref[:, gs] = dyg * eg * z - dff
            dcsl_ref[:, gs] = jnp.broadcast_to(
                jnp.sum(dff, axis=0, keepdims=True) + jnp.sum(dht * ht, axis=0, keepdims=True) * gg, (8, SSM_GW))
            cb = _dot_nt(cg, bg)
            dcb = jnp.zeros((CHUNK, CHUNK), F32)
            for jp in range(SSM_GW // CHUNK):
                j = g * (SSM_GW // CHUNK) + jp
                ps = slice(j * CHUNK, (j + 1) * CHUNK)
                x2 = xdt[:, ps]
                dy2 = dy_all[:, ps]
                dxh = []
                for hh in range(2):
                    h = 2 * j + hh
                    lam = _ssd_lambda(csb_ref, cst_ref, h, causal)
                    mh = cb * lam
                    dyh = jnp.where(lo, dy2, 0.0) if hh == 0 else jnp.where(lo, 0.0, dy2)
                    dm = _dot_nt(dyh, x2)
                    dcb = dcb + dm * lam
                    gm = dm * mh
                    rs = jnp.sum(gm, axis=1, keepdims=True)
                    csum = _dot_hi(gm, ones, ((0,), (0,)))
                    dcs_heads = dcs_heads + jnp.where(head_lane == h, rs - csum, 0.0)
                    dxh.append(_dot_tn(mh, dy2))
                dxs_ref[:, ps] = jnp.where(lo, dxh[0], dxh[1]) + dxf[:, jp * CHUNK:(jp + 1) * CHUNK] * fg[:, jp * CHUNK:(jp + 1) * CHUNK]
            dxbc_ref[:, b0:b0 + SSM_N] = (dbg + _dot_tn(dcb, cg)).astype(dxbc_ref.dtype)
            dxbc_ref[:, c0:c0 + SSM_N] = (dcg + _dot(dcb, bg)).astype(dxbc_ref.dtype)
            dht_ref[:, gs] = dht_new
        dxs = dxs_ref[...]
        dcs_heads = dcs_heads + _dot_hi(dcsf_ref[...], e64, ((1,), (1,)))
        dcs_last = _dot_hi(dcsl_ref[...], e64, ((1,), (1,)))
        dcs_heads = dcs_heads + jnp.where(_iota((CHUNK, HPAD), 0) == CHUNK - 1, dcs_last[0:1, :], 0.0)
        triu = (_iota((CHUNK, CHUNK), 0) <= _iota((CHUNK, CHUNK), 1)).astype(F32)
        dda = _dot_hi(triu, dcs_heads)
        ddt = dda * a_row + _dot_hi(dxs * x, e64, ((1,), (1,)))
        dxbc_ref[:, :D_INNER] = (dxs * dt_full + dy_all * df_ref[...]).astype(dxbc_ref.dtype)
        dalog_ref[...] += jnp.sum(dda * dt_c, axis=0, keepdims=True) * a_row
        ddt_raw = ddt * _sigmoid(dtc_ref[...] + br_ref[...])
        ddt_ref[...] = ddt_raw.astype(ddt_ref.dtype)
        dbias_ref[...] += jnp.sum(ddt_raw, axis=0, keepdims=True)

        @pl.when(step == nc - 1)
        def _():
            dd_ref[...] = _dot_hi(ddf_ref[...], e64, ((1,), (1,)))[0:1, :]

    vec = pl.BlockSpec((1, HPAD), lambda c: (0, 0))
    return pl.pallas_call(
        body, name=name, grid=(nc,),
        in_specs=[pl.BlockSpec((CHUNK, CONV_DIM), lambda c: (rev(c), 0)), pl.BlockSpec((CHUNK, HPAD), lambda c: (rev(c), 0)),
                  pl.BlockSpec((HPAD, CHUNK), lambda c: (0, rev(c)))] + _SSD_VEC_SPECS()
                 + [pl.BlockSpec((CHUNK, D_INNER), lambda c: (rev(c), 0)),
                    pl.BlockSpec((None, SSM_N, D_INNER), lambda c: (rev(c), 0, 0))],
        out_specs=[pl.BlockSpec((CHUNK, CONV_DIM), lambda c: (rev(c), 0)), pl.BlockSpec((CHUNK, HPAD), lambda c: (rev(c), 0)),
                   vec, vec, vec],
        out_shape=[jax.ShapeDtypeStruct((s, CONV_DIM), F32), jax.ShapeDtypeStruct((s, HPAD), _ACT),
                   jax.ShapeDtypeStruct((1, HPAD), F32), jax.ShapeDtypeStruct((1, HPAD), F32),
                   jax.ShapeDtypeStruct((1, HPAD), F32)],
        scratch_shapes=[pltpu.VMEM((SSM_N, D_INNER), F32), pltpu.VMEM((CHUNK, SSM_HEADS * CHUNK), F32),
                        pltpu.VMEM((HPAD, CHUNK), F32), pltpu.VMEM((CHUNK, D_INNER), F32),
                        pltpu.VMEM((8, D_INNER), F32), pltpu.VMEM((CHUNK, D_INNER), F32),
                        pltpu.VMEM((CHUNK, D_INNER), F32), pltpu.VMEM((8, D_INNER), F32)],
        compiler_params=_cp(VMEM_BIG))(xbc, dtc, dtr, bias_row, bias_col, alog_row, alog_col, dfull, dy, states)


def _gate_fwd(y, proj, gn, name, tm=256):
    s = y.shape[0]
    tm = min(tm, s)

    def body(y_ref, z_ref, gn_ref, o_ref):
        for g in range(SSM_GROUPS):
            gs = slice(g * SSM_GW, (g + 1) * SSM_GW)
            z = z_ref[:, gs]
            t = y_ref[:, gs] * (z * _sigmoid(z))
            r = lax.rsqrt(jnp.mean(t * t, axis=-1, keepdims=True) + EPS)
            o_ref[:, gs] = (t * r * gn_ref[:, gs]).astype(o_ref.dtype)

    row = pl.BlockSpec((tm, D_INNER), lambda i: (i, 0))
    return pl.pallas_call(
        body, name=name, grid=(s // tm,), in_specs=[row, row, pl.BlockSpec((1, D_INNER), lambda i: (0, 0))],
        out_specs=row, out_shape=jax.ShapeDtypeStruct((s, D_INNER + X_WIDTH), _ACT))(y, proj, gn)


def _gate_bwd(y, proj, gn, dcat, name, tm=256):
    s = y.shape[0]
    tm = min(tm, s)

    def body(y_ref, z_ref, gn_ref, dm_ref, dy_ref, dz_ref, dgn_ref):
        @pl.when(pl.program_id(0) == 0)
        def _():
            dgn_ref[...] = jnp.zeros_like(dgn_ref)

        for g in range(SSM_GROUPS):
            gs = slice(g * SSM_GW, (g + 1) * SSM_GW)
            z = z_ref[:, gs]
            yv = y_ref[:, gs]
            sig = _sigmoid(z)
            sz = z * sig
            t = yv * sz
            r = lax.rsqrt(jnp.mean(t * t, axis=-1, keepdims=True) + EPS)
            th = t * r
            dm = dm_ref[:, gs].astype(F32)
            dmg = dm * gn_ref[:, gs]
            dt_ = r * (dmg - th * jnp.mean(dmg * th, axis=-1, keepdims=True))
            dgn_ref[:, gs] += jnp.sum(dm * th, axis=0, keepdims=True)
            dy_ref[:, gs] = dt_ * sz
            dz_ref[:, gs] = (dt_ * yv * (sig * (1.0 + z * (1.0 - sig)))).astype(dz_ref.dtype)

    row = pl.BlockSpec((tm, D_INNER), lambda i: (i, 0))
    vec = pl.BlockSpec((1, D_INNER), lambda i: (0, 0))
    return pl.pallas_call(
        body, name=name, grid=(s // tm,), in_specs=[row, row, vec, row], out_specs=[row, row, vec],
        out_shape=[jax.ShapeDtypeStruct((s, D_INNER), F32), jax.ShapeDtypeStruct((s, 6 * D_MODEL), _ACT),
                   jax.ShapeDtypeStruct((1, D_INNER), F32)])(y, proj, gn, dcat)


def _block_of(kind, width):
    if kind == "col":
        return lambda ref, j: ref.at[:, :, pl.ds(pl.multiple_of(j * width, 128), width)]
    if kind == "row":
        return lambda ref, j: ref.at[:, pl.ds(pl.multiple_of(j * width, 8), width), :]
    return lambda ref, j: ref.at[j]


def _coords():
    return lax.axis_index("x"), lax.axis_index("y"), lax.axis_index("c")


def _rel_chip(x, y, k):
    return (1 - x if k & 1 else x), (1 - y if k & 2 else y)


_HBM = lambda: pl.BlockSpec(memory_space=pltpu.HBM)


def _all_gather(shards, layouts, name):
    n = len(shards)
    blocks = [_block_of(kind, width) for kind, width, _ in layouts]

    def body(*refs):
        ins, outs = refs[:n], refs[n:2 * n]
        send_sems, recv_sems, local_sems = refs[2 * n:]
        x, y, c = _coords()
        sibling = (x, y, 1 - c)

        def copy(t, k, chip, core, to, src=None):
            dst = blocks[t](outs[t], 4 * chip[0] + 2 * chip[1] + core)
            return pltpu.make_async_remote_copy(
                src_ref=dst if src is None else src, dst_ref=dst, send_sem=send_sems.at[t, k],
                recv_sem=recv_sems.at[t, k], device_id=to, device_id_type=MESH)

        started = []
        for t in range(n):
            mine = pltpu.make_async_copy(ins[t], blocks[t](outs[t], 4 * x + 2 * y + c), local_sems.at[t])
            mine.start()
            started.append(mine)
        sends = []
        for t in range(n):
            for k in range(4):
                px, py = _rel_chip(x, y, k)
                cp = copy(t, k, (x, y), c, (px, py, 1 - c if k == 0 else c), src=ins[t])
                cp.start()
                sends.append(cp)
        for t in range(n):
            for k in range(1, 4):
                chip = _rel_chip(x, y, k)
                copy(t, k, chip, c, sibling).wait_recv()
                fwd = copy(t, 3 + k, chip, c, sibling)
                fwd.start()
                sends.append(fwd)
        for t in range(n):
            copy(t, 0, (x, y), 1 - c, sibling).wait_recv()
            for k in range(1, 4):
                copy(t, 3 + k, _rel_chip(x, y, k), 1 - c, sibling).wait_recv()
        for cp in sends:
            cp.wait_send()
        for mine in started:
            mine.wait()

    return pl.pallas_call(
        body, name=name, in_specs=[_HBM()] * n, out_specs=[_HBM()] * n,
        out_shape=[jax.ShapeDtypeStruct(shape, sh.dtype) for sh, (_, _, shape) in zip(shards, layouts)],
        scratch_shapes=[pltpu.SemaphoreType.DMA((n, 7)), pltpu.SemaphoreType.DMA((n, 7)), pltpu.SemaphoreType.DMA((n,))])(*shards)


def _rs_to_sibling(grads, layouts, name):
    n = len(grads)
    blocks = [_block_of(kind, width) for kind, width, _ in layouts]

    def body(*refs):
        ins, outs = refs[:n], refs[n:2 * n]
        send_sems, recv_sems = refs[2 * n:]
        x, y, c = _coords()
        sibling = (x, y, 1 - c)
        cps = []
        for t in range(n):
            for k in range(4):
                px, py = _rel_chip(x, y, k)
                cp = pltpu.make_async_remote_copy(
                    src_ref=blocks[t](ins[t], 4 * px + 2 * py + (1 - c)), dst_ref=outs[t].at[k],
                    send_sem=send_sems.at[t, k], recv_sem=recv_sems.at[t, k], device_id=sibling, device_id_type=MESH)
                cp.start()
                cps.append(cp)
        for cp in cps:
            cp.wait_recv()
        for cp in cps:
            cp.wait_send()

    return pl.pallas_call(
        body, name=name, in_specs=[_HBM()] * n, out_specs=[_HBM()] * n,
        out_shape=[jax.ShapeDtypeStruct((4,) + shape, g.dtype) for g, (_, _, shape) in zip(grads, layouts)],
        scratch_shapes=[pltpu.SemaphoreType.DMA((n, 4)), pltpu.SemaphoreType.DMA((n, 4))])(*grads)


def _rs_chip_sum(grad, recv, layout, xyc, name):
    kind, width, shape = layout
    r, ccols = shape

    def src_index(k, xyc_ref):
        px = jnp.where(k % 2 == 1, 1 - xyc_ref[0], xyc_ref[0])
        py = jnp.where(k // 2 == 1, 1 - xyc_ref[1], xyc_ref[1])
        return 4 * px + 2 * py + xyc_ref[2]

    if kind == "col":
        g_spec = pl.BlockSpec((r, ccols), lambda k, s_: (0, src_index(k, s_)))
    elif kind == "row":
        g_spec = pl.BlockSpec((r, ccols), lambda k, s_: (src_index(k, s_), 0))
    else:
        g_spec = pl.BlockSpec((None, r, ccols), lambda k, s_: (src_index(k, s_), 0, 0))

    def body(xyc_ref, g_ref, r_ref, o_ref):
        o_ref[...] = (g_ref[...].astype(F32) + r_ref[...].astype(F32)).astype(o_ref.dtype)

    slot = pl.BlockSpec((None, r, ccols), lambda k, s_: (k, 0, 0))
    return pl.pallas_call(
        body, name=name,
        grid_spec=pltpu.PrefetchScalarGridSpec(num_scalar_prefetch=1, grid=(4,), in_specs=[g_spec, slot], out_specs=slot),
        out_shape=jax.ShapeDtypeStruct((4, r, ccols), grad.dtype), compiler_params=_cp(VMEM_BIG))(xyc, grad, recv)


def _rs_across_chips(parts, name):
    n = len(parts)

    def body(*refs):
        ins, outs = refs[:n], refs[n:2 * n]
        send_sems, recv_sems = refs[2 * n:]
        x, y, c = _coords()
        cps = []
        for t in range(n):
            for k in range(1, 4):
                px, py = _rel_chip(x, y, k)
                cp = pltpu.make_async_remote_copy(
                    src_ref=ins[t].at[k], dst_ref=outs[t].at[k - 1], send_sem=send_sems.at[t, k - 1],
                    recv_sem=recv_sems.at[t, k - 1], device_id=(px, py, c), device_id_type=MESH)
                cp.start()
                cps.append(cp)
        for cp in cps:
            cp.wait_recv()
        for cp in cps:
            cp.wait_send()

    return pl.pallas_call(
        body, name=name, in_specs=[_HBM()] * n, out_specs=[_HBM()] * n,
        out_shape=[jax.ShapeDtypeStruct((3,) + p.shape[1:], p.dtype) for p in parts],
        scratch_shapes=[pltpu.SemaphoreType.DMA((n, 3)), pltpu.SemaphoreType.DMA((n, 3))])(*parts)


def _adamw_math(w, g, m, v):
    m = ADAM_B1 * m + (1.0 - ADAM_B1) * g
    v = ADAM_B2 * v + (1.0 - ADAM_B2) * jnp.square(g)
    m_hat = m / (1.0 - ADAM_B1 ** ADAM_STEP)
    v_hat = v / (1.0 - ADAM_B2 ** ADAM_STEP)
    delta = -ADAM_LR * (m_hat / (jnp.sqrt(v_hat) + ADAM_EPS) + ADAM_WD * w)
    return delta, m, v


def _row_tile(rows, cap):
    best = None
    for cand in range(8, min(rows, cap) + 1, 8):
        if rows % cand == 0:
            best = cand
    assert best is not None, rows
    return best


def _adamw(w, m, v, parts, name, layer=None, prev=None, tr=256):
    r, ccols = w.shape[-2:]
    tr = _row_tile(r, tr)
    npart = len(parts)

    def wspec():
        if layer is None:
            return pl.BlockSpec((tr, ccols), lambda i: (i, 0))
        return pl.BlockSpec((None, tr, ccols), lambda i: (layer, i, 0))

    def pspec(lead):
        if lead is None:
            return pl.BlockSpec((tr, ccols), lambda i: (i, 0))
        return pl.BlockSpec((None, tr, ccols), lambda i: (lead, i, 0))

    def body(*refs):
        w_ref, m_ref, v_ref = refs[:3]
        p_refs = refs[3:3 + npart]
        outs = refs[len(refs) - 4:]
        g = p_refs[0][...].astype(F32)
        for p_ref in p_refs[1:]:
            g = g + p_ref[...].astype(F32)
        delta, mn, vn = _adamw_math(w_ref[...], g, m_ref[...], v_ref[...])
        outs[0][...] = g
        outs[1][...] = delta
        outs[2][...] = mn
        outs[3][...] = vn

    operands = [w, m, v] + [p for p, _ in parts]
    in_specs = [wspec(), wspec(), wspec()] + [pspec(lead) for _, lead in parts]
    aliases = {}
    if prev is not None:
        for i, p in enumerate(prev):
            aliases[len(operands)] = i
            operands.append(p)
            in_specs.append(pl.BlockSpec(memory_space=pl.ANY))
    return pl.pallas_call(
        body, name=name, grid=(r // tr,), in_specs=in_specs, out_specs=[wspec()] * 4,
        out_shape=[jax.ShapeDtypeStruct(w.shape, F32)] * 4, input_output_aliases=aliases)(*operands)


def _sum8(buf, name):
    _, r, ccols = buf.shape

    def body(b_ref, o_ref):
        acc = b_ref[0]
        for j in range(1, N_DEV):
            acc = acc + b_ref[j]
        o_ref[...] = acc

    tr = _row_tile(r, 256)
    return pl.pallas_call(
        body, name=name, grid=(r // tr,), in_specs=[pl.BlockSpec((N_DEV, tr, ccols), lambda i: (0, i, 0))],
        out_specs=pl.BlockSpec((tr, ccols), lambda i: (i, 0)), out_shape=jax.ShapeDtypeStruct((r, ccols), F32))(buf)


def _pack(arrays):
    pieces, layout, off = [], [], 0
    for a in arrays:
        n = a.size
        padded = -(-n // 1024) * 1024
        flat = a.reshape(-1).astype(F32)
        if padded != n:
            flat = jnp.pad(flat, (0, padded - n))
        pieces.append(flat.reshape(padded // 128, 128))
        layout.append((off, n, a.shape))
        off += padded // 128
    return jnp.concatenate(pieces, axis=0), layout


def _unpack(packed, layout):
    out = []
    for off, n, shape in layout:
        rows = -(-n // 1024) * 8
        out.append(packed[off:off + rows].reshape(-1)[:n].reshape(shape))
    return out


def kernel(x, mem, norm_mix, norm_ffn, mem_norm, w_kv, w_out, w_ffn1, w_ffn2, a_in, a_ln_g, a_ln_b, a_ws, a_bs, b_in, b_conv_w, b_conv_b, b_dt_bias, b_a_log, b_d, b_gnorm, final_norm, loss_target, m_norm_mix, m_norm_ffn, m_mem_norm, m_w_kv, m_w_out, m_w_ffn1, m_w_ffn2, m_a_in, m_a_ln_g, m_a_ln_b, m_a_ws, m_a_bs, m_b_in, m_b_conv_w, m_b_conv_b, m_b_dt_bias, m_b_a_log, m_b_d, m_b_gnorm, m_final_norm, v_norm_mix, v_norm_ffn, v_mem_norm, v_w_kv, v_w_out, v_w_ffn1, v_w_ffn2, v_a_in, v_a_ln_g, v_a_ln_b, v_a_ws, v_a_bs, v_b_in, v_b_conv_w, v_b_conv_b, v_b_dt_bias, v_b_a_log, v_b_d, v_b_gnorm, v_final_norm):
    s = x.shape[1]
    xs = x.reshape(s, D_MODEL)
    mems = mem.reshape(N_MEM, D_MODEL)
    target = loss_target.reshape(s, D_MODEL)
    ax, ay, ac = lax.axis_index("x"), lax.axis_index("y"), lax.axis_index("c")
    me = 4 * ax + 2 * ay + ac
    xyc = jnp.stack([ax, ay, ac]).astype(jnp.int32)

    b_cols = b_in.shape[2]
    lay_w = [("col", 512, (2, D_MODEL, D_FF)), ("row", 512, (2, D_FF, D_MODEL)), ("row", 384, (2, 3 * D_MODEL, D_MODEL)),
             ("col", 256, (2, D_MODEL, 2 * X_WIDTH)), ("col", 640, (1, D_MODEL, 5 * D_MODEL)),
             ("blk", 0, (N_DEV, D_MODEL, b_cols))]
    shards = [w_ffn1.astype(_ACT), w_ffn2.astype(_ACT), w_out.astype(_ACT), w_kv.astype(_ACT), a_in.astype(_ACT),
              b_in[0].astype(_ACT)]
    W1, W2, WO, WKV, WA, wb_blk = _all_gather(shards, lay_w, "ag_weights")
    wb_full = jnp.transpose(wb_blk, (1, 0, 2)).reshape(D_MODEL, N_DEV * b_cols)
    dt0 = D_INNER + CONV_DIM
    WB = jnp.concatenate([wb_full[:, :dt0], wb_full[:, dt0 + SSM_HEADS:]], axis=1)
    WBDT = jnp.pad(wb_full[:, dt0:dt0 + SSM_HEADS], ((0, 0), (0, HPAD - SSM_HEADS)))

    row = lambda a: a.reshape(1, -1)
    nmix = [row(norm_mix[0]), row(norm_mix[1])]
    nffn = [row(norm_ffn[0]), row(norm_ffn[1])]
    nmem = [row(mem_norm[0]), row(mem_norm[1])]
    fin = row(final_norm)
    lng, lnb = a_ln_g.reshape(1, D_INNER), a_ln_b.reshape(1, D_INNER)
    ws = a_ws[0]
    bs3 = a_bs[0].reshape(A_GROUPS, CHUNK, 1)
    pad_h = lambda a: jnp.pad(a.reshape(-1), (0, HPAD - SSM_HEADS))
    bias_row, bias_col = pad_h(b_dt_bias).reshape(1, HPAD), pad_h(b_dt_bias).reshape(HPAD, 1)
    alog_row, alog_col = pad_h(b_a_log).reshape(1, HPAD), pad_h(b_a_log).reshape(HPAD, 1)
    dfull = jnp.repeat(b_d.reshape(-1), SSM_P).reshape(1, D_INNER)

    (small_w,) = _all_gather([_pack([b_conv_w[0], b_conv_b[0], b_gnorm[0]])[0]],
                             [("blk", 0, (N_DEV, 32, 128))], "ag_small_w")
    cw_sh, cb_sh, gn_sh = 4 * 384, 384, 256
    sw = small_w.reshape(N_DEV, 32 * 128)
    conv_w = jnp.transpose(sw[:, :cw_sh].reshape(N_DEV, CONV_K, 384), (1, 0, 2)).reshape(CONV_K, CONV_DIM)
    conv_b = sw[:, 2048:2048 + cb_sh].reshape(1, CONV_DIM)
    gnorm = sw[:, 3072:3072 + gn_sh].reshape(1, D_INNER)

    kvs, mns = [], []
    for i in range(2):
        mn = _rms_fwd(mems, nmem[i], f"mem_norm{i}")
        mns.append(mn)
        kvs.append(_mm(mn, WKV, m=N_MEM, n=2 * X_WIDTH, k=D_MODEL, b_at=(i, 0, 0), out_dtype=_ACT, name=f"kv{i}"))

    def ffn_fwd(h, i):
        f = _rms_fwd(h, nffn[i], f"ffn_norm{i}")
        p = _mm(f, W1, m=s, n=D_FF, k=D_MODEL, b_at=(i, 0, 0), out_dtype=_ACT, name=f"ffn_up{i}")
        hn = _mm(p, W2, m=s, n=D_MODEL, k=D_FF, b_at=(i, 0, 0), a_pro="relu2", add=h, name=f"ffn_down{i}")
        return f, p, hn

    def out_proj(h, cat, i):
        return _mm(cat, WO, m=s, n=D_MODEL, k=3 * D_MODEL, b_at=(i, 0, 0), add=h, name=f"out_proj{i}")

    a0 = _rms_fwd(xs, nmix[0], "mix_norm0")
    proj_a = _mm(a0, WA, m=s, n=5 * D_MODEL, k=D_MODEL, b_at=(0, 0, 0), name="proj_a")
    cat_a = _gmlp_fwd(proj_a, lng, lnb, ws, bs3, "gmlp_fwd")
    cat_a = _attn_fwd(proj_a, 4, kvs[0], cat_a, "attn_fwd0")
    h1 = out_proj(xs, cat_a, 0)
    f0, p0, h2 = ffn_fwd(h1, 0)

    a1 = _rms_fwd(h2, nmix[1], "mix_norm1")
    proj_b = _mm(a1, WB, m=s, n=6 * D_MODEL, k=D_MODEL, name="proj_b")
    dt_raw = _mm(a1, WBDT, m=s, n=HPAD, k=D_MODEL, name="proj_dt")
    dt_raw_t = dt_raw.T
    xbc = _conv_fwd(proj_b, conv_w, conv_b, "conv_fwd")
    y_ssd, states = _ssd_fwd(xbc, dt_raw, dt_raw_t, bias_row, bias_col, alog_row, alog_col, dfull, "ssd_fwd")
    cat_b = _gate_fwd(y_ssd, proj_b, gnorm, "gate_fwd")
    cat_b = _attn_fwd(proj_b, 5, kvs[1], cat_b, "attn_fwd1")
    h3 = out_proj(h2, cat_b, 1)
    f1, p1, h4 = ffn_fwd(h3, 1)

    loss_part, dh, dh_act, d_fin = _loss_head(h4, fin, target, "loss_head")
    loss = lax.psum(loss_part[0, 0], ("x", "y", "c"))

    g_f1, g_f2, g_out, g_kv = [None, None], [None, None], [None, None], [None, None]
    d_nffn, d_nmix, d_nmem = [None, None], [None, None], [None, None]

    def ffn_bwd(dh, dh_act, h_in, f, p, i):
        dp = _mm(dh_act, W2, m=s, n=D_FF, k=D_MODEL, tb=True, b_at=(i, 0, 0), epi_p=p, out_dtype=_ACT, name=f"ffn_down_dx{i}")
        g_f2[i] = _mm(p, dh_act, m=D_FF, n=D_MODEL, k=s, ta=True, a_pro="relu2", out_dtype=_ACT, name=f"ffn_down_dw{i}")
        g_f1[i] = _mm(f, dp, m=D_MODEL, n=D_FF, k=s, ta=True, out_dtype=_ACT, name=f"ffn_up_dw{i}")
        df = _mm(dp, W1, m=s, n=D_MODEL, k=D_FF, tb=True, b_at=(i, 0, 0), name=f"ffn_up_dx{i}")
        dh_in, dh_in_act, d_nffn[i] = _rms_bwd(h_in, nffn[i], df, dh, f"ffn_norm_bwd{i}")
        return dh_in, dh_in_act

    def out_bwd(dh_act, cat, i):
        dcat = _mm(dh_act, WO, m=s, n=3 * D_MODEL, k=D_MODEL, tb=True, b_at=(i, 0, 0), out_dtype=_ACT, name=f"out_dx{i}")
        g_out[i] = _mm(cat, dh_act, m=3 * D_MODEL, n=D_MODEL, k=s, ta=True, out_dtype=_ACT, name=f"out_dw{i}")
        return dcat

    def mem_bwd(dkv, i):
        g_kv[i] = _mm(mns[i], dkv, m=D_MODEL, n=2 * X_WIDTH, k=N_MEM, ta=True, out_dtype=_ACT, name=f"kv_dw{i}")
        dmn = _mm(dkv, WKV, m=N_MEM, n=D_MODEL, k=2 * X_WIDTH, tb=True, b_at=(i, 0, 0), name=f"kv_dx{i}")
        _, _, d_nmem[i] = _rms_bwd(mems, nmem[i], dmn, None, f"mem_norm_bwd{i}")

    dh3, dh3_act = ffn_bwd(dh, dh_act, h3, f1, p1, 1)
    dcat_b = out_bwd(dh3_act, cat_b, 1)
    dy_ssd, dproj_b, d_gnorm = _gate_bwd(y_ssd, proj_b, gnorm, dcat_b, "gate_bwd")
    dproj_b, dkv_b = _attn_bwd(proj_b, 5, kvs[1], dcat_b, dproj_b, "attn_bwd1")
    mem_bwd(dkv_b, 1)
    dxbc, ddt_raw, d_alog, d_dskip, d_dtbias = _ssd_bwd(
        xbc, dt_raw, dt_raw_t, bias_row, bias_col, alog_row, alog_col, dfull, dy_ssd, states, "ssd_bwd")
    dproj_b, d_convw, d_convb = _conv_bwd(proj_b, conv_w, conv_b, dxbc, dproj_b, "conv_bwd")
    gb = _mm(a1, dproj_b, m=D_MODEL, n=6 * D_MODEL, k=s, ta=True, out_dtype=_ACT, name="proj_b_dw")
    gb_dt = _mm(a1, ddt_raw, m=D_MODEL, n=HPAD, k=s, ta=True, out_dtype=_ACT, name="proj_b_dw_dt")
    da1 = _mm(dproj_b, WB, m=s, n=D_MODEL, k=6 * D_MODEL, tb=True, name="proj_b_dx")
    da1 = _mm(ddt_raw, WBDT, m=s, n=D_MODEL, k=HPAD, tb=True, add=da1, name="proj_b_dx_dt")
    dh2, dh2_act, d_nmix[1] = _rms_bwd(h2, nmix[1], da1, dh3, "mix_norm_bwd1")

    dh1, dh1_act = ffn_bwd(dh2, dh2_act, h1, f0, p0, 0)
    dcat_a = out_bwd(dh1_act, cat_a, 0)
    dproj_a, d_ws, d_bs3, d_lng, d_lnb = _gmlp_bwd(proj_a, dcat_a, lng, lnb, ws, bs3, "gmlp_bwd")
    dproj_a, dkv_a = _attn_bwd(proj_a, 4, kvs[0], dcat_a, dproj_a, "attn_bwd0")
    mem_bwd(dkv_a, 0)
    ga = _mm(a0, dproj_a, m=D_MODEL, n=5 * D_MODEL, k=s, ta=True, out_dtype=_ACT, name="proj_a_dw")
    da0 = _mm(dproj_a, WA, m=s, n=D_MODEL, k=5 * D_MODEL, tb=True, b_at=(0, 0, 0), name="proj_a_dx")
    grad_x, _, d_nmix[0] = _rms_bwd(xs, nmix[0], da0, dh1, "mix_norm_bwd0")

    gb_full = jnp.concatenate([gb[:, :dt0], gb_dt[:, :SSM_HEADS], gb[:, dt0:]], axis=1)
    gb_blk = jnp.transpose(gb_full.reshape(D_MODEL, N_DEV, b_cols), (1, 0, 2))
    fams = []
    for i in range(2):
        fams.append((g_f1[i], ("col", 512, (D_MODEL, 512))))
        fams.append((g_f2[i], ("row", 512, (512, D_MODEL))))
        fams.append((g_out[i], ("row", 384, (384, D_MODEL))))
        fams.append((g_kv[i], ("col", 256, (D_MODEL, 256))))
    fams.append((ga, ("col", 640, (D_MODEL, 640))))
    fams.append((gb_blk, ("blk", 0, (D_MODEL, b_cols))))

    grads3, lays3 = [], []
    for g, (kind, width, shape) in fams:
        if kind == "blk":
            grads3.append(g)
            lays3.append((kind, width, shape))
        else:
            grads3.append(g.reshape((1,) + g.shape))
            lays3.append((kind, width, (1,) + shape))
    recv1 = _rs_to_sibling(grads3, lays3, "rs_sibling")
    parts = []
    for t, (g, lay) in enumerate(fams):
        r1 = recv1[t].reshape((4,) + lay[2])
        parts.append(_rs_chip_sum(g, r1, lay, xyc, f"rs_chip_sum{t}"))
    recv2 = _rs_across_chips(parts, "rs_chips")

    def big_update(w, m, v, idx, nlayer):
        res = None
        for i in range(nlayer):
            t = idx[i]
            plist = [(parts[t], 0), (recv2[t], 0), (recv2[t], 1), (recv2[t], 2)]
            res = _adamw(w, m, v, plist, f"adamw_{t}", layer=i, prev=res)
        return res

    r_f1 = big_update(w_ffn1, m_w_ffn1, v_w_ffn1, [0, 4], 2)
    r_f2 = big_update(w_ffn2, m_w_ffn2, v_w_ffn2, [1, 5], 2)
    r_out = big_update(w_out, m_w_out, v_w_out, [2, 6], 2)
    r_kv = big_update(w_kv, m_w_kv, v_w_kv, [3, 7], 2)
    r_a = big_update(a_in, m_a_in, v_a_in, [8], 1)
    r_b = big_update(b_in, m_b_in, v_b_in, [9], 1)

    rep_names = ["norm_mix", "norm_ffn", "mem_norm", "a_ln_g", "a_ln_b", "a_ws", "a_bs", "b_dt_bias", "b_a_log", "b_d",
                 "final_norm"]
    rep_grads = [jnp.concatenate(d_nmix, axis=0), jnp.concatenate(d_nffn, axis=0), jnp.concatenate(d_nmem, axis=0),
                 d_lng, d_lnb, d_ws.reshape(1, A_GROUPS, CHUNK, CHUNK), d_bs3.reshape(1, A_GROUPS, CHUNK),
                 d_dtbias[:, :SSM_HEADS], d_alog[:, :SSM_HEADS], d_dskip[:, :SSM_HEADS], d_fin.reshape(D_MODEL)]
    rep_w = [norm_mix, norm_ffn, mem_norm, a_ln_g, a_ln_b, a_ws, a_bs, b_dt_bias, b_a_log, b_d, final_norm]
    rep_m = [m_norm_mix, m_norm_ffn, m_mem_norm, m_a_ln_g, m_a_ln_b, m_a_ws, m_a_bs, m_b_dt_bias, m_b_a_log, m_b_d, m_final_norm]
    rep_v = [v_norm_mix, v_norm_ffn, v_mem_norm, v_a_ln_g, v_a_ln_b, v_a_ws, v_a_bs, v_b_dt_bias, v_b_a_log, v_b_d, v_final_norm]
    rep_grads = [g.reshape(w.shape) for g, w in zip(rep_grads, rep_w)]
    sh_grads = [d_convw, d_convb, d_gnorm]
    g_pack, g_layout = _pack(rep_grads + sh_grads)
    n_rep = len(rep_grads)
    (g_all,) = _all_gather([g_pack], [("blk", 0, (N_DEV,) + g_pack.shape)], "ag_small_grads")
    g_small = _sum8(g_all, "sum_small_grads")
    g_list = _unpack(g_small, g_layout)
    wp, w_layout = _pack(rep_w)
    mp, _ = _pack(rep_m)
    vp, _ = _pack(rep_v)
    gp, _ = _pack(g_list[:n_rep])
    rep_res = [_unpack(o, w_layout) for o in _adamw(wp, mp, vp, [(gp, None)], "adamw_replicated", tr=88)]

    gcw = lax.dynamic_slice_in_dim(g_list[n_rep], me * 384, 384, axis=1).reshape(1, CONV_K, 384)
    gcb = lax.dynamic_slice_in_dim(g_list[n_rep + 1], me * 384, 384, axis=1)
    ggn = lax.dynamic_slice_in_dim(g_list[n_rep + 2], me * 256, 256, axis=1)
    sh_w = [b_conv_w, b_conv_b, b_gnorm]
    sh_m = [m_b_conv_w, m_b_conv_b, m_b_gnorm]
    sh_v = [v_b_conv_w, v_b_conv_b, v_b_gnorm]
    swp, sw_layout = _pack(sh_w)
    smp, _ = _pack(sh_m)
    svp, _ = _pack(sh_v)
    sgp, _ = _pack([gcw, gcb, ggn])
    sh_res = [_unpack(o, sw_layout) for o in _adamw(swp, smp, svp, [(sgp, None)], "adamw_sharded_small", tr=8)]

    names = ["norm_mix", "norm_ffn", "mem_norm", "w_kv", "w_out", "w_ffn1", "w_ffn2", "a_in", "a_ln_g", "a_ln_b", "a_ws",
             "a_bs", "b_in", "b_conv_w", "b_conv_b", "b_dt_bias", "b_a_log", "b_d", "b_gnorm", "final_norm"]
    big = {"w_kv": r_kv, "w_out": r_out, "w_ffn1": r_f1, "w_ffn2": r_f2, "a_in": r_a, "b_in": r_b}
    sh_names = ["b_conv_w", "b_conv_b", "b_gnorm"]
    outs = [loss, grad_x.reshape(x.shape)]
    for kind in range(4):
        for nm in names:
            if nm in big:
                outs.append(big[nm][kind])
            elif nm in sh_names:
                outs.append(sh_res[kind][sh_names.index(nm)])
            else:
                outs.append(rep_res[kind][rep_names.index(nm)])
    return tuple(outs)
```

```python
import functools
import math

import jax
import jax.numpy as jnp
from jax import lax
from jax.experimental import pallas as pl
from jax.experimental.pallas import tpu as pltpu
from jax.experimental.pallas import tpu_sc as plsc

F32 = jnp.float32
_MXU = jnp.bfloat16
_ACT = jnp.bfloat16
_HI = lax.Precision.HIGHEST

D_MODEL = 1024
CHUNK = 128
N_MEM = 256
D_INNER = 2048
A_GROUPS = 8
A_GW = D_INNER // A_GROUPS
SSM_HEADS = 32
SSM_P = 64
SSM_GROUPS = 4
SSM_GW = D_INNER // SSM_GROUPS
SSM_N = 128
CONV_K = 4
CONV_DIM = 3072
X_HEADS = 4
X_HD = 256
X_WIDTH = 1024
D_FF = 4096
EPS = 1e-6
HPAD = 128
N_DEV = 8

ADAM_LR = 0.001
ADAM_B1 = 0.9
ADAM_B2 = 0.999
ADAM_EPS = 1e-08
ADAM_WD = 0.01
ADAM_STEP = 10

VMEM_BIG = 56 * 1024 * 1024
MESH = pl.DeviceIdType.MESH


def _cp(vmem=None):
    if vmem is None:
        return pltpu.CompilerParams()
    return pltpu.CompilerParams(vmem_limit_bytes=vmem)


def _dot(a, b, dims=((1,), (0,))):
    return lax.dot_general(a.astype(_MXU), b.astype(_MXU), (dims, ((), ())), preferred_element_type=F32)


def _dot_nt(a, b):
    return _dot(a, b, ((1,), (1,)))


def _dot_tn(a, b):
    return _dot(a, b, ((0,), (0,)))


def _dot_hi(a, b, dims=((1,), (0,))):
    return lax.dot_general(a.astype(F32), b.astype(F32), (dims, ((), ())), precision=_HI, preferred_element_type=F32)


def _sigmoid(x):
    return 1.0 / (1.0 + jnp.exp(-x))


def _gelu(x):
    return 0.5 * x * (1.0 + lax.erf(x * (1.0 / math.sqrt(2.0))))


def _gelu_grad(x):
    return 0.5 * (1.0 + lax.erf(x * (1.0 / math.sqrt(2.0)))) + x * jnp.exp(-0.5 * x * x) * (1.0 / math.sqrt(2.0 * math.pi))


def _softplus(x):
    return jnp.maximum(x, 0.0) + jnp.log1p(jnp.exp(-jnp.abs(x)))


def _iota(shape, dim):
    return lax.broadcasted_iota(jnp.int32, shape, dim)


MM_VMEM_BUDGET = 40 * 1024 * 1024
HBM_BYTES_PER_S = 2.5e12
GRID_STEP_S = 0.35e-6
VMEM_ACC_BYTES_PER_S = 6e12


def _divisors(dim, unit):
    out = [d for d in range(unit, min(dim, 2048) + 1, unit) if dim % d == 0]
    return out if out else [dim]


def _mm_tiles(m, n, k, sa, sb, s_mn, a_pro, offsets):
    best = None
    (a_r0, a_c0, ta), (b_r0, b_c0, tb), (o_r0, o_c0) = offsets
    for tm in _divisors(m, 128):
        for tn in _divisors(n, 128):
            for tk in [k // d for d in (1, 2, 3, 4, 6, 8) if k % d == 0 and (k // d) % 128 == 0]:
                a_t = (tk, tm) if ta else (tm, tk)
                b_t = (tn, tk) if tb else (tk, tn)
                if a_r0 % a_t[0] or a_c0 % a_t[1] or b_r0 % b_t[0] or b_c0 % b_t[1] or o_r0 % tm or o_c0 % tn:
                    continue
                nk = k // tk
                vmem = 2 * (tm * tk * sa + tk * tn * sb + tm * tn * s_mn) + tm * tn * 4 * (2 if nk > 1 else 1)
                if a_pro or sa == 4:
                    vmem += tm * tk * 6
                if sb == 4:
                    vmem += tk * tn * 2
                if vmem > MM_VMEM_BUDGET:
                    continue
                gi, gj = m // tm, n // tn
                for j_inner in (True, False):
                    if nk > 1:
                        traffic = gj * m * k * sa + gi * k * n * sb
                    elif j_inner:
                        traffic = m * k * sa + gi * k * n * sb
                    else:
                        traffic = gj * m * k * sa + k * n * sb
                    traffic += m * n * s_mn + (tm * tk * sa + tk * tn * sb)
                    cost = traffic / HBM_BYTES_PER_S + gi * gj * nk * GRID_STEP_S
                    if nk > 1:
                        cost += m * n * 8 * nk / VMEM_ACC_BYTES_PER_S
                    if best is None or cost < best[0]:
                        best = (cost, tm, tn, tk, j_inner)
    assert best is not None, (m, n, k)
    return best[1:]


def _mm(a, b, *, m, n, k, name, ta=False, tb=False, a_at=(None, 0, 0), b_at=(None, 0, 0),
        out_dtype=F32, add=None, epi_p=None, epi_at=(None, 0, 0), out=None, out_at=(None, 0, 0),
        out_full=None, a_pro=None):
    s_mn = jnp.dtype(out.dtype if out is not None else out_dtype).itemsize
    s_mn += add.dtype.itemsize if add is not None else 0
    s_mn += epi_p.dtype.itemsize if epi_p is not None else 0
    tm, tn, tk, j_inner = _mm_tiles(m, n, k, a.dtype.itemsize, b.dtype.itemsize, s_mn, a_pro is not None,
                                    ((a_at[1], a_at[2], ta), (b_at[1], b_at[2], tb), (out_at[1], out_at[2])))
    nk = k // tk

    def spec(at, tr, tc, rsel, csel):
        lead, r0, c0 = at
        assert r0 % tr == 0 and c0 % tc == 0, (name, at, tr, tc)
        rb, cb = r0 // tr, c0 // tc
        if lead is None:
            return pl.BlockSpec((tr, tc), lambda g0, g1, kk: (rb + rsel(g0, g1, kk), cb + csel(g0, g1, kk)))
        return pl.BlockSpec((None, tr, tc), lambda g0, g1, kk: (lead, rb + rsel(g0, g1, kk), cb + csel(g0, g1, kk)))

    gi = (lambda g0, g1, kk: g0) if j_inner else (lambda g0, g1, kk: g1)
    gj = (lambda g0, g1, kk: g1) if j_inner else (lambda g0, g1, kk: g0)
    gk = lambda g0, g1, kk: kk
    a_spec = spec(a_at, tk, tm, gk, gi) if ta else spec(a_at, tm, tk, gi, gk)
    b_spec = spec(b_at, tn, tk, gj, gk) if tb else spec(b_at, tk, tn, gk, gj)
    dims = ((0,), (0,)) if ta else (((1,), (1,)) if tb else ((1,), (0,)))
    assert not (ta and tb)

    operands, in_specs = [a, b], [a_spec, b_spec]
    if add is not None:
        operands.append(add)
        in_specs.append(spec((None, 0, 0), tm, tn, gi, gj))
    if epi_p is not None:
        operands.append(epi_p)
        in_specs.append(spec(epi_at, tm, tn, gi, gj))
    aliases = {}
    if out is not None:
        aliases = {len(operands): 0}
        operands.append(out)
        in_specs.append(pl.BlockSpec(memory_space=pl.ANY))
        out_struct = jax.ShapeDtypeStruct(out.shape, out.dtype)
        out_dtype = out.dtype
    else:
        out_struct = jax.ShapeDtypeStruct(out_full if out_full is not None else (m, n), out_dtype)
    has_add, has_epi, has_alias = add is not None, epi_p is not None, out is not None

    def body(*refs):
        a_ref, b_ref = refs[0], refs[1]
        pos = 2
        add_ref = epi_ref = None
        if has_add:
            add_ref = refs[pos]
            pos += 1
        if has_epi:
            epi_ref = refs[pos]
            pos += 1
        if has_alias:
            pos += 1
        o_ref = refs[pos]

        def finish(r):
            if has_add:
                r = r + add_ref[...].astype(F32)
            if has_epi:
                r = r * (2.0 * jnp.maximum(epi_ref[...].astype(F32), 0.0))
            o_ref[...] = r.astype(o_ref.dtype)

        av = a_ref[...]
        if a_pro == "relu2":
            av = jnp.square(jnp.maximum(av.astype(F32), 0.0))
        part = _dot(av, b_ref[...], dims)
        if nk == 1:
            finish(part)
        else:
            acc_ref = refs[pos + 1]
            kk = pl.program_id(2)

            @pl.when(kk == 0)
            def _():
                acc_ref[...] = part

            @pl.when(kk > 0)
            def _():
                acc_ref[...] += part

            @pl.when(kk == nk - 1)
            def _():
                finish(acc_ref[...])

    grid = (m // tm, n // tn, nk) if j_inner else (n // tn, m // tm, nk)
    return pl.pallas_call(
        body, name=name, grid=grid, in_specs=in_specs,
        out_specs=spec(out_at, tm, tn, gi, gj), out_shape=out_struct,
        scratch_shapes=[pltpu.VMEM((tm, tn), F32)] if nk > 1 else [], input_output_aliases=aliases,
        compiler_params=_cp(VMEM_BIG))(*operands)


def _rms_fwd(x, g, name, tm=256):
    s, d = x.shape
    tm = min(tm, s)

    def body(x_ref, g_ref, o_ref):
        xv = x_ref[...]
        r = lax.rsqrt(jnp.mean(xv * xv, axis=-1, keepdims=True) + EPS)
        o_ref[...] = (xv * r * g_ref[...]).astype(o_ref.dtype)

    return pl.pallas_call(
        body, name=name, grid=(s // tm,),
        in_specs=[pl.BlockSpec((tm, d), lambda i: (i, 0)), pl.BlockSpec((1, d), lambda i: (0, 0))],
        out_specs=pl.BlockSpec((tm, d), lambda i: (i, 0)),
        out_shape=jax.ShapeDtypeStruct((s, d), _ACT))(x, g)


def _rms_bwd(x, g, dy, dres, name, tm=256):
    s, d = x.shape
    tm = min(tm, s)
    has_res = dres is not None

    def body(*refs):
        if has_res:
            x_ref, g_ref, dy_ref, dres_ref, dx_ref, dxa_ref, dg_ref = refs
        else:
            x_ref, g_ref, dy_ref, dx_ref, dxa_ref, dg_ref = refs

        @pl.when(pl.program_id(0) == 0)
        def _():
            dg_ref[...] = jnp.zeros_like(dg_ref)

        xv = x_ref[...]
        dyv = dy_ref[...].astype(F32)
        r = lax.rsqrt(jnp.mean(xv * xv, axis=-1, keepdims=True) + EPS)
        xh = xv * r
        dyg = dyv * g_ref[...]
        dx = r * (dyg - xh * jnp.mean(dyg * xh, axis=-1, keepdims=True))
        if has_res:
            dx = dx + dres_ref[...]
        dx_ref[...] = dx
        dxa_ref[...] = dx.astype(dxa_ref.dtype)
        dg_ref[...] += jnp.sum(dyv * xh, axis=0, keepdims=True)

    row = pl.BlockSpec((tm, d), lambda i: (i, 0))
    vec = pl.BlockSpec((1, d), lambda i: (0, 0))
    in_specs = [row, vec, row] + ([row] if has_res else [])
    operands = [x, g, dy] + ([dres] if has_res else [])
    return pl.pallas_call(
        body, name=name, grid=(s // tm,), in_specs=in_specs, out_specs=[row, row, vec],
        out_shape=[jax.ShapeDtypeStruct((s, d), F32), jax.ShapeDtypeStruct((s, d), _ACT),
                   jax.ShapeDtypeStruct((1, d), F32)])(*operands)


def _loss_head(h, g, target, name, tm=256):
    s, d = h.shape
    tm = min(tm, s)

    def body(h_ref, g_ref, t_ref, loss_ref, dh_ref, dha_ref, dg_ref):
        @pl.when(pl.program_id(0) == 0)
        def _():
            dg_ref[...] = jnp.zeros_like(dg_ref)
            loss_ref[...] = jnp.zeros_like(loss_ref)

        xv = h_ref[...]
        r = lax.rsqrt(jnp.mean(xv * xv, axis=-1, keepdims=True) + EPS)
        xh = xv * r
        err = xh * g_ref[...] - t_ref[...]
        loss_ref[...] += jnp.full(loss_ref.shape, 0.5 * jnp.sum(jnp.mean(err * err, axis=-1, keepdims=True)), F32)
        dyv = err * (1.0 / d)
        dyg = dyv * g_ref[...]
        dh = r * (dyg - xh * jnp.mean(dyg * xh, axis=-1, keepdims=True))
        dh_ref[...] = dh
        dha_ref[...] = dh.astype(dha_ref.dtype)
        dg_ref[...] += jnp.sum(dyv * xh, axis=0, keepdims=True)

    row = pl.BlockSpec((tm, d), lambda i: (i, 0))
    vec = pl.BlockSpec((1, d), lambda i: (0, 0))
    return pl.pallas_call(
        body, name=name, grid=(s // tm,), in_specs=[row, vec, row],
        out_specs=[pl.BlockSpec((1, 128), lambda i: (0, 0)), row, row, vec],
        out_shape=[jax.ShapeDtypeStruct((1, 128), F32), jax.ShapeDtypeStruct((s, d), F32),
                   jax.ShapeDtypeStruct((s, d), _ACT), jax.ShapeDtypeStruct((1, d), F32)])(h, g, target)


def _gmlp_parts(pu, pv, lng, lnb):
    u = _gelu(pu)
    v = _gelu(pv)
    mu = jnp.mean(v, axis=-1, keepdims=True)
    vc = v - mu
    rstd = lax.rsqrt(jnp.mean(vc * vc, axis=-1, keepdims=True) + EPS)
    xhat = vc * rstd
    vn = xhat * lng + lnb
    return u, xhat, rstd, vn


def _gmlp_fwd(proj, lng, lnb, ws, bs3, name):
    s = proj.shape[0]

    def body(pu_ref, pv_ref, lng_ref, lnb_ref, ws_ref, bs_ref, o_ref):
        u, _, _, vn = _gmlp_parts(pu_ref[...], pv_ref[...], lng_ref[...], lnb_ref[...])
        causal = _iota((CHUNK, CHUNK), 0) >= _iota((CHUNK, CHUNK), 1)
        for g in range(A_GROUPS):
            sl = slice(g * A_GW, (g + 1) * A_GW)
            w = jnp.where(causal, ws_ref[g], 0.0)
            sv = _dot(w, vn[:, sl]) + bs_ref[g]
            o_ref[:, sl] = (u[:, sl] * sv).astype(o_ref.dtype)

    full = lambda shape: pl.BlockSpec(shape, lambda c: (0,) * len(shape))
    return pl.pallas_call(
        body, name=name, grid=(s // CHUNK,),
        in_specs=[pl.BlockSpec((CHUNK, D_INNER), lambda c: (c, 0)), pl.BlockSpec((CHUNK, D_INNER), lambda c: (c, 1)),
                  full((1, D_INNER)), full((1, D_INNER)), full((A_GROUPS, CHUNK, CHUNK)), full((A_GROUPS, CHUNK, 1))],
        out_specs=pl.BlockSpec((CHUNK, D_INNER), lambda c: (c, 0)),
        out_shape=jax.ShapeDtypeStruct((s, D_INNER + X_WIDTH), _ACT), compiler_params=_cp(VMEM_BIG))(proj, proj, lng, lnb, ws, bs3)


def _gmlp_bwd(proj, dcat, lng, lnb, ws, bs3, name):
    s = proj.shape[0]

    def body(pu_ref, pv_ref, dm_ref, lng_ref, lnb_ref, ws_ref, bs_ref, dp_ref, dws_ref, dbs_ref, dlng_ref, dlnb_ref, dvn_ref):
        @pl.when(pl.program_id(0) == 0)
        def _():
            dws_ref[...] = jnp.zeros_like(dws_ref)
            dbs_ref[...] = jnp.zeros_like(dbs_ref)
            dlng_ref[...] = jnp.zeros_like(dlng_ref)
            dlnb_ref[...] = jnp.zeros_like(dlnb_ref)

        pu, pv = pu_ref[...], pv_ref[...]
        lng = lng_ref[...]
        u, xhat, rstd, vn = _gmlp_parts(pu, pv, lng, lnb_ref[...])
        dm = dm_ref[...].astype(F32)
        causal = _iota((CHUNK, CHUNK), 0) >= _iota((CHUNK, CHUNK), 1)
        for g in range(A_GROUPS):
            sl = slice(g * A_GW, (g + 1) * A_GW)
            w = jnp.where(causal, ws_ref[g], 0.0)
            sv = _dot(w, vn[:, sl]) + bs_ref[g]
            dsv = dm[:, sl] * u[:, sl]
            dp_ref[:, sl] = (dm[:, sl] * sv * _gelu_grad(pu[:, sl])).astype(dp_ref.dtype)
            dvn_ref[:, sl] = _dot_tn(w, dsv)
            dws_ref[g] += jnp.where(causal, _dot_nt(dsv, vn[:, sl]), 0.0)
            dbs_ref[g] += jnp.sum(dsv, axis=-1, keepdims=True)
        dvn = dvn_ref[...]
        dlng_ref[...] += jnp.sum(dvn * xhat, axis=0, keepdims=True)
        dlnb_ref[...] += jnp.sum(dvn, axis=0, keepdims=True)
        dxh = dvn * lng
        dv = rstd * (dxh - jnp.mean(dxh, axis=-1, keepdims=True) - xhat * jnp.mean(dxh * xhat, axis=-1, keepdims=True))
        dp_ref[:, D_INNER:] = (dv * _gelu_grad(pv)).astype(dp_ref.dtype)

    full = lambda shape: pl.BlockSpec(shape, lambda c: (0,) * len(shape))
    return pl.pallas_call(
        body, name=name, grid=(s // CHUNK,),
        in_specs=[pl.BlockSpec((CHUNK, D_INNER), lambda c: (c, 0)), pl.BlockSpec((CHUNK, D_INNER), lambda c: (c, 1)),
                  pl.BlockSpec((CHUNK, D_INNER), lambda c: (c, 0)),
                  full((1, D_INNER)), full((1, D_INNER)), full((A_GROUPS, CHUNK, CHUNK)), full((A_GROUPS, CHUNK, 1))],
        out_specs=[pl.BlockSpec((CHUNK, 2 * D_INNER), lambda c: (c, 0)), full((A_GROUPS, CHUNK, CHUNK)),
                   full((A_GROUPS, CHUNK, 1)), full((1, D_INNER)), full((1, D_INNER))],
        out_shape=[jax.ShapeDtypeStruct((s, 2 * D_INNER + X_WIDTH), _ACT), jax.ShapeDtypeStruct((A_GROUPS, CHUNK, CHUNK), F32),
                   jax.ShapeDtypeStruct((A_GROUPS, CHUNK, 1), F32), jax.ShapeDtypeStruct((1, D_INNER), F32),
                   jax.ShapeDtypeStruct((1, D_INNER), F32)],
        scratch_shapes=[pltpu.VMEM((CHUNK, D_INNER), F32)],
        compiler_params=_cp(VMEM_BIG))(proj, proj, dcat, lng, lnb, ws, bs3)


_X_SCALE = 1.0 / math.sqrt(X_HD)


def _attn_fwd(proj, qblk, kv, cat, name, tm=256):
    s = proj.shape[0]
    tm = min(tm, s)

    def body(q_ref, kv_ref, cat_ref, o_ref):
        for h in range(X_HEADS):
            sl = slice(h * X_HD, (h + 1) * X_HD)
            k = kv_ref[:, sl]
            v = kv_ref[:, X_WIDTH + h * X_HD:X_WIDTH + (h + 1) * X_HD]
            sc = _dot_nt(q_ref[:, sl], k) * _X_SCALE
            e = jnp.exp(sc - jnp.max(sc, axis=-1, keepdims=True))
            p = e / jnp.sum(e, axis=-1, keepdims=True)
            o_ref[:, sl] = _dot(p, v).astype(o_ref.dtype)

    return pl.pallas_call(
        body, name=name, grid=(s // tm,),
        in_specs=[pl.BlockSpec((tm, X_WIDTH), lambda i: (i, qblk)), pl.BlockSpec((N_MEM, 2 * X_WIDTH), lambda i: (0, 0)),
                  pl.BlockSpec(memory_space=pl.ANY)],
        out_specs=pl.BlockSpec((tm, X_WIDTH), lambda i: (i, D_INNER // X_WIDTH)),
        out_shape=jax.ShapeDtypeStruct(cat.shape, cat.dtype), input_output_aliases={2: 0})(proj, kv, cat)


def _attn_bwd(proj, qblk, kv, dcat, dproj, name, tm=256):
    s = proj.shape[0]
    tm = min(tm, s)

    def body(q_ref, kv_ref, do_ref, dproj_ref, dq_ref, dkv_ref):
        @pl.when(pl.program_id(0) == 0)
        def _():
            dkv_ref[...] = jnp.zeros_like(dkv_ref)

        for h in range(X_HEADS):
            sl = slice(h * X_HD, (h + 1) * X_HD)
            slv = slice(X_WIDTH + h * X_HD, X_WIDTH + (h + 1) * X_HD)
            q = q_ref[:, sl]
            k = kv_ref[:, sl]
            v = kv_ref[:, slv]
            do = do_ref[:, sl].astype(F32)
            sc = _dot_nt(q, k) * _X_SCALE
            e = jnp.exp(sc - jnp.max(sc, axis=-1, keepdims=True))
            p = e / jnp.sum(e, axis=-1, keepdims=True)
            dp = _dot_nt(do, v)
            ds = p * (dp - jnp.sum(dp * p, axis=-1, keepdims=True)) * _X_SCALE
            dq_ref[:, sl] = _dot(ds, k).astype(dq_ref.dtype)
            dkv_ref[:, sl] += _dot_tn(ds, q)
            dkv_ref[:, slv] += _dot_tn(p, do)

    return pl.pallas_call(
        body, name=name, grid=(s // tm,),
        in_specs=[pl.BlockSpec((tm, X_WIDTH), lambda i: (i, qblk)), pl.BlockSpec((N_MEM, 2 * X_WIDTH), lambda i: (0, 0)),
                  pl.BlockSpec((tm, X_WIDTH), lambda i: (i, 2)), pl.BlockSpec(memory_space=pl.ANY)],
        out_specs=[pl.BlockSpec((tm, X_WIDTH), lambda i: (i, qblk)), pl.BlockSpec((N_MEM, 2 * X_WIDTH), lambda i: (0, 0))],
        out_shape=[jax.ShapeDtypeStruct(dproj.shape, dproj.dtype), jax.ShapeDtypeStruct((N_MEM, 2 * X_WIDTH), F32)],
        input_output_aliases={3: 0})(proj, kv, dcat, dproj)


CONV_TC = 256
_XBC_BLK0 = D_INNER // CONV_TC


def _shift_down(x, j):
    if j == 0:
        return x
    return jnp.where(_iota(x.shape, 0) >= j, pltpu.roll(x, j, 0), 0.0)


def _shift_up(x, j):
    if j == 0:
        return x
    n = x.shape[0]
    return jnp.where(_iota(x.shape, 0) < n - j, pltpu.roll(x, n - j, 0), 0.0)


def _conv_fwd(proj, w, b, name):
    s = proj.shape[0]

    def body(x_ref, w_ref, b_ref, o_ref):
        xv = x_ref[...]
        pre = b_ref[...] + w_ref[CONV_K - 1:CONV_K, :] * xv
        for kk in range(CONV_K - 1):
            pre = pre + w_ref[kk:kk + 1, :] * _shift_down(xv, CONV_K - 1 - kk)
        o_ref[...] = pre * _sigmoid(pre)

    return pl.pallas_call(
        body, name=name, grid=(CONV_DIM // CONV_TC,),
        in_specs=[pl.BlockSpec((s, CONV_TC), lambda j: (0, _XBC_BLK0 + j)), pl.BlockSpec((CONV_K, CONV_TC), lambda j: (0, j)),
                  pl.BlockSpec((1, CONV_TC), lambda j: (0, j))],
        out_specs=pl.BlockSpec((s, CONV_TC), lambda j: (0, j)),
        out_shape=jax.ShapeDtypeStruct((s, CONV_DIM), F32), compiler_params=_cp(VMEM_BIG))(proj, w, b)


def _conv_bwd(proj, w, b, dxbc, dproj, name):
    s = proj.shape[0]

    def body(x_ref, w_ref, b_ref, d_ref, dproj_ref, dx_ref, dw_ref, db_ref):
        xv = x_ref[...]
        pre = b_ref[...] + w_ref[CONV_K - 1:CONV_K, :] * xv
        for kk in range(CONV_K - 1):
            pre = pre + w_ref[kk:kk + 1, :] * _shift_down(xv, CONV_K - 1 - kk)
        sig = _sigmoid(pre)
        dpre = d_ref[...] * (sig * (1.0 + pre * (1.0 - sig)))
        dx = w_ref[CONV_K - 1:CONV_K, :] * dpre
        dw_ref[CONV_K - 1:CONV_K, :] = jnp.sum(dpre * xv, axis=0, keepdims=True)
        for kk in range(CONV_K - 1):
            j = CONV_K - 1 - kk
            dx = dx + w_ref[kk:kk + 1, :] * _shift_up(dpre, j)
            dw_ref[kk:kk + 1, :] = jnp.sum(dpre * _shift_down(xv, j), axis=0, keepdims=True)
        dx_ref[...] = dx.astype(dx_ref.dtype)
        db_ref[...] = jnp.sum(dpre, axis=0, keepdims=True)

    return pl.pallas_call(
        body, name=name, grid=(CONV_DIM // CONV_TC,),
        in_specs=[pl.BlockSpec((s, CONV_TC), lambda j: (0, _XBC_BLK0 + j)), pl.BlockSpec((CONV_K, CONV_TC), lambda j: (0, j)),
                  pl.BlockSpec((1, CONV_TC), lambda j: (0, j)), pl.BlockSpec((s, CONV_TC), lambda j: (0, j)),
                  pl.BlockSpec(memory_space=pl.ANY)],
        out_specs=[pl.BlockSpec((s, CONV_TC), lambda j: (0, _XBC_BLK0 + j)), pl.BlockSpec((CONV_K, CONV_TC), lambda j: (0, j)),
                   pl.BlockSpec((1, CONV_TC), lambda j: (0, j))],
        out_shape=[jax.ShapeDtypeStruct(dproj.shape, dproj.dtype), jax.ShapeDtypeStruct((CONV_K, CONV_DIM), F32),
                   jax.ShapeDtypeStruct((1, CONV_DIM), F32)], input_output_aliases={4: 0},
        compiler_params=_cp(VMEM_BIG))(proj, w, b, dxbc, dproj)


def _ssd_common(dtc_ref, dtr_ref, br_ref, bc_ref, ar_ref, ac_ref, csb_ref, cst_ref, csf_ref):
    a_row = -jnp.exp(ar_ref[...])
    dt_c = _softplus(dtc_ref[...] + br_ref[...])
    a_col = -jnp.exp(ac_ref[...])
    dt_r = _softplus(dtr_ref[...] + bc_ref[...])
    row = _iota((CHUNK, CHUNK), 0)
    col = _iota((CHUNK, CHUNK), 1)
    tril = (row >= col).astype(F32)
    triu = (row <= col).astype(F32)
    cs = _dot_hi(tril, dt_c * a_row)
    cst_ref[...] = _dot_hi(dt_r * a_col, triu)
    e64 = (jnp.right_shift(_iota((HPAD, D_INNER), 1), 6) == _iota((HPAD, D_INNER), 0)).astype(F32)
    e128 = (jnp.right_shift(_iota((HPAD, SSM_HEADS * CHUNK), 1), 7) == _iota((HPAD, SSM_HEADS * CHUNK), 0)).astype(F32)
    csb_ref[...] = _dot_hi(cs, e128)
    dt_full = _dot_hi(dt_c, e64)
    csf_ref[...] = _dot_hi(cs, e64)
    cs_full = csf_ref[...]
    cs_last = csf_ref[CHUNK - 1:CHUNK, :]
    e_full = jnp.exp(cs_full)
    f_full = jnp.exp(cs_last - cs_full)
    gamma = jnp.exp(cs_last)
    return a_row, dt_c, cs, dt_full, e_full, f_full, gamma, e64


def _ssd_lambda(csb_ref, cst_ref, h, causal):
    diff = csb_ref[:, h * CHUNK:(h + 1) * CHUNK] - cst_ref[h:h + 1, :]
    return jnp.exp(jnp.where(causal, diff, -1e30))


_SSD_VEC_SPECS = lambda: [pl.BlockSpec((1, HPAD), lambda c: (0, 0)), pl.BlockSpec((HPAD, 1), lambda c: (0, 0)),
                          pl.BlockSpec((1, HPAD), lambda c: (0, 0)), pl.BlockSpec((HPAD, 1), lambda c: (0, 0)),
                          pl.BlockSpec((1, D_INNER), lambda c: (0, 0))]


def _ssd_fwd(xbc, dtc, dtr, bias_row, bias_col, alog_row, alog_col, dfull, name):
    s = xbc.shape[0]
    nc = s // CHUNK

    def body(xbc_ref, dtc_ref, dtr_ref, br_ref, bc_ref, ar_ref, ac_ref, df_ref, y_ref, st_ref,
             ht_ref, csb_ref, cst_ref, csf_ref):
        @pl.when(pl.program_id(0) == 0)
        def _():
            ht_ref[...] = jnp.zeros_like(ht_ref)

        _, _, _, dt_full, e_full, f_full, gamma, _ = _ssd_common(
            dtc_ref, dtr_ref, br_ref, bc_ref, ar_ref, ac_ref, csb_ref, cst_ref, csf_ref)
        x = xbc_ref[:, :D_INNER]
        xdt = x * dt_full
        st_ref[...] = ht_ref[...]
        causal = _iota((CHUNK, CHUNK), 0) >= _iota((CHUNK, CHUNK), 1)
        lo = _iota((CHUNK, CHUNK), 1) < SSM_P
        for g in range(SSM_GROUPS):
            gs = slice(g * SSM_GW, (g + 1) * SSM_GW)
            bg = xbc_ref[:, D_INNER + g * SSM_N:D_INNER + (g + 1) * SSM_N]
            cg = xbc_ref[:, D_INNER + SSM_GROUPS * SSM_N + g * SSM_N:D_INNER + SSM_GROUPS * SSM_N + (g + 1) * SSM_N]
            ht = ht_ref[:, gs]
            cb = _dot_nt(cg, bg)
            yoff = e_full[:, gs] * _dot(cg, ht)
            for jp in range(SSM_GW // CHUNK):
                j = g * (SSM_GW // CHUNK) + jp
                ps = slice(j * CHUNK, (j + 1) * CHUNK)
                x2 = xdt[:, ps]
                y0 = _dot(cb * _ssd_lambda(csb_ref, cst_ref, 2 * j, causal), x2)
                y1 = _dot(cb * _ssd_lambda(csb_ref, cst_ref, 2 * j + 1, causal), x2)
                y_ref[:, ps] = (jnp.where(lo, y0, y1) + yoff[:, jp * CHUNK:(jp + 1) * CHUNK]
                                + x[:, ps] * df_ref[:, ps])
            ht_ref[:, gs] = gamma[:, gs] * ht + _dot_tn(bg, xdt[:, gs] * f_full[:, gs])

    return pl.pallas_call(
        body, name=name, grid=(nc,),
        in_specs=[pl.BlockSpec((CHUNK, CONV_DIM), lambda c: (c, 0)), pl.BlockSpec((CHUNK, HPAD), lambda c: (c, 0)),
                  pl.BlockSpec((HPAD, CHUNK), lambda c: (0, c))] + _SSD_VEC_SPECS(),
        out_specs=[pl.BlockSpec((CHUNK, D_INNER), lambda c: (c, 0)), pl.BlockSpec((None, SSM_N, D_INNER), lambda c: (c, 0, 0))],
        out_shape=[jax.ShapeDtypeStruct((s, D_INNER), F32), jax.ShapeDtypeStruct((nc, SSM_N, D_INNER), F32)],
        scratch_shapes=[pltpu.VMEM((SSM_N, D_INNER), F32), pltpu.VMEM((CHUNK, SSM_HEADS * CHUNK), F32),
                        pltpu.VMEM((HPAD, CHUNK), F32), pltpu.VMEM((CHUNK, D_INNER), F32)],
        compiler_params=_cp(VMEM_BIG))(xbc, dtc, dtr, bias_row, bias_col, alog_row, alog_col, dfull)


def _ssd_bwd(xbc, dtc, dtr, bias_row, bias_col, alog_row, alog_col, dfull, dy, states, name):
    s = xbc.shape[0]
    nc = s // CHUNK
    rev = lambda c: nc - 1 - c

    def body(xbc_ref, dtc_ref, dtr_ref, br_ref, bc_ref, ar_ref, ac_ref, df_ref, dy_ref, st_ref,
             dxbc_ref, ddt_ref, dalog_ref, dd_ref, dbias_ref,
             dht_ref, csb_ref, cst_ref, csf_ref, ddf_ref, dxs_ref, dcsf_ref, dcsl_ref):
        step = pl.program_id(0)

        @pl.when(step == 0)
        def _():
            dht_ref[...] = jnp.zeros_like(dht_ref)
            ddf_ref[...] = jnp.zeros_like(ddf_ref)
            dalog_ref[...] = jnp.zeros_like(dalog_ref)
            dbias_ref[...] = jnp.zeros_like(dbias_ref)
            dd_ref[...] = jnp.zeros_like(dd_ref)

        a_row, dt_c, _, dt_full, e_full, f_full, gamma, e64 = _ssd_common(
            dtc_ref, dtr_ref, br_ref, bc_ref, ar_ref, ac_ref, csb_ref, cst_ref, csf_ref)
        x = xbc_ref[:, :D_INNER]
        xdt = x * dt_full
        dy_all = dy_ref[...]
        ddf_ref[...] += jnp.broadcast_to(jnp.sum(dy_all * x, axis=0, keepdims=True), ddf_ref.shape)
        causal = _iota((CHUNK, CHUNK), 0) >= _iota((CHUNK, CHUNK), 1)
        lo = _iota((CHUNK, CHUNK), 1) < SSM_P
        ones = jnp.ones((CHUNK, HPAD), F32)
        head_lane = _iota((CHUNK, HPAD), 1)
        dcs_heads = jnp.zeros((CHUNK, HPAD), F32)
        for g in range(SSM_GROUPS):
            gs = slice(g * SSM_GW, (g + 1) * SSM_GW)
            b0 = D_INNER + g * SSM_N
            c0 = D_INNER + SSM_GROUPS * SSM_N + g * SSM_N
            bg = xbc_ref[:, b0:b0 + SSM_N]
            cg = xbc_ref[:, c0:c0 + SSM_N]
            ht = st_ref[:, gs]
            dht = dht_ref[:, gs]
            dyg = dy_all[:, gs]
            eg, fg, gg = e_full[:, gs], f_full[:, gs], gamma[:, gs]
            z = _dot(cg, ht)
            dz = dyg * eg
            dcg = _dot_nt(dz, ht)
            dht_new = _dot_tn(cg, dz) + gg * dht
            xf = xdt[:, gs] * fg
            dxf = _dot(bg, dht)
            dbg = _dot_nt(xf, dht)
            dff = dxf * xf
            dcsf_ref[:, gs] = dyg * eg * z - dff
            dcsl_ref[:, gs] = jnp.broadcast_to(
                jnp.sum(dff, axis=0, keepdims=True) + jnp.sum(dht * ht, axis=0, keepdims=True) * gg, (8, SSM_GW))
            cb = _dot_nt(cg, bg)
            dcb = jnp.zeros((CHUNK, CHUNK), F32)
            for jp in range(SSM_GW // CHUNK):
                j = g * (SSM_GW // CHUNK) + jp
                ps = slice(j * CHUNK, (j + 1) * CHUNK)
                x2 = xdt[:, ps]
                dy2 = dy_all[:, ps]
                dxh = []
                for hh in range(2):
                    h = 2 * j + hh
                    lam = _ssd_lambda(csb_ref, cst_ref, h, causal)
                    mh = cb * lam
                    dyh = jnp.where(lo, dy2, 0.0) if hh == 0 else jnp.where(lo, 0.0, dy2)
                    dm = _dot_nt(dyh, x2)
                    dcb = dcb + dm * lam
                    gm = dm * mh
                    rs = jnp.sum(gm, axis=1, keepdims=True)
                    csum = _dot_hi(gm, ones, ((0,), (0,)))
                    dcs_heads = dcs_heads + jnp.where(head_lane == h, rs - csum, 0.0)
                    dxh.append(_dot_tn(mh, dy2))
                dxs_ref[:, ps] = jnp.where(lo, dxh[0], dxh[1]) + dxf[:, jp * CHUNK:(jp + 1) * CHUNK] * fg[:, jp * CHUNK:(jp + 1) * CHUNK]
            dxbc_ref[:, b0:b0 + SSM_N] = (dbg + _dot_tn(dcb, cg)).astype(dxbc_ref.dtype)
            dxbc_ref[:, c0:c0 + SSM_N] = (dcg + _dot(dcb, bg)).astype(dxbc_ref.dtype)
            dht_ref[:, gs] = dht_new
        dxs = dxs_ref[...]
        dcs_heads = dcs_heads + _dot_hi(dcsf_ref[...], e64, ((1,), (1,)))
        dcs_last = _dot_hi(dcsl_ref[...], e64, ((1,), (1,)))
        dcs_heads = dcs_heads + jnp.where(_iota((CHUNK, HPAD), 0) == CHUNK - 1, dcs_last[0:1, :], 0.0)
        triu = (_iota((CHUNK, CHUNK), 0) <= _iota((CHUNK, CHUNK), 1)).astype(F32)
        dda = _dot_hi(triu, dcs_heads)
        ddt = dda * a_row + _dot_hi(dxs * x, e64, ((1,), (1,)))
        dxbc_ref[:, :D_INNER] = (dxs * dt_full + dy_all * df_ref[...]).astype(dxbc_ref.dtype)
        dalog_ref[...] += jnp.sum(dda * dt_c, axis=0, keepdims=True) * a_row
        ddt_raw = ddt * _sigmoid(dtc_ref[...] + br_ref[...])
        ddt_ref[...] = ddt_raw.astype(ddt_ref.dtype)
        dbias_ref[...] += jnp.sum(ddt_raw, axis=0, keepdims=True)

        @pl.when(step == nc - 1)
        def _():
            dd_ref[...] = _dot_hi(ddf_ref[...], e64, ((1,), (1,)))[0:1, :]

    vec = pl.BlockSpec((1, HPAD), lambda c: (0, 0))
    return pl.pallas_call(
        body, name=name, grid=(nc,),
        in_specs=[pl.BlockSpec((CHUNK, CONV_DIM), lambda c: (rev(c), 0)), pl.BlockSpec((CHUNK, HPAD), lambda c: (rev(c), 0)),
                  pl.BlockSpec((HPAD, CHUNK), lambda c: (0, rev(c)))] + _SSD_VEC_SPECS()
                 + [pl.BlockSpec((CHUNK, D_INNER), lambda c: (rev(c), 0)),
                    pl.BlockSpec((None, SSM_N, D_INNER), lambda c: (rev(c), 0, 0))],
        out_specs=[pl.BlockSpec((CHUNK, CONV_DIM), lambda c: (rev(c), 0)), pl.BlockSpec((CHUNK, HPAD), lambda c: (rev(c), 0)),
                   vec, vec, vec],
        out_shape=[jax.ShapeDtypeStruct((s, CONV_DIM), F32), jax.ShapeDtypeStruct((s, HPAD), _ACT),
                   jax.ShapeDtypeStruct((1, HPAD), F32), jax.ShapeDtypeStruct((1, HPAD), F32),
                   jax.ShapeDtypeStruct((1, HPAD), F32)],
        scratch_shapes=[pltpu.VMEM((SSM_N, D_INNER), F32), pltpu.VMEM((CHUNK, SSM_HEADS * CHUNK), F32),
                        pltpu.VMEM((HPAD, CHUNK), F32), pltpu.VMEM((CHUNK, D_INNER), F32),
                        pltpu.VMEM((8, D_INNER), F32), pltpu.VMEM((CHUNK, D_INNER), F32),
                        pltpu.VMEM((CHUNK, D_INNER), F32), pltpu.VMEM((8, D_INNER), F32)],
        compiler_params=_cp(VMEM_BIG))(xbc, dtc, dtr, bias_row, bias_col, alog_row, alog_col, dfull, dy, states)


def _gate_fwd(y, proj, gn, name, tm=256):
    s = y.shape[0]
    tm = min(tm, s)

    def body(y_ref, z_ref, gn_ref, o_ref):
        for g in range(SSM_GROUPS):
            gs = slice(g * SSM_GW, (g + 1) * SSM_GW)
            z = z_ref[:, gs]
            t = y_ref[:, gs] * (z * _sigmoid(z))
            r = lax.rsqrt(jnp.mean(t * t, axis=-1, keepdims=True) + EPS)
            o_ref[:, gs] = (t * r * gn_ref[:, gs]).astype(o_ref.dtype)

    row = pl.BlockSpec((tm, D_INNER), lambda i: (i, 0))
    return pl.pallas_call(
        body, name=name, grid=(s // tm,), in_specs=[row, row, pl.BlockSpec((1, D_INNER), lambda i: (0, 0))],
        out_specs=row, out_shape=jax.ShapeDtypeStruct((s, D_INNER + X_WIDTH), _ACT))(y, proj, gn)


def _gate_bwd(y, proj, gn, dcat, name, tm=256):
    s = y.shape[0]
    tm = min(tm, s)

    def body(y_ref, z_ref, gn_ref, dm_ref, dy_ref, dz_ref, dgn_ref):
        @pl.when(pl.program_id(0) == 0)
        def _():
            dgn_ref[...] = jnp.zeros_like(dgn_ref)

        for g in range(SSM_GROUPS):
            gs = slice(g * SSM_GW, (g + 1) * SSM_GW)
            z = z_ref[:, gs]
            yv = y_ref[:, gs]
            sig = _sigmoid(z)
            sz = z * sig
            t = yv * sz
            r = lax.rsqrt(jnp.mean(t * t, axis=-1, keepdims=True) + EPS)
            th = t * r
            dm = dm_ref[:, gs].astype(F32)
            dmg = dm * gn_ref[:, gs]
            dt_ = r * (dmg - th * jnp.mean(dmg * th, axis=-1, keepdims=True))
            dgn_ref[:, gs] += jnp.sum(dm * th, axis=0, keepdims=True)
            dy_ref[:, gs] = dt_ * sz
            dz_ref[:, gs] = (dt_ * yv * (sig * (1.0 + z * (1.0 - sig)))).astype(dz_ref.dtype)

    row = pl.BlockSpec((tm, D_INNER), lambda i: (i, 0))
    vec = pl.BlockSpec((1, D_INNER), lambda i: (0, 0))
    return pl.pallas_call(
        body, name=name, grid=(s // tm,), in_specs=[row, row, vec, row], out_specs=[row, row, vec],
        out_shape=[jax.ShapeDtypeStruct((s, D_INNER), F32), jax.ShapeDtypeStruct((s, 6 * D_MODEL), _ACT),
                   jax.ShapeDtypeStruct((1, D_INNER), F32)])(y, proj, gn, dcat)


def _block_of(kind, width):
    if kind == "col":
        return lambda ref, j: ref.at[:, :, pl.ds(pl.multiple_of(j * width, 128), width)]
    if kind == "row":
        return lambda ref, j: ref.at[:, pl.ds(pl.multiple_of(j * width, 8), width), :]
    return lambda ref, j: ref.at[j]


def _coords():
    return lax.axis_index("x"), lax.axis_index("y"), lax.axis_index("c")


def _rel_chip(x, y, k):
    return (1 - x if k & 1 else x), (1 - y if k & 2 else y)


_HBM = lambda: pl.BlockSpec(memory_space=pltpu.HBM)


def _all_gather(shards, layouts, name):
    n = len(shards)
    blocks = [_block_of(kind, width) for kind, width, _ in layouts]

    def body(*refs):
        _all_gather_body(refs[:n], refs[n:2 * n], *refs[2 * n:], blocks)

    return pl.pallas_call(
        body, name=name, in_specs=[_HBM()] * n, out_specs=[_HBM()] * n,
        out_shape=[jax.ShapeDtypeStruct(shape, sh.dtype) for sh, (_, _, shape) in zip(shards, layouts)],
        scratch_shapes=[pltpu.SemaphoreType.DMA((n, 7)), pltpu.SemaphoreType.DMA((n, 7)), pltpu.SemaphoreType.DMA((n,))])(*shards)


def _all_gather_body(ins, outs, send_sems, recv_sems, local_sems, blocks):
    n = len(ins)
    x, y, c = _coords()
    sibling = (x, y, 1 - c)

    def copy(t, k, chip, core, to, src=None):
        dst = blocks[t](outs[t], 4 * chip[0] + 2 * chip[1] + core)
        return pltpu.make_async_remote_copy(
            src_ref=dst if src is None else src, dst_ref=dst, send_sem=send_sems.at[t, k],
            recv_sem=recv_sems.at[t, k], device_id=to, device_id_type=MESH)

    started = []
    for t in range(n):
        mine = pltpu.make_async_copy(ins[t], blocks[t](outs[t], 4 * x + 2 * y + c), local_sems.at[t])
        mine.start()
        started.append(mine)
    sends = []
    for t in range(n):
        for k in range(4):
            px, py = _rel_chip(x, y, k)
            cp = copy(t, k, (x, y), c, (px, py, 1 - c if k == 0 else c), src=ins[t])
            cp.start()
            sends.append(cp)
    for t in range(n):
        for k in range(1, 4):
            chip = _rel_chip(x, y, k)
            copy(t, k, chip, c, sibling).wait_recv()
            fwd = copy(t, 3 + k, chip, c, sibling)
            fwd.start()
            sends.append(fwd)
    for t in range(n):
        copy(t, 0, (x, y), 1 - c, sibling).wait_recv()
        for k in range(1, 4):
            copy(t, 3 + k, _rel_chip(x, y, k), 1 - c, sibling).wait_recv()
    for cp in sends:
        cp.wait_send()
    for mine in started:
        mine.wait()


def _handshake(peers):
    barrier = pltpu.get_barrier_semaphore()
    for peer in peers:
        pl.semaphore_signal(barrier, inc=1, device_id=peer, device_id_type=MESH)
    pl.semaphore_wait(barrier, len(peers))


def _two_level_peers():
    x, y, c = _coords()
    return [(x, y, 1 - c)] + [(*_rel_chip(x, y, k), c) for k in range(1, 4)]


SEQ_ID_GATHER, SEQ_ID_SIBLING, SEQ_ID_CHIPS = 1, 2, 3


def _sequencer_call(body, peers, operands, out_types, sems, name, collective_id):
    n_in, n_out = len(operands), len(out_types)

    def launch(*refs):
        _handshake(peers())
        body(refs[:n_in], refs[n_in:n_in + n_out], *refs[n_in + n_out:])

    return pl.kernel(
        launch, name=name, out_type=out_types, mesh=plsc.ScalarSubcoreMesh(axis_name="seq", num_cores=1),
        scratch_types=sems, compiler_params=pltpu.CompilerParams(collective_id=collective_id))(*operands)


def _all_gather_seq(shards, layouts, name):
    n = len(shards)
    blocks = [_block_of(kind, width) for kind, width, _ in layouts]
    return _sequencer_call(
        lambda ins, outs, *sems: _all_gather_body(ins, outs, *sems, blocks), _two_level_peers, shards,
        [jax.ShapeDtypeStruct(shape, sh.dtype) for sh, (_, _, shape) in zip(shards, layouts)],
        [pltpu.SemaphoreType.DMA((n, 7)), pltpu.SemaphoreType.DMA((n, 7)), pltpu.SemaphoreType.DMA((n,))],
        name, SEQ_ID_GATHER)


def _rs_to_sibling(grads, layouts, name):
    n = len(grads)
    blocks = [_block_of(kind, width) for kind, width, _ in layouts]

    def body(ins, outs, send_sems, recv_sems):
        x, y, c = _coords()
        sibling = (x, y, 1 - c)
        cps = []
        for t in range(n):
            for k in range(4):
                px, py = _rel_chip(x, y, k)
                cp = pltpu.make_async_remote_copy(
                    src_ref=blocks[t](ins[t], 4 * px + 2 * py + (1 - c)), dst_ref=outs[t].at[k],
                    send_sem=send_sems.at[t, k], recv_sem=recv_sems.at[t, k], device_id=sibling, device_id_type=MESH)
                cp.start()
                cps.append(cp)
        for cp in cps:
            cp.wait_recv()
        for cp in cps:
            cp.wait_send()

    def sibling_only():
        x, y, c = _coords()
        return [(x, y, 1 - c)]

    return _sequencer_call(
        body, sibling_only, grads,
        [jax.ShapeDtypeStruct((4,) + shape, g.dtype) for g, (_, _, shape) in zip(grads, layouts)],
        [pltpu.SemaphoreType.DMA((n, 4)), pltpu.SemaphoreType.DMA((n, 4))], name, SEQ_ID_SIBLING)


def _rs_chip_sum(grad, recv, layout, xyc, name):
    kind, width, shape = layout
    r, ccols = shape

    def src_index(k, xyc_ref):
        px = jnp.where(k % 2 == 1, 1 - xyc_ref[0], xyc_ref[0])
        py = jnp.where(k // 2 == 1, 1 - xyc_ref[1], xyc_ref[1])
        return 4 * px + 2 * py + xyc_ref[2]

    if kind == "col":
        g_spec = pl.BlockSpec((r, ccols), lambda k, s_: (0, src_index(k, s_)))
    elif kind == "row":
        g_spec = pl.BlockSpec((r, ccols), lambda k, s_: (src_index(k, s_), 0))
    else:
        g_spec = pl.BlockSpec((None, r, ccols), lambda k, s_: (src_index(k, s_), 0, 0))

    def body(xyc_ref, g_ref, r_ref, o_ref):
        o_ref[...] = (g_ref[...].astype(F32) + r_ref[...].astype(F32)).astype(o_ref.dtype)

    slot = pl.BlockSpec((None, r, ccols), lambda k, s_: (k, 0, 0))
    return pl.pallas_call(
        body, name=name,
        grid_spec=pltpu.PrefetchScalarGridSpec(num_scalar_prefetch=1, grid=(4,), in_specs=[g_spec, slot], out_specs=slot),
        out_shape=jax.ShapeDtypeStruct((4, r, ccols), grad.dtype), compiler_params=_cp(VMEM_BIG))(xyc, grad, recv)


def _rs_across_chips(parts, name):
    n = len(parts)

    def body(ins, outs, send_sems, recv_sems):
        x, y, c = _coords()
        cps = []
        for t in range(n):
            for k in range(1, 4):
                px, py = _rel_chip(x, y, k)
                cp = pltpu.make_async_remote_copy(
                    src_ref=ins[t].at[k], dst_ref=outs[t].at[k - 1], send_sem=send_sems.at[t, k - 1],
                    recv_sem=recv_sems.at[t, k - 1], device_id=(px, py, c), device_id_type=MESH)
                cp.start()
                cps.append(cp)
        for cp in cps:
            cp.wait_recv()
        for cp in cps:
            cp.wait_send()

    def other_chips():
        x, y, c = _coords()
        return [(*_rel_chip(x, y, k), c) for k in range(1, 4)]

    return _sequencer_call(
        body, other_chips, parts, [jax.ShapeDtypeStruct((3,) + p.shape[1:], p.dtype) for p in parts],
        [pltpu.SemaphoreType.DMA((n, 3)), pltpu.SemaphoreType.DMA((n, 3))], name, SEQ_ID_CHIPS)


def _adamw_math(w, g, m, v):
    m = ADAM_B1 * m + (1.0 - ADAM_B1) * g
    v = ADAM_B2 * v + (1.0 - ADAM_B2) * jnp.square(g)
    m_hat = m / (1.0 - ADAM_B1 ** ADAM_STEP)
    v_hat = v / (1.0 - ADAM_B2 ** ADAM_STEP)
    delta = -ADAM_LR * (m_hat / (jnp.sqrt(v_hat) + ADAM_EPS) + ADAM_WD * w)
    return delta, m, v


def _row_tile(rows, cap):
    best = None
    for cand in range(8, min(rows, cap) + 1, 8):
        if rows % cand == 0:
            best = cand
    assert best is not None, rows
    return best


def _adamw(w, m, v, parts, name, layer=None, prev=None, tr=256):
    r, ccols = w.shape[-2:]
    tr = _row_tile(r, tr)
    npart = len(parts)

    def wspec():
        if layer is None:
            return pl.BlockSpec((tr, ccols), lambda i: (i, 0))
        return pl.BlockSpec((None, tr, ccols), lambda i: (layer, i, 0))

    def pspec(lead):
        if lead is None:
            return pl.BlockSpec((tr, ccols), lambda i: (i, 0))
        return pl.BlockSpec((None, tr, ccols), lambda i: (lead, i, 0))

    def body(*refs):
        w_ref, m_ref, v_ref = refs[:3]
        p_refs = refs[3:3 + npart]
        outs = refs[len(refs) - 4:]
        g = p_refs[0][...].astype(F32)
        for p_ref in p_refs[1:]:
            g = g + p_ref[...].astype(F32)
        delta, mn, vn = _adamw_math(w_ref[...], g, m_ref[...], v_ref[...])
        outs[0][...] = g
        outs[1][...] = delta
        outs[2][...] = mn
        outs[3][...] = vn

    operands = [w, m, v] + [p for p, _ in parts]
    in_specs = [wspec(), wspec(), wspec()] + [pspec(lead) for _, lead in parts]
    aliases = {}
    if prev is not None:
        for i, p in enumerate(prev):
            aliases[len(operands)] = i
            operands.append(p)
            in_specs.append(pl.BlockSpec(memory_space=pl.ANY))
    return pl.pallas_call(
        body, name=name, grid=(r // tr,), in_specs=in_specs, out_specs=[wspec()] * 4,
        out_shape=[jax.ShapeDtypeStruct(w.shape, F32)] * 4, input_output_aliases=aliases)(*operands)


def _sum8(buf, name):
    _, r, ccols = buf.shape

    def body(b_ref, o_ref):
        acc = b_ref[0]
        for j in range(1, N_DEV):
            acc = acc + b_ref[j]
        o_ref[...] = acc

    tr = _row_tile(r, 256)
    return pl.pallas_call(
        body, name=name, grid=(r // tr,), in_specs=[pl.BlockSpec((N_DEV, tr, ccols), lambda i: (0, i, 0))],
        out_specs=pl.BlockSpec((tr, ccols), lambda i: (i, 0)), out_shape=jax.ShapeDtypeStruct((r, ccols), F32))(buf)


def _pack(arrays):
    pieces, layout, off = [], [], 0
    for a in arrays:
        n = a.size
        padded = -(-n // 1024) * 1024
        flat = a.reshape(-1).astype(F32)
        if padded != n:
            flat = jnp.pad(flat, (0, padded - n))
        pieces.append(flat.reshape(padded // 128, 128))
        layout.append((off, n, a.shape))
        off += padded // 128
    return jnp.concatenate(pieces, axis=0), layout


def _unpack(packed, layout):
    out = []
    for off, n, shape in layout:
        rows = -(-n // 1024) * 8
        out.append(packed[off:off + rows].reshape(-1)[:n].reshape(shape))
    return out


def kernel(x, mem, norm_mix, norm_ffn, mem_norm, w_kv, w_out, w_ffn1, w_ffn2, a_in, a_ln_g, a_ln_b, a_ws, a_bs, b_in, b_conv_w, b_conv_b, b_dt_bias, b_a_log, b_d, b_gnorm, final_norm, loss_target, m_norm_mix, m_norm_ffn, m_mem_norm, m_w_kv, m_w_out, m_w_ffn1, m_w_ffn2, m_a_in, m_a_ln_g, m_a_ln_b, m_a_ws, m_a_bs, m_b_in, m_b_conv_w, m_b_conv_b, m_b_dt_bias, m_b_a_log, m_b_d, m_b_gnorm, m_final_norm, v_norm_mix, v_norm_ffn, v_mem_norm, v_w_kv, v_w_out, v_w_ffn1, v_w_ffn2, v_a_in, v_a_ln_g, v_a_ln_b, v_a_ws, v_a_bs, v_b_in, v_b_conv_w, v_b_conv_b, v_b_dt_bias, v_b_a_log, v_b_d, v_b_gnorm, v_final_norm):
    s = x.shape[1]
    xs = x.reshape(s, D_MODEL)
    mems = mem.reshape(N_MEM, D_MODEL)
    target = loss_target.reshape(s, D_MODEL)
    ax, ay, ac = lax.axis_index("x"), lax.axis_index("y"), lax.axis_index("c")
    me = 4 * ax + 2 * ay + ac
    xyc = jnp.stack([ax, ay, ac]).astype(jnp.int32)

    b_cols = b_in.shape[2]
    act = lambda a: a.astype(_ACT)
    lay_f1, lay_f2 = ("col", 512, (1, D_MODEL, D_FF)), ("row", 512, (1, D_FF, D_MODEL))
    lay_out, lay_kv = ("row", 384, (1, 3 * D_MODEL, D_MODEL)), ("col", 256, (1, D_MODEL, 2 * X_WIDTH))
    small_w_pack = _pack([b_conv_w[0], b_conv_b[0], b_gnorm[0]])[0]
    WA, wkv0 = _all_gather_seq([act(a_in), act(w_kv[0:1])], [("col", 640, (1, D_MODEL, 5 * D_MODEL)), lay_kv], "ag_proj_a")
    (wo0,) = _all_gather_seq([act(w_out[0:1])], [lay_out], "ag_out0")
    w1_0, w2_0 = _all_gather_seq([act(w_ffn1[0:1]), act(w_ffn2[0:1])], [lay_f1, lay_f2], "ag_ffn0")
    wb_blk, wkv1, small_w = _all_gather_seq(
        [act(b_in[0]), act(w_kv[1:2]), small_w_pack],
        [("blk", 0, (N_DEV, D_MODEL, b_cols)), lay_kv, ("blk", 0, (N_DEV, 32, 128))], "ag_proj_b")
    (wo1,) = _all_gather_seq([act(w_out[1:2])], [lay_out], "ag_out1")
    w1_1, w2_1 = _all_gather_seq([act(w_ffn1[1:2]), act(w_ffn2[1:2])], [lay_f1, lay_f2], "ag_ffn1")
    W1, W2, WO, WKV = [w1_0, w1_1], [w2_0, w2_1], [wo0, wo1], [wkv0, wkv1]
    wb_full = jnp.transpose(wb_blk, (1, 0, 2)).reshape(D_MODEL, N_DEV * b_cols)
    dt0 = D_INNER + CONV_DIM
    WB = jnp.concatenate([wb_full[:, :dt0], wb_full[:, dt0 + SSM_HEADS:]], axis=1)
    WBDT = jnp.pad(wb_full[:, dt0:dt0 + SSM_HEADS], ((0, 0), (0, HPAD - SSM_HEADS)))

    row = lambda a: a.reshape(1, -1)
    nmix = [row(norm_mix[0]), row(norm_mix[1])]
    nffn = [row(norm_ffn[0]), row(norm_ffn[1])]
    nmem = [row(mem_norm[0]), row(mem_norm[1])]
    fin = row(final_norm)
    lng, lnb = a_ln_g.reshape(1, D_INNER), a_ln_b.reshape(1, D_INNER)
    ws = a_ws[0]
    bs3 = a_bs[0].reshape(A_GROUPS, CHUNK, 1)
    pad_h = lambda a: jnp.pad(a.reshape(-1), (0, HPAD - SSM_HEADS))
    bias_row, bias_col = pad_h(b_dt_bias).reshape(1, HPAD), pad_h(b_dt_bias).reshape(HPAD, 1)
    alog_row, alog_col = pad_h(b_a_log).reshape(1, HPAD), pad_h(b_a_log).reshape(HPAD, 1)
    dfull = jnp.repeat(b_d.reshape(-1), SSM_P).reshape(1, D_INNER)

    cw_sh, cb_sh, gn_sh = 4 * 384, 384, 256
    sw = small_w.reshape(N_DEV, 32 * 128)
    conv_w = jnp.transpose(sw[:, :cw_sh].reshape(N_DEV, CONV_K, 384), (1, 0, 2)).reshape(CONV_K, CONV_DIM)
    conv_b = sw[:, 2048:2048 + cb_sh].reshape(1, CONV_DIM)
    gnorm = sw[:, 3072:3072 + gn_sh].reshape(1, D_INNER)

    kvs, mns = [None, None], [None, None]

    def mem_kv(i):
        mns[i] = _rms_fwd(mems, nmem[i], f"mem_norm{i}")
        kvs[i] = _mm(mns[i], WKV[i], m=N_MEM, n=2 * X_WIDTH, k=D_MODEL, b_at=(0, 0, 0), out_dtype=_ACT, name=f"kv{i}")

    def ffn_fwd(h, i):
        f = _rms_fwd(h, nffn[i], f"ffn_norm{i}")
        p = _mm(f, W1[i], m=s, n=D_FF, k=D_MODEL, b_at=(0, 0, 0), out_dtype=_ACT, name=f"ffn_up{i}")
        hn = _mm(p, W2[i], m=s, n=D_MODEL, k=D_FF, b_at=(0, 0, 0), a_pro="relu2", add=h, name=f"ffn_down{i}")
        return f, p, hn

    def out_proj(h, cat, i):
        return _mm(cat, WO[i], m=s, n=D_MODEL, k=3 * D_MODEL, b_at=(0, 0, 0), add=h, name=f"out_proj{i}")

    a0 = _rms_fwd(xs, nmix[0], "mix_norm0")
    proj_a = _mm(a0, WA, m=s, n=5 * D_MODEL, k=D_MODEL, b_at=(0, 0, 0), name="proj_a")
    mem_kv(0)
    cat_a = _gmlp_fwd(proj_a, lng, lnb, ws, bs3, "gmlp_fwd")
    cat_a = _attn_fwd(proj_a, 4, kvs[0], cat_a, "attn_fwd0")
    h1 = out_proj(xs, cat_a, 0)
    f0, p0, h2 = ffn_fwd(h1, 0)

    a1 = _rms_fwd(h2, nmix[1], "mix_norm1")
    proj_b = _mm(a1, WB, m=s, n=6 * D_MODEL, k=D_MODEL, name="proj_b")
    dt_raw = _mm(a1, WBDT, m=s, n=HPAD, k=D_MODEL, name="proj_dt")
    dt_raw_t = dt_raw.T
    xbc = _conv_fwd(proj_b, conv_w, conv_b, "conv_fwd")
    y_ssd, states = _ssd_fwd(xbc, dt_raw, dt_raw_t, bias_row, bias_col, alog_row, alog_col, dfull, "ssd_fwd")
    cat_b = _gate_fwd(y_ssd, proj_b, gnorm, "gate_fwd")
    mem_kv(1)
    cat_b = _attn_fwd(proj_b, 5, kvs[1], cat_b, "attn_fwd1")
    h3 = out_proj(h2, cat_b, 1)
    f1, p1, h4 = ffn_fwd(h3, 1)

    loss_part, dh, dh_act, d_fin = _loss_head(h4, fin, target, "loss_head")
    loss = lax.psum(loss_part[0, 0], ("x", "y", "c"))

    g_f1, g_f2, g_out, g_kv = [None, None], [None, None], [None, None], [None, None]
    d_nffn, d_nmix, d_nmem = [None, None], [None, None], [None, None]

    def ffn_bwd(dh, dh_act, h_in, f, p, i):
        dp = _mm(dh_act, W2[i], m=s, n=D_FF, k=D_MODEL, tb=True, b_at=(0, 0, 0), epi_p=p, out_dtype=_ACT, name=f"ffn_down_dx{i}")
        g_f2[i] = _mm(p, dh_act, m=D_FF, n=D_MODEL, k=s, ta=True, a_pro="relu2", out_dtype=_ACT, name=f"ffn_down_dw{i}")
        g_f1[i] = _mm(f, dp, m=D_MODEL, n=D_FF, k=s, ta=True, out_dtype=_ACT, name=f"ffn_up_dw{i}")
        df = _mm(dp, W1[i], m=s, n=D_MODEL, k=D_FF, tb=True, b_at=(0, 0, 0), name=f"ffn_up_dx{i}")
        dh_in, dh_in_act, d_nffn[i] = _rms_bwd(h_in, nffn[i], df, dh, f"ffn_norm_bwd{i}")
        return dh_in, dh_in_act

    def out_bwd(dh_act, cat, i):
        dcat = _mm(dh_act, WO[i], m=s, n=3 * D_MODEL, k=D_MODEL, tb=True, b_at=(0, 0, 0), out_dtype=_ACT, name=f"out_dx{i}")
        g_out[i] = _mm(cat, dh_act, m=3 * D_MODEL, n=D_MODEL, k=s, ta=True, out_dtype=_ACT, name=f"out_dw{i}")
        return dcat

    def mem_bwd(dkv, i):
        g_kv[i] = _mm(mns[i], dkv, m=D_MODEL, n=2 * X_WIDTH, k=N_MEM, ta=True, out_dtype=_ACT, name=f"kv_dw{i}")
        dmn = _mm(dkv, WKV[i], m=N_MEM, n=D_MODEL, k=2 * X_WIDTH, tb=True, b_at=(0, 0, 0), name=f"kv_dx{i}")
        _, _, d_nmem[i] = _rms_bwd(mems, nmem[i], dmn, None, f"mem_norm_bwd{i}")

    lay_g = {"f1": ("col", 512, (D_MODEL, 512)), "f2": ("row", 512, (512, D_MODEL)), "out": ("row", 384, (384, D_MODEL)),
             "kv": ("col", 256, (D_MODEL, 256)), "a": ("col", 640, (D_MODEL, 640)), "b": ("blk", 0, (D_MODEL, b_cols))}
    reduced = {}

    def reduce_scatter(group, tag):
        grads3, lays3 = [], []
        for fam, _, g in group:
            kind, width, shape = lay_g[fam]
            grads3.append(g if kind == "blk" else g.reshape((1,) + g.shape))
            lays3.append((kind, width, shape if kind == "blk" else (1,) + shape))
        recv1 = _rs_to_sibling(grads3, lays3, f"rs_sibling_{tag}")
        parts = [_rs_chip_sum(g, recv1[t].reshape((4,) + lay_g[fam][2]), lay_g[fam], xyc, f"rs_chip_sum_{fam}{i}")
                 for t, (fam, i, g) in enumerate(group)]
        recv2 = _rs_across_chips(parts, f"rs_chips_{tag}")
        for (fam, i, _), p, r2 in zip(group, parts, recv2):
            reduced[fam, i] = (p, r2)

    dh3, dh3_act = ffn_bwd(dh, dh_act, h3, f1, p1, 1)
    reduce_scatter([("f1", 1, g_f1[1]), ("f2", 1, g_f2[1])], "ffn1")
    dcat_b = out_bwd(dh3_act, cat_b, 1)
    dy_ssd, dproj_b, d_gnorm = _gate_bwd(y_ssd, proj_b, gnorm, dcat_b, "gate_bwd")
    dproj_b, dkv_b = _attn_bwd(proj_b, 5, kvs[1], dcat_b, dproj_b, "attn_bwd1")
    mem_bwd(dkv_b, 1)
    dxbc, ddt_raw, d_alog, d_dskip, d_dtbias = _ssd_bwd(
        xbc, dt_raw, dt_raw_t, bias_row, bias_col, alog_row, alog_col, dfull, dy_ssd, states, "ssd_bwd")
    dproj_b, d_convw, d_convb = _conv_bwd(proj_b, conv_w, conv_b, dxbc, dproj_b, "conv_bwd")
    gb = _mm(a1, dproj_b, m=D_MODEL, n=6 * D_MODEL, k=s, ta=True, out_dtype=_ACT, name="proj_b_dw")
    gb_dt = _mm(a1, ddt_raw, m=D_MODEL, n=HPAD, k=s, ta=True, out_dtype=_ACT, name="proj_b_dw_dt")
    gb_full = jnp.concatenate([gb[:, :dt0], gb_dt[:, :SSM_HEADS], gb[:, dt0:]], axis=1)
    gb_blk = jnp.transpose(gb_full.reshape(D_MODEL, N_DEV, b_cols), (1, 0, 2))
    reduce_scatter([("out", 1, g_out[1]), ("kv", 1, g_kv[1]), ("b", 0, gb_blk)], "mix1")
    da1 = _mm(dproj_b, WB, m=s, n=D_MODEL, k=6 * D_MODEL, tb=True, name="proj_b_dx")
    da1 = _mm(ddt_raw, WBDT, m=s, n=D_MODEL, k=HPAD, tb=True, add=da1, name="proj_b_dx_dt")
    dh2, dh2_act, d_nmix[1] = _rms_bwd(h2, nmix[1], da1, dh3, "mix_norm_bwd1")

    dh1, dh1_act = ffn_bwd(dh2, dh2_act, h1, f0, p0, 0)
    reduce_scatter([("f1", 0, g_f1[0]), ("f2", 0, g_f2[0])], "ffn0")
    dcat_a = out_bwd(dh1_act, cat_a, 0)
    dproj_a, d_ws, d_bs3, d_lng, d_lnb = _gmlp_bwd(proj_a, dcat_a, lng, lnb, ws, bs3, "gmlp_bwd")
    dproj_a, dkv_a = _attn_bwd(proj_a, 4, kvs[0], dcat_a, dproj_a, "attn_bwd0")
    mem_bwd(dkv_a, 0)
    ga = _mm(a0, dproj_a, m=D_MODEL, n=5 * D_MODEL, k=s, ta=True, out_dtype=_ACT, name="proj_a_dw")
    reduce_scatter([("out", 0, g_out[0]), ("kv", 0, g_kv[0]), ("a", 0, ga)], "mix0")
    da0 = _mm(dproj_a, WA, m=s, n=D_MODEL, k=5 * D_MODEL, tb=True, b_at=(0, 0, 0), name="proj_a_dx")
    grad_x, _, d_nmix[0] = _rms_bwd(xs, nmix[0], da0, dh1, "mix_norm_bwd0")

    def big_update(w, m, v, fam, nlayer):
        res = None
        for i in range(nlayer):
            part, recv2 = reduced[fam, i]
            plist = [(part, 0), (recv2, 0), (recv2, 1), (recv2, 2)]
            res = _adamw(w, m, v, plist, f"adamw_{fam}{i}", layer=i, prev=res)
        return res

    r_f1 = big_update(w_ffn1, m_w_ffn1, v_w_ffn1, "f1", 2)
    r_f2 = big_update(w_ffn2, m_w_ffn2, v_w_ffn2, "f2", 2)
    r_out = big_update(w_out, m_w_out, v_w_out, "out", 2)
    r_kv = big_update(w_kv, m_w_kv, v_w_kv, "kv", 2)
    r_a = big_update(a_in, m_a_in, v_a_in, "a", 1)
    r_b = big_update(b_in, m_b_in, v_b_in, "b", 1)

    rep_names = ["norm_mix", "norm_ffn", "mem_norm", "a_ln_g", "a_ln_b", "a_ws", "a_bs", "b_dt_bias", "b_a_log", "b_d",
                 "final_norm"]
    rep_grads = [jnp.concatenate(d_nmix, axis=0), jnp.concatenate(d_nffn, axis=0), jnp.concatenate(d_nmem, axis=0),
                 d_lng, d_lnb, d_ws.reshape(1, A_GROUPS, CHUNK, CHUNK), d_bs3.reshape(1, A_GROUPS, CHUNK),
                 d_dtbias[:, :SSM_HEADS], d_alog[:, :SSM_HEADS], d_dskip[:, :SSM_HEADS], d_fin.reshape(D_MODEL)]
    rep_w = [norm_mix, norm_ffn, mem_norm, a_ln_g, a_ln_b, a_ws, a_bs, b_dt_bias, b_a_log, b_d, final_norm]
    rep_m = [m_norm_mix, m_norm_ffn, m_mem_norm, m_a_ln_g, m_a_ln_b, m_a_ws, m_a_bs, m_b_dt_bias, m_b_a_log, m_b_d, m_final_norm]
    rep_v = [v_norm_mix, v_norm_ffn, v_mem_norm, v_a_ln_g, v_a_ln_b, v_a_ws, v_a_bs, v_b_dt_bias, v_b_a_log, v_b_d, v_final_norm]
    rep_grads = [g.reshape(w.shape) for g, w in zip(rep_grads, rep_w)]
    sh_grads = [d_convw, d_convb, d_gnorm]
    g_pack, g_layout = _pack(rep_grads + sh_grads)
    n_rep = len(rep_grads)
    (g_all,) = _all_gather([g_pack], [("blk", 0, (N_DEV,) + g_pack.shape)], "ag_small_grads")
    g_small = _sum8(g_all, "sum_small_grads")
    g_list = _unpack(g_small, g_layout)
    wp, w_layout = _pack(rep_w)
    mp, _ = _pack(rep_m)
    vp, _ = _pack(rep_v)
    gp, _ = _pack(g_list[:n_rep])
    rep_res = [_unpack(o, w_layout) for o in _adamw(wp, mp, vp, [(gp, None)], "adamw_replicated", tr=88)]

    gcw = lax.dynamic_slice_in_dim(g_list[n_rep], me * 384, 384, axis=1).reshape(1, CONV_K, 384)
    gcb = lax.dynamic_slice_in_dim(g_list[n_rep + 1], me * 384, 384, axis=1)
    ggn = lax.dynamic_slice_in_dim(g_list[n_rep + 2], me * 256, 256, axis=1)
    sh_w = [b_conv_w, b_conv_b, b_gnorm]
    sh_m = [m_b_conv_w, m_b_conv_b, m_b_gnorm]
    sh_v = [v_b_conv_w, v_b_conv_b, v_b_gnorm]
    swp, sw_layout = _pack(sh_w)
    smp, _ = _pack(sh_m)
    svp, _ = _pack(sh_v)
    sgp, _ = _pack([gcw, gcb, ggn])
    sh_res = [_unpack(o, sw_layout) for o in _adamw(swp, smp, svp, [(sgp, None)], "adamw_sharded_small", tr=8)]

    names = ["norm_mix", "norm_ffn", "mem_norm", "w_kv", "w_out", "w_ffn1", "w_ffn2", "a_in", "a_ln_g", "a_ln_b", "a_ws",
             "a_bs", "b_in", "b_conv_w", "b_conv_b", "b_dt_bias", "b_a_log", "b_d", "b_gnorm", "final_norm"]
    big = {"w_kv": r_kv, "w_out": r_out, "w_ffn1": r_f1, "w_ffn2": r_f2, "a_in": r_a, "b_in": r_b}
    sh_names = ["b_conv_w", "b_conv_b", "b_gnorm"]
    outs = [loss, grad_x.reshape(x.shape)]
    for kind in range(4):
        for nm in names:
            if nm in big:
                outs.append(big[nm][kind])
            elif nm in sh_names:
                outs.append(sh_res[kind][sh_names.index(nm)])
            else:
                outs.append(rep_res[kind][rep_names.index(nm)])
    return tuple(outs)
```

```python
import functools
import math

import jax
import jax.numpy as jnp
from jax import lax
from jax.experimental import pallas as pl
from jax.experimental.pallas import tpu as pltpu
from jax.experimental.pallas import tpu_sc as plsc

F32 = jnp.float32
_MXU = jnp.bfloat16
_ACT = jnp.bfloat16
_HI = lax.Precision.HIGHEST

D_MODEL = 1024
CHUNK = 128
N_MEM = 256
D_INNER = 2048
A_GROUPS = 8
A_GW = D_INNER // A_GROUPS
SSM_HEADS = 32
SSM_P = 64
SSM_GROUPS = 4
SSM_GW = D_INNER // SSM_GROUPS
SSM_N = 128
CONV_K = 4
CONV_DIM = 3072
X_HEADS = 4
X_HD = 256
X_WIDTH = 1024
D_FF = 4096
EPS = 1e-6
HPAD = 128
N_DEV = 8

ADAM_LR = 0.001
ADAM_B1 = 0.9
ADAM_B2 = 0.999
ADAM_EPS = 1e-08
ADAM_WD = 0.01
ADAM_STEP = 10

VMEM_BIG = 56 * 1024 * 1024
MESH = pl.DeviceIdType.MESH


def _cp(vmem=None):
    if vmem is None:
        return pltpu.CompilerParams()
    return pltpu.CompilerParams(vmem_limit_bytes=vmem)


def _dot(a, b, dims=((1,), (0,))):
    return lax.dot_general(a.astype(_MXU), b.astype(_MXU), (dims, ((), ())), preferred_element_type=F32)


def _dot_nt(a, b):
    return _dot(a, b, ((1,), (1,)))


def _dot_tn(a, b):
    return _dot(a, b, ((0,), (0,)))


def _dot_hi(a, b, dims=((1,), (0,))):
    return lax.dot_general(a.astype(F32), b.astype(F32), (dims, ((), ())), precision=_HI, preferred_element_type=F32)


def _sigmoid(x):
    return 1.0 / (1.0 + jnp.exp(-x))


def _gelu(x):
    return 0.5 * x * (1.0 + lax.erf(x * (1.0 / math.sqrt(2.0))))


def _gelu_grad(x):
    return 0.5 * (1.0 + lax.erf(x * (1.0 / math.sqrt(2.0)))) + x * jnp.exp(-0.5 * x * x) * (1.0 / math.sqrt(2.0 * math.pi))


def _softplus(x):
    return jnp.maximum(x, 0.0) + jnp.log1p(jnp.exp(-jnp.abs(x)))


def _iota(shape, dim):
    return lax.broadcasted_iota(jnp.int32, shape, dim)


MM_VMEM_BUDGET = 40 * 1024 * 1024
HBM_BYTES_PER_S = 2.5e12
GRID_STEP_S = 0.35e-6
VMEM_ACC_BYTES_PER_S = 6e12


def _divisors(dim, unit):
    out = [d for d in range(unit, min(dim, 2048) + 1, unit) if dim % d == 0]
    return out if out else [dim]


def _mm_tiles(m, n, k, sa, sb, s_mn, a_pro, offsets):
    best = None
    (a_r0, a_c0, ta), (b_r0, b_c0, tb), (o_r0, o_c0) = offsets
    for tm in _divisors(m, 128):
        for tn in _divisors(n, 128):
            for tk in [k // d for d in (1, 2, 3, 4, 6, 8) if k % d == 0 and (k // d) % 128 == 0]:
                a_t = (tk, tm) if ta else (tm, tk)
                b_t = (tn, tk) if tb else (tk, tn)
                if a_r0 % a_t[0] or a_c0 % a_t[1] or b_r0 % b_t[0] or b_c0 % b_t[1] or o_r0 % tm or o_c0 % tn:
                    continue
                nk = k // tk
                vmem = 2 * (tm * tk * sa + tk * tn * sb + tm * tn * s_mn) + tm * tn * 4 * (2 if nk > 1 else 1)
                if a_pro or sa == 4:
                    vmem += tm * tk * 6
                if sb == 4:
                    vmem += tk * tn * 2
                if vmem > MM_VMEM_BUDGET:
                    continue
                gi, gj = m // tm, n // tn
                for j_inner in (True, False):
                    if nk > 1:
                        traffic = gj * m * k * sa + gi * k * n * sb
                    elif j_inner:
                        traffic = m * k * sa + gi * k * n * sb
                    else:
                        traffic = gj * m * k * sa + k * n * sb
                    traffic += m * n * s_mn + (tm * tk * sa + tk * tn * sb)
                    cost = traffic / HBM_BYTES_PER_S + gi * gj * nk * GRID_STEP_S
                    if nk > 1:
                        cost += m * n * 8 * nk / VMEM_ACC_BYTES_PER_S
                    if best is None or cost < best[0]:
                        best = (cost, tm, tn, tk, j_inner)
    assert best is not None, (m, n, k)
    return best[1:]


def _mm(a, b, *, m, n, k, name, ta=False, tb=False, a_at=(None, 0, 0), b_at=(None, 0, 0),
        out_dtype=F32, add=None, epi_p=None, epi_at=(None, 0, 0), out=None, out_at=(None, 0, 0),
        out_full=None, a_pro=None, after=()):
    s_mn =jnp.dtype(out.dtype if out is not None else out_dtype).itemsize
    s_mn += add.dtype.itemsize if add is not None else 0
    s_mn += epi_p.dtype.itemsize if epi_p is not None else 0
    tm, tn, tk, j_inner = _mm_tiles(m, n, k, a.dtype.itemsize, b.dtype.itemsize, s_mn, a_pro is not None,
                                    ((a_at[1], a_at[2], ta), (b_at[1], b_at[2], tb), (out_at[1], out_at[2])))
    nk = k // tk

    def spec(at, tr, tc, rsel, csel):
        lead, r0, c0 = at
        assert r0 % tr == 0 and c0 % tc == 0, (name, at, tr, tc)
        rb, cb = r0 // tr, c0 // tc
        if lead is None:
            return pl.BlockSpec((tr, tc), lambda g0, g1, kk: (rb + rsel(g0, g1, kk), cb + csel(g0, g1, kk)))
        return pl.BlockSpec((None, tr, tc), lambda g0, g1, kk: (lead, rb + rsel(g0, g1, kk), cb + csel(g0, g1, kk)))

    gi = (lambda g0, g1, kk: g0) if j_inner else (lambda g0, g1, kk: g1)
    gj = (lambda g0, g1, kk: g1) if j_inner else (lambda g0, g1, kk: g0)
    gk = lambda g0, g1, kk: kk
    a_spec = spec(a_at, tk, tm, gk, gi) if ta else spec(a_at, tm, tk, gi, gk)
    b_spec = spec(b_at, tn, tk, gj, gk) if tb else spec(b_at, tk, tn, gk, gj)
    dims = ((0,), (0,)) if ta else (((1,), (1,)) if tb else ((1,), (0,)))
    assert not (ta and tb)

    operands, in_specs = [a, b], [a_spec, b_spec]
    if add is not None:
        operands.append(add)
        in_specs.append(spec((None, 0, 0), tm, tn, gi, gj))
    if epi_p is not None:
        operands.append(epi_p)
        in_specs.append(spec(epi_at, tm, tn, gi, gj))
    aliases = {}
    if out is not None:
        aliases = {len(operands): 0}
        operands.append(out)
        in_specs.append(pl.BlockSpec(memory_space=pl.ANY))
        out_struct = jax.ShapeDtypeStruct(out.shape, out.dtype)
        out_dtype = out.dtype
    else:
        out_struct = jax.ShapeDtypeStruct(out_full if out_full is not None else (m, n), out_dtype)
    has_add, has_epi = add is not None, epi_p is not None
    n_skip = (1 if out is not None else 0) + len(after)
    operands += list(after)
    in_specs += [pl.BlockSpec(memory_space=pl.ANY)] * len(after)

    def body(*refs):
        a_ref, b_ref = refs[0], refs[1]
        pos = 2
        add_ref = epi_ref = None
        if has_add:
            add_ref = refs[pos]
            pos += 1
        if has_epi:
            epi_ref = refs[pos]
            pos += 1
        pos += n_skip
        o_ref = refs[pos]

        def finish(r):
            if has_add:
                r = r + add_ref[...].astype(F32)
            if has_epi:
                r = r * (2.0 * jnp.maximum(epi_ref[...].astype(F32), 0.0))
            o_ref[...] = r.astype(o_ref.dtype)

        av = a_ref[...]
        if a_pro == "relu2":
            av = jnp.square(jnp.maximum(av.astype(F32), 0.0))
        part = _dot(av, b_ref[...], dims)
        if nk == 1:
            finish(part)
        else:
            acc_ref = refs[pos + 1]
            kk = pl.program_id(2)

            @pl.when(kk == 0)
            def _():
                acc_ref[...] = part

            @pl.when(kk > 0)
            def _():
                acc_ref[...] += part

            @pl.when(kk == nk - 1)
            def _():
                finish(acc_ref[...])

    grid = (m // tm, n // tn, nk) if j_inner else (n // tn, m // tm, nk)
    return pl.pallas_call(
        body, name=name, grid=grid, in_specs=in_specs,
        out_specs=spec(out_at, tm, tn, gi, gj), out_shape=out_struct,
        scratch_shapes=[pltpu.VMEM((tm, tn), F32)] if nk > 1 else [], input_output_aliases=aliases,
        compiler_params=_cp(VMEM_BIG))(*operands)


def _rms_fwd(x, g, name, tm=256):
    s, d = x.shape
    tm = min(tm, s)

    def body(x_ref, g_ref, o_ref):
        xv = x_ref[...]
        r = lax.rsqrt(jnp.mean(xv * xv, axis=-1, keepdims=True) + EPS)
        o_ref[...] = (xv * r * g_ref[...]).astype(o_ref.dtype)

    return pl.pallas_call(
        body, name=name, grid=(s // tm,),
        in_specs=[pl.BlockSpec((tm, d), lambda i: (i, 0)), pl.BlockSpec((1, d), lambda i: (0, 0))],
        out_specs=pl.BlockSpec((tm, d), lambda i: (i, 0)),
        out_shape=jax.ShapeDtypeStruct((s, d), _ACT))(x, g)


def _rms_bwd(x, g, dy, dres, name, tm=256):
    s, d = x.shape
    tm = min(tm, s)
    has_res = dres is not None

    def body(*refs):
        if has_res:
            x_ref, g_ref, dy_ref, dres_ref, dx_ref, dxa_ref, dg_ref = refs
        else:
            x_ref, g_ref, dy_ref, dx_ref, dxa_ref, dg_ref = refs

        @pl.when(pl.program_id(0) == 0)
        def _():
            dg_ref[...] = jnp.zeros_like(dg_ref)

        xv = x_ref[...]
        dyv = dy_ref[...].astype(F32)
        r = lax.rsqrt(jnp.mean(xv * xv, axis=-1, keepdims=True) + EPS)
        xh = xv * r
        dyg = dyv * g_ref[...]
        dx = r * (dyg - xh * jnp.mean(dyg * xh, axis=-1, keepdims=True))
        if has_res:
            dx = dx + dres_ref[...]
        dx_ref[...] = dx
        dxa_ref[...] = dx.astype(dxa_ref.dtype)
        dg_ref[...] += jnp.sum(dyv * xh, axis=0, keepdims=True)

    row = pl.BlockSpec((tm, d), lambda i: (i, 0))
    vec = pl.BlockSpec((1, d), lambda i: (0, 0))
    in_specs = [row, vec, row] + ([row] if has_res else [])
    operands = [x, g, dy] + ([dres] if has_res else [])
    return pl.pallas_call(
        body, name=name, grid=(s // tm,), in_specs=in_specs, out_specs=[row, row, vec],
        out_shape=[jax.ShapeDtypeStruct((s, d), F32), jax.ShapeDtypeStruct((s, d), _ACT),
                   jax.ShapeDtypeStruct((1, d), F32)])(*operands)


def _loss_head(h, g, target, name, tm=256):
    s, d = h.shape
    tm = min(tm, s)

    def body(h_ref, g_ref, t_ref, loss_ref, dh_ref, dha_ref, dg_ref):
        @pl.when(pl.program_id(0) == 0)
        def _():
            dg_ref[...] = jnp.zeros_like(dg_ref)
            loss_ref[...] = jnp.zeros_like(loss_ref)

        xv = h_ref[...]
        r = lax.rsqrt(jnp.mean(xv * xv, axis=-1, keepdims=True) + EPS)
        xh = xv * r
        err = xh * g_ref[...] - t_ref[...]
        loss_ref[...] += jnp.full(loss_ref.shape, 0.5 * jnp.sum(jnp.mean(err * err, axis=-1, keepdims=True)), F32)
        dyv = err * (1.0 / d)
        dyg = dyv * g_ref[...]
        dh = r * (dyg - xh * jnp.mean(dyg * xh, axis=-1, keepdims=True))
        dh_ref[...] = dh
        dha_ref[...] = dh.astype(dha_ref.dtype)
        dg_ref[...] += jnp.sum(dyv * xh, axis=0, keepdims=True)

    row = pl.BlockSpec((tm, d), lambda i: (i, 0))
    vec = pl.BlockSpec((1, d), lambda i: (0, 0))
    return pl.pallas_call(
        body, name=name, grid=(s // tm,), in_specs=[row, vec, row],
        out_specs=[pl.BlockSpec((1, 128), lambda i: (0, 0)), row, row, vec],
        out_shape=[jax.ShapeDtypeStruct((1, 128), F32), jax.ShapeDtypeStruct((s, d), F32),
                   jax.ShapeDtypeStruct((s, d), _ACT), jax.ShapeDtypeStruct((1, d), F32)])(h, g, target)


def _gmlp_parts(pu, pv, lng, lnb):
    u = _gelu(pu)
    v = _gelu(pv)
    mu = jnp.mean(v, axis=-1, keepdims=True)
    vc = v - mu
    rstd = lax.rsqrt(jnp.mean(vc * vc, axis=-1, keepdims=True) + EPS)
    xhat = vc * rstd
    vn = xhat * lng + lnb
    return u, xhat, rstd, vn


def _gmlp_fwd(proj, lng, lnb, ws, bs3, name):
    s = proj.shape[0]

    def body(pu_ref, pv_ref, lng_ref, lnb_ref, ws_ref, bs_ref, o_ref):
        u, _, _, vn = _gmlp_parts(pu_ref[...], pv_ref[...], lng_ref[...], lnb_ref[...])
        causal = _iota((CHUNK, CHUNK), 0) >= _iota((CHUNK, CHUNK), 1)
        for g in range(A_GROUPS):
            sl = slice(g * A_GW, (g + 1) * A_GW)
            w = jnp.where(causal, ws_ref[g], 0.0)
            sv = _dot(w, vn[:, sl]) + bs_ref[g]
            o_ref[:, sl] = (u[:, sl] * sv).astype(o_ref.dtype)

    full = lambda shape: pl.BlockSpec(shape, lambda c: (0,) * len(shape))
    return pl.pallas_call(
        body, name=name, grid=(s // CHUNK,),
        in_specs=[pl.BlockSpec((CHUNK, D_INNER), lambda c: (c, 0)), pl.BlockSpec((CHUNK, D_INNER), lambda c: (c, 1)),
                  full((1, D_INNER)), full((1, D_INNER)), full((A_GROUPS, CHUNK, CHUNK)), full((A_GROUPS, CHUNK, 1))],
        out_specs=pl.BlockSpec((CHUNK, D_INNER), lambda c: (c, 0)),
        out_shape=jax.ShapeDtypeStruct((s, D_INNER + X_WIDTH), _ACT), compiler_params=_cp(VMEM_BIG))(proj, proj, lng, lnb, ws, bs3)


def _gmlp_bwd(proj, dcat, lng, lnb, ws, bs3, name):
    s = proj.shape[0]

    def body(pu_ref, pv_ref, dm_ref, lng_ref, lnb_ref, ws_ref, bs_ref, dp_ref, dws_ref, dbs_ref, dlng_ref, dlnb_ref, dvn_ref):
        @pl.when(pl.program_id(0) == 0)
        def _():
            dws_ref[...] = jnp.zeros_like(dws_ref)
            dbs_ref[...] = jnp.zeros_like(dbs_ref)
            dlng_ref[...] = jnp.zeros_like(dlng_ref)
            dlnb_ref[...] = jnp.zeros_like(dlnb_ref)

        pu, pv = pu_ref[...], pv_ref[...]
        lng = lng_ref[...]
        u, xhat, rstd, vn = _gmlp_parts(pu, pv, lng, lnb_ref[...])
        dm = dm_ref[...].astype(F32)
        causal = _iota((CHUNK, CHUNK), 0) >= _iota((CHUNK, CHUNK), 1)
        for g in range(A_GROUPS):
            sl = slice(g * A_GW, (g + 1) * A_GW)
            w = jnp.where(causal, ws_ref[g], 0.0)
            sv = _dot(w, vn[:, sl]) + bs_ref[g]
            dsv = dm[:, sl] * u[:, sl]
            dp_ref[:, sl] = (dm[:, sl] * sv * _gelu_grad(pu[:, sl])).astype(dp_ref.dtype)
            dvn_ref[:, sl] = _dot_tn(w, dsv)
            dws_ref[g] += jnp.where(causal, _dot_nt(dsv, vn[:, sl]), 0.0)
            dbs_ref[g] += jnp.sum(dsv, axis=-1, keepdims=True)
        dvn = dvn_ref[...]
        dlng_ref[...] += jnp.sum(dvn * xhat, axis=0, keepdims=True)
        dlnb_ref[...] += jnp.sum(dvn, axis=0, keepdims=True)
        dxh = dvn * lng
        dv = rstd * (dxh - jnp.mean(dxh, axis=-1, keepdims=True) - xhat * jnp.mean(dxh * xhat, axis=-1, keepdims=True))
        dp_ref[:, D_INNER:] = (dv * _gelu_grad(pv)).astype(dp_ref.dtype)

    full = lambda shape: pl.BlockSpec(shape, lambda c: (0,) * len(shape))
    return pl.pallas_call(
        body, name=name, grid=(s // CHUNK,),
        in_specs=[pl.BlockSpec((CHUNK, D_INNER), lambda c: (c, 0)), pl.BlockSpec((CHUNK, D_INNER), lambda c: (c, 1)),
                  pl.BlockSpec((CHUNK, D_INNER), lambda c: (c, 0)),
                  full((1, D_INNER)), full((1, D_INNER)), full((A_GROUPS, CHUNK, CHUNK)), full((A_GROUPS, CHUNK, 1))],
        out_specs=[pl.BlockSpec((CHUNK, 2 * D_INNER), lambda c: (c, 0)), full((A_GROUPS, CHUNK, CHUNK)),
                   full((A_GROUPS, CHUNK, 1)), full((1, D_INNER)), full((1, D_INNER))],
        out_shape=[jax.ShapeDtypeStruct((s, 2 * D_INNER + X_WIDTH), _ACT), jax.ShapeDtypeStruct((A_GROUPS, CHUNK, CHUNK), F32),
                   jax.ShapeDtypeStruct((A_GROUPS, CHUNK, 1), F32), jax.ShapeDtypeStruct((1, D_INNER), F32),
                   jax.ShapeDtypeStruct((1, D_INNER), F32)],
        scratch_shapes=[pltpu.VMEM((CHUNK, D_INNER), F32)],
        compiler_params=_cp(VMEM_BIG))(proj, proj, dcat, lng, lnb, ws, bs3)


_X_SCALE = 1.0 / math.sqrt(X_HD)


def _attn_fwd(proj, qblk, kv, cat, name, tm=256):
    s = proj.shape[0]
    tm = min(tm, s)

    def body(q_ref, kv_ref, cat_ref, o_ref):
        for h in range(X_HEADS):
            sl = slice(h * X_HD, (h + 1) * X_HD)
            k = kv_ref[:, sl]
            v = kv_ref[:, X_WIDTH + h * X_HD:X_WIDTH + (h + 1) * X_HD]
            sc = _dot_nt(q_ref[:, sl], k) * _X_SCALE
            e = jnp.exp(sc - jnp.max(sc, axis=-1, keepdims=True))
            p = e / jnp.sum(e, axis=-1, keepdims=True)
            o_ref[:, sl] = _dot(p, v).astype(o_ref.dtype)

    return pl.pallas_call(
        body, name=name, grid=(s // tm,),
        in_specs=[pl.BlockSpec((tm, X_WIDTH), lambda i: (i, qblk)), pl.BlockSpec((N_MEM, 2 * X_WIDTH), lambda i: (0, 0)),
                  pl.BlockSpec(memory_space=pl.ANY)],
        out_specs=pl.BlockSpec((tm, X_WIDTH), lambda i: (i, D_INNER // X_WIDTH)),
        out_shape=jax.ShapeDtypeStruct(cat.shape, cat.dtype), input_output_aliases={2: 0})(proj, kv, cat)


def _attn_bwd(proj, qblk, kv, dcat, dproj, name, tm=256):
    s = proj.shape[0]
    tm = min(tm, s)

    def body(q_ref, kv_ref, do_ref, dproj_ref, dq_ref, dkv_ref):
        @pl.when(pl.program_id(0) == 0)
        def _():
            dkv_ref[...] = jnp.zeros_like(dkv_ref)

        for h in range(X_HEADS):
            sl = slice(h * X_HD, (h + 1) * X_HD)
            slv = slice(X_WIDTH + h * X_HD, X_WIDTH + (h + 1) * X_HD)
            q = q_ref[:, sl]
            k = kv_ref[:, sl]
            v = kv_ref[:, slv]
            do = do_ref[:, sl].astype(F32)
            sc = _dot_nt(q, k) * _X_SCALE
            e = jnp.exp(sc - jnp.max(sc, axis=-1, keepdims=True))
            p = e / jnp.sum(e, axis=-1, keepdims=True)
            dp = _dot_nt(do, v)
            ds = p * (dp - jnp.sum(dp * p, axis=-1, keepdims=True)) * _X_SCALE
            dq_ref[:, sl] = _dot(ds, k).astype(dq_ref.dtype)
            dkv_ref[:, sl] += _dot_tn(ds, q)
            dkv_ref[:, slv] += _dot_tn(p, do)

    return pl.pallas_call(
        body, name=name, grid=(s // tm,),
        in_specs=[pl.BlockSpec((tm, X_WIDTH), lambda i: (i, qblk)), pl.BlockSpec((N_MEM, 2 * X_WIDTH), lambda i: (0, 0)),
                  pl.BlockSpec((tm, X_WIDTH), lambda i: (i, 2)), pl.BlockSpec(memory_space=pl.ANY)],
        out_specs=[pl.BlockSpec((tm, X_WIDTH), lambda i: (i, qblk)), pl.BlockSpec((N_MEM, 2 * X_WIDTH), lambda i: (0, 0))],
        out_shape=[jax.ShapeDtypeStruct(dproj.shape, dproj.dtype), jax.ShapeDtypeStruct((N_MEM, 2 * X_WIDTH), F32)],
        input_output_aliases={3: 0})(proj, kv, dcat, dproj)


CONV_TC = 256
_XBC_BLK0 = D_INNER // CONV_TC


def _shift_down(x, j):
    if j == 0:
        return x
    return jnp.where(_iota(x.shape, 0) >= j, pltpu.roll(x, j, 0), 0.0)


def _shift_up(x, j):
    if j == 0:
        return x
    n = x.shape[0]
    return jnp.where(_iota(x.shape, 0) < n - j, pltpu.roll(x, n - j, 0), 0.0)


def _conv_fwd(proj, w, b, name):
    s = proj.shape[0]

    def body(x_ref, w_ref, b_ref, o_ref):
        xv = x_ref[...]
        pre = b_ref[...] + w_ref[CONV_K - 1:CONV_K, :] * xv
        for kk in range(CONV_K - 1):
            pre = pre + w_ref[kk:kk + 1, :] * _shift_down(xv, CONV_K - 1 - kk)
        o_ref[...] = pre * _sigmoid(pre)

    return pl.pallas_call(
        body, name=name, grid=(CONV_DIM // CONV_TC,),
        in_specs=[pl.BlockSpec((s, CONV_TC), lambda j: (0, _XBC_BLK0 + j)), pl.BlockSpec((CONV_K, CONV_TC), lambda j: (0, j)),
                  pl.BlockSpec((1, CONV_TC), lambda j: (0, j))],
        out_specs=pl.BlockSpec((s, CONV_TC), lambda j: (0, j)),
        out_shape=jax.ShapeDtypeStruct((s, CONV_DIM), F32), compiler_params=_cp(VMEM_BIG))(proj, w, b)


def _conv_bwd(proj, w, b, dxbc, dproj, name):
    s = proj.shape[0]

    def body(x_ref, w_ref, b_ref, d_ref, dproj_ref, dx_ref, dw_ref, db_ref):
        xv = x_ref[...]
        pre = b_ref[...] + w_ref[CONV_K - 1:CONV_K, :] * xv
        for kk in range(CONV_K - 1):
            pre = pre + w_ref[kk:kk + 1, :] * _shift_down(xv, CONV_K - 1 - kk)
        sig = _sigmoid(pre)
        dpre = d_ref[...] * (sig * (1.0 + pre * (1.0 - sig)))
        dx = w_ref[CONV_K - 1:CONV_K, :] * dpre
        dw_ref[CONV_K - 1:CONV_K, :] = jnp.sum(dpre * xv, axis=0, keepdims=True)
        for kk in range(CONV_K - 1):
            j = CONV_K - 1 - kk
            dx = dx + w_ref[kk:kk + 1, :] * _shift_up(dpre, j)
            dw_ref[kk:kk + 1, :] = jnp.sum(dpre * _shift_down(xv, j), axis=0, keepdims=True)
        dx_ref[...] = dx.astype(dx_ref.dtype)
        db_ref[...] = jnp.sum(dpre, axis=0, keepdims=True)

    return pl.pallas_call(
        body, name=name, grid=(CONV_DIM // CONV_TC,),
        in_specs=[pl.BlockSpec((s, CONV_TC), lambda j: (0, _XBC_BLK0 + j)), pl.BlockSpec((CONV_K, CONV_TC), lambda j: (0, j)),
                  pl.BlockSpec((1, CONV_TC), lambda j: (0, j)), pl.BlockSpec((s, CONV_TC), lambda j: (0, j)),
                  pl.BlockSpec(memory_space=pl.ANY)],
        out_specs=[pl.BlockSpec((s, CONV_TC), lambda j: (0, _XBC_BLK0 + j)), pl.BlockSpec((CONV_K, CONV_TC), lambda j: (0, j)),
                   pl.BlockSpec((1, CONV_TC), lambda j: (0, j))],
        out_shape=[jax.ShapeDtypeStruct(dproj.shape, dproj.dtype), jax.ShapeDtypeStruct((CONV_K, CONV_DIM), F32),
                   jax.ShapeDtypeStruct((1, CONV_DIM), F32)], input_output_aliases={4: 0},
        compiler_params=_cp(VMEM_BIG))(proj, w, b, dxbc, dproj)


def _ssd_common(dtc_ref, dtr_ref, br_ref, bc_ref, ar_ref, ac_ref, csb_ref, cst_ref, csf_ref):
    a_row = -jnp.exp(ar_ref[...])
    dt_c = _softplus(dtc_ref[...] + br_ref[...])
    a_col = -jnp.exp(ac_ref[...])
    dt_r = _softplus(dtr_ref[...] + bc_ref[...])
    row = _iota((CHUNK, CHUNK), 0)
    col = _iota((CHUNK, CHUNK), 1)
    tril = (row >= col).astype(F32)
    triu = (row <= col).astype(F32)
    cs = _dot_hi(tril, dt_c * a_row)
    cst_ref[...] = _dot_hi(dt_r * a_col, triu)
    e64 = (jnp.right_shift(_iota((HPAD, D_INNER), 1), 6) == _iota((HPAD, D_INNER), 0)).astype(F32)
    e128 = (jnp.right_shift(_iota((HPAD, SSM_HEADS * CHUNK), 1), 7) == _iota((HPAD, SSM_HEADS * CHUNK), 0)).astype(F32)
    csb_ref[...] = _dot_hi(cs, e128)
    dt_full = _dot_hi(dt_c, e64)
    csf_ref[...] = _dot_hi(cs, e64)
    cs_full = csf_ref[...]
    cs_last = csf_ref[CHUNK - 1:CHUNK, :]
    e_full = jnp.exp(cs_full)
    f_full = jnp.exp(cs_last - cs_full)
    gamma = jnp.exp(cs_last)
    return a_row, dt_c, cs, dt_full, e_full, f_full, gamma, e64


def _ssd_lambda(csb_ref, cst_ref, h, causal):
    diff = csb_ref[:, h * CHUNK:(h + 1) * CHUNK] - cst_ref[h:h + 1, :]
    return jnp.exp(jnp.where(causal, diff, -1e30))


_SSD_VEC_SPECS = lambda: [pl.BlockSpec((1, HPAD), lambda c: (0, 0)), pl.BlockSpec((HPAD, 1), lambda c: (0, 0)),
                          pl.BlockSpec((1, HPAD), lambda c: (0, 0)), pl.BlockSpec((HPAD, 1), lambda c: (0, 0)),
                          pl.BlockSpec((1, D_INNER), lambda c: (0, 0))]


def _ssd_fwd(xbc, dtc, dtr, bias_row, bias_col, alog_row, alog_col, dfull, name):
    s = xbc.shape[0]
    nc = s // CHUNK

    def body(xbc_ref, dtc_ref, dtr_ref, br_ref, bc_ref, ar_ref, ac_ref, df_ref, y_ref, st_ref,
             ht_ref, csb_ref, cst_ref, csf_ref):
        @pl.when(pl.program_id(0) == 0)
        def _():
            ht_ref[...] = jnp.zeros_like(ht_ref)

        _, _, _, dt_full, e_full, f_full, gamma, _ = _ssd_common(
            dtc_ref, dtr_ref, br_ref, bc_ref, ar_ref, ac_ref, csb_ref, cst_ref, csf_ref)
        x = xbc_ref[:, :D_INNER]
        xdt = x * dt_full
        st_ref[...] = ht_ref[...]
        causal = _iota((CHUNK, CHUNK), 0) >= _iota((CHUNK, CHUNK), 1)
        lo = _iota((CHUNK, CHUNK), 1) < SSM_P
        for g in range(SSM_GROUPS):
            gs = slice(g * SSM_GW, (g + 1) * SSM_GW)
            bg = xbc_ref[:, D_INNER + g * SSM_N:D_INNER + (g + 1) * SSM_N]
            cg = xbc_ref[:, D_INNER + SSM_GROUPS * SSM_N + g * SSM_N:D_INNER + SSM_GROUPS * SSM_N + (g + 1) * SSM_N]
            ht = ht_ref[:, gs]
            cb = _dot_nt(cg, bg)
            yoff = e_full[:, gs] * _dot(cg, ht)
            for jp in range(SSM_GW // CHUNK):
                j = g * (SSM_GW // CHUNK) + jp
                ps = slice(j * CHUNK, (j + 1) * CHUNK)
                x2 = xdt[:, ps]
                y0 = _dot(cb * _ssd_lambda(csb_ref, cst_ref, 2 * j, causal), x2)
                y1 = _dot(cb * _ssd_lambda(csb_ref, cst_ref, 2 * j + 1, causal), x2)
                y_ref[:, ps] = (jnp.where(lo, y0, y1) + yoff[:, jp * CHUNK:(jp + 1) * CHUNK]
                                + x[:, ps] * df_ref[:, ps])
            ht_ref[:, gs] = gamma[:, gs] * ht + _dot_tn(bg, xdt[:, gs] * f_full[:, gs])

    return pl.pallas_call(
        body, name=name, grid=(nc,),
        in_specs=[pl.BlockSpec((CHUNK, CONV_DIM), lambda c: (c, 0)), pl.BlockSpec((CHUNK, HPAD), lambda c: (c, 0)),
                  pl.BlockSpec((HPAD, CHUNK), lambda c: (0, c))] + _SSD_VEC_SPECS(),
        out_specs=[pl.BlockSpec((CHUNK, D_INNER), lambda c: (c, 0)), pl.BlockSpec((None, SSM_N, D_INNER), lambda c: (c, 0, 0))],
        out_shape=[jax.ShapeDtypeStruct((s, D_INNER), F32), jax.ShapeDtypeStruct((nc, SSM_N, D_INNER), F32)],
        scratch_shapes=[pltpu.VMEM((SSM_N, D_INNER), F32), pltpu.VMEM((CHUNK, SSM_HEADS * CHUNK), F32),
                        pltpu.VMEM((HPAD, CHUNK), F32), pltpu.VMEM((CHUNK, D_INNER), F32)],
        compiler_params=_cp(VMEM_BIG))(xbc, dtc, dtr, bias_row, bias_col, alog_row, alog_col, dfull)


def _ssd_bwd(xbc, dtc, dtr, bias_row, bias_col, alog_row, alog_col, dfull, dy, states, name):
    s = xbc.shape[0]
    nc = s // CHUNK
    rev = lambda c: nc - 1 - c

    def body(xbc_ref, dtc_ref, dtr_ref, br_ref, bc_ref, ar_ref, ac_ref, df_ref, dy_ref, st_ref,
             dxbc_ref, ddt_ref, dalog_ref, dd_ref, dbias_ref,
             dht_ref, csb_ref, cst_ref, csf_ref, ddf_ref, dxs_ref, dcsf_ref, dcsl_ref):
        step = pl.program_id(0)

        @pl.when(step == 0)
        def _():
            dht_ref[...] = jnp.zeros_like(dht_ref)
            ddf_ref[...] = jnp.zeros_like(ddf_ref)
            dalog_ref[...] = jnp.zeros_like(dalog_ref)
            dbias_ref[...] = jnp.zeros_like(dbias_ref)
            dd_ref[...] = jnp.zeros_like(dd_ref)

        a_row, dt_c, _, dt_full, e_full, f_full, gamma, e64 = _ssd_common(
            dtc_ref, dtr_ref, br_ref, bc_ref, ar_ref, ac_ref, csb_ref, cst_ref, csf_ref)
        x = xbc_ref[:, :D_INNER]
        xdt = x * dt_full
        dy_all = dy_ref[...]
        ddf_ref[...] += jnp.broadcast_to(jnp.sum(dy_all * x, axis=0, keepdims=True), ddf_ref.shape)
        causal = _iota((CHUNK, CHUNK), 0) >= _iota((CHUNK, CHUNK), 1)
        lo = _iota((CHUNK, CHUNK), 1) < SSM_P
        ones = jnp.ones((CHUNK, HPAD), F32)
        head_lane = _iota((CHUNK, HPAD), 1)
        dcs_heads = jnp.zeros((CHUNK, HPAD), F32)
        for g in range(SSM_GROUPS):
            gs = slice(g * SSM_GW, (g + 1) * SSM_GW)
            b0 = D_INNER + g * SSM_N
            c0 = D_INNER + SSM_GROUPS * SSM_N + g * SSM_N
            bg = xbc_ref[:, b0:b0 + SSM_N]
            cg = xbc_ref[:, c0:c0 + SSM_N]
            ht = st_ref[:, gs]
            dht = dht_ref[:, gs]
            dyg = dy_all[:, gs]
            eg, fg, gg = e_full[:, gs], f_full[:, gs], gamma[:, gs]
            z = _dot(cg, ht)
            dz = dyg * eg
            dcg = _dot_nt(dz, ht)
            dht_new = _dot_tn(cg, dz) + gg * dht
            xf = xdt[:, gs] * fg
            dxf = _dot(bg, dht)
            dbg = _dot_nt(xf, dht)
            dff = dxf * xf
            dcsf_ref[:, gs] = dyg * eg * z - dff
            dcsl_ref[:, gs] = jnp.broadcast_to(
                jnp.sum(dff, axis=0, keepdims=True) + jnp.sum(dht * ht, axis=0, keepdims=True) * gg, (8, SSM_GW))
            cb = _dot_nt(cg, bg)
            dcb = jnp.zeros((CHUNK, CHUNK), F32)
            for jp in range(SSM_GW // CHUNK):
                j = g * (SSM_GW // CHUNK) + jp
                ps = slice(j * CHUNK, (j + 1) * CHUNK)
                x2 = xdt[:, ps]
                dy2 = dy_all[:, ps]
                dxh = []
                for hh in range(2):
                    h = 2 * j + hh
                    lam = _ssd_lambda(csb_ref, cst_ref, h, causal)
                    mh = cb * lam
                    dyh = jnp.where(lo, dy2, 0.0) if hh == 0 else jnp.where(lo, 0.0, dy2)
                    dm = _dot_nt(dyh, x2)
                    dcb = dcb + dm * lam
                    gm = dm * mh
                    rs = jnp.sum(gm, axis=1, keepdims=True)
                    csum = _dot_hi(gm, ones, ((0,), (0,)))
                    dcs_heads = dcs_heads + jnp.where(head_lane == h, rs - csum, 0.0)
                    dxh.append(_dot_tn(mh, dy2))
                dxs_ref[:, ps] = jnp.where(lo, dxh[0], dxh[1]) + dxf[:, jp * CHUNK:(jp + 1) * CHUNK] * fg[:, jp * CHUNK:(jp + 1) * CHUNK]
            dxbc_ref[:, b0:b0 + SSM_N] = (dbg + _dot_tn(dcb, cg)).astype(dxbc_ref.dtype)
            dxbc_ref[:, c0:c0 + SSM_N] = (dcg + _dot(dcb, bg)).astype(dxbc_ref.dtype)
            dht_ref[:, gs] = dht_new
        dxs = dxs_ref[...]
        dcs_heads = dcs_heads + _dot_hi(dcsf_ref[...], e64, ((1,), (1,)))
        dcs_last = _dot_hi(dcsl_ref[...], e64, ((1,), (1,)))
        dcs_heads = dcs_heads + jnp.where(_iota((CHUNK, HPAD), 0) == CHUNK - 1, dcs_last[0:1, :], 0.0)
        triu = (_iota((CHUNK, CHUNK), 0) <= _iota((CHUNK, CHUNK), 1)).astype(F32)
        dda = _dot_hi(triu, dcs_heads)
        ddt = dda * a_row + _dot_hi(dxs * x, e64, ((1,), (1,)))
        dxbc_ref[:, :D_INNER] = (dxs * dt_full + dy_all * df_ref[...]).astype(dxbc_ref.dtype)
        dalog_ref[...] += jnp.sum(dda * dt_c, axis=0, keepdims=True) * a_row
        ddt_raw = ddt * _sigmoid(dtc_ref[...] + br_ref[...])
        ddt_ref[...] = ddt_raw.astype(ddt_ref.dtype)
        dbias_ref[...] += jnp.sum(ddt_raw, axis=0, keepdims=True)

        @pl.when(step == nc - 1)
        def _():
            dd_ref[...] = _dot_hi(ddf_ref[...], e64, ((1,), (1,)))[0:1, :]

    vec = pl.BlockSpec((1, HPAD), lambda c: (0, 0))
    return pl.pallas_call(
        body, name=name, grid=(nc,),
        in_specs=[pl.BlockSpec((CHUNK, CONV_DIM), lambda c: (rev(c), 0)), pl.BlockSpec((CHUNK, HPAD), lambda c: (rev(c), 0)),
                  pl.BlockSpec((HPAD, CHUNK), lambda c: (0, rev(c)))] + _SSD_VEC_SPECS()
                 + [pl.BlockSpec((CHUNK, D_INNER), lambda c: (rev(c), 0)),
                    pl.BlockSpec((None, SSM_N, D_INNER), lambda c: (rev(c), 0, 0))],
        out_specs=[pl.BlockSpec((CHUNK, CONV_DIM), lambda c: (rev(c), 0)), pl.BlockSpec((CHUNK, HPAD), lambda c: (rev(c), 0)),
                   vec, vec, vec],
        out_shape=[jax.ShapeDtypeStruct((s, CONV_DIM), F32), jax.ShapeDtypeStruct((s, HPAD), _ACT),
                   jax.ShapeDtypeStruct((1, HPAD), F32), jax.ShapeDtypeStruct((1, HPAD), F32),
                   jax.ShapeDtypeStruct((1, HPAD), F32)],
        scratch_shapes=[pltpu.VMEM((SSM_N, D_INNER), F32), pltpu.VMEM((CHUNK, SSM_HEADS * CHUNK), F32),
                        pltpu.VMEM((HPAD, CHUNK), F32), pltpu.VMEM((CHUNK, D_INNER), F32),
                        pltpu.VMEM((8, D_INNER), F32), pltpu.VMEM((CHUNK, D_INNER), F32),
                        pltpu.VMEM((CHUNK, D_INNER), F32), pltpu.VMEM((8, D_INNER), F32)],
        compiler_params=_cp(VMEM_BIG))(xbc, dtc, dtr, bias_row, bias_col, alog_row, alog_col, dfull, dy, states)


def _gate_fwd(y, proj, gn, name, tm=256):
    s = y.shape[0]
    tm = min(tm, s)

    def body(y_ref, z_ref, gn_ref, o_ref):
        for g in range(SSM_GROUPS):
            gs = slice(g * SSM_GW, (g + 1) * SSM_GW)
            z = z_ref[:, gs]
            t = y_ref[:, gs] * (z * _sigmoid(z))
            r = lax.rsqrt(jnp.mean(t * t, axis=-1, keepdims=True) + EPS)
            o_ref[:, gs] = (t * r * gn_ref[:, gs]).astype(o_ref.dtype)

    row = pl.BlockSpec((tm, D_INNER), lambda i: (i, 0))
    return pl.pallas_call(
        body, name=name, grid=(s // tm,), in_specs=[row, row, pl.BlockSpec((1, D_INNER), lambda i: (0, 0))],
        out_specs=row, out_shape=jax.ShapeDtypeStruct((s, D_INNER + X_WIDTH), _ACT))(y, proj, gn)


def _gate_bwd(y, proj, gn, dcat, name, tm=256):
    s = y.shape[0]
    tm = min(tm, s)

    def body(y_ref, z_ref, gn_ref, dm_ref, dy_ref, dz_ref, dgn_ref):
        @pl.when(pl.program_id(0) == 0)
        def _():
            dgn_ref[...] = jnp.zeros_like(dgn_ref)

        for g in range(SSM_GROUPS):
            gs = slice(g * SSM_GW, (g + 1) * SSM_GW)
            z = z_ref[:, gs]
            yv = y_ref[:, gs]
            sig = _sigmoid(z)
            sz = z * sig
            t = yv * sz
            r = lax.rsqrt(jnp.mean(t * t, axis=-1, keepdims=True) + EPS)
            th = t * r
            dm = dm_ref[:, gs].astype(F32)
            dmg = dm * gn_ref[:, gs]
            dt_ = r * (dmg - th * jnp.mean(dmg * th, axis=-1, keepdims=True))
            dgn_ref[:, gs] += jnp.sum(dm * th, axis=0, keepdims=True)
            dy_ref[:, gs] = dt_ * sz
            dz_ref[:, gs] = (dt_ * yv * (sig * (1.0 + z * (1.0 - sig)))).astype(dz_ref.dtype)

    row = pl.BlockSpec((tm, D_INNER), lambda i: (i, 0))
    vec = pl.BlockSpec((1, D_INNER), lambda i: (0, 0))
    return pl.pallas_call(
        body, name=name, grid=(s // tm,), in_specs=[row, row, vec, row], out_specs=[row, row, vec],
        out_shape=[jax.ShapeDtypeStruct((s, D_INNER), F32), jax.ShapeDtypeStruct((s, 6 * D_MODEL), _ACT),
                   jax.ShapeDtypeStruct((1, D_INNER), F32)])(y, proj, gn, dcat)


def _block_of(kind, width):
    if kind == "col":
        return lambda ref, j: ref.at[:, :, pl.ds(pl.multiple_of(j * width, 128), width)]
    if kind == "row":
        return lambda ref, j: ref.at[:, pl.ds(pl.multiple_of(j * width, 8), width), :]
    return lambda ref, j: ref.at[j]


def _coords():
    return lax.axis_index("x"), lax.axis_index("y"), lax.axis_index("c")


def _rel_chip(x, y, k):
    return (1 - x if k & 1 else x), (1 - y if k & 2 else y)


_HBM = lambda: pl.BlockSpec(memory_space=pltpu.HBM)


def _all_gather(shards, layouts, name):
    n = len(shards)
    blocks = [_block_of(kind, width) for kind, width, _ in layouts]

    def body(*refs):
        _all_gather_body(refs[:n], refs[n:2 * n], *refs[2 * n:], blocks)

    return pl.pallas_call(
        body, name=name, in_specs=[_HBM()] * n, out_specs=[_HBM()] * n,
        out_shape=[jax.ShapeDtypeStruct(shape, sh.dtype) for sh, (_, _, shape) in zip(shards, layouts)],
        scratch_shapes=[pltpu.SemaphoreType.DMA((n, 7)), pltpu.SemaphoreType.DMA((n, 7)), pltpu.SemaphoreType.DMA((n,))])(*shards)


def _all_gather_body(ins, outs, send_sems, recv_sems, local_sems, blocks):
    n = len(ins)
    x, y, c = _coords()
    sibling = (x, y, 1 - c)

    def copy(t, k, chip, core, to, src=None):
        dst = blocks[t](outs[t], 4 * chip[0] + 2 * chip[1] + core)
        return pltpu.make_async_remote_copy(
            src_ref=dst if src is None else src, dst_ref=dst, send_sem=send_sems.at[t, k],
            recv_sem=recv_sems.at[t, k], device_id=to, device_id_type=MESH)

    started = []
    for t in range(n):
        mine = pltpu.make_async_copy(ins[t], blocks[t](outs[t], 4 * x + 2 * y + c), local_sems.at[t])
        mine.start()
        started.append(mine)
    sends = []
    for t in range(n):
        for k in range(4):
            px, py = _rel_chip(x, y, k)
            cp = copy(t, k, (x, y), c, (px, py, 1 - c if k == 0 else c), src=ins[t])
            cp.start()
            sends.append(cp)
    for t in range(n):
        for k in range(1, 4):
            chip = _rel_chip(x, y, k)
            copy(t, k, chip, c, sibling).wait_recv()
            fwd = copy(t, 3 + k, chip, c, sibling)
            fwd.start()
            sends.append(fwd)
    for t in range(n):
        copy(t, 0, (x, y), 1 - c, sibling).wait_recv()
        for k in range(1, 4):
            copy(t, 3 + k, _rel_chip(x, y, k), 1 - c, sibling).wait_recv()
    for cp in sends:
        cp.wait_send()
    for mine in started:
        mine.wait()


def _handshake(peers):
    barrier = pltpu.get_barrier_semaphore()
    for peer in peers:
        pl.semaphore_signal(barrier, inc=1, device_id=peer, device_id_type=MESH)
    pl.semaphore_wait(barrier, len(peers))


def _two_level_peers():
    x, y, c = _coords()
    return [(x, y, 1 - c)] + [(*_rel_chip(x, y, k), c) for k in range(1, 4)]


SEQ_ID_GATHER, SEQ_ID_SIBLING, SEQ_ID_CHIPS = 1, 2, 3


def _sequencer_call(body, peers, operands, out_types, sems, name, collective_id, after=()):
    n_in, n_out, n_after = len(operands), len(out_types), len(after)

    def launch(*refs):
        _handshake(peers())
        body(refs[:n_in], refs[n_in + n_after:n_in + n_after + n_out], *refs[n_in + n_after + n_out:])

    return pl.kernel(
        launch, name=name, out_type=out_types, mesh=plsc.ScalarSubcoreMesh(axis_name="seq", num_cores=1),
        scratch_types=sems, compiler_params=pltpu.CompilerParams(collective_id=collective_id))(*operands, *after)


def _all_gather_seq(shards, layouts, name, after=()):
    n = len(shards)
    blocks = [_block_of(kind, width) for kind, width, _ in layouts]
    return _sequencer_call(
        lambda ins, outs, *sems: _all_gather_body(ins, outs, *sems, blocks), _two_level_peers, shards,
        [jax.ShapeDtypeStruct(shape, sh.dtype) for sh, (_, _, shape) in zip(shards, layouts)],
        [pltpu.SemaphoreType.DMA((n, 7)), pltpu.SemaphoreType.DMA((n, 7)), pltpu.SemaphoreType.DMA((n,))],
        name, SEQ_ID_GATHER, after)


def _tie(small, after, name):
    def body(*refs):
        refs[-1][...] = refs[0][...]

    vmem = pl.BlockSpec(memory_space=pltpu.VMEM)
    return pl.pallas_call(
        body, name=name, in_specs=[vmem] + [pl.BlockSpec(memory_space=pl.ANY)] * len(after), out_specs=vmem,
        out_shape=jax.ShapeDtypeStruct(small.shape, small.dtype))(small, *after)


def _rs_to_sibling(grads, layouts, name):
    n = len(grads)
    blocks = [_block_of(kind, width) for kind, width, _ in layouts]

    def body(ins, outs, send_sems, recv_sems):
        x, y, c = _coords()
        sibling = (x, y, 1 - c)
        cps = []
        for t in range(n):
            for k in range(4):
                px, py = _rel_chip(x, y, k)
                cp = pltpu.make_async_remote_copy(
                    src_ref=blocks[t](ins[t], 4 * px + 2 * py + (1 - c)), dst_ref=outs[t].at[k],
                    send_sem=send_sems.at[t, k], recv_sem=recv_sems.at[t, k], device_id=sibling, device_id_type=MESH)
                cp.start()
                cps.append(cp)
        for cp in cps:
            cp.wait_recv()
        for cp in cps:
            cp.wait_send()

    def sibling_only():
        x, y, c = _coords()
        return [(x, y, 1 - c)]

    return _sequencer_call(
        body, sibling_only, grads,
        [jax.ShapeDtypeStruct((4,) + shape, g.dtype) for g, (_, _, shape) in zip(grads, layouts)],
        [pltpu.SemaphoreType.DMA((n, 4)), pltpu.SemaphoreType.DMA((n, 4))], name, SEQ_ID_SIBLING)


def _rs_chip_sum(grad, recv, layout, xyc, name):
    kind, width, shape = layout
    r, ccols = shape

    def src_index(k, xyc_ref):
        px = jnp.where(k % 2 == 1, 1 - xyc_ref[0], xyc_ref[0])
        py = jnp.where(k // 2 == 1, 1 - xyc_ref[1], xyc_ref[1])
        return 4 * px + 2 * py + xyc_ref[2]

    if kind == "col":
        g_spec = pl.BlockSpec((r, ccols), lambda k, s_: (0, src_index(k, s_)))
    elif kind == "row":
        g_spec = pl.BlockSpec((r, ccols), lambda k, s_: (src_index(k, s_), 0))
    else:
        g_spec = pl.BlockSpec((None, r, ccols), lambda k, s_: (src_index(k, s_), 0, 0))

    def body(xyc_ref, g_ref, r_ref, o_ref):
        o_ref[...] = (g_ref[...].astype(F32) + r_ref[...].astype(F32)).astype(o_ref.dtype)

    slot = pl.BlockSpec((None, r, ccols), lambda k, s_: (k, 0, 0))
    return pl.pallas_call(
        body, name=name,
        grid_spec=pltpu.PrefetchScalarGridSpec(num_scalar_prefetch=1, grid=(4,), in_specs=[g_spec, slot], out_specs=slot),
        out_shape=jax.ShapeDtypeStruct((4, r, ccols), grad.dtype), compiler_params=_cp(VMEM_BIG))(xyc, grad, recv)


def _rs_across_chips(parts, name):
    n = len(parts)

    def body(ins, outs, send_sems, recv_sems):
        x, y, c = _coords()
        cps = []
        for t in range(n):
            for k in range(1, 4):
                px, py = _rel_chip(x, y, k)
                cp = pltpu.make_async_remote_copy(
                    src_ref=ins[t].at[k], dst_ref=outs[t].at[k - 1], send_sem=send_sems.at[t, k - 1],
                    recv_sem=recv_sems.at[t, k - 1], device_id=(px, py, c), device_id_type=MESH)
                cp.start()
                cps.append(cp)
        for cp in cps:
            cp.wait_recv()
        for cp in cps:
            cp.wait_send()

    def other_chips():
        x, y, c = _coords()
        return [(*_rel_chip(x, y, k), c) for k in range(1, 4)]

    return _sequencer_call(
        body, other_chips, parts, [jax.ShapeDtypeStruct((3,) + p.shape[1:], p.dtype) for p in parts],
        [pltpu.SemaphoreType.DMA((n, 3)), pltpu.SemaphoreType.DMA((n, 3))], name, SEQ_ID_CHIPS)


def _adamw_math(w, g, m, v):
    m = ADAM_B1 * m + (1.0 - ADAM_B1) * g
    v = ADAM_B2 * v + (1.0 - ADAM_B2) * jnp.square(g)
    m_hat = m / (1.0 - ADAM_B1 ** ADAM_STEP)
    v_hat = v / (1.0 - ADAM_B2 ** ADAM_STEP)
    delta = -ADAM_LR * (m_hat / (jnp.sqrt(v_hat) + ADAM_EPS) + ADAM_WD * w)
    return delta, m, v


def _row_tile(rows, cap):
    best = None
    for cand in range(8, min(rows, cap) + 1, 8):
        if rows % cand == 0:
            best = cand
    assert best is not None, rows
    return best


def _adamw(w, m, v, parts, name, layer=None, prev=None, tr=256):
    r, ccols = w.shape[-2:]
    tr = _row_tile(r, tr)
    npart = len(parts)

    def wspec():
        if layer is None:
            return pl.BlockSpec((tr, ccols), lambda i: (i, 0))
        return pl.BlockSpec((None, tr, ccols), lambda i: (layer, i, 0))

    def pspec(lead):
        if lead is None:
            return pl.BlockSpec((tr, ccols), lambda i: (i, 0))
        return pl.BlockSpec((None, tr, ccols), lambda i: (lead, i, 0))

    def body(*refs):
        w_ref, m_ref, v_ref = refs[:3]
        p_refs = refs[3:3 + npart]
        outs = refs[len(refs) - 4:]
        g = p_refs[0][...].astype(F32)
        for p_ref in p_refs[1:]:
            g = g + p_ref[...].astype(F32)
        delta, mn, vn = _adamw_math(w_ref[...], g, m_ref[...], v_ref[...])
        outs[0][...] = g
        outs[1][...] = delta
        outs[2][...] = mn
        outs[3][...] = vn

    operands = [w, m, v] + [p for p, _ in parts]
    in_specs = [wspec(), wspec(), wspec()] + [pspec(lead) for _, lead in parts]
    aliases = {}
    if prev is not None:
        for i, p in enumerate(prev):
            aliases[len(operands)] = i
            operands.append(p)
            in_specs.append(pl.BlockSpec(memory_space=pl.ANY))
    return pl.pallas_call(
        body, name=name, grid=(r // tr,), in_specs=in_specs, out_specs=[wspec()] * 4,
        out_shape=[jax.ShapeDtypeStruct(w.shape, F32)] * 4, input_output_aliases=aliases)(*operands)


def _sum8(buf, name):
    _, r, ccols = buf.shape

    def body(b_ref, o_ref):
        acc = b_ref[0]
        for j in range(1, N_DEV):
            acc = acc + b_ref[j]
        o_ref[...] = acc

    tr = _row_tile(r, 256)
    return pl.pallas_call(
        body, name=name, grid=(r // tr,), in_specs=[pl.BlockSpec((N_DEV, tr, ccols), lambda i: (0, i, 0))],
        out_specs=pl.BlockSpec((tr, ccols), lambda i: (i, 0)), out_shape=jax.ShapeDtypeStruct((r, ccols), F32))(buf)


def _pack(arrays):
    pieces, layout, off = [], [], 0
    for a in arrays:
        n = a.size
        padded = -(-n // 1024) * 1024
        flat = a.reshape(-1).astype(F32)
        if padded != n:
            flat = jnp.pad(flat, (0, padded - n))
        pieces.append(flat.reshape(padded // 128, 128))
        layout.append((off, n, a.shape))
        off += padded // 128
    return jnp.concatenate(pieces, axis=0), layout


def _unpack(packed, layout):
    out = []
    for off, n, shape in layout:
        rows = -(-n // 1024) * 8
        out.append(packed[off:off + rows].reshape(-1)[:n].reshape(shape))
    return out


def kernel(x, mem, norm_mix, norm_ffn, mem_norm, w_kv, w_out, w_ffn1, w_ffn2, a_in, a_ln_g, a_ln_b, a_ws, a_bs, b_in, b_conv_w, b_conv_b, b_dt_bias, b_a_log, b_d, b_gnorm, final_norm, loss_target, m_norm_mix, m_norm_ffn, m_mem_norm, m_w_kv, m_w_out, m_w_ffn1, m_w_ffn2, m_a_in, m_a_ln_g, m_a_ln_b, m_a_ws, m_a_bs, m_b_in, m_b_conv_w, m_b_conv_b, m_b_dt_bias, m_b_a_log, m_b_d, m_b_gnorm, m_final_norm, v_norm_mix, v_norm_ffn, v_mem_norm, v_w_kv, v_w_out, v_w_ffn1, v_w_ffn2, v_a_in, v_a_ln_g, v_a_ln_b, v_a_ws, v_a_bs, v_b_in, v_b_conv_w, v_b_conv_b, v_b_dt_bias, v_b_a_log, v_b_d, v_b_gnorm, v_final_norm):
    s = x.shape[1]
    xs = x.reshape(s, D_MODEL)
    mems = mem.reshape(N_MEM, D_MODEL)
    target = loss_target.reshape(s, D_MODEL)
    ax, ay, ac = lax.axis_index("x"), lax.axis_index("y"), lax.axis_index("c")
    me = 4 * ax + 2 * ay + ac
    xyc = jnp.stack([ax, ay, ac]).astype(jnp.int32)

    b_cols = b_in.shape[2]
    act = lambda a: a.astype(_ACT)
    lay_f1, lay_f2 = ("col", 512, (1, D_MODEL, D_FF)), ("row", 512, (1, D_FF, D_MODEL))
    lay_out, lay_kv = ("row", 384, (1, 3 * D_MODEL, D_MODEL)), ("col", 256, (1, D_MODEL, 2 * X_WIDTH))
    small_w_pack = _pack([b_conv_w[0], b_conv_b[0], b_gnorm[0]])[0]
    WA, wkv0 = _all_gather_seq([act(a_in), act(w_kv[0:1])], [("col", 640, (1, D_MODEL, 5 * D_MODEL)), lay_kv], "ag_proj_a")
    (wo0,) = _all_gather_seq([act(w_out[0:1])], [lay_out], "ag_out0", after=[WA])
    w1_0, w2_0 = _all_gather_seq([act(w_ffn1[0:1]), act(w_ffn2[0:1])], [lay_f1, lay_f2], "ag_ffn0", after=[wo0])
    wb_blk, wkv1, small_w = _all_gather_seq(
        [act(b_in[0]), act(w_kv[1:2]), small_w_pack],
        [("blk", 0, (N_DEV, D_MODEL, b_cols)), lay_kv, ("blk", 0, (N_DEV, 32, 128))], "ag_proj_b", after=[w1_0])
    (wo1,) = _all_gather_seq([act(w_out[1:2])], [lay_out], "ag_out1", after=[wb_blk])
    w1_1, w2_1 = _all_gather_seq([act(w_ffn1[1:2]), act(w_ffn2[1:2])], [lay_f1, lay_f2], "ag_ffn1", after=[wo1])
    W1, W2, WO, WKV = [w1_0, w1_1], [w2_0, w2_1], [wo0, wo1], [wkv0, wkv1]
    wb_full = jnp.transpose(wb_blk, (1, 0, 2)).reshape(D_MODEL, N_DEV * b_cols)
    dt0 = D_INNER + CONV_DIM
    WB = jnp.concatenate([wb_full[:, :dt0], wb_full[:, dt0 + SSM_HEADS:]], axis=1)
    WBDT = jnp.pad(wb_full[:, dt0:dt0 + SSM_HEADS], ((0, 0), (0, HPAD - SSM_HEADS)))

    row = lambda a: a.reshape(1, -1)
    nmix = [row(norm_mix[0]), row(norm_mix[1])]
    nffn = [row(norm_ffn[0]), row(norm_ffn[1])]
    nmem = [row(mem_norm[0]), row(mem_norm[1])]
    fin = row(final_norm)
    lng, lnb = a_ln_g.reshape(1, D_INNER), a_ln_b.reshape(1, D_INNER)
    ws = a_ws[0]
    bs3 = a_bs[0].reshape(A_GROUPS, CHUNK, 1)
    pad_h = lambda a: jnp.pad(a.reshape(-1), (0, HPAD - SSM_HEADS))
    bias_row, bias_col = pad_h(b_dt_bias).reshape(1, HPAD), pad_h(b_dt_bias).reshape(HPAD, 1)
    alog_row, alog_col = pad_h(b_a_log).reshape(1, HPAD), pad_h(b_a_log).reshape(HPAD, 1)
    dfull = jnp.repeat(b_d.reshape(-1), SSM_P).reshape(1, D_INNER)

    cw_sh, cb_sh, gn_sh = 4 * 384, 384, 256
    sw = small_w.reshape(N_DEV, 32 * 128)
    conv_w = jnp.transpose(sw[:, :cw_sh].reshape(N_DEV, CONV_K, 384), (1, 0, 2)).reshape(CONV_K, CONV_DIM)
    conv_b = sw[:, 2048:2048 + cb_sh].reshape(1, CONV_DIM)
    gnorm = sw[:, 3072:3072 + gn_sh].reshape(1, D_INNER)

    kvs, mns = [None, None], [None, None]

    def mem_kv(i, after=None):
        gain = nmem[i] if after is None else _tie(nmem[i], after, f"tie_mem{i}")
        mns[i] = _rms_fwd(mems, gain, f"mem_norm{i}")
        kvs[i] = _mm(mns[i], WKV[i], m=N_MEM, n=2 * X_WIDTH, k=D_MODEL, b_at=(0, 0, 0), out_dtype=_ACT, name=f"kv{i}")

    def ffn_fwd(h, i):
        f = _rms_fwd(h, nffn[i], f"ffn_norm{i}")
        p = _mm(f, W1[i], m=s, n=D_FF, k=D_MODEL, b_at=(0, 0, 0), out_dtype=_ACT, name=f"ffn_up{i}")
        hn = _mm(p, W2[i], m=s, n=D_MODEL, k=D_FF, b_at=(0, 0, 0), a_pro="relu2", add=h, name=f"ffn_down{i}")
        return f, p, hn

    def out_proj(h, cat, i):
        return _mm(cat, WO[i], m=s, n=D_MODEL, k=3 * D_MODEL, b_at=(0, 0, 0), add=h, name=f"out_proj{i}")

    a0 = _rms_fwd(xs, nmix[0], "mix_norm0")
    proj_a = _mm(a0, WA, m=s, n=5 * D_MODEL, k=D_MODEL, b_at=(0, 0, 0), name="proj_a")
    mem_kv(0)
    cat_a = _gmlp_fwd(proj_a, lng, lnb, ws, bs3, "gmlp_fwd")
    cat_a = _attn_fwd(proj_a, 4, kvs[0], cat_a, "attn_fwd0")
    h1 = out_proj(xs, cat_a, 0)
    f0, p0, h2 = ffn_fwd(h1, 0)

    a1 = _rms_fwd(h2, nmix[1], "mix_norm1")
    proj_b = _mm(a1, WB, m=s, n=6 * D_MODEL, k=D_MODEL, name="proj_b")
    dt_raw = _mm(a1, WBDT, m=s, n=HPAD, k=D_MODEL, name="proj_dt")
    dt_raw_t = dt_raw.T
    xbc = _conv_fwd(proj_b, conv_w, conv_b, "conv_fwd")
    y_ssd, states = _ssd_fwd(xbc, dt_raw, dt_raw_t, bias_row, bias_col, alog_row, alog_col, dfull, "ssd_fwd")
    cat_b = _gate_fwd(y_ssd, proj_b, gnorm, "gate_fwd")
    mem_kv(1, after=[cat_b])
    cat_b = _attn_fwd(proj_b, 5, kvs[1], cat_b, "attn_fwd1")
    h3 = out_proj(h2, cat_b, 1)
    f1, p1, h4 = ffn_fwd(h3, 1)

    loss_part, dh, dh_act, d_fin = _loss_head(h4, fin, target, "loss_head")

    g_f1, g_f2, g_out, g_kv = [None, None], [None, None], [None, None], [None, None]
    d_nffn, d_nmix, d_nmem = [None, None], [None, None], [None, None]

    def ffn_bwd(dh, dh_act, h_in, f, p, i):
        dp = _mm(dh_act, W2[i], m=s, n=D_FF, k=D_MODEL, tb=True, b_at=(0, 0, 0), epi_p=p, out_dtype=_ACT, name=f"ffn_down_dx{i}")
        g_f2[i] = _mm(p, dh_act, m=D_FF, n=D_MODEL, k=s, ta=True, a_pro="relu2", out_dtype=_ACT, name=f"ffn_down_dw{i}")
        g_f1[i] = _mm(f, dp, m=D_MODEL, n=D_FF, k=s, ta=True, out_dtype=_ACT, name=f"ffn_up_dw{i}")
        df = _mm(dp, W1[i], m=s, n=D_MODEL, k=D_FF, tb=True, b_at=(0, 0, 0), name=f"ffn_up_dx{i}")
        dh_in, dh_in_act, d_nffn[i] = _rms_bwd(h_in, nffn[i], df, dh, f"ffn_norm_bwd{i}")
        return dh_in, dh_in_act

    def out_bwd(dh_act, cat, i):
        dcat = _mm(dh_act, WO[i], m=s, n=3 * D_MODEL, k=D_MODEL, tb=True, b_at=(0, 0, 0), out_dtype=_ACT, name=f"out_dx{i}")
        g_out[i] = _mm(cat, dh_act, m=3 * D_MODEL, n=D_MODEL, k=s, ta=True, out_dtype=_ACT, name=f"out_dw{i}")
        return dcat

    def mem_bwd(dkv, i):
        g_kv[i] = _mm(mns[i], dkv, m=D_MODEL, n=2 * X_WIDTH, k=N_MEM, ta=True, out_dtype=_ACT, name=f"kv_dw{i}")
        dmn = _mm(dkv, WKV[i], m=N_MEM, n=D_MODEL, k=2 * X_WIDTH, tb=True, b_at=(0, 0, 0), name=f"kv_dx{i}")
        _, _, d_nmem[i] = _rms_bwd(mems, nmem[i], dmn, None, f"mem_norm_bwd{i}")

    lay_g = {"f1": ("col", 512, (D_MODEL, 512)), "f2": ("row", 512, (512, D_MODEL)), "out": ("row", 384, (384, D_MODEL)),
             "kv": ("col", 256, (D_MODEL, 256)), "a": ("col", 640, (D_MODEL, 640)), "b": ("blk", 0, (D_MODEL, b_cols))}
    reduced = {}

    def reduce_scatter(group, tag):
        grads3, lays3 = [], []
        for fam, _, g in group:
            kind, width, shape = lay_g[fam]
            grads3.append(g if kind == "blk" else g.reshape((1,) + g.shape))
            lays3.append((kind, width, shape if kind == "blk" else (1,) + shape))
        recv1 = _rs_to_sibling(grads3, lays3, f"rs_sibling_{tag}")
        parts = [_rs_chip_sum(g, recv1[t].reshape((4,) + lay_g[fam][2]), lay_g[fam], xyc, f"rs_chip_sum_{fam}{i}")
                 for t, (fam, i, g) in enumerate(group)]
        recv2 = _rs_across_chips(parts, f"rs_chips_{tag}")
        for (fam, i, _), p, r2 in zip(group, parts, recv2):
            reduced[fam, i] = (p, r2)
        return parts

    dh3, dh3_act = ffn_bwd(dh, dh_act, h3, f1, p1, 1)
    dcat_b = out_bwd(dh3_act, cat_b, 1)
    sums = reduce_scatter([("f1", 1, g_f1[1]), ("f2", 1, g_f2[1]), ("out", 1, g_out[1])], "ffn1")
    dy_ssd, dproj_b, d_gnorm = _gate_bwd(y_ssd, proj_b, gnorm, dcat_b, "gate_bwd")
    dproj_b, dkv_b = _attn_bwd(proj_b, 5, kvs[1], dcat_b, dproj_b, "attn_bwd1")
    mem_bwd(dkv_b, 1)
    dxbc, ddt_raw, d_alog, d_dskip, d_dtbias = _ssd_bwd(
        xbc, dt_raw, dt_raw_t, _tie(bias_row, sums, "tie_ffn1"), bias_col, alog_row, alog_col, dfull, dy_ssd, states, "ssd_bwd")
    dproj_b, d_convw, d_convb = _conv_bwd(proj_b, conv_w, conv_b, dxbc, dproj_b, "conv_bwd")
    gb = _mm(a1, dproj_b, m=D_MODEL, n=6 * D_MODEL, k=s, ta=True, out_dtype=_ACT, name="proj_b_dw")
    gb_dt = _mm(a1, ddt_raw, m=D_MODEL, n=HPAD, k=s, ta=True, out_dtype=_ACT, name="proj_b_dw_dt")
    gb_full = jnp.concatenate([gb[:, :dt0], gb_dt[:, :SSM_HEADS], gb[:, dt0:]], axis=1)
    gb_blk = jnp.transpose(gb_full.reshape(D_MODEL, N_DEV, b_cols), (1, 0, 2))
    sums = reduce_scatter([("kv", 1, g_kv[1]), ("b", 0, gb_blk)], "mix1")
    da1 = _mm(dproj_b, WB, m=s, n=D_MODEL, k=6 * D_MODEL, tb=True, name="proj_b_dx")
    da1 = _mm(ddt_raw, WBDT, m=s, n=D_MODEL, k=HPAD, tb=True, add=da1, name="proj_b_dx_dt")
    dh2, dh2_act, d_nmix[1] = _rms_bwd(h2, _tie(nmix[1], sums, "tie_mix1"), da1, dh3, "mix_norm_bwd1")

    dh1, dh1_act = ffn_bwd(dh2, dh2_act, h1, f0, p0, 0)
    dcat_a = out_bwd(dh1_act, cat_a, 0)
    sums = reduce_scatter([("f1", 0, g_f1[0]), ("f2", 0, g_f2[0]), ("out", 0, g_out[0])], "ffn0")
    dproj_a, d_ws, d_bs3, d_lng, d_lnb = _gmlp_bwd(proj_a, dcat_a, lng, lnb, ws, bs3, "gmlp_bwd")
    dproj_a, dkv_a = _attn_bwd(proj_a, 4, kvs[0], dcat_a, dproj_a, "attn_bwd0")
    mem_bwd(dkv_a, 0)
    ga = _mm(a0, dproj_a, m=D_MODEL, n=5 * D_MODEL, k=s, ta=True, out_dtype=_ACT, after=sums, name="proj_a_dw")
    sums = reduce_scatter([("kv", 0, g_kv[0]), ("a", 0, ga)], "mix0")
    da0 = _mm(dproj_a, WA, m=s, n=D_MODEL, k=5 * D_MODEL, tb=True, b_at=(0, 0, 0), name="proj_a_dx")
    grad_x, _, d_nmix[0] = _rms_bwd(xs, _tie(nmix[0], sums, "tie_mix0"), da0, dh1, "mix_norm_bwd0")

    def big_update(w, m, v, fam, nlayer):
        res = None
        for i in range(nlayer):
            part, recv2 = reduced[fam, i]
            plist = [(part, 0), (recv2, 0), (recv2, 1), (recv2, 2)]
            res = _adamw(w, m, v, plist, f"adamw_{fam}{i}", layer=i, prev=res)
        return res

    r_f1 = big_update(w_ffn1, m_w_ffn1, v_w_ffn1, "f1", 2)
    r_f2 = big_update(w_ffn2, m_w_ffn2, v_w_ffn2, "f2", 2)
    r_out = big_update(w_out, m_w_out, v_w_out, "out", 2)
    r_kv = big_update(w_kv, m_w_kv, v_w_kv, "kv", 2)
    r_a = big_update(a_in, m_a_in, v_a_in, "a", 1)
    r_b = big_update(b_in, m_b_in, v_b_in, "b", 1)

    rep_names = ["norm_mix", "norm_ffn", "mem_norm", "a_ln_g", "a_ln_b", "a_ws", "a_bs", "b_dt_bias", "b_a_log", "b_d",
                 "final_norm"]
    rep_grads = [jnp.concatenate(d_nmix, axis=0), jnp.concatenate(d_nffn, axis=0), jnp.concatenate(d_nmem, axis=0),
                 d_lng, d_lnb, d_ws.reshape(1, A_GROUPS, CHUNK, CHUNK), d_bs3.reshape(1, A_GROUPS, CHUNK),
                 d_dtbias[:, :SSM_HEADS], d_alog[:, :SSM_HEADS], d_dskip[:, :SSM_HEADS], d_fin.reshape(D_MODEL)]
    rep_w = [norm_mix, norm_ffn, mem_norm, a_ln_g, a_ln_b, a_ws, a_bs, b_dt_bias, b_a_log, b_d, final_norm]
    rep_m = [m_norm_mix, m_norm_ffn, m_mem_norm, m_a_ln_g, m_a_ln_b, m_a_ws, m_a_bs, m_b_dt_bias, m_b_a_log, m_b_d, m_final_norm]
    rep_v = [v_norm_mix, v_norm_ffn, v_mem_norm, v_a_ln_g, v_a_ln_b, v_a_ws, v_a_bs, v_b_dt_bias, v_b_a_log, v_b_d, v_final_norm]
    rep_grads = [g.reshape(w.shape) for g, w in zip(rep_grads, rep_w)]
    sh_grads = [d_convw, d_convb, d_gnorm]
    g_pack, g_layout = _pack(rep_grads + sh_grads + [loss_part])
    n_rep = len(rep_grads)
    (g_all,) = _all_gather([g_pack], [("blk", 0, (N_DEV,) + g_pack.shape)], "ag_small_grads")
    g_small = _sum8(g_all, "sum_small_grads")
    g_list = _unpack(g_small, g_layout)
    loss = g_list[-1][0, 0]
    wp, w_layout = _pack(rep_w)
    mp, _ = _pack(rep_m)
    vp, _ = _pack(rep_v)
    gp, _ = _pack(g_list[:n_rep])
    rep_res = [_unpack(o, w_layout) for o in _adamw(wp, mp, vp, [(gp, None)], "adamw_replicated", tr=88)]

    gcw = lax.dynamic_slice_in_dim(g_list[n_rep], me * 384, 384, axis=1).reshape(1, CONV_K, 384)
    gcb = lax.dynamic_slice_in_dim(g_list[n_rep + 1], me * 384, 384, axis=1)
    ggn = lax.dynamic_slice_in_dim(g_list[n_rep + 2], me * 256, 256, axis=1)
    sh_w = [b_conv_w, b_conv_b, b_gnorm]
    sh_m = [m_b_conv_w, m_b_conv_b, m_b_gnorm]
    sh_v = [v_b_conv_w, v_b_conv_b, v_b_gnorm]
    swp, sw_layout = _pack(sh_w)
    smp, _ = _pack(sh_m)
    svp, _ = _pack(sh_v)
    sgp, _ = _pack([gcw, gcb, ggn])
    sh_res = [_unpack(o, sw_layout) for o in _adamw(swp, smp, svp, [(sgp, None)], "adamw_sharded_small", tr=8)]

    names = ["norm_mix", "norm_ffn", "mem_norm", "w_kv", "w_out", "w_ffn1", "w_ffn2", "a_in", "a_ln_g", "a_ln_b", "a_ws",
             "a_bs", "b_in", "b_conv_w", "b_conv_b", "b_dt_bias", "b_a_log", "b_d", "b_gnorm", "final_norm"]
    big = {"w_kv": r_kv, "w_out": r_out, "w_ffn1": r_f1, "w_ffn2": r_f2, "a_in": r_a, "b_in": r_b}
    sh_names = ["b_conv_w", "b_conv_b", "b_gnorm"]
    outs = [loss, grad_x.reshape(x.shape)]
    for kind in range(4):
        for nm in names:
            if nm in big:
                outs.append(big[nm][kind])
            elif nm in sh_names:
                outs.append(sh_res[kind][sh_names.index(nm)])
            else:
                outs.append(rep_res[kind][rep_names.index(nm)])
    return tuple(outs)
```

```python
import functools
import math

import jax
import jax.numpy as jnp
from jax import lax
from jax.experimental import pallas as pl
from jax.experimental.pallas import tpu as pltpu
from jax.experimental.pallas import tpu_sc as plsc

F32 = jnp.float32
_MXU = jnp.bfloat16
_ACT = jnp.bfloat16
_HI = lax.Precision.HIGHEST

D_MODEL = 1024
CHUNK = 128
N_MEM = 256
D_INNER = 2048
A_GROUPS = 8
A_GW = D_INNER // A_GROUPS
SSM_HEADS = 32
SSM_P = 64
SSM_GROUPS = 4
SSM_GW = D_INNER // SSM_GROUPS
SSM_N = 128
CONV_K = 4
CONV_DIM = 3072
X_HEADS = 4
X_HD = 256
X_WIDTH = 1024
D_FF = 4096
EPS = 1e-6
HPAD = 128
N_DEV = 8

ADAM_LR = 0.001
ADAM_B1 = 0.9
ADAM_B2 = 0.999
ADAM_EPS = 1e-08
ADAM_WD = 0.01
ADAM_STEP = 10

VMEM_BIG = 56 * 1024 * 1024
MESH = pl.DeviceIdType.MESH


def _cp(vmem=None):
    if vmem is None:
        return pltpu.CompilerParams()
    return pltpu.CompilerParams(vmem_limit_bytes=vmem)


def _dot(a, b, dims=((1,), (0,))):
    return lax.dot_general(a.astype(_MXU), b.astype(_MXU), (dims, ((), ())), preferred_element_type=F32)


def _dot_nt(a, b):
    return _dot(a, b, ((1,), (1,)))


def _dot_tn(a, b):
    return _dot(a, b, ((0,), (0,)))


def _dot_hi(a, b, dims=((1,), (0,))):
    return lax.dot_general(a.astype(F32), b.astype(F32), (dims, ((), ())), precision=_HI, preferred_element_type=F32)


def _sigmoid(x):
    return 1.0 / (1.0 + jnp.exp(-x))


def _gelu(x):
    return 0.5 * x * (1.0 + lax.erf(x * (1.0 / math.sqrt(2.0))))


def _gelu_grad(x):
    return 0.5 * (1.0 + lax.erf(x * (1.0 / math.sqrt(2.0)))) + x * jnp.exp(-0.5 * x * x) * (1.0 / math.sqrt(2.0 * math.pi))


def _softplus(x):
    return jnp.maximum(x, 0.0) + jnp.log1p(jnp.exp(-jnp.abs(x)))


def _iota(shape, dim):
    return lax.broadcasted_iota(jnp.int32, shape, dim)


MM_VMEM_BUDGET = 40 * 1024 * 1024
HBM_BYTES_PER_S = 2.5e12
GRID_STEP_S = 0.35e-6
VMEM_ACC_BYTES_PER_S = 6e12


def _divisors(dim, unit):
    out = [d for d in range(unit, min(dim, 2048) + 1, unit) if dim % d == 0]
    return out if out else [dim]


def _mm_tiles(m, n, k, sa, sb, s_mn, a_pro, offsets):
    best = None
    (a_r0, a_c0, ta), (b_r0, b_c0, tb), (o_r0, o_c0) = offsets
    for tm in _divisors(m, 128):
        for tn in _divisors(n, 128):
            for tk in [k // d for d in (1, 2, 3, 4, 6, 8) if k % d == 0 and (k // d) % 128 == 0]:
                a_t = (tk, tm) if ta else (tm, tk)
                b_t = (tn, tk) if tb else (tk, tn)
                if a_r0 % a_t[0] or a_c0 % a_t[1] or b_r0 % b_t[0] or b_c0 % b_t[1] or o_r0 % tm or o_c0 % tn:
                    continue
                nk = k // tk
                vmem = 2 * (tm * tk * sa + tk * tn * sb + tm * tn * s_mn) + tm * tn * 4 * (2 if nk > 1 else 1)
                if a_pro or sa == 4:
                    vmem += tm * tk * 6
                if sb == 4:
                    vmem += tk * tn * 2
                if vmem > MM_VMEM_BUDGET:
                    continue
                gi, gj = m // tm, n // tn
                for j_inner in (True, False):
                    if nk > 1:
                        traffic = gj * m * k * sa + gi * k * n * sb
                    elif j_inner:
                        traffic = m * k * sa + gi * k * n * sb
                    else:
                        traffic = gj * m * k * sa + k * n * sb
                    traffic += m * n * s_mn + (tm * tk * sa + tk * tn * sb)
                    cost = traffic / HBM_BYTES_PER_S + gi * gj * nk * GRID_STEP_S
                    if nk > 1:
                        cost += m * n * 8 * nk / VMEM_ACC_BYTES_PER_S
                    if best is None or cost < best[0]:
                        best = (cost, tm, tn, tk, j_inner)
    assert best is not None, (m, n, k)
    return best[1:]


def _mm(a, b, *, m, n, k, name, ta=False, tb=False, a_at=(None, 0, 0), b_at=(None, 0, 0),
        out_dtype=F32, add=None, epi_p=None, epi_at=(None, 0, 0), out=None, out_at=(None, 0, 0),
        out_full=None, a_pro=None, after=()):
    s_mn =jnp.dtype(out.dtype if out is not None else out_dtype).itemsize
    s_mn += add.dtype.itemsize if add is not None else 0
    s_mn += epi_p.dtype.itemsize if epi_p is not None else 0
    tm, tn, tk, j_inner = _mm_tiles(m, n, k, a.dtype.itemsize, b.dtype.itemsize, s_mn, a_pro is not None,
                                    ((a_at[1], a_at[2], ta), (b_at[1], b_at[2], tb), (out_at[1], out_at[2])))
    nk = k // tk

    def spec(at, tr, tc, rsel, csel):
        lead, r0, c0 = at
        assert r0 % tr == 0 and c0 % tc == 0, (name, at, tr, tc)
        rb, cb = r0 // tr, c0 // tc
        if lead is None:
            return pl.BlockSpec((tr, tc), lambda g0, g1, kk: (rb + rsel(g0, g1, kk), cb + csel(g0, g1, kk)))
        return pl.BlockSpec((None, tr, tc), lambda g0, g1, kk: (lead, rb + rsel(g0, g1, kk), cb + csel(g0, g1, kk)))

    gi = (lambda g0, g1, kk: g0) if j_inner else (lambda g0, g1, kk: g1)
    gj = (lambda g0, g1, kk: g1) if j_inner else (lambda g0, g1, kk: g0)
    gk = lambda g0, g1, kk: kk
    a_spec = spec(a_at, tk, tm, gk, gi) if ta else spec(a_at, tm, tk, gi, gk)
    b_spec = spec(b_at, tn, tk, gj, gk) if tb else spec(b_at, tk, tn, gk, gj)
    dims = ((0,), (0,)) if ta else (((1,), (1,)) if tb else ((1,), (0,)))
    assert not (ta and tb)

    operands, in_specs = [a, b], [a_spec, b_spec]
    if add is not None:
        operands.append(add)
        in_specs.append(spec((None, 0, 0), tm, tn, gi, gj))
    if epi_p is not None:
        operands.append(epi_p)
        in_specs.append(spec(epi_at, tm, tn, gi, gj))
    aliases = {}
    if out is not None:
        aliases = {len(operands): 0}
        operands.append(out)
        in_specs.append(pl.BlockSpec(memory_space=pl.ANY))
        out_struct = jax.ShapeDtypeStruct(out.shape, out.dtype)
        out_dtype = out.dtype
    else:
        out_struct = jax.ShapeDtypeStruct(out_full if out_full is not None else (m, n), out_dtype)
    has_add, has_epi = add is not None, epi_p is not None
    n_skip = (1 if out is not None else 0) + len(after)
    operands += list(after)
    in_specs += [pl.BlockSpec(memory_space=pl.ANY)] * len(after)

    def body(*refs):
        a_ref, b_ref = refs[0], refs[1]
        pos = 2
        add_ref = epi_ref = None
        if has_add:
            add_ref = refs[pos]
            pos += 1
        if has_epi:
            epi_ref = refs[pos]
            pos += 1
        pos += n_skip
        o_ref = refs[pos]

        def finish(r):
            if has_add:
                r = r + add_ref[...].astype(F32)
            if has_epi:
                r = r * (2.0 * jnp.maximum(epi_ref[...].astype(F32), 0.0))
            o_ref[...] = r.astype(o_ref.dtype)

        av = a_ref[...]
        if a_pro == "relu2":
            av = jnp.square(jnp.maximum(av.astype(F32), 0.0))
        part = _dot(av, b_ref[...], dims)
        if nk == 1:
            finish(part)
        else:
            acc_ref = refs[pos + 1]
            kk = pl.program_id(2)

            @pl.when(kk == 0)
            def _():
                acc_ref[...] = part

            @pl.when(kk > 0)
            def _():
                acc_ref[...] += part

            @pl.when(kk == nk - 1)
            def _():
                finish(acc_ref[...])

    grid = (m // tm, n // tn, nk) if j_inner else (n // tn, m // tm, nk)
    return pl.pallas_call(
        body, name=name, grid=grid, in_specs=in_specs,
        out_specs=spec(out_at, tm, tn, gi, gj), out_shape=out_struct,
        scratch_shapes=[pltpu.VMEM((tm, tn), F32)] if nk > 1 else [], input_output_aliases=aliases,
        compiler_params=_cp(VMEM_BIG))(*operands)


def _rms_fwd(x, g, name, tm=256):
    s, d = x.shape
    tm = min(tm, s)

    def body(x_ref, g_ref, o_ref):
        xv = x_ref[...]
        r = lax.rsqrt(jnp.mean(xv * xv, axis=-1, keepdims=True) + EPS)
        o_ref[...] = (xv * r * g_ref[...]).astype(o_ref.dtype)

    return pl.pallas_call(
        body, name=name, grid=(s // tm,),
        in_specs=[pl.BlockSpec((tm, d), lambda i: (i, 0)), pl.BlockSpec((1, d), lambda i: (0, 0))],
        out_specs=pl.BlockSpec((tm, d), lambda i: (i, 0)),
        out_shape=jax.ShapeDtypeStruct((s, d), _ACT))(x, g)


def _rms_bwd(x, g, dy, dres, name, tm=256):
    s, d = x.shape
    tm = min(tm, s)
    has_res = dres is not None

    def body(*refs):
        if has_res:
            x_ref, g_ref, dy_ref, dres_ref, dx_ref, dxa_ref, dg_ref = refs
        else:
            x_ref, g_ref, dy_ref, dx_ref, dxa_ref, dg_ref = refs

        @pl.when(pl.program_id(0) == 0)
        def _():
            dg_ref[...] = jnp.zeros_like(dg_ref)

        xv = x_ref[...]
        dyv = dy_ref[...].astype(F32)
        r = lax.rsqrt(jnp.mean(xv * xv, axis=-1, keepdims=True) + EPS)
        xh = xv * r
        dyg = dyv * g_ref[...]
        dx = r * (dyg - xh * jnp.mean(dyg * xh, axis=-1, keepdims=True))
        if has_res:
            dx = dx + dres_ref[...]
        dx_ref[...] = dx
        dxa_ref[...] = dx.astype(dxa_ref.dtype)
        dg_ref[...] += jnp.sum(dyv * xh, axis=0, keepdims=True)

    row = pl.BlockSpec((tm, d), lambda i: (i, 0))
    vec = pl.BlockSpec((1, d), lambda i: (0, 0))
    in_specs = [row, vec, row] + ([row] if has_res else [])
    operands = [x, g, dy] + ([dres] if has_res else [])
    return pl.pallas_call(
        body, name=name, grid=(s // tm,), in_specs=in_specs, out_specs=[row, row, vec],
        out_shape=[jax.ShapeDtypeStruct((s, d), F32), jax.ShapeDtypeStruct((s, d), _ACT),
                   jax.ShapeDtypeStruct((1, d), F32)])(*operands)


def _loss_head(h, g, target, name, tm=256):
    s, d = h.shape
    tm = min(tm, s)

    def body(h_ref, g_ref, t_ref, loss_ref, dh_ref, dha_ref, dg_ref):
        @pl.when(pl.program_id(0) == 0)
        def _():
            dg_ref[...] = jnp.zeros_like(dg_ref)
            loss_ref[...] = jnp.zeros_like(loss_ref)

        xv = h_ref[...]
        r = lax.rsqrt(jnp.mean(xv * xv, axis=-1, keepdims=True) + EPS)
        xh = xv * r
        err = xh * g_ref[...] - t_ref[...]
        loss_ref[...] += jnp.full(loss_ref.shape, 0.5 * jnp.sum(jnp.mean(err * err, axis=-1, keepdims=True)), F32)
        dyv = err * (1.0 / d)
        dyg = dyv * g_ref[...]
        dh = r * (dyg - xh * jnp.mean(dyg * xh, axis=-1, keepdims=True))
        dh_ref[...] = dh
        dha_ref[...] = dh.astype(dha_ref.dtype)
        dg_ref[...] += jnp.sum(dyv * xh, axis=0, keepdims=True)

    row = pl.BlockSpec((tm, d), lambda i: (i, 0))
    vec = pl.BlockSpec((1, d), lambda i: (0, 0))
    return pl.pallas_call(
        body, name=name, grid=(s // tm,), in_specs=[row, vec, row],
        out_specs=[pl.BlockSpec((1, 128), lambda i: (0, 0)), row, row, vec],
        out_shape=[jax.ShapeDtypeStruct((1, 128), F32), jax.ShapeDtypeStruct((s, d), F32),
                   jax.ShapeDtypeStruct((s, d), _ACT), jax.ShapeDtypeStruct((1, d), F32)])(h, g, target)


def _gmlp_parts(pu, pv, lng, lnb):
    u = _gelu(pu)
    v = _gelu(pv)
    mu = jnp.mean(v, axis=-1, keepdims=True)
    vc = v - mu
    rstd = lax.rsqrt(jnp.mean(vc * vc, axis=-1, keepdims=True) + EPS)
    xhat = vc * rstd
    vn = xhat * lng + lnb
    return u, xhat, rstd, vn


def _gmlp_fwd(proj, lng, lnb, ws, bs3, name):
    s = proj.shape[0]

    def body(pu_ref, pv_ref, lng_ref, lnb_ref, ws_ref, bs_ref, o_ref):
        u, _, _, vn = _gmlp_parts(pu_ref[...], pv_ref[...], lng_ref[...], lnb_ref[...])
        causal = _iota((CHUNK, CHUNK), 0) >= _iota((CHUNK, CHUNK), 1)
        for g in range(A_GROUPS):
            sl = slice(g * A_GW, (g + 1) * A_GW)
            w = jnp.where(causal, ws_ref[g], 0.0)
            sv = _dot(w, vn[:, sl]) + bs_ref[g]
            o_ref[:, sl] = (u[:, sl] * sv).astype(o_ref.dtype)

    full = lambda shape: pl.BlockSpec(shape, lambda c: (0,) * len(shape))
    return pl.pallas_call(
        body, name=name, grid=(s // CHUNK,),
        in_specs=[pl.BlockSpec((CHUNK, D_INNER), lambda c: (c, 0)), pl.BlockSpec((CHUNK, D_INNER), lambda c: (c, 1)),
                  full((1, D_INNER)), full((1, D_INNER)), full((A_GROUPS, CHUNK, CHUNK)), full((A_GROUPS, CHUNK, 1))],
        out_specs=pl.BlockSpec((CHUNK, D_INNER), lambda c: (c, 0)),
        out_shape=jax.ShapeDtypeStruct((s, D_INNER + X_WIDTH), _ACT), compiler_params=_cp(VMEM_BIG))(proj, proj, lng, lnb, ws, bs3)


def _gmlp_bwd(proj, dcat, lng, lnb, ws, bs3, name):
    s = proj.shape[0]

    def body(pu_ref, pv_ref, dm_ref, lng_ref, lnb_ref, ws_ref, bs_ref, dp_ref, dws_ref, dbs_ref, dlng_ref, dlnb_ref, dvn_ref):
        @pl.when(pl.program_id(0) == 0)
        def _():
            dws_ref[...] = jnp.zeros_like(dws_ref)
            dbs_ref[...] = jnp.zeros_like(dbs_ref)
            dlng_ref[...] = jnp.zeros_like(dlng_ref)
            dlnb_ref[...] = jnp.zeros_like(dlnb_ref)

        pu, pv = pu_ref[...], pv_ref[...]
        lng = lng_ref[...]
        u, xhat, rstd, vn = _gmlp_parts(pu, pv, lng, lnb_ref[...])
        dm = dm_ref[...].astype(F32)
        causal = _iota((CHUNK, CHUNK), 0) >= _iota((CHUNK, CHUNK), 1)
        for g in range(A_GROUPS):
            sl = slice(g * A_GW, (g + 1) * A_GW)
            w = jnp.where(causal, ws_ref[g], 0.0)
            sv = _dot(w, vn[:, sl]) + bs_ref[g]
            dsv = dm[:, sl] * u[:, sl]
            dp_ref[:, sl] = (dm[:, sl] * sv * _gelu_grad(pu[:, sl])).astype(dp_ref.dtype)
            dvn_ref[:, sl] = _dot_tn(w, dsv)
            dws_ref[g] += jnp.where(causal, _dot_nt(dsv, vn[:, sl]), 0.0)
            dbs_ref[g] += jnp.sum(dsv, axis=-1, keepdims=True)
        dvn = dvn_ref[...]
        dlng_ref[...] += jnp.sum(dvn * xhat, axis=0, keepdims=True)
        dlnb_ref[...] += jnp.sum(dvn, axis=0, keepdims=True)
        dxh = dvn * lng
        dv = rstd * (dxh - jnp.mean(dxh, axis=-1, keepdims=True) - xhat * jnp.mean(dxh * xhat, axis=-1, keepdims=True))
        dp_ref[:, D_INNER:] = (dv * _gelu_grad(pv)).astype(dp_ref.dtype)

    full = lambda shape: pl.BlockSpec(shape, lambda c: (0,) * len(shape))
    return pl.pallas_call(
        body, name=name, grid=(s // CHUNK,),
        in_specs=[pl.BlockSpec((CHUNK, D_INNER), lambda c: (c, 0)), pl.BlockSpec((CHUNK, D_INNER), lambda c: (c, 1)),
                  pl.BlockSpec((CHUNK, D_INNER), lambda c: (c, 0)),
                  full((1, D_INNER)), full((1, D_INNER)), full((A_GROUPS, CHUNK, CHUNK)), full((A_GROUPS, CHUNK, 1))],
        out_specs=[pl.BlockSpec((CHUNK, 2 * D_INNER), lambda c: (c, 0)), full((A_GROUPS, CHUNK, CHUNK)),
                   full((A_GROUPS, CHUNK, 1)), full((1, D_INNER)), full((1, D_INNER))],
        out_shape=[jax.ShapeDtypeStruct((s, 2 * D_INNER + X_WIDTH), _ACT), jax.ShapeDtypeStruct((A_GROUPS, CHUNK, CHUNK), F32),
                   jax.ShapeDtypeStruct((A_GROUPS, CHUNK, 1), F32), jax.ShapeDtypeStruct((1, D_INNER), F32),
                   jax.ShapeDtypeStruct((1, D_INNER), F32)],
        scratch_shapes=[pltpu.VMEM((CHUNK, D_INNER), F32)],
        compiler_params=_cp(VMEM_BIG))(proj, proj, dcat, lng, lnb, ws, bs3)


_X_SCALE = 1.0 / math.sqrt(X_HD)


def _attn_fwd(proj, qblk, kv, cat, name, tm=256):
    s = proj.shape[0]
    tm = min(tm, s)

    def body(q_ref, kv_ref, cat_ref, o_ref):
        for h in range(X_HEADS):
            sl = slice(h * X_HD, (h + 1) * X_HD)
            k = kv_ref[:, sl]
            v = kv_ref[:, X_WIDTH + h * X_HD:X_WIDTH + (h + 1) * X_HD]
            sc = _dot_nt(q_ref[:, sl], k) * _X_SCALE
            e = jnp.exp(sc - jnp.max(sc, axis=-1, keepdims=True))
            p = e / jnp.sum(e, axis=-1, keepdims=True)
            o_ref[:, sl] = _dot(p, v).astype(o_ref.dtype)

    return pl.pallas_call(
        body, name=name, grid=(s // tm,),
        in_specs=[pl.BlockSpec((tm, X_WIDTH), lambda i: (i, qblk)), pl.BlockSpec((N_MEM, 2 * X_WIDTH), lambda i: (0, 0)),
                  pl.BlockSpec(memory_space=pl.ANY)],
        out_specs=pl.BlockSpec((tm, X_WIDTH), lambda i: (i, D_INNER // X_WIDTH)),
        out_shape=jax.ShapeDtypeStruct(cat.shape, cat.dtype), input_output_aliases={2: 0})(proj, kv, cat)


def _attn_bwd(proj, qblk, kv, dcat, dproj, name, tm=256):
    s = proj.shape[0]
    tm = min(tm, s)

    def body(q_ref, kv_ref, do_ref, dproj_ref, dq_ref, dkv_ref):
        @pl.when(pl.program_id(0) == 0)
        def _():
            dkv_ref[...] = jnp.zeros_like(dkv_ref)

        for h in range(X_HEADS):
            sl = slice(h * X_HD, (h + 1) * X_HD)
            slv = slice(X_WIDTH + h * X_HD, X_WIDTH + (h + 1) * X_HD)
            q = q_ref[:, sl]
            k = kv_ref[:, sl]
            v = kv_ref[:, slv]
            do = do_ref[:, sl].astype(F32)
            sc = _dot_nt(q, k) * _X_SCALE
            e = jnp.exp(sc - jnp.max(sc, axis=-1, keepdims=True))
            p = e / jnp.sum(e, axis=-1, keepdims=True)
            dp = _dot_nt(do, v)
            ds = p * (dp - jnp.sum(dp * p, axis=-1, keepdims=True)) * _X_SCALE
            dq_ref[:, sl] = _dot(ds, k).astype(dq_ref.dtype)
            dkv_ref[:, sl] += _dot_tn(ds, q)
            dkv_ref[:, slv] += _dot_tn(p, do)

    return pl.pallas_call(
        body, name=name, grid=(s // tm,),
        in_specs=[pl.BlockSpec((tm, X_WIDTH), lambda i: (i, qblk)), pl.BlockSpec((N_MEM, 2 * X_WIDTH), lambda i: (0, 0)),
                  pl.BlockSpec((tm, X_WIDTH), lambda i: (i, 2)), pl.BlockSpec(memory_space=pl.ANY)],
        out_specs=[pl.BlockSpec((tm, X_WIDTH), lambda i: (i, qblk)), pl.BlockSpec((N_MEM, 2 * X_WIDTH), lambda i: (0, 0))],
        out_shape=[jax.ShapeDtypeStruct(dproj.shape, dproj.dtype), jax.ShapeDtypeStruct((N_MEM, 2 * X_WIDTH), F32)],
        input_output_aliases={3: 0})(proj, kv, dcat, dproj)


CONV_TC = 256
_XBC_BLK0 = D_INNER // CONV_TC


def _shift_down(x, j):
    if j == 0:
        return x
    return jnp.where(_iota(x.shape, 0) >= j, pltpu.roll(x, j, 0), 0.0)


def _shift_up(x, j):
    if j == 0:
        return x
    n = x.shape[0]
    return jnp.where(_iota(x.shape, 0) < n - j, pltpu.roll(x, n - j, 0), 0.0)


def _conv_fwd(proj, w, b, name):
    s = proj.shape[0]

    def body(x_ref, w_ref, b_ref, o_ref):
        xv = x_ref[...]
        pre = b_ref[...] + w_ref[CONV_K - 1:CONV_K, :] * xv
        for kk in range(CONV_K - 1):
            pre = pre + w_ref[kk:kk + 1, :] * _shift_down(xv, CONV_K - 1 - kk)
        o_ref[...] = pre * _sigmoid(pre)

    return pl.pallas_call(
        body, name=name, grid=(CONV_DIM // CONV_TC,),
        in_specs=[pl.BlockSpec((s, CONV_TC), lambda j: (0, _XBC_BLK0 + j)), pl.BlockSpec((CONV_K, CONV_TC), lambda j: (0, j)),
                  pl.BlockSpec((1, CONV_TC), lambda j: (0, j))],
        out_specs=pl.BlockSpec((s, CONV_TC), lambda j: (0, j)),
        out_shape=jax.ShapeDtypeStruct((s, CONV_DIM), F32), compiler_params=_cp(VMEM_BIG))(proj, w, b)


def _conv_bwd(proj, w, b, dxbc, dproj, name):
    s = proj.shape[0]

    def body(x_ref, w_ref, b_ref, d_ref, dproj_ref, dx_ref, dw_ref, db_ref):
        xv = x_ref[...]
        pre = b_ref[...] + w_ref[CONV_K - 1:CONV_K, :] * xv
        for kk in range(CONV_K - 1):
            pre = pre + w_ref[kk:kk + 1, :] * _shift_down(xv, CONV_K - 1 - kk)
        sig = _sigmoid(pre)
        dpre = d_ref[...] * (sig * (1.0 + pre * (1.0 - sig)))
        dx = w_ref[CONV_K - 1:CONV_K, :] * dpre
        dw_ref[CONV_K - 1:CONV_K, :] = jnp.sum(dpre * xv, axis=0, keepdims=True)
        for kk in range(CONV_K - 1):
            j = CONV_K - 1 - kk
            dx = dx + w_ref[kk:kk + 1, :] * _shift_up(dpre, j)
            dw_ref[kk:kk + 1, :] = jnp.sum(dpre * _shift_down(xv, j), axis=0, keepdims=True)
        dx_ref[...] = dx.astype(dx_ref.dtype)
        db_ref[...] = jnp.sum(dpre, axis=0, keepdims=True)

    return pl.pallas_call(
        body, name=name, grid=(CONV_DIM // CONV_TC,),
        in_specs=[pl.BlockSpec((s, CONV_TC), lambda j: (0, _XBC_BLK0 + j)), pl.BlockSpec((CONV_K, CONV_TC), lambda j: (0, j)),
                  pl.BlockSpec((1, CONV_TC), lambda j: (0, j)), pl.BlockSpec((s, CONV_TC), lambda j: (0, j)),
                  pl.BlockSpec(memory_space=pl.ANY)],
        out_specs=[pl.BlockSpec((s, CONV_TC), lambda j: (0, _XBC_BLK0 + j)), pl.BlockSpec((CONV_K, CONV_TC), lambda j: (0, j)),
                   pl.BlockSpec((1, CONV_TC), lambda j: (0, j))],
        out_shape=[jax.ShapeDtypeStruct(dproj.shape, dproj.dtype), jax.ShapeDtypeStruct((CONV_K, CONV_DIM), F32),
                   jax.ShapeDtypeStruct((1, CONV_DIM), F32)], input_output_aliases={4: 0},
        compiler_params=_cp(VMEM_BIG))(proj, w, b, dxbc, dproj)


def _ssd_common(dtc_ref, dtr_ref, br_ref, bc_ref, ar_ref, ac_ref, csb_ref, cst_ref, csf_ref):
    a_row = -jnp.exp(ar_ref[...])
    dt_c = _softplus(dtc_ref[...] + br_ref[...])
    a_col = -jnp.exp(ac_ref[...])
    dt_r = _softplus(dtr_ref[...] + bc_ref[...])
    row = _iota((CHUNK, CHUNK), 0)
    col = _iota((CHUNK, CHUNK), 1)
    tril = (row >= col).astype(F32)
    triu = (row <= col).astype(F32)
    cs = _dot_hi(tril, dt_c * a_row)
    cst_ref[...] = _dot_hi(dt_r * a_col, triu)
    e64 = (jnp.right_shift(_iota((HPAD, D_INNER), 1), 6) == _iota((HPAD, D_INNER), 0)).astype(F32)
    e128 = (jnp.right_shift(_iota((HPAD, SSM_HEADS * CHUNK), 1), 7) == _iota((HPAD, SSM_HEADS * CHUNK), 0)).astype(F32)
    csb_ref[...] = _dot_hi(cs, e128)
    dt_full = _dot_hi(dt_c, e64)
    csf_ref[...] = _dot_hi(cs, e64)
    cs_full = csf_ref[...]
    cs_last = csf_ref[CHUNK - 1:CHUNK, :]
    e_full = jnp.exp(cs_full)
    f_full = jnp.exp(cs_last - cs_full)
    gamma = jnp.exp(cs_last)
    return a_row, dt_c, cs, dt_full, e_full, f_full, gamma, e64


def _ssd_lambda(csb_ref, cst_ref, h, causal):
    diff = csb_ref[:, h * CHUNK:(h + 1) * CHUNK] - cst_ref[h:h + 1, :]
    return jnp.exp(jnp.where(causal, diff, -1e30))


_SSD_VEC_SPECS = lambda: [pl.BlockSpec((1, HPAD), lambda c: (0, 0)), pl.BlockSpec((HPAD, 1), lambda c: (0, 0)),
                          pl.BlockSpec((1, HPAD), lambda c: (0, 0)), pl.BlockSpec((HPAD, 1), lambda c: (0, 0)),
                          pl.BlockSpec((1, D_INNER), lambda c: (0, 0))]


def _ssd_fwd(xbc, dtc, dtr, bias_row, bias_col, alog_row, alog_col, dfull, name):
    s = xbc.shape[0]
    nc = s // CHUNK

    def body(xbc_ref, dtc_ref, dtr_ref, br_ref, bc_ref, ar_ref, ac_ref, df_ref, y_ref, st_ref,
             ht_ref, csb_ref, cst_ref, csf_ref):
        @pl.when(pl.program_id(0) == 0)
        def _():
            ht_ref[...] = jnp.zeros_like(ht_ref)

        _, _, _, dt_full, e_full, f_full, gamma, _ = _ssd_common(
            dtc_ref, dtr_ref, br_ref, bc_ref, ar_ref, ac_ref, csb_ref, cst_ref, csf_ref)
        x = xbc_ref[:, :D_INNER]
        xdt = x * dt_full
        st_ref[...] = ht_ref[...]
        causal = _iota((CHUNK, CHUNK), 0) >= _iota((CHUNK, CHUNK), 1)
        lo = _iota((CHUNK, CHUNK), 1) < SSM_P
        for g in range(SSM_GROUPS):
            gs = slice(g * SSM_GW, (g + 1) * SSM_GW)
            bg = xbc_ref[:, D_INNER + g * SSM_N:D_INNER + (g + 1) * SSM_N]
            cg = xbc_ref[:, D_INNER + SSM_GROUPS * SSM_N + g * SSM_N:D_INNER + SSM_GROUPS * SSM_N + (g + 1) * SSM_N]
            ht = ht_ref[:, gs]
            cb = _dot_nt(cg, bg)
            yoff = e_full[:, gs] * _dot(cg, ht)
            for jp in range(SSM_GW // CHUNK):
                j = g * (SSM_GW // CHUNK) + jp
                ps = slice(j * CHUNK, (j + 1) * CHUNK)
                x2 = xdt[:, ps]
                y0 = _dot(cb * _ssd_lambda(csb_ref, cst_ref, 2 * j, causal), x2)
                y1 = _dot(cb * _ssd_lambda(csb_ref, cst_ref, 2 * j + 1, causal), x2)
                y_ref[:, ps] = (jnp.where(lo, y0, y1) + yoff[:, jp * CHUNK:(jp + 1) * CHUNK]
                                + x[:, ps] * df_ref[:, ps])
            ht_ref[:, gs] = gamma[:, gs] * ht + _dot_tn(bg, xdt[:, gs] * f_full[:, gs])

    return pl.pallas_call(
        body, name=name, grid=(nc,),
        in_specs=[pl.BlockSpec((CHUNK, CONV_DIM), lambda c: (c, 0)), pl.BlockSpec((CHUNK, HPAD), lambda c: (c, 0)),
                  pl.BlockSpec((HPAD, CHUNK), lambda c: (0, c))] + _SSD_VEC_SPECS(),
        out_specs=[pl.BlockSpec((CHUNK, D_INNER), lambda c: (c, 0)), pl.BlockSpec((None, SSM_N, D_INNER), lambda c: (c, 0, 0))],
        out_shape=[jax.ShapeDtypeStruct((s, D_INNER), F32), jax.ShapeDtypeStruct((nc, SSM_N, D_INNER), F32)],
        scratch_shapes=[pltpu.VMEM((SSM_N, D_INNER), F32), pltpu.VMEM((CHUNK, SSM_HEADS * CHUNK), F32),
                        pltpu.VMEM((HPAD, CHUNK), F32), pltpu.VMEM((CHUNK, D_INNER), F32)],
        compiler_params=_cp(VMEM_BIG))(xbc, dtc, dtr, bias_row, bias_col, alog_row, alog_col, dfull)


def _ssd_bwd(xbc, dtc, dtr, bias_row, bias_col, alog_row, alog_col, dfull, dy, states, name):
    s = xbc.shape[0]
    nc = s // CHUNK
    rev = lambda c: nc - 1 - c

    def body(xbc_ref, dtc_ref, dtr_ref, br_ref, bc_ref, ar_ref, ac_ref, df_ref, dy_ref, st_ref,
             dxbc_ref, ddt_ref, dalog_ref, dd_ref, dbias_ref,
             dht_ref, csb_ref, cst_ref, csf_ref, ddf_ref, dxs_ref, dcsf_ref, dcsl_ref):
        step = pl.program_id(0)

        @pl.when(step == 0)
        def _():
            dht_ref[...] = jnp.zeros_like(dht_ref)
            ddf_ref[...] = jnp.zeros_like(ddf_ref)
            dalog_ref[...] = jnp.zeros_like(dalog_ref)
            dbias_ref[...] = jnp.zeros_like(dbias_ref)
            dd_ref[...] = jnp.zeros_like(dd_ref)

        a_row, dt_c, _, dt_full, e_full, f_full, gamma, e64 = _ssd_common(
            dtc_ref, dtr_ref, br_ref, bc_ref, ar_ref, ac_ref, csb_ref, cst_ref, csf_ref)
        x = xbc_ref[:, :D_INNER]
        xdt = x * dt_full
        dy_all = dy_ref[...]
        ddf_ref[...] += jnp.broadcast_to(jnp.sum(dy_all * x, axis=0, keepdims=True), ddf_ref.shape)
        causal = _iota((CHUNK, CHUNK), 0) >= _iota((CHUNK, CHUNK), 1)
        lo = _iota((CHUNK, CHUNK), 1) < SSM_P
        ones = jnp.ones((CHUNK, HPAD), F32)
        head_lane = _iota((CHUNK, HPAD), 1)
        dcs_heads = jnp.zeros((CHUNK, HPAD), F32)
        for g in range(SSM_GROUPS):
            gs = slice(g * SSM_GW, (g + 1) * SSM_GW)
            b0 = D_INNER + g * SSM_N
            c0 = D_INNER + SSM_GROUPS * SSM_N + g * SSM_N
            bg = xbc_ref[:, b0:b0 + SSM_N]
            cg = xbc_ref[:, c0:c0 + SSM_N]
            ht = st_ref[:, gs]
            dht = dht_ref[:, gs]
            dyg = dy_all[:, gs]
            eg, fg, gg = e_full[:, gs], f_full[:, gs], gamma[:, gs]
            z = _dot(cg, ht)
            dz = dyg * eg
            dcg = _dot_nt(dz, ht)
            dht_new = _dot_tn(cg, dz) + gg * dht
            xf = xdt[:, gs] * fg
            dxf = _dot(bg, dht)
            dbg = _dot_nt(xf, dht)
            dff = dxf * xf
            dcsf_ref[:, gs] = dyg * eg * z - dff
            dcsl_ref[:, gs] = jnp.broadcast_to(
                jnp.sum(dff, axis=0, keepdims=True) + jnp.sum(dht * ht, axis=0, keepdims=True) * gg, (8, SSM_GW))
            cb = _dot_nt(cg, bg)
            dcb = jnp.zeros((CHUNK, CHUNK), F32)
            for jp in range(SSM_GW // CHUNK):
                j = g * (SSM_GW // CHUNK) + jp
                ps = slice(j * CHUNK, (j + 1) * CHUNK)
                x2 = xdt[:, ps]
                dy2 = dy_all[:, ps]
                dxh = []
                for hh in range(2):
                    h = 2 * j + hh
                    lam = _ssd_lambda(csb_ref, cst_ref, h, causal)
                    mh = cb * lam
                    dyh = jnp.where(lo, dy2, 0.0) if hh == 0 else jnp.where(lo, 0.0, dy2)
                    dm = _dot_nt(dyh, x2)
                    dcb = dcb + dm * lam
                    gm = dm * mh
                    rs = jnp.sum(gm, axis=1, keepdims=True)
                    csum = _dot_hi(gm, ones, ((0,), (0,)))
                    dcs_heads = dcs_heads + jnp.where(head_lane == h, rs - csum, 0.0)
                    dxh.append(_dot_tn(mh, dy2))
                dxs_ref[:, ps] = jnp.where(lo, dxh[0], dxh[1]) + dxf[:, jp * CHUNK:(jp + 1) * CHUNK] * fg[:, jp * CHUNK:(jp + 1) * CHUNK]
            dxbc_ref[:, b0:b0 + SSM_N] = (dbg + _dot_tn(dcb, cg)).astype(dxbc_ref.dtype)
            dxbc_ref[:, c0:c0 + SSM_N] = (dcg + _dot(dcb, bg)).astype(dxbc_ref.dtype)
            dht_ref[:, gs] = dht_new
        dxs = dxs_ref[...]
        dcs_heads = dcs_heads + _dot_hi(dcsf_ref[...], e64, ((1,), (1,)))
        dcs_last = _dot_hi(dcsl_ref[...], e64, ((1,), (1,)))
        dcs_heads = dcs_heads + jnp.where(_iota((CHUNK, HPAD), 0) == CHUNK - 1, dcs_last[0:1, :], 0.0)
        triu = (_iota((CHUNK, CHUNK), 0) <= _iota((CHUNK, CHUNK), 1)).astype(F32)
        dda = _dot_hi(triu, dcs_heads)
        ddt = dda * a_row + _dot_hi(dxs * x, e64, ((1,), (1,)))
        dxbc_ref[:, :D_INNER] = (dxs * dt_full + dy_all * df_ref[...]).astype(dxbc_ref.dtype)
        dalog_ref[...] += jnp.sum(dda * dt_c, axis=0, keepdims=True) * a_row
        ddt_raw = ddt * _sigmoid(dtc_ref[...] + br_ref[...])
        ddt_ref[...] = ddt_raw.astype(ddt_ref.dtype)
        dbias_ref[...] += jnp.sum(ddt_raw, axis=0, keepdims=True)

        @pl.when(step == nc - 1)
        def _():
            dd_ref[...] = _dot_hi(ddf_ref[...], e64, ((1,), (1,)))[0:1, :]

    vec = pl.BlockSpec((1, HPAD), lambda c: (0, 0))
    return pl.pallas_call(
        body, name=name, grid=(nc,),
        in_specs=[pl.BlockSpec((CHUNK, CONV_DIM), lambda c: (rev(c), 0)), pl.BlockSpec((CHUNK, HPAD), lambda c: (rev(c), 0)),
                  pl.BlockSpec((HPAD, CHUNK), lambda c: (0, rev(c)))] + _SSD_VEC_SPECS()
                 + [pl.BlockSpec((CHUNK, D_INNER), lambda c: (rev(c), 0)),
                    pl.BlockSpec((None, SSM_N, D_INNER), lambda c: (rev(c), 0, 0))],
        out_specs=[pl.BlockSpec((CHUNK, CONV_DIM), lambda c: (rev(c), 0)), pl.BlockSpec((CHUNK, HPAD), lambda c: (rev(c), 0)),
                   vec, vec, vec],
        out_shape=[jax.ShapeDtypeStruct((s, CONV_DIM), F32), jax.ShapeDtypeStruct((s, HPAD), _ACT),
                   jax.ShapeDtypeStruct((1, HPAD), F32), jax.ShapeDtypeStruct((1, HPAD), F32),
                   jax.ShapeDtypeStruct((1, HPAD), F32)],
        scratch_shapes=[pltpu.VMEM((SSM_N, D_INNER), F32), pltpu.VMEM((CHUNK, SSM_HEADS * CHUNK), F32),
                        pltpu.VMEM((HPAD, CHUNK), F32), pltpu.VMEM((CHUNK, D_INNER), F32),
                        pltpu.VMEM((8, D_INNER), F32), pltpu.VMEM((CHUNK, D_INNER), F32),
                        pltpu.VMEM((CHUNK, D_INNER), F32), pltpu.VMEM((8, D_INNER), F32)],
        compiler_params=_cp(VMEM_BIG))(xbc, dtc, dtr, bias_row, bias_col, alog_row, alog_col, dfull, dy, states)


def _gate_fwd(y, proj, gn, name, tm=256):
    s = y.shape[0]
    tm = min(tm, s)

    def body(y_ref, z_ref, gn_ref, o_ref):
        for g in range(SSM_GROUPS):
            gs = slice(g * SSM_GW, (g + 1) * SSM_GW)
            z = z_ref[:, gs]
            t = y_ref[:, gs] * (z * _sigmoid(z))
            r = lax.rsqrt(jnp.mean(t * t, axis=-1, keepdims=True) + EPS)
            o_ref[:, gs] = (t * r * gn_ref[:, gs]).astype(o_ref.dtype)

    row = pl.BlockSpec((tm, D_INNER), lambda i: (i, 0))
    return pl.pallas_call(
        body, name=name, grid=(s // tm,), in_specs=[row, row, pl.BlockSpec((1, D_INNER), lambda i: (0, 0))],
        out_specs=row, out_shape=jax.ShapeDtypeStruct((s, D_INNER + X_WIDTH), _ACT))(y, proj, gn)


def _gate_bwd(y, proj, gn, dcat, name, tm=256):
    s = y.shape[0]
    tm = min(tm, s)

    def body(y_ref, z_ref, gn_ref, dm_ref, dy_ref, dz_ref, dgn_ref):
        @pl.when(pl.program_id(0) == 0)
        def _():
            dgn_ref[...] = jnp.zeros_like(dgn_ref)

        for g in range(SSM_GROUPS):
            gs = slice(g * SSM_GW, (g + 1) * SSM_GW)
            z = z_ref[:, gs]
            yv = y_ref[:, gs]
            sig = _sigmoid(z)
            sz = z * sig
            t = yv * sz
            r = lax.rsqrt(jnp.mean(t * t, axis=-1, keepdims=True) + EPS)
            th = t * r
            dm = dm_ref[:, gs].astype(F32)
            dmg = dm * gn_ref[:, gs]
            dt_ = r * (dmg - th * jnp.mean(dmg * th, axis=-1, keepdims=True))
            dgn_ref[:, gs] += jnp.sum(dm * th, axis=0, keepdims=True)
            dy_ref[:, gs] = dt_ * sz
            dz_ref[:, gs] = (dt_ * yv * (sig * (1.0 + z * (1.0 - sig)))).astype(dz_ref.dtype)

    row = pl.BlockSpec((tm, D_INNER), lambda i: (i, 0))
    vec = pl.BlockSpec((1, D_INNER), lambda i: (0, 0))
    return pl.pallas_call(
        body, name=name, grid=(s // tm,), in_specs=[row, row, vec, row], out_specs=[row, row, vec],
        out_shape=[jax.ShapeDtypeStruct((s, D_INNER), F32), jax.ShapeDtypeStruct((s, 6 * D_MODEL), _ACT),
                   jax.ShapeDtypeStruct((1, D_INNER), F32)])(y, proj, gn, dcat)


def _block_of(kind, width):
    if kind == "col":
        return lambda ref, j: ref.at[:, :, pl.ds(pl.multiple_of(j * width, 128), width)]
    if kind == "row":
        return lambda ref, j: ref.at[:, pl.ds(pl.multiple_of(j * width, 8), width), :]
    return lambda ref, j: ref.at[j]


def _coords():
    return lax.axis_index("x"), lax.axis_index("y"), lax.axis_index("c")


def _rel_chip(x, y, k):
    return (1 - x if k & 1 else x), (1 - y if k & 2 else y)


_HBM = lambda: pl.BlockSpec(memory_space=pltpu.HBM)


def _all_gather(shards, layouts, name):
    n = len(shards)
    blocks = [_block_of(kind, width) for kind, width, _ in layouts]

    def body(*refs):
        _all_gather_body(refs[:n], refs[n:2 * n], *refs[2 * n:], blocks)

    return pl.pallas_call(
        body, name=name, in_specs=[_HBM()] * n, out_specs=[_HBM()] * n,
        out_shape=[jax.ShapeDtypeStruct(shape, sh.dtype) for sh, (_, _, shape) in zip(shards, layouts)],
        scratch_shapes=[pltpu.SemaphoreType.DMA((n, 7)), pltpu.SemaphoreType.DMA((n, 7)), pltpu.SemaphoreType.DMA((n,))])(*shards)


def _all_gather_body(ins, outs, send_sems, recv_sems, local_sems, blocks):
    n = len(ins)
    x, y, c = _coords()
    sibling = (x, y, 1 - c)

    def copy(t, k, chip, core, to, src=None):
        dst = blocks[t](outs[t], 4 * chip[0] + 2 * chip[1] + core)
        return pltpu.make_async_remote_copy(
            src_ref=dst if src is None else src, dst_ref=dst, send_sem=send_sems.at[t, k],
            recv_sem=recv_sems.at[t, k], device_id=to, device_id_type=MESH)

    started = []
    for t in range(n):
        mine = pltpu.make_async_copy(ins[t], blocks[t](outs[t], 4 * x + 2 * y + c), local_sems.at[t])
        mine.start()
        started.append(mine)
    sends = []
    for t in range(n):
        for k in range(4):
            px, py = _rel_chip(x, y, k)
            cp = copy(t, k, (x, y), c, (px, py, 1 - c if k == 0 else c), src=ins[t])
            cp.start()
            sends.append(cp)
    for t in range(n):
        for k in range(1, 4):
            chip = _rel_chip(x, y, k)
            copy(t, k, chip, c, sibling).wait_recv()
            fwd = copy(t, 3 + k, chip, c, sibling)
            fwd.start()
            sends.append(fwd)
    for t in range(n):
        copy(t, 0, (x, y), 1 - c, sibling).wait_recv()
        for k in range(1, 4):
            copy(t, 3 + k, _rel_chip(x, y, k), 1 - c, sibling).wait_recv()
    for cp in sends:
        cp.wait_send()
    for mine in started:
        mine.wait()


def _handshake(peers):
    barrier = pltpu.get_barrier_semaphore()
    for peer in peers:
        pl.semaphore_signal(barrier, inc=1, device_id=peer, device_id_type=MESH)
    pl.semaphore_wait(barrier, len(peers))


def _two_level_peers():
    x, y, c = _coords()
    return [(x, y, 1 - c)] + [(*_rel_chip(x, y, k), c) for k in range(1, 4)]


SEQ_ID_GATHER, SEQ_ID_SIBLING, SEQ_ID_CHIPS = 1, 2, 3


def _sequencer_call(body, peers, operands, out_types, sems, name, collective_id, after=()):
    n_in, n_out, n_after = len(operands), len(out_types), len(after)

    def launch(*refs):
        _handshake(peers())
        body(refs[:n_in], refs[n_in + n_after:n_in + n_after + n_out], *refs[n_in + n_after + n_out:])

    return pl.kernel(
        launch, name=name, out_type=out_types, mesh=plsc.ScalarSubcoreMesh(axis_name="seq", num_cores=1),
        scratch_types=sems, compiler_params=pltpu.CompilerParams(collective_id=collective_id))(*operands, *after)


def _all_gather_seq(shards, layouts, name, after=()):
    n = len(shards)
    blocks = [_block_of(kind, width) for kind, width, _ in layouts]
    return _sequencer_call(
        lambda ins, outs, *sems: _all_gather_body(ins, outs, *sems, blocks), _two_level_peers, shards,
        [jax.ShapeDtypeStruct(shape, sh.dtype) for sh, (_, _, shape) in zip(shards, layouts)],
        [pltpu.SemaphoreType.DMA((n, 7)), pltpu.SemaphoreType.DMA((n, 7)), pltpu.SemaphoreType.DMA((n,))],
        name, SEQ_ID_GATHER, after)


def _tie(small, after, name):
    def body(*refs):
        refs[-1][...] = refs[0][...]

    vmem = pl.BlockSpec(memory_space=pltpu.VMEM)
    return pl.pallas_call(
        body, name=name, in_specs=[vmem] + [pl.BlockSpec(memory_space=pl.ANY)] * len(after), out_specs=vmem,
        out_shape=jax.ShapeDtypeStruct(small.shape, small.dtype))(small, *after)


def _rs_to_sibling(grads, layouts, name):
    n = len(grads)
    blocks = [_block_of(kind, width) for kind, width, _ in layouts]

    def body(ins, outs, send_sems, recv_sems):
        x, y, c = _coords()
        sibling = (x, y, 1 - c)
        cps = []
        for t in range(n):
            for k in range(4):
                px, py = _rel_chip(x, y, k)
                cp = pltpu.make_async_remote_copy(
                    src_ref=blocks[t](ins[t], 4 * px + 2 * py + (1 - c)), dst_ref=outs[t].at[k],
                    send_sem=send_sems.at[t, k], recv_sem=recv_sems.at[t, k], device_id=sibling, device_id_type=MESH)
                cp.start()
                cps.append(cp)
        for cp in cps:
            cp.wait_recv()
        for cp in cps:
            cp.wait_send()

    def sibling_only():
        x, y, c = _coords()
        return [(x, y, 1 - c)]

    return _sequencer_call(
        body, sibling_only, grads,
        [jax.ShapeDtypeStruct((4,) + shape, g.dtype) for g, (_, _, shape) in zip(grads, layouts)],
        [pltpu.SemaphoreType.DMA((n, 4)), pltpu.SemaphoreType.DMA((n, 4))], name, SEQ_ID_SIBLING)


def _rs_chip_sum(grad, recv, layout, xyc, name):
    kind, width, shape = layout
    r, ccols = shape

    def src_index(k, xyc_ref):
        px = jnp.where(k % 2 == 1, 1 - xyc_ref[0], xyc_ref[0])
        py = jnp.where(k // 2 == 1, 1 - xyc_ref[1], xyc_ref[1])
        return 4 * px + 2 * py + xyc_ref[2]

    if kind == "col":
        g_spec = pl.BlockSpec((r, ccols), lambda k, s_: (0, src_index(k, s_)))
    elif kind == "row":
        g_spec = pl.BlockSpec((r, ccols), lambda k, s_: (src_index(k, s_), 0))
    else:
        g_spec = pl.BlockSpec((None, r, ccols), lambda k, s_: (src_index(k, s_), 0, 0))

    def body(xyc_ref, g_ref, r_ref, o_ref):
        o_ref[...] = (g_ref[...].astype(F32) + r_ref[...].astype(F32)).astype(o_ref.dtype)

    slot = pl.BlockSpec((None, r, ccols), lambda k, s_: (k, 0, 0))
    return pl.pallas_call(
        body, name=name,
        grid_spec=pltpu.PrefetchScalarGridSpec(num_scalar_prefetch=1, grid=(4,), in_specs=[g_spec, slot], out_specs=slot),
        out_shape=jax.ShapeDtypeStruct((4, r, ccols), grad.dtype), compiler_params=_cp(VMEM_BIG))(xyc, grad, recv)


def _rs_across_chips(parts, name):
    n = len(parts)

    def body(ins, outs, send_sems, recv_sems):
        x, y, c = _coords()
        cps = []
        for t in range(n):
            for k in range(1, 4):
                px, py = _rel_chip(x, y, k)
                cp = pltpu.make_async_remote_copy(
                    src_ref=ins[t].at[k], dst_ref=outs[t].at[k - 1], send_sem=send_sems.at[t, k - 1],
                    recv_sem=recv_sems.at[t, k - 1], device_id=(px, py, c), device_id_type=MESH)
                cp.start()
                cps.append(cp)
        for cp in cps:
            cp.wait_recv()
        for cp in cps:
            cp.wait_send()

    def other_chips():
        x, y, c = _coords()
        return [(*_rel_chip(x, y, k), c) for k in range(1, 4)]

    return _sequencer_call(
        body, other_chips, parts, [jax.ShapeDtypeStruct((3,) + p.shape[1:], p.dtype) for p in parts],
        [pltpu.SemaphoreType.DMA((n, 3)), pltpu.SemaphoreType.DMA((n, 3))], name, SEQ_ID_CHIPS)


def _adamw_math(w, g, m, v):
    m = ADAM_B1 * m + (1.0 - ADAM_B1) * g
    v = ADAM_B2 * v + (1.0 - ADAM_B2) * jnp.square(g)
    m_hat = m / (1.0 - ADAM_B1 ** ADAM_STEP)
    v_hat = v / (1.0 - ADAM_B2 ** ADAM_STEP)
    delta = -ADAM_LR * (m_hat / (jnp.sqrt(v_hat) + ADAM_EPS) + ADAM_WD * w)
    return delta, m, v


def _row_tile(rows, cap):
    best = None
    for cand in range(8, min(rows, cap) + 1, 8):
        if rows % cand == 0:
            best = cand
    assert best is not None, rows
    return best


def _adamw(w, m, v, parts, name, layer=None, prev=None, tr=256):
    r, ccols = w.shape[-2:]
    tr = _row_tile(r, tr)
    npart = len(parts)

    def wspec():
        if layer is None:
            return pl.BlockSpec((tr, ccols), lambda i: (i, 0))
        return pl.BlockSpec((None, tr, ccols), lambda i: (layer, i, 0))

    def pspec(lead):
        if lead is None:
            return pl.BlockSpec((tr, ccols), lambda i: (i, 0))
        return pl.BlockSpec((None, tr, ccols), lambda i: (lead, i, 0))

    def body(*refs):
        w_ref, m_ref, v_ref = refs[:3]
        p_refs = refs[3:3 + npart]
        outs = refs[len(refs) - 4:]
        g = p_refs[0][...].astype(F32)
        for p_ref in p_refs[1:]:
            g = g + p_ref[...].astype(F32)
        delta, mn, vn = _adamw_math(w_ref[...], g, m_ref[...], v_ref[...])
        outs[0][...] = g
        outs[1][...] = delta
        outs[2][...] = mn
        outs[3][...] = vn

    operands = [w, m, v] + [p for p, _ in parts]
    in_specs = [wspec(), wspec(), wspec()] + [pspec(lead) for _, lead in parts]
    aliases = {}
    if prev is not None:
        for i, p in enumerate(prev):
            aliases[len(operands)] = i
            operands.append(p)
            in_specs.append(pl.BlockSpec(memory_space=pl.ANY))
    return pl.pallas_call(
        body, name=name, grid=(r // tr,), in_specs=in_specs, out_specs=[wspec()] * 4,
        out_shape=[jax.ShapeDtypeStruct(w.shape, F32)] * 4, input_output_aliases=aliases)(*operands)


def _sum8(buf, name):
    _, r, ccols = buf.shape

    def body(b_ref, o_ref):
        acc = b_ref[0]
        for j in range(1, N_DEV):
            acc = acc + b_ref[j]
        o_ref[...] = acc

    tr = _row_tile(r, 256)
    return pl.pallas_call(
        body, name=name, grid=(r // tr,), in_specs=[pl.BlockSpec((N_DEV, tr, ccols), lambda i: (0, i, 0))],
        out_specs=pl.BlockSpec((tr, ccols), lambda i: (i, 0)), out_shape=jax.ShapeDtypeStruct((r, ccols), F32))(buf)


def _pack(arrays):
    pieces, layout, off = [], [], 0
    for a in arrays:
        n = a.size
        padded = -(-n // 1024) * 1024
        flat = a.reshape(-1).astype(F32)
        if padded != n:
            flat = jnp.pad(flat, (0, padded - n))
        pieces.append(flat.reshape(padded // 128, 128))
        layout.append((off, n, a.shape))
        off += padded // 128
    return jnp.concatenate(pieces, axis=0), layout


def _unpack(packed, layout):
    out = []
    for off, n, shape in layout:
        rows = -(-n // 1024) * 8
        out.append(packed[off:off + rows].reshape(-1)[:n].reshape(shape))
    return out


def kernel(x, mem, norm_mix, norm_ffn, mem_norm, w_kv, w_out, w_ffn1, w_ffn2, a_in, a_ln_g, a_ln_b, a_ws, a_bs, b_in, b_conv_w, b_conv_b, b_dt_bias, b_a_log, b_d, b_gnorm, final_norm, loss_target, m_norm_mix, m_norm_ffn, m_mem_norm, m_w_kv, m_w_out, m_w_ffn1, m_w_ffn2, m_a_in, m_a_ln_g, m_a_ln_b, m_a_ws, m_a_bs, m_b_in, m_b_conv_w, m_b_conv_b, m_b_dt_bias, m_b_a_log, m_b_d, m_b_gnorm, m_final_norm, v_norm_mix, v_norm_ffn, v_mem_norm, v_w_kv, v_w_out, v_w_ffn1, v_w_ffn2, v_a_in, v_a_ln_g, v_a_ln_b, v_a_ws, v_a_bs, v_b_in, v_b_conv_w, v_b_conv_b, v_b_dt_bias, v_b_a_log, v_b_d, v_b_gnorm, v_final_norm):
    s = x.shape[1]
    xs = x.reshape(s, D_MODEL)
    mems = mem.reshape(N_MEM, D_MODEL)
    target = loss_target.reshape(s, D_MODEL)
    ax, ay, ac = lax.axis_index("x"), lax.axis_index("y"), lax.axis_index("c")
    me = 4 * ax + 2 * ay + ac
    xyc = jnp.stack([ax, ay, ac]).astype(jnp.int32)

    b_cols = b_in.shape[2]
    act = lambda a: a.astype(_ACT)
    lay_f1, lay_f2 = ("col", 512, (1, D_MODEL, D_FF)), ("row", 512, (1, D_FF, D_MODEL))
    lay_out, lay_kv = ("row", 384, (1, 3 * D_MODEL, D_MODEL)), ("col", 256, (1, D_MODEL, 2 * X_WIDTH))
    small_w_pack = _pack([b_conv_w[0], b_conv_b[0], b_gnorm[0]])[0]
    WA, wkv0 = _all_gather_seq([act(a_in), act(w_kv[0:1])], [("col", 640, (1, D_MODEL, 5 * D_MODEL)), lay_kv], "ag_proj_a")
    (wo0,) = _all_gather_seq([act(w_out[0:1])], [lay_out], "ag_out0")
    w1_0, w2_0 = _all_gather_seq([act(w_ffn1[0:1]), act(w_ffn2[0:1])], [lay_f1, lay_f2], "ag_ffn0")
    wb_blk, wkv1, small_w = _all_gather_seq(
        [act(b_in[0]), act(w_kv[1:2]), small_w_pack],
        [("blk", 0, (N_DEV, D_MODEL, b_cols)), lay_kv, ("blk", 0, (N_DEV, 32, 128))], "ag_proj_b")
    (wo1,) = _all_gather_seq([act(w_out[1:2])], [lay_out], "ag_out1")
    w1_1, w2_1 = _all_gather_seq([act(w_ffn1[1:2]), act(w_ffn2[1:2])], [lay_f1, lay_f2], "ag_ffn1")
    W1, W2, WO, WKV = [w1_0, w1_1], [w2_0, w2_1], [wo0, wo1], [wkv0, wkv1]
    wb_full = jnp.transpose(wb_blk, (1, 0, 2)).reshape(D_MODEL, N_DEV * b_cols)
    dt0 = D_INNER + CONV_DIM
    WB = jnp.concatenate([wb_full[:, :dt0], wb_full[:, dt0 + SSM_HEADS:]], axis=1)
    WBDT = jnp.pad(wb_full[:, dt0:dt0 + SSM_HEADS], ((0, 0), (0, HPAD - SSM_HEADS)))

    row = lambda a: a.reshape(1, -1)
    nmix = [row(norm_mix[0]), row(norm_mix[1])]
    nffn = [row(norm_ffn[0]), row(norm_ffn[1])]
    nmem = [row(mem_norm[0]), row(mem_norm[1])]
    fin = row(final_norm)
    lng, lnb = a_ln_g.reshape(1, D_INNER), a_ln_b.reshape(1, D_INNER)
    ws = a_ws[0]
    bs3 = a_bs[0].reshape(A_GROUPS, CHUNK, 1)
    pad_h = lambda a: jnp.pad(a.reshape(-1), (0, HPAD - SSM_HEADS))
    bias_row, bias_col = pad_h(b_dt_bias).reshape(1, HPAD), pad_h(b_dt_bias).reshape(HPAD, 1)
    alog_row, alog_col = pad_h(b_a_log).reshape(1, HPAD), pad_h(b_a_log).reshape(HPAD, 1)
    dfull = jnp.repeat(b_d.reshape(-1), SSM_P).reshape(1, D_INNER)

    cw_sh, cb_sh, gn_sh = 4 * 384, 384, 256
    sw = small_w.reshape(N_DEV, 32 * 128)
    conv_w = jnp.transpose(sw[:, :cw_sh].reshape(N_DEV, CONV_K, 384), (1, 0, 2)).reshape(CONV_K, CONV_DIM)
    conv_b = sw[:, 2048:2048 + cb_sh].reshape(1, CONV_DIM)
    gnorm = sw[:, 3072:3072 + gn_sh].reshape(1, D_INNER)

    kvs, mns = [None, None], [None, None]

    def mem_kv(i, after=None):
        gain = nmem[i] if after is None else _tie(nmem[i], after, f"tie_mem{i}")
        mns[i] = _rms_fwd(mems, gain, f"mem_norm{i}")
        kvs[i] = _mm(mns[i], WKV[i], m=N_MEM, n=2 * X_WIDTH, k=D_MODEL, b_at=(0, 0, 0), out_dtype=_ACT, name=f"kv{i}")

    def ffn_fwd(h, i):
        f = _rms_fwd(h, nffn[i], f"ffn_norm{i}")
        p = _mm(f, W1[i], m=s, n=D_FF, k=D_MODEL, b_at=(0, 0, 0), out_dtype=_ACT, name=f"ffn_up{i}")
        hn = _mm(p, W2[i], m=s, n=D_MODEL, k=D_FF, b_at=(0, 0, 0), a_pro="relu2", add=h, name=f"ffn_down{i}")
        return f, p, hn

    def out_proj(h, cat, i):
        return _mm(cat, WO[i], m=s, n=D_MODEL, k=3 * D_MODEL, b_at=(0, 0, 0), add=h, name=f"out_proj{i}")

    a0 = _rms_fwd(xs, nmix[0], "mix_norm0")
    proj_a = _mm(a0, WA, m=s, n=5 * D_MODEL, k=D_MODEL, b_at=(0, 0, 0), name="proj_a")
    mem_kv(0)
    cat_a = _gmlp_fwd(proj_a, lng, lnb, ws, bs3, "gmlp_fwd")
    cat_a = _attn_fwd(proj_a, 4, kvs[0], cat_a, "attn_fwd0")
    h1 = out_proj(xs, cat_a, 0)
    f0, p0, h2 = ffn_fwd(h1, 0)

    a1 = _rms_fwd(h2, nmix[1], "mix_norm1")
    proj_b = _mm(a1, WB, m=s, n=6 * D_MODEL, k=D_MODEL, name="proj_b")
    dt_raw = _mm(a1, WBDT, m=s, n=HPAD, k=D_MODEL, name="proj_dt")
    dt_raw_t = dt_raw.T
    xbc = _conv_fwd(proj_b, conv_w, conv_b, "conv_fwd")
    y_ssd, states = _ssd_fwd(xbc, dt_raw, dt_raw_t, bias_row, bias_col, alog_row, alog_col, dfull, "ssd_fwd")
    cat_b = _gate_fwd(y_ssd, proj_b, gnorm, "gate_fwd")
    mem_kv(1, after=[cat_b])
    cat_b = _attn_fwd(proj_b, 5, kvs[1], cat_b, "attn_fwd1")
    h3 = out_proj(h2, cat_b, 1)
    f1, p1, h4 = ffn_fwd(h3, 1)

    loss_part, dh, dh_act, d_fin = _loss_head(h4, fin, target, "loss_head")

    g_f1, g_f2, g_out, g_kv = [None, None], [None, None], [None, None], [None, None]
    d_nffn, d_nmix, d_nmem = [None, None], [None, None], [None, None]

    def ffn_bwd(dh, dh_act, h_in, f, p, i, after=()):
        dp = _mm(dh_act, W2[i], m=s, n=D_FF, k=D_MODEL, tb=True, b_at=(0, 0, 0), epi_p=p, out_dtype=_ACT, name=f"ffn_down_dx{i}")
        g_f2[i] = _mm(p, dh_act, m=D_FF, n=D_MODEL, k=s, ta=True, a_pro="relu2", out_dtype=_ACT, name=f"ffn_down_dw{i}")
        g_f1[i] = _mm(f, dp, m=D_MODEL, n=D_FF, k=s, ta=True, out_dtype=_ACT, name=f"ffn_up_dw{i}")
        df = _mm(dp, W1[i], m=s, n=D_MODEL, k=D_FF, tb=True, b_at=(0, 0, 0), after=after, name=f"ffn_up_dx{i}")
        dh_in, dh_in_act, d_nffn[i] = _rms_bwd(h_in, nffn[i], df, dh, f"ffn_norm_bwd{i}")
        return dh_in, dh_in_act

    def out_bwd(dh_act, cat, i):
        dcat = _mm(dh_act, WO[i], m=s, n=3 * D_MODEL, k=D_MODEL, tb=True, b_at=(0, 0, 0), out_dtype=_ACT, name=f"out_dx{i}")
        g_out[i] = _mm(cat, dh_act, m=3 * D_MODEL, n=D_MODEL, k=s, ta=True, out_dtype=_ACT, name=f"out_dw{i}")
        return dcat

    def mem_bwd(dkv, i):
        g_kv[i] = _mm(mns[i], dkv, m=D_MODEL, n=2 * X_WIDTH, k=N_MEM, ta=True, out_dtype=_ACT, name=f"kv_dw{i}")
        dmn = _mm(dkv, WKV[i], m=N_MEM, n=D_MODEL, k=2 * X_WIDTH, tb=True, b_at=(0, 0, 0), name=f"kv_dx{i}")
        _, _, d_nmem[i] = _rms_bwd(mems, nmem[i], dmn, None, f"mem_norm_bwd{i}")

    lay_g = {"f1": ("col", 512, (D_MODEL, 512)), "f2": ("row", 512, (512, D_MODEL)), "out": ("row", 384, (384, D_MODEL)),
             "kv": ("col", 256, (D_MODEL, 256)), "a": ("col", 640, (D_MODEL, 640)), "b": ("blk", 0, (D_MODEL, b_cols))}
    reduced = {}

    def reduce_scatter(group, tag):
        grads3, lays3 = [], []
        for fam, _, g in group:
            kind, width, shape = lay_g[fam]
            grads3.append(g if kind == "blk" else g.reshape((1,) + g.shape))
            lays3.append((kind, width, shape if kind == "blk" else (1,) + shape))
        recv1 = _rs_to_sibling(grads3, lays3, f"rs_sibling_{tag}")
        parts = [_rs_chip_sum(g, recv1[t].reshape((4,) + lay_g[fam][2]), lay_g[fam], xyc, f"rs_chip_sum_{fam}{i}")
                 for t, (fam, i, g) in enumerate(group)]
        recv2 = _rs_across_chips(parts, f"rs_chips_{tag}")
        for (fam, i, _), p, r2 in zip(group, parts, recv2):
            reduced[fam, i] = (p, r2)
        return parts, recv2

    dh3, dh3_act = ffn_bwd(dh, dh_act, h3, f1, p1, 1)
    dcat_b = out_bwd(dh3_act, cat_b, 1)
    sums, got_ffn1 = reduce_scatter([("f1", 1, g_f1[1]), ("f2", 1, g_f2[1]), ("out", 1, g_out[1])], "ffn1")
    dy_ssd, dproj_b, d_gnorm = _gate_bwd(y_ssd, proj_b, gnorm, dcat_b, "gate_bwd")
    dproj_b, dkv_b = _attn_bwd(proj_b, 5, kvs[1], dcat_b, dproj_b, "attn_bwd1")
    mem_bwd(dkv_b, 1)
    dxbc, ddt_raw, d_alog, d_dskip, d_dtbias = _ssd_bwd(
        xbc, dt_raw, dt_raw_t, _tie(bias_row, sums, "tie_ffn1"), bias_col, alog_row, alog_col, dfull, dy_ssd, states, "ssd_bwd")
    dproj_b, d_convw, d_convb = _conv_bwd(proj_b, conv_w, conv_b, dxbc, dproj_b, "conv_bwd")
    gb = _mm(a1, dproj_b, m=D_MODEL, n=6 * D_MODEL, k=s, ta=True, out_dtype=_ACT, name="proj_b_dw")
    gb_dt = _mm(a1, ddt_raw, m=D_MODEL, n=HPAD, k=s, ta=True, out_dtype=_ACT, name="proj_b_dw_dt")
    gb_full = jnp.concatenate([gb[:, :dt0], gb_dt[:, :SSM_HEADS], gb[:, dt0:]], axis=1)
    gb_blk = jnp.transpose(gb_full.reshape(D_MODEL, N_DEV, b_cols), (1, 0, 2))
    sums, got_mix1 = reduce_scatter([("kv", 1, g_kv[1]), ("b", 0, gb_blk)], "mix1")
    da1 = _mm(dproj_b, WB, m=s, n=D_MODEL, k=6 * D_MODEL, tb=True, name="proj_b_dx")
    da1 = _mm(ddt_raw, WBDT, m=s, n=D_MODEL, k=HPAD, tb=True, add=da1, name="proj_b_dx_dt")
    dh2, dh2_act, d_nmix[1] = _rms_bwd(h2, _tie(nmix[1], sums, "tie_mix1"), da1, dh3, "mix_norm_bwd1")

    dh1, dh1_act = ffn_bwd(dh2, dh2_act, h1, f0, p0, 0, after=got_ffn1)
    dcat_a = out_bwd(dh1_act, cat_a, 0)
    sums, _ = reduce_scatter([("f1", 0, g_f1[0]), ("f2", 0, g_f2[0]), ("out", 0, g_out[0])], "ffn0")
    dproj_a, d_ws, d_bs3, d_lng, d_lnb = _gmlp_bwd(proj_a, dcat_a, _tie(lng, got_mix1, "tie_got_mix1"), lnb, ws, bs3, "gmlp_bwd")
    dproj_a, dkv_a = _attn_bwd(proj_a, 4, kvs[0], dcat_a, dproj_a, "attn_bwd0")
    mem_bwd(dkv_a, 0)
    ga = _mm(a0, dproj_a, m=D_MODEL, n=5 * D_MODEL, k=s, ta=True, out_dtype=_ACT, after=sums, name="proj_a_dw")
    sums, _ = reduce_scatter([("kv", 0, g_kv[0]), ("a", 0, ga)], "mix0")
    da0 = _mm(dproj_a, WA, m=s, n=D_MODEL, k=5 * D_MODEL, tb=True, b_at=(0, 0, 0), name="proj_a_dx")
    grad_x, _, d_nmix[0] = _rms_bwd(xs, _tie(nmix[0], sums, "tie_mix0"), da0, dh1, "mix_norm_bwd0")

    def big_update(w, m, v, fam, nlayer):
        res = None
        for i in range(nlayer):
            part, recv2 = reduced[fam, i]
            plist = [(part, 0), (recv2, 0), (recv2, 1), (recv2, 2)]
            res = _adamw(w, m, v, plist, f"adamw_{fam}{i}", layer=i, prev=res)
        return res

    r_f1 = big_update(w_ffn1, m_w_ffn1, v_w_ffn1, "f1", 2)
    r_f2 = big_update(w_ffn2, m_w_ffn2, v_w_ffn2, "f2", 2)
    r_out = big_update(w_out, m_w_out, v_w_out, "out", 2)
    r_kv = big_update(w_kv, m_w_kv, v_w_kv, "kv", 2)
    r_a = big_update(a_in, m_a_in, v_a_in, "a", 1)
    r_b = big_update(b_in, m_b_in, v_b_in, "b", 1)

    rep_names = ["norm_mix", "norm_ffn", "mem_norm", "a_ln_g", "a_ln_b", "a_ws", "a_bs", "b_dt_bias", "b_a_log", "b_d",
                 "final_norm"]
    rep_grads = [jnp.concatenate(d_nmix, axis=0), jnp.concatenate(d_nffn, axis=0), jnp.concatenate(d_nmem, axis=0),
                 d_lng, d_lnb, d_ws.reshape(1, A_GROUPS, CHUNK, CHUNK), d_bs3.reshape(1, A_GROUPS, CHUNK),
                 d_dtbias[:, :SSM_HEADS], d_alog[:, :SSM_HEADS], d_dskip[:, :SSM_HEADS], d_fin.reshape(D_MODEL)]
    rep_w = [norm_mix, norm_ffn, mem_norm, a_ln_g, a_ln_b, a_ws, a_bs, b_dt_bias, b_a_log, b_d, final_norm]
    rep_m = [m_norm_mix, m_norm_ffn, m_mem_norm, m_a_ln_g, m_a_ln_b, m_a_ws, m_a_bs, m_b_dt_bias, m_b_a_log, m_b_d, m_final_norm]
    rep_v = [v_norm_mix, v_norm_ffn, v_mem_norm, v_a_ln_g, v_a_ln_b, v_a_ws, v_a_bs, v_b_dt_bias, v_b_a_log, v_b_d, v_final_norm]
    rep_grads = [g.reshape(w.shape) for g, w in zip(rep_grads, rep_w)]
    sh_grads = [d_convw, d_convb, d_gnorm]
    g_pack, g_layout = _pack(rep_grads + sh_grads + [loss_part])
    n_rep = len(rep_grads)
    (g_all,) = _all_gather([g_pack], [("blk", 0, (N_DEV,) + g_pack.shape)], "ag_small_grads")
    g_small = _sum8(g_all, "sum_small_grads")
    g_list = _unpack(g_small, g_layout)
    loss = g_list[-1][0, 0]
    wp, w_layout = _pack(rep_w)
    mp, _ = _pack(rep_m)
    vp, _ = _pack(rep_v)
    gp, _ = _pack(g_list[:n_rep])
    rep_res = [_unpack(o, w_layout) for o in _adamw(wp, mp, vp, [(gp, None)], "adamw_replicated", tr=88)]

    gcw = lax.dynamic_slice_in_dim(g_list[n_rep], me * 384, 384, axis=1).reshape(1, CONV_K, 384)
    gcb = lax.dynamic_slice_in_dim(g_list[n_rep + 1], me * 384, 384, axis=1)
    ggn = lax.dynamic_slice_in_dim(g_list[n_rep + 2], me * 256, 256, axis=1)
    sh_w = [b_conv_w, b_conv_b, b_gnorm]
    sh_m = [m_b_conv_w, m_b_conv_b, m_b_gnorm]
    sh_v = [v_b_conv_w, v_b_conv_b, v_b_gnorm]
    swp, sw_layout = _pack(sh_w)
    smp, _ = _pack(sh_m)
    svp, _ = _pack(sh_v)
    sgp, _ = _pack([gcw, gcb, ggn])
    sh_res = [_unpack(o, sw_layout) for o in _adamw(swp, smp, svp, [(sgp, None)], "adamw_sharded_small", tr=8)]

    names = ["norm_mix", "norm_ffn", "mem_norm", "w_kv", "w_out", "w_ffn1", "w_ffn2", "a_in", "a_ln_g", "a_ln_b", "a_ws",
             "a_bs", "b_in", "b_conv_w", "b_conv_b", "b_dt_bias", "b_a_log", "b_d", "b_gnorm", "final_norm"]
    big = {"w_kv": r_kv, "w_out": r_out, "w_ffn1": r_f1, "w_ffn2": r_f2, "a_in": r_a, "b_in": r_b}
    sh_names = ["b_conv_w", "b_conv_b", "b_gnorm"]
    outs = [loss, grad_x.reshape(x.shape)]
    for kind in range(4):
        for nm in names:
            if nm in big:
                outs.append(big[nm][kind])
            elif nm in sh_names:
                outs.append(sh_res[kind][sh_names.index(nm)])
            else:
                outs.append(rep_res[kind][rep_names.index(nm)])
    return tuple(outs)
```

```python
import functools
import math

import jax
import jax.numpy as jnp
from jax import lax
from jax.experimental import pallas as pl
from jax.experimental.pallas import tpu as pltpu
from jax.experimental.pallas import tpu_sc as plsc

F32 = jnp.float32
_MXU = jnp.bfloat16
_ACT = jnp.bfloat16
_HI = lax.Precision.HIGHEST

D_MODEL = 1024
CHUNK = 128
N_MEM = 256
D_INNER = 2048
A_GROUPS = 8
A_GW = D_INNER // A_GROUPS
SSM_HEADS = 32
SSM_P = 64
SSM_GROUPS = 4
SSM_GW = D_INNER // SSM_GROUPS
SSM_N = 128
CONV_K = 4
CONV_DIM = 3072
X_HEADS = 4
X_HD = 256
X_WIDTH = 1024
D_FF = 4096
EPS = 1e-6
HPAD = 128
N_DEV = 8

ADAM_LR = 0.001
ADAM_B1 = 0.9
ADAM_B2 = 0.999
ADAM_EPS = 1e-08
ADAM_WD = 0.01
ADAM_STEP = 10

VMEM_BIG = 56 * 1024 * 1024
MESH = pl.DeviceIdType.MESH


def _cp(vmem=None):
    if vmem is None:
        return pltpu.CompilerParams()
    return pltpu.CompilerParams(vmem_limit_bytes=vmem)


def _dot(a, b, dims=((1,), (0,))):
    return lax.dot_general(a.astype(_MXU), b.astype(_MXU), (dims, ((), ())), preferred_element_type=F32)


def _dot_nt(a, b):
    return _dot(a, b, ((1,), (1,)))


def _dot_tn(a, b):
    return _dot(a, b, ((0,), (0,)))


def _dot_hi(a, b, dims=((1,), (0,))):
    return lax.dot_general(a.astype(F32), b.astype(F32), (dims, ((), ())), precision=_HI, preferred_element_type=F32)


def _split3(x):
    x1 = x.astype(jnp.bfloat16)
    r = x - x1.astype(F32)
    x2 = r.astype(jnp.bfloat16)
    x3 = (r - x2.astype(F32)).astype(jnp.bfloat16)
    return x1, x2, x3


def _dot_sel(x, sel, dims=((1,), (0,))):
    sel = sel.astype(jnp.bfloat16)
    parts = [lax.dot_general(t, sel, (dims, ((), ())), preferred_element_type=F32) for t in _split3(x)]
    return (parts[0] + parts[1]) + parts[2]


def _sel_dot(sel, x, dims=((1,), (0,))):
    sel = sel.astype(jnp.bfloat16)
    parts = [lax.dot_general(sel, t, (dims, ((), ())), preferred_element_type=F32) for t in _split3(x)]
    return (parts[0] + parts[1]) + parts[2]


def _sigmoid(x):
    return 1.0 / (1.0 + jnp.exp(-x))


def _gelu(x):
    return 0.5 * x * (1.0 + lax.erf(x * (1.0 / math.sqrt(2.0))))


def _gelu_grad(x):
    return 0.5 * (1.0 + lax.erf(x * (1.0 / math.sqrt(2.0)))) + x * jnp.exp(-0.5 * x * x) * (1.0 / math.sqrt(2.0 * math.pi))


def _softplus(x):
    return jnp.maximum(x, 0.0) + jnp.log1p(jnp.exp(-jnp.abs(x)))


def _iota(shape, dim):
    return lax.broadcasted_iota(jnp.int32, shape, dim)


MM_VMEM_BUDGET = 40 * 1024 * 1024
HBM_BYTES_PER_S = 2.5e12
GRID_STEP_S = 0.35e-6
VMEM_ACC_BYTES_PER_S = 6e12


def _divisors(dim, unit):
    out = [d for d in range(unit, min(dim, 2048) + 1, unit) if dim % d == 0]
    return out if out else [dim]


def _mm_tiles(m, n, k, sa, sb, s_mn, a_pro, offsets):
    best = None
    (a_r0, a_c0, ta), (b_r0, b_c0, tb), (o_r0, o_c0) = offsets
    for tm in _divisors(m, 128):
        for tn in _divisors(n, 128):
            for tk in [k // d for d in (1, 2, 3, 4, 6, 8) if k % d == 0 and (k // d) % 128 == 0]:
                a_t = (tk, tm) if ta else (tm, tk)
                b_t = (tn, tk) if tb else (tk, tn)
                if a_r0 % a_t[0] or a_c0 % a_t[1] or b_r0 % b_t[0] or b_c0 % b_t[1] or o_r0 % tm or o_c0 % tn:
                    continue
                nk = k // tk
                vmem = 2 * (tm * tk * sa + tk * tn * sb + tm * tn * s_mn) + tm * tn * 4 * (2 if nk > 1 else 1)
                if a_pro or sa == 4:
                    vmem += tm * tk * 6
                if sb == 4:
                    vmem += tk * tn * 2
                if vmem > MM_VMEM_BUDGET:
                    continue
                gi, gj = m // tm, n // tn
                for j_inner in (True, False):
                    if nk > 1:
                        traffic = gj * m * k * sa + gi * k * n * sb
                    elif j_inner:
                        traffic = m * k * sa + gi * k * n * sb
                    else:
                        traffic = gj * m * k * sa + k * n * sb
                    traffic += m * n * s_mn + (tm * tk * sa + tk * tn * sb)
                    cost = traffic / HBM_BYTES_PER_S + gi * gj * nk * GRID_STEP_S
                    if nk > 1:
                        cost += m * n * 8 * nk / VMEM_ACC_BYTES_PER_S
                    if best is None or cost < best[0]:
                        best = (cost, tm, tn, tk, j_inner)
    assert best is not None, (m, n, k)
    return best[1:]


def _mm(a, b, *, m, n, k, name, ta=False, tb=False, a_at=(None, 0, 0), b_at=(None, 0, 0),
        out_dtype=F32, add=None, epi_p=None, epi_at=(None, 0, 0), out=None, out_at=(None, 0, 0),
        out_full=None, a_pro=None, after=()):
    s_mn =jnp.dtype(out.dtype if out is not None else out_dtype).itemsize
    s_mn += add.dtype.itemsize if add is not None else 0
    s_mn += epi_p.dtype.itemsize if epi_p is not None else 0
    tm, tn, tk, j_inner = _mm_tiles(m, n, k, a.dtype.itemsize, b.dtype.itemsize, s_mn, a_pro is not None,
                                    ((a_at[1], a_at[2], ta), (b_at[1], b_at[2], tb), (out_at[1], out_at[2])))
    nk = k // tk

    def spec(at, tr, tc, rsel, csel):
        lead, r0, c0 = at
        assert r0 % tr == 0 and c0 % tc == 0, (name, at, tr, tc)
        rb, cb = r0 // tr, c0 // tc
        if lead is None:
            return pl.BlockSpec((tr, tc), lambda g0, g1, kk: (rb + rsel(g0, g1, kk), cb + csel(g0, g1, kk)))
        return pl.BlockSpec((None, tr, tc), lambda g0, g1, kk: (lead, rb + rsel(g0, g1, kk), cb + csel(g0, g1, kk)))

    gi = (lambda g0, g1, kk: g0) if j_inner else (lambda g0, g1, kk: g1)
    gj = (lambda g0, g1, kk: g1) if j_inner else (lambda g0, g1, kk: g0)
    gk = lambda g0, g1, kk: kk
    a_spec = spec(a_at, tk, tm, gk, gi) if ta else spec(a_at, tm, tk, gi, gk)
    b_spec = spec(b_at, tn, tk, gj, gk) if tb else spec(b_at, tk, tn, gk, gj)
    dims = ((0,), (0,)) if ta else (((1,), (1,)) if tb else ((1,), (0,)))
    assert not (ta and tb)

    operands, in_specs = [a, b], [a_spec, b_spec]
    if add is not None:
        operands.append(add)
        in_specs.append(spec((None, 0, 0), tm, tn, gi, gj))
    if epi_p is not None:
        operands.append(epi_p)
        in_specs.append(spec(epi_at, tm, tn, gi, gj))
    aliases = {}
    if out is not None:
        aliases = {len(operands): 0}
        operands.append(out)
        in_specs.append(pl.BlockSpec(memory_space=pl.ANY))
        out_struct = jax.ShapeDtypeStruct(out.shape, out.dtype)
        out_dtype = out.dtype
    else:
        out_struct = jax.ShapeDtypeStruct(out_full if out_full is not None else (m, n), out_dtype)
    has_add, has_epi = add is not None, epi_p is not None
    n_skip = (1 if out is not None else 0) + len(after)
    operands += list(after)
    in_specs += [pl.BlockSpec(memory_space=pl.ANY)] * len(after)

    def body(*refs):
        a_ref, b_ref = refs[0], refs[1]
        pos = 2
        add_ref = epi_ref = None
        if has_add:
            add_ref = refs[pos]
            pos += 1
        if has_epi:
            epi_ref = refs[pos]
            pos += 1
        pos += n_skip
        o_ref = refs[pos]

        def finish(r):
            if has_add:
                r = r + add_ref[...].astype(F32)
            if has_epi:
                r = r * (2.0 * jnp.maximum(epi_ref[...].astype(F32), 0.0))
            o_ref[...] = r.astype(o_ref.dtype)

        av = a_ref[...]
        if a_pro == "relu2":
            av = jnp.square(jnp.maximum(av.astype(F32), 0.0))
        part = _dot(av, b_ref[...], dims)
        if nk == 1:
            finish(part)
        else:
            acc_ref = refs[pos + 1]
            kk = pl.program_id(2)

            @pl.when(kk == 0)
            def _():
                acc_ref[...] = part

            @pl.when(kk > 0)
            def _():
                acc_ref[...] += part

            @pl.when(kk == nk - 1)
            def _():
                finish(acc_ref[...])

    grid = (m // tm, n // tn, nk) if j_inner else (n // tn, m // tm, nk)
    return pl.pallas_call(
        body, name=name, grid=grid, in_specs=in_specs,
        out_specs=spec(out_at, tm, tn, gi, gj), out_shape=out_struct,
        scratch_shapes=[pltpu.VMEM((tm, tn), F32)] if nk > 1 else [], input_output_aliases=aliases,
        compiler_params=_cp(VMEM_BIG))(*operands)


def _rms_fwd(x, g, name, tm=256):
    s, d = x.shape
    tm = min(tm, s)

    def body(x_ref, g_ref, o_ref):
        xv = x_ref[...]
        r = lax.rsqrt(jnp.mean(xv * xv, axis=-1, keepdims=True) + EPS)
        o_ref[...] = (xv * r * g_ref[...]).astype(o_ref.dtype)

    return pl.pallas_call(
        body, name=name, grid=(s // tm,),
        in_specs=[pl.BlockSpec((tm, d), lambda i: (i, 0)), pl.BlockSpec((1, d), lambda i: (0, 0))],
        out_specs=pl.BlockSpec((tm, d), lambda i: (i, 0)),
        out_shape=jax.ShapeDtypeStruct((s, d), _ACT))(x, g)


def _rms_bwd(x, g, dy, dres, name, tm=256):
    s, d = x.shape
    tm = min(tm, s)
    has_res = dres is not None

    def body(*refs):
        if has_res:
            x_ref, g_ref, dy_ref, dres_ref, dx_ref, dxa_ref, dg_ref = refs
        else:
            x_ref, g_ref, dy_ref, dx_ref, dxa_ref, dg_ref = refs

        @pl.when(pl.program_id(0) == 0)
        def _():
            dg_ref[...] = jnp.zeros_like(dg_ref)

        xv = x_ref[...]
        dyv = dy_ref[...].astype(F32)
        r = lax.rsqrt(jnp.mean(xv * xv, axis=-1, keepdims=True) + EPS)
        xh = xv * r
        dyg = dyv * g_ref[...]
        dx = r * (dyg - xh * jnp.mean(dyg * xh, axis=-1, keepdims=True))
        if has_res:
            dx = dx + dres_ref[...]
        dx_ref[...] = dx
        dxa_ref[...] = dx.astype(dxa_ref.dtype)
        dg_ref[...] += jnp.sum(dyv * xh, axis=0, keepdims=True)

    row = pl.BlockSpec((tm, d), lambda i: (i, 0))
    vec = pl.BlockSpec((1, d), lambda i: (0, 0))
    in_specs = [row, vec, row] + ([row] if has_res else [])
    operands = [x, g, dy] + ([dres] if has_res else [])
    return pl.pallas_call(
        body, name=name, grid=(s // tm,), in_specs=in_specs, out_specs=[row, row, vec],
        out_shape=[jax.ShapeDtypeStruct((s, d), F32), jax.ShapeDtypeStruct((s, d), _ACT),
                   jax.ShapeDtypeStruct((1, d), F32)])(*operands)


def _loss_head(h, g, target, name, tm=256):
    s, d = h.shape
    tm = min(tm, s)

    def body(h_ref, g_ref, t_ref, loss_ref, dh_ref, dha_ref, dg_ref):
        @pl.when(pl.program_id(0) == 0)
        def _():
            dg_ref[...] = jnp.zeros_like(dg_ref)
            loss_ref[...] = jnp.zeros_like(loss_ref)

        xv = h_ref[...]
        r = lax.rsqrt(jnp.mean(xv * xv, axis=-1, keepdims=True) + EPS)
        xh = xv * r
        err = xh * g_ref[...] - t_ref[...]
        loss_ref[...] += jnp.full(loss_ref.shape, 0.5 * jnp.sum(jnp.mean(err * err, axis=-1, keepdims=True)), F32)
        dyv = err * (1.0 / d)
        dyg = dyv * g_ref[...]
        dh = r * (dyg - xh * jnp.mean(dyg * xh, axis=-1, keepdims=True))
        dh_ref[...] = dh
        dha_ref[...] = dh.astype(dha_ref.dtype)
        dg_ref[...] += jnp.sum(dyv * xh, axis=0, keepdims=True)

    row = pl.BlockSpec((tm, d), lambda i: (i, 0))
    vec = pl.BlockSpec((1, d), lambda i: (0, 0))
    return pl.pallas_call(
        body, name=name, grid=(s // tm,), in_specs=[row, vec, row],
        out_specs=[pl.BlockSpec((1, 128), lambda i: (0, 0)), row, row, vec],
        out_shape=[jax.ShapeDtypeStruct((1, 128), F32), jax.ShapeDtypeStruct((s, d), F32),
                   jax.ShapeDtypeStruct((s, d), _ACT), jax.ShapeDtypeStruct((1, d), F32)])(h, g, target)


def _gmlp_parts(pu, pv, lng, lnb):
    u = _gelu(pu)
    v = _gelu(pv)
    mu = jnp.mean(v, axis=-1, keepdims=True)
    vc = v - mu
    rstd = lax.rsqrt(jnp.mean(vc * vc, axis=-1, keepdims=True) + EPS)
    xhat = vc * rstd
    vn = xhat * lng + lnb
    return u, xhat, rstd, vn


def _gmlp_fwd(proj, lng, lnb, ws, bs3, name):
    s = proj.shape[0]

    def body(pu_ref, pv_ref, lng_ref, lnb_ref, ws_ref, bs_ref, o_ref):
        u, _, _, vn = _gmlp_parts(pu_ref[...], pv_ref[...], lng_ref[...], lnb_ref[...])
        causal = _iota((CHUNK, CHUNK), 0) >= _iota((CHUNK, CHUNK), 1)
        for g in range(A_GROUPS):
            sl = slice(g * A_GW, (g + 1) * A_GW)
            w = jnp.where(causal, ws_ref[g], 0.0)
            sv = _dot(w, vn[:, sl]) + bs_ref[g]
            o_ref[:, sl] = (u[:, sl] * sv).astype(o_ref.dtype)

    full = lambda shape: pl.BlockSpec(shape, lambda c: (0,) * len(shape))
    return pl.pallas_call(
        body, name=name, grid=(s // CHUNK,),
        in_specs=[pl.BlockSpec((CHUNK, D_INNER), lambda c: (c, 0)), pl.BlockSpec((CHUNK, D_INNER), lambda c: (c, 1)),
                  full((1, D_INNER)), full((1, D_INNER)), full((A_GROUPS, CHUNK, CHUNK)), full((A_GROUPS, CHUNK, 1))],
        out_specs=pl.BlockSpec((CHUNK, D_INNER), lambda c: (c, 0)),
        out_shape=jax.ShapeDtypeStruct((s, D_INNER + X_WIDTH), _ACT), compiler_params=_cp(VMEM_BIG))(proj, proj, lng, lnb, ws, bs3)


def _gmlp_bwd(proj, dcat, lng, lnb, ws, bs3, name):
    s = proj.shape[0]

    def body(pu_ref, pv_ref, dm_ref, lng_ref, lnb_ref, ws_ref, bs_ref, dp_ref, dws_ref, dbs_ref, dlng_ref, dlnb_ref, dvn_ref):
        @pl.when(pl.program_id(0) == 0)
        def _():
            dws_ref[...] = jnp.zeros_like(dws_ref)
            dbs_ref[...] = jnp.zeros_like(dbs_ref)
            dlng_ref[...] = jnp.zeros_like(dlng_ref)
            dlnb_ref[...] = jnp.zeros_like(dlnb_ref)

        pu, pv = pu_ref[...], pv_ref[...]
        lng = lng_ref[...]
        u, xhat, rstd, vn = _gmlp_parts(pu, pv, lng, lnb_ref[...])
        dm = dm_ref[...].astype(F32)
        causal = _iota((CHUNK, CHUNK), 0) >= _iota((CHUNK, CHUNK), 1)
        for g in range(A_GROUPS):
            sl = slice(g * A_GW, (g + 1) * A_GW)
            w = jnp.where(causal, ws_ref[g], 0.0)
            sv = _dot(w, vn[:, sl]) + bs_ref[g]
            dsv = dm[:, sl] * u[:, sl]
            dp_ref[:, sl] = (dm[:, sl] * sv * _gelu_grad(pu[:, sl])).astype(dp_ref.dtype)
            dvn_ref[:, sl] = _dot_tn(w, dsv)
            dws_ref[g] += jnp.where(causal, _dot_nt(dsv, vn[:, sl]), 0.0)
            dbs_ref[g] += jnp.sum(dsv, axis=-1, keepdims=True)
        dvn = dvn_ref[...]
        dlng_ref[...] += jnp.sum(dvn * xhat, axis=0, keepdims=True)
        dlnb_ref[...] += jnp.sum(dvn, axis=0, keepdims=True)
        dxh = dvn * lng
        dv = rstd * (dxh - jnp.mean(dxh, axis=-1, keepdims=True) - xhat * jnp.mean(dxh * xhat, axis=-1, keepdims=True))
        dp_ref[:, D_INNER:] = (dv * _gelu_grad(pv)).astype(dp_ref.dtype)

    full = lambda shape: pl.BlockSpec(shape, lambda c: (0,) * len(shape))
    return pl.pallas_call(
        body, name=name, grid=(s // CHUNK,),
        in_specs=[pl.BlockSpec((CHUNK, D_INNER), lambda c: (c, 0)), pl.BlockSpec((CHUNK, D_INNER), lambda c: (c, 1)),
                  pl.BlockSpec((CHUNK, D_INNER), lambda c: (c, 0)),
                  full((1, D_INNER)), full((1, D_INNER)), full((A_GROUPS, CHUNK, CHUNK)), full((A_GROUPS, CHUNK, 1))],
        out_specs=[pl.BlockSpec((CHUNK, 2 * D_INNER), lambda c: (c, 0)), full((A_GROUPS, CHUNK, CHUNK)),
                   full((A_GROUPS, CHUNK, 1)), full((1, D_INNER)), full((1, D_INNER))],
        out_shape=[jax.ShapeDtypeStruct((s, 2 * D_INNER + X_WIDTH), _ACT), jax.ShapeDtypeStruct((A_GROUPS, CHUNK, CHUNK), F32),
                   jax.ShapeDtypeStruct((A_GROUPS, CHUNK, 1), F32), jax.ShapeDtypeStruct((1, D_INNER), F32),
                   jax.ShapeDtypeStruct((1, D_INNER), F32)],
        scratch_shapes=[pltpu.VMEM((CHUNK, D_INNER), F32)],
        compiler_params=_cp(VMEM_BIG))(proj, proj, dcat, lng, lnb, ws, bs3)


_X_SCALE = 1.0 / math.sqrt(X_HD)


def _attn_fwd(proj, qblk, kv, cat, name, tm=256):
    s = proj.shape[0]
    tm = min(tm, s)

    def body(q_ref, kv_ref, cat_ref, o_ref):
        for h in range(X_HEADS):
            sl = slice(h * X_HD, (h + 1) * X_HD)
            k = kv_ref[:, sl]
            v = kv_ref[:, X_WIDTH + h * X_HD:X_WIDTH + (h + 1) * X_HD]
            sc = _dot_nt(q_ref[:, sl], k) * _X_SCALE
            e = jnp.exp(sc - jnp.max(sc, axis=-1, keepdims=True))
            p = e / jnp.sum(e, axis=-1, keepdims=True)
            o_ref[:, sl] = _dot(p, v).astype(o_ref.dtype)

    return pl.pallas_call(
        body, name=name, grid=(s // tm,),
        in_specs=[pl.BlockSpec((tm, X_WIDTH), lambda i: (i, qblk)), pl.BlockSpec((N_MEM, 2 * X_WIDTH), lambda i: (0, 0)),
                  pl.BlockSpec(memory_space=pl.ANY)],
        out_specs=pl.BlockSpec((tm, X_WIDTH), lambda i: (i, D_INNER // X_WIDTH)),
        out_shape=jax.ShapeDtypeStruct(cat.shape, cat.dtype), input_output_aliases={2: 0})(proj, kv, cat)


def _attn_bwd(proj, qblk, kv, dcat, dproj, name, tm=256):
    s = proj.shape[0]
    tm = min(tm, s)

    def body(q_ref, kv_ref, do_ref, dproj_ref, dq_ref, dkv_ref):
        @pl.when(pl.program_id(0) == 0)
        def _():
            dkv_ref[...] = jnp.zeros_like(dkv_ref)

        for h in range(X_HEADS):
            sl = slice(h * X_HD, (h + 1) * X_HD)
            slv = slice(X_WIDTH + h * X_HD, X_WIDTH + (h + 1) * X_HD)
            q = q_ref[:, sl]
            k = kv_ref[:, sl]
            v = kv_ref[:, slv]
            do = do_ref[:, sl].astype(F32)
            sc = _dot_nt(q, k) * _X_SCALE
            e = jnp.exp(sc - jnp.max(sc, axis=-1, keepdims=True))
            p = e / jnp.sum(e, axis=-1, keepdims=True)
            dp = _dot_nt(do, v)
            ds = p * (dp - jnp.sum(dp * p, axis=-1, keepdims=True)) * _X_SCALE
            dq_ref[:, sl] = _dot(ds, k).astype(dq_ref.dtype)
            dkv_ref[:, sl] += _dot_tn(ds, q)
            dkv_ref[:, slv] += _dot_tn(p, do)

    return pl.pallas_call(
        body, name=name, grid=(s // tm,),
        in_specs=[pl.BlockSpec((tm, X_WIDTH), lambda i: (i, qblk)), pl.BlockSpec((N_MEM, 2 * X_WIDTH), lambda i: (0, 0)),
                  pl.BlockSpec((tm, X_WIDTH), lambda i: (i, 2)), pl.BlockSpec(memory_space=pl.ANY)],
        out_specs=[pl.BlockSpec((tm, X_WIDTH), lambda i: (i, qblk)), pl.BlockSpec((N_MEM, 2 * X_WIDTH), lambda i: (0, 0))],
        out_shape=[jax.ShapeDtypeStruct(dproj.shape, dproj.dtype), jax.ShapeDtypeStruct((N_MEM, 2 * X_WIDTH), F32)],
        input_output_aliases={3: 0})(proj, kv, dcat, dproj)


CONV_TC = 256
_XBC_BLK0 = D_INNER // CONV_TC


def _shift_down(x, j):
    if j == 0:
        return x
    return jnp.where(_iota(x.shape, 0) >= j, pltpu.roll(x, j, 0), 0.0)


def _shift_up(x, j):
    if j == 0:
        return x
    n = x.shape[0]
    return jnp.where(_iota(x.shape, 0) < n - j, pltpu.roll(x, n - j, 0), 0.0)


def _conv_fwd(proj, w, b, name):
    s = proj.shape[0]

    def body(x_ref, w_ref, b_ref, o_ref):
        xv = x_ref[...]
        pre = b_ref[...] + w_ref[CONV_K - 1:CONV_K, :] * xv
        for kk in range(CONV_K - 1):
            pre = pre + w_ref[kk:kk + 1, :] * _shift_down(xv, CONV_K - 1 - kk)
        o_ref[...] = pre * _sigmoid(pre)

    return pl.pallas_call(
        body, name=name, grid=(CONV_DIM // CONV_TC,),
        in_specs=[pl.BlockSpec((s, CONV_TC), lambda j: (0, _XBC_BLK0 + j)), pl.BlockSpec((CONV_K, CONV_TC), lambda j: (0, j)),
                  pl.BlockSpec((1, CONV_TC), lambda j: (0, j))],
        out_specs=pl.BlockSpec((s, CONV_TC), lambda j: (0, j)),
        out_shape=jax.ShapeDtypeStruct((s, CONV_DIM), F32), compiler_params=_cp(VMEM_BIG))(proj, w, b)


def _conv_bwd(proj, w, b, dxbc, dproj, name):
    s = proj.shape[0]

    def body(x_ref, w_ref, b_ref, d_ref, dproj_ref, dx_ref, dw_ref, db_ref):
        xv = x_ref[...]
        pre = b_ref[...] + w_ref[CONV_K - 1:CONV_K, :] * xv
        for kk in range(CONV_K - 1):
            pre = pre + w_ref[kk:kk + 1, :] * _shift_down(xv, CONV_K - 1 - kk)
        sig = _sigmoid(pre)
        dpre = d_ref[...] * (sig * (1.0 + pre * (1.0 - sig)))
        dx = w_ref[CONV_K - 1:CONV_K, :] * dpre
        dw_ref[CONV_K - 1:CONV_K, :] = jnp.sum(dpre * xv, axis=0, keepdims=True)
        for kk in range(CONV_K - 1):
            j = CONV_K - 1 - kk
            dx = dx + w_ref[kk:kk + 1, :] * _shift_up(dpre, j)
            dw_ref[kk:kk + 1, :] = jnp.sum(dpre * _shift_down(xv, j), axis=0, keepdims=True)
        dx_ref[...] = dx.astype(dx_ref.dtype)
        db_ref[...] = jnp.sum(dpre, axis=0, keepdims=True)

    return pl.pallas_call(
        body, name=name, grid=(CONV_DIM // CONV_TC,),
        in_specs=[pl.BlockSpec((s, CONV_TC), lambda j: (0, _XBC_BLK0 + j)), pl.BlockSpec((CONV_K, CONV_TC), lambda j: (0, j)),
                  pl.BlockSpec((1, CONV_TC), lambda j: (0, j)), pl.BlockSpec((s, CONV_TC), lambda j: (0, j)),
                  pl.BlockSpec(memory_space=pl.ANY)],
        out_specs=[pl.BlockSpec((s, CONV_TC), lambda j: (0, _XBC_BLK0 + j)), pl.BlockSpec((CONV_K, CONV_TC), lambda j: (0, j)),
                   pl.BlockSpec((1, CONV_TC), lambda j: (0, j))],
        out_shape=[jax.ShapeDtypeStruct(dproj.shape, dproj.dtype), jax.ShapeDtypeStruct((CONV_K, CONV_DIM), F32),
                   jax.ShapeDtypeStruct((1, CONV_DIM), F32)], input_output_aliases={4: 0},
        compiler_params=_cp(VMEM_BIG))(proj, w, b, dxbc, dproj)


def _ssd_common(dtc_ref, br_ref, ar_ref, csb_ref, cst_ref, csf_ref):
    a_row = -jnp.exp(ar_ref[...])
    dt_c = _softplus(dtc_ref[...] + br_ref[...])
    tril = _iota((CHUNK, CHUNK), 0) >= _iota((CHUNK, CHUNK), 1)
    cs = _sel_dot(tril, dt_c * a_row)
    cst_ref[...] = cs.T
    e64 = (jnp.right_shift(_iota((HPAD, D_INNER), 1), 6) == _iota((HPAD, D_INNER), 0)).astype(jnp.bfloat16)
    e128 = jnp.right_shift(_iota((HPAD, SSM_HEADS * CHUNK), 1), 7) == _iota((HPAD, SSM_HEADS * CHUNK), 0)
    csb_ref[...] = _dot_sel(cs, e128)
    dt_full = _dot_sel(dt_c, e64)
    csf_ref[...] = _dot_sel(cs, e64)
    cs_full = csf_ref[...]
    cs_last = csf_ref[CHUNK - 1:CHUNK, :]
    e_full = jnp.exp(cs_full)
    f_full = jnp.exp(cs_last - cs_full)
    gamma = jnp.exp(cs_last)
    return a_row, dt_c, cs, dt_full, e_full, f_full, gamma, e64


def _ssd_lambda(csb_ref, cst_ref, h, causal):
    diff = csb_ref[:, h * CHUNK:(h + 1) * CHUNK] - cst_ref[h:h + 1, :]
    return jnp.exp(jnp.where(causal, diff, -1e30))


_SSD_VEC_SPECS = lambda: [pl.BlockSpec((1, HPAD), lambda c: (0, 0)), pl.BlockSpec((1, HPAD), lambda c: (0, 0)),
                          pl.BlockSpec((1, D_INNER), lambda c: (0, 0))]


def _ssd_fwd(xbc, dtc, bias_row, alog_row, dfull, name):
    s = xbc.shape[0]
    nc = s // CHUNK

    def body(xbc_ref, dtc_ref, br_ref, ar_ref, df_ref, y_ref, st_ref, ht_ref, csb_ref, cst_ref, csf_ref):
        @pl.when(pl.program_id(0) == 0)
        def _():
            ht_ref[...] = jnp.zeros_like(ht_ref)

        _, _, _, dt_full, e_full, f_full, gamma, _ = _ssd_common(dtc_ref, br_ref, ar_ref, csb_ref, cst_ref, csf_ref)
        x = xbc_ref[:, :D_INNER]
        xdt = x * dt_full
        st_ref[...] = ht_ref[...]
        causal = _iota((CHUNK, CHUNK), 0) >= _iota((CHUNK, CHUNK), 1)
        lo = _iota((CHUNK, CHUNK), 1) < SSM_P
        for g in range(SSM_GROUPS):
            gs = slice(g * SSM_GW, (g + 1) * SSM_GW)
            bg = xbc_ref[:, D_INNER + g * SSM_N:D_INNER + (g + 1) * SSM_N]
            cg = xbc_ref[:, D_INNER + SSM_GROUPS * SSM_N + g * SSM_N:D_INNER + SSM_GROUPS * SSM_N + (g + 1) * SSM_N]
            ht = ht_ref[:, gs]
            cb = _dot_nt(cg, bg)
            yoff = e_full[:, gs] * _dot(cg, ht)
            for jp in range(SSM_GW // CHUNK):
                j = g * (SSM_GW // CHUNK) + jp
                ps = slice(j * CHUNK, (j + 1) * CHUNK)
                x2 = xdt[:, ps]
                y0 = _dot(cb * _ssd_lambda(csb_ref, cst_ref, 2 * j, causal), x2)
                y1 = _dot(cb * _ssd_lambda(csb_ref, cst_ref, 2 * j + 1, causal), x2)
                y_ref[:, ps] = (jnp.where(lo, y0, y1) + yoff[:, jp * CHUNK:(jp + 1) * CHUNK]
                                + x[:, ps] * df_ref[:, ps])
            ht_ref[:, gs] = gamma[:, gs] * ht + _dot_tn(bg, xdt[:, gs] * f_full[:, gs])

    return pl.pallas_call(
        body, name=name, grid=(nc,),
        in_specs=[pl.BlockSpec((CHUNK, CONV_DIM), lambda c: (c, 0)), pl.BlockSpec((CHUNK, HPAD), lambda c: (c, 0))]
                 + _SSD_VEC_SPECS(),
        out_specs=[pl.BlockSpec((CHUNK, D_INNER), lambda c: (c, 0)), pl.BlockSpec((None, SSM_N, D_INNER), lambda c: (c, 0, 0))],
        out_shape=[jax.ShapeDtypeStruct((s, D_INNER), F32), jax.ShapeDtypeStruct((nc, SSM_N, D_INNER), F32)],
        scratch_shapes=[pltpu.VMEM((SSM_N, D_INNER), F32), pltpu.VMEM((CHUNK, SSM_HEADS * CHUNK), F32),
                        pltpu.VMEM((HPAD, CHUNK), F32), pltpu.VMEM((CHUNK, D_INNER), F32)],
        compiler_params=_cp(VMEM_BIG))(xbc, dtc, bias_row, alog_row, dfull)


def _ssd_bwd(xbc, dtc, bias_row, alog_row, dfull, dy, states, name):
    s = xbc.shape[0]
    nc = s // CHUNK
    rev = lambda c: nc - 1 - c

    def body(xbc_ref, dtc_ref, br_ref, ar_ref, df_ref, dy_ref, st_ref,
             dxbc_ref, ddt_ref, dalog_ref, dd_ref, dbias_ref,
             dht_ref, csb_ref, cst_ref, csf_ref, ddf_ref, dxs_ref, dcsf_ref, dcsl_ref):
        step = pl.program_id(0)

        @pl.when(step == 0)
        def _():
            dht_ref[...] = jnp.zeros_like(dht_ref)
            ddf_ref[...] = jnp.zeros_like(ddf_ref)
            dalog_ref[...] = jnp.zeros_like(dalog_ref)
            dbias_ref[...] = jnp.zeros_like(dbias_ref)
            dd_ref[...] = jnp.zeros_like(dd_ref)

        a_row, dt_c, _, dt_full, e_full, f_full, gamma, e64 = _ssd_common(dtc_ref, br_ref, ar_ref, csb_ref, cst_ref, csf_ref)
        x = xbc_ref[:, :D_INNER]
        xdt = x * dt_full
        dy_all = dy_ref[...]
        ddf_ref[...] += jnp.broadcast_to(jnp.sum(dy_all * x, axis=0, keepdims=True), ddf_ref.shape)
        causal = _iota((CHUNK, CHUNK), 0) >= _iota((CHUNK, CHUNK), 1)
        lo = _iota((CHUNK, CHUNK), 1) < SSM_P
        head_lane = _iota((CHUNK, HPAD), 1)
        head_row = _iota((HPAD, CHUNK), 0)
        dcs_heads = jnp.zeros((CHUNK, HPAD), F32)
        dcs_cols = jnp.zeros((HPAD, CHUNK), F32)
        for g in range(SSM_GROUPS):
            gs = slice(g * SSM_GW, (g + 1) * SSM_GW)
            b0 = D_INNER + g * SSM_N
            c0 = D_INNER + SSM_GROUPS * SSM_N + g * SSM_N
            bg = xbc_ref[:, b0:b0 + SSM_N]
            cg = xbc_ref[:, c0:c0 + SSM_N]
            ht = st_ref[:, gs]
            dht = dht_ref[:, gs]
            dyg = dy_all[:, gs]
            eg, fg, gg = e_full[:, gs], f_full[:, gs], gamma[:, gs]
            z = _dot(cg, ht)
            dz = dyg * eg
            dcg = _dot_nt(dz, ht)
            dht_new = _dot_tn(cg, dz) + gg * dht
            xf = xdt[:, gs] * fg
            dxf = _dot(bg, dht)
            dbg = _dot_nt(xf, dht)
            dff = dxf * xf
            dcsf_ref[:, gs] = dyg * eg * z - dff
            dcsl_ref[:, gs] = jnp.broadcast_to(
                jnp.sum(dff, axis=0, keepdims=True) + jnp.sum(dht * ht, axis=0, keepdims=True) * gg, (8, SSM_GW))
            cb = _dot_nt(cg, bg)
            dcb = jnp.zeros((CHUNK, CHUNK), F32)
            for jp in range(SSM_GW // CHUNK):
                j = g * (SSM_GW // CHUNK) + jp
                ps = slice(j * CHUNK, (j + 1) * CHUNK)
                x2 = xdt[:, ps]
                dy2 = dy_all[:, ps]
                dxh = []
                for hh in range(2):
                    h = 2 * j + hh
                    lam = _ssd_lambda(csb_ref, cst_ref, h, causal)
                    mh = cb * lam
                    dyh = jnp.where(lo, dy2, 0.0) if hh == 0 else jnp.where(lo, 0.0, dy2)
                    dm = _dot_nt(dyh, x2)
                    dcb = dcb + dm * lam
                    gm = dm * mh
                    dcs_heads = dcs_heads + jnp.where(head_lane == h, jnp.sum(gm, axis=1, keepdims=True), 0.0)
                    dcs_cols = dcs_cols + jnp.where(head_row == h, jnp.sum(gm, axis=0, keepdims=True), 0.0)
                    dxh.append(_dot_tn(mh, dy2))
                dxs_ref[:, ps] = jnp.where(lo, dxh[0], dxh[1]) + dxf[:, jp * CHUNK:(jp + 1) * CHUNK] * fg[:, jp * CHUNK:(jp + 1) * CHUNK]
            dxbc_ref[:, b0:b0 + SSM_N] = (dbg + _dot_tn(dcb, cg)).astype(dxbc_ref.dtype)
            dxbc_ref[:, c0:c0 + SSM_N] = (dcg + _dot(dcb, bg)).astype(dxbc_ref.dtype)
            dht_ref[:, gs] = dht_new
        dxs = dxs_ref[...]
        dcs_heads = dcs_heads - dcs_cols.T + _dot_sel(dcsf_ref[...], e64, ((1,), (1,)))
        dcs_last = _dot_sel(dcsl_ref[...], e64, ((1,), (1,)))
        dcs_heads = dcs_heads + jnp.where(_iota((CHUNK, HPAD), 0) == CHUNK - 1, dcs_last[0:1, :], 0.0)
        triu = _iota((CHUNK, CHUNK), 0) <= _iota((CHUNK, CHUNK), 1)
        dda = _sel_dot(triu, dcs_heads)
        ddt = dda * a_row + _dot_sel(dxs * x, e64, ((1,), (1,)))
        dxbc_ref[:, :D_INNER] = (dxs * dt_full + dy_all * df_ref[...]).astype(dxbc_ref.dtype)
        dalog_ref[...] += jnp.sum(dda * dt_c, axis=0, keepdims=True) * a_row
        ddt_raw = ddt * _sigmoid(dtc_ref[...] + br_ref[...])
        ddt_ref[...] = ddt_raw.astype(ddt_ref.dtype)
        dbias_ref[...] += jnp.sum(ddt_raw, axis=0, keepdims=True)

        @pl.when(step == nc - 1)
        def _():
            dd_ref[...] = _dot_sel(ddf_ref[...], e64, ((1,), (1,)))[0:1, :]

    vec = pl.BlockSpec((1, HPAD), lambda c: (0, 0))
    return pl.pallas_call(
        body, name=name, grid=(nc,),
        in_specs=[pl.BlockSpec((CHUNK, CONV_DIM), lambda c: (rev(c), 0)), pl.BlockSpec((CHUNK, HPAD), lambda c: (rev(c), 0))]
                 + _SSD_VEC_SPECS()
                 + [pl.BlockSpec((CHUNK, D_INNER), lambda c: (rev(c), 0)),
                    pl.BlockSpec((None, SSM_N, D_INNER), lambda c: (rev(c), 0, 0))],
        out_specs=[pl.BlockSpec((CHUNK, CONV_DIM), lambda c: (rev(c), 0)), pl.BlockSpec((CHUNK, HPAD), lambda c: (rev(c), 0)),
                   vec, vec, vec],
        out_shape=[jax.ShapeDtypeStruct((s, CONV_DIM), F32), jax.ShapeDtypeStruct((s, HPAD), _ACT),
                   jax.ShapeDtypeStruct((1, HPAD), F32), jax.ShapeDtypeStruct((1, HPAD), F32),
                   jax.ShapeDtypeStruct((1, HPAD), F32)],
        scratch_shapes=[pltpu.VMEM((SSM_N, D_INNER), F32), pltpu.VMEM((CHUNK, SSM_HEADS * CHUNK), F32),
                        pltpu.VMEM((HPAD, CHUNK), F32), pltpu.VMEM((CHUNK, D_INNER), F32),
                        pltpu.VMEM((8, D_INNER), F32), pltpu.VMEM((CHUNK, D_INNER), F32),
                        pltpu.VMEM((CHUNK, D_INNER), F32), pltpu.VMEM((8, D_INNER), F32)],
        compiler_params=_cp(VMEM_BIG))(xbc, dtc, bias_row, alog_row, dfull, dy, states)


def _gate_fwd(y, proj, gn, name, tm=256):
    s = y.shape[0]
    tm = min(tm, s)

    def body(y_ref, z_ref, gn_ref, o_ref):
        for g in range(SSM_GROUPS):
            gs = slice(g * SSM_GW, (g + 1) * SSM_GW)
            z = z_ref[:, gs]
            t = y_ref[:, gs] * (z * _sigmoid(z))
            r = lax.rsqrt(jnp.mean(t * t, axis=-1, keepdims=True) + EPS)
            o_ref[:, gs] = (t * r * gn_ref[:, gs]).astype(o_ref.dtype)

    row = pl.BlockSpec((tm, D_INNER), lambda i: (i, 0))
    return pl.pallas_call(
        body, name=name, grid=(s // tm,), in_specs=[row, row, pl.BlockSpec((1, D_INNER), lambda i: (0, 0))],
        out_specs=row, out_shape=jax.ShapeDtypeStruct((s, D_INNER + X_WIDTH), _ACT))(y, proj, gn)


def _gate_bwd(y, proj, gn, dcat, name, tm=256):
    s = y.shape[0]
    tm = min(tm, s)

    def body(y_ref, z_ref, gn_ref, dm_ref, dy_ref, dz_ref, dgn_ref):
        @pl.when(pl.program_id(0) == 0)
        def _():
            dgn_ref[...] = jnp.zeros_like(dgn_ref)

        for g in range(SSM_GROUPS):
            gs = slice(g * SSM_GW, (g + 1) * SSM_GW)
            z = z_ref[:, gs]
            yv = y_ref[:, gs]
            sig = _sigmoid(z)
            sz = z * sig
            t = yv * sz
            r = lax.rsqrt(jnp.mean(t * t, axis=-1, keepdims=True) + EPS)
            th = t * r
            dm = dm_ref[:, gs].astype(F32)
            dmg = dm * gn_ref[:, gs]
            dt_ = r * (dmg - th * jnp.mean(dmg * th, axis=-1, keepdims=True))
            dgn_ref[:, gs] += jnp.sum(dm * th, axis=0, keepdims=True)
            dy_ref[:, gs] = dt_ * sz
            dz_ref[:, gs] = (dt_ * yv * (sig * (1.0 + z * (1.0 - sig)))).astype(dz_ref.dtype)

    row = pl.BlockSpec((tm, D_INNER), lambda i: (i, 0))
    vec = pl.BlockSpec((1, D_INNER), lambda i: (0, 0))
    return pl.pallas_call(
        body, name=name, grid=(s // tm,), in_specs=[row, row, vec, row], out_specs=[row, row, vec],
        out_shape=[jax.ShapeDtypeStruct((s, D_INNER), F32), jax.ShapeDtypeStruct((s, 6 * D_MODEL), _ACT),
                   jax.ShapeDtypeStruct((1, D_INNER), F32)])(y, proj, gn, dcat)


def _block_of(kind, width):
    if kind == "col":
        return lambda ref, j: ref.at[:, :, pl.ds(pl.multiple_of(j * width, 128), width)]
    if kind == "row":
        return lambda ref, j: ref.at[:, pl.ds(pl.multiple_of(j * width, 8), width), :]
    return lambda ref, j: ref.at[j]


def _coords():
    return lax.axis_index("x"), lax.axis_index("y"), lax.axis_index("c")


def _rel_chip(x, y, k):
    return (1 - x if k & 1 else x), (1 - y if k & 2 else y)


_HBM = lambda: pl.BlockSpec(memory_space=pltpu.HBM)


def _all_gather(shards, layouts, name):
    n = len(shards)
    blocks = [_block_of(kind, width) for kind, width, _ in layouts]

    def body(*refs):
        _all_gather_body(refs[:n], refs[n:2 * n], *refs[2 * n:], blocks)

    return pl.pallas_call(
        body, name=name, in_specs=[_HBM()] * n, out_specs=[_HBM()] * n,
        out_shape=[jax.ShapeDtypeStruct(shape, sh.dtype) for sh, (_, _, shape) in zip(shards, layouts)],
        scratch_shapes=[pltpu.SemaphoreType.DMA((n, 7)), pltpu.SemaphoreType.DMA((n, 7)), pltpu.SemaphoreType.DMA((n,))])(*shards)


def _all_gather_body(ins, outs, send_sems, recv_sems, local_sems, blocks):
    n = len(ins)
    x, y, c = _coords()
    sibling = (x, y, 1 - c)

    def copy(t, k, chip, core, to, src=None):
        dst = blocks[t](outs[t], 4 * chip[0] + 2 * chip[1] + core)
        return pltpu.make_async_remote_copy(
            src_ref=dst if src is None else src, dst_ref=dst, send_sem=send_sems.at[t, k],
            recv_sem=recv_sems.at[t, k], device_id=to, device_id_type=MESH)

    started = []
    for t in range(n):
        mine = pltpu.make_async_copy(ins[t], blocks[t](outs[t], 4 * x + 2 * y + c), local_sems.at[t])
        mine.start()
        started.append(mine)
    sends = []
    for t in range(n):
        for k in range(4):
            px, py = _rel_chip(x, y, k)
            cp = copy(t, k, (x, y), c, (px, py, 1 - c if k == 0 else c), src=ins[t])
            cp.start()
            sends.append(cp)
    for t in range(n):
        for k in range(1, 4):
            chip = _rel_chip(x, y, k)
            copy(t, k, chip, c, sibling).wait_recv()
            fwd = copy(t, 3 + k, chip, c, sibling)
            fwd.start()
            sends.append(fwd)
    for t in range(n):
        copy(t, 0, (x, y), 1 - c, sibling).wait_recv()
        for k in range(1, 4):
            copy(t, 3 + k, _rel_chip(x, y, k), 1 - c, sibling).wait_recv()
    for cp in sends:
        cp.wait_send()
    for mine in started:
        mine.wait()


def _handshake(peers):
    barrier = pltpu.get_barrier_semaphore()
    for peer in peers:
        pl.semaphore_signal(barrier, inc=1, device_id=peer, device_id_type=MESH)
    pl.semaphore_wait(barrier, len(peers))


def _two_level_peers():
    x, y, c = _coords()
    return [(x, y, 1 - c)] + [(*_rel_chip(x, y, k), c) for k in range(1, 4)]


SEQ_ID_GATHER, SEQ_ID_SIBLING, SEQ_ID_CHIPS = 1, 2, 3


def _sequencer_call(body, peers, operands, out_types, sems, name, collective_id, after=()):
    n_in, n_out, n_after = len(operands), len(out_types), len(after)

    def launch(*refs):
        _handshake(peers())
        body(refs[:n_in], refs[n_in + n_after:n_in + n_after + n_out], *refs[n_in + n_after + n_out:])

    return pl.kernel(
        launch, name=name, out_type=out_types, mesh=plsc.ScalarSubcoreMesh(axis_name="seq", num_cores=1),
        scratch_types=sems, compiler_params=pltpu.CompilerParams(collective_id=collective_id))(*operands, *after)


def _all_gather_seq(shards, layouts, name, after=()):
    n = len(shards)
    blocks = [_block_of(kind, width) for kind, width, _ in layouts]
    return _sequencer_call(
        lambda ins, outs, *sems: _all_gather_body(ins, outs, *sems, blocks), _two_level_peers, shards,
        [jax.ShapeDtypeStruct(shape, sh.dtype) for sh, (_, _, shape) in zip(shards, layouts)],
        [pltpu.SemaphoreType.DMA((n, 7)), pltpu.SemaphoreType.DMA((n, 7)), pltpu.SemaphoreType.DMA((n,))],
        name, SEQ_ID_GATHER, after)


def _tie(small, after, name):
    def body(*refs):
        refs[-1][...] = refs[0][...]

    vmem = pl.BlockSpec(memory_space=pltpu.VMEM)
    return pl.pallas_call(
        body, name=name, in_specs=[vmem] + [pl.BlockSpec(memory_space=pl.ANY)] * len(after), out_specs=vmem,
        out_shape=jax.ShapeDtypeStruct(small.shape, small.dtype))(small, *after)


def _rs_to_sibling(grads, layouts, name):
    n = len(grads)
    blocks = [_block_of(kind, width) for kind, width, _ in layouts]

    def body(ins, outs, send_sems, recv_sems):
        x, y, c = _coords()
        sibling = (x, y, 1 - c)
        cps = []
        for t in range(n):
            for k in range(4):
                px, py = _rel_chip(x, y, k)
                cp = pltpu.make_async_remote_copy(
                    src_ref=blocks[t](ins[t], 4 * px + 2 * py + (1 - c)), dst_ref=outs[t].at[k],
                    send_sem=send_sems.at[t, k], recv_sem=recv_sems.at[t, k], device_id=sibling, device_id_type=MESH)
                cp.start()
                cps.append(cp)
        for cp in cps:
            cp.wait_recv()
        for cp in cps:
            cp.wait_send()

    def sibling_only():
        x, y, c = _coords()
        return [(x, y, 1 - c)]

    return _sequencer_call(
        body, sibling_only, grads,
        [jax.ShapeDtypeStruct((4,) + shape, g.dtype) for g, (_, _, shape) in zip(grads, layouts)],
        [pltpu.SemaphoreType.DMA((n, 4)), pltpu.SemaphoreType.DMA((n, 4))], name, SEQ_ID_SIBLING)


def _rs_chip_sum(grad, recv, layout, xyc, name):
    kind, width, shape = layout
    r, ccols = shape

    def src_index(k, xyc_ref):
        px = jnp.where(k % 2 == 1, 1 - xyc_ref[0], xyc_ref[0])
        py = jnp.where(k // 2 == 1, 1 - xyc_ref[1], xyc_ref[1])
        return 4 * px + 2 * py + xyc_ref[2]

    if kind == "col":
        g_spec = pl.BlockSpec((r, ccols), lambda k, s_: (0, src_index(k, s_)))
    elif kind == "row":
        g_spec = pl.BlockSpec((r, ccols), lambda k, s_: (src_index(k, s_), 0))
    else:
        g_spec = pl.BlockSpec((None, r, ccols), lambda k, s_: (src_index(k, s_), 0, 0))

    def body(xyc_ref, g_ref, r_ref, o_ref):
        o_ref[...] = (g_ref[...].astype(F32) + r_ref[...].astype(F32)).astype(o_ref.dtype)

    slot = pl.BlockSpec((None, r, ccols), lambda k, s_: (k, 0, 0))
    return pl.pallas_call(
        body, name=name,
        grid_spec=pltpu.PrefetchScalarGridSpec(num_scalar_prefetch=1, grid=(4,), in_specs=[g_spec, slot], out_specs=slot),
        out_shape=jax.ShapeDtypeStruct((4, r, ccols), grad.dtype), compiler_params=_cp(VMEM_BIG))(xyc, grad, recv)


def _rs_across_chips(parts, name):
    n = len(parts)

    def body(ins, outs, send_sems, recv_sems):
        x, y, c = _coords()
        cps = []
        for t in range(n):
            for k in range(1, 4):
                px, py = _rel_chip(x, y, k)
                cp = pltpu.make_async_remote_copy(
                    src_ref=ins[t].at[k], dst_ref=outs[t].at[k - 1], send_sem=send_sems.at[t, k - 1],
                    recv_sem=recv_sems.at[t, k - 1], device_id=(px, py, c), device_id_type=MESH)
                cp.start()
                cps.append(cp)
        for cp in cps:
            cp.wait_recv()
        for cp in cps:
            cp.wait_send()

    def other_chips():
        x, y, c = _coords()
        return [(*_rel_chip(x, y, k), c) for k in range(1, 4)]

    return _sequencer_call(
        body, other_chips, parts, [jax.ShapeDtypeStruct((3,) + p.shape[1:], p.dtype) for p in parts],
        [pltpu.SemaphoreType.DMA((n, 3)), pltpu.SemaphoreType.DMA((n, 3))], name, SEQ_ID_CHIPS)


def _adamw_math(w, g, m, v):
    m = ADAM_B1 * m + (1.0 - ADAM_B1) * g
    v = ADAM_B2 * v + (1.0 - ADAM_B2) * jnp.square(g)
    m_hat = m / (1.0 - ADAM_B1 ** ADAM_STEP)
    v_hat = v / (1.0 - ADAM_B2 ** ADAM_STEP)
    delta = -ADAM_LR * (m_hat / (jnp.sqrt(v_hat) + ADAM_EPS) + ADAM_WD * w)
    return delta, m, v


def _row_tile(rows, cap):
    best = None
    for cand in range(8, min(rows, cap) + 1, 8):
        if rows % cand == 0:
            best = cand
    assert best is not None, rows
    return best


def _adamw(w, m, v, parts, name, layer=None, prev=None, tr=256):
    r, ccols = w.shape[-2:]
    tr = _row_tile(r, tr)
    npart = len(parts)

    def wspec():
        if layer is None:
            return pl.BlockSpec((tr, ccols), lambda i: (i, 0))
        return pl.BlockSpec((None, tr, ccols), lambda i: (layer, i, 0))

    def pspec(lead):
        if lead is None:
            return pl.BlockSpec((tr, ccols), lambda i: (i, 0))
        return pl.BlockSpec((None, tr, ccols), lambda i: (lead, i, 0))

    def body(*refs):
        w_ref, m_ref, v_ref = refs[:3]
        p_refs = refs[3:3 + npart]
        outs = refs[len(refs) - 4:]
        g = p_refs[0][...].astype(F32)
        for p_ref in p_refs[1:]:
            g = g + p_ref[...].astype(F32)
        delta, mn, vn = _adamw_math(w_ref[...], g, m_ref[...], v_ref[...])
        outs[0][...] = g
        outs[1][...] = delta
        outs[2][...] = mn
        outs[3][...] = vn

    operands = [w, m, v] + [p for p, _ in parts]
    in_specs = [wspec(), wspec(), wspec()] + [pspec(lead) for _, lead in parts]
    aliases = {}
    if prev is not None:
        for i, p in enumerate(prev):
            aliases[len(operands)] = i
            operands.append(p)
            in_specs.append(pl.BlockSpec(memory_space=pl.ANY))
    return pl.pallas_call(
        body, name=name, grid=(r // tr,), in_specs=in_specs, out_specs=[wspec()] * 4,
        out_shape=[jax.ShapeDtypeStruct(w.shape, F32)] * 4, input_output_aliases=aliases)(*operands)


def _sum8(buf, name):
    _, r, ccols = buf.shape

    def body(b_ref, o_ref):
        acc = b_ref[0]
        for j in range(1, N_DEV):
            acc = acc + b_ref[j]
        o_ref[...] = acc

    tr = _row_tile(r, 256)
    return pl.pallas_call(
        body, name=name, grid=(r // tr,), in_specs=[pl.BlockSpec((N_DEV, tr, ccols), lambda i: (0, i, 0))],
        out_specs=pl.BlockSpec((tr, ccols), lambda i: (i, 0)), out_shape=jax.ShapeDtypeStruct((r, ccols), F32))(buf)


def _pack(arrays):
    pieces, layout, off = [], [], 0
    for a in arrays:
        n = a.size
        padded = -(-n // 1024) * 1024
        flat = a.reshape(-1).astype(F32)
        if padded != n:
            flat = jnp.pad(flat, (0, padded - n))
        pieces.append(flat.reshape(padded // 128, 128))
        layout.append((off, n, a.shape))
        off += padded // 128
    return jnp.concatenate(pieces, axis=0), layout


def _unpack(packed, layout):
    out = []
    for off, n, shape in layout:
        rows = -(-n // 1024) * 8
        out.append(packed[off:off + rows].reshape(-1)[:n].reshape(shape))
    return out


def kernel(x, mem, norm_mix, norm_ffn, mem_norm, w_kv, w_out, w_ffn1, w_ffn2, a_in, a_ln_g, a_ln_b, a_ws, a_bs, b_in, b_conv_w, b_conv_b, b_dt_bias, b_a_log, b_d, b_gnorm, final_norm, loss_target, m_norm_mix, m_norm_ffn, m_mem_norm, m_w_kv, m_w_out, m_w_ffn1, m_w_ffn2, m_a_in, m_a_ln_g, m_a_ln_b, m_a_ws, m_a_bs, m_b_in, m_b_conv_w, m_b_conv_b, m_b_dt_bias, m_b_a_log, m_b_d, m_b_gnorm, m_final_norm, v_norm_mix, v_norm_ffn, v_mem_norm, v_w_kv, v_w_out, v_w_ffn1, v_w_ffn2, v_a_in, v_a_ln_g, v_a_ln_b, v_a_ws, v_a_bs, v_b_in, v_b_conv_w, v_b_conv_b, v_b_dt_bias, v_b_a_log, v_b_d, v_b_gnorm, v_final_norm):
    s = x.shape[1]
    xs = x.reshape(s, D_MODEL)
    mems = mem.reshape(N_MEM, D_MODEL)
    target = loss_target.reshape(s, D_MODEL)
    ax, ay, ac = lax.axis_index("x"), lax.axis_index("y"), lax.axis_index("c")
    me = 4 * ax + 2 * ay + ac
    xyc = jnp.stack([ax, ay, ac]).astype(jnp.int32)

    b_cols = b_in.shape[2]
    act = lambda a: a.astype(_ACT)
    lay_f1, lay_f2 = ("col", 512, (1, D_MODEL, D_FF)), ("row", 512, (1, D_FF, D_MODEL))
    lay_out, lay_kv = ("row", 384, (1, 3 * D_MODEL, D_MODEL)), ("col", 256, (1, D_MODEL, 2 * X_WIDTH))
    small_w_pack = _pack([b_conv_w[0], b_conv_b[0], b_gnorm[0]])[0]
    WA, wkv0 = _all_gather_seq([act(a_in), act(w_kv[0:1])], [("col", 640, (1, D_MODEL, 5 * D_MODEL)), lay_kv], "ag_proj_a")
    (wo0,) = _all_gather_seq([act(w_out[0:1])], [lay_out], "ag_out0")
    w1_0, w2_0 = _all_gather_seq([act(w_ffn1[0:1]), act(w_ffn2[0:1])], [lay_f1, lay_f2], "ag_ffn0")
    a0 = _rms_fwd(xs, norm_mix[0].reshape(1, -1), "mix_norm0")
    wb_blk, wkv1, small_w = _all_gather_seq(
        [act(b_in[0]), act(w_kv[1:2]), small_w_pack],
        [("blk", 0, (N_DEV, D_MODEL, b_cols)), lay_kv, ("blk", 0, (N_DEV, 32, 128))], "ag_proj_b", after=[a0])
    (wo1,) = _all_gather_seq([act(w_out[1:2])], [lay_out], "ag_out1", after=[a0])
    w1_1, w2_1 = _all_gather_seq([act(w_ffn1[1:2]), act(w_ffn2[1:2])], [lay_f1, lay_f2], "ag_ffn1", after=[a0])
    W1, W2, WO, WKV = [w1_0, w1_1], [w2_0, w2_1], [wo0, wo1], [wkv0, wkv1]
    wb_full = jnp.transpose(wb_blk, (1, 0, 2)).reshape(D_MODEL, N_DEV * b_cols)
    dt0 = D_INNER + CONV_DIM
    WB = jnp.concatenate([wb_full[:, :dt0], wb_full[:, dt0 + SSM_HEADS:]], axis=1)
    WBDT = jnp.pad(wb_full[:, dt0:dt0 + SSM_HEADS], ((0, 0), (0, HPAD - SSM_HEADS)))

    row = lambda a: a.reshape(1, -1)
    nmix = [row(norm_mix[0]), row(norm_mix[1])]
    nffn = [row(norm_ffn[0]), row(norm_ffn[1])]
    nmem = [row(mem_norm[0]), row(mem_norm[1])]
    fin = row(final_norm)
    lng, lnb = a_ln_g.reshape(1, D_INNER), a_ln_b.reshape(1, D_INNER)
    ws = a_ws[0]
    bs3 = a_bs[0].reshape(A_GROUPS, CHUNK, 1)
    pad_h = lambda a: jnp.pad(a.reshape(-1), (0, HPAD - SSM_HEADS))
    bias_row = pad_h(b_dt_bias).reshape(1, HPAD)
    alog_row = pad_h(b_a_log).reshape(1, HPAD)
    dfull = jnp.repeat(b_d.reshape(-1), SSM_P).reshape(1, D_INNER)

    cw_sh, cb_sh, gn_sh = 4 * 384, 384, 256
    sw = small_w.reshape(N_DEV, 32 * 128)
    conv_w = jnp.transpose(sw[:, :cw_sh].reshape(N_DEV, CONV_K, 384), (1, 0, 2)).reshape(CONV_K, CONV_DIM)
    conv_b = sw[:, 2048:2048 + cb_sh].reshape(1, CONV_DIM)
    gnorm = sw[:, 3072:3072 + gn_sh].reshape(1, D_INNER)

    kvs, mns = [None, None], [None, None]

    def mem_kv(i, after=None):
        gain = nmem[i] if after is None else _tie(nmem[i], after, f"tie_mem{i}")
        mns[i] = _rms_fwd(mems, gain, f"mem_norm{i}")
        kvs[i] = _mm(mns[i], WKV[i], m=N_MEM, n=2 * X_WIDTH, k=D_MODEL, b_at=(0, 0, 0), out_dtype=_ACT, name=f"kv{i}")

    def ffn_fwd(h, i):
        f = _rms_fwd(h, nffn[i], f"ffn_norm{i}")
        p = _mm(f, W1[i], m=s, n=D_FF, k=D_MODEL, b_at=(0, 0, 0), out_dtype=_ACT, name=f"ffn_up{i}")
        hn = _mm(p, W2[i], m=s, n=D_MODEL, k=D_FF, b_at=(0, 0, 0), a_pro="relu2", add=h, name=f"ffn_down{i}")
        return f, p, hn

    def out_proj(h, cat, i):
        return _mm(cat, WO[i], m=s, n=D_MODEL, k=3 * D_MODEL, b_at=(0, 0, 0), add=h, name=f"out_proj{i}")

    proj_a = _mm(a0, WA, m=s, n=5 * D_MODEL, k=D_MODEL, b_at=(0, 0, 0), name="proj_a")
    mem_kv(0)
    cat_a = _gmlp_fwd(proj_a, lng, lnb, ws, bs3, "gmlp_fwd")
    cat_a = _attn_fwd(proj_a, 4, kvs[0], cat_a, "attn_fwd0")
    h1 = out_proj(xs, cat_a, 0)
    f0, p0, h2 = ffn_fwd(h1, 0)

    a1 = _rms_fwd(h2, nmix[1], "mix_norm1")
    proj_b = _mm(a1, WB, m=s, n=6 * D_MODEL, k=D_MODEL, name="proj_b")
    dt_raw = _mm(a1, WBDT, m=s, n=HPAD, k=D_MODEL, name="proj_dt")
    xbc = _conv_fwd(proj_b, conv_w, conv_b, "conv_fwd")
    y_ssd, states = _ssd_fwd(xbc, dt_raw, bias_row, alog_row, dfull, "ssd_fwd")
    cat_b = _gate_fwd(y_ssd, proj_b, gnorm, "gate_fwd")
    mem_kv(1, after=[cat_b])
    cat_b = _attn_fwd(proj_b, 5, kvs[1], cat_b, "attn_fwd1")
    h3 = out_proj(h2, cat_b, 1)
    f1, p1, h4 = ffn_fwd(h3, 1)

    loss_part, dh, dh_act, d_fin = _loss_head(h4, fin, target, "loss_head")

    g_f1, g_f2, g_out, g_kv = [None, None], [None, None], [None, None], [None, None]
    d_nffn, d_nmix, d_nmem = [None, None], [None, None], [None, None]

    def ffn_bwd(dh, dh_act, h_in, f, p, i, after=()):
        dp = _mm(dh_act, W2[i], m=s, n=D_FF, k=D_MODEL, tb=True, b_at=(0, 0, 0), epi_p=p, out_dtype=_ACT, name=f"ffn_down_dx{i}")
        g_f2[i] = _mm(p, dh_act, m=D_FF, n=D_MODEL, k=s, ta=True, a_pro="relu2", out_dtype=_ACT, name=f"ffn_down_dw{i}")
        g_f1[i] = _mm(f, dp, m=D_MODEL, n=D_FF, k=s, ta=True, out_dtype=_ACT, name=f"ffn_up_dw{i}")
        df = _mm(dp, W1[i], m=s, n=D_MODEL, k=D_FF, tb=True, b_at=(0, 0, 0), after=after, name=f"ffn_up_dx{i}")
        dh_in, dh_in_act, d_nffn[i] = _rms_bwd(h_in, nffn[i], df, dh, f"ffn_norm_bwd{i}")
        return dh_in, dh_in_act

    def out_bwd(dh_act, cat, i):
        dcat = _mm(dh_act, WO[i], m=s, n=3 * D_MODEL, k=D_MODEL, tb=True, b_at=(0, 0, 0), out_dtype=_ACT, name=f"out_dx{i}")
        g_out[i] = _mm(cat, dh_act, m=3 * D_MODEL, n=D_MODEL, k=s, ta=True, out_dtype=_ACT, name=f"out_dw{i}")
        return dcat

    def mem_bwd(dkv, i):
        g_kv[i] = _mm(mns[i], dkv, m=D_MODEL, n=2 * X_WIDTH, k=N_MEM, ta=True, out_dtype=_ACT, name=f"kv_dw{i}")
        dmn = _mm(dkv, WKV[i], m=N_MEM, n=D_MODEL, k=2 * X_WIDTH, tb=True, b_at=(0, 0, 0), name=f"kv_dx{i}")
        _, _, d_nmem[i] = _rms_bwd(mems, nmem[i], dmn, None, f"mem_norm_bwd{i}")

    lay_g = {"f1": ("col", 512, (D_MODEL, 512)), "f2": ("row", 512, (512, D_MODEL)), "out": ("row", 384, (384, D_MODEL)),
             "kv": ("col", 256, (D_MODEL, 256)), "a": ("col", 640, (D_MODEL, 640)), "b": ("blk", 0, (D_MODEL, b_cols))}
    reduced = {}

    def reduce_scatter(group, tag):
        grads3, lays3 = [], []
        for fam, _, g in group:
            kind, width, shape = lay_g[fam]
            grads3.append(g if kind == "blk" else g.reshape((1,) + g.shape))
            lays3.append((kind, width, shape if kind == "blk" else (1,) + shape))
        recv1 = _rs_to_sibling(grads3, lays3, f"rs_sibling_{tag}")
        parts = [_rs_chip_sum(g, recv1[t].reshape((4,) + lay_g[fam][2]), lay_g[fam], xyc, f"rs_chip_sum_{fam}{i}")
                 for t, (fam, i, g) in enumerate(group)]
        recv2 = _rs_across_chips(parts, f"rs_chips_{tag}")
        for (fam, i, _), p, r2 in zip(group, parts, recv2):
            reduced[fam, i] = (p, r2)
        return parts, recv2

    dh3, dh3_act = ffn_bwd(dh, dh_act, h3, f1, p1, 1)
    dcat_b = out_bwd(dh3_act, cat_b, 1)
    sums, got_ffn1 = reduce_scatter([("f1", 1, g_f1[1]), ("f2", 1, g_f2[1]), ("out", 1, g_out[1])], "ffn1")
    dy_ssd, dproj_b, d_gnorm = _gate_bwd(y_ssd, proj_b, gnorm, dcat_b, "gate_bwd")
    dproj_b, dkv_b = _attn_bwd(proj_b, 5, kvs[1], dcat_b, dproj_b, "attn_bwd1")
    mem_bwd(dkv_b, 1)
    dxbc, ddt_raw, d_alog, d_dskip, d_dtbias = _ssd_bwd(
        xbc, dt_raw, _tie(bias_row, sums, "tie_ffn1"), alog_row, dfull, dy_ssd, states, "ssd_bwd")
    dproj_b, d_convw, d_convb = _conv_bwd(proj_b, conv_w, _tie(conv_b, got_ffn1, "tie_got_ffn1"), dxbc, dproj_b, "conv_bwd")
    gb = _mm(a1, dproj_b, m=D_MODEL, n=6 * D_MODEL, k=s, ta=True, out_dtype=_ACT, name="proj_b_dw")
    gb_dt = _mm(a1, ddt_raw, m=D_MODEL, n=HPAD, k=s, ta=True, out_dtype=_ACT, name="proj_b_dw_dt")
    gb_full = jnp.concatenate([gb[:, :dt0], gb_dt[:, :SSM_HEADS], gb[:, dt0:]], axis=1)
    gb_blk = jnp.transpose(gb_full.reshape(D_MODEL, N_DEV, b_cols), (1, 0, 2))
    sums, got_mix1 = reduce_scatter([("kv", 1, g_kv[1]), ("b", 0, gb_blk)], "mix1")
    da1 = _mm(dproj_b, WB, m=s, n=D_MODEL, k=6 * D_MODEL, tb=True, name="proj_b_dx")
    da1 = _mm(ddt_raw, WBDT, m=s, n=D_MODEL, k=HPAD, tb=True, add=da1, name="proj_b_dx_dt")
    dh2, dh2_act, d_nmix[1] = _rms_bwd(h2, _tie(nmix[1], sums, "tie_mix1"), da1, dh3, "mix_norm_bwd1")

    dh1, dh1_act = ffn_bwd(dh2, dh2_act, h1, f0, p0, 0, after=got_ffn1)
    dcat_a = out_bwd(dh1_act, cat_a, 0)
    sums, _ = reduce_scatter([("f1", 0, g_f1[0]), ("f2", 0, g_f2[0]), ("out", 0, g_out[0])], "ffn0")
    dproj_a, d_ws, d_bs3, d_lng, d_lnb = _gmlp_bwd(proj_a, dcat_a, _tie(lng, got_mix1, "tie_got_mix1"), lnb, ws, bs3, "gmlp_bwd")
    dproj_a, dkv_a = _attn_bwd(proj_a, 4, kvs[0], dcat_a, dproj_a, "attn_bwd0")
    mem_bwd(dkv_a, 0)
    ga = _mm(a0, dproj_a, m=D_MODEL, n=5 * D_MODEL, k=s, ta=True, out_dtype=_ACT, after=sums, name="proj_a_dw")
    sums, _ = reduce_scatter([("kv", 0, g_kv[0]), ("a", 0, ga)], "mix0")
    da0 = _mm(dproj_a, WA, m=s, n=D_MODEL, k=5 * D_MODEL, tb=True, b_at=(0, 0, 0), name="proj_a_dx")
    grad_x, _, d_nmix[0] = _rms_bwd(xs, _tie(nmix[0], sums, "tie_mix0"), da0, dh1, "mix_norm_bwd0")

    def big_update(w, m, v, fam, nlayer):
        res = None
        for i in range(nlayer):
            part, recv2 = reduced[fam, i]
            plist = [(part, 0), (recv2, 0), (recv2, 1), (recv2, 2)]
            res = _adamw(w, m, v, plist, f"adamw_{fam}{i}", layer=i, prev=res)
        return res

    r_f1 = big_update(w_ffn1, m_w_ffn1, v_w_ffn1, "f1", 2)
    r_f2 = big_update(w_ffn2, m_w_ffn2, v_w_ffn2, "f2", 2)
    r_out = big_update(w_out, m_w_out, v_w_out, "out", 2)
    r_kv = big_update(w_kv, m_w_kv, v_w_kv, "kv", 2)
    r_a = big_update(a_in, m_a_in, v_a_in, "a", 1)
    r_b = big_update(b_in, m_b_in, v_b_in, "b", 1)

    rep_names = ["norm_mix", "norm_ffn", "mem_norm", "a_ln_g", "a_ln_b", "a_ws", "a_bs", "b_dt_bias", "b_a_log", "b_d",
                 "final_norm"]
    rep_grads = [jnp.concatenate(d_nmix, axis=0), jnp.concatenate(d_nffn, axis=0), jnp.concatenate(d_nmem, axis=0),
                 d_lng, d_lnb, d_ws.reshape(1, A_GROUPS, CHUNK, CHUNK), d_bs3.reshape(1, A_GROUPS, CHUNK),
                 d_dtbias[:, :SSM_HEADS], d_alog[:, :SSM_HEADS], d_dskip[:, :SSM_HEADS], d_fin.reshape(D_MODEL)]
    rep_w = [norm_mix, norm_ffn, mem_norm, a_ln_g, a_ln_b, a_ws, a_bs, b_dt_bias, b_a_log, b_d, final_norm]
    rep_m = [m_norm_mix, m_norm_ffn, m_mem_norm, m_a_ln_g, m_a_ln_b, m_a_ws, m_a_bs, m_b_dt_bias, m_b_a_log, m_b_d, m_final_norm]
    rep_v = [v_norm_mix, v_norm_ffn, v_mem_norm, v_a_ln_g, v_a_ln_b, v_a_ws, v_a_bs, v_b_dt_bias, v_b_a_log, v_b_d, v_final_norm]
    rep_grads = [g.reshape(w.shape) for g, w in zip(rep_grads, rep_w)]
    sh_grads = [d_convw, d_convb, d_gnorm]
    g_pack, g_layout = _pack(rep_grads + sh_grads + [loss_part])
    n_rep = len(rep_grads)
    (g_all,) = _all_gather([g_pack], [("blk", 0, (N_DEV,) + g_pack.shape)], "ag_small_grads")
    g_small = _sum8(g_all, "sum_small_grads")
    g_list = _unpack(g_small, g_layout)
    loss = g_list[-1][0, 0]
    wp, w_layout = _pack(rep_w)
    mp, _ = _pack(rep_m)
    vp, _ = _pack(rep_v)
    gp, _ = _pack(g_list[:n_rep])
    rep_res = [_unpack(o, w_layout) for o in _adamw(wp, mp, vp, [(gp, None)], "adamw_replicated", tr=88)]

    gcw = lax.dynamic_slice_in_dim(g_list[n_rep], me * 384, 384, axis=1).reshape(1, CONV_K, 384)
    gcb = lax.dynamic_slice_in_dim(g_list[n_rep + 1], me * 384, 384, axis=1)
    ggn = lax.dynamic_slice_in_dim(g_list[n_rep + 2], me * 256, 256, axis=1)
    sh_w = [b_conv_w, b_conv_b, b_gnorm]
    sh_m = [m_b_conv_w, m_b_conv_b, m_b_gnorm]
    sh_v = [v_b_conv_w, v_b_conv_b, v_b_gnorm]
    swp, sw_layout = _pack(sh_w)
    smp, _ = _pack(sh_m)
    svp, _ = _pack(sh_v)
    sgp, _ = _pack([gcw, gcb, ggn])
    sh_res = [_unpack(o, sw_layout) for o in _adamw(swp, smp, svp, [(sgp, None)], "adamw_sharded_small", tr=8)]

    names = ["norm_mix", "norm_ffn", "mem_norm", "w_kv", "w_out", "w_ffn1", "w_ffn2", "a_in", "a_ln_g", "a_ln_b", "a_ws",
             "a_bs", "b_in", "b_conv_w", "b_conv_b", "b_dt_bias", "b_a_log", "b_d", "b_gnorm", "final_norm"]
    big = {"w_kv": r_kv, "w_out": r_out, "w_ffn1": r_f1, "w_ffn2": r_f2, "a_in": r_a, "b_in": r_b}
    sh_names = ["b_conv_w", "b_conv_b", "b_gnorm"]
    outs = [loss, grad_x.reshape(x.shape)]
    for kind in range(4):
        for nm in names:
            if nm in big:
                outs.append(big[nm][kind])
            elif nm in sh_names:
                outs.append(sh_res[kind][sh_names.index(nm)])
            else:
                outs.append(rep_res[kind][rep_names.index(nm)])
    return tuple(outs)
```

```python
import functools
import math

import jax
import jax.numpy as jnp
from jax import lax
from jax.experimental import pallas as pl
from jax.experimental.pallas import tpu as pltpu
from jax.experimental.pallas import tpu_sc as plsc

F32 = jnp.float32
_MXU = jnp.bfloat16
_ACT = jnp.bfloat16
_HI = lax.Precision.HIGHEST

D_MODEL = 1024
CHUNK = 128
N_MEM = 256
D_INNER = 2048
A_GROUPS = 8
A_GW = D_INNER // A_GROUPS
SSM_HEADS = 32
SSM_P = 64
SSM_GROUPS = 4
SSM_GW = D_INNER // SSM_GROUPS
SSM_N = 128
CONV_K = 4
CONV_DIM = 3072
X_HEADS = 4
X_HD = 256
X_WIDTH = 1024
D_FF = 4096
EPS = 1e-6
HPAD = 128
N_DEV = 8

ADAM_LR = 0.001
ADAM_B1 = 0.9
ADAM_B2 = 0.999
ADAM_EPS = 1e-08
ADAM_WD = 0.01
ADAM_STEP = 10

VMEM_BIG = 56 * 1024 * 1024
MESH = pl.DeviceIdType.MESH


def _cp(vmem=None):
    if vmem is None:
        return pltpu.CompilerParams()
    return pltpu.CompilerParams(vmem_limit_bytes=vmem)


def _dot(a, b, dims=((1,), (0,))):
    return lax.dot_general(a.astype(_MXU), b.astype(_MXU), (dims, ((), ())), preferred_element_type=F32)


def _dot_nt(a, b):
    return _dot(a, b, ((1,), (1,)))


def _dot_tn(a, b):
    return _dot(a, b, ((0,), (0,)))


def _dot_hi(a, b, dims=((1,), (0,))):
    return lax.dot_general(a.astype(F32), b.astype(F32), (dims, ((), ())), precision=_HI, preferred_element_type=F32)


def _split3(x):
    x1 = x.astype(jnp.bfloat16)
    r = x - x1.astype(F32)
    x2 = r.astype(jnp.bfloat16)
    x3 = (r - x2.astype(F32)).astype(jnp.bfloat16)
    return x1, x2, x3


def _dot_sel(x, sel, dims=((1,), (0,))):
    sel = sel.astype(jnp.bfloat16)
    parts = [lax.dot_general(t, sel, (dims, ((), ())), preferred_element_type=F32) for t in _split3(x)]
    return (parts[0] + parts[1]) + parts[2]


def _sel_dot(sel, x, dims=((1,), (0,))):
    sel = sel.astype(jnp.bfloat16)
    parts = [lax.dot_general(sel, t, (dims, ((), ())), preferred_element_type=F32) for t in _split3(x)]
    return (parts[0] + parts[1]) + parts[2]


def _sigmoid(x):
    return 1.0 / (1.0 + jnp.exp(-x))


def _gelu(x):
    return 0.5 * x * (1.0 + lax.erf(x * (1.0 / math.sqrt(2.0))))


def _gelu_grad(x):
    return 0.5 * (1.0 + lax.erf(x * (1.0 / math.sqrt(2.0)))) + x * jnp.exp(-0.5 * x * x) * (1.0 / math.sqrt(2.0 * math.pi))


def _softplus(x):
    return jnp.maximum(x, 0.0) + jnp.log1p(jnp.exp(-jnp.abs(x)))


def _iota(shape, dim):
    return lax.broadcasted_iota(jnp.int32, shape, dim)


MM_VMEM_BUDGET = 40 * 1024 * 1024
HBM_BYTES_PER_S = 2.5e12
GRID_STEP_S = 0.35e-6
VMEM_ACC_BYTES_PER_S = 6e12


def _divisors(dim, unit):
    out = [d for d in range(unit, min(dim, 2048) + 1, unit) if dim % d == 0]
    return out if out else [dim]


def _mm_tiles(m, n, k, sa, sb, s_mn, a_pro, offsets):
    best = None
    (a_r0, a_c0, ta), (b_r0, b_c0, tb), (o_r0, o_c0) = offsets
    for tm in _divisors(m, 128):
        for tn in _divisors(n, 128):
            for tk in [k // d for d in (1, 2, 3, 4, 6, 8) if k % d == 0 and (k // d) % 128 == 0]:
                a_t = (tk, tm) if ta else (tm, tk)
                b_t = (tn, tk) if tb else (tk, tn)
                if a_r0 % a_t[0] or a_c0 % a_t[1] or b_r0 % b_t[0] or b_c0 % b_t[1] or o_r0 % tm or o_c0 % tn:
                    continue
                nk = k // tk
                vmem = 2 * (tm * tk * sa + tk * tn * sb + tm * tn * s_mn) + tm * tn * 4 * (2 if nk > 1 else 1)
                if a_pro or sa == 4:
                    vmem += tm * tk * 6
                if sb == 4:
                    vmem += tk * tn * 2
                if vmem > MM_VMEM_BUDGET:
                    continue
                gi, gj = m // tm, n // tn
                for j_inner in (True, False):
                    if nk > 1:
                        traffic = gj * m * k * sa + gi * k * n * sb
                    elif j_inner:
                        traffic = m * k * sa + gi * k * n * sb
                    else:
                        traffic = gj * m * k * sa + k * n * sb
                    traffic += m * n * s_mn + (tm * tk * sa + tk * tn * sb)
                    cost = traffic / HBM_BYTES_PER_S + gi * gj * nk * GRID_STEP_S
                    if nk > 1:
                        cost += m * n * 8 * nk / VMEM_ACC_BYTES_PER_S
                    if best is None or cost < best[0]:
                        best = (cost, tm, tn, tk, j_inner)
    assert best is not None, (m, n, k)
    return best[1:]


def _mm(a, b, *, m, n, k, name, ta=False, tb=False, a_at=(None, 0, 0), b_at=(None, 0, 0),
        out_dtype=F32, add=None, epi_p=None, epi_at=(None, 0, 0), out=None, out_at=(None, 0, 0),
        out_full=None, a_pro=None, after=()):
    s_mn =jnp.dtype(out.dtype if out is not None else out_dtype).itemsize
    s_mn += add.dtype.itemsize if add is not None else 0
    s_mn += epi_p.dtype.itemsize if epi_p is not None else 0
    tm, tn, tk, j_inner = _mm_tiles(m, n, k, a.dtype.itemsize, b.dtype.itemsize, s_mn, a_pro is not None,
                                    ((a_at[1], a_at[2], ta), (b_at[1], b_at[2], tb), (out_at[1], out_at[2])))
    nk = k // tk

    def spec(at, tr, tc, rsel, csel):
        lead, r0, c0 = at
        assert r0 % tr == 0 and c0 % tc == 0, (name, at, tr, tc)
        rb, cb = r0 // tr, c0 // tc
        if lead is None:
            return pl.BlockSpec((tr, tc), lambda g0, g1, kk: (rb + rsel(g0, g1, kk), cb + csel(g0, g1, kk)))
        return pl.BlockSpec((None, tr, tc), lambda g0, g1, kk: (lead, rb + rsel(g0, g1, kk), cb + csel(g0, g1, kk)))

    gi = (lambda g0, g1, kk: g0) if j_inner else (lambda g0, g1, kk: g1)
    gj = (lambda g0, g1, kk: g1) if j_inner else (lambda g0, g1, kk: g0)
    gk = lambda g0, g1, kk: kk
    a_spec = spec(a_at, tk, tm, gk, gi) if ta else spec(a_at, tm, tk, gi, gk)
    b_spec = spec(b_at, tn, tk, gj, gk) if tb else spec(b_at, tk, tn, gk, gj)
    dims = ((0,), (0,)) if ta else (((1,), (1,)) if tb else ((1,), (0,)))
    assert not (ta and tb)

    operands, in_specs = [a, b], [a_spec, b_spec]
    if add is not None:
        operands.append(add)
        in_specs.append(spec((None, 0, 0), tm, tn, gi, gj))
    if epi_p is not None:
        operands.append(epi_p)
        in_specs.append(spec(epi_at, tm, tn, gi, gj))
    aliases = {}
    if out is not None:
        aliases = {len(operands): 0}
        operands.append(out)
        in_specs.append(pl.BlockSpec(memory_space=pl.ANY))
        out_struct = jax.ShapeDtypeStruct(out.shape, out.dtype)
        out_dtype = out.dtype
    else:
        out_struct = jax.ShapeDtypeStruct(out_full if out_full is not None else (m, n), out_dtype)
    has_add, has_epi = add is not None, epi_p is not None
    n_skip = (1 if out is not None else 0) + len(after)
    operands += list(after)
    in_specs += [pl.BlockSpec(memory_space=pl.ANY)] * len(after)

    def body(*refs):
        a_ref, b_ref = refs[0], refs[1]
        pos = 2
        add_ref = epi_ref = None
        if has_add:
            add_ref = refs[pos]
            pos += 1
        if has_epi:
            epi_ref = refs[pos]
            pos += 1
        pos += n_skip
        o_ref = refs[pos]

        def finish(r):
            if has_add:
                r = r + add_ref[...].astype(F32)
            if has_epi:
                r = r * (2.0 * jnp.maximum(epi_ref[...].astype(F32), 0.0))
            o_ref[...] = r.astype(o_ref.dtype)

        av = a_ref[...]
        if a_pro == "relu2":
            av = jnp.square(jnp.maximum(av.astype(F32), 0.0))
        part = _dot(av, b_ref[...], dims)
        if nk == 1:
            finish(part)
        else:
            acc_ref = refs[pos + 1]
            kk = pl.program_id(2)

            @pl.when(kk == 0)
            def _():
                acc_ref[...] = part

            @pl.when(kk > 0)
            def _():
                acc_ref[...] += part

            @pl.when(kk == nk - 1)
            def _():
                finish(acc_ref[...])

    grid = (m // tm, n // tn, nk) if j_inner else (n // tn, m // tm, nk)
    return pl.pallas_call(
        body, name=name, grid=grid, in_specs=in_specs,
        out_specs=spec(out_at, tm, tn, gi, gj), out_shape=out_struct,
        scratch_shapes=[pltpu.VMEM((tm, tn), F32)] if nk > 1 else [], input_output_aliases=aliases,
        compiler_params=_cp(VMEM_BIG))(*operands)


def _rms_fwd(x, g, name, tm=256):
    s, d = x.shape
    tm = min(tm, s)

    def body(x_ref, g_ref, o_ref):
        xv = x_ref[...]
        r = lax.rsqrt(jnp.mean(xv * xv, axis=-1, keepdims=True) + EPS)
        o_ref[...] = (xv * r * g_ref[...]).astype(o_ref.dtype)

    return pl.pallas_call(
        body, name=name, grid=(s // tm,),
        in_specs=[pl.BlockSpec((tm, d), lambda i: (i, 0)), pl.BlockSpec((1, d), lambda i: (0, 0))],
        out_specs=pl.BlockSpec((tm, d), lambda i: (i, 0)),
        out_shape=jax.ShapeDtypeStruct((s, d), _ACT))(x, g)


def _rms_bwd(x, g, dy, dres, name, tm=256):
    s, d = x.shape
    tm = min(tm, s)
    has_res = dres is not None

    def body(*refs):
        if has_res:
            x_ref, g_ref, dy_ref, dres_ref, dx_ref, dxa_ref, dg_ref = refs
        else:
            x_ref, g_ref, dy_ref, dx_ref, dxa_ref, dg_ref = refs

        @pl.when(pl.program_id(0) == 0)
        def _():
            dg_ref[...] = jnp.zeros_like(dg_ref)

        xv = x_ref[...]
        dyv = dy_ref[...].astype(F32)
        r = lax.rsqrt(jnp.mean(xv * xv, axis=-1, keepdims=True) + EPS)
        xh = xv * r
        dyg = dyv * g_ref[...]
        dx = r * (dyg - xh * jnp.mean(dyg * xh, axis=-1, keepdims=True))
        if has_res:
            dx = dx + dres_ref[...]
        dx_ref[...] = dx
        dxa_ref[...] = dx.astype(dxa_ref.dtype)
        dg_ref[...] += jnp.sum(dyv * xh, axis=0, keepdims=True)

    row = pl.BlockSpec((tm, d), lambda i: (i, 0))
    vec = pl.BlockSpec((1, d), lambda i: (0, 0))
    in_specs = [row, vec, row] + ([row] if has_res else [])
    operands = [x, g, dy] + ([dres] if has_res else [])
    return pl.pallas_call(
        body, name=name, grid=(s // tm,), in_specs=in_specs, out_specs=[row, row, vec],
        out_shape=[jax.ShapeDtypeStruct((s, d), F32), jax.ShapeDtypeStruct((s, d), _ACT),
                   jax.ShapeDtypeStruct((1, d), F32)])(*operands)


def _loss_head(h, g, target, name, tm=256):
    s, d = h.shape
    tm = min(tm, s)

    def body(h_ref, g_ref, t_ref, loss_ref, dh_ref, dha_ref, dg_ref):
        @pl.when(pl.program_id(0) == 0)
        def _():
            dg_ref[...] = jnp.zeros_like(dg_ref)
            loss_ref[...] = jnp.zeros_like(loss_ref)

        xv = h_ref[...]
        r = lax.rsqrt(jnp.mean(xv * xv, axis=-1, keepdims=True) + EPS)
        xh = xv * r
        err = xh * g_ref[...] - t_ref[...]
        loss_ref[...] += jnp.full(loss_ref.shape, 0.5 * jnp.sum(jnp.mean(err * err, axis=-1, keepdims=True)), F32)
        dyv = err * (1.0 / d)
        dyg = dyv * g_ref[...]
        dh = r * (dyg - xh * jnp.mean(dyg * xh, axis=-1, keepdims=True))
        dh_ref[...] = dh
        dha_ref[...] = dh.astype(dha_ref.dtype)
        dg_ref[...] += jnp.sum(dyv * xh, axis=0, keepdims=True)

    row = pl.BlockSpec((tm, d), lambda i: (i, 0))
    vec = pl.BlockSpec((1, d), lambda i: (0, 0))
    return pl.pallas_call(
        body, name=name, grid=(s // tm,), in_specs=[row, vec, row],
        out_specs=[pl.BlockSpec((1, 128), lambda i: (0, 0)), row, row, vec],
        out_shape=[jax.ShapeDtypeStruct((1, 128), F32), jax.ShapeDtypeStruct((s, d), F32),
                   jax.ShapeDtypeStruct((s, d), _ACT), jax.ShapeDtypeStruct((1, d), F32)])(h, g, target)


def _gmlp_parts(pu, pv, lng, lnb):
    u = _gelu(pu)
    v = _gelu(pv)
    mu = jnp.mean(v, axis=-1, keepdims=True)
    vc = v - mu
    rstd = lax.rsqrt(jnp.mean(vc * vc, axis=-1, keepdims=True) + EPS)
    xhat = vc * rstd
    vn = xhat * lng + lnb
    return u, xhat, rstd, vn


def _gmlp_fwd(proj, lng, lnb, ws, bs3, name):
    s = proj.shape[0]

    def body(pu_ref, pv_ref, lng_ref, lnb_ref, ws_ref, bs_ref, o_ref):
        u, _, _, vn = _gmlp_parts(pu_ref[...], pv_ref[...], lng_ref[...], lnb_ref[...])
        causal = _iota((CHUNK, CHUNK), 0) >= _iota((CHUNK, CHUNK), 1)
        for g in range(A_GROUPS):
            sl = slice(g * A_GW, (g + 1) * A_GW)
            w = jnp.where(causal, ws_ref[g], 0.0)
            sv = _dot(w, vn[:, sl]) + bs_ref[g]
            o_ref[:, sl] = (u[:, sl] * sv).astype(o_ref.dtype)

    full = lambda shape: pl.BlockSpec(shape, lambda c: (0,) * len(shape))
    return pl.pallas_call(
        body, name=name, grid=(s // CHUNK,),
        in_specs=[pl.BlockSpec((CHUNK, D_INNER), lambda c: (c, 0)), pl.BlockSpec((CHUNK, D_INNER), lambda c: (c, 1)),
                  full((1, D_INNER)), full((1, D_INNER)), full((A_GROUPS, CHUNK, CHUNK)), full((A_GROUPS, CHUNK, 1))],
        out_specs=pl.BlockSpec((CHUNK, D_INNER), lambda c: (c, 0)),
        out_shape=jax.ShapeDtypeStruct((s, D_INNER + X_WIDTH), _ACT), compiler_params=_cp(VMEM_BIG))(proj, proj, lng, lnb, ws, bs3)


def _gmlp_bwd(proj, dcat, lng, lnb, ws, bs3, name):
    s = proj.shape[0]

    def body(pu_ref, pv_ref, dm_ref, lng_ref, lnb_ref, ws_ref, bs_ref, dp_ref, dws_ref, dbs_ref, dlng_ref, dlnb_ref, dvn_ref):
        @pl.when(pl.program_id(0) == 0)
        def _():
            dws_ref[...] = jnp.zeros_like(dws_ref)
            dbs_ref[...] = jnp.zeros_like(dbs_ref)
            dlng_ref[...] = jnp.zeros_like(dlng_ref)
            dlnb_ref[...] = jnp.zeros_like(dlnb_ref)

        pu, pv = pu_ref[...], pv_ref[...]
        lng = lng_ref[...]
        u, xhat, rstd, vn = _gmlp_parts(pu, pv, lng, lnb_ref[...])
        dm = dm_ref[...].astype(F32)
        causal = _iota((CHUNK, CHUNK), 0) >= _iota((CHUNK, CHUNK), 1)
        for g in range(A_GROUPS):
            sl = slice(g * A_GW, (g + 1) * A_GW)
            w = jnp.where(causal, ws_ref[g], 0.0)
            sv = _dot(w, vn[:, sl]) + bs_ref[g]
            dsv = dm[:, sl] * u[:, sl]
            dp_ref[:, sl] = (dm[:, sl] * sv * _gelu_grad(pu[:, sl])).astype(dp_ref.dtype)
            dvn_ref[:, sl] = _dot_tn(w, dsv)
            dws_ref[g] += jnp.where(causal, _dot_nt(dsv, vn[:, sl]), 0.0)
            dbs_ref[g] += jnp.sum(dsv, axis=-1, keepdims=True)
        dvn = dvn_ref[...]
        dlng_ref[...] += jnp.sum(dvn * xhat, axis=0, keepdims=True)
        dlnb_ref[...] += jnp.sum(dvn, axis=0, keepdims=True)
        dxh = dvn * lng
        dv = rstd * (dxh - jnp.mean(dxh, axis=-1, keepdims=True) - xhat * jnp.mean(dxh * xhat, axis=-1, keepdims=True))
        dp_ref[:, D_INNER:] = (dv * _gelu_grad(pv)).astype(dp_ref.dtype)

    full = lambda shape: pl.BlockSpec(shape, lambda c: (0,) * len(shape))
    return pl.pallas_call(
        body, name=name, grid=(s // CHUNK,),
        in_specs=[pl.BlockSpec((CHUNK, D_INNER), lambda c: (c, 0)), pl.BlockSpec((CHUNK, D_INNER), lambda c: (c, 1)),
                  pl.BlockSpec((CHUNK, D_INNER), lambda c: (c, 0)),
                  full((1, D_INNER)), full((1, D_INNER)), full((A_GROUPS, CHUNK, CHUNK)), full((A_GROUPS, CHUNK, 1))],
        out_specs=[pl.BlockSpec((CHUNK, 2 * D_INNER), lambda c: (c, 0)), full((A_GROUPS, CHUNK, CHUNK)),
                   full((A_GROUPS, CHUNK, 1)), full((1, D_INNER)), full((1, D_INNER))],
        out_shape=[jax.ShapeDtypeStruct((s, 2 * D_INNER + X_WIDTH), _ACT), jax.ShapeDtypeStruct((A_GROUPS, CHUNK, CHUNK), F32),
                   jax.ShapeDtypeStruct((A_GROUPS, CHUNK, 1), F32), jax.ShapeDtypeStruct((1, D_INNER), F32),
                   jax.ShapeDtypeStruct((1, D_INNER), F32)],
        scratch_shapes=[pltpu.VMEM((CHUNK, D_INNER), F32)],
        compiler_params=_cp(VMEM_BIG))(proj, proj, dcat, lng, lnb, ws, bs3)


_X_SCALE = 1.0 / math.sqrt(X_HD)


def _attn_fwd(proj, qblk, kv, cat, name, tm=256):
    s = proj.shape[0]
    tm = min(tm, s)

    def body(q_ref, kv_ref, cat_ref, o_ref):
        for h in range(X_HEADS):
            sl = slice(h * X_HD, (h + 1) * X_HD)
            k = kv_ref[:, sl]
            v = kv_ref[:, X_WIDTH + h * X_HD:X_WIDTH + (h + 1) * X_HD]
            sc = _dot_nt(q_ref[:, sl], k) * _X_SCALE
            e = jnp.exp(sc - jnp.max(sc, axis=-1, keepdims=True))
            p = e / jnp.sum(e, axis=-1, keepdims=True)
            o_ref[:, sl] = _dot(p, v).astype(o_ref.dtype)

    return pl.pallas_call(
        body, name=name, grid=(s // tm,),
        in_specs=[pl.BlockSpec((tm, X_WIDTH), lambda i: (i, qblk)), pl.BlockSpec((N_MEM, 2 * X_WIDTH), lambda i: (0, 0)),
                  pl.BlockSpec(memory_space=pl.ANY)],
        out_specs=pl.BlockSpec((tm, X_WIDTH), lambda i: (i, D_INNER // X_WIDTH)),
        out_shape=jax.ShapeDtypeStruct(cat.shape, cat.dtype), input_output_aliases={2: 0})(proj, kv, cat)


def _attn_bwd(proj, qblk, kv, dcat, dproj, name, tm=256):
    s = proj.shape[0]
    tm = min(tm, s)

    def body(q_ref, kv_ref, do_ref, dproj_ref, dq_ref, dkv_ref):
        @pl.when(pl.program_id(0) == 0)
        def _():
            dkv_ref[...] = jnp.zeros_like(dkv_ref)

        for h in range(X_HEADS):
            sl = slice(h * X_HD, (h + 1) * X_HD)
            slv = slice(X_WIDTH + h * X_HD, X_WIDTH + (h + 1) * X_HD)
            q = q_ref[:, sl]
            k = kv_ref[:, sl]
            v = kv_ref[:, slv]
            do = do_ref[:, sl].astype(F32)
            sc = _dot_nt(q, k) * _X_SCALE
            e = jnp.exp(sc - jnp.max(sc, axis=-1, keepdims=True))
            p = e / jnp.sum(e, axis=-1, keepdims=True)
            dp = _dot_nt(do, v)
            ds = p * (dp - jnp.sum(dp * p, axis=-1, keepdims=True)) * _X_SCALE
            dq_ref[:, sl] = _dot(ds, k).astype(dq_ref.dtype)
            dkv_ref[:, sl] += _dot_tn(ds, q)
            dkv_ref[:, slv] += _dot_tn(p, do)

    return pl.pallas_call(
        body, name=name, grid=(s // tm,),
        in_specs=[pl.BlockSpec((tm, X_WIDTH), lambda i: (i, qblk)), pl.BlockSpec((N_MEM, 2 * X_WIDTH), lambda i: (0, 0)),
                  pl.BlockSpec((tm, X_WIDTH), lambda i: (i, 2)), pl.BlockSpec(memory_space=pl.ANY)],
        out_specs=[pl.BlockSpec((tm, X_WIDTH), lambda i: (i, qblk)), pl.BlockSpec((N_MEM, 2 * X_WIDTH), lambda i: (0, 0))],
        out_shape=[jax.ShapeDtypeStruct(dproj.shape, dproj.dtype), jax.ShapeDtypeStruct((N_MEM, 2 * X_WIDTH), F32)],
        input_output_aliases={3: 0})(proj, kv, dcat, dproj)


CONV_TC = 256
_XBC_BLK0 = D_INNER // CONV_TC


def _shift_down(x, j):
    if j == 0:
        return x
    return jnp.where(_iota(x.shape, 0) >= j, pltpu.roll(x, j, 0), 0.0)


def _shift_up(x, j):
    if j == 0:
        return x
    n = x.shape[0]
    return jnp.where(_iota(x.shape, 0) < n - j, pltpu.roll(x, n - j, 0), 0.0)


def _conv_fwd(proj, w, b, name):
    s = proj.shape[0]

    def body(x_ref, w_ref, b_ref, o_ref):
        xv = x_ref[...]
        pre = b_ref[...] + w_ref[CONV_K - 1:CONV_K, :] * xv
        for kk in range(CONV_K - 1):
            pre = pre + w_ref[kk:kk + 1, :] * _shift_down(xv, CONV_K - 1 - kk)
        o_ref[...] = pre * _sigmoid(pre)

    return pl.pallas_call(
        body, name=name, grid=(CONV_DIM // CONV_TC,),
        in_specs=[pl.BlockSpec((s, CONV_TC), lambda j: (0, _XBC_BLK0 + j)), pl.BlockSpec((CONV_K, CONV_TC), lambda j: (0, j)),
                  pl.BlockSpec((1, CONV_TC), lambda j: (0, j))],
        out_specs=pl.BlockSpec((s, CONV_TC), lambda j: (0, j)),
        out_shape=jax.ShapeDtypeStruct((s, CONV_DIM), F32), compiler_params=_cp(VMEM_BIG))(proj, w, b)


def _conv_bwd(proj, w, b, dxbc, dproj, name):
    s = proj.shape[0]

    def body(x_ref, w_ref, b_ref, d_ref, dproj_ref, dx_ref, dw_ref, db_ref):
        xv = x_ref[...]
        pre = b_ref[...] + w_ref[CONV_K - 1:CONV_K, :] * xv
        for kk in range(CONV_K - 1):
            pre = pre + w_ref[kk:kk + 1, :] * _shift_down(xv, CONV_K - 1 - kk)
        sig = _sigmoid(pre)
        dpre = d_ref[...] * (sig * (1.0 + pre * (1.0 - sig)))
        dx = w_ref[CONV_K - 1:CONV_K, :] * dpre
        dw_ref[CONV_K - 1:CONV_K, :] = jnp.sum(dpre * xv, axis=0, keepdims=True)
        for kk in range(CONV_K - 1):
            j = CONV_K - 1 - kk
            dx = dx + w_ref[kk:kk + 1, :] * _shift_up(dpre, j)
            dw_ref[kk:kk + 1, :] = jnp.sum(dpre * _shift_down(xv, j), axis=0, keepdims=True)
        dx_ref[...] = dx.astype(dx_ref.dtype)
        db_ref[...] = jnp.sum(dpre, axis=0, keepdims=True)

    return pl.pallas_call(
        body, name=name, grid=(CONV_DIM // CONV_TC,),
        in_specs=[pl.BlockSpec((s, CONV_TC), lambda j: (0, _XBC_BLK0 + j)), pl.BlockSpec((CONV_K, CONV_TC), lambda j: (0, j)),
                  pl.BlockSpec((1, CONV_TC), lambda j: (0, j)), pl.BlockSpec((s, CONV_TC), lambda j: (0, j)),
                  pl.BlockSpec(memory_space=pl.ANY)],
        out_specs=[pl.BlockSpec((s, CONV_TC), lambda j: (0, _XBC_BLK0 + j)), pl.BlockSpec((CONV_K, CONV_TC), lambda j: (0, j)),
                   pl.BlockSpec((1, CONV_TC), lambda j: (0, j))],
        out_shape=[jax.ShapeDtypeStruct(dproj.shape, dproj.dtype), jax.ShapeDtypeStruct((CONV_K, CONV_DIM), F32),
                   jax.ShapeDtypeStruct((1, CONV_DIM), F32)], input_output_aliases={4: 0},
        compiler_params=_cp(VMEM_BIG))(proj, w, b, dxbc, dproj)


def _ssd_common(dtc_ref, br_ref, ar_ref, csb_ref, cst_ref, csf_ref):
    a_row = -jnp.exp(ar_ref[...])
    dt_c = _softplus(dtc_ref[...] + br_ref[...])
    tril = _iota((CHUNK, CHUNK), 0) >= _iota((CHUNK, CHUNK), 1)
    cs = _sel_dot(tril, dt_c * a_row)
    cst_ref[...] = cs.T
    e64 = (jnp.right_shift(_iota((HPAD, D_INNER), 1), 6) == _iota((HPAD, D_INNER), 0)).astype(jnp.bfloat16)
    e128 = jnp.right_shift(_iota((HPAD, SSM_HEADS * CHUNK), 1), 7) == _iota((HPAD, SSM_HEADS * CHUNK), 0)
    csb_ref[...] = _dot_sel(cs, e128)
    dt_full = _dot_sel(dt_c, e64)
    csf_ref[...] = _dot_sel(cs, e64)
    cs_full = csf_ref[...]
    cs_last = csf_ref[CHUNK - 1:CHUNK, :]
    e_full = jnp.exp(cs_full)
    f_full = jnp.exp(cs_last - cs_full)
    gamma = jnp.exp(cs_last)
    return a_row, dt_c, cs, dt_full, e_full, f_full, gamma, e64


def _ssd_lambda(csb_ref, cst_ref, h, causal):
    diff = csb_ref[:, h * CHUNK:(h + 1) * CHUNK] - cst_ref[h:h + 1, :]
    return jnp.exp(jnp.where(causal, diff, -1e30))


_SSD_VEC_SPECS = lambda: [pl.BlockSpec((1, HPAD), lambda c: (0, 0)), pl.BlockSpec((1, HPAD), lambda c: (0, 0)),
                          pl.BlockSpec((1, D_INNER), lambda c: (0, 0))]


def _ssd_fwd(xbc, dtc, bias_row, alog_row, dfull, name):
    s = xbc.shape[0]
    nc = s // CHUNK

    def body(xbc_ref, dtc_ref, br_ref, ar_ref, df_ref, y_ref, st_ref, ht_ref, csb_ref, cst_ref, csf_ref):
        @pl.when(pl.program_id(0) == 0)
        def _():
            ht_ref[...] = jnp.zeros_like(ht_ref)

        _, _, _, dt_full, e_full, f_full, gamma, _ = _ssd_common(dtc_ref, br_ref, ar_ref, csb_ref, cst_ref, csf_ref)
        x = xbc_ref[:, :D_INNER]
        xdt = x * dt_full
        st_ref[...] = ht_ref[...]
        causal = _iota((CHUNK, CHUNK), 0) >= _iota((CHUNK, CHUNK), 1)
        lo = _iota((CHUNK, CHUNK), 1) < SSM_P
        for g in range(SSM_GROUPS):
            gs = slice(g * SSM_GW, (g + 1) * SSM_GW)
            bg = xbc_ref[:, D_INNER + g * SSM_N:D_INNER + (g + 1) * SSM_N]
            cg = xbc_ref[:, D_INNER + SSM_GROUPS * SSM_N + g * SSM_N:D_INNER + SSM_GROUPS * SSM_N + (g + 1) * SSM_N]
            ht = ht_ref[:, gs]
            cb = _dot_nt(cg, bg)
            yoff = e_full[:, gs] * _dot(cg, ht)
            for jp in range(SSM_GW // CHUNK):
                j = g * (SSM_GW // CHUNK) + jp
                ps = slice(j * CHUNK, (j + 1) * CHUNK)
                x2 = xdt[:, ps]
                y0 = _dot(cb * _ssd_lambda(csb_ref, cst_ref, 2 * j, causal), x2)
                y1 = _dot(cb * _ssd_lambda(csb_ref, cst_ref, 2 * j + 1, causal), x2)
                y_ref[:, ps] = (jnp.where(lo, y0, y1) + yoff[:, jp * CHUNK:(jp + 1) * CHUNK]
                                + x[:, ps] * df_ref[:, ps])
            ht_ref[:, gs] = gamma[:, gs] * ht + _dot_tn(bg, xdt[:, gs] * f_full[:, gs])

    return pl.pallas_call(
        body, name=name, grid=(nc,),
        in_specs=[pl.BlockSpec((CHUNK, CONV_DIM), lambda c: (c, 0)), pl.BlockSpec((CHUNK, HPAD), lambda c: (c, 0))]
                 + _SSD_VEC_SPECS(),
        out_specs=[pl.BlockSpec((CHUNK, D_INNER), lambda c: (c, 0)), pl.BlockSpec((None, SSM_N, D_INNER), lambda c: (c, 0, 0))],
        out_shape=[jax.ShapeDtypeStruct((s, D_INNER), F32), jax.ShapeDtypeStruct((nc, SSM_N, D_INNER), F32)],
        scratch_shapes=[pltpu.VMEM((SSM_N, D_INNER), F32), pltpu.VMEM((CHUNK, SSM_HEADS * CHUNK), F32),
                        pltpu.VMEM((HPAD, CHUNK), F32), pltpu.VMEM((CHUNK, D_INNER), F32)],
        compiler_params=_cp(VMEM_BIG))(xbc, dtc, bias_row, alog_row, dfull)


def _ssd_bwd(xbc, dtc, bias_row, alog_row, dfull, dy, states, name):
    s = xbc.shape[0]
    nc = s // CHUNK
    rev = lambda c: nc - 1 - c

    def body(xbc_ref, dtc_ref, br_ref, ar_ref, df_ref, dy_ref, st_ref,
             dxbc_ref, ddt_ref, dalog_ref, dd_ref, dbias_ref,
             dht_ref, csb_ref, cst_ref, csf_ref, ddf_ref, dxs_ref, dcsf_ref, dcsl_ref):
        step = pl.program_id(0)

        @pl.when(step == 0)
        def _():
            dht_ref[...] = jnp.zeros_like(dht_ref)
            ddf_ref[...] = jnp.zeros_like(ddf_ref)
            dalog_ref[...] = jnp.zeros_like(dalog_ref)
            dbias_ref[...] = jnp.zeros_like(dbias_ref)
            dd_ref[...] = jnp.zeros_like(dd_ref)

        a_row, dt_c, _, dt_full, e_full, f_full, gamma, e64 = _ssd_common(dtc_ref, br_ref, ar_ref, csb_ref, cst_ref, csf_ref)
        x = xbc_ref[:, :D_INNER]
        xdt = x * dt_full
        dy_all = dy_ref[...]
        ddf_ref[...] += jnp.broadcast_to(jnp.sum(dy_all * x, axis=0, keepdims=True), ddf_ref.shape)
        causal = _iota((CHUNK, CHUNK), 0) >= _iota((CHUNK, CHUNK), 1)
        lo = _iota((CHUNK, CHUNK), 1) < SSM_P
        head_lane = _iota((CHUNK, HPAD), 1)
        head_row = _iota((HPAD, CHUNK), 0)
        dcs_heads = jnp.zeros((CHUNK, HPAD), F32)
        dcs_cols = jnp.zeros((HPAD, CHUNK), F32)
        for g in range(SSM_GROUPS):
            gs = slice(g * SSM_GW, (g + 1) * SSM_GW)
            b0 = D_INNER + g * SSM_N
            c0 = D_INNER + SSM_GROUPS * SSM_N + g * SSM_N
            bg = xbc_ref[:, b0:b0 + SSM_N]
            cg = xbc_ref[:, c0:c0 + SSM_N]
            ht = st_ref[:, gs]
            dht = dht_ref[:, gs]
            dyg = dy_all[:, gs]
            eg, fg, gg = e_full[:, gs], f_full[:, gs], gamma[:, gs]
            z = _dot(cg, ht)
            dz = dyg * eg
            dcg = _dot_nt(dz, ht)
            dht_new = _dot_tn(cg, dz) + gg * dht
            xf = xdt[:, gs] * fg
            dxf = _dot(bg, dht)
            dbg = _dot_nt(xf, dht)
            dff = dxf * xf
            dcsf_ref[:, gs] = dyg * eg * z - dff
            dcsl_ref[:, gs] = jnp.broadcast_to(
                jnp.sum(dff, axis=0, keepdims=True) + jnp.sum(dht * ht, axis=0, keepdims=True) * gg, (8, SSM_GW))
            cb = _dot_nt(cg, bg)
            dcb = jnp.zeros((CHUNK, CHUNK), F32)
            for jp in range(SSM_GW // CHUNK):
                j = g * (SSM_GW // CHUNK) + jp
                ps = slice(j * CHUNK, (j + 1) * CHUNK)
                x2 = xdt[:, ps]
                dy2 = dy_all[:, ps]
                dxh = []
                for hh in range(2):
                    h = 2 * j + hh
                    lam = _ssd_lambda(csb_ref, cst_ref, h, causal)
                    mh = cb * lam
                    dyh = jnp.where(lo, dy2, 0.0) if hh == 0 else jnp.where(lo, 0.0, dy2)
                    dm = _dot_nt(dyh, x2)
                    dcb = dcb + dm * lam
                    gm = dm * mh
                    dcs_heads = dcs_heads + jnp.where(head_lane == h, jnp.sum(gm, axis=1, keepdims=True), 0.0)
                    dcs_cols = dcs_cols + jnp.where(head_row == h, jnp.sum(gm, axis=0, keepdims=True), 0.0)
                    dxh.append(_dot_tn(mh, dy2))
                dxs_ref[:, ps] = jnp.where(lo, dxh[0], dxh[1]) + dxf[:, jp * CHUNK:(jp + 1) * CHUNK] * fg[:, jp * CHUNK:(jp + 1) * CHUNK]
            dxbc_ref[:, b0:b0 + SSM_N] = (dbg + _dot_tn(dcb, cg)).astype(dxbc_ref.dtype)
            dxbc_ref[:, c0:c0 + SSM_N] = (dcg + _dot(dcb, bg)).astype(dxbc_ref.dtype)
            dht_ref[:, gs] = dht_new
        dxs = dxs_ref[...]
        dcs_heads = dcs_heads - dcs_cols.T + _dot_sel(dcsf_ref[...], e64, ((1,), (1,)))
        dcs_last = _dot_sel(dcsl_ref[...], e64, ((1,), (1,)))
        dcs_heads = dcs_heads + jnp.where(_iota((CHUNK, HPAD), 0) == CHUNK - 1, dcs_last[0:1, :], 0.0)
        triu = _iota((CHUNK, CHUNK), 0) <= _iota((CHUNK, CHUNK), 1)
        dda = _sel_dot(triu, dcs_heads)
        ddt = dda * a_row + _dot_sel(dxs * x, e64, ((1,), (1,)))
        dxbc_ref[:, :D_INNER] = (dxs * dt_full + dy_all * df_ref[...]).astype(dxbc_ref.dtype)
        dalog_ref[...] += jnp.sum(dda * dt_c, axis=0, keepdims=True) * a_row
        ddt_raw = ddt * _sigmoid(dtc_ref[...] + br_ref[...])
        ddt_ref[...] = ddt_raw.astype(ddt_ref.dtype)
        dbias_ref[...] += jnp.sum(ddt_raw, axis=0, keepdims=True)

        @pl.when(step == nc - 1)
        def _():
            dd_ref[...] = _dot_sel(ddf_ref[...], e64, ((1,), (1,)))[0:1, :]

    vec = pl.BlockSpec((1, HPAD), lambda c: (0, 0))
    return pl.pallas_call(
        body, name=name, grid=(nc,),
        in_specs=[pl.BlockSpec((CHUNK, CONV_DIM), lambda c: (rev(c), 0)), pl.BlockSpec((CHUNK, HPAD), lambda c: (rev(c), 0))]
                 + _SSD_VEC_SPECS()
                 + [pl.BlockSpec((CHUNK, D_INNER), lambda c: (rev(c), 0)),
                    pl.BlockSpec((None, SSM_N, D_INNER), lambda c: (rev(c), 0, 0))],
        out_specs=[pl.BlockSpec((CHUNK, CONV_DIM), lambda c: (rev(c), 0)), pl.BlockSpec((CHUNK, HPAD), lambda c: (rev(c), 0)),
                   vec, vec, vec],
        out_shape=[jax.ShapeDtypeStruct((s, CONV_DIM), F32), jax.ShapeDtypeStruct((s, HPAD), _ACT),
                   jax.ShapeDtypeStruct((1, HPAD), F32), jax.ShapeDtypeStruct((1, HPAD), F32),
                   jax.ShapeDtypeStruct((1, HPAD), F32)],
        scratch_shapes=[pltpu.VMEM((SSM_N, D_INNER), F32), pltpu.VMEM((CHUNK, SSM_HEADS * CHUNK), F32),
                        pltpu.VMEM((HPAD, CHUNK), F32), pltpu.VMEM((CHUNK, D_INNER), F32),
                        pltpu.VMEM((8, D_INNER), F32), pltpu.VMEM((CHUNK, D_INNER), F32),
                        pltpu.VMEM((CHUNK, D_INNER), F32), pltpu.VMEM((8, D_INNER), F32)],
        compiler_params=_cp(VMEM_BIG))(xbc, dtc, bias_row, alog_row, dfull, dy, states)


def _gate_fwd(y, proj, gn, name, tm=256):
    s = y.shape[0]
    tm = min(tm, s)

    def body(y_ref, z_ref, gn_ref, o_ref):
        for g in range(SSM_GROUPS):
            gs = slice(g * SSM_GW, (g + 1) * SSM_GW)
            z = z_ref[:, gs]
            t = y_ref[:, gs] * (z * _sigmoid(z))
            r = lax.rsqrt(jnp.mean(t * t, axis=-1, keepdims=True) + EPS)
            o_ref[:, gs] = (t * r * gn_ref[:, gs]).astype(o_ref.dtype)

    row = pl.BlockSpec((tm, D_INNER), lambda i: (i, 0))
    return pl.pallas_call(
        body, name=name, grid=(s // tm,), in_specs=[row, row, pl.BlockSpec((1, D_INNER), lambda i: (0, 0))],
        out_specs=row, out_shape=jax.ShapeDtypeStruct((s, D_INNER + X_WIDTH), _ACT))(y, proj, gn)


def _gate_bwd(y, proj, gn, dcat, name, tm=256):
    s = y.shape[0]
    tm = min(tm, s)

    def body(y_ref, z_ref, gn_ref, dm_ref, dy_ref, dz_ref, dgn_ref):
        @pl.when(pl.program_id(0) == 0)
        def _():
            dgn_ref[...] = jnp.zeros_like(dgn_ref)

        for g in range(SSM_GROUPS):
            gs = slice(g * SSM_GW, (g + 1) * SSM_GW)
            z = z_ref[:, gs]
            yv = y_ref[:, gs]
            sig = _sigmoid(z)
            sz = z * sig
            t = yv * sz
            r = lax.rsqrt(jnp.mean(t * t, axis=-1, keepdims=True) + EPS)
            th = t * r
            dm = dm_ref[:, gs].astype(F32)
            dmg = dm * gn_ref[:, gs]
            dt_ = r * (dmg - th * jnp.mean(dmg * th, axis=-1, keepdims=True))
            dgn_ref[:, gs] += jnp.sum(dm * th, axis=0, keepdims=True)
            dy_ref[:, gs] = dt_ * sz
            dz_ref[:, gs] = (dt_ * yv * (sig * (1.0 + z * (1.0 - sig)))).astype(dz_ref.dtype)

    row = pl.BlockSpec((tm, D_INNER), lambda i: (i, 0))
    vec = pl.BlockSpec((1, D_INNER), lambda i: (0, 0))
    return pl.pallas_call(
        body, name=name, grid=(s // tm,), in_specs=[row, row, vec, row], out_specs=[row, row, vec],
        out_shape=[jax.ShapeDtypeStruct((s, D_INNER), F32), jax.ShapeDtypeStruct((s, 6 * D_MODEL), _ACT),
                   jax.ShapeDtypeStruct((1, D_INNER), F32)])(y, proj, gn, dcat)


def _block_of(kind, width):
    if kind == "col":
        return lambda ref, j: ref.at[:, :, pl.ds(pl.multiple_of(j * width, 128), width)]
    if kind == "row":
        return lambda ref, j: ref.at[:, pl.ds(pl.multiple_of(j * width, 8), width), :]
    return lambda ref, j: ref.at[j]


def _coords():
    return lax.axis_index("x"), lax.axis_index("y"), lax.axis_index("c")


def _rel_chip(x, y, k):
    return (1 - x if k & 1 else x), (1 - y if k & 2 else y)


_HBM = lambda: pl.BlockSpec(memory_space=pltpu.HBM)


def _all_gather(shards, layouts, name, after=()):
    n, n_after = len(shards), len(after)
    blocks = [_block_of(kind, width) for kind, width, _ in layouts]

    def body(*refs):
        _all_gather_body(refs[:n], refs[n + n_after:2 * n + n_after], *refs[2 * n + n_after:], blocks)

    return pl.pallas_call(
        body, name=name, in_specs=[_HBM()] * n + [pl.BlockSpec(memory_space=pl.ANY)] * n_after, out_specs=[_HBM()] * n,
        out_shape=[jax.ShapeDtypeStruct(shape, sh.dtype) for sh, (_, _, shape) in zip(shards, layouts)],
        scratch_shapes=[pltpu.SemaphoreType.DMA((n, 7)), pltpu.SemaphoreType.DMA((n, 7)), pltpu.SemaphoreType.DMA((n,))],
    )(*shards, *after)


def _all_gather_body(ins, outs, send_sems, recv_sems, local_sems, blocks):
    n = len(ins)
    x, y, c = _coords()
    sibling = (x, y, 1 - c)

    def copy(t, k, chip, core, to, src=None):
        dst = blocks[t](outs[t], 4 * chip[0] + 2 * chip[1] + core)
        return pltpu.make_async_remote_copy(
            src_ref=dst if src is None else src, dst_ref=dst, send_sem=send_sems.at[t, k],
            recv_sem=recv_sems.at[t, k], device_id=to, device_id_type=MESH)

    started = []
    for t in range(n):
        mine = pltpu.make_async_copy(ins[t], blocks[t](outs[t], 4 * x + 2 * y + c), local_sems.at[t])
        mine.start()
        started.append(mine)
    sends = []
    for t in range(n):
        for k in range(4):
            px, py = _rel_chip(x, y, k)
            cp = copy(t, k, (x, y), c, (px, py, 1 - c if k == 0 else c), src=ins[t])
            cp.start()
            sends.append(cp)
    for t in range(n):
        for k in range(1, 4):
            chip = _rel_chip(x, y, k)
            copy(t, k, chip, c, sibling).wait_recv()
            fwd = copy(t, 3 + k, chip, c, sibling)
            fwd.start()
            sends.append(fwd)
    for t in range(n):
        copy(t, 0, (x, y), 1 - c, sibling).wait_recv()
        for k in range(1, 4):
            copy(t, 3 + k, _rel_chip(x, y, k), 1 - c, sibling).wait_recv()
    for cp in sends:
        cp.wait_send()
    for mine in started:
        mine.wait()


def _handshake(peers):
    barrier = pltpu.get_barrier_semaphore()
    for peer in peers:
        pl.semaphore_signal(barrier, inc=1, device_id=peer, device_id_type=MESH)
    pl.semaphore_wait(barrier, len(peers))


def _two_level_peers():
    x, y, c = _coords()
    return [(x, y, 1 - c)] + [(*_rel_chip(x, y, k), c) for k in range(1, 4)]


SEQ_ID_GATHER, SEQ_ID_SIBLING, SEQ_ID_CHIPS = 1, 2, 3


def _sequencer_call(body, peers, operands, out_types, sems, name, collective_id, after=()):
    n_in, n_out, n_after = len(operands), len(out_types), len(after)

    def launch(*refs):
        _handshake(peers())
        body(refs[:n_in], refs[n_in + n_after:n_in + n_after + n_out], *refs[n_in + n_after + n_out:])

    return pl.kernel(
        launch, name=name, out_type=out_types, mesh=plsc.ScalarSubcoreMesh(axis_name="seq", num_cores=1),
        scratch_types=sems, compiler_params=pltpu.CompilerParams(collective_id=collective_id))(*operands, *after)


def _all_gather_seq(shards, layouts, name, after=()):
    n = len(shards)
    blocks = [_block_of(kind, width) for kind, width, _ in layouts]
    return _sequencer_call(
        lambda ins, outs, *sems: _all_gather_body(ins, outs, *sems, blocks), _two_level_peers, shards,
        [jax.ShapeDtypeStruct(shape, sh.dtype) for sh, (_, _, shape) in zip(shards, layouts)],
        [pltpu.SemaphoreType.DMA((n, 7)), pltpu.SemaphoreType.DMA((n, 7)), pltpu.SemaphoreType.DMA((n,))],
        name, SEQ_ID_GATHER, after)


def _tie(small, after, name):
    def body(*refs):
        refs[-1][...] = refs[0][...]

    vmem = pl.BlockSpec(memory_space=pltpu.VMEM)
    return pl.pallas_call(
        body, name=name, in_specs=[vmem] + [pl.BlockSpec(memory_space=pl.ANY)] * len(after), out_specs=vmem,
        out_shape=jax.ShapeDtypeStruct(small.shape, small.dtype))(small, *after)


def _rs_to_sibling(grads, layouts, name):
    n = len(grads)
    blocks = [_block_of(kind, width) for kind, width, _ in layouts]

    def body(ins, outs, send_sems, recv_sems):
        x, y, c = _coords()
        sibling = (x, y, 1 - c)
        cps = []
        for t in range(n):
            for k in range(4):
                px, py = _rel_chip(x, y, k)
                cp = pltpu.make_async_remote_copy(
                    src_ref=blocks[t](ins[t], 4 * px + 2 * py + (1 - c)), dst_ref=outs[t].at[k],
                    send_sem=send_sems.at[t, k], recv_sem=recv_sems.at[t, k], device_id=sibling, device_id_type=MESH)
                cp.start()
                cps.append(cp)
        for cp in cps:
            cp.wait_recv()
        for cp in cps:
            cp.wait_send()

    def sibling_only():
        x, y, c = _coords()
        return [(x, y, 1 - c)]

    return _sequencer_call(
        body, sibling_only, grads,
        [jax.ShapeDtypeStruct((4,) + shape, g.dtype) for g, (_, _, shape) in zip(grads, layouts)],
        [pltpu.SemaphoreType.DMA((n, 4)), pltpu.SemaphoreType.DMA((n, 4))], name, SEQ_ID_SIBLING)


def _rs_chip_sum(grad, recv, layout, xyc, name):
    kind, width, shape = layout
    r, ccols = shape

    def src_index(k, xyc_ref):
        px = jnp.where(k % 2 == 1, 1 - xyc_ref[0], xyc_ref[0])
        py = jnp.where(k // 2 == 1, 1 - xyc_ref[1], xyc_ref[1])
        return 4 * px + 2 * py + xyc_ref[2]

    if kind == "col":
        g_spec = pl.BlockSpec((r, ccols), lambda k, s_: (0, src_index(k, s_)))
    elif kind == "row":
        g_spec = pl.BlockSpec((r, ccols), lambda k, s_: (src_index(k, s_), 0))
    else:
        g_spec = pl.BlockSpec((None, r, ccols), lambda k, s_: (src_index(k, s_), 0, 0))

    def body(xyc_ref, g_ref, r_ref, o_ref):
        o_ref[...] = (g_ref[...].astype(F32) + r_ref[...].astype(F32)).astype(o_ref.dtype)

    slot = pl.BlockSpec((None, r, ccols), lambda k, s_: (k, 0, 0))
    return pl.pallas_call(
        body, name=name,
        grid_spec=pltpu.PrefetchScalarGridSpec(num_scalar_prefetch=1, grid=(4,), in_specs=[g_spec, slot], out_specs=slot),
        out_shape=jax.ShapeDtypeStruct((4, r, ccols), grad.dtype), compiler_params=_cp(VMEM_BIG))(xyc, grad, recv)


def _rs_across_chips(parts, name):
    n = len(parts)

    def body(ins, outs, send_sems, recv_sems):
        x, y, c = _coords()
        cps = []
        for t in range(n):
            for k in range(1, 4):
                px, py = _rel_chip(x, y, k)
                cp = pltpu.make_async_remote_copy(
                    src_ref=ins[t].at[k], dst_ref=outs[t].at[k - 1], send_sem=send_sems.at[t, k - 1],
                    recv_sem=recv_sems.at[t, k - 1], device_id=(px, py, c), device_id_type=MESH)
                cp.start()
                cps.append(cp)
        for cp in cps:
            cp.wait_recv()
        for cp in cps:
            cp.wait_send()

    def other_chips():
        x, y, c = _coords()
        return [(*_rel_chip(x, y, k), c) for k in range(1, 4)]

    return _sequencer_call(
        body, other_chips, parts, [jax.ShapeDtypeStruct((3,) + p.shape[1:], p.dtype) for p in parts],
        [pltpu.SemaphoreType.DMA((n, 3)), pltpu.SemaphoreType.DMA((n, 3))], name, SEQ_ID_CHIPS)


def _adamw_math(w, g, m, v):
    m = ADAM_B1 * m + (1.0 - ADAM_B1) * g
    v = ADAM_B2 * v + (1.0 - ADAM_B2) * jnp.square(g)
    m_hat = m / (1.0 - ADAM_B1 ** ADAM_STEP)
    v_hat = v / (1.0 - ADAM_B2 ** ADAM_STEP)
    delta = -ADAM_LR * (m_hat / (jnp.sqrt(v_hat) + ADAM_EPS) + ADAM_WD * w)
    return delta, m, v


def _row_tile(rows, cap):
    best = None
    for cand in range(8, min(rows, cap) + 1, 8):
        if rows % cand == 0:
            best = cand
    assert best is not None, rows
    return best


def _adamw(w, m, v, parts, name, layer=None, prev=None, tr=256):
    r, ccols = w.shape[-2:]
    npart = len(parts)
    if r % 8 == 0:
        tr, tc = _row_tile(r, tr), ccols
        steps, at = r // tr, (lambda i: (i, 0))
    else:
        tr, tc = r, 256
        assert ccols % tc == 0
        steps, at = ccols // tc, (lambda i: (0, i))

    def spec(lead):
        if lead is None:
            return pl.BlockSpec((tr, tc), at)
        return pl.BlockSpec((None, tr, tc), lambda i: (lead,) + at(i))

    wspec = lambda: spec(layer)
    pspec = spec

    def body(*refs):
        w_ref, m_ref, v_ref = refs[:3]
        p_refs = refs[3:3 + npart]
        outs = refs[len(refs) - 4:]
        g = p_refs[0][...].astype(F32)
        for p_ref in p_refs[1:]:
            g = g + p_ref[...].astype(F32)
        delta, mn, vn = _adamw_math(w_ref[...], g, m_ref[...], v_ref[...])
        outs[0][...] = g
        outs[1][...] = delta
        outs[2][...] = mn
        outs[3][...] = vn

    operands = [w, m, v] + [p for p, _ in parts]
    in_specs = [wspec(), wspec(), wspec()] + [pspec(lead) for _, lead in parts]
    aliases = {}
    if prev is not None:
        for i, p in enumerate(prev):
            aliases[len(operands)] = i
            operands.append(p)
            in_specs.append(pl.BlockSpec(memory_space=pl.ANY))
    return pl.pallas_call(
        body, name=name, grid=(steps,), in_specs=in_specs, out_specs=[wspec()] * 4,
        out_shape=[jax.ShapeDtypeStruct(w.shape, F32)] * 4, input_output_aliases=aliases)(*operands)


def _sum8(buf, name):
    _, r, ccols = buf.shape

    def body(b_ref, o_ref):
        acc = b_ref[0]
        for j in range(1, N_DEV):
            acc = acc + b_ref[j]
        o_ref[...] = acc

    tr = _row_tile(r, 256)
    return pl.pallas_call(
        body, name=name, grid=(r // tr,), in_specs=[pl.BlockSpec((N_DEV, tr, ccols), lambda i: (0, i, 0))],
        out_specs=pl.BlockSpec((tr, ccols), lambda i: (i, 0)), out_shape=jax.ShapeDtypeStruct((r, ccols), F32))(buf)


def _pack(arrays):
    pieces, layout, off = [], [], 0
    for a in arrays:
        n = a.size
        padded = -(-n // 1024) * 1024
        flat = a.reshape(-1).astype(F32)
        if padded != n:
            flat = jnp.pad(flat, (0, padded - n))
        pieces.append(flat.reshape(padded // 128, 128))
        layout.append((off, n, a.shape))
        off += padded // 128
    return jnp.concatenate(pieces, axis=0), layout


def _unpack(packed, layout):
    out = []
    for off, n, shape in layout:
        rows = -(-n // 1024) * 8
        out.append(packed[off:off + rows].reshape(-1)[:n].reshape(shape))
    return out


def kernel(x, mem, norm_mix, norm_ffn, mem_norm, w_kv, w_out, w_ffn1, w_ffn2, a_in, a_ln_g, a_ln_b, a_ws, a_bs, b_in, b_conv_w, b_conv_b, b_dt_bias, b_a_log, b_d, b_gnorm, final_norm, loss_target, m_norm_mix, m_norm_ffn, m_mem_norm, m_w_kv, m_w_out, m_w_ffn1, m_w_ffn2, m_a_in, m_a_ln_g, m_a_ln_b, m_a_ws, m_a_bs, m_b_in, m_b_conv_w, m_b_conv_b, m_b_dt_bias, m_b_a_log, m_b_d, m_b_gnorm, m_final_norm, v_norm_mix, v_norm_ffn, v_mem_norm, v_w_kv, v_w_out, v_w_ffn1, v_w_ffn2, v_a_in, v_a_ln_g, v_a_ln_b, v_a_ws, v_a_bs, v_b_in, v_b_conv_w, v_b_conv_b, v_b_dt_bias, v_b_a_log, v_b_d, v_b_gnorm, v_final_norm):
    s = x.shape[1]
    xs = x.reshape(s, D_MODEL)
    mems = mem.reshape(N_MEM, D_MODEL)
    target = loss_target.reshape(s, D_MODEL)
    ax, ay, ac = lax.axis_index("x"), lax.axis_index("y"), lax.axis_index("c")
    me = 4 * ax + 2 * ay + ac
    xyc = jnp.stack([ax, ay, ac]).astype(jnp.int32)

    b_cols = b_in.shape[2]
    act = lambda a: a.astype(_ACT)
    lay_f1, lay_f2 = ("col", 512, (1, D_MODEL, D_FF)), ("row", 512, (1, D_FF, D_MODEL))
    lay_out, lay_kv = ("row", 384, (1, 3 * D_MODEL, D_MODEL)), ("col", 256, (1, D_MODEL, 2 * X_WIDTH))
    small_w_pack = _pack([b_conv_w[0], b_conv_b[0], b_gnorm[0]])[0]
    WA, wkv0 = _all_gather_seq([act(a_in), act(w_kv[0:1])], [("col", 640, (1, D_MODEL, 5 * D_MODEL)), lay_kv], "ag_proj_a")
    (wo0,) = _all_gather_seq([act(w_out[0:1])], [lay_out], "ag_out0")
    w1_0, w2_0 = _all_gather_seq([act(w_ffn1[0:1]), act(w_ffn2[0:1])], [lay_f1, lay_f2], "ag_ffn0")
    a0 = _rms_fwd(xs, norm_mix[0].reshape(1, -1), "mix_norm0")
    tr_b = lambda a: jnp.swapaxes(a, 1, 2)
    wbt_blk, small_w = _all_gather_seq(
        [act(tr_b(b_in)[0]), small_w_pack],
        [("blk", 0, (N_DEV, b_cols, D_MODEL)), ("blk", 0, (N_DEV, 32, 128))], "ag_proj_b", after=[a0])
    wo1, wkv1 = _all_gather_seq([act(w_out[1:2]), act(w_kv[1:2])], [lay_out, lay_kv], "ag_out1", after=[a0])
    w1_1, w2_1 = _all_gather_seq([act(w_ffn1[1:2]), act(w_ffn2[1:2])], [lay_f1, lay_f2], "ag_ffn1", after=[a0])
    W1, W2, WO, WKV = [w1_0, w1_1], [w2_0, w2_1], [wo0, wo1], [wkv0, wkv1]
    dt0 = D_INNER + CONV_DIM

    row = lambda a: a.reshape(1, -1)
    nmix = [row(norm_mix[0]), row(norm_mix[1])]
    nffn = [row(norm_ffn[0]), row(norm_ffn[1])]
    nmem = [row(mem_norm[0]), row(mem_norm[1])]
    fin = row(final_norm)
    lng, lnb = a_ln_g.reshape(1, D_INNER), a_ln_b.reshape(1, D_INNER)
    ws = a_ws[0]
    bs3 = a_bs[0].reshape(A_GROUPS, CHUNK, 1)
    pad_h = lambda a: jnp.pad(a.reshape(-1), (0, HPAD - SSM_HEADS))
    bias_row = pad_h(b_dt_bias).reshape(1, HPAD)
    alog_row = pad_h(b_a_log).reshape(1, HPAD)
    dfull = jnp.repeat(b_d.reshape(-1), SSM_P).reshape(1, D_INNER)

    kvs, mns = [None, None], [None, None]

    def mem_kv(i, after=None):
        gain = nmem[i] if after is None else _tie(nmem[i], after, f"tie_mem{i}")
        mns[i] = _rms_fwd(mems, gain, f"mem_norm{i}")
        kvs[i] = _mm(mns[i], WKV[i], m=N_MEM, n=2 * X_WIDTH, k=D_MODEL, b_at=(0, 0, 0), out_dtype=_ACT, name=f"kv{i}")

    def ffn_fwd(h, i):
        f = _rms_fwd(h, nffn[i], f"ffn_norm{i}")
        p = _mm(f, W1[i], m=s, n=D_FF, k=D_MODEL, b_at=(0, 0, 0), out_dtype=_ACT, name=f"ffn_up{i}")
        hn = _mm(p, W2[i], m=s, n=D_MODEL, k=D_FF, b_at=(0, 0, 0), a_pro="relu2", add=h, name=f"ffn_down{i}")
        return f, p, hn

    def out_proj(h, cat, i):
        return _mm(cat, WO[i], m=s, n=D_MODEL, k=3 * D_MODEL, b_at=(0, 0, 0), add=h, name=f"out_proj{i}")

    proj_a = _mm(a0, WA, m=s, n=5 * D_MODEL, k=D_MODEL, b_at=(0, 0, 0), name="proj_a")
    mem_kv(0)
    cat_a = _gmlp_fwd(proj_a, lng, lnb, ws, bs3, "gmlp_fwd")
    cat_a = _attn_fwd(proj_a, 4, kvs[0], cat_a, "attn_fwd0")
    h1 = out_proj(xs, cat_a, 0)
    f0, p0, h2 = ffn_fwd(h1, 0)

    wbt_blk, small_w, _ = lax.optimization_barrier((wbt_blk, small_w, f0))
    wbt_full = wbt_blk.reshape(N_DEV * b_cols, D_MODEL)
    WBT = jnp.concatenate([wbt_full[:dt0], wbt_full[dt0 + SSM_HEADS:]], axis=0)
    WBDT = jnp.pad(wbt_full[dt0:dt0 + SSM_HEADS], ((0, HPAD - SSM_HEADS), (0, 0)))
    cw_sh, cb_sh, gn_sh = 4 * 384, 384, 256
    sw = small_w.reshape(N_DEV, 32 * 128)
    conv_w = jnp.transpose(sw[:, :cw_sh].reshape(N_DEV, CONV_K, 384), (1, 0, 2)).reshape(CONV_K, CONV_DIM)
    conv_b = sw[:, 2048:2048 + cb_sh].reshape(1, CONV_DIM)
    gnorm = sw[:, 3072:3072 + gn_sh].reshape(1, D_INNER)

    a1 = _rms_fwd(h2, nmix[1], "mix_norm1")
    proj_b = _mm(a1, WBT, m=s, n=6 * D_MODEL, k=D_MODEL, tb=True, name="proj_b")
    dt_raw = _mm(a1, WBDT, m=s, n=HPAD, k=D_MODEL, tb=True, name="proj_dt")
    xbc = _conv_fwd(proj_b, conv_w, conv_b, "conv_fwd")
    y_ssd, states = _ssd_fwd(xbc, dt_raw, bias_row, alog_row, dfull, "ssd_fwd")
    cat_b = _gate_fwd(y_ssd, proj_b, gnorm, "gate_fwd")
    mem_kv(1, after=[cat_b])
    cat_b = _attn_fwd(proj_b, 5, kvs[1], cat_b, "attn_fwd1")
    h3 = out_proj(h2, cat_b, 1)
    f1, p1, h4 = ffn_fwd(h3, 1)

    loss_part, dh, dh_act, d_fin = _loss_head(h4, fin, target, "loss_head")

    g_f1, g_f2, g_out, g_kv = [None, None], [None, None], [None, None], [None, None]
    d_nffn, d_nmix, d_nmem = [None, None], [None, None], [None, None]

    def ffn_bwd(dh, dh_act, h_in, f, p, i, after=()):
        dp = _mm(dh_act, W2[i], m=s, n=D_FF, k=D_MODEL, tb=True, b_at=(0, 0, 0), epi_p=p, out_dtype=_ACT, name=f"ffn_down_dx{i}")
        g_f2[i] = _mm(p, dh_act, m=D_FF, n=D_MODEL, k=s, ta=True, a_pro="relu2", out_dtype=_ACT, name=f"ffn_down_dw{i}")
        g_f1[i] = _mm(f, dp, m=D_MODEL, n=D_FF, k=s, ta=True, out_dtype=_ACT, name=f"ffn_up_dw{i}")
        df = _mm(dp, W1[i], m=s, n=D_MODEL, k=D_FF, tb=True, b_at=(0, 0, 0), after=after, name=f"ffn_up_dx{i}")
        dh_in, dh_in_act, d_nffn[i] = _rms_bwd(h_in, nffn[i], df, dh, f"ffn_norm_bwd{i}")
        return dh_in, dh_in_act

    def out_bwd(dh_act, cat, i):
        dcat = _mm(dh_act, WO[i], m=s, n=3 * D_MODEL, k=D_MODEL, tb=True, b_at=(0, 0, 0), out_dtype=_ACT, name=f"out_dx{i}")
        g_out[i] = _mm(cat, dh_act, m=3 * D_MODEL, n=D_MODEL, k=s, ta=True, out_dtype=_ACT, name=f"out_dw{i}")
        return dcat

    def mem_bwd(dkv, i):
        g_kv[i] = _mm(mns[i], dkv, m=D_MODEL, n=2 * X_WIDTH, k=N_MEM, ta=True, out_dtype=_ACT, name=f"kv_dw{i}")
        dmn = _mm(dkv, WKV[i], m=N_MEM, n=D_MODEL, k=2 * X_WIDTH, tb=True, b_at=(0, 0, 0), name=f"kv_dx{i}")
        _, _, d_nmem[i] = _rms_bwd(mems, nmem[i], dmn, None, f"mem_norm_bwd{i}")

    lay_g = {"f1": ("col", 512, (D_MODEL, 512)), "f2": ("row", 512, (512, D_MODEL)), "out": ("row", 384, (384, D_MODEL)),
             "kv": ("col", 256, (D_MODEL, 256)), "a": ("col", 640, (D_MODEL, 640)), "b": ("blk", 0, (b_cols, D_MODEL))}
    reduced = {}

    def reduce_scatter(group, tag):
        grads3, lays3 = [], []
        for fam, _, g in group:
            kind, width, shape = lay_g[fam]
            grads3.append(g if kind == "blk" else g.reshape((1,) + g.shape))
            lays3.append((kind, width, shape if kind == "blk" else (1,) + shape))
        recv1 = _rs_to_sibling(grads3, lays3, f"rs_sibling_{tag}")
        parts = [_rs_chip_sum(g, recv1[t].reshape((4,) + lay_g[fam][2]), lay_g[fam], xyc, f"rs_chip_sum_{fam}{i}")
                 for t, (fam, i, g) in enumerate(group)]
        recv2 = _rs_across_chips(parts, f"rs_chips_{tag}")
        for (fam, i, _), p, r2 in zip(group, parts, recv2):
            reduced[fam, i] = (p, r2)
        return parts, recv2

    dh3, dh3_act = ffn_bwd(dh, dh_act, h3, f1, p1, 1)
    dcat_b = out_bwd(dh3_act, cat_b, 1)
    sums, got_ffn1 = reduce_scatter([("f1", 1, g_f1[1]), ("f2", 1, g_f2[1]), ("out", 1, g_out[1])], "ffn1")
    dy_ssd, dproj_b, d_gnorm = _gate_bwd(y_ssd, proj_b, gnorm, dcat_b, "gate_bwd")
    dproj_b, dkv_b = _attn_bwd(proj_b, 5, kvs[1], dcat_b, dproj_b, "attn_bwd1")
    mem_bwd(dkv_b, 1)
    dxbc, ddt_raw, d_alog, d_dskip, d_dtbias = _ssd_bwd(
        xbc, dt_raw, _tie(bias_row, sums, "tie_ffn1"), alog_row, dfull, dy_ssd, states, "ssd_bwd")
    dproj_b, d_convw, d_convb = _conv_bwd(proj_b, conv_w, _tie(conv_b, got_ffn1, "tie_got_ffn1"), dxbc, dproj_b, "conv_bwd")
    gb = _mm(dproj_b, a1, m=6 * D_MODEL, n=D_MODEL, k=s, ta=True, out_dtype=_ACT, name="proj_b_dw")
    gb_dt = _mm(ddt_raw, a1, m=HPAD, n=D_MODEL, k=s, ta=True, out_dtype=_ACT, name="proj_b_dw_dt")
    gb_full = jnp.concatenate([gb[:dt0], gb_dt[:SSM_HEADS], gb[dt0:]], axis=0)
    gb_blk = gb_full.reshape(N_DEV, b_cols, D_MODEL)
    sums, got_mix1 = reduce_scatter([("kv", 1, g_kv[1]), ("b", 0, gb_blk)], "mix1")
    da1 = _mm(dproj_b, WBT, m=s, n=D_MODEL, k=6 * D_MODEL, name="proj_b_dx")
    da1 = _mm(ddt_raw, WBDT, m=s, n=D_MODEL, k=HPAD, add=da1, name="proj_b_dx_dt")
    dh2, dh2_act, d_nmix[1] = _rms_bwd(h2, _tie(nmix[1], sums, "tie_mix1"), da1, dh3, "mix_norm_bwd1")

    dh1, dh1_act = ffn_bwd(dh2, dh2_act, h1, f0, p0, 0, after=got_ffn1)
    dcat_a = out_bwd(dh1_act, cat_a, 0)
    sums, _ = reduce_scatter([("f1", 0, g_f1[0]), ("f2", 0, g_f2[0]), ("out", 0, g_out[0])], "ffn0")
    dproj_a, d_ws, d_bs3, d_lng, d_lnb = _gmlp_bwd(proj_a, dcat_a, _tie(lng, got_mix1, "tie_got_mix1"), lnb, ws, bs3, "gmlp_bwd")
    dproj_a, dkv_a = _attn_bwd(proj_a, 4, kvs[0], dcat_a, dproj_a, "attn_bwd0")
    da0 = _mm(dproj_a, WA, m=s, n=D_MODEL, k=5 * D_MODEL, tb=True, b_at=(0, 0, 0), after=sums, name="proj_a_dx")
    grad_x, _, d_nmix[0] = _rms_bwd(xs, nmix[0], da0, dh1, "mix_norm_bwd0")
    mem_bwd(dkv_a, 0)
    ga = _mm(a0, dproj_a, m=D_MODEL, n=5 * D_MODEL, k=s, ta=True, out_dtype=_ACT, name="proj_a_dw")
    sums, _ = reduce_scatter([("kv", 0, g_kv[0]), ("a", 0, ga)], "mix0")

    def big_update(w, m, v, fam, nlayer):
        res = None
        for i in range(nlayer):
            part, recv2 = reduced[fam, i]
            plist = [(part, 0), (recv2, 0), (recv2, 1), (recv2, 2)]
            res = _adamw(w, m, v, plist, f"adamw_{fam}{i}", layer=i, prev=res)
        return res

    r_f1 = big_update(w_ffn1, m_w_ffn1, v_w_ffn1, "f1", 2)
    r_f2 = big_update(w_ffn2, m_w_ffn2, v_w_ffn2, "f2", 2)
    r_out = big_update(w_out, m_w_out, v_w_out, "out", 2)
    r_kv = big_update(w_kv, m_w_kv, v_w_kv, "kv", 2)
    r_a = big_update(a_in, m_a_in, v_a_in, "a", 1)
    r_b = [tr_b(o) for o in big_update(tr_b(b_in), tr_b(m_b_in), tr_b(v_b_in), "b", 1)]

    rep_names = ["norm_mix", "norm_ffn", "mem_norm", "a_ln_g", "a_ln_b", "a_ws", "a_bs", "b_dt_bias", "b_a_log", "b_d",
                 "final_norm"]
    rep_grads = [jnp.concatenate(d_nmix, axis=0), jnp.concatenate(d_nffn, axis=0), jnp.concatenate(d_nmem, axis=0),
                 d_lng, d_lnb, d_ws.reshape(1, A_GROUPS, CHUNK, CHUNK), d_bs3.reshape(1, A_GROUPS, CHUNK),
                 d_dtbias[:, :SSM_HEADS], d_alog[:, :SSM_HEADS], d_dskip[:, :SSM_HEADS], d_fin.reshape(D_MODEL)]
    rep_w = [norm_mix, norm_ffn, mem_norm, a_ln_g, a_ln_b, a_ws, a_bs, b_dt_bias, b_a_log, b_d, final_norm]
    rep_m = [m_norm_mix, m_norm_ffn, m_mem_norm, m_a_ln_g, m_a_ln_b, m_a_ws, m_a_bs, m_b_dt_bias, m_b_a_log, m_b_d, m_final_norm]
    rep_v = [v_norm_mix, v_norm_ffn, v_mem_norm, v_a_ln_g, v_a_ln_b, v_a_ws, v_a_bs, v_b_dt_bias, v_b_a_log, v_b_d, v_final_norm]
    rep_grads = [g.reshape(w.shape) for g, w in zip(rep_grads, rep_w)]
    sh_grads = [d_convw, d_convb, d_gnorm]
    g_pack, g_layout = _pack(rep_grads + sh_grads + [loss_part])
    n_rep = len(rep_grads)
    (g_all,) = _all_gather([g_pack], [("blk", 0, (N_DEV,) + g_pack.shape)], "ag_small_grads", after=sums)
    g_small = _sum8(g_all, "sum_small_grads")
    g_list = _unpack(g_small, g_layout)
    loss = g_list[-1][0, 0]
    wp, w_layout = _pack(rep_w)
    mp, _ = _pack(rep_m)
    vp, _ = _pack(rep_v)
    gp, _ = _pack(g_list[:n_rep])
    rep_res = [_unpack(o, w_layout) for o in _adamw(wp, mp, vp, [(gp, None)], "adamw_replicated", tr=88)]

    gcw = lax.dynamic_slice_in_dim(g_list[n_rep], me * 384, 384, axis=1).reshape(1, CONV_K, 384)
    gcb = lax.dynamic_slice_in_dim(g_list[n_rep + 1], me * 384, 384, axis=1)
    ggn = lax.dynamic_slice_in_dim(g_list[n_rep + 2], me * 256, 256, axis=1)
    sh_w = [b_conv_w, b_conv_b, b_gnorm]
    sh_m = [m_b_conv_w, m_b_conv_b, m_b_gnorm]
    sh_v = [v_b_conv_w, v_b_conv_b, v_b_gnorm]
    swp, sw_layout = _pack(sh_w)
    smp, _ = _pack(sh_m)
    svp, _ = _pack(sh_v)
    sgp, _ = _pack([gcw, gcb, ggn])
    sh_res = [_unpack(o, sw_layout) for o in _adamw(swp, smp, svp, [(sgp, None)], "adamw_sharded_small", tr=8)]

    names = ["norm_mix", "norm_ffn", "mem_norm", "w_kv", "w_out", "w_ffn1", "w_ffn2", "a_in", "a_ln_g", "a_ln_b", "a_ws",
             "a_bs", "b_in", "b_conv_w", "b_conv_b", "b_dt_bias", "b_a_log", "b_d", "b_gnorm", "final_norm"]
    big = {"w_kv": r_kv, "w_out": r_out, "w_ffn1": r_f1, "w_ffn2": r_f2, "a_in": r_a, "b_in": r_b}
    sh_names = ["b_conv_w", "b_conv_b", "b_gnorm"]
    outs = [loss, grad_x.reshape(x.shape)]
    for kind in range(4):
        for nm in names:
            if nm in big:
                outs.append(big[nm][kind])
            elif nm in sh_names:
                outs.append(sh_res[kind][sh_names.index(nm)])
            else:
                outs.append(rep_res[kind][rep_names.index(nm)])
    return tuple(outs)
```

```python
import functools
import math

import jax
import jax.numpy as jnp
from jax import lax
from jax.experimental import pallas as pl
from jax.experimental.pallas import tpu as pltpu
from jax.experimental.pallas import tpu_sc as plsc

F32 = jnp.float32
_MXU = jnp.bfloat16
_ACT = jnp.bfloat16
_HI = lax.Precision.HIGHEST

D_MODEL = 1024
CHUNK = 128
N_MEM = 256
D_INNER = 2048
A_GROUPS = 8
A_GW = D_INNER // A_GROUPS
SSM_HEADS = 32
SSM_P = 64
SSM_GROUPS = 4
SSM_GW = D_INNER // SSM_GROUPS
SSM_N = 128
CONV_K = 4
CONV_DIM = 3072
X_HEADS = 4
X_HD = 256
X_WIDTH = 1024
D_FF = 4096
EPS = 1e-6
HPAD = 128
N_DEV = 8

ADAM_LR = 0.001
ADAM_B1 = 0.9
ADAM_B2 = 0.999
ADAM_EPS = 1e-08
ADAM_WD = 0.01
ADAM_STEP = 10

VMEM_BIG = 56 * 1024 * 1024
MESH = pl.DeviceIdType.MESH


def _cp(vmem=None):
    if vmem is None:
        return pltpu.CompilerParams()
    return pltpu.CompilerParams(vmem_limit_bytes=vmem)


def _dot(a, b, dims=((1,), (0,))):
    return lax.dot_general(a.astype(_MXU), b.astype(_MXU), (dims, ((), ())), preferred_element_type=F32)


def _dot_nt(a, b):
    return _dot(a, b, ((1,), (1,)))


def _dot_tn(a, b):
    return _dot(a, b, ((0,), (0,)))


def _dot_hi(a, b, dims=((1,), (0,))):
    return lax.dot_general(a.astype(F32), b.astype(F32), (dims, ((), ())), precision=_HI, preferred_element_type=F32)


def _split3(x):
    x1 = x.astype(jnp.bfloat16)
    r = x - x1.astype(F32)
    x2 = r.astype(jnp.bfloat16)
    x3 = (r - x2.astype(F32)).astype(jnp.bfloat16)
    return x1, x2, x3


def _dot_sel(x, sel, dims=((1,), (0,))):
    sel = sel.astype(jnp.bfloat16)
    parts = [lax.dot_general(t, sel, (dims, ((), ())), preferred_element_type=F32) for t in _split3(x)]
    return (parts[0] + parts[1]) + parts[2]


def _sel_dot(sel, x, dims=((1,), (0,))):
    sel = sel.astype(jnp.bfloat16)
    parts = [lax.dot_general(sel, t, (dims, ((), ())), preferred_element_type=F32) for t in _split3(x)]
    return (parts[0] + parts[1]) + parts[2]


def _sigmoid(x):
    return 1.0 / (1.0 + jnp.exp(-x))


def _gelu(x):
    return 0.5 * x * (1.0 + lax.erf(x * (1.0 / math.sqrt(2.0))))


def _gelu_grad(x):
    return 0.5 * (1.0 + lax.erf(x * (1.0 / math.sqrt(2.0)))) + x * jnp.exp(-0.5 * x * x) * (1.0 / math.sqrt(2.0 * math.pi))


def _softplus(x):
    return jnp.maximum(x, 0.0) + jnp.log1p(jnp.exp(-jnp.abs(x)))


def _iota(shape, dim):
    return lax.broadcasted_iota(jnp.int32, shape, dim)


MM_VMEM_BUDGET = 40 * 1024 * 1024
HBM_BYTES_PER_S = 2.5e12
GRID_STEP_S = 0.35e-6
VMEM_ACC_BYTES_PER_S = 6e12


def _divisors(dim, unit):
    out = [d for d in range(unit, min(dim, 2048) + 1, unit) if dim % d == 0]
    return out if out else [dim]


def _mm_tiles(m, n, k, sa, sb, s_mn, a_pro, offsets):
    best = None
    (a_r0, a_c0, ta), (b_r0, b_c0, tb), (o_r0, o_c0) = offsets
    for tm in _divisors(m, 128):
        for tn in _divisors(n, 128):
            for tk in [k // d for d in (1, 2, 3, 4, 6, 8) if k % d == 0 and (k // d) % 128 == 0]:
                a_t = (tk, tm) if ta else (tm, tk)
                b_t = (tn, tk) if tb else (tk, tn)
                if a_r0 % a_t[0] or a_c0 % a_t[1] or b_r0 % b_t[0] or b_c0 % b_t[1] or o_r0 % tm or o_c0 % tn:
                    continue
                nk = k // tk
                vmem = 2 * (tm * tk * sa + tk * tn * sb + tm * tn * s_mn) + tm * tn * 4 * (2 if nk > 1 else 1)
                if a_pro or sa == 4:
                    vmem += tm * tk * 6
                if sb == 4:
                    vmem += tk * tn * 2
                if vmem > MM_VMEM_BUDGET:
                    continue
                gi, gj = m // tm, n // tn
                for j_inner in (True, False):
                    if nk > 1:
                        traffic = gj * m * k * sa + gi * k * n * sb
                    elif j_inner:
                        traffic = m * k * sa + gi * k * n * sb
                    else:
                        traffic = gj * m * k * sa + k * n * sb
                    traffic += m * n * s_mn + (tm * tk * sa + tk * tn * sb)
                    cost = traffic / HBM_BYTES_PER_S + gi * gj * nk * GRID_STEP_S
                    if nk > 1:
                        cost += m * n * 8 * nk / VMEM_ACC_BYTES_PER_S
                    if best is None or cost < best[0]:
                        best = (cost, tm, tn, tk, j_inner)
    assert best is not None, (m, n, k)
    return best[1:]


def _mm(a, b, *, m, n, k, name, ta=False, tb=False, a_at=(None, 0, 0), b_at=(None, 0, 0),
        out_dtype=F32, add=None, epi_p=None, epi_at=(None, 0, 0), out=None, out_at=(None, 0, 0),
        out_full=None, a_pro=None, after=()):
    s_mn =jnp.dtype(out.dtype if out is not None else out_dtype).itemsize
    s_mn += add.dtype.itemsize if add is not None else 0
    s_mn += epi_p.dtype.itemsize if epi_p is not None else 0
    tm, tn, tk, j_inner = _mm_tiles(m, n, k, a.dtype.itemsize, b.dtype.itemsize, s_mn, a_pro is not None,
                                    ((a_at[1], a_at[2], ta), (b_at[1], b_at[2], tb), (out_at[1], out_at[2])))
    nk = k // tk

    def spec(at, tr, tc, rsel, csel):
        lead, r0, c0 = at
        assert r0 % tr == 0 and c0 % tc == 0, (name, at, tr, tc)
        rb, cb = r0 // tr, c0 // tc
        if lead is None:
            return pl.BlockSpec((tr, tc), lambda g0, g1, kk: (rb + rsel(g0, g1, kk), cb + csel(g0, g1, kk)))
        return pl.BlockSpec((None, tr, tc), lambda g0, g1, kk: (lead, rb + rsel(g0, g1, kk), cb + csel(g0, g1, kk)))

    gi = (lambda g0, g1, kk: g0) if j_inner else (lambda g0, g1, kk: g1)
    gj = (lambda g0, g1, kk: g1) if j_inner else (lambda g0, g1, kk: g0)
    gk = lambda g0, g1, kk: kk
    a_spec = spec(a_at, tk, tm, gk, gi) if ta else spec(a_at, tm, tk, gi, gk)
    b_spec = spec(b_at, tn, tk, gj, gk) if tb else spec(b_at, tk, tn, gk, gj)
    dims = ((0,), (0,)) if ta else (((1,), (1,)) if tb else ((1,), (0,)))
    assert not (ta and tb)

    operands, in_specs = [a, b], [a_spec, b_spec]
    if add is not None:
        operands.append(add)
        in_specs.append(spec((None, 0, 0), tm, tn, gi, gj))
    if epi_p is not None:
        operands.append(epi_p)
        in_specs.append(spec(epi_at, tm, tn, gi, gj))
    aliases = {}
    if out is not None:
        aliases = {len(operands): 0}
        operands.append(out)
        in_specs.append(pl.BlockSpec(memory_space=pl.ANY))
        out_struct = jax.ShapeDtypeStruct(out.shape, out.dtype)
        out_dtype = out.dtype
    else:
        out_struct = jax.ShapeDtypeStruct(out_full if out_full is not None else (m, n), out_dtype)
    has_add, has_epi = add is not None, epi_p is not None
    n_skip = (1 if out is not None else 0) + len(after)
    operands += list(after)
    in_specs += [pl.BlockSpec(memory_space=pl.ANY)] * len(after)

    def body(*refs):
        a_ref, b_ref = refs[0], refs[1]
        pos = 2
        add_ref = epi_ref = None
        if has_add:
            add_ref = refs[pos]
            pos += 1
        if has_epi:
            epi_ref = refs[pos]
            pos += 1
        pos += n_skip
        o_ref = refs[pos]

        def finish(r):
            if has_add:
                r = r + add_ref[...].astype(F32)
            if has_epi:
                r = r * (2.0 * jnp.maximum(epi_ref[...].astype(F32), 0.0))
            o_ref[...] = r.astype(o_ref.dtype)

        av = a_ref[...]
        if a_pro == "relu2":
            av = jnp.square(jnp.maximum(av.astype(F32), 0.0))
        part = _dot(av, b_ref[...], dims)
        if nk == 1:
            finish(part)
        else:
            acc_ref = refs[pos + 1]
            kk = pl.program_id(2)

            @pl.when(kk == 0)
            def _():
                acc_ref[...] = part

            @pl.when(kk > 0)
            def _():
                acc_ref[...] += part

            @pl.when(kk == nk - 1)
            def _():
                finish(acc_ref[...])

    grid = (m // tm, n // tn, nk) if j_inner else (n // tn, m // tm, nk)
    return pl.pallas_call(
        body, name=name, grid=grid, in_specs=in_specs,
        out_specs=spec(out_at, tm, tn, gi, gj), out_shape=out_struct,
        scratch_shapes=[pltpu.VMEM((tm, tn), F32)] if nk > 1 else [], input_output_aliases=aliases,
        compiler_params=_cp(VMEM_BIG))(*operands)


def _rms_fwd(x, g, name, tm=256):
    s, d = x.shape
    tm = min(tm, s)

    def body(x_ref, g_ref, o_ref):
        xv = x_ref[...]
        r = lax.rsqrt(jnp.mean(xv * xv, axis=-1, keepdims=True) + EPS)
        o_ref[...] = (xv * r * g_ref[...]).astype(o_ref.dtype)

    return pl.pallas_call(
        body, name=name, grid=(s // tm,),
        in_specs=[pl.BlockSpec((tm, d), lambda i: (i, 0)), pl.BlockSpec((1, d), lambda i: (0, 0))],
        out_specs=pl.BlockSpec((tm, d), lambda i: (i, 0)),
        out_shape=jax.ShapeDtypeStruct((s, d), _ACT))(x, g)


def _rms_bwd(x, g, dy, dres, name, tm=256):
    s, d = x.shape
    tm = min(tm, s)
    has_res = dres is not None

    def body(*refs):
        if has_res:
            x_ref, g_ref, dy_ref, dres_ref, dx_ref, dxa_ref, dg_ref = refs
        else:
            x_ref, g_ref, dy_ref, dx_ref, dxa_ref, dg_ref = refs

        @pl.when(pl.program_id(0) == 0)
        def _():
            dg_ref[...] = jnp.zeros_like(dg_ref)

        xv = x_ref[...]
        dyv = dy_ref[...].astype(F32)
        r = lax.rsqrt(jnp.mean(xv * xv, axis=-1, keepdims=True) + EPS)
        xh = xv * r
        dyg = dyv * g_ref[...]
        dx = r * (dyg - xh * jnp.mean(dyg * xh, axis=-1, keepdims=True))
        if has_res:
            dx = dx + dres_ref[...]
        dx_ref[...] = dx
        dxa_ref[...] = dx.astype(dxa_ref.dtype)
        dg_ref[...] += jnp.sum(dyv * xh, axis=0, keepdims=True)

    row = pl.BlockSpec((tm, d), lambda i: (i, 0))
    vec = pl.BlockSpec((1, d), lambda i: (0, 0))
    in_specs = [row, vec, row] + ([row] if has_res else [])
    operands = [x, g, dy] + ([dres] if has_res else [])
    return pl.pallas_call(
        body, name=name, grid=(s // tm,), in_specs=in_specs, out_specs=[row, row, vec],
        out_shape=[jax.ShapeDtypeStruct((s, d), F32), jax.ShapeDtypeStruct((s, d), _ACT),
                   jax.ShapeDtypeStruct((1, d), F32)])(*operands)


def _loss_head(h, g, target, name, tm=256):
    s, d = h.shape
    tm = min(tm, s)

    def body(h_ref, g_ref, t_ref, loss_ref, dh_ref, dha_ref, dg_ref):
        @pl.when(pl.program_id(0) == 0)
        def _():
            dg_ref[...] = jnp.zeros_like(dg_ref)
            loss_ref[...] = jnp.zeros_like(loss_ref)

        xv = h_ref[...]
        r = lax.rsqrt(jnp.mean(xv * xv, axis=-1, keepdims=True) + EPS)
        xh = xv * r
        err = xh * g_ref[...] - t_ref[...]
        loss_ref[...] += jnp.full(loss_ref.shape, 0.5 * jnp.sum(jnp.mean(err * err, axis=-1, keepdims=True)), F32)
        dyv = err * (1.0 / d)
        dyg = dyv * g_ref[...]
        dh = r * (dyg - xh * jnp.mean(dyg * xh, axis=-1, keepdims=True))
        dh_ref[...] = dh
        dha_ref[...] = dh.astype(dha_ref.dtype)
        dg_ref[...] += jnp.sum(dyv * xh, axis=0, keepdims=True)

    row = pl.BlockSpec((tm, d), lambda i: (i, 0))
    vec = pl.BlockSpec((1, d), lambda i: (0, 0))
    return pl.pallas_call(
        body, name=name, grid=(s // tm,), in_specs=[row, vec, row],
        out_specs=[pl.BlockSpec((1, 128), lambda i: (0, 0)), row, row, vec],
        out_shape=[jax.ShapeDtypeStruct((1, 128), F32), jax.ShapeDtypeStruct((s, d), F32),
                   jax.ShapeDtypeStruct((s, d), _ACT), jax.ShapeDtypeStruct((1, d), F32)])(h, g, target)


def _gmlp_parts(pu, pv, lng, lnb):
    u = _gelu(pu)
    v = _gelu(pv)
    mu = jnp.mean(v, axis=-1, keepdims=True)
    vc = v - mu
    rstd = lax.rsqrt(jnp.mean(vc * vc, axis=-1, keepdims=True) + EPS)
    xhat = vc * rstd
    vn = xhat * lng + lnb
    return u, xhat, rstd, vn


def _gmlp_fwd(proj, lng, lnb, ws, bs3, name):
    s = proj.shape[0]

    def body(pu_ref, pv_ref, lng_ref, lnb_ref, ws_ref, bs_ref, o_ref):
        u, _, _, vn = _gmlp_parts(pu_ref[...], pv_ref[...], lng_ref[...], lnb_ref[...])
        causal = _iota((CHUNK, CHUNK), 0) >= _iota((CHUNK, CHUNK), 1)
        for g in range(A_GROUPS):
            sl = slice(g * A_GW, (g + 1) * A_GW)
            w = jnp.where(causal, ws_ref[g], 0.0)
            sv = _dot(w, vn[:, sl]) + bs_ref[g]
            o_ref[:, sl] = (u[:, sl] * sv).astype(o_ref.dtype)

    full = lambda shape: pl.BlockSpec(shape, lambda c: (0,) * len(shape))
    return pl.pallas_call(
        body, name=name, grid=(s // CHUNK,),
        in_specs=[pl.BlockSpec((CHUNK, D_INNER), lambda c: (c, 0)), pl.BlockSpec((CHUNK, D_INNER), lambda c: (c, 1)),
                  full((1, D_INNER)), full((1, D_INNER)), full((A_GROUPS, CHUNK, CHUNK)), full((A_GROUPS, CHUNK, 1))],
        out_specs=pl.BlockSpec((CHUNK, D_INNER), lambda c: (c, 0)),
        out_shape=jax.ShapeDtypeStruct((s, D_INNER + X_WIDTH), _ACT), compiler_params=_cp(VMEM_BIG))(proj, proj, lng, lnb, ws, bs3)


def _gmlp_bwd(proj, dcat, lng, lnb, ws, bs3, name):
    s = proj.shape[0]

    def body(pu_ref, pv_ref, dm_ref, lng_ref, lnb_ref, ws_ref, bs_ref, dp_ref, dws_ref, dbs_ref, dlng_ref, dlnb_ref, dvn_ref):
        @pl.when(pl.program_id(0) == 0)
        def _():
            dws_ref[...] = jnp.zeros_like(dws_ref)
            dbs_ref[...] = jnp.zeros_like(dbs_ref)
            dlng_ref[...] = jnp.zeros_like(dlng_ref)
            dlnb_ref[...] = jnp.zeros_like(dlnb_ref)

        pu, pv = pu_ref[...], pv_ref[...]
        lng = lng_ref[...]
        u, xhat, rstd, vn = _gmlp_parts(pu, pv, lng, lnb_ref[...])
        dm = dm_ref[...].astype(F32)
        causal = _iota((CHUNK, CHUNK), 0) >= _iota((CHUNK, CHUNK), 1)
        for g in range(A_GROUPS):
            sl = slice(g * A_GW, (g + 1) * A_GW)
            w = jnp.where(causal, ws_ref[g], 0.0)
            sv = _dot(w, vn[:, sl]) + bs_ref[g]
            dsv = dm[:, sl] * u[:, sl]
            dp_ref[:, sl] = (dm[:, sl] * sv * _gelu_grad(pu[:, sl])).astype(dp_ref.dtype)
            dvn_ref[:, sl] = _dot_tn(w, dsv)
            dws_ref[g] += jnp.where(causal, _dot_nt(dsv, vn[:, sl]), 0.0)
            dbs_ref[g] += jnp.sum(dsv, axis=-1, keepdims=True)
        dvn = dvn_ref[...]
        dlng_ref[...] += jnp.sum(dvn * xhat, axis=0, keepdims=True)
        dlnb_ref[...] += jnp.sum(dvn, axis=0, keepdims=True)
        dxh = dvn * lng
        dv = rstd * (dxh - jnp.mean(dxh, axis=-1, keepdims=True) - xhat * jnp.mean(dxh * xhat, axis=-1, keepdims=True))
        dp_ref[:, D_INNER:] = (dv * _gelu_grad(pv)).astype(dp_ref.dtype)

    full = lambda shape: pl.BlockSpec(shape, lambda c: (0,) * len(shape))
    return pl.pallas_call(
        body, name=name, grid=(s // CHUNK,),
        in_specs=[pl.BlockSpec((CHUNK, D_INNER), lambda c: (c, 0)), pl.BlockSpec((CHUNK, D_INNER), lambda c: (c, 1)),
                  pl.BlockSpec((CHUNK, D_INNER), lambda c: (c, 0)),
                  full((1, D_INNER)), full((1, D_INNER)), full((A_GROUPS, CHUNK, CHUNK)), full((A_GROUPS, CHUNK, 1))],
        out_specs=[pl.BlockSpec((CHUNK, 2 * D_INNER), lambda c: (c, 0)), full((A_GROUPS, CHUNK, CHUNK)),
                   full((A_GROUPS, CHUNK, 1)), full((1, D_INNER)), full((1, D_INNER))],
        out_shape=[jax.ShapeDtypeStruct((s, 2 * D_INNER + X_WIDTH), _ACT), jax.ShapeDtypeStruct((A_GROUPS, CHUNK, CHUNK), F32),
                   jax.ShapeDtypeStruct((A_GROUPS, CHUNK, 1), F32), jax.ShapeDtypeStruct((1, D_INNER), F32),
                   jax.ShapeDtypeStruct((1, D_INNER), F32)],
        scratch_shapes=[pltpu.VMEM((CHUNK, D_INNER), F32)],
        compiler_params=_cp(VMEM_BIG))(proj, proj, dcat, lng, lnb, ws, bs3)


_X_SCALE = 1.0 / math.sqrt(X_HD)


def _attn_fwd(proj, qblk, kv, cat, name, tm=256):
    s = proj.shape[0]
    tm = min(tm, s)

    def body(q_ref, kv_ref, cat_ref, o_ref):
        for h in range(X_HEADS):
            sl = slice(h * X_HD, (h + 1) * X_HD)
            k = kv_ref[:, sl]
            v = kv_ref[:, X_WIDTH + h * X_HD:X_WIDTH + (h + 1) * X_HD]
            sc = _dot_nt(q_ref[:, sl], k) * _X_SCALE
            e = jnp.exp(sc - jnp.max(sc, axis=-1, keepdims=True))
            p = e / jnp.sum(e, axis=-1, keepdims=True)
            o_ref[:, sl] = _dot(p, v).astype(o_ref.dtype)

    return pl.pallas_call(
        body, name=name, grid=(s // tm,),
        in_specs=[pl.BlockSpec((tm, X_WIDTH), lambda i: (i, qblk)), pl.BlockSpec((N_MEM, 2 * X_WIDTH), lambda i: (0, 0)),
                  pl.BlockSpec(memory_space=pl.ANY)],
        out_specs=pl.BlockSpec((tm, X_WIDTH), lambda i: (i, D_INNER // X_WIDTH)),
        out_shape=jax.ShapeDtypeStruct(cat.shape, cat.dtype), input_output_aliases={2: 0})(proj, kv, cat)


def _attn_bwd(proj, qblk, kv, dcat, dproj, name, tm=256):
    s = proj.shape[0]
    tm = min(tm, s)

    def body(q_ref, kv_ref, do_ref, dproj_ref, dq_ref, dkv_ref):
        @pl.when(pl.program_id(0) == 0)
        def _():
            dkv_ref[...] = jnp.zeros_like(dkv_ref)

        for h in range(X_HEADS):
            sl = slice(h * X_HD, (h + 1) * X_HD)
            slv = slice(X_WIDTH + h * X_HD, X_WIDTH + (h + 1) * X_HD)
            q = q_ref[:, sl]
            k = kv_ref[:, sl]
            v = kv_ref[:, slv]
            do = do_ref[:, sl].astype(F32)
            sc = _dot_nt(q, k) * _X_SCALE
            e = jnp.exp(sc - jnp.max(sc, axis=-1, keepdims=True))
            p = e / jnp.sum(e, axis=-1, keepdims=True)
            dp = _dot_nt(do, v)
            ds = p * (dp - jnp.sum(dp * p, axis=-1, keepdims=True)) * _X_SCALE
            dq_ref[:, sl] = _dot(ds, k).astype(dq_ref.dtype)
            dkv_ref[:, sl] += _dot_tn(ds, q)
            dkv_ref[:, slv] += _dot_tn(p, do)

    return pl.pallas_call(
        body, name=name, grid=(s // tm,),
        in_specs=[pl.BlockSpec((tm, X_WIDTH), lambda i: (i, qblk)), pl.BlockSpec((N_MEM, 2 * X_WIDTH), lambda i: (0, 0)),
                  pl.BlockSpec((tm, X_WIDTH), lambda i: (i, 2)), pl.BlockSpec(memory_space=pl.ANY)],
        out_specs=[pl.BlockSpec((tm, X_WIDTH), lambda i: (i, qblk)), pl.BlockSpec((N_MEM, 2 * X_WIDTH), lambda i: (0, 0))],
        out_shape=[jax.ShapeDtypeStruct(dproj.shape, dproj.dtype), jax.ShapeDtypeStruct((N_MEM, 2 * X_WIDTH), F32)],
        input_output_aliases={3: 0})(proj, kv, dcat, dproj)


CONV_TC = 256
_XBC_BLK0 = D_INNER // CONV_TC


CONV_RB = 64
SUBLANES = 8


def _rows_before(cur, prev_last, j):
    rolled = pltpu.roll(cur, j, 0)
    head = jnp.where(_iota((SUBLANES, cur.shape[1]), 0) < j, pltpu.roll(prev_last, j, 0), rolled[:SUBLANES])
    return jnp.concatenate([head, rolled[SUBLANES:]], axis=0)


def _rows_after(cur, next_first, j):
    n = cur.shape[0]
    rolled = pltpu.roll(cur, n - j, 0)
    tail = jnp.where(_iota((SUBLANES, cur.shape[1]), 0) >= SUBLANES - j, pltpu.roll(next_first, SUBLANES - j, 0),
                     rolled[n - SUBLANES:])
    return jnp.concatenate([rolled[:n - SUBLANES], tail], axis=0)


def _conv_pre(x_ref, w_ref, b_ref, r0, prev_last):
    cur = x_ref[pl.ds(r0, CONV_RB), :]
    shifts = [_rows_before(cur, prev_last, j) for j in range(1, CONV_K)]
    pre = b_ref[...] + w_ref[CONV_K - 1:CONV_K, :] * cur
    for j in range(1, CONV_K):
        pre = pre + w_ref[CONV_K - 1 - j:CONV_K - j, :] * shifts[j - 1]
    return pre, cur, shifts


def _conv_fwd(proj, w, b, name):
    s = proj.shape[0]

    def body(x_ref, w_ref, b_ref, o_ref):
        def step(i, prev_last):
            r0 = pl.multiple_of(i * CONV_RB, CONV_RB)
            pre, cur, _ = _conv_pre(x_ref, w_ref, b_ref, r0, prev_last)
            o_ref[pl.ds(r0, CONV_RB), :] = pre * _sigmoid(pre)
            return cur[CONV_RB - SUBLANES:]

        lax.fori_loop(0, s // CONV_RB, step, jnp.zeros((SUBLANES, CONV_TC), F32))

    return pl.pallas_call(
        body, name=name, grid=(CONV_DIM // CONV_TC,),
        in_specs=[pl.BlockSpec((s, CONV_TC), lambda j: (0, _XBC_BLK0 + j)), pl.BlockSpec((CONV_K, CONV_TC), lambda j: (0, j)),
                  pl.BlockSpec((1, CONV_TC), lambda j: (0, j))],
        out_specs=pl.BlockSpec((s, CONV_TC), lambda j: (0, j)),
        out_shape=jax.ShapeDtypeStruct((s, CONV_DIM), F32), compiler_params=_cp(VMEM_BIG))(proj, w, b)


def _conv_bwd(proj, w, b, dxbc, dproj, name):
    s = proj.shape[0]

    nb = s // CONV_RB

    def body(x_ref, w_ref, b_ref, d_ref, dproj_ref, dx_ref, dw_ref, db_ref, dpre_ref):
        def fold(v):
            out = v[:SUBLANES]
            for t in range(1, CONV_RB // SUBLANES):
                out = out + v[t * SUBLANES:(t + 1) * SUBLANES]
            return out

        def first(i, carry):
            prev_last, acc = carry
            r0 = pl.multiple_of(i * CONV_RB, CONV_RB)
            pre, cur, shifts = _conv_pre(x_ref, w_ref, b_ref, r0, prev_last)
            sig = _sigmoid(pre)
            dpre = d_ref[pl.ds(r0, CONV_RB), :] * (sig * (1.0 + pre * (1.0 - sig)))
            dpre_ref[pl.ds(r0, CONV_RB), :] = dpre
            taps = [cur] + shifts
            acc = tuple(a + fold(dpre * t) for a, t in zip(acc[:CONV_K], taps)) + (acc[CONV_K] + fold(dpre),)
            return cur[CONV_RB - SUBLANES:], acc

        zero8 = jnp.zeros((SUBLANES, CONV_TC), F32)
        _, acc = lax.fori_loop(0, nb, first, (zero8, (zero8,) * (CONV_K + 1)))
        for j in range(CONV_K):
            dw_ref[CONV_K - 1 - j:CONV_K - j, :] = jnp.sum(acc[j], axis=0, keepdims=True)
        db_ref[...] = jnp.sum(acc[CONV_K], axis=0, keepdims=True)

        def second(i, next_first):
            r0 = pl.multiple_of((nb - 1 - i) * CONV_RB, CONV_RB)
            cur = dpre_ref[pl.ds(r0, CONV_RB), :]
            dx = w_ref[CONV_K - 1:CONV_K, :] * cur
            for j in range(1, CONV_K):
                dx = dx + w_ref[CONV_K - 1 - j:CONV_K - j, :] * _rows_after(cur, next_first, j)
            dx_ref[pl.ds(r0, CONV_RB), :] = dx.astype(dx_ref.dtype)
            return cur[:SUBLANES]

        lax.fori_loop(0, nb, second, zero8)

    return pl.pallas_call(
        body, name=name, grid=(CONV_DIM // CONV_TC,),
        in_specs=[pl.BlockSpec((s, CONV_TC), lambda j: (0, _XBC_BLK0 + j)), pl.BlockSpec((CONV_K, CONV_TC), lambda j: (0, j)),
                  pl.BlockSpec((1, CONV_TC), lambda j: (0, j)), pl.BlockSpec((s, CONV_TC), lambda j: (0, j)),
                  pl.BlockSpec(memory_space=pl.ANY)],
        out_specs=[pl.BlockSpec((s, CONV_TC), lambda j: (0, _XBC_BLK0 + j)), pl.BlockSpec((CONV_K, CONV_TC), lambda j: (0, j)),
                   pl.BlockSpec((1, CONV_TC), lambda j: (0, j))],
        out_shape=[jax.ShapeDtypeStruct(dproj.shape, dproj.dtype), jax.ShapeDtypeStruct((CONV_K, CONV_DIM), F32),
                   jax.ShapeDtypeStruct((1, CONV_DIM), F32)], input_output_aliases={4: 0},
        scratch_shapes=[pltpu.VMEM((s, CONV_TC), F32)],
        compiler_params=_cp(VMEM_BIG))(proj, w, b, dxbc, dproj)


def _ssd_common(dtc_ref, br_ref, ar_ref, csb_ref, cst_ref, csf_ref):
    a_row = -jnp.exp(ar_ref[...])
    dt_c = _softplus(dtc_ref[...] + br_ref[...])
    tril = _iota((CHUNK, CHUNK), 0) >= _iota((CHUNK, CHUNK), 1)
    cs = _sel_dot(tril, dt_c * a_row)
    cst_ref[...] = cs.T
    e64 = (jnp.right_shift(_iota((HPAD, D_INNER), 1), 6) == _iota((HPAD, D_INNER), 0)).astype(jnp.bfloat16)
    e128 = jnp.right_shift(_iota((HPAD, SSM_HEADS * CHUNK), 1), 7) == _iota((HPAD, SSM_HEADS * CHUNK), 0)
    csb_ref[...] = _dot_sel(cs, e128)
    dt_full = _dot_sel(dt_c, e64)
    csf_ref[...] = _dot_sel(cs, e64)
    cs_full = csf_ref[...]
    cs_last = csf_ref[CHUNK - 1:CHUNK, :]
    e_full = jnp.exp(cs_full)
    f_full = jnp.exp(cs_last - cs_full)
    gamma = jnp.exp(cs_last)
    return a_row, dt_c, cs, dt_full, e_full, f_full, gamma, e64


def _ssd_lambda(csb_ref, cst_ref, h, causal):
    diff = csb_ref[:, h * CHUNK:(h + 1) * CHUNK] - cst_ref[h:h + 1, :]
    return jnp.exp(jnp.where(causal, diff, -1e30))


_SSD_VEC_SPECS = lambda: [pl.BlockSpec((1, HPAD), lambda c: (0, 0)), pl.BlockSpec((1, HPAD), lambda c: (0, 0)),
                          pl.BlockSpec((1, D_INNER), lambda c: (0, 0))]


def _ssd_fwd(xbc, dtc, bias_row, alog_row, dfull, name):
    s = xbc.shape[0]
    nc = s // CHUNK

    def body(xbc_ref, dtc_ref, br_ref, ar_ref, df_ref, y_ref, st_ref, ht_ref, csb_ref, cst_ref, csf_ref):
        @pl.when(pl.program_id(0) == 0)
        def _():
            ht_ref[...] = jnp.zeros_like(ht_ref)

        _, _, _, dt_full, e_full, f_full, gamma, _ = _ssd_common(dtc_ref, br_ref, ar_ref, csb_ref, cst_ref, csf_ref)
        x = xbc_ref[:, :D_INNER]
        xdt = x * dt_full
        st_ref[...] = ht_ref[...]
        causal = _iota((CHUNK, CHUNK), 0) >= _iota((CHUNK, CHUNK), 1)
        lo = _iota((CHUNK, CHUNK), 1) < SSM_P
        for g in range(SSM_GROUPS):
            gs = slice(g * SSM_GW, (g + 1) * SSM_GW)
            bg = xbc_ref[:, D_INNER + g * SSM_N:D_INNER + (g + 1) * SSM_N]
            cg = xbc_ref[:, D_INNER + SSM_GROUPS * SSM_N + g * SSM_N:D_INNER + SSM_GROUPS * SSM_N + (g + 1) * SSM_N]
            ht = ht_ref[:, gs]
            cb = _dot_nt(cg, bg)
            yoff = e_full[:, gs] * _dot(cg, ht)
            for jp in range(SSM_GW // CHUNK):
                j = g * (SSM_GW // CHUNK) + jp
                ps = slice(j * CHUNK, (j + 1) * CHUNK)
                x2 = xdt[:, ps]
                y0 = _dot(cb * _ssd_lambda(csb_ref, cst_ref, 2 * j, causal), x2)
                y1 = _dot(cb * _ssd_lambda(csb_ref, cst_ref, 2 * j + 1, causal), x2)
                y_ref[:, ps] = (jnp.where(lo, y0, y1) + yoff[:, jp * CHUNK:(jp + 1) * CHUNK]
                                + x[:, ps] * df_ref[:, ps])
            ht_ref[:, gs] = gamma[:, gs] * ht + _dot_tn(bg, xdt[:, gs] * f_full[:, gs])

    return pl.pallas_call(
        body, name=name, grid=(nc,),
        in_specs=[pl.BlockSpec((CHUNK, CONV_DIM), lambda c: (c, 0)), pl.BlockSpec((CHUNK, HPAD), lambda c: (c, 0))]
                 + _SSD_VEC_SPECS(),
        out_specs=[pl.BlockSpec((CHUNK, D_INNER), lambda c: (c, 0)), pl.BlockSpec((None, SSM_N, D_INNER), lambda c: (c, 0, 0))],
        out_shape=[jax.ShapeDtypeStruct((s, D_INNER), F32), jax.ShapeDtypeStruct((nc, SSM_N, D_INNER), F32)],
        scratch_shapes=[pltpu.VMEM((SSM_N, D_INNER), F32), pltpu.VMEM((CHUNK, SSM_HEADS * CHUNK), F32),
                        pltpu.VMEM((HPAD, CHUNK), F32), pltpu.VMEM((CHUNK, D_INNER), F32)],
        compiler_params=_cp(VMEM_BIG))(xbc, dtc, bias_row, alog_row, dfull)


def _ssd_bwd(xbc, dtc, bias_row, alog_row, dfull, dy, states, name):
    s = xbc.shape[0]
    nc = s // CHUNK
    rev = lambda c: nc - 1 - c

    def body(xbc_ref, dtc_ref, br_ref, ar_ref, df_ref, dy_ref, st_ref,
             dxbc_ref, ddt_ref, dalog_ref, dd_ref, dbias_ref,
             dht_ref, csb_ref, cst_ref, csf_ref, ddf_ref, dxs_ref, dcsf_ref, dcsl_ref):
        step = pl.program_id(0)

        @pl.when(step == 0)
        def _():
            dht_ref[...] = jnp.zeros_like(dht_ref)
            ddf_ref[...] = jnp.zeros_like(ddf_ref)
            dalog_ref[...] = jnp.zeros_like(dalog_ref)
            dbias_ref[...] = jnp.zeros_like(dbias_ref)
            dd_ref[...] = jnp.zeros_like(dd_ref)

        a_row, dt_c, _, dt_full, e_full, f_full, gamma, e64 = _ssd_common(dtc_ref, br_ref, ar_ref, csb_ref, cst_ref, csf_ref)
        x = xbc_ref[:, :D_INNER]
        xdt = x * dt_full
        dy_all = dy_ref[...]
        ddf_ref[...] += jnp.broadcast_to(jnp.sum(dy_all * x, axis=0, keepdims=True), ddf_ref.shape)
        causal = _iota((CHUNK, CHUNK), 0) >= _iota((CHUNK, CHUNK), 1)
        lo = _iota((CHUNK, CHUNK), 1) < SSM_P
        head_lane = _iota((CHUNK, HPAD), 1)
        head_row = _iota((HPAD, CHUNK), 0)
        dcs_heads = jnp.zeros((CHUNK, HPAD), F32)
        dcs_cols = jnp.zeros((HPAD, CHUNK), F32)
        for g in range(SSM_GROUPS):
            gs = slice(g * SSM_GW, (g + 1) * SSM_GW)
            b0 = D_INNER + g * SSM_N
            c0 = D_INNER + SSM_GROUPS * SSM_N + g * SSM_N
            bg = xbc_ref[:, b0:b0 + SSM_N]
            cg = xbc_ref[:, c0:c0 + SSM_N]
            ht = st_ref[:, gs]
            dht = dht_ref[:, gs]
            dyg = dy_all[:, gs]
            eg, fg, gg = e_full[:, gs], f_full[:, gs], gamma[:, gs]
            z = _dot(cg, ht)
            dz = dyg * eg
            dcg = _dot_nt(dz, ht)
            dht_new = _dot_tn(cg, dz) + gg * dht
            xf = xdt[:, gs] * fg
            dxf = _dot(bg, dht)
            dbg = _dot_nt(xf, dht)
            dff = dxf * xf
            dcsf_ref[:, gs] = dyg * eg * z - dff
            dcsl_ref[:, gs] = jnp.broadcast_to(
                jnp.sum(dff, axis=0, keepdims=True) + jnp.sum(dht * ht, axis=0, keepdims=True) * gg, (8, SSM_GW))
            cb = _dot_nt(cg, bg)
            dcb = jnp.zeros((CHUNK, CHUNK), F32)
            for jp in range(SSM_GW // CHUNK):
                j = g * (SSM_GW // CHUNK) + jp
                ps = slice(j * CHUNK, (j + 1) * CHUNK)
                x2 = xdt[:, ps]
                dy2 = dy_all[:, ps]
                dxh = []
                for hh in range(2):
                    h = 2 * j + hh
                    lam = _ssd_lambda(csb_ref, cst_ref, h, causal)
                    mh = cb * lam
                    dyh = jnp.where(lo, dy2, 0.0) if hh == 0 else jnp.where(lo, 0.0, dy2)
                    dm = _dot_nt(dyh, x2)
                    dcb = dcb + dm * lam
                    gm = dm * mh
                    dcs_heads = dcs_heads + jnp.where(head_lane == h, jnp.sum(gm, axis=1, keepdims=True), 0.0)
                    dcs_cols = dcs_cols + jnp.where(head_row == h, jnp.sum(gm, axis=0, keepdims=True), 0.0)
                    dxh.append(_dot_tn(mh, dy2))
                dxs_ref[:, ps] = jnp.where(lo, dxh[0], dxh[1]) + dxf[:, jp * CHUNK:(jp + 1) * CHUNK] * fg[:, jp * CHUNK:(jp + 1) * CHUNK]
            dxbc_ref[:, b0:b0 + SSM_N] = (dbg + _dot_tn(dcb, cg)).astype(dxbc_ref.dtype)
            dxbc_ref[:, c0:c0 + SSM_N] = (dcg + _dot(dcb, bg)).astype(dxbc_ref.dtype)
            dht_ref[:, gs] = dht_new
        dxs = dxs_ref[...]
        dcs_heads = dcs_heads - dcs_cols.T + _dot_sel(dcsf_ref[...], e64, ((1,), (1,)))
        dcs_last = _dot_sel(dcsl_ref[...], e64, ((1,), (1,)))
        dcs_heads = dcs_heads + jnp.where(_iota((CHUNK, HPAD), 0) == CHUNK - 1, dcs_last[0:1, :], 0.0)
        triu = _iota((CHUNK, CHUNK), 0) <= _iota((CHUNK, CHUNK), 1)
        dda = _sel_dot(triu, dcs_heads)
        ddt = dda * a_row + _dot_sel(dxs * x, e64, ((1,), (1,)))
        dxbc_ref[:, :D_INNER] = (dxs * dt_full + dy_all * df_ref[...]).astype(dxbc_ref.dtype)
        dalog_ref[...] += jnp.sum(dda * dt_c, axis=0, keepdims=True) * a_row
        ddt_raw = ddt * _sigmoid(dtc_ref[...] + br_ref[...])
        ddt_ref[...] = ddt_raw.astype(ddt_ref.dtype)
        dbias_ref[...] += jnp.sum(ddt_raw, axis=0, keepdims=True)

        @pl.when(step == nc - 1)
        def _():
            dd_ref[...] = _dot_sel(ddf_ref[...], e64, ((1,), (1,)))[0:1, :]

    vec = pl.BlockSpec((1, HPAD), lambda c: (0, 0))
    return pl.pallas_call(
        body, name=name, grid=(nc,),
        in_specs=[pl.BlockSpec((CHUNK, CONV_DIM), lambda c: (rev(c), 0)), pl.BlockSpec((CHUNK, HPAD), lambda c: (rev(c), 0))]
                 + _SSD_VEC_SPECS()
                 + [pl.BlockSpec((CHUNK, D_INNER), lambda c: (rev(c), 0)),
                    pl.BlockSpec((None, SSM_N, D_INNER), lambda c: (rev(c), 0, 0))],
        out_specs=[pl.BlockSpec((CHUNK, CONV_DIM), lambda c: (rev(c), 0)), pl.BlockSpec((CHUNK, HPAD), lambda c: (rev(c), 0)),
                   vec, vec, vec],
        out_shape=[jax.ShapeDtypeStruct((s, CONV_DIM), F32), jax.ShapeDtypeStruct((s, HPAD), _ACT),
                   jax.ShapeDtypeStruct((1, HPAD), F32), jax.ShapeDtypeStruct((1, HPAD), F32),
                   jax.ShapeDtypeStruct((1, HPAD), F32)],
        scratch_shapes=[pltpu.VMEM((SSM_N, D_INNER), F32), pltpu.VMEM((CHUNK, SSM_HEADS * CHUNK), F32),
                        pltpu.VMEM((HPAD, CHUNK), F32), pltpu.VMEM((CHUNK, D_INNER), F32),
                        pltpu.VMEM((8, D_INNER), F32), pltpu.VMEM((CHUNK, D_INNER), F32),
                        pltpu.VMEM((CHUNK, D_INNER), F32), pltpu.VMEM((8, D_INNER), F32)],
        compiler_params=_cp(VMEM_BIG))(xbc, dtc, bias_row, alog_row, dfull, dy, states)


def _gate_fwd(y, proj, gn, name, tm=256):
    s = y.shape[0]
    tm = min(tm, s)

    def body(y_ref, z_ref, gn_ref, o_ref):
        for g in range(SSM_GROUPS):
            gs = slice(g * SSM_GW, (g + 1) * SSM_GW)
            z = z_ref[:, gs]
            t = y_ref[:, gs] * (z * _sigmoid(z))
            r = lax.rsqrt(jnp.mean(t * t, axis=-1, keepdims=True) + EPS)
            o_ref[:, gs] = (t * r * gn_ref[:, gs]).astype(o_ref.dtype)

    row = pl.BlockSpec((tm, D_INNER), lambda i: (i, 0))
    return pl.pallas_call(
        body, name=name, grid=(s // tm,), in_specs=[row, row, pl.BlockSpec((1, D_INNER), lambda i: (0, 0))],
        out_specs=row, out_shape=jax.ShapeDtypeStruct((s, D_INNER + X_WIDTH), _ACT))(y, proj, gn)


def _gate_bwd(y, proj, gn, dcat, name, tm=256):
    s = y.shape[0]
    tm = min(tm, s)

    def body(y_ref, z_ref, gn_ref, dm_ref, dy_ref, dz_ref, dgn_ref):
        @pl.when(pl.program_id(0) == 0)
        def _():
            dgn_ref[...] = jnp.zeros_like(dgn_ref)

        for g in range(SSM_GROUPS):
            gs = slice(g * SSM_GW, (g + 1) * SSM_GW)
            z = z_ref[:, gs]
            yv = y_ref[:, gs]
            sig = _sigmoid(z)
            sz = z * sig
            t = yv * sz
            r = lax.rsqrt(jnp.mean(t * t, axis=-1, keepdims=True) + EPS)
            th = t * r
            dm = dm_ref[:, gs].astype(F32)
            dmg = dm * gn_ref[:, gs]
            dt_ = r * (dmg - th * jnp.mean(dmg * th, axis=-1, keepdims=True))
            dgn_ref[:, gs] += jnp.sum(dm * th, axis=0, keepdims=True)
            dy_ref[:, gs] = dt_ * sz
            dz_ref[:, gs] = (dt_ * yv * (sig * (1.0 + z * (1.0 - sig)))).astype(dz_ref.dtype)

    row = pl.BlockSpec((tm, D_INNER), lambda i: (i, 0))
    vec = pl.BlockSpec((1, D_INNER), lambda i: (0, 0))
    return pl.pallas_call(
        body, name=name, grid=(s // tm,), in_specs=[row, row, vec, row], out_specs=[row, row, vec],
        out_shape=[jax.ShapeDtypeStruct((s, D_INNER), F32), jax.ShapeDtypeStruct((s, 6 * D_MODEL), _ACT),
                   jax.ShapeDtypeStruct((1, D_INNER), F32)])(y, proj, gn, dcat)


def _block_of(kind, width):
    if kind == "col":
        return lambda ref, j: ref.at[:, :, pl.ds(pl.multiple_of(j * width, 128), width)]
    if kind == "row":
        return lambda ref, j: ref.at[:, pl.ds(pl.multiple_of(j * width, 8), width), :]
    return lambda ref, j: ref.at[j]


def _coords():
    return lax.axis_index("x"), lax.axis_index("y"), lax.axis_index("c")


def _rel_chip(x, y, k):
    return (1 - x if k & 1 else x), (1 - y if k & 2 else y)


_HBM = lambda: pl.BlockSpec(memory_space=pltpu.HBM)


def _all_gather(shards, layouts, name, after=()):
    n, n_after = len(shards), len(after)
    blocks = [_block_of(kind, width) for kind, width, _ in layouts]

    def body(*refs):
        _all_gather_body(refs[:n], refs[n + n_after:2 * n + n_after], *refs[2 * n + n_after:], blocks)

    return pl.pallas_call(
        body, name=name, in_specs=[_HBM()] * n + [pl.BlockSpec(memory_space=pl.ANY)] * n_after, out_specs=[_HBM()] * n,
        out_shape=[jax.ShapeDtypeStruct(shape, sh.dtype) for sh, (_, _, shape) in zip(shards, layouts)],
        scratch_shapes=[pltpu.SemaphoreType.DMA((n, 7)), pltpu.SemaphoreType.DMA((n, 7)), pltpu.SemaphoreType.DMA((n,))],
    )(*shards, *after)


def _all_gather_body(ins, outs, send_sems, recv_sems, local_sems, blocks):
    n = len(ins)
    x, y, c = _coords()
    sibling = (x, y, 1 - c)

    def copy(t, k, chip, core, to, src=None):
        dst = blocks[t](outs[t], 4 * chip[0] + 2 * chip[1] + core)
        return pltpu.make_async_remote_copy(
            src_ref=dst if src is None else src, dst_ref=dst, send_sem=send_sems.at[t, k],
            recv_sem=recv_sems.at[t, k], device_id=to, device_id_type=MESH)

    started = []
    for t in range(n):
        mine = pltpu.make_async_copy(ins[t], blocks[t](outs[t], 4 * x + 2 * y + c), local_sems.at[t])
        mine.start()
        started.append(mine)
    sends = []
    for t in range(n):
        for k in range(4):
            px, py = _rel_chip(x, y, k)
            cp = copy(t, k, (x, y), c, (px, py, 1 - c if k == 0 else c), src=ins[t])
            cp.start()
            sends.append(cp)
    for t in range(n):
        for k in range(1, 4):
            chip = _rel_chip(x, y, k)
            copy(t, k, chip, c, sibling).wait_recv()
            fwd = copy(t, 3 + k, chip, c, sibling)
            fwd.start()
            sends.append(fwd)
    for t in range(n):
        copy(t, 0, (x, y), 1 - c, sibling).wait_recv()
        for k in range(1, 4):
            copy(t, 3 + k, _rel_chip(x, y, k), 1 - c, sibling).wait_recv()
    for cp in sends:
        cp.wait_send()
    for mine in started:
        mine.wait()


def _handshake(peers):
    barrier = pltpu.get_barrier_semaphore()
    for peer in peers:
        pl.semaphore_signal(barrier, inc=1, device_id=peer, device_id_type=MESH)
    pl.semaphore_wait(barrier, len(peers))


def _two_level_peers():
    x, y, c = _coords()
    return [(x, y, 1 - c)] + [(*_rel_chip(x, y, k), c) for k in range(1, 4)]


SEQ_ID_GATHER, SEQ_ID_SIBLING, SEQ_ID_CHIPS = 1, 2, 3


def _sequencer_call(body, peers, operands, out_types, sems, name, collective_id, after=()):
    n_in, n_out, n_after = len(operands), len(out_types), len(after)

    def launch(*refs):
        _handshake(peers())
        body(refs[:n_in], refs[n_in + n_after:n_in + n_after + n_out], *refs[n_in + n_after + n_out:])

    return pl.kernel(
        launch, name=name, out_type=out_types, mesh=plsc.ScalarSubcoreMesh(axis_name="seq", num_cores=1),
        scratch_types=sems, compiler_params=pltpu.CompilerParams(collective_id=collective_id))(*operands, *after)


def _all_gather_seq(shards, layouts, name, after=()):
    n = len(shards)
    blocks = [_block_of(kind, width) for kind, width, _ in layouts]
    return _sequencer_call(
        lambda ins, outs, *sems: _all_gather_body(ins, outs, *sems, blocks), _two_level_peers, shards,
        [jax.ShapeDtypeStruct(shape, sh.dtype) for sh, (_, _, shape) in zip(shards, layouts)],
        [pltpu.SemaphoreType.DMA((n, 7)), pltpu.SemaphoreType.DMA((n, 7)), pltpu.SemaphoreType.DMA((n,))],
        name, SEQ_ID_GATHER, after)


def _tie(small, after, name):
    def body(*refs):
        refs[-1][...] = refs[0][...]

    vmem = pl.BlockSpec(memory_space=pltpu.VMEM)
    return pl.pallas_call(
        body, name=name, in_specs=[vmem] + [pl.BlockSpec(memory_space=pl.ANY)] * len(after), out_specs=vmem,
        out_shape=jax.ShapeDtypeStruct(small.shape, small.dtype))(small, *after)


def _rs_to_sibling(grads, layouts, name, after=()):
    n = len(grads)
    blocks = [_block_of(kind, width) for kind, width, _ in layouts]

    def body(ins, outs, send_sems, recv_sems):
        x, y, c = _coords()
        sibling = (x, y, 1 - c)
        cps = []
        for t in range(n):
            for k in range(4):
                px, py = _rel_chip(x, y, k)
                cp = pltpu.make_async_remote_copy(
                    src_ref=blocks[t](ins[t], 4 * px + 2 * py + (1 - c)), dst_ref=outs[t].at[k],
                    send_sem=send_sems.at[t, k], recv_sem=recv_sems.at[t, k], device_id=sibling, device_id_type=MESH)
                cp.start()
                cps.append(cp)
        for cp in cps:
            cp.wait_recv()
        for cp in cps:
            cp.wait_send()

    def sibling_only():
        x, y, c = _coords()
        return [(x, y, 1 - c)]

    return _sequencer_call(
        body, sibling_only, grads,
        [jax.ShapeDtypeStruct((4,) + shape, g.dtype) for g, (_, _, shape) in zip(grads, layouts)],
        [pltpu.SemaphoreType.DMA((n, 4)), pltpu.SemaphoreType.DMA((n, 4))], name, SEQ_ID_SIBLING, after)


def _rs_chip_sum(grad, recv, layout, xyc, name):
    kind, width, shape = layout
    r, ccols = shape

    def src_index(k, xyc_ref):
        px = jnp.where(k % 2 == 1, 1 - xyc_ref[0], xyc_ref[0])
        py = jnp.where(k // 2 == 1, 1 - xyc_ref[1], xyc_ref[1])
        return 4 * px + 2 * py + xyc_ref[2]

    if kind == "col":
        g_spec = pl.BlockSpec((r, ccols), lambda k, s_: (0, src_index(k, s_)))
    elif kind == "row":
        g_spec = pl.BlockSpec((r, ccols), lambda k, s_: (src_index(k, s_), 0))
    else:
        g_spec = pl.BlockSpec((None, r, ccols), lambda k, s_: (src_index(k, s_), 0, 0))

    def body(xyc_ref, g_ref, r_ref, o_ref):
        o_ref[...] = (g_ref[...].astype(F32) + r_ref[...].astype(F32)).astype(o_ref.dtype)

    slot = pl.BlockSpec((None, r, ccols), lambda k, s_: (k, 0, 0))
    return pl.pallas_call(
        body, name=name,
        grid_spec=pltpu.PrefetchScalarGridSpec(num_scalar_prefetch=1, grid=(4,), in_specs=[g_spec, slot], out_specs=slot),
        out_shape=jax.ShapeDtypeStruct((4, r, ccols), grad.dtype), compiler_params=_cp(VMEM_BIG))(xyc, grad, recv)


def _rs_across_chips(parts, name):
    n = len(parts)

    def body(ins, outs, send_sems, recv_sems):
        x, y, c = _coords()
        cps = []
        for t in range(n):
            for k in range(1, 4):
                px, py = _rel_chip(x, y, k)
                cp = pltpu.make_async_remote_copy(
                    src_ref=ins[t].at[k], dst_ref=outs[t].at[k - 1], send_sem=send_sems.at[t, k - 1],
                    recv_sem=recv_sems.at[t, k - 1], device_id=(px, py, c), device_id_type=MESH)
                cp.start()
                cps.append(cp)
        for cp in cps:
            cp.wait_recv()
        for cp in cps:
            cp.wait_send()

    def other_chips():
        x, y, c = _coords()
        return [(*_rel_chip(x, y, k), c) for k in range(1, 4)]

    return _sequencer_call(
        body, other_chips, parts, [jax.ShapeDtypeStruct((3,) + p.shape[1:], p.dtype) for p in parts],
        [pltpu.SemaphoreType.DMA((n, 3)), pltpu.SemaphoreType.DMA((n, 3))], name, SEQ_ID_CHIPS)


def _adamw_math(w, g, m, v):
    m = ADAM_B1 * m + (1.0 - ADAM_B1) * g
    v = ADAM_B2 * v + (1.0 - ADAM_B2) * jnp.square(g)
    m_hat = m / (1.0 - ADAM_B1 ** ADAM_STEP)
    v_hat = v / (1.0 - ADAM_B2 ** ADAM_STEP)
    delta = -ADAM_LR * (m_hat / (jnp.sqrt(v_hat) + ADAM_EPS) + ADAM_WD * w)
    return delta, m, v


def _row_tile(rows, cap):
    best = None
    for cand in range(8, min(rows, cap) + 1, 8):
        if rows % cand == 0:
            best = cand
    assert best is not None, rows
    return best


def _adamw(w, m, v, parts, name, layer=None, prev=None, tr=256):
    r, ccols = w.shape[-2:]
    npart = len(parts)
    if r % 8 == 0:
        tr, tc = _row_tile(r, tr), ccols
        steps, at = r // tr, (lambda i: (i, 0))
    else:
        tr, tc = r, 256
        assert ccols % tc == 0
        steps, at = ccols // tc, (lambda i: (0, i))

    def spec(lead):
        if lead is None:
            return pl.BlockSpec((tr, tc), at)
        return pl.BlockSpec((None, tr, tc), lambda i: (lead,) + at(i))

    wspec = lambda: spec(layer)
    pspec = spec

    def body(*refs):
        w_ref, m_ref, v_ref = refs[:3]
        p_refs = refs[3:3 + npart]
        outs = refs[len(refs) - 4:]
        g = p_refs[0][...].astype(F32)
        for p_ref in p_refs[1:]:
            g = g + p_ref[...].astype(F32)
        delta, mn, vn = _adamw_math(w_ref[...], g, m_ref[...], v_ref[...])
        outs[0][...] = g
        outs[1][...] = delta
        outs[2][...] = mn
        outs[3][...] = vn

    operands = [w, m, v] + [p for p, _ in parts]
    in_specs = [wspec(), wspec(), wspec()] + [pspec(lead) for _, lead in parts]
    aliases = {}
    if prev is not None:
        for i, p in enumerate(prev):
            aliases[len(operands)] = i
            operands.append(p)
            in_specs.append(pl.BlockSpec(memory_space=pl.ANY))
    return pl.pallas_call(
        body, name=name, grid=(steps,), in_specs=in_specs, out_specs=[wspec()] * 4,
        out_shape=[jax.ShapeDtypeStruct(w.shape, F32)] * 4, input_output_aliases=aliases)(*operands)


def _sum8(buf, name):
    _, r, ccols = buf.shape

    def body(b_ref, o_ref):
        acc = b_ref[0]
        for j in range(1, N_DEV):
            acc = acc + b_ref[j]
        o_ref[...] = acc

    tr = _row_tile(r, 256)
    return pl.pallas_call(
        body, name=name, grid=(r // tr,), in_specs=[pl.BlockSpec((N_DEV, tr, ccols), lambda i: (0, i, 0))],
        out_specs=pl.BlockSpec((tr, ccols), lambda i: (i, 0)), out_shape=jax.ShapeDtypeStruct((r, ccols), F32))(buf)


def _pack(arrays):
    pieces, layout, off = [], [], 0
    for a in arrays:
        n = a.size
        padded = -(-n // 1024) * 1024
        flat = a.reshape(-1).astype(F32)
        if padded != n:
            flat = jnp.pad(flat, (0, padded - n))
        pieces.append(flat.reshape(padded // 128, 128))
        layout.append((off, n, a.shape))
        off += padded // 128
    return jnp.concatenate(pieces, axis=0), layout


def _unpack(packed, layout):
    out = []
    for off, n, shape in layout:
        rows = -(-n // 1024) * 8
        out.append(packed[off:off + rows].reshape(-1)[:n].reshape(shape))
    return out


def kernel(x, mem, norm_mix, norm_ffn, mem_norm, w_kv, w_out, w_ffn1, w_ffn2, a_in, a_ln_g, a_ln_b, a_ws, a_bs, b_in, b_conv_w, b_conv_b, b_dt_bias, b_a_log, b_d, b_gnorm, final_norm, loss_target, m_norm_mix, m_norm_ffn, m_mem_norm, m_w_kv, m_w_out, m_w_ffn1, m_w_ffn2, m_a_in, m_a_ln_g, m_a_ln_b, m_a_ws, m_a_bs, m_b_in, m_b_conv_w, m_b_conv_b, m_b_dt_bias, m_b_a_log, m_b_d, m_b_gnorm, m_final_norm, v_norm_mix, v_norm_ffn, v_mem_norm, v_w_kv, v_w_out, v_w_ffn1, v_w_ffn2, v_a_in, v_a_ln_g, v_a_ln_b, v_a_ws, v_a_bs, v_b_in, v_b_conv_w, v_b_conv_b, v_b_dt_bias, v_b_a_log, v_b_d, v_b_gnorm, v_final_norm):
    s = x.shape[1]
    xs = x.reshape(s, D_MODEL)
    mems = mem.reshape(N_MEM, D_MODEL)
    target = loss_target.reshape(s, D_MODEL)
    ax, ay, ac = lax.axis_index("x"), lax.axis_index("y"), lax.axis_index("c")
    me = 4 * ax + 2 * ay + ac
    xyc = jnp.stack([ax, ay, ac]).astype(jnp.int32)

    b_cols = b_in.shape[2]
    act = lambda a: a.astype(_ACT)
    lay_f1, lay_f2 = ("col", 512, (1, D_MODEL, D_FF)), ("row", 512, (1, D_FF, D_MODEL))
    lay_out, lay_kv = ("row", 384, (1, 3 * D_MODEL, D_MODEL)), ("col", 256, (1, D_MODEL, 2 * X_WIDTH))
    small_w_pack = _pack([b_conv_w[0], b_conv_b[0], b_gnorm[0]])[0]
    WA, wkv0 = _all_gather_seq([act(a_in), act(w_kv[0:1])], [("col", 640, (1, D_MODEL, 5 * D_MODEL)), lay_kv], "ag_proj_a")
    (wo0,) = _all_gather_seq([act(w_out[0:1])], [lay_out], "ag_out0")
    w1_0, w2_0 = _all_gather_seq([act(w_ffn1[0:1]), act(w_ffn2[0:1])], [lay_f1, lay_f2], "ag_ffn0")
    a0 = _rms_fwd(xs, norm_mix[0].reshape(1, -1), "mix_norm0")
    tr_b = lambda a: jnp.swapaxes(a, 1, 2)
    wbt_blk, small_w = _all_gather_seq(
        [act(tr_b(b_in)[0]), small_w_pack],
        [("blk", 0, (N_DEV, b_cols, D_MODEL)), ("blk", 0, (N_DEV, 32, 128))], "ag_proj_b", after=[a0])
    wo1, wkv1 = _all_gather_seq([act(w_out[1:2]), act(w_kv[1:2])], [lay_out, lay_kv], "ag_out1", after=[a0])
    w1_1, w2_1 = _all_gather_seq([act(w_ffn1[1:2]), act(w_ffn2[1:2])], [lay_f1, lay_f2], "ag_ffn1", after=[a0])
    W1, W2, WO, WKV = [w1_0, w1_1], [w2_0, w2_1], [wo0, wo1], [wkv0, wkv1]
    dt0 = D_INNER + CONV_DIM

    row = lambda a: a.reshape(1, -1)
    nmix = [row(norm_mix[0]), row(norm_mix[1])]
    nffn = [row(norm_ffn[0]), row(norm_ffn[1])]
    nmem = [row(mem_norm[0]), row(mem_norm[1])]
    fin = row(final_norm)
    lng, lnb = a_ln_g.reshape(1, D_INNER), a_ln_b.reshape(1, D_INNER)
    ws = a_ws[0]
    bs3 = a_bs[0].reshape(A_GROUPS, CHUNK, 1)
    pad_h = lambda a: jnp.pad(a.reshape(-1), (0, HPAD - SSM_HEADS))
    bias_row = pad_h(b_dt_bias).reshape(1, HPAD)
    alog_row = pad_h(b_a_log).reshape(1, HPAD)
    dfull = jnp.repeat(b_d.reshape(-1), SSM_P).reshape(1, D_INNER)

    kvs, mns = [None, None], [None, None]

    def mem_kv(i, after=None):
        gain = nmem[i] if after is None else _tie(nmem[i], after, f"tie_mem{i}")
        mns[i] = _rms_fwd(mems, gain, f"mem_norm{i}")
        kvs[i] = _mm(mns[i], WKV[i], m=N_MEM, n=2 * X_WIDTH, k=D_MODEL, b_at=(0, 0, 0), out_dtype=_ACT, name=f"kv{i}")

    def ffn_fwd(h, i):
        f = _rms_fwd(h, nffn[i], f"ffn_norm{i}")
        p = _mm(f, W1[i], m=s, n=D_FF, k=D_MODEL, b_at=(0, 0, 0), out_dtype=_ACT, name=f"ffn_up{i}")
        hn = _mm(p, W2[i], m=s, n=D_MODEL, k=D_FF, b_at=(0, 0, 0), a_pro="relu2", add=h, name=f"ffn_down{i}")
        return f, p, hn

    def out_proj(h, cat, i):
        return _mm(cat, WO[i], m=s, n=D_MODEL, k=3 * D_MODEL, b_at=(0, 0, 0), add=h, name=f"out_proj{i}")

    proj_a = _mm(a0, WA, m=s, n=5 * D_MODEL, k=D_MODEL, b_at=(0, 0, 0), name="proj_a")
    mem_kv(0)
    cat_a = _gmlp_fwd(proj_a, lng, lnb, ws, bs3, "gmlp_fwd")
    cat_a = _attn_fwd(proj_a, 4, kvs[0], cat_a, "attn_fwd0")
    h1 = out_proj(xs, cat_a, 0)
    f0, p0, h2 = ffn_fwd(h1, 0)

    wbt_blk, small_w, _ = lax.optimization_barrier((wbt_blk, small_w, p0))
    wbt_full = wbt_blk.reshape(N_DEV * b_cols, D_MODEL)
    WBT = jnp.concatenate([wbt_full[:dt0], wbt_full[dt0 + SSM_HEADS:]], axis=0)
    WBDT = jnp.pad(wbt_full[dt0:dt0 + SSM_HEADS], ((0, HPAD - SSM_HEADS), (0, 0)))
    cw_sh, cb_sh, gn_sh = 4 * 384, 384, 256
    sw = small_w.reshape(N_DEV, 32 * 128)
    conv_w = jnp.transpose(sw[:, :cw_sh].reshape(N_DEV, CONV_K, 384), (1, 0, 2)).reshape(CONV_K, CONV_DIM)
    conv_b = sw[:, 2048:2048 + cb_sh].reshape(1, CONV_DIM)
    gnorm = sw[:, 3072:3072 + gn_sh].reshape(1, D_INNER)

    a1 = _rms_fwd(h2, nmix[1], "mix_norm1")
    proj_b = _mm(a1, WBT, m=s, n=6 * D_MODEL, k=D_MODEL, tb=True, name="proj_b")
    dt_raw = _mm(a1, WBDT, m=s, n=HPAD, k=D_MODEL, tb=True, name="proj_dt")
    xbc = _conv_fwd(proj_b, conv_w, conv_b, "conv_fwd")
    y_ssd, states = _ssd_fwd(xbc, dt_raw, bias_row, alog_row, dfull, "ssd_fwd")
    cat_b = _gate_fwd(y_ssd, proj_b, gnorm, "gate_fwd")
    mem_kv(1, after=[cat_b])
    cat_b = _attn_fwd(proj_b, 5, kvs[1], cat_b, "attn_fwd1")
    h3 = out_proj(h2, cat_b, 1)
    f1, p1, h4 = ffn_fwd(h3, 1)

    loss_part, dh, dh_act, d_fin = _loss_head(h4, fin, target, "loss_head")

    g_f1, g_f2, g_out, g_kv = [None, None], [None, None], [None, None], [None, None]
    d_nffn, d_nmix, d_nmem = [None, None], [None, None], [None, None]

    def ffn_bwd(dh, dh_act, h_in, f, p, i, after=()):
        dp = _mm(dh_act, W2[i], m=s, n=D_FF, k=D_MODEL, tb=True, b_at=(0, 0, 0), epi_p=p, out_dtype=_ACT, name=f"ffn_down_dx{i}")
        g_f2[i] = _mm(p, dh_act, m=D_FF, n=D_MODEL, k=s, ta=True, a_pro="relu2", out_dtype=_ACT, name=f"ffn_down_dw{i}")
        g_f1[i] = _mm(f, dp, m=D_MODEL, n=D_FF, k=s, ta=True, out_dtype=_ACT, name=f"ffn_up_dw{i}")
        df = _mm(dp, W1[i], m=s, n=D_MODEL, k=D_FF, tb=True, b_at=(0, 0, 0), after=after, name=f"ffn_up_dx{i}")
        dh_in, dh_in_act, d_nffn[i] = _rms_bwd(h_in, nffn[i], df, dh, f"ffn_norm_bwd{i}")
        return dh_in, dh_in_act

    def out_bwd(dh_act, cat, i):
        dcat = _mm(dh_act, WO[i], m=s, n=3 * D_MODEL, k=D_MODEL, tb=True, b_at=(0, 0, 0), out_dtype=_ACT, name=f"out_dx{i}")
        g_out[i] = _mm(cat, dh_act, m=3 * D_MODEL, n=D_MODEL, k=s, ta=True, out_dtype=_ACT, name=f"out_dw{i}")
        return dcat

    def mem_bwd(dkv, i):
        g_kv[i] = _mm(mns[i], dkv, m=D_MODEL, n=2 * X_WIDTH, k=N_MEM, ta=True, out_dtype=_ACT, name=f"kv_dw{i}")
        dmn = _mm(dkv, WKV[i], m=N_MEM, n=D_MODEL, k=2 * X_WIDTH, tb=True, b_at=(0, 0, 0), name=f"kv_dx{i}")
        _, _, d_nmem[i] = _rms_bwd(mems, nmem[i], dmn, None, f"mem_norm_bwd{i}")

    lay_g = {"f1": ("col", 512, (D_MODEL, 512)), "f2": ("row", 512, (512, D_MODEL)), "out": ("row", 384, (384, D_MODEL)),
             "kv": ("col", 256, (D_MODEL, 256)), "a": ("col", 640, (D_MODEL, 640)), "b": ("blk", 0, (b_cols, D_MODEL))}
    reduced = {}

    def reduce_scatter(group, tag, after=()):
        grads3, lays3 = [], []
        for fam, _, g in group:
            kind, width, shape = lay_g[fam]
            grads3.append(g if kind == "blk" else g.reshape((1,) + g.shape))
            lays3.append((kind, width, shape if kind == "blk" else (1,) + shape))
        recv1 = _rs_to_sibling(grads3, lays3, f"rs_sibling_{tag}", after)
        parts = [_rs_chip_sum(g, recv1[t].reshape((4,) + lay_g[fam][2]), lay_g[fam], xyc, f"rs_chip_sum_{fam}{i}")
                 for t, (fam, i, g) in enumerate(group)]
        recv2 = _rs_across_chips(parts, f"rs_chips_{tag}")
        for (fam, i, _), p, r2 in zip(group, parts, recv2):
            reduced[fam, i] = (p, r2)
        return parts, recv2

    dh3, dh3_act = ffn_bwd(dh, dh_act, h3, f1, p1, 1)
    dcat_b = out_bwd(dh3_act, cat_b, 1)
    sums, got_ffn1 = reduce_scatter([("f1", 1, g_f1[1]), ("f2", 1, g_f2[1]), ("out", 1, g_out[1])], "ffn1")
    dy_ssd, dproj_b, d_gnorm = _gate_bwd(y_ssd, proj_b, gnorm, dcat_b, "gate_bwd")
    dproj_b, dkv_b = _attn_bwd(proj_b, 5, kvs[1], dcat_b, dproj_b, "attn_bwd1")
    mem_bwd(dkv_b, 1)
    dxbc, ddt_raw, d_alog, d_dskip, d_dtbias = _ssd_bwd(
        xbc, dt_raw, _tie(bias_row, sums, "tie_ffn1"), alog_row, dfull, dy_ssd, states, "ssd_bwd")
    dproj_b, d_convw, d_convb = _conv_bwd(proj_b, conv_w, _tie(conv_b, got_ffn1, "tie_got_ffn1"), dxbc, dproj_b, "conv_bwd")
    gb = _mm(dproj_b, a1, m=6 * D_MODEL, n=D_MODEL, k=s, ta=True, out_dtype=_ACT, name="proj_b_dw")
    gb_dt = _mm(ddt_raw, a1, m=HPAD, n=D_MODEL, k=s, ta=True, out_dtype=_ACT, name="proj_b_dw_dt")
    gb_full = jnp.concatenate([gb[:dt0], gb_dt[:SSM_HEADS], gb[dt0:]], axis=0)
    gb_blk = gb_full.reshape(N_DEV, b_cols, D_MODEL)
    sums, got_mix1 = reduce_scatter([("kv", 1, g_kv[1]), ("b", 0, gb_blk)], "mix1")
    da1 = _mm(dproj_b, WBT, m=s, n=D_MODEL, k=6 * D_MODEL, name="proj_b_dx")
    da1 = _mm(ddt_raw, WBDT, m=s, n=D_MODEL, k=HPAD, add=da1, name="proj_b_dx_dt")
    dh2, dh2_act, d_nmix[1] = _rms_bwd(h2, _tie(nmix[1], sums, "tie_mix1"), da1, dh3, "mix_norm_bwd1")

    dh1, dh1_act = ffn_bwd(dh2, dh2_act, h1, f0, p0, 0, after=got_ffn1)
    dcat_a = out_bwd(dh1_act, cat_a, 0)
    sums, _ = reduce_scatter([("f1", 0, g_f1[0]), ("f2", 0, g_f2[0]), ("out", 0, g_out[0])], "ffn0")
    dproj_a, d_ws, d_bs3, d_lng, d_lnb = _gmlp_bwd(
        proj_a, dcat_a, _tie(lng, list(got_mix1) + list(sums), "tie_got_mix1"), lnb, ws, bs3, "gmlp_bwd")
    dproj_a, dkv_a = _attn_bwd(proj_a, 4, kvs[0], dcat_a, dproj_a, "attn_bwd0")
    da0 = _mm(dproj_a, WA, m=s, n=D_MODEL, k=5 * D_MODEL, tb=True, b_at=(0, 0, 0), name="proj_a_dx")
    grad_x, _, d_nmix[0] = _rms_bwd(xs, nmix[0], da0, dh1, "mix_norm_bwd0")
    mem_bwd(dkv_a, 0)

    rep_grads = [jnp.concatenate(d_nmix, axis=0), jnp.concatenate(d_nffn, axis=0), jnp.concatenate(d_nmem, axis=0),
                 d_lng, d_lnb, d_ws.reshape(1, A_GROUPS, CHUNK, CHUNK), d_bs3.reshape(1, A_GROUPS, CHUNK),
                 d_dtbias[:, :SSM_HEADS], d_alog[:, :SSM_HEADS], d_dskip[:, :SSM_HEADS], d_fin.reshape(D_MODEL)]
    rep_w = [norm_mix, norm_ffn, mem_norm, a_ln_g, a_ln_b, a_ws, a_bs, b_dt_bias, b_a_log, b_d, final_norm]
    rep_grads = [g.reshape(w.shape) for g, w in zip(rep_grads, rep_w)]
    sh_grads = [d_convw, d_convb, d_gnorm]
    g_pack, g_layout = _pack(rep_grads + sh_grads + [loss_part])
    n_rep = len(rep_grads)
    (g_all,) = _all_gather_seq([g_pack], [("blk", 0, (N_DEV,) + g_pack.shape)], "ag_small_grads")

    ga = _mm(a0, dproj_a, m=D_MODEL, n=5 * D_MODEL, k=s, ta=True, out_dtype=_ACT, after=[g_pack], name="proj_a_dw")
    reduce_scatter([("kv", 0, g_kv[0]), ("a", 0, ga)], "mix0", after=[g_all])

    def big_update(w, m, v, fam, nlayer):
        res = None
        for i in range(nlayer):
            part, recv2 = reduced[fam, i]
            plist = [(part, 0), (recv2, 0), (recv2, 1), (recv2, 2)]
            res = _adamw(w, m, v, plist, f"adamw_{fam}{i}", layer=i, prev=res)
        return res

    r_f1 = big_update(w_ffn1, m_w_ffn1, v_w_ffn1, "f1", 2)
    r_f2 = big_update(w_ffn2, m_w_ffn2, v_w_ffn2, "f2", 2)
    r_out = big_update(w_out, m_w_out, v_w_out, "out", 2)
    r_kv = big_update(w_kv, m_w_kv, v_w_kv, "kv", 2)
    r_a = big_update(a_in, m_a_in, v_a_in, "a", 1)
    r_b = [tr_b(o) for o in big_update(tr_b(b_in), tr_b(m_b_in), tr_b(v_b_in), "b", 1)]

    rep_names = ["norm_mix", "norm_ffn", "mem_norm", "a_ln_g", "a_ln_b", "a_ws", "a_bs", "b_dt_bias", "b_a_log", "b_d",
                 "final_norm"]
    rep_m = [m_norm_mix, m_norm_ffn, m_mem_norm, m_a_ln_g, m_a_ln_b, m_a_ws, m_a_bs, m_b_dt_bias, m_b_a_log, m_b_d, m_final_norm]
    rep_v = [v_norm_mix, v_norm_ffn, v_mem_norm, v_a_ln_g, v_a_ln_b, v_a_ws, v_a_bs, v_b_dt_bias, v_b_a_log, v_b_d, v_final_norm]
    g_small = _sum8(g_all, "sum_small_grads")
    g_list = _unpack(g_small, g_layout)
    loss = g_list[-1][0, 0]
    wp, w_layout = _pack(rep_w)
    mp, _ = _pack(rep_m)
    vp, _ = _pack(rep_v)
    gp, _ = _pack(g_list[:n_rep])
    rep_res = [_unpack(o, w_layout) for o in _adamw(wp, mp, vp, [(gp, None)], "adamw_replicated", tr=88)]

    gcw = lax.dynamic_slice_in_dim(g_list[n_rep], me * 384, 384, axis=1).reshape(1, CONV_K, 384)
    gcb = lax.dynamic_slice_in_dim(g_list[n_rep + 1], me * 384, 384, axis=1)
    ggn = lax.dynamic_slice_in_dim(g_list[n_rep + 2], me * 256, 256, axis=1)
    sh_w = [b_conv_w, b_conv_b, b_gnorm]
    sh_m = [m_b_conv_w, m_b_conv_b, m_b_gnorm]
    sh_v = [v_b_conv_w, v_b_conv_b, v_b_gnorm]
    swp, sw_layout = _pack(sh_w)
    smp, _ = _pack(sh_m)
    svp, _ = _pack(sh_v)
    sgp, _ = _pack([gcw, gcb, ggn])
    sh_res = [_unpack(o, sw_layout) for o in _adamw(swp, smp, svp, [(sgp, None)], "adamw_sharded_small", tr=8)]

    names = ["norm_mix", "norm_ffn", "mem_norm", "w_kv", "w_out", "w_ffn1", "w_ffn2", "a_in", "a_ln_g", "a_ln_b", "a_ws",
             "a_bs", "b_in", "b_conv_w", "b_conv_b", "b_dt_bias", "b_a_log", "b_d", "b_gnorm", "final_norm"]
    big = {"w_kv": r_kv, "w_out": r_out, "w_ffn1": r_f1, "w_ffn2": r_f2, "a_in": r_a, "b_in": r_b}
    sh_names = ["b_conv_w", "b_conv_b", "b_gnorm"]
    outs = [loss, grad_x.reshape(x.shape)]
    for kind in range(4):
        for nm in names:
            if nm in big:
                outs.append(big[nm][kind])
            elif nm in sh_names:
                outs.append(sh_res[kind][sh_names.index(nm)])
            else:
                outs.append(rep_res[kind][rep_names.index(nm)])
    return tuple(outs)
```

```python
import functools
import math

import jax
import jax.numpy as jnp
from jax import lax
from jax.experimental import pallas as pl
from jax.experimental.pallas import tpu as pltpu
from jax.experimental.pallas import tpu_sc as plsc

F32 = jnp.float32
_MXU = jnp.bfloat16
_ACT = jnp.bfloat16
_HI = lax.Precision.HIGHEST

D_MODEL = 1024
CHUNK = 128
N_MEM = 256
D_INNER = 2048
A_GROUPS = 8
A_GW = D_INNER // A_GROUPS
SSM_HEADS = 32
SSM_P = 64
SSM_GROUPS = 4
SSM_GW = D_INNER // SSM_GROUPS
SSM_N = 128
CONV_K = 4
CONV_DIM = 3072
X_HEADS = 4
X_HD = 256
X_WIDTH = 1024
D_FF = 4096
EPS = 1e-6
HPAD = 128
N_DEV = 8

ADAM_LR = 0.001
ADAM_B1 = 0.9
ADAM_B2 = 0.999
ADAM_EPS = 1e-08
ADAM_WD = 0.01
ADAM_STEP = 10

VMEM_BIG = 56 * 1024 * 1024
MESH = pl.DeviceIdType.MESH


def _cp(vmem=None):
    if vmem is None:
        return pltpu.CompilerParams()
    return pltpu.CompilerParams(vmem_limit_bytes=vmem)


def _dot(a, b, dims=((1,), (0,))):
    return lax.dot_general(a.astype(_MXU), b.astype(_MXU), (dims, ((), ())), preferred_element_type=F32)


def _dot_nt(a, b):
    return _dot(a, b, ((1,), (1,)))


def _dot_tn(a, b):
    return _dot(a, b, ((0,), (0,)))


def _dot_hi(a, b, dims=((1,), (0,))):
    return lax.dot_general(a.astype(F32), b.astype(F32), (dims, ((), ())), precision=_HI, preferred_element_type=F32)


def _split3(x):
    x1 = x.astype(jnp.bfloat16)
    r = x - x1.astype(F32)
    x2 = r.astype(jnp.bfloat16)
    x3 = (r - x2.astype(F32)).astype(jnp.bfloat16)
    return x1, x2, x3


def _dot_sel(x, sel, dims=((1,), (0,))):
    sel = sel.astype(jnp.bfloat16)
    parts = [lax.dot_general(t, sel, (dims, ((), ())), preferred_element_type=F32) for t in _split3(x)]
    return (parts[0] + parts[1]) + parts[2]


def _sel_dot(sel, x, dims=((1,), (0,))):
    sel = sel.astype(jnp.bfloat16)
    parts = [lax.dot_general(sel, t, (dims, ((), ())), preferred_element_type=F32) for t in _split3(x)]
    return (parts[0] + parts[1]) + parts[2]


def _sigmoid(x):
    return 1.0 / (1.0 + jnp.exp(-x))


def _gelu(x):
    return 0.5 * x * (1.0 + lax.erf(x * (1.0 / math.sqrt(2.0))))


def _gelu_grad(x):
    return 0.5 * (1.0 + lax.erf(x * (1.0 / math.sqrt(2.0)))) + x * jnp.exp(-0.5 * x * x) * (1.0 / math.sqrt(2.0 * math.pi))


def _softplus(x):
    return jnp.maximum(x, 0.0) + jnp.log1p(jnp.exp(-jnp.abs(x)))


def _iota(shape, dim):
    return lax.broadcasted_iota(jnp.int32, shape, dim)


MM_VMEM_BUDGET = 40 * 1024 * 1024
HBM_BYTES_PER_S = 2.5e12
GRID_STEP_S = 0.35e-6
VMEM_ACC_BYTES_PER_S = 6e12


def _divisors(dim, unit):
    out = [d for d in range(unit, min(dim, 2048) + 1, unit) if dim % d == 0]
    return out if out else [dim]


def _mm_tiles(m, n, k, sa, sb, s_mn, a_pro, offsets):
    best = None
    (a_r0, a_c0, ta), (b_r0, b_c0, tb), (o_r0, o_c0) = offsets
    for tm in _divisors(m, 128):
        for tn in _divisors(n, 128):
            for tk in [k // d for d in (1, 2, 3, 4, 6, 8) if k % d == 0 and (k // d) % 128 == 0]:
                a_t = (tk, tm) if ta else (tm, tk)
                b_t = (tn, tk) if tb else (tk, tn)
                if a_r0 % a_t[0] or a_c0 % a_t[1] or b_r0 % b_t[0] or b_c0 % b_t[1] or o_r0 % tm or o_c0 % tn:
                    continue
                nk = k // tk
                vmem = 2 * (tm * tk * sa + tk * tn * sb + tm * tn * s_mn) + tm * tn * 4 * (2 if nk > 1 else 1)
                if a_pro or sa == 4:
                    vmem += tm * tk * 6
                if sb == 4:
                    vmem += tk * tn * 2
                if vmem > MM_VMEM_BUDGET:
                    continue
                gi, gj = m // tm, n // tn
                for j_inner in (True, False):
                    if nk > 1:
                        traffic = gj * m * k * sa + gi * k * n * sb
                    elif j_inner:
                        traffic = m * k * sa + gi * k * n * sb
                    else:
                        traffic = gj * m * k * sa + k * n * sb
                    traffic += m * n * s_mn + (tm * tk * sa + tk * tn * sb)
                    cost = traffic / HBM_BYTES_PER_S + gi * gj * nk * GRID_STEP_S
                    if nk > 1:
                        cost += m * n * 8 * nk / VMEM_ACC_BYTES_PER_S
                    if best is None or cost < best[0]:
                        best = (cost, tm, tn, tk, j_inner)
    assert best is not None, (m, n, k)
    return best[1:]


def _mm(a, b, *, m, n, k, name, ta=False, tb=False, a_at=(None, 0, 0), b_at=(None, 0, 0),
        out_dtype=F32, add=None, epi_p=None, epi_at=(None, 0, 0), out=None, out_at=(None, 0, 0),
        out_full=None, a_pro=None, after=()):
    s_mn =jnp.dtype(out.dtype if out is not None else out_dtype).itemsize
    s_mn += add.dtype.itemsize if add is not None else 0
    s_mn += epi_p.dtype.itemsize if epi_p is not None else 0
    tm, tn, tk, j_inner = _mm_tiles(m, n, k, a.dtype.itemsize, b.dtype.itemsize, s_mn, a_pro is not None,
                                    ((a_at[1], a_at[2], ta), (b_at[1], b_at[2], tb), (out_at[1], out_at[2])))
    nk = k // tk

    def spec(at, tr, tc, rsel, csel):
        lead, r0, c0 = at
        assert r0 % tr == 0 and c0 % tc == 0, (name, at, tr, tc)
        rb, cb = r0 // tr, c0 // tc
        if lead is None:
            return pl.BlockSpec((tr, tc), lambda g0, g1, kk: (rb + rsel(g0, g1, kk), cb + csel(g0, g1, kk)))
        return pl.BlockSpec((None, tr, tc), lambda g0, g1, kk: (lead, rb + rsel(g0, g1, kk), cb + csel(g0, g1, kk)))

    gi = (lambda g0, g1, kk: g0) if j_inner else (lambda g0, g1, kk: g1)
    gj = (lambda g0, g1, kk: g1) if j_inner else (lambda g0, g1, kk: g0)
    gk = lambda g0, g1, kk: kk
    a_spec = spec(a_at, tk, tm, gk, gi) if ta else spec(a_at, tm, tk, gi, gk)
    b_spec = spec(b_at, tn, tk, gj, gk) if tb else spec(b_at, tk, tn, gk, gj)
    dims = ((0,), (0,)) if ta else (((1,), (1,)) if tb else ((1,), (0,)))
    assert not (ta and tb)

    operands, in_specs = [a, b], [a_spec, b_spec]
    if add is not None:
        operands.append(add)
        in_specs.append(spec((None, 0, 0), tm, tn, gi, gj))
    if epi_p is not None:
        operands.append(epi_p)
        in_specs.append(spec(epi_at, tm, tn, gi, gj))
    aliases = {}
    if out is not None:
        aliases = {len(operands): 0}
        operands.append(out)
        in_specs.append(pl.BlockSpec(memory_space=pl.ANY))
        out_struct = jax.ShapeDtypeStruct(out.shape, out.dtype)
        out_dtype = out.dtype
    else:
        out_struct = jax.ShapeDtypeStruct(out_full if out_full is not None else (m, n), out_dtype)
    has_add, has_epi = add is not None, epi_p is not None
    n_skip = (1 if out is not None else 0) + len(after)
    operands += list(after)
    in_specs += [pl.BlockSpec(memory_space=pl.ANY)] * len(after)

    def body(*refs):
        a_ref, b_ref = refs[0], refs[1]
        pos = 2
        add_ref = epi_ref = None
        if has_add:
            add_ref = refs[pos]
            pos += 1
        if has_epi:
            epi_ref = refs[pos]
            pos += 1
        pos += n_skip
        o_ref = refs[pos]

        def finish(r):
            if has_add:
                r = r + add_ref[...].astype(F32)
            if has_epi:
                r = r * (2.0 * jnp.maximum(epi_ref[...].astype(F32), 0.0))
            o_ref[...] = r.astype(o_ref.dtype)

        av = a_ref[...]
        if a_pro == "relu2":
            av = jnp.square(jnp.maximum(av.astype(F32), 0.0))
        part = _dot(av, b_ref[...], dims)
        if nk == 1:
            finish(part)
        else:
            acc_ref = refs[pos + 1]
            kk = pl.program_id(2)

            @pl.when(kk == 0)
            def _():
                acc_ref[...] = part

            @pl.when(kk > 0)
            def _():
                acc_ref[...] += part

            @pl.when(kk == nk - 1)
            def _():
                finish(acc_ref[...])

    grid = (m // tm, n // tn, nk) if j_inner else (n // tn, m // tm, nk)
    return pl.pallas_call(
        body, name=name, grid=grid, in_specs=in_specs,
        out_specs=spec(out_at, tm, tn, gi, gj), out_shape=out_struct,
        scratch_shapes=[pltpu.VMEM((tm, tn), F32)] if nk > 1 else [], input_output_aliases=aliases,
        compiler_params=_cp(VMEM_BIG))(*operands)


def _rms_fwd(x, g, name, tm=256):
    s, d = x.shape
    tm = min(tm, s)

    def body(x_ref, g_ref, o_ref):
        xv = x_ref[...]
        r = lax.rsqrt(jnp.mean(xv * xv, axis=-1, keepdims=True) + EPS)
        o_ref[...] = (xv * r * g_ref[...]).astype(o_ref.dtype)

    return pl.pallas_call(
        body, name=name, grid=(s // tm,),
        in_specs=[pl.BlockSpec((tm, d), lambda i: (i, 0)), pl.BlockSpec((1, d), lambda i: (0, 0))],
        out_specs=pl.BlockSpec((tm, d), lambda i: (i, 0)),
        out_shape=jax.ShapeDtypeStruct((s, d), _ACT))(x, g)


def _rms_bwd(x, g, dy, dres, name, tm=256):
    s, d = x.shape
    tm = min(tm, s)
    has_res = dres is not None

    def body(*refs):
        if has_res:
            x_ref, g_ref, dy_ref, dres_ref, dx_ref, dxa_ref, dg_ref = refs
        else:
            x_ref, g_ref, dy_ref, dx_ref, dxa_ref, dg_ref = refs

        @pl.when(pl.program_id(0) == 0)
        def _():
            dg_ref[...] = jnp.zeros_like(dg_ref)

        xv = x_ref[...]
        dyv = dy_ref[...].astype(F32)
        r = lax.rsqrt(jnp.mean(xv * xv, axis=-1, keepdims=True) + EPS)
        xh = xv * r
        dyg = dyv * g_ref[...]
        dx = r * (dyg - xh * jnp.mean(dyg * xh, axis=-1, keepdims=True))
        if has_res:
            dx = dx + dres_ref[...]
        dx_ref[...] = dx
        dxa_ref[...] = dx.astype(dxa_ref.dtype)
        dg_ref[...] += jnp.sum(dyv * xh, axis=0, keepdims=True)

    row = pl.BlockSpec((tm, d), lambda i: (i, 0))
    vec = pl.BlockSpec((1, d), lambda i: (0, 0))
    in_specs = [row, vec, row] + ([row] if has_res else [])
    operands = [x, g, dy] + ([dres] if has_res else [])
    return pl.pallas_call(
        body, name=name, grid=(s // tm,), in_specs=in_specs, out_specs=[row, row, vec],
        out_shape=[jax.ShapeDtypeStruct((s, d), F32), jax.ShapeDtypeStruct((s, d), _ACT),
                   jax.ShapeDtypeStruct((1, d), F32)])(*operands)


def _loss_head(h, g, target, name, tm=256):
    s, d = h.shape
    tm = min(tm, s)

    def body(h_ref, g_ref, t_ref, loss_ref, dh_ref, dha_ref, dg_ref):
        @pl.when(pl.program_id(0) == 0)
        def _():
            dg_ref[...] = jnp.zeros_like(dg_ref)
            loss_ref[...] = jnp.zeros_like(loss_ref)

        xv = h_ref[...]
        r = lax.rsqrt(jnp.mean(xv * xv, axis=-1, keepdims=True) + EPS)
        xh = xv * r
        err = xh * g_ref[...] - t_ref[...]
        loss_ref[...] += jnp.full(loss_ref.shape, 0.5 * jnp.sum(jnp.mean(err * err, axis=-1, keepdims=True)), F32)
        dyv = err * (1.0 / d)
        dyg = dyv * g_ref[...]
        dh = r * (dyg - xh * jnp.mean(dyg * xh, axis=-1, keepdims=True))
        dh_ref[...] = dh
        dha_ref[...] = dh.astype(dha_ref.dtype)
        dg_ref[...] += jnp.sum(dyv * xh, axis=0, keepdims=True)

    row = pl.BlockSpec((tm, d), lambda i: (i, 0))
    vec = pl.BlockSpec((1, d), lambda i: (0, 0))
    return pl.pallas_call(
        body, name=name, grid=(s // tm,), in_specs=[row, vec, row],
        out_specs=[pl.BlockSpec((1, 128), lambda i: (0, 0)), row, row, vec],
        out_shape=[jax.ShapeDtypeStruct((1, 128), F32), jax.ShapeDtypeStruct((s, d), F32),
                   jax.ShapeDtypeStruct((s, d), _ACT), jax.ShapeDtypeStruct((1, d), F32)])(h, g, target)


def _gmlp_parts(pu, pv, lng, lnb):
    u = _gelu(pu)
    v = _gelu(pv)
    mu = jnp.mean(v, axis=-1, keepdims=True)
    vc = v - mu
    rstd = lax.rsqrt(jnp.mean(vc * vc, axis=-1, keepdims=True) + EPS)
    xhat = vc * rstd
    vn = xhat * lng + lnb
    return u, xhat, rstd, vn


def _gmlp_fwd(proj, lng, lnb, ws, bs3, name):
    s = proj.shape[0]

    def body(pu_ref, pv_ref, lng_ref, lnb_ref, ws_ref, bs_ref, o_ref):
        u, _, _, vn = _gmlp_parts(pu_ref[...], pv_ref[...], lng_ref[...], lnb_ref[...])
        causal = _iota((CHUNK, CHUNK), 0) >= _iota((CHUNK, CHUNK), 1)
        for g in range(A_GROUPS):
            sl = slice(g * A_GW, (g + 1) * A_GW)
            w = jnp.where(causal, ws_ref[g], 0.0)
            sv = _dot(w, vn[:, sl]) + bs_ref[g]
            o_ref[:, sl] = (u[:, sl] * sv).astype(o_ref.dtype)

    full = lambda shape: pl.BlockSpec(shape, lambda c: (0,) * len(shape))
    return pl.pallas_call(
        body, name=name, grid=(s // CHUNK,),
        in_specs=[pl.BlockSpec((CHUNK, D_INNER), lambda c: (c, 0)), pl.BlockSpec((CHUNK, D_INNER), lambda c: (c, 1)),
                  full((1, D_INNER)), full((1, D_INNER)), full((A_GROUPS, CHUNK, CHUNK)), full((A_GROUPS, CHUNK, 1))],
        out_specs=pl.BlockSpec((CHUNK, D_INNER), lambda c: (c, 0)),
        out_shape=jax.ShapeDtypeStruct((s, D_INNER + X_WIDTH), _ACT), compiler_params=_cp(VMEM_BIG))(proj, proj, lng, lnb, ws, bs3)


def _gmlp_bwd(proj, dcat, lng, lnb, ws, bs3, name):
    s = proj.shape[0]

    def body(pu_ref, pv_ref, dm_ref, lng_ref, lnb_ref, ws_ref, bs_ref, dp_ref, dws_ref, dbs_ref, dlng_ref, dlnb_ref, dvn_ref):
        @pl.when(pl.program_id(0) == 0)
        def _():
            dws_ref[...] = jnp.zeros_like(dws_ref)
            dbs_ref[...] = jnp.zeros_like(dbs_ref)
            dlng_ref[...] = jnp.zeros_like(dlng_ref)
            dlnb_ref[...] = jnp.zeros_like(dlnb_ref)

        pu, pv = pu_ref[...], pv_ref[...]
        lng = lng_ref[...]
        u, xhat, rstd, vn = _gmlp_parts(pu, pv, lng, lnb_ref[...])
        dm = dm_ref[...].astype(F32)
        causal = _iota((CHUNK, CHUNK), 0) >= _iota((CHUNK, CHUNK), 1)
        for g in range(A_GROUPS):
            sl = slice(g * A_GW, (g + 1) * A_GW)
            w = jnp.where(causal, ws_ref[g], 0.0)
            sv = _dot(w, vn[:, sl]) + bs_ref[g]
            dsv = dm[:, sl] * u[:, sl]
            dp_ref[:, sl] = (dm[:, sl] * sv * _gelu_grad(pu[:, sl])).astype(dp_ref.dtype)
            dvn_ref[:, sl] = _dot_tn(w, dsv)
            dws_ref[g] += jnp.where(causal, _dot_nt(dsv, vn[:, sl]), 0.0)
            dbs_ref[g] += jnp.sum(dsv, axis=-1, keepdims=True)
        dvn = dvn_ref[...]
        dlng_ref[...] += jnp.sum(dvn * xhat, axis=0, keepdims=True)
        dlnb_ref[...] += jnp.sum(dvn, axis=0, keepdims=True)
        dxh = dvn * lng
        dv = rstd * (dxh - jnp.mean(dxh, axis=-1, keepdims=True) - xhat * jnp.mean(dxh * xhat, axis=-1, keepdims=True))
        dp_ref[:, D_INNER:] = (dv * _gelu_grad(pv)).astype(dp_ref.dtype)

    full = lambda shape: pl.BlockSpec(shape, lambda c: (0,) * len(shape))
    return pl.pallas_call(
        body, name=name, grid=(s // CHUNK,),
        in_specs=[pl.BlockSpec((CHUNK, D_INNER), lambda c: (c, 0)), pl.BlockSpec((CHUNK, D_INNER), lambda c: (c, 1)),
                  pl.BlockSpec((CHUNK, D_INNER), lambda c: (c, 0)),
                  full((1, D_INNER)), full((1, D_INNER)), full((A_GROUPS, CHUNK, CHUNK)), full((A_GROUPS, CHUNK, 1))],
        out_specs=[pl.BlockSpec((CHUNK, 2 * D_INNER), lambda c: (c, 0)), full((A_GROUPS, CHUNK, CHUNK)),
                   full((A_GROUPS, CHUNK, 1)), full((1, D_INNER)), full((1, D_INNER))],
        out_shape=[jax.ShapeDtypeStruct((s, 2 * D_INNER + X_WIDTH), _ACT), jax.ShapeDtypeStruct((A_GROUPS, CHUNK, CHUNK), F32),
                   jax.ShapeDtypeStruct((A_GROUPS, CHUNK, 1), F32), jax.ShapeDtypeStruct((1, D_INNER), F32),
                   jax.ShapeDtypeStruct((1, D_INNER), F32)],
        scratch_shapes=[pltpu.VMEM((CHUNK, D_INNER), F32)],
        compiler_params=_cp(VMEM_BIG))(proj, proj, dcat, lng, lnb, ws, bs3)


_X_SCALE = 1.0 / math.sqrt(X_HD)


def _attn_fwd(proj, qblk, kv, cat, name, tm=256):
    s = proj.shape[0]
    tm = min(tm, s)

    def body(q_ref, kv_ref, cat_ref, o_ref):
        for h in range(X_HEADS):
            sl = slice(h * X_HD, (h + 1) * X_HD)
            k = kv_ref[:, sl]
            v = kv_ref[:, X_WIDTH + h * X_HD:X_WIDTH + (h + 1) * X_HD]
            sc = _dot_nt(q_ref[:, sl], k) * _X_SCALE
            e = jnp.exp(sc - jnp.max(sc, axis=-1, keepdims=True))
            p = e / jnp.sum(e, axis=-1, keepdims=True)
            o_ref[:, sl] = _dot(p, v).astype(o_ref.dtype)

    return pl.pallas_call(
        body, name=name, grid=(s // tm,),
        in_specs=[pl.BlockSpec((tm, X_WIDTH), lambda i: (i, qblk)), pl.BlockSpec((N_MEM, 2 * X_WIDTH), lambda i: (0, 0)),
                  pl.BlockSpec(memory_space=pl.ANY)],
        out_specs=pl.BlockSpec((tm, X_WIDTH), lambda i: (i, D_INNER // X_WIDTH)),
        out_shape=jax.ShapeDtypeStruct(cat.shape, cat.dtype), input_output_aliases={2: 0})(proj, kv, cat)


def _attn_bwd(proj, qblk, kv, dcat, dproj, name, tm=256):
    s = proj.shape[0]
    tm = min(tm, s)

    def body(q_ref, kv_ref, do_ref, dproj_ref, dq_ref, dkv_ref):
        @pl.when(pl.program_id(0) == 0)
        def _():
            dkv_ref[...] = jnp.zeros_like(dkv_ref)

        for h in range(X_HEADS):
            sl = slice(h * X_HD, (h + 1) * X_HD)
            slv = slice(X_WIDTH + h * X_HD, X_WIDTH + (h + 1) * X_HD)
            q = q_ref[:, sl]
            k = kv_ref[:, sl]
            v = kv_ref[:, slv]
            do = do_ref[:, sl].astype(F32)
            sc = _dot_nt(q, k) * _X_SCALE
            e = jnp.exp(sc - jnp.max(sc, axis=-1, keepdims=True))
            p = e / jnp.sum(e, axis=-1, keepdims=True)
            dp = _dot_nt(do, v)
            ds = p * (dp - jnp.sum(dp * p, axis=-1, keepdims=True)) * _X_SCALE
            dq_ref[:, sl] = _dot(ds, k).astype(dq_ref.dtype)
            dkv_ref[:, sl] += _dot_tn(ds, q)
            dkv_ref[:, slv] += _dot_tn(p, do)

    return pl.pallas_call(
        body, name=name, grid=(s // tm,),
        in_specs=[pl.BlockSpec((tm, X_WIDTH), lambda i: (i, qblk)), pl.BlockSpec((N_MEM, 2 * X_WIDTH), lambda i: (0, 0)),
                  pl.BlockSpec((tm, X_WIDTH), lambda i: (i, 2)), pl.BlockSpec(memory_space=pl.ANY)],
        out_specs=[pl.BlockSpec((tm, X_WIDTH), lambda i: (i, qblk)), pl.BlockSpec((N_MEM, 2 * X_WIDTH), lambda i: (0, 0))],
        out_shape=[jax.ShapeDtypeStruct(dproj.shape, dproj.dtype), jax.ShapeDtypeStruct((N_MEM, 2 * X_WIDTH), F32)],
        input_output_aliases={3: 0})(proj, kv, dcat, dproj)


CONV_TC = 256
_XBC_BLK0 = D_INNER // CONV_TC


CONV_RB = 64
SUBLANES = 8


def _rows_before(cur, prev_last, j):
    rolled = pltpu.roll(cur, j, 0)
    head = jnp.where(_iota((SUBLANES, cur.shape[1]), 0) < j, pltpu.roll(prev_last, j, 0), rolled[:SUBLANES])
    return jnp.concatenate([head, rolled[SUBLANES:]], axis=0)


def _rows_after(cur, next_first, j):
    n = cur.shape[0]
    rolled = pltpu.roll(cur, n - j, 0)
    tail = jnp.where(_iota((SUBLANES, cur.shape[1]), 0) >= SUBLANES - j, pltpu.roll(next_first, SUBLANES - j, 0),
                     rolled[n - SUBLANES:])
    return jnp.concatenate([rolled[:n - SUBLANES], tail], axis=0)


def _conv_pre(x_ref, w_ref, b_ref, r0, prev_last):
    cur = x_ref[pl.ds(r0, CONV_RB), :]
    shifts = [_rows_before(cur, prev_last, j) for j in range(1, CONV_K)]
    pre = b_ref[...] + w_ref[CONV_K - 1:CONV_K, :] * cur
    for j in range(1, CONV_K):
        pre = pre + w_ref[CONV_K - 1 - j:CONV_K - j, :] * shifts[j - 1]
    return pre, cur, shifts


def _conv_fwd(proj, w, b, name):
    s = proj.shape[0]

    def body(x_ref, w_ref, b_ref, o_ref):
        xv = x_ref[...]
        rows = _iota(xv.shape, 0)
        pre = b_ref[...] + w_ref[CONV_K - 1:CONV_K, :] * xv
        for j in range(1, CONV_K):
            pre = pre + w_ref[CONV_K - 1 - j:CONV_K - j, :] * jnp.where(rows >= j, pltpu.roll(xv, j, 0), 0.0)
        o_ref[...] = pre * _sigmoid(pre)

    return pl.pallas_call(
        body, name=name, grid=(CONV_DIM // CONV_TC,),
        in_specs=[pl.BlockSpec((s, CONV_TC), lambda j: (0, _XBC_BLK0 + j)), pl.BlockSpec((CONV_K, CONV_TC), lambda j: (0, j)),
                  pl.BlockSpec((1, CONV_TC), lambda j: (0, j))],
        out_specs=pl.BlockSpec((s, CONV_TC), lambda j: (0, j)),
        out_shape=jax.ShapeDtypeStruct((s, CONV_DIM), F32), compiler_params=_cp(VMEM_BIG))(proj, w, b)


def _conv_bwd(proj, w, b, dxbc, dproj, name):
    s = proj.shape[0]

    nb = s // CONV_RB

    def body(x_ref, w_ref, b_ref, d_ref, dproj_ref, dx_ref, dw_ref, db_ref, dpre_ref):
        def fold(v):
            out = v[:SUBLANES]
            for t in range(1, CONV_RB // SUBLANES):
                out = out + v[t * SUBLANES:(t + 1) * SUBLANES]
            return out

        def first(i, carry):
            prev_last, acc = carry
            r0 = pl.multiple_of(i * CONV_RB, CONV_RB)
            pre, cur, shifts = _conv_pre(x_ref, w_ref, b_ref, r0, prev_last)
            sig = _sigmoid(pre)
            dpre = d_ref[pl.ds(r0, CONV_RB), :] * (sig * (1.0 + pre * (1.0 - sig)))
            dpre_ref[pl.ds(r0, CONV_RB), :] = dpre
            taps = [cur] + shifts
            acc = tuple(a + fold(dpre * t) for a, t in zip(acc[:CONV_K], taps)) + (acc[CONV_K] + fold(dpre),)
            return cur[CONV_RB - SUBLANES:], acc

        zero8 = jnp.zeros((SUBLANES, CONV_TC), F32)
        _, acc = lax.fori_loop(0, nb, first, (zero8, (zero8,) * (CONV_K + 1)))
        for j in range(CONV_K):
            dw_ref[CONV_K - 1 - j:CONV_K - j, :] = jnp.sum(acc[j], axis=0, keepdims=True)
        db_ref[...] = jnp.sum(acc[CONV_K], axis=0, keepdims=True)

        def second(i, next_first):
            r0 = pl.multiple_of((nb - 1 - i) * CONV_RB, CONV_RB)
            cur = dpre_ref[pl.ds(r0, CONV_RB), :]
            dx = w_ref[CONV_K - 1:CONV_K, :] * cur
            for j in range(1, CONV_K):
                dx = dx + w_ref[CONV_K - 1 - j:CONV_K - j, :] * _rows_after(cur, next_first, j)
            dx_ref[pl.ds(r0, CONV_RB), :] = dx.astype(dx_ref.dtype)
            return cur[:SUBLANES]

        lax.fori_loop(0, nb, second, zero8)

    return pl.pallas_call(
        body, name=name, grid=(CONV_DIM // CONV_TC,),
        in_specs=[pl.BlockSpec((s, CONV_TC), lambda j: (0, _XBC_BLK0 + j)), pl.BlockSpec((CONV_K, CONV_TC), lambda j: (0, j)),
                  pl.BlockSpec((1, CONV_TC), lambda j: (0, j)), pl.BlockSpec((s, CONV_TC), lambda j: (0, j)),
                  pl.BlockSpec(memory_space=pl.ANY)],
        out_specs=[pl.BlockSpec((s, CONV_TC), lambda j: (0, _XBC_BLK0 + j)), pl.BlockSpec((CONV_K, CONV_TC), lambda j: (0, j)),
                   pl.BlockSpec((1, CONV_TC), lambda j: (0, j))],
        out_shape=[jax.ShapeDtypeStruct(dproj.shape, dproj.dtype), jax.ShapeDtypeStruct((CONV_K, CONV_DIM), F32),
                   jax.ShapeDtypeStruct((1, CONV_DIM), F32)], input_output_aliases={4: 0},
        scratch_shapes=[pltpu.VMEM((s, CONV_TC), F32)],
        compiler_params=_cp(VMEM_BIG))(proj, w, b, dxbc, dproj)


def _ssd_common(dtc_ref, br_ref, ar_ref, csb_ref, cst_ref, csf_ref):
    a_row = -jnp.exp(ar_ref[...])
    dt_c = _softplus(dtc_ref[...] + br_ref[...])
    tril = _iota((CHUNK, CHUNK), 0) >= _iota((CHUNK, CHUNK), 1)
    cs = _sel_dot(tril, dt_c * a_row)
    cst_ref[...] = cs.T
    e64 = (jnp.right_shift(_iota((HPAD, D_INNER), 1), 6) == _iota((HPAD, D_INNER), 0)).astype(jnp.bfloat16)
    e128 = jnp.right_shift(_iota((HPAD, SSM_HEADS * CHUNK), 1), 7) == _iota((HPAD, SSM_HEADS * CHUNK), 0)
    csb_ref[...] = _dot_sel(cs, e128)
    dt_full = _dot_sel(dt_c, e64)
    csf_ref[...] = _dot_sel(cs, e64)
    cs_full = csf_ref[...]
    cs_last = csf_ref[CHUNK - 1:CHUNK, :]
    e_full = jnp.exp(cs_full)
    f_full = jnp.exp(cs_last - cs_full)
    gamma = jnp.exp(cs_last)
    return a_row, dt_c, cs, dt_full, e_full, f_full, gamma, e64


def _ssd_lambda(csb_ref, cst_ref, h, causal):
    diff = csb_ref[:, h * CHUNK:(h + 1) * CHUNK] - cst_ref[h:h + 1, :]
    return jnp.exp(jnp.where(causal, diff, -1e30))


_SSD_VEC_SPECS = lambda: [pl.BlockSpec((1, HPAD), lambda c: (0, 0)), pl.BlockSpec((1, HPAD), lambda c: (0, 0)),
                          pl.BlockSpec((1, D_INNER), lambda c: (0, 0))]


def _ssd_fwd(xbc, dtc, bias_row, alog_row, dfull, name):
    s = xbc.shape[0]
    nc = s // CHUNK

    def body(xbc_ref, dtc_ref, br_ref, ar_ref, df_ref, y_ref, st_ref, ht_ref, csb_ref, cst_ref, csf_ref):
        @pl.when(pl.program_id(0) == 0)
        def _():
            ht_ref[...] = jnp.zeros_like(ht_ref)

        _, _, _, dt_full, e_full, f_full, gamma, _ = _ssd_common(dtc_ref, br_ref, ar_ref, csb_ref, cst_ref, csf_ref)
        x = xbc_ref[:, :D_INNER]
        xdt = x * dt_full
        st_ref[...] = ht_ref[...]
        causal = _iota((CHUNK, CHUNK), 0) >= _iota((CHUNK, CHUNK), 1)
        lo = _iota((CHUNK, CHUNK), 1) < SSM_P
        for g in range(SSM_GROUPS):
            gs = slice(g * SSM_GW, (g + 1) * SSM_GW)
            bg = xbc_ref[:, D_INNER + g * SSM_N:D_INNER + (g + 1) * SSM_N]
            cg = xbc_ref[:, D_INNER + SSM_GROUPS * SSM_N + g * SSM_N:D_INNER + SSM_GROUPS * SSM_N + (g + 1) * SSM_N]
            ht = ht_ref[:, gs]
            cb = _dot_nt(cg, bg)
            yoff = e_full[:, gs] * _dot(cg, ht)
            for jp in range(SSM_GW // CHUNK):
                j = g * (SSM_GW // CHUNK) + jp
                ps = slice(j * CHUNK, (j + 1) * CHUNK)
                x2 = xdt[:, ps]
                y0 = _dot(cb * _ssd_lambda(csb_ref, cst_ref, 2 * j, causal), x2)
                y1 = _dot(cb * _ssd_lambda(csb_ref, cst_ref, 2 * j + 1, causal), x2)
                y_ref[:, ps] = (jnp.where(lo, y0, y1) + yoff[:, jp * CHUNK:(jp + 1) * CHUNK]
                                + x[:, ps] * df_ref[:, ps])
            ht_ref[:, gs] = gamma[:, gs] * ht + _dot_tn(bg, xdt[:, gs] * f_full[:, gs])

    return pl.pallas_call(
        body, name=name, grid=(nc,),
        in_specs=[pl.BlockSpec((CHUNK, CONV_DIM), lambda c: (c, 0)), pl.BlockSpec((CHUNK, HPAD), lambda c: (c, 0))]
                 + _SSD_VEC_SPECS(),
        out_specs=[pl.BlockSpec((CHUNK, D_INNER), lambda c: (c, 0)), pl.BlockSpec((None, SSM_N, D_INNER), lambda c: (c, 0, 0))],
        out_shape=[jax.ShapeDtypeStruct((s, D_INNER), F32), jax.ShapeDtypeStruct((nc, SSM_N, D_INNER), F32)],
        scratch_shapes=[pltpu.VMEM((SSM_N, D_INNER), F32), pltpu.VMEM((CHUNK, SSM_HEADS * CHUNK), F32),
                        pltpu.VMEM((HPAD, CHUNK), F32), pltpu.VMEM((CHUNK, D_INNER), F32)],
        compiler_params=_cp(VMEM_BIG))(xbc, dtc, bias_row, alog_row, dfull)


def _ssd_bwd(xbc, dtc, bias_row, alog_row, dfull, dy, states, name):
    s = xbc.shape[0]
    nc = s // CHUNK
    rev = lambda c: nc - 1 - c

    def body(xbc_ref, dtc_ref, br_ref, ar_ref, df_ref, dy_ref, st_ref,
             dxbc_ref, ddt_ref, dalog_ref, dd_ref, dbias_ref,
             dht_ref, csb_ref, cst_ref, csf_ref, ddf_ref, dxs_ref, dcsf_ref, dcsl_ref):
        step = pl.program_id(0)

        @pl.when(step == 0)
        def _():
            dht_ref[...] = jnp.zeros_like(dht_ref)
            ddf_ref[...] = jnp.zeros_like(ddf_ref)
            dalog_ref[...] = jnp.zeros_like(dalog_ref)
            dbias_ref[...] = jnp.zeros_like(dbias_ref)
            dd_ref[...] = jnp.zeros_like(dd_ref)

        a_row, dt_c, _, dt_full, e_full, f_full, gamma, e64 = _ssd_common(dtc_ref, br_ref, ar_ref, csb_ref, cst_ref, csf_ref)
        x = xbc_ref[:, :D_INNER]
        xdt = x * dt_full
        dy_all = dy_ref[...]
        ddf_ref[...] += jnp.broadcast_to(jnp.sum(dy_all * x, axis=0, keepdims=True), ddf_ref.shape)
        causal = _iota((CHUNK, CHUNK), 0) >= _iota((CHUNK, CHUNK), 1)
        lo = _iota((CHUNK, CHUNK), 1) < SSM_P
        head_lane = _iota((CHUNK, HPAD), 1)
        head_row = _iota((HPAD, CHUNK), 0)
        dcs_heads = jnp.zeros((CHUNK, HPAD), F32)
        dcs_cols = jnp.zeros((HPAD, CHUNK), F32)
        for g in range(SSM_GROUPS):
            gs = slice(g * SSM_GW, (g + 1) * SSM_GW)
            b0 = D_INNER + g * SSM_N
            c0 = D_INNER + SSM_GROUPS * SSM_N + g * SSM_N
            bg = xbc_ref[:, b0:b0 + SSM_N]
            cg = xbc_ref[:, c0:c0 + SSM_N]
            ht = st_ref[:, gs]
            dht = dht_ref[:, gs]
            dyg = dy_all[:, gs]
            eg, fg, gg = e_full[:, gs], f_full[:, gs], gamma[:, gs]
            z = _dot(cg, ht)
            dz = dyg * eg
            dcg = _dot_nt(dz, ht)
            dht_new = _dot_tn(cg, dz) + gg * dht
            xf = xdt[:, gs] * fg
            dxf = _dot(bg, dht)
            dbg = _dot_nt(xf, dht)
            dff = dxf * xf
            dcsf_ref[:, gs] = dyg * eg * z - dff
            dcsl_ref[:, gs] = jnp.broadcast_to(
                jnp.sum(dff, axis=0, keepdims=True) + jnp.sum(dht * ht, axis=0, keepdims=True) * gg, (8, SSM_GW))
            cb = _dot_nt(cg, bg)
            dcb = jnp.zeros((CHUNK, CHUNK), F32)
            for jp in range(SSM_GW // CHUNK):
                j = g * (SSM_GW // CHUNK) + jp
                ps = slice(j * CHUNK, (j + 1) * CHUNK)
                x2 = xdt[:, ps]
                dy2 = dy_all[:, ps]
                dxh = []
                for hh in range(2):
                    h = 2 * j + hh
                    lam = _ssd_lambda(csb_ref, cst_ref, h, causal)
                    mh = cb * lam
                    dyh = jnp.where(lo, dy2, 0.0) if hh == 0 else jnp.where(lo, 0.0, dy2)
                    dm = _dot_nt(dyh, x2)
                    dcb = dcb + dm * lam
                    gm = dm * mh
                    dcs_heads = dcs_heads + jnp.where(head_lane == h, jnp.sum(gm, axis=1, keepdims=True), 0.0)
                    dcs_cols = dcs_cols + jnp.where(head_row == h, jnp.sum(gm, axis=0, keepdims=True), 0.0)
                    dxh.append(_dot_tn(mh, dy2))
                dxs_ref[:, ps] = jnp.where(lo, dxh[0], dxh[1]) + dxf[:, jp * CHUNK:(jp + 1) * CHUNK] * fg[:, jp * CHUNK:(jp + 1) * CHUNK]
            dxbc_ref[:, b0:b0 + SSM_N] = (dbg + _dot_tn(dcb, cg)).astype(dxbc_ref.dtype)
            dxbc_ref[:, c0:c0 + SSM_N] = (dcg + _dot(dcb, bg)).astype(dxbc_ref.dtype)
            dht_ref[:, gs] = dht_new
        dxs = dxs_ref[...]
        dcs_heads = dcs_heads - dcs_cols.T + _dot_sel(dcsf_ref[...], e64, ((1,), (1,)))
        dcs_last = _dot_sel(dcsl_ref[...], e64, ((1,), (1,)))
        dcs_heads = dcs_heads + jnp.where(_iota((CHUNK, HPAD), 0) == CHUNK - 1, dcs_last[0:1, :], 0.0)
        triu = _iota((CHUNK, CHUNK), 0) <= _iota((CHUNK, CHUNK), 1)
        dda = _sel_dot(triu, dcs_heads)
        ddt = dda * a_row + _dot_sel(dxs * x, e64, ((1,), (1,)))
        dxbc_ref[:, :D_INNER] = (dxs * dt_full + dy_all * df_ref[...]).astype(dxbc_ref.dtype)
        dalog_ref[...] += jnp.sum(dda * dt_c, axis=0, keepdims=True) * a_row
        ddt_raw = ddt * _sigmoid(dtc_ref[...] + br_ref[...])
        ddt_ref[...] = ddt_raw.astype(ddt_ref.dtype)
        dbias_ref[...] += jnp.sum(ddt_raw, axis=0, keepdims=True)

        @pl.when(step == nc - 1)
        def _():
            dd_ref[...] = _dot_sel(ddf_ref[...], e64, ((1,), (1,)))[0:1, :]

    vec = pl.BlockSpec((1, HPAD), lambda c: (0, 0))
    return pl.pallas_call(
        body, name=name, grid=(nc,),
        in_specs=[pl.BlockSpec((CHUNK, CONV_DIM), lambda c: (rev(c), 0)), pl.BlockSpec((CHUNK, HPAD), lambda c: (rev(c), 0))]
                 + _SSD_VEC_SPECS()
                 + [pl.BlockSpec((CHUNK, D_INNER), lambda c: (rev(c), 0)),
                    pl.BlockSpec((None, SSM_N, D_INNER), lambda c: (rev(c), 0, 0))],
        out_specs=[pl.BlockSpec((CHUNK, CONV_DIM), lambda c: (rev(c), 0)), pl.BlockSpec((CHUNK, HPAD), lambda c: (rev(c), 0)),
                   vec, vec, vec],
        out_shape=[jax.ShapeDtypeStruct((s, CONV_DIM), F32), jax.ShapeDtypeStruct((s, HPAD), _ACT),
                   jax.ShapeDtypeStruct((1, HPAD), F32), jax.ShapeDtypeStruct((1, HPAD), F32),
                   jax.ShapeDtypeStruct((1, HPAD), F32)],
        scratch_shapes=[pltpu.VMEM((SSM_N, D_INNER), F32), pltpu.VMEM((CHUNK, SSM_HEADS * CHUNK), F32),
                        pltpu.VMEM((HPAD, CHUNK), F32), pltpu.VMEM((CHUNK, D_INNER), F32),
                        pltpu.VMEM((8, D_INNER), F32), pltpu.VMEM((CHUNK, D_INNER), F32),
                        pltpu.VMEM((CHUNK, D_INNER), F32), pltpu.VMEM((8, D_INNER), F32)],
        compiler_params=_cp(VMEM_BIG))(xbc, dtc, bias_row, alog_row, dfull, dy, states)


def _gate_fwd(y, proj, gn, name, tm=256):
    s = y.shape[0]
    tm = min(tm, s)

    def body(y_ref, z_ref, gn_ref, o_ref):
        for g in range(SSM_GROUPS):
            gs = slice(g * SSM_GW, (g + 1) * SSM_GW)
            z = z_ref[:, gs]
            t = y_ref[:, gs] * (z * _sigmoid(z))
            r = lax.rsqrt(jnp.mean(t * t, axis=-1, keepdims=True) + EPS)
            o_ref[:, gs] = (t * r * gn_ref[:, gs]).astype(o_ref.dtype)

    row = pl.BlockSpec((tm, D_INNER), lambda i: (i, 0))
    return pl.pallas_call(
        body, name=name, grid=(s // tm,), in_specs=[row, row, pl.BlockSpec((1, D_INNER), lambda i: (0, 0))],
        out_specs=row, out_shape=jax.ShapeDtypeStruct((s, D_INNER + X_WIDTH), _ACT))(y, proj, gn)


def _gate_bwd(y, proj, gn, dcat, name, tm=256):
    s = y.shape[0]
    tm = min(tm, s)

    def body(y_ref, z_ref, gn_ref, dm_ref, dy_ref, dz_ref, dgn_ref):
        @pl.when(pl.program_id(0) == 0)
        def _():
            dgn_ref[...] = jnp.zeros_like(dgn_ref)

        for g in range(SSM_GROUPS):
            gs = slice(g * SSM_GW, (g + 1) * SSM_GW)
            z = z_ref[:, gs]
            yv = y_ref[:, gs]
            sig = _sigmoid(z)
            sz = z * sig
            t = yv * sz
            r = lax.rsqrt(jnp.mean(t * t, axis=-1, keepdims=True) + EPS)
            th = t * r
            dm = dm_ref[:, gs].astype(F32)
            dmg = dm * gn_ref[:, gs]
            dt_ = r * (dmg - th * jnp.mean(dmg * th, axis=-1, keepdims=True))
            dgn_ref[:, gs] += jnp.sum(dm * th, axis=0, keepdims=True)
            dy_ref[:, gs] = dt_ * sz
            dz_ref[:, gs] = (dt_ * yv * (sig * (1.0 + z * (1.0 - sig)))).astype(dz_ref.dtype)

    row = pl.BlockSpec((tm, D_INNER), lambda i: (i, 0))
    vec = pl.BlockSpec((1, D_INNER), lambda i: (0, 0))
    return pl.pallas_call(
        body, name=name, grid=(s // tm,), in_specs=[row, row, vec, row], out_specs=[row, row, vec],
        out_shape=[jax.ShapeDtypeStruct((s, D_INNER), F32), jax.ShapeDtypeStruct((s, 6 * D_MODEL), _ACT),
                   jax.ShapeDtypeStruct((1, D_INNER), F32)])(y, proj, gn, dcat)


def _block_of(kind, width):
    if kind == "col":
        return lambda ref, j: ref.at[:, :, pl.ds(pl.multiple_of(j * width, 128), width)]
    if kind == "row":
        return lambda ref, j: ref.at[:, pl.ds(pl.multiple_of(j * width, 8), width), :]
    return lambda ref, j: ref.at[j]


def _coords():
    return lax.axis_index("x"), lax.axis_index("y"), lax.axis_index("c")


def _rel_chip(x, y, k):
    return (1 - x if k & 1 else x), (1 - y if k & 2 else y)


_HBM = lambda: pl.BlockSpec(memory_space=pltpu.HBM)


def _all_gather(shards, layouts, name, after=()):
    n, n_after = len(shards), len(after)
    blocks = [_block_of(kind, width) for kind, width, _ in layouts]

    def body(*refs):
        _all_gather_body(refs[:n], refs[n + n_after:2 * n + n_after], *refs[2 * n + n_after:], blocks)

    return pl.pallas_call(
        body, name=name, in_specs=[_HBM()] * n + [pl.BlockSpec(memory_space=pl.ANY)] * n_after, out_specs=[_HBM()] * n,
        out_shape=[jax.ShapeDtypeStruct(shape, sh.dtype) for sh, (_, _, shape) in zip(shards, layouts)],
        scratch_shapes=[pltpu.SemaphoreType.DMA((n, 7)), pltpu.SemaphoreType.DMA((n, 7)), pltpu.SemaphoreType.DMA((n,))],
    )(*shards, *after)


def _all_gather_body(ins, outs, send_sems, recv_sems, local_sems, blocks):
    n = len(ins)
    x, y, c = _coords()
    sibling = (x, y, 1 - c)

    def copy(t, k, chip, core, to, src=None):
        dst = blocks[t](outs[t], 4 * chip[0] + 2 * chip[1] + core)
        return pltpu.make_async_remote_copy(
            src_ref=dst if src is None else src, dst_ref=dst, send_sem=send_sems.at[t, k],
            recv_sem=recv_sems.at[t, k], device_id=to, device_id_type=MESH)

    started = []
    for t in range(n):
        mine = pltpu.make_async_copy(ins[t], blocks[t](outs[t], 4 * x + 2 * y + c), local_sems.at[t])
        mine.start()
        started.append(mine)
    sends = []
    for t in range(n):
        for k in range(4):
            px, py = _rel_chip(x, y, k)
            cp = copy(t, k, (x, y), c, (px, py, 1 - c if k == 0 else c), src=ins[t])
            cp.start()
            sends.append(cp)
    for t in range(n):
        for k in range(1, 4):
            chip = _rel_chip(x, y, k)
            copy(t, k, chip, c, sibling).wait_recv()
            fwd = copy(t, 3 + k, chip, c, sibling)
            fwd.start()
            sends.append(fwd)
    for t in range(n):
        copy(t, 0, (x, y), 1 - c, sibling).wait_recv()
        for k in range(1, 4):
            copy(t, 3 + k, _rel_chip(x, y, k), 1 - c, sibling).wait_recv()
    for cp in sends:
        cp.wait_send()
    for mine in started:
        mine.wait()


def _handshake(peers):
    barrier = pltpu.get_barrier_semaphore()
    for peer in peers:
        pl.semaphore_signal(barrier, inc=1, device_id=peer, device_id_type=MESH)
    pl.semaphore_wait(barrier, len(peers))


def _two_level_peers():
    x, y, c = _coords()
    return [(x, y, 1 - c)] + [(*_rel_chip(x, y, k), c) for k in range(1, 4)]


SEQ_ID_GATHER, SEQ_ID_SIBLING, SEQ_ID_CHIPS = 1, 2, 3


def _sequencer_call(body, peers, operands, out_types, sems, name, collective_id, after=()):
    n_in, n_out, n_after = len(operands), len(out_types), len(after)

    def launch(*refs):
        _handshake(peers())
        body(refs[:n_in], refs[n_in + n_after:n_in + n_after + n_out], *refs[n_in + n_after + n_out:])

    return pl.kernel(
        launch, name=name, out_type=out_types, mesh=plsc.ScalarSubcoreMesh(axis_name="seq", num_cores=1),
        scratch_types=sems, compiler_params=pltpu.CompilerParams(collective_id=collective_id))(*operands, *after)


def _all_gather_seq(shards, layouts, name, after=()):
    n = len(shards)
    blocks = [_block_of(kind, width) for kind, width, _ in layouts]
    return _sequencer_call(
        lambda ins, outs, *sems: _all_gather_body(ins, outs, *sems, blocks), _two_level_peers, shards,
        [jax.ShapeDtypeStruct(shape, sh.dtype) for sh, (_, _, shape) in zip(shards, layouts)],
        [pltpu.SemaphoreType.DMA((n, 7)), pltpu.SemaphoreType.DMA((n, 7)), pltpu.SemaphoreType.DMA((n,))],
        name, SEQ_ID_GATHER, after)


def _tie(small, after, name):
    def body(*refs):
        refs[-1][...] = refs[0][...]

    vmem = pl.BlockSpec(memory_space=pltpu.VMEM)
    return pl.pallas_call(
        body, name=name, in_specs=[vmem] + [pl.BlockSpec(memory_space=pl.ANY)] * len(after), out_specs=vmem,
        out_shape=jax.ShapeDtypeStruct(small.shape, small.dtype))(small, *after)


def _rs_to_sibling(grads, layouts, name, after=()):
    n = len(grads)
    blocks = [_block_of(kind, width) for kind, width, _ in layouts]

    def body(ins, outs, send_sems, recv_sems):
        x, y, c = _coords()
        sibling = (x, y, 1 - c)
        cps = []
        for t in range(n):
            for k in range(4):
                px, py = _rel_chip(x, y, k)
                cp = pltpu.make_async_remote_copy(
                    src_ref=blocks[t](ins[t], 4 * px + 2 * py + (1 - c)), dst_ref=outs[t].at[k],
                    send_sem=send_sems.at[t, k], recv_sem=recv_sems.at[t, k], device_id=sibling, device_id_type=MESH)
                cp.start()
                cps.append(cp)
        for cp in cps:
            cp.wait_recv()
        for cp in cps:
            cp.wait_send()

    def sibling_only():
        x, y, c = _coords()
        return [(x, y, 1 - c)]

    return _sequencer_call(
        body, sibling_only, grads,
        [jax.ShapeDtypeStruct((4,) + shape, g.dtype) for g, (_, _, shape) in zip(grads, layouts)],
        [pltpu.SemaphoreType.DMA((n, 4)), pltpu.SemaphoreType.DMA((n, 4))], name, SEQ_ID_SIBLING, after)


def _rs_chip_sum(grad, recv, layout, xyc, name):
    kind, width, shape = layout
    r, ccols = shape

    def src_index(k, xyc_ref):
        px = jnp.where(k % 2 == 1, 1 - xyc_ref[0], xyc_ref[0])
        py = jnp.where(k // 2 == 1, 1 - xyc_ref[1], xyc_ref[1])
        return 4 * px + 2 * py + xyc_ref[2]

    if kind == "col":
        g_spec = pl.BlockSpec((r, ccols), lambda k, s_: (0, src_index(k, s_)))
    elif kind == "row":
        g_spec = pl.BlockSpec((r, ccols), lambda k, s_: (src_index(k, s_), 0))
    else:
        g_spec = pl.BlockSpec((None, r, ccols), lambda k, s_: (src_index(k, s_), 0, 0))

    def body(xyc_ref, g_ref, r_ref, o_ref):
        o_ref[...] = (g_ref[...].astype(F32) + r_ref[...].astype(F32)).astype(o_ref.dtype)

    slot = pl.BlockSpec((None, r, ccols), lambda k, s_: (k, 0, 0))
    return pl.pallas_call(
        body, name=name,
        grid_spec=pltpu.PrefetchScalarGridSpec(num_scalar_prefetch=1, grid=(4,), in_specs=[g_spec, slot], out_specs=slot),
        out_shape=jax.ShapeDtypeStruct((4, r, ccols), grad.dtype), compiler_params=_cp(VMEM_BIG))(xyc, grad, recv)


def _rs_across_chips(parts, name):
    n = len(parts)

    def body(ins, outs, send_sems, recv_sems):
        x, y, c = _coords()
        cps = []
        for t in range(n):
            for k in range(1, 4):
                px, py = _rel_chip(x, y, k)
                cp = pltpu.make_async_remote_copy(
                    src_ref=ins[t].at[k], dst_ref=outs[t].at[k - 1], send_sem=send_sems.at[t, k - 1],
                    recv_sem=recv_sems.at[t, k - 1], device_id=(px, py, c), device_id_type=MESH)
                cp.start()
                cps.append(cp)
        for cp in cps:
            cp.wait_recv()
        for cp in cps:
            cp.wait_send()

    def other_chips():
        x, y, c = _coords()
        return [(*_rel_chip(x, y, k), c) for k in range(1, 4)]

    return _sequencer_call(
        body, other_chips, parts, [jax.ShapeDtypeStruct((3,) + p.shape[1:], p.dtype) for p in parts],
        [pltpu.SemaphoreType.DMA((n, 3)), pltpu.SemaphoreType.DMA((n, 3))], name, SEQ_ID_CHIPS)


def _adamw_math(w, g, m, v):
    m = ADAM_B1 * m + (1.0 - ADAM_B1) * g
    v = ADAM_B2 * v + (1.0 - ADAM_B2) * jnp.square(g)
    m_hat = m / (1.0 - ADAM_B1 ** ADAM_STEP)
    v_hat = v / (1.0 - ADAM_B2 ** ADAM_STEP)
    delta = -ADAM_LR * (m_hat / (jnp.sqrt(v_hat) + ADAM_EPS) + ADAM_WD * w)
    return delta, m, v


def _row_tile(rows, cap):
    best = None
    for cand in range(8, min(rows, cap) + 1, 8):
        if rows % cand == 0:
            best = cand
    assert best is not None, rows
    return best


def _adamw(w, m, v, parts, name, layer=None, prev=None, tr=256):
    r, ccols = w.shape[-2:]
    npart = len(parts)
    if r % 8 == 0:
        tr, tc = _row_tile(r, tr), ccols
        steps, at = r // tr, (lambda i: (i, 0))
    else:
        tr, tc = r, 256
        assert ccols % tc == 0
        steps, at = ccols // tc, (lambda i: (0, i))

    def spec(lead):
        if lead is None:
            return pl.BlockSpec((tr, tc), at)
        return pl.BlockSpec((None, tr, tc), lambda i: (lead,) + at(i))

    wspec = lambda: spec(layer)
    pspec = spec

    def body(*refs):
        w_ref, m_ref, v_ref = refs[:3]
        p_refs = refs[3:3 + npart]
        outs = refs[len(refs) - 4:]
        g = p_refs[0][...].astype(F32)
        for p_ref in p_refs[1:]:
            g = g + p_ref[...].astype(F32)
        delta, mn, vn = _adamw_math(w_ref[...], g, m_ref[...], v_ref[...])
        outs[0][...] = g
        outs[1][...] = delta
        outs[2][...] = mn
        outs[3][...] = vn

    operands = [w, m, v] + [p for p, _ in parts]
    in_specs = [wspec(), wspec(), wspec()] + [pspec(lead) for _, lead in parts]
    aliases = {}
    if prev is not None:
        for i, p in enumerate(prev):
            aliases[len(operands)] = i
            operands.append(p)
            in_specs.append(pl.BlockSpec(memory_space=pl.ANY))
    return pl.pallas_call(
        body, name=name, grid=(steps,), in_specs=in_specs, out_specs=[wspec()] * 4,
        out_shape=[jax.ShapeDtypeStruct(w.shape, F32)] * 4, input_output_aliases=aliases)(*operands)


def _sum8(buf, name):
    _, r, ccols = buf.shape

    def body(b_ref, o_ref):
        acc = b_ref[0]
        for j in range(1, N_DEV):
            acc = acc + b_ref[j]
        o_ref[...] = acc

    tr = _row_tile(r, 256)
    return pl.pallas_call(
        body, name=name, grid=(r // tr,), in_specs=[pl.BlockSpec((N_DEV, tr, ccols), lambda i: (0, i, 0))],
        out_specs=pl.BlockSpec((tr, ccols), lambda i: (i, 0)), out_shape=jax.ShapeDtypeStruct((r, ccols), F32))(buf)


def _pack(arrays):
    pieces, layout, off = [], [], 0
    for a in arrays:
        n = a.size
        padded = -(-n // 1024) * 1024
        flat = a.reshape(-1).astype(F32)
        if padded != n:
            flat = jnp.pad(flat, (0, padded - n))
        pieces.append(flat.reshape(padded // 128, 128))
        layout.append((off, n, a.shape))
        off += padded // 128
    return jnp.concatenate(pieces, axis=0), layout


def _unpack(packed, layout):
    out = []
    for off, n, shape in layout:
        rows = -(-n // 1024) * 8
        out.append(packed[off:off + rows].reshape(-1)[:n].reshape(shape))
    return out


def kernel(x, mem, norm_mix, norm_ffn, mem_norm, w_kv, w_out, w_ffn1, w_ffn2, a_in, a_ln_g, a_ln_b, a_ws, a_bs, b_in, b_conv_w, b_conv_b, b_dt_bias, b_a_log, b_d, b_gnorm, final_norm, loss_target, m_norm_mix, m_norm_ffn, m_mem_norm, m_w_kv, m_w_out, m_w_ffn1, m_w_ffn2, m_a_in, m_a_ln_g, m_a_ln_b, m_a_ws, m_a_bs, m_b_in, m_b_conv_w, m_b_conv_b, m_b_dt_bias, m_b_a_log, m_b_d, m_b_gnorm, m_final_norm, v_norm_mix, v_norm_ffn, v_mem_norm, v_w_kv, v_w_out, v_w_ffn1, v_w_ffn2, v_a_in, v_a_ln_g, v_a_ln_b, v_a_ws, v_a_bs, v_b_in, v_b_conv_w, v_b_conv_b, v_b_dt_bias, v_b_a_log, v_b_d, v_b_gnorm, v_final_norm):
    s = x.shape[1]
    xs = x.reshape(s, D_MODEL)
    mems = mem.reshape(N_MEM, D_MODEL)
    target = loss_target.reshape(s, D_MODEL)
    ax, ay, ac = lax.axis_index("x"), lax.axis_index("y"), lax.axis_index("c")
    me = 4 * ax + 2 * ay + ac
    xyc = jnp.stack([ax, ay, ac]).astype(jnp.int32)

    b_cols = b_in.shape[2]
    act = lambda a: a.astype(_ACT)
    lay_f1, lay_f2 = ("col", 512, (1, D_MODEL, D_FF)), ("row", 512, (1, D_FF, D_MODEL))
    lay_out, lay_kv = ("row", 384, (1, 3 * D_MODEL, D_MODEL)), ("col", 256, (1, D_MODEL, 2 * X_WIDTH))
    small_w_pack = _pack([b_conv_w[0], b_conv_b[0], b_gnorm[0]])[0]
    WA, wkv0 = _all_gather_seq([act(a_in), act(w_kv[0:1])], [("col", 640, (1, D_MODEL, 5 * D_MODEL)), lay_kv], "ag_proj_a")
    (wo0,) = _all_gather_seq([act(w_out[0:1])], [lay_out], "ag_out0")
    w1_0, w2_0 = _all_gather_seq([act(w_ffn1[0:1]), act(w_ffn2[0:1])], [lay_f1, lay_f2], "ag_ffn0")
    a0 = _rms_fwd(xs, norm_mix[0].reshape(1, -1), "mix_norm0")
    tr_b = lambda a: jnp.swapaxes(a, 1, 2)
    wbt_blk, small_w = _all_gather_seq(
        [act(tr_b(b_in)[0]), small_w_pack],
        [("blk", 0, (N_DEV, b_cols, D_MODEL)), ("blk", 0, (N_DEV, 32, 128))], "ag_proj_b", after=[a0])
    wo1, wkv1 = _all_gather_seq([act(w_out[1:2]), act(w_kv[1:2])], [lay_out, lay_kv], "ag_out1", after=[a0])
    w1_1, w2_1 = _all_gather_seq([act(w_ffn1[1:2]), act(w_ffn2[1:2])], [lay_f1, lay_f2], "ag_ffn1", after=[a0])
    W1, W2, WO, WKV = [w1_0, w1_1], [w2_0, w2_1], [wo0, wo1], [wkv0, wkv1]
    dt0 = D_INNER + CONV_DIM

    row = lambda a: a.reshape(1, -1)
    nmix = [row(norm_mix[0]), row(norm_mix[1])]
    nffn = [row(norm_ffn[0]), row(norm_ffn[1])]
    nmem = [row(mem_norm[0]), row(mem_norm[1])]
    fin = row(final_norm)
    lng, lnb = a_ln_g.reshape(1, D_INNER), a_ln_b.reshape(1, D_INNER)
    ws = a_ws[0]
    bs3 = a_bs[0].reshape(A_GROUPS, CHUNK, 1)
    pad_h = lambda a: jnp.pad(a.reshape(-1), (0, HPAD - SSM_HEADS))
    bias_row = pad_h(b_dt_bias).reshape(1, HPAD)
    alog_row = pad_h(b_a_log).reshape(1, HPAD)
    dfull = jnp.repeat(b_d.reshape(-1), SSM_P).reshape(1, D_INNER)

    kvs, mns = [None, None], [None, None]

    def mem_kv(i, after=None):
        gain = nmem[i] if after is None else _tie(nmem[i], after, f"tie_mem{i}")
        mns[i] = _rms_fwd(mems, gain, f"mem_norm{i}")
        kvs[i] = _mm(mns[i], WKV[i], m=N_MEM, n=2 * X_WIDTH, k=D_MODEL, b_at=(0, 0, 0), out_dtype=_ACT, name=f"kv{i}")

    def ffn_fwd(h, i):
        f = _rms_fwd(h, nffn[i], f"ffn_norm{i}")
        p = _mm(f, W1[i], m=s, n=D_FF, k=D_MODEL, b_at=(0, 0, 0), out_dtype=_ACT, name=f"ffn_up{i}")
        hn = _mm(p, W2[i], m=s, n=D_MODEL, k=D_FF, b_at=(0, 0, 0), a_pro="relu2", add=h, name=f"ffn_down{i}")
        return f, p, hn

    def out_proj(h, cat, i):
        return _mm(cat, WO[i], m=s, n=D_MODEL, k=3 * D_MODEL, b_at=(0, 0, 0), add=h, name=f"out_proj{i}")

    proj_a = _mm(a0, WA, m=s, n=5 * D_MODEL, k=D_MODEL, b_at=(0, 0, 0), name="proj_a")
    mem_kv(0)
    cat_a = _gmlp_fwd(proj_a, lng, lnb, ws, bs3, "gmlp_fwd")
    cat_a = _attn_fwd(proj_a, 4, kvs[0], cat_a, "attn_fwd0")
    h1 = out_proj(xs, cat_a, 0)
    f0, p0, h2 = ffn_fwd(h1, 0)

    wbt_blk, small_w, _ = lax.optimization_barrier((wbt_blk, small_w, p0))
    wbt_full = wbt_blk.reshape(N_DEV * b_cols, D_MODEL)
    WBT = jnp.concatenate([wbt_full[:dt0], wbt_full[dt0 + SSM_HEADS:]], axis=0)
    WBDT = jnp.pad(wbt_full[dt0:dt0 + SSM_HEADS], ((0, HPAD - SSM_HEADS), (0, 0)))
    cw_sh, cb_sh, gn_sh = 4 * 384, 384, 256
    sw = small_w.reshape(N_DEV, 32 * 128)
    conv_w = jnp.transpose(sw[:, :cw_sh].reshape(N_DEV, CONV_K, 384), (1, 0, 2)).reshape(CONV_K, CONV_DIM)
    conv_b = sw[:, 2048:2048 + cb_sh].reshape(1, CONV_DIM)
    gnorm = sw[:, 3072:3072 + gn_sh].reshape(1, D_INNER)

    a1 = _rms_fwd(h2, nmix[1], "mix_norm1")
    proj_b = _mm(a1, WBT, m=s, n=6 * D_MODEL, k=D_MODEL, tb=True, name="proj_b")
    dt_raw = _mm(a1, WBDT, m=s, n=HPAD, k=D_MODEL, tb=True, name="proj_dt")
    xbc = _conv_fwd(proj_b, conv_w, conv_b, "conv_fwd")
    y_ssd, states = _ssd_fwd(xbc, dt_raw, bias_row, alog_row, dfull, "ssd_fwd")
    cat_b = _gate_fwd(y_ssd, proj_b, gnorm, "gate_fwd")
    mem_kv(1, after=[cat_b])
    cat_b = _attn_fwd(proj_b, 5, kvs[1], cat_b, "attn_fwd1")
    h3 = out_proj(h2, cat_b, 1)
    f1, p1, h4 = ffn_fwd(h3, 1)

    loss_part, dh, dh_act, d_fin = _loss_head(h4, fin, target, "loss_head")

    g_f1, g_f2, g_out, g_kv = [None, None], [None, None], [None, None], [None, None]
    d_nffn, d_nmix, d_nmem = [None, None], [None, None], [None, None]

    def ffn_bwd(dh, dh_act, h_in, f, p, i, after=(), after_last=()):
        dp = _mm(dh_act, W2[i], m=s, n=D_FF, k=D_MODEL, tb=True, b_at=(0, 0, 0), epi_p=p, out_dtype=_ACT, name=f"ffn_down_dx{i}")
        g_f2[i] = _mm(p, dh_act, m=D_FF, n=D_MODEL, k=s, ta=True, a_pro="relu2", out_dtype=_ACT, name=f"ffn_down_dw{i}")
        g_f1[i] = _mm(f, dp, m=D_MODEL, n=D_FF, k=s, ta=True, out_dtype=_ACT, name=f"ffn_up_dw{i}")
        df = _mm(dp, W1[i], m=s, n=D_MODEL, k=D_FF, tb=True, b_at=(0, 0, 0), after=after, name=f"ffn_up_dx{i}")
        gain = _tie(nffn[i], after_last, f"tie_ffn_norm{i}") if after_last else nffn[i]
        dh_in, dh_in_act, d_nffn[i] = _rms_bwd(h_in, gain, df, dh, f"ffn_norm_bwd{i}")
        return dh_in, dh_in_act

    def out_bwd(dh_act, cat, i):
        dcat = _mm(dh_act, WO[i], m=s, n=3 * D_MODEL, k=D_MODEL, tb=True, b_at=(0, 0, 0), out_dtype=_ACT, name=f"out_dx{i}")
        g_out[i] = _mm(cat, dh_act, m=3 * D_MODEL, n=D_MODEL, k=s, ta=True, out_dtype=_ACT, name=f"out_dw{i}")
        return dcat

    def mem_bwd(dkv, i):
        g_kv[i] = _mm(mns[i], dkv, m=D_MODEL, n=2 * X_WIDTH, k=N_MEM, ta=True, out_dtype=_ACT, name=f"kv_dw{i}")
        dmn = _mm(dkv, WKV[i], m=N_MEM, n=D_MODEL, k=2 * X_WIDTH, tb=True, b_at=(0, 0, 0), name=f"kv_dx{i}")
        _, _, d_nmem[i] = _rms_bwd(mems, nmem[i], dmn, None, f"mem_norm_bwd{i}")

    lay_g = {"f1": ("col", 512, (D_MODEL, 512)), "f2": ("row", 512, (512, D_MODEL)), "out": ("row", 384, (384, D_MODEL)),
             "kv": ("col", 256, (D_MODEL, 256)), "a": ("col", 640, (D_MODEL, 640)), "b": ("blk", 0, (b_cols, D_MODEL))}
    reduced = {}

    def reduce_scatter(group, tag, after=()):
        grads3, lays3 = [], []
        for fam, _, g in group:
            kind, width, shape = lay_g[fam]
            grads3.append(g if kind == "blk" else g.reshape((1,) + g.shape))
            lays3.append((kind, width, shape if kind == "blk" else (1,) + shape))
        recv1 = _rs_to_sibling(grads3, lays3, f"rs_sibling_{tag}", after)
        parts = [_rs_chip_sum(g, recv1[t].reshape((4,) + lay_g[fam][2]), lay_g[fam], xyc, f"rs_chip_sum_{fam}{i}")
                 for t, (fam, i, g) in enumerate(group)]
        recv2 = _rs_across_chips(parts, f"rs_chips_{tag}")
        for (fam, i, _), p, r2 in zip(group, parts, recv2):
            reduced[fam, i] = (p, r2)
        return parts, recv2

    dh3, dh3_act = ffn_bwd(dh, dh_act, h3, f1, p1, 1)
    dcat_b = out_bwd(dh3_act, cat_b, 1)
    sums, got_ffn1 = reduce_scatter([("f1", 1, g_f1[1]), ("f2", 1, g_f2[1]), ("out", 1, g_out[1])], "ffn1")
    dy_ssd, dproj_b, d_gnorm = _gate_bwd(y_ssd, proj_b, gnorm, dcat_b, "gate_bwd")
    dproj_b, dkv_b = _attn_bwd(proj_b, 5, kvs[1], dcat_b, dproj_b, "attn_bwd1")
    mem_bwd(dkv_b, 1)
    dxbc, ddt_raw, d_alog, d_dskip, d_dtbias = _ssd_bwd(
        xbc, dt_raw, _tie(bias_row, sums, "tie_ffn1"), alog_row, dfull, dy_ssd, states, "ssd_bwd")
    dproj_b, d_convw, d_convb = _conv_bwd(proj_b, conv_w, _tie(conv_b, got_ffn1, "tie_got_ffn1"), dxbc, dproj_b, "conv_bwd")
    gb = _mm(dproj_b, a1, m=6 * D_MODEL, n=D_MODEL, k=s, ta=True, out_dtype=_ACT, name="proj_b_dw")
    gb_dt = _mm(ddt_raw, a1, m=HPAD, n=D_MODEL, k=s, ta=True, out_dtype=_ACT, name="proj_b_dw_dt")
    gb_full = jnp.concatenate([gb[:dt0], gb_dt[:SSM_HEADS], gb[dt0:]], axis=0)
    gb_blk = gb_full.reshape(N_DEV, b_cols, D_MODEL)
    sums, got_mix1 = reduce_scatter([("kv", 1, g_kv[1]), ("b", 0, gb_blk)], "mix1")
    da1 = _mm(dproj_b, WBT, m=s, n=D_MODEL, k=6 * D_MODEL, name="proj_b_dx")
    da1 = _mm(ddt_raw, WBDT, m=s, n=D_MODEL, k=HPAD, add=da1, name="proj_b_dx_dt")
    dh2, dh2_act, d_nmix[1] = _rms_bwd(h2, _tie(nmix[1], sums, "tie_mix1"), da1, dh3, "mix_norm_bwd1")

    dh1, dh1_act = ffn_bwd(dh2, dh2_act, h1, f0, p0, 0, after=got_ffn1, after_last=got_mix1)
    dcat_a = out_bwd(dh1_act, cat_a, 0)
    sums, got_ffn0 = reduce_scatter([("f1", 0, g_f1[0]), ("f2", 0, g_f2[0]), ("out", 0, g_out[0])], "ffn0")
    dproj_a, d_ws, d_bs3, d_lng, d_lnb = _gmlp_bwd(proj_a, dcat_a, _tie(lng, sums, "tie_ffn0"), lnb, ws, bs3, "gmlp_bwd")
    dproj_a, dkv_a = _attn_bwd(proj_a, 4, kvs[0], dcat_a, dproj_a, "attn_bwd0")
    da0 = _mm(dproj_a, WA, m=s, n=D_MODEL, k=5 * D_MODEL, tb=True, b_at=(0, 0, 0), name="proj_a_dx")
    grad_x, _, d_nmix[0] = _rms_bwd(xs, nmix[0], da0, dh1, "mix_norm_bwd0")
    mem_bwd(dkv_a, 0)

    rep_grads = [jnp.concatenate(d_nmix, axis=0), jnp.concatenate(d_nffn, axis=0), jnp.concatenate(d_nmem, axis=0),
                 d_lng, d_lnb, d_ws.reshape(1, A_GROUPS, CHUNK, CHUNK), d_bs3.reshape(1, A_GROUPS, CHUNK),
                 d_dtbias[:, :SSM_HEADS], d_alog[:, :SSM_HEADS], d_dskip[:, :SSM_HEADS], d_fin.reshape(D_MODEL)]
    rep_w = [norm_mix, norm_ffn, mem_norm, a_ln_g, a_ln_b, a_ws, a_bs, b_dt_bias, b_a_log, b_d, final_norm]
    rep_grads = [g.reshape(w.shape) for g, w in zip(rep_grads, rep_w)]
    sh_grads = [d_convw, d_convb, d_gnorm]
    g_pack, g_layout = _pack(rep_grads + sh_grads + [loss_part])
    n_rep = len(rep_grads)
    (g_all,) = _all_gather_seq([g_pack], [("blk", 0, (N_DEV,) + g_pack.shape)], "ag_small_grads", after=got_ffn0)

    ga = _mm(a0, dproj_a, m=D_MODEL, n=5 * D_MODEL, k=s, ta=True, out_dtype=_ACT, after=[g_pack], name="proj_a_dw")
    reduce_scatter([("kv", 0, g_kv[0]), ("a", 0, ga)], "mix0", after=[g_all])

    def big_update(w, m, v, fam, nlayer):
        res = None
        for i in range(nlayer):
            part, recv2 = reduced[fam, i]
            plist = [(part, 0), (recv2, 0), (recv2, 1), (recv2, 2)]
            res = _adamw(w, m, v, plist, f"adamw_{fam}{i}", layer=i, prev=res)
        return res

    r_f1 = big_update(w_ffn1, m_w_ffn1, v_w_ffn1, "f1", 2)
    r_f2 = big_update(w_ffn2, m_w_ffn2, v_w_ffn2, "f2", 2)
    r_out = big_update(w_out, m_w_out, v_w_out, "out", 2)
    r_kv = big_update(w_kv, m_w_kv, v_w_kv, "kv", 2)
    r_a = big_update(a_in, m_a_in, v_a_in, "a", 1)
    r_b = [tr_b(o) for o in big_update(tr_b(b_in), tr_b(m_b_in), tr_b(v_b_in), "b", 1)]

    rep_names = ["norm_mix", "norm_ffn", "mem_norm", "a_ln_g", "a_ln_b", "a_ws", "a_bs", "b_dt_bias", "b_a_log", "b_d",
                 "final_norm"]
    rep_m = [m_norm_mix, m_norm_ffn, m_mem_norm, m_a_ln_g, m_a_ln_b, m_a_ws, m_a_bs, m_b_dt_bias, m_b_a_log, m_b_d, m_final_norm]
    rep_v = [v_norm_mix, v_norm_ffn, v_mem_norm, v_a_ln_g, v_a_ln_b, v_a_ws, v_a_bs, v_b_dt_bias, v_b_a_log, v_b_d, v_final_norm]
    g_small = _sum8(g_all, "sum_small_grads")
    g_list = _unpack(g_small, g_layout)
    loss = g_list[-1][0, 0]
    wp, w_layout = _pack(rep_w)
    mp, _ = _pack(rep_m)
    vp, _ = _pack(rep_v)
    gp, _ = _pack(g_list[:n_rep])
    rep_res = [_unpack(o, w_layout) for o in _adamw(wp, mp, vp, [(gp, None)], "adamw_replicated", tr=88)]

    gcw = lax.dynamic_slice_in_dim(g_list[n_rep], me * 384, 384, axis=1).reshape(1, CONV_K, 384)
    gcb = lax.dynamic_slice_in_dim(g_list[n_rep + 1], me * 384, 384, axis=1)
    ggn = lax.dynamic_slice_in_dim(g_list[n_rep + 2], me * 256, 256, axis=1)
    sh_w = [b_conv_w, b_conv_b, b_gnorm]
    sh_m = [m_b_conv_w, m_b_conv_b, m_b_gnorm]
    sh_v = [v_b_conv_w, v_b_conv_b, v_b_gnorm]
    swp, sw_layout = _pack(sh_w)
    smp, _ = _pack(sh_m)
    svp, _ = _pack(sh_v)
    sgp, _ = _pack([gcw, gcb, ggn])
    sh_res = [_unpack(o, sw_layout) for o in _adamw(swp, smp, svp, [(sgp, None)], "adamw_sharded_small", tr=8)]

    names = ["norm_mix", "norm_ffn", "mem_norm", "w_kv", "w_out", "w_ffn1", "w_ffn2", "a_in", "a_ln_g", "a_ln_b", "a_ws",
             "a_bs", "b_in", "b_conv_w", "b_conv_b", "b_dt_bias", "b_a_log", "b_d", "b_gnorm", "final_norm"]
    big = {"w_kv": r_kv, "w_out": r_out, "w_ffn1": r_f1, "w_ffn2": r_f2, "a_in": r_a, "b_in": r_b}
    sh_names = ["b_conv_w", "b_conv_b", "b_gnorm"]
    outs = [loss, grad_x.reshape(x.shape)]
    for kind in range(4):
        for nm in names:
            if nm in big:
                outs.append(big[nm][kind])
            elif nm in sh_names:
                outs.append(sh_res[kind][sh_names.index(nm)])
            else:
                outs.append(rep_res[kind][rep_names.index(nm)])
    return tuple(outs)
```

```python
import functools
import math

import jax
import jax.numpy as jnp
from jax import lax
from jax.experimental import pallas as pl
from jax.experimental.pallas import tpu as pltpu
from jax.experimental.pallas import tpu_sc as plsc

F32 = jnp.float32
_MXU = jnp.bfloat16
_ACT = jnp.bfloat16
_HI = lax.Precision.HIGHEST

D_MODEL = 1024
CHUNK = 128
N_MEM = 256
D_INNER = 2048
A_GROUPS = 8
A_GW = D_INNER // A_GROUPS
SSM_HEADS = 32
SSM_P = 64
SSM_GROUPS = 4
SSM_GW = D_INNER // SSM_GROUPS
SSM_N = 128
CONV_K = 4
CONV_DIM = 3072
X_HEADS = 4
X_HD = 256
X_WIDTH = 1024
D_FF = 4096
EPS = 1e-6
HPAD = 128
N_DEV = 8

ADAM_LR = 0.001
ADAM_B1 = 0.9
ADAM_B2 = 0.999
ADAM_EPS = 1e-08
ADAM_WD = 0.01
ADAM_STEP = 10

VMEM_BIG = 56 * 1024 * 1024
MESH = pl.DeviceIdType.MESH


def _cp(vmem=None):
    if vmem is None:
        return pltpu.CompilerParams()
    return pltpu.CompilerParams(vmem_limit_bytes=vmem)


def _dot(a, b, dims=((1,), (0,))):
    return lax.dot_general(a.astype(_MXU), b.astype(_MXU), (dims, ((), ())), preferred_element_type=F32)


def _dot_nt(a, b):
    return _dot(a, b, ((1,), (1,)))


def _dot_tn(a, b):
    return _dot(a, b, ((0,), (0,)))


def _dot_hi(a, b, dims=((1,), (0,))):
    return lax.dot_general(a.astype(F32), b.astype(F32), (dims, ((), ())), precision=_HI, preferred_element_type=F32)


def _split3(x):
    x1 = x.astype(jnp.bfloat16)
    r = x - x1.astype(F32)
    x2 = r.astype(jnp.bfloat16)
    x3 = (r - x2.astype(F32)).astype(jnp.bfloat16)
    return x1, x2, x3


def _dot_sel(x, sel, dims=((1,), (0,)), terms=2):
    sel = sel.astype(jnp.bfloat16)
    parts = [lax.dot_general(t, sel, (dims, ((), ())), preferred_element_type=F32) for t in _split3(x)[:terms]]
    return functools.reduce(lambda a, b: a + b, parts)


def _sel_dot(sel, x, dims=((1,), (0,))):
    sel = sel.astype(jnp.bfloat16)
    parts = [lax.dot_general(sel, t, (dims, ((), ())), preferred_element_type=F32) for t in _split3(x)]
    return (parts[0] + parts[1]) + parts[2]


def _sigmoid(x):
    return 1.0 / (1.0 + jnp.exp(-x))


def _gelu(x):
    return 0.5 * x * (1.0 + lax.erf(x * (1.0 / math.sqrt(2.0))))


def _gelu_grad(x):
    return 0.5 * (1.0 + lax.erf(x * (1.0 / math.sqrt(2.0)))) + x * jnp.exp(-0.5 * x * x) * (1.0 / math.sqrt(2.0 * math.pi))


def _softplus(x):
    return jnp.maximum(x, 0.0) + jnp.log1p(jnp.exp(-jnp.abs(x)))


def _iota(shape, dim):
    return lax.broadcasted_iota(jnp.int32, shape, dim)


MM_VMEM_BUDGET = 40 * 1024 * 1024
HBM_BYTES_PER_S = 2.5e12
GRID_STEP_S = 0.35e-6
VMEM_ACC_BYTES_PER_S = 6e12


def _divisors(dim, unit):
    out = [d for d in range(unit, min(dim, 2048) + 1, unit) if dim % d == 0]
    return out if out else [dim]


def _mm_tiles(m, n, k, sa, sb, s_mn, a_pro, offsets):
    best = None
    (a_r0, a_c0, ta), (b_r0, b_c0, tb), (o_r0, o_c0) = offsets
    for tm in _divisors(m, 128):
        for tn in _divisors(n, 128):
            for tk in [k // d for d in (1, 2, 3, 4, 6, 8) if k % d == 0 and (k // d) % 128 == 0]:
                a_t = (tk, tm) if ta else (tm, tk)
                b_t = (tn, tk) if tb else (tk, tn)
                if a_r0 % a_t[0] or a_c0 % a_t[1] or b_r0 % b_t[0] or b_c0 % b_t[1] or o_r0 % tm or o_c0 % tn:
                    continue
                nk = k // tk
                vmem = 2 * (tm * tk * sa + tk * tn * sb + tm * tn * s_mn) + tm * tn * 4 * (2 if nk > 1 else 1)
                if a_pro or sa == 4:
                    vmem += tm * tk * 6
                if sb == 4:
                    vmem += tk * tn * 2
                if vmem > MM_VMEM_BUDGET:
                    continue
                gi, gj = m // tm, n // tn
                for j_inner in (True, False):
                    if nk > 1:
                        traffic = gj * m * k * sa + gi * k * n * sb
                    elif j_inner:
                        traffic = m * k * sa + gi * k * n * sb
                    else:
                        traffic = gj * m * k * sa + k * n * sb
                    traffic += m * n * s_mn + (tm * tk * sa + tk * tn * sb)
                    cost = traffic / HBM_BYTES_PER_S + gi * gj * nk * GRID_STEP_S
                    if nk > 1:
                        cost += m * n * 8 * nk / VMEM_ACC_BYTES_PER_S
                    if best is None or cost < best[0]:
                        best = (cost, tm, tn, tk, j_inner)
    assert best is not None, (m, n, k)
    return best[1:]


def _mm(a, b, *, m, n, k, name, ta=False, tb=False, a_at=(None, 0, 0), b_at=(None, 0, 0),
        out_dtype=F32, add=None, epi_p=None, epi_at=(None, 0, 0), out=None, out_at=(None, 0, 0),
        out_full=None, a_pro=None, after=()):
    s_mn =jnp.dtype(out.dtype if out is not None else out_dtype).itemsize
    s_mn += add.dtype.itemsize if add is not None else 0
    s_mn += epi_p.dtype.itemsize if epi_p is not None else 0
    tm, tn, tk, j_inner = _mm_tiles(m, n, k, a.dtype.itemsize, b.dtype.itemsize, s_mn, a_pro is not None,
                                    ((a_at[1], a_at[2], ta), (b_at[1], b_at[2], tb), (out_at[1], out_at[2])))
    nk = k // tk

    def spec(at, tr, tc, rsel, csel):
        lead, r0, c0 = at
        assert r0 % tr == 0 and c0 % tc == 0, (name, at, tr, tc)
        rb, cb = r0 // tr, c0 // tc
        if lead is None:
            return pl.BlockSpec((tr, tc), lambda g0, g1, kk: (rb + rsel(g0, g1, kk), cb + csel(g0, g1, kk)))
        return pl.BlockSpec((None, tr, tc), lambda g0, g1, kk: (lead, rb + rsel(g0, g1, kk), cb + csel(g0, g1, kk)))

    gi = (lambda g0, g1, kk: g0) if j_inner else (lambda g0, g1, kk: g1)
    gj = (lambda g0, g1, kk: g1) if j_inner else (lambda g0, g1, kk: g0)
    gk = lambda g0, g1, kk: kk
    a_spec = spec(a_at, tk, tm, gk, gi) if ta else spec(a_at, tm, tk, gi, gk)
    b_spec = spec(b_at, tn, tk, gj, gk) if tb else spec(b_at, tk, tn, gk, gj)
    dims = ((0,), (0,)) if ta else (((1,), (1,)) if tb else ((1,), (0,)))
    assert not (ta and tb)

    operands, in_specs = [a, b], [a_spec, b_spec]
    if add is not None:
        operands.append(add)
        in_specs.append(spec((None, 0, 0), tm, tn, gi, gj))
    if epi_p is not None:
        operands.append(epi_p)
        in_specs.append(spec(epi_at, tm, tn, gi, gj))
    aliases = {}
    if out is not None:
        aliases = {len(operands): 0}
        operands.append(out)
        in_specs.append(pl.BlockSpec(memory_space=pl.ANY))
        out_struct = jax.ShapeDtypeStruct(out.shape, out.dtype)
        out_dtype = out.dtype
    else:
        out_struct = jax.ShapeDtypeStruct(out_full if out_full is not None else (m, n), out_dtype)
    has_add, has_epi = add is not None, epi_p is not None
    n_skip = (1 if out is not None else 0) + len(after)
    operands += list(after)
    in_specs += [pl.BlockSpec(memory_space=pl.ANY)] * len(after)

    def body(*refs):
        a_ref, b_ref = refs[0], refs[1]
        pos = 2
        add_ref = epi_ref = None
        if has_add:
            add_ref = refs[pos]
            pos += 1
        if has_epi:
            epi_ref = refs[pos]
            pos += 1
        pos += n_skip
        o_ref = refs[pos]

        def finish(r):
            if has_add:
                r = r + add_ref[...].astype(F32)
            if has_epi:
                r = r * (2.0 * jnp.maximum(epi_ref[...].astype(F32), 0.0))
            o_ref[...] = r.astype(o_ref.dtype)

        av = a_ref[...]
        if a_pro == "relu2":
            av = jnp.square(jnp.maximum(av.astype(F32), 0.0))
        part = _dot(av, b_ref[...], dims)
        if nk == 1:
            finish(part)
        else:
            acc_ref = refs[pos + 1]
            kk = pl.program_id(2)

            @pl.when(kk == 0)
            def _():
                acc_ref[...] = part

            @pl.when(kk > 0)
            def _():
                acc_ref[...] += part

            @pl.when(kk == nk - 1)
            def _():
                finish(acc_ref[...])

    grid = (m // tm, n // tn, nk) if j_inner else (n // tn, m // tm, nk)
    return pl.pallas_call(
        body, name=name, grid=grid, in_specs=in_specs,
        out_specs=spec(out_at, tm, tn, gi, gj), out_shape=out_struct,
        scratch_shapes=[pltpu.VMEM((tm, tn), F32)] if nk > 1 else [], input_output_aliases=aliases,
        compiler_params=_cp(VMEM_BIG))(*operands)


def _rms_fwd(x, g, name, tm=256):
    s, d = x.shape
    tm = min(tm, s)

    def body(x_ref, g_ref, o_ref):
        xv = x_ref[...]
        r = lax.rsqrt(jnp.mean(xv * xv, axis=-1, keepdims=True) + EPS)
        o_ref[...] = (xv * r * g_ref[...]).astype(o_ref.dtype)

    return pl.pallas_call(
        body, name=name, grid=(s // tm,),
        in_specs=[pl.BlockSpec((tm, d), lambda i: (i, 0)), pl.BlockSpec((1, d), lambda i: (0, 0))],
        out_specs=pl.BlockSpec((tm, d), lambda i: (i, 0)),
        out_shape=jax.ShapeDtypeStruct((s, d), _ACT))(x, g)


def _rms_bwd(x, g, dy, dres, name, tm=256):
    s, d = x.shape
    tm = min(tm, s)
    has_res = dres is not None

    def body(*refs):
        if has_res:
            x_ref, g_ref, dy_ref, dres_ref, dx_ref, dxa_ref, dg_ref = refs
        else:
            x_ref, g_ref, dy_ref, dx_ref, dxa_ref, dg_ref = refs

        @pl.when(pl.program_id(0) == 0)
        def _():
            dg_ref[...] = jnp.zeros_like(dg_ref)

        xv = x_ref[...]
        dyv = dy_ref[...].astype(F32)
        r = lax.rsqrt(jnp.mean(xv * xv, axis=-1, keepdims=True) + EPS)
        xh = xv * r
        dyg = dyv * g_ref[...]
        dx = r * (dyg - xh * jnp.mean(dyg * xh, axis=-1, keepdims=True))
        if has_res:
            dx = dx + dres_ref[...]
        dx_ref[...] = dx
        dxa_ref[...] = dx.astype(dxa_ref.dtype)
        dg_ref[...] += jnp.sum(dyv * xh, axis=0, keepdims=True)

    row = pl.BlockSpec((tm, d), lambda i: (i, 0))
    vec = pl.BlockSpec((1, d), lambda i: (0, 0))
    in_specs = [row, vec, row] + ([row] if has_res else [])
    operands = [x, g, dy] + ([dres] if has_res else [])
    return pl.pallas_call(
        body, name=name, grid=(s // tm,), in_specs=in_specs, out_specs=[row, row, vec],
        out_shape=[jax.ShapeDtypeStruct((s, d), F32), jax.ShapeDtypeStruct((s, d), _ACT),
                   jax.ShapeDtypeStruct((1, d), F32)])(*operands)


def _loss_head(h, g, target, name, tm=256):
    s, d = h.shape
    tm = min(tm, s)

    def body(h_ref, g_ref, t_ref, loss_ref, dh_ref, dha_ref, dg_ref):
        @pl.when(pl.program_id(0) == 0)
        def _():
            dg_ref[...] = jnp.zeros_like(dg_ref)
            loss_ref[...] = jnp.zeros_like(loss_ref)

        xv = h_ref[...]
        r = lax.rsqrt(jnp.mean(xv * xv, axis=-1, keepdims=True) + EPS)
        xh = xv * r
        err = xh * g_ref[...] - t_ref[...]
        loss_ref[...] += jnp.full(loss_ref.shape, 0.5 * jnp.sum(jnp.mean(err * err, axis=-1, keepdims=True)), F32)
        dyv = err * (1.0 / d)
        dyg = dyv * g_ref[...]
        dh = r * (dyg - xh * jnp.mean(dyg * xh, axis=-1, keepdims=True))
        dh_ref[...] = dh
        dha_ref[...] = dh.astype(dha_ref.dtype)
        dg_ref[...] += jnp.sum(dyv * xh, axis=0, keepdims=True)

    row = pl.BlockSpec((tm, d), lambda i: (i, 0))
    vec = pl.BlockSpec((1, d), lambda i: (0, 0))
    return pl.pallas_call(
        body, name=name, grid=(s // tm,), in_specs=[row, vec, row],
        out_specs=[pl.BlockSpec((1, 128), lambda i: (0, 0)), row, row, vec],
        out_shape=[jax.ShapeDtypeStruct((1, 128), F32), jax.ShapeDtypeStruct((s, d), F32),
                   jax.ShapeDtypeStruct((s, d), _ACT), jax.ShapeDtypeStruct((1, d), F32)])(h, g, target)


def _gmlp_parts(pu, pv, lng, lnb):
    u = _gelu(pu)
    v = _gelu(pv)
    mu = jnp.mean(v, axis=-1, keepdims=True)
    vc = v - mu
    rstd = lax.rsqrt(jnp.mean(vc * vc, axis=-1, keepdims=True) + EPS)
    xhat = vc * rstd
    vn = xhat * lng + lnb
    return u, xhat, rstd, vn


def _gmlp_fwd(proj, lng, lnb, ws, bs3, name):
    s = proj.shape[0]

    def body(pu_ref, pv_ref, lng_ref, lnb_ref, ws_ref, bs_ref, o_ref):
        u, _, _, vn = _gmlp_parts(pu_ref[...], pv_ref[...], lng_ref[...], lnb_ref[...])
        causal = _iota((CHUNK, CHUNK), 0) >= _iota((CHUNK, CHUNK), 1)
        for g in range(A_GROUPS):
            sl = slice(g * A_GW, (g + 1) * A_GW)
            w = jnp.where(causal, ws_ref[g], 0.0)
            sv = _dot(w, vn[:, sl]) + bs_ref[g]
            o_ref[:, sl] = (u[:, sl] * sv).astype(o_ref.dtype)

    full = lambda shape: pl.BlockSpec(shape, lambda c: (0,) * len(shape))
    return pl.pallas_call(
        body, name=name, grid=(s // CHUNK,),
        in_specs=[pl.BlockSpec((CHUNK, D_INNER), lambda c: (c, 0)), pl.BlockSpec((CHUNK, D_INNER), lambda c: (c, 1)),
                  full((1, D_INNER)), full((1, D_INNER)), full((A_GROUPS, CHUNK, CHUNK)), full((A_GROUPS, CHUNK, 1))],
        out_specs=pl.BlockSpec((CHUNK, D_INNER), lambda c: (c, 0)),
        out_shape=jax.ShapeDtypeStruct((s, D_INNER + X_WIDTH), _ACT), compiler_params=_cp(VMEM_BIG))(proj, proj, lng, lnb, ws, bs3)


def _gmlp_bwd(proj, dcat, lng, lnb, ws, bs3, name):
    s = proj.shape[0]

    def body(pu_ref, pv_ref, dm_ref, lng_ref, lnb_ref, ws_ref, bs_ref, dp_ref, dws_ref, dbs_ref, dlng_ref, dlnb_ref, dvn_ref):
        @pl.when(pl.program_id(0) == 0)
        def _():
            dws_ref[...] = jnp.zeros_like(dws_ref)
            dbs_ref[...] = jnp.zeros_like(dbs_ref)
            dlng_ref[...] = jnp.zeros_like(dlng_ref)
            dlnb_ref[...] = jnp.zeros_like(dlnb_ref)

        pu, pv = pu_ref[...], pv_ref[...]
        lng = lng_ref[...]
        u, xhat, rstd, vn = _gmlp_parts(pu, pv, lng, lnb_ref[...])
        dm = dm_ref[...].astype(F32)
        causal = _iota((CHUNK, CHUNK), 0) >= _iota((CHUNK, CHUNK), 1)
        for g in range(A_GROUPS):
            sl = slice(g * A_GW, (g + 1) * A_GW)
            w = jnp.where(causal, ws_ref[g], 0.0)
            sv = _dot(w, vn[:, sl]) + bs_ref[g]
            dsv = dm[:, sl] * u[:, sl]
            dp_ref[:, sl] = (dm[:, sl] * sv * _gelu_grad(pu[:, sl])).astype(dp_ref.dtype)
            dvn_ref[:, sl] = _dot_tn(w, dsv)
            dws_ref[g] += jnp.where(causal, _dot_nt(dsv, vn[:, sl]), 0.0)
            dbs_ref[g] += jnp.sum(dsv, axis=-1, keepdims=True)
        dvn = dvn_ref[...]
        dlng_ref[...] += jnp.sum(dvn * xhat, axis=0, keepdims=True)
        dlnb_ref[...] += jnp.sum(dvn, axis=0, keepdims=True)
        dxh = dvn * lng
        dv = rstd * (dxh - jnp.mean(dxh, axis=-1, keepdims=True) - xhat * jnp.mean(dxh * xhat, axis=-1, keepdims=True))
        dp_ref[:, D_INNER:] = (dv * _gelu_grad(pv)).astype(dp_ref.dtype)

    full = lambda shape: pl.BlockSpec(shape, lambda c: (0,) * len(shape))
    return pl.pallas_call(
        body, name=name, grid=(s // CHUNK,),
        in_specs=[pl.BlockSpec((CHUNK, D_INNER), lambda c: (c, 0)), pl.BlockSpec((CHUNK, D_INNER), lambda c: (c, 1)),
                  pl.BlockSpec((CHUNK, D_INNER), lambda c: (c, 0)),
                  full((1, D_INNER)), full((1, D_INNER)), full((A_GROUPS, CHUNK, CHUNK)), full((A_GROUPS, CHUNK, 1))],
        out_specs=[pl.BlockSpec((CHUNK, 2 * D_INNER), lambda c: (c, 0)), full((A_GROUPS, CHUNK, CHUNK)),
                   full((A_GROUPS, CHUNK, 1)), full((1, D_INNER)), full((1, D_INNER))],
        out_shape=[jax.ShapeDtypeStruct((s, 2 * D_INNER + X_WIDTH), _ACT), jax.ShapeDtypeStruct((A_GROUPS, CHUNK, CHUNK), F32),
                   jax.ShapeDtypeStruct((A_GROUPS, CHUNK, 1), F32), jax.ShapeDtypeStruct((1, D_INNER), F32),
                   jax.ShapeDtypeStruct((1, D_INNER), F32)],
        scratch_shapes=[pltpu.VMEM((CHUNK, D_INNER), F32)],
        compiler_params=_cp(VMEM_BIG))(proj, proj, dcat, lng, lnb, ws, bs3)


_X_SCALE = 1.0 / math.sqrt(X_HD)


def _attn_fwd(proj, qblk, kv, cat, name, tm=256):
    s = proj.shape[0]
    tm = min(tm, s)

    def body(q_ref, kv_ref, cat_ref, o_ref):
        for h in range(X_HEADS):
            sl = slice(h * X_HD, (h + 1) * X_HD)
            k = kv_ref[:, sl]
            v = kv_ref[:, X_WIDTH + h * X_HD:X_WIDTH + (h + 1) * X_HD]
            sc = _dot_nt(q_ref[:, sl], k) * _X_SCALE
            e = jnp.exp(sc - jnp.max(sc, axis=-1, keepdims=True))
            p = e / jnp.sum(e, axis=-1, keepdims=True)
            o_ref[:, sl] = _dot(p, v).astype(o_ref.dtype)

    return pl.pallas_call(
        body, name=name, grid=(s // tm,),
        in_specs=[pl.BlockSpec((tm, X_WIDTH), lambda i: (i, qblk)), pl.BlockSpec((N_MEM, 2 * X_WIDTH), lambda i: (0, 0)),
                  pl.BlockSpec(memory_space=pl.ANY)],
        out_specs=pl.BlockSpec((tm, X_WIDTH), lambda i: (i, D_INNER // X_WIDTH)),
        out_shape=jax.ShapeDtypeStruct(cat.shape, cat.dtype), input_output_aliases={2: 0})(proj, kv, cat)


def _attn_bwd(proj, qblk, kv, dcat, dproj, name, tm=256):
    s = proj.shape[0]
    tm = min(tm, s)

    def body(q_ref, kv_ref, do_ref, dproj_ref, dq_ref, dkv_ref):
        @pl.when(pl.program_id(0) == 0)
        def _():
            dkv_ref[...] = jnp.zeros_like(dkv_ref)

        for h in range(X_HEADS):
            sl = slice(h * X_HD, (h + 1) * X_HD)
            slv = slice(X_WIDTH + h * X_HD, X_WIDTH + (h + 1) * X_HD)
            q = q_ref[:, sl]
            k = kv_ref[:, sl]
            v = kv_ref[:, slv]
            do = do_ref[:, sl].astype(F32)
            sc = _dot_nt(q, k) * _X_SCALE
            e = jnp.exp(sc - jnp.max(sc, axis=-1, keepdims=True))
            p = e / jnp.sum(e, axis=-1, keepdims=True)
            dp = _dot_nt(do, v)
            ds = p * (dp - jnp.sum(dp * p, axis=-1, keepdims=True)) * _X_SCALE
            dq_ref[:, sl] = _dot(ds, k).astype(dq_ref.dtype)
            dkv_ref[:, sl] += _dot_tn(ds, q)
            dkv_ref[:, slv] += _dot_tn(p, do)

    return pl.pallas_call(
        body, name=name, grid=(s // tm,),
        in_specs=[pl.BlockSpec((tm, X_WIDTH), lambda i: (i, qblk)), pl.BlockSpec((N_MEM, 2 * X_WIDTH), lambda i: (0, 0)),
                  pl.BlockSpec((tm, X_WIDTH), lambda i: (i, 2)), pl.BlockSpec(memory_space=pl.ANY)],
        out_specs=[pl.BlockSpec((tm, X_WIDTH), lambda i: (i, qblk)), pl.BlockSpec((N_MEM, 2 * X_WIDTH), lambda i: (0, 0))],
        out_shape=[jax.ShapeDtypeStruct(dproj.shape, dproj.dtype), jax.ShapeDtypeStruct((N_MEM, 2 * X_WIDTH), F32)],
        input_output_aliases={3: 0})(proj, kv, dcat, dproj)


CONV_TC = 256
_XBC_BLK0 = D_INNER // CONV_TC


CONV_RB = 64
SUBLANES = 8


def _rows_before(cur, prev_last, j):
    rolled = pltpu.roll(cur, j, 0)
    head = jnp.where(_iota((SUBLANES, cur.shape[1]), 0) < j, pltpu.roll(prev_last, j, 0), rolled[:SUBLANES])
    return jnp.concatenate([head, rolled[SUBLANES:]], axis=0)


def _rows_after(cur, next_first, j):
    n = cur.shape[0]
    rolled = pltpu.roll(cur, n - j, 0)
    tail = jnp.where(_iota((SUBLANES, cur.shape[1]), 0) >= SUBLANES - j, pltpu.roll(next_first, SUBLANES - j, 0),
                     rolled[n - SUBLANES:])
    return jnp.concatenate([rolled[:n - SUBLANES], tail], axis=0)


def _conv_pre(x_ref, w_ref, b_ref, r0, prev_last):
    cur = x_ref[pl.ds(r0, CONV_RB), :]
    shifts = [_rows_before(cur, prev_last, j) for j in range(1, CONV_K)]
    pre = b_ref[...] + w_ref[CONV_K - 1:CONV_K, :] * cur
    for j in range(1, CONV_K):
        pre = pre + w_ref[CONV_K - 1 - j:CONV_K - j, :] * shifts[j - 1]
    return pre, cur, shifts


def _conv_fwd(proj, w, b, name):
    s = proj.shape[0]

    def body(x_ref, w_ref, b_ref, o_ref):
        xv = x_ref[...]
        rows = _iota(xv.shape, 0)
        pre = b_ref[...] + w_ref[CONV_K - 1:CONV_K, :] * xv
        for j in range(1, CONV_K):
            pre = pre + w_ref[CONV_K - 1 - j:CONV_K - j, :] * jnp.where(rows >= j, pltpu.roll(xv, j, 0), 0.0)
        o_ref[...] = pre * _sigmoid(pre)

    return pl.pallas_call(
        body, name=name, grid=(CONV_DIM // CONV_TC,),
        in_specs=[pl.BlockSpec((s, CONV_TC), lambda j: (0, _XBC_BLK0 + j)), pl.BlockSpec((CONV_K, CONV_TC), lambda j: (0, j)),
                  pl.BlockSpec((1, CONV_TC), lambda j: (0, j))],
        out_specs=pl.BlockSpec((s, CONV_TC), lambda j: (0, j)),
        out_shape=jax.ShapeDtypeStruct((s, CONV_DIM), F32), compiler_params=_cp(VMEM_BIG))(proj, w, b)


def _conv_bwd(proj, w, b, dxbc, dproj, name):
    s = proj.shape[0]

    nb = s // CONV_RB

    def body(x_ref, w_ref, b_ref, d_ref, dproj_ref, dx_ref, dw_ref, db_ref, dpre_ref):
        def fold(v):
            out = v[:SUBLANES]
            for t in range(1, CONV_RB // SUBLANES):
                out = out + v[t * SUBLANES:(t + 1) * SUBLANES]
            return out

        def first(i, carry):
            prev_last, acc = carry
            r0 = pl.multiple_of(i * CONV_RB, CONV_RB)
            pre, cur, shifts = _conv_pre(x_ref, w_ref, b_ref, r0, prev_last)
            sig = _sigmoid(pre)
            dpre = d_ref[pl.ds(r0, CONV_RB), :] * (sig * (1.0 + pre * (1.0 - sig)))
            dpre_ref[pl.ds(r0, CONV_RB), :] = dpre
            taps = [cur] + shifts
            acc = tuple(a + fold(dpre * t) for a, t in zip(acc[:CONV_K], taps)) + (acc[CONV_K] + fold(dpre),)
            return cur[CONV_RB - SUBLANES:], acc

        zero8 = jnp.zeros((SUBLANES, CONV_TC), F32)
        _, acc = lax.fori_loop(0, nb, first, (zero8, (zero8,) * (CONV_K + 1)))
        for j in range(CONV_K):
            dw_ref[CONV_K - 1 - j:CONV_K - j, :] = jnp.sum(acc[j], axis=0, keepdims=True)
        db_ref[...] = jnp.sum(acc[CONV_K], axis=0, keepdims=True)

        def second(i, next_first):
            r0 = pl.multiple_of((nb - 1 - i) * CONV_RB, CONV_RB)
            cur = dpre_ref[pl.ds(r0, CONV_RB), :]
            dx = w_ref[CONV_K - 1:CONV_K, :] * cur
            for j in range(1, CONV_K):
                dx = dx + w_ref[CONV_K - 1 - j:CONV_K - j, :] * _rows_after(cur, next_first, j)
            dx_ref[pl.ds(r0, CONV_RB), :] = dx.astype(dx_ref.dtype)
            return cur[:SUBLANES]

        lax.fori_loop(0, nb, second, zero8)

    return pl.pallas_call(
        body, name=name, grid=(CONV_DIM // CONV_TC,),
        in_specs=[pl.BlockSpec((s, CONV_TC), lambda j: (0, _XBC_BLK0 + j)), pl.BlockSpec((CONV_K, CONV_TC), lambda j: (0, j)),
                  pl.BlockSpec((1, CONV_TC), lambda j: (0, j)), pl.BlockSpec((s, CONV_TC), lambda j: (0, j)),
                  pl.BlockSpec(memory_space=pl.ANY)],
        out_specs=[pl.BlockSpec((s, CONV_TC), lambda j: (0, _XBC_BLK0 + j)), pl.BlockSpec((CONV_K, CONV_TC), lambda j: (0, j)),
                   pl.BlockSpec((1, CONV_TC), lambda j: (0, j))],
        out_shape=[jax.ShapeDtypeStruct(dproj.shape, dproj.dtype), jax.ShapeDtypeStruct((CONV_K, CONV_DIM), F32),
                   jax.ShapeDtypeStruct((1, CONV_DIM), F32)], input_output_aliases={4: 0},
        scratch_shapes=[pltpu.VMEM((s, CONV_TC), F32)],
        compiler_params=_cp(VMEM_BIG))(proj, w, b, dxbc, dproj)


def _ssd_common(dtc_ref, br_ref, ar_ref, csb_ref, cst_ref, csf_ref):
    a_row = -jnp.exp(ar_ref[...])
    dt_c = _softplus(dtc_ref[...] + br_ref[...])
    tril = _iota((CHUNK, CHUNK), 0) >= _iota((CHUNK, CHUNK), 1)
    cs = _sel_dot(tril, dt_c * a_row)
    cst_ref[...] = cs.T
    e64 = (jnp.right_shift(_iota((HPAD, D_INNER), 1), 6) == _iota((HPAD, D_INNER), 0)).astype(jnp.bfloat16)
    e128 = jnp.right_shift(_iota((HPAD, SSM_HEADS * CHUNK), 1), 7) == _iota((HPAD, SSM_HEADS * CHUNK), 0)
    csb_ref[...] = _dot_sel(cs, e128)
    dt_full = _dot_sel(dt_c, e64)
    csf_ref[...] = _dot_sel(cs, e64)
    cs_full = csf_ref[...]
    cs_last = csf_ref[CHUNK - 1:CHUNK, :]
    e_full = jnp.exp(cs_full)
    f_full = jnp.exp(cs_last - cs_full)
    gamma = jnp.exp(cs_last)
    return a_row, dt_c, cs, dt_full, e_full, f_full, gamma, e64


def _ssd_lambda(csb_ref, cst_ref, h, causal):
    diff = csb_ref[:, h * CHUNK:(h + 1) * CHUNK] - cst_ref[h:h + 1, :]
    return jnp.exp(jnp.where(causal, diff, -1e30))


_SSD_VEC_SPECS = lambda: [pl.BlockSpec((1, HPAD), lambda c: (0, 0)), pl.BlockSpec((1, HPAD), lambda c: (0, 0)),
                          pl.BlockSpec((1, D_INNER), lambda c: (0, 0))]


def _ssd_fwd(xbc, dtc, bias_row, alog_row, dfull, name):
    s = xbc.shape[0]
    nc = s // CHUNK

    def body(xbc_ref, dtc_ref, br_ref, ar_ref, df_ref, y_ref, st_ref, ht_ref, csb_ref, cst_ref, csf_ref):
        @pl.when(pl.program_id(0) == 0)
        def _():
            ht_ref[...] = jnp.zeros_like(ht_ref)

        _, _, _, dt_full, e_full, f_full, gamma, _ = _ssd_common(dtc_ref, br_ref, ar_ref, csb_ref, cst_ref, csf_ref)
        x = xbc_ref[:, :D_INNER]
        xdt = x * dt_full
        st_ref[...] = ht_ref[...]
        causal = _iota((CHUNK, CHUNK), 0) >= _iota((CHUNK, CHUNK), 1)
        lo = _iota((CHUNK, CHUNK), 1) < SSM_P
        for g in range(SSM_GROUPS):
            gs = slice(g * SSM_GW, (g + 1) * SSM_GW)
            bg = xbc_ref[:, D_INNER + g * SSM_N:D_INNER + (g + 1) * SSM_N]
            cg = xbc_ref[:, D_INNER + SSM_GROUPS * SSM_N + g * SSM_N:D_INNER + SSM_GROUPS * SSM_N + (g + 1) * SSM_N]
            ht = ht_ref[:, gs]
            cb = _dot_nt(cg, bg)
            yoff = e_full[:, gs] * _dot(cg, ht)
            for jp in range(SSM_GW // CHUNK):
                j = g * (SSM_GW // CHUNK) + jp
                ps = slice(j * CHUNK, (j + 1) * CHUNK)
                x2 = xdt[:, ps]
                y0 = _dot(cb * _ssd_lambda(csb_ref, cst_ref, 2 * j, causal), x2)
                y1 = _dot(cb * _ssd_lambda(csb_ref, cst_ref, 2 * j + 1, causal), x2)
                y_ref[:, ps] = (jnp.where(lo, y0, y1) + yoff[:, jp * CHUNK:(jp + 1) * CHUNK]
                                + x[:, ps] * df_ref[:, ps])
            ht_ref[:, gs] = gamma[:, gs] * ht + _dot_tn(bg, xdt[:, gs] * f_full[:, gs])

    return pl.pallas_call(
        body, name=name, grid=(nc,),
        in_specs=[pl.BlockSpec((CHUNK, CONV_DIM), lambda c: (c, 0)), pl.BlockSpec((CHUNK, HPAD), lambda c: (c, 0))]
                 + _SSD_VEC_SPECS(),
        out_specs=[pl.BlockSpec((CHUNK, D_INNER), lambda c: (c, 0)), pl.BlockSpec((None, SSM_N, D_INNER), lambda c: (c, 0, 0))],
        out_shape=[jax.ShapeDtypeStruct((s, D_INNER), F32), jax.ShapeDtypeStruct((nc, SSM_N, D_INNER), F32)],
        scratch_shapes=[pltpu.VMEM((SSM_N, D_INNER), F32), pltpu.VMEM((CHUNK, SSM_HEADS * CHUNK), F32),
                        pltpu.VMEM((HPAD, CHUNK), F32), pltpu.VMEM((CHUNK, D_INNER), F32)],
        compiler_params=_cp(VMEM_BIG))(xbc, dtc, bias_row, alog_row, dfull)


def _ssd_bwd(xbc, dtc, bias_row, alog_row, dfull, dy, states, name):
    s = xbc.shape[0]
    nc = s // CHUNK
    rev = lambda c: nc - 1 - c

    def body(xbc_ref, dtc_ref, br_ref, ar_ref, df_ref, dy_ref, st_ref,
             dxbc_ref, ddt_ref, dalog_ref, dd_ref, dbias_ref,
             dht_ref, csb_ref, cst_ref, csf_ref, ddf_ref, dxs_ref, dcsf_ref, dcsl_ref):
        step = pl.program_id(0)

        @pl.when(step == 0)
        def _():
            dht_ref[...] = jnp.zeros_like(dht_ref)
            ddf_ref[...] = jnp.zeros_like(ddf_ref)
            dalog_ref[...] = jnp.zeros_like(dalog_ref)
            dbias_ref[...] = jnp.zeros_like(dbias_ref)
            dd_ref[...] = jnp.zeros_like(dd_ref)

        a_row, dt_c, _, dt_full, e_full, f_full, gamma, e64 = _ssd_common(dtc_ref, br_ref, ar_ref, csb_ref, cst_ref, csf_ref)
        x = xbc_ref[:, :D_INNER]
        xdt = x * dt_full
        dy_all = dy_ref[...]
        ddf_ref[...] += jnp.broadcast_to(jnp.sum(dy_all * x, axis=0, keepdims=True), ddf_ref.shape)
        causal = _iota((CHUNK, CHUNK), 0) >= _iota((CHUNK, CHUNK), 1)
        lo = _iota((CHUNK, CHUNK), 1) < SSM_P
        head_lane = _iota((CHUNK, HPAD), 1)
        head_row = _iota((HPAD, CHUNK), 0)
        dcs_heads = jnp.zeros((CHUNK, HPAD), F32)
        dcs_cols = jnp.zeros((HPAD, CHUNK), F32)
        for g in range(SSM_GROUPS):
            gs = slice(g * SSM_GW, (g + 1) * SSM_GW)
            b0 = D_INNER + g * SSM_N
            c0 = D_INNER + SSM_GROUPS * SSM_N + g * SSM_N
            bg = xbc_ref[:, b0:b0 + SSM_N]
            cg = xbc_ref[:, c0:c0 + SSM_N]
            ht = st_ref[:, gs]
            dht = dht_ref[:, gs]
            dyg = dy_all[:, gs]
            eg, fg, gg = e_full[:, gs], f_full[:, gs], gamma[:, gs]
            z = _dot(cg, ht)
            dz = dyg * eg
            dcg = _dot_nt(dz, ht)
            dht_new = _dot_tn(cg, dz) + gg * dht
            xf = xdt[:, gs] * fg
            dxf = _dot(bg, dht)
            dbg = _dot_nt(xf, dht)
            dff = dxf * xf
            dcsf_ref[:, gs] = dyg * eg * z - dff
            dcsl_ref[:, gs] = jnp.broadcast_to(
                jnp.sum(dff, axis=0, keepdims=True) + jnp.sum(dht * ht, axis=0, keepdims=True) * gg, (8, SSM_GW))
            cb = _dot_nt(cg, bg)
            dcb = jnp.zeros((CHUNK, CHUNK), F32)
            for jp in range(SSM_GW // CHUNK):
                j = g * (SSM_GW // CHUNK) + jp
                ps = slice(j * CHUNK, (j + 1) * CHUNK)
                x2 = xdt[:, ps]
                dy2 = dy_all[:, ps]
                dxh = []
                for hh in range(2):
                    h = 2 * j + hh
                    lam = _ssd_lambda(csb_ref, cst_ref, h, causal)
                    mh = cb * lam
                    dyh = jnp.where(lo, dy2, 0.0) if hh == 0 else jnp.where(lo, 0.0, dy2)
                    dm = _dot_nt(dyh, x2)
                    dcb = dcb + dm * lam
                    gm = dm * mh
                    dcs_heads = dcs_heads + jnp.where(head_lane == h, jnp.sum(gm, axis=1, keepdims=True), 0.0)
                    dcs_cols = dcs_cols + jnp.where(head_row == h, jnp.sum(gm, axis=0, keepdims=True), 0.0)
                    dxh.append(_dot_tn(mh, dy2))
                dxs_ref[:, ps] = jnp.where(lo, dxh[0], dxh[1]) + dxf[:, jp * CHUNK:(jp + 1) * CHUNK] * fg[:, jp * CHUNK:(jp + 1) * CHUNK]
            dxbc_ref[:, b0:b0 + SSM_N] = (dbg + _dot_tn(dcb, cg)).astype(dxbc_ref.dtype)
            dxbc_ref[:, c0:c0 + SSM_N] = (dcg + _dot(dcb, bg)).astype(dxbc_ref.dtype)
            dht_ref[:, gs] = dht_new
        dxs = dxs_ref[...]
        dcs_heads = dcs_heads - dcs_cols.T + _dot_sel(dcsf_ref[...], e64, ((1,), (1,)))
        dcs_last = _dot_sel(dcsl_ref[...], e64, ((1,), (1,)))
        dcs_heads = dcs_heads + jnp.where(_iota((CHUNK, HPAD), 0) == CHUNK - 1, dcs_last[0:1, :], 0.0)
        triu = _iota((CHUNK, CHUNK), 0) <= _iota((CHUNK, CHUNK), 1)
        dda = _sel_dot(triu, dcs_heads)
        ddt = dda * a_row + _dot_sel(dxs * x, e64, ((1,), (1,)))
        dxbc_ref[:, :D_INNER] = (dxs * dt_full + dy_all * df_ref[...]).astype(dxbc_ref.dtype)
        dalog_ref[...] += jnp.sum(dda * dt_c, axis=0, keepdims=True) * a_row
        ddt_raw = ddt * _sigmoid(dtc_ref[...] + br_ref[...])
        ddt_ref[...] = ddt_raw.astype(ddt_ref.dtype)
        dbias_ref[...] += jnp.sum(ddt_raw, axis=0, keepdims=True)

        @pl.when(step == nc - 1)
        def _():
            dd_ref[...] = _dot_sel(ddf_ref[...], e64, ((1,), (1,)))[0:1, :]

    vec = pl.BlockSpec((1, HPAD), lambda c: (0, 0))
    return pl.pallas_call(
        body, name=name, grid=(nc,),
        in_specs=[pl.BlockSpec((CHUNK, CONV_DIM), lambda c: (rev(c), 0)), pl.BlockSpec((CHUNK, HPAD), lambda c: (rev(c), 0))]
                 + _SSD_VEC_SPECS()
                 + [pl.BlockSpec((CHUNK, D_INNER), lambda c: (rev(c), 0)),
                    pl.BlockSpec((None, SSM_N, D_INNER), lambda c: (rev(c), 0, 0))],
        out_specs=[pl.BlockSpec((CHUNK, CONV_DIM), lambda c: (rev(c), 0)), pl.BlockSpec((CHUNK, HPAD), lambda c: (rev(c), 0)),
                   vec, vec, vec],
        out_shape=[jax.ShapeDtypeStruct((s, CONV_DIM), F32), jax.ShapeDtypeStruct((s, HPAD), _ACT),
                   jax.ShapeDtypeStruct((1, HPAD), F32), jax.ShapeDtypeStruct((1, HPAD), F32),
                   jax.ShapeDtypeStruct((1, HPAD), F32)],
        scratch_shapes=[pltpu.VMEM((SSM_N, D_INNER), F32), pltpu.VMEM((CHUNK, SSM_HEADS * CHUNK), F32),
                        pltpu.VMEM((HPAD, CHUNK), F32), pltpu.VMEM((CHUNK, D_INNER), F32),
                        pltpu.VMEM((8, D_INNER), F32), pltpu.VMEM((CHUNK, D_INNER), F32),
                        pltpu.VMEM((CHUNK, D_INNER), F32), pltpu.VMEM((8, D_INNER), F32)],
        compiler_params=_cp(VMEM_BIG))(xbc, dtc, bias_row, alog_row, dfull, dy, states)


def _gate_fwd(y, proj, gn, name, tm=256):
    s = y.shape[0]
    tm = min(tm, s)

    def body(y_ref, z_ref, gn_ref, o_ref):
        for g in range(SSM_GROUPS):
            gs = slice(g * SSM_GW, (g + 1) * SSM_GW)
            z = z_ref[:, gs]
            t = y_ref[:, gs] * (z * _sigmoid(z))
            r = lax.rsqrt(jnp.mean(t * t, axis=-1, keepdims=True) + EPS)
            o_ref[:, gs] = (t * r * gn_ref[:, gs]).astype(o_ref.dtype)

    row = pl.BlockSpec((tm, D_INNER), lambda i: (i, 0))
    return pl.pallas_call(
        body, name=name, grid=(s // tm,), in_specs=[row, row, pl.BlockSpec((1, D_INNER), lambda i: (0, 0))],
        out_specs=row, out_shape=jax.ShapeDtypeStruct((s, D_INNER + X_WIDTH), _ACT))(y, proj, gn)


def _gate_bwd(y, proj, gn, dcat, name, tm=256):
    s = y.shape[0]
    tm = min(tm, s)

    def body(y_ref, z_ref, gn_ref, dm_ref, dy_ref, dz_ref, dgn_ref):
        @pl.when(pl.program_id(0) == 0)
        def _():
            dgn_ref[...] = jnp.zeros_like(dgn_ref)

        for g in range(SSM_GROUPS):
            gs = slice(g * SSM_GW, (g + 1) * SSM_GW)
            z = z_ref[:, gs]
            yv = y_ref[:, gs]
            sig = _sigmoid(z)
            sz = z * sig
            t = yv * sz
            r = lax.rsqrt(jnp.mean(t * t, axis=-1, keepdims=True) + EPS)
            th = t * r
            dm = dm_ref[:, gs].astype(F32)
            dmg = dm * gn_ref[:, gs]
            dt_ = r * (dmg - th * jnp.mean(dmg * th, axis=-1, keepdims=True))
            dgn_ref[:, gs] += jnp.sum(dm * th, axis=0, keepdims=True)
            dy_ref[:, gs] = dt_ * sz
            dz_ref[:, gs] = (dt_ * yv * (sig * (1.0 + z * (1.0 - sig)))).astype(dz_ref.dtype)

    row = pl.BlockSpec((tm, D_INNER), lambda i: (i, 0))
    vec = pl.BlockSpec((1, D_INNER), lambda i: (0, 0))
    return pl.pallas_call(
        body, name=name, grid=(s // tm,), in_specs=[row, row, vec, row], out_specs=[row, row, vec],
        out_shape=[jax.ShapeDtypeStruct((s, D_INNER), F32), jax.ShapeDtypeStruct((s, 6 * D_MODEL), _ACT),
                   jax.ShapeDtypeStruct((1, D_INNER), F32)])(y, proj, gn, dcat)


def _block_of(kind, width):
    if kind == "col":
        return lambda ref, j: ref.at[:, :, pl.ds(pl.multiple_of(j * width, 128), width)]
    if kind == "row":
        return lambda ref, j: ref.at[:, pl.ds(pl.multiple_of(j * width, 8), width), :]
    return lambda ref, j: ref.at[j]


def _coords():
    return lax.axis_index("x"), lax.axis_index("y"), lax.axis_index("c")


def _rel_chip(x, y, k):
    return (1 - x if k & 1 else x), (1 - y if k & 2 else y)


def _all_gather_body(ins, outs, send_sems, recv_sems, local_sems, blocks):
    n = len(ins)
    x, y, c = _coords()
    sibling = (x, y, 1 - c)
    via = (x + (1 - c) * (1 - 2 * x), y + c * (1 - 2 * y))
    onto = (x + c * (1 - 2 * x), y + (1 - c) * (1 - 2 * y))

    def copy(t, k, chip, core, to, src=None):
        dst = blocks[t](outs[t], 4 * chip[0] + 2 * chip[1] + core)
        return pltpu.make_async_remote_copy(
            src_ref=dst if src is None else src, dst_ref=dst, send_sem=send_sems.at[t, k],
            recv_sem=recv_sems.at[t, k], device_id=to, device_id_type=MESH)

    started = []
    for t in range(n):
        mine = pltpu.make_async_copy(ins[t], blocks[t](outs[t], 4 * x + 2 * y + c), local_sems.at[t])
        mine.start()
        started.append(mine)
    sends = []
    for t in range(n):
        for k in range(3):
            px, py = _rel_chip(x, y, k)
            cp = copy(t, k, (x, y), c, (px, py, 1 - c if k == 0 else c), src=ins[t])
            cp.start()
            sends.append(cp)
    for t in range(n):
        for k in (1, 2):
            chip = _rel_chip(x, y, k)
            copy(t, k, chip, c, sibling).wait_recv()
            fwd = copy(t, 3 + k, chip, c, sibling)
            fwd.start()
            sends.append(fwd)
        hop = copy(t, 3, via, c, (*onto, c))
        hop.start()
        sends.append(hop)
    for t in range(n):
        diagonal = _rel_chip(x, y, 3)
        copy(t, 3, diagonal, c, sibling).wait_recv()
        fwd = copy(t, 6, diagonal, c, sibling)
        fwd.start()
        sends.append(fwd)
    for t in range(n):
        copy(t, 0, (x, y), 1 - c, sibling).wait_recv()
        for k in range(1, 4):
            copy(t, 3 + k, _rel_chip(x, y, k), 1 - c, sibling).wait_recv()
    for cp in sends:
        cp.wait_send()
    for mine in started:
        mine.wait()


def _handshake(peers):
    barrier = pltpu.get_barrier_semaphore()
    for peer in peers:
        pl.semaphore_signal(barrier, inc=1, device_id=peer, device_id_type=MESH)
    pl.semaphore_wait(barrier, len(peers))


def _gather_peers():
    x, y, c = _coords()
    return [(x, y, 1 - c)] + [(*_rel_chip(x, y, k), c) for k in (1, 2)]


SEQ_ID_GATHER, SEQ_ID_SIBLING, SEQ_ID_CHIPS = 1, 2, 3


def _sequencer_call(body, peers, operands, out_types, sems, name, collective_id, after=()):
    n_in, n_out, n_after = len(operands), len(out_types), len(after)

    def launch(*refs):
        _handshake(peers())
        body(refs[:n_in], refs[n_in + n_after:n_in + n_after + n_out], *refs[n_in + n_after + n_out:])

    return pl.kernel(
        launch, name=name, out_type=out_types, mesh=plsc.ScalarSubcoreMesh(axis_name="seq", num_cores=1),
        scratch_types=sems, compiler_params=pltpu.CompilerParams(collective_id=collective_id))(*operands, *after)


def _all_gather_seq(shards, layouts, name, after=()):
    n = len(shards)
    blocks = [_block_of(kind, width) for kind, width, _ in layouts]
    return _sequencer_call(
        lambda ins, outs, *sems: _all_gather_body(ins, outs, *sems, blocks), _gather_peers, shards,
        [jax.ShapeDtypeStruct(shape, sh.dtype) for sh, (_, _, shape) in zip(shards, layouts)],
        [pltpu.SemaphoreType.DMA((n, 7)), pltpu.SemaphoreType.DMA((n, 7)), pltpu.SemaphoreType.DMA((n,))],
        name, SEQ_ID_GATHER, after)


def _tie(small, after, name):
    del name
    return lax.optimization_barrier((small, *after))[0]


def _rs_to_sibling(grads, layouts, name, after=()):
    n = len(grads)
    blocks = [_block_of(kind, width) for kind, width, _ in layouts]

    def body(ins, outs, send_sems, recv_sems):
        x, y, c = _coords()
        sibling = (x, y, 1 - c)
        cps = []
        for t in range(n):
            for k in range(4):
                px, py = _rel_chip(x, y, k)
                cp = pltpu.make_async_remote_copy(
                    src_ref=blocks[t](ins[t], 4 * px + 2 * py + (1 - c)), dst_ref=outs[t].at[k],
                    send_sem=send_sems.at[t, k], recv_sem=recv_sems.at[t, k], device_id=sibling, device_id_type=MESH)
                cp.start()
                cps.append(cp)
        for cp in cps:
            cp.wait_recv()
        for cp in cps:
            cp.wait_send()

    def sibling_only():
        x, y, c = _coords()
        return [(x, y, 1 - c)]

    return _sequencer_call(
        body, sibling_only, grads,
        [jax.ShapeDtypeStruct((4,) + shape, g.dtype) for g, (_, _, shape) in zip(grads, layouts)],
        [pltpu.SemaphoreType.DMA((n, 4)), pltpu.SemaphoreType.DMA((n, 4))], name, SEQ_ID_SIBLING, after)


def _rs_chip_sum(grad, recv, layout, xyc, name):
    kind, width, shape = layout
    r, ccols = shape

    def src_index(k, xyc_ref):
        px = jnp.where(k % 2 == 1, 1 - xyc_ref[0], xyc_ref[0])
        py = jnp.where(k // 2 == 1, 1 - xyc_ref[1], xyc_ref[1])
        return 4 * px + 2 * py + xyc_ref[2]

    if kind == "col":
        g_spec = pl.BlockSpec((r, ccols), lambda k, s_: (0, src_index(k, s_)))
    elif kind == "row":
        g_spec = pl.BlockSpec((r, ccols), lambda k, s_: (src_index(k, s_), 0))
    else:
        g_spec = pl.BlockSpec((None, r, ccols), lambda k, s_: (src_index(k, s_), 0, 0))

    def body(xyc_ref, g_ref, r_ref, o_ref):
        o_ref[...] = (g_ref[...].astype(F32) + r_ref[...].astype(F32)).astype(o_ref.dtype)

    slot = pl.BlockSpec((None, r, ccols), lambda k, s_: (k, 0, 0))
    return pl.pallas_call(
        body, name=name,
        grid_spec=pltpu.PrefetchScalarGridSpec(num_scalar_prefetch=1, grid=(4,), in_specs=[g_spec, slot], out_specs=slot),
        out_shape=jax.ShapeDtypeStruct((4, r, ccols), grad.dtype), compiler_params=_cp(VMEM_BIG))(xyc, grad, recv)


def _rs_across_chips(parts, name):
    n = len(parts)

    def body(ins, outs, send_sems, recv_sems):
        x, y, c = _coords()
        cps = []
        for t in range(n):
            for k in range(1, 4):
                px, py = _rel_chip(x, y, k)
                cp = pltpu.make_async_remote_copy(
                    src_ref=ins[t].at[k], dst_ref=outs[t].at[k - 1], send_sem=send_sems.at[t, k - 1],
                    recv_sem=recv_sems.at[t, k - 1], device_id=(px, py, c), device_id_type=MESH)
                cp.start()
                cps.append(cp)
        for cp in cps:
            cp.wait_recv()
        for cp in cps:
            cp.wait_send()

    def other_chips():
        x, y, c = _coords()
        return [(*_rel_chip(x, y, k), c) for k in range(1, 4)]

    return _sequencer_call(
        body, other_chips, parts, [jax.ShapeDtypeStruct((3,) + p.shape[1:], p.dtype) for p in parts],
        [pltpu.SemaphoreType.DMA((n, 3)), pltpu.SemaphoreType.DMA((n, 3))], name, SEQ_ID_CHIPS)


def _adamw_math(w, g, m, v):
    m = ADAM_B1 * m + (1.0 - ADAM_B1) * g
    v = ADAM_B2 * v + (1.0 - ADAM_B2) * jnp.square(g)
    m_hat = m / (1.0 - ADAM_B1 ** ADAM_STEP)
    v_hat = v / (1.0 - ADAM_B2 ** ADAM_STEP)
    delta = -ADAM_LR * (m_hat / (jnp.sqrt(v_hat) + ADAM_EPS) + ADAM_WD * w)
    return delta, m, v


def _row_tile(rows, cap):
    best = None
    for cand in range(8, min(rows, cap) + 1, 8):
        if rows % cand == 0:
            best = cand
    assert best is not None, rows
    return best


def _adamw(w, m, v, parts, name, layer=None, prev=None, tr=256):
    r, ccols = w.shape[-2:]
    npart = len(parts)
    if r % 8 == 0:
        tr, tc = _row_tile(r, tr), ccols
        steps, at = r // tr, (lambda i: (i, 0))
    else:
        tr, tc = r, 256
        assert ccols % tc == 0
        steps, at = ccols // tc, (lambda i: (0, i))

    def spec(lead):
        if lead is None:
            return pl.BlockSpec((tr, tc), at)
        return pl.BlockSpec((None, tr, tc), lambda i: (lead,) + at(i))

    wspec = lambda: spec(layer)
    pspec = spec

    def body(*refs):
        w_ref, m_ref, v_ref = refs[:3]
        p_refs = refs[3:3 + npart]
        outs = refs[len(refs) - 4:]
        g = p_refs[0][...].astype(F32)
        for p_ref in p_refs[1:]:
            g = g + p_ref[...].astype(F32)
        delta, mn, vn = _adamw_math(w_ref[...], g, m_ref[...], v_ref[...])
        outs[0][...] = g
        outs[1][...] = delta
        outs[2][...] = mn
        outs[3][...] = vn

    operands = [w, m, v] + [p for p, _ in parts]
    in_specs = [wspec(), wspec(), wspec()] + [pspec(lead) for _, lead in parts]
    aliases = {}
    if prev is not None:
        for i, p in enumerate(prev):
            aliases[len(operands)] = i
            operands.append(p)
            in_specs.append(pl.BlockSpec(memory_space=pl.ANY))
    return pl.pallas_call(
        body, name=name, grid=(steps,), in_specs=in_specs, out_specs=[wspec()] * 4,
        out_shape=[jax.ShapeDtypeStruct(w.shape, F32)] * 4, input_output_aliases=aliases)(*operands)


def _sum8(buf, name):
    _, r, ccols = buf.shape

    def body(b_ref, o_ref):
        acc = b_ref[0]
        for j in range(1, N_DEV):
            acc = acc + b_ref[j]
        o_ref[...] = acc

    tr = _row_tile(r, 256)
    return pl.pallas_call(
        body, name=name, grid=(r // tr,), in_specs=[pl.BlockSpec((N_DEV, tr, ccols), lambda i: (0, i, 0))],
        out_specs=pl.BlockSpec((tr, ccols), lambda i: (i, 0)), out_shape=jax.ShapeDtypeStruct((r, ccols), F32))(buf)


def _pack(arrays):
    pieces, layout, off = [], [], 0
    for a in arrays:
        n = a.size
        padded = -(-n // 1024) * 1024
        flat = a.reshape(-1).astype(F32)
        if padded != n:
            flat = jnp.pad(flat, (0, padded - n))
        pieces.append(flat.reshape(padded // 128, 128))
        layout.append((off, n, a.shape))
        off += padded // 128
    return jnp.concatenate(pieces, axis=0), layout


def _unpack(packed, layout):
    out = []
    for off, n, shape in layout:
        rows = -(-n // 1024) * 8
        out.append(packed[off:off + rows].reshape(-1)[:n].reshape(shape))
    return out


def kernel(x, mem, norm_mix, norm_ffn, mem_norm, w_kv, w_out, w_ffn1, w_ffn2, a_in, a_ln_g, a_ln_b, a_ws, a_bs, b_in, b_conv_w, b_conv_b, b_dt_bias, b_a_log, b_d, b_gnorm, final_norm, loss_target, m_norm_mix, m_norm_ffn, m_mem_norm, m_w_kv, m_w_out, m_w_ffn1, m_w_ffn2, m_a_in, m_a_ln_g, m_a_ln_b, m_a_ws, m_a_bs, m_b_in, m_b_conv_w, m_b_conv_b, m_b_dt_bias, m_b_a_log, m_b_d, m_b_gnorm, m_final_norm, v_norm_mix, v_norm_ffn, v_mem_norm, v_w_kv, v_w_out, v_w_ffn1, v_w_ffn2, v_a_in, v_a_ln_g, v_a_ln_b, v_a_ws, v_a_bs, v_b_in, v_b_conv_w, v_b_conv_b, v_b_dt_bias, v_b_a_log, v_b_d, v_b_gnorm, v_final_norm):
    s = x.shape[1]
    xs = x.reshape(s, D_MODEL)
    mems = mem.reshape(N_MEM, D_MODEL)
    target = loss_target.reshape(s, D_MODEL)
    ax, ay, ac = lax.axis_index("x"), lax.axis_index("y"), lax.axis_index("c")
    me = 4 * ax + 2 * ay + ac
    xyc = jnp.stack([ax, ay, ac]).astype(jnp.int32)

    b_cols = b_in.shape[2]
    act = lambda a: a.astype(_ACT)
    lay_f1, lay_f2 = ("col", 512, (1, D_MODEL, D_FF)), ("row", 512, (1, D_FF, D_MODEL))
    lay_out, lay_kv = ("row", 384, (1, 3 * D_MODEL, D_MODEL)), ("col", 256, (1, D_MODEL, 2 * X_WIDTH))
    small_w_pack = _pack([b_conv_w[0], b_conv_b[0], b_gnorm[0]])[0]
    WA, wkv0 = _all_gather_seq([act(a_in), act(w_kv[0:1])], [("col", 640, (1, D_MODEL, 5 * D_MODEL)), lay_kv], "ag_proj_a")
    (wo0,) = _all_gather_seq([act(w_out[0:1])], [lay_out], "ag_out0")
    w1_0, w2_0 = _all_gather_seq([act(w_ffn1[0:1]), act(w_ffn2[0:1])], [lay_f1, lay_f2], "ag_ffn0")
    a0 = _rms_fwd(xs, norm_mix[0].reshape(1, -1), "mix_norm0")
    tr_b = lambda a: jnp.swapaxes(a, 1, 2)
    wbt_blk, small_w = _all_gather_seq(
        [act(tr_b(b_in)[0]), small_w_pack],
        [("blk", 0, (N_DEV, b_cols, D_MODEL)), ("blk", 0, (N_DEV, 32, 128))], "ag_proj_b", after=[a0])
    wo1, wkv1 = _all_gather_seq([act(w_out[1:2]), act(w_kv[1:2])], [lay_out, lay_kv], "ag_out1", after=[a0])
    w1_1, w2_1 = _all_gather_seq([act(w_ffn1[1:2]), act(w_ffn2[1:2])], [lay_f1, lay_f2], "ag_ffn1", after=[a0])
    W1, W2, WO, WKV = [w1_0, w1_1], [w2_0, w2_1], [wo0, wo1], [wkv0, wkv1]
    dt0 = D_INNER + CONV_DIM

    row = lambda a: a.reshape(1, -1)
    nmix = [row(norm_mix[0]), row(norm_mix[1])]
    nffn = [row(norm_ffn[0]), row(norm_ffn[1])]
    nmem = [row(mem_norm[0]), row(mem_norm[1])]
    fin = row(final_norm)
    lng, lnb = a_ln_g.reshape(1, D_INNER), a_ln_b.reshape(1, D_INNER)
    ws = a_ws[0]
    bs3 = a_bs[0].reshape(A_GROUPS, CHUNK, 1)
    pad_h = lambda a: jnp.pad(a.reshape(-1), (0, HPAD - SSM_HEADS))
    bias_row = pad_h(b_dt_bias).reshape(1, HPAD)
    alog_row = pad_h(b_a_log).reshape(1, HPAD)
    dfull = jnp.repeat(b_d.reshape(-1), SSM_P).reshape(1, D_INNER)

    kvs, mns = [None, None], [None, None]

    def mem_kv(i, after=None):
        gain = nmem[i] if after is None else _tie(nmem[i], after, f"tie_mem{i}")
        mns[i] = _rms_fwd(mems, gain, f"mem_norm{i}")
        kvs[i] = _mm(mns[i], WKV[i], m=N_MEM, n=2 * X_WIDTH, k=D_MODEL, b_at=(0, 0, 0), out_dtype=_ACT, name=f"kv{i}")

    def ffn_fwd(h, i):
        f = _rms_fwd(h, nffn[i], f"ffn_norm{i}")
        p = _mm(f, W1[i], m=s, n=D_FF, k=D_MODEL, b_at=(0, 0, 0), out_dtype=_ACT, name=f"ffn_up{i}")
        hn = _mm(p, W2[i], m=s, n=D_MODEL, k=D_FF, b_at=(0, 0, 0), a_pro="relu2", add=h, name=f"ffn_down{i}")
        return f, p, hn

    def out_proj(h, cat, i):
        return _mm(cat, WO[i], m=s, n=D_MODEL, k=3 * D_MODEL, b_at=(0, 0, 0), add=h, name=f"out_proj{i}")

    proj_a = _mm(a0, WA, m=s, n=5 * D_MODEL, k=D_MODEL, b_at=(0, 0, 0), name="proj_a")
    mem_kv(0)
    cat_a = _gmlp_fwd(proj_a, lng, lnb, ws, bs3, "gmlp_fwd")
    cat_a = _attn_fwd(proj_a, 4, kvs[0], cat_a, "attn_fwd0")
    h1 = out_proj(xs, cat_a, 0)
    f0, p0, h2 = ffn_fwd(h1, 0)

    wbt_blk, small_w, _ = lax.optimization_barrier((wbt_blk, small_w, p0))
    wbt_full = wbt_blk.reshape(N_DEV * b_cols, D_MODEL)
    WBT = jnp.concatenate([wbt_full[:dt0], wbt_full[dt0 + SSM_HEADS:]], axis=0)
    WBDT = jnp.pad(wbt_full[dt0:dt0 + SSM_HEADS], ((0, HPAD - SSM_HEADS), (0, 0)))
    cw_sh, cb_sh, gn_sh = 4 * 384, 384, 256
    sw = small_w.reshape(N_DEV, 32 * 128)
    conv_w = jnp.transpose(sw[:, :cw_sh].reshape(N_DEV, CONV_K, 384), (1, 0, 2)).reshape(CONV_K, CONV_DIM)
    conv_b = sw[:, 2048:2048 + cb_sh].reshape(1, CONV_DIM)
    gnorm = sw[:, 3072:3072 + gn_sh].reshape(1, D_INNER)

    a1 = _rms_fwd(h2, nmix[1], "mix_norm1")
    proj_b = _mm(a1, WBT, m=s, n=6 * D_MODEL, k=D_MODEL, tb=True, name="proj_b")
    dt_raw = _mm(a1, WBDT, m=s, n=HPAD, k=D_MODEL, tb=True, name="proj_dt")
    xbc = _conv_fwd(proj_b, conv_w, conv_b, "conv_fwd")
    y_ssd, states = _ssd_fwd(xbc, dt_raw, bias_row, alog_row, dfull, "ssd_fwd")
    cat_b = _gate_fwd(y_ssd, proj_b, gnorm, "gate_fwd")
    mem_kv(1, after=[cat_b])
    cat_b = _attn_fwd(proj_b, 5, kvs[1], cat_b, "attn_fwd1")
    h3 = out_proj(h2, cat_b, 1)
    f1, p1, h4 = ffn_fwd(h3, 1)

    loss_part, dh, dh_act, d_fin = _loss_head(h4, fin, target, "loss_head")

    g_f1, g_f2, g_out, g_kv = [None, None], [None, None], [None, None], [None, None]
    d_nffn, d_nmix, d_nmem = [None, None], [None, None], [None, None]

    def ffn_bwd(dh, dh_act, h_in, f, p, i, after=(), after_last=()):
        dp = _mm(dh_act, W2[i], m=s, n=D_FF, k=D_MODEL, tb=True, b_at=(0, 0, 0), epi_p=p, out_dtype=_ACT, name=f"ffn_down_dx{i}")
        g_f2[i] = _mm(p, dh_act, m=D_FF, n=D_MODEL, k=s, ta=True, a_pro="relu2", out_dtype=_ACT, name=f"ffn_down_dw{i}")
        g_f1[i] = _mm(f, dp, m=D_MODEL, n=D_FF, k=s, ta=True, out_dtype=_ACT, name=f"ffn_up_dw{i}")
        df = _mm(dp, W1[i], m=s, n=D_MODEL, k=D_FF, tb=True, b_at=(0, 0, 0), after=after, name=f"ffn_up_dx{i}")
        gain = _tie(nffn[i], after_last, f"tie_ffn_norm{i}") if after_last else nffn[i]
        dh_in, dh_in_act, d_nffn[i] = _rms_bwd(h_in, gain, df, dh, f"ffn_norm_bwd{i}")
        return dh_in, dh_in_act

    def out_bwd(dh_act, cat, i):
        dcat = _mm(dh_act, WO[i], m=s, n=3 * D_MODEL, k=D_MODEL, tb=True, b_at=(0, 0, 0), out_dtype=_ACT, name=f"out_dx{i}")
        g_out[i] = _mm(cat, dh_act, m=3 * D_MODEL, n=D_MODEL, k=s, ta=True, out_dtype=_ACT, name=f"out_dw{i}")
        return dcat

    def mem_bwd(dkv, i):
        g_kv[i] = _mm(mns[i], dkv, m=D_MODEL, n=2 * X_WIDTH, k=N_MEM, ta=True, out_dtype=_ACT, name=f"kv_dw{i}")
        dmn = _mm(dkv, WKV[i], m=N_MEM, n=D_MODEL, k=2 * X_WIDTH, tb=True, b_at=(0, 0, 0), name=f"kv_dx{i}")
        _, _, d_nmem[i] = _rms_bwd(mems, nmem[i], dmn, None, f"mem_norm_bwd{i}")

    lay_g = {"f1": ("col", 512, (D_MODEL, 512)), "f2": ("row", 512, (512, D_MODEL)), "out": ("row", 384, (384, D_MODEL)),
             "kv": ("col", 256, (D_MODEL, 256)), "a": ("col", 640, (D_MODEL, 640)), "b": ("blk", 0, (b_cols, D_MODEL))}
    reduced = {}

    def reduce_scatter(group, tag, after=()):
        grads3, lays3 = [], []
        for fam, _, g in group:
            kind, width, shape = lay_g[fam]
            grads3.append(g if kind == "blk" else g.reshape((1,) + g.shape))
            lays3.append((kind, width, shape if kind == "blk" else (1,) + shape))
        recv1 = _rs_to_sibling(grads3, lays3, f"rs_sibling_{tag}", after)
        parts = [_rs_chip_sum(g, recv1[t].reshape((4,) + lay_g[fam][2]), lay_g[fam], xyc, f"rs_chip_sum_{fam}{i}")
                 for t, (fam, i, g) in enumerate(group)]
        recv2 = _rs_across_chips(parts, f"rs_chips_{tag}")
        for (fam, i, _), p, r2 in zip(group, parts, recv2):
            reduced[fam, i] = (p, r2)
        return parts, recv2

    dh3, dh3_act = ffn_bwd(dh, dh_act, h3, f1, p1, 1)
    dcat_b = out_bwd(dh3_act, cat_b, 1)
    sums, got_ffn1 = reduce_scatter([("f1", 1, g_f1[1]), ("f2", 1, g_f2[1]), ("out", 1, g_out[1])], "ffn1")
    dy_ssd, dproj_b, d_gnorm = _gate_bwd(y_ssd, proj_b, gnorm, dcat_b, "gate_bwd")
    dproj_b, dkv_b = _attn_bwd(proj_b, 5, kvs[1], dcat_b, dproj_b, "attn_bwd1")
    mem_bwd(dkv_b, 1)
    dxbc, ddt_raw, d_alog, d_dskip, d_dtbias = _ssd_bwd(
        xbc, dt_raw, _tie(bias_row, sums, "tie_ffn1"), alog_row, dfull, dy_ssd, states, "ssd_bwd")
    dproj_b, d_convw, d_convb = _conv_bwd(proj_b, conv_w, _tie(conv_b, got_ffn1, "tie_got_ffn1"), dxbc, dproj_b, "conv_bwd")
    gb = _mm(dproj_b, a1, m=6 * D_MODEL, n=D_MODEL, k=s, ta=True, out_dtype=_ACT, name="proj_b_dw")
    gb_dt = _mm(ddt_raw, a1, m=HPAD, n=D_MODEL, k=s, ta=True, out_dtype=_ACT, name="proj_b_dw_dt")
    gb_full = jnp.concatenate([gb[:dt0], gb_dt[:SSM_HEADS], gb[dt0:]], axis=0)
    gb_blk = gb_full.reshape(N_DEV, b_cols, D_MODEL)
    sums, got_mix1 = reduce_scatter([("kv", 1, g_kv[1]), ("b", 0, gb_blk)], "mix1")
    da1 = _mm(dproj_b, WBT, m=s, n=D_MODEL, k=6 * D_MODEL, name="proj_b_dx")
    da1 = _mm(ddt_raw, WBDT, m=s, n=D_MODEL, k=HPAD, add=da1, name="proj_b_dx_dt")
    dh2, dh2_act, d_nmix[1] = _rms_bwd(h2, _tie(nmix[1], sums, "tie_mix1"), da1, dh3, "mix_norm_bwd1")

    dh1, dh1_act = ffn_bwd(dh2, dh2_act, h1, f0, p0, 0, after=got_ffn1, after_last=got_mix1)
    dcat_a = out_bwd(dh1_act, cat_a, 0)
    sums, got_ffn0 = reduce_scatter([("f1", 0, g_f1[0]), ("f2", 0, g_f2[0]), ("out", 0, g_out[0])], "ffn0")
    dproj_a, d_ws, d_bs3, d_lng, d_lnb = _gmlp_bwd(proj_a, dcat_a, _tie(lng, sums, "tie_ffn0"), lnb, ws, bs3, "gmlp_bwd")
    dproj_a, dkv_a = _attn_bwd(proj_a, 4, kvs[0], dcat_a, dproj_a, "attn_bwd0")
    da0 = _mm(dproj_a, WA, m=s, n=D_MODEL, k=5 * D_MODEL, tb=True, b_at=(0, 0, 0), name="proj_a_dx")
    grad_x, _, d_nmix[0] = _rms_bwd(xs, nmix[0], da0, dh1, "mix_norm_bwd0")
    mem_bwd(dkv_a, 0)

    rep_grads = [jnp.concatenate(d_nmix, axis=0), jnp.concatenate(d_nffn, axis=0), jnp.concatenate(d_nmem, axis=0),
                 d_lng, d_lnb, d_ws.reshape(1, A_GROUPS, CHUNK, CHUNK), d_bs3.reshape(1, A_GROUPS, CHUNK),
                 d_dtbias[:, :SSM_HEADS], d_alog[:, :SSM_HEADS], d_dskip[:, :SSM_HEADS], d_fin.reshape(D_MODEL)]
    rep_w = [norm_mix, norm_ffn, mem_norm, a_ln_g, a_ln_b, a_ws, a_bs, b_dt_bias, b_a_log, b_d, final_norm]
    rep_grads = [g.reshape(w.shape) for g, w in zip(rep_grads, rep_w)]
    sh_grads = [d_convw, d_convb, d_gnorm]
    g_pack, g_layout = _pack(rep_grads + sh_grads + [loss_part])
    n_rep = len(rep_grads)
    (g_all,) = _all_gather_seq([g_pack], [("blk", 0, (N_DEV,) + g_pack.shape)], "ag_small_grads", after=got_ffn0)

    ga = _mm(a0, dproj_a, m=D_MODEL, n=5 * D_MODEL, k=s, ta=True, out_dtype=_ACT, after=[g_pack], name="proj_a_dw")
    reduce_scatter([("kv", 0, g_kv[0]), ("a", 0, ga)], "mix0", after=[g_all])

    def big_update(w, m, v, fam, nlayer):
        res = None
        for i in range(nlayer):
            part, recv2 = reduced[fam, i]
            plist = [(part, 0), (recv2, 0), (recv2, 1), (recv2, 2)]
            res = _adamw(w, m, v, plist, f"adamw_{fam}{i}", layer=i, prev=res)
        return res

    r_f1 = big_update(w_ffn1, m_w_ffn1, v_w_ffn1, "f1", 2)
    r_f2 = big_update(w_ffn2, m_w_ffn2, v_w_ffn2, "f2", 2)
    r_out = big_update(w_out, m_w_out, v_w_out, "out", 2)
    r_kv = big_update(w_kv, m_w_kv, v_w_kv, "kv", 2)
    r_a = big_update(a_in, m_a_in, v_a_in, "a", 1)
    r_b = [tr_b(o) for o in big_update(tr_b(b_in), tr_b(m_b_in), tr_b(v_b_in), "b", 1)]

    rep_names = ["norm_mix", "norm_ffn", "mem_norm", "a_ln_g", "a_ln_b", "a_ws", "a_bs", "b_dt_bias", "b_a_log", "b_d",
                 "final_norm"]
    rep_m = [m_norm_mix, m_norm_ffn, m_mem_norm, m_a_ln_g, m_a_ln_b, m_a_ws, m_a_bs, m_b_dt_bias, m_b_a_log, m_b_d, m_final_norm]
    rep_v = [v_norm_mix, v_norm_ffn, v_mem_norm, v_a_ln_g, v_a_ln_b, v_a_ws, v_a_bs, v_b_dt_bias, v_b_a_log, v_b_d, v_final_norm]
    g_small = _sum8(g_all, "sum_small_grads")
    g_list = _unpack(g_small, g_layout)
    loss = g_list[-1][0, 0]
    wp, w_layout = _pack(rep_w)
    mp, _ = _pack(rep_m)
    vp, _ = _pack(rep_v)
    gp, _ = _pack(g_list[:n_rep])
    rep_res = [_unpack(o, w_layout) for o in _adamw(wp, mp, vp, [(gp, None)], "adamw_replicated", tr=88)]

    gcw = lax.dynamic_slice_in_dim(g_list[n_rep], me * 384, 384, axis=1).reshape(1, CONV_K, 384)
    gcb = lax.dynamic_slice_in_dim(g_list[n_rep + 1], me * 384, 384, axis=1)
    ggn = lax.dynamic_slice_in_dim(g_list[n_rep + 2], me * 256, 256, axis=1)
    sh_w = [b_conv_w, b_conv_b, b_gnorm]
    sh_m = [m_b_conv_w, m_b_conv_b, m_b_gnorm]
    sh_v = [v_b_conv_w, v_b_conv_b, v_b_gnorm]
    swp, sw_layout = _pack(sh_w)
    smp, _ = _pack(sh_m)
    svp, _ = _pack(sh_v)
    sgp, _ = _pack([gcw, gcb, ggn])
    sh_res = [_unpack(o, sw_layout) for o in _adamw(swp, smp, svp, [(sgp, None)], "adamw_sharded_small", tr=8)]

    names = ["norm_mix", "norm_ffn", "mem_norm", "w_kv", "w_out", "w_ffn1", "w_ffn2", "a_in", "a_ln_g", "a_ln_b", "a_ws",
             "a_bs", "b_in", "b_conv_w", "b_conv_b", "b_dt_bias", "b_a_log", "b_d", "b_gnorm", "final_norm"]
    big = {"w_kv": r_kv, "w_out": r_out, "w_ffn1": r_f1, "w_ffn2": r_f2, "a_in": r_a, "b_in": r_b}
    sh_names = ["b_conv_w", "b_conv_b", "b_gnorm"]
    outs = [loss, grad_x.reshape(x.shape)]
    for kind in range(4):
        for nm in names:
            if nm in big:
                outs.append(big[nm][kind])
            elif nm in sh_names:
                outs.append(sh_res[kind][sh_names.index(nm)])
            else:
                outs.append(rep_res[kind][rep_names.index(nm)])
    return tuple(outs)
```

```python
import functools
import math

import jax
import jax.numpy as jnp
from jax import lax
from jax.experimental import pallas as pl
from jax.experimental.pallas import tpu as pltpu
from jax.experimental.pallas import tpu_sc as plsc

F32 = jnp.float32
_MXU = jnp.bfloat16
_ACT = jnp.bfloat16
_HI = lax.Precision.HIGHEST

D_MODEL = 1024
CHUNK = 128
N_MEM = 256
D_INNER = 2048
A_GROUPS = 8
A_GW = D_INNER // A_GROUPS
SSM_HEADS = 32
SSM_P = 64
SSM_GROUPS = 4
SSM_GW = D_INNER // SSM_GROUPS
SSM_N = 128
CONV_K = 4
CONV_DIM = 3072
X_HEADS = 4
X_HD = 256
X_WIDTH = 1024
D_FF = 4096
EPS = 1e-6
HPAD = 128
N_DEV = 8

ADAM_LR = 0.001
ADAM_B1 = 0.9
ADAM_B2 = 0.999
ADAM_EPS = 1e-08
ADAM_WD = 0.01
ADAM_STEP = 10

VMEM_BIG = 56 * 1024 * 1024
MESH = pl.DeviceIdType.MESH


def _cp(vmem=None):
    if vmem is None:
        return pltpu.CompilerParams()
    return pltpu.CompilerParams(vmem_limit_bytes=vmem)


def _dot(a, b, dims=((1,), (0,))):
    return lax.dot_general(a.astype(_MXU), b.astype(_MXU), (dims, ((), ())), preferred_element_type=F32)


def _dot_nt(a, b):
    return _dot(a, b, ((1,), (1,)))


def _dot_tn(a, b):
    return _dot(a, b, ((0,), (0,)))


def _dot_hi(a, b, dims=((1,), (0,))):
    return lax.dot_general(a.astype(F32), b.astype(F32), (dims, ((), ())), precision=_HI, preferred_element_type=F32)


def _split3(x):
    x1 = x.astype(jnp.bfloat16)
    r = x - x1.astype(F32)
    x2 = r.astype(jnp.bfloat16)
    x3 = (r - x2.astype(F32)).astype(jnp.bfloat16)
    return x1, x2, x3


def _dot_sel(x, sel, dims=((1,), (0,)), terms=2):
    sel = sel.astype(jnp.bfloat16)
    parts = [lax.dot_general(t, sel, (dims, ((), ())), preferred_element_type=F32) for t in _split3(x)[:terms]]
    return functools.reduce(lambda a, b: a + b, parts)


def _sel_dot(sel, x, dims=((1,), (0,))):
    sel = sel.astype(jnp.bfloat16)
    parts = [lax.dot_general(sel, t, (dims, ((), ())), preferred_element_type=F32) for t in _split3(x)]
    return (parts[0] + parts[1]) + parts[2]


def _sigmoid(x):
    return 1.0 / (1.0 + jnp.exp(-x))


def _gelu(x):
    return 0.5 * x * (1.0 + lax.erf(x * (1.0 / math.sqrt(2.0))))


def _gelu_grad(x):
    return 0.5 * (1.0 + lax.erf(x * (1.0 / math.sqrt(2.0)))) + x * jnp.exp(-0.5 * x * x) * (1.0 / math.sqrt(2.0 * math.pi))


def _softplus(x):
    return jnp.maximum(x, 0.0) + jnp.log1p(jnp.exp(-jnp.abs(x)))


def _iota(shape, dim):
    return lax.broadcasted_iota(jnp.int32, shape, dim)


MM_VMEM_BUDGET = 40 * 1024 * 1024
HBM_BYTES_PER_S = 2.5e12
GRID_STEP_S = 0.35e-6
VMEM_ACC_BYTES_PER_S = 6e12


def _divisors(dim, unit):
    out = [d for d in range(unit, min(dim, 2048) + 1, unit) if dim % d == 0]
    return out if out else [dim]


def _mm_tiles(m, n, k, sa, sb, s_mn, a_pro, offsets):
    best = None
    (a_r0, a_c0, ta), (b_r0, b_c0, tb), (o_r0, o_c0) = offsets
    for tm in _divisors(m, 128):
        for tn in _divisors(n, 128):
            for tk in [k // d for d in (1, 2, 3, 4, 6, 8) if k % d == 0 and (k // d) % 128 == 0]:
                a_t = (tk, tm) if ta else (tm, tk)
                b_t = (tn, tk) if tb else (tk, tn)
                if a_r0 % a_t[0] or a_c0 % a_t[1] or b_r0 % b_t[0] or b_c0 % b_t[1] or o_r0 % tm or o_c0 % tn:
                    continue
                nk = k // tk
                vmem = 2 * (tm * tk * sa + tk * tn * sb + tm * tn * s_mn) + tm * tn * 4 * (2 if nk > 1 else 1)
                if a_pro or sa == 4:
                    vmem += tm * tk * 6
                if sb == 4:
                    vmem += tk * tn * 2
                if vmem > MM_VMEM_BUDGET:
                    continue
                gi, gj = m // tm, n // tn
                for j_inner in (True, False):
                    if nk > 1:
                        traffic = gj * m * k * sa + gi * k * n * sb
                    elif j_inner:
                        traffic = m * k * sa + gi * k * n * sb
                    else:
                        traffic = gj * m * k * sa + k * n * sb
                    traffic += m * n * s_mn + (tm * tk * sa + tk * tn * sb)
                    cost = traffic / HBM_BYTES_PER_S + gi * gj * nk * GRID_STEP_S
                    if nk > 1:
                        cost += m * n * 8 * nk / VMEM_ACC_BYTES_PER_S
                    if best is None or cost < best[0]:
                        best = (cost, tm, tn, tk, j_inner)
    assert best is not None, (m, n, k)
    return best[1:]


def _mm(a, b, *, m, n, k, name, ta=False, tb=False, a_at=(None, 0, 0), b_at=(None, 0, 0),
        out_dtype=F32, add=None, epi_p=None, epi_at=(None, 0, 0), out=None, out_at=(None, 0, 0),
        out_full=None, a_pro=None, after=()):
    s_mn =jnp.dtype(out.dtype if out is not None else out_dtype).itemsize
    s_mn += add.dtype.itemsize if add is not None else 0
    s_mn += epi_p.dtype.itemsize if epi_p is not None else 0
    tm, tn, tk, j_inner = _mm_tiles(m, n, k, a.dtype.itemsize, b.dtype.itemsize, s_mn, a_pro is not None,
                                    ((a_at[1], a_at[2], ta), (b_at[1], b_at[2], tb), (out_at[1], out_at[2])))
    nk = k // tk

    def spec(at, tr, tc, rsel, csel):
        lead, r0, c0 = at
        assert r0 % tr == 0 and c0 % tc == 0, (name, at, tr, tc)
        rb, cb = r0 // tr, c0 // tc
        if lead is None:
            return pl.BlockSpec((tr, tc), lambda g0, g1, kk: (rb + rsel(g0, g1, kk), cb + csel(g0, g1, kk)))
        return pl.BlockSpec((None, tr, tc), lambda g0, g1, kk: (lead, rb + rsel(g0, g1, kk), cb + csel(g0, g1, kk)))

    gi = (lambda g0, g1, kk: g0) if j_inner else (lambda g0, g1, kk: g1)
    gj = (lambda g0, g1, kk: g1) if j_inner else (lambda g0, g1, kk: g0)
    gk = lambda g0, g1, kk: kk
    a_spec = spec(a_at, tk, tm, gk, gi) if ta else spec(a_at, tm, tk, gi, gk)
    b_spec = spec(b_at, tn, tk, gj, gk) if tb else spec(b_at, tk, tn, gk, gj)
    dims = ((0,), (0,)) if ta else (((1,), (1,)) if tb else ((1,), (0,)))
    assert not (ta and tb)

    operands, in_specs = [a, b], [a_spec, b_spec]
    if add is not None:
        operands.append(add)
        in_specs.append(spec((None, 0, 0), tm, tn, gi, gj))
    if epi_p is not None:
        operands.append(epi_p)
        in_specs.append(spec(epi_at, tm, tn, gi, gj))
    aliases = {}
    if out is not None:
        aliases = {len(operands): 0}
        operands.append(out)
        in_specs.append(pl.BlockSpec(memory_space=pl.ANY))
        out_struct = jax.ShapeDtypeStruct(out.shape, out.dtype)
        out_dtype = out.dtype
    else:
        out_struct = jax.ShapeDtypeStruct(out_full if out_full is not None else (m, n), out_dtype)
    has_add, has_epi = add is not None, epi_p is not None
    n_skip = (1 if out is not None else 0) + len(after)
    operands += list(after)
    in_specs += [pl.BlockSpec(memory_space=pl.ANY)] * len(after)

    def body(*refs):
        a_ref, b_ref = refs[0], refs[1]
        pos = 2
        add_ref = epi_ref = None
        if has_add:
            add_ref = refs[pos]
            pos += 1
        if has_epi:
            epi_ref = refs[pos]
            pos += 1
        pos += n_skip
        o_ref = refs[pos]

        def finish(r):
            if has_add:
                r = r + add_ref[...].astype(F32)
            if has_epi:
                r = r * (2.0 * jnp.maximum(epi_ref[...].astype(F32), 0.0))
            o_ref[...] = r.astype(o_ref.dtype)

        av = a_ref[...]
        if a_pro == "relu2":
            av = jnp.square(jnp.maximum(av.astype(F32), 0.0))
        part = _dot(av, b_ref[...], dims)
        if nk == 1:
            finish(part)
        else:
            acc_ref = refs[pos + 1]
            kk = pl.program_id(2)

            @pl.when(kk == 0)
            def _():
                acc_ref[...] = part

            @pl.when(kk > 0)
            def _():
                acc_ref[...] += part

            @pl.when(kk == nk - 1)
            def _():
                finish(acc_ref[...])

    grid = (m // tm, n // tn, nk) if j_inner else (n // tn, m // tm, nk)
    return pl.pallas_call(
        body, name=name, grid=grid, in_specs=in_specs,
        out_specs=spec(out_at, tm, tn, gi, gj), out_shape=out_struct,
        scratch_shapes=[pltpu.VMEM((tm, tn), F32)] if nk > 1 else [], input_output_aliases=aliases,
        compiler_params=_cp(VMEM_BIG))(*operands)


def _rms_fwd(x, g, name, tm=256):
    s, d = x.shape
    tm = min(tm, s)

    def body(x_ref, g_ref, o_ref):
        xv = x_ref[...]
        r = lax.rsqrt(jnp.mean(xv * xv, axis=-1, keepdims=True) + EPS)
        o_ref[...] = (xv * r * g_ref[...]).astype(o_ref.dtype)

    return pl.pallas_call(
        body, name=name, grid=(s // tm,),
        in_specs=[pl.BlockSpec((tm, d), lambda i: (i, 0)), pl.BlockSpec((1, d), lambda i: (0, 0))],
        out_specs=pl.BlockSpec((tm, d), lambda i: (i, 0)),
        out_shape=jax.ShapeDtypeStruct((s, d), _ACT))(x, g)


def _rms_bwd(x, g, dy, dres, name, tm=256):
    s, d = x.shape
    tm = min(tm, s)
    has_res = dres is not None

    def body(*refs):
        if has_res:
            x_ref, g_ref, dy_ref, dres_ref, dx_ref, dxa_ref, dg_ref = refs
        else:
            x_ref, g_ref, dy_ref, dx_ref, dxa_ref, dg_ref = refs

        @pl.when(pl.program_id(0) == 0)
        def _():
            dg_ref[...] = jnp.zeros_like(dg_ref)

        xv = x_ref[...]
        dyv = dy_ref[...].astype(F32)
        r = lax.rsqrt(jnp.mean(xv * xv, axis=-1, keepdims=True) + EPS)
        xh = xv * r
        dyg = dyv * g_ref[...]
        dx = r * (dyg - xh * jnp.mean(dyg * xh, axis=-1, keepdims=True))
        if has_res:
            dx = dx + dres_ref[...]
        dx_ref[...] = dx
        dxa_ref[...] = dx.astype(dxa_ref.dtype)
        dg_ref[...] += jnp.sum(dyv * xh, axis=0, keepdims=True)

    row = pl.BlockSpec((tm, d), lambda i: (i, 0))
    vec = pl.BlockSpec((1, d), lambda i: (0, 0))
    in_specs = [row, vec, row] + ([row] if has_res else [])
    operands = [x, g, dy] + ([dres] if has_res else [])
    return pl.pallas_call(
        body, name=name, grid=(s // tm,), in_specs=in_specs, out_specs=[row, row, vec],
        out_shape=[jax.ShapeDtypeStruct((s, d), F32), jax.ShapeDtypeStruct((s, d), _ACT),
                   jax.ShapeDtypeStruct((1, d), F32)])(*operands)


def _loss_head(h, g, target, name, tm=256):
    s, d = h.shape
    tm = min(tm, s)

    def body(h_ref, g_ref, t_ref, loss_ref, dh_ref, dha_ref, dg_ref):
        @pl.when(pl.program_id(0) == 0)
        def _():
            dg_ref[...] = jnp.zeros_like(dg_ref)
            loss_ref[...] = jnp.zeros_like(loss_ref)

        xv = h_ref[...]
        r = lax.rsqrt(jnp.mean(xv * xv, axis=-1, keepdims=True) + EPS)
        xh = xv * r
        err = xh * g_ref[...] - t_ref[...]
        loss_ref[...] += jnp.full(loss_ref.shape, 0.5 * jnp.sum(jnp.mean(err * err, axis=-1, keepdims=True)), F32)
        dyv = err * (1.0 / d)
        dyg = dyv * g_ref[...]
        dh = r * (dyg - xh * jnp.mean(dyg * xh, axis=-1, keepdims=True))
        dh_ref[...] = dh
        dha_ref[...] = dh.astype(dha_ref.dtype)
        dg_ref[...] += jnp.sum(dyv * xh, axis=0, keepdims=True)

    row = pl.BlockSpec((tm, d), lambda i: (i, 0))
    vec = pl.BlockSpec((1, d), lambda i: (0, 0))
    return pl.pallas_call(
        body, name=name, grid=(s // tm,), in_specs=[row, vec, row],
        out_specs=[pl.BlockSpec((1, 128), lambda i: (0, 0)), row, row, vec],
        out_shape=[jax.ShapeDtypeStruct((1, 128), F32), jax.ShapeDtypeStruct((s, d), F32),
                   jax.ShapeDtypeStruct((s, d), _ACT), jax.ShapeDtypeStruct((1, d), F32)])(h, g, target)


def _gmlp_parts(pu, pv, lng, lnb):
    u = _gelu(pu)
    v = _gelu(pv)
    mu = jnp.mean(v, axis=-1, keepdims=True)
    vc = v - mu
    rstd = lax.rsqrt(jnp.mean(vc * vc, axis=-1, keepdims=True) + EPS)
    xhat = vc * rstd
    vn = xhat * lng + lnb
    return u, xhat, rstd, vn


def _gmlp_fwd(proj, lng, lnb, ws, bs3, name):
    s = proj.shape[0]

    def body(pu_ref, pv_ref, lng_ref, lnb_ref, ws_ref, bs_ref, o_ref):
        u, _, _, vn = _gmlp_parts(pu_ref[...], pv_ref[...], lng_ref[...], lnb_ref[...])
        causal = _iota((CHUNK, CHUNK), 0) >= _iota((CHUNK, CHUNK), 1)
        for g in range(A_GROUPS):
            sl = slice(g * A_GW, (g + 1) * A_GW)
            w = jnp.where(causal, ws_ref[g], 0.0)
            sv = _dot(w, vn[:, sl]) + bs_ref[g]
            o_ref[:, sl] = (u[:, sl] * sv).astype(o_ref.dtype)

    full = lambda shape: pl.BlockSpec(shape, lambda c: (0,) * len(shape))
    return pl.pallas_call(
        body, name=name, grid=(s // CHUNK,),
        in_specs=[pl.BlockSpec((CHUNK, D_INNER), lambda c: (c, 0)), pl.BlockSpec((CHUNK, D_INNER), lambda c: (c, 1)),
                  full((1, D_INNER)), full((1, D_INNER)), full((A_GROUPS, CHUNK, CHUNK)), full((A_GROUPS, CHUNK, 1))],
        out_specs=pl.BlockSpec((CHUNK, D_INNER), lambda c: (c, 0)),
        out_shape=jax.ShapeDtypeStruct((s, D_INNER + X_WIDTH), _ACT), compiler_params=_cp(VMEM_BIG))(proj, proj, lng, lnb, ws, bs3)


def _gmlp_bwd(proj, dcat, lng, lnb, ws, bs3, name):
    s = proj.shape[0]

    def body(pu_ref, pv_ref, dm_ref, lng_ref, lnb_ref, ws_ref, bs_ref, dp_ref, dws_ref, dbs_ref, dlng_ref, dlnb_ref, dvn_ref):
        @pl.when(pl.program_id(0) == 0)
        def _():
            dws_ref[...] = jnp.zeros_like(dws_ref)
            dbs_ref[...] = jnp.zeros_like(dbs_ref)
            dlng_ref[...] = jnp.zeros_like(dlng_ref)
            dlnb_ref[...] = jnp.zeros_like(dlnb_ref)

        pu, pv = pu_ref[...], pv_ref[...]
        lng = lng_ref[...]
        u, xhat, rstd, vn = _gmlp_parts(pu, pv, lng, lnb_ref[...])
        dm = dm_ref[...].astype(F32)
        causal = _iota((CHUNK, CHUNK), 0) >= _iota((CHUNK, CHUNK), 1)
        for g in range(A_GROUPS):
            sl = slice(g * A_GW, (g + 1) * A_GW)
            w = jnp.where(causal, ws_ref[g], 0.0)
            sv = _dot(w, vn[:, sl]) + bs_ref[g]
            dsv = dm[:, sl] * u[:, sl]
            dp_ref[:, sl] = (dm[:, sl] * sv * _gelu_grad(pu[:, sl])).astype(dp_ref.dtype)
            dvn_ref[:, sl] = _dot_tn(w, dsv)
            dws_ref[g] += jnp.where(causal, _dot_nt(dsv, vn[:, sl]), 0.0)
            dbs_ref[g] += jnp.sum(dsv, axis=-1, keepdims=True)
        dvn = dvn_ref[...]
        dlng_ref[...] += jnp.sum(dvn * xhat, axis=0, keepdims=True)
        dlnb_ref[...] += jnp.sum(dvn, axis=0, keepdims=True)
        dxh = dvn * lng
        dv = rstd * (dxh - jnp.mean(dxh, axis=-1, keepdims=True) - xhat * jnp.mean(dxh * xhat, axis=-1, keepdims=True))
        dp_ref[:, D_INNER:] = (dv * _gelu_grad(pv)).astype(dp_ref.dtype)

    full = lambda shape: pl.BlockSpec(shape, lambda c: (0,) * len(shape))
    return pl.pallas_call(
        body, name=name, grid=(s // CHUNK,),
        in_specs=[pl.BlockSpec((CHUNK, D_INNER), lambda c: (c, 0)), pl.BlockSpec((CHUNK, D_INNER), lambda c: (c, 1)),
                  pl.BlockSpec((CHUNK, D_INNER), lambda c: (c, 0)),
                  full((1, D_INNER)), full((1, D_INNER)), full((A_GROUPS, CHUNK, CHUNK)), full((A_GROUPS, CHUNK, 1))],
        out_specs=[pl.BlockSpec((CHUNK, 2 * D_INNER), lambda c: (c, 0)), full((A_GROUPS, CHUNK, CHUNK)),
                   full((A_GROUPS, CHUNK, 1)), full((1, D_INNER)), full((1, D_INNER))],
        out_shape=[jax.ShapeDtypeStruct((s, 2 * D_INNER + X_WIDTH), _ACT), jax.ShapeDtypeStruct((A_GROUPS, CHUNK, CHUNK), F32),
                   jax.ShapeDtypeStruct((A_GROUPS, CHUNK, 1), F32), jax.ShapeDtypeStruct((1, D_INNER), F32),
                   jax.ShapeDtypeStruct((1, D_INNER), F32)],
        scratch_shapes=[pltpu.VMEM((CHUNK, D_INNER), F32)],
        compiler_params=_cp(VMEM_BIG))(proj, proj, dcat, lng, lnb, ws, bs3)


_X_SCALE = 1.0 / math.sqrt(X_HD)


def _attn_fwd(proj, qblk, kv, cat, name, tm=256):
    s = proj.shape[0]
    tm = min(tm, s)

    def body(q_ref, kv_ref, cat_ref, o_ref):
        for h in range(X_HEADS):
            sl = slice(h * X_HD, (h + 1) * X_HD)
            k = kv_ref[:, sl]
            v = kv_ref[:, X_WIDTH + h * X_HD:X_WIDTH + (h + 1) * X_HD]
            sc = _dot_nt(q_ref[:, sl], k) * _X_SCALE
            e = jnp.exp(sc - jnp.max(sc, axis=-1, keepdims=True))
            p = e / jnp.sum(e, axis=-1, keepdims=True)
            o_ref[:, sl] = _dot(p, v).astype(o_ref.dtype)

    return pl.pallas_call(
        body, name=name, grid=(s // tm,),
        in_specs=[pl.BlockSpec((tm, X_WIDTH), lambda i: (i, qblk)), pl.BlockSpec((N_MEM, 2 * X_WIDTH), lambda i: (0, 0)),
                  pl.BlockSpec(memory_space=pl.ANY)],
        out_specs=pl.BlockSpec((tm, X_WIDTH), lambda i: (i, D_INNER // X_WIDTH)),
        out_shape=jax.ShapeDtypeStruct(cat.shape, cat.dtype), input_output_aliases={2: 0})(proj, kv, cat)


def _attn_bwd(proj, qblk, kv, dcat, dproj, name, tm=256):
    s = proj.shape[0]
    tm = min(tm, s)

    def body(q_ref, kv_ref, do_ref, dproj_ref, dq_ref, dkv_ref):
        @pl.when(pl.program_id(0) == 0)
        def _():
            dkv_ref[...] = jnp.zeros_like(dkv_ref)

        for h in range(X_HEADS):
            sl = slice(h * X_HD, (h + 1) * X_HD)
            slv = slice(X_WIDTH + h * X_HD, X_WIDTH + (h + 1) * X_HD)
            q = q_ref[:, sl]
            k = kv_ref[:, sl]
            v = kv_ref[:, slv]
            do = do_ref[:, sl].astype(F32)
            sc = _dot_nt(q, k) * _X_SCALE
            e = jnp.exp(sc - jnp.max(sc, axis=-1, keepdims=True))
            p = e / jnp.sum(e, axis=-1, keepdims=True)
            dp = _dot_nt(do, v)
            ds = p * (dp - jnp.sum(dp * p, axis=-1, keepdims=True)) * _X_SCALE
            dq_ref[:, sl] = _dot(ds, k).astype(dq_ref.dtype)
            dkv_ref[:, sl] += _dot_tn(ds, q)
            dkv_ref[:, slv] += _dot_tn(p, do)

    return pl.pallas_call(
        body, name=name, grid=(s // tm,),
        in_specs=[pl.BlockSpec((tm, X_WIDTH), lambda i: (i, qblk)), pl.BlockSpec((N_MEM, 2 * X_WIDTH), lambda i: (0, 0)),
                  pl.BlockSpec((tm, X_WIDTH), lambda i: (i, 2)), pl.BlockSpec(memory_space=pl.ANY)],
        out_specs=[pl.BlockSpec((tm, X_WIDTH), lambda i: (i, qblk)), pl.BlockSpec((N_MEM, 2 * X_WIDTH), lambda i: (0, 0))],
        out_shape=[jax.ShapeDtypeStruct(dproj.shape, dproj.dtype), jax.ShapeDtypeStruct((N_MEM, 2 * X_WIDTH), F32)],
        input_output_aliases={3: 0})(proj, kv, dcat, dproj)


CONV_TC = 256
_XBC_BLK0 = D_INNER // CONV_TC


CONV_RB = 64
SUBLANES = 8


def _rows_before(cur, prev_last, j):
    rolled = pltpu.roll(cur, j, 0)
    head = jnp.where(_iota((SUBLANES, cur.shape[1]), 0) < j, pltpu.roll(prev_last, j, 0), rolled[:SUBLANES])
    return jnp.concatenate([head, rolled[SUBLANES:]], axis=0)


def _rows_after(cur, next_first, j):
    n = cur.shape[0]
    rolled = pltpu.roll(cur, n - j, 0)
    tail = jnp.where(_iota((SUBLANES, cur.shape[1]), 0) >= SUBLANES - j, pltpu.roll(next_first, SUBLANES - j, 0),
                     rolled[n - SUBLANES:])
    return jnp.concatenate([rolled[:n - SUBLANES], tail], axis=0)


def _conv_pre(x_ref, w_ref, b_ref, r0, prev_last):
    cur = x_ref[pl.ds(r0, CONV_RB), :]
    shifts = [_rows_before(cur, prev_last, j) for j in range(1, CONV_K)]
    pre = b_ref[...] + w_ref[CONV_K - 1:CONV_K, :] * cur
    for j in range(1, CONV_K):
        pre = pre + w_ref[CONV_K - 1 - j:CONV_K - j, :] * shifts[j - 1]
    return pre, cur, shifts


def _conv_fwd(proj, w, b, name):
    s = proj.shape[0]

    def body(x_ref, w_ref, b_ref, o_ref):
        xv = x_ref[...]
        rows = _iota(xv.shape, 0)
        pre = b_ref[...] + w_ref[CONV_K - 1:CONV_K, :] * xv
        for j in range(1, CONV_K):
            pre = pre + w_ref[CONV_K - 1 - j:CONV_K - j, :] * jnp.where(rows >= j, pltpu.roll(xv, j, 0), 0.0)
        o_ref[...] = pre * _sigmoid(pre)

    return pl.pallas_call(
        body, name=name, grid=(CONV_DIM // CONV_TC,),
        in_specs=[pl.BlockSpec((s, CONV_TC), lambda j: (0, _XBC_BLK0 + j)), pl.BlockSpec((CONV_K, CONV_TC), lambda j: (0, j)),
                  pl.BlockSpec((1, CONV_TC), lambda j: (0, j))],
        out_specs=pl.BlockSpec((s, CONV_TC), lambda j: (0, j)),
        out_shape=jax.ShapeDtypeStruct((s, CONV_DIM), F32), compiler_params=_cp(VMEM_BIG))(proj, w, b)


def _conv_bwd(proj, w, b, dxbc, dproj, name):
    s = proj.shape[0]

    nb = s // CONV_RB

    def body(x_ref, w_ref, b_ref, d_ref, dproj_ref, dx_ref, dw_ref, db_ref, dpre_ref):
        def fold(v):
            out = v[:SUBLANES]
            for t in range(1, CONV_RB // SUBLANES):
                out = out + v[t * SUBLANES:(t + 1) * SUBLANES]
            return out

        def first(i, carry):
            prev_last, acc = carry
            r0 = pl.multiple_of(i * CONV_RB, CONV_RB)
            pre, cur, shifts = _conv_pre(x_ref, w_ref, b_ref, r0, prev_last)
            sig = _sigmoid(pre)
            dpre = d_ref[pl.ds(r0, CONV_RB), :] * (sig * (1.0 + pre * (1.0 - sig)))
            dpre_ref[pl.ds(r0, CONV_RB), :] = dpre
            taps = [cur] + shifts
            acc = tuple(a + fold(dpre * t) for a, t in zip(acc[:CONV_K], taps)) + (acc[CONV_K] + fold(dpre),)
            return cur[CONV_RB - SUBLANES:], acc

        zero8 = jnp.zeros((SUBLANES, CONV_TC), F32)
        _, acc = lax.fori_loop(0, nb, first, (zero8, (zero8,) * (CONV_K + 1)))
        for j in range(CONV_K):
            dw_ref[CONV_K - 1 - j:CONV_K - j, :] = jnp.sum(acc[j], axis=0, keepdims=True)
        db_ref[...] = jnp.sum(acc[CONV_K], axis=0, keepdims=True)

        def second(i, next_first):
            r0 = pl.multiple_of((nb - 1 - i) * CONV_RB, CONV_RB)
            cur = dpre_ref[pl.ds(r0, CONV_RB), :]
            dx = w_ref[CONV_K - 1:CONV_K, :] * cur
            for j in range(1, CONV_K):
                dx = dx + w_ref[CONV_K - 1 - j:CONV_K - j, :] * _rows_after(cur, next_first, j)
            dx_ref[pl.ds(r0, CONV_RB), :] = dx.astype(dx_ref.dtype)
            return cur[:SUBLANES]

        lax.fori_loop(0, nb, second, zero8)

    return pl.pallas_call(
        body, name=name, grid=(CONV_DIM // CONV_TC,),
        in_specs=[pl.BlockSpec((s, CONV_TC), lambda j: (0, _XBC_BLK0 + j)), pl.BlockSpec((CONV_K, CONV_TC), lambda j: (0, j)),
                  pl.BlockSpec((1, CONV_TC), lambda j: (0, j)), pl.BlockSpec((s, CONV_TC), lambda j: (0, j)),
                  pl.BlockSpec(memory_space=pl.ANY)],
        out_specs=[pl.BlockSpec((s, CONV_TC), lambda j: (0, _XBC_BLK0 + j)), pl.BlockSpec((CONV_K, CONV_TC), lambda j: (0, j)),
                   pl.BlockSpec((1, CONV_TC), lambda j: (0, j))],
        out_shape=[jax.ShapeDtypeStruct(dproj.shape, dproj.dtype), jax.ShapeDtypeStruct((CONV_K, CONV_DIM), F32),
                   jax.ShapeDtypeStruct((1, CONV_DIM), F32)], input_output_aliases={4: 0},
        scratch_shapes=[pltpu.VMEM((s, CONV_TC), F32)],
        compiler_params=_cp(VMEM_BIG))(proj, w, b, dxbc, dproj)


def _ssd_common(dtc_ref, br_ref, ar_ref, csb_ref, cst_ref, csf_ref):
    a_row = -jnp.exp(ar_ref[...])
    dt_c = _softplus(dtc_ref[...] + br_ref[...])
    tril = _iota((CHUNK, CHUNK), 0) >= _iota((CHUNK, CHUNK), 1)
    cs = _sel_dot(tril, dt_c * a_row)
    cst_ref[...] = cs.T
    e64 = (jnp.right_shift(_iota((HPAD, D_INNER), 1), 6) == _iota((HPAD, D_INNER), 0)).astype(jnp.bfloat16)
    e128 = jnp.right_shift(_iota((HPAD, SSM_HEADS * CHUNK), 1), 7) == _iota((HPAD, SSM_HEADS * CHUNK), 0)
    csb_ref[...] = _dot_sel(cs, e128)
    dt_full = _dot_sel(dt_c, e64)
    csf_ref[...] = _dot_sel(cs, e64)
    cs_full = csf_ref[...]
    cs_last = csf_ref[CHUNK - 1:CHUNK, :]
    e_full = jnp.exp(cs_full)
    f_full = jnp.exp(cs_last - cs_full)
    gamma = jnp.exp(cs_last)
    return a_row, dt_c, cs, dt_full, e_full, f_full, gamma, e64


def _ssd_lambda(csb_ref, cst_ref, h, causal):
    diff = csb_ref[:, h * CHUNK:(h + 1) * CHUNK] - cst_ref[h:h + 1, :]
    return jnp.exp(jnp.where(causal, diff, -1e30))


_SSD_VEC_SPECS = lambda: [pl.BlockSpec((1, HPAD), lambda c: (0, 0)), pl.BlockSpec((1, HPAD), lambda c: (0, 0)),
                          pl.BlockSpec((1, D_INNER), lambda c: (0, 0))]


def _ssd_fwd(xbc, dtc, bias_row, alog_row, dfull, name):
    s = xbc.shape[0]
    nc = s // CHUNK

    def body(xbc_ref, dtc_ref, br_ref, ar_ref, df_ref, y_ref, st_ref, ht_ref, csb_ref, cst_ref, csf_ref):
        @pl.when(pl.program_id(0) == 0)
        def _():
            ht_ref[...] = jnp.zeros_like(ht_ref)

        _, _, _, dt_full, e_full, f_full, gamma, _ = _ssd_common(dtc_ref, br_ref, ar_ref, csb_ref, cst_ref, csf_ref)
        x = xbc_ref[:, :D_INNER]
        xdt = x * dt_full
        st_ref[...] = ht_ref[...]
        causal = _iota((CHUNK, CHUNK), 0) >= _iota((CHUNK, CHUNK), 1)
        lo = _iota((CHUNK, CHUNK), 1) < SSM_P
        for g in range(SSM_GROUPS):
            gs = slice(g * SSM_GW, (g + 1) * SSM_GW)
            bg = xbc_ref[:, D_INNER + g * SSM_N:D_INNER + (g + 1) * SSM_N]
            cg = xbc_ref[:, D_INNER + SSM_GROUPS * SSM_N + g * SSM_N:D_INNER + SSM_GROUPS * SSM_N + (g + 1) * SSM_N]
            ht = ht_ref[:, gs]
            cb = _dot_nt(cg, bg)
            yoff = e_full[:, gs] * _dot(cg, ht)
            for jp in range(SSM_GW // CHUNK):
                j = g * (SSM_GW // CHUNK) + jp
                ps = slice(j * CHUNK, (j + 1) * CHUNK)
                x2 = xdt[:, ps]
                y0 = _dot(cb * _ssd_lambda(csb_ref, cst_ref, 2 * j, causal), x2)
                y1 = _dot(cb * _ssd_lambda(csb_ref, cst_ref, 2 * j + 1, causal), x2)
                y_ref[:, ps] = (jnp.where(lo, y0, y1) + yoff[:, jp * CHUNK:(jp + 1) * CHUNK]
                                + x[:, ps] * df_ref[:, ps])
            ht_ref[:, gs] = gamma[:, gs] * ht + _dot_tn(bg, xdt[:, gs] * f_full[:, gs])

    return pl.pallas_call(
        body, name=name, grid=(nc,),
        in_specs=[pl.BlockSpec((CHUNK, CONV_DIM), lambda c: (c, 0)), pl.BlockSpec((CHUNK, HPAD), lambda c: (c, 0))]
                 + _SSD_VEC_SPECS(),
        out_specs=[pl.BlockSpec((CHUNK, D_INNER), lambda c: (c, 0)), pl.BlockSpec((None, SSM_N, D_INNER), lambda c: (c, 0, 0))],
        out_shape=[jax.ShapeDtypeStruct((s, D_INNER), F32), jax.ShapeDtypeStruct((nc, SSM_N, D_INNER), F32)],
        scratch_shapes=[pltpu.VMEM((SSM_N, D_INNER), F32), pltpu.VMEM((CHUNK, SSM_HEADS * CHUNK), F32),
                        pltpu.VMEM((HPAD, CHUNK), F32), pltpu.VMEM((CHUNK, D_INNER), F32)],
        compiler_params=_cp(VMEM_BIG))(xbc, dtc, bias_row, alog_row, dfull)


def _ssd_bwd(xbc, dtc, bias_row, alog_row, dfull, dy, states, name):
    s = xbc.shape[0]
    nc = s // CHUNK
    rev = lambda c: nc - 1 - c

    def body(xbc_ref, dtc_ref, br_ref, ar_ref, df_ref, dy_ref, st_ref,
             dxbc_ref, ddt_ref, dalog_ref, dd_ref, dbias_ref,
             dht_ref, csb_ref, cst_ref, csf_ref, ddf_ref, dxs_ref, dcsf_ref, dcsl_ref):
        step = pl.program_id(0)

        @pl.when(step == 0)
        def _():
            dht_ref[...] = jnp.zeros_like(dht_ref)
            ddf_ref[...] = jnp.zeros_like(ddf_ref)
            dalog_ref[...] = jnp.zeros_like(dalog_ref)
            dbias_ref[...] = jnp.zeros_like(dbias_ref)
            dd_ref[...] = jnp.zeros_like(dd_ref)

        a_row, dt_c, _, dt_full, e_full, f_full, gamma, e64 = _ssd_common(dtc_ref, br_ref, ar_ref, csb_ref, cst_ref, csf_ref)
        x = xbc_ref[:, :D_INNER]
        xdt = x * dt_full
        dy_all = dy_ref[...]
        ddf_ref[...] += jnp.broadcast_to(jnp.sum(dy_all * x, axis=0, keepdims=True), ddf_ref.shape)
        causal = _iota((CHUNK, CHUNK), 0) >= _iota((CHUNK, CHUNK), 1)
        lo = _iota((CHUNK, CHUNK), 1) < SSM_P
        head_lane = _iota((CHUNK, HPAD), 1)
        head_row = _iota((HPAD, CHUNK), 0)
        dcs_heads = jnp.zeros((CHUNK, HPAD), F32)
        dcs_cols = jnp.zeros((HPAD, CHUNK), F32)
        for g in range(SSM_GROUPS):
            gs = slice(g * SSM_GW, (g + 1) * SSM_GW)
            b0 = D_INNER + g * SSM_N
            c0 = D_INNER + SSM_GROUPS * SSM_N + g * SSM_N
            bg = xbc_ref[:, b0:b0 + SSM_N]
            cg = xbc_ref[:, c0:c0 + SSM_N]
            ht = st_ref[:, gs]
            dht = dht_ref[:, gs]
            dyg = dy_all[:, gs]
            eg, fg, gg = e_full[:, gs], f_full[:, gs], gamma[:, gs]
            z = _dot(cg, ht)
            dz = dyg * eg
            dcg = _dot_nt(dz, ht)
            dht_new = _dot_tn(cg, dz) + gg * dht
            xf = xdt[:, gs] * fg
            dxf = _dot(bg, dht)
            dbg = _dot_nt(xf, dht)
            dff = dxf * xf
            dcsf_ref[:, gs] = dyg * eg * z - dff
            dcsl_ref[:, gs] = jnp.broadcast_to(
                jnp.sum(dff, axis=0, keepdims=True) + jnp.sum(dht * ht, axis=0, keepdims=True) * gg, (8, SSM_GW))
            cb = _dot_nt(cg, bg)
            dcb = jnp.zeros((CHUNK, CHUNK), F32)
            for jp in range(SSM_GW // CHUNK):
                j = g * (SSM_GW // CHUNK) + jp
                ps = slice(j * CHUNK, (j + 1) * CHUNK)
                x2 = xdt[:, ps]
                dy2 = dy_all[:, ps]
                dxh = []
                for hh in range(2):
                    h = 2 * j + hh
                    lam = _ssd_lambda(csb_ref, cst_ref, h, causal)
                    mh = cb * lam
                    dyh = jnp.where(lo, dy2, 0.0) if hh == 0 else jnp.where(lo, 0.0, dy2)
                    dm = _dot_nt(dyh, x2)
                    dcb = dcb + dm * lam
                    gm = dm * mh
                    dcs_heads = dcs_heads + jnp.where(head_lane == h, jnp.sum(gm, axis=1, keepdims=True), 0.0)
                    dcs_cols = dcs_cols + jnp.where(head_row == h, jnp.sum(gm, axis=0, keepdims=True), 0.0)
                    dxh.append(_dot_tn(mh, dy2))
                dxs_ref[:, ps] = jnp.where(lo, dxh[0], dxh[1]) + dxf[:, jp * CHUNK:(jp + 1) * CHUNK] * fg[:, jp * CHUNK:(jp + 1) * CHUNK]
            dxbc_ref[:, b0:b0 + SSM_N] = (dbg + _dot_tn(dcb, cg)).astype(dxbc_ref.dtype)
            dxbc_ref[:, c0:c0 + SSM_N] = (dcg + _dot(dcb, bg)).astype(dxbc_ref.dtype)
            dht_ref[:, gs] = dht_new
        dxs = dxs_ref[...]
        dcs_heads = dcs_heads - dcs_cols.T + _dot_sel(dcsf_ref[...], e64, ((1,), (1,)))
        dcs_last = _dot_sel(dcsl_ref[...], e64, ((1,), (1,)))
        dcs_heads = dcs_heads + jnp.where(_iota((CHUNK, HPAD), 0) == CHUNK - 1, dcs_last[0:1, :], 0.0)
        triu = _iota((CHUNK, CHUNK), 0) <= _iota((CHUNK, CHUNK), 1)
        dda = _sel_dot(triu, dcs_heads)
        ddt = dda * a_row + _dot_sel(dxs * x, e64, ((1,), (1,)))
        dxbc_ref[:, :D_INNER] = (dxs * dt_full + dy_all * df_ref[...]).astype(dxbc_ref.dtype)
        dalog_ref[...] += jnp.sum(dda * dt_c, axis=0, keepdims=True) * a_row
        ddt_raw = ddt * _sigmoid(dtc_ref[...] + br_ref[...])
        ddt_ref[...] = ddt_raw.astype(ddt_ref.dtype)
        dbias_ref[...] += jnp.sum(ddt_raw, axis=0, keepdims=True)

        @pl.when(step == nc - 1)
        def _():
            dd_ref[...] = _dot_sel(ddf_ref[...], e64, ((1,), (1,)))[0:1, :]

    vec = pl.BlockSpec((1, HPAD), lambda c: (0, 0))
    return pl.pallas_call(
        body, name=name, grid=(nc,),
        in_specs=[pl.BlockSpec((CHUNK, CONV_DIM), lambda c: (rev(c), 0)), pl.BlockSpec((CHUNK, HPAD), lambda c: (rev(c), 0))]
                 + _SSD_VEC_SPECS()
                 + [pl.BlockSpec((CHUNK, D_INNER), lambda c: (rev(c), 0)),
                    pl.BlockSpec((None, SSM_N, D_INNER), lambda c: (rev(c), 0, 0))],
        out_specs=[pl.BlockSpec((CHUNK, CONV_DIM), lambda c: (rev(c), 0)), pl.BlockSpec((CHUNK, HPAD), lambda c: (rev(c), 0)),
                   vec, vec, vec],
        out_shape=[jax.ShapeDtypeStruct((s, CONV_DIM), F32), jax.ShapeDtypeStruct((s, HPAD), _ACT),
                   jax.ShapeDtypeStruct((1, HPAD), F32), jax.ShapeDtypeStruct((1, HPAD), F32),
                   jax.ShapeDtypeStruct((1, HPAD), F32)],
        scratch_shapes=[pltpu.VMEM((SSM_N, D_INNER), F32), pltpu.VMEM((CHUNK, SSM_HEADS * CHUNK), F32),
                        pltpu.VMEM((HPAD, CHUNK), F32), pltpu.VMEM((CHUNK, D_INNER), F32),
                        pltpu.VMEM((8, D_INNER), F32), pltpu.VMEM((CHUNK, D_INNER), F32),
                        pltpu.VMEM((CHUNK, D_INNER), F32), pltpu.VMEM((8, D_INNER), F32)],
        compiler_params=_cp(VMEM_BIG))(xbc, dtc, bias_row, alog_row, dfull, dy, states)


def _gate_fwd(y, proj, gn, name, tm=256):
    s = y.shape[0]
    tm = min(tm, s)

    def body(y_ref, z_ref, gn_ref, o_ref):
        for g in range(SSM_GROUPS):
            gs = slice(g * SSM_GW, (g + 1) * SSM_GW)
            z = z_ref[:, gs]
            t = y_ref[:, gs] * (z * _sigmoid(z))
            r = lax.rsqrt(jnp.mean(t * t, axis=-1, keepdims=True) + EPS)
            o_ref[:, gs] = (t * r * gn_ref[:, gs]).astype(o_ref.dtype)

    row = pl.BlockSpec((tm, D_INNER), lambda i: (i, 0))
    return pl.pallas_call(
        body, name=name, grid=(s // tm,), in_specs=[row, row, pl.BlockSpec((1, D_INNER), lambda i: (0, 0))],
        out_specs=row, out_shape=jax.ShapeDtypeStruct((s, D_INNER + X_WIDTH), _ACT))(y, proj, gn)


def _gate_bwd(y, proj, gn, dcat, name, tm=256):
    s = y.shape[0]
    tm = min(tm, s)

    def body(y_ref, z_ref, gn_ref, dm_ref, dy_ref, dz_ref, dgn_ref):
        @pl.when(pl.program_id(0) == 0)
        def _():
            dgn_ref[...] = jnp.zeros_like(dgn_ref)

        for g in range(SSM_GROUPS):
            gs = slice(g * SSM_GW, (g + 1) * SSM_GW)
            z = z_ref[:, gs]
            yv = y_ref[:, gs]
            sig = _sigmoid(z)
            sz = z * sig
            t = yv * sz
            r = lax.rsqrt(jnp.mean(t * t, axis=-1, keepdims=True) + EPS)
            th = t * r
            dm = dm_ref[:, gs].astype(F32)
            dmg = dm * gn_ref[:, gs]
            dt_ = r * (dmg - th * jnp.mean(dmg * th, axis=-1, keepdims=True))
            dgn_ref[:, gs] += jnp.sum(dm * th, axis=0, keepdims=True)
            dy_ref[:, gs] = dt_ * sz
            dz_ref[:, gs] = (dt_ * yv * (sig * (1.0 + z * (1.0 - sig)))).astype(dz_ref.dtype)

    row = pl.BlockSpec((tm, D_INNER), lambda i: (i, 0))
    vec = pl.BlockSpec((1, D_INNER), lambda i: (0, 0))
    return pl.pallas_call(
        body, name=name, grid=(s // tm,), in_specs=[row, row, vec, row], out_specs=[row, row, vec],
        out_shape=[jax.ShapeDtypeStruct((s, D_INNER), F32), jax.ShapeDtypeStruct((s, 6 * D_MODEL), _ACT),
                   jax.ShapeDtypeStruct((1, D_INNER), F32)])(y, proj, gn, dcat)


def _block_of(kind, width):
    if kind == "col":
        return lambda ref, j: ref.at[:, :, pl.ds(pl.multiple_of(j * width, 128), width)]
    if kind == "row":
        return lambda ref, j: ref.at[:, pl.ds(pl.multiple_of(j * width, 8), width), :]
    return lambda ref, j: ref.at[j]


def _coords():
    return lax.axis_index("x"), lax.axis_index("y"), lax.axis_index("c")


def _rel_chip(x, y, k):
    return (1 - x if k & 1 else x), (1 - y if k & 2 else y)


def _all_gather_body(ins, outs, send_sems, recv_sems, local_sems, blocks):
    n = len(ins)
    x, y, c = _coords()
    sibling = (x, y, 1 - c)
    via = (x + (1 - c) * (1 - 2 * x), y + c * (1 - 2 * y))
    onto = (x + c * (1 - 2 * x), y + (1 - c) * (1 - 2 * y))

    def copy(t, k, chip, core, to, src=None):
        dst = blocks[t](outs[t], 4 * chip[0] + 2 * chip[1] + core)
        return pltpu.make_async_remote_copy(
            src_ref=dst if src is None else src, dst_ref=dst, send_sem=send_sems.at[t, k],
            recv_sem=recv_sems.at[t, k], device_id=to, device_id_type=MESH)

    started = []
    for t in range(n):
        mine = pltpu.make_async_copy(ins[t], blocks[t](outs[t], 4 * x + 2 * y + c), local_sems.at[t])
        mine.start()
        started.append(mine)
    sends = []
    for t in range(n):
        for k in range(3):
            px, py = _rel_chip(x, y, k)
            cp = copy(t, k, (x, y), c, (px, py, 1 - c if k == 0 else c), src=ins[t])
            cp.start()
            sends.append(cp)
    for t in range(n):
        for k in (1, 2):
            chip = _rel_chip(x, y, k)
            copy(t, k, chip, c, sibling).wait_recv()
            fwd = copy(t, 3 + k, chip, c, sibling)
            fwd.start()
            sends.append(fwd)
        hop = copy(t, 3, via, c, (*onto, c))
        hop.start()
        sends.append(hop)
    for t in range(n):
        diagonal = _rel_chip(x, y, 3)
        copy(t, 3, diagonal, c, sibling).wait_recv()
        fwd = copy(t, 6, diagonal, c, sibling)
        fwd.start()
        sends.append(fwd)
    for t in range(n):
        copy(t, 0, (x, y), 1 - c, sibling).wait_recv()
        for k in range(1, 4):
            copy(t, 3 + k, _rel_chip(x, y, k), 1 - c, sibling).wait_recv()
    for cp in sends:
        cp.wait_send()
    for mine in started:
        mine.wait()


def _handshake(peers):
    barrier = pltpu.get_barrier_semaphore()
    for peer in peers:
        pl.semaphore_signal(barrier, inc=1, device_id=peer, device_id_type=MESH)
    pl.semaphore_wait(barrier, len(peers))


def _gather_peers():
    x, y, c = _coords()
    return [(x, y, 1 - c)] + [(*_rel_chip(x, y, k), c) for k in (1, 2)]


SEQ_ID_GATHER, SEQ_ID_SIBLING, SEQ_ID_CHIPS = 1, 2, 3


def _sequencer_call(body, peers, operands, out_types, sems, name, collective_id, after=()):
    n_in, n_out, n_after = len(operands), len(out_types), len(after)

    def launch(*refs):
        _handshake(peers())
        body(refs[:n_in], refs[n_in + n_after:n_in + n_after + n_out], *refs[n_in + n_after + n_out:])

    return pl.kernel(
        launch, name=name, out_type=out_types, mesh=plsc.ScalarSubcoreMesh(axis_name="seq", num_cores=1),
        scratch_types=sems, compiler_params=pltpu.CompilerParams(collective_id=collective_id))(*operands, *after)


def _all_gather_seq(shards, layouts, name, after=()):
    n = len(shards)
    blocks = [_block_of(kind, width) for kind, width, _ in layouts]
    return _sequencer_call(
        lambda ins, outs, *sems: _all_gather_body(ins, outs, *sems, blocks), _gather_peers, shards,
        [jax.ShapeDtypeStruct(shape, sh.dtype) for sh, (_, _, shape) in zip(shards, layouts)],
        [pltpu.SemaphoreType.DMA((n, 7)), pltpu.SemaphoreType.DMA((n, 7)), pltpu.SemaphoreType.DMA((n,))],
        name, SEQ_ID_GATHER, after)


def _tie(small, after, name):
    del name
    return lax.optimization_barrier((small, *after))[0]


def _rs_to_sibling(grads, layouts, name, after=()):
    n = len(grads)
    blocks = [_block_of(kind, width) for kind, width, _ in layouts]

    def body(ins, outs, send_sems, recv_sems):
        x, y, c = _coords()
        sibling = (x, y, 1 - c)
        cps = []
        for t in range(n):
            for k in range(4):
                px, py = _rel_chip(x, y, k)
                cp = pltpu.make_async_remote_copy(
                    src_ref=blocks[t](ins[t], 4 * px + 2 * py + (1 - c)), dst_ref=outs[t].at[k],
                    send_sem=send_sems.at[t, k], recv_sem=recv_sems.at[t, k], device_id=sibling, device_id_type=MESH)
                cp.start()
                cps.append(cp)
        for cp in cps:
            cp.wait_recv()
        for cp in cps:
            cp.wait_send()

    def sibling_only():
        x, y, c = _coords()
        return [(x, y, 1 - c)]

    return _sequencer_call(
        body, sibling_only, grads,
        [jax.ShapeDtypeStruct((4,) + shape, g.dtype) for g, (_, _, shape) in zip(grads, layouts)],
        [pltpu.SemaphoreType.DMA((n, 4)), pltpu.SemaphoreType.DMA((n, 4))], name, SEQ_ID_SIBLING, after)


def _rs_chip_sum(grad, recv, layout, xyc, name):
    kind, width, shape = layout
    r, ccols = shape

    def src_index(k, xyc_ref):
        px = jnp.where(k % 2 == 1, 1 - xyc_ref[0], xyc_ref[0])
        py = jnp.where(k // 2 == 1, 1 - xyc_ref[1], xyc_ref[1])
        return 4 * px + 2 * py + xyc_ref[2]

    if kind == "col":
        g_spec = pl.BlockSpec((r, ccols), lambda k, s_: (0, src_index(k, s_)))
    elif kind == "row":
        g_spec = pl.BlockSpec((r, ccols), lambda k, s_: (src_index(k, s_), 0))
    else:
        g_spec = pl.BlockSpec((None, r, ccols), lambda k, s_: (src_index(k, s_), 0, 0))

    def body(xyc_ref, g_ref, r_ref, o_ref):
        o_ref[...] = (g_ref[...].astype(F32) + r_ref[...].astype(F32)).astype(o_ref.dtype)

    slot = pl.BlockSpec((None, r, ccols), lambda k, s_: (k, 0, 0))
    return pl.pallas_call(
        body, name=name,
        grid_spec=pltpu.PrefetchScalarGridSpec(num_scalar_prefetch=1, grid=(4,), in_specs=[g_spec, slot], out_specs=slot),
        out_shape=jax.ShapeDtypeStruct((4, r, ccols), grad.dtype), compiler_params=_cp(VMEM_BIG))(xyc, grad, recv)


def _rs_across_chips(parts, name):
    n = len(parts)

    def body(ins, outs, send_sems, recv_sems):
        x, y, c = _coords()
        cps = []
        for t in range(n):
            for k in range(1, 4):
                px, py = _rel_chip(x, y, k)
                cp = pltpu.make_async_remote_copy(
                    src_ref=ins[t].at[k], dst_ref=outs[t].at[k - 1], send_sem=send_sems.at[t, k - 1],
                    recv_sem=recv_sems.at[t, k - 1], device_id=(px, py, c), device_id_type=MESH)
                cp.start()
                cps.append(cp)
        for cp in cps:
            cp.wait_recv()
        for cp in cps:
            cp.wait_send()

    def other_chips():
        x, y, c = _coords()
        return [(*_rel_chip(x, y, k), c) for k in range(1, 4)]

    return _sequencer_call(
        body, other_chips, parts, [jax.ShapeDtypeStruct((3,) + p.shape[1:], p.dtype) for p in parts],
        [pltpu.SemaphoreType.DMA((n, 3)), pltpu.SemaphoreType.DMA((n, 3))], name, SEQ_ID_CHIPS)


def _adamw_math(w, g, m, v):
    m = ADAM_B1 * m + (1.0 - ADAM_B1) * g
    v = ADAM_B2 * v + (1.0 - ADAM_B2) * jnp.square(g)
    m_hat = m / (1.0 - ADAM_B1 ** ADAM_STEP)
    v_hat = v / (1.0 - ADAM_B2 ** ADAM_STEP)
    delta = -ADAM_LR * (m_hat / (jnp.sqrt(v_hat) + ADAM_EPS) + ADAM_WD * w)
    return delta, m, v


def _row_tile(rows, cap):
    best = None
    for cand in range(8, min(rows, cap) + 1, 8):
        if rows % cand == 0:
            best = cand
    assert best is not None, rows
    return best


def _adamw(w, m, v, parts, name, layer=None, prev=None, tr=256):
    r, ccols = w.shape[-2:]
    npart = len(parts)
    if r % 8 == 0:
        tr, tc = _row_tile(r, tr), ccols
        steps, at = r // tr, (lambda i: (i, 0))
    else:
        tr, tc = r, 256
        assert ccols % tc == 0
        steps, at = ccols // tc, (lambda i: (0, i))

    def spec(lead):
        if lead is None:
            return pl.BlockSpec((tr, tc), at)
        return pl.BlockSpec((None, tr, tc), lambda i: (lead,) + at(i))

    wspec = lambda: spec(layer)
    pspec = spec

    def body(*refs):
        w_ref, m_ref, v_ref = refs[:3]
        p_refs = refs[3:3 + npart]
        outs = refs[len(refs) - 4:]
        g = p_refs[0][...].astype(F32)
        for p_ref in p_refs[1:]:
            g = g + p_ref[...].astype(F32)
        delta, mn, vn = _adamw_math(w_ref[...], g, m_ref[...], v_ref[...])
        outs[0][...] = g
        outs[1][...] = delta
        outs[2][...] = mn
        outs[3][...] = vn

    operands = [w, m, v] + [p for p, _ in parts]
    in_specs = [wspec(), wspec(), wspec()] + [pspec(lead) for _, lead in parts]
    aliases = {}
    if prev is not None:
        for i, p in enumerate(prev):
            aliases[len(operands)] = i
            operands.append(p)
            in_specs.append(pl.BlockSpec(memory_space=pl.ANY))
    return pl.pallas_call(
        body, name=name, grid=(steps,), in_specs=in_specs, out_specs=[wspec()] * 4,
        out_shape=[jax.ShapeDtypeStruct(w.shape, F32)] * 4, input_output_aliases=aliases)(*operands)


def _sum8(buf, name):
    _, r, ccols = buf.shape

    def body(b_ref, o_ref):
        acc = b_ref[0]
        for j in range(1, N_DEV):
            acc = acc + b_ref[j]
        o_ref[...] = acc

    tr = _row_tile(r, 256)
    return pl.pallas_call(
        body, name=name, grid=(r // tr,), in_specs=[pl.BlockSpec((N_DEV, tr, ccols), lambda i: (0, i, 0))],
        out_specs=pl.BlockSpec((tr, ccols), lambda i: (i, 0)), out_shape=jax.ShapeDtypeStruct((r, ccols), F32))(buf)


def _pack(arrays):
    pieces, layout, off = [], [], 0
    for a in arrays:
        n = a.size
        padded = -(-n // 1024) * 1024
        flat = a.reshape(-1).astype(F32)
        if padded != n:
            flat = jnp.pad(flat, (0, padded - n))
        pieces.append(flat.reshape(padded // 128, 128))
        layout.append((off, n, a.shape))
        off += padded // 128
    return jnp.concatenate(pieces, axis=0), layout


def _unpack(packed, layout):
    out = []
    for off, n, shape in layout:
        rows = -(-n // 1024) * 8
        out.append(packed[off:off + rows].reshape(-1)[:n].reshape(shape))
    return out


def kernel(x, mem, norm_mix, norm_ffn, mem_norm, w_kv, w_out, w_ffn1, w_ffn2, a_in, a_ln_g, a_ln_b, a_ws, a_bs, b_in, b_conv_w, b_conv_b, b_dt_bias, b_a_log, b_d, b_gnorm, final_norm, loss_target, m_norm_mix, m_norm_ffn, m_mem_norm, m_w_kv, m_w_out, m_w_ffn1, m_w_ffn2, m_a_in, m_a_ln_g, m_a_ln_b, m_a_ws, m_a_bs, m_b_in, m_b_conv_w, m_b_conv_b, m_b_dt_bias, m_b_a_log, m_b_d, m_b_gnorm, m_final_norm, v_norm_mix, v_norm_ffn, v_mem_norm, v_w_kv, v_w_out, v_w_ffn1, v_w_ffn2, v_a_in, v_a_ln_g, v_a_ln_b, v_a_ws, v_a_bs, v_b_in, v_b_conv_w, v_b_conv_b, v_b_dt_bias, v_b_a_log, v_b_d, v_b_gnorm, v_final_norm):
    s = x.shape[1]
    xs = x.reshape(s, D_MODEL)
    mems = mem.reshape(N_MEM, D_MODEL)
    target = loss_target.reshape(s, D_MODEL)
    ax, ay, ac = lax.axis_index("x"), lax.axis_index("y"), lax.axis_index("c")
    me = 4 * ax + 2 * ay + ac
    xyc = jnp.stack([ax, ay, ac]).astype(jnp.int32)

    b_cols = b_in.shape[2]
    act = lambda a: a.astype(_ACT)
    lay_f1, lay_f2 = ("col", 512, (1, D_MODEL, D_FF)), ("row", 512, (1, D_FF, D_MODEL))
    lay_out, lay_kv = ("row", 384, (1, 3 * D_MODEL, D_MODEL)), ("col", 256, (1, D_MODEL, 2 * X_WIDTH))
    small_w_pack = _pack([b_conv_w[0], b_conv_b[0], b_gnorm[0]])[0]
    (WA,) = _all_gather_seq([act(a_in)], [("col", 640, (1, D_MODEL, 5 * D_MODEL))], "ag_proj_a")
    wo0, wkv0 = _all_gather_seq([act(w_out[0:1]), act(w_kv[0:1])], [lay_out, lay_kv], "ag_out0")
    w1_0, w2_0 = _all_gather_seq([act(w_ffn1[0:1]), act(w_ffn2[0:1])], [lay_f1, lay_f2], "ag_ffn0")
    a0 = _rms_fwd(xs, norm_mix[0].reshape(1, -1), "mix_norm0")
    tr_b = lambda a: jnp.swapaxes(a, 1, 2)
    wbt_blk, small_w = _all_gather_seq(
        [act(tr_b(b_in)[0]), small_w_pack],
        [("blk", 0, (N_DEV, b_cols, D_MODEL)), ("blk", 0, (N_DEV, 32, 128))], "ag_proj_b", after=[a0])
    wo1, wkv1 = _all_gather_seq([act(w_out[1:2]), act(w_kv[1:2])], [lay_out, lay_kv], "ag_out1", after=[a0])
    w1_1, w2_1 = _all_gather_seq([act(w_ffn1[1:2]), act(w_ffn2[1:2])], [lay_f1, lay_f2], "ag_ffn1", after=[a0])
    W1, W2, WO, WKV = [w1_0, w1_1], [w2_0, w2_1], [wo0, wo1], [wkv0, wkv1]
    dt0 = D_INNER + CONV_DIM

    row = lambda a: a.reshape(1, -1)
    nmix = [row(norm_mix[0]), row(norm_mix[1])]
    nffn = [row(norm_ffn[0]), row(norm_ffn[1])]
    nmem = [row(mem_norm[0]), row(mem_norm[1])]
    fin = row(final_norm)
    lng, lnb = a_ln_g.reshape(1, D_INNER), a_ln_b.reshape(1, D_INNER)
    ws = a_ws[0]
    bs3 = a_bs[0].reshape(A_GROUPS, CHUNK, 1)
    pad_h = lambda a: jnp.pad(a.reshape(-1), (0, HPAD - SSM_HEADS))
    bias_row = pad_h(b_dt_bias).reshape(1, HPAD)
    alog_row = pad_h(b_a_log).reshape(1, HPAD)
    dfull = jnp.repeat(b_d.reshape(-1), SSM_P).reshape(1, D_INNER)

    kvs, mns = [None, None], [None, None]

    def mem_kv(i, after=None):
        gain = nmem[i] if after is None else _tie(nmem[i], after, f"tie_mem{i}")
        mns[i] = _rms_fwd(mems, gain, f"mem_norm{i}")
        kvs[i] = _mm(mns[i], WKV[i], m=N_MEM, n=2 * X_WIDTH, k=D_MODEL, b_at=(0, 0, 0), out_dtype=_ACT, name=f"kv{i}")

    def ffn_fwd(h, i):
        f = _rms_fwd(h, nffn[i], f"ffn_norm{i}")
        p = _mm(f, W1[i], m=s, n=D_FF, k=D_MODEL, b_at=(0, 0, 0), out_dtype=_ACT, name=f"ffn_up{i}")
        hn = _mm(p, W2[i], m=s, n=D_MODEL, k=D_FF, b_at=(0, 0, 0), a_pro="relu2", add=h, name=f"ffn_down{i}")
        return f, p, hn

    def out_proj(h, cat, i):
        return _mm(cat, WO[i], m=s, n=D_MODEL, k=3 * D_MODEL, b_at=(0, 0, 0), add=h, name=f"out_proj{i}")

    proj_a = _mm(a0, WA, m=s, n=5 * D_MODEL, k=D_MODEL, b_at=(0, 0, 0), name="proj_a")
    mem_kv(0, after=[proj_a])
    cat_a = _gmlp_fwd(proj_a, lng, lnb, ws, bs3, "gmlp_fwd")
    cat_a = _attn_fwd(proj_a, 4, kvs[0], cat_a, "attn_fwd0")
    h1 = out_proj(xs, cat_a, 0)
    f0, p0, h2 = ffn_fwd(h1, 0)

    wbt_blk, small_w, _ = lax.optimization_barrier((wbt_blk, small_w, p0))
    wbt_full = wbt_blk.reshape(N_DEV * b_cols, D_MODEL)
    WBT = jnp.concatenate([wbt_full[:dt0], wbt_full[dt0 + SSM_HEADS:]], axis=0)
    WBDT = jnp.pad(wbt_full[dt0:dt0 + SSM_HEADS], ((0, HPAD - SSM_HEADS), (0, 0)))
    cw_sh, cb_sh, gn_sh = 4 * 384, 384, 256
    sw = small_w.reshape(N_DEV, 32 * 128)
    conv_w = jnp.transpose(sw[:, :cw_sh].reshape(N_DEV, CONV_K, 384), (1, 0, 2)).reshape(CONV_K, CONV_DIM)
    conv_b = sw[:, 2048:2048 + cb_sh].reshape(1, CONV_DIM)
    gnorm = sw[:, 3072:3072 + gn_sh].reshape(1, D_INNER)

    a1 = _rms_fwd(h2, nmix[1], "mix_norm1")
    proj_b = _mm(a1, WBT, m=s, n=6 * D_MODEL, k=D_MODEL, tb=True, name="proj_b")
    dt_raw = _mm(a1, WBDT, m=s, n=HPAD, k=D_MODEL, tb=True, name="proj_dt")
    xbc = _conv_fwd(proj_b, conv_w, conv_b, "conv_fwd")
    y_ssd, states = _ssd_fwd(xbc, dt_raw, bias_row, alog_row, dfull, "ssd_fwd")
    cat_b = _gate_fwd(y_ssd, proj_b, gnorm, "gate_fwd")
    mem_kv(1, after=[cat_b])
    cat_b = _attn_fwd(proj_b, 5, kvs[1], cat_b, "attn_fwd1")
    h3 = out_proj(h2, cat_b, 1)
    f1, p1, h4 = ffn_fwd(h3, 1)

    loss_part, dh, dh_act, d_fin = _loss_head(h4, fin, target, "loss_head")

    g_f1, g_f2, g_out, g_kv = [None, None], [None, None], [None, None], [None, None]
    d_nffn, d_nmix, d_nmem = [None, None], [None, None], [None, None]

    def ffn_bwd(dh, dh_act, h_in, f, p, i, after=(), after_last=()):
        dp = _mm(dh_act, W2[i], m=s, n=D_FF, k=D_MODEL, tb=True, b_at=(0, 0, 0), epi_p=p, out_dtype=_ACT, name=f"ffn_down_dx{i}")
        g_f2[i] = _mm(p, dh_act, m=D_FF, n=D_MODEL, k=s, ta=True, a_pro="relu2", out_dtype=_ACT, name=f"ffn_down_dw{i}")
        g_f1[i] = _mm(f, dp, m=D_MODEL, n=D_FF, k=s, ta=True, out_dtype=_ACT, name=f"ffn_up_dw{i}")
        df = _mm(dp, W1[i], m=s, n=D_MODEL, k=D_FF, tb=True, b_at=(0, 0, 0), after=after, name=f"ffn_up_dx{i}")
        gain = _tie(nffn[i], after_last, f"tie_ffn_norm{i}") if after_last else nffn[i]
        dh_in, dh_in_act, d_nffn[i] = _rms_bwd(h_in, gain, df, dh, f"ffn_norm_bwd{i}")
        return dh_in, dh_in_act

    def out_bwd(dh_act, cat, i):
        dcat = _mm(dh_act, WO[i], m=s, n=3 * D_MODEL, k=D_MODEL, tb=True, b_at=(0, 0, 0), out_dtype=_ACT, name=f"out_dx{i}")
        g_out[i] = _mm(cat, dh_act, m=3 * D_MODEL, n=D_MODEL, k=s, ta=True, out_dtype=_ACT, name=f"out_dw{i}")
        return dcat

    def mem_bwd(dkv, i):
        g_kv[i] = _mm(mns[i], dkv, m=D_MODEL, n=2 * X_WIDTH, k=N_MEM, ta=True, out_dtype=_ACT, name=f"kv_dw{i}")
        dmn = _mm(dkv, WKV[i], m=N_MEM, n=D_MODEL, k=2 * X_WIDTH, tb=True, b_at=(0, 0, 0), name=f"kv_dx{i}")
        _, _, d_nmem[i] = _rms_bwd(mems, nmem[i], dmn, None, f"mem_norm_bwd{i}")

    lay_g = {"f1": ("col", 512, (D_MODEL, 512)), "f2": ("row", 512, (512, D_MODEL)), "out": ("row", 384, (384, D_MODEL)),
             "kv": ("col", 256, (D_MODEL, 256)), "a": ("col", 640, (D_MODEL, 640)), "b": ("blk", 0, (b_cols, D_MODEL))}
    reduced = {}

    def reduce_scatter(group, tag, after=(), sums_after=()):
        grads3, lays3 = [], []
        for fam, _, g in group:
            kind, width, shape = lay_g[fam]
            grads3.append(g if kind == "blk" else g.reshape((1,) + g.shape))
            lays3.append((kind, width, shape if kind == "blk" else (1,) + shape))
        recv1 = _rs_to_sibling(grads3, lays3, f"rs_sibling_{tag}", after)
        if sums_after:
            recv1 = lax.optimization_barrier((tuple(recv1), tuple(sums_after)))[0]
        parts = [_rs_chip_sum(g, recv1[t].reshape((4,) + lay_g[fam][2]), lay_g[fam], xyc, f"rs_chip_sum_{fam}{i}")
                 for t, (fam, i, g) in enumerate(group)]
        recv2 = _rs_across_chips(parts, f"rs_chips_{tag}")
        for (fam, i, _), p, r2 in zip(group, parts, recv2):
            reduced[fam, i] = (p, r2)
        return parts, recv2

    dh3, dh3_act = ffn_bwd(dh, dh_act, h3, f1, p1, 1)
    dcat_b = out_bwd(dh3_act, cat_b, 1)
    sums, got_ffn1 = reduce_scatter([("f1", 1, g_f1[1]), ("f2", 1, g_f2[1]), ("out", 1, g_out[1])], "ffn1", sums_after=[dcat_b])
    dy_ssd, dproj_b, d_gnorm = _gate_bwd(y_ssd, proj_b, gnorm, dcat_b, "gate_bwd")
    dproj_b, dkv_b = _attn_bwd(proj_b, 5, kvs[1], dcat_b, dproj_b, "attn_bwd1")
    mem_bwd(dkv_b, 1)
    dxbc, ddt_raw, d_alog, d_dskip, d_dtbias = _ssd_bwd(
        xbc, dt_raw, _tie(bias_row, sums, "tie_ffn1"), alog_row, dfull, dy_ssd, states, "ssd_bwd")
    dproj_b, d_convw, d_convb = _conv_bwd(proj_b, conv_w, _tie(conv_b, got_ffn1, "tie_got_ffn1"), dxbc, dproj_b, "conv_bwd")
    gb = _mm(dproj_b, a1, m=6 * D_MODEL, n=D_MODEL, k=s, ta=True, out_dtype=_ACT, name="proj_b_dw")
    gb_dt = _mm(ddt_raw, a1, m=HPAD, n=D_MODEL, k=s, ta=True, out_dtype=_ACT, name="proj_b_dw_dt")
    gb_full = jnp.concatenate([gb[:dt0], gb_dt[:SSM_HEADS], gb[dt0:]], axis=0)
    gb_blk = gb_full.reshape(N_DEV, b_cols, D_MODEL)
    da1 = _mm(dproj_b, WBT, m=s, n=D_MODEL, k=6 * D_MODEL, name="proj_b_dx")
    sums, got_mix1 = reduce_scatter([("kv", 1, g_kv[1]), ("b", 0, gb_blk)], "mix1", sums_after=[da1])
    da1 = _mm(ddt_raw, WBDT, m=s, n=D_MODEL, k=HPAD, add=da1, name="proj_b_dx_dt")
    dh2, dh2_act, d_nmix[1] = _rms_bwd(h2, _tie(nmix[1], sums, "tie_mix1"), da1, dh3, "mix_norm_bwd1")

    dh1, dh1_act = ffn_bwd(dh2, dh2_act, h1, f0, p0, 0, after=got_ffn1, after_last=got_mix1)
    dcat_a = out_bwd(dh1_act, cat_a, 0)
    sums, got_ffn0 = reduce_scatter([("f1", 0, g_f1[0]), ("f2", 0, g_f2[0]), ("out", 0, g_out[0])], "ffn0", sums_after=[dcat_a])
    dproj_a, d_ws, d_bs3, d_lng, d_lnb = _gmlp_bwd(proj_a, dcat_a, _tie(lng, sums, "tie_ffn0"), lnb, ws, bs3, "gmlp_bwd")
    dproj_a, dkv_a = _attn_bwd(proj_a, 4, kvs[0], dcat_a, dproj_a, "attn_bwd0")
    da0 = _mm(dproj_a, WA, m=s, n=D_MODEL, k=5 * D_MODEL, tb=True, b_at=(0, 0, 0), name="proj_a_dx")
    grad_x, _, d_nmix[0] = _rms_bwd(xs, nmix[0], da0, dh1, "mix_norm_bwd0")
    mem_bwd(dkv_a, 0)

    rep_grads = [jnp.concatenate(d_nmix, axis=0), jnp.concatenate(d_nffn, axis=0), jnp.concatenate(d_nmem, axis=0),
                 d_lng, d_lnb, d_ws.reshape(1, A_GROUPS, CHUNK, CHUNK), d_bs3.reshape(1, A_GROUPS, CHUNK),
                 d_dtbias[:, :SSM_HEADS], d_alog[:, :SSM_HEADS], d_dskip[:, :SSM_HEADS], d_fin.reshape(D_MODEL)]
    rep_w = [norm_mix, norm_ffn, mem_norm, a_ln_g, a_ln_b, a_ws, a_bs, b_dt_bias, b_a_log, b_d, final_norm]
    rep_grads = [g.reshape(w.shape) for g, w in zip(rep_grads, rep_w)]
    sh_grads = [d_convw, d_convb, d_gnorm]
    g_pack, g_layout = _pack(rep_grads + sh_grads + [loss_part])
    n_rep = len(rep_grads)
    (g_all,) = _all_gather_seq([g_pack], [("blk", 0, (N_DEV,) + g_pack.shape)], "ag_small_grads", after=got_ffn0)

    def big_update(w, m, v, fam, nlayer):
        res = None
        for i in range(nlayer):
            part, recv2 = reduced[fam, i]
            plist = [(part, 0), (recv2, 0), (recv2, 1), (recv2, 2)]
            res = _adamw(w, m, v, plist, f"adamw_{fam}{i}", layer=i, prev=res)
        return res

    ga = _mm(a0, dproj_a, m=D_MODEL, n=5 * D_MODEL, k=s, ta=True, out_dtype=_ACT, after=[g_pack], name="proj_a_dw")
    r_b = big_update(tr_b(b_in), tr_b(m_b_in), tr_b(v_b_in), "b", 1)
    reduce_scatter([("kv", 0, g_kv[0]), ("a", 0, ga)], "mix0", after=[g_all], sums_after=r_b)
    r_b = [tr_b(o) for o in r_b]

    r_f1 = big_update(w_ffn1, m_w_ffn1, v_w_ffn1, "f1", 2)
    r_f2 = big_update(w_ffn2, m_w_ffn2, v_w_ffn2, "f2", 2)
    r_out = big_update(w_out, m_w_out, v_w_out, "out", 2)
    r_kv = big_update(w_kv, m_w_kv, v_w_kv, "kv", 2)
    r_a = big_update(a_in, m_a_in, v_a_in, "a", 1)

    rep_names = ["norm_mix", "norm_ffn", "mem_norm", "a_ln_g", "a_ln_b", "a_ws", "a_bs", "b_dt_bias", "b_a_log", "b_d",
                 "final_norm"]
    rep_m = [m_norm_mix, m_norm_ffn, m_mem_norm, m_a_ln_g, m_a_ln_b, m_a_ws, m_a_bs, m_b_dt_bias, m_b_a_log, m_b_d, m_final_norm]
    rep_v = [v_norm_mix, v_norm_ffn, v_mem_norm, v_a_ln_g, v_a_ln_b, v_a_ws, v_a_bs, v_b_dt_bias, v_b_a_log, v_b_d, v_final_norm]
    g_small = _sum8(g_all, "sum_small_grads")
    g_list = _unpack(g_small, g_layout)
    loss = g_list[-1][0, 0]
    wp, w_layout = _pack(rep_w)
    mp, _ = _pack(rep_m)
    vp, _ = _pack(rep_v)
    gp = g_small[:g_layout[n_rep][0]]
    rep_res = [_unpack(o, w_layout) for o in _adamw(wp, mp, vp, [(gp, None)], "adamw_replicated", tr=88)]

    gcw = lax.dynamic_slice_in_dim(g_list[n_rep], me * 384, 384, axis=1).reshape(1, CONV_K, 384)
    gcb = lax.dynamic_slice_in_dim(g_list[n_rep + 1], me * 384, 384, axis=1)
    ggn = lax.dynamic_slice_in_dim(g_list[n_rep + 2], me * 256, 256, axis=1)
    sh_w = [b_conv_w, b_conv_b, b_gnorm]
    sh_m = [m_b_conv_w, m_b_conv_b, m_b_gnorm]
    sh_v = [v_b_conv_w, v_b_conv_b, v_b_gnorm]
    swp, sw_layout = _pack(sh_w)
    smp, _ = _pack(sh_m)
    svp, _ = _pack(sh_v)
    sgp, _ = _pack([gcw, gcb, ggn])
    sh_res = [_unpack(o, sw_layout) for o in _adamw(swp, smp, svp, [(sgp, None)], "adamw_sharded_small", tr=8)]

    names = ["norm_mix", "norm_ffn", "mem_norm", "w_kv", "w_out", "w_ffn1", "w_ffn2", "a_in", "a_ln_g", "a_ln_b", "a_ws",
             "a_bs", "b_in", "b_conv_w", "b_conv_b", "b_dt_bias", "b_a_log", "b_d", "b_gnorm", "final_norm"]
    big = {"w_kv": r_kv, "w_out": r_out, "w_ffn1": r_f1, "w_ffn2": r_f2, "a_in": r_a, "b_in": r_b}
    sh_names = ["b_conv_w", "b_conv_b", "b_gnorm"]
    outs = [loss, grad_x.reshape(x.shape)]
    for kind in range(4):
        for nm in names:
            if nm in big:
                outs.append(big[nm][kind])
            elif nm in sh_names:
                outs.append(sh_res[kind][sh_names.index(nm)])
            else:
                outs.append(rep_res[kind][rep_names.index(nm)])
    return tuple(outs)
```

```python
import functools
import math

import jax
import jax.numpy as jnp
from jax import lax
from jax.experimental import pallas as pl
from jax.experimental.pallas import tpu as pltpu
from jax.experimental.pallas import tpu_sc as plsc

F32 = jnp.float32
_MXU = jnp.bfloat16
_ACT = jnp.bfloat16
_HI = lax.Precision.HIGHEST

D_MODEL = 1024
CHUNK = 128
N_MEM = 256
D_INNER = 2048
A_GROUPS = 8
A_GW = D_INNER // A_GROUPS
SSM_HEADS = 32
SSM_P = 64
SSM_GROUPS = 4
SSM_GW = D_INNER // SSM_GROUPS
SSM_N = 128
CONV_K = 4
CONV_DIM = 3072
X_HEADS = 4
X_HD = 256
X_WIDTH = 1024
D_FF = 4096
EPS = 1e-6
HPAD = 128
N_DEV = 8

ADAM_LR = 0.001
ADAM_B1 = 0.9
ADAM_B2 = 0.999
ADAM_EPS = 1e-08
ADAM_WD = 0.01
ADAM_STEP = 10

VMEM_BIG = 56 * 1024 * 1024
MESH = pl.DeviceIdType.MESH


def _cp(vmem=None):
    if vmem is None:
        return pltpu.CompilerParams()
    return pltpu.CompilerParams(vmem_limit_bytes=vmem)


def _dot(a, b, dims=((1,), (0,))):
    return lax.dot_general(a.astype(_MXU), b.astype(_MXU), (dims, ((), ())), preferred_element_type=F32)


def _dot_nt(a, b):
    return _dot(a, b, ((1,), (1,)))


def _dot_tn(a, b):
    return _dot(a, b, ((0,), (0,)))


def _dot_hi(a, b, dims=((1,), (0,))):
    return lax.dot_general(a.astype(F32), b.astype(F32), (dims, ((), ())), precision=_HI, preferred_element_type=F32)


def _split3(x):
    x1 = x.astype(jnp.bfloat16)
    r = x - x1.astype(F32)
    x2 = r.astype(jnp.bfloat16)
    x3 = (r - x2.astype(F32)).astype(jnp.bfloat16)
    return x1, x2, x3


def _dot_sel(x, sel, dims=((1,), (0,)), terms=2):
    sel = sel.astype(jnp.bfloat16)
    parts = [lax.dot_general(t, sel, (dims, ((), ())), preferred_element_type=F32) for t in _split3(x)[:terms]]
    return functools.reduce(lambda a, b: a + b, parts)


def _sel_dot(sel, x, dims=((1,), (0,))):
    sel = sel.astype(jnp.bfloat16)
    parts = [lax.dot_general(sel, t, (dims, ((), ())), preferred_element_type=F32) for t in _split3(x)]
    return (parts[0] + parts[1]) + parts[2]


def _sigmoid(x):
    return 1.0 / (1.0 + jnp.exp(-x))


def _gelu(x):
    return 0.5 * x * (1.0 + lax.erf(x * (1.0 / math.sqrt(2.0))))


def _gelu_grad(x):
    return 0.5 * (1.0 + lax.erf(x * (1.0 / math.sqrt(2.0)))) + x * jnp.exp(-0.5 * x * x) * (1.0 / math.sqrt(2.0 * math.pi))


def _softplus(x):
    return jnp.maximum(x, 0.0) + jnp.log1p(jnp.exp(-jnp.abs(x)))


def _iota(shape, dim):
    return lax.broadcasted_iota(jnp.int32, shape, dim)


MM_VMEM_BUDGET = 40 * 1024 * 1024
HBM_BYTES_PER_S = 2.5e12
GRID_STEP_S = 0.35e-6
VMEM_ACC_BYTES_PER_S = 6e12


def _divisors(dim, unit):
    out = [d for d in range(unit, min(dim, 2048) + 1, unit) if dim % d == 0]
    return out if out else [dim]


def _mm_tiles(m, n, k, sa, sb, s_mn, a_pro, offsets):
    best = None
    (a_r0, a_c0, ta), (b_r0, b_c0, tb), (o_r0, o_c0) = offsets
    for tm in _divisors(m, 128):
        for tn in _divisors(n, 128):
            for tk in [k // d for d in (1, 2, 3, 4, 6, 8) if k % d == 0 and (k // d) % 128 == 0]:
                a_t = (tk, tm) if ta else (tm, tk)
                b_t = (tn, tk) if tb else (tk, tn)
                if a_r0 % a_t[0] or a_c0 % a_t[1] or b_r0 % b_t[0] or b_c0 % b_t[1] or o_r0 % tm or o_c0 % tn:
                    continue
                nk = k // tk
                vmem = 2 * (tm * tk * sa + tk * tn * sb + tm * tn * s_mn) + tm * tn * 4 * (2 if nk > 1 else 1)
                if a_pro or sa == 4:
                    vmem += tm * tk * 6
                if sb == 4:
                    vmem += tk * tn * 2
                if vmem > MM_VMEM_BUDGET:
                    continue
                gi, gj = m // tm, n // tn
                for j_inner in (True, False):
                    if nk > 1:
                        traffic = gj * m * k * sa + gi * k * n * sb
                    elif j_inner:
                        traffic = m * k * sa + gi * k * n * sb
                    else:
                        traffic = gj * m * k * sa + k * n * sb
                    traffic += m * n * s_mn + (tm * tk * sa + tk * tn * sb)
                    cost = traffic / HBM_BYTES_PER_S + gi * gj * nk * GRID_STEP_S
                    if nk > 1:
                        cost += m * n * 8 * nk / VMEM_ACC_BYTES_PER_S
                    if best is None or cost < best[0]:
                        best = (cost, tm, tn, tk, j_inner)
    assert best is not None, (m, n, k)
    return best[1:]


def _mm(a, b, *, m, n, k, name, ta=False, tb=False, a_at=(None, 0, 0), b_at=(None, 0, 0),
        out_dtype=F32, add=None, epi_p=None, epi_at=(None, 0, 0), out=None, out_at=(None, 0, 0),
        out_full=None, a_pro=None, after=()):
    s_mn =jnp.dtype(out.dtype if out is not None else out_dtype).itemsize
    s_mn += add.dtype.itemsize if add is not None else 0
    s_mn += epi_p.dtype.itemsize if epi_p is not None else 0
    tm, tn, tk, j_inner = _mm_tiles(m, n, k, a.dtype.itemsize, b.dtype.itemsize, s_mn, a_pro is not None,
                                    ((a_at[1], a_at[2], ta), (b_at[1], b_at[2], tb), (out_at[1], out_at[2])))
    nk = k // tk

    def spec(at, tr, tc, rsel, csel):
        lead, r0, c0 = at
        assert r0 % tr == 0 and c0 % tc == 0, (name, at, tr, tc)
        rb, cb = r0 // tr, c0 // tc
        if lead is None:
            return pl.BlockSpec((tr, tc), lambda g0, g1, kk: (rb + rsel(g0, g1, kk), cb + csel(g0, g1, kk)))
        return pl.BlockSpec((None, tr, tc), lambda g0, g1, kk: (lead, rb + rsel(g0, g1, kk), cb + csel(g0, g1, kk)))

    gi = (lambda g0, g1, kk: g0) if j_inner else (lambda g0, g1, kk: g1)
    gj = (lambda g0, g1, kk: g1) if j_inner else (lambda g0, g1, kk: g0)
    gk = lambda g0, g1, kk: kk
    a_spec = spec(a_at, tk, tm, gk, gi) if ta else spec(a_at, tm, tk, gi, gk)
    b_spec = spec(b_at, tn, tk, gj, gk) if tb else spec(b_at, tk, tn, gk, gj)
    dims = ((0,), (0,)) if ta else (((1,), (1,)) if tb else ((1,), (0,)))
    assert not (ta and tb)

    operands, in_specs = [a, b], [a_spec, b_spec]
    if add is not None:
        operands.append(add)
        in_specs.append(spec((None, 0, 0), tm, tn, gi, gj))
    if epi_p is not None:
        operands.append(epi_p)
        in_specs.append(spec(epi_at, tm, tn, gi, gj))
    aliases = {}
    if out is not None:
        aliases = {len(operands): 0}
        operands.append(out)
        in_specs.append(pl.BlockSpec(memory_space=pl.ANY))
        out_struct = jax.ShapeDtypeStruct(out.shape, out.dtype)
        out_dtype = out.dtype
    else:
        out_struct = jax.ShapeDtypeStruct(out_full if out_full is not None else (m, n), out_dtype)
    has_add, has_epi = add is not None, epi_p is not None
    n_skip = (1 if out is not None else 0) + len(after)
    operands += list(after)
    in_specs += [pl.BlockSpec(memory_space=pl.ANY)] * len(after)

    def body(*refs):
        a_ref, b_ref = refs[0], refs[1]
        pos = 2
        add_ref = epi_ref = None
        if has_add:
            add_ref = refs[pos]
            pos += 1
        if has_epi:
            epi_ref = refs[pos]
            pos += 1
        pos += n_skip
        o_ref = refs[pos]

        def finish(r):
            if has_add:
                r = r + add_ref[...].astype(F32)
            if has_epi:
                r = r * (2.0 * jnp.maximum(epi_ref[...].astype(F32), 0.0))
            o_ref[...] = r.astype(o_ref.dtype)

        av = a_ref[...]
        if a_pro == "relu2":
            av = jnp.square(jnp.maximum(av.astype(F32), 0.0))
        part = _dot(av, b_ref[...], dims)
        if nk == 1:
            finish(part)
        else:
            acc_ref = refs[pos + 1]
            kk = pl.program_id(2)

            @pl.when(kk == 0)
            def _():
                acc_ref[...] = part

            @pl.when(kk > 0)
            def _():
                acc_ref[...] += part

            @pl.when(kk == nk - 1)
            def _():
                finish(acc_ref[...])

    grid = (m // tm, n // tn, nk) if j_inner else (n // tn, m // tm, nk)
    return pl.pallas_call(
        body, name=name, grid=grid, in_specs=in_specs,
        out_specs=spec(out_at, tm, tn, gi, gj), out_shape=out_struct,
        scratch_shapes=[pltpu.VMEM((tm, tn), F32)] if nk > 1 else [], input_output_aliases=aliases,
        compiler_params=_cp(VMEM_BIG))(*operands)


def _rms_fwd(x, g, name, tm=1024):
    s, d = x.shape
    tm = min(tm, s)

    def body(x_ref, g_ref, o_ref):
        xv = x_ref[...]
        r = lax.rsqrt(jnp.mean(xv * xv, axis=-1, keepdims=True) + EPS)
        o_ref[...] = (xv * r * g_ref[...]).astype(o_ref.dtype)

    return pl.pallas_call(
        body, name=name, grid=(s // tm,),
        in_specs=[pl.BlockSpec((tm, d), lambda i: (i, 0)), pl.BlockSpec((1, d), lambda i: (0, 0))],
        out_specs=pl.BlockSpec((tm, d), lambda i: (i, 0)),
        out_shape=jax.ShapeDtypeStruct((s, d), _ACT), compiler_params=_cp(VMEM_BIG))(x, g)


def _rms_bwd(x, g, dy, dres, name, tm=512):
    s, d = x.shape
    tm = min(tm, s)
    has_res = dres is not None

    def body(*refs):
        if has_res:
            x_ref, g_ref, dy_ref, dres_ref, dx_ref, dxa_ref, dg_ref = refs
        else:
            x_ref, g_ref, dy_ref, dx_ref, dxa_ref, dg_ref = refs

        @pl.when(pl.program_id(0) == 0)
        def _():
            dg_ref[...] = jnp.zeros_like(dg_ref)

        xv = x_ref[...]
        dyv = dy_ref[...].astype(F32)
        r = lax.rsqrt(jnp.mean(xv * xv, axis=-1, keepdims=True) + EPS)
        xh = xv * r
        dyg = dyv * g_ref[...]
        dx = r * (dyg - xh * jnp.mean(dyg * xh, axis=-1, keepdims=True))
        if has_res:
            dx = dx + dres_ref[...]
        dx_ref[...] = dx
        dxa_ref[...] = dx.astype(dxa_ref.dtype)
        dg_ref[...] += jnp.sum(dyv * xh, axis=0, keepdims=True)

    row = pl.BlockSpec((tm, d), lambda i: (i, 0))
    vec = pl.BlockSpec((1, d), lambda i: (0, 0))
    in_specs = [row, vec, row] + ([row] if has_res else [])
    operands = [x, g, dy] + ([dres] if has_res else [])
    return pl.pallas_call(
        body, name=name, grid=(s // tm,), in_specs=in_specs, out_specs=[row, row, vec],
        out_shape=[jax.ShapeDtypeStruct((s, d), F32), jax.ShapeDtypeStruct((s, d), _ACT),
                   jax.ShapeDtypeStruct((1, d), F32)], compiler_params=_cp(VMEM_BIG))(*operands)


def _loss_head(h, g, target, name, tm=512):
    s, d = h.shape
    tm = min(tm, s)

    def body(h_ref, g_ref, t_ref, loss_ref, dh_ref, dha_ref, dg_ref):
        @pl.when(pl.program_id(0) == 0)
        def _():
            dg_ref[...] = jnp.zeros_like(dg_ref)
            loss_ref[...] = jnp.zeros_like(loss_ref)

        xv = h_ref[...]
        r = lax.rsqrt(jnp.mean(xv * xv, axis=-1, keepdims=True) + EPS)
        xh = xv * r
        err = xh * g_ref[...] - t_ref[...]
        loss_ref[...] += jnp.full(loss_ref.shape, 0.5 * jnp.sum(jnp.mean(err * err, axis=-1, keepdims=True)), F32)
        dyv = err * (1.0 / d)
        dyg = dyv * g_ref[...]
        dh = r * (dyg - xh * jnp.mean(dyg * xh, axis=-1, keepdims=True))
        dh_ref[...] = dh
        dha_ref[...] = dh.astype(dha_ref.dtype)
        dg_ref[...] += jnp.sum(dyv * xh, axis=0, keepdims=True)

    row = pl.BlockSpec((tm, d), lambda i: (i, 0))
    vec = pl.BlockSpec((1, d), lambda i: (0, 0))
    return pl.pallas_call(
        body, name=name, grid=(s // tm,), in_specs=[row, vec, row],
        out_specs=[pl.BlockSpec((1, 128), lambda i: (0, 0)), row, row, vec],
        out_shape=[jax.ShapeDtypeStruct((1, 128), F32), jax.ShapeDtypeStruct((s, d), F32),
                   jax.ShapeDtypeStruct((s, d), _ACT), jax.ShapeDtypeStruct((1, d), F32)],
        compiler_params=_cp(VMEM_BIG))(h, g, target)


def _gmlp_parts(pu, pv, lng, lnb):
    u = _gelu(pu)
    v = _gelu(pv)
    mu = jnp.mean(v, axis=-1, keepdims=True)
    vc = v - mu
    rstd = lax.rsqrt(jnp.mean(vc * vc, axis=-1, keepdims=True) + EPS)
    xhat = vc * rstd
    vn = xhat * lng + lnb
    return u, xhat, rstd, vn


def _gmlp_fwd(proj, lng, lnb, ws, bs3, name):
    s = proj.shape[0]

    def body(pu_ref, pv_ref, lng_ref, lnb_ref, ws_ref, bs_ref, o_ref):
        u, _, _, vn = _gmlp_parts(pu_ref[...], pv_ref[...], lng_ref[...], lnb_ref[...])
        causal = _iota((CHUNK, CHUNK), 0) >= _iota((CHUNK, CHUNK), 1)
        for g in range(A_GROUPS):
            sl = slice(g * A_GW, (g + 1) * A_GW)
            w = jnp.where(causal, ws_ref[g], 0.0)
            sv = _dot(w, vn[:, sl]) + bs_ref[g]
            o_ref[:, sl] = (u[:, sl] * sv).astype(o_ref.dtype)

    full = lambda shape: pl.BlockSpec(shape, lambda c: (0,) * len(shape))
    return pl.pallas_call(
        body, name=name, grid=(s // CHUNK,),
        in_specs=[pl.BlockSpec((CHUNK, D_INNER), lambda c: (c, 0)), pl.BlockSpec((CHUNK, D_INNER), lambda c: (c, 1)),
                  full((1, D_INNER)), full((1, D_INNER)), full((A_GROUPS, CHUNK, CHUNK)), full((A_GROUPS, CHUNK, 1))],
        out_specs=pl.BlockSpec((CHUNK, D_INNER), lambda c: (c, 0)),
        out_shape=jax.ShapeDtypeStruct((s, D_INNER + X_WIDTH), _ACT), compiler_params=_cp(VMEM_BIG))(proj, proj, lng, lnb, ws, bs3)


def _gmlp_bwd(proj, dcat, lng, lnb, ws, bs3, name):
    s = proj.shape[0]

    def body(pu_ref, pv_ref, dm_ref, lng_ref, lnb_ref, ws_ref, bs_ref, dp_ref, dws_ref, dbs_ref, dlng_ref, dlnb_ref, dvn_ref):
        @pl.when(pl.program_id(0) == 0)
        def _():
            dws_ref[...] = jnp.zeros_like(dws_ref)
            dbs_ref[...] = jnp.zeros_like(dbs_ref)
            dlng_ref[...] = jnp.zeros_like(dlng_ref)
            dlnb_ref[...] = jnp.zeros_like(dlnb_ref)

        pu, pv = pu_ref[...], pv_ref[...]
        lng = lng_ref[...]
        u, xhat, rstd, vn = _gmlp_parts(pu, pv, lng, lnb_ref[...])
        dm = dm_ref[...].astype(F32)
        causal = _iota((CHUNK, CHUNK), 0) >= _iota((CHUNK, CHUNK), 1)
        for g in range(A_GROUPS):
            sl = slice(g * A_GW, (g + 1) * A_GW)
            w = jnp.where(causal, ws_ref[g], 0.0)
            sv = _dot(w, vn[:, sl]) + bs_ref[g]
            dsv = dm[:, sl] * u[:, sl]
            dp_ref[:, sl] = (dm[:, sl] * sv * _gelu_grad(pu[:, sl])).astype(dp_ref.dtype)
            dvn_ref[:, sl] = _dot_tn(w, dsv)
            dws_ref[g] += jnp.where(causal, _dot_nt(dsv, vn[:, sl]), 0.0)
            dbs_ref[g] += jnp.sum(dsv, axis=-1, keepdims=True)
        dvn = dvn_ref[...]
        dlng_ref[...] += jnp.sum(dvn * xhat, axis=0, keepdims=True)
        dlnb_ref[...] += jnp.sum(dvn, axis=0, keepdims=True)
        dxh = dvn * lng
        dv = rstd * (dxh - jnp.mean(dxh, axis=-1, keepdims=True) - xhat * jnp.mean(dxh * xhat, axis=-1, keepdims=True))
        dp_ref[:, D_INNER:] = (dv * _gelu_grad(pv)).astype(dp_ref.dtype)

    full = lambda shape: pl.BlockSpec(shape, lambda c: (0,) * len(shape))
    return pl.pallas_call(
        body, name=name, grid=(s // CHUNK,),
        in_specs=[pl.BlockSpec((CHUNK, D_INNER), lambda c: (c, 0)), pl.BlockSpec((CHUNK, D_INNER), lambda c: (c, 1)),
                  pl.BlockSpec((CHUNK, D_INNER), lambda c: (c, 0)),
                  full((1, D_INNER)), full((1, D_INNER)), full((A_GROUPS, CHUNK, CHUNK)), full((A_GROUPS, CHUNK, 1))],
        out_specs=[pl.BlockSpec((CHUNK, 2 * D_INNER), lambda c: (c, 0)), full((A_GROUPS, CHUNK, CHUNK)),
                   full((A_GROUPS, CHUNK, 1)), full((1, D_INNER)), full((1, D_INNER))],
        out_shape=[jax.ShapeDtypeStruct((s, 2 * D_INNER + X_WIDTH), _ACT), jax.ShapeDtypeStruct((A_GROUPS, CHUNK, CHUNK), F32),
                   jax.ShapeDtypeStruct((A_GROUPS, CHUNK, 1), F32), jax.ShapeDtypeStruct((1, D_INNER), F32),
                   jax.ShapeDtypeStruct((1, D_INNER), F32)],
        scratch_shapes=[pltpu.VMEM((CHUNK, D_INNER), F32)],
        compiler_params=_cp(VMEM_BIG))(proj, proj, dcat, lng, lnb, ws, bs3)


_X_SCALE = 1.0 / math.sqrt(X_HD)


def _attn_fwd(proj, qblk, kv, cat, name, tm=512):
    s = proj.shape[0]
    tm = min(tm, s)

    def body(q_ref, kv_ref, cat_ref, o_ref):
        for h in range(X_HEADS):
            sl = slice(h * X_HD, (h + 1) * X_HD)
            k = kv_ref[:, sl]
            v = kv_ref[:, X_WIDTH + h * X_HD:X_WIDTH + (h + 1) * X_HD]
            sc = _dot_nt(q_ref[:, sl], k) * _X_SCALE
            e = jnp.exp(sc - jnp.max(sc, axis=-1, keepdims=True))
            p = e / jnp.sum(e, axis=-1, keepdims=True)
            o_ref[:, sl] = _dot(p, v).astype(o_ref.dtype)

    return pl.pallas_call(
        body, name=name, grid=(s // tm,),
        in_specs=[pl.BlockSpec((tm, X_WIDTH), lambda i: (i, qblk)), pl.BlockSpec((N_MEM, 2 * X_WIDTH), lambda i: (0, 0)),
                  pl.BlockSpec(memory_space=pl.ANY)],
        out_specs=pl.BlockSpec((tm, X_WIDTH), lambda i: (i, D_INNER // X_WIDTH)),
        out_shape=jax.ShapeDtypeStruct(cat.shape, cat.dtype), input_output_aliases={2: 0},
        compiler_params=_cp(VMEM_BIG))(proj, kv, cat)


def _attn_bwd(proj, qblk, kv, dcat, dproj, name, tm=512):
    s = proj.shape[0]
    tm = min(tm, s)

    def body(q_ref, kv_ref, do_ref, dproj_ref, dq_ref, dkv_ref):
        @pl.when(pl.program_id(0) == 0)
        def _():
            dkv_ref[...] = jnp.zeros_like(dkv_ref)

        for h in range(X_HEADS):
            sl = slice(h * X_HD, (h + 1) * X_HD)
            slv = slice(X_WIDTH + h * X_HD, X_WIDTH + (h + 1) * X_HD)
            q = q_ref[:, sl]
            k = kv_ref[:, sl]
            v = kv_ref[:, slv]
            do = do_ref[:, sl].astype(F32)
            sc = _dot_nt(q, k) * _X_SCALE
            e = jnp.exp(sc - jnp.max(sc, axis=-1, keepdims=True))
            p = e / jnp.sum(e, axis=-1, keepdims=True)
            dp = _dot_nt(do, v)
            ds = p * (dp - jnp.sum(dp * p, axis=-1, keepdims=True)) * _X_SCALE
            dq_ref[:, sl] = _dot(ds, k).astype(dq_ref.dtype)
            dkv_ref[:, sl] += _dot_tn(ds, q)
            dkv_ref[:, slv] += _dot_tn(p, do)

    return pl.pallas_call(
        body, name=name, grid=(s // tm,),
        in_specs=[pl.BlockSpec((tm, X_WIDTH), lambda i: (i, qblk)), pl.BlockSpec((N_MEM, 2 * X_WIDTH), lambda i: (0, 0)),
                  pl.BlockSpec((tm, X_WIDTH), lambda i: (i, 2)), pl.BlockSpec(memory_space=pl.ANY)],
        out_specs=[pl.BlockSpec((tm, X_WIDTH), lambda i: (i, qblk)), pl.BlockSpec((N_MEM, 2 * X_WIDTH), lambda i: (0, 0))],
        out_shape=[jax.ShapeDtypeStruct(dproj.shape, dproj.dtype), jax.ShapeDtypeStruct((N_MEM, 2 * X_WIDTH), F32)],
        input_output_aliases={3: 0}, compiler_params=_cp(VMEM_BIG))(proj, kv, dcat, dproj)


CONV_TC = 256
_XBC_BLK0 = D_INNER // CONV_TC


CONV_RB = 64
SUBLANES = 8


def _rows_before(cur, prev_last, j):
    rolled = pltpu.roll(cur, j, 0)
    head = jnp.where(_iota((SUBLANES, cur.shape[1]), 0) < j, pltpu.roll(prev_last, j, 0), rolled[:SUBLANES])
    return jnp.concatenate([head, rolled[SUBLANES:]], axis=0)


def _rows_after(cur, next_first, j):
    n = cur.shape[0]
    rolled = pltpu.roll(cur, n - j, 0)
    tail = jnp.where(_iota((SUBLANES, cur.shape[1]), 0) >= SUBLANES - j, pltpu.roll(next_first, SUBLANES - j, 0),
                     rolled[n - SUBLANES:])
    return jnp.concatenate([rolled[:n - SUBLANES], tail], axis=0)


def _conv_pre(x_ref, w_ref, b_ref, r0, prev_last):
    cur = x_ref[pl.ds(r0, CONV_RB), :]
    shifts = [_rows_before(cur, prev_last, j) for j in range(1, CONV_K)]
    pre = b_ref[...] + w_ref[CONV_K - 1:CONV_K, :] * cur
    for j in range(1, CONV_K):
        pre = pre + w_ref[CONV_K - 1 - j:CONV_K - j, :] * shifts[j - 1]
    return pre, cur, shifts


def _conv_fwd(proj, w, b, name):
    s = proj.shape[0]

    def body(x_ref, w_ref, b_ref, o_ref):
        xv = x_ref[...]
        rows = _iota(xv.shape, 0)
        pre = b_ref[...] + w_ref[CONV_K - 1:CONV_K, :] * xv
        for j in range(1, CONV_K):
            pre = pre + w_ref[CONV_K - 1 - j:CONV_K - j, :] * jnp.where(rows >= j, pltpu.roll(xv, j, 0), 0.0)
        o_ref[...] = pre * _sigmoid(pre)

    return pl.pallas_call(
        body, name=name, grid=(CONV_DIM // CONV_TC,),
        in_specs=[pl.BlockSpec((s, CONV_TC), lambda j: (0, _XBC_BLK0 + j)), pl.BlockSpec((CONV_K, CONV_TC), lambda j: (0, j)),
                  pl.BlockSpec((1, CONV_TC), lambda j: (0, j))],
        out_specs=pl.BlockSpec((s, CONV_TC), lambda j: (0, j)),
        out_shape=jax.ShapeDtypeStruct((s, CONV_DIM), F32), compiler_params=_cp(VMEM_BIG))(proj, w, b)


def _conv_bwd(proj, w, b, dxbc, dproj, name):
    s = proj.shape[0]

    nb = s // CONV_RB

    def body(x_ref, w_ref, b_ref, d_ref, dproj_ref, dx_ref, dw_ref, db_ref, dpre_ref):
        def fold(v):
            out = v[:SUBLANES]
            for t in range(1, CONV_RB // SUBLANES):
                out = out + v[t * SUBLANES:(t + 1) * SUBLANES]
            return out

        def first(i, carry):
            prev_last, acc = carry
            r0 = pl.multiple_of(i * CONV_RB, CONV_RB)
            pre, cur, shifts = _conv_pre(x_ref, w_ref, b_ref, r0, prev_last)
            sig = _sigmoid(pre)
            dpre = d_ref[pl.ds(r0, CONV_RB), :] * (sig * (1.0 + pre * (1.0 - sig)))
            dpre_ref[pl.ds(r0, CONV_RB), :] = dpre
            taps = [cur] + shifts
            acc = tuple(a + fold(dpre * t) for a, t in zip(acc[:CONV_K], taps)) + (acc[CONV_K] + fold(dpre),)
            return cur[CONV_RB - SUBLANES:], acc

        zero8 = jnp.zeros((SUBLANES, CONV_TC), F32)
        _, acc = lax.fori_loop(0, nb, first, (zero8, (zero8,) * (CONV_K + 1)))
        for j in range(CONV_K):
            dw_ref[CONV_K - 1 - j:CONV_K - j, :] = jnp.sum(acc[j], axis=0, keepdims=True)
        db_ref[...] = jnp.sum(acc[CONV_K], axis=0, keepdims=True)

        def second(i, next_first):
            r0 = pl.multiple_of((nb - 1 - i) * CONV_RB, CONV_RB)
            cur = dpre_ref[pl.ds(r0, CONV_RB), :]
            dx = w_ref[CONV_K - 1:CONV_K, :] * cur
            for j in range(1, CONV_K):
                dx = dx + w_ref[CONV_K - 1 - j:CONV_K - j, :] * _rows_after(cur, next_first, j)
            dx_ref[pl.ds(r0, CONV_RB), :] = dx.astype(dx_ref.dtype)
            return cur[:SUBLANES]

        lax.fori_loop(0, nb, second, zero8)

    return pl.pallas_call(
        body, name=name, grid=(CONV_DIM // CONV_TC,),
        in_specs=[pl.BlockSpec((s, CONV_TC), lambda j: (0, _XBC_BLK0 + j)), pl.BlockSpec((CONV_K, CONV_TC), lambda j: (0, j)),
                  pl.BlockSpec((1, CONV_TC), lambda j: (0, j)), pl.BlockSpec((s, CONV_TC), lambda j: (0, j)),
                  pl.BlockSpec(memory_space=pl.ANY)],
        out_specs=[pl.BlockSpec((s, CONV_TC), lambda j: (0, _XBC_BLK0 + j)), pl.BlockSpec((CONV_K, CONV_TC), lambda j: (0, j)),
                   pl.BlockSpec((1, CONV_TC), lambda j: (0, j))],
        out_shape=[jax.ShapeDtypeStruct(dproj.shape, dproj.dtype), jax.ShapeDtypeStruct((CONV_K, CONV_DIM), F32),
                   jax.ShapeDtypeStruct((1, CONV_DIM), F32)], input_output_aliases={4: 0},
        scratch_shapes=[pltpu.VMEM((s, CONV_TC), F32)],
        compiler_params=_cp(VMEM_BIG))(proj, w, b, dxbc, dproj)


def _ssd_common(dtc_ref, br_ref, ar_ref, csb_ref, cst_ref, csf_ref):
    a_row = -jnp.exp(ar_ref[...])
    dt_c = _softplus(dtc_ref[...] + br_ref[...])
    tril = _iota((CHUNK, CHUNK), 0) >= _iota((CHUNK, CHUNK), 1)
    cs = _sel_dot(tril, dt_c * a_row)
    cst_ref[...] = cs.T
    e64 = (jnp.right_shift(_iota((HPAD, D_INNER), 1), 6) == _iota((HPAD, D_INNER), 0)).astype(jnp.bfloat16)
    e128 = jnp.right_shift(_iota((HPAD, SSM_HEADS * CHUNK), 1), 7) == _iota((HPAD, SSM_HEADS * CHUNK), 0)
    csb_ref[...] = _dot_sel(cs, e128)
    dt_full = _dot_sel(dt_c, e64)
    csf_ref[...] = _dot_sel(cs, e64)
    cs_full = csf_ref[...]
    cs_last = csf_ref[CHUNK - 1:CHUNK, :]
    e_full = jnp.exp(cs_full)
    f_full = jnp.exp(cs_last - cs_full)
    gamma = jnp.exp(cs_last)
    return a_row, dt_c, cs, dt_full, e_full, f_full, gamma, e64


def _ssd_lambda(csb_ref, cst_ref, h, causal):
    diff = csb_ref[:, h * CHUNK:(h + 1) * CHUNK] - cst_ref[h:h + 1, :]
    return jnp.exp(jnp.where(causal, diff, -1e30))


_SSD_VEC_SPECS = lambda: [pl.BlockSpec((1, HPAD), lambda c: (0, 0)), pl.BlockSpec((1, HPAD), lambda c: (0, 0)),
                          pl.BlockSpec((1, D_INNER), lambda c: (0, 0))]


def _ssd_fwd(xbc, dtc, bias_row, alog_row, dfull, name):
    s = xbc.shape[0]
    nc = s // CHUNK

    def body(xbc_ref, dtc_ref, br_ref, ar_ref, df_ref, y_ref, st_ref, ht_ref, csb_ref, cst_ref, csf_ref):
        @pl.when(pl.program_id(0) == 0)
        def _():
            ht_ref[...] = jnp.zeros_like(ht_ref)

        _, _, _, dt_full, e_full, f_full, gamma, _ = _ssd_common(dtc_ref, br_ref, ar_ref, csb_ref, cst_ref, csf_ref)
        x = xbc_ref[:, :D_INNER]
        xdt = x * dt_full
        st_ref[...] = ht_ref[...]
        causal = _iota((CHUNK, CHUNK), 0) >= _iota((CHUNK, CHUNK), 1)
        lo = _iota((CHUNK, CHUNK), 1) < SSM_P
        for g in range(SSM_GROUPS):
            gs = slice(g * SSM_GW, (g + 1) * SSM_GW)
            bg = xbc_ref[:, D_INNER + g * SSM_N:D_INNER + (g + 1) * SSM_N]
            cg = xbc_ref[:, D_INNER + SSM_GROUPS * SSM_N + g * SSM_N:D_INNER + SSM_GROUPS * SSM_N + (g + 1) * SSM_N]
            ht = ht_ref[:, gs]
            cb = _dot_nt(cg, bg)
            yoff = e_full[:, gs] * _dot(cg, ht)
            for jp in range(SSM_GW // CHUNK):
                j = g * (SSM_GW // CHUNK) + jp
                ps = slice(j * CHUNK, (j + 1) * CHUNK)
                x2 = xdt[:, ps]
                y0 = _dot(cb * _ssd_lambda(csb_ref, cst_ref, 2 * j, causal), x2)
                y1 = _dot(cb * _ssd_lambda(csb_ref, cst_ref, 2 * j + 1, causal), x2)
                y_ref[:, ps] = (jnp.where(lo, y0, y1) + yoff[:, jp * CHUNK:(jp + 1) * CHUNK]
                                + x[:, ps] * df_ref[:, ps])
            ht_ref[:, gs] = gamma[:, gs] * ht + _dot_tn(bg, xdt[:, gs] * f_full[:, gs])

    return pl.pallas_call(
        body, name=name, grid=(nc,),
        in_specs=[pl.BlockSpec((CHUNK, CONV_DIM), lambda c: (c, 0)), pl.BlockSpec((CHUNK, HPAD), lambda c: (c, 0))]
                 + _SSD_VEC_SPECS(),
        out_specs=[pl.BlockSpec((CHUNK, D_INNER), lambda c: (c, 0)), pl.BlockSpec((None, SSM_N, D_INNER), lambda c: (c, 0, 0))],
        out_shape=[jax.ShapeDtypeStruct((s, D_INNER), F32), jax.ShapeDtypeStruct((nc, SSM_N, D_INNER), F32)],
        scratch_shapes=[pltpu.VMEM((SSM_N, D_INNER), F32), pltpu.VMEM((CHUNK, SSM_HEADS * CHUNK), F32),
                        pltpu.VMEM((HPAD, CHUNK), F32), pltpu.VMEM((CHUNK, D_INNER), F32)],
        compiler_params=_cp(VMEM_BIG))(xbc, dtc, bias_row, alog_row, dfull)


def _ssd_bwd(xbc, dtc, bias_row, alog_row, dfull, dy, states, name):
    s = xbc.shape[0]
    nc = s // CHUNK
    rev = lambda c: nc - 1 - c

    def body(xbc_ref, dtc_ref, br_ref, ar_ref, df_ref, dy_ref, st_ref,
             dxbc_ref, ddt_ref, dalog_ref, dd_ref, dbias_ref,
             dht_ref, csb_ref, cst_ref, csf_ref, ddf_ref, dxs_ref, dcsf_ref, dcsl_ref):
        step = pl.program_id(0)

        @pl.when(step == 0)
        def _():
            dht_ref[...] = jnp.zeros_like(dht_ref)
            ddf_ref[...] = jnp.zeros_like(ddf_ref)
            dalog_ref[...] = jnp.zeros_like(dalog_ref)
            dbias_ref[...] = jnp.zeros_like(dbias_ref)
            dd_ref[...] = jnp.zeros_like(dd_ref)

        a_row, dt_c, _, dt_full, e_full, f_full, gamma, e64 = _ssd_common(dtc_ref, br_ref, ar_ref, csb_ref, cst_ref, csf_ref)
        x = xbc_ref[:, :D_INNER]
        xdt = x * dt_full
        dy_all = dy_ref[...]
        ddf_ref[...] += jnp.broadcast_to(jnp.sum(dy_all * x, axis=0, keepdims=True), ddf_ref.shape)
        causal = _iota((CHUNK, CHUNK), 0) >= _iota((CHUNK, CHUNK), 1)
        lo = _iota((CHUNK, CHUNK), 1) < SSM_P
        head_lane = _iota((CHUNK, HPAD), 1)
        head_row = _iota((HPAD, CHUNK), 0)
        dcs_heads = jnp.zeros((CHUNK, HPAD), F32)
        dcs_cols = jnp.zeros((HPAD, CHUNK), F32)
        for g in range(SSM_GROUPS):
            gs = slice(g * SSM_GW, (g + 1) * SSM_GW)
            b0 = D_INNER + g * SSM_N
            c0 = D_INNER + SSM_GROUPS * SSM_N + g * SSM_N
            bg = xbc_ref[:, b0:b0 + SSM_N]
            cg = xbc_ref[:, c0:c0 + SSM_N]
            ht = st_ref[:, gs]
            dht = dht_ref[:, gs]
            dyg = dy_all[:, gs]
            eg, fg, gg = e_full[:, gs], f_full[:, gs], gamma[:, gs]
            z = _dot(cg, ht)
            dz = dyg * eg
            dcg = _dot_nt(dz, ht)
            dht_new = _dot_tn(cg, dz) + gg * dht
            xf = xdt[:, gs] * fg
            dxf = _dot(bg, dht)
            dbg = _dot_nt(xf, dht)
            dff = dxf * xf
            dcsf_ref[:, gs] = dyg * eg * z - dff
            dcsl_ref[:, gs] = jnp.broadcast_to(
                jnp.sum(dff, axis=0, keepdims=True) + jnp.sum(dht * ht, axis=0, keepdims=True) * gg, (8, SSM_GW))
            cb = _dot_nt(cg, bg)
            dcb = jnp.zeros((CHUNK, CHUNK), F32)
            for jp in range(SSM_GW // CHUNK):
                j = g * (SSM_GW // CHUNK) + jp
                ps = slice(j * CHUNK, (j + 1) * CHUNK)
                x2 = xdt[:, ps]
                dy2 = dy_all[:, ps]
                dxh = []
                for hh in range(2):
                    h = 2 * j + hh
                    lam = _ssd_lambda(csb_ref, cst_ref, h, causal)
                    mh = cb * lam
                    dyh = jnp.where(lo, dy2, 0.0) if hh == 0 else jnp.where(lo, 0.0, dy2)
                    dm = _dot_nt(dyh, x2)
                    dcb = dcb + dm * lam
                    gm = dm * mh
                    dcs_heads = dcs_heads + jnp.where(head_lane == h, jnp.sum(gm, axis=1, keepdims=True), 0.0)
                    dcs_cols = dcs_cols + jnp.where(head_row == h, jnp.sum(gm, axis=0, keepdims=True), 0.0)
                    dxh.append(_dot_tn(mh, dy2))
                dxs_ref[:, ps] = jnp.where(lo, dxh[0], dxh[1]) + dxf[:, jp * CHUNK:(jp + 1) * CHUNK] * fg[:, jp * CHUNK:(jp + 1) * CHUNK]
            dxbc_ref[:, b0:b0 + SSM_N] = (dbg + _dot_tn(dcb, cg)).astype(dxbc_ref.dtype)
            dxbc_ref[:, c0:c0 + SSM_N] = (dcg + _dot(dcb, bg)).astype(dxbc_ref.dtype)
            dht_ref[:, gs] = dht_new
        dxs = dxs_ref[...]
        dcs_heads = dcs_heads - dcs_cols.T + _dot_sel(dcsf_ref[...], e64, ((1,), (1,)))
        dcs_last = _dot_sel(dcsl_ref[...], e64, ((1,), (1,)))
        dcs_heads = dcs_heads + jnp.where(_iota((CHUNK, HPAD), 0) == CHUNK - 1, dcs_last[0:1, :], 0.0)
        triu = _iota((CHUNK, CHUNK), 0) <= _iota((CHUNK, CHUNK), 1)
        dda = _sel_dot(triu, dcs_heads)
        ddt = dda * a_row + _dot_sel(dxs * x, e64, ((1,), (1,)))
        dxbc_ref[:, :D_INNER] = (dxs * dt_full + dy_all * df_ref[...]).astype(dxbc_ref.dtype)
        dalog_ref[...] += jnp.sum(dda * dt_c, axis=0, keepdims=True) * a_row
        ddt_raw = ddt * _sigmoid(dtc_ref[...] + br_ref[...])
        ddt_ref[...] = ddt_raw.astype(ddt_ref.dtype)
        dbias_ref[...] += jnp.sum(ddt_raw, axis=0, keepdims=True)

        @pl.when(step == nc - 1)
        def _():
            dd_ref[...] = _dot_sel(ddf_ref[...], e64, ((1,), (1,)))[0:1, :]

    vec = pl.BlockSpec((1, HPAD), lambda c: (0, 0))
    return pl.pallas_call(
        body, name=name, grid=(nc,),
        in_specs=[pl.BlockSpec((CHUNK, CONV_DIM), lambda c: (rev(c), 0)), pl.BlockSpec((CHUNK, HPAD), lambda c: (rev(c), 0))]
                 + _SSD_VEC_SPECS()
                 + [pl.BlockSpec((CHUNK, D_INNER), lambda c: (rev(c), 0)),
                    pl.BlockSpec((None, SSM_N, D_INNER), lambda c: (rev(c), 0, 0))],
        out_specs=[pl.BlockSpec((CHUNK, CONV_DIM), lambda c: (rev(c), 0)), pl.BlockSpec((CHUNK, HPAD), lambda c: (rev(c), 0)),
                   vec, vec, vec],
        out_shape=[jax.ShapeDtypeStruct((s, CONV_DIM), F32), jax.ShapeDtypeStruct((s, HPAD), _ACT),
                   jax.ShapeDtypeStruct((1, HPAD), F32), jax.ShapeDtypeStruct((1, HPAD), F32),
                   jax.ShapeDtypeStruct((1, HPAD), F32)],
        scratch_shapes=[pltpu.VMEM((SSM_N, D_INNER), F32), pltpu.VMEM((CHUNK, SSM_HEADS * CHUNK), F32),
                        pltpu.VMEM((HPAD, CHUNK), F32), pltpu.VMEM((CHUNK, D_INNER), F32),
                        pltpu.VMEM((8, D_INNER), F32), pltpu.VMEM((CHUNK, D_INNER), F32),
                        pltpu.VMEM((CHUNK, D_INNER), F32), pltpu.VMEM((8, D_INNER), F32)],
        compiler_params=_cp(VMEM_BIG))(xbc, dtc, bias_row, alog_row, dfull, dy, states)


def _gate_fwd(y, proj, gn, name, tm=512):
    s = y.shape[0]
    tm = min(tm, s)

    def body(y_ref, z_ref, gn_ref, o_ref):
        for g in range(SSM_GROUPS):
            gs = slice(g * SSM_GW, (g + 1) * SSM_GW)
            z = z_ref[:, gs]
            t = y_ref[:, gs] * (z * _sigmoid(z))
            r = lax.rsqrt(jnp.mean(t * t, axis=-1, keepdims=True) + EPS)
            o_ref[:, gs] = (t * r * gn_ref[:, gs]).astype(o_ref.dtype)

    row = pl.BlockSpec((tm, D_INNER), lambda i: (i, 0))
    return pl.pallas_call(
        body, name=name, grid=(s // tm,), in_specs=[row, row, pl.BlockSpec((1, D_INNER), lambda i: (0, 0))],
        out_specs=row, out_shape=jax.ShapeDtypeStruct((s, D_INNER + X_WIDTH), _ACT),
        compiler_params=_cp(VMEM_BIG))(y, proj, gn)


def _gate_bwd(y, proj, gn, dcat, name, tm=512):
    s = y.shape[0]
    tm = min(tm, s)

    def body(y_ref, z_ref, gn_ref, dm_ref, dy_ref, dz_ref, dgn_ref):
        @pl.when(pl.program_id(0) == 0)
        def _():
            dgn_ref[...] = jnp.zeros_like(dgn_ref)

        for g in range(SSM_GROUPS):
            gs = slice(g * SSM_GW, (g + 1) * SSM_GW)
            z = z_ref[:, gs]
            yv = y_ref[:, gs]
            sig = _sigmoid(z)
            sz = z * sig
            t = yv * sz
            r = lax.rsqrt(jnp.mean(t * t, axis=-1, keepdims=True) + EPS)
            th = t * r
            dm = dm_ref[:, gs].astype(F32)
            dmg = dm * gn_ref[:, gs]
            dt_ = r * (dmg - th * jnp.mean(dmg * th, axis=-1, keepdims=True))
            dgn_ref[:, gs] += jnp.sum(dm * th, axis=0, keepdims=True)
            dy_ref[:, gs] = dt_ * sz
            dz_ref[:, gs] = (dt_ * yv * (sig * (1.0 + z * (1.0 - sig)))).astype(dz_ref.dtype)

    row = pl.BlockSpec((tm, D_INNER), lambda i: (i, 0))
    vec = pl.BlockSpec((1, D_INNER), lambda i: (0, 0))
    return pl.pallas_call(
        body, name=name, grid=(s // tm,), in_specs=[row, row, vec, row], out_specs=[row, row, vec],
        out_shape=[jax.ShapeDtypeStruct((s, D_INNER), F32), jax.ShapeDtypeStruct((s, 6 * D_MODEL), _ACT),
                   jax.ShapeDtypeStruct((1, D_INNER), F32)], compiler_params=_cp(VMEM_BIG))(y, proj, gn, dcat)


def _block_of(kind, width):
    if kind == "col":
        return lambda ref, j: ref.at[:, :, pl.ds(pl.multiple_of(j * width, 128), width)]
    if kind == "row":
        return lambda ref, j: ref.at[:, pl.ds(pl.multiple_of(j * width, 8), width), :]
    return lambda ref, j: ref.at[j]


def _coords():
    return lax.axis_index("x"), lax.axis_index("y"), lax.axis_index("c")


def _rel_chip(x, y, k):
    return (1 - x if k & 1 else x), (1 - y if k & 2 else y)


def _all_gather_body(ins, outs, send_sems, recv_sems, local_sems, blocks):
    n = len(ins)
    x, y, c = _coords()
    sibling = (x, y, 1 - c)
    via = (x + (1 - c) * (1 - 2 * x), y + c * (1 - 2 * y))
    onto = (x + c * (1 - 2 * x), y + (1 - c) * (1 - 2 * y))

    def copy(t, k, chip, core, to, src=None):
        dst = blocks[t](outs[t], 4 * chip[0] + 2 * chip[1] + core)
        return pltpu.make_async_remote_copy(
            src_ref=dst if src is None else src, dst_ref=dst, send_sem=send_sems.at[t, k],
            recv_sem=recv_sems.at[t, k], device_id=to, device_id_type=MESH)

    started = []
    for t in range(n):
        mine = pltpu.make_async_copy(ins[t], blocks[t](outs[t], 4 * x + 2 * y + c), local_sems.at[t])
        mine.start()
        started.append(mine)
    sends = []
    for t in range(n):
        for k in range(3):
            px, py = _rel_chip(x, y, k)
            cp = copy(t, k, (x, y), c, (px, py, 1 - c if k == 0 else c), src=ins[t])
            cp.start()
            sends.append(cp)
    for t in range(n):
        for k in (1, 2):
            chip = _rel_chip(x, y, k)
            copy(t, k, chip, c, sibling).wait_recv()
            fwd = copy(t, 3 + k, chip, c, sibling)
            fwd.start()
            sends.append(fwd)
        hop = copy(t, 3, via, c, (*onto, c))
        hop.start()
        sends.append(hop)
    for t in range(n):
        diagonal = _rel_chip(x, y, 3)
        copy(t, 3, diagonal, c, sibling).wait_recv()
        fwd = copy(t, 6, diagonal, c, sibling)
        fwd.start()
        sends.append(fwd)
    for t in range(n):
        copy(t, 0, (x, y), 1 - c, sibling).wait_recv()
        for k in range(1, 4):
            copy(t, 3 + k, _rel_chip(x, y, k), 1 - c, sibling).wait_recv()
    for cp in sends:
        cp.wait_send()
    for mine in started:
        mine.wait()


def _handshake(peers):
    barrier = pltpu.get_barrier_semaphore()
    for peer in peers:
        pl.semaphore_signal(barrier, inc=1, device_id=peer, device_id_type=MESH)
    pl.semaphore_wait(barrier, len(peers))


def _gather_peers():
    x, y, c = _coords()
    return [(x, y, 1 - c)] + [(*_rel_chip(x, y, k), c) for k in (1, 2)]


SEQ_ID_GATHER, SEQ_ID_SIBLING, SEQ_ID_CHIPS = 1, 2, 3


def _sequencer_call(body, peers, operands, out_types, sems, name, collective_id, after=()):
    n_in, n_out, n_after = len(operands), len(out_types), len(after)

    def launch(*refs):
        _handshake(peers())
        body(refs[:n_in], refs[n_in + n_after:n_in + n_after + n_out], *refs[n_in + n_after + n_out:])

    return pl.kernel(
        launch, name=name, out_type=out_types, mesh=plsc.ScalarSubcoreMesh(axis_name="seq", num_cores=1),
        scratch_types=sems, compiler_params=pltpu.CompilerParams(collective_id=collective_id))(*operands, *after)


def _all_gather_seq(shards, layouts, name, after=()):
    n = len(shards)
    blocks = [_block_of(kind, width) for kind, width, _ in layouts]
    return _sequencer_call(
        lambda ins, outs, *sems: _all_gather_body(ins, outs, *sems, blocks), _gather_peers, shards,
        [jax.ShapeDtypeStruct(shape, sh.dtype) for sh, (_, _, shape) in zip(shards, layouts)],
        [pltpu.SemaphoreType.DMA((n, 7)), pltpu.SemaphoreType.DMA((n, 7)), pltpu.SemaphoreType.DMA((n,))],
        name, SEQ_ID_GATHER, after)


def _tie(small, after, name):
    del name
    return lax.optimization_barrier((small, *after))[0]


def _rs_to_sibling(grads, layouts, name, after=()):
    n = len(grads)
    blocks = [_block_of(kind, width) for kind, width, _ in layouts]

    def body(ins, outs, send_sems, recv_sems):
        x, y, c = _coords()
        sibling = (x, y, 1 - c)
        cps = []
        for t in range(n):
            for k in range(4):
                px, py = _rel_chip(x, y, k)
                cp = pltpu.make_async_remote_copy(
                    src_ref=blocks[t](ins[t], 4 * px + 2 * py + (1 - c)), dst_ref=outs[t].at[k],
                    send_sem=send_sems.at[t, k], recv_sem=recv_sems.at[t, k], device_id=sibling, device_id_type=MESH)
                cp.start()
                cps.append(cp)
        for cp in cps:
            cp.wait_recv()
        for cp in cps:
            cp.wait_send()

    def sibling_only():
        x, y, c = _coords()
        return [(x, y, 1 - c)]

    return _sequencer_call(
        body, sibling_only, grads,
        [jax.ShapeDtypeStruct((4,) + shape, g.dtype) for g, (_, _, shape) in zip(grads, layouts)],
        [pltpu.SemaphoreType.DMA((n, 4)), pltpu.SemaphoreType.DMA((n, 4))], name, SEQ_ID_SIBLING, after)


def _rs_chip_sum(grad, recv, layout, xyc, name):
    kind, width, shape = layout
    r, ccols = shape

    def src_index(k, xyc_ref):
        px = jnp.where(k % 2 == 1, 1 - xyc_ref[0], xyc_ref[0])
        py = jnp.where(k // 2 == 1, 1 - xyc_ref[1], xyc_ref[1])
        return 4 * px + 2 * py + xyc_ref[2]

    if kind == "col":
        g_spec = pl.BlockSpec((r, ccols), lambda k, s_: (0, src_index(k, s_)))
    elif kind == "row":
        g_spec = pl.BlockSpec((r, ccols), lambda k, s_: (src_index(k, s_), 0))
    else:
        g_spec = pl.BlockSpec((None, r, ccols), lambda k, s_: (src_index(k, s_), 0, 0))

    def body(xyc_ref, g_ref, r_ref, o_ref):
        o_ref[...] = (g_ref[...].astype(F32) + r_ref[...].astype(F32)).astype(o_ref.dtype)

    slot = pl.BlockSpec((None, r, ccols), lambda k, s_: (k, 0, 0))
    return pl.pallas_call(
        body, name=name,
        grid_spec=pltpu.PrefetchScalarGridSpec(num_scalar_prefetch=1, grid=(4,), in_specs=[g_spec, slot], out_specs=slot),
        out_shape=jax.ShapeDtypeStruct((4, r, ccols), grad.dtype), compiler_params=_cp(VMEM_BIG))(xyc, grad, recv)


def _rs_across_chips(parts, name):
    n = len(parts)

    def body(ins, outs, send_sems, recv_sems):
        x, y, c = _coords()
        cps = []
        for t in range(n):
            for k in range(1, 4):
                px, py = _rel_chip(x, y, k)
                cp = pltpu.make_async_remote_copy(
                    src_ref=ins[t].at[k], dst_ref=outs[t].at[k - 1], send_sem=send_sems.at[t, k - 1],
                    recv_sem=recv_sems.at[t, k - 1], device_id=(px, py, c), device_id_type=MESH)
                cp.start()
                cps.append(cp)
        for cp in cps:
            cp.wait_recv()
        for cp in cps:
            cp.wait_send()

    def other_chips():
        x, y, c = _coords()
        return [(*_rel_chip(x, y, k), c) for k in range(1, 4)]

    return _sequencer_call(
        body, other_chips, parts, [jax.ShapeDtypeStruct((3,) + p.shape[1:], p.dtype) for p in parts],
        [pltpu.SemaphoreType.DMA((n, 3)), pltpu.SemaphoreType.DMA((n, 3))], name, SEQ_ID_CHIPS)


def _adamw_math(w, g, m, v):
    m = ADAM_B1 * m + (1.0 - ADAM_B1) * g
    v = ADAM_B2 * v + (1.0 - ADAM_B2) * jnp.square(g)
    m_hat = m / (1.0 - ADAM_B1 ** ADAM_STEP)
    v_hat = v / (1.0 - ADAM_B2 ** ADAM_STEP)
    delta = -ADAM_LR * (m_hat / (jnp.sqrt(v_hat) + ADAM_EPS) + ADAM_WD * w)
    return delta, m, v


def _row_tile(rows, cap):
    best = None
    for cand in range(8, min(rows, cap) + 1, 8):
        if rows % cand == 0:
            best = cand
    assert best is not None, rows
    return best


def _adamw(w, m, v, parts, name, layer=None, prev=None, tr=256):
    r, ccols = w.shape[-2:]
    npart = len(parts)
    if r % 8 == 0:
        tr, tc = _row_tile(r, tr), ccols
        steps, at = r // tr, (lambda i: (i, 0))
    else:
        tr, tc = r, 256
        assert ccols % tc == 0
        steps, at = ccols // tc, (lambda i: (0, i))

    def spec(lead):
        if lead is None:
            return pl.BlockSpec((tr, tc), at)
        return pl.BlockSpec((None, tr, tc), lambda i: (lead,) + at(i))

    wspec = lambda: spec(layer)
    pspec = spec

    def body(*refs):
        w_ref, m_ref, v_ref = refs[:3]
        p_refs = refs[3:3 + npart]
        outs = refs[len(refs) - 4:]
        g = p_refs[0][...].astype(F32)
        for p_ref in p_refs[1:]:
            g = g + p_ref[...].astype(F32)
        delta, mn, vn = _adamw_math(w_ref[...], g, m_ref[...], v_ref[...])
        outs[0][...] = g
        outs[1][...] = delta
        outs[2][...] = mn
        outs[3][...] = vn

    operands = [w, m, v] + [p for p, _ in parts]
    in_specs = [wspec(), wspec(), wspec()] + [pspec(lead) for _, lead in parts]
    aliases = {}
    if prev is not None:
        for i, p in enumerate(prev):
            aliases[len(operands)] = i
            operands.append(p)
            in_specs.append(pl.BlockSpec(memory_space=pl.ANY))
    return pl.pallas_call(
        body, name=name, grid=(steps,), in_specs=in_specs, out_specs=[wspec()] * 4,
        out_shape=[jax.ShapeDtypeStruct(w.shape, F32)] * 4, input_output_aliases=aliases)(*operands)


def _sum8(buf, name):
    _, r, ccols = buf.shape

    def body(b_ref, o_ref):
        acc = b_ref[0]
        for j in range(1, N_DEV):
            acc = acc + b_ref[j]
        o_ref[...] = acc

    tr = _row_tile(r, 256)
    return pl.pallas_call(
        body, name=name, grid=(r // tr,), in_specs=[pl.BlockSpec((N_DEV, tr, ccols), lambda i: (0, i, 0))],
        out_specs=pl.BlockSpec((tr, ccols), lambda i: (i, 0)), out_shape=jax.ShapeDtypeStruct((r, ccols), F32))(buf)


def _pack(arrays):
    pieces, layout, off = [], [], 0
    for a in arrays:
        n = a.size
        padded = -(-n // 1024) * 1024
        flat = a.reshape(-1).astype(F32)
        if padded != n:
            flat = jnp.pad(flat, (0, padded - n))
        pieces.append(flat.reshape(padded // 128, 128))
        layout.append((off, n, a.shape))
        off += padded // 128
    return jnp.concatenate(pieces, axis=0), layout


def _unpack(packed, layout):
    out = []
    for off, n, shape in layout:
        rows = -(-n // 1024) * 8
        out.append(packed[off:off + rows].reshape(-1)[:n].reshape(shape))
    return out


def kernel(x, mem, norm_mix, norm_ffn, mem_norm, w_kv, w_out, w_ffn1, w_ffn2, a_in, a_ln_g, a_ln_b, a_ws, a_bs, b_in, b_conv_w, b_conv_b, b_dt_bias, b_a_log, b_d, b_gnorm, final_norm, loss_target, m_norm_mix, m_norm_ffn, m_mem_norm, m_w_kv, m_w_out, m_w_ffn1, m_w_ffn2, m_a_in, m_a_ln_g, m_a_ln_b, m_a_ws, m_a_bs, m_b_in, m_b_conv_w, m_b_conv_b, m_b_dt_bias, m_b_a_log, m_b_d, m_b_gnorm, m_final_norm, v_norm_mix, v_norm_ffn, v_mem_norm, v_w_kv, v_w_out, v_w_ffn1, v_w_ffn2, v_a_in, v_a_ln_g, v_a_ln_b, v_a_ws, v_a_bs, v_b_in, v_b_conv_w, v_b_conv_b, v_b_dt_bias, v_b_a_log, v_b_d, v_b_gnorm, v_final_norm):
    s = x.shape[1]
    xs = x.reshape(s, D_MODEL)
    mems = mem.reshape(N_MEM, D_MODEL)
    target = loss_target.reshape(s, D_MODEL)
    ax, ay, ac = lax.axis_index("x"), lax.axis_index("y"), lax.axis_index("c")
    me = 4 * ax + 2 * ay + ac
    xyc = jnp.stack([ax, ay, ac]).astype(jnp.int32)

    b_cols = b_in.shape[2]
    act = lambda a: a.astype(_ACT)
    lay_f1, lay_f2 = ("col", 512, (1, D_MODEL, D_FF)), ("row", 512, (1, D_FF, D_MODEL))
    lay_out, lay_kv = ("row", 384, (1, 3 * D_MODEL, D_MODEL)), ("col", 256, (1, D_MODEL, 2 * X_WIDTH))
    small_w_pack = _pack([b_conv_w[0], b_conv_b[0], b_gnorm[0]])[0]
    (WA,) = _all_gather_seq([act(a_in)], [("col", 640, (1, D_MODEL, 5 * D_MODEL))], "ag_proj_a")
    wo0, wkv0 = _all_gather_seq([act(w_out[0:1]), act(w_kv[0:1])], [lay_out, lay_kv], "ag_out0")
    w1_0, w2_0 = _all_gather_seq([act(w_ffn1[0:1]), act(w_ffn2[0:1])], [lay_f1, lay_f2], "ag_ffn0")
    a0 = _rms_fwd(xs, norm_mix[0].reshape(1, -1), "mix_norm0")
    tr_b = lambda a: jnp.swapaxes(a, 1, 2)
    wbt_blk, small_w = _all_gather_seq(
        [act(tr_b(b_in)[0]), small_w_pack],
        [("blk", 0, (N_DEV, b_cols, D_MODEL)), ("blk", 0, (N_DEV, 32, 128))], "ag_proj_b", after=[a0])
    wo1, wkv1 = _all_gather_seq([act(w_out[1:2]), act(w_kv[1:2])], [lay_out, lay_kv], "ag_out1", after=[a0])
    w1_1, w2_1 = _all_gather_seq([act(w_ffn1[1:2]), act(w_ffn2[1:2])], [lay_f1, lay_f2], "ag_ffn1", after=[a0])
    W1, W2, WO, WKV = [w1_0, w1_1], [w2_0, w2_1], [wo0, wo1], [wkv0, wkv1]
    dt0 = D_INNER + CONV_DIM

    row = lambda a: a.reshape(1, -1)
    nmix = [row(norm_mix[0]), row(norm_mix[1])]
    nffn = [row(norm_ffn[0]), row(norm_ffn[1])]
    nmem = [row(mem_norm[0]), row(mem_norm[1])]
    fin = row(final_norm)
    lng, lnb = a_ln_g.reshape(1, D_INNER), a_ln_b.reshape(1, D_INNER)
    ws = a_ws[0]
    bs3 = a_bs[0].reshape(A_GROUPS, CHUNK, 1)
    pad_h = lambda a: jnp.pad(a.reshape(-1), (0, HPAD - SSM_HEADS))
    bias_row = pad_h(b_dt_bias).reshape(1, HPAD)
    alog_row = pad_h(b_a_log).reshape(1, HPAD)
    dfull = jnp.repeat(b_d.reshape(-1), SSM_P).reshape(1, D_INNER)

    kvs, mns = [None, None], [None, None]

    def mem_kv(i, after=None):
        gain = nmem[i] if after is None else _tie(nmem[i], after, f"tie_mem{i}")
        mns[i] = _rms_fwd(mems, gain, f"mem_norm{i}")
        kvs[i] = _mm(mns[i], WKV[i], m=N_MEM, n=2 * X_WIDTH, k=D_MODEL, b_at=(0, 0, 0), out_dtype=_ACT, name=f"kv{i}")

    def ffn_fwd(h, i):
        f = _rms_fwd(h, nffn[i], f"ffn_norm{i}")
        p = _mm(f, W1[i], m=s, n=D_FF, k=D_MODEL, b_at=(0, 0, 0), out_dtype=_ACT, name=f"ffn_up{i}")
        hn = _mm(p, W2[i], m=s, n=D_MODEL, k=D_FF, b_at=(0, 0, 0), a_pro="relu2", add=h, name=f"ffn_down{i}")
        return f, p, hn

    def out_proj(h, cat, i):
        return _mm(cat, WO[i], m=s, n=D_MODEL, k=3 * D_MODEL, b_at=(0, 0, 0), add=h, name=f"out_proj{i}")

    proj_a = _mm(a0, WA, m=s, n=5 * D_MODEL, k=D_MODEL, b_at=(0, 0, 0), name="proj_a")
    mem_kv(0, after=[proj_a])
    cat_a = _gmlp_fwd(proj_a, lng, lnb, ws, bs3, "gmlp_fwd")
    cat_a = _attn_fwd(proj_a, 4, kvs[0], cat_a, "attn_fwd0")
    h1 = out_proj(xs, cat_a, 0)
    f0, p0, h2 = ffn_fwd(h1, 0)

    wbt_blk, small_w, _ = lax.optimization_barrier((wbt_blk, small_w, p0))
    wbt_full = wbt_blk.reshape(N_DEV * b_cols, D_MODEL)
    WBT = jnp.concatenate([wbt_full[:dt0], wbt_full[dt0 + SSM_HEADS:]], axis=0)
    WBDT = jnp.pad(wbt_full[dt0:dt0 + SSM_HEADS], ((0, HPAD - SSM_HEADS), (0, 0)))
    cw_sh, cb_sh, gn_sh = 4 * 384, 384, 256
    sw = small_w.reshape(N_DEV, 32 * 128)
    conv_w = jnp.transpose(sw[:, :cw_sh].reshape(N_DEV, CONV_K, 384), (1, 0, 2)).reshape(CONV_K, CONV_DIM)
    conv_b = sw[:, 2048:2048 + cb_sh].reshape(1, CONV_DIM)
    gnorm = sw[:, 3072:3072 + gn_sh].reshape(1, D_INNER)

    a1 = _rms_fwd(h2, nmix[1], "mix_norm1")
    proj_b = _mm(a1, WBT, m=s, n=6 * D_MODEL, k=D_MODEL, tb=True, name="proj_b")
    dt_raw = _mm(a1, WBDT, m=s, n=HPAD, k=D_MODEL, tb=True, name="proj_dt")
    xbc = _conv_fwd(proj_b, conv_w, conv_b, "conv_fwd")
    y_ssd, states = _ssd_fwd(xbc, dt_raw, bias_row, alog_row, dfull, "ssd_fwd")
    cat_b = _gate_fwd(y_ssd, proj_b, gnorm, "gate_fwd")
    mem_kv(1, after=[cat_b])
    cat_b = _attn_fwd(proj_b, 5, kvs[1], cat_b, "attn_fwd1")
    h3 = out_proj(h2, cat_b, 1)
    f1, p1, h4 = ffn_fwd(h3, 1)

    loss_part, dh, dh_act, d_fin = _loss_head(h4, fin, target, "loss_head")

    g_f1, g_f2, g_out, g_kv = [None, None], [None, None], [None, None], [None, None]
    d_nffn, d_nmix, d_nmem = [None, None], [None, None], [None, None]

    def ffn_bwd(dh, dh_act, h_in, f, p, i, after=(), after_last=()):
        dp = _mm(dh_act, W2[i], m=s, n=D_FF, k=D_MODEL, tb=True, b_at=(0, 0, 0), epi_p=p, out_dtype=_ACT, name=f"ffn_down_dx{i}")
        g_f2[i] = _mm(p, dh_act, m=D_FF, n=D_MODEL, k=s, ta=True, a_pro="relu2", out_dtype=_ACT, name=f"ffn_down_dw{i}")
        g_f1[i] = _mm(f, dp, m=D_MODEL, n=D_FF, k=s, ta=True, out_dtype=_ACT, name=f"ffn_up_dw{i}")
        df = _mm(dp, W1[i], m=s, n=D_MODEL, k=D_FF, tb=True, b_at=(0, 0, 0), after=after, name=f"ffn_up_dx{i}")
        gain = _tie(nffn[i], after_last, f"tie_ffn_norm{i}") if after_last else nffn[i]
        dh_in, dh_in_act, d_nffn[i] = _rms_bwd(h_in, gain, df, dh, f"ffn_norm_bwd{i}")
        return dh_in, dh_in_act

    def out_bwd(dh_act, cat, i):
        dcat = _mm(dh_act, WO[i], m=s, n=3 * D_MODEL, k=D_MODEL, tb=True, b_at=(0, 0, 0), out_dtype=_ACT, name=f"out_dx{i}")
        g_out[i] = _mm(cat, dh_act, m=3 * D_MODEL, n=D_MODEL, k=s, ta=True, out_dtype=_ACT, name=f"out_dw{i}")
        return dcat

    def mem_bwd(dkv, i):
        g_kv[i] = _mm(mns[i], dkv, m=D_MODEL, n=2 * X_WIDTH, k=N_MEM, ta=True, out_dtype=_ACT, name=f"kv_dw{i}")
        dmn = _mm(dkv, WKV[i], m=N_MEM, n=D_MODEL, k=2 * X_WIDTH, tb=True, b_at=(0, 0, 0), name=f"kv_dx{i}")
        _, _, d_nmem[i] = _rms_bwd(mems, nmem[i], dmn, None, f"mem_norm_bwd{i}")

    lay_g = {"f1": ("col", 512, (D_MODEL, 512)), "f2": ("row", 512, (512, D_MODEL)), "out": ("row", 384, (384, D_MODEL)),
             "kv": ("col", 256, (D_MODEL, 256)), "a": ("col", 640, (D_MODEL, 640)), "b": ("blk", 0, (b_cols, D_MODEL))}
    reduced = {}

    def reduce_scatter(group, tag, after=(), sums_after=()):
        grads3, lays3 = [], []
        for fam, _, g in group:
            kind, width, shape = lay_g[fam]
            grads3.append(g if kind == "blk" else g.reshape((1,) + g.shape))
            lays3.append((kind, width, shape if kind == "blk" else (1,) + shape))
        recv1 = _rs_to_sibling(grads3, lays3, f"rs_sibling_{tag}", after)
        if sums_after:
            recv1 = lax.optimization_barrier((tuple(recv1), tuple(sums_after)))[0]
        parts = [_rs_chip_sum(g, recv1[t].reshape((4,) + lay_g[fam][2]), lay_g[fam], xyc, f"rs_chip_sum_{fam}{i}")
                 for t, (fam, i, g) in enumerate(group)]
        recv2 = _rs_across_chips(parts, f"rs_chips_{tag}")
        for (fam, i, _), p, r2 in zip(group, parts, recv2):
            reduced[fam, i] = (p, r2)
        return parts, recv2

    dh3, dh3_act = ffn_bwd(dh, dh_act, h3, f1, p1, 1)
    dcat_b = out_bwd(dh3_act, cat_b, 1)
    sums, got_ffn1 = reduce_scatter([("f1", 1, g_f1[1]), ("f2", 1, g_f2[1]), ("out", 1, g_out[1])], "ffn1", sums_after=[dcat_b])
    dy_ssd, dproj_b, d_gnorm = _gate_bwd(y_ssd, proj_b, gnorm, dcat_b, "gate_bwd")
    dproj_b, dkv_b = _attn_bwd(proj_b, 5, kvs[1], dcat_b, dproj_b, "attn_bwd1")
    mem_bwd(dkv_b, 1)
    dxbc, ddt_raw, d_alog, d_dskip, d_dtbias = _ssd_bwd(
        xbc, dt_raw, _tie(bias_row, sums, "tie_ffn1"), alog_row, dfull, dy_ssd, states, "ssd_bwd")
    dproj_b, d_convw, d_convb = _conv_bwd(proj_b, conv_w, _tie(conv_b, got_ffn1, "tie_got_ffn1"), dxbc, dproj_b, "conv_bwd")
    gb = _mm(dproj_b, a1, m=6 * D_MODEL, n=D_MODEL, k=s, ta=True, out_dtype=_ACT, name="proj_b_dw")
    gb_dt = _mm(ddt_raw, a1, m=HPAD, n=D_MODEL, k=s, ta=True, out_dtype=_ACT, name="proj_b_dw_dt")
    gb_full = jnp.concatenate([gb[:dt0], gb_dt[:SSM_HEADS], gb[dt0:]], axis=0)
    gb_blk = gb_full.reshape(N_DEV, b_cols, D_MODEL)
    da1 = _mm(dproj_b, WBT, m=s, n=D_MODEL, k=6 * D_MODEL, name="proj_b_dx")
    sums, got_mix1 = reduce_scatter([("kv", 1, g_kv[1]), ("b", 0, gb_blk)], "mix1", sums_after=[da1])
    da1 = _mm(ddt_raw, WBDT, m=s, n=D_MODEL, k=HPAD, add=da1, name="proj_b_dx_dt")
    dh2, dh2_act, d_nmix[1] = _rms_bwd(h2, _tie(nmix[1], sums, "tie_mix1"), da1, dh3, "mix_norm_bwd1")

    dh1, dh1_act = ffn_bwd(dh2, dh2_act, h1, f0, p0, 0, after=got_ffn1, after_last=got_mix1)
    dcat_a = out_bwd(dh1_act, cat_a, 0)
    sums, got_ffn0 = reduce_scatter([("f1", 0, g_f1[0]), ("f2", 0, g_f2[0]), ("out", 0, g_out[0])], "ffn0", sums_after=[dcat_a])
    dproj_a, d_ws, d_bs3, d_lng, d_lnb = _gmlp_bwd(proj_a, dcat_a, _tie(lng, sums, "tie_ffn0"), lnb, ws, bs3, "gmlp_bwd")
    dproj_a, dkv_a = _attn_bwd(proj_a, 4, kvs[0], dcat_a, dproj_a, "attn_bwd0")
    da0 = _mm(dproj_a, WA, m=s, n=D_MODEL, k=5 * D_MODEL, tb=True, b_at=(0, 0, 0), name="proj_a_dx")
    grad_x, _, d_nmix[0] = _rms_bwd(xs, nmix[0], da0, dh1, "mix_norm_bwd0")
    mem_bwd(dkv_a, 0)

    rep_grads = [jnp.concatenate(d_nmix, axis=0), jnp.concatenate(d_nffn, axis=0), jnp.concatenate(d_nmem, axis=0),
                 d_lng, d_lnb, d_ws.reshape(1, A_GROUPS, CHUNK, CHUNK), d_bs3.reshape(1, A_GROUPS, CHUNK),
                 d_dtbias[:, :SSM_HEADS], d_alog[:, :SSM_HEADS], d_dskip[:, :SSM_HEADS], d_fin.reshape(D_MODEL)]
    rep_w = [norm_mix, norm_ffn, mem_norm, a_ln_g, a_ln_b, a_ws, a_bs, b_dt_bias, b_a_log, b_d, final_norm]
    rep_grads = [g.reshape(w.shape) for g, w in zip(rep_grads, rep_w)]
    sh_grads = [d_convw, d_convb, d_gnorm]
    g_pack, g_layout = _pack(rep_grads + sh_grads + [loss_part])
    n_rep = len(rep_grads)
    (g_all,) = _all_gather_seq([g_pack], [("blk", 0, (N_DEV,) + g_pack.shape)], "ag_small_grads", after=got_ffn0)

    def big_update(w, m, v, fam, nlayer):
        res = None
        for i in range(nlayer):
            part, recv2 = reduced[fam, i]
            plist = [(part, 0), (recv2, 0), (recv2, 1), (recv2, 2)]
            res = _adamw(w, m, v, plist, f"adamw_{fam}{i}", layer=i, prev=res)
        return res

    ga = _mm(a0, dproj_a, m=D_MODEL, n=5 * D_MODEL, k=s, ta=True, out_dtype=_ACT, after=[g_pack], name="proj_a_dw")
    r_b = big_update(tr_b(b_in), tr_b(m_b_in), tr_b(v_b_in), "b", 1)
    reduce_scatter([("kv", 0, g_kv[0]), ("a", 0, ga)], "mix0", after=[g_all], sums_after=r_b)
    r_b = [tr_b(o) for o in r_b]

    r_f1 = big_update(w_ffn1, m_w_ffn1, v_w_ffn1, "f1", 2)
    r_f2 = big_update(w_ffn2, m_w_ffn2, v_w_ffn2, "f2", 2)
    r_out = big_update(w_out, m_w_out, v_w_out, "out", 2)
    r_kv = big_update(w_kv, m_w_kv, v_w_kv, "kv", 2)
    r_a = big_update(a_in, m_a_in, v_a_in, "a", 1)

    rep_names = ["norm_mix", "norm_ffn", "mem_norm", "a_ln_g", "a_ln_b", "a_ws", "a_bs", "b_dt_bias", "b_a_log", "b_d",
                 "final_norm"]
    rep_m = [m_norm_mix, m_norm_ffn, m_mem_norm, m_a_ln_g, m_a_ln_b, m_a_ws, m_a_bs, m_b_dt_bias, m_b_a_log, m_b_d, m_final_norm]
    rep_v = [v_norm_mix, v_norm_ffn, v_mem_norm, v_a_ln_g, v_a_ln_b, v_a_ws, v_a_bs, v_b_dt_bias, v_b_a_log, v_b_d, v_final_norm]
    g_small = _sum8(g_all, "sum_small_grads")
    g_list = _unpack(g_small, g_layout)
    loss = g_list[-1][0, 0]
    wp, w_layout = _pack(rep_w)
    mp, _ = _pack(rep_m)
    vp, _ = _pack(rep_v)
    gp = g_small[:g_layout[n_rep][0]]
    rep_res = [_unpack(o, w_layout) for o in _adamw(wp, mp, vp, [(gp, None)], "adamw_replicated", tr=88)]

    gcw = lax.dynamic_slice_in_dim(g_list[n_rep], me * 384, 384, axis=1).reshape(1, CONV_K, 384)
    gcb = lax.dynamic_slice_in_dim(g_list[n_rep + 1], me * 384, 384, axis=1)
    ggn = lax.dynamic_slice_in_dim(g_list[n_rep + 2], me * 256, 256, axis=1)
    sh_w = [b_conv_w, b_conv_b, b_gnorm]
    sh_m = [m_b_conv_w, m_b_conv_b, m_b_gnorm]
    sh_v = [v_b_conv_w, v_b_conv_b, v_b_gnorm]
    swp, sw_layout = _pack(sh_w)
    smp, _ = _pack(sh_m)
    svp, _ = _pack(sh_v)
    sgp, _ = _pack([gcw, gcb, ggn])
    sh_res = [_unpack(o, sw_layout) for o in _adamw(swp, smp, svp, [(sgp, None)], "adamw_sharded_small", tr=8)]

    names = ["norm_mix", "norm_ffn", "mem_norm", "w_kv", "w_out", "w_ffn1", "w_ffn2", "a_in", "a_ln_g", "a_ln_b", "a_ws",
             "a_bs", "b_in", "b_conv_w", "b_conv_b", "b_dt_bias", "b_a_log", "b_d", "b_gnorm", "final_norm"]
    big = {"w_kv": r_kv, "w_out": r_out, "w_ffn1": r_f1, "w_ffn2": r_f2, "a_in": r_a, "b_in": r_b}
    sh_names = ["b_conv_w", "b_conv_b", "b_gnorm"]
    outs = [loss, grad_x.reshape(x.shape)]
    for kind in range(4):
        for nm in names:
            if nm in big:
                outs.append(big[nm][kind])
            elif nm in sh_names:
                outs.append(sh_res[kind][sh_names.index(nm)])
            else:
                outs.append(rep_res[kind][rep_names.index(nm)])
    return tuple(outs)
```

```python
import functools
import math

import jax
import jax.numpy as jnp
from jax import lax
from jax.experimental import pallas as pl
from jax.experimental.pallas import tpu as pltpu
from jax.experimental.pallas import tpu_sc as plsc

F32 = jnp.float32
_MXU = jnp.bfloat16
_ACT = jnp.bfloat16
_HI = lax.Precision.HIGHEST

D_MODEL = 1024
CHUNK = 128
N_MEM = 256
D_INNER = 2048
A_GROUPS = 8
A_GW = D_INNER // A_GROUPS
SSM_HEADS = 32
SSM_P = 64
SSM_GROUPS = 4
SSM_GW = D_INNER // SSM_GROUPS
SSM_N = 128
CONV_K = 4
CONV_DIM = 3072
X_HEADS = 4
X_HD = 256
X_WIDTH = 1024
D_FF = 4096
EPS = 1e-6
HPAD = 128
N_DEV = 8

ADAM_LR = 0.001
ADAM_B1 = 0.9
ADAM_B2 = 0.999
ADAM_EPS = 1e-08
ADAM_WD = 0.01
ADAM_STEP = 10

VMEM_BIG = 56 * 1024 * 1024
MESH = pl.DeviceIdType.MESH


def _cp(vmem=None):
    if vmem is None:
        return pltpu.CompilerParams()
    return pltpu.CompilerParams(vmem_limit_bytes=vmem)


def _dot(a, b, dims=((1,), (0,))):
    return lax.dot_general(a.astype(_MXU), b.astype(_MXU), (dims, ((), ())), preferred_element_type=F32)


def _dot_nt(a, b):
    return _dot(a, b, ((1,), (1,)))


def _dot_tn(a, b):
    return _dot(a, b, ((0,), (0,)))


def _dot_hi(a, b, dims=((1,), (0,))):
    return lax.dot_general(a.astype(F32), b.astype(F32), (dims, ((), ())), precision=_HI, preferred_element_type=F32)


def _split3(x):
    x1 = x.astype(jnp.bfloat16)
    r = x - x1.astype(F32)
    x2 = r.astype(jnp.bfloat16)
    x3 = (r - x2.astype(F32)).astype(jnp.bfloat16)
    return x1, x2, x3


def _dot_sel(x, sel, dims=((1,), (0,)), terms=2):
    sel = sel.astype(jnp.bfloat16)
    parts = [lax.dot_general(t, sel, (dims, ((), ())), preferred_element_type=F32) for t in _split3(x)[:terms]]
    return functools.reduce(lambda a, b: a + b, parts)


def _sel_dot(sel, x, dims=((1,), (0,))):
    sel = sel.astype(jnp.bfloat16)
    parts = [lax.dot_general(sel, t, (dims, ((), ())), preferred_element_type=F32) for t in _split3(x)]
    return (parts[0] + parts[1]) + parts[2]


def _sigmoid(x):
    return 1.0 / (1.0 + jnp.exp(-x))


def _gelu(x):
    return 0.5 * x * (1.0 + lax.erf(x * (1.0 / math.sqrt(2.0))))


def _gelu_grad(x):
    return 0.5 * (1.0 + lax.erf(x * (1.0 / math.sqrt(2.0)))) + x * jnp.exp(-0.5 * x * x) * (1.0 / math.sqrt(2.0 * math.pi))


def _softplus(x):
    return jnp.maximum(x, 0.0) + jnp.log1p(jnp.exp(-jnp.abs(x)))


def _iota(shape, dim):
    return lax.broadcasted_iota(jnp.int32, shape, dim)


MM_VMEM_BUDGET = 40 * 1024 * 1024
HBM_BYTES_PER_S = 2.5e12
GRID_STEP_S = 0.35e-6
VMEM_ACC_BYTES_PER_S = 6e12


def _divisors(dim, unit):
    out = [d for d in range(unit, min(dim, 2048) + 1, unit) if dim % d == 0]
    return out if out else [dim]


def _mm_tiles(m, n, k, sa, sb, s_mn, a_pro, offsets):
    best = None
    (a_r0, a_c0, ta), (b_r0, b_c0, tb), (o_r0, o_c0) = offsets
    for tm in _divisors(m, 128):
        for tn in _divisors(n, 128):
            for tk in [k // d for d in (1, 2, 3, 4, 6, 8) if k % d == 0 and (k // d) % 128 == 0]:
                a_t = (tk, tm) if ta else (tm, tk)
                b_t = (tn, tk) if tb else (tk, tn)
                if a_r0 % a_t[0] or a_c0 % a_t[1] or b_r0 % b_t[0] or b_c0 % b_t[1] or o_r0 % tm or o_c0 % tn:
                    continue
                nk = k // tk
                vmem = 2 * (tm * tk * sa + tk * tn * sb + tm * tn * s_mn) + tm * tn * 4 * (2 if nk > 1 else 1)
                if a_pro or sa == 4:
                    vmem += tm * tk * 6
                if sb == 4:
                    vmem += tk * tn * 2
                if vmem > MM_VMEM_BUDGET:
                    continue
                gi, gj = m // tm, n // tn
                for j_inner in (True, False):
                    if nk > 1:
                        traffic = gj * m * k * sa + gi * k * n * sb
                    elif j_inner:
                        traffic = m * k * sa + gi * k * n * sb
                    else:
                        traffic = gj * m * k * sa + k * n * sb
                    traffic += m * n * s_mn + (tm * tk * sa + tk * tn * sb)
                    cost = traffic / HBM_BYTES_PER_S + gi * gj * nk * GRID_STEP_S
                    if nk > 1:
                        cost += m * n * 8 * nk / VMEM_ACC_BYTES_PER_S
                    if best is None or cost < best[0]:
                        best = (cost, tm, tn, tk, j_inner)
    assert best is not None, (m, n, k)
    return best[1:]


def _mm(a, b, *, m, n, k, name, ta=False, tb=False, a_at=(None, 0, 0), b_at=(None, 0, 0),
        out_dtype=F32, add=None, epi_p=None, epi_at=(None, 0, 0), out=None, out_at=(None, 0, 0),
        out_full=None, a_pro=None, after=()):
    s_mn =jnp.dtype(out.dtype if out is not None else out_dtype).itemsize
    s_mn += add.dtype.itemsize if add is not None else 0
    s_mn += epi_p.dtype.itemsize if epi_p is not None else 0
    tm, tn, tk, j_inner = _mm_tiles(m, n, k, a.dtype.itemsize, b.dtype.itemsize, s_mn, a_pro is not None,
                                    ((a_at[1], a_at[2], ta), (b_at[1], b_at[2], tb), (out_at[1], out_at[2])))
    nk = k // tk

    def spec(at, tr, tc, rsel, csel):
        lead, r0, c0 = at
        assert r0 % tr == 0 and c0 % tc == 0, (name, at, tr, tc)
        rb, cb = r0 // tr, c0 // tc
        if lead is None:
            return pl.BlockSpec((tr, tc), lambda g0, g1, kk: (rb + rsel(g0, g1, kk), cb + csel(g0, g1, kk)))
        return pl.BlockSpec((None, tr, tc), lambda g0, g1, kk: (lead, rb + rsel(g0, g1, kk), cb + csel(g0, g1, kk)))

    gi = (lambda g0, g1, kk: g0) if j_inner else (lambda g0, g1, kk: g1)
    gj = (lambda g0, g1, kk: g1) if j_inner else (lambda g0, g1, kk: g0)
    gk = lambda g0, g1, kk: kk
    a_spec = spec(a_at, tk, tm, gk, gi) if ta else spec(a_at, tm, tk, gi, gk)
    b_spec = spec(b_at, tn, tk, gj, gk) if tb else spec(b_at, tk, tn, gk, gj)
    dims = ((0,), (0,)) if ta else (((1,), (1,)) if tb else ((1,), (0,)))
    assert not (ta and tb)

    operands, in_specs = [a, b], [a_spec, b_spec]
    if add is not None:
        operands.append(add)
        in_specs.append(spec((None, 0, 0), tm, tn, gi, gj))
    if epi_p is not None:
        operands.append(epi_p)
        in_specs.append(spec(epi_at, tm, tn, gi, gj))
    aliases = {}
    if out is not None:
        aliases = {len(operands): 0}
        operands.append(out)
        in_specs.append(pl.BlockSpec(memory_space=pl.ANY))
        out_struct = jax.ShapeDtypeStruct(out.shape, out.dtype)
        out_dtype = out.dtype
    else:
        out_struct = jax.ShapeDtypeStruct(out_full if out_full is not None else (m, n), out_dtype)
    has_add, has_epi = add is not None, epi_p is not None
    n_skip = (1 if out is not None else 0) + len(after)
    operands += list(after)
    in_specs += [pl.BlockSpec(memory_space=pl.ANY)] * len(after)

    def body(*refs):
        a_ref, b_ref = refs[0], refs[1]
        pos = 2
        add_ref = epi_ref = None
        if has_add:
            add_ref = refs[pos]
            pos += 1
        if has_epi:
            epi_ref = refs[pos]
            pos += 1
        pos += n_skip
        o_ref = refs[pos]

        def finish(r):
            if has_add:
                r = r + add_ref[...].astype(F32)
            if has_epi:
                r = r * (2.0 * jnp.maximum(epi_ref[...].astype(F32), 0.0))
            o_ref[...] = r.astype(o_ref.dtype)

        av = a_ref[...]
        if a_pro == "relu2":
            av = jnp.square(jnp.maximum(av.astype(F32), 0.0))
        part = _dot(av, b_ref[...], dims)
        if nk == 1:
            finish(part)
        else:
            acc_ref = refs[pos + 1]
            kk = pl.program_id(2)

            @pl.when(kk == 0)
            def _():
                acc_ref[...] = part

            @pl.when(kk > 0)
            def _():
                acc_ref[...] += part

            @pl.when(kk == nk - 1)
            def _():
                finish(acc_ref[...])

    grid = (m // tm, n // tn, nk) if j_inner else (n // tn, m // tm, nk)
    return pl.pallas_call(
        body, name=name, grid=grid, in_specs=in_specs,
        out_specs=spec(out_at, tm, tn, gi, gj), out_shape=out_struct,
        scratch_shapes=[pltpu.VMEM((tm, tn), F32)] if nk > 1 else [], input_output_aliases=aliases,
        compiler_params=_cp(VMEM_BIG))(*operands)


def _rms_fwd(x, g, name, tm=1024):
    s, d = x.shape
    tm = min(tm, s)

    def body(x_ref, g_ref, o_ref):
        xv = x_ref[...]
        r = lax.rsqrt(jnp.mean(xv * xv, axis=-1, keepdims=True) + EPS)
        o_ref[...] = (xv * r * g_ref[...]).astype(o_ref.dtype)

    return pl.pallas_call(
        body, name=name, grid=(s // tm,),
        in_specs=[pl.BlockSpec((tm, d), lambda i: (i, 0)), pl.BlockSpec((1, d), lambda i: (0, 0))],
        out_specs=pl.BlockSpec((tm, d), lambda i: (i, 0)),
        out_shape=jax.ShapeDtypeStruct((s, d), _ACT), compiler_params=_cp(VMEM_BIG))(x, g)


def _rms_bwd(x, g, dy, dres, name, tm=512):
    s, d = x.shape
    tm = min(tm, s)
    has_res = dres is not None

    def body(*refs):
        if has_res:
            x_ref, g_ref, dy_ref, dres_ref, dx_ref, dxa_ref, dg_ref = refs
        else:
            x_ref, g_ref, dy_ref, dx_ref, dxa_ref, dg_ref = refs

        @pl.when(pl.program_id(0) == 0)
        def _():
            dg_ref[...] = jnp.zeros_like(dg_ref)

        xv = x_ref[...]
        dyv = dy_ref[...].astype(F32)
        r = lax.rsqrt(jnp.mean(xv * xv, axis=-1, keepdims=True) + EPS)
        xh = xv * r
        dyg = dyv * g_ref[...]
        dx = r * (dyg - xh * jnp.mean(dyg * xh, axis=-1, keepdims=True))
        if has_res:
            dx = dx + dres_ref[...]
        dx_ref[...] = dx
        dxa_ref[...] = dx.astype(dxa_ref.dtype)
        dg_ref[...] += jnp.sum(dyv * xh, axis=0, keepdims=True)

    row = pl.BlockSpec((tm, d), lambda i: (i, 0))
    vec = pl.BlockSpec((1, d), lambda i: (0, 0))
    in_specs = [row, vec, row] + ([row] if has_res else [])
    operands = [x, g, dy] + ([dres] if has_res else [])
    return pl.pallas_call(
        body, name=name, grid=(s // tm,), in_specs=in_specs, out_specs=[row, row, vec],
        out_shape=[jax.ShapeDtypeStruct((s, d), F32), jax.ShapeDtypeStruct((s, d), _ACT),
                   jax.ShapeDtypeStruct((1, d), F32)], compiler_params=_cp(VMEM_BIG))(*operands)


def _loss_head(h, g, target, name, tm=512):
    s, d = h.shape
    tm = min(tm, s)

    def body(h_ref, g_ref, t_ref, loss_ref, dh_ref, dha_ref, dg_ref):
        @pl.when(pl.program_id(0) == 0)
        def _():
            dg_ref[...] = jnp.zeros_like(dg_ref)
            loss_ref[...] = jnp.zeros_like(loss_ref)

        xv = h_ref[...]
        r = lax.rsqrt(jnp.mean(xv * xv, axis=-1, keepdims=True) + EPS)
        xh = xv * r
        err = xh * g_ref[...] - t_ref[...]
        loss_ref[...] += jnp.full(loss_ref.shape, 0.5 * jnp.sum(jnp.mean(err * err, axis=-1, keepdims=True)), F32)
        dyv = err * (1.0 / d)
        dyg = dyv * g_ref[...]
        dh = r * (dyg - xh * jnp.mean(dyg * xh, axis=-1, keepdims=True))
        dh_ref[...] = dh
        dha_ref[...] = dh.astype(dha_ref.dtype)
        dg_ref[...] += jnp.sum(dyv * xh, axis=0, keepdims=True)

    row = pl.BlockSpec((tm, d), lambda i: (i, 0))
    vec = pl.BlockSpec((1, d), lambda i: (0, 0))
    return pl.pallas_call(
        body, name=name, grid=(s // tm,), in_specs=[row, vec, row],
        out_specs=[pl.BlockSpec((1, 128), lambda i: (0, 0)), row, row, vec],
        out_shape=[jax.ShapeDtypeStruct((1, 128), F32), jax.ShapeDtypeStruct((s, d), F32),
                   jax.ShapeDtypeStruct((s, d), _ACT), jax.ShapeDtypeStruct((1, d), F32)],
        compiler_params=_cp(VMEM_BIG))(h, g, target)


def _gmlp_parts(pu, pv, lng, lnb):
    u = _gelu(pu)
    v = _gelu(pv)
    mu = jnp.mean(v, axis=-1, keepdims=True)
    vc = v - mu
    rstd = lax.rsqrt(jnp.mean(vc * vc, axis=-1, keepdims=True) + EPS)
    xhat = vc * rstd
    vn = xhat * lng + lnb
    return u, xhat, rstd, vn


def _gmlp_fwd(proj, lng, lnb, ws, bs3, name):
    s = proj.shape[0]

    def body(pu_ref, pv_ref, lng_ref, lnb_ref, ws_ref, bs_ref, o_ref):
        u, _, _, vn = _gmlp_parts(pu_ref[...], pv_ref[...], lng_ref[...], lnb_ref[...])
        causal = _iota((CHUNK, CHUNK), 0) >= _iota((CHUNK, CHUNK), 1)
        for g in range(A_GROUPS):
            sl = slice(g * A_GW, (g + 1) * A_GW)
            w = jnp.where(causal, ws_ref[g], 0.0)
            sv = _dot(w, vn[:, sl]) + bs_ref[g]
            o_ref[:, sl] = (u[:, sl] * sv).astype(o_ref.dtype)

    full = lambda shape: pl.BlockSpec(shape, lambda c: (0,) * len(shape))
    return pl.pallas_call(
        body, name=name, grid=(s // CHUNK,),
        in_specs=[pl.BlockSpec((CHUNK, D_INNER), lambda c: (c, 0)), pl.BlockSpec((CHUNK, D_INNER), lambda c: (c, 1)),
                  full((1, D_INNER)), full((1, D_INNER)), full((A_GROUPS, CHUNK, CHUNK)), full((A_GROUPS, CHUNK, 1))],
        out_specs=pl.BlockSpec((CHUNK, D_INNER), lambda c: (c, 0)),
        out_shape=jax.ShapeDtypeStruct((s, D_INNER + X_WIDTH), _ACT), compiler_params=_cp(VMEM_BIG))(proj, proj, lng, lnb, ws, bs3)


def _gmlp_bwd(proj, dcat, lng, lnb, ws, bs3, name):
    s = proj.shape[0]

    def body(pu_ref, pv_ref, dm_ref, lng_ref, lnb_ref, ws_ref, bs_ref, dp_ref, dws_ref, dbs_ref, dlng_ref, dlnb_ref, dvn_ref):
        @pl.when(pl.program_id(0) == 0)
        def _():
            dws_ref[...] = jnp.zeros_like(dws_ref)
            dbs_ref[...] = jnp.zeros_like(dbs_ref)
            dlng_ref[...] = jnp.zeros_like(dlng_ref)
            dlnb_ref[...] = jnp.zeros_like(dlnb_ref)

        pu, pv = pu_ref[...], pv_ref[...]
        lng = lng_ref[...]
        u, xhat, rstd, vn = _gmlp_parts(pu, pv, lng, lnb_ref[...])
        dm = dm_ref[...].astype(F32)
        causal = _iota((CHUNK, CHUNK), 0) >= _iota((CHUNK, CHUNK), 1)
        for g in range(A_GROUPS):
            sl = slice(g * A_GW, (g + 1) * A_GW)
            w = jnp.where(causal, ws_ref[g], 0.0)
            sv = _dot(w, vn[:, sl]) + bs_ref[g]
            dsv = dm[:, sl] * u[:, sl]
            dp_ref[:, sl] = (dm[:, sl] * sv * _gelu_grad(pu[:, sl])).astype(dp_ref.dtype)
            dvn_ref[:, sl] = _dot_tn(w, dsv)
            dws_ref[g] += jnp.where(causal, _dot_nt(dsv, vn[:, sl]), 0.0)
            dbs_ref[g] += jnp.sum(dsv, axis=-1, keepdims=True)
        dvn = dvn_ref[...]
        dlng_ref[...] += jnp.sum(dvn * xhat, axis=0, keepdims=True)
        dlnb_ref[...] += jnp.sum(dvn, axis=0, keepdims=True)
        dxh = dvn * lng
        dv = rstd * (dxh - jnp.mean(dxh, axis=-1, keepdims=True) - xhat * jnp.mean(dxh * xhat, axis=-1, keepdims=True))
        dp_ref[:, D_INNER:] = (dv * _gelu_grad(pv)).astype(dp_ref.dtype)

    full = lambda shape: pl.BlockSpec(shape, lambda c: (0,) * len(shape))
    return pl.pallas_call(
        body, name=name, grid=(s // CHUNK,),
        in_specs=[pl.BlockSpec((CHUNK, D_INNER), lambda c: (c, 0)), pl.BlockSpec((CHUNK, D_INNER), lambda c: (c, 1)),
                  pl.BlockSpec((CHUNK, D_INNER), lambda c: (c, 0)),
                  full((1, D_INNER)), full((1, D_INNER)), full((A_GROUPS, CHUNK, CHUNK)), full((A_GROUPS, CHUNK, 1))],
        out_specs=[pl.BlockSpec((CHUNK, 2 * D_INNER), lambda c: (c, 0)), full((A_GROUPS, CHUNK, CHUNK)),
                   full((A_GROUPS, CHUNK, 1)), full((1, D_INNER)), full((1, D_INNER))],
        out_shape=[jax.ShapeDtypeStruct((s, 2 * D_INNER + X_WIDTH), _ACT), jax.ShapeDtypeStruct((A_GROUPS, CHUNK, CHUNK), F32),
                   jax.ShapeDtypeStruct((A_GROUPS, CHUNK, 1), F32), jax.ShapeDtypeStruct((1, D_INNER), F32),
                   jax.ShapeDtypeStruct((1, D_INNER), F32)],
        scratch_shapes=[pltpu.VMEM((CHUNK, D_INNER), F32)],
        compiler_params=_cp(VMEM_BIG))(proj, proj, dcat, lng, lnb, ws, bs3)


_X_SCALE = 1.0 / math.sqrt(X_HD)


def _attn_fwd(proj, qblk, kv, cat, name, tm=512):
    s = proj.shape[0]
    tm = min(tm, s)

    def body(q_ref, kv_ref, cat_ref, o_ref):
        for h in range(X_HEADS):
            sl = slice(h * X_HD, (h + 1) * X_HD)
            k = kv_ref[:, sl]
            v = kv_ref[:, X_WIDTH + h * X_HD:X_WIDTH + (h + 1) * X_HD]
            sc = _dot_nt(q_ref[:, sl], k) * _X_SCALE
            e = jnp.exp(sc - jnp.max(sc, axis=-1, keepdims=True))
            p = e / jnp.sum(e, axis=-1, keepdims=True)
            o_ref[:, sl] = _dot(p, v).astype(o_ref.dtype)

    return pl.pallas_call(
        body, name=name, grid=(s // tm,),
        in_specs=[pl.BlockSpec((tm, X_WIDTH), lambda i: (i, qblk)), pl.BlockSpec((N_MEM, 2 * X_WIDTH), lambda i: (0, 0)),
                  pl.BlockSpec(memory_space=pl.ANY)],
        out_specs=pl.BlockSpec((tm, X_WIDTH), lambda i: (i, D_INNER // X_WIDTH)),
        out_shape=jax.ShapeDtypeStruct(cat.shape, cat.dtype), input_output_aliases={2: 0},
        compiler_params=_cp(VMEM_BIG))(proj, kv, cat)


def _attn_bwd(proj, qblk, kv, dcat, dproj, name, tm=512):
    s = proj.shape[0]
    tm = min(tm, s)

    def body(q_ref, kv_ref, do_ref, dproj_ref, dq_ref, dkv_ref):
        @pl.when(pl.program_id(0) == 0)
        def _():
            dkv_ref[...] = jnp.zeros_like(dkv_ref)

        for h in range(X_HEADS):
            sl = slice(h * X_HD, (h + 1) * X_HD)
            slv = slice(X_WIDTH + h * X_HD, X_WIDTH + (h + 1) * X_HD)
            q = q_ref[:, sl]
            k = kv_ref[:, sl]
            v = kv_ref[:, slv]
            do = do_ref[:, sl].astype(F32)
            sc = _dot_nt(q, k) * _X_SCALE
            e = jnp.exp(sc - jnp.max(sc, axis=-1, keepdims=True))
            p = e / jnp.sum(e, axis=-1, keepdims=True)
            dp = _dot_nt(do, v)
            ds = p * (dp - jnp.sum(dp * p, axis=-1, keepdims=True)) * _X_SCALE
            dq_ref[:, sl] = _dot(ds, k).astype(dq_ref.dtype)
            dkv_ref[:, sl] += _dot_tn(ds, q)
            dkv_ref[:, slv] += _dot_tn(p, do)

    return pl.pallas_call(
        body, name=name, grid=(s // tm,),
        in_specs=[pl.BlockSpec((tm, X_WIDTH), lambda i: (i, qblk)), pl.BlockSpec((N_MEM, 2 * X_WIDTH), lambda i: (0, 0)),
                  pl.BlockSpec((tm, X_WIDTH), lambda i: (i, 2)), pl.BlockSpec(memory_space=pl.ANY)],
        out_specs=[pl.BlockSpec((tm, X_WIDTH), lambda i: (i, qblk)), pl.BlockSpec((N_MEM, 2 * X_WIDTH), lambda i: (0, 0))],
        out_shape=[jax.ShapeDtypeStruct(dproj.shape, dproj.dtype), jax.ShapeDtypeStruct((N_MEM, 2 * X_WIDTH), F32)],
        input_output_aliases={3: 0}, compiler_params=_cp(VMEM_BIG))(proj, kv, dcat, dproj)


CONV_TC = 256
_XBC_BLK0 = D_INNER // CONV_TC


CONV_RB = 64
SUBLANES = 8


def _rows_before(cur, prev_last, j):
    rolled = pltpu.roll(cur, j, 0)
    head = jnp.where(_iota((SUBLANES, cur.shape[1]), 0) < j, pltpu.roll(prev_last, j, 0), rolled[:SUBLANES])
    return jnp.concatenate([head, rolled[SUBLANES:]], axis=0)


def _rows_after(cur, next_first, j):
    n = cur.shape[0]
    rolled = pltpu.roll(cur, n - j, 0)
    tail = jnp.where(_iota((SUBLANES, cur.shape[1]), 0) >= SUBLANES - j, pltpu.roll(next_first, SUBLANES - j, 0),
                     rolled[n - SUBLANES:])
    return jnp.concatenate([rolled[:n - SUBLANES], tail], axis=0)


def _conv_pre(x_ref, w_ref, b_ref, r0, prev_last):
    cur = x_ref[pl.ds(r0, CONV_RB), :]
    shifts = [_rows_before(cur, prev_last, j) for j in range(1, CONV_K)]
    pre = b_ref[...] + w_ref[CONV_K - 1:CONV_K, :] * cur
    for j in range(1, CONV_K):
        pre = pre + w_ref[CONV_K - 1 - j:CONV_K - j, :] * shifts[j - 1]
    return pre, cur, shifts


def _conv_fwd(proj, w, b, name):
    s = proj.shape[0]

    def body(x_ref, w_ref, b_ref, o_ref):
        xv = x_ref[...]
        rows = _iota(xv.shape, 0)
        pre = b_ref[...] + w_ref[CONV_K - 1:CONV_K, :] * xv
        for j in range(1, CONV_K):
            pre = pre + w_ref[CONV_K - 1 - j:CONV_K - j, :] * jnp.where(rows >= j, pltpu.roll(xv, j, 0), 0.0)
        o_ref[...] = pre * _sigmoid(pre)

    return pl.pallas_call(
        body, name=name, grid=(CONV_DIM // CONV_TC,),
        in_specs=[pl.BlockSpec((s, CONV_TC), lambda j: (0, _XBC_BLK0 + j)), pl.BlockSpec((CONV_K, CONV_TC), lambda j: (0, j)),
                  pl.BlockSpec((1, CONV_TC), lambda j: (0, j))],
        out_specs=pl.BlockSpec((s, CONV_TC), lambda j: (0, j)),
        out_shape=jax.ShapeDtypeStruct((s, CONV_DIM), F32), compiler_params=_cp(VMEM_BIG))(proj, w, b)


def _conv_bwd(proj, w, b, dxbc, dproj, name):
    s = proj.shape[0]

    nb = s // CONV_RB

    def body(x_ref, w_ref, b_ref, d_ref, dproj_ref, dx_ref, dw_ref, db_ref, dpre_ref):
        def fold(v):
            out = v[:SUBLANES]
            for t in range(1, CONV_RB // SUBLANES):
                out = out + v[t * SUBLANES:(t + 1) * SUBLANES]
            return out

        def first(i, carry):
            prev_last, acc = carry
            r0 = pl.multiple_of(i * CONV_RB, CONV_RB)
            pre, cur, shifts = _conv_pre(x_ref, w_ref, b_ref, r0, prev_last)
            sig = _sigmoid(pre)
            dpre = d_ref[pl.ds(r0, CONV_RB), :] * (sig * (1.0 + pre * (1.0 - sig)))
            dpre_ref[pl.ds(r0, CONV_RB), :] = dpre
            taps = [cur] + shifts
            acc = tuple(a + fold(dpre * t) for a, t in zip(acc[:CONV_K], taps)) + (acc[CONV_K] + fold(dpre),)
            return cur[CONV_RB - SUBLANES:], acc

        zero8 = jnp.zeros((SUBLANES, CONV_TC), F32)
        _, acc = lax.fori_loop(0, nb, first, (zero8, (zero8,) * (CONV_K + 1)))
        for j in range(CONV_K):
            dw_ref[CONV_K - 1 - j:CONV_K - j, :] = jnp.sum(acc[j], axis=0, keepdims=True)
        db_ref[...] = jnp.sum(acc[CONV_K], axis=0, keepdims=True)

        def second(i, next_first):
            r0 = pl.multiple_of((nb - 1 - i) * CONV_RB, CONV_RB)
            cur = dpre_ref[pl.ds(r0, CONV_RB), :]
            dx = w_ref[CONV_K - 1:CONV_K, :] * cur
            for j in range(1, CONV_K):
                dx = dx + w_ref[CONV_K - 1 - j:CONV_K - j, :] * _rows_after(cur, next_first, j)
            dx_ref[pl.ds(r0, CONV_RB), :] = dx.astype(dx_ref.dtype)
            return cur[:SUBLANES]

        lax.fori_loop(0, nb, second, zero8)

    return pl.pallas_call(
        body, name=name, grid=(CONV_DIM // CONV_TC,),
        in_specs=[pl.BlockSpec((s, CONV_TC), lambda j: (0, _XBC_BLK0 + j)), pl.BlockSpec((CONV_K, CONV_TC), lambda j: (0, j)),
                  pl.BlockSpec((1, CONV_TC), lambda j: (0, j)), pl.BlockSpec((s, CONV_TC), lambda j: (0, j)),
                  pl.BlockSpec(memory_space=pl.ANY)],
        out_specs=[pl.BlockSpec((s, CONV_TC), lambda j: (0, _XBC_BLK0 + j)), pl.BlockSpec((CONV_K, CONV_TC), lambda j: (0, j)),
                   pl.BlockSpec((1, CONV_TC), lambda j: (0, j))],
        out_shape=[jax.ShapeDtypeStruct(dproj.shape, dproj.dtype), jax.ShapeDtypeStruct((CONV_K, CONV_DIM), F32),
                   jax.ShapeDtypeStruct((1, CONV_DIM), F32)], input_output_aliases={4: 0},
        scratch_shapes=[pltpu.VMEM((s, CONV_TC), F32)],
        compiler_params=_cp(VMEM_BIG))(proj, w, b, dxbc, dproj)


def _ssd_common(dtc_ref, br_ref, ar_ref, csb_ref, cst_ref, csf_ref):
    a_row = -jnp.exp(ar_ref[...])
    dt_c = _softplus(dtc_ref[...] + br_ref[...])
    tril = _iota((CHUNK, CHUNK), 0) >= _iota((CHUNK, CHUNK), 1)
    cs = _sel_dot(tril, dt_c * a_row)
    cst_ref[...] = cs.T
    e64 = (jnp.right_shift(_iota((HPAD, D_INNER), 1), 6) == _iota((HPAD, D_INNER), 0)).astype(jnp.bfloat16)
    e128 = jnp.right_shift(_iota((HPAD, SSM_HEADS * CHUNK), 1), 7) == _iota((HPAD, SSM_HEADS * CHUNK), 0)
    csb_ref[...] = _dot_sel(cs, e128)
    dt_full = _dot_sel(dt_c, e64)
    csf_ref[...] = _dot_sel(cs, e64)
    cs_full = csf_ref[...]
    cs_last = csf_ref[CHUNK - 1:CHUNK, :]
    e_full = jnp.exp(cs_full)
    f_full = jnp.exp(cs_last - cs_full)
    gamma = jnp.exp(cs_last)
    return a_row, dt_c, cs, dt_full, e_full, f_full, gamma, e64


def _ssd_lambda(csb_ref, cst_ref, h, causal):
    diff = csb_ref[:, h * CHUNK:(h + 1) * CHUNK] - cst_ref[h:h + 1, :]
    return jnp.exp(jnp.where(causal, diff, -1e30))


_SSD_VEC_SPECS = lambda: [pl.BlockSpec((1, HPAD), lambda c: (0, 0)), pl.BlockSpec((1, HPAD), lambda c: (0, 0)),
                          pl.BlockSpec((1, D_INNER), lambda c: (0, 0))]


def _ssd_fwd(xbc, dtc, bias_row, alog_row, dfull, name):
    s = xbc.shape[0]
    nc = s // CHUNK

    def body(xbc_ref, dtc_ref, br_ref, ar_ref, df_ref, y_ref, st_ref, ht_ref, csb_ref, cst_ref, csf_ref):
        @pl.when(pl.program_id(0) == 0)
        def _():
            ht_ref[...] = jnp.zeros_like(ht_ref)

        _, _, _, dt_full, e_full, f_full, gamma, _ = _ssd_common(dtc_ref, br_ref, ar_ref, csb_ref, cst_ref, csf_ref)
        x = xbc_ref[:, :D_INNER]
        xdt = x * dt_full
        st_ref[...] = ht_ref[...]
        causal = _iota((CHUNK, CHUNK), 0) >= _iota((CHUNK, CHUNK), 1)
        lo = _iota((CHUNK, CHUNK), 1) < SSM_P
        for g in range(SSM_GROUPS):
            gs = slice(g * SSM_GW, (g + 1) * SSM_GW)
            bg = xbc_ref[:, D_INNER + g * SSM_N:D_INNER + (g + 1) * SSM_N]
            cg = xbc_ref[:, D_INNER + SSM_GROUPS * SSM_N + g * SSM_N:D_INNER + SSM_GROUPS * SSM_N + (g + 1) * SSM_N]
            ht = ht_ref[:, gs]
            cb = _dot_nt(cg, bg)
            yoff = e_full[:, gs] * _dot(cg, ht)
            for jp in range(SSM_GW // CHUNK):
                j = g * (SSM_GW // CHUNK) + jp
                ps = slice(j * CHUNK, (j + 1) * CHUNK)
                x2 = xdt[:, ps]
                y0 = _dot(cb * _ssd_lambda(csb_ref, cst_ref, 2 * j, causal), x2)
                y1 = _dot(cb * _ssd_lambda(csb_ref, cst_ref, 2 * j + 1, causal), x2)
                y_ref[:, ps] = (jnp.where(lo, y0, y1) + yoff[:, jp * CHUNK:(jp + 1) * CHUNK]
                                + x[:, ps] * df_ref[:, ps])
            ht_ref[:, gs] = gamma[:, gs] * ht + _dot_tn(bg, xdt[:, gs] * f_full[:, gs])

    return pl.pallas_call(
        body, name=name, grid=(nc,),
        in_specs=[pl.BlockSpec((CHUNK, CONV_DIM), lambda c: (c, 0)), pl.BlockSpec((CHUNK, HPAD), lambda c: (c, 0))]
                 + _SSD_VEC_SPECS(),
        out_specs=[pl.BlockSpec((CHUNK, D_INNER), lambda c: (c, 0)), pl.BlockSpec((None, SSM_N, D_INNER), lambda c: (c, 0, 0))],
        out_shape=[jax.ShapeDtypeStruct((s, D_INNER), F32), jax.ShapeDtypeStruct((nc, SSM_N, D_INNER), F32)],
        scratch_shapes=[pltpu.VMEM((SSM_N, D_INNER), F32), pltpu.VMEM((CHUNK, SSM_HEADS * CHUNK), F32),
                        pltpu.VMEM((HPAD, CHUNK), F32), pltpu.VMEM((CHUNK, D_INNER), F32)],
        compiler_params=_cp(VMEM_BIG))(xbc, dtc, bias_row, alog_row, dfull)


def _ssd_bwd(xbc, dtc, bias_row, alog_row, dfull, dy, states, name):
    s = xbc.shape[0]
    nc = s // CHUNK
    rev = lambda c: nc - 1 - c

    def body(xbc_ref, dtc_ref, br_ref, ar_ref, df_ref, dy_ref, st_ref,
             dxbc_ref, ddt_ref, dalog_ref, dd_ref, dbias_ref,
             dht_ref, csb_ref, cst_ref, csf_ref, ddf_ref, dxs_ref, dcsf_ref, dcsl_ref):
        step = pl.program_id(0)

        @pl.when(step == 0)
        def _():
            dht_ref[...] = jnp.zeros_like(dht_ref)
            ddf_ref[...] = jnp.zeros_like(ddf_ref)
            dalog_ref[...] = jnp.zeros_like(dalog_ref)
            dbias_ref[...] = jnp.zeros_like(dbias_ref)
            dd_ref[...] = jnp.zeros_like(dd_ref)

        a_row, dt_c, _, dt_full, e_full, f_full, gamma, e64 = _ssd_common(dtc_ref, br_ref, ar_ref, csb_ref, cst_ref, csf_ref)
        x = xbc_ref[:, :D_INNER]
        xdt = x * dt_full
        dy_all = dy_ref[...]
        ddf_ref[...] += jnp.broadcast_to(jnp.sum(dy_all * x, axis=0, keepdims=True), ddf_ref.shape)
        causal = _iota((CHUNK, CHUNK), 0) >= _iota((CHUNK, CHUNK), 1)
        lo = _iota((CHUNK, CHUNK), 1) < SSM_P
        head_lane = _iota((CHUNK, HPAD), 1)
        head_row = _iota((HPAD, CHUNK), 0)
        dcs_heads = jnp.zeros((CHUNK, HPAD), F32)
        dcs_cols = jnp.zeros((HPAD, CHUNK), F32)
        for g in range(SSM_GROUPS):
            gs = slice(g * SSM_GW, (g + 1) * SSM_GW)
            b0 = D_INNER + g * SSM_N
            c0 = D_INNER + SSM_GROUPS * SSM_N + g * SSM_N
            bg = xbc_ref[:, b0:b0 + SSM_N]
            cg = xbc_ref[:, c0:c0 + SSM_N]
            ht = st_ref[:, gs]
            dht = dht_ref[:, gs]
            dyg = dy_all[:, gs]
            eg, fg, gg = e_full[:, gs], f_full[:, gs], gamma[:, gs]
            z = _dot(cg, ht)
            dz = dyg * eg
            dcg = _dot_nt(dz, ht)
            dht_new = _dot_tn(cg, dz) + gg * dht
            xf = xdt[:, gs] * fg
            dxf = _dot(bg, dht)
            dbg = _dot_nt(xf, dht)
            dff = dxf * xf
            dcsf_ref[:, gs] = dyg * eg * z - dff
            dcsl_ref[:, gs] = jnp.broadcast_to(
                jnp.sum(dff, axis=0, keepdims=True) + jnp.sum(dht * ht, axis=0, keepdims=True) * gg, (8, SSM_GW))
            cb = _dot_nt(cg, bg)
            dcb = jnp.zeros((CHUNK, CHUNK), F32)
            for jp in range(SSM_GW // CHUNK):
                j = g * (SSM_GW // CHUNK) + jp
                ps = slice(j * CHUNK, (j + 1) * CHUNK)
                x2 = xdt[:, ps]
                dy2 = dy_all[:, ps]
                dxh = []
                for hh in range(2):
                    h = 2 * j + hh
                    lam = _ssd_lambda(csb_ref, cst_ref, h, causal)
                    mh = cb * lam
                    dyh = jnp.where(lo, dy2, 0.0) if hh == 0 else jnp.where(lo, 0.0, dy2)
                    dm = _dot_nt(dyh, x2)
                    dcb = dcb + dm * lam
                    gm = dm * mh
                    dcs_heads = dcs_heads + jnp.where(head_lane == h, jnp.sum(gm, axis=1, keepdims=True), 0.0)
                    dcs_cols = dcs_cols + jnp.where(head_row == h, jnp.sum(gm, axis=0, keepdims=True), 0.0)
                    dxh.append(_dot_tn(mh, dy2))
                dxs_ref[:, ps] = jnp.where(lo, dxh[0], dxh[1]) + dxf[:, jp * CHUNK:(jp + 1) * CHUNK] * fg[:, jp * CHUNK:(jp + 1) * CHUNK]
            dxbc_ref[:, b0:b0 + SSM_N] = (dbg + _dot_tn(dcb, cg)).astype(dxbc_ref.dtype)
            dxbc_ref[:, c0:c0 + SSM_N] = (dcg + _dot(dcb, bg)).astype(dxbc_ref.dtype)
            dht_ref[:, gs] = dht_new
        dxs = dxs_ref[...]
        dcs_heads = dcs_heads - dcs_cols.T + _dot_sel(dcsf_ref[...], e64, ((1,), (1,)))
        dcs_last = _dot_sel(dcsl_ref[...], e64, ((1,), (1,)))
        dcs_heads = dcs_heads + jnp.where(_iota((CHUNK, HPAD), 0) == CHUNK - 1, dcs_last[0:1, :], 0.0)
        triu = _iota((CHUNK, CHUNK), 0) <= _iota((CHUNK, CHUNK), 1)
        dda = _sel_dot(triu, dcs_heads)
        ddt = dda * a_row + _dot_sel(dxs * x, e64, ((1,), (1,)))
        dxbc_ref[:, :D_INNER] = (dxs * dt_full + dy_all * df_ref[...]).astype(dxbc_ref.dtype)
        dalog_ref[...] += jnp.sum(dda * dt_c, axis=0, keepdims=True) * a_row
        ddt_raw = ddt * _sigmoid(dtc_ref[...] + br_ref[...])
        ddt_ref[...] = ddt_raw.astype(ddt_ref.dtype)
        dbias_ref[...] += jnp.sum(ddt_raw, axis=0, keepdims=True)

        @pl.when(step == nc - 1)
        def _():
            dd_ref[...] = _dot_sel(ddf_ref[...], e64, ((1,), (1,)))[0:1, :]

    vec = pl.BlockSpec((1, HPAD), lambda c: (0, 0))
    return pl.pallas_call(
        body, name=name, grid=(nc,),
        in_specs=[pl.BlockSpec((CHUNK, CONV_DIM), lambda c: (rev(c), 0)), pl.BlockSpec((CHUNK, HPAD), lambda c: (rev(c), 0))]
                 + _SSD_VEC_SPECS()
                 + [pl.BlockSpec((CHUNK, D_INNER), lambda c: (rev(c), 0)),
                    pl.BlockSpec((None, SSM_N, D_INNER), lambda c: (rev(c), 0, 0))],
        out_specs=[pl.BlockSpec((CHUNK, CONV_DIM), lambda c: (rev(c), 0)), pl.BlockSpec((CHUNK, HPAD), lambda c: (rev(c), 0)),
                   vec, vec, vec],
        out_shape=[jax.ShapeDtypeStruct((s, CONV_DIM), F32), jax.ShapeDtypeStruct((s, HPAD), _ACT),
                   jax.ShapeDtypeStruct((1, HPAD), F32), jax.ShapeDtypeStruct((1, HPAD), F32),
                   jax.ShapeDtypeStruct((1, HPAD), F32)],
        scratch_shapes=[pltpu.VMEM((SSM_N, D_INNER), F32), pltpu.VMEM((CHUNK, SSM_HEADS * CHUNK), F32),
                        pltpu.VMEM((HPAD, CHUNK), F32), pltpu.VMEM((CHUNK, D_INNER), F32),
                        pltpu.VMEM((8, D_INNER), F32), pltpu.VMEM((CHUNK, D_INNER), F32),
                        pltpu.VMEM((CHUNK, D_INNER), F32), pltpu.VMEM((8, D_INNER), F32)],
        compiler_params=_cp(VMEM_BIG))(xbc, dtc, bias_row, alog_row, dfull, dy, states)


def _gate_fwd(y, proj, gn, name, tm=512):
    s = y.shape[0]
    tm = min(tm, s)

    def body(y_ref, z_ref, gn_ref, o_ref):
        for g in range(SSM_GROUPS):
            gs = slice(g * SSM_GW, (g + 1) * SSM_GW)
            z = z_ref[:, gs]
            t = y_ref[:, gs] * (z * _sigmoid(z))
            r = lax.rsqrt(jnp.mean(t * t, axis=-1, keepdims=True) + EPS)
            o_ref[:, gs] = (t * r * gn_ref[:, gs]).astype(o_ref.dtype)

    row = pl.BlockSpec((tm, D_INNER), lambda i: (i, 0))
    return pl.pallas_call(
        body, name=name, grid=(s // tm,), in_specs=[row, row, pl.BlockSpec((1, D_INNER), lambda i: (0, 0))],
        out_specs=row, out_shape=jax.ShapeDtypeStruct((s, D_INNER + X_WIDTH), _ACT),
        compiler_params=_cp(VMEM_BIG))(y, proj, gn)


def _gate_bwd(y, proj, gn, dcat, name, tm=512):
    s = y.shape[0]
    tm = min(tm, s)

    def body(y_ref, z_ref, gn_ref, dm_ref, dy_ref, dz_ref, dgn_ref):
        @pl.when(pl.program_id(0) == 0)
        def _():
            dgn_ref[...] = jnp.zeros_like(dgn_ref)

        for g in range(SSM_GROUPS):
            gs = slice(g * SSM_GW, (g + 1) * SSM_GW)
            z = z_ref[:, gs]
            yv = y_ref[:, gs]
            sig = _sigmoid(z)
            sz = z * sig
            t = yv * sz
            r = lax.rsqrt(jnp.mean(t * t, axis=-1, keepdims=True) + EPS)
            th = t * r
            dm = dm_ref[:, gs].astype(F32)
            dmg = dm * gn_ref[:, gs]
            dt_ = r * (dmg - th * jnp.mean(dmg * th, axis=-1, keepdims=True))
            dgn_ref[:, gs] += jnp.sum(dm * th, axis=0, keepdims=True)
            dy_ref[:, gs] = dt_ * sz
            dz_ref[:, gs] = (dt_ * yv * (sig * (1.0 + z * (1.0 - sig)))).astype(dz_ref.dtype)

    row = pl.BlockSpec((tm, D_INNER), lambda i: (i, 0))
    vec = pl.BlockSpec((1, D_INNER), lambda i: (0, 0))
    return pl.pallas_call(
        body, name=name, grid=(s // tm,), in_specs=[row, row, vec, row], out_specs=[row, row, vec],
        out_shape=[jax.ShapeDtypeStruct((s, D_INNER), F32), jax.ShapeDtypeStruct((s, 6 * D_MODEL), _ACT),
                   jax.ShapeDtypeStruct((1, D_INNER), F32)], compiler_params=_cp(VMEM_BIG))(y, proj, gn, dcat)


def _block_of(kind, width):
    if kind == "col":
        return lambda ref, j: ref.at[:, :, pl.ds(pl.multiple_of(j * width, 128), width)]
    if kind == "row":
        return lambda ref, j: ref.at[:, pl.ds(pl.multiple_of(j * width, 8), width), :]
    return lambda ref, j: ref.at[j]


def _coords():
    return lax.axis_index("x"), lax.axis_index("y"), lax.axis_index("c")


def _rel_chip(x, y, k):
    return (1 - x if k & 1 else x), (1 - y if k & 2 else y)


def _all_gather_body(ins, outs, send_sems, recv_sems, local_sems, blocks):
    n = len(ins)
    x, y, c = _coords()
    sibling = (x, y, 1 - c)
    via = (x + (1 - c) * (1 - 2 * x), y + c * (1 - 2 * y))
    onto = (x + c * (1 - 2 * x), y + (1 - c) * (1 - 2 * y))

    def copy(t, k, chip, core, to, src=None):
        dst = blocks[t](outs[t], 4 * chip[0] + 2 * chip[1] + core)
        return pltpu.make_async_remote_copy(
            src_ref=dst if src is None else src, dst_ref=dst, send_sem=send_sems.at[t, k],
            recv_sem=recv_sems.at[t, k], device_id=to, device_id_type=MESH)

    started = []
    for t in range(n):
        mine = pltpu.make_async_copy(ins[t], blocks[t](outs[t], 4 * x + 2 * y + c), local_sems.at[t])
        mine.start()
        started.append(mine)
    sends = []
    for t in range(n):
        for k in range(3):
            px, py = _rel_chip(x, y, k)
            cp = copy(t, k, (x, y), c, (px, py, 1 - c if k == 0 else c), src=ins[t])
            cp.start()
            sends.append(cp)
    for t in range(n):
        for k in (1, 2):
            chip = _rel_chip(x, y, k)
            copy(t, k, chip, c, sibling).wait_recv()
            fwd = copy(t, 3 + k, chip, c, sibling)
            fwd.start()
            sends.append(fwd)
        hop = copy(t, 3, via, c, (*onto, c))
        hop.start()
        sends.append(hop)
    for t in range(n):
        diagonal = _rel_chip(x, y, 3)
        copy(t, 3, diagonal, c, sibling).wait_recv()
        fwd = copy(t, 6, diagonal, c, sibling)
        fwd.start()
        sends.append(fwd)
    for t in range(n):
        copy(t, 0, (x, y), 1 - c, sibling).wait_recv()
        for k in range(1, 4):
            copy(t, 3 + k, _rel_chip(x, y, k), 1 - c, sibling).wait_recv()
    for cp in sends:
        cp.wait_send()
    for mine in started:
        mine.wait()


def _handshake(peers):
    barrier = pltpu.get_barrier_semaphore()
    for peer in peers:
        pl.semaphore_signal(barrier, inc=1, device_id=peer, device_id_type=MESH)
    pl.semaphore_wait(barrier, len(peers))


def _gather_peers():
    x, y, c = _coords()
    return [(x, y, 1 - c)] + [(*_rel_chip(x, y, k), c) for k in (1, 2)]


SEQ_ID_GATHER, SEQ_ID_SIBLING, SEQ_ID_CHIPS = 1, 2, 3


def _sequencer_call(body, peers, operands, out_types, sems, name, collective_id, after=()):
    n_in, n_out, n_after = len(operands), len(out_types), len(after)

    def launch(*refs):
        _handshake(peers())
        body(refs[:n_in], refs[n_in + n_after:n_in + n_after + n_out], *refs[n_in + n_after + n_out:])

    return pl.kernel(
        launch, name=name, out_type=out_types, mesh=plsc.ScalarSubcoreMesh(axis_name="seq", num_cores=1),
        scratch_types=sems, compiler_params=pltpu.CompilerParams(collective_id=collective_id))(*operands, *after)


def _all_gather_seq(shards, layouts, name, after=()):
    n = len(shards)
    blocks = [_block_of(kind, width) for kind, width, _ in layouts]
    return _sequencer_call(
        lambda ins, outs, *sems: _all_gather_body(ins, outs, *sems, blocks), _gather_peers, shards,
        [jax.ShapeDtypeStruct(shape, sh.dtype) for sh, (_, _, shape) in zip(shards, layouts)],
        [pltpu.SemaphoreType.DMA((n, 7)), pltpu.SemaphoreType.DMA((n, 7)), pltpu.SemaphoreType.DMA((n,))],
        name, SEQ_ID_GATHER, after)


def _tie(small, after, name):
    del name
    return lax.optimization_barrier((small, *after))[0]


def _rs_to_sibling(grads, layouts, name, after=()):
    n = len(grads)
    blocks = [_block_of(kind, width) for kind, width, _ in layouts]

    def body(ins, outs, send_sems, recv_sems):
        x, y, c = _coords()
        sibling = (x, y, 1 - c)
        cps = []
        for t in range(n):
            for k in range(4):
                px, py = _rel_chip(x, y, k)
                cp = pltpu.make_async_remote_copy(
                    src_ref=blocks[t](ins[t], 4 * px + 2 * py + (1 - c)), dst_ref=outs[t].at[k],
                    send_sem=send_sems.at[t, k], recv_sem=recv_sems.at[t, k], device_id=sibling, device_id_type=MESH)
                cp.start()
                cps.append(cp)
        for cp in cps:
            cp.wait_recv()
        for cp in cps:
            cp.wait_send()

    def sibling_only():
        x, y, c = _coords()
        return [(x, y, 1 - c)]

    return _sequencer_call(
        body, sibling_only, grads,
        [jax.ShapeDtypeStruct((4,) + shape, g.dtype) for g, (_, _, shape) in zip(grads, layouts)],
        [pltpu.SemaphoreType.DMA((n, 4)), pltpu.SemaphoreType.DMA((n, 4))], name, SEQ_ID_SIBLING, after)


def _rs_chip_sum(grad, recv, layout, xyc, name):
    kind, width, shape = layout
    r, ccols = shape

    def src_index(k, xyc_ref):
        px = jnp.where(k % 2 == 1, 1 - xyc_ref[0], xyc_ref[0])
        py = jnp.where(k // 2 == 1, 1 - xyc_ref[1], xyc_ref[1])
        return 4 * px + 2 * py + xyc_ref[2]

    if kind == "col":
        g_spec = pl.BlockSpec((r, ccols), lambda k, s_: (0, src_index(k, s_)))
    elif kind == "row":
        g_spec = pl.BlockSpec((r, ccols), lambda k, s_: (src_index(k, s_), 0))
    else:
        g_spec = pl.BlockSpec((None, r, ccols), lambda k, s_: (src_index(k, s_), 0, 0))

    def body(xyc_ref, g_ref, r_ref, o_ref):
        o_ref[...] = (g_ref[...].astype(F32) + r_ref[...].astype(F32)).astype(o_ref.dtype)

    slot = pl.BlockSpec((None, r, ccols), lambda k, s_: (k, 0, 0))
    return pl.pallas_call(
        body, name=name,
        grid_spec=pltpu.PrefetchScalarGridSpec(num_scalar_prefetch=1, grid=(4,), in_specs=[g_spec, slot], out_specs=slot),
        out_shape=jax.ShapeDtypeStruct((4, r, ccols), grad.dtype), compiler_params=_cp(VMEM_BIG))(xyc, grad, recv)


def _rs_across_chips(parts, name):
    n = len(parts)

    def body(ins, outs, send_sems, recv_sems):
        x, y, c = _coords()
        cps = []
        for t in range(n):
            for k in range(1, 4):
                px, py = _rel_chip(x, y, k)
                cp = pltpu.make_async_remote_copy(
                    src_ref=ins[t].at[k], dst_ref=outs[t].at[k - 1], send_sem=send_sems.at[t, k - 1],
                    recv_sem=recv_sems.at[t, k - 1], device_id=(px, py, c), device_id_type=MESH)
                cp.start()
                cps.append(cp)
        for cp in cps:
            cp.wait_recv()
        for cp in cps:
            cp.wait_send()

    def other_chips():
        x, y, c = _coords()
        return [(*_rel_chip(x, y, k), c) for k in range(1, 4)]

    return _sequencer_call(
        body, other_chips, parts, [jax.ShapeDtypeStruct((3,) + p.shape[1:], p.dtype) for p in parts],
        [pltpu.SemaphoreType.DMA((n, 3)), pltpu.SemaphoreType.DMA((n, 3))], name, SEQ_ID_CHIPS)


def _adamw_math(w, g, m, v):
    m = ADAM_B1 * m + (1.0 - ADAM_B1) * g
    v = ADAM_B2 * v + (1.0 - ADAM_B2) * jnp.square(g)
    m_hat = m / (1.0 - ADAM_B1 ** ADAM_STEP)
    v_hat = v / (1.0 - ADAM_B2 ** ADAM_STEP)
    delta = -ADAM_LR * (m_hat / (jnp.sqrt(v_hat) + ADAM_EPS) + ADAM_WD * w)
    return delta, m, v


def _row_tile(rows, cap):
    best = None
    for cand in range(8, min(rows, cap) + 1, 8):
        if rows % cand == 0:
            best = cand
    assert best is not None, rows
    return best


def _adamw(w, m, v, parts, name, layer=None, prev=None, tr=256):
    r, ccols = w.shape[-2:]
    npart = len(parts)
    if r % 8 == 0:
        tr, tc = _row_tile(r, tr), ccols
        steps, at = r // tr, (lambda i: (i, 0))
    else:
        tr, tc = r, 256
        assert ccols % tc == 0
        steps, at = ccols // tc, (lambda i: (0, i))

    def spec(lead):
        if lead is None:
            return pl.BlockSpec((tr, tc), at)
        return pl.BlockSpec((None, tr, tc), lambda i: (lead,) + at(i))

    wspec = lambda: spec(layer)
    pspec = spec

    def body(*refs):
        w_ref, m_ref, v_ref = refs[:3]
        p_refs = refs[3:3 + npart]
        outs = refs[len(refs) - 4:]
        g = p_refs[0][...].astype(F32)
        for p_ref in p_refs[1:]:
            g = g + p_ref[...].astype(F32)
        delta, mn, vn = _adamw_math(w_ref[...], g, m_ref[...], v_ref[...])
        outs[0][...] = g
        outs[1][...] = delta
        outs[2][...] = mn
        outs[3][...] = vn

    operands = [w, m, v] + [p for p, _ in parts]
    in_specs = [wspec(), wspec(), wspec()] + [pspec(lead) for _, lead in parts]
    aliases = {}
    if prev is not None:
        for i, p in enumerate(prev):
            aliases[len(operands)] = i
            operands.append(p)
            in_specs.append(pl.BlockSpec(memory_space=pl.ANY))
    return pl.pallas_call(
        body, name=name, grid=(steps,), in_specs=in_specs, out_specs=[wspec()] * 4,
        out_shape=[jax.ShapeDtypeStruct(w.shape, F32)] * 4, input_output_aliases=aliases)(*operands)


def _small_update(gathered, params, loss_all, me, name):
    n = len(gathered)
    shapes = [w.shape for w, _, _ in params]

    def body(me_ref, *refs):
        g_refs, loss_ref = refs[:n], refs[n]
        p_refs = refs[n + 1:n + 1 + 3 * n]
        o_refs = refs[n + 1 + 3 * n:]
        for i in range(n):
            r, c = shapes[i]
            if gathered[i].shape[2] == c:
                parts = [g_refs[i][j] for j in range(N_DEV)]
            else:
                off = pl.multiple_of(me_ref[0] * c, 128)
                parts = [g_refs[i][j, :, pl.ds(off, c)] for j in range(N_DEV)]
            g = functools.reduce(lambda a, b: a + b, parts)
            delta, mn, vn = _adamw_math(p_refs[3 * i][...], g, p_refs[3 * i + 1][...], p_refs[3 * i + 2][...])
            for k, val in enumerate((g, delta, mn, vn)):
                o_refs[4 * i + k][...] = val
        o_refs[4 * n][...] = functools.reduce(lambda a, b: a + b, [loss_ref[j] for j in range(N_DEV)])

    vmem = pl.BlockSpec(memory_space=pltpu.VMEM)
    flat_params = [a for p in params for a in p]
    outs = pl.pallas_call(
        body, name=name, in_specs=[pl.BlockSpec(memory_space=pltpu.SMEM)] + [vmem] * (n + 1 + 3 * n),
        out_specs=[vmem] * (4 * n + 1),
        out_shape=[jax.ShapeDtypeStruct(shp, F32) for shp in shapes for _ in range(4)] + [jax.ShapeDtypeStruct((1, 128), F32)],
        compiler_params=_cp(VMEM_BIG))(me, *gathered, loss_all, *flat_params)
    return [tuple(outs[4 * i:4 * i + 4]) for i in range(n)], outs[4 * n]


def _sum8(buf, name):
    _, r, ccols = buf.shape

    def body(b_ref, o_ref):
        acc = b_ref[0]
        for j in range(1, N_DEV):
            acc = acc + b_ref[j]
        o_ref[...] = acc

    tr = _row_tile(r, 256)
    return pl.pallas_call(
        body, name=name, grid=(r // tr,), in_specs=[pl.BlockSpec((N_DEV, tr, ccols), lambda i: (0, i, 0))],
        out_specs=pl.BlockSpec((tr, ccols), lambda i: (i, 0)), out_shape=jax.ShapeDtypeStruct((r, ccols), F32))(buf)


def _pack(arrays):
    pieces, layout, off = [], [], 0
    for a in arrays:
        n = a.size
        padded = -(-n // 1024) * 1024
        flat = a.reshape(-1).astype(F32)
        if padded != n:
            flat = jnp.pad(flat, (0, padded - n))
        pieces.append(flat.reshape(padded // 128, 128))
        layout.append((off, n, a.shape))
        off += padded // 128
    return jnp.concatenate(pieces, axis=0), layout


def _unpack(packed, layout):
    out = []
    for off, n, shape in layout:
        rows = -(-n // 1024) * 8
        out.append(packed[off:off + rows].reshape(-1)[:n].reshape(shape))
    return out


def kernel(x, mem, norm_mix, norm_ffn, mem_norm, w_kv, w_out, w_ffn1, w_ffn2, a_in, a_ln_g, a_ln_b, a_ws, a_bs, b_in, b_conv_w, b_conv_b, b_dt_bias, b_a_log, b_d, b_gnorm, final_norm, loss_target, m_norm_mix, m_norm_ffn, m_mem_norm, m_w_kv, m_w_out, m_w_ffn1, m_w_ffn2, m_a_in, m_a_ln_g, m_a_ln_b, m_a_ws, m_a_bs, m_b_in, m_b_conv_w, m_b_conv_b, m_b_dt_bias, m_b_a_log, m_b_d, m_b_gnorm, m_final_norm, v_norm_mix, v_norm_ffn, v_mem_norm, v_w_kv, v_w_out, v_w_ffn1, v_w_ffn2, v_a_in, v_a_ln_g, v_a_ln_b, v_a_ws, v_a_bs, v_b_in, v_b_conv_w, v_b_conv_b, v_b_dt_bias, v_b_a_log, v_b_d, v_b_gnorm, v_final_norm):
    s = x.shape[1]
    xs = x.reshape(s, D_MODEL)
    mems = mem.reshape(N_MEM, D_MODEL)
    target = loss_target.reshape(s, D_MODEL)
    ax, ay, ac = lax.axis_index("x"), lax.axis_index("y"), lax.axis_index("c")
    me = 4 * ax + 2 * ay + ac
    xyc = jnp.stack([ax, ay, ac]).astype(jnp.int32)

    b_cols = b_in.shape[2]
    act = lambda a: a.astype(_ACT)
    lay_f1, lay_f2 = ("col", 512, (1, D_MODEL, D_FF)), ("row", 512, (1, D_FF, D_MODEL))
    lay_out, lay_kv = ("row", 384, (1, 3 * D_MODEL, D_MODEL)), ("col", 256, (1, D_MODEL, 2 * X_WIDTH))
    small_w_pack = _pack([b_conv_w[0], b_conv_b[0], b_gnorm[0]])[0]
    (WA,) = _all_gather_seq([act(a_in)], [("col", 640, (1, D_MODEL, 5 * D_MODEL))], "ag_proj_a")
    wo0, wkv0 = _all_gather_seq([act(w_out[0:1]), act(w_kv[0:1])], [lay_out, lay_kv], "ag_out0")
    w1_0, w2_0 = _all_gather_seq([act(w_ffn1[0:1]), act(w_ffn2[0:1])], [lay_f1, lay_f2], "ag_ffn0")
    a0 = _rms_fwd(xs, norm_mix[0].reshape(1, -1), "mix_norm0")
    tr_b = lambda a: jnp.swapaxes(a, 1, 2)
    wbt_blk, small_w = _all_gather_seq(
        [act(tr_b(b_in)[0]), small_w_pack],
        [("blk", 0, (N_DEV, b_cols, D_MODEL)), ("blk", 0, (N_DEV, 32, 128))], "ag_proj_b", after=[a0])
    wo1, wkv1 = _all_gather_seq([act(w_out[1:2]), act(w_kv[1:2])], [lay_out, lay_kv], "ag_out1", after=[a0])
    w1_1, w2_1 = _all_gather_seq([act(w_ffn1[1:2]), act(w_ffn2[1:2])], [lay_f1, lay_f2], "ag_ffn1", after=[a0])
    W1, W2, WO, WKV = [w1_0, w1_1], [w2_0, w2_1], [wo0, wo1], [wkv0, wkv1]
    dt0 = D_INNER + CONV_DIM

    row = lambda a: a.reshape(1, -1)
    nmix = [row(norm_mix[0]), row(norm_mix[1])]
    nffn = [row(norm_ffn[0]), row(norm_ffn[1])]
    nmem = [row(mem_norm[0]), row(mem_norm[1])]
    fin = row(final_norm)
    lng, lnb = a_ln_g.reshape(1, D_INNER), a_ln_b.reshape(1, D_INNER)
    ws = a_ws[0]
    bs3 = a_bs[0].reshape(A_GROUPS, CHUNK, 1)
    pad_h = lambda a: jnp.pad(a.reshape(-1), (0, HPAD - SSM_HEADS))
    bias_row = pad_h(b_dt_bias).reshape(1, HPAD)
    alog_row = pad_h(b_a_log).reshape(1, HPAD)
    dfull = jnp.repeat(b_d.reshape(-1), SSM_P).reshape(1, D_INNER)

    kvs, mns = [None, None], [None, None]

    def mem_kv(i, after=None):
        gain = nmem[i] if after is None else _tie(nmem[i], after, f"tie_mem{i}")
        mns[i] = _rms_fwd(mems, gain, f"mem_norm{i}")
        kvs[i] = _mm(mns[i], WKV[i], m=N_MEM, n=2 * X_WIDTH, k=D_MODEL, b_at=(0, 0, 0), out_dtype=_ACT, name=f"kv{i}")

    def ffn_fwd(h, i):
        f = _rms_fwd(h, nffn[i], f"ffn_norm{i}")
        p = _mm(f, W1[i], m=s, n=D_FF, k=D_MODEL, b_at=(0, 0, 0), out_dtype=_ACT, name=f"ffn_up{i}")
        hn = _mm(p, W2[i], m=s, n=D_MODEL, k=D_FF, b_at=(0, 0, 0), a_pro="relu2", add=h, name=f"ffn_down{i}")
        return f, p, hn

    def out_proj(h, cat, i):
        return _mm(cat, WO[i], m=s, n=D_MODEL, k=3 * D_MODEL, b_at=(0, 0, 0), add=h, name=f"out_proj{i}")

    proj_a = _mm(a0, WA, m=s, n=5 * D_MODEL, k=D_MODEL, b_at=(0, 0, 0), name="proj_a")
    mem_kv(0, after=[proj_a])
    cat_a = _gmlp_fwd(proj_a, lng, lnb, ws, bs3, "gmlp_fwd")
    cat_a = _attn_fwd(proj_a, 4, kvs[0], cat_a, "attn_fwd0")
    h1 = out_proj(xs, cat_a, 0)
    f0, p0, h2 = ffn_fwd(h1, 0)

    wbt_blk, small_w, _ = lax.optimization_barrier((wbt_blk, small_w, p0))
    wbt_full = wbt_blk.reshape(N_DEV * b_cols, D_MODEL)
    WBT = jnp.concatenate([wbt_full[:dt0], wbt_full[dt0 + SSM_HEADS:]], axis=0)
    WBDT = jnp.pad(wbt_full[dt0:dt0 + SSM_HEADS], ((0, HPAD - SSM_HEADS), (0, 0)))
    cw_sh, cb_sh, gn_sh = 4 * 384, 384, 256
    sw = small_w.reshape(N_DEV, 32 * 128)
    conv_w = jnp.transpose(sw[:, :cw_sh].reshape(N_DEV, CONV_K, 384), (1, 0, 2)).reshape(CONV_K, CONV_DIM)
    conv_b = sw[:, 2048:2048 + cb_sh].reshape(1, CONV_DIM)
    gnorm = sw[:, 3072:3072 + gn_sh].reshape(1, D_INNER)

    a1 = _rms_fwd(h2, nmix[1], "mix_norm1")
    proj_b = _mm(a1, WBT, m=s, n=6 * D_MODEL, k=D_MODEL, tb=True, name="proj_b")
    dt_raw = _mm(a1, WBDT, m=s, n=HPAD, k=D_MODEL, tb=True, name="proj_dt")
    xbc = _conv_fwd(proj_b, conv_w, conv_b, "conv_fwd")
    y_ssd, states = _ssd_fwd(xbc, dt_raw, bias_row, alog_row, dfull, "ssd_fwd")
    cat_b = _gate_fwd(y_ssd, proj_b, gnorm, "gate_fwd")
    mem_kv(1, after=[cat_b])
    cat_b = _attn_fwd(proj_b, 5, kvs[1], cat_b, "attn_fwd1")
    h3 = out_proj(h2, cat_b, 1)
    f1, p1, h4 = ffn_fwd(h3, 1)

    loss_part, dh, dh_act, d_fin = _loss_head(h4, fin, target, "loss_head")

    g_f1, g_f2, g_out, g_kv = [None, None], [None, None], [None, None], [None, None]
    d_nffn, d_nmix, d_nmem = [None, None], [None, None], [None, None]

    def ffn_bwd(dh, dh_act, h_in, f, p, i, after=(), after_last=()):
        dp = _mm(dh_act, W2[i], m=s, n=D_FF, k=D_MODEL, tb=True, b_at=(0, 0, 0), epi_p=p, out_dtype=_ACT, name=f"ffn_down_dx{i}")
        g_f2[i] = _mm(p, dh_act, m=D_FF, n=D_MODEL, k=s, ta=True, a_pro="relu2", out_dtype=_ACT, name=f"ffn_down_dw{i}")
        g_f1[i] = _mm(f, dp, m=D_MODEL, n=D_FF, k=s, ta=True, out_dtype=_ACT, name=f"ffn_up_dw{i}")
        df = _mm(dp, W1[i], m=s, n=D_MODEL, k=D_FF, tb=True, b_at=(0, 0, 0), after=after, name=f"ffn_up_dx{i}")
        gain = _tie(nffn[i], after_last, f"tie_ffn_norm{i}") if after_last else nffn[i]
        dh_in, dh_in_act, d_nffn[i] = _rms_bwd(h_in, gain, df, dh, f"ffn_norm_bwd{i}")
        return dh_in, dh_in_act

    def out_bwd(dh_act, cat, i):
        dcat = _mm(dh_act, WO[i], m=s, n=3 * D_MODEL, k=D_MODEL, tb=True, b_at=(0, 0, 0), out_dtype=_ACT, name=f"out_dx{i}")
        g_out[i] = _mm(cat, dh_act, m=3 * D_MODEL, n=D_MODEL, k=s, ta=True, out_dtype=_ACT, name=f"out_dw{i}")
        return dcat

    def mem_bwd(dkv, i):
        g_kv[i] = _mm(mns[i], dkv, m=D_MODEL, n=2 * X_WIDTH, k=N_MEM, ta=True, out_dtype=_ACT, name=f"kv_dw{i}")
        dmn = _mm(dkv, WKV[i], m=N_MEM, n=D_MODEL, k=2 * X_WIDTH, tb=True, b_at=(0, 0, 0), name=f"kv_dx{i}")
        _, _, d_nmem[i] = _rms_bwd(mems, nmem[i], dmn, None, f"mem_norm_bwd{i}")

    lay_g = {"f1": ("col", 512, (D_MODEL, 512)), "f2": ("row", 512, (512, D_MODEL)), "out": ("row", 384, (384, D_MODEL)),
             "kv": ("col", 256, (D_MODEL, 256)), "a": ("col", 640, (D_MODEL, 640)), "b": ("blk", 0, (b_cols, D_MODEL))}
    reduced = {}

    def reduce_scatter(group, tag, after=(), sums_after=()):
        grads3, lays3 = [], []
        for fam, _, g in group:
            kind, width, shape = lay_g[fam]
            grads3.append(g if kind == "blk" else g.reshape((1,) + g.shape))
            lays3.append((kind, width, shape if kind == "blk" else (1,) + shape))
        recv1 = _rs_to_sibling(grads3, lays3, f"rs_sibling_{tag}", after)
        if sums_after:
            recv1 = lax.optimization_barrier((tuple(recv1), tuple(sums_after)))[0]
        parts = [_rs_chip_sum(g, recv1[t].reshape((4,) + lay_g[fam][2]), lay_g[fam], xyc, f"rs_chip_sum_{fam}{i}")
                 for t, (fam, i, g) in enumerate(group)]
        recv2 = _rs_across_chips(parts, f"rs_chips_{tag}")
        for (fam, i, _), p, r2 in zip(group, parts, recv2):
            reduced[fam, i] = (p, r2)
        return parts, recv2

    dh3, dh3_act = ffn_bwd(dh, dh_act, h3, f1, p1, 1)
    dcat_b = out_bwd(dh3_act, cat_b, 1)
    sums, got_ffn1 = reduce_scatter([("f1", 1, g_f1[1]), ("f2", 1, g_f2[1]), ("out", 1, g_out[1])], "ffn1", sums_after=[dcat_b])
    dy_ssd, dproj_b, d_gnorm = _gate_bwd(y_ssd, proj_b, gnorm, dcat_b, "gate_bwd")
    dproj_b, dkv_b = _attn_bwd(proj_b, 5, kvs[1], dcat_b, dproj_b, "attn_bwd1")
    mem_bwd(dkv_b, 1)
    dxbc, ddt_raw, d_alog, d_dskip, d_dtbias = _ssd_bwd(
        xbc, dt_raw, _tie(bias_row, sums, "tie_ffn1"), alog_row, dfull, dy_ssd, states, "ssd_bwd")
    dproj_b, d_convw, d_convb = _conv_bwd(proj_b, conv_w, _tie(conv_b, got_ffn1, "tie_got_ffn1"), dxbc, dproj_b, "conv_bwd")
    gb = _mm(dproj_b, a1, m=6 * D_MODEL, n=D_MODEL, k=s, ta=True, out_dtype=_ACT, name="proj_b_dw")
    gb_dt = _mm(ddt_raw, a1, m=HPAD, n=D_MODEL, k=s, ta=True, out_dtype=_ACT, name="proj_b_dw_dt")
    gb_full = jnp.concatenate([gb[:dt0], gb_dt[:SSM_HEADS], gb[dt0:]], axis=0)
    gb_blk = gb_full.reshape(N_DEV, b_cols, D_MODEL)
    da1 = _mm(dproj_b, WBT, m=s, n=D_MODEL, k=6 * D_MODEL, name="proj_b_dx")
    sums, got_mix1 = reduce_scatter([("kv", 1, g_kv[1]), ("b", 0, gb_blk)], "mix1", sums_after=[da1])
    da1 = _mm(ddt_raw, WBDT, m=s, n=D_MODEL, k=HPAD, add=da1, name="proj_b_dx_dt")
    dh2, dh2_act, d_nmix[1] = _rms_bwd(h2, _tie(nmix[1], sums, "tie_mix1"), da1, dh3, "mix_norm_bwd1")

    dh1, dh1_act = ffn_bwd(dh2, dh2_act, h1, f0, p0, 0, after=got_ffn1, after_last=got_mix1)
    dcat_a = out_bwd(dh1_act, cat_a, 0)
    sums, got_ffn0 = reduce_scatter([("f1", 0, g_f1[0]), ("f2", 0, g_f2[0]), ("out", 0, g_out[0])], "ffn0", sums_after=[dcat_a])
    dproj_a, d_ws, d_bs3, d_lng, d_lnb = _gmlp_bwd(proj_a, dcat_a, _tie(lng, sums, "tie_ffn0"), lnb, ws, bs3, "gmlp_bwd")
    dproj_a, dkv_a = _attn_bwd(proj_a, 4, kvs[0], dcat_a, dproj_a, "attn_bwd0")
    da0 = _mm(dproj_a, WA, m=s, n=D_MODEL, k=5 * D_MODEL, tb=True, b_at=(0, 0, 0), name="proj_a_dx")
    grad_x, _, d_nmix[0] = _rms_bwd(xs, nmix[0], da0, dh1, "mix_norm_bwd0")
    mem_bwd(dkv_a, 0)

    small_names = ["norm_mix", "norm_ffn", "mem_norm", "a_ln_g", "a_ln_b", "a_ws", "a_bs", "b_dt_bias", "b_a_log", "b_d",
                   "final_norm", "b_conv_w", "b_conv_b", "b_gnorm"]
    small_grads = [jnp.concatenate(d_nmix, axis=0), jnp.concatenate(d_nffn, axis=0), jnp.concatenate(d_nmem, axis=0),
                   d_lng, d_lnb, d_ws.reshape(A_GROUPS * CHUNK, CHUNK), d_bs3.reshape(A_GROUPS, CHUNK),
                   d_dtbias[:, :SSM_HEADS], d_alog[:, :SSM_HEADS], d_dskip[:, :SSM_HEADS], d_fin,
                   d_convw, d_convb, d_gnorm]
    small_2d = [(2, D_MODEL)] * 3 + [(1, D_INNER)] * 2 + [(A_GROUPS * CHUNK, CHUNK), (A_GROUPS, CHUNK)] + [(1, SSM_HEADS)] * 3 \
        + [(1, D_MODEL), (CONV_K, 384), (1, 384), (1, 256)]
    gathered = _all_gather_seq(
        small_grads + [loss_part], [("blk", 0, (N_DEV,) + g.shape) for g in small_grads + [loss_part]],
        "ag_small_grads", after=got_ffn0)
    g_all = gathered[0]

    def big_update(w, m, v, fam, nlayer):
        res = None
        for i in range(nlayer):
            part, recv2 = reduced[fam, i]
            plist = [(part, 0), (recv2, 0), (recv2, 1), (recv2, 2)]
            res = _adamw(w, m, v, plist, f"adamw_{fam}{i}", layer=i, prev=res)
        return res

    ga = _mm(a0, dproj_a, m=D_MODEL, n=5 * D_MODEL, k=s, ta=True, out_dtype=_ACT, after=small_grads[:3], name="proj_a_dw")
    r_b = big_update(tr_b(b_in), tr_b(m_b_in), tr_b(v_b_in), "b", 1)
    reduce_scatter([("kv", 0, g_kv[0]), ("a", 0, ga)], "mix0", after=[g_all], sums_after=r_b)
    r_b = [tr_b(o) for o in r_b]

    r_f1 = big_update(w_ffn1, m_w_ffn1, v_w_ffn1, "f1", 2)
    r_f2 = big_update(w_ffn2, m_w_ffn2, v_w_ffn2, "f2", 2)
    r_out = big_update(w_out, m_w_out, v_w_out, "out", 2)
    r_kv = big_update(w_kv, m_w_kv, v_w_kv, "kv", 2)
    r_a = big_update(a_in, m_a_in, v_a_in, "a", 1)

    small_w = [norm_mix, norm_ffn, mem_norm, a_ln_g, a_ln_b, a_ws, a_bs, b_dt_bias, b_a_log, b_d, final_norm,
               b_conv_w, b_conv_b, b_gnorm]
    small_m = [m_norm_mix, m_norm_ffn, m_mem_norm, m_a_ln_g, m_a_ln_b, m_a_ws, m_a_bs, m_b_dt_bias, m_b_a_log, m_b_d,
               m_final_norm, m_b_conv_w, m_b_conv_b, m_b_gnorm]
    small_v = [v_norm_mix, v_norm_ffn, v_mem_norm, v_a_ln_g, v_a_ln_b, v_a_ws, v_a_bs, v_b_dt_bias, v_b_a_log, v_b_d,
               v_final_norm, v_b_conv_w, v_b_conv_b, v_b_gnorm]
    params = [tuple(a.reshape(shp) for a in wmv) for shp, wmv in zip(small_2d, zip(small_w, small_m, small_v))]
    small_res, loss_sum = _small_update(gathered[:-1], params, gathered[-1], me.astype(jnp.int32).reshape(1), "adamw_small")
    loss = loss_sum[0, 0]

    names = ["norm_mix", "norm_ffn", "mem_norm", "w_kv", "w_out", "w_ffn1", "w_ffn2", "a_in", "a_ln_g", "a_ln_b", "a_ws",
             "a_bs", "b_in", "b_conv_w", "b_conv_b", "b_dt_bias", "b_a_log", "b_d", "b_gnorm", "final_norm"]
    big = {"w_kv": r_kv, "w_out": r_out, "w_ffn1": r_f1, "w_ffn2": r_f2, "a_in": r_a, "b_in": r_b}
    outs = [loss, grad_x.reshape(x.shape)]
    for kind in range(4):
        for nm in names:
            if nm in big:
                outs.append(big[nm][kind])
            else:
                i = small_names.index(nm)
                outs.append(small_res[i][kind].reshape(small_w[i].shape))
    return tuple(outs)
```

```python
import functools
import math

import jax
import jax.numpy as jnp
from jax import lax
from jax.experimental import pallas as pl
from jax.experimental.pallas import tpu as pltpu
from jax.experimental.pallas import tpu_sc as plsc

F32 = jnp.float32
_MXU = jnp.bfloat16
_ACT = jnp.bfloat16
_HI = lax.Precision.HIGHEST

D_MODEL = 1024
CHUNK = 128
N_MEM = 256
D_INNER = 2048
A_GROUPS = 8
A_GW = D_INNER // A_GROUPS
SSM_HEADS = 32
SSM_P = 64
SSM_GROUPS = 4
SSM_GW = D_INNER // SSM_GROUPS
SSM_N = 128
CONV_K = 4
CONV_DIM = 3072
X_HEADS = 4
X_HD = 256
X_WIDTH = 1024
D_FF = 4096
EPS = 1e-6
HPAD = 128
N_DEV = 8

ADAM_LR = 0.001
ADAM_B1 = 0.9
ADAM_B2 = 0.999
ADAM_EPS = 1e-08
ADAM_WD = 0.01
ADAM_STEP = 10

VMEM_BIG = 56 * 1024 * 1024
MESH = pl.DeviceIdType.MESH


def _cp(vmem=None):
    if vmem is None:
        return pltpu.CompilerParams()
    return pltpu.CompilerParams(vmem_limit_bytes=vmem)


def _dot(a, b, dims=((1,), (0,))):
    return lax.dot_general(a.astype(_MXU), b.astype(_MXU), (dims, ((), ())), preferred_element_type=F32)


def _dot_nt(a, b):
    return _dot(a, b, ((1,), (1,)))


def _dot_tn(a, b):
    return _dot(a, b, ((0,), (0,)))


def _dot_hi(a, b, dims=((1,), (0,))):
    return lax.dot_general(a.astype(F32), b.astype(F32), (dims, ((), ())), precision=_HI, preferred_element_type=F32)


def _split3(x):
    x1 = x.astype(jnp.bfloat16)
    r = x - x1.astype(F32)
    x2 = r.astype(jnp.bfloat16)
    x3 = (r - x2.astype(F32)).astype(jnp.bfloat16)
    return x1, x2, x3


def _dot_sel(x, sel, dims=((1,), (0,)), terms=2):
    sel = sel.astype(jnp.bfloat16)
    parts = [lax.dot_general(t, sel, (dims, ((), ())), preferred_element_type=F32) for t in _split3(x)[:terms]]
    return functools.reduce(lambda a, b: a + b, parts)


def _sel_dot(sel, x, dims=((1,), (0,))):
    sel = sel.astype(jnp.bfloat16)
    parts = [lax.dot_general(sel, t, (dims, ((), ())), preferred_element_type=F32) for t in _split3(x)]
    return (parts[0] + parts[1]) + parts[2]


def _sigmoid(x):
    return 1.0 / (1.0 + jnp.exp(-x))


def _gelu(x):
    return 0.5 * x * (1.0 + lax.erf(x * (1.0 / math.sqrt(2.0))))


def _gelu_grad(x):
    return 0.5 * (1.0 + lax.erf(x * (1.0 / math.sqrt(2.0)))) + x * jnp.exp(-0.5 * x * x) * (1.0 / math.sqrt(2.0 * math.pi))


def _softplus(x):
    return jnp.maximum(x, 0.0) + jnp.log1p(jnp.exp(-jnp.abs(x)))


def _iota(shape, dim):
    return lax.broadcasted_iota(jnp.int32, shape, dim)


MM_VMEM_BUDGET = 44 * 1024 * 1024
HBM_BYTES_PER_S = 2.5e12
GRID_STEP_S = 0.35e-6
VMEM_ACC_BYTES_PER_S = 6e12
MXU_FLOPS_PER_S = 9e14
MXU_TILE = 256


def _divisors(dim, unit):
    out = [d for d in range(unit, min(dim, 2048) + 1, unit) if dim % d == 0]
    return out if out else [dim]


def _mm_tiles(m, n, k, sa, sb, s_mn, a_pro, offsets):
    best = None
    (a_r0, a_c0, ta), (b_r0, b_c0, tb), (o_r0, o_c0) = offsets
    for tm in _divisors(m, 128):
        for tn in _divisors(n, 128):
            for tk in [k // d for d in (1, 2, 3, 4, 6, 8) if k % d == 0 and (k // d) % 128 == 0]:
                a_t = (tk, tm) if ta else (tm, tk)
                b_t = (tn, tk) if tb else (tk, tn)
                if a_r0 % a_t[0] or a_c0 % a_t[1] or b_r0 % b_t[0] or b_c0 % b_t[1] or o_r0 % tm or o_c0 % tn:
                    continue
                nk = k // tk
                vmem = 2 * (tm * tk * sa + tk * tn * sb + tm * tn * s_mn) + tm * tn * 4 * (2 if nk > 1 else 1)
                if a_pro or sa == 4:
                    vmem += tm * tk * 6
                if sb == 4:
                    vmem += tk * tn * 2
                if vmem > MM_VMEM_BUDGET:
                    continue
                gi, gj = m // tm, n // tn
                for j_inner in (True, False):
                    if nk > 1:
                        traffic = gj * m * k * sa + gi * k * n * sb
                    elif j_inner:
                        traffic = m * k * sa + gi * k * n * sb
                    else:
                        traffic = gj * m * k * sa + k * n * sb
                    traffic += m * n * s_mn + (tm * tk * sa + tk * tn * sb)
                    cost = traffic / HBM_BYTES_PER_S + gi * gj * nk * GRID_STEP_S
                    cost += 2 * m * n * k / MXU_FLOPS_PER_S * (1 + MXU_TILE / tm)
                    if nk > 1:
                        cost += m * n * 8 * nk / VMEM_ACC_BYTES_PER_S
                    if best is None or cost < best[0]:
                        best = (cost, tm, tn, tk, j_inner)
    assert best is not None, (m, n, k)
    return best[1:]


def _mm(a, b, *, m, n, k, name, ta=False, tb=False, a_at=(None, 0, 0), b_at=(None, 0, 0),
        out_dtype=F32, add=None, epi_p=None, epi_at=(None, 0, 0), out=None, out_at=(None, 0, 0),
        out_full=None, a_pro=None, after=()):
    s_mn =jnp.dtype(out.dtype if out is not None else out_dtype).itemsize
    s_mn += add.dtype.itemsize if add is not None else 0
    s_mn += epi_p.dtype.itemsize if epi_p is not None else 0
    tm, tn, tk, j_inner = _mm_tiles(m, n, k, a.dtype.itemsize, b.dtype.itemsize, s_mn, a_pro is not None,
                                    ((a_at[1], a_at[2], ta), (b_at[1], b_at[2], tb), (out_at[1], out_at[2])))
    nk = k // tk

    def spec(at, tr, tc, rsel, csel):
        lead, r0, c0 = at
        assert r0 % tr == 0 and c0 % tc == 0, (name, at, tr, tc)
        rb, cb = r0 // tr, c0 // tc
        if lead is None:
            return pl.BlockSpec((tr, tc), lambda g0, g1, kk: (rb + rsel(g0, g1, kk), cb + csel(g0, g1, kk)))
        return pl.BlockSpec((None, tr, tc), lambda g0, g1, kk: (lead, rb + rsel(g0, g1, kk), cb + csel(g0, g1, kk)))

    gi = (lambda g0, g1, kk: g0) if j_inner else (lambda g0, g1, kk: g1)
    gj = (lambda g0, g1, kk: g1) if j_inner else (lambda g0, g1, kk: g0)
    gk = lambda g0, g1, kk: kk
    a_spec = spec(a_at, tk, tm, gk, gi) if ta else spec(a_at, tm, tk, gi, gk)
    b_spec = spec(b_at, tn, tk, gj, gk) if tb else spec(b_at, tk, tn, gk, gj)
    dims = ((0,), (0,)) if ta else (((1,), (1,)) if tb else ((1,), (0,)))
    assert not (ta and tb)

    operands, in_specs = [a, b], [a_spec, b_spec]
    if add is not None:
        operands.append(add)
        in_specs.append(spec((None, 0, 0), tm, tn, gi, gj))
    if epi_p is not None:
        operands.append(epi_p)
        in_specs.append(spec(epi_at, tm, tn, gi, gj))
    aliases = {}
    if out is not None:
        aliases = {len(operands): 0}
        operands.append(out)
        in_specs.append(pl.BlockSpec(memory_space=pl.ANY))
        out_struct = jax.ShapeDtypeStruct(out.shape, out.dtype)
        out_dtype = out.dtype
    else:
        out_struct = jax.ShapeDtypeStruct(out_full if out_full is not None else (m, n), out_dtype)
    has_add, has_epi = add is not None, epi_p is not None
    n_skip = (1 if out is not None else 0) + len(after)
    operands += list(after)
    in_specs += [pl.BlockSpec(memory_space=pl.ANY)] * len(after)

    def body(*refs):
        a_ref, b_ref = refs[0], refs[1]
        pos = 2
        add_ref = epi_ref = None
        if has_add:
            add_ref = refs[pos]
            pos += 1
        if has_epi:
            epi_ref = refs[pos]
            pos += 1
        pos += n_skip
        o_ref = refs[pos]

        def finish(r):
            if has_add:
                r = r + add_ref[...].astype(F32)
            if has_epi:
                r = r * (2.0 * jnp.maximum(epi_ref[...].astype(F32), 0.0))
            o_ref[...] = r.astype(o_ref.dtype)

        av = a_ref[...]
        if a_pro == "relu2":
            av = jnp.square(jnp.maximum(av.astype(F32), 0.0))
        part = _dot(av, b_ref[...], dims)
        if nk == 1:
            finish(part)
        else:
            acc_ref = refs[pos + 1]
            kk = pl.program_id(2)

            @pl.when(kk == 0)
            def _():
                acc_ref[...] = part

            @pl.when(kk > 0)
            def _():
                acc_ref[...] += part

            @pl.when(kk == nk - 1)
            def _():
                finish(acc_ref[...])

    grid = (m // tm, n // tn, nk) if j_inner else (n // tn, m // tm, nk)
    return pl.pallas_call(
        body, name=name, grid=grid, in_specs=in_specs,
        out_specs=spec(out_at, tm, tn, gi, gj), out_shape=out_struct,
        scratch_shapes=[pltpu.VMEM((tm, tn), F32)] if nk > 1 else [], input_output_aliases=aliases,
        compiler_params=_cp(VMEM_BIG))(*operands)


def _rms_fwd(x, g, name, tm=1024):
    s, d = x.shape
    tm = min(tm, s)

    def body(x_ref, g_ref, o_ref):
        xv = x_ref[...]
        r = lax.rsqrt(jnp.mean(xv * xv, axis=-1, keepdims=True) + EPS)
        o_ref[...] = (xv * r * g_ref[...]).astype(o_ref.dtype)

    return pl.pallas_call(
        body, name=name, grid=(s // tm,),
        in_specs=[pl.BlockSpec((tm, d), lambda i: (i, 0)), pl.BlockSpec((1, d), lambda i: (0, 0))],
        out_specs=pl.BlockSpec((tm, d), lambda i: (i, 0)),
        out_shape=jax.ShapeDtypeStruct((s, d), _ACT), compiler_params=_cp(VMEM_BIG))(x, g)


def _rms_bwd(x, g, dy, dres, name, tm=512):
    s, d = x.shape
    tm = min(tm, s)
    has_res = dres is not None

    def body(*refs):
        if has_res:
            x_ref, g_ref, dy_ref, dres_ref, dx_ref, dxa_ref, dg_ref = refs
        else:
            x_ref, g_ref, dy_ref, dx_ref, dxa_ref, dg_ref = refs

        @pl.when(pl.program_id(0) == 0)
        def _():
            dg_ref[...] = jnp.zeros_like(dg_ref)

        xv = x_ref[...]
        dyv = dy_ref[...].astype(F32)
        r = lax.rsqrt(jnp.mean(xv * xv, axis=-1, keepdims=True) + EPS)
        xh = xv * r
        dyg = dyv * g_ref[...]
        dx = r * (dyg - xh * jnp.mean(dyg * xh, axis=-1, keepdims=True))
        if has_res:
            dx = dx + dres_ref[...]
        dx_ref[...] = dx
        dxa_ref[...] = dx.astype(dxa_ref.dtype)
        dg_ref[...] += jnp.sum(dyv * xh, axis=0, keepdims=True)

    row = pl.BlockSpec((tm, d), lambda i: (i, 0))
    vec = pl.BlockSpec((1, d), lambda i: (0, 0))
    in_specs = [row, vec, row] + ([row] if has_res else [])
    operands = [x, g, dy] + ([dres] if has_res else [])
    return pl.pallas_call(
        body, name=name, grid=(s // tm,), in_specs=in_specs, out_specs=[row, row, vec],
        out_shape=[jax.ShapeDtypeStruct((s, d), F32), jax.ShapeDtypeStruct((s, d), _ACT),
                   jax.ShapeDtypeStruct((1, d), F32)], compiler_params=_cp(VMEM_BIG))(*operands)


def _loss_head(h, g, target, name, tm=512):
    s, d = h.shape
    tm = min(tm, s)

    def body(h_ref, g_ref, t_ref, loss_ref, dh_ref, dha_ref, dg_ref):
        @pl.when(pl.program_id(0) == 0)
        def _():
            dg_ref[...] = jnp.zeros_like(dg_ref)
            loss_ref[...] = jnp.zeros_like(loss_ref)

        xv = h_ref[...]
        r = lax.rsqrt(jnp.mean(xv * xv, axis=-1, keepdims=True) + EPS)
        xh = xv * r
        err = xh * g_ref[...] - t_ref[...]
        loss_ref[...] += jnp.full(loss_ref.shape, 0.5 * jnp.sum(jnp.mean(err * err, axis=-1, keepdims=True)), F32)
        dyv = err * (1.0 / d)
        dyg = dyv * g_ref[...]
        dh = r * (dyg - xh * jnp.mean(dyg * xh, axis=-1, keepdims=True))
        dh_ref[...] = dh
        dha_ref[...] = dh.astype(dha_ref.dtype)
        dg_ref[...] += jnp.sum(dyv * xh, axis=0, keepdims=True)

    row = pl.BlockSpec((tm, d), lambda i: (i, 0))
    vec = pl.BlockSpec((1, d), lambda i: (0, 0))
    return pl.pallas_call(
        body, name=name, grid=(s // tm,), in_specs=[row, vec, row],
        out_specs=[pl.BlockSpec((1, 128), lambda i: (0, 0)), row, row, vec],
        out_shape=[jax.ShapeDtypeStruct((1, 128), F32), jax.ShapeDtypeStruct((s, d), F32),
                   jax.ShapeDtypeStruct((s, d), _ACT), jax.ShapeDtypeStruct((1, d), F32)],
        compiler_params=_cp(VMEM_BIG))(h, g, target)


def _gmlp_parts(pu, pv, lng, lnb):
    u = _gelu(pu)
    v = _gelu(pv)
    mu = jnp.mean(v, axis=-1, keepdims=True)
    vc = v - mu
    rstd = lax.rsqrt(jnp.mean(vc * vc, axis=-1, keepdims=True) + EPS)
    xhat = vc * rstd
    vn = xhat * lng + lnb
    return u, xhat, rstd, vn


def _gmlp_fwd(proj, lng, lnb, ws, bs3, name):
    s = proj.shape[0]

    def body(pu_ref, pv_ref, lng_ref, lnb_ref, ws_ref, bs_ref, o_ref):
        u, _, _, vn = _gmlp_parts(pu_ref[...], pv_ref[...], lng_ref[...], lnb_ref[...])
        causal = _iota((CHUNK, CHUNK), 0) >= _iota((CHUNK, CHUNK), 1)
        for g in range(A_GROUPS):
            sl = slice(g * A_GW, (g + 1) * A_GW)
            w = jnp.where(causal, ws_ref[g], 0.0)
            sv = _dot(w, vn[:, sl]) + bs_ref[g]
            o_ref[:, sl] = (u[:, sl] * sv).astype(o_ref.dtype)

    full = lambda shape: pl.BlockSpec(shape, lambda c: (0,) * len(shape))
    return pl.pallas_call(
        body, name=name, grid=(s // CHUNK,),
        in_specs=[pl.BlockSpec((CHUNK, D_INNER), lambda c: (c, 0)), pl.BlockSpec((CHUNK, D_INNER), lambda c: (c, 1)),
                  full((1, D_INNER)), full((1, D_INNER)), full((A_GROUPS, CHUNK, CHUNK)), full((A_GROUPS, CHUNK, 1))],
        out_specs=pl.BlockSpec((CHUNK, D_INNER), lambda c: (c, 0)),
        out_shape=jax.ShapeDtypeStruct((s, D_INNER + X_WIDTH), _ACT), compiler_params=_cp(VMEM_BIG))(proj, proj, lng, lnb, ws, bs3)


def _gmlp_bwd(proj, dcat, lng, lnb, ws, bs3, name):
    s = proj.shape[0]

    def body(pu_ref, pv_ref, dm_ref, lng_ref, lnb_ref, ws_ref, bs_ref, dp_ref, dws_ref, dbs_ref, dlng_ref, dlnb_ref, dvn_ref):
        @pl.when(pl.program_id(0) == 0)
        def _():
            dws_ref[...] = jnp.zeros_like(dws_ref)
            dbs_ref[...] = jnp.zeros_like(dbs_ref)
            dlng_ref[...] = jnp.zeros_like(dlng_ref)
            dlnb_ref[...] = jnp.zeros_like(dlnb_ref)

        pu, pv = pu_ref[...], pv_ref[...]
        lng = lng_ref[...]
        u, xhat, rstd, vn = _gmlp_parts(pu, pv, lng, lnb_ref[...])
        dm = dm_ref[...].astype(F32)
        causal = _iota((CHUNK, CHUNK), 0) >= _iota((CHUNK, CHUNK), 1)
        for g in range(A_GROUPS):
            sl = slice(g * A_GW, (g + 1) * A_GW)
            w = jnp.where(causal, ws_ref[g], 0.0)
            sv = _dot(w, vn[:, sl]) + bs_ref[g]
            dsv = dm[:, sl] * u[:, sl]
            dp_ref[:, sl] = (dm[:, sl] * sv * _gelu_grad(pu[:, sl])).astype(dp_ref.dtype)
            dvn_ref[:, sl] = _dot_tn(w, dsv)
            dws_ref[g] += jnp.where(causal, _dot_nt(dsv, vn[:, sl]), 0.0)
            dbs_ref[g] += jnp.sum(dsv, axis=-1, keepdims=True)
        dvn = dvn_ref[...]
        dlng_ref[...] += jnp.sum(dvn * xhat, axis=0, keepdims=True)
        dlnb_ref[...] += jnp.sum(dvn, axis=0, keepdims=True)
        dxh = dvn * lng
        dv = rstd * (dxh - jnp.mean(dxh, axis=-1, keepdims=True) - xhat * jnp.mean(dxh * xhat, axis=-1, keepdims=True))
        dp_ref[:, D_INNER:] = (dv * _gelu_grad(pv)).astype(dp_ref.dtype)

    full = lambda shape: pl.BlockSpec(shape, lambda c: (0,) * len(shape))
    return pl.pallas_call(
        body, name=name, grid=(s // CHUNK,),
        in_specs=[pl.BlockSpec((CHUNK, D_INNER), lambda c: (c, 0)), pl.BlockSpec((CHUNK, D_INNER), lambda c: (c, 1)),
                  pl.BlockSpec((CHUNK, D_INNER), lambda c: (c, 0)),
                  full((1, D_INNER)), full((1, D_INNER)), full((A_GROUPS, CHUNK, CHUNK)), full((A_GROUPS, CHUNK, 1))],
        out_specs=[pl.BlockSpec((CHUNK, 2 * D_INNER), lambda c: (c, 0)), full((A_GROUPS, CHUNK, CHUNK)),
                   full((A_GROUPS, CHUNK, 1)), full((1, D_INNER)), full((1, D_INNER))],
        out_shape=[jax.ShapeDtypeStruct((s, 2 * D_INNER + X_WIDTH), _ACT), jax.ShapeDtypeStruct((A_GROUPS, CHUNK, CHUNK), F32),
                   jax.ShapeDtypeStruct((A_GROUPS, CHUNK, 1), F32), jax.ShapeDtypeStruct((1, D_INNER), F32),
                   jax.ShapeDtypeStruct((1, D_INNER), F32)],
        scratch_shapes=[pltpu.VMEM((CHUNK, D_INNER), F32)],
        compiler_params=_cp(VMEM_BIG))(proj, proj, dcat, lng, lnb, ws, bs3)


_X_SCALE = 1.0 / math.sqrt(X_HD)


def _attn_fwd(proj, qblk, kv, cat, name, tm=512):
    s = proj.shape[0]
    tm = min(tm, s)

    def body(q_ref, kv_ref, cat_ref, o_ref):
        for h in range(X_HEADS):
            sl = slice(h * X_HD, (h + 1) * X_HD)
            k = kv_ref[:, sl]
            v = kv_ref[:, X_WIDTH + h * X_HD:X_WIDTH + (h + 1) * X_HD]
            sc = _dot_nt(q_ref[:, sl], k) * _X_SCALE
            e = jnp.exp(sc - jnp.max(sc, axis=-1, keepdims=True))
            p = e / jnp.sum(e, axis=-1, keepdims=True)
            o_ref[:, sl] = _dot(p, v).astype(o_ref.dtype)

    return pl.pallas_call(
        body, name=name, grid=(s // tm,),
        in_specs=[pl.BlockSpec((tm, X_WIDTH), lambda i: (i, qblk)), pl.BlockSpec((N_MEM, 2 * X_WIDTH), lambda i: (0, 0)),
                  pl.BlockSpec(memory_space=pl.ANY)],
        out_specs=pl.BlockSpec((tm, X_WIDTH), lambda i: (i, D_INNER // X_WIDTH)),
        out_shape=jax.ShapeDtypeStruct(cat.shape, cat.dtype), input_output_aliases={2: 0},
        compiler_params=_cp(VMEM_BIG))(proj, kv, cat)


def _attn_bwd(proj, qblk, kv, dcat, dproj, name, tm=512):
    s = proj.shape[0]
    tm = min(tm, s)

    def body(q_ref, kv_ref, do_ref, dproj_ref, dq_ref, dkv_ref):
        @pl.when(pl.program_id(0) == 0)
        def _():
            dkv_ref[...] = jnp.zeros_like(dkv_ref)

        for h in range(X_HEADS):
            sl = slice(h * X_HD, (h + 1) * X_HD)
            slv = slice(X_WIDTH + h * X_HD, X_WIDTH + (h + 1) * X_HD)
            q = q_ref[:, sl]
            k = kv_ref[:, sl]
            v = kv_ref[:, slv]
            do = do_ref[:, sl].astype(F32)
            sc = _dot_nt(q, k) * _X_SCALE
            e = jnp.exp(sc - jnp.max(sc, axis=-1, keepdims=True))
            p = e / jnp.sum(e, axis=-1, keepdims=True)
            dp = _dot_nt(do, v)
            ds = p * (dp - jnp.sum(dp * p, axis=-1, keepdims=True)) * _X_SCALE
            dq_ref[:, sl] = _dot(ds, k).astype(dq_ref.dtype)
            dkv_ref[:, sl] += _dot_tn(ds, q)
            dkv_ref[:, slv] += _dot_tn(p, do)

    return pl.pallas_call(
        body, name=name, grid=(s // tm,),
        in_specs=[pl.BlockSpec((tm, X_WIDTH), lambda i: (i, qblk)), pl.BlockSpec((N_MEM, 2 * X_WIDTH), lambda i: (0, 0)),
                  pl.BlockSpec((tm, X_WIDTH), lambda i: (i, 2)), pl.BlockSpec(memory_space=pl.ANY)],
        out_specs=[pl.BlockSpec((tm, X_WIDTH), lambda i: (i, qblk)), pl.BlockSpec((N_MEM, 2 * X_WIDTH), lambda i: (0, 0))],
        out_shape=[jax.ShapeDtypeStruct(dproj.shape, dproj.dtype), jax.ShapeDtypeStruct((N_MEM, 2 * X_WIDTH), F32)],
        input_output_aliases={3: 0}, compiler_params=_cp(VMEM_BIG))(proj, kv, dcat, dproj)


CONV_TC = 256
_XBC_BLK0 = D_INNER // CONV_TC


CONV_RB = 64
SUBLANES = 8


def _rows_before(cur, prev_last, j):
    rolled = pltpu.roll(cur, j, 0)
    head = jnp.where(_iota((SUBLANES, cur.shape[1]), 0) < j, pltpu.roll(prev_last, j, 0), rolled[:SUBLANES])
    return jnp.concatenate([head, rolled[SUBLANES:]], axis=0)


def _rows_after(cur, next_first, j):
    n = cur.shape[0]
    rolled = pltpu.roll(cur, n - j, 0)
    tail = jnp.where(_iota((SUBLANES, cur.shape[1]), 0) >= SUBLANES - j, pltpu.roll(next_first, SUBLANES - j, 0),
                     rolled[n - SUBLANES:])
    return jnp.concatenate([rolled[:n - SUBLANES], tail], axis=0)


def _conv_pre(x_ref, w_ref, b_ref, r0, prev_last):
    cur = x_ref[pl.ds(r0, CONV_RB), :]
    shifts = [_rows_before(cur, prev_last, j) for j in range(1, CONV_K)]
    pre = b_ref[...] + w_ref[CONV_K - 1:CONV_K, :] * cur
    for j in range(1, CONV_K):
        pre = pre + w_ref[CONV_K - 1 - j:CONV_K - j, :] * shifts[j - 1]
    return pre, cur, shifts


def _conv_fwd(proj, w, b, name):
    s = proj.shape[0]

    def body(x_ref, w_ref, b_ref, o_ref):
        xv = x_ref[...]
        rows = _iota(xv.shape, 0)
        pre = b_ref[...] + w_ref[CONV_K - 1:CONV_K, :] * xv
        for j in range(1, CONV_K):
            pre = pre + w_ref[CONV_K - 1 - j:CONV_K - j, :] * jnp.where(rows >= j, pltpu.roll(xv, j, 0), 0.0)
        o_ref[...] = pre * _sigmoid(pre)

    return pl.pallas_call(
        body, name=name, grid=(CONV_DIM // CONV_TC,),
        in_specs=[pl.BlockSpec((s, CONV_TC), lambda j: (0, _XBC_BLK0 + j)), pl.BlockSpec((CONV_K, CONV_TC), lambda j: (0, j)),
                  pl.BlockSpec((1, CONV_TC), lambda j: (0, j))],
        out_specs=pl.BlockSpec((s, CONV_TC), lambda j: (0, j)),
        out_shape=jax.ShapeDtypeStruct((s, CONV_DIM), F32), compiler_params=_cp(VMEM_BIG))(proj, w, b)


def _conv_bwd(proj, w, b, dxbc, dproj, name):
    s = proj.shape[0]

    nb = s // CONV_RB

    def body(x_ref, w_ref, b_ref, d_ref, dproj_ref, dx_ref, dw_ref, db_ref, dpre_ref):
        def fold(v):
            out = v[:SUBLANES]
            for t in range(1, CONV_RB // SUBLANES):
                out = out + v[t * SUBLANES:(t + 1) * SUBLANES]
            return out

        def first(i, carry):
            prev_last, acc = carry
            r0 = pl.multiple_of(i * CONV_RB, CONV_RB)
            pre, cur, shifts = _conv_pre(x_ref, w_ref, b_ref, r0, prev_last)
            sig = _sigmoid(pre)
            dpre = d_ref[pl.ds(r0, CONV_RB), :] * (sig * (1.0 + pre * (1.0 - sig)))
            dpre_ref[pl.ds(r0, CONV_RB), :] = dpre
            taps = [cur] + shifts
            acc = tuple(a + fold(dpre * t) for a, t in zip(acc[:CONV_K], taps)) + (acc[CONV_K] + fold(dpre),)
            return cur[CONV_RB - SUBLANES:], acc

        zero8 = jnp.zeros((SUBLANES, CONV_TC), F32)
        _, acc = lax.fori_loop(0, nb, first, (zero8, (zero8,) * (CONV_K + 1)))
        for j in range(CONV_K):
            dw_ref[CONV_K - 1 - j:CONV_K - j, :] = jnp.sum(acc[j], axis=0, keepdims=True)
        db_ref[...] = jnp.sum(acc[CONV_K], axis=0, keepdims=True)

        def second(i, next_first):
            r0 = pl.multiple_of((nb - 1 - i) * CONV_RB, CONV_RB)
            cur = dpre_ref[pl.ds(r0, CONV_RB), :]
            dx = w_ref[CONV_K - 1:CONV_K, :] * cur
            for j in range(1, CONV_K):
                dx = dx + w_ref[CONV_K - 1 - j:CONV_K - j, :] * _rows_after(cur, next_first, j)
            dx_ref[pl.ds(r0, CONV_RB), :] = dx.astype(dx_ref.dtype)
            return cur[:SUBLANES]

        lax.fori_loop(0, nb, second, zero8)

    return pl.pallas_call(
        body, name=name, grid=(CONV_DIM // CONV_TC,),
        in_specs=[pl.BlockSpec((s, CONV_TC), lambda j: (0, _XBC_BLK0 + j)), pl.BlockSpec((CONV_K, CONV_TC), lambda j: (0, j)),
                  pl.BlockSpec((1, CONV_TC), lambda j: (0, j)), pl.BlockSpec((s, CONV_TC), lambda j: (0, j)),
                  pl.BlockSpec(memory_space=pl.ANY)],
        out_specs=[pl.BlockSpec((s, CONV_TC), lambda j: (0, _XBC_BLK0 + j)), pl.BlockSpec((CONV_K, CONV_TC), lambda j: (0, j)),
                   pl.BlockSpec((1, CONV_TC), lambda j: (0, j))],
        out_shape=[jax.ShapeDtypeStruct(dproj.shape, dproj.dtype), jax.ShapeDtypeStruct((CONV_K, CONV_DIM), F32),
                   jax.ShapeDtypeStruct((1, CONV_DIM), F32)], input_output_aliases={4: 0},
        scratch_shapes=[pltpu.VMEM((s, CONV_TC), F32)],
        compiler_params=_cp(VMEM_BIG))(proj, w, b, dxbc, dproj)


def _ssd_common(dtc_ref, br_ref, ar_ref, csb_ref, cst_ref, csf_ref):
    a_row = -jnp.exp(ar_ref[...])
    dt_c = _softplus(dtc_ref[...] + br_ref[...])
    tril = _iota((CHUNK, CHUNK), 0) >= _iota((CHUNK, CHUNK), 1)
    cs = _sel_dot(tril, dt_c * a_row)
    cst_ref[...] = cs.T
    e64 = (jnp.right_shift(_iota((HPAD, D_INNER), 1), 6) == _iota((HPAD, D_INNER), 0)).astype(jnp.bfloat16)
    e128 = jnp.right_shift(_iota((HPAD, SSM_HEADS * CHUNK), 1), 7) == _iota((HPAD, SSM_HEADS * CHUNK), 0)
    csb_ref[...] = _dot_sel(cs, e128)
    dt_full = _dot_sel(dt_c, e64)
    csf_ref[...] = _dot_sel(cs, e64)
    cs_full = csf_ref[...]
    cs_last = csf_ref[CHUNK - 1:CHUNK, :]
    e_full = jnp.exp(cs_full)
    f_full = jnp.exp(cs_last - cs_full)
    gamma = jnp.exp(cs_last)
    return a_row, dt_c, cs, dt_full, e_full, f_full, gamma, e64


def _ssd_lambda(csb_ref, cst_ref, h, causal):
    diff = csb_ref[:, h * CHUNK:(h + 1) * CHUNK] - cst_ref[h:h + 1, :]
    return jnp.exp(jnp.where(causal, diff, -1e30))


_SSD_VEC_SPECS = lambda: [pl.BlockSpec((1, HPAD), lambda c: (0, 0)), pl.BlockSpec((1, HPAD), lambda c: (0, 0)),
                          pl.BlockSpec((1, D_INNER), lambda c: (0, 0))]


def _ssd_fwd(xbc, dtc, bias_row, alog_row, dfull, name):
    s = xbc.shape[0]
    nc = s // CHUNK

    def body(xbc_ref, dtc_ref, br_ref, ar_ref, df_ref, y_ref, st_ref, ht_ref, csb_ref, cst_ref, csf_ref):
        @pl.when(pl.program_id(0) == 0)
        def _():
            ht_ref[...] = jnp.zeros_like(ht_ref)

        _, _, _, dt_full, e_full, f_full, gamma, _ = _ssd_common(dtc_ref, br_ref, ar_ref, csb_ref, cst_ref, csf_ref)
        x = xbc_ref[:, :D_INNER]
        xdt = x * dt_full
        st_ref[...] = ht_ref[...]
        causal = _iota((CHUNK, CHUNK), 0) >= _iota((CHUNK, CHUNK), 1)
        lo = _iota((CHUNK, CHUNK), 1) < SSM_P
        for g in range(SSM_GROUPS):
            gs = slice(g * SSM_GW, (g + 1) * SSM_GW)
            bg = xbc_ref[:, D_INNER + g * SSM_N:D_INNER + (g + 1) * SSM_N]
            cg = xbc_ref[:, D_INNER + SSM_GROUPS * SSM_N + g * SSM_N:D_INNER + SSM_GROUPS * SSM_N + (g + 1) * SSM_N]
            ht = ht_ref[:, gs]
            cb = _dot_nt(cg, bg)
            yoff = e_full[:, gs] * _dot(cg, ht)
            for jp in range(SSM_GW // CHUNK):
                j = g * (SSM_GW // CHUNK) + jp
                ps = slice(j * CHUNK, (j + 1) * CHUNK)
                x2 = xdt[:, ps]
                y0 = _dot(cb * _ssd_lambda(csb_ref, cst_ref, 2 * j, causal), x2)
                y1 = _dot(cb * _ssd_lambda(csb_ref, cst_ref, 2 * j + 1, causal), x2)
                y_ref[:, ps] = (jnp.where(lo, y0, y1) + yoff[:, jp * CHUNK:(jp + 1) * CHUNK]
                                + x[:, ps] * df_ref[:, ps])
            ht_ref[:, gs] = gamma[:, gs] * ht + _dot_tn(bg, xdt[:, gs] * f_full[:, gs])

    return pl.pallas_call(
        body, name=name, grid=(nc,),
        in_specs=[pl.BlockSpec((CHUNK, CONV_DIM), lambda c: (c, 0)), pl.BlockSpec((CHUNK, HPAD), lambda c: (c, 0))]
                 + _SSD_VEC_SPECS(),
        out_specs=[pl.BlockSpec((CHUNK, D_INNER), lambda c: (c, 0)), pl.BlockSpec((None, SSM_N, D_INNER), lambda c: (c, 0, 0))],
        out_shape=[jax.ShapeDtypeStruct((s, D_INNER), F32), jax.ShapeDtypeStruct((nc, SSM_N, D_INNER), F32)],
        scratch_shapes=[pltpu.VMEM((SSM_N, D_INNER), F32), pltpu.VMEM((CHUNK, SSM_HEADS * CHUNK), F32),
                        pltpu.VMEM((HPAD, CHUNK), F32), pltpu.VMEM((CHUNK, D_INNER), F32)],
        compiler_params=_cp(VMEM_BIG))(xbc, dtc, bias_row, alog_row, dfull)


def _ssd_bwd(xbc, dtc, bias_row, alog_row, dfull, dy, states, name):
    s = xbc.shape[0]
    nc = s // CHUNK
    rev = lambda c: nc - 1 - c

    def body(xbc_ref, dtc_ref, br_ref, ar_ref, df_ref, dy_ref, st_ref,
             dxbc_ref, ddt_ref, dalog_ref, dd_ref, dbias_ref,
             dht_ref, csb_ref, cst_ref, csf_ref, ddf_ref, dxs_ref, dcsf_ref, dcsl_ref):
        step = pl.program_id(0)

        @pl.when(step == 0)
        def _():
            dht_ref[...] = jnp.zeros_like(dht_ref)
            ddf_ref[...] = jnp.zeros_like(ddf_ref)
            dalog_ref[...] = jnp.zeros_like(dalog_ref)
            dbias_ref[...] = jnp.zeros_like(dbias_ref)
            dd_ref[...] = jnp.zeros_like(dd_ref)

        a_row, dt_c, _, dt_full, e_full, f_full, gamma, e64 = _ssd_common(dtc_ref, br_ref, ar_ref, csb_ref, cst_ref, csf_ref)
        x = xbc_ref[:, :D_INNER]
        xdt = x * dt_full
        dy_all = dy_ref[...]
        ddf_ref[...] += jnp.broadcast_to(jnp.sum(dy_all * x, axis=0, keepdims=True), ddf_ref.shape)
        causal = _iota((CHUNK, CHUNK), 0) >= _iota((CHUNK, CHUNK), 1)
        lo = _iota((CHUNK, CHUNK), 1) < SSM_P
        head_lane = _iota((CHUNK, HPAD), 1)
        head_row = _iota((HPAD, CHUNK), 0)
        dcs_heads = jnp.zeros((CHUNK, HPAD), F32)
        dcs_cols = jnp.zeros((HPAD, CHUNK), F32)
        for g in range(SSM_GROUPS):
            gs = slice(g * SSM_GW, (g + 1) * SSM_GW)
            b0 = D_INNER + g * SSM_N
            c0 = D_INNER + SSM_GROUPS * SSM_N + g * SSM_N
            bg = xbc_ref[:, b0:b0 + SSM_N]
            cg = xbc_ref[:, c0:c0 + SSM_N]
            ht = st_ref[:, gs]
            dht = dht_ref[:, gs]
            dyg = dy_all[:, gs]
            eg, fg, gg = e_full[:, gs], f_full[:, gs], gamma[:, gs]
            z = _dot(cg, ht)
            dz = dyg * eg
            dcg = _dot_nt(dz, ht)
            dht_new = _dot_tn(cg, dz) + gg * dht
            xf = xdt[:, gs] * fg
            dxf = _dot(bg, dht)
            dbg = _dot_nt(xf, dht)
            dff = dxf * xf
            dcsf_ref[:, gs] = dyg * eg * z - dff
            dcsl_ref[:, gs] = jnp.broadcast_to(
                jnp.sum(dff, axis=0, keepdims=True) + jnp.sum(dht * ht, axis=0, keepdims=True) * gg, (8, SSM_GW))
            cb = _dot_nt(cg, bg)
            dcb = jnp.zeros((CHUNK, CHUNK), F32)
            for jp in range(SSM_GW // CHUNK):
                j = g * (SSM_GW // CHUNK) + jp
                ps = slice(j * CHUNK, (j + 1) * CHUNK)
                x2 = xdt[:, ps]
                dy2 = dy_all[:, ps]
                dxh = []
                for hh in range(2):
                    h = 2 * j + hh
                    lam = _ssd_lambda(csb_ref, cst_ref, h, causal)
                    mh = cb * lam
                    dyh = jnp.where(lo, dy2, 0.0) if hh == 0 else jnp.where(lo, 0.0, dy2)
                    dm = _dot_nt(dyh, x2)
                    dcb = dcb + dm * lam
                    gm = dm * mh
                    dcs_heads = dcs_heads + jnp.where(head_lane == h, jnp.sum(gm, axis=1, keepdims=True), 0.0)
                    dcs_cols = dcs_cols + jnp.where(head_row == h, jnp.sum(gm, axis=0, keepdims=True), 0.0)
                    dxh.append(_dot_tn(mh, dy2))
                dxs_ref[:, ps] = jnp.where(lo, dxh[0], dxh[1]) + dxf[:, jp * CHUNK:(jp + 1) * CHUNK] * fg[:, jp * CHUNK:(jp + 1) * CHUNK]
            dxbc_ref[:, b0:b0 + SSM_N] = (dbg + _dot_tn(dcb, cg)).astype(dxbc_ref.dtype)
            dxbc_ref[:, c0:c0 + SSM_N] = (dcg + _dot(dcb, bg)).astype(dxbc_ref.dtype)
            dht_ref[:, gs] = dht_new
        dxs = dxs_ref[...]
        dcs_heads = dcs_heads - dcs_cols.T + _dot_sel(dcsf_ref[...], e64, ((1,), (1,)))
        dcs_last = _dot_sel(dcsl_ref[...], e64, ((1,), (1,)))
        dcs_heads = dcs_heads + jnp.where(_iota((CHUNK, HPAD), 0) == CHUNK - 1, dcs_last[0:1, :], 0.0)
        triu = _iota((CHUNK, CHUNK), 0) <= _iota((CHUNK, CHUNK), 1)
        dda = _sel_dot(triu, dcs_heads)
        ddt = dda * a_row + _dot_sel(dxs * x, e64, ((1,), (1,)))
        dxbc_ref[:, :D_INNER] = (dxs * dt_full + dy_all * df_ref[...]).astype(dxbc_ref.dtype)
        dalog_ref[...] += jnp.sum(dda * dt_c, axis=0, keepdims=True) * a_row
        ddt_raw = ddt * _sigmoid(dtc_ref[...] + br_ref[...])
        ddt_ref[...] = ddt_raw.astype(ddt_ref.dtype)
        dbias_ref[...] += jnp.sum(ddt_raw, axis=0, keepdims=True)

        @pl.when(step == nc - 1)
        def _():
            dd_ref[...] = _dot_sel(ddf_ref[...], e64, ((1,), (1,)))[0:1, :]

    vec = pl.BlockSpec((1, HPAD), lambda c: (0, 0))
    return pl.pallas_call(
        body, name=name, grid=(nc,),
        in_specs=[pl.BlockSpec((CHUNK, CONV_DIM), lambda c: (rev(c), 0)), pl.BlockSpec((CHUNK, HPAD), lambda c: (rev(c), 0))]
                 + _SSD_VEC_SPECS()
                 + [pl.BlockSpec((CHUNK, D_INNER), lambda c: (rev(c), 0)),
                    pl.BlockSpec((None, SSM_N, D_INNER), lambda c: (rev(c), 0, 0))],
        out_specs=[pl.BlockSpec((CHUNK, CONV_DIM), lambda c: (rev(c), 0)), pl.BlockSpec((CHUNK, HPAD), lambda c: (rev(c), 0)),
                   vec, vec, vec],
        out_shape=[jax.ShapeDtypeStruct((s, CONV_DIM), F32), jax.ShapeDtypeStruct((s, HPAD), _ACT),
                   jax.ShapeDtypeStruct((1, HPAD), F32), jax.ShapeDtypeStruct((1, HPAD), F32),
                   jax.ShapeDtypeStruct((1, HPAD), F32)],
        scratch_shapes=[pltpu.VMEM((SSM_N, D_INNER), F32), pltpu.VMEM((CHUNK, SSM_HEADS * CHUNK), F32),
                        pltpu.VMEM((HPAD, CHUNK), F32), pltpu.VMEM((CHUNK, D_INNER), F32),
                        pltpu.VMEM((8, D_INNER), F32), pltpu.VMEM((CHUNK, D_INNER), F32),
                        pltpu.VMEM((CHUNK, D_INNER), F32), pltpu.VMEM((8, D_INNER), F32)],
        compiler_params=_cp(VMEM_BIG))(xbc, dtc, bias_row, alog_row, dfull, dy, states)


def _gate_fwd(y, proj, gn, name, tm=512):
    s = y.shape[0]
    tm = min(tm, s)

    def body(y_ref, z_ref, gn_ref, o_ref):
        for g in range(SSM_GROUPS):
            gs = slice(g * SSM_GW, (g + 1) * SSM_GW)
            z = z_ref[:, gs]
            t = y_ref[:, gs] * (z * _sigmoid(z))
            r = lax.rsqrt(jnp.mean(t * t, axis=-1, keepdims=True) + EPS)
            o_ref[:, gs] = (t * r * gn_ref[:, gs]).astype(o_ref.dtype)

    row = pl.BlockSpec((tm, D_INNER), lambda i: (i, 0))
    return pl.pallas_call(
        body, name=name, grid=(s // tm,), in_specs=[row, row, pl.BlockSpec((1, D_INNER), lambda i: (0, 0))],
        out_specs=row, out_shape=jax.ShapeDtypeStruct((s, D_INNER + X_WIDTH), _ACT),
        compiler_params=_cp(VMEM_BIG))(y, proj, gn)


def _gate_bwd(y, proj, gn, dcat, name, tm=512):
    s = y.shape[0]
    tm = min(tm, s)

    def body(y_ref, z_ref, gn_ref, dm_ref, dy_ref, dz_ref, dgn_ref):
        @pl.when(pl.program_id(0) == 0)
        def _():
            dgn_ref[...] = jnp.zeros_like(dgn_ref)

        for g in range(SSM_GROUPS):
            gs = slice(g * SSM_GW, (g + 1) * SSM_GW)
            z = z_ref[:, gs]
            yv = y_ref[:, gs]
            sig = _sigmoid(z)
            sz = z * sig
            t = yv * sz
            r = lax.rsqrt(jnp.mean(t * t, axis=-1, keepdims=True) + EPS)
            th = t * r
            dm = dm_ref[:, gs].astype(F32)
            dmg = dm * gn_ref[:, gs]
            dt_ = r * (dmg - th * jnp.mean(dmg * th, axis=-1, keepdims=True))
            dgn_ref[:, gs] += jnp.sum(dm * th, axis=0, keepdims=True)
            dy_ref[:, gs] = dt_ * sz
            dz_ref[:, gs] = (dt_ * yv * (sig * (1.0 + z * (1.0 - sig)))).astype(dz_ref.dtype)

    row = pl.BlockSpec((tm, D_INNER), lambda i: (i, 0))
    vec = pl.BlockSpec((1, D_INNER), lambda i: (0, 0))
    return pl.pallas_call(
        body, name=name, grid=(s // tm,), in_specs=[row, row, vec, row], out_specs=[row, row, vec],
        out_shape=[jax.ShapeDtypeStruct((s, D_INNER), F32), jax.ShapeDtypeStruct((s, 6 * D_MODEL), _ACT),
                   jax.ShapeDtypeStruct((1, D_INNER), F32)], compiler_params=_cp(VMEM_BIG))(y, proj, gn, dcat)


def _block_of(kind, width):
    if kind == "col":
        return lambda ref, j: ref.at[:, :, pl.ds(pl.multiple_of(j * width, 128), width)]
    if kind == "row":
        return lambda ref, j: ref.at[:, pl.ds(pl.multiple_of(j * width, 8), width), :]
    return lambda ref, j: ref.at[j]


def _coords():
    return lax.axis_index("x"), lax.axis_index("y"), lax.axis_index("c")


def _rel_chip(x, y, k):
    return (1 - x if k & 1 else x), (1 - y if k & 2 else y)


def _all_gather_body(ins, outs, send_sems, recv_sems, local_sems, blocks):
    n = len(ins)
    x, y, c = _coords()
    sibling = (x, y, 1 - c)
    via = (x + (1 - c) * (1 - 2 * x), y + c * (1 - 2 * y))
    onto = (x + c * (1 - 2 * x), y + (1 - c) * (1 - 2 * y))

    def copy(t, k, chip, core, to, src=None):
        dst = blocks[t](outs[t], 4 * chip[0] + 2 * chip[1] + core)
        return pltpu.make_async_remote_copy(
            src_ref=dst if src is None else src, dst_ref=dst, send_sem=send_sems.at[t, k],
            recv_sem=recv_sems.at[t, k], device_id=to, device_id_type=MESH)

    started = []
    for t in range(n):
        mine = pltpu.make_async_copy(ins[t], blocks[t](outs[t], 4 * x + 2 * y + c), local_sems.at[t])
        mine.start()
        started.append(mine)
    sends = []
    for t in range(n):
        for k in range(3):
            px, py = _rel_chip(x, y, k)
            cp = copy(t, k, (x, y), c, (px, py, 1 - c if k == 0 else c), src=ins[t])
            cp.start()
            sends.append(cp)
    for t in range(n):
        for k in (1, 2):
            chip = _rel_chip(x, y, k)
            copy(t, k, chip, c, sibling).wait_recv()
            fwd = copy(t, 3 + k, chip, c, sibling)
            fwd.start()
            sends.append(fwd)
        hop = copy(t, 3, via, c, (*onto, c))
        hop.start()
        sends.append(hop)
    for t in range(n):
        diagonal = _rel_chip(x, y, 3)
        copy(t, 3, diagonal, c, sibling).wait_recv()
        fwd = copy(t, 6, diagonal, c, sibling)
        fwd.start()
        sends.append(fwd)
    for t in range(n):
        copy(t, 0, (x, y), 1 - c, sibling).wait_recv()
        for k in range(1, 4):
            copy(t, 3 + k, _rel_chip(x, y, k), 1 - c, sibling).wait_recv()
    for cp in sends:
        cp.wait_send()
    for mine in started:
        mine.wait()


def _handshake(peers):
    barrier = pltpu.get_barrier_semaphore()
    for peer in peers:
        pl.semaphore_signal(barrier, inc=1, device_id=peer, device_id_type=MESH)
    pl.semaphore_wait(barrier, len(peers))


def _gather_peers():
    x, y, c = _coords()
    return [(x, y, 1 - c)] + [(*_rel_chip(x, y, k), c) for k in (1, 2)]


SEQ_ID_GATHER, SEQ_ID_SIBLING, SEQ_ID_CHIPS = 1, 2, 3


def _sequencer_call(body, peers, operands, out_types, sems, name, collective_id, after=()):
    n_in, n_out, n_after = len(operands), len(out_types), len(after)

    def launch(*refs):
        _handshake(peers())
        body(refs[:n_in], refs[n_in + n_after:n_in + n_after + n_out], *refs[n_in + n_after + n_out:])

    return pl.kernel(
        launch, name=name, out_type=out_types, mesh=plsc.ScalarSubcoreMesh(axis_name="seq", num_cores=1),
        scratch_types=sems, compiler_params=pltpu.CompilerParams(collective_id=collective_id))(*operands, *after)


def _all_gather_seq(shards, layouts, name, after=()):
    n = len(shards)
    blocks = [_block_of(kind, width) for kind, width, _ in layouts]
    return _sequencer_call(
        lambda ins, outs, *sems: _all_gather_body(ins, outs, *sems, blocks), _gather_peers, shards,
        [jax.ShapeDtypeStruct(shape, sh.dtype) for sh, (_, _, shape) in zip(shards, layouts)],
        [pltpu.SemaphoreType.DMA((n, 7)), pltpu.SemaphoreType.DMA((n, 7)), pltpu.SemaphoreType.DMA((n,))],
        name, SEQ_ID_GATHER, after)


def _tie(small, after, name):
    del name
    return lax.optimization_barrier((small, *after))[0]


def _rs_to_sibling(grads, layouts, name, after=()):
    n = len(grads)
    blocks = [_block_of(kind, width) for kind, width, _ in layouts]

    def body(ins, outs, send_sems, recv_sems):
        x, y, c = _coords()
        sibling = (x, y, 1 - c)
        cps = []
        for t in range(n):
            for k in range(4):
                px, py = _rel_chip(x, y, k)
                cp = pltpu.make_async_remote_copy(
                    src_ref=blocks[t](ins[t], 4 * px + 2 * py + (1 - c)), dst_ref=outs[t].at[k],
                    send_sem=send_sems.at[t, k], recv_sem=recv_sems.at[t, k], device_id=sibling, device_id_type=MESH)
                cp.start()
                cps.append(cp)
        for cp in cps:
            cp.wait_recv()
        for cp in cps:
            cp.wait_send()

    def sibling_only():
        x, y, c = _coords()
        return [(x, y, 1 - c)]

    return _sequencer_call(
        body, sibling_only, grads,
        [jax.ShapeDtypeStruct((4,) + shape, g.dtype) for g, (_, _, shape) in zip(grads, layouts)],
        [pltpu.SemaphoreType.DMA((n, 4)), pltpu.SemaphoreType.DMA((n, 4))], name, SEQ_ID_SIBLING, after)


def _rs_chip_sum(grad, recv, layout, xyc, name):
    kind, width, shape = layout
    r, ccols = shape

    def src_index(k, xyc_ref):
        px = jnp.where(k % 2 == 1, 1 - xyc_ref[0], xyc_ref[0])
        py = jnp.where(k // 2 == 1, 1 - xyc_ref[1], xyc_ref[1])
        return 4 * px + 2 * py + xyc_ref[2]

    if kind == "col":
        g_spec = pl.BlockSpec((r, ccols), lambda k, s_: (0, src_index(k, s_)))
    elif kind == "row":
        g_spec = pl.BlockSpec((r, ccols), lambda k, s_: (src_index(k, s_), 0))
    else:
        g_spec = pl.BlockSpec((None, r, ccols), lambda k, s_: (src_index(k, s_), 0, 0))

    def body(xyc_ref, g_ref, r_ref, o_ref):
        o_ref[...] = (g_ref[...].astype(F32) + r_ref[...].astype(F32)).astype(o_ref.dtype)

    slot = pl.BlockSpec((None, r, ccols), lambda k, s_: (k, 0, 0))
    return pl.pallas_call(
        body, name=name,
        grid_spec=pltpu.PrefetchScalarGridSpec(num_scalar_prefetch=1, grid=(4,), in_specs=[g_spec, slot], out_specs=slot),
        out_shape=jax.ShapeDtypeStruct((4, r, ccols), grad.dtype), compiler_params=_cp(VMEM_BIG))(xyc, grad, recv)


def _rs_across_chips(parts, name):
    n = len(parts)

    def body(ins, outs, send_sems, recv_sems):
        x, y, c = _coords()
        cps = []
        for t in range(n):
            for k in range(1, 4):
                px, py = _rel_chip(x, y, k)
                cp = pltpu.make_async_remote_copy(
                    src_ref=ins[t].at[k], dst_ref=outs[t].at[k - 1], send_sem=send_sems.at[t, k - 1],
                    recv_sem=recv_sems.at[t, k - 1], device_id=(px, py, c), device_id_type=MESH)
                cp.start()
                cps.append(cp)
        for cp in cps:
            cp.wait_recv()
        for cp in cps:
            cp.wait_send()

    def other_chips():
        x, y, c = _coords()
        return [(*_rel_chip(x, y, k), c) for k in range(1, 4)]

    return _sequencer_call(
        body, other_chips, parts, [jax.ShapeDtypeStruct((3,) + p.shape[1:], p.dtype) for p in parts],
        [pltpu.SemaphoreType.DMA((n, 3)), pltpu.SemaphoreType.DMA((n, 3))], name, SEQ_ID_CHIPS)


def _adamw_math(w, g, m, v):
    m = ADAM_B1 * m + (1.0 - ADAM_B1) * g
    v = ADAM_B2 * v + (1.0 - ADAM_B2) * jnp.square(g)
    m_hat = m / (1.0 - ADAM_B1 ** ADAM_STEP)
    v_hat = v / (1.0 - ADAM_B2 ** ADAM_STEP)
    delta = -ADAM_LR * (m_hat / (jnp.sqrt(v_hat) + ADAM_EPS) + ADAM_WD * w)
    return delta, m, v


def _row_tile(rows, cap):
    best = None
    for cand in range(8, min(rows, cap) + 1, 8):
        if rows % cand == 0:
            best = cand
    assert best is not None, rows
    return best


def _adamw(w, m, v, parts, name, layer=None, prev=None, tr=256):
    r, ccols = w.shape[-2:]
    npart = len(parts)
    if r % 8 == 0:
        tr, tc = _row_tile(r, tr), ccols
        steps, at = r // tr, (lambda i: (i, 0))
    else:
        tr, tc = r, 256
        assert ccols % tc == 0
        steps, at = ccols // tc, (lambda i: (0, i))

    def spec(lead):
        if lead is None:
            return pl.BlockSpec((tr, tc), at)
        return pl.BlockSpec((None, tr, tc), lambda i: (lead,) + at(i))

    wspec = lambda: spec(layer)
    pspec = spec

    def body(*refs):
        w_ref, m_ref, v_ref = refs[:3]
        p_refs = refs[3:3 + npart]
        outs = refs[len(refs) - 4:]
        g = p_refs[0][...].astype(F32)
        for p_ref in p_refs[1:]:
            g = g + p_ref[...].astype(F32)
        delta, mn, vn = _adamw_math(w_ref[...], g, m_ref[...], v_ref[...])
        outs[0][...] = g
        outs[1][...] = delta
        outs[2][...] = mn
        outs[3][...] = vn

    operands = [w, m, v] + [p for p, _ in parts]
    in_specs = [wspec(), wspec(), wspec()] + [pspec(lead) for _, lead in parts]
    aliases = {}
    if prev is not None:
        for i, p in enumerate(prev):
            aliases[len(operands)] = i
            operands.append(p)
            in_specs.append(pl.BlockSpec(memory_space=pl.ANY))
    return pl.pallas_call(
        body, name=name, grid=(steps,), in_specs=in_specs, out_specs=[wspec()] * 4,
        out_shape=[jax.ShapeDtypeStruct(w.shape, F32)] * 4, input_output_aliases=aliases)(*operands)


def _small_update(gathered, params, loss_all, me, name):
    n = len(gathered)
    shapes = [w.shape for w, _, _ in params]

    def body(me_ref, *refs):
        g_refs, loss_ref = refs[:n], refs[n]
        p_refs = refs[n + 1:n + 1 + 3 * n]
        o_refs = refs[n + 1 + 3 * n:]
        for i in range(n):
            r, c = shapes[i]
            if gathered[i].shape[2] == c:
                parts = [g_refs[i][j] for j in range(N_DEV)]
            else:
                off = pl.multiple_of(me_ref[0] * c, 128)
                parts = [g_refs[i][j, :, pl.ds(off, c)] for j in range(N_DEV)]
            g = functools.reduce(lambda a, b: a + b, parts)
            delta, mn, vn = _adamw_math(p_refs[3 * i][...], g, p_refs[3 * i + 1][...], p_refs[3 * i + 2][...])
            for k, val in enumerate((g, delta, mn, vn)):
                o_refs[4 * i + k][...] = val
        o_refs[4 * n][...] = functools.reduce(lambda a, b: a + b, [loss_ref[j] for j in range(N_DEV)])

    vmem = pl.BlockSpec(memory_space=pltpu.VMEM)
    flat_params = [a for p in params for a in p]
    outs = pl.pallas_call(
        body, name=name, in_specs=[pl.BlockSpec(memory_space=pltpu.SMEM)] + [vmem] * (n + 1 + 3 * n),
        out_specs=[vmem] * (4 * n + 1),
        out_shape=[jax.ShapeDtypeStruct(shp, F32) for shp in shapes for _ in range(4)] + [jax.ShapeDtypeStruct((1, 128), F32)],
        compiler_params=_cp(VMEM_BIG))(me, *gathered, loss_all, *flat_params)
    return [tuple(outs[4 * i:4 * i + 4]) for i in range(n)], outs[4 * n]


def _sum8(buf, name):
    _, r, ccols = buf.shape

    def body(b_ref, o_ref):
        acc = b_ref[0]
        for j in range(1, N_DEV):
            acc = acc + b_ref[j]
        o_ref[...] = acc

    tr = _row_tile(r, 256)
    return pl.pallas_call(
        body, name=name, grid=(r // tr,), in_specs=[pl.BlockSpec((N_DEV, tr, ccols), lambda i: (0, i, 0))],
        out_specs=pl.BlockSpec((tr, ccols), lambda i: (i, 0)), out_shape=jax.ShapeDtypeStruct((r, ccols), F32))(buf)


def _pack(arrays):
    pieces, layout, off = [], [], 0
    for a in arrays:
        n = a.size
        padded = -(-n // 1024) * 1024
        flat = a.reshape(-1).astype(F32)
        if padded != n:
            flat = jnp.pad(flat, (0, padded - n))
        pieces.append(flat.reshape(padded // 128, 128))
        layout.append((off, n, a.shape))
        off += padded // 128
    return jnp.concatenate(pieces, axis=0), layout


def _unpack(packed, layout):
    out = []
    for off, n, shape in layout:
        rows = -(-n // 1024) * 8
        out.append(packed[off:off + rows].reshape(-1)[:n].reshape(shape))
    return out


def kernel(x, mem, norm_mix, norm_ffn, mem_norm, w_kv, w_out, w_ffn1, w_ffn2, a_in, a_ln_g, a_ln_b, a_ws, a_bs, b_in, b_conv_w, b_conv_b, b_dt_bias, b_a_log, b_d, b_gnorm, final_norm, loss_target, m_norm_mix, m_norm_ffn, m_mem_norm, m_w_kv, m_w_out, m_w_ffn1, m_w_ffn2, m_a_in, m_a_ln_g, m_a_ln_b, m_a_ws, m_a_bs, m_b_in, m_b_conv_w, m_b_conv_b, m_b_dt_bias, m_b_a_log, m_b_d, m_b_gnorm, m_final_norm, v_norm_mix, v_norm_ffn, v_mem_norm, v_w_kv, v_w_out, v_w_ffn1, v_w_ffn2, v_a_in, v_a_ln_g, v_a_ln_b, v_a_ws, v_a_bs, v_b_in, v_b_conv_w, v_b_conv_b, v_b_dt_bias, v_b_a_log, v_b_d, v_b_gnorm, v_final_norm):
    s = x.shape[1]
    xs = x.reshape(s, D_MODEL)
    mems = mem.reshape(N_MEM, D_MODEL)
    target = loss_target.reshape(s, D_MODEL)
    ax, ay, ac = lax.axis_index("x"), lax.axis_index("y"), lax.axis_index("c")
    me = 4 * ax + 2 * ay + ac
    xyc = jnp.stack([ax, ay, ac]).astype(jnp.int32)

    b_cols = b_in.shape[2]
    act = lambda a: a.astype(_ACT)
    lay_f1, lay_f2 = ("col", 512, (1, D_MODEL, D_FF)), ("row", 512, (1, D_FF, D_MODEL))
    lay_out, lay_kv = ("row", 384, (1, 3 * D_MODEL, D_MODEL)), ("col", 256, (1, D_MODEL, 2 * X_WIDTH))
    small_w_pack = _pack([b_conv_w[0], b_conv_b[0], b_gnorm[0]])[0]
    (WA,) = _all_gather_seq([act(a_in)], [("col", 640, (1, D_MODEL, 5 * D_MODEL))], "ag_proj_a")
    wo0, wkv0 = _all_gather_seq([act(w_out[0:1]), act(w_kv[0:1])], [lay_out, lay_kv], "ag_out0")
    w1_0, w2_0 = _all_gather_seq([act(w_ffn1[0:1]), act(w_ffn2[0:1])], [lay_f1, lay_f2], "ag_ffn0")
    a0 = _rms_fwd(xs, norm_mix[0].reshape(1, -1), "mix_norm0")
    tr_b = lambda a: jnp.swapaxes(a, 1, 2)
    wbt_blk, small_w = _all_gather_seq(
        [act(tr_b(b_in)[0]), small_w_pack],
        [("blk", 0, (N_DEV, b_cols, D_MODEL)), ("blk", 0, (N_DEV, 32, 128))], "ag_proj_b", after=[a0])
    wo1, wkv1 = _all_gather_seq([act(w_out[1:2]), act(w_kv[1:2])], [lay_out, lay_kv], "ag_out1", after=[a0])
    w1_1, w2_1 = _all_gather_seq([act(w_ffn1[1:2]), act(w_ffn2[1:2])], [lay_f1, lay_f2], "ag_ffn1", after=[a0])
    W1, W2, WO, WKV = [w1_0, w1_1], [w2_0, w2_1], [wo0, wo1], [wkv0, wkv1]
    dt0 = D_INNER + CONV_DIM

    row = lambda a: a.reshape(1, -1)
    nmix = [row(norm_mix[0]), row(norm_mix[1])]
    nffn = [row(norm_ffn[0]), row(norm_ffn[1])]
    nmem = [row(mem_norm[0]), row(mem_norm[1])]
    fin = row(final_norm)
    lng, lnb = a_ln_g.reshape(1, D_INNER), a_ln_b.reshape(1, D_INNER)
    ws = a_ws[0]
    bs3 = a_bs[0].reshape(A_GROUPS, CHUNK, 1)
    pad_h = lambda a: jnp.pad(a.reshape(-1), (0, HPAD - SSM_HEADS))
    bias_row = pad_h(b_dt_bias).reshape(1, HPAD)
    alog_row = pad_h(b_a_log).reshape(1, HPAD)
    dfull = jnp.repeat(b_d.reshape(-1), SSM_P).reshape(1, D_INNER)

    kvs, mns = [None, None], [None, None]

    def mem_kv(i, after=None):
        gain = nmem[i] if after is None else _tie(nmem[i], after, f"tie_mem{i}")
        mns[i] = _rms_fwd(mems, gain, f"mem_norm{i}")
        kvs[i] = _mm(mns[i], WKV[i], m=N_MEM, n=2 * X_WIDTH, k=D_MODEL, b_at=(0, 0, 0), out_dtype=_ACT, name=f"kv{i}")

    def ffn_fwd(h, i):
        f = _rms_fwd(h, nffn[i], f"ffn_norm{i}")
        p = _mm(f, W1[i], m=s, n=D_FF, k=D_MODEL, b_at=(0, 0, 0), out_dtype=_ACT, name=f"ffn_up{i}")
        hn = _mm(p, W2[i], m=s, n=D_MODEL, k=D_FF, b_at=(0, 0, 0), a_pro="relu2", add=h, name=f"ffn_down{i}")
        return f, p, hn

    def out_proj(h, cat, i):
        return _mm(cat, WO[i], m=s, n=D_MODEL, k=3 * D_MODEL, b_at=(0, 0, 0), add=h, name=f"out_proj{i}")

    proj_a = _mm(a0, WA, m=s, n=5 * D_MODEL, k=D_MODEL, b_at=(0, 0, 0), name="proj_a")
    mem_kv(0, after=[proj_a])
    cat_a = _gmlp_fwd(proj_a, lng, lnb, ws, bs3, "gmlp_fwd")
    cat_a = _attn_fwd(proj_a, 4, kvs[0], cat_a, "attn_fwd0")
    h1 = out_proj(xs, cat_a, 0)
    f0, p0, h2 = ffn_fwd(h1, 0)

    wbt_blk, small_w, _ = lax.optimization_barrier((wbt_blk, small_w, p0))
    wbt_full = wbt_blk.reshape(N_DEV * b_cols, D_MODEL)
    WBT = jnp.concatenate([wbt_full[:dt0], wbt_full[dt0 + SSM_HEADS:]], axis=0)
    WBDT = jnp.pad(wbt_full[dt0:dt0 + SSM_HEADS], ((0, HPAD - SSM_HEADS), (0, 0)))
    cw_sh, cb_sh, gn_sh = 4 * 384, 384, 256
    sw = small_w.reshape(N_DEV, 32 * 128)
    conv_w = jnp.transpose(sw[:, :cw_sh].reshape(N_DEV, CONV_K, 384), (1, 0, 2)).reshape(CONV_K, CONV_DIM)
    conv_b = sw[:, 2048:2048 + cb_sh].reshape(1, CONV_DIM)
    gnorm = sw[:, 3072:3072 + gn_sh].reshape(1, D_INNER)

    a1 = _rms_fwd(h2, nmix[1], "mix_norm1")
    proj_b = _mm(a1, WBT, m=s, n=6 * D_MODEL, k=D_MODEL, tb=True, name="proj_b")
    dt_raw = _mm(a1, WBDT, m=s, n=HPAD, k=D_MODEL, tb=True, name="proj_dt")
    xbc = _conv_fwd(proj_b, conv_w, conv_b, "conv_fwd")
    y_ssd, states = _ssd_fwd(xbc, dt_raw, bias_row, alog_row, dfull, "ssd_fwd")
    cat_b = _gate_fwd(y_ssd, proj_b, gnorm, "gate_fwd")
    mem_kv(1, after=[cat_b])
    cat_b = _attn_fwd(proj_b, 5, kvs[1], cat_b, "attn_fwd1")
    h3 = out_proj(h2, cat_b, 1)
    f1, p1, h4 = ffn_fwd(h3, 1)

    loss_part, dh, dh_act, d_fin = _loss_head(h4, fin, target, "loss_head")

    g_f1, g_f2, g_out, g_kv = [None, None], [None, None], [None, None], [None, None]
    d_nffn, d_nmix, d_nmem = [None, None], [None, None], [None, None]

    def ffn_bwd(dh, dh_act, h_in, f, p, i, after=(), after_last=()):
        dp = _mm(dh_act, W2[i], m=s, n=D_FF, k=D_MODEL, tb=True, b_at=(0, 0, 0), epi_p=p, out_dtype=_ACT, name=f"ffn_down_dx{i}")
        g_f2[i] = _mm(p, dh_act, m=D_FF, n=D_MODEL, k=s, ta=True, a_pro="relu2", out_dtype=_ACT, name=f"ffn_down_dw{i}")
        g_f1[i] = _mm(f, dp, m=D_MODEL, n=D_FF, k=s, ta=True, out_dtype=_ACT, name=f"ffn_up_dw{i}")
        df = _mm(dp, W1[i], m=s, n=D_MODEL, k=D_FF, tb=True, b_at=(0, 0, 0), after=after, name=f"ffn_up_dx{i}")
        gain = _tie(nffn[i], after_last, f"tie_ffn_norm{i}") if after_last else nffn[i]
        dh_in, dh_in_act, d_nffn[i] = _rms_bwd(h_in, gain, df, dh, f"ffn_norm_bwd{i}")
        return dh_in, dh_in_act

    def out_bwd(dh_act, cat, i):
        dcat = _mm(dh_act, WO[i], m=s, n=3 * D_MODEL, k=D_MODEL, tb=True, b_at=(0, 0, 0), out_dtype=_ACT, name=f"out_dx{i}")
        g_out[i] = _mm(cat, dh_act, m=3 * D_MODEL, n=D_MODEL, k=s, ta=True, out_dtype=_ACT, name=f"out_dw{i}")
        return dcat

    def mem_bwd(dkv, i):
        g_kv[i] = _mm(mns[i], dkv, m=D_MODEL, n=2 * X_WIDTH, k=N_MEM, ta=True, out_dtype=_ACT, name=f"kv_dw{i}")
        dmn = _mm(dkv, WKV[i], m=N_MEM, n=D_MODEL, k=2 * X_WIDTH, tb=True, b_at=(0, 0, 0), name=f"kv_dx{i}")
        _, _, d_nmem[i] = _rms_bwd(mems, nmem[i], dmn, None, f"mem_norm_bwd{i}")

    lay_g = {"f1": ("col", 512, (D_MODEL, 512)), "f2": ("row", 512, (512, D_MODEL)), "out": ("row", 384, (384, D_MODEL)),
             "kv": ("col", 256, (D_MODEL, 256)), "a": ("col", 640, (D_MODEL, 640)), "b": ("blk", 0, (b_cols, D_MODEL))}
    reduced = {}

    def reduce_scatter(group, tag, after=(), sums_after=()):
        grads3, lays3 = [], []
        for fam, _, g in group:
            kind, width, shape = lay_g[fam]
            grads3.append(g if kind == "blk" else g.reshape((1,) + g.shape))
            lays3.append((kind, width, shape if kind == "blk" else (1,) + shape))
        recv1 = _rs_to_sibling(grads3, lays3, f"rs_sibling_{tag}", after)
        if sums_after:
            recv1 = lax.optimization_barrier((tuple(recv1), tuple(sums_after)))[0]
        parts = [_rs_chip_sum(g, recv1[t].reshape((4,) + lay_g[fam][2]), lay_g[fam], xyc, f"rs_chip_sum_{fam}{i}")
                 for t, (fam, i, g) in enumerate(group)]
        recv2 = _rs_across_chips(parts, f"rs_chips_{tag}")
        for (fam, i, _), p, r2 in zip(group, parts, recv2):
            reduced[fam, i] = (p, r2)
        return parts, recv2

    dh3, dh3_act = ffn_bwd(dh, dh_act, h3, f1, p1, 1)
    dcat_b = out_bwd(dh3_act, cat_b, 1)
    sums, got_ffn1 = reduce_scatter([("f1", 1, g_f1[1]), ("f2", 1, g_f2[1]), ("out", 1, g_out[1])], "ffn1", sums_after=[dcat_b])
    dy_ssd, dproj_b, d_gnorm = _gate_bwd(y_ssd, proj_b, gnorm, dcat_b, "gate_bwd")
    dproj_b, dkv_b = _attn_bwd(proj_b, 5, kvs[1], dcat_b, dproj_b, "attn_bwd1")
    mem_bwd(dkv_b, 1)
    dxbc, ddt_raw, d_alog, d_dskip, d_dtbias = _ssd_bwd(
        xbc, dt_raw, _tie(bias_row, sums, "tie_ffn1"), alog_row, dfull, dy_ssd, states, "ssd_bwd")
    dproj_b, d_convw, d_convb = _conv_bwd(proj_b, conv_w, _tie(conv_b, got_ffn1, "tie_got_ffn1"), dxbc, dproj_b, "conv_bwd")
    gb = _mm(dproj_b, a1, m=6 * D_MODEL, n=D_MODEL, k=s, ta=True, out_dtype=_ACT, name="proj_b_dw")
    gb_dt = _mm(ddt_raw, a1, m=HPAD, n=D_MODEL, k=s, ta=True, out_dtype=_ACT, name="proj_b_dw_dt")
    gb_full = jnp.concatenate([gb[:dt0], gb_dt[:SSM_HEADS], gb[dt0:]], axis=0)
    gb_blk = gb_full.reshape(N_DEV, b_cols, D_MODEL)
    da1 = _mm(dproj_b, WBT, m=s, n=D_MODEL, k=6 * D_MODEL, name="proj_b_dx")
    sums, got_mix1 = reduce_scatter([("kv", 1, g_kv[1]), ("b", 0, gb_blk)], "mix1", sums_after=[da1])
    da1 = _mm(ddt_raw, WBDT, m=s, n=D_MODEL, k=HPAD, add=da1, name="proj_b_dx_dt")
    dh2, dh2_act, d_nmix[1] = _rms_bwd(h2, _tie(nmix[1], sums, "tie_mix1"), da1, dh3, "mix_norm_bwd1")

    dh1, dh1_act = ffn_bwd(dh2, dh2_act, h1, f0, p0, 0, after=got_ffn1, after_last=got_mix1)
    dcat_a = out_bwd(dh1_act, cat_a, 0)
    sums, got_ffn0 = reduce_scatter([("f1", 0, g_f1[0]), ("f2", 0, g_f2[0]), ("out", 0, g_out[0])], "ffn0", sums_after=[dcat_a])
    dproj_a, d_ws, d_bs3, d_lng, d_lnb = _gmlp_bwd(proj_a, dcat_a, _tie(lng, sums, "tie_ffn0"), lnb, ws, bs3, "gmlp_bwd")
    dproj_a, dkv_a = _attn_bwd(proj_a, 4, kvs[0], dcat_a, dproj_a, "attn_bwd0")
    mem_bwd(dkv_a, 0)

    def big_update(w, m, v, fam, nlayer):
        res = None
        for i in range(nlayer):
            part, recv2 = reduced[fam, i]
            plist = [(part, 0), (recv2, 0), (recv2, 1), (recv2, 2)]
            res = _adamw(w, m, v, plist, f"adamw_{fam}{i}", layer=i, prev=res)
        return res

    ga = _mm(a0, dproj_a, m=D_MODEL, n=5 * D_MODEL, k=s, ta=True, out_dtype=_ACT, name="proj_a_dw")
    r_b = big_update(tr_b(b_in), tr_b(m_b_in), tr_b(v_b_in), "b", 1)
    sums, _ = reduce_scatter([("kv", 0, g_kv[0]), ("a", 0, ga)], "mix0", after=got_ffn0, sums_after=r_b)
    r_b = [tr_b(o) for o in r_b]
    da0 = _mm(dproj_a, WA, m=s, n=D_MODEL, k=5 * D_MODEL, tb=True, b_at=(0, 0, 0), after=sums, name="proj_a_dx")
    grad_x, _, d_nmix[0] = _rms_bwd(xs, nmix[0], da0, dh1, "mix_norm_bwd0")

    small_names = ["norm_mix", "norm_ffn", "mem_norm", "a_ln_g", "a_ln_b", "a_ws", "a_bs", "b_dt_bias", "b_a_log", "b_d",
                   "final_norm", "b_conv_w", "b_conv_b", "b_gnorm"]
    small_grads = [jnp.concatenate(d_nmix, axis=0), jnp.concatenate(d_nffn, axis=0), jnp.concatenate(d_nmem, axis=0),
                   d_lng, d_lnb, d_ws.reshape(A_GROUPS * CHUNK, CHUNK), d_bs3.reshape(A_GROUPS, CHUNK),
                   d_dtbias[:, :SSM_HEADS], d_alog[:, :SSM_HEADS], d_dskip[:, :SSM_HEADS], d_fin,
                   d_convw, d_convb, d_gnorm]
    small_2d = [(2, D_MODEL)] * 3 + [(1, D_INNER)] * 2 + [(A_GROUPS * CHUNK, CHUNK), (A_GROUPS, CHUNK)] + [(1, SSM_HEADS)] * 3 \
        + [(1, D_MODEL), (CONV_K, 384), (1, 384), (1, 256)]
    gathered = _all_gather_seq(
        small_grads + [loss_part], [("blk", 0, (N_DEV,) + g.shape) for g in small_grads + [loss_part]], "ag_small_grads")

    r_f1 = big_update(w_ffn1, m_w_ffn1, v_w_ffn1, "f1", 2)
    r_f2 = big_update(w_ffn2, m_w_ffn2, v_w_ffn2, "f2", 2)
    r_out = big_update(w_out, m_w_out, v_w_out, "out", 2)
    r_kv = big_update(w_kv, m_w_kv, v_w_kv, "kv", 2)
    r_a = big_update(a_in, m_a_in, v_a_in, "a", 1)

    small_w = [norm_mix, norm_ffn, mem_norm, a_ln_g, a_ln_b, a_ws, a_bs, b_dt_bias, b_a_log, b_d, final_norm,
               b_conv_w, b_conv_b, b_gnorm]
    small_m = [m_norm_mix, m_norm_ffn, m_mem_norm, m_a_ln_g, m_a_ln_b, m_a_ws, m_a_bs, m_b_dt_bias, m_b_a_log, m_b_d,
               m_final_norm, m_b_conv_w, m_b_conv_b, m_b_gnorm]
    small_v = [v_norm_mix, v_norm_ffn, v_mem_norm, v_a_ln_g, v_a_ln_b, v_a_ws, v_a_bs, v_b_dt_bias, v_b_a_log, v_b_d,
               v_final_norm, v_b_conv_w, v_b_conv_b, v_b_gnorm]
    params = [tuple(a.reshape(shp) for a in wmv) for shp, wmv in zip(small_2d, zip(small_w, small_m, small_v))]
    loss_all = _tie(gathered[-1], [r_a[0], r_kv[0]], "tie_small")
    small_res, loss_sum = _small_update(gathered[:-1], params, loss_all, me.astype(jnp.int32).reshape(1), "adamw_small")
    loss = loss_sum[0, 0]

    names = ["norm_mix", "norm_ffn", "mem_norm", "w_kv", "w_out", "w_ffn1", "w_ffn2", "a_in", "a_ln_g", "a_ln_b", "a_ws",
             "a_bs", "b_in", "b_conv_w", "b_conv_b", "b_dt_bias", "b_a_log", "b_d", "b_gnorm", "final_norm"]
    big = {"w_kv": r_kv, "w_out": r_out, "w_ffn1": r_f1, "w_ffn2": r_f2, "a_in": r_a, "b_in": r_b}
    outs = [loss, grad_x.reshape(x.shape)]
    for kind in range(4):
        for nm in names:
            if nm in big:
                outs.append(big[nm][kind])
            else:
                i = small_names.index(nm)
                outs.append(small_res[i][kind].reshape(small_w[i].shape))
    return tuple(outs)
```

```python
import functools
import math

import jax
import jax.numpy as jnp
from jax import lax
from jax.experimental import pallas as pl
from jax.experimental.pallas import tpu as pltpu
from jax.experimental.pallas import tpu_sc as plsc

F32 = jnp.float32
_MXU = jnp.bfloat16
_ACT = jnp.bfloat16
_HI = lax.Precision.HIGHEST

D_MODEL = 1024
CHUNK = 128
N_MEM = 256
D_INNER = 2048
A_GROUPS = 8
A_GW = D_INNER // A_GROUPS
SSM_HEADS = 32
SSM_P = 64
SSM_GROUPS = 4
SSM_GW = D_INNER // SSM_GROUPS
SSM_N = 128
CONV_K = 4
CONV_DIM = 3072
X_HEADS = 4
X_HD = 256
X_WIDTH = 1024
D_FF = 4096
EPS = 1e-6
HPAD = 128
N_DEV = 8

ADAM_LR = 0.001
ADAM_B1 = 0.9
ADAM_B2 = 0.999
ADAM_EPS = 1e-08
ADAM_WD = 0.01
ADAM_STEP = 10

VMEM_BIG = 56 * 1024 * 1024
MESH = pl.DeviceIdType.MESH


def _cp(vmem=None):
    if vmem is None:
        return pltpu.CompilerParams()
    return pltpu.CompilerParams(vmem_limit_bytes=vmem)


def _dot(a, b, dims=((1,), (0,))):
    return lax.dot_general(a.astype(_MXU), b.astype(_MXU), (dims, ((), ())), preferred_element_type=F32)


def _dot_nt(a, b):
    return _dot(a, b, ((1,), (1,)))


def _dot_tn(a, b):
    return _dot(a, b, ((0,), (0,)))


def _dot_hi(a, b, dims=((1,), (0,))):
    return lax.dot_general(a.astype(F32), b.astype(F32), (dims, ((), ())), precision=_HI, preferred_element_type=F32)


def _split3(x):
    x1 = x.astype(jnp.bfloat16)
    r = x - x1.astype(F32)
    x2 = r.astype(jnp.bfloat16)
    x3 = (r - x2.astype(F32)).astype(jnp.bfloat16)
    return x1, x2, x3


def _dot_sel(x, sel, dims=((1,), (0,)), terms=2):
    sel = sel.astype(jnp.bfloat16)
    parts = [lax.dot_general(t, sel, (dims, ((), ())), preferred_element_type=F32) for t in _split3(x)[:terms]]
    return functools.reduce(lambda a, b: a + b, parts)


def _sel_dot(sel, x, dims=((1,), (0,))):
    sel = sel.astype(jnp.bfloat16)
    parts = [lax.dot_general(sel, t, (dims, ((), ())), preferred_element_type=F32) for t in _split3(x)]
    return (parts[0] + parts[1]) + parts[2]


def _sigmoid(x):
    return 1.0 / (1.0 + jnp.exp(-x))


def _gelu(x):
    return 0.5 * x * (1.0 + lax.erf(x * (1.0 / math.sqrt(2.0))))


def _gelu_grad(x):
    return 0.5 * (1.0 + lax.erf(x * (1.0 / math.sqrt(2.0)))) + x * jnp.exp(-0.5 * x * x) * (1.0 / math.sqrt(2.0 * math.pi))


def _softplus(x):
    return jnp.maximum(x, 0.0) + jnp.log1p(jnp.exp(-jnp.abs(x)))


def _iota(shape, dim):
    return lax.broadcasted_iota(jnp.int32, shape, dim)


MM_VMEM_BUDGET = 40 * 1024 * 1024
HBM_BYTES_PER_S = 2.5e12
GRID_STEP_S = 0.35e-6
VMEM_ACC_BYTES_PER_S = 6e12


def _divisors(dim, unit):
    out = [d for d in range(unit, min(dim, 2048) + 1, unit) if dim % d == 0]
    return out if out else [dim]


def _mm_tiles(m, n, k, sa, sb, s_mn, a_pro, offsets):
    best = None
    (a_r0, a_c0, ta), (b_r0, b_c0, tb), (o_r0, o_c0) = offsets
    for tm in _divisors(m, 128):
        for tn in _divisors(n, 128):
            for tk in [k // d for d in (1, 2, 3, 4, 6, 8) if k % d == 0 and (k // d) % 128 == 0]:
                a_t = (tk, tm) if ta else (tm, tk)
                b_t = (tn, tk) if tb else (tk, tn)
                if a_r0 % a_t[0] or a_c0 % a_t[1] or b_r0 % b_t[0] or b_c0 % b_t[1] or o_r0 % tm or o_c0 % tn:
                    continue
                nk = k // tk
                vmem = 2 * (tm * tk * sa + tk * tn * sb + tm * tn * s_mn) + tm * tn * 4 * (2 if nk > 1 else 1)
                if a_pro or sa == 4:
                    vmem += tm * tk * 6
                if sb == 4:
                    vmem += tk * tn * 2
                if vmem > MM_VMEM_BUDGET:
                    continue
                gi, gj = m // tm, n // tn
                for j_inner in (True, False):
                    if nk > 1:
                        traffic = gj * m * k * sa + gi * k * n * sb
                    elif j_inner:
                        traffic = m * k * sa + gi * k * n * sb
                    else:
                        traffic = gj * m * k * sa + k * n * sb
                    traffic += m * n * s_mn + (tm * tk * sa + tk * tn * sb)
                    cost = traffic / HBM_BYTES_PER_S + gi * gj * nk * GRID_STEP_S
                    if nk > 1:
                        cost += m * n * 8 * nk / VMEM_ACC_BYTES_PER_S
                    if best is None or cost < best[0]:
                        best = (cost, tm, tn, tk, j_inner)
    assert best is not None, (m, n, k)
    return best[1:]


def _mm(a, b, *, m, n, k, name, ta=False, tb=False, a_at=(None, 0, 0), b_at=(None, 0, 0),
        out_dtype=F32, add=None, epi_p=None, epi_at=(None, 0, 0), out=None, out_at=(None, 0, 0),
        out_full=None, a_pro=None, after=()):
    s_mn =jnp.dtype(out.dtype if out is not None else out_dtype).itemsize
    s_mn += add.dtype.itemsize if add is not None else 0
    s_mn += epi_p.dtype.itemsize if epi_p is not None else 0
    tm, tn, tk, j_inner = _mm_tiles(m, n, k, a.dtype.itemsize, b.dtype.itemsize, s_mn, a_pro is not None,
                                    ((a_at[1], a_at[2], ta), (b_at[1], b_at[2], tb), (out_at[1], out_at[2])))
    nk = k // tk

    def spec(at, tr, tc, rsel, csel):
        lead, r0, c0 = at
        assert r0 % tr == 0 and c0 % tc == 0, (name, at, tr, tc)
        rb, cb = r0 // tr, c0 // tc
        if lead is None:
            return pl.BlockSpec((tr, tc), lambda g0, g1, kk: (rb + rsel(g0, g1, kk), cb + csel(g0, g1, kk)))
        return pl.BlockSpec((None, tr, tc), lambda g0, g1, kk: (lead, rb + rsel(g0, g1, kk), cb + csel(g0, g1, kk)))

    gi = (lambda g0, g1, kk: g0) if j_inner else (lambda g0, g1, kk: g1)
    gj = (lambda g0, g1, kk: g1) if j_inner else (lambda g0, g1, kk: g0)
    gk = lambda g0, g1, kk: kk
    a_spec = spec(a_at, tk, tm, gk, gi) if ta else spec(a_at, tm, tk, gi, gk)
    b_spec = spec(b_at, tn, tk, gj, gk) if tb else spec(b_at, tk, tn, gk, gj)
    dims = ((0,), (0,)) if ta else (((1,), (1,)) if tb else ((1,), (0,)))
    assert not (ta and tb)

    operands, in_specs = [a, b], [a_spec, b_spec]
    if add is not None:
        operands.append(add)
        in_specs.append(spec((None, 0, 0), tm, tn, gi, gj))
    if epi_p is not None:
        operands.append(epi_p)
        in_specs.append(spec(epi_at, tm, tn, gi, gj))
    aliases = {}
    if out is not None:
        aliases = {len(operands): 0}
        operands.append(out)
        in_specs.append(pl.BlockSpec(memory_space=pl.ANY))
        out_struct = jax.ShapeDtypeStruct(out.shape, out.dtype)
        out_dtype = out.dtype
    else:
        out_struct = jax.ShapeDtypeStruct(out_full if out_full is not None else (m, n), out_dtype)
    has_add, has_epi = add is not None, epi_p is not None
    n_skip = (1 if out is not None else 0) + len(after)
    operands += list(after)
    in_specs += [pl.BlockSpec(memory_space=pl.ANY)] * len(after)

    def body(*refs):
        a_ref, b_ref = refs[0], refs[1]
        pos = 2
        add_ref = epi_ref = None
        if has_add:
            add_ref = refs[pos]
            pos += 1
        if has_epi:
            epi_ref = refs[pos]
            pos += 1
        pos += n_skip
        o_ref = refs[pos]

        def finish(r):
            if has_add:
                r = r + add_ref[...].astype(F32)
            if has_epi:
                r = r * (2.0 * jnp.maximum(epi_ref[...].astype(F32), 0.0))
            o_ref[...] = r.astype(o_ref.dtype)

        av = a_ref[...]
        if a_pro == "relu2":
            av = jnp.square(jnp.maximum(av.astype(F32), 0.0))
        part = _dot(av, b_ref[...], dims)
        if nk == 1:
            finish(part)
        else:
            acc_ref = refs[pos + 1]
            kk = pl.program_id(2)

            @pl.when(kk == 0)
            def _():
                acc_ref[...] = part

            @pl.when(kk > 0)
            def _():
                acc_ref[...] += part

            @pl.when(kk == nk - 1)
            def _():
                finish(acc_ref[...])

    grid = (m // tm, n // tn, nk) if j_inner else (n // tn, m // tm, nk)
    return pl.pallas_call(
        body, name=name, grid=grid, in_specs=in_specs,
        out_specs=spec(out_at, tm, tn, gi, gj), out_shape=out_struct,
        scratch_shapes=[pltpu.VMEM((tm, tn), F32)] if nk > 1 else [], input_output_aliases=aliases,
        compiler_params=_cp(VMEM_BIG))(*operands)


def _rms_fwd(x, g, name, tm=1024):
    s, d = x.shape
    tm = min(tm, s)

    def body(x_ref, g_ref, o_ref):
        xv = x_ref[...]
        r = lax.rsqrt(jnp.mean(xv * xv, axis=-1, keepdims=True) + EPS)
        o_ref[...] = (xv * r * g_ref[...]).astype(o_ref.dtype)

    return pl.pallas_call(
        body, name=name, grid=(s // tm,),
        in_specs=[pl.BlockSpec((tm, d), lambda i: (i, 0)), pl.BlockSpec((1, d), lambda i: (0, 0))],
        out_specs=pl.BlockSpec((tm, d), lambda i: (i, 0)),
        out_shape=jax.ShapeDtypeStruct((s, d), _ACT), compiler_params=_cp(VMEM_BIG))(x, g)


def _rms_bwd(x, g, dy, dres, name, tm=512):
    s, d = x.shape
    tm = min(tm, s)
    has_res = dres is not None

    def body(*refs):
        if has_res:
            x_ref, g_ref, dy_ref, dres_ref, dx_ref, dxa_ref, dg_ref = refs
        else:
            x_ref, g_ref, dy_ref, dx_ref, dxa_ref, dg_ref = refs

        @pl.when(pl.program_id(0) == 0)
        def _():
            dg_ref[...] = jnp.zeros_like(dg_ref)

        xv = x_ref[...]
        dyv = dy_ref[...].astype(F32)
        r = lax.rsqrt(jnp.mean(xv * xv, axis=-1, keepdims=True) + EPS)
        xh = xv * r
        dyg = dyv * g_ref[...]
        dx = r * (dyg - xh * jnp.mean(dyg * xh, axis=-1, keepdims=True))
        if has_res:
            dx = dx + dres_ref[...]
        dx_ref[...] = dx
        dxa_ref[...] = dx.astype(dxa_ref.dtype)
        dg_ref[...] += jnp.sum(dyv * xh, axis=0, keepdims=True)

    row = pl.BlockSpec((tm, d), lambda i: (i, 0))
    vec = pl.BlockSpec((1, d), lambda i: (0, 0))
    in_specs = [row, vec, row] + ([row] if has_res else [])
    operands = [x, g, dy] + ([dres] if has_res else [])
    return pl.pallas_call(
        body, name=name, grid=(s // tm,), in_specs=in_specs, out_specs=[row, row, vec],
        out_shape=[jax.ShapeDtypeStruct((s, d), F32), jax.ShapeDtypeStruct((s, d), _ACT),
                   jax.ShapeDtypeStruct((1, d), F32)], compiler_params=_cp(VMEM_BIG))(*operands)


def _loss_head(h, g, target, name, tm=512):
    s, d = h.shape
    tm = min(tm, s)

    def body(h_ref, g_ref, t_ref, loss_ref, dh_ref, dha_ref, dg_ref):
        @pl.when(pl.program_id(0) == 0)
        def _():
            dg_ref[...] = jnp.zeros_like(dg_ref)
            loss_ref[...] = jnp.zeros_like(loss_ref)

        xv = h_ref[...]
        r = lax.rsqrt(jnp.mean(xv * xv, axis=-1, keepdims=True) + EPS)
        xh = xv * r
        err = xh * g_ref[...] - t_ref[...]
        loss_ref[...] += jnp.full(loss_ref.shape, 0.5 * jnp.sum(jnp.mean(err * err, axis=-1, keepdims=True)), F32)
        dyv = err * (1.0 / d)
        dyg = dyv * g_ref[...]
        dh = r * (dyg - xh * jnp.mean(dyg * xh, axis=-1, keepdims=True))
        dh_ref[...] = dh
        dha_ref[...] = dh.astype(dha_ref.dtype)
        dg_ref[...] += jnp.sum(dyv * xh, axis=0, keepdims=True)

    row = pl.BlockSpec((tm, d), lambda i: (i, 0))
    vec = pl.BlockSpec((1, d), lambda i: (0, 0))
    return pl.pallas_call(
        body, name=name, grid=(s // tm,), in_specs=[row, vec, row],
        out_specs=[pl.BlockSpec((1, 128), lambda i: (0, 0)), row, row, vec],
        out_shape=[jax.ShapeDtypeStruct((1, 128), F32), jax.ShapeDtypeStruct((s, d), F32),
                   jax.ShapeDtypeStruct((s, d), _ACT), jax.ShapeDtypeStruct((1, d), F32)],
        compiler_params=_cp(VMEM_BIG))(h, g, target)


def _gmlp_parts(pu, pv, lng, lnb):
    u = _gelu(pu)
    v = _gelu(pv)
    mu = jnp.mean(v, axis=-1, keepdims=True)
    vc = v - mu
    rstd = lax.rsqrt(jnp.mean(vc * vc, axis=-1, keepdims=True) + EPS)
    xhat = vc * rstd
    vn = xhat * lng + lnb
    return u, xhat, rstd, vn


def _gmlp_fwd(proj, lng, lnb, ws, bs3, name):
    s = proj.shape[0]

    def body(pu_ref, pv_ref, lng_ref, lnb_ref, ws_ref, bs_ref, o_ref):
        u, _, _, vn = _gmlp_parts(pu_ref[...], pv_ref[...], lng_ref[...], lnb_ref[...])
        causal = _iota((CHUNK, CHUNK), 0) >= _iota((CHUNK, CHUNK), 1)
        for g in range(A_GROUPS):
            sl = slice(g * A_GW, (g + 1) * A_GW)
            w = jnp.where(causal, ws_ref[g], 0.0)
            sv = _dot(w, vn[:, sl]) + bs_ref[g]
            o_ref[:, sl] = (u[:, sl] * sv).astype(o_ref.dtype)

    full = lambda shape: pl.BlockSpec(shape, lambda c: (0,) * len(shape))
    return pl.pallas_call(
        body, name=name, grid=(s // CHUNK,),
        in_specs=[pl.BlockSpec((CHUNK, D_INNER), lambda c: (c, 0)), pl.BlockSpec((CHUNK, D_INNER), lambda c: (c, 1)),
                  full((1, D_INNER)), full((1, D_INNER)), full((A_GROUPS, CHUNK, CHUNK)), full((A_GROUPS, CHUNK, 1))],
        out_specs=pl.BlockSpec((CHUNK, D_INNER), lambda c: (c, 0)),
        out_shape=jax.ShapeDtypeStruct((s, D_INNER + X_WIDTH), _ACT), compiler_params=_cp(VMEM_BIG))(proj, proj, lng, lnb, ws, bs3)


def _gmlp_bwd(proj, dcat, lng, lnb, ws, bs3, name):
    s = proj.shape[0]

    def body(pu_ref, pv_ref, dm_ref, lng_ref, lnb_ref, ws_ref, bs_ref, dp_ref, dws_ref, dbs_ref, dlng_ref, dlnb_ref, dvn_ref):
        @pl.when(pl.program_id(0) == 0)
        def _():
            dws_ref[...] = jnp.zeros_like(dws_ref)
            dbs_ref[...] = jnp.zeros_like(dbs_ref)
            dlng_ref[...] = jnp.zeros_like(dlng_ref)
            dlnb_ref[...] = jnp.zeros_like(dlnb_ref)

        pu, pv = pu_ref[...], pv_ref[...]
        lng = lng_ref[...]
        u, xhat, rstd, vn = _gmlp_parts(pu, pv, lng, lnb_ref[...])
        dm = dm_ref[...].astype(F32)
        causal = _iota((CHUNK, CHUNK), 0) >= _iota((CHUNK, CHUNK), 1)
        for g in range(A_GROUPS):
            sl = slice(g * A_GW, (g + 1) * A_GW)
            w = jnp.where(causal, ws_ref[g], 0.0)
            sv = _dot(w, vn[:, sl]) + bs_ref[g]
            dsv = dm[:, sl] * u[:, sl]
            dp_ref[:, sl] = (dm[:, sl] * sv * _gelu_grad(pu[:, sl])).astype(dp_ref.dtype)
            dvn_ref[:, sl] = _dot_tn(w, dsv)
            dws_ref[g] += jnp.where(causal, _dot_nt(dsv, vn[:, sl]), 0.0)
            dbs_ref[g] += jnp.sum(dsv, axis=-1, keepdims=True)
        dvn = dvn_ref[...]
        dlng_ref[...] += jnp.sum(dvn * xhat, axis=0, keepdims=True)
        dlnb_ref[...] += jnp.sum(dvn, axis=0, keepdims=True)
        dxh = dvn * lng
        dv = rstd * (dxh - jnp.mean(dxh, axis=-1, keepdims=True) - xhat * jnp.mean(dxh * xhat, axis=-1, keepdims=True))
        dp_ref[:, D_INNER:] = (dv * _gelu_grad(pv)).astype(dp_ref.dtype)

    full = lambda shape: pl.BlockSpec(shape, lambda c: (0,) * len(shape))
    return pl.pallas_call(
        body, name=name, grid=(s // CHUNK,),
        in_specs=[pl.BlockSpec((CHUNK, D_INNER), lambda c: (c, 0)), pl.BlockSpec((CHUNK, D_INNER), lambda c: (c, 1)),
                  pl.BlockSpec((CHUNK, D_INNER), lambda c: (c, 0)),
                  full((1, D_INNER)), full((1, D_INNER)), full((A_GROUPS, CHUNK, CHUNK)), full((A_GROUPS, CHUNK, 1))],
        out_specs=[pl.BlockSpec((CHUNK, 2 * D_INNER), lambda c: (c, 0)), full((A_GROUPS, CHUNK, CHUNK)),
                   full((A_GROUPS, CHUNK, 1)), full((1, D_INNER)), full((1, D_INNER))],
        out_shape=[jax.ShapeDtypeStruct((s, 2 * D_INNER + X_WIDTH), _ACT), jax.ShapeDtypeStruct((A_GROUPS, CHUNK, CHUNK), F32),
                   jax.ShapeDtypeStruct((A_GROUPS, CHUNK, 1), F32), jax.ShapeDtypeStruct((1, D_INNER), F32),
                   jax.ShapeDtypeStruct((1, D_INNER), F32)],
        scratch_shapes=[pltpu.VMEM((CHUNK, D_INNER), F32)],
        compiler_params=_cp(VMEM_BIG))(proj, proj, dcat, lng, lnb, ws, bs3)


_X_SCALE = 1.0 / math.sqrt(X_HD)


def _attn_fwd(proj, qblk, kv, cat, name, tm=512):
    s = proj.shape[0]
    tm = min(tm, s)

    def body(q_ref, kv_ref, cat_ref, o_ref):
        for h in range(X_HEADS):
            sl = slice(h * X_HD, (h + 1) * X_HD)
            k = kv_ref[:, sl]
            v = kv_ref[:, X_WIDTH + h * X_HD:X_WIDTH + (h + 1) * X_HD]
            sc = _dot_nt(q_ref[:, sl], k) * _X_SCALE
            e = jnp.exp(sc - jnp.max(sc, axis=-1, keepdims=True))
            p = e / jnp.sum(e, axis=-1, keepdims=True)
            o_ref[:, sl] = _dot(p, v).astype(o_ref.dtype)

    return pl.pallas_call(
        body, name=name, grid=(s // tm,),
        in_specs=[pl.BlockSpec((tm, X_WIDTH), lambda i: (i, qblk)), pl.BlockSpec((N_MEM, 2 * X_WIDTH), lambda i: (0, 0)),
                  pl.BlockSpec(memory_space=pl.ANY)],
        out_specs=pl.BlockSpec((tm, X_WIDTH), lambda i: (i, D_INNER // X_WIDTH)),
        out_shape=jax.ShapeDtypeStruct(cat.shape, cat.dtype), input_output_aliases={2: 0},
        compiler_params=_cp(VMEM_BIG))(proj, kv, cat)


def _attn_bwd(proj, qblk, kv, dcat, dproj, name, tm=512):
    s = proj.shape[0]
    tm = min(tm, s)

    def body(q_ref, kv_ref, do_ref, dproj_ref, dq_ref, dkv_ref):
        @pl.when(pl.program_id(0) == 0)
        def _():
            dkv_ref[...] = jnp.zeros_like(dkv_ref)

        for h in range(X_HEADS):
            sl = slice(h * X_HD, (h + 1) * X_HD)
            slv = slice(X_WIDTH + h * X_HD, X_WIDTH + (h + 1) * X_HD)
            q = q_ref[:, sl]
            k = kv_ref[:, sl]
            v = kv_ref[:, slv]
            do = do_ref[:, sl].astype(F32)
            sc = _dot_nt(q, k) * _X_SCALE
            e = jnp.exp(sc - jnp.max(sc, axis=-1, keepdims=True))
            p = e / jnp.sum(e, axis=-1, keepdims=True)
            dp = _dot_nt(do, v)
            ds = p * (dp - jnp.sum(dp * p, axis=-1, keepdims=True)) * _X_SCALE
            dq_ref[:, sl] = _dot(ds, k).astype(dq_ref.dtype)
            dkv_ref[:, sl] += _dot_tn(ds, q)
            dkv_ref[:, slv] += _dot_tn(p, do)

    return pl.pallas_call(
        body, name=name, grid=(s // tm,),
        in_specs=[pl.BlockSpec((tm, X_WIDTH), lambda i: (i, qblk)), pl.BlockSpec((N_MEM, 2 * X_WIDTH), lambda i: (0, 0)),
                  pl.BlockSpec((tm, X_WIDTH), lambda i: (i, 2)), pl.BlockSpec(memory_space=pl.ANY)],
        out_specs=[pl.BlockSpec((tm, X_WIDTH), lambda i: (i, qblk)), pl.BlockSpec((N_MEM, 2 * X_WIDTH), lambda i: (0, 0))],
        out_shape=[jax.ShapeDtypeStruct(dproj.shape, dproj.dtype), jax.ShapeDtypeStruct((N_MEM, 2 * X_WIDTH), F32)],
        input_output_aliases={3: 0}, compiler_params=_cp(VMEM_BIG))(proj, kv, dcat, dproj)


CONV_TC = 256
_XBC_BLK0 = D_INNER // CONV_TC


CONV_RB = 64
SUBLANES = 8


def _rows_before(cur, prev_last, j):
    rolled = pltpu.roll(cur, j, 0)
    head = jnp.where(_iota((SUBLANES, cur.shape[1]), 0) < j, pltpu.roll(prev_last, j, 0), rolled[:SUBLANES])
    return jnp.concatenate([head, rolled[SUBLANES:]], axis=0)


def _rows_after(cur, next_first, j):
    n = cur.shape[0]
    rolled = pltpu.roll(cur, n - j, 0)
    tail = jnp.where(_iota((SUBLANES, cur.shape[1]), 0) >= SUBLANES - j, pltpu.roll(next_first, SUBLANES - j, 0),
                     rolled[n - SUBLANES:])
    return jnp.concatenate([rolled[:n - SUBLANES], tail], axis=0)


def _conv_pre(x_ref, w_ref, b_ref, r0, prev_last):
    cur = x_ref[pl.ds(r0, CONV_RB), :]
    shifts = [_rows_before(cur, prev_last, j) for j in range(1, CONV_K)]
    pre = b_ref[...] + w_ref[CONV_K - 1:CONV_K, :] * cur
    for j in range(1, CONV_K):
        pre = pre + w_ref[CONV_K - 1 - j:CONV_K - j, :] * shifts[j - 1]
    return pre, cur, shifts


def _conv_fwd(proj, w, b, name):
    s = proj.shape[0]

    def body(x_ref, w_ref, b_ref, o_ref):
        xv = x_ref[...]
        rows = _iota(xv.shape, 0)
        pre = b_ref[...] + w_ref[CONV_K - 1:CONV_K, :] * xv
        for j in range(1, CONV_K):
            pre = pre + w_ref[CONV_K - 1 - j:CONV_K - j, :] * jnp.where(rows >= j, pltpu.roll(xv, j, 0), 0.0)
        o_ref[...] = pre * _sigmoid(pre)

    return pl.pallas_call(
        body, name=name, grid=(CONV_DIM // CONV_TC,),
        in_specs=[pl.BlockSpec((s, CONV_TC), lambda j: (0, _XBC_BLK0 + j)), pl.BlockSpec((CONV_K, CONV_TC), lambda j: (0, j)),
                  pl.BlockSpec((1, CONV_TC), lambda j: (0, j))],
        out_specs=pl.BlockSpec((s, CONV_TC), lambda j: (0, j)),
        out_shape=jax.ShapeDtypeStruct((s, CONV_DIM), F32), compiler_params=_cp(VMEM_BIG))(proj, w, b)


def _conv_bwd(proj, w, b, dxbc, dproj, name):
    s = proj.shape[0]

    nb = s // CONV_RB

    def body(x_ref, w_ref, b_ref, d_ref, dproj_ref, dx_ref, dw_ref, db_ref, dpre_ref):
        def fold(v):
            out = v[:SUBLANES]
            for t in range(1, CONV_RB // SUBLANES):
                out = out + v[t * SUBLANES:(t + 1) * SUBLANES]
            return out

        def first(i, carry):
            prev_last, acc = carry
            r0 = pl.multiple_of(i * CONV_RB, CONV_RB)
            pre, cur, shifts = _conv_pre(x_ref, w_ref, b_ref, r0, prev_last)
            sig = _sigmoid(pre)
            dpre = d_ref[pl.ds(r0, CONV_RB), :] * (sig * (1.0 + pre * (1.0 - sig)))
            dpre_ref[pl.ds(r0, CONV_RB), :] = dpre
            taps = [cur] + shifts
            acc = tuple(a + fold(dpre * t) for a, t in zip(acc[:CONV_K], taps)) + (acc[CONV_K] + fold(dpre),)
            return cur[CONV_RB - SUBLANES:], acc

        zero8 = jnp.zeros((SUBLANES, CONV_TC), F32)
        _, acc = lax.fori_loop(0, nb, first, (zero8, (zero8,) * (CONV_K + 1)))
        for j in range(CONV_K):
            dw_ref[CONV_K - 1 - j:CONV_K - j, :] = jnp.sum(acc[j], axis=0, keepdims=True)
        db_ref[...] = jnp.sum(acc[CONV_K], axis=0, keepdims=True)

        def second(i, next_first):
            r0 = pl.multiple_of((nb - 1 - i) * CONV_RB, CONV_RB)
            cur = dpre_ref[pl.ds(r0, CONV_RB), :]
            dx = w_ref[CONV_K - 1:CONV_K, :] * cur
            for j in range(1, CONV_K):
                dx = dx + w_ref[CONV_K - 1 - j:CONV_K - j, :] * _rows_after(cur, next_first, j)
            dx_ref[pl.ds(r0, CONV_RB), :] = dx.astype(dx_ref.dtype)
            return cur[:SUBLANES]

        lax.fori_loop(0, nb, second, zero8)

    return pl.pallas_call(
        body, name=name, grid=(CONV_DIM // CONV_TC,),
        in_specs=[pl.BlockSpec((s, CONV_TC), lambda j: (0, _XBC_BLK0 + j)), pl.BlockSpec((CONV_K, CONV_TC), lambda j: (0, j)),
                  pl.BlockSpec((1, CONV_TC), lambda j: (0, j)), pl.BlockSpec((s, CONV_TC), lambda j: (0, j)),
                  pl.BlockSpec(memory_space=pl.ANY)],
        out_specs=[pl.BlockSpec((s, CONV_TC), lambda j: (0, _XBC_BLK0 + j)), pl.BlockSpec((CONV_K, CONV_TC), lambda j: (0, j)),
                   pl.BlockSpec((1, CONV_TC), lambda j: (0, j))],
        out_shape=[jax.ShapeDtypeStruct(dproj.shape, dproj.dtype), jax.ShapeDtypeStruct((CONV_K, CONV_DIM), F32),
                   jax.ShapeDtypeStruct((1, CONV_DIM), F32)], input_output_aliases={4: 0},
        scratch_shapes=[pltpu.VMEM((s, CONV_TC), F32)],
        compiler_params=_cp(VMEM_BIG))(proj, w, b, dxbc, dproj)


def _ssd_common(dtc_ref, br_ref, ar_ref, csb_ref, cst_ref, csf_ref):
    a_row = -jnp.exp(ar_ref[...])
    dt_c = _softplus(dtc_ref[...] + br_ref[...])
    tril = _iota((CHUNK, CHUNK), 0) >= _iota((CHUNK, CHUNK), 1)
    cs = _sel_dot(tril, dt_c * a_row)
    cst_ref[...] = cs.T
    e64 = (jnp.right_shift(_iota((HPAD, D_INNER), 1), 6) == _iota((HPAD, D_INNER), 0)).astype(jnp.bfloat16)
    e128 = jnp.right_shift(_iota((HPAD, SSM_HEADS * CHUNK), 1), 7) == _iota((HPAD, SSM_HEADS * CHUNK), 0)
    csb_ref[...] = _dot_sel(cs, e128)
    dt_full = _dot_sel(dt_c, e64)
    csf_ref[...] = _dot_sel(cs, e64)
    cs_full = csf_ref[...]
    cs_last = csf_ref[CHUNK - 1:CHUNK, :]
    e_full = jnp.exp(cs_full)
    f_full = jnp.exp(cs_last - cs_full)
    gamma = jnp.exp(cs_last)
    return a_row, dt_c, cs, dt_full, e_full, f_full, gamma, e64


def _ssd_lambda(csb_ref, cst_ref, h, causal):
    diff = csb_ref[:, h * CHUNK:(h + 1) * CHUNK] - cst_ref[h:h + 1, :]
    return jnp.exp(jnp.where(causal, diff, -1e30))


_SSD_VEC_SPECS = lambda: [pl.BlockSpec((1, HPAD), lambda c: (0, 0)), pl.BlockSpec((1, HPAD), lambda c: (0, 0)),
                          pl.BlockSpec((1, D_INNER), lambda c: (0, 0))]


def _ssd_fwd(xbc, dtc, bias_row, alog_row, dfull, name):
    s = xbc.shape[0]
    nc = s // CHUNK

    def body(xbc_ref, dtc_ref, br_ref, ar_ref, df_ref, y_ref, st_ref, ht_ref, csb_ref, cst_ref, csf_ref):
        @pl.when(pl.program_id(0) == 0)
        def _():
            ht_ref[...] = jnp.zeros_like(ht_ref)

        _, _, _, dt_full, e_full, f_full, gamma, _ = _ssd_common(dtc_ref, br_ref, ar_ref, csb_ref, cst_ref, csf_ref)
        x = xbc_ref[:, :D_INNER]
        xdt = x * dt_full
        st_ref[...] = ht_ref[...]
        causal = _iota((CHUNK, CHUNK), 0) >= _iota((CHUNK, CHUNK), 1)
        lo = _iota((CHUNK, CHUNK), 1) < SSM_P
        for g in range(SSM_GROUPS):
            gs = slice(g * SSM_GW, (g + 1) * SSM_GW)
            bg = xbc_ref[:, D_INNER + g * SSM_N:D_INNER + (g + 1) * SSM_N]
            cg = xbc_ref[:, D_INNER + SSM_GROUPS * SSM_N + g * SSM_N:D_INNER + SSM_GROUPS * SSM_N + (g + 1) * SSM_N]
            ht = ht_ref[:, gs]
            cb = _dot_nt(cg, bg)
            yoff = e_full[:, gs] * _dot(cg, ht)
            for jp in range(SSM_GW // CHUNK):
                j = g * (SSM_GW // CHUNK) + jp
                ps = slice(j * CHUNK, (j + 1) * CHUNK)
                x2 = xdt[:, ps]
                y0 = _dot(cb * _ssd_lambda(csb_ref, cst_ref, 2 * j, causal), x2)
                y1 = _dot(cb * _ssd_lambda(csb_ref, cst_ref, 2 * j + 1, causal), x2)
                y_ref[:, ps] = (jnp.where(lo, y0, y1) + yoff[:, jp * CHUNK:(jp + 1) * CHUNK]
                                + x[:, ps] * df_ref[:, ps])
            ht_ref[:, gs] = gamma[:, gs] * ht + _dot_tn(bg, xdt[:, gs] * f_full[:, gs])

    return pl.pallas_call(
        body, name=name, grid=(nc,),
        in_specs=[pl.BlockSpec((CHUNK, CONV_DIM), lambda c: (c, 0)), pl.BlockSpec((CHUNK, HPAD), lambda c: (c, 0))]
                 + _SSD_VEC_SPECS(),
        out_specs=[pl.BlockSpec((CHUNK, D_INNER), lambda c: (c, 0)), pl.BlockSpec((None, SSM_N, D_INNER), lambda c: (c, 0, 0))],
        out_shape=[jax.ShapeDtypeStruct((s, D_INNER), F32), jax.ShapeDtypeStruct((nc, SSM_N, D_INNER), F32)],
        scratch_shapes=[pltpu.VMEM((SSM_N, D_INNER), F32), pltpu.VMEM((CHUNK, SSM_HEADS * CHUNK), F32),
                        pltpu.VMEM((HPAD, CHUNK), F32), pltpu.VMEM((CHUNK, D_INNER), F32)],
        compiler_params=_cp(VMEM_BIG))(xbc, dtc, bias_row, alog_row, dfull)


def _ssd_bwd(xbc, dtc, bias_row, alog_row, dfull, dy, states, name):
    s = xbc.shape[0]
    nc = s // CHUNK
    rev = lambda c: nc - 1 - c

    def body(xbc_ref, dtc_ref, br_ref, ar_ref, df_ref, dy_ref, st_ref,
             dxbc_ref, ddt_ref, dalog_ref, dd_ref, dbias_ref,
             dht_ref, csb_ref, cst_ref, csf_ref, ddf_ref, dxs_ref, dcsf_ref, dcsl_ref):
        step = pl.program_id(0)

        @pl.when(step == 0)
        def _():
            dht_ref[...] = jnp.zeros_like(dht_ref)
            ddf_ref[...] = jnp.zeros_like(ddf_ref)
            dalog_ref[...] = jnp.zeros_like(dalog_ref)
            dbias_ref[...] = jnp.zeros_like(dbias_ref)
            dd_ref[...] = jnp.zeros_like(dd_ref)

        a_row, dt_c, _, dt_full, e_full, f_full, gamma, e64 = _ssd_common(dtc_ref, br_ref, ar_ref, csb_ref, cst_ref, csf_ref)
        x = xbc_ref[:, :D_INNER]
        xdt = x * dt_full
        dy_all = dy_ref[...]
        ddf_ref[...] += jnp.broadcast_to(jnp.sum(dy_all * x, axis=0, keepdims=True), ddf_ref.shape)
        causal = _iota((CHUNK, CHUNK), 0) >= _iota((CHUNK, CHUNK), 1)
        lo = _iota((CHUNK, CHUNK), 1) < SSM_P
        head_lane = _iota((CHUNK, HPAD), 1)
        head_row = _iota((HPAD, CHUNK), 0)
        dcs_heads = jnp.zeros((CHUNK, HPAD), F32)
        dcs_cols = jnp.zeros((HPAD, CHUNK), F32)
        for g in range(SSM_GROUPS):
            gs = slice(g * SSM_GW, (g + 1) * SSM_GW)
            b0 = D_INNER + g * SSM_N
            c0 = D_INNER + SSM_GROUPS * SSM_N + g * SSM_N
            bg = xbc_ref[:, b0:b0 + SSM_N]
            cg = xbc_ref[:, c0:c0 + SSM_N]
            ht = st_ref[:, gs]
            dht = dht_ref[:, gs]
            dyg = dy_all[:, gs]
            eg, fg, gg = e_full[:, gs], f_full[:, gs], gamma[:, gs]
            z = _dot(cg, ht)
            dz = dyg * eg
            dcg = _dot_nt(dz, ht)
            dht_new = _dot_tn(cg, dz) + gg * dht
            xf = xdt[:, gs] * fg
            dxf = _dot(bg, dht)
            dbg = _dot_nt(xf, dht)
            dff = dxf * xf
            dcsf_ref[:, gs] = dyg * eg * z - dff
            dcsl_ref[:, gs] = jnp.broadcast_to(
                jnp.sum(dff, axis=0, keepdims=True) + jnp.sum(dht * ht, axis=0, keepdims=True) * gg, (8, SSM_GW))
            cb = _dot_nt(cg, bg)
            dcb = jnp.zeros((CHUNK, CHUNK), F32)
            for jp in range(SSM_GW // CHUNK):
                j = g * (SSM_GW // CHUNK) + jp
                ps = slice(j * CHUNK, (j + 1) * CHUNK)
                x2 = xdt[:, ps]
                dy2 = dy_all[:, ps]
                dxh = []
                for hh in range(2):
                    h = 2 * j + hh
                    lam = _ssd_lambda(csb_ref, cst_ref, h, causal)
                    mh = cb * lam
                    dyh = jnp.where(lo, dy2, 0.0) if hh == 0 else jnp.where(lo, 0.0, dy2)
                    dm = _dot_nt(dyh, x2)
                    dcb = dcb + dm * lam
                    gm = dm * mh
                    dcs_heads = dcs_heads + jnp.where(head_lane == h, jnp.sum(gm, axis=1, keepdims=True), 0.0)
                    dcs_cols = dcs_cols + jnp.where(head_row == h, jnp.sum(gm, axis=0, keepdims=True), 0.0)
                    dxh.append(_dot_tn(mh, dy2))
                dxs_ref[:, ps] = jnp.where(lo, dxh[0], dxh[1]) + dxf[:, jp * CHUNK:(jp + 1) * CHUNK] * fg[:, jp * CHUNK:(jp + 1) * CHUNK]
            dxbc_ref[:, b0:b0 + SSM_N] = (dbg + _dot_tn(dcb, cg)).astype(dxbc_ref.dtype)
            dxbc_ref[:, c0:c0 + SSM_N] = (dcg + _dot(dcb, bg)).astype(dxbc_ref.dtype)
            dht_ref[:, gs] = dht_new
        dxs = dxs_ref[...]
        dcs_heads = dcs_heads - dcs_cols.T + _dot_sel(dcsf_ref[...], e64, ((1,), (1,)))
        dcs_last = _dot_sel(dcsl_ref[...], e64, ((1,), (1,)))
        dcs_heads = dcs_heads + jnp.where(_iota((CHUNK, HPAD), 0) == CHUNK - 1, dcs_last[0:1, :], 0.0)
        triu = _iota((CHUNK, CHUNK), 0) <= _iota((CHUNK, CHUNK), 1)
        dda = _sel_dot(triu, dcs_heads)
        ddt = dda * a_row + _dot_sel(dxs * x, e64, ((1,), (1,)))
        dxbc_ref[:, :D_INNER] = (dxs * dt_full + dy_all * df_ref[...]).astype(dxbc_ref.dtype)
        dalog_ref[...] += jnp.sum(dda * dt_c, axis=0, keepdims=True) * a_row
        ddt_raw = ddt * _sigmoid(dtc_ref[...] + br_ref[...])
        ddt_ref[...] = ddt_raw.astype(ddt_ref.dtype)
        dbias_ref[...] += jnp.sum(ddt_raw, axis=0, keepdims=True)

        @pl.when(step == nc - 1)
        def _():
            dd_ref[...] = _dot_sel(ddf_ref[...], e64, ((1,), (1,)))[0:1, :]

    vec = pl.BlockSpec((1, HPAD), lambda c: (0, 0))
    return pl.pallas_call(
        body, name=name, grid=(nc,),
        in_specs=[pl.BlockSpec((CHUNK, CONV_DIM), lambda c: (rev(c), 0)), pl.BlockSpec((CHUNK, HPAD), lambda c: (rev(c), 0))]
                 + _SSD_VEC_SPECS()
                 + [pl.BlockSpec((CHUNK, D_INNER), lambda c: (rev(c), 0)),
                    pl.BlockSpec((None, SSM_N, D_INNER), lambda c: (rev(c), 0, 0))],
        out_specs=[pl.BlockSpec((CHUNK, CONV_DIM), lambda c: (rev(c), 0)), pl.BlockSpec((CHUNK, HPAD), lambda c: (rev(c), 0)),
                   vec, vec, vec],
        out_shape=[jax.ShapeDtypeStruct((s, CONV_DIM), F32), jax.ShapeDtypeStruct((s, HPAD), _ACT),
                   jax.ShapeDtypeStruct((1, HPAD), F32), jax.ShapeDtypeStruct((1, HPAD), F32),
                   jax.ShapeDtypeStruct((1, HPAD), F32)],
        scratch_shapes=[pltpu.VMEM((SSM_N, D_INNER), F32), pltpu.VMEM((CHUNK, SSM_HEADS * CHUNK), F32),
                        pltpu.VMEM((HPAD, CHUNK), F32), pltpu.VMEM((CHUNK, D_INNER), F32),
                        pltpu.VMEM((8, D_INNER), F32), pltpu.VMEM((CHUNK, D_INNER), F32),
                        pltpu.VMEM((CHUNK, D_INNER), F32), pltpu.VMEM((8, D_INNER), F32)],
        compiler_params=_cp(VMEM_BIG))(xbc, dtc, bias_row, alog_row, dfull, dy, states)


def _gate_fwd(y, proj, gn, name, tm=512):
    s = y.shape[0]
    tm = min(tm, s)

    def body(y_ref, z_ref, gn_ref, o_ref):
        for g in range(SSM_GROUPS):
            gs = slice(g * SSM_GW, (g + 1) * SSM_GW)
            z = z_ref[:, gs]
            t = y_ref[:, gs] * (z * _sigmoid(z))
            r = lax.rsqrt(jnp.mean(t * t, axis=-1, keepdims=True) + EPS)
            o_ref[:, gs] = (t * r * gn_ref[:, gs]).astype(o_ref.dtype)

    row = pl.BlockSpec((tm, D_INNER), lambda i: (i, 0))
    return pl.pallas_call(
        body, name=name, grid=(s // tm,), in_specs=[row, row, pl.BlockSpec((1, D_INNER), lambda i: (0, 0))],
        out_specs=row, out_shape=jax.ShapeDtypeStruct((s, D_INNER + X_WIDTH), _ACT),
        compiler_params=_cp(VMEM_BIG))(y, proj, gn)


def _gate_bwd(y, proj, gn, dcat, name, tm=512):
    s = y.shape[0]
    tm = min(tm, s)

    def body(y_ref, z_ref, gn_ref, dm_ref, dy_ref, dz_ref, dgn_ref):
        @pl.when(pl.program_id(0) == 0)
        def _():
            dgn_ref[...] = jnp.zeros_like(dgn_ref)

        for g in range(SSM_GROUPS):
            gs = slice(g * SSM_GW, (g + 1) * SSM_GW)
            z = z_ref[:, gs]
            yv = y_ref[:, gs]
            sig = _sigmoid(z)
            sz = z * sig
            t = yv * sz
            r = lax.rsqrt(jnp.mean(t * t, axis=-1, keepdims=True) + EPS)
            th = t * r
            dm = dm_ref[:, gs].astype(F32)
            dmg = dm * gn_ref[:, gs]
            dt_ = r * (dmg - th * jnp.mean(dmg * th, axis=-1, keepdims=True))
            dgn_ref[:, gs] += jnp.sum(dm * th, axis=0, keepdims=True)
            dy_ref[:, gs] = dt_ * sz
            dz_ref[:, gs] = (dt_ * yv * (sig * (1.0 + z * (1.0 - sig)))).astype(dz_ref.dtype)

    row = pl.BlockSpec((tm, D_INNER), lambda i: (i, 0))
    vec = pl.BlockSpec((1, D_INNER), lambda i: (0, 0))
    return pl.pallas_call(
        body, name=name, grid=(s // tm,), in_specs=[row, row, vec, row], out_specs=[row, row, vec],
        out_shape=[jax.ShapeDtypeStruct((s, D_INNER), F32), jax.ShapeDtypeStruct((s, 6 * D_MODEL), _ACT),
                   jax.ShapeDtypeStruct((1, D_INNER), F32)], compiler_params=_cp(VMEM_BIG))(y, proj, gn, dcat)


def _block_of(kind, width):
    if kind == "col":
        return lambda ref, j: ref.at[:, :, pl.ds(pl.multiple_of(j * width, 128), width)]
    if kind == "row":
        return lambda ref, j: ref.at[:, pl.ds(pl.multiple_of(j * width, 8), width), :]
    return lambda ref, j: ref.at[j]


def _coords():
    return lax.axis_index("x"), lax.axis_index("y"), lax.axis_index("c")


def _rel_chip(x, y, k):
    return (1 - x if k & 1 else x), (1 - y if k & 2 else y)


def _all_gather_body(ins, outs, send_sems, recv_sems, local_sems, blocks):
    n = len(ins)
    x, y, c = _coords()
    sibling = (x, y, 1 - c)
    via = (x + (1 - c) * (1 - 2 * x), y + c * (1 - 2 * y))
    onto = (x + c * (1 - 2 * x), y + (1 - c) * (1 - 2 * y))

    def copy(t, k, chip, core, to, src=None):
        dst = blocks[t](outs[t], 4 * chip[0] + 2 * chip[1] + core)
        return pltpu.make_async_remote_copy(
            src_ref=dst if src is None else src, dst_ref=dst, send_sem=send_sems.at[t, k],
            recv_sem=recv_sems.at[t, k], device_id=to, device_id_type=MESH)

    started = []
    for t in range(n):
        mine = pltpu.make_async_copy(ins[t], blocks[t](outs[t], 4 * x + 2 * y + c), local_sems.at[t])
        mine.start()
        started.append(mine)
    sends = []
    for t in range(n):
        for k in range(3):
            px, py = _rel_chip(x, y, k)
            cp = copy(t, k, (x, y), c, (px, py, 1 - c if k == 0 else c), src=ins[t])
            cp.start()
            sends.append(cp)
    for t in range(n):
        for k in (1, 2):
            chip = _rel_chip(x, y, k)
            copy(t, k, chip, c, sibling).wait_recv()
            fwd = copy(t, 3 + k, chip, c, sibling)
            fwd.start()
            sends.append(fwd)
        hop = copy(t, 3, via, c, (*onto, c))
        hop.start()
        sends.append(hop)
    for t in range(n):
        diagonal = _rel_chip(x, y, 3)
        copy(t, 3, diagonal, c, sibling).wait_recv()
        fwd = copy(t, 6, diagonal, c, sibling)
        fwd.start()
        sends.append(fwd)
    for t in range(n):
        copy(t, 0, (x, y), 1 - c, sibling).wait_recv()
        for k in range(1, 4):
            copy(t, 3 + k, _rel_chip(x, y, k), 1 - c, sibling).wait_recv()
    for cp in sends:
        cp.wait_send()
    for mine in started:
        mine.wait()


def _handshake(peers):
    barrier = pltpu.get_barrier_semaphore()
    for peer in peers:
        pl.semaphore_signal(barrier, inc=1, device_id=peer, device_id_type=MESH)
    pl.semaphore_wait(barrier, len(peers))


def _gather_peers():
    x, y, c = _coords()
    return [(x, y, 1 - c)] + [(*_rel_chip(x, y, k), c) for k in (1, 2)]


SEQ_ID_GATHER, SEQ_ID_SIBLING, SEQ_ID_CHIPS = 1, 2, 3


def _sequencer_call(body, peers, operands, out_types, sems, name, collective_id, after=()):
    n_in, n_out, n_after = len(operands), len(out_types), len(after)

    def launch(*refs):
        _handshake(peers())
        body(refs[:n_in], refs[n_in + n_after:n_in + n_after + n_out], *refs[n_in + n_after + n_out:])

    return pl.kernel(
        launch, name=name, out_type=out_types, mesh=plsc.ScalarSubcoreMesh(axis_name="seq", num_cores=1),
        scratch_types=sems, compiler_params=pltpu.CompilerParams(collective_id=collective_id))(*operands, *after)


def _all_gather_seq(shards, layouts, name, after=()):
    n = len(shards)
    blocks = [_block_of(kind, width) for kind, width, _ in layouts]
    return _sequencer_call(
        lambda ins, outs, *sems: _all_gather_body(ins, outs, *sems, blocks), _gather_peers, shards,
        [jax.ShapeDtypeStruct(shape, sh.dtype) for sh, (_, _, shape) in zip(shards, layouts)],
        [pltpu.SemaphoreType.DMA((n, 7)), pltpu.SemaphoreType.DMA((n, 7)), pltpu.SemaphoreType.DMA((n,))],
        name, SEQ_ID_GATHER, after)


def _tie(small, after, name):
    del name
    return lax.optimization_barrier((small, *after))[0]


def _rs_to_sibling(grads, layouts, name, after=()):
    n = len(grads)
    blocks = [_block_of(kind, width) for kind, width, _ in layouts]

    def body(ins, outs, send_sems, recv_sems):
        x, y, c = _coords()
        sibling = (x, y, 1 - c)
        cps = []
        for t in range(n):
            for k in range(4):
                px, py = _rel_chip(x, y, k)
                cp = pltpu.make_async_remote_copy(
                    src_ref=blocks[t](ins[t], 4 * px + 2 * py + (1 - c)), dst_ref=outs[t].at[k],
                    send_sem=send_sems.at[t, k], recv_sem=recv_sems.at[t, k], device_id=sibling, device_id_type=MESH)
                cp.start()
                cps.append(cp)
        for cp in cps:
            cp.wait_recv()
        for cp in cps:
            cp.wait_send()

    def sibling_only():
        x, y, c = _coords()
        return [(x, y, 1 - c)]

    return _sequencer_call(
        body, sibling_only, grads,
        [jax.ShapeDtypeStruct((4,) + shape, g.dtype) for g, (_, _, shape) in zip(grads, layouts)],
        [pltpu.SemaphoreType.DMA((n, 4)), pltpu.SemaphoreType.DMA((n, 4))], name, SEQ_ID_SIBLING, after)


def _rs_chip_sum(grad, recv, layout, xyc, name):
    kind, width, shape = layout
    r, ccols = shape

    def src_index(k, xyc_ref):
        px = jnp.where(k % 2 == 1, 1 - xyc_ref[0], xyc_ref[0])
        py = jnp.where(k // 2 == 1, 1 - xyc_ref[1], xyc_ref[1])
        return 4 * px + 2 * py + xyc_ref[2]

    if kind == "col":
        g_spec = pl.BlockSpec((r, ccols), lambda k, s_: (0, src_index(k, s_)))
    elif kind == "row":
        g_spec = pl.BlockSpec((r, ccols), lambda k, s_: (src_index(k, s_), 0))
    else:
        g_spec = pl.BlockSpec((None, r, ccols), lambda k, s_: (src_index(k, s_), 0, 0))

    def body(xyc_ref, g_ref, r_ref, o_ref):
        o_ref[...] = (g_ref[...].astype(F32) + r_ref[...].astype(F32)).astype(o_ref.dtype)

    slot = pl.BlockSpec((None, r, ccols), lambda k, s_: (k, 0, 0))
    return pl.pallas_call(
        body, name=name,
        grid_spec=pltpu.PrefetchScalarGridSpec(num_scalar_prefetch=1, grid=(4,), in_specs=[g_spec, slot], out_specs=slot),
        out_shape=jax.ShapeDtypeStruct((4, r, ccols), grad.dtype), compiler_params=_cp(VMEM_BIG))(xyc, grad, recv)


def _rs_across_chips(parts, name):
    n = len(parts)

    def body(ins, outs, send_sems, recv_sems):
        x, y, c = _coords()
        cps = []
        for t in range(n):
            for k in range(1, 4):
                px, py = _rel_chip(x, y, k)
                cp = pltpu.make_async_remote_copy(
                    src_ref=ins[t].at[k], dst_ref=outs[t].at[k - 1], send_sem=send_sems.at[t, k - 1],
                    recv_sem=recv_sems.at[t, k - 1], device_id=(px, py, c), device_id_type=MESH)
                cp.start()
                cps.append(cp)
        for cp in cps:
            cp.wait_recv()
        for cp in cps:
            cp.wait_send()

    def other_chips():
        x, y, c = _coords()
        return [(*_rel_chip(x, y, k), c) for k in range(1, 4)]

    return _sequencer_call(
        body, other_chips, parts, [jax.ShapeDtypeStruct((3,) + p.shape[1:], p.dtype) for p in parts],
        [pltpu.SemaphoreType.DMA((n, 3)), pltpu.SemaphoreType.DMA((n, 3))], name, SEQ_ID_CHIPS)


def _adamw_math(w, g, m, v):
    m = ADAM_B1 * m + (1.0 - ADAM_B1) * g
    v = ADAM_B2 * v + (1.0 - ADAM_B2) * jnp.square(g)
    m_hat = m / (1.0 - ADAM_B1 ** ADAM_STEP)
    v_hat = v / (1.0 - ADAM_B2 ** ADAM_STEP)
    delta = -ADAM_LR * (m_hat / (jnp.sqrt(v_hat) + ADAM_EPS) + ADAM_WD * w)
    return delta, m, v


def _row_tile(rows, cap):
    best = None
    for cand in range(8, min(rows, cap) + 1, 8):
        if rows % cand == 0:
            best = cand
    assert best is not None, rows
    return best


def _adamw(w, m, v, parts, name, layer=None, prev=None, tr=256):
    r, ccols = w.shape[-2:]
    npart = len(parts)
    if r % 8 == 0:
        tr, tc = _row_tile(r, tr), ccols
        steps, at = r // tr, (lambda i: (i, 0))
    else:
        tr, tc = r, 256
        assert ccols % tc == 0
        steps, at = ccols // tc, (lambda i: (0, i))

    def spec(lead):
        if lead is None:
            return pl.BlockSpec((tr, tc), at)
        return pl.BlockSpec((None, tr, tc), lambda i: (lead,) + at(i))

    wspec = lambda: spec(layer)
    pspec = spec

    def body(*refs):
        w_ref, m_ref, v_ref = refs[:3]
        p_refs = refs[3:3 + npart]
        outs = refs[len(refs) - 4:]
        g = p_refs[0][...].astype(F32)
        for p_ref in p_refs[1:]:
            g = g + p_ref[...].astype(F32)
        delta, mn, vn = _adamw_math(w_ref[...], g, m_ref[...], v_ref[...])
        outs[0][...] = g
        outs[1][...] = delta
        outs[2][...] = mn
        outs[3][...] = vn

    operands = [w, m, v] + [p for p, _ in parts]
    in_specs = [wspec(), wspec(), wspec()] + [pspec(lead) for _, lead in parts]
    aliases = {}
    if prev is not None:
        for i, p in enumerate(prev):
            aliases[len(operands)] = i
            operands.append(p)
            in_specs.append(pl.BlockSpec(memory_space=pl.ANY))
    return pl.pallas_call(
        body, name=name, grid=(steps,), in_specs=in_specs, out_specs=[wspec()] * 4,
        out_shape=[jax.ShapeDtypeStruct(w.shape, F32)] * 4, input_output_aliases=aliases,
        compiler_params=_cp(VMEM_BIG))(*operands)


def _small_update(gathered, params, loss_all, me, name):
    n = len(gathered)
    shapes = [w.shape for w, _, _ in params]

    def body(me_ref, *refs):
        g_refs, loss_ref = refs[:n], refs[n]
        p_refs = refs[n + 1:n + 1 + 3 * n]
        o_refs = refs[n + 1 + 3 * n:]
        for i in range(n):
            r, c = shapes[i]
            if gathered[i].shape[2] == c:
                parts = [g_refs[i][j] for j in range(N_DEV)]
            else:
                off = pl.multiple_of(me_ref[0] * c, 128)
                parts = [g_refs[i][j, :, pl.ds(off, c)] for j in range(N_DEV)]
            g = functools.reduce(lambda a, b: a + b, parts)
            delta, mn, vn = _adamw_math(p_refs[3 * i][...], g, p_refs[3 * i + 1][...], p_refs[3 * i + 2][...])
            for k, val in enumerate((g, delta, mn, vn)):
                o_refs[4 * i + k][...] = val
        o_refs[4 * n][...] = functools.reduce(lambda a, b: a + b, [loss_ref[j] for j in range(N_DEV)])

    vmem = pl.BlockSpec(memory_space=pltpu.VMEM)
    flat_params = [a for p in params for a in p]
    outs = pl.pallas_call(
        body, name=name, in_specs=[pl.BlockSpec(memory_space=pltpu.SMEM)] + [vmem] * (n + 1 + 3 * n),
        out_specs=[vmem] * (4 * n + 1),
        out_shape=[jax.ShapeDtypeStruct(shp, F32) for shp in shapes for _ in range(4)] + [jax.ShapeDtypeStruct((1, 128), F32)],
        compiler_params=_cp(VMEM_BIG))(me, *gathered, loss_all, *flat_params)
    return [tuple(outs[4 * i:4 * i + 4]) for i in range(n)], outs[4 * n]


def _sum8(buf, name):
    _, r, ccols = buf.shape

    def body(b_ref, o_ref):
        acc = b_ref[0]
        for j in range(1, N_DEV):
            acc = acc + b_ref[j]
        o_ref[...] = acc

    tr = _row_tile(r, 256)
    return pl.pallas_call(
        body, name=name, grid=(r // tr,), in_specs=[pl.BlockSpec((N_DEV, tr, ccols), lambda i: (0, i, 0))],
        out_specs=pl.BlockSpec((tr, ccols), lambda i: (i, 0)), out_shape=jax.ShapeDtypeStruct((r, ccols), F32))(buf)


def _pack(arrays):
    pieces, layout, off = [], [], 0
    for a in arrays:
        n = a.size
        padded = -(-n // 1024) * 1024
        flat = a.reshape(-1).astype(F32)
        if padded != n:
            flat = jnp.pad(flat, (0, padded - n))
        pieces.append(flat.reshape(padded // 128, 128))
        layout.append((off, n, a.shape))
        off += padded // 128
    return jnp.concatenate(pieces, axis=0), layout


def _unpack(packed, layout):
    out = []
    for off, n, shape in layout:
        rows = -(-n // 1024) * 8
        out.append(packed[off:off + rows].reshape(-1)[:n].reshape(shape))
    return out


def kernel(x, mem, norm_mix, norm_ffn, mem_norm, w_kv, w_out, w_ffn1, w_ffn2, a_in, a_ln_g, a_ln_b, a_ws, a_bs, b_in, b_conv_w, b_conv_b, b_dt_bias, b_a_log, b_d, b_gnorm, final_norm, loss_target, m_norm_mix, m_norm_ffn, m_mem_norm, m_w_kv, m_w_out, m_w_ffn1, m_w_ffn2, m_a_in, m_a_ln_g, m_a_ln_b, m_a_ws, m_a_bs, m_b_in, m_b_conv_w, m_b_conv_b, m_b_dt_bias, m_b_a_log, m_b_d, m_b_gnorm, m_final_norm, v_norm_mix, v_norm_ffn, v_mem_norm, v_w_kv, v_w_out, v_w_ffn1, v_w_ffn2, v_a_in, v_a_ln_g, v_a_ln_b, v_a_ws, v_a_bs, v_b_in, v_b_conv_w, v_b_conv_b, v_b_dt_bias, v_b_a_log, v_b_d, v_b_gnorm, v_final_norm):
    s = x.shape[1]
    xs = x.reshape(s, D_MODEL)
    mems = mem.reshape(N_MEM, D_MODEL)
    target = loss_target.reshape(s, D_MODEL)
    ax, ay, ac = lax.axis_index("x"), lax.axis_index("y"), lax.axis_index("c")
    me = 4 * ax + 2 * ay + ac
    xyc = jnp.stack([ax, ay, ac]).astype(jnp.int32)

    b_cols = b_in.shape[2]
    act = lambda a: a.astype(_ACT)
    lay_f1, lay_f2 = ("col", 512, (1, D_MODEL, D_FF)), ("row", 512, (1, D_FF, D_MODEL))
    lay_out, lay_kv = ("row", 384, (1, 3 * D_MODEL, D_MODEL)), ("col", 256, (1, D_MODEL, 2 * X_WIDTH))
    small_w_pack = _pack([b_conv_w[0], b_conv_b[0], b_gnorm[0]])[0]
    (WA,) = _all_gather_seq([act(a_in)], [("col", 640, (1, D_MODEL, 5 * D_MODEL))], "ag_proj_a")
    wo0, wkv0 = _all_gather_seq([act(w_out[0:1]), act(w_kv[0:1])], [lay_out, lay_kv], "ag_out0")
    w1_0, w2_0 = _all_gather_seq([act(w_ffn1[0:1]), act(w_ffn2[0:1])], [lay_f1, lay_f2], "ag_ffn0")
    a0 = _rms_fwd(xs, norm_mix[0].reshape(1, -1), "mix_norm0")
    tr_b = lambda a: jnp.swapaxes(a, 1, 2)
    wbt_blk, small_w = _all_gather_seq(
        [act(tr_b(b_in)[0]), small_w_pack],
        [("blk", 0, (N_DEV, b_cols, D_MODEL)), ("blk", 0, (N_DEV, 32, 128))], "ag_proj_b", after=[a0])
    wo1, wkv1 = _all_gather_seq([act(w_out[1:2]), act(w_kv[1:2])], [lay_out, lay_kv], "ag_out1", after=[a0])
    w1_1, w2_1 = _all_gather_seq([act(w_ffn1[1:2]), act(w_ffn2[1:2])], [lay_f1, lay_f2], "ag_ffn1", after=[a0])
    W1, W2, WO, WKV = [w1_0, w1_1], [w2_0, w2_1], [wo0, wo1], [wkv0, wkv1]
    dt0 = D_INNER + CONV_DIM

    row = lambda a: a.reshape(1, -1)
    nmix = [row(norm_mix[0]), row(norm_mix[1])]
    nffn = [row(norm_ffn[0]), row(norm_ffn[1])]
    nmem = [row(mem_norm[0]), row(mem_norm[1])]
    fin = row(final_norm)
    lng, lnb = a_ln_g.reshape(1, D_INNER), a_ln_b.reshape(1, D_INNER)
    ws = a_ws[0]
    bs3 = a_bs[0].reshape(A_GROUPS, CHUNK, 1)
    pad_h = lambda a: jnp.pad(a.reshape(-1), (0, HPAD - SSM_HEADS))
    bias_row = pad_h(b_dt_bias).reshape(1, HPAD)
    alog_row = pad_h(b_a_log).reshape(1, HPAD)
    dfull = jnp.repeat(b_d.reshape(-1), SSM_P).reshape(1, D_INNER)

    kvs, mns = [None, None], [None, None]

    def mem_kv(i, after=None):
        gain = nmem[i] if after is None else _tie(nmem[i], after, f"tie_mem{i}")
        mns[i] = _rms_fwd(mems, gain, f"mem_norm{i}")
        kvs[i] = _mm(mns[i], WKV[i], m=N_MEM, n=2 * X_WIDTH, k=D_MODEL, b_at=(0, 0, 0), out_dtype=_ACT, name=f"kv{i}")

    def ffn_fwd(h, i):
        f = _rms_fwd(h, nffn[i], f"ffn_norm{i}")
        p = _mm(f, W1[i], m=s, n=D_FF, k=D_MODEL, b_at=(0, 0, 0), out_dtype=_ACT, name=f"ffn_up{i}")
        hn = _mm(p, W2[i], m=s, n=D_MODEL, k=D_FF, b_at=(0, 0, 0), a_pro="relu2", add=h, name=f"ffn_down{i}")
        return f, p, hn

    def out_proj(h, cat, i):
        return _mm(cat, WO[i], m=s, n=D_MODEL, k=3 * D_MODEL, b_at=(0, 0, 0), add=h, name=f"out_proj{i}")

    proj_a = _mm(a0, WA, m=s, n=5 * D_MODEL, k=D_MODEL, b_at=(0, 0, 0), name="proj_a")
    mem_kv(0, after=[proj_a])
    cat_a = _gmlp_fwd(proj_a, lng, lnb, ws, bs3, "gmlp_fwd")
    cat_a = _attn_fwd(proj_a, 4, kvs[0], cat_a, "attn_fwd0")
    h1 = out_proj(xs, cat_a, 0)
    f0, p0, h2 = ffn_fwd(h1, 0)

    wbt_blk, small_w, _ = lax.optimization_barrier((wbt_blk, small_w, p0))
    wbt_full = wbt_blk.reshape(N_DEV * b_cols, D_MODEL)
    WBT = jnp.concatenate([wbt_full[:dt0], wbt_full[dt0 + SSM_HEADS:]], axis=0)
    WBDT = jnp.pad(wbt_full[dt0:dt0 + SSM_HEADS], ((0, HPAD - SSM_HEADS), (0, 0)))
    cw_sh, cb_sh, gn_sh = 4 * 384, 384, 256
    sw = small_w.reshape(N_DEV, 32 * 128)
    conv_w = jnp.transpose(sw[:, :cw_sh].reshape(N_DEV, CONV_K, 384), (1, 0, 2)).reshape(CONV_K, CONV_DIM)
    conv_b = sw[:, 2048:2048 + cb_sh].reshape(1, CONV_DIM)
    gnorm = sw[:, 3072:3072 + gn_sh].reshape(1, D_INNER)

    a1 = _rms_fwd(h2, nmix[1], "mix_norm1")
    proj_b = _mm(a1, WBT, m=s, n=6 * D_MODEL, k=D_MODEL, tb=True, name="proj_b")
    dt_raw = _mm(a1, WBDT, m=s, n=HPAD, k=D_MODEL, tb=True, name="proj_dt")
    xbc = _conv_fwd(proj_b, conv_w, conv_b, "conv_fwd")
    y_ssd, states = _ssd_fwd(xbc, dt_raw, bias_row, alog_row, dfull, "ssd_fwd")
    cat_b = _gate_fwd(y_ssd, proj_b, gnorm, "gate_fwd")
    mem_kv(1, after=[cat_b])
    cat_b = _attn_fwd(proj_b, 5, kvs[1], cat_b, "attn_fwd1")
    h3 = out_proj(h2, cat_b, 1)
    f1, p1, h4 = ffn_fwd(h3, 1)

    loss_part, dh, dh_act, d_fin = _loss_head(h4, fin, target, "loss_head")

    g_f1, g_f2, g_out, g_kv = [None, None], [None, None], [None, None], [None, None]
    d_nffn, d_nmix, d_nmem = [None, None], [None, None], [None, None]

    def ffn_bwd(dh, dh_act, h_in, f, p, i, after=(), after_last=()):
        dp = _mm(dh_act, W2[i], m=s, n=D_FF, k=D_MODEL, tb=True, b_at=(0, 0, 0), epi_p=p, out_dtype=_ACT, name=f"ffn_down_dx{i}")
        g_f2[i] = _mm(p, dh_act, m=D_FF, n=D_MODEL, k=s, ta=True, a_pro="relu2", out_dtype=_ACT, name=f"ffn_down_dw{i}")
        g_f1[i] = _mm(f, dp, m=D_MODEL, n=D_FF, k=s, ta=True, out_dtype=_ACT, name=f"ffn_up_dw{i}")
        df = _mm(dp, W1[i], m=s, n=D_MODEL, k=D_FF, tb=True, b_at=(0, 0, 0), after=after, name=f"ffn_up_dx{i}")
        gain = _tie(nffn[i], after_last, f"tie_ffn_norm{i}") if after_last else nffn[i]
        dh_in, dh_in_act, d_nffn[i] = _rms_bwd(h_in, gain, df, dh, f"ffn_norm_bwd{i}")
        return dh_in, dh_in_act

    def out_bwd(dh_act, cat, i):
        dcat = _mm(dh_act, WO[i], m=s, n=3 * D_MODEL, k=D_MODEL, tb=True, b_at=(0, 0, 0), out_dtype=_ACT, name=f"out_dx{i}")
        g_out[i] = _mm(cat, dh_act, m=3 * D_MODEL, n=D_MODEL, k=s, ta=True, out_dtype=_ACT, name=f"out_dw{i}")
        return dcat

    def mem_bwd(dkv, i):
        g_kv[i] = _mm(mns[i], dkv, m=D_MODEL, n=2 * X_WIDTH, k=N_MEM, ta=True, out_dtype=_ACT, name=f"kv_dw{i}")
        dmn = _mm(dkv, WKV[i], m=N_MEM, n=D_MODEL, k=2 * X_WIDTH, tb=True, b_at=(0, 0, 0), name=f"kv_dx{i}")
        _, _, d_nmem[i] = _rms_bwd(mems, nmem[i], dmn, None, f"mem_norm_bwd{i}")

    lay_g = {"f1": ("col", 512, (D_MODEL, 512)), "f2": ("row", 512, (512, D_MODEL)), "out": ("row", 384, (384, D_MODEL)),
             "kv": ("col", 256, (D_MODEL, 256)), "a": ("col", 640, (D_MODEL, 640)), "b": ("blk", 0, (b_cols, D_MODEL))}
    reduced = {}

    def reduce_scatter(group, tag, after=(), sums_after=()):
        grads3, lays3 = [], []
        for fam, _, g in group:
            kind, width, shape = lay_g[fam]
            grads3.append(g if kind == "blk" else g.reshape((1,) + g.shape))
            lays3.append((kind, width, shape if kind == "blk" else (1,) + shape))
        recv1 = _rs_to_sibling(grads3, lays3, f"rs_sibling_{tag}", after)
        if sums_after:
            recv1 = lax.optimization_barrier((tuple(recv1), tuple(sums_after)))[0]
        parts = [_rs_chip_sum(g, recv1[t].reshape((4,) + lay_g[fam][2]), lay_g[fam], xyc, f"rs_chip_sum_{fam}{i}")
                 for t, (fam, i, g) in enumerate(group)]
        recv2 = _rs_across_chips(parts, f"rs_chips_{tag}")
        for (fam, i, _), p, r2 in zip(group, parts, recv2):
            reduced[fam, i] = (p, r2)
        return parts, recv2

    dh3, dh3_act = ffn_bwd(dh, dh_act, h3, f1, p1, 1)
    dcat_b = out_bwd(dh3_act, cat_b, 1)
    sums, got_ffn1 = reduce_scatter([("f1", 1, g_f1[1]), ("f2", 1, g_f2[1]), ("out", 1, g_out[1])], "ffn1", sums_after=[dcat_b])
    dy_ssd, dproj_b, d_gnorm = _gate_bwd(y_ssd, proj_b, gnorm, dcat_b, "gate_bwd")
    dproj_b, dkv_b = _attn_bwd(proj_b, 5, kvs[1], dcat_b, dproj_b, "attn_bwd1")
    mem_bwd(dkv_b, 1)
    dxbc, ddt_raw, d_alog, d_dskip, d_dtbias = _ssd_bwd(
        xbc, dt_raw, _tie(bias_row, sums, "tie_ffn1"), alog_row, dfull, dy_ssd, states, "ssd_bwd")
    dproj_b, d_convw, d_convb = _conv_bwd(proj_b, conv_w, _tie(conv_b, got_ffn1, "tie_got_ffn1"), dxbc, dproj_b, "conv_bwd")
    gb = _mm(dproj_b, a1, m=6 * D_MODEL, n=D_MODEL, k=s, ta=True, out_dtype=_ACT, name="proj_b_dw")
    gb_dt = _mm(ddt_raw, a1, m=HPAD, n=D_MODEL, k=s, ta=True, out_dtype=_ACT, name="proj_b_dw_dt")
    gb_full = jnp.concatenate([gb[:dt0], gb_dt[:SSM_HEADS], gb[dt0:]], axis=0)
    gb_blk = gb_full.reshape(N_DEV, b_cols, D_MODEL)
    da1 = _mm(dproj_b, WBT, m=s, n=D_MODEL, k=6 * D_MODEL, name="proj_b_dx")
    sums, got_mix1 = reduce_scatter([("kv", 1, g_kv[1]), ("b", 0, gb_blk)], "mix1", sums_after=[da1])
    da1 = _mm(ddt_raw, WBDT, m=s, n=D_MODEL, k=HPAD, add=da1, name="proj_b_dx_dt")
    dh2, dh2_act, d_nmix[1] = _rms_bwd(h2, _tie(nmix[1], sums, "tie_mix1"), da1, dh3, "mix_norm_bwd1")

    dh1, dh1_act = ffn_bwd(dh2, dh2_act, h1, f0, p0, 0, after=got_ffn1, after_last=got_mix1)
    dcat_a = out_bwd(dh1_act, cat_a, 0)
    sums, got_ffn0 = reduce_scatter([("f1", 0, g_f1[0]), ("f2", 0, g_f2[0]), ("out", 0, g_out[0])], "ffn0", sums_after=[dcat_a])
    dproj_a, d_ws, d_bs3, d_lng, d_lnb = _gmlp_bwd(proj_a, dcat_a, _tie(lng, sums, "tie_ffn0"), lnb, ws, bs3, "gmlp_bwd")
    dproj_a, dkv_a = _attn_bwd(proj_a, 4, kvs[0], dcat_a, dproj_a, "attn_bwd0")
    mem_bwd(dkv_a, 0)

    def big_update(w, m, v, fam, nlayer):
        res = None
        for i in range(nlayer):
            part, recv2 = reduced[fam, i]
            plist = [(part, 0), (recv2, 0), (recv2, 1), (recv2, 2)]
            res = _adamw(w, m, v, plist, f"adamw_{fam}{i}", layer=i, prev=res)
        return res

    da0 = _mm(dproj_a, WA, m=s, n=D_MODEL, k=5 * D_MODEL, tb=True, b_at=(0, 0, 0), name="proj_a_dx")
    grad_x, _, d_nmix[0] = _rms_bwd(xs, nmix[0], da0, dh1, "mix_norm_bwd0")
    ga = _mm(a0, dproj_a, m=D_MODEL, n=5 * D_MODEL, k=s, ta=True, out_dtype=_ACT, after=[grad_x], name="proj_a_dw")
    r_b = big_update(tr_b(b_in), tr_b(m_b_in), tr_b(v_b_in), "b", 1)
    reduce_scatter([("kv", 0, g_kv[0]), ("a", 0, ga)], "mix0", after=got_ffn0, sums_after=r_b)
    r_b = [tr_b(o) for o in r_b]

    small_names = ["norm_mix", "norm_ffn", "mem_norm", "a_ln_g", "a_ln_b", "a_ws", "a_bs", "b_dt_bias", "b_a_log", "b_d",
                   "final_norm", "b_conv_w", "b_conv_b", "b_gnorm"]
    small_grads = [jnp.concatenate(d_nmix, axis=0), jnp.concatenate(d_nffn, axis=0), jnp.concatenate(d_nmem, axis=0),
                   d_lng, d_lnb, d_ws.reshape(A_GROUPS * CHUNK, CHUNK), d_bs3.reshape(A_GROUPS, CHUNK),
                   d_dtbias[:, :SSM_HEADS], d_alog[:, :SSM_HEADS], d_dskip[:, :SSM_HEADS], d_fin,
                   d_convw, d_convb, d_gnorm]
    small_2d = [(2, D_MODEL)] * 3 + [(1, D_INNER)] * 2 + [(A_GROUPS * CHUNK, CHUNK), (A_GROUPS, CHUNK)] + [(1, SSM_HEADS)] * 3 \
        + [(1, D_MODEL), (CONV_K, 384), (1, 384), (1, 256)]
    gathered = _all_gather_seq(
        small_grads + [loss_part], [("blk", 0, (N_DEV,) + g.shape) for g in small_grads + [loss_part]], "ag_small_grads")

    r_f1 = big_update(w_ffn1, m_w_ffn1, v_w_ffn1, "f1", 2)
    r_f2 = big_update(w_ffn2, m_w_ffn2, v_w_ffn2, "f2", 2)
    r_out = big_update(w_out, m_w_out, v_w_out, "out", 2)
    r_kv = big_update(w_kv, m_w_kv, v_w_kv, "kv", 2)
    r_a = big_update(a_in, m_a_in, v_a_in, "a", 1)

    small_w = [norm_mix, norm_ffn, mem_norm, a_ln_g, a_ln_b, a_ws, a_bs, b_dt_bias, b_a_log, b_d, final_norm,
               b_conv_w, b_conv_b, b_gnorm]
    small_m = [m_norm_mix, m_norm_ffn, m_mem_norm, m_a_ln_g, m_a_ln_b, m_a_ws, m_a_bs, m_b_dt_bias, m_b_a_log, m_b_d,
               m_final_norm, m_b_conv_w, m_b_conv_b, m_b_gnorm]
    small_v = [v_norm_mix, v_norm_ffn, v_mem_norm, v_a_ln_g, v_a_ln_b, v_a_ws, v_a_bs, v_b_dt_bias, v_b_a_log, v_b_d,
               v_final_norm, v_b_conv_w, v_b_conv_b, v_b_gnorm]
    params = [tuple(a.reshape(shp) for a in wmv) for shp, wmv in zip(small_2d, zip(small_w, small_m, small_v))]
    loss_all = _tie(gathered[-1], [r_a[0], r_kv[0]], "tie_small")
    small_res, loss_sum = _small_update(gathered[:-1], params, loss_all, me.astype(jnp.int32).reshape(1), "adamw_small")
    loss = loss_sum[0, 0]

    names = ["norm_mix", "norm_ffn", "mem_norm", "w_kv", "w_out", "w_ffn1", "w_ffn2", "a_in", "a_ln_g", "a_ln_b", "a_ws",
             "a_bs", "b_in", "b_conv_w", "b_conv_b", "b_dt_bias", "b_a_log", "b_d", "b_gnorm", "final_norm"]
    big = {"w_kv": r_kv, "w_out": r_out, "w_ffn1": r_f1, "w_ffn2": r_f2, "a_in": r_a, "b_in": r_b}
    outs = [loss, grad_x.reshape(x.shape)]
    for kind in range(4):
        for nm in names:
            if nm in big:
                outs.append(big[nm][kind])
            else:
                i = small_names.index(nm)
                outs.append(small_res[i][kind].reshape(small_w[i].shape))
    return tuple(outs)
```

```python
import functools
import math

import jax
import jax.numpy as jnp
from jax import lax
from jax.experimental import pallas as pl
from jax.experimental.pallas import tpu as pltpu
from jax.experimental.pallas import tpu_sc as plsc

F32 = jnp.float32
_MXU = jnp.bfloat16
_ACT = jnp.bfloat16

D_MODEL = 1024
CHUNK = 128
N_MEM = 256
D_INNER = 2048
A_GROUPS = 8
A_GW = D_INNER // A_GROUPS
SSM_HEADS = 32
SSM_P = 64
SSM_GROUPS = 4
SSM_GW = D_INNER // SSM_GROUPS
SSM_N = 128
CONV_K = 4
CONV_DIM = 3072
X_HEADS = 4
X_HD = 256
X_WIDTH = 1024
D_FF = 4096
EPS = 1e-6
HPAD = 128
N_DEV = 8

ADAM_LR = 0.001
ADAM_B1 = 0.9
ADAM_B2 = 0.999
ADAM_EPS = 1e-08
ADAM_WD = 0.01
ADAM_STEP = 10

VMEM_BIG = 56 * 1024 * 1024
MESH = pl.DeviceIdType.MESH


def _cp(vmem=None):
    if vmem is None:
        return pltpu.CompilerParams()
    return pltpu.CompilerParams(vmem_limit_bytes=vmem)


def _dot(a, b, dims=((1,), (0,))):
    return lax.dot_general(a.astype(_MXU), b.astype(_MXU), (dims, ((), ())), preferred_element_type=F32)


def _dot_nt(a, b):
    return _dot(a, b, ((1,), (1,)))


def _dot_tn(a, b):
    return _dot(a, b, ((0,), (0,)))


def _split3(x):
    x1 = x.astype(jnp.bfloat16)
    r = x - x1.astype(F32)
    x2 = r.astype(jnp.bfloat16)
    x3 = (r - x2.astype(F32)).astype(jnp.bfloat16)
    return x1, x2, x3


def _dot_sel(x, sel, dims=((1,), (0,)), terms=2):
    sel = sel.astype(jnp.bfloat16)
    parts = [lax.dot_general(t, sel, (dims, ((), ())), preferred_element_type=F32) for t in _split3(x)[:terms]]
    return functools.reduce(lambda a, b: a + b, parts)


def _sel_dot(sel, x, dims=((1,), (0,))):
    sel = sel.astype(jnp.bfloat16)
    parts = [lax.dot_general(sel, t, (dims, ((), ())), preferred_element_type=F32) for t in _split3(x)]
    return (parts[0] + parts[1]) + parts[2]


def _sigmoid(x):
    return 1.0 / (1.0 + jnp.exp(-x))


def _gelu(x):
    return 0.5 * x * (1.0 + lax.erf(x * (1.0 / math.sqrt(2.0))))


def _gelu_grad(x):
    return 0.5 * (1.0 + lax.erf(x * (1.0 / math.sqrt(2.0)))) + x * jnp.exp(-0.5 * x * x) * (1.0 / math.sqrt(2.0 * math.pi))


def _softplus(x):
    return jnp.maximum(x, 0.0) + jnp.log1p(jnp.exp(-jnp.abs(x)))


def _iota(shape, dim):
    return lax.broadcasted_iota(jnp.int32, shape, dim)


MM_VMEM_BUDGET = 40 * 1024 * 1024
HBM_BYTES_PER_S = 2.5e12
GRID_STEP_S = 0.35e-6
VMEM_ACC_BYTES_PER_S = 6e12


def _divisors(dim, unit):
    out = [d for d in range(unit, min(dim, 2048) + 1, unit) if dim % d == 0]
    return out if out else [dim]


def _mm_tiles(m, n, k, sa, sb, s_mn, a_pro, offsets):
    best = None
    (a_r0, a_c0, ta), (b_r0, b_c0, tb), (o_r0, o_c0) = offsets
    for tm in _divisors(m, 128):
        for tn in _divisors(n, 128):
            for tk in [k // d for d in (1, 2, 3, 4, 6, 8) if k % d == 0 and (k // d) % 128 == 0]:
                a_t = (tk, tm) if ta else (tm, tk)
                b_t = (tn, tk) if tb else (tk, tn)
                if a_r0 % a_t[0] or a_c0 % a_t[1] or b_r0 % b_t[0] or b_c0 % b_t[1] or o_r0 % tm or o_c0 % tn:
                    continue
                nk = k // tk
                vmem = 2 * (tm * tk * sa + tk * tn * sb + tm * tn * s_mn) + tm * tn * 4 * (2 if nk > 1 else 1)
                if a_pro or sa == 4:
                    vmem += tm * tk * 6
                if sb == 4:
                    vmem += tk * tn * 2
                if vmem > MM_VMEM_BUDGET:
                    continue
                gi, gj = m // tm, n // tn
                for j_inner in (True, False):
                    if nk > 1:
                        traffic = gj * m * k * sa + gi * k * n * sb
                    elif j_inner:
                        traffic = m * k * sa + gi * k * n * sb
                    else:
                        traffic = gj * m * k * sa + k * n * sb
                    traffic += m * n * s_mn + (tm * tk * sa + tk * tn * sb)
                    cost = traffic / HBM_BYTES_PER_S + gi * gj * nk * GRID_STEP_S
                    if nk > 1:
                        cost += m * n * 8 * nk / VMEM_ACC_BYTES_PER_S
                    if best is None or cost < best[0]:
                        best = (cost, tm, tn, tk, j_inner)
    assert best is not None, (m, n, k)
    return best[1:]


def _mm(a, b, *, m, n, k, name, ta=False, tb=False, a_at=(None, 0, 0), b_at=(None, 0, 0),
        out_dtype=F32, add=None, epi_p=None, epi_at=(None, 0, 0), out=None, out_at=(None, 0, 0),
        out_full=None, a_pro=None, after=()):
    s_mn =jnp.dtype(out.dtype if out is not None else out_dtype).itemsize
    s_mn += add.dtype.itemsize if add is not None else 0
    s_mn += epi_p.dtype.itemsize if epi_p is not None else 0
    tm, tn, tk, j_inner = _mm_tiles(m, n, k, a.dtype.itemsize, b.dtype.itemsize, s_mn, a_pro is not None,
                                    ((a_at[1], a_at[2], ta), (b_at[1], b_at[2], tb), (out_at[1], out_at[2])))
    nk = k // tk

    def spec(at, tr, tc, rsel, csel):
        lead, r0, c0 = at
        assert r0 % tr == 0 and c0 % tc == 0, (name, at, tr, tc)
        rb, cb = r0 // tr, c0 // tc
        if lead is None:
            return pl.BlockSpec((tr, tc), lambda g0, g1, kk: (rb + rsel(g0, g1, kk), cb + csel(g0, g1, kk)))
        return pl.BlockSpec((None, tr, tc), lambda g0, g1, kk: (lead, rb + rsel(g0, g1, kk), cb + csel(g0, g1, kk)))

    gi = (lambda g0, g1, kk: g0) if j_inner else (lambda g0, g1, kk: g1)
    gj = (lambda g0, g1, kk: g1) if j_inner else (lambda g0, g1, kk: g0)
    gk = lambda g0, g1, kk: kk
    a_spec = spec(a_at, tk, tm, gk, gi) if ta else spec(a_at, tm, tk, gi, gk)
    b_spec = spec(b_at, tn, tk, gj, gk) if tb else spec(b_at, tk, tn, gk, gj)
    dims = ((0,), (0,)) if ta else (((1,), (1,)) if tb else ((1,), (0,)))
    assert not (ta and tb)

    operands, in_specs = [a, b], [a_spec, b_spec]
    if add is not None:
        operands.append(add)
        in_specs.append(spec((None, 0, 0), tm, tn, gi, gj))
    if epi_p is not None:
        operands.append(epi_p)
        in_specs.append(spec(epi_at, tm, tn, gi, gj))
    aliases = {}
    if out is not None:
        aliases = {len(operands): 0}
        operands.append(out)
        in_specs.append(pl.BlockSpec(memory_space=pl.ANY))
        out_struct = jax.ShapeDtypeStruct(out.shape, out.dtype)
        out_dtype = out.dtype
    else:
        out_struct = jax.ShapeDtypeStruct(out_full if out_full is not None else (m, n), out_dtype)
    has_add, has_epi = add is not None, epi_p is not None
    n_skip = (1 if out is not None else 0) + len(after)
    operands += list(after)
    in_specs += [pl.BlockSpec(memory_space=pl.ANY)] * len(after)

    def body(*refs):
        a_ref, b_ref = refs[0], refs[1]
        pos = 2
        add_ref = epi_ref = None
        if has_add:
            add_ref = refs[pos]
            pos += 1
        if has_epi:
            epi_ref = refs[pos]
            pos += 1
        pos += n_skip
        o_ref = refs[pos]

        def finish(r):
            if has_add:
                r = r + add_ref[...].astype(F32)
            if has_epi:
                r = r * (2.0 * jnp.maximum(epi_ref[...].astype(F32), 0.0))
            o_ref[...] = r.astype(o_ref.dtype)

        av = a_ref[...]
        if a_pro == "relu2":
            av = jnp.square(jnp.maximum(av.astype(F32), 0.0))
        part = _dot(av, b_ref[...], dims)
        if nk == 1:
            finish(part)
        else:
            acc_ref = refs[pos + 1]
            kk = pl.program_id(2)

            @pl.when(kk == 0)
            def _():
                acc_ref[...] = part

            @pl.when(kk > 0)
            def _():
                acc_ref[...] += part

            @pl.when(kk == nk - 1)
            def _():
                finish(acc_ref[...])

    grid = (m // tm, n // tn, nk) if j_inner else (n // tn, m // tm, nk)
    return pl.pallas_call(
        body, name=name, grid=grid, in_specs=in_specs,
        out_specs=spec(out_at, tm, tn, gi, gj), out_shape=out_struct,
        scratch_shapes=[pltpu.VMEM((tm, tn), F32)] if nk > 1 else [], input_output_aliases=aliases,
        compiler_params=_cp(VMEM_BIG))(*operands)


def _rms_fwd(x, g, name, tm=1024):
    s, d = x.shape
    tm = min(tm, s)

    def body(x_ref, g_ref, o_ref):
        xv = x_ref[...]
        r = lax.rsqrt(jnp.mean(xv * xv, axis=-1, keepdims=True) + EPS)
        o_ref[...] = (xv * r * g_ref[...]).astype(o_ref.dtype)

    return pl.pallas_call(
        body, name=name, grid=(s // tm,),
        in_specs=[pl.BlockSpec((tm, d), lambda i: (i, 0)), pl.BlockSpec((1, d), lambda i: (0, 0))],
        out_specs=pl.BlockSpec((tm, d), lambda i: (i, 0)),
        out_shape=jax.ShapeDtypeStruct((s, d), _ACT), compiler_params=_cp(VMEM_BIG))(x, g)


def _rms_bwd(x, g, dy, dres, name, tm=512):
    s, d = x.shape
    tm = min(tm, s)
    has_res = dres is not None

    def body(*refs):
        if has_res:
            x_ref, g_ref, dy_ref, dres_ref, dx_ref, dxa_ref, dg_ref = refs
        else:
            x_ref, g_ref, dy_ref, dx_ref, dxa_ref, dg_ref = refs

        @pl.when(pl.program_id(0) == 0)
        def _():
            dg_ref[...] = jnp.zeros_like(dg_ref)

        xv = x_ref[...]
        dyv = dy_ref[...].astype(F32)
        r = lax.rsqrt(jnp.mean(xv * xv, axis=-1, keepdims=True) + EPS)
        xh = xv * r
        dyg = dyv * g_ref[...]
        dx = r * (dyg - xh * jnp.mean(dyg * xh, axis=-1, keepdims=True))
        if has_res:
            dx = dx + dres_ref[...]
        dx_ref[...] = dx
        dxa_ref[...] = dx.astype(dxa_ref.dtype)
        dg_ref[...] += jnp.sum(dyv * xh, axis=0, keepdims=True)

    row = pl.BlockSpec((tm, d), lambda i: (i, 0))
    vec = pl.BlockSpec((1, d), lambda i: (0, 0))
    in_specs = [row, vec, row] + ([row] if has_res else [])
    operands = [x, g, dy] + ([dres] if has_res else [])
    return pl.pallas_call(
        body, name=name, grid=(s // tm,), in_specs=in_specs, out_specs=[row, row, vec],
        out_shape=[jax.ShapeDtypeStruct((s, d), F32), jax.ShapeDtypeStruct((s, d), _ACT),
                   jax.ShapeDtypeStruct((1, d), F32)], compiler_params=_cp(VMEM_BIG))(*operands)


def _loss_head(h, g, target, name, tm=512):
    s, d = h.shape
    tm = min(tm, s)

    def body(h_ref, g_ref, t_ref, loss_ref, dh_ref, dha_ref, dg_ref):
        @pl.when(pl.program_id(0) == 0)
        def _():
            dg_ref[...] = jnp.zeros_like(dg_ref)
            loss_ref[...] = jnp.zeros_like(loss_ref)

        xv = h_ref[...]
        r = lax.rsqrt(jnp.mean(xv * xv, axis=-1, keepdims=True) + EPS)
        xh = xv * r
        err = xh * g_ref[...] - t_ref[...]
        loss_ref[...] += jnp.full(loss_ref.shape, 0.5 * jnp.sum(jnp.mean(err * err, axis=-1, keepdims=True)), F32)
        dyv = err * (1.0 / d)
        dyg = dyv * g_ref[...]
        dh = r * (dyg - xh * jnp.mean(dyg * xh, axis=-1, keepdims=True))
        dh_ref[...] = dh
        dha_ref[...] = dh.astype(dha_ref.dtype)
        dg_ref[...] += jnp.sum(dyv * xh, axis=0, keepdims=True)

    row = pl.BlockSpec((tm, d), lambda i: (i, 0))
    vec = pl.BlockSpec((1, d), lambda i: (0, 0))
    return pl.pallas_call(
        body, name=name, grid=(s // tm,), in_specs=[row, vec, row],
        out_specs=[pl.BlockSpec((1, 128), lambda i: (0, 0)), row, row, vec],
        out_shape=[jax.ShapeDtypeStruct((1, 128), F32), jax.ShapeDtypeStruct((s, d), F32),
                   jax.ShapeDtypeStruct((s, d), _ACT), jax.ShapeDtypeStruct((1, d), F32)],
        compiler_params=_cp(VMEM_BIG))(h, g, target)


def _gmlp_parts(pu, pv, lng, lnb):
    u = _gelu(pu)
    v = _gelu(pv)
    mu = jnp.mean(v, axis=-1, keepdims=True)
    vc = v - mu
    rstd = lax.rsqrt(jnp.mean(vc * vc, axis=-1, keepdims=True) + EPS)
    xhat = vc * rstd
    vn = xhat * lng + lnb
    return u, xhat, rstd, vn


def _gmlp_fwd(proj, lng, lnb, ws, bs3, name):
    s = proj.shape[0]

    def body(pu_ref, pv_ref, lng_ref, lnb_ref, ws_ref, bs_ref, o_ref):
        u, _, _, vn = _gmlp_parts(pu_ref[...], pv_ref[...], lng_ref[...], lnb_ref[...])
        causal = _iota((CHUNK, CHUNK), 0) >= _iota((CHUNK, CHUNK), 1)
        for g in range(A_GROUPS):
            sl = slice(g * A_GW, (g + 1) * A_GW)
            w = jnp.where(causal, ws_ref[g], 0.0)
            sv = _dot(w, vn[:, sl]) + bs_ref[g]
            o_ref[:, sl] = (u[:, sl] * sv).astype(o_ref.dtype)

    full = lambda shape: pl.BlockSpec(shape, lambda c: (0,) * len(shape))
    return pl.pallas_call(
        body, name=name, grid=(s // CHUNK,),
        in_specs=[pl.BlockSpec((CHUNK, D_INNER), lambda c: (c, 0)), pl.BlockSpec((CHUNK, D_INNER), lambda c: (c, 1)),
                  full((1, D_INNER)), full((1, D_INNER)), full((A_GROUPS, CHUNK, CHUNK)), full((A_GROUPS, CHUNK, 1))],
        out_specs=pl.BlockSpec((CHUNK, D_INNER), lambda c: (c, 0)),
        out_shape=jax.ShapeDtypeStruct((s, D_INNER + X_WIDTH), _ACT), compiler_params=_cp(VMEM_BIG))(proj, proj, lng, lnb, ws, bs3)


def _gmlp_bwd(proj, dcat, lng, lnb, ws, bs3, name):
    s = proj.shape[0]

    def body(pu_ref, pv_ref, dm_ref, lng_ref, lnb_ref, ws_ref, bs_ref, dp_ref, dws_ref, dbs_ref, dlng_ref, dlnb_ref, dvn_ref):
        @pl.when(pl.program_id(0) == 0)
        def _():
            dws_ref[...] = jnp.zeros_like(dws_ref)
            dbs_ref[...] = jnp.zeros_like(dbs_ref)
            dlng_ref[...] = jnp.zeros_like(dlng_ref)
            dlnb_ref[...] = jnp.zeros_like(dlnb_ref)

        pu, pv = pu_ref[...], pv_ref[...]
        lng = lng_ref[...]
        u, xhat, rstd, vn = _gmlp_parts(pu, pv, lng, lnb_ref[...])
        dm = dm_ref[...].astype(F32)
        causal = _iota((CHUNK, CHUNK), 0) >= _iota((CHUNK, CHUNK), 1)
        for g in range(A_GROUPS):
            sl = slice(g * A_GW, (g + 1) * A_GW)
            w = jnp.where(causal, ws_ref[g], 0.0)
            sv = _dot(w, vn[:, sl]) + bs_ref[g]
            dsv = dm[:, sl] * u[:, sl]
            dp_ref[:, sl] = (dm[:, sl] * sv * _gelu_grad(pu[:, sl])).astype(dp_ref.dtype)
            dvn_ref[:, sl] = _dot_tn(w, dsv)
            dws_ref[g] += jnp.where(causal, _dot_nt(dsv, vn[:, sl]), 0.0)
            dbs_ref[g] += jnp.sum(dsv, axis=-1, keepdims=True)
        dvn = dvn_ref[...]
        dlng_ref[...] += jnp.sum(dvn * xhat, axis=0, keepdims=True)
        dlnb_ref[...] += jnp.sum(dvn, axis=0, keepdims=True)
        dxh = dvn * lng
        dv = rstd * (dxh - jnp.mean(dxh, axis=-1, keepdims=True) - xhat * jnp.mean(dxh * xhat, axis=-1, keepdims=True))
        dp_ref[:, D_INNER:] = (dv * _gelu_grad(pv)).astype(dp_ref.dtype)

    full = lambda shape: pl.BlockSpec(shape, lambda c: (0,) * len(shape))
    return pl.pallas_call(
        body, name=name, grid=(s // CHUNK,),
        in_specs=[pl.BlockSpec((CHUNK, D_INNER), lambda c: (c, 0)), pl.BlockSpec((CHUNK, D_INNER), lambda c: (c, 1)),
                  pl.BlockSpec((CHUNK, D_INNER), lambda c: (c, 0)),
                  full((1, D_INNER)), full((1, D_INNER)), full((A_GROUPS, CHUNK, CHUNK)), full((A_GROUPS, CHUNK, 1))],
        out_specs=[pl.BlockSpec((CHUNK, 2 * D_INNER), lambda c: (c, 0)), full((A_GROUPS, CHUNK, CHUNK)),
                   full((A_GROUPS, CHUNK, 1)), full((1, D_INNER)), full((1, D_INNER))],
        out_shape=[jax.ShapeDtypeStruct((s, 2 * D_INNER + X_WIDTH), _ACT), jax.ShapeDtypeStruct((A_GROUPS, CHUNK, CHUNK), F32),
                   jax.ShapeDtypeStruct((A_GROUPS, CHUNK, 1), F32), jax.ShapeDtypeStruct((1, D_INNER), F32),
                   jax.ShapeDtypeStruct((1, D_INNER), F32)],
        scratch_shapes=[pltpu.VMEM((CHUNK, D_INNER), F32)],
        compiler_params=_cp(VMEM_BIG))(proj, proj, dcat, lng, lnb, ws, bs3)


_X_SCALE = 1.0 / math.sqrt(X_HD)


def _attn_fwd(proj, qblk, kv, cat, name, tm=512):
    s = proj.shape[0]
    tm = min(tm, s)

    def body(q_ref, kv_ref, cat_ref, o_ref):
        for h in range(X_HEADS):
            sl = slice(h * X_HD, (h + 1) * X_HD)
            k = kv_ref[:, sl]
            v = kv_ref[:, X_WIDTH + h * X_HD:X_WIDTH + (h + 1) * X_HD]
            sc = _dot_nt(q_ref[:, sl], k) * _X_SCALE
            e = jnp.exp(sc - jnp.max(sc, axis=-1, keepdims=True))
            p = e / jnp.sum(e, axis=-1, keepdims=True)
            o_ref[:, sl] = _dot(p, v).astype(o_ref.dtype)

    return pl.pallas_call(
        body, name=name, grid=(s // tm,),
        in_specs=[pl.BlockSpec((tm, X_WIDTH), lambda i: (i, qblk)), pl.BlockSpec((N_MEM, 2 * X_WIDTH), lambda i: (0, 0)),
                  pl.BlockSpec(memory_space=pl.ANY)],
        out_specs=pl.BlockSpec((tm, X_WIDTH), lambda i: (i, D_INNER // X_WIDTH)),
        out_shape=jax.ShapeDtypeStruct(cat.shape, cat.dtype), input_output_aliases={2: 0},
        compiler_params=_cp(VMEM_BIG))(proj, kv, cat)


def _attn_bwd(proj, qblk, kv, dcat, dproj, name, tm=512):
    s = proj.shape[0]
    tm = min(tm, s)

    def body(q_ref, kv_ref, do_ref, dproj_ref, dq_ref, dkv_ref):
        @pl.when(pl.program_id(0) == 0)
        def _():
            dkv_ref[...] = jnp.zeros_like(dkv_ref)

        for h in range(X_HEADS):
            sl = slice(h * X_HD, (h + 1) * X_HD)
            slv = slice(X_WIDTH + h * X_HD, X_WIDTH + (h + 1) * X_HD)
            q = q_ref[:, sl]
            k = kv_ref[:, sl]
            v = kv_ref[:, slv]
            do = do_ref[:, sl].astype(F32)
            sc = _dot_nt(q, k) * _X_SCALE
            e = jnp.exp(sc - jnp.max(sc, axis=-1, keepdims=True))
            p = e / jnp.sum(e, axis=-1, keepdims=True)
            dp = _dot_nt(do, v)
            ds = p * (dp - jnp.sum(dp * p, axis=-1, keepdims=True)) * _X_SCALE
            dq_ref[:, sl] = _dot(ds, k).astype(dq_ref.dtype)
            dkv_ref[:, sl] += _dot_tn(ds, q)
            dkv_ref[:, slv] += _dot_tn(p, do)

    return pl.pallas_call(
        body, name=name, grid=(s // tm,),
        in_specs=[pl.BlockSpec((tm, X_WIDTH), lambda i: (i, qblk)), pl.BlockSpec((N_MEM, 2 * X_WIDTH), lambda i: (0, 0)),
                  pl.BlockSpec((tm, X_WIDTH), lambda i: (i, 2)), pl.BlockSpec(memory_space=pl.ANY)],
        out_specs=[pl.BlockSpec((tm, X_WIDTH), lambda i: (i, qblk)), pl.BlockSpec((N_MEM, 2 * X_WIDTH), lambda i: (0, 0))],
        out_shape=[jax.ShapeDtypeStruct(dproj.shape, dproj.dtype), jax.ShapeDtypeStruct((N_MEM, 2 * X_WIDTH), F32)],
        input_output_aliases={3: 0}, compiler_params=_cp(VMEM_BIG))(proj, kv, dcat, dproj)


CONV_TC = 256
_XBC_BLK0 = D_INNER // CONV_TC


CONV_RB = 64
SUBLANES = 8


def _rows_before(cur, prev_last, j):
    rolled = pltpu.roll(cur, j, 0)
    head = jnp.where(_iota((SUBLANES, cur.shape[1]), 0) < j, pltpu.roll(prev_last, j, 0), rolled[:SUBLANES])
    return jnp.concatenate([head, rolled[SUBLANES:]], axis=0)


def _rows_after(cur, next_first, j):
    n = cur.shape[0]
    rolled = pltpu.roll(cur, n - j, 0)
    tail = jnp.where(_iota((SUBLANES, cur.shape[1]), 0) >= SUBLANES - j, pltpu.roll(next_first, SUBLANES - j, 0),
                     rolled[n - SUBLANES:])
    return jnp.concatenate([rolled[:n - SUBLANES], tail], axis=0)


def _conv_pre(x_ref, w_ref, b_ref, r0, prev_last):
    cur = x_ref[pl.ds(r0, CONV_RB), :]
    shifts = [_rows_before(cur, prev_last, j) for j in range(1, CONV_K)]
    pre = b_ref[...] + w_ref[CONV_K - 1:CONV_K, :] * cur
    for j in range(1, CONV_K):
        pre = pre + w_ref[CONV_K - 1 - j:CONV_K - j, :] * shifts[j - 1]
    return pre, cur, shifts


def _conv_fwd(proj, w, b, name):
    s = proj.shape[0]

    def body(x_ref, w_ref, b_ref, o_ref):
        xv = x_ref[...]
        rows = _iota(xv.shape, 0)
        pre = b_ref[...] + w_ref[CONV_K - 1:CONV_K, :] * xv
        for j in range(1, CONV_K):
            pre = pre + w_ref[CONV_K - 1 - j:CONV_K - j, :] * jnp.where(rows >= j, pltpu.roll(xv, j, 0), 0.0)
        o_ref[...] = pre * _sigmoid(pre)

    return pl.pallas_call(
        body, name=name, grid=(CONV_DIM // CONV_TC,),
        in_specs=[pl.BlockSpec((s, CONV_TC), lambda j: (0, _XBC_BLK0 + j)), pl.BlockSpec((CONV_K, CONV_TC), lambda j: (0, j)),
                  pl.BlockSpec((1, CONV_TC), lambda j: (0, j))],
        out_specs=pl.BlockSpec((s, CONV_TC), lambda j: (0, j)),
        out_shape=jax.ShapeDtypeStruct((s, CONV_DIM), F32), compiler_params=_cp(VMEM_BIG))(proj, w, b)


def _conv_bwd(proj, w, b, dxbc, dproj, name):
    s = proj.shape[0]

    nb = s // CONV_RB

    def body(x_ref, w_ref, b_ref, d_ref, dproj_ref, dx_ref, dw_ref, db_ref, dpre_ref):
        def fold(v):
            out = v[:SUBLANES]
            for t in range(1, CONV_RB // SUBLANES):
                out = out + v[t * SUBLANES:(t + 1) * SUBLANES]
            return out

        def first(i, carry):
            prev_last, acc = carry
            r0 = pl.multiple_of(i * CONV_RB, CONV_RB)
            pre, cur, shifts = _conv_pre(x_ref, w_ref, b_ref, r0, prev_last)
            sig = _sigmoid(pre)
            dpre = d_ref[pl.ds(r0, CONV_RB), :] * (sig * (1.0 + pre * (1.0 - sig)))
            dpre_ref[pl.ds(r0, CONV_RB), :] = dpre
            taps = [cur] + shifts
            acc = tuple(a + fold(dpre * t) for a, t in zip(acc[:CONV_K], taps)) + (acc[CONV_K] + fold(dpre),)
            return cur[CONV_RB - SUBLANES:], acc

        zero8 = jnp.zeros((SUBLANES, CONV_TC), F32)
        _, acc = lax.fori_loop(0, nb, first, (zero8, (zero8,) * (CONV_K + 1)))
        for j in range(CONV_K):
            dw_ref[CONV_K - 1 - j:CONV_K - j, :] = jnp.sum(acc[j], axis=0, keepdims=True)
        db_ref[...] = jnp.sum(acc[CONV_K], axis=0, keepdims=True)

        def second(i, next_first):
            r0 = pl.multiple_of((nb - 1 - i) * CONV_RB, CONV_RB)
            cur = dpre_ref[pl.ds(r0, CONV_RB), :]
            dx = w_ref[CONV_K - 1:CONV_K, :] * cur
            for j in range(1, CONV_K):
                dx = dx + w_ref[CONV_K - 1 - j:CONV_K - j, :] * _rows_after(cur, next_first, j)
            dx_ref[pl.ds(r0, CONV_RB), :] = dx.astype(dx_ref.dtype)
            return cur[:SUBLANES]

        lax.fori_loop(0, nb, second, zero8)

    return pl.pallas_call(
        body, name=name, grid=(CONV_DIM // CONV_TC,),
        in_specs=[pl.BlockSpec((s, CONV_TC), lambda j: (0, _XBC_BLK0 + j)), pl.BlockSpec((CONV_K, CONV_TC), lambda j: (0, j)),
                  pl.BlockSpec((1, CONV_TC), lambda j: (0, j)), pl.BlockSpec((s, CONV_TC), lambda j: (0, j)),
                  pl.BlockSpec(memory_space=pl.ANY)],
        out_specs=[pl.BlockSpec((s, CONV_TC), lambda j: (0, _XBC_BLK0 + j)), pl.BlockSpec((CONV_K, CONV_TC), lambda j: (0, j)),
                   pl.BlockSpec((1, CONV_TC), lambda j: (0, j))],
        out_shape=[jax.ShapeDtypeStruct(dproj.shape, dproj.dtype), jax.ShapeDtypeStruct((CONV_K, CONV_DIM), F32),
                   jax.ShapeDtypeStruct((1, CONV_DIM), F32)], input_output_aliases={4: 0},
        scratch_shapes=[pltpu.VMEM((s, CONV_TC), F32)],
        compiler_params=_cp(VMEM_BIG))(proj, w, b, dxbc, dproj)


def _ssd_common(dtc_ref, br_ref, ar_ref, csb_ref, cst_ref, csf_ref):
    a_row = -jnp.exp(ar_ref[...])
    dt_c = _softplus(dtc_ref[...] + br_ref[...])
    tril = _iota((CHUNK, CHUNK), 0) >= _iota((CHUNK, CHUNK), 1)
    cs = _sel_dot(tril, dt_c * a_row)
    cst_ref[...] = cs.T
    e64 = (jnp.right_shift(_iota((HPAD, D_INNER), 1), 6) == _iota((HPAD, D_INNER), 0)).astype(jnp.bfloat16)
    e128 = jnp.right_shift(_iota((HPAD, SSM_HEADS * CHUNK), 1), 7) == _iota((HPAD, SSM_HEADS * CHUNK), 0)
    csb_ref[...] = _dot_sel(cs, e128)
    dt_full = _dot_sel(dt_c, e64)
    csf_ref[...] = _dot_sel(cs, e64)
    cs_full = csf_ref[...]
    cs_last = csf_ref[CHUNK - 1:CHUNK, :]
    e_full = jnp.exp(cs_full)
    f_full = jnp.exp(cs_last - cs_full)
    gamma = jnp.exp(cs_last)
    return a_row, dt_c, cs, dt_full, e_full, f_full, gamma, e64


def _ssd_lambda(csb_ref, cst_ref, h, causal):
    diff = csb_ref[:, h * CHUNK:(h + 1) * CHUNK] - cst_ref[h:h + 1, :]
    return jnp.exp(jnp.where(causal, diff, -1e30))


_SSD_VEC_SPECS = lambda: [pl.BlockSpec((1, HPAD), lambda c: (0, 0)), pl.BlockSpec((1, HPAD), lambda c: (0, 0)),
                          pl.BlockSpec((1, D_INNER), lambda c: (0, 0))]


def _ssd_fwd(xbc, dtc, bias_row, alog_row, dfull, name):
    s = xbc.shape[0]
    nc = s // CHUNK

    def body(xbc_ref, dtc_ref, br_ref, ar_ref, df_ref, y_ref, st_ref, ht_ref, csb_ref, cst_ref, csf_ref):
        @pl.when(pl.program_id(0) == 0)
        def _():
            ht_ref[...] = jnp.zeros_like(ht_ref)

        _, _, _, dt_full, e_full, f_full, gamma, _ = _ssd_common(dtc_ref, br_ref, ar_ref, csb_ref, cst_ref, csf_ref)
        x = xbc_ref[:, :D_INNER]
        xdt = x * dt_full
        st_ref[...] = ht_ref[...]
        causal = _iota((CHUNK, CHUNK), 0) >= _iota((CHUNK, CHUNK), 1)
        lo = _iota((CHUNK, CHUNK), 1) < SSM_P
        for g in range(SSM_GROUPS):
            gs = slice(g * SSM_GW, (g + 1) * SSM_GW)
            bg = xbc_ref[:, D_INNER + g * SSM_N:D_INNER + (g + 1) * SSM_N]
            cg = xbc_ref[:, D_INNER + SSM_GROUPS * SSM_N + g * SSM_N:D_INNER + SSM_GROUPS * SSM_N + (g + 1) * SSM_N]
            ht = ht_ref[:, gs]
            cb = _dot_nt(cg, bg)
            yoff = e_full[:, gs] * _dot(cg, ht)
            for jp in range(SSM_GW // CHUNK):
                j = g * (SSM_GW // CHUNK) + jp
                ps = slice(j * CHUNK, (j + 1) * CHUNK)
                x2 = xdt[:, ps]
                y0 = _dot(cb * _ssd_lambda(csb_ref, cst_ref, 2 * j, causal), x2)
                y1 = _dot(cb * _ssd_lambda(csb_ref, cst_ref, 2 * j + 1, causal), x2)
                y_ref[:, ps] = (jnp.where(lo, y0, y1) + yoff[:, jp * CHUNK:(jp + 1) * CHUNK]
                                + x[:, ps] * df_ref[:, ps])
            ht_ref[:, gs] = gamma[:, gs] * ht + _dot_tn(bg, xdt[:, gs] * f_full[:, gs])

    return pl.pallas_call(
        body, name=name, grid=(nc,),
        in_specs=[pl.BlockSpec((CHUNK, CONV_DIM), lambda c: (c, 0)), pl.BlockSpec((CHUNK, HPAD), lambda c: (c, 0))]
                 + _SSD_VEC_SPECS(),
        out_specs=[pl.BlockSpec((CHUNK, D_INNER), lambda c: (c, 0)), pl.BlockSpec((None, SSM_N, D_INNER), lambda c: (c, 0, 0))],
        out_shape=[jax.ShapeDtypeStruct((s, D_INNER), F32), jax.ShapeDtypeStruct((nc, SSM_N, D_INNER), F32)],
        scratch_shapes=[pltpu.VMEM((SSM_N, D_INNER), F32), pltpu.VMEM((CHUNK, SSM_HEADS * CHUNK), F32),
                        pltpu.VMEM((HPAD, CHUNK), F32), pltpu.VMEM((CHUNK, D_INNER), F32)],
        compiler_params=_cp(VMEM_BIG))(xbc, dtc, bias_row, alog_row, dfull)


def _ssd_bwd(xbc, dtc, bias_row, alog_row, dfull, dy, states, name):
    s = xbc.shape[0]
    nc = s // CHUNK
    rev = lambda c: nc - 1 - c

    def body(xbc_ref, dtc_ref, br_ref, ar_ref, df_ref, dy_ref, st_ref,
             dxbc_ref, ddt_ref, dalog_ref, dd_ref, dbias_ref,
             dht_ref, csb_ref, cst_ref, csf_ref, ddf_ref, dxs_ref, dcsf_ref, dcsl_ref):
        step = pl.program_id(0)

        @pl.when(step == 0)
        def _():
            dht_ref[...] = jnp.zeros_like(dht_ref)
            ddf_ref[...] = jnp.zeros_like(ddf_ref)
            dalog_ref[...] = jnp.zeros_like(dalog_ref)
            dbias_ref[...] = jnp.zeros_like(dbias_ref)
            dd_ref[...] = jnp.zeros_like(dd_ref)

        a_row, dt_c, _, dt_full, e_full, f_full, gamma, e64 = _ssd_common(dtc_ref, br_ref, ar_ref, csb_ref, cst_ref, csf_ref)
        x = xbc_ref[:, :D_INNER]
        xdt = x * dt_full
        dy_all = dy_ref[...]
        ddf_ref[...] += jnp.broadcast_to(jnp.sum(dy_all * x, axis=0, keepdims=True), ddf_ref.shape)
        causal = _iota((CHUNK, CHUNK), 0) >= _iota((CHUNK, CHUNK), 1)
        lo = _iota((CHUNK, CHUNK), 1) < SSM_P
        head_lane = _iota((CHUNK, HPAD), 1)
        head_row = _iota((HPAD, CHUNK), 0)
        dcs_heads = jnp.zeros((CHUNK, HPAD), F32)
        dcs_cols = jnp.zeros((HPAD, CHUNK), F32)
        for g in range(SSM_GROUPS):
            gs = slice(g * SSM_GW, (g + 1) * SSM_GW)
            b0 = D_INNER + g * SSM_N
            c0 = D_INNER + SSM_GROUPS * SSM_N + g * SSM_N
            bg = xbc_ref[:, b0:b0 + SSM_N]
            cg = xbc_ref[:, c0:c0 + SSM_N]
            ht = st_ref[:, gs]
            dht = dht_ref[:, gs]
            dyg = dy_all[:, gs]
            eg, fg, gg = e_full[:, gs], f_full[:, gs], gamma[:, gs]
            z = _dot(cg, ht)
            dz = dyg * eg
            dcg = _dot_nt(dz, ht)
            dht_new = _dot_tn(cg, dz) + gg * dht
            xf = xdt[:, gs] * fg
            dxf = _dot(bg, dht)
            dbg = _dot_nt(xf, dht)
            dff = dxf * xf
            dcsf_ref[:, gs] = dyg * eg * z - dff
            dcsl_ref[:, gs] = jnp.broadcast_to(
                jnp.sum(dff, axis=0, keepdims=True) + jnp.sum(dht * ht, axis=0, keepdims=True) * gg, (8, SSM_GW))
            cb = _dot_nt(cg, bg)
            dcb = jnp.zeros((CHUNK, CHUNK), F32)
            for jp in range(SSM_GW // CHUNK):
                j = g * (SSM_GW // CHUNK) + jp
                ps = slice(j * CHUNK, (j + 1) * CHUNK)
                x2 = xdt[:, ps]
                dy2 = dy_all[:, ps]
                dxh = []
                for hh in range(2):
                    h = 2 * j + hh
                    lam = _ssd_lambda(csb_ref, cst_ref, h, causal)
                    mh = cb * lam
                    dyh = jnp.where(lo, dy2, 0.0) if hh == 0 else jnp.where(lo, 0.0, dy2)
                    dm = _dot_nt(dyh, x2)
                    dcb = dcb + dm * lam
                    gm = dm * mh
                    dcs_heads = dcs_heads + jnp.where(head_lane == h, jnp.sum(gm, axis=1, keepdims=True), 0.0)
                    dcs_cols = dcs_cols + jnp.where(head_row == h, jnp.sum(gm, axis=0, keepdims=True), 0.0)
                    dxh.append(_dot_tn(mh, dy2))
                dxs_ref[:, ps] = jnp.where(lo, dxh[0], dxh[1]) + dxf[:, jp * CHUNK:(jp + 1) * CHUNK] * fg[:, jp * CHUNK:(jp + 1) * CHUNK]
            dxbc_ref[:, b0:b0 + SSM_N] = (dbg + _dot_tn(dcb, cg)).astype(dxbc_ref.dtype)
            dxbc_ref[:, c0:c0 + SSM_N] = (dcg + _dot(dcb, bg)).astype(dxbc_ref.dtype)
            dht_ref[:, gs] = dht_new
        dxs = dxs_ref[...]
        dcs_heads = dcs_heads - dcs_cols.T + _dot_sel(dcsf_ref[...], e64, ((1,), (1,)))
        dcs_last = _dot_sel(dcsl_ref[...], e64, ((1,), (1,)))
        dcs_heads = dcs_heads + jnp.where(_iota((CHUNK, HPAD), 0) == CHUNK - 1, dcs_last[0:1, :], 0.0)
        triu = _iota((CHUNK, CHUNK), 0) <= _iota((CHUNK, CHUNK), 1)
        dda = _sel_dot(triu, dcs_heads)
        ddt = dda * a_row + _dot_sel(dxs * x, e64, ((1,), (1,)))
        dxbc_ref[:, :D_INNER] = (dxs * dt_full + dy_all * df_ref[...]).astype(dxbc_ref.dtype)
        dalog_ref[...] += jnp.sum(dda * dt_c, axis=0, keepdims=True) * a_row
        ddt_raw = ddt * _sigmoid(dtc_ref[...] + br_ref[...])
        ddt_ref[...] = ddt_raw.astype(ddt_ref.dtype)
        dbias_ref[...] += jnp.sum(ddt_raw, axis=0, keepdims=True)

        @pl.when(step == nc - 1)
        def _():
            dd_ref[...] = _dot_sel(ddf_ref[...], e64, ((1,), (1,)))[0:1, :]

    vec = pl.BlockSpec((1, HPAD), lambda c: (0, 0))
    return pl.pallas_call(
        body, name=name, grid=(nc,),
        in_specs=[pl.BlockSpec((CHUNK, CONV_DIM), lambda c: (rev(c), 0)), pl.BlockSpec((CHUNK, HPAD), lambda c: (rev(c), 0))]
                 + _SSD_VEC_SPECS()
                 + [pl.BlockSpec((CHUNK, D_INNER), lambda c: (rev(c), 0)),
                    pl.BlockSpec((None, SSM_N, D_INNER), lambda c: (rev(c), 0, 0))],
        out_specs=[pl.BlockSpec((CHUNK, CONV_DIM), lambda c: (rev(c), 0)), pl.BlockSpec((CHUNK, HPAD), lambda c: (rev(c), 0)),
                   vec, vec, vec],
        out_shape=[jax.ShapeDtypeStruct((s, CONV_DIM), F32), jax.ShapeDtypeStruct((s, HPAD), _ACT),
                   jax.ShapeDtypeStruct((1, HPAD), F32), jax.ShapeDtypeStruct((1, HPAD), F32),
                   jax.ShapeDtypeStruct((1, HPAD), F32)],
        scratch_shapes=[pltpu.VMEM((SSM_N, D_INNER), F32), pltpu.VMEM((CHUNK, SSM_HEADS * CHUNK), F32),
                        pltpu.VMEM((HPAD, CHUNK), F32), pltpu.VMEM((CHUNK, D_INNER), F32),
                        pltpu.VMEM((8, D_INNER), F32), pltpu.VMEM((CHUNK, D_INNER), F32),
                        pltpu.VMEM((CHUNK, D_INNER), F32), pltpu.VMEM((8, D_INNER), F32)],
        compiler_params=_cp(VMEM_BIG))(xbc, dtc, bias_row, alog_row, dfull, dy, states)


def _gate_fwd(y, proj, gn, name, tm=512):
    s = y.shape[0]
    tm = min(tm, s)

    def body(y_ref, z_ref, gn_ref, o_ref):
        for g in range(SSM_GROUPS):
            gs = slice(g * SSM_GW, (g + 1) * SSM_GW)
            z = z_ref[:, gs]
            t = y_ref[:, gs] * (z * _sigmoid(z))
            r = lax.rsqrt(jnp.mean(t * t, axis=-1, keepdims=True) + EPS)
            o_ref[:, gs] = (t * r * gn_ref[:, gs]).astype(o_ref.dtype)

    row = pl.BlockSpec((tm, D_INNER), lambda i: (i, 0))
    return pl.pallas_call(
        body, name=name, grid=(s // tm,), in_specs=[row, row, pl.BlockSpec((1, D_INNER), lambda i: (0, 0))],
        out_specs=row, out_shape=jax.ShapeDtypeStruct((s, D_INNER + X_WIDTH), _ACT),
        compiler_params=_cp(VMEM_BIG))(y, proj, gn)


def _gate_bwd(y, proj, gn, dcat, name, tm=512):
    s = y.shape[0]
    tm = min(tm, s)

    def body(y_ref, z_ref, gn_ref, dm_ref, dy_ref, dz_ref, dgn_ref):
        @pl.when(pl.program_id(0) == 0)
        def _():
            dgn_ref[...] = jnp.zeros_like(dgn_ref)

        for g in range(SSM_GROUPS):
            gs = slice(g * SSM_GW, (g + 1) * SSM_GW)
            z = z_ref[:, gs]
            yv = y_ref[:, gs]
            sig = _sigmoid(z)
            sz = z * sig
            t = yv * sz
            r = lax.rsqrt(jnp.mean(t * t, axis=-1, keepdims=True) + EPS)
            th = t * r
            dm = dm_ref[:, gs].astype(F32)
            dmg = dm * gn_ref[:, gs]
            dt_ = r * (dmg - th * jnp.mean(dmg * th, axis=-1, keepdims=True))
            dgn_ref[:, gs] += jnp.sum(dm * th, axis=0, keepdims=True)
            dy_ref[:, gs] = dt_ * sz
            dz_ref[:, gs] = (dt_ * yv * (sig * (1.0 + z * (1.0 - sig)))).astype(dz_ref.dtype)

    row = pl.BlockSpec((tm, D_INNER), lambda i: (i, 0))
    vec = pl.BlockSpec((1, D_INNER), lambda i: (0, 0))
    return pl.pallas_call(
        body, name=name, grid=(s // tm,), in_specs=[row, row, vec, row], out_specs=[row, row, vec],
        out_shape=[jax.ShapeDtypeStruct((s, D_INNER), F32), jax.ShapeDtypeStruct((s, 6 * D_MODEL), _ACT),
                   jax.ShapeDtypeStruct((1, D_INNER), F32)], compiler_params=_cp(VMEM_BIG))(y, proj, gn, dcat)


def _block_of(kind, width):
    if kind == "col":
        return lambda ref, j: ref.at[:, :, pl.ds(pl.multiple_of(j * width, 128), width)]
    if kind == "row":
        return lambda ref, j: ref.at[:, pl.ds(pl.multiple_of(j * width, 8), width), :]
    return lambda ref, j: ref.at[j]


def _coords():
    return lax.axis_index("x"), lax.axis_index("y"), lax.axis_index("c")


def _rel_chip(x, y, k):
    return (1 - x if k & 1 else x), (1 - y if k & 2 else y)


def _all_gather_body(ins, outs, send_sems, recv_sems, local_sems, blocks):
    n = len(ins)
    x, y, c = _coords()
    sibling = (x, y, 1 - c)
    via = (x + (1 - c) * (1 - 2 * x), y + c * (1 - 2 * y))
    onto = (x + c * (1 - 2 * x), y + (1 - c) * (1 - 2 * y))

    def copy(t, k, chip, core, to, src=None):
        dst = blocks[t](outs[t], 4 * chip[0] + 2 * chip[1] + core)
        return pltpu.make_async_remote_copy(
            src_ref=dst if src is None else src, dst_ref=dst, send_sem=send_sems.at[t, k],
            recv_sem=recv_sems.at[t, k], device_id=to, device_id_type=MESH)

    started = []
    for t in range(n):
        mine = pltpu.make_async_copy(ins[t], blocks[t](outs[t], 4 * x + 2 * y + c), local_sems.at[t])
        mine.start()
        started.append(mine)
    sends = []
    for t in range(n):
        for k in range(3):
            px, py = _rel_chip(x, y, k)
            cp = copy(t, k, (x, y), c, (px, py, 1 - c if k == 0 else c), src=ins[t])
            cp.start()
            sends.append(cp)
    for t in range(n):
        for k in (1, 2):
            chip = _rel_chip(x, y, k)
            copy(t, k, chip, c, sibling).wait_recv()
            fwd = copy(t, 3 + k, chip, c, sibling)
            fwd.start()
            sends.append(fwd)
        hop = copy(t, 3, via, c, (*onto, c))
        hop.start()
        sends.append(hop)
    for t in range(n):
        diagonal = _rel_chip(x, y, 3)
        copy(t, 3, diagonal, c, sibling).wait_recv()
        fwd = copy(t, 6, diagonal, c, sibling)
        fwd.start()
        sends.append(fwd)
    for t in range(n):
        copy(t, 0, (x, y), 1 - c, sibling).wait_recv()
        for k in range(1, 4):
            copy(t, 3 + k, _rel_chip(x, y, k), 1 - c, sibling).wait_recv()
    for cp in sends:
        cp.wait_send()
    for mine in started:
        mine.wait()


def _handshake(peers):
    barrier = pltpu.get_barrier_semaphore()
    for peer in peers:
        pl.semaphore_signal(barrier, inc=1, device_id=peer, device_id_type=MESH)
    pl.semaphore_wait(barrier, len(peers))


def _gather_peers():
    x, y, c = _coords()
    return [(x, y, 1 - c)] + [(*_rel_chip(x, y, k), c) for k in (1, 2)]


SEQ_ID_GATHER, SEQ_ID_SIBLING, SEQ_ID_CHIPS = 1, 2, 3


def _sequencer_call(body, peers, operands, out_types, sems, name, collective_id, after=()):
    n_in, n_out, n_after = len(operands), len(out_types), len(after)

    def launch(*refs):
        _handshake(peers())
        body(refs[:n_in], refs[n_in + n_after:n_in + n_after + n_out], *refs[n_in + n_after + n_out:])

    return pl.kernel(
        launch, name=name, out_type=out_types, mesh=plsc.ScalarSubcoreMesh(axis_name="seq", num_cores=1),
        scratch_types=sems, compiler_params=pltpu.CompilerParams(collective_id=collective_id))(*operands, *after)


def _all_gather_seq(shards, layouts, name, after=()):
    n = len(shards)
    blocks = [_block_of(kind, width) for kind, width, _ in layouts]
    return _sequencer_call(
        lambda ins, outs, *sems: _all_gather_body(ins, outs, *sems, blocks), _gather_peers, shards,
        [jax.ShapeDtypeStruct(shape, sh.dtype) for sh, (_, _, shape) in zip(shards, layouts)],
        [pltpu.SemaphoreType.DMA((n, 7)), pltpu.SemaphoreType.DMA((n, 7)), pltpu.SemaphoreType.DMA((n,))],
        name, SEQ_ID_GATHER, after)


def _tie(small, after, name):
    del name
    return lax.optimization_barrier((small, *after))[0]


def _rs_to_sibling(grads, layouts, name, after=()):
    n = len(grads)
    blocks = [_block_of(kind, width) for kind, width, _ in layouts]

    def body(ins, outs, send_sems, recv_sems):
        x, y, c = _coords()
        sibling = (x, y, 1 - c)
        cps = []
        for t in range(n):
            for k in range(4):
                px, py = _rel_chip(x, y, k)
                cp = pltpu.make_async_remote_copy(
                    src_ref=blocks[t](ins[t], 4 * px + 2 * py + (1 - c)), dst_ref=outs[t].at[k],
                    send_sem=send_sems.at[t, k], recv_sem=recv_sems.at[t, k], device_id=sibling, device_id_type=MESH)
                cp.start()
                cps.append(cp)
        for cp in cps:
            cp.wait_recv()
        for cp in cps:
            cp.wait_send()

    def sibling_only():
        x, y, c = _coords()
        return [(x, y, 1 - c)]

    return _sequencer_call(
        body, sibling_only, grads,
        [jax.ShapeDtypeStruct((4,) + shape, g.dtype) for g, (_, _, shape) in zip(grads, layouts)],
        [pltpu.SemaphoreType.DMA((n, 4)), pltpu.SemaphoreType.DMA((n, 4))], name, SEQ_ID_SIBLING, after)


def _rs_chip_sum(grad, recv, layout, xyc, name):
    kind, width, shape = layout
    r, ccols = shape

    def src_index(k, xyc_ref):
        px = jnp.where(k % 2 == 1, 1 - xyc_ref[0], xyc_ref[0])
        py = jnp.where(k // 2 == 1, 1 - xyc_ref[1], xyc_ref[1])
        return 4 * px + 2 * py + xyc_ref[2]

    if kind == "col":
        g_spec = pl.BlockSpec((r, ccols), lambda k, s_: (0, src_index(k, s_)))
    elif kind == "row":
        g_spec = pl.BlockSpec((r, ccols), lambda k, s_: (src_index(k, s_), 0))
    else:
        g_spec = pl.BlockSpec((None, r, ccols), lambda k, s_: (src_index(k, s_), 0, 0))

    def body(xyc_ref, g_ref, r_ref, o_ref):
        o_ref[...] = (g_ref[...].astype(F32) + r_ref[...].astype(F32)).astype(o_ref.dtype)

    slot = pl.BlockSpec((None, r, ccols), lambda k, s_: (k, 0, 0))
    return pl.pallas_call(
        body, name=name,
        grid_spec=pltpu.PrefetchScalarGridSpec(num_scalar_prefetch=1, grid=(4,), in_specs=[g_spec, slot], out_specs=slot),
        out_shape=jax.ShapeDtypeStruct((4, r, ccols), grad.dtype), compiler_params=_cp(VMEM_BIG))(xyc, grad, recv)


def _rs_across_chips(parts, name):
    n = len(parts)

    def body(ins, outs, send_sems, recv_sems):
        x, y, c = _coords()
        cps = []
        for t in range(n):
            for k in range(1, 4):
                px, py = _rel_chip(x, y, k)
                cp = pltpu.make_async_remote_copy(
                    src_ref=ins[t].at[k], dst_ref=outs[t].at[k - 1], send_sem=send_sems.at[t, k - 1],
                    recv_sem=recv_sems.at[t, k - 1], device_id=(px, py, c), device_id_type=MESH)
                cp.start()
                cps.append(cp)
        for cp in cps:
            cp.wait_recv()
        for cp in cps:
            cp.wait_send()

    def other_chips():
        x, y, c = _coords()
        return [(*_rel_chip(x, y, k), c) for k in range(1, 4)]

    return _sequencer_call(
        body, other_chips, parts, [jax.ShapeDtypeStruct((3,) + p.shape[1:], p.dtype) for p in parts],
        [pltpu.SemaphoreType.DMA((n, 3)), pltpu.SemaphoreType.DMA((n, 3))], name, SEQ_ID_CHIPS)


def _adamw_math(w, g, m, v):
    m = ADAM_B1 * m + (1.0 - ADAM_B1) * g
    v = ADAM_B2 * v + (1.0 - ADAM_B2) * jnp.square(g)
    m_hat = m / (1.0 - ADAM_B1 ** ADAM_STEP)
    v_hat = v / (1.0 - ADAM_B2 ** ADAM_STEP)
    delta = -ADAM_LR * (m_hat / (jnp.sqrt(v_hat) + ADAM_EPS) + ADAM_WD * w)
    return delta, m, v


def _row_tile(rows, cap):
    best = None
    for cand in range(8, min(rows, cap) + 1, 8):
        if rows % cand == 0:
            best = cand
    assert best is not None, rows
    return best


def _adamw(w, m, v, parts, name, layer=None, prev=None, tr=256):
    r, ccols = w.shape[-2:]
    npart = len(parts)
    if r % 8 == 0:
        tr, tc = _row_tile(r, tr), ccols
        steps, at = r // tr, (lambda i: (i, 0))
    else:
        tr, tc = r, 256
        assert ccols % tc == 0
        steps, at = ccols // tc, (lambda i: (0, i))

    def spec(lead):
        if lead is None:
            return pl.BlockSpec((tr, tc), at)
        return pl.BlockSpec((None, tr, tc), lambda i: (lead,) + at(i))

    wspec = lambda: spec(layer)
    pspec = spec

    def body(*refs):
        w_ref, m_ref, v_ref = refs[:3]
        p_refs = refs[3:3 + npart]
        outs = refs[len(refs) - 4:]
        g = p_refs[0][...].astype(F32)
        for p_ref in p_refs[1:]:
            g = g + p_ref[...].astype(F32)
        delta, mn, vn = _adamw_math(w_ref[...], g, m_ref[...], v_ref[...])
        outs[0][...] = g
        outs[1][...] = delta
        outs[2][...] = mn
        outs[3][...] = vn

    operands = [w, m, v] + [p for p, _ in parts]
    in_specs = [wspec(), wspec(), wspec()] + [pspec(lead) for _, lead in parts]
    aliases = {}
    if prev is not None:
        for i, p in enumerate(prev):
            aliases[len(operands)] = i
            operands.append(p)
            in_specs.append(pl.BlockSpec(memory_space=pl.ANY))
    return pl.pallas_call(
        body, name=name, grid=(steps,), in_specs=in_specs, out_specs=[wspec()] * 4,
        out_shape=[jax.ShapeDtypeStruct(w.shape, F32)] * 4, input_output_aliases=aliases,
        compiler_params=_cp(VMEM_BIG))(*operands)


def _small_update(gathered, params, loss_all, me, name):
    n = len(gathered)
    shapes = [w.shape for w, _, _ in params]

    def body(me_ref, *refs):
        g_refs, loss_ref = refs[:n], refs[n]
        p_refs = refs[n + 1:n + 1 + 3 * n]
        o_refs = refs[n + 1 + 3 * n:]
        for i in range(n):
            r, c = shapes[i]
            if gathered[i].shape[2] == c:
                parts = [g_refs[i][j] for j in range(N_DEV)]
            else:
                off = pl.multiple_of(me_ref[0] * c, 128)
                parts = [g_refs[i][j, :, pl.ds(off, c)] for j in range(N_DEV)]
            g = functools.reduce(lambda a, b: a + b, parts)
            delta, mn, vn = _adamw_math(p_refs[3 * i][...], g, p_refs[3 * i + 1][...], p_refs[3 * i + 2][...])
            for k, val in enumerate((g, delta, mn, vn)):
                o_refs[4 * i + k][...] = val
        o_refs[4 * n][...] = functools.reduce(lambda a, b: a + b, [loss_ref[j] for j in range(N_DEV)])

    vmem = pl.BlockSpec(memory_space=pltpu.VMEM)
    flat_params = [a for p in params for a in p]
    outs = pl.pallas_call(
        body, name=name, in_specs=[pl.BlockSpec(memory_space=pltpu.SMEM)] + [vmem] * (n + 1 + 3 * n),
        out_specs=[vmem] * (4 * n + 1),
        out_shape=[jax.ShapeDtypeStruct(shp, F32) for shp in shapes for _ in range(4)] + [jax.ShapeDtypeStruct((1, 128), F32)],
        compiler_params=_cp(VMEM_BIG))(me, *gathered, loss_all, *flat_params)
    return [tuple(outs[4 * i:4 * i + 4]) for i in range(n)], outs[4 * n]


def _pack(arrays):
    pieces, layout, off = [], [], 0
    for a in arrays:
        n = a.size
        padded = -(-n // 1024) * 1024
        flat = a.reshape(-1).astype(F32)
        if padded != n:
            flat = jnp.pad(flat, (0, padded - n))
        pieces.append(flat.reshape(padded // 128, 128))
        layout.append((off, n, a.shape))
        off += padded // 128
    return jnp.concatenate(pieces, axis=0), layout


def kernel(x, mem, norm_mix, norm_ffn, mem_norm, w_kv, w_out, w_ffn1, w_ffn2, a_in, a_ln_g, a_ln_b, a_ws, a_bs, b_in, b_conv_w, b_conv_b, b_dt_bias, b_a_log, b_d, b_gnorm, final_norm, loss_target, m_norm_mix, m_norm_ffn, m_mem_norm, m_w_kv, m_w_out, m_w_ffn1, m_w_ffn2, m_a_in, m_a_ln_g, m_a_ln_b, m_a_ws, m_a_bs, m_b_in, m_b_conv_w, m_b_conv_b, m_b_dt_bias, m_b_a_log, m_b_d, m_b_gnorm, m_final_norm, v_norm_mix, v_norm_ffn, v_mem_norm, v_w_kv, v_w_out, v_w_ffn1, v_w_ffn2, v_a_in, v_a_ln_g, v_a_ln_b, v_a_ws, v_a_bs, v_b_in, v_b_conv_w, v_b_conv_b, v_b_dt_bias, v_b_a_log, v_b_d, v_b_gnorm, v_final_norm):
    s = x.shape[1]
    xs = x.reshape(s, D_MODEL)
    mems = mem.reshape(N_MEM, D_MODEL)
    target = loss_target.reshape(s, D_MODEL)
    ax, ay, ac = lax.axis_index("x"), lax.axis_index("y"), lax.axis_index("c")
    me = 4 * ax + 2 * ay + ac
    xyc = jnp.stack([ax, ay, ac]).astype(jnp.int32)

    b_cols = b_in.shape[2]
    act = lambda a: a.astype(_ACT)
    lay_f1, lay_f2 = ("col", 512, (1, D_MODEL, D_FF)), ("row", 512, (1, D_FF, D_MODEL))
    lay_out, lay_kv = ("row", 384, (1, 3 * D_MODEL, D_MODEL)), ("col", 256, (1, D_MODEL, 2 * X_WIDTH))
    small_w_pack = _pack([b_conv_w[0], b_conv_b[0], b_gnorm[0]])[0]
    (WA,) = _all_gather_seq([act(a_in)], [("col", 640, (1, D_MODEL, 5 * D_MODEL))], "ag_proj_a")
    wo0, wkv0 = _all_gather_seq([act(w_out[0:1]), act(w_kv[0:1])], [lay_out, lay_kv], "ag_out0")
    w1_0, w2_0 = _all_gather_seq([act(w_ffn1[0:1]), act(w_ffn2[0:1])], [lay_f1, lay_f2], "ag_ffn0")
    a0 = _rms_fwd(xs, norm_mix[0].reshape(1, -1), "mix_norm0")
    tr_b = lambda a: jnp.swapaxes(a, 1, 2)
    wbt_blk, small_w = _all_gather_seq(
        [act(tr_b(b_in)[0]), small_w_pack],
        [("blk", 0, (N_DEV, b_cols, D_MODEL)), ("blk", 0, (N_DEV, 32, 128))], "ag_proj_b", after=[a0])
    wo1, wkv1 = _all_gather_seq([act(w_out[1:2]), act(w_kv[1:2])], [lay_out, lay_kv], "ag_out1", after=[a0])
    w1_1, w2_1 = _all_gather_seq([act(w_ffn1[1:2]), act(w_ffn2[1:2])], [lay_f1, lay_f2], "ag_ffn1", after=[a0])
    W1, W2, WO, WKV = [w1_0, w1_1], [w2_0, w2_1], [wo0, wo1], [wkv0, wkv1]
    dt0 = D_INNER + CONV_DIM

    row = lambda a: a.reshape(1, -1)
    nmix = [row(norm_mix[0]), row(norm_mix[1])]
    nffn = [row(norm_ffn[0]), row(norm_ffn[1])]
    nmem = [row(mem_norm[0]), row(mem_norm[1])]
    fin = row(final_norm)
    lng, lnb = a_ln_g.reshape(1, D_INNER), a_ln_b.reshape(1, D_INNER)
    ws = a_ws[0]
    bs3 = a_bs[0].reshape(A_GROUPS, CHUNK, 1)
    pad_h = lambda a: jnp.pad(a.reshape(-1), (0, HPAD - SSM_HEADS))
    bias_row = pad_h(b_dt_bias).reshape(1, HPAD)
    alog_row = pad_h(b_a_log).reshape(1, HPAD)
    dfull = jnp.repeat(b_d.reshape(-1), SSM_P).reshape(1, D_INNER)

    kvs, mns = [None, None], [None, None]

    def mem_kv(i, after=None):
        gain = nmem[i] if after is None else _tie(nmem[i], after, f"tie_mem{i}")
        mns[i] = _rms_fwd(mems, gain, f"mem_norm{i}")
        kvs[i] = _mm(mns[i], WKV[i], m=N_MEM, n=2 * X_WIDTH, k=D_MODEL, b_at=(0, 0, 0), out_dtype=_ACT, name=f"kv{i}")

    def ffn_fwd(h, i):
        f = _rms_fwd(h, nffn[i], f"ffn_norm{i}")
        p = _mm(f, W1[i], m=s, n=D_FF, k=D_MODEL, b_at=(0, 0, 0), out_dtype=_ACT, name=f"ffn_up{i}")
        hn = _mm(p, W2[i], m=s, n=D_MODEL, k=D_FF, b_at=(0, 0, 0), a_pro="relu2", add=h, name=f"ffn_down{i}")
        return f, p, hn

    def out_proj(h, cat, i):
        return _mm(cat, WO[i], m=s, n=D_MODEL, k=3 * D_MODEL, b_at=(0, 0, 0), add=h, name=f"out_proj{i}")

    proj_a = _mm(a0, WA, m=s, n=5 * D_MODEL, k=D_MODEL, b_at=(0, 0, 0), name="proj_a")
    mem_kv(0, after=[proj_a])
    cat_a = _gmlp_fwd(proj_a, lng, lnb, ws, bs3, "gmlp_fwd")
    cat_a = _attn_fwd(proj_a, 4, kvs[0], cat_a, "attn_fwd0")
    h1 = out_proj(xs, cat_a, 0)
    f0, p0, h2 = ffn_fwd(h1, 0)

    wbt_blk, small_w, _ = lax.optimization_barrier((wbt_blk, small_w, p0))
    jd, lo = divmod(dt0, b_cols)
    assert lo + SSM_HEADS <= b_cols
    wbt_full = wbt_blk.reshape(N_DEV * b_cols, D_MODEL)
    WBT = jnp.concatenate([wbt_full[:dt0], wbt_full[dt0 + SSM_HEADS:]], axis=0)
    WBDT = jnp.pad(wbt_full[dt0:dt0 + SSM_HEADS], ((0, HPAD - SSM_HEADS), (0, 0)))
    cw_sh, cb_sh, gn_sh = 4 * 384, 384, 256
    sw = small_w.reshape(N_DEV, 32 * 128)
    conv_w = jnp.transpose(sw[:, :cw_sh].reshape(N_DEV, CONV_K, 384), (1, 0, 2)).reshape(CONV_K, CONV_DIM)
    conv_b = sw[:, 2048:2048 + cb_sh].reshape(1, CONV_DIM)
    gnorm = sw[:, 3072:3072 + gn_sh].reshape(1, D_INNER)

    a1 = _rms_fwd(h2, nmix[1], "mix_norm1")
    proj_b = _mm(a1, WBT, m=s, n=6 * D_MODEL, k=D_MODEL, tb=True, name="proj_b")
    dt_raw = _mm(a1, WBDT, m=s, n=HPAD, k=D_MODEL, tb=True, name="proj_dt")
    xbc = _conv_fwd(proj_b, conv_w, conv_b, "conv_fwd")
    y_ssd, states = _ssd_fwd(xbc, dt_raw, bias_row, alog_row, dfull, "ssd_fwd")
    cat_b = _gate_fwd(y_ssd, proj_b, gnorm, "gate_fwd")
    mem_kv(1, after=[cat_b])
    cat_b = _attn_fwd(proj_b, 5, kvs[1], cat_b, "attn_fwd1")
    h3 = out_proj(h2, cat_b, 1)
    f1, p1, h4 = ffn_fwd(h3, 1)

    loss_part, dh, dh_act, d_fin = _loss_head(h4, fin, target, "loss_head")

    g_f1, g_f2, g_out, g_kv = [None, None], [None, None], [None, None], [None, None]
    d_nffn, d_nmix, d_nmem = [None, None], [None, None], [None, None]

    def ffn_bwd(dh, dh_act, h_in, f, p, i, after=(), after_last=()):
        dp = _mm(dh_act, W2[i], m=s, n=D_FF, k=D_MODEL, tb=True, b_at=(0, 0, 0), epi_p=p, out_dtype=_ACT, name=f"ffn_down_dx{i}")
        g_f2[i] = _mm(p, dh_act, m=D_FF, n=D_MODEL, k=s, ta=True, a_pro="relu2", out_dtype=_ACT, name=f"ffn_down_dw{i}")
        g_f1[i] = _mm(f, dp, m=D_MODEL, n=D_FF, k=s, ta=True, out_dtype=_ACT, name=f"ffn_up_dw{i}")
        df = _mm(dp, W1[i], m=s, n=D_MODEL, k=D_FF, tb=True, b_at=(0, 0, 0), after=after, name=f"ffn_up_dx{i}")
        gain = _tie(nffn[i], after_last, f"tie_ffn_norm{i}") if after_last else nffn[i]
        dh_in, dh_in_act, d_nffn[i] = _rms_bwd(h_in, gain, df, dh, f"ffn_norm_bwd{i}")
        return dh_in, dh_in_act

    def out_bwd(dh_act, cat, i):
        dcat = _mm(dh_act, WO[i], m=s, n=3 * D_MODEL, k=D_MODEL, tb=True, b_at=(0, 0, 0), out_dtype=_ACT, name=f"out_dx{i}")
        g_out[i] = _mm(cat, dh_act, m=3 * D_MODEL, n=D_MODEL, k=s, ta=True, out_dtype=_ACT, name=f"out_dw{i}")
        return dcat

    def mem_bwd(dkv, i):
        g_kv[i] = _mm(mns[i], dkv, m=D_MODEL, n=2 * X_WIDTH, k=N_MEM, ta=True, out_dtype=_ACT, name=f"kv_dw{i}")
        dmn = _mm(dkv, WKV[i], m=N_MEM, n=D_MODEL, k=2 * X_WIDTH, tb=True, b_at=(0, 0, 0), name=f"kv_dx{i}")
        _, _, d_nmem[i] = _rms_bwd(mems, nmem[i], dmn, None, f"mem_norm_bwd{i}")

    lay_g = {"f1": ("col", 512, (D_MODEL, 512)), "f2": ("row", 512, (512, D_MODEL)), "out": ("row", 384, (384, D_MODEL)),
             "kv": ("col", 256, (D_MODEL, 256)), "a": ("col", 640, (D_MODEL, 640)), "b": ("blk", 0, (b_cols, D_MODEL))}
    reduced = {}

    def reduce_scatter(group, tag, after=(), sums_after=()):
        grads3, lays3 = [], []
        for fam, _, g in group:
            kind, width, shape = lay_g[fam]
            grads3.append(g if kind == "blk" else g.reshape((1,) + g.shape))
            lays3.append((kind, width, shape if kind == "blk" else (1,) + shape))
        recv1 = _rs_to_sibling(grads3, lays3, f"rs_sibling_{tag}", after)
        if sums_after:
            recv1 = lax.optimization_barrier((tuple(recv1), tuple(sums_after)))[0]
        parts = [_rs_chip_sum(g, recv1[t].reshape((4,) + lay_g[fam][2]), lay_g[fam], xyc, f"rs_chip_sum_{fam}{i}")
                 for t, (fam, i, g) in enumerate(group)]
        recv2 = _rs_across_chips(parts, f"rs_chips_{tag}")
        for (fam, i, _), p, r2 in zip(group, parts, recv2):
            reduced[fam, i] = (p, r2)
        return parts, recv2

    dh3, dh3_act = ffn_bwd(dh, dh_act, h3, f1, p1, 1)
    dcat_b = out_bwd(dh3_act, cat_b, 1)
    sums, got_ffn1 = reduce_scatter([("f1", 1, g_f1[1]), ("f2", 1, g_f2[1]), ("out", 1, g_out[1])], "ffn1", sums_after=[dcat_b])
    dy_ssd, dproj_b, d_gnorm = _gate_bwd(y_ssd, proj_b, gnorm, dcat_b, "gate_bwd")
    dproj_b, dkv_b = _attn_bwd(proj_b, 5, kvs[1], dcat_b, dproj_b, "attn_bwd1")
    mem_bwd(dkv_b, 1)
    dxbc, ddt_raw, d_alog, d_dskip, d_dtbias = _ssd_bwd(
        xbc, dt_raw, _tie(bias_row, sums, "tie_ffn1"), alog_row, dfull, dy_ssd, states, "ssd_bwd")
    dproj_b, d_convw, d_convb = _conv_bwd(proj_b, conv_w, _tie(conv_b, got_ffn1, "tie_got_ffn1"), dxbc, dproj_b, "conv_bwd")
    gb = _mm(dproj_b, a1, m=6 * D_MODEL, n=D_MODEL, k=s, ta=True, out_dtype=_ACT, name="proj_b_dw")
    gb_dt = _mm(ddt_raw, a1, m=HPAD, n=D_MODEL, k=s, ta=True, out_dtype=_ACT, name="proj_b_dw_dt")
    blocks_b = [gb[j * b_cols:(j + 1) * b_cols] for j in range(jd)]
    blocks_b.append(jnp.concatenate([gb[jd * b_cols:dt0], gb_dt[:SSM_HEADS], gb[dt0:(jd + 1) * b_cols - SSM_HEADS]], axis=0))
    blocks_b += [gb[j * b_cols - SSM_HEADS:(j + 1) * b_cols - SSM_HEADS] for j in range(jd + 1, N_DEV)]
    gb_blk = jnp.stack(blocks_b)
    da1 = _mm(dproj_b, WBT, m=s, n=D_MODEL, k=6 * D_MODEL, name="proj_b_dx")
    sums, got_mix1 = reduce_scatter([("kv", 1, g_kv[1]), ("b", 0, gb_blk)], "mix1", sums_after=[da1])
    da1 = _mm(ddt_raw, WBDT, m=s, n=D_MODEL, k=HPAD, add=da1, name="proj_b_dx_dt")
    dh2, dh2_act, d_nmix[1] = _rms_bwd(h2, _tie(nmix[1], sums, "tie_mix1"), da1, dh3, "mix_norm_bwd1")

    dh1, dh1_act = ffn_bwd(dh2, dh2_act, h1, f0, p0, 0, after=got_ffn1, after_last=got_mix1)
    dcat_a = out_bwd(dh1_act, cat_a, 0)
    sums, got_ffn0 = reduce_scatter([("f1", 0, g_f1[0]), ("f2", 0, g_f2[0]), ("out", 0, g_out[0])], "ffn0", sums_after=[dcat_a])
    dproj_a, d_ws, d_bs3, d_lng, d_lnb = _gmlp_bwd(proj_a, dcat_a, _tie(lng, sums, "tie_ffn0"), lnb, ws, bs3, "gmlp_bwd")
    dproj_a, dkv_a = _attn_bwd(proj_a, 4, kvs[0], dcat_a, dproj_a, "attn_bwd0")
    mem_bwd(dkv_a, 0)

    def big_update(w, m, v, fam, nlayer):
        res = None
        for i in range(nlayer):
            part, recv2 = reduced[fam, i]
            plist = [(part, 0), (recv2, 0), (recv2, 1), (recv2, 2)]
            res = _adamw(w, m, v, plist, f"adamw_{fam}{i}", layer=i, prev=res)
        return res

    da0 = _mm(dproj_a, WA, m=s, n=D_MODEL, k=5 * D_MODEL, tb=True, b_at=(0, 0, 0), name="proj_a_dx")
    grad_x, _, d_nmix[0] = _rms_bwd(xs, nmix[0], da0, dh1, "mix_norm_bwd0")
    ga = _mm(a0, dproj_a, m=D_MODEL, n=5 * D_MODEL, k=s, ta=True, out_dtype=_ACT, after=[grad_x], name="proj_a_dw")
    r_b = big_update(tr_b(b_in), tr_b(m_b_in), tr_b(v_b_in), "b", 1)
    reduce_scatter([("kv", 0, g_kv[0]), ("a", 0, ga)], "mix0", after=got_ffn0, sums_after=r_b)
    r_b = [tr_b(o) for o in r_b]

    small_names = ["norm_mix", "norm_ffn", "mem_norm", "a_ln_g", "a_ln_b", "a_ws", "a_bs", "b_dt_bias", "b_a_log", "b_d",
                   "final_norm", "b_conv_w", "b_conv_b", "b_gnorm"]
    small_grads = [jnp.concatenate(d_nmix, axis=0), jnp.concatenate(d_nffn, axis=0), jnp.concatenate(d_nmem, axis=0),
                   d_lng, d_lnb, d_ws.reshape(A_GROUPS * CHUNK, CHUNK), d_bs3.reshape(A_GROUPS, CHUNK),
                   d_dtbias[:, :SSM_HEADS], d_alog[:, :SSM_HEADS], d_dskip[:, :SSM_HEADS], d_fin,
                   d_convw, d_convb, d_gnorm]
    small_2d = [(2, D_MODEL)] * 3 + [(1, D_INNER)] * 2 + [(A_GROUPS * CHUNK, CHUNK), (A_GROUPS, CHUNK)] + [(1, SSM_HEADS)] * 3 \
        + [(1, D_MODEL), (CONV_K, 384), (1, 384), (1, 256)]
    gathered = _all_gather_seq(
        small_grads + [loss_part], [("blk", 0, (N_DEV,) + g.shape) for g in small_grads + [loss_part]], "ag_small_grads")

    r_f1 = big_update(w_ffn1, m_w_ffn1, v_w_ffn1, "f1", 2)
    r_f2 = big_update(w_ffn2, m_w_ffn2, v_w_ffn2, "f2", 2)
    r_out = big_update(w_out, m_w_out, v_w_out, "out", 2)
    r_kv = big_update(w_kv, m_w_kv, v_w_kv, "kv", 2)
    r_a = big_update(a_in, m_a_in, v_a_in, "a", 1)

    small_w = [norm_mix, norm_ffn, mem_norm, a_ln_g, a_ln_b, a_ws, a_bs, b_dt_bias, b_a_log, b_d, final_norm,
               b_conv_w, b_conv_b, b_gnorm]
    small_m = [m_norm_mix, m_norm_ffn, m_mem_norm, m_a_ln_g, m_a_ln_b, m_a_ws, m_a_bs, m_b_dt_bias, m_b_a_log, m_b_d,
               m_final_norm, m_b_conv_w, m_b_conv_b, m_b_gnorm]
    small_v = [v_norm_mix, v_norm_ffn, v_mem_norm, v_a_ln_g, v_a_ln_b, v_a_ws, v_a_bs, v_b_dt_bias, v_b_a_log, v_b_d,
               v_final_norm, v_b_conv_w, v_b_conv_b, v_b_gnorm]
    params = [tuple(a.reshape(shp) for a in wmv) for shp, wmv in zip(small_2d, zip(small_w, small_m, small_v))]
    loss_all = _tie(gathered[-1], [r_a[0], r_kv[0]], "tie_small")
    small_res, loss_sum = _small_update(gathered[:-1], params, loss_all, me.astype(jnp.int32).reshape(1), "adamw_small")
    loss = loss_sum[0, 0]

    names = ["norm_mix", "norm_ffn", "mem_norm", "w_kv", "w_out", "w_ffn1", "w_ffn2", "a_in", "a_ln_g", "a_ln_b", "a_ws",
             "a_bs", "b_in", "b_conv_w", "b_conv_b", "b_dt_bias", "b_a_log", "b_d", "b_gnorm", "final_norm"]
    big = {"w_kv": r_kv, "w_out": r_out, "w_ffn1": r_f1, "w_ffn2": r_f2, "a_in": r_a, "b_in": r_b}
    outs = [loss, grad_x.reshape(x.shape)]
    for kind in range(4):
        for nm in names:
            if nm in big:
                outs.append(big[nm][kind])
            else:
                i = small_names.index(nm)
                outs.append(small_res[i][kind].reshape(small_w[i].shape))
    return tuple(outs)
```

```python
import functools
import math

import jax
import jax.numpy as jnp
from jax import lax
from jax.experimental import pallas as pl
from jax.experimental.pallas import tpu as pltpu
from jax.experimental.pallas import tpu_sc as plsc

F32 = jnp.float32
_MXU = jnp.bfloat16
_ACT = jnp.bfloat16

D_MODEL = 1024
CHUNK = 128
N_MEM = 256
D_INNER = 2048
A_GROUPS = 8
A_GW = D_INNER // A_GROUPS
SSM_HEADS = 32
SSM_P = 64
SSM_GROUPS = 4
SSM_GW = D_INNER // SSM_GROUPS
SSM_N = 128
CONV_K = 4
CONV_DIM = 3072
X_HEADS = 4
X_HD = 256
X_WIDTH = 1024
D_FF = 4096
EPS = 1e-6
HPAD = 128
N_DEV = 8

ADAM_LR = 0.001
ADAM_B1 = 0.9
ADAM_B2 = 0.999
ADAM_EPS = 1e-08
ADAM_WD = 0.01
ADAM_STEP = 10

VMEM_BIG = 56 * 1024 * 1024
MESH = pl.DeviceIdType.MESH


def _cp(vmem=None):
    if vmem is None:
        return pltpu.CompilerParams()
    return pltpu.CompilerParams(vmem_limit_bytes=vmem)


def _dot(a, b, dims=((1,), (0,))):
    return lax.dot_general(a.astype(_MXU), b.astype(_MXU), (dims, ((), ())), preferred_element_type=F32)


def _dot_nt(a, b):
    return _dot(a, b, ((1,), (1,)))


def _dot_tn(a, b):
    return _dot(a, b, ((0,), (0,)))


def _split3(x):
    x1 = x.astype(jnp.bfloat16)
    r = x - x1.astype(F32)
    x2 = r.astype(jnp.bfloat16)
    x3 = (r - x2.astype(F32)).astype(jnp.bfloat16)
    return x1, x2, x3


def _dot_sel(x, sel, dims=((1,), (0,)), terms=2):
    sel = sel.astype(jnp.bfloat16)
    parts = [lax.dot_general(t, sel, (dims, ((), ())), preferred_element_type=F32) for t in _split3(x)[:terms]]
    return functools.reduce(lambda a, b: a + b, parts)


def _sel_dot(sel, x, dims=((1,), (0,))):
    sel = sel.astype(jnp.bfloat16)
    parts = [lax.dot_general(sel, t, (dims, ((), ())), preferred_element_type=F32) for t in _split3(x)]
    return (parts[0] + parts[1]) + parts[2]


def _sigmoid(x):
    return 1.0 / (1.0 + jnp.exp(-x))


def _gelu(x):
    return 0.5 * x * (1.0 + lax.erf(x * (1.0 / math.sqrt(2.0))))


def _gelu_with_grad(x):
    phi = 0.5 * (1.0 + lax.erf(x * (1.0 / math.sqrt(2.0))))
    return x * phi, phi + x * jnp.exp(-0.5 * x * x) * (1.0 / math.sqrt(2.0 * math.pi))


def _softplus(x):
    return jnp.maximum(x, 0.0) + jnp.log1p(jnp.exp(-jnp.abs(x)))


def _iota(shape, dim):
    return lax.broadcasted_iota(jnp.int32, shape, dim)


MM_VMEM_BUDGET = 40 * 1024 * 1024
HBM_BYTES_PER_S = 2.5e12
GRID_STEP_S = 0.35e-6
VMEM_ACC_BYTES_PER_S = 6e12


def _divisors(dim, unit):
    out = [d for d in range(unit, min(dim, 2048) + 1, unit) if dim % d == 0]
    return out if out else [dim]


def _mm_tiles(m, n, k, sa, sb, s_mn, a_pro, offsets):
    best = None
    (a_r0, a_c0, ta), (b_r0, b_c0, tb), (o_r0, o_c0) = offsets
    for tm in _divisors(m, 128):
        for tn in _divisors(n, 128):
            for tk in [k // d for d in (1, 2, 3, 4, 6, 8) if k % d == 0 and (k // d) % 128 == 0]:
                a_t = (tk, tm) if ta else (tm, tk)
                b_t = (tn, tk) if tb else (tk, tn)
                if a_r0 % a_t[0] or a_c0 % a_t[1] or b_r0 % b_t[0] or b_c0 % b_t[1] or o_r0 % tm or o_c0 % tn:
                    continue
                nk = k // tk
                vmem = 2 * (tm * tk * sa + tk * tn * sb + tm * tn * s_mn) + tm * tn * 4 * (2 if nk > 1 else 1)
                if a_pro or sa == 4:
                    vmem += tm * tk * 6
                if sb == 4:
                    vmem += tk * tn * 2
                if vmem > MM_VMEM_BUDGET:
                    continue
                gi, gj = m // tm, n // tn
                for j_inner in (True, False):
                    if nk > 1:
                        traffic = gj * m * k * sa + gi * k * n * sb
                    elif j_inner:
                        traffic = m * k * sa + gi * k * n * sb
                    else:
                        traffic = gj * m * k * sa + k * n * sb
                    traffic += m * n * s_mn + (tm * tk * sa + tk * tn * sb)
                    cost = traffic / HBM_BYTES_PER_S + gi * gj * nk * GRID_STEP_S
                    if nk > 1:
                        cost += m * n * 8 * nk / VMEM_ACC_BYTES_PER_S
                    if best is None or cost < best[0]:
                        best = (cost, tm, tn, tk, j_inner)
    assert best is not None, (m, n, k)
    return best[1:]


def _mm(a, b, *, m, n, k, name, ta=False, tb=False, a_at=(None, 0, 0), b_at=(None, 0, 0),
        out_dtype=F32, add=None, epi_p=None, epi_at=(None, 0, 0), out=None, out_at=(None, 0, 0),
        out_full=None, a_pro=None, after=()):
    s_mn =jnp.dtype(out.dtype if out is not None else out_dtype).itemsize
    s_mn += add.dtype.itemsize if add is not None else 0
    s_mn += epi_p.dtype.itemsize if epi_p is not None else 0
    tm, tn, tk, j_inner = _mm_tiles(m, n, k, a.dtype.itemsize, b.dtype.itemsize, s_mn, a_pro is not None,
                                    ((a_at[1], a_at[2], ta), (b_at[1], b_at[2], tb), (out_at[1], out_at[2])))
    nk = k // tk

    def spec(at, tr, tc, rsel, csel):
        lead, r0, c0 = at
        assert r0 % tr == 0 and c0 % tc == 0, (name, at, tr, tc)
        rb, cb = r0 // tr, c0 // tc
        if lead is None:
            return pl.BlockSpec((tr, tc), lambda g0, g1, kk: (rb + rsel(g0, g1, kk), cb + csel(g0, g1, kk)))
        return pl.BlockSpec((None, tr, tc), lambda g0, g1, kk: (lead, rb + rsel(g0, g1, kk), cb + csel(g0, g1, kk)))

    gi = (lambda g0, g1, kk: g0) if j_inner else (lambda g0, g1, kk: g1)
    gj = (lambda g0, g1, kk: g1) if j_inner else (lambda g0, g1, kk: g0)
    gk = lambda g0, g1, kk: kk
    a_spec = spec(a_at, tk, tm, gk, gi) if ta else spec(a_at, tm, tk, gi, gk)
    b_spec = spec(b_at, tn, tk, gj, gk) if tb else spec(b_at, tk, tn, gk, gj)
    dims = ((0,), (0,)) if ta else (((1,), (1,)) if tb else ((1,), (0,)))
    assert not (ta and tb)

    operands, in_specs = [a, b], [a_spec, b_spec]
    if add is not None:
        operands.append(add)
        in_specs.append(spec((None, 0, 0), tm, tn, gi, gj))
    if epi_p is not None:
        operands.append(epi_p)
        in_specs.append(spec(epi_at, tm, tn, gi, gj))
    aliases = {}
    if out is not None:
        aliases = {len(operands): 0}
        operands.append(out)
        in_specs.append(pl.BlockSpec(memory_space=pl.ANY))
        out_struct = jax.ShapeDtypeStruct(out.shape, out.dtype)
        out_dtype = out.dtype
    else:
        out_struct = jax.ShapeDtypeStruct(out_full if out_full is not None else (m, n), out_dtype)
    has_add, has_epi = add is not None, epi_p is not None
    n_skip = (1 if out is not None else 0) + len(after)
    operands += list(after)
    in_specs += [pl.BlockSpec(memory_space=pl.ANY)] * len(after)

    def body(*refs):
        a_ref, b_ref = refs[0], refs[1]
        pos = 2
        add_ref = epi_ref = None
        if has_add:
            add_ref = refs[pos]
            pos += 1
        if has_epi:
            epi_ref = refs[pos]
            pos += 1
        pos += n_skip
        o_ref = refs[pos]

        def finish(r):
            if has_add:
                r = r + add_ref[...].astype(F32)
            if has_epi:
                r = r * (2.0 * jnp.maximum(epi_ref[...].astype(F32), 0.0))
            o_ref[...] = r.astype(o_ref.dtype)

        av = a_ref[...]
        if a_pro == "relu2":
            av = jnp.square(jnp.maximum(av.astype(F32), 0.0))
        part = _dot(av, b_ref[...], dims)
        if nk == 1:
            finish(part)
        else:
            acc_ref = refs[pos + 1]
            kk = pl.program_id(2)

            @pl.when(kk == 0)
            def _():
                acc_ref[...] = part

            @pl.when(kk > 0)
            def _():
                acc_ref[...] += part

            @pl.when(kk == nk - 1)
            def _():
                finish(acc_ref[...])

    grid = (m // tm, n // tn, nk) if j_inner else (n // tn, m // tm, nk)
    return pl.pallas_call(
        body, name=name, grid=grid, in_specs=in_specs,
        out_specs=spec(out_at, tm, tn, gi, gj), out_shape=out_struct,
        scratch_shapes=[pltpu.VMEM((tm, tn), F32)] if nk > 1 else [], input_output_aliases=aliases,
        compiler_params=_cp(VMEM_BIG))(*operands)


def _rms_fwd(x, g, name, tm=1024):
    s, d = x.shape
    tm = min(tm, s)

    def body(x_ref, g_ref, o_ref):
        xv = x_ref[...]
        r = lax.rsqrt(jnp.mean(xv * xv, axis=-1, keepdims=True) + EPS)
        o_ref[...] = (xv * r * g_ref[...]).astype(o_ref.dtype)

    return pl.pallas_call(
        body, name=name, grid=(s // tm,),
        in_specs=[pl.BlockSpec((tm, d), lambda i: (i, 0)), pl.BlockSpec((1, d), lambda i: (0, 0))],
        out_specs=pl.BlockSpec((tm, d), lambda i: (i, 0)),
        out_shape=jax.ShapeDtypeStruct((s, d), _ACT), compiler_params=_cp(VMEM_BIG))(x, g)


def _rms_bwd(x, g, dy, dres, name, tm=512):
    s, d = x.shape
    tm = min(tm, s)
    has_res = dres is not None

    def body(*refs):
        if has_res:
            x_ref, g_ref, dy_ref, dres_ref, dx_ref, dxa_ref, dg_ref = refs
        else:
            x_ref, g_ref, dy_ref, dx_ref, dxa_ref, dg_ref = refs

        @pl.when(pl.program_id(0) == 0)
        def _():
            dg_ref[...] = jnp.zeros_like(dg_ref)

        xv = x_ref[...]
        dyv = dy_ref[...].astype(F32)
        r = lax.rsqrt(jnp.mean(xv * xv, axis=-1, keepdims=True) + EPS)
        xh = xv * r
        dyg = dyv * g_ref[...]
        dx = r * (dyg - xh * jnp.mean(dyg * xh, axis=-1, keepdims=True))
        if has_res:
            dx = dx + dres_ref[...]
        dx_ref[...] = dx
        dxa_ref[...] = dx.astype(dxa_ref.dtype)
        dg_ref[...] += jnp.sum(dyv * xh, axis=0, keepdims=True)

    row = pl.BlockSpec((tm, d), lambda i: (i, 0))
    vec = pl.BlockSpec((1, d), lambda i: (0, 0))
    in_specs = [row, vec, row] + ([row] if has_res else [])
    operands = [x, g, dy] + ([dres] if has_res else [])
    return pl.pallas_call(
        body, name=name, grid=(s // tm,), in_specs=in_specs, out_specs=[row, row, vec],
        out_shape=[jax.ShapeDtypeStruct((s, d), F32), jax.ShapeDtypeStruct((s, d), _ACT),
                   jax.ShapeDtypeStruct((1, d), F32)], compiler_params=_cp(VMEM_BIG))(*operands)


def _loss_head(h, g, target, name, tm=512):
    s, d = h.shape
    tm = min(tm, s)

    def body(h_ref, g_ref, t_ref, loss_ref, dh_ref, dha_ref, dg_ref):
        @pl.when(pl.program_id(0) == 0)
        def _():
            dg_ref[...] = jnp.zeros_like(dg_ref)
            loss_ref[...] = jnp.zeros_like(loss_ref)

        xv = h_ref[...]
        r = lax.rsqrt(jnp.mean(xv * xv, axis=-1, keepdims=True) + EPS)
        xh = xv * r
        err = xh * g_ref[...] - t_ref[...]
        loss_ref[...] += jnp.full(loss_ref.shape, 0.5 * jnp.sum(jnp.mean(err * err, axis=-1, keepdims=True)), F32)
        dyv = err * (1.0 / d)
        dyg = dyv * g_ref[...]
        dh = r * (dyg - xh * jnp.mean(dyg * xh, axis=-1, keepdims=True))
        dh_ref[...] = dh
        dha_ref[...] = dh.astype(dha_ref.dtype)
        dg_ref[...] += jnp.sum(dyv * xh, axis=0, keepdims=True)

    row = pl.BlockSpec((tm, d), lambda i: (i, 0))
    vec = pl.BlockSpec((1, d), lambda i: (0, 0))
    return pl.pallas_call(
        body, name=name, grid=(s // tm,), in_specs=[row, vec, row],
        out_specs=[pl.BlockSpec((1, 128), lambda i: (0, 0)), row, row, vec],
        out_shape=[jax.ShapeDtypeStruct((1, 128), F32), jax.ShapeDtypeStruct((s, d), F32),
                   jax.ShapeDtypeStruct((s, d), _ACT), jax.ShapeDtypeStruct((1, d), F32)],
        compiler_params=_cp(VMEM_BIG))(h, g, target)


def _gmlp_parts(u, v, lng, lnb):
    mu = jnp.mean(v, axis=-1, keepdims=True)
    vc = v - mu
    rstd = lax.rsqrt(jnp.mean(vc * vc, axis=-1, keepdims=True) + EPS)
    xhat = vc * rstd
    vn = xhat * lng + lnb
    return u, xhat, rstd, vn


def _gmlp_fwd(proj, lng, lnb, ws, bs3, name):
    s = proj.shape[0]

    def body(pu_ref, pv_ref, lng_ref, lnb_ref, ws_ref, bs_ref, o_ref):
        u, _, _, vn = _gmlp_parts(_gelu(pu_ref[...]), _gelu(pv_ref[...]), lng_ref[...], lnb_ref[...])
        causal = _iota((CHUNK, CHUNK), 0) >= _iota((CHUNK, CHUNK), 1)
        for g in range(A_GROUPS):
            sl = slice(g * A_GW, (g + 1) * A_GW)
            w = jnp.where(causal, ws_ref[g], 0.0)
            sv = _dot(w, vn[:, sl]) + bs_ref[g]
            o_ref[:, sl] = (u[:, sl] * sv).astype(o_ref.dtype)

    full = lambda shape: pl.BlockSpec(shape, lambda c: (0,) * len(shape))
    return pl.pallas_call(
        body, name=name, grid=(s // CHUNK,),
        in_specs=[pl.BlockSpec((CHUNK, D_INNER), lambda c: (c, 0)), pl.BlockSpec((CHUNK, D_INNER), lambda c: (c, 1)),
                  full((1, D_INNER)), full((1, D_INNER)), full((A_GROUPS, CHUNK, CHUNK)), full((A_GROUPS, CHUNK, 1))],
        out_specs=pl.BlockSpec((CHUNK, D_INNER), lambda c: (c, 0)),
        out_shape=jax.ShapeDtypeStruct((s, D_INNER + X_WIDTH), _ACT), compiler_params=_cp(VMEM_BIG))(proj, proj, lng, lnb, ws, bs3)


def _gmlp_bwd(proj, dcat, lng, lnb, ws, bs3, name):
    s = proj.shape[0]

    def body(pu_ref, pv_ref, dm_ref, lng_ref, lnb_ref, ws_ref, bs_ref, dp_ref, dws_ref, dbs_ref, dlng_ref, dlnb_ref, dvn_ref):
        @pl.when(pl.program_id(0) == 0)
        def _():
            dws_ref[...] = jnp.zeros_like(dws_ref)
            dbs_ref[...] = jnp.zeros_like(dbs_ref)
            dlng_ref[...] = jnp.zeros_like(dlng_ref)
            dlnb_ref[...] = jnp.zeros_like(dlnb_ref)

        lng = lng_ref[...]
        u, u_grad = _gelu_with_grad(pu_ref[...])
        v, v_grad = _gelu_with_grad(pv_ref[...])
        u, xhat, rstd, vn = _gmlp_parts(u, v, lng, lnb_ref[...])
        dm = dm_ref[...].astype(F32)
        causal = _iota((CHUNK, CHUNK), 0) >= _iota((CHUNK, CHUNK), 1)
        for g in range(A_GROUPS):
            sl = slice(g * A_GW, (g + 1) * A_GW)
            w = jnp.where(causal, ws_ref[g], 0.0)
            sv = _dot(w, vn[:, sl]) + bs_ref[g]
            dsv = dm[:, sl] * u[:, sl]
            dp_ref[:, sl] = (dm[:, sl] * sv * u_grad[:, sl]).astype(dp_ref.dtype)
            dvn_ref[:, sl] = _dot_tn(w, dsv)
            dws_ref[g] += jnp.where(causal, _dot_nt(dsv, vn[:, sl]), 0.0)
            dbs_ref[g] += jnp.sum(dsv, axis=-1, keepdims=True)
        dvn = dvn_ref[...]
        dlng_ref[...] += jnp.sum(dvn * xhat, axis=0, keepdims=True)
        dlnb_ref[...] += jnp.sum(dvn, axis=0, keepdims=True)
        dxh = dvn * lng
        dv = rstd * (dxh - jnp.mean(dxh, axis=-1, keepdims=True) - xhat * jnp.mean(dxh * xhat, axis=-1, keepdims=True))
        dp_ref[:, D_INNER:] = (dv * v_grad).astype(dp_ref.dtype)

    full = lambda shape: pl.BlockSpec(shape, lambda c: (0,) * len(shape))
    return pl.pallas_call(
        body, name=name, grid=(s // CHUNK,),
        in_specs=[pl.BlockSpec((CHUNK, D_INNER), lambda c: (c, 0)), pl.BlockSpec((CHUNK, D_INNER), lambda c: (c, 1)),
                  pl.BlockSpec((CHUNK, D_INNER), lambda c: (c, 0)),
                  full((1, D_INNER)), full((1, D_INNER)), full((A_GROUPS, CHUNK, CHUNK)), full((A_GROUPS, CHUNK, 1))],
        out_specs=[pl.BlockSpec((CHUNK, 2 * D_INNER), lambda c: (c, 0)), full((A_GROUPS, CHUNK, CHUNK)),
                   full((A_GROUPS, CHUNK, 1)), full((1, D_INNER)), full((1, D_INNER))],
        out_shape=[jax.ShapeDtypeStruct((s, 2 * D_INNER + X_WIDTH), _ACT), jax.ShapeDtypeStruct((A_GROUPS, CHUNK, CHUNK), F32),
                   jax.ShapeDtypeStruct((A_GROUPS, CHUNK, 1), F32), jax.ShapeDtypeStruct((1, D_INNER), F32),
                   jax.ShapeDtypeStruct((1, D_INNER), F32)],
        scratch_shapes=[pltpu.VMEM((CHUNK, D_INNER), F32)],
        compiler_params=_cp(VMEM_BIG))(proj, proj, dcat, lng, lnb, ws, bs3)


_X_SCALE = 1.0 / math.sqrt(X_HD)


def _attn_fwd(proj, qblk, kv, cat, name, tm=512):
    s = proj.shape[0]
    tm = min(tm, s)

    def body(q_ref, kv_ref, cat_ref, o_ref):
        for h in range(X_HEADS):
            sl = slice(h * X_HD, (h + 1) * X_HD)
            k = kv_ref[:, sl]
            v = kv_ref[:, X_WIDTH + h * X_HD:X_WIDTH + (h + 1) * X_HD]
            sc = _dot_nt(q_ref[:, sl], k) * _X_SCALE
            e = jnp.exp(sc - jnp.max(sc, axis=-1, keepdims=True))
            p = e / jnp.sum(e, axis=-1, keepdims=True)
            o_ref[:, sl] = _dot(p, v).astype(o_ref.dtype)

    return pl.pallas_call(
        body, name=name, grid=(s // tm,),
        in_specs=[pl.BlockSpec((tm, X_WIDTH), lambda i: (i, qblk)), pl.BlockSpec((N_MEM, 2 * X_WIDTH), lambda i: (0, 0)),
                  pl.BlockSpec(memory_space=pl.ANY)],
        out_specs=pl.BlockSpec((tm, X_WIDTH), lambda i: (i, D_INNER // X_WIDTH)),
        out_shape=jax.ShapeDtypeStruct(cat.shape, cat.dtype), input_output_aliases={2: 0},
        compiler_params=_cp(VMEM_BIG))(proj, kv, cat)


def _attn_bwd(proj, qblk, kv, dcat, dproj, name, tm=512):
    s = proj.shape[0]
    tm = min(tm, s)

    def body(q_ref, kv_ref, do_ref, dproj_ref, dq_ref, dkv_ref):
        @pl.when(pl.program_id(0) == 0)
        def _():
            dkv_ref[...] = jnp.zeros_like(dkv_ref)

        for h in range(X_HEADS):
            sl = slice(h * X_HD, (h + 1) * X_HD)
            slv = slice(X_WIDTH + h * X_HD, X_WIDTH + (h + 1) * X_HD)
            q = q_ref[:, sl]
            k = kv_ref[:, sl]
            v = kv_ref[:, slv]
            do = do_ref[:, sl].astype(F32)
            sc = _dot_nt(q, k) * _X_SCALE
            e = jnp.exp(sc - jnp.max(sc, axis=-1, keepdims=True))
            p = e / jnp.sum(e, axis=-1, keepdims=True)
            dp = _dot_nt(do, v)
            ds = p * (dp - jnp.sum(dp * p, axis=-1, keepdims=True)) * _X_SCALE
            dq_ref[:, sl] = _dot(ds, k).astype(dq_ref.dtype)
            dkv_ref[:, sl] += _dot_tn(ds, q)
            dkv_ref[:, slv] += _dot_tn(p, do)

    return pl.pallas_call(
        body, name=name, grid=(s // tm,),
        in_specs=[pl.BlockSpec((tm, X_WIDTH), lambda i: (i, qblk)), pl.BlockSpec((N_MEM, 2 * X_WIDTH), lambda i: (0, 0)),
                  pl.BlockSpec((tm, X_WIDTH), lambda i: (i, 2)), pl.BlockSpec(memory_space=pl.ANY)],
        out_specs=[pl.BlockSpec((tm, X_WIDTH), lambda i: (i, qblk)), pl.BlockSpec((N_MEM, 2 * X_WIDTH), lambda i: (0, 0))],
        out_shape=[jax.ShapeDtypeStruct(dproj.shape, dproj.dtype), jax.ShapeDtypeStruct((N_MEM, 2 * X_WIDTH), F32)],
        input_output_aliases={3: 0}, compiler_params=_cp(VMEM_BIG))(proj, kv, dcat, dproj)


CONV_TC = 256
_XBC_BLK0 = D_INNER // CONV_TC


CONV_RB = 64
SUBLANES = 8


def _rows_before(cur, prev_last, j):
    rolled = pltpu.roll(cur, j, 0)
    head = jnp.where(_iota((SUBLANES, cur.shape[1]), 0) < j, pltpu.roll(prev_last, j, 0), rolled[:SUBLANES])
    return jnp.concatenate([head, rolled[SUBLANES:]], axis=0)


def _rows_after(cur, next_first, j):
    n = cur.shape[0]
    rolled = pltpu.roll(cur, n - j, 0)
    tail = jnp.where(_iota((SUBLANES, cur.shape[1]), 0) >= SUBLANES - j, pltpu.roll(next_first, SUBLANES - j, 0),
                     rolled[n - SUBLANES:])
    return jnp.concatenate([rolled[:n - SUBLANES], tail], axis=0)


def _conv_pre(x_ref, w_ref, b_ref, r0, prev_last):
    cur = x_ref[pl.ds(r0, CONV_RB), :]
    shifts = [_rows_before(cur, prev_last, j) for j in range(1, CONV_K)]
    pre = b_ref[...] + w_ref[CONV_K - 1:CONV_K, :] * cur
    for j in range(1, CONV_K):
        pre = pre + w_ref[CONV_K - 1 - j:CONV_K - j, :] * shifts[j - 1]
    return pre, cur, shifts


def _conv_fwd(proj, w, b, name):
    s = proj.shape[0]

    def body(x_ref, w_ref, b_ref, o_ref):
        xv = x_ref[...]
        rows = _iota(xv.shape, 0)
        pre = b_ref[...] + w_ref[CONV_K - 1:CONV_K, :] * xv
        for j in range(1, CONV_K):
            pre = pre + w_ref[CONV_K - 1 - j:CONV_K - j, :] * jnp.where(rows >= j, pltpu.roll(xv, j, 0), 0.0)
        o_ref[...] = pre * _sigmoid(pre)

    return pl.pallas_call(
        body, name=name, grid=(CONV_DIM // CONV_TC,),
        in_specs=[pl.BlockSpec((s, CONV_TC), lambda j: (0, _XBC_BLK0 + j)), pl.BlockSpec((CONV_K, CONV_TC), lambda j: (0, j)),
                  pl.BlockSpec((1, CONV_TC), lambda j: (0, j))],
        out_specs=pl.BlockSpec((s, CONV_TC), lambda j: (0, j)),
        out_shape=jax.ShapeDtypeStruct((s, CONV_DIM), F32), compiler_params=_cp(VMEM_BIG))(proj, w, b)


def _conv_bwd(proj, w, b, dxbc, dproj, name):
    s = proj.shape[0]

    nb = s // CONV_RB

    def body(x_ref, w_ref, b_ref, d_ref, dproj_ref, dx_ref, dw_ref, db_ref, dpre_ref):
        def fold(v):
            out = v[:SUBLANES]
            for t in range(1, CONV_RB // SUBLANES):
                out = out + v[t * SUBLANES:(t + 1) * SUBLANES]
            return out

        def first(i, carry):
            prev_last, acc = carry
            r0 = pl.multiple_of(i * CONV_RB, CONV_RB)
            pre, cur, shifts = _conv_pre(x_ref, w_ref, b_ref, r0, prev_last)
            sig = _sigmoid(pre)
            dpre = d_ref[pl.ds(r0, CONV_RB), :] * (sig * (1.0 + pre * (1.0 - sig)))
            dpre_ref[pl.ds(r0, CONV_RB), :] = dpre
            taps = [cur] + shifts
            acc = tuple(a + fold(dpre * t) for a, t in zip(acc[:CONV_K], taps)) + (acc[CONV_K] + fold(dpre),)
            return cur[CONV_RB - SUBLANES:], acc

        zero8 = jnp.zeros((SUBLANES, CONV_TC), F32)
        _, acc = lax.fori_loop(0, nb, first, (zero8, (zero8,) * (CONV_K + 1)))
        for j in range(CONV_K):
            dw_ref[CONV_K - 1 - j:CONV_K - j, :] = jnp.sum(acc[j], axis=0, keepdims=True)
        db_ref[...] = jnp.sum(acc[CONV_K], axis=0, keepdims=True)

        def second(i, next_first):
            r0 = pl.multiple_of((nb - 1 - i) * CONV_RB, CONV_RB)
            cur = dpre_ref[pl.ds(r0, CONV_RB), :]
            dx = w_ref[CONV_K - 1:CONV_K, :] * cur
            for j in range(1, CONV_K):
                dx = dx + w_ref[CONV_K - 1 - j:CONV_K - j, :] * _rows_after(cur, next_first, j)
            dx_ref[pl.ds(r0, CONV_RB), :] = dx.astype(dx_ref.dtype)
            return cur[:SUBLANES]

        lax.fori_loop(0, nb, second, zero8)

    return pl.pallas_call(
        body, name=name, grid=(CONV_DIM // CONV_TC,),
        in_specs=[pl.BlockSpec((s, CONV_TC), lambda j: (0, _XBC_BLK0 + j)), pl.BlockSpec((CONV_K, CONV_TC), lambda j: (0, j)),
                  pl.BlockSpec((1, CONV_TC), lambda j: (0, j)), pl.BlockSpec((s, CONV_TC), lambda j: (0, j)),
                  pl.BlockSpec(memory_space=pl.ANY)],
        out_specs=[pl.BlockSpec((s, CONV_TC), lambda j: (0, _XBC_BLK0 + j)), pl.BlockSpec((CONV_K, CONV_TC), lambda j: (0, j)),
                   pl.BlockSpec((1, CONV_TC), lambda j: (0, j))],
        out_shape=[jax.ShapeDtypeStruct(dproj.shape, dproj.dtype), jax.ShapeDtypeStruct((CONV_K, CONV_DIM), F32),
                   jax.ShapeDtypeStruct((1, CONV_DIM), F32)], input_output_aliases={4: 0},
        scratch_shapes=[pltpu.VMEM((s, CONV_TC), F32)],
        compiler_params=_cp(VMEM_BIG))(proj, w, b, dxbc, dproj)


def _ssd_common(dtc_ref, br_ref, ar_ref, csb_ref, cst_ref, csf_ref, dtf_ref, expand):
    a_row = -jnp.exp(ar_ref[...])
    dt_c = _softplus(dtc_ref[...] + br_ref[...])
    tril = _iota((CHUNK, CHUNK), 0) >= _iota((CHUNK, CHUNK), 1)
    cs = _sel_dot(tril, dt_c * a_row)
    cst_ref[...] = cs.T
    e64 = (jnp.right_shift(_iota((HPAD, D_INNER), 1), 6) == _iota((HPAD, D_INNER), 0)).astype(jnp.bfloat16)
    if expand:
        e128 = jnp.right_shift(_iota((HPAD, SSM_HEADS * CHUNK), 1), 7) == _iota((HPAD, SSM_HEADS * CHUNK), 0)
        csb_ref[...] = _dot_sel(cs, e128)
        dtf_ref[...] = _dot_sel(dt_c, e64)
        csf_ref[...] = _dot_sel(cs, e64)
    dt_full = dtf_ref[...]
    cs_full = csf_ref[...]
    cs_last = csf_ref[CHUNK - 1:CHUNK, :]
    e_full = jnp.exp(cs_full)
    f_full = jnp.exp(cs_last - cs_full)
    gamma = jnp.exp(cs_last)
    return a_row, dt_c, cs, dt_full, e_full, f_full, gamma, e64


def _ssd_lambda(csb_ref, cst_ref, h, causal):
    diff = csb_ref[:, h * CHUNK:(h + 1) * CHUNK] - cst_ref[h:h + 1, :]
    return jnp.exp(jnp.where(causal, diff, -1e30))


_SSD_VEC_SPECS = lambda: [pl.BlockSpec((1, HPAD), lambda c: (0, 0)), pl.BlockSpec((1, HPAD), lambda c: (0, 0)),
                          pl.BlockSpec((1, D_INNER), lambda c: (0, 0))]


def _ssd_fwd(xbc, dtc, bias_row, alog_row, dfull, name):
    s = xbc.shape[0]
    nc = s // CHUNK

    def body(xbc_ref, dtc_ref, br_ref, ar_ref, df_ref, y_ref, st_ref, csb_ref, csf_ref, dtf_ref, ht_ref, cst_ref):
        @pl.when(pl.program_id(0) == 0)
        def _():
            ht_ref[...] = jnp.zeros_like(ht_ref)

        _, _, _, dt_full, e_full, f_full, gamma, _ = _ssd_common(
            dtc_ref, br_ref, ar_ref, csb_ref, cst_ref, csf_ref, dtf_ref, expand=True)
        x = xbc_ref[:, :D_INNER]
        xdt = x * dt_full
        st_ref[...] = ht_ref[...]
        causal = _iota((CHUNK, CHUNK), 0) >= _iota((CHUNK, CHUNK), 1)
        lo = _iota((CHUNK, CHUNK), 1) < SSM_P
        for g in range(SSM_GROUPS):
            gs = slice(g * SSM_GW, (g + 1) * SSM_GW)
            bg = xbc_ref[:, D_INNER + g * SSM_N:D_INNER + (g + 1) * SSM_N]
            cg = xbc_ref[:, D_INNER + SSM_GROUPS * SSM_N + g * SSM_N:D_INNER + SSM_GROUPS * SSM_N + (g + 1) * SSM_N]
            ht = ht_ref[:, gs]
            cb = _dot_nt(cg, bg)
            yoff = e_full[:, gs] * _dot(cg, ht)
            for jp in range(SSM_GW // CHUNK):
                j = g * (SSM_GW // CHUNK) + jp
                ps = slice(j * CHUNK, (j + 1) * CHUNK)
                x2 = xdt[:, ps]
                y0 = _dot(cb * _ssd_lambda(csb_ref, cst_ref, 2 * j, causal), x2)
                y1 = _dot(cb * _ssd_lambda(csb_ref, cst_ref, 2 * j + 1, causal), x2)
                y_ref[:, ps] = (jnp.where(lo, y0, y1) + yoff[:, jp * CHUNK:(jp + 1) * CHUNK]
                                + x[:, ps] * df_ref[:, ps])
            ht_ref[:, gs] = gamma[:, gs] * ht + _dot_tn(bg, xdt[:, gs] * f_full[:, gs])

    return pl.pallas_call(
        body, name=name, grid=(nc,),
        in_specs=[pl.BlockSpec((CHUNK, CONV_DIM), lambda c: (c, 0)), pl.BlockSpec((CHUNK, HPAD), lambda c: (c, 0))]
                 + _SSD_VEC_SPECS(),
        out_specs=[pl.BlockSpec((CHUNK, D_INNER), lambda c: (c, 0)), pl.BlockSpec((None, SSM_N, D_INNER), lambda c: (c, 0, 0)),
                   pl.BlockSpec((CHUNK, SSM_HEADS * CHUNK), lambda c: (c, 0)), pl.BlockSpec((CHUNK, D_INNER), lambda c: (c, 0)),
                   pl.BlockSpec((CHUNK, D_INNER), lambda c: (c, 0))],
        out_shape=[jax.ShapeDtypeStruct((s, D_INNER), F32), jax.ShapeDtypeStruct((nc, SSM_N, D_INNER), F32),
                   jax.ShapeDtypeStruct((s, SSM_HEADS * CHUNK), F32), jax.ShapeDtypeStruct((s, D_INNER), F32),
                   jax.ShapeDtypeStruct((s, D_INNER), F32)],
        scratch_shapes=[pltpu.VMEM((SSM_N, D_INNER), F32), pltpu.VMEM((HPAD, CHUNK), F32)],
        compiler_params=_cp(VMEM_BIG))(xbc, dtc, bias_row, alog_row, dfull)


def _ssd_bwd(xbc, dtc, bias_row, alog_row, dfull, dy, states, expansions, name):
    s = xbc.shape[0]
    nc = s // CHUNK
    rev = lambda c: nc - 1 - c

    def body(xbc_ref, dtc_ref, br_ref, ar_ref, df_ref, dy_ref, st_ref, csb_ref, csf_ref, dtf_ref,
             dxbc_ref, ddt_ref, dalog_ref, dd_ref, dbias_ref,
             dht_ref, cst_ref, ddf_ref, dxs_ref, dcsf_ref, dcsl_ref):
        step = pl.program_id(0)

        @pl.when(step == 0)
        def _():
            dht_ref[...] = jnp.zeros_like(dht_ref)
            ddf_ref[...] = jnp.zeros_like(ddf_ref)
            dalog_ref[...] = jnp.zeros_like(dalog_ref)
            dbias_ref[...] = jnp.zeros_like(dbias_ref)
            dd_ref[...] = jnp.zeros_like(dd_ref)

        a_row, dt_c, _, dt_full, e_full, f_full, gamma, e64 = _ssd_common(
            dtc_ref, br_ref, ar_ref, csb_ref, cst_ref, csf_ref, dtf_ref, expand=False)
        x = xbc_ref[:, :D_INNER]
        xdt = x * dt_full
        dy_all = dy_ref[...]
        ddf_ref[...] += jnp.broadcast_to(jnp.sum(dy_all * x, axis=0, keepdims=True), ddf_ref.shape)
        causal = _iota((CHUNK, CHUNK), 0) >= _iota((CHUNK, CHUNK), 1)
        lo = _iota((CHUNK, CHUNK), 1) < SSM_P
        head_lane = _iota((CHUNK, HPAD), 1)
        head_row = _iota((HPAD, CHUNK), 0)
        dcs_heads = jnp.zeros((CHUNK, HPAD), F32)
        dcs_cols = jnp.zeros((HPAD, CHUNK), F32)
        for g in range(SSM_GROUPS):
            gs = slice(g * SSM_GW, (g + 1) * SSM_GW)
            b0 = D_INNER + g * SSM_N
            c0 = D_INNER + SSM_GROUPS * SSM_N + g * SSM_N
            bg = xbc_ref[:, b0:b0 + SSM_N]
            cg = xbc_ref[:, c0:c0 + SSM_N]
            ht = st_ref[:, gs]
            dht = dht_ref[:, gs]
            dyg = dy_all[:, gs]
            eg, fg, gg = e_full[:, gs], f_full[:, gs], gamma[:, gs]
            z = _dot(cg, ht)
            dz = dyg * eg
            dcg = _dot_nt(dz, ht)
            dht_new = _dot_tn(cg, dz) + gg * dht
            xf = xdt[:, gs] * fg
            dxf = _dot(bg, dht)
            dbg = _dot_nt(xf, dht)
            dff = dxf * xf
            dcsf_ref[:, gs] = dyg * eg * z - dff
            dcsl_ref[:, gs] = jnp.broadcast_to(
                jnp.sum(dff, axis=0, keepdims=True) + jnp.sum(dht * ht, axis=0, keepdims=True) * gg, (8, SSM_GW))
            cb = _dot_nt(cg, bg)
            dcb = jnp.zeros((CHUNK, CHUNK), F32)
            for jp in range(SSM_GW // CHUNK):
                j = g * (SSM_GW // CHUNK) + jp
                ps = slice(j * CHUNK, (j + 1) * CHUNK)
                x2 = xdt[:, ps]
                dy2 = dy_all[:, ps]
                dxh = []
                for hh in range(2):
                    h = 2 * j + hh
                    lam = _ssd_lambda(csb_ref, cst_ref, h, causal)
                    mh = cb * lam
                    dyh = jnp.where(lo, dy2, 0.0) if hh == 0 else jnp.where(lo, 0.0, dy2)
                    dm = _dot_nt(dyh, x2)
                    dcb = dcb + dm * lam
                    gm = dm * mh
                    dcs_heads = dcs_heads + jnp.where(head_lane == h, jnp.sum(gm, axis=1, keepdims=True), 0.0)
                    dcs_cols = dcs_cols + jnp.where(head_row == h, jnp.sum(gm, axis=0, keepdims=True), 0.0)
                    dxh.append(_dot_tn(mh, dy2))
                dxs_ref[:, ps] = jnp.where(lo, dxh[0], dxh[1]) + dxf[:, jp * CHUNK:(jp + 1) * CHUNK] * fg[:, jp * CHUNK:(jp + 1) * CHUNK]
            dxbc_ref[:, b0:b0 + SSM_N] = (dbg + _dot_tn(dcb, cg)).astype(dxbc_ref.dtype)
            dxbc_ref[:, c0:c0 + SSM_N] = (dcg + _dot(dcb, bg)).astype(dxbc_ref.dtype)
            dht_ref[:, gs] = dht_new
        dxs = dxs_ref[...]
        dcs_heads = dcs_heads - dcs_cols.T + _dot_sel(dcsf_ref[...], e64, ((1,), (1,)))
        dcs_last = _dot_sel(dcsl_ref[...], e64, ((1,), (1,)))
        dcs_heads = dcs_heads + jnp.where(_iota((CHUNK, HPAD), 0) == CHUNK - 1, dcs_last[0:1, :], 0.0)
        triu = _iota((CHUNK, CHUNK), 0) <= _iota((CHUNK, CHUNK), 1)
        dda = _sel_dot(triu, dcs_heads)
        ddt = dda * a_row + _dot_sel(dxs * x, e64, ((1,), (1,)))
        dxbc_ref[:, :D_INNER] = (dxs * dt_full + dy_all * df_ref[...]).astype(dxbc_ref.dtype)
        dalog_ref[...] += jnp.sum(dda * dt_c, axis=0, keepdims=True) * a_row
        ddt_raw = ddt * _sigmoid(dtc_ref[...] + br_ref[...])
        ddt_ref[...] = ddt_raw.astype(ddt_ref.dtype)
        dbias_ref[...] += jnp.sum(ddt_raw, axis=0, keepdims=True)

        @pl.when(step == nc - 1)
        def _():
            dd_ref[...] = _dot_sel(ddf_ref[...], e64, ((1,), (1,)))[0:1, :]

    vec = pl.BlockSpec((1, HPAD), lambda c: (0, 0))
    return pl.pallas_call(
        body, name=name, grid=(nc,),
        in_specs=[pl.BlockSpec((CHUNK, CONV_DIM), lambda c: (rev(c), 0)), pl.BlockSpec((CHUNK, HPAD), lambda c: (rev(c), 0))]
                 + _SSD_VEC_SPECS()
                 + [pl.BlockSpec((CHUNK, D_INNER), lambda c: (rev(c), 0)),
                    pl.BlockSpec((None, SSM_N, D_INNER), lambda c: (rev(c), 0, 0)),
                    pl.BlockSpec((CHUNK, SSM_HEADS * CHUNK), lambda c: (rev(c), 0)),
                    pl.BlockSpec((CHUNK, D_INNER), lambda c: (rev(c), 0)), pl.BlockSpec((CHUNK, D_INNER), lambda c: (rev(c), 0))],
        out_specs=[pl.BlockSpec((CHUNK, CONV_DIM), lambda c: (rev(c), 0)), pl.BlockSpec((CHUNK, HPAD), lambda c: (rev(c), 0)),
                   vec, vec, vec],
        out_shape=[jax.ShapeDtypeStruct((s, CONV_DIM), F32), jax.ShapeDtypeStruct((s, HPAD), _ACT),
                   jax.ShapeDtypeStruct((1, HPAD), F32), jax.ShapeDtypeStruct((1, HPAD), F32),
                   jax.ShapeDtypeStruct((1, HPAD), F32)],
        scratch_shapes=[pltpu.VMEM((SSM_N, D_INNER), F32), pltpu.VMEM((HPAD, CHUNK), F32),
                        pltpu.VMEM((8, D_INNER), F32), pltpu.VMEM((CHUNK, D_INNER), F32),
                        pltpu.VMEM((CHUNK, D_INNER), F32), pltpu.VMEM((8, D_INNER), F32)],
        compiler_params=_cp(VMEM_BIG))(xbc, dtc, bias_row, alog_row, dfull, dy, states, *expansions)


def _gate_fwd(y, proj, gn, name, tm=512):
    s = y.shape[0]
    tm = min(tm, s)

    def body(y_ref, z_ref, gn_ref, o_ref):
        for g in range(SSM_GROUPS):
            gs = slice(g * SSM_GW, (g + 1) * SSM_GW)
            z = z_ref[:, gs]
            t = y_ref[:, gs] * (z * _sigmoid(z))
            r = lax.rsqrt(jnp.mean(t * t, axis=-1, keepdims=True) + EPS)
            o_ref[:, gs] = (t * r * gn_ref[:, gs]).astype(o_ref.dtype)

    row = pl.BlockSpec((tm, D_INNER), lambda i: (i, 0))
    return pl.pallas_call(
        body, name=name, grid=(s // tm,), in_specs=[row, row, pl.BlockSpec((1, D_INNER), lambda i: (0, 0))],
        out_specs=row, out_shape=jax.ShapeDtypeStruct((s, D_INNER + X_WIDTH), _ACT),
        compiler_params=_cp(VMEM_BIG))(y, proj, gn)


def _gate_bwd(y, proj, gn, dcat, name, tm=512):
    s = y.shape[0]
    tm = min(tm, s)

    def body(y_ref, z_ref, gn_ref, dm_ref, dy_ref, dz_ref, dgn_ref):
        @pl.when(pl.program_id(0) == 0)
        def _():
            dgn_ref[...] = jnp.zeros_like(dgn_ref)

        for g in range(SSM_GROUPS):
            gs = slice(g * SSM_GW, (g + 1) * SSM_GW)
            z = z_ref[:, gs]
            yv = y_ref[:, gs]
            sig = _sigmoid(z)
            sz = z * sig
            t = yv * sz
            r = lax.rsqrt(jnp.mean(t * t, axis=-1, keepdims=True) + EPS)
            th = t * r
            dm = dm_ref[:, gs].astype(F32)
            dmg = dm * gn_ref[:, gs]
            dt_ = r * (dmg - th * jnp.mean(dmg * th, axis=-1, keepdims=True))
            dgn_ref[:, gs] += jnp.sum(dm * th, axis=0, keepdims=True)
            dy_ref[:, gs] = dt_ * sz
            dz_ref[:, gs] = (dt_ * yv * (sig * (1.0 + z * (1.0 - sig)))).astype(dz_ref.dtype)

    row = pl.BlockSpec((tm, D_INNER), lambda i: (i, 0))
    vec = pl.BlockSpec((1, D_INNER), lambda i: (0, 0))
    return pl.pallas_call(
        body, name=name, grid=(s // tm,), in_specs=[row, row, vec, row], out_specs=[row, row, vec],
        out_shape=[jax.ShapeDtypeStruct((s, D_INNER), F32), jax.ShapeDtypeStruct((s, 6 * D_MODEL), _ACT),
                   jax.ShapeDtypeStruct((1, D_INNER), F32)], compiler_params=_cp(VMEM_BIG))(y, proj, gn, dcat)


def _block_of(kind, width):
    if kind == "col":
        return lambda ref, j: ref.at[:, :, pl.ds(pl.multiple_of(j * width, 128), width)]
    if kind == "row":
        return lambda ref, j: ref.at[:, pl.ds(pl.multiple_of(j * width, 8), width), :]
    return lambda ref, j: ref.at[j]


def _coords():
    return lax.axis_index("x"), lax.axis_index("y"), lax.axis_index("c")


def _rel_chip(x, y, k):
    return (1 - x if k & 1 else x), (1 - y if k & 2 else y)


def _all_gather_body(ins, outs, send_sems, recv_sems, local_sems, blocks):
    n = len(ins)
    x, y, c = _coords()
    sibling = (x, y, 1 - c)
    via = (x + (1 - c) * (1 - 2 * x), y + c * (1 - 2 * y))
    onto = (x + c * (1 - 2 * x), y + (1 - c) * (1 - 2 * y))

    def copy(t, k, chip, core, to, src=None):
        dst = blocks[t](outs[t], 4 * chip[0] + 2 * chip[1] + core)
        return pltpu.make_async_remote_copy(
            src_ref=dst if src is None else src, dst_ref=dst, send_sem=send_sems.at[t, k],
            recv_sem=recv_sems.at[t, k], device_id=to, device_id_type=MESH)

    started = []
    for t in range(n):
        mine = pltpu.make_async_copy(ins[t], blocks[t](outs[t], 4 * x + 2 * y + c), local_sems.at[t])
        mine.start()
        started.append(mine)
    sends = []
    for t in range(n):
        for k in range(3):
            px, py = _rel_chip(x, y, k)
            cp = copy(t, k, (x, y), c, (px, py, 1 - c if k == 0 else c), src=ins[t])
            cp.start()
            sends.append(cp)
    for t in range(n):
        for k in (1, 2):
            chip = _rel_chip(x, y, k)
            copy(t, k, chip, c, sibling).wait_recv()
            fwd = copy(t, 3 + k, chip, c, sibling)
            fwd.start()
            sends.append(fwd)
        hop = copy(t, 3, via, c, (*onto, c))
        hop.start()
        sends.append(hop)
    for t in range(n):
        diagonal = _rel_chip(x, y, 3)
        copy(t, 3, diagonal, c, sibling).wait_recv()
        fwd = copy(t, 6, diagonal, c, sibling)
        fwd.start()
        sends.append(fwd)
    for t in range(n):
        copy(t, 0, (x, y), 1 - c, sibling).wait_recv()
        for k in range(1, 4):
            copy(t, 3 + k, _rel_chip(x, y, k), 1 - c, sibling).wait_recv()
    for cp in sends:
        cp.wait_send()
    for mine in started:
        mine.wait()


def _handshake(peers):
    barrier = pltpu.get_barrier_semaphore()
    for peer in peers:
        pl.semaphore_signal(barrier, inc=1, device_id=peer, device_id_type=MESH)
    pl.semaphore_wait(barrier, len(peers))


def _gather_peers():
    x, y, c = _coords()
    return [(x, y, 1 - c)] + [(*_rel_chip(x, y, k), c) for k in (1, 2)]


SEQ_ID_GATHER, SEQ_ID_SIBLING, SEQ_ID_CHIPS = 1, 2, 3


def _sequencer_call(body, peers, operands, out_types, sems, name, collective_id, after=()):
    n_in, n_out, n_after = len(operands), len(out_types), len(after)

    def launch(*refs):
        _handshake(peers())
        body(refs[:n_in], refs[n_in + n_after:n_in + n_after + n_out], *refs[n_in + n_after + n_out:])

    return pl.kernel(
        launch, name=name, out_type=out_types, mesh=plsc.ScalarSubcoreMesh(axis_name="seq", num_cores=1),
        scratch_types=sems, compiler_params=pltpu.CompilerParams(collective_id=collective_id))(*operands, *after)


def _all_gather_seq(shards, layouts, name, after=()):
    n = len(shards)
    blocks = [_block_of(kind, width) for kind, width, _ in layouts]
    return _sequencer_call(
        lambda ins, outs, *sems: _all_gather_body(ins, outs, *sems, blocks), _gather_peers, shards,
        [jax.ShapeDtypeStruct(shape, sh.dtype) for sh, (_, _, shape) in zip(shards, layouts)],
        [pltpu.SemaphoreType.DMA((n, 7)), pltpu.SemaphoreType.DMA((n, 7)), pltpu.SemaphoreType.DMA((n,))],
        name, SEQ_ID_GATHER, after)


def _tie(small, after, name):
    del name
    return lax.optimization_barrier((small, *after))[0]


def _rs_to_sibling(grads, layouts, name, after=()):
    n = len(grads)
    blocks = [_block_of(kind, width) for kind, width, _ in layouts]

    def body(ins, outs, send_sems, recv_sems):
        x, y, c = _coords()
        sibling = (x, y, 1 - c)
        cps = []
        for t in range(n):
            for k in range(4):
                px, py = _rel_chip(x, y, k)
                cp = pltpu.make_async_remote_copy(
                    src_ref=blocks[t](ins[t], 4 * px + 2 * py + (1 - c)), dst_ref=outs[t].at[k],
                    send_sem=send_sems.at[t, k], recv_sem=recv_sems.at[t, k], device_id=sibling, device_id_type=MESH)
                cp.start()
                cps.append(cp)
        for cp in cps:
            cp.wait_recv()
        for cp in cps:
            cp.wait_send()

    def sibling_only():
        x, y, c = _coords()
        return [(x, y, 1 - c)]

    return _sequencer_call(
        body, sibling_only, grads,
        [jax.ShapeDtypeStruct((4,) + shape, g.dtype) for g, (_, _, shape) in zip(grads, layouts)],
        [pltpu.SemaphoreType.DMA((n, 4)), pltpu.SemaphoreType.DMA((n, 4))], name, SEQ_ID_SIBLING, after)


def _rs_chip_sum(grad, recv, layout, xyc, name):
    kind, width, shape = layout
    r, ccols = shape

    def src_index(k, xyc_ref):
        px = jnp.where(k % 2 == 1, 1 - xyc_ref[0], xyc_ref[0])
        py = jnp.where(k // 2 == 1, 1 - xyc_ref[1], xyc_ref[1])
        return 4 * px + 2 * py + xyc_ref[2]

    if kind == "col":
        g_spec = pl.BlockSpec((r, ccols), lambda k, s_: (0, src_index(k, s_)))
    elif kind == "row":
        g_spec = pl.BlockSpec((r, ccols), lambda k, s_: (src_index(k, s_), 0))
    else:
        g_spec = pl.BlockSpec((None, r, ccols), lambda k, s_: (src_index(k, s_), 0, 0))

    def body(xyc_ref, g_ref, r_ref, o_ref):
        o_ref[...] = (g_ref[...].astype(F32) + r_ref[...].astype(F32)).astype(o_ref.dtype)

    slot = pl.BlockSpec((None, r, ccols), lambda k, s_: (k, 0, 0))
    return pl.pallas_call(
        body, name=name,
        grid_spec=pltpu.PrefetchScalarGridSpec(num_scalar_prefetch=1, grid=(4,), in_specs=[g_spec, slot], out_specs=slot),
        out_shape=jax.ShapeDtypeStruct((4, r, ccols), grad.dtype), compiler_params=_cp(VMEM_BIG))(xyc, grad, recv)


def _rs_across_chips(parts, name):
    n = len(parts)

    def body(ins, outs, send_sems, recv_sems):
        x, y, c = _coords()
        cps = []
        for t in range(n):
            for k in range(1, 4):
                px, py = _rel_chip(x, y, k)
                cp = pltpu.make_async_remote_copy(
                    src_ref=ins[t].at[k], dst_ref=outs[t].at[k - 1], send_sem=send_sems.at[t, k - 1],
                    recv_sem=recv_sems.at[t, k - 1], device_id=(px, py, c), device_id_type=MESH)
                cp.start()
                cps.append(cp)
        for cp in cps:
            cp.wait_recv()
        for cp in cps:
            cp.wait_send()

    def other_chips():
        x, y, c = _coords()
        return [(*_rel_chip(x, y, k), c) for k in range(1, 4)]

    return _sequencer_call(
        body, other_chips, parts, [jax.ShapeDtypeStruct((3,) + p.shape[1:], p.dtype) for p in parts],
        [pltpu.SemaphoreType.DMA((n, 3)), pltpu.SemaphoreType.DMA((n, 3))], name, SEQ_ID_CHIPS)


def _adamw_math(w, g, m, v):
    m = ADAM_B1 * m + (1.0 - ADAM_B1) * g
    v = ADAM_B2 * v + (1.0 - ADAM_B2) * jnp.square(g)
    m_hat = m / (1.0 - ADAM_B1 ** ADAM_STEP)
    v_hat = v / (1.0 - ADAM_B2 ** ADAM_STEP)
    delta = -ADAM_LR * (m_hat / (jnp.sqrt(v_hat) + ADAM_EPS) + ADAM_WD * w)
    return delta, m, v


def _row_tile(rows, cap):
    best = None
    for cand in range(8, min(rows, cap) + 1, 8):
        if rows % cand == 0:
            best = cand
    assert best is not None, rows
    return best


def _adamw(w, m, v, parts, name, layer=None, prev=None, tr=256):
    r, ccols = w.shape[-2:]
    npart = len(parts)
    if r % 8 == 0:
        tr, tc = _row_tile(r, tr), ccols
        steps, at = r // tr, (lambda i: (i, 0))
    else:
        tr, tc = r, 256
        assert ccols % tc == 0
        steps, at = ccols // tc, (lambda i: (0, i))

    def spec(lead):
        if lead is None:
            return pl.BlockSpec((tr, tc), at)
        return pl.BlockSpec((None, tr, tc), lambda i: (lead,) + at(i))

    wspec = lambda: spec(layer)
    pspec = spec

    def body(*refs):
        w_ref, m_ref, v_ref = refs[:3]
        p_refs = refs[3:3 + npart]
        outs = refs[len(refs) - 4:]
        g = p_refs[0][...].astype(F32)
        for p_ref in p_refs[1:]:
            g = g + p_ref[...].astype(F32)
        delta, mn, vn = _adamw_math(w_ref[...], g, m_ref[...], v_ref[...])
        outs[0][...] = g
        outs[1][...] = delta
        outs[2][...] = mn
        outs[3][...] = vn

    operands = [w, m, v] + [p for p, _ in parts]
    in_specs = [wspec(), wspec(), wspec()] + [pspec(lead) for _, lead in parts]
    aliases = {}
    if prev is not None:
        for i, p in enumerate(prev):
            aliases[len(operands)] = i
            operands.append(p)
            in_specs.append(pl.BlockSpec(memory_space=pl.ANY))
    return pl.pallas_call(
        body, name=name, grid=(steps,), in_specs=in_specs, out_specs=[wspec()] * 4,
        out_shape=[jax.ShapeDtypeStruct(w.shape, F32)] * 4, input_output_aliases=aliases,
        compiler_params=_cp(VMEM_BIG))(*operands)


def _small_update(gathered, params, loss_all, me, name):
    n = len(gathered)
    shapes = [w.shape for w, _, _ in params]

    def body(me_ref, *refs):
        g_refs, loss_ref = refs[:n], refs[n]
        p_refs = refs[n + 1:n + 1 + 3 * n]
        o_refs = refs[n + 1 + 3 * n:]
        for i in range(n):
            r, c = shapes[i]
            if gathered[i].shape[2] == c:
                parts = [g_refs[i][j] for j in range(N_DEV)]
            else:
                off = pl.multiple_of(me_ref[0] * c, 128)
                parts = [g_refs[i][j, :, pl.ds(off, c)] for j in range(N_DEV)]
            g = functools.reduce(lambda a, b: a + b, parts)
            delta, mn, vn = _adamw_math(p_refs[3 * i][...], g, p_refs[3 * i + 1][...], p_refs[3 * i + 2][...])
            for k, val in enumerate((g, delta, mn, vn)):
                o_refs[4 * i + k][...] = val
        o_refs[4 * n][...] = functools.reduce(lambda a, b: a + b, [loss_ref[j] for j in range(N_DEV)])

    vmem = pl.BlockSpec(memory_space=pltpu.VMEM)
    flat_params = [a for p in params for a in p]
    outs = pl.pallas_call(
        body, name=name, in_specs=[pl.BlockSpec(memory_space=pltpu.SMEM)] + [vmem] * (n + 1 + 3 * n),
        out_specs=[vmem] * (4 * n + 1),
        out_shape=[jax.ShapeDtypeStruct(shp, F32) for shp in shapes for _ in range(4)] + [jax.ShapeDtypeStruct((1, 128), F32)],
        compiler_params=_cp(VMEM_BIG))(me, *gathered, loss_all, *flat_params)
    return [tuple(outs[4 * i:4 * i + 4]) for i in range(n)], outs[4 * n]


def _pack(arrays):
    pieces, layout, off = [], [], 0
    for a in arrays:
        n = a.size
        padded = -(-n // 1024) * 1024
        flat = a.reshape(-1).astype(F32)
        if padded != n:
            flat = jnp.pad(flat, (0, padded - n))
        pieces.append(flat.reshape(padded // 128, 128))
        layout.append((off, n, a.shape))
        off += padded // 128
    return jnp.concatenate(pieces, axis=0), layout


def kernel(x, mem, norm_mix, norm_ffn, mem_norm, w_kv, w_out, w_ffn1, w_ffn2, a_in, a_ln_g, a_ln_b, a_ws, a_bs, b_in, b_conv_w, b_conv_b, b_dt_bias, b_a_log, b_d, b_gnorm, final_norm, loss_target, m_norm_mix, m_norm_ffn, m_mem_norm, m_w_kv, m_w_out, m_w_ffn1, m_w_ffn2, m_a_in, m_a_ln_g, m_a_ln_b, m_a_ws, m_a_bs, m_b_in, m_b_conv_w, m_b_conv_b, m_b_dt_bias, m_b_a_log, m_b_d, m_b_gnorm, m_final_norm, v_norm_mix, v_norm_ffn, v_mem_norm, v_w_kv, v_w_out, v_w_ffn1, v_w_ffn2, v_a_in, v_a_ln_g, v_a_ln_b, v_a_ws, v_a_bs, v_b_in, v_b_conv_w, v_b_conv_b, v_b_dt_bias, v_b_a_log, v_b_d, v_b_gnorm, v_final_norm):
    s = x.shape[1]
    xs = x.reshape(s, D_MODEL)
    mems = mem.reshape(N_MEM, D_MODEL)
    target = loss_target.reshape(s, D_MODEL)
    ax, ay, ac = lax.axis_index("x"), lax.axis_index("y"), lax.axis_index("c")
    me = 4 * ax + 2 * ay + ac
    xyc = jnp.stack([ax, ay, ac]).astype(jnp.int32)

    b_cols = b_in.shape[2]
    act = lambda a: a.astype(_ACT)
    lay_f1, lay_f2 = ("col", 512, (1, D_MODEL, D_FF)), ("row", 512, (1, D_FF, D_MODEL))
    lay_out, lay_kv = ("row", 384, (1, 3 * D_MODEL, D_MODEL)), ("col", 256, (1, D_MODEL, 2 * X_WIDTH))
    small_w_pack = _pack([b_conv_w[0], b_conv_b[0], b_gnorm[0]])[0]
    (WA,) = _all_gather_seq([act(a_in)], [("col", 640, (1, D_MODEL, 5 * D_MODEL))], "ag_proj_a")
    wo0, wkv0 = _all_gather_seq([act(w_out[0:1]), act(w_kv[0:1])], [lay_out, lay_kv], "ag_out0")
    w1_0, w2_0 = _all_gather_seq([act(w_ffn1[0:1]), act(w_ffn2[0:1])], [lay_f1, lay_f2], "ag_ffn0")
    a0 = _rms_fwd(xs, norm_mix[0].reshape(1, -1), "mix_norm0")
    tr_b = lambda a: jnp.swapaxes(a, 1, 2)
    wbt_blk, small_w = _all_gather_seq(
        [act(tr_b(b_in)[0]), small_w_pack],
        [("blk", 0, (N_DEV, b_cols, D_MODEL)), ("blk", 0, (N_DEV, 32, 128))], "ag_proj_b", after=[a0])
    wo1, wkv1 = _all_gather_seq([act(w_out[1:2]), act(w_kv[1:2])], [lay_out, lay_kv], "ag_out1", after=[a0])
    w1_1, w2_1 = _all_gather_seq([act(w_ffn1[1:2]), act(w_ffn2[1:2])], [lay_f1, lay_f2], "ag_ffn1", after=[a0])
    W1, W2, WO, WKV = [w1_0, w1_1], [w2_0, w2_1], [wo0, wo1], [wkv0, wkv1]
    dt0 = D_INNER + CONV_DIM

    row = lambda a: a.reshape(1, -1)
    nmix = [row(norm_mix[0]), row(norm_mix[1])]
    nffn = [row(norm_ffn[0]), row(norm_ffn[1])]
    nmem = [row(mem_norm[0]), row(mem_norm[1])]
    fin = row(final_norm)
    lng, lnb = a_ln_g.reshape(1, D_INNER), a_ln_b.reshape(1, D_INNER)
    ws = a_ws[0]
    bs3 = a_bs[0].reshape(A_GROUPS, CHUNK, 1)
    pad_h = lambda a: jnp.pad(a.reshape(-1), (0, HPAD - SSM_HEADS))
    bias_row = pad_h(b_dt_bias).reshape(1, HPAD)
    alog_row = pad_h(b_a_log).reshape(1, HPAD)
    dfull = jnp.repeat(b_d.reshape(-1), SSM_P).reshape(1, D_INNER)

    kvs, mns = [None, None], [None, None]

    def mem_kv(i, after=None):
        gain = nmem[i] if after is None else _tie(nmem[i], after, f"tie_mem{i}")
        mns[i] = _rms_fwd(mems, gain, f"mem_norm{i}")
        kvs[i] = _mm(mns[i], WKV[i], m=N_MEM, n=2 * X_WIDTH, k=D_MODEL, b_at=(0, 0, 0), out_dtype=_ACT, name=f"kv{i}")

    def ffn_fwd(h, i):
        f = _rms_fwd(h, nffn[i], f"ffn_norm{i}")
        p = _mm(f, W1[i], m=s, n=D_FF, k=D_MODEL, b_at=(0, 0, 0), out_dtype=_ACT, name=f"ffn_up{i}")
        hn = _mm(p, W2[i], m=s, n=D_MODEL, k=D_FF, b_at=(0, 0, 0), a_pro="relu2", add=h, name=f"ffn_down{i}")
        return f, p, hn

    def out_proj(h, cat, i):
        return _mm(cat, WO[i], m=s, n=D_MODEL, k=3 * D_MODEL, b_at=(0, 0, 0), add=h, name=f"out_proj{i}")

    proj_a = _mm(a0, WA, m=s, n=5 * D_MODEL, k=D_MODEL, b_at=(0, 0, 0), name="proj_a")
    mem_kv(0, after=[proj_a])
    cat_a = _gmlp_fwd(proj_a, lng, lnb, ws, bs3, "gmlp_fwd")
    cat_a = _attn_fwd(proj_a, 4, kvs[0], cat_a, "attn_fwd0")
    h1 = out_proj(xs, cat_a, 0)
    f0, p0, h2 = ffn_fwd(h1, 0)

    wbt_blk, small_w, _ = lax.optimization_barrier((wbt_blk, small_w, p0))
    jd, lo = divmod(dt0, b_cols)
    assert lo + SSM_HEADS <= b_cols
    wbt_full = wbt_blk.reshape(N_DEV * b_cols, D_MODEL)
    WBT = jnp.concatenate([wbt_full[:dt0], wbt_full[dt0 + SSM_HEADS:]], axis=0)
    WBDT = jnp.pad(wbt_full[dt0:dt0 + SSM_HEADS], ((0, HPAD - SSM_HEADS), (0, 0)))
    cw_sh, cb_sh, gn_sh = 4 * 384, 384, 256
    sw = small_w.reshape(N_DEV, 32 * 128)
    conv_w = jnp.transpose(sw[:, :cw_sh].reshape(N_DEV, CONV_K, 384), (1, 0, 2)).reshape(CONV_K, CONV_DIM)
    conv_b = sw[:, 2048:2048 + cb_sh].reshape(1, CONV_DIM)
    gnorm = sw[:, 3072:3072 + gn_sh].reshape(1, D_INNER)

    a1 = _rms_fwd(h2, nmix[1], "mix_norm1")
    proj_b = _mm(a1, WBT, m=s, n=6 * D_MODEL, k=D_MODEL, tb=True, name="proj_b")
    dt_raw = _mm(a1, WBDT, m=s, n=HPAD, k=D_MODEL, tb=True, name="proj_dt")
    xbc = _conv_fwd(proj_b, conv_w, conv_b, "conv_fwd")
    y_ssd, states, *ssd_expansions = _ssd_fwd(xbc, dt_raw, bias_row, alog_row, dfull, "ssd_fwd")
    cat_b = _gate_fwd(y_ssd, proj_b, gnorm, "gate_fwd")
    mem_kv(1, after=[cat_b])
    cat_b = _attn_fwd(proj_b, 5, kvs[1], cat_b, "attn_fwd1")
    h3 = out_proj(h2, cat_b, 1)
    f1, p1, h4 = ffn_fwd(h3, 1)

    loss_part, dh, dh_act, d_fin = _loss_head(h4, fin, target, "loss_head")

    g_f1, g_f2, g_out, g_kv = [None, None], [None, None], [None, None], [None, None]
    d_nffn, d_nmix, d_nmem = [None, None], [None, None], [None, None]

    def ffn_bwd(dh, dh_act, h_in, f, p, i, after=(), after_last=()):
        dp = _mm(dh_act, W2[i], m=s, n=D_FF, k=D_MODEL, tb=True, b_at=(0, 0, 0), epi_p=p, out_dtype=_ACT, name=f"ffn_down_dx{i}")
        g_f2[i] = _mm(p, dh_act, m=D_FF, n=D_MODEL, k=s, ta=True, a_pro="relu2", out_dtype=_ACT, name=f"ffn_down_dw{i}")
        g_f1[i] = _mm(f, dp, m=D_MODEL, n=D_FF, k=s, ta=True, out_dtype=_ACT, name=f"ffn_up_dw{i}")
        df = _mm(dp, W1[i], m=s, n=D_MODEL, k=D_FF, tb=True, b_at=(0, 0, 0), after=after, name=f"ffn_up_dx{i}")
        gain = _tie(nffn[i], after_last, f"tie_ffn_norm{i}") if after_last else nffn[i]
        dh_in, dh_in_act, d_nffn[i] = _rms_bwd(h_in, gain, df, dh, f"ffn_norm_bwd{i}")
        return dh_in, dh_in_act

    def out_bwd(dh_act, cat, i):
        dcat = _mm(dh_act, WO[i], m=s, n=3 * D_MODEL, k=D_MODEL, tb=True, b_at=(0, 0, 0), out_dtype=_ACT, name=f"out_dx{i}")
        g_out[i] = _mm(cat, dh_act, m=3 * D_MODEL, n=D_MODEL, k=s, ta=True, out_dtype=_ACT, name=f"out_dw{i}")
        return dcat

    def mem_bwd(dkv, i):
        g_kv[i] = _mm(mns[i], dkv, m=D_MODEL, n=2 * X_WIDTH, k=N_MEM, ta=True, out_dtype=_ACT, name=f"kv_dw{i}")
        dmn = _mm(dkv, WKV[i], m=N_MEM, n=D_MODEL, k=2 * X_WIDTH, tb=True, b_at=(0, 0, 0), name=f"kv_dx{i}")
        _, _, d_nmem[i] = _rms_bwd(mems, nmem[i], dmn, None, f"mem_norm_bwd{i}")

    lay_g = {"f1": ("col", 512, (D_MODEL, 512)), "f2": ("row", 512, (512, D_MODEL)), "out": ("row", 384, (384, D_MODEL)),
             "kv": ("col", 256, (D_MODEL, 256)), "a": ("col", 640, (D_MODEL, 640)), "b": ("blk", 0, (b_cols, D_MODEL))}
    reduced = {}

    def reduce_scatter(group, tag, after=(), sums_after=()):
        grads3, lays3 = [], []
        for fam, _, g in group:
            kind, width, shape = lay_g[fam]
            grads3.append(g if kind == "blk" else g.reshape((1,) + g.shape))
            lays3.append((kind, width, shape if kind == "blk" else (1,) + shape))
        recv1 = _rs_to_sibling(grads3, lays3, f"rs_sibling_{tag}", after)
        if sums_after:
            recv1 = lax.optimization_barrier((tuple(recv1), tuple(sums_after)))[0]
        parts = [_rs_chip_sum(g, recv1[t].reshape((4,) + lay_g[fam][2]), lay_g[fam], xyc, f"rs_chip_sum_{fam}{i}")
                 for t, (fam, i, g) in enumerate(group)]
        recv2 = _rs_across_chips(parts, f"rs_chips_{tag}")
        for (fam, i, _), p, r2 in zip(group, parts, recv2):
            reduced[fam, i] = (p, r2)
        return parts, recv2

    dh3, dh3_act = ffn_bwd(dh, dh_act, h3, f1, p1, 1)
    dcat_b = out_bwd(dh3_act, cat_b, 1)
    sums, got_ffn1 = reduce_scatter([("f1", 1, g_f1[1]), ("f2", 1, g_f2[1]), ("out", 1, g_out[1])], "ffn1", sums_after=[dcat_b])
    dy_ssd, dproj_b, d_gnorm = _gate_bwd(y_ssd, proj_b, gnorm, dcat_b, "gate_bwd")
    dproj_b, dkv_b = _attn_bwd(proj_b, 5, kvs[1], dcat_b, dproj_b, "attn_bwd1")
    mem_bwd(dkv_b, 1)
    dxbc, ddt_raw, d_alog, d_dskip, d_dtbias = _ssd_bwd(
        xbc, dt_raw, _tie(bias_row, sums, "tie_ffn1"), alog_row, dfull, dy_ssd, states, ssd_expansions, "ssd_bwd")
    dproj_b, d_convw, d_convb = _conv_bwd(proj_b, conv_w, _tie(conv_b, got_ffn1, "tie_got_ffn1"), dxbc, dproj_b, "conv_bwd")
    gb = _mm(dproj_b, a1, m=6 * D_MODEL, n=D_MODEL, k=s, ta=True, out_dtype=_ACT, name="proj_b_dw")
    gb_dt = _mm(ddt_raw, a1, m=HPAD, n=D_MODEL, k=s, ta=True, out_dtype=_ACT, name="proj_b_dw_dt")
    blocks_b = [gb[j * b_cols:(j + 1) * b_cols] for j in range(jd)]
    blocks_b.append(jnp.concatenate([gb[jd * b_cols:dt0], gb_dt[:SSM_HEADS], gb[dt0:(jd + 1) * b_cols - SSM_HEADS]], axis=0))
    blocks_b += [gb[j * b_cols - SSM_HEADS:(j + 1) * b_cols - SSM_HEADS] for j in range(jd + 1, N_DEV)]
    gb_blk = jnp.stack(blocks_b)
    da1 = _mm(dproj_b, WBT, m=s, n=D_MODEL, k=6 * D_MODEL, name="proj_b_dx")
    sums, got_mix1 = reduce_scatter([("kv", 1, g_kv[1]), ("b", 0, gb_blk)], "mix1", sums_after=[da1])
    da1 = _mm(ddt_raw, WBDT, m=s, n=D_MODEL, k=HPAD, add=da1, name="proj_b_dx_dt")
    dh2, dh2_act, d_nmix[1] = _rms_bwd(h2, _tie(nmix[1], sums, "tie_mix1"), da1, dh3, "mix_norm_bwd1")

    dh1, dh1_act = ffn_bwd(dh2, dh2_act, h1, f0, p0, 0, after=got_ffn1, after_last=got_mix1)
    dcat_a = out_bwd(dh1_act, cat_a, 0)
    sums, got_ffn0 = reduce_scatter([("f1", 0, g_f1[0]), ("f2", 0, g_f2[0]), ("out", 0, g_out[0])], "ffn0", sums_after=[dcat_a])
    dproj_a, d_ws, d_bs3, d_lng, d_lnb = _gmlp_bwd(proj_a, dcat_a, _tie(lng, sums, "tie_ffn0"), lnb, ws, bs3, "gmlp_bwd")
    dproj_a, dkv_a = _attn_bwd(proj_a, 4, kvs[0], dcat_a, dproj_a, "attn_bwd0")
    mem_bwd(dkv_a, 0)

    def big_update(w, m, v, fam, nlayer):
        res = None
        for i in range(nlayer):
            part, recv2 = reduced[fam, i]
            plist = [(part, 0), (recv2, 0), (recv2, 1), (recv2, 2)]
            res = _adamw(w, m, v, plist, f"adamw_{fam}{i}", layer=i, prev=res)
        return res

    da0 = _mm(dproj_a, WA, m=s, n=D_MODEL, k=5 * D_MODEL, tb=True, b_at=(0, 0, 0), name="proj_a_dx")
    grad_x, _, d_nmix[0] = _rms_bwd(xs, nmix[0], da0, dh1, "mix_norm_bwd0")
    ga = _mm(a0, dproj_a, m=D_MODEL, n=5 * D_MODEL, k=s, ta=True, out_dtype=_ACT, after=[grad_x], name="proj_a_dw")
    r_b = big_update(tr_b(b_in), tr_b(m_b_in), tr_b(v_b_in), "b", 1)
    reduce_scatter([("kv", 0, g_kv[0]), ("a", 0, ga)], "mix0", after=got_ffn0, sums_after=r_b)
    r_b = [tr_b(o) for o in r_b]

    small_names = ["norm_mix", "norm_ffn", "mem_norm", "a_ln_g", "a_ln_b", "a_ws", "a_bs", "b_dt_bias", "b_a_log", "b_d",
                   "final_norm", "b_conv_w", "b_conv_b", "b_gnorm"]
    small_grads = [jnp.concatenate(d_nmix, axis=0), jnp.concatenate(d_nffn, axis=0), jnp.concatenate(d_nmem, axis=0),
                   d_lng, d_lnb, d_ws.reshape(A_GROUPS * CHUNK, CHUNK), d_bs3.reshape(A_GROUPS, CHUNK),
                   d_dtbias[:, :SSM_HEADS], d_alog[:, :SSM_HEADS], d_dskip[:, :SSM_HEADS], d_fin,
                   d_convw, d_convb, d_gnorm]
    small_2d = [(2, D_MODEL)] * 3 + [(1, D_INNER)] * 2 + [(A_GROUPS * CHUNK, CHUNK), (A_GROUPS, CHUNK)] + [(1, SSM_HEADS)] * 3 \
        + [(1, D_MODEL), (CONV_K, 384), (1, 384), (1, 256)]
    gathered = _all_gather_seq(
        small_grads + [loss_part], [("blk", 0, (N_DEV,) + g.shape) for g in small_grads + [loss_part]], "ag_small_grads")

    r_f1 = big_update(w_ffn1, m_w_ffn1, v_w_ffn1, "f1", 2)
    r_f2 = big_update(w_ffn2, m_w_ffn2, v_w_ffn2, "f2", 2)
    r_out = big_update(w_out, m_w_out, v_w_out, "out", 2)
    r_kv = big_update(w_kv, m_w_kv, v_w_kv, "kv", 2)
    r_a = big_update(a_in, m_a_in, v_a_in, "a", 1)

    small_w = [norm_mix, norm_ffn, mem_norm, a_ln_g, a_ln_b, a_ws, a_bs, b_dt_bias, b_a_log, b_d, final_norm,
               b_conv_w, b_conv_b, b_gnorm]
    small_m = [m_norm_mix, m_norm_ffn, m_mem_norm, m_a_ln_g, m_a_ln_b, m_a_ws, m_a_bs, m_b_dt_bias, m_b_a_log, m_b_d,
               m_final_norm, m_b_conv_w, m_b_conv_b, m_b_gnorm]
    small_v = [v_norm_mix, v_norm_ffn, v_mem_norm, v_a_ln_g, v_a_ln_b, v_a_ws, v_a_bs, v_b_dt_bias, v_b_a_log, v_b_d,
               v_final_norm, v_b_conv_w, v_b_conv_b, v_b_gnorm]
    params = [tuple(a.reshape(shp) for a in wmv) for shp, wmv in zip(small_2d, zip(small_w, small_m, small_v))]
    loss_all = _tie(gathered[-1], [r_a[0], r_kv[0]], "tie_small")
    small_res, loss_sum = _small_update(gathered[:-1], params, loss_all, me.astype(jnp.int32).reshape(1), "adamw_small")
    loss = loss_sum[0, 0]

    names = ["norm_mix", "norm_ffn", "mem_norm", "w_kv", "w_out", "w_ffn1", "w_ffn2", "a_in", "a_ln_g", "a_ln_b", "a_ws",
             "a_bs", "b_in", "b_conv_w", "b_conv_b", "b_dt_bias", "b_a_log", "b_d", "b_gnorm", "final_norm"]
    big = {"w_kv": r_kv, "w_out": r_out, "w_ffn1": r_f1, "w_ffn2": r_f2, "a_in": r_a, "b_in": r_b}
    outs = [loss, grad_x.reshape(x.shape)]
    for kind in range(4):
        for nm in names:
            if nm in big:
                outs.append(big[nm][kind])
            else:
                i = small_names.index(nm)
                outs.append(small_res[i][kind].reshape(small_w[i].shape))
    return tuple(outs)
```

```python
import functools
import math

import jax
import jax.numpy as jnp
from jax import lax
from jax.experimental import pallas as pl
from jax.experimental.pallas import tpu as pltpu
from jax.experimental.pallas import tpu_sc as plsc

F32 = jnp.float32
_MXU = jnp.bfloat16
_ACT = jnp.bfloat16

D_MODEL = 1024
CHUNK = 128
N_MEM = 256
D_INNER = 2048
A_GROUPS = 8
A_GW = D_INNER // A_GROUPS
SSM_HEADS = 32
SSM_P = 64
SSM_GROUPS = 4
SSM_GW = D_INNER // SSM_GROUPS
SSM_N = 128
CONV_K = 4
CONV_DIM = 3072
X_HEADS = 4
X_HD = 256
X_WIDTH = 1024
D_FF = 4096
EPS = 1e-6
HPAD = 128
N_DEV = 8

ADAM_LR = 0.001
ADAM_B1 = 0.9
ADAM_B2 = 0.999
ADAM_EPS = 1e-08
ADAM_WD = 0.01
ADAM_STEP = 10

VMEM_BIG = 56 * 1024 * 1024
MESH = pl.DeviceIdType.MESH


def _cp(vmem=None):
    if vmem is None:
        return pltpu.CompilerParams()
    return pltpu.CompilerParams(vmem_limit_bytes=vmem)


def _dot(a, b, dims=((1,), (0,))):
    return lax.dot_general(a.astype(_MXU), b.astype(_MXU), (dims, ((), ())), preferred_element_type=F32)


def _dot_nt(a, b):
    return _dot(a, b, ((1,), (1,)))


def _dot_tn(a, b):
    return _dot(a, b, ((0,), (0,)))


def _split3(x):
    x1 = x.astype(jnp.bfloat16)
    r = x - x1.astype(F32)
    x2 = r.astype(jnp.bfloat16)
    x3 = (r - x2.astype(F32)).astype(jnp.bfloat16)
    return x1, x2, x3


def _dot_sel(x, sel, dims=((1,), (0,)), terms=2):
    sel = sel.astype(jnp.bfloat16)
    parts = [lax.dot_general(t, sel, (dims, ((), ())), preferred_element_type=F32) for t in _split3(x)[:terms]]
    return functools.reduce(lambda a, b: a + b, parts)


def _sel_dot(sel, x, dims=((1,), (0,))):
    sel = sel.astype(jnp.bfloat16)
    parts = [lax.dot_general(sel, t, (dims, ((), ())), preferred_element_type=F32) for t in _split3(x)]
    return (parts[0] + parts[1]) + parts[2]


def _sigmoid(x):
    return 1.0 / (1.0 + jnp.exp(-x))


def _gelu(x):
    return 0.5 * x * (1.0 + lax.erf(x * (1.0 / math.sqrt(2.0))))


def _gelu_with_grad(x):
    phi = 0.5 * (1.0 + lax.erf(x * (1.0 / math.sqrt(2.0))))
    return x * phi, phi + x * jnp.exp(-0.5 * x * x) * (1.0 / math.sqrt(2.0 * math.pi))


def _softplus(x):
    return jnp.maximum(x, 0.0) + jnp.log1p(jnp.exp(-jnp.abs(x)))


def _iota(shape, dim):
    return lax.broadcasted_iota(jnp.int32, shape, dim)


MM_VMEM_BUDGET = 40 * 1024 * 1024
HBM_BYTES_PER_S = 2.5e12
GRID_STEP_S = 0.35e-6
VMEM_ACC_BYTES_PER_S = 6e12


def _divisors(dim, unit):
    out = [d for d in range(unit, min(dim, 2048) + 1, unit) if dim % d == 0]
    return out if out else [dim]


def _mm_tiles(m, n, k, sa, sb, s_mn, a_pro, offsets):
    best = None
    (a_r0, a_c0, ta), (b_r0, b_c0, tb), (o_r0, o_c0) = offsets
    for tm in _divisors(m, 128):
        for tn in _divisors(n, 128):
            for tk in [k // d for d in (1, 2, 3, 4, 6, 8) if k % d == 0 and (k // d) % 128 == 0]:
                a_t = (tk, tm) if ta else (tm, tk)
                b_t = (tn, tk) if tb else (tk, tn)
                if a_r0 % a_t[0] or a_c0 % a_t[1] or b_r0 % b_t[0] or b_c0 % b_t[1] or o_r0 % tm or o_c0 % tn:
                    continue
                nk = k // tk
                vmem = 2 * (tm * tk * sa + tk * tn * sb + tm * tn * s_mn) + tm * tn * 4 * (2 if nk > 1 else 1)
                if a_pro or sa == 4:
                    vmem += tm * tk * 6
                if sb == 4:
                    vmem += tk * tn * 2
                if vmem > MM_VMEM_BUDGET:
                    continue
                gi, gj = m // tm, n // tn
                for j_inner in (True, False):
                    if nk > 1:
                        traffic = gj * m * k * sa + gi * k * n * sb
                    elif j_inner:
                        traffic = m * k * sa + gi * k * n * sb
                    else:
                        traffic = gj * m * k * sa + k * n * sb
                    traffic += m * n * s_mn + (tm * tk * sa + tk * tn * sb)
                    cost = traffic / HBM_BYTES_PER_S + gi * gj * nk * GRID_STEP_S
                    if nk > 1:
                        cost += m * n * 8 * nk / VMEM_ACC_BYTES_PER_S
                    if best is None or cost < best[0]:
                        best = (cost, tm, tn, tk, j_inner)
    assert best is not None, (m, n, k)
    return best[1:]


def _mm(a, b, *, m, n, k, name, ta=False, tb=False, a_at=(None, 0, 0), b_at=(None, 0, 0),
        out_dtype=F32, add=None, epi_p=None, epi_at=(None, 0, 0), out=None, out_at=(None, 0, 0),
        out_full=None, a_pro=None, after=()):
    s_mn =jnp.dtype(out.dtype if out is not None else out_dtype).itemsize
    s_mn += add.dtype.itemsize if add is not None else 0
    s_mn += epi_p.dtype.itemsize if epi_p is not None else 0
    tm, tn, tk, j_inner = _mm_tiles(m, n, k, a.dtype.itemsize, b.dtype.itemsize, s_mn, a_pro is not None,
                                    ((a_at[1], a_at[2], ta), (b_at[1], b_at[2], tb), (out_at[1], out_at[2])))
    nk = k // tk

    def spec(at, tr, tc, rsel, csel):
        lead, r0, c0 = at
        assert r0 % tr == 0 and c0 % tc == 0, (name, at, tr, tc)
        rb, cb = r0 // tr, c0 // tc
        if lead is None:
            return pl.BlockSpec((tr, tc), lambda g0, g1, kk: (rb + rsel(g0, g1, kk), cb + csel(g0, g1, kk)))
        return pl.BlockSpec((None, tr, tc), lambda g0, g1, kk: (lead, rb + rsel(g0, g1, kk), cb + csel(g0, g1, kk)))

    gi = (lambda g0, g1, kk: g0) if j_inner else (lambda g0, g1, kk: g1)
    gj = (lambda g0, g1, kk: g1) if j_inner else (lambda g0, g1, kk: g0)
    gk = lambda g0, g1, kk: kk
    a_spec = spec(a_at, tk, tm, gk, gi) if ta else spec(a_at, tm, tk, gi, gk)
    b_spec = spec(b_at, tn, tk, gj, gk) if tb else spec(b_at, tk, tn, gk, gj)
    dims = ((0,), (0,)) if ta else (((1,), (1,)) if tb else ((1,), (0,)))
    assert not (ta and tb)

    operands, in_specs = [a, b], [a_spec, b_spec]
    if add is not None:
        operands.append(add)
        in_specs.append(spec((None, 0, 0), tm, tn, gi, gj))
    if epi_p is not None:
        operands.append(epi_p)
        in_specs.append(spec(epi_at, tm, tn, gi, gj))
    aliases = {}
    if out is not None:
        aliases = {len(operands): 0}
        operands.append(out)
        in_specs.append(pl.BlockSpec(memory_space=pl.ANY))
        out_struct = jax.ShapeDtypeStruct(out.shape, out.dtype)
        out_dtype = out.dtype
    else:
        out_struct = jax.ShapeDtypeStruct(out_full if out_full is not None else (m, n), out_dtype)
    has_add, has_epi = add is not None, epi_p is not None
    n_skip = (1 if out is not None else 0) + len(after)
    operands += list(after)
    in_specs += [pl.BlockSpec(memory_space=pl.ANY)] * len(after)

    def body(*refs):
        a_ref, b_ref = refs[0], refs[1]
        pos = 2
        add_ref = epi_ref = None
        if has_add:
            add_ref = refs[pos]
            pos += 1
        if has_epi:
            epi_ref = refs[pos]
            pos += 1
        pos += n_skip
        o_ref = refs[pos]

        def finish(r):
            if has_add:
                r = r + add_ref[...].astype(F32)
            if has_epi:
                r = r * (2.0 * jnp.maximum(epi_ref[...].astype(F32), 0.0))
            o_ref[...] = r.astype(o_ref.dtype)

        av = a_ref[...]
        if a_pro == "relu2":
            av = jnp.square(jnp.maximum(av.astype(F32), 0.0))
        part = _dot(av, b_ref[...], dims)
        if nk == 1:
            finish(part)
        else:
            acc_ref = refs[pos + 1]
            kk = pl.program_id(2)

            @pl.when(kk == 0)
            def _():
                acc_ref[...] = part

            @pl.when(kk > 0)
            def _():
                acc_ref[...] += part

            @pl.when(kk == nk - 1)
            def _():
                finish(acc_ref[...])

    grid = (m // tm, n // tn, nk) if j_inner else (n // tn, m // tm, nk)
    return pl.pallas_call(
        body, name=name, grid=grid, in_specs=in_specs,
        out_specs=spec(out_at, tm, tn, gi, gj), out_shape=out_struct,
        scratch_shapes=[pltpu.VMEM((tm, tn), F32)] if nk > 1 else [], input_output_aliases=aliases,
        compiler_params=_cp(VMEM_BIG))(*operands)


def _rms_fwd(x, g, name, tm=1024):
    s, d = x.shape
    tm = min(tm, s)

    def body(x_ref, g_ref, o_ref):
        xv = x_ref[...]
        r = lax.rsqrt(jnp.mean(xv * xv, axis=-1, keepdims=True) + EPS)
        o_ref[...] = (xv * r * g_ref[...]).astype(o_ref.dtype)

    return pl.pallas_call(
        body, name=name, grid=(s // tm,),
        in_specs=[pl.BlockSpec((tm, d), lambda i: (i, 0)), pl.BlockSpec((1, d), lambda i: (0, 0))],
        out_specs=pl.BlockSpec((tm, d), lambda i: (i, 0)),
        out_shape=jax.ShapeDtypeStruct((s, d), _ACT), compiler_params=_cp(VMEM_BIG))(x, g)


def _rms_bwd(x, g, dy, dres, name, tm=512):
    s, d = x.shape
    tm = min(tm, s)
    has_res = dres is not None

    def body(*refs):
        if has_res:
            x_ref, g_ref, dy_ref, dres_ref, dx_ref, dxa_ref, dg_ref = refs
        else:
            x_ref, g_ref, dy_ref, dx_ref, dxa_ref, dg_ref = refs

        @pl.when(pl.program_id(0) == 0)
        def _():
            dg_ref[...] = jnp.zeros_like(dg_ref)

        xv = x_ref[...]
        dyv = dy_ref[...].astype(F32)
        r = lax.rsqrt(jnp.mean(xv * xv, axis=-1, keepdims=True) + EPS)
        xh = xv * r
        dyg = dyv * g_ref[...]
        dx = r * (dyg - xh * jnp.mean(dyg * xh, axis=-1, keepdims=True))
        if has_res:
            dx = dx + dres_ref[...]
        dx_ref[...] = dx
        dxa_ref[...] = dx.astype(dxa_ref.dtype)
        dg_ref[...] += jnp.sum(dyv * xh, axis=0, keepdims=True)

    row = pl.BlockSpec((tm, d), lambda i: (i, 0))
    vec = pl.BlockSpec((1, d), lambda i: (0, 0))
    in_specs = [row, vec, row] + ([row] if has_res else [])
    operands = [x, g, dy] + ([dres] if has_res else [])
    return pl.pallas_call(
        body, name=name, grid=(s // tm,), in_specs=in_specs, out_specs=[row, row, vec],
        out_shape=[jax.ShapeDtypeStruct((s, d), F32), jax.ShapeDtypeStruct((s, d), _ACT),
                   jax.ShapeDtypeStruct((1, d), F32)], compiler_params=_cp(VMEM_BIG))(*operands)


def _loss_head(h, g, target, name, tm=512):
    s, d = h.shape
    tm = min(tm, s)

    def body(h_ref, g_ref, t_ref, loss_ref, dh_ref, dha_ref, dg_ref):
        @pl.when(pl.program_id(0) == 0)
        def _():
            dg_ref[...] = jnp.zeros_like(dg_ref)
            loss_ref[...] = jnp.zeros_like(loss_ref)

        xv = h_ref[...]
        r = lax.rsqrt(jnp.mean(xv * xv, axis=-1, keepdims=True) + EPS)
        xh = xv * r
        err = xh * g_ref[...] - t_ref[...]
        loss_ref[...] += jnp.full(loss_ref.shape, 0.5 * jnp.sum(jnp.mean(err * err, axis=-1, keepdims=True)), F32)
        dyv = err * (1.0 / d)
        dyg = dyv * g_ref[...]
        dh = r * (dyg - xh * jnp.mean(dyg * xh, axis=-1, keepdims=True))
        dh_ref[...] = dh
        dha_ref[...] = dh.astype(dha_ref.dtype)
        dg_ref[...] += jnp.sum(dyv * xh, axis=0, keepdims=True)

    row = pl.BlockSpec((tm, d), lambda i: (i, 0))
    vec = pl.BlockSpec((1, d), lambda i: (0, 0))
    return pl.pallas_call(
        body, name=name, grid=(s // tm,), in_specs=[row, vec, row],
        out_specs=[pl.BlockSpec((1, 128), lambda i: (0, 0)), row, row, vec],
        out_shape=[jax.ShapeDtypeStruct((1, 128), F32), jax.ShapeDtypeStruct((s, d), F32),
                   jax.ShapeDtypeStruct((s, d), _ACT), jax.ShapeDtypeStruct((1, d), F32)],
        compiler_params=_cp(VMEM_BIG))(h, g, target)


def _gmlp_parts(u, v, lng, lnb):
    mu = jnp.mean(v, axis=-1, keepdims=True)
    vc = v - mu
    rstd = lax.rsqrt(jnp.mean(vc * vc, axis=-1, keepdims=True) + EPS)
    xhat = vc * rstd
    vn = xhat * lng + lnb
    return u, xhat, rstd, vn


def _gmlp_fwd(proj, lng, lnb, ws, bs3, name):
    s = proj.shape[0]

    def body(pu_ref, pv_ref, lng_ref, lnb_ref, ws_ref, bs_ref, o_ref):
        u, _, _, vn = _gmlp_parts(_gelu(pu_ref[...]), _gelu(pv_ref[...]), lng_ref[...], lnb_ref[...])
        causal = _iota((CHUNK, CHUNK), 0) >= _iota((CHUNK, CHUNK), 1)
        for g in range(A_GROUPS):
            sl = slice(g * A_GW, (g + 1) * A_GW)
            w = jnp.where(causal, ws_ref[g], 0.0)
            sv = _dot(w, vn[:, sl]) + bs_ref[g]
            o_ref[:, sl] = (u[:, sl] * sv).astype(o_ref.dtype)

    full = lambda shape: pl.BlockSpec(shape, lambda c: (0,) * len(shape))
    return pl.pallas_call(
        body, name=name, grid=(s // CHUNK,),
        in_specs=[pl.BlockSpec((CHUNK, D_INNER), lambda c: (c, 0)), pl.BlockSpec((CHUNK, D_INNER), lambda c: (c, 1)),
                  full((1, D_INNER)), full((1, D_INNER)), full((A_GROUPS, CHUNK, CHUNK)), full((A_GROUPS, CHUNK, 1))],
        out_specs=pl.BlockSpec((CHUNK, D_INNER), lambda c: (c, 0)),
        out_shape=jax.ShapeDtypeStruct((s, D_INNER + X_WIDTH), _ACT), compiler_params=_cp(VMEM_BIG))(proj, proj, lng, lnb, ws, bs3)


def _gmlp_bwd(proj, dcat, lng, lnb, ws, bs3, name):
    s = proj.shape[0]

    def body(pu_ref, pv_ref, dm_ref, lng_ref, lnb_ref, ws_ref, bs_ref, dp_ref, dws_ref, dbs_ref, dlng_ref, dlnb_ref, dvn_ref):
        @pl.when(pl.program_id(0) == 0)
        def _():
            dws_ref[...] = jnp.zeros_like(dws_ref)
            dbs_ref[...] = jnp.zeros_like(dbs_ref)
            dlng_ref[...] = jnp.zeros_like(dlng_ref)
            dlnb_ref[...] = jnp.zeros_like(dlnb_ref)

        lng = lng_ref[...]
        u, u_grad = _gelu_with_grad(pu_ref[...])
        v, v_grad = _gelu_with_grad(pv_ref[...])
        u, xhat, rstd, vn = _gmlp_parts(u, v, lng, lnb_ref[...])
        dm = dm_ref[...].astype(F32)
        causal = _iota((CHUNK, CHUNK), 0) >= _iota((CHUNK, CHUNK), 1)
        for g in range(A_GROUPS):
            sl = slice(g * A_GW, (g + 1) * A_GW)
            w = jnp.where(causal, ws_ref[g], 0.0)
            sv = _dot(w, vn[:, sl]) + bs_ref[g]
            dsv = dm[:, sl] * u[:, sl]
            dp_ref[:, sl] = (dm[:, sl] * sv * u_grad[:, sl]).astype(dp_ref.dtype)
            dvn_ref[:, sl] = _dot_tn(w, dsv)
            dws_ref[g] += jnp.where(causal, _dot_nt(dsv, vn[:, sl]), 0.0)
            dbs_ref[g] += jnp.sum(dsv, axis=-1, keepdims=True)
        dvn = dvn_ref[...]
        dlng_ref[...] += jnp.sum(dvn * xhat, axis=0, keepdims=True)
        dlnb_ref[...] += jnp.sum(dvn, axis=0, keepdims=True)
        dxh = dvn * lng
        dv = rstd * (dxh - jnp.mean(dxh, axis=-1, keepdims=True) - xhat * jnp.mean(dxh * xhat, axis=-1, keepdims=True))
        dp_ref[:, D_INNER:] = (dv * v_grad).astype(dp_ref.dtype)

    full = lambda shape: pl.BlockSpec(shape, lambda c: (0,) * len(shape))
    return pl.pallas_call(
        body, name=name, grid=(s // CHUNK,),
        in_specs=[pl.BlockSpec((CHUNK, D_INNER), lambda c: (c, 0)), pl.BlockSpec((CHUNK, D_INNER), lambda c: (c, 1)),
                  pl.BlockSpec((CHUNK, D_INNER), lambda c: (c, 0)),
                  full((1, D_INNER)), full((1, D_INNER)), full((A_GROUPS, CHUNK, CHUNK)), full((A_GROUPS, CHUNK, 1))],
        out_specs=[pl.BlockSpec((CHUNK, 2 * D_INNER), lambda c: (c, 0)), full((A_GROUPS, CHUNK, CHUNK)),
                   full((A_GROUPS, CHUNK, 1)), full((1, D_INNER)), full((1, D_INNER))],
        out_shape=[jax.ShapeDtypeStruct((s, 2 * D_INNER + X_WIDTH), _ACT), jax.ShapeDtypeStruct((A_GROUPS, CHUNK, CHUNK), F32),
                   jax.ShapeDtypeStruct((A_GROUPS, CHUNK, 1), F32), jax.ShapeDtypeStruct((1, D_INNER), F32),
                   jax.ShapeDtypeStruct((1, D_INNER), F32)],
        scratch_shapes=[pltpu.VMEM((CHUNK, D_INNER), F32)],
        compiler_params=_cp(VMEM_BIG))(proj, proj, dcat, lng, lnb, ws, bs3)


_X_SCALE = 1.0 / math.sqrt(X_HD)


def _attn_fwd(proj, qblk, kv, cat, name, tm=512):
    s = proj.shape[0]
    tm = min(tm, s)

    def body(q_ref, kv_ref, cat_ref, o_ref):
        for h in range(X_HEADS):
            sl = slice(h * X_HD, (h + 1) * X_HD)
            k = kv_ref[:, sl]
            v = kv_ref[:, X_WIDTH + h * X_HD:X_WIDTH + (h + 1) * X_HD]
            sc = _dot_nt(q_ref[:, sl], k) * _X_SCALE
            e = jnp.exp(sc - jnp.max(sc, axis=-1, keepdims=True))
            p = e / jnp.sum(e, axis=-1, keepdims=True)
            o_ref[:, sl] = _dot(p, v).astype(o_ref.dtype)

    return pl.pallas_call(
        body, name=name, grid=(s // tm,),
        in_specs=[pl.BlockSpec((tm, X_WIDTH), lambda i: (i, qblk)), pl.BlockSpec((N_MEM, 2 * X_WIDTH), lambda i: (0, 0)),
                  pl.BlockSpec(memory_space=pl.ANY)],
        out_specs=pl.BlockSpec((tm, X_WIDTH), lambda i: (i, D_INNER // X_WIDTH)),
        out_shape=jax.ShapeDtypeStruct(cat.shape, cat.dtype), input_output_aliases={2: 0},
        compiler_params=_cp(VMEM_BIG))(proj, kv, cat)


def _attn_bwd(proj, qblk, kv, dcat, dproj, name, tm=512):
    s = proj.shape[0]
    tm = min(tm, s)

    def body(q_ref, kv_ref, do_ref, dproj_ref, dq_ref, dkv_ref):
        @pl.when(pl.program_id(0) == 0)
        def _():
            dkv_ref[...] = jnp.zeros_like(dkv_ref)

        for h in range(X_HEADS):
            sl = slice(h * X_HD, (h + 1) * X_HD)
            slv = slice(X_WIDTH + h * X_HD, X_WIDTH + (h + 1) * X_HD)
            q = q_ref[:, sl]
            k = kv_ref[:, sl]
            v = kv_ref[:, slv]
            do = do_ref[:, sl].astype(F32)
            sc = _dot_nt(q, k) * _X_SCALE
            e = jnp.exp(sc - jnp.max(sc, axis=-1, keepdims=True))
            p = e / jnp.sum(e, axis=-1, keepdims=True)
            dp = _dot_nt(do, v)
            ds = p * (dp - jnp.sum(dp * p, axis=-1, keepdims=True)) * _X_SCALE
            dq_ref[:, sl] = _dot(ds, k).astype(dq_ref.dtype)
            dkv_ref[:, sl] += _dot_tn(ds, q)
            dkv_ref[:, slv] += _dot_tn(p, do)

    return pl.pallas_call(
        body, name=name, grid=(s // tm,),
        in_specs=[pl.BlockSpec((tm, X_WIDTH), lambda i: (i, qblk)), pl.BlockSpec((N_MEM, 2 * X_WIDTH), lambda i: (0, 0)),
                  pl.BlockSpec((tm, X_WIDTH), lambda i: (i, 2)), pl.BlockSpec(memory_space=pl.ANY)],
        out_specs=[pl.BlockSpec((tm, X_WIDTH), lambda i: (i, qblk)), pl.BlockSpec((N_MEM, 2 * X_WIDTH), lambda i: (0, 0))],
        out_shape=[jax.ShapeDtypeStruct(dproj.shape, dproj.dtype), jax.ShapeDtypeStruct((N_MEM, 2 * X_WIDTH), F32)],
        input_output_aliases={3: 0}, compiler_params=_cp(VMEM_BIG))(proj, kv, dcat, dproj)


CONV_TC = 256
_XBC_BLK0 = D_INNER // CONV_TC


CONV_RB = 64
SUBLANES = 8


def _rows_before(cur, prev_last, j):
    rolled = pltpu.roll(cur, j, 0)
    head = jnp.where(_iota((SUBLANES, cur.shape[1]), 0) < j, pltpu.roll(prev_last, j, 0), rolled[:SUBLANES])
    return jnp.concatenate([head, rolled[SUBLANES:]], axis=0)


def _rows_after(cur, next_first, j):
    n = cur.shape[0]
    rolled = pltpu.roll(cur, n - j, 0)
    tail = jnp.where(_iota((SUBLANES, cur.shape[1]), 0) >= SUBLANES - j, pltpu.roll(next_first, SUBLANES - j, 0),
                     rolled[n - SUBLANES:])
    return jnp.concatenate([rolled[:n - SUBLANES], tail], axis=0)


def _conv_pre(x_ref, w_ref, b_ref, r0, prev_last):
    cur = x_ref[pl.ds(r0, CONV_RB), :]
    shifts = [_rows_before(cur, prev_last, j) for j in range(1, CONV_K)]
    pre = b_ref[...] + w_ref[CONV_K - 1:CONV_K, :] * cur
    for j in range(1, CONV_K):
        pre = pre + w_ref[CONV_K - 1 - j:CONV_K - j, :] * shifts[j - 1]
    return pre, cur, shifts


def _conv_fwd(proj, w, b, name):
    s = proj.shape[0]

    def body(x_ref, w_ref, b_ref, o_ref):
        xv = x_ref[...]
        rows = _iota(xv.shape, 0)
        pre = b_ref[...] + w_ref[CONV_K - 1:CONV_K, :] * xv
        for j in range(1, CONV_K):
            pre = pre + w_ref[CONV_K - 1 - j:CONV_K - j, :] * jnp.where(rows >= j, pltpu.roll(xv, j, 0), 0.0)
        o_ref[...] = pre * _sigmoid(pre)

    return pl.pallas_call(
        body, name=name, grid=(CONV_DIM // CONV_TC,),
        in_specs=[pl.BlockSpec((s, CONV_TC), lambda j: (0, _XBC_BLK0 + j)), pl.BlockSpec((CONV_K, CONV_TC), lambda j: (0, j)),
                  pl.BlockSpec((1, CONV_TC), lambda j: (0, j))],
        out_specs=pl.BlockSpec((s, CONV_TC), lambda j: (0, j)),
        out_shape=jax.ShapeDtypeStruct((s, CONV_DIM), F32), compiler_params=_cp(VMEM_BIG))(proj, w, b)


def _conv_bwd(proj, w, b, dxbc, dproj, name):
    s = proj.shape[0]

    nb = s // CONV_RB

    def body(x_ref, w_ref, b_ref, d_ref, dproj_ref, dx_ref, dw_ref, db_ref, dpre_ref):
        def fold(v):
            out = v[:SUBLANES]
            for t in range(1, CONV_RB // SUBLANES):
                out = out + v[t * SUBLANES:(t + 1) * SUBLANES]
            return out

        def first(i, carry):
            prev_last, acc = carry
            r0 = pl.multiple_of(i * CONV_RB, CONV_RB)
            pre, cur, shifts = _conv_pre(x_ref, w_ref, b_ref, r0, prev_last)
            sig = _sigmoid(pre)
            dpre = d_ref[pl.ds(r0, CONV_RB), :] * (sig * (1.0 + pre * (1.0 - sig)))
            dpre_ref[pl.ds(r0, CONV_RB), :] = dpre
            taps = [cur] + shifts
            acc = tuple(a + fold(dpre * t) for a, t in zip(acc[:CONV_K], taps)) + (acc[CONV_K] + fold(dpre),)
            return cur[CONV_RB - SUBLANES:], acc

        zero8 = jnp.zeros((SUBLANES, CONV_TC), F32)
        _, acc = lax.fori_loop(0, nb, first, (zero8, (zero8,) * (CONV_K + 1)))
        for j in range(CONV_K):
            dw_ref[CONV_K - 1 - j:CONV_K - j, :] = jnp.sum(acc[j], axis=0, keepdims=True)
        db_ref[...] = jnp.sum(acc[CONV_K], axis=0, keepdims=True)

        def second(i, next_first):
            r0 = pl.multiple_of((nb - 1 - i) * CONV_RB, CONV_RB)
            cur = dpre_ref[pl.ds(r0, CONV_RB), :]
            dx = w_ref[CONV_K - 1:CONV_K, :] * cur
            for j in range(1, CONV_K):
                dx = dx + w_ref[CONV_K - 1 - j:CONV_K - j, :] * _rows_after(cur, next_first, j)
            dx_ref[pl.ds(r0, CONV_RB), :] = dx.astype(dx_ref.dtype)
            return cur[:SUBLANES]

        lax.fori_loop(0, nb, second, zero8)

    return pl.pallas_call(
        body, name=name, grid=(CONV_DIM // CONV_TC,),
        in_specs=[pl.BlockSpec((s, CONV_TC), lambda j: (0, _XBC_BLK0 + j)), pl.BlockSpec((CONV_K, CONV_TC), lambda j: (0, j)),
                  pl.BlockSpec((1, CONV_TC), lambda j: (0, j)), pl.BlockSpec((s, CONV_TC), lambda j: (0, j)),
                  pl.BlockSpec(memory_space=pl.ANY)],
        out_specs=[pl.BlockSpec((s, CONV_TC), lambda j: (0, _XBC_BLK0 + j)), pl.BlockSpec((CONV_K, CONV_TC), lambda j: (0, j)),
                   pl.BlockSpec((1, CONV_TC), lambda j: (0, j))],
        out_shape=[jax.ShapeDtypeStruct(dproj.shape, dproj.dtype), jax.ShapeDtypeStruct((CONV_K, CONV_DIM), F32),
                   jax.ShapeDtypeStruct((1, CONV_DIM), F32)], input_output_aliases={4: 0},
        scratch_shapes=[pltpu.VMEM((s, CONV_TC), F32)],
        compiler_params=_cp(VMEM_BIG))(proj, w, b, dxbc, dproj)


def _ssd_common(dtc_ref, br_ref, ar_ref, csb_ref, cst_ref, csf_ref, dtf_ref, expand):
    a_row = -jnp.exp(ar_ref[...])
    dt_c = _softplus(dtc_ref[...] + br_ref[...])
    tril = _iota((CHUNK, CHUNK), 0) >= _iota((CHUNK, CHUNK), 1)
    cs = _sel_dot(tril, dt_c * a_row)
    cst_ref[...] = cs.T
    e64 = (jnp.right_shift(_iota((HPAD, D_INNER), 1), 6) == _iota((HPAD, D_INNER), 0)).astype(jnp.bfloat16)
    if expand:
        e128 = jnp.right_shift(_iota((HPAD, SSM_HEADS * CHUNK), 1), 7) == _iota((HPAD, SSM_HEADS * CHUNK), 0)
        csb_ref[...] = _dot_sel(cs, e128)
        dtf_ref[...] = _dot_sel(dt_c, e64)
        csf_ref[...] = _dot_sel(cs, e64)
    dt_full = dtf_ref[...]
    cs_full = csf_ref[...]
    cs_last = csf_ref[CHUNK - 1:CHUNK, :]
    e_full = jnp.exp(cs_full)
    f_full = jnp.exp(cs_last - cs_full)
    gamma = jnp.exp(cs_last)
    return a_row, dt_c, cs, dt_full, e_full, f_full, gamma, e64


def _ssd_lambda(csb_ref, cst_ref, h, causal):
    diff = csb_ref[:, h * CHUNK:(h + 1) * CHUNK] - cst_ref[h:h + 1, :]
    return jnp.exp(jnp.where(causal, diff, -1e30))


_SSD_VEC_SPECS = lambda: [pl.BlockSpec((1, HPAD), lambda c: (0, 0)), pl.BlockSpec((1, HPAD), lambda c: (0, 0)),
                          pl.BlockSpec((1, D_INNER), lambda c: (0, 0))]


def _ssd_fwd(xbc, dtc, bias_row, alog_row, dfull, name):
    s = xbc.shape[0]
    nc = s // CHUNK

    def body(xbc_ref, dtc_ref, br_ref, ar_ref, df_ref, y_ref, st_ref, csb_ref, csf_ref, dtf_ref, ht_ref, cst_ref):
        @pl.when(pl.program_id(0) == 0)
        def _():
            ht_ref[...] = jnp.zeros_like(ht_ref)

        _, _, _, dt_full, e_full, f_full, gamma, _ = _ssd_common(
            dtc_ref, br_ref, ar_ref, csb_ref, cst_ref, csf_ref, dtf_ref, expand=True)
        x = xbc_ref[:, :D_INNER]
        xdt = x * dt_full
        st_ref[...] = ht_ref[...]
        causal = _iota((CHUNK, CHUNK), 0) >= _iota((CHUNK, CHUNK), 1)
        lo = _iota((CHUNK, CHUNK), 1) < SSM_P
        for g in range(SSM_GROUPS):
            gs = slice(g * SSM_GW, (g + 1) * SSM_GW)
            bg = xbc_ref[:, D_INNER + g * SSM_N:D_INNER + (g + 1) * SSM_N]
            cg = xbc_ref[:, D_INNER + SSM_GROUPS * SSM_N + g * SSM_N:D_INNER + SSM_GROUPS * SSM_N + (g + 1) * SSM_N]
            ht = ht_ref[:, gs]
            cb = _dot_nt(cg, bg)
            yoff = e_full[:, gs] * _dot(cg, ht)
            for jp in range(SSM_GW // CHUNK):
                j = g * (SSM_GW // CHUNK) + jp
                ps = slice(j * CHUNK, (j + 1) * CHUNK)
                x2 = xdt[:, ps]
                y0 = _dot(cb * _ssd_lambda(csb_ref, cst_ref, 2 * j, causal), x2)
                y1 = _dot(cb * _ssd_lambda(csb_ref, cst_ref, 2 * j + 1, causal), x2)
                y_ref[:, ps] = (jnp.where(lo, y0, y1) + yoff[:, jp * CHUNK:(jp + 1) * CHUNK]
                                + x[:, ps] * df_ref[:, ps])
            ht_ref[:, gs] = gamma[:, gs] * ht + _dot_tn(bg, xdt[:, gs] * f_full[:, gs])

    return pl.pallas_call(
        body, name=name, grid=(nc,),
        in_specs=[pl.BlockSpec((CHUNK, CONV_DIM), lambda c: (c, 0)), pl.BlockSpec((CHUNK, HPAD), lambda c: (c, 0))]
                 + _SSD_VEC_SPECS(),
        out_specs=[pl.BlockSpec((CHUNK, D_INNER), lambda c: (c, 0)), pl.BlockSpec((None, SSM_N, D_INNER), lambda c: (c, 0, 0)),
                   pl.BlockSpec((CHUNK, SSM_HEADS * CHUNK), lambda c: (c, 0)), pl.BlockSpec((CHUNK, D_INNER), lambda c: (c, 0)),
                   pl.BlockSpec((CHUNK, D_INNER), lambda c: (c, 0))],
        out_shape=[jax.ShapeDtypeStruct((s, D_INNER), F32), jax.ShapeDtypeStruct((nc, SSM_N, D_INNER), F32),
                   jax.ShapeDtypeStruct((s, SSM_HEADS * CHUNK), F32), jax.ShapeDtypeStruct((s, D_INNER), F32),
                   jax.ShapeDtypeStruct((s, D_INNER), F32)],
        scratch_shapes=[pltpu.VMEM((SSM_N, D_INNER), F32), pltpu.VMEM((HPAD, CHUNK), F32)],
        compiler_params=_cp(VMEM_BIG))(xbc, dtc, bias_row, alog_row, dfull)


def _ssd_bwd(xbc, dtc, bias_row, alog_row, dfull, dy, states, expansions, name):
    s = xbc.shape[0]
    nc = s // CHUNK
    rev = lambda c: nc - 1 - c

    def body(xbc_ref, dtc_ref, br_ref, ar_ref, df_ref, dy_ref, st_ref, csb_ref, csf_ref, dtf_ref,
             dxbc_ref, ddt_ref, dalog_ref, dd_ref, dbias_ref,
             dht_ref, cst_ref, ddf_ref, dxs_ref, dcsf_ref, dcsl_ref):
        step = pl.program_id(0)

        @pl.when(step == 0)
        def _():
            dht_ref[...] = jnp.zeros_like(dht_ref)
            ddf_ref[...] = jnp.zeros_like(ddf_ref)
            dalog_ref[...] = jnp.zeros_like(dalog_ref)
            dbias_ref[...] = jnp.zeros_like(dbias_ref)
            dd_ref[...] = jnp.zeros_like(dd_ref)

        a_row, dt_c, _, dt_full, e_full, f_full, gamma, e64 = _ssd_common(
            dtc_ref, br_ref, ar_ref, csb_ref, cst_ref, csf_ref, dtf_ref, expand=False)
        x = xbc_ref[:, :D_INNER]
        xdt = x * dt_full
        dy_all = dy_ref[...]
        ddf_ref[...] += jnp.broadcast_to(jnp.sum(dy_all * x, axis=0, keepdims=True), ddf_ref.shape)
        causal = _iota((CHUNK, CHUNK), 0) >= _iota((CHUNK, CHUNK), 1)
        lo = _iota((CHUNK, CHUNK), 1) < SSM_P
        head_lane = _iota((CHUNK, HPAD), 1)
        head_row = _iota((HPAD, CHUNK), 0)
        dcs_heads = jnp.zeros((CHUNK, HPAD), F32)
        dcs_cols = jnp.zeros((HPAD, CHUNK), F32)
        for g in range(SSM_GROUPS):
            gs = slice(g * SSM_GW, (g + 1) * SSM_GW)
            b0 = D_INNER + g * SSM_N
            c0 = D_INNER + SSM_GROUPS * SSM_N + g * SSM_N
            bg = xbc_ref[:, b0:b0 + SSM_N]
            cg = xbc_ref[:, c0:c0 + SSM_N]
            ht = st_ref[:, gs]
            dht = dht_ref[:, gs]
            dyg = dy_all[:, gs]
            eg, fg, gg = e_full[:, gs], f_full[:, gs], gamma[:, gs]
            z = _dot(cg, ht)
            dz = dyg * eg
            dcg = _dot_nt(dz, ht)
            dht_new = _dot_tn(cg, dz) + gg * dht
            xf = xdt[:, gs] * fg
            dxf = _dot(bg, dht)
            dbg = _dot_nt(xf, dht)
            dff = dxf * xf
            dcsf_ref[:, gs] = dyg * eg * z - dff
            dcsl_ref[:, gs] = jnp.broadcast_to(
                jnp.sum(dff, axis=0, keepdims=True) + jnp.sum(dht * ht, axis=0, keepdims=True) * gg, (8, SSM_GW))
            cb = _dot_nt(cg, bg)
            dcb = jnp.zeros((CHUNK, CHUNK), F32)
            for jp in range(SSM_GW // CHUNK):
                j = g * (SSM_GW // CHUNK) + jp
                ps = slice(j * CHUNK, (j + 1) * CHUNK)
                x2 = xdt[:, ps]
                dy2 = dy_all[:, ps]
                dxh = []
                for hh in range(2):
                    h = 2 * j + hh
                    lam = _ssd_lambda(csb_ref, cst_ref, h, causal)
                    mh = cb * lam
                    dyh = jnp.where(lo, dy2, 0.0) if hh == 0 else jnp.where(lo, 0.0, dy2)
                    dm = _dot_nt(dyh, x2)
                    dcb = dcb + dm * lam
                    gm = dm * mh
                    dcs_heads = dcs_heads + jnp.where(head_lane == h, jnp.sum(gm, axis=1, keepdims=True), 0.0)
                    dcs_cols = dcs_cols + jnp.where(head_row == h, jnp.sum(gm, axis=0, keepdims=True), 0.0)
                    dxh.append(_dot_tn(mh, dy2))
                dxs_ref[:, ps] = jnp.where(lo, dxh[0], dxh[1]) + dxf[:, jp * CHUNK:(jp + 1) * CHUNK] * fg[:, jp * CHUNK:(jp + 1) * CHUNK]
            dxbc_ref[:, b0:b0 + SSM_N] = (dbg + _dot_tn(dcb, cg)).astype(dxbc_ref.dtype)
            dxbc_ref[:, c0:c0 + SSM_N] = (dcg + _dot(dcb, bg)).astype(dxbc_ref.dtype)
            dht_ref[:, gs] = dht_new
        dxs = dxs_ref[...]
        dcs_heads = dcs_heads - dcs_cols.T + _dot_sel(dcsf_ref[...], e64, ((1,), (1,)))
        dcs_last = _dot_sel(dcsl_ref[...], e64, ((1,), (1,)))
        dcs_heads = dcs_heads + jnp.where(_iota((CHUNK, HPAD), 0) == CHUNK - 1, dcs_last[0:1, :], 0.0)
        triu = _iota((CHUNK, CHUNK), 0) <= _iota((CHUNK, CHUNK), 1)
        dda = _sel_dot(triu, dcs_heads)
        ddt = dda * a_row + _dot_sel(dxs * x, e64, ((1,), (1,)))
        dxbc_ref[:, :D_INNER] = (dxs * dt_full + dy_all * df_ref[...]).astype(dxbc_ref.dtype)
        dalog_ref[...] += jnp.sum(dda * dt_c, axis=0, keepdims=True) * a_row
        ddt_raw = ddt * _sigmoid(dtc_ref[...] + br_ref[...])
        ddt_ref[...] = ddt_raw.astype(ddt_ref.dtype)
        dbias_ref[...] += jnp.sum(ddt_raw, axis=0, keepdims=True)

        @pl.when(step == nc - 1)
        def _():
            dd_ref[...] = _dot_sel(ddf_ref[...], e64, ((1,), (1,)))[0:1, :]

    vec = pl.BlockSpec((1, HPAD), lambda c: (0, 0))
    return pl.pallas_call(
        body, name=name, grid=(nc,),
        in_specs=[pl.BlockSpec((CHUNK, CONV_DIM), lambda c: (rev(c), 0)), pl.BlockSpec((CHUNK, HPAD), lambda c: (rev(c), 0))]
                 + _SSD_VEC_SPECS()
                 + [pl.BlockSpec((CHUNK, D_INNER), lambda c: (rev(c), 0)),
                    pl.BlockSpec((None, SSM_N, D_INNER), lambda c: (rev(c), 0, 0)),
                    pl.BlockSpec((CHUNK, SSM_HEADS * CHUNK), lambda c: (rev(c), 0)),
                    pl.BlockSpec((CHUNK, D_INNER), lambda c: (rev(c), 0)), pl.BlockSpec((CHUNK, D_INNER), lambda c: (rev(c), 0))],
        out_specs=[pl.BlockSpec((CHUNK, CONV_DIM), lambda c: (rev(c), 0)), pl.BlockSpec((CHUNK, HPAD), lambda c: (rev(c), 0)),
                   vec, vec, vec],
        out_shape=[jax.ShapeDtypeStruct((s, CONV_DIM), F32), jax.ShapeDtypeStruct((s, HPAD), _ACT),
                   jax.ShapeDtypeStruct((1, HPAD), F32), jax.ShapeDtypeStruct((1, HPAD), F32),
                   jax.ShapeDtypeStruct((1, HPAD), F32)],
        scratch_shapes=[pltpu.VMEM((SSM_N, D_INNER), F32), pltpu.VMEM((HPAD, CHUNK), F32),
                        pltpu.VMEM((8, D_INNER), F32), pltpu.VMEM((CHUNK, D_INNER), F32),
                        pltpu.VMEM((CHUNK, D_INNER), F32), pltpu.VMEM((8, D_INNER), F32)],
        compiler_params=_cp(VMEM_BIG))(xbc, dtc, bias_row, alog_row, dfull, dy, states, *expansions)


def _gate_fwd(y, proj, gn, name, tm=512):
    s = y.shape[0]
    tm = min(tm, s)

    def body(y_ref, z_ref, gn_ref, o_ref):
        for g in range(SSM_GROUPS):
            gs = slice(g * SSM_GW, (g + 1) * SSM_GW)
            z = z_ref[:, gs]
            t = y_ref[:, gs] * (z * _sigmoid(z))
            r = lax.rsqrt(jnp.mean(t * t, axis=-1, keepdims=True) + EPS)
            o_ref[:, gs] = (t * r * gn_ref[:, gs]).astype(o_ref.dtype)

    row = pl.BlockSpec((tm, D_INNER), lambda i: (i, 0))
    return pl.pallas_call(
        body, name=name, grid=(s // tm,), in_specs=[row, row, pl.BlockSpec((1, D_INNER), lambda i: (0, 0))],
        out_specs=row, out_shape=jax.ShapeDtypeStruct((s, D_INNER + X_WIDTH), _ACT),
        compiler_params=_cp(VMEM_BIG))(y, proj, gn)


def _gate_bwd(y, proj, gn, dcat, name, tm=512):
    s = y.shape[0]
    tm = min(tm, s)

    def body(y_ref, z_ref, gn_ref, dm_ref, dy_ref, dz_ref, dgn_ref):
        @pl.when(pl.program_id(0) == 0)
        def _():
            dgn_ref[...] = jnp.zeros_like(dgn_ref)

        for g in range(SSM_GROUPS):
            gs = slice(g * SSM_GW, (g + 1) * SSM_GW)
            z = z_ref[:, gs]
            yv = y_ref[:, gs]
            sig = _sigmoid(z)
            sz = z * sig
            t = yv * sz
            r = lax.rsqrt(jnp.mean(t * t, axis=-1, keepdims=True) + EPS)
            th = t * r
            dm = dm_ref[:, gs].astype(F32)
            dmg = dm * gn_ref[:, gs]
            dt_ = r * (dmg - th * jnp.mean(dmg * th, axis=-1, keepdims=True))
            dgn_ref[:, gs] += jnp.sum(dm * th, axis=0, keepdims=True)
            dy_ref[:, gs] = dt_ * sz
            dz_ref[:, gs] = (dt_ * yv * (sig * (1.0 + z * (1.0 - sig)))).astype(dz_ref.dtype)

    row = pl.BlockSpec((tm, D_INNER), lambda i: (i, 0))
    vec = pl.BlockSpec((1, D_INNER), lambda i: (0, 0))
    return pl.pallas_call(
        body, name=name, grid=(s // tm,), in_specs=[row, row, vec, row], out_specs=[row, row, vec],
        out_shape=[jax.ShapeDtypeStruct((s, D_INNER), F32), jax.ShapeDtypeStruct((s, 6 * D_MODEL), _ACT),
                   jax.ShapeDtypeStruct((1, D_INNER), F32)], compiler_params=_cp(VMEM_BIG))(y, proj, gn, dcat)


def _block_of(kind, width):
    if kind == "col":
        return lambda ref, j: ref.at[:, :, pl.ds(pl.multiple_of(j * width, 128), width)]
    if kind == "row":
        return lambda ref, j: ref.at[:, pl.ds(pl.multiple_of(j * width, 8), width), :]
    return lambda ref, j: ref.at[j]


def _coords():
    return lax.axis_index("x"), lax.axis_index("y"), lax.axis_index("c")


def _rel_chip(x, y, k):
    return (1 - x if k & 1 else x), (1 - y if k & 2 else y)


def _all_gather_body(ins, outs, send_sems, recv_sems, local_sems, blocks):
    n = len(ins)
    x, y, c = _coords()
    sibling = (x, y, 1 - c)
    via = (x + (1 - c) * (1 - 2 * x), y + c * (1 - 2 * y))
    onto = (x + c * (1 - 2 * x), y + (1 - c) * (1 - 2 * y))

    def copy(t, k, chip, core, to, src=None):
        dst = blocks[t](outs[t], 4 * chip[0] + 2 * chip[1] + core)
        return pltpu.make_async_remote_copy(
            src_ref=dst if src is None else src, dst_ref=dst, send_sem=send_sems.at[t, k],
            recv_sem=recv_sems.at[t, k], device_id=to, device_id_type=MESH)

    started = []
    for t in range(n):
        mine = pltpu.make_async_copy(ins[t], blocks[t](outs[t], 4 * x + 2 * y + c), local_sems.at[t])
        mine.start()
        started.append(mine)
    sends = []
    for t in range(n):
        for k in range(3):
            px, py = _rel_chip(x, y, k)
            cp = copy(t, k, (x, y), c, (px, py, 1 - c if k == 0 else c), src=ins[t])
            cp.start()
            sends.append(cp)
    for t in range(n):
        for k in (1, 2):
            chip = _rel_chip(x, y, k)
            copy(t, k, chip, c, sibling).wait_recv()
            fwd = copy(t, 3 + k, chip, c, sibling)
            fwd.start()
            sends.append(fwd)
        hop = copy(t, 3, via, c, (*onto, c))
        hop.start()
        sends.append(hop)
    for t in range(n):
        diagonal = _rel_chip(x, y, 3)
        copy(t, 3, diagonal, c, sibling).wait_recv()
        fwd = copy(t, 6, diagonal, c, sibling)
        fwd.start()
        sends.append(fwd)
    for t in range(n):
        copy(t, 0, (x, y), 1 - c, sibling).wait_recv()
        for k in range(1, 4):
            copy(t, 3 + k, _rel_chip(x, y, k), 1 - c, sibling).wait_recv()
    for cp in sends:
        cp.wait_send()
    for mine in started:
        mine.wait()


def _handshake(peers):
    barrier = pltpu.get_barrier_semaphore()
    for peer in peers:
        pl.semaphore_signal(barrier, inc=1, device_id=peer, device_id_type=MESH)
    pl.semaphore_wait(barrier, len(peers))


def _gather_peers():
    x, y, c = _coords()
    return [(x, y, 1 - c)] + [(*_rel_chip(x, y, k), c) for k in (1, 2)]


SEQ_ID_GATHER, SEQ_ID_SIBLING, SEQ_ID_CHIPS = 1, 2, 3


def _sequencer_call(body, peers, operands, out_types, sems, name, collective_id, after=()):
    n_in, n_out, n_after = len(operands), len(out_types), len(after)

    def launch(*refs):
        _handshake(peers())
        body(refs[:n_in], refs[n_in + n_after:n_in + n_after + n_out], *refs[n_in + n_after + n_out:])

    return pl.kernel(
        launch, name=name, out_type=out_types, mesh=plsc.ScalarSubcoreMesh(axis_name="seq", num_cores=1),
        scratch_types=sems, compiler_params=pltpu.CompilerParams(collective_id=collective_id))(*operands, *after)


def _all_gather_seq(shards, layouts, name, after=()):
    n = len(shards)
    blocks = [_block_of(kind, width) for kind, width, _ in layouts]
    return _sequencer_call(
        lambda ins, outs, *sems: _all_gather_body(ins, outs, *sems, blocks), _gather_peers, shards,
        [jax.ShapeDtypeStruct(shape, sh.dtype) for sh, (_, _, shape) in zip(shards, layouts)],
        [pltpu.SemaphoreType.DMA((n, 7)), pltpu.SemaphoreType.DMA((n, 7)), pltpu.SemaphoreType.DMA((n,))],
        name, SEQ_ID_GATHER, after)


def _tie(small, after):
    return lax.optimization_barrier((small, *after))[0]


def _rs_to_sibling(grads, layouts, name, after=()):
    n = len(grads)
    blocks = [_block_of(kind, width) for kind, width, _ in layouts]

    def body(ins, outs, send_sems, recv_sems):
        x, y, c = _coords()
        sibling = (x, y, 1 - c)
        cps = []
        for t in range(n):
            for k in range(4):
                px, py = _rel_chip(x, y, k)
                cp = pltpu.make_async_remote_copy(
                    src_ref=blocks[t](ins[t], 4 * px + 2 * py + (1 - c)), dst_ref=outs[t].at[k],
                    send_sem=send_sems.at[t, k], recv_sem=recv_sems.at[t, k], device_id=sibling, device_id_type=MESH)
                cp.start()
                cps.append(cp)
        for cp in cps:
            cp.wait_recv()
        for cp in cps:
            cp.wait_send()

    def sibling_only():
        x, y, c = _coords()
        return [(x, y, 1 - c)]

    return _sequencer_call(
        body, sibling_only, grads,
        [jax.ShapeDtypeStruct((4,) + shape, g.dtype) for g, (_, _, shape) in zip(grads, layouts)],
        [pltpu.SemaphoreType.DMA((n, 4)), pltpu.SemaphoreType.DMA((n, 4))], name, SEQ_ID_SIBLING, after)


def _rs_chip_sum(grad, recv, layout, xyc, name):
    kind, width, shape = layout
    r, ccols = shape

    def src_index(step, xyc_ref):
        k = step + 1
        px = jnp.where(k % 2 == 1, 1 - xyc_ref[0], xyc_ref[0])
        py = jnp.where(k // 2 == 1, 1 - xyc_ref[1], xyc_ref[1])
        return 4 * px + 2 * py + xyc_ref[2]

    if kind == "col":
        g_spec = pl.BlockSpec((r, ccols), lambda k, s_: (0, src_index(k, s_)))
    elif kind == "row":
        g_spec = pl.BlockSpec((r, ccols), lambda k, s_: (src_index(k, s_), 0))
    else:
        g_spec = pl.BlockSpec((None, r, ccols), lambda k, s_: (src_index(k, s_), 0, 0))

    def body(xyc_ref, g_ref, r_ref, o_ref):
        o_ref[...] = (g_ref[...].astype(F32) + r_ref[...].astype(F32)).astype(o_ref.dtype)

    slot = pl.BlockSpec((None, r, ccols), lambda k, s_: (k + 1, 0, 0))
    return pl.pallas_call(
        body, name=name,
        grid_spec=pltpu.PrefetchScalarGridSpec(num_scalar_prefetch=1, grid=(3,), in_specs=[g_spec, slot], out_specs=slot),
        out_shape=jax.ShapeDtypeStruct((4, r, ccols), grad.dtype), compiler_params=_cp(VMEM_BIG))(xyc, grad, recv)


def _rs_across_chips(parts, name):
    n = len(parts)

    def body(ins, outs, send_sems, recv_sems):
        x, y, c = _coords()
        cps = []
        for t in range(n):
            for k in range(1, 4):
                px, py = _rel_chip(x, y, k)
                cp = pltpu.make_async_remote_copy(
                    src_ref=ins[t].at[k], dst_ref=outs[t].at[k - 1], send_sem=send_sems.at[t, k - 1],
                    recv_sem=recv_sems.at[t, k - 1], device_id=(px, py, c), device_id_type=MESH)
                cp.start()
                cps.append(cp)
        for cp in cps:
            cp.wait_recv()
        for cp in cps:
            cp.wait_send()

    def other_chips():
        x, y, c = _coords()
        return [(*_rel_chip(x, y, k), c) for k in range(1, 4)]

    return _sequencer_call(
        body, other_chips, parts, [jax.ShapeDtypeStruct((3,) + p.shape[1:], p.dtype) for p in parts],
        [pltpu.SemaphoreType.DMA((n, 3)), pltpu.SemaphoreType.DMA((n, 3))], name, SEQ_ID_CHIPS)


def _adamw_math(w, g, m, v):
    m = ADAM_B1 * m + (1.0 - ADAM_B1) * g
    v = ADAM_B2 * v + (1.0 - ADAM_B2) * jnp.square(g)
    m_hat = m / (1.0 - ADAM_B1 ** ADAM_STEP)
    v_hat = v / (1.0 - ADAM_B2 ** ADAM_STEP)
    delta = -ADAM_LR * (m_hat / (jnp.sqrt(v_hat) + ADAM_EPS) + ADAM_WD * w)
    return delta, m, v


def _row_tile(rows, cap):
    best = None
    for cand in range(8, min(rows, cap) + 1, 8):
        if rows % cand == 0:
            best = cand
    assert best is not None, rows
    return best


def _adamw(w, m, v, own, parts, me, name, layer, prev=None, tr=256):
    r, ccols = w.shape[-2:]
    npart = len(parts)
    if r % 8 == 0:
        tr, tc = _row_tile(r, tr), ccols
        steps, at = r // tr, (lambda i: (i, 0))
    else:
        tr, tc = r, 256
        assert ccols % tc == 0
        steps, at = ccols // tc, (lambda i: (0, i))

    def spec(lead):
        return pl.BlockSpec((None, tr, tc), lambda i, me_ref: (lead,) + at(i))

    grad, kind = own
    if kind == "col":
        own_spec = pl.BlockSpec((tr, tc), lambda i, me_ref: (at(i)[0], me_ref[0]))
    elif kind == "row":
        own_spec = pl.BlockSpec((tr, tc), lambda i, me_ref: (me_ref[0] * (r // tr) + at(i)[0], 0))
    else:
        own_spec = pl.BlockSpec((None, tr, tc), lambda i, me_ref: (me_ref[0],) + at(i))

    def body(me_ref, *refs):
        w_ref, m_ref, v_ref = refs[:3]
        p_refs = refs[3:4 + npart]
        outs = refs[len(refs) - 4:]
        g = p_refs[0][...].astype(F32)
        for p_ref in p_refs[1:]:
            g = g + p_ref[...].astype(F32)
        delta, mn, vn = _adamw_math(w_ref[...], g, m_ref[...], v_ref[...])
        outs[0][...] = g
        outs[1][...] = delta
        outs[2][...] = mn
        outs[3][...] = vn

    operands = [w, m, v, grad] + [p for p, _ in parts]
    in_specs = [spec(layer)] * 3 + [own_spec] + [spec(lead) for _, lead in parts]
    aliases = {}
    if prev is not None:
        for i, p in enumerate(prev):
            aliases[1 + len(operands)] = i
            operands.append(p)
            in_specs.append(pl.BlockSpec(memory_space=pl.ANY))
    return pl.pallas_call(
        body, name=name,
        grid_spec=pltpu.PrefetchScalarGridSpec(num_scalar_prefetch=1, grid=(steps,), in_specs=in_specs,
                                               out_specs=[spec(layer)] * 4),
        out_shape=[jax.ShapeDtypeStruct(w.shape, F32)] * 4, input_output_aliases=aliases,
        compiler_params=_cp(VMEM_BIG))(me, *operands)


def _small_update(gathered, params, loss_all, me, name):
    n = len(gathered)
    shapes = [w.shape for w, _, _ in params]

    def body(me_ref, *refs):
        g_refs, loss_ref = refs[:n], refs[n]
        p_refs = refs[n + 1:n + 1 + 3 * n]
        o_refs = refs[n + 1 + 3 * n:]
        for i in range(n):
            r, c = shapes[i]
            if gathered[i].shape[2] == c:
                parts = [g_refs[i][j] for j in range(N_DEV)]
            else:
                off = pl.multiple_of(me_ref[0] * c, 128)
                parts = [g_refs[i][j, :, pl.ds(off, c)] for j in range(N_DEV)]
            g = functools.reduce(lambda a, b: a + b, parts)
            delta, mn, vn = _adamw_math(p_refs[3 * i][...], g, p_refs[3 * i + 1][...], p_refs[3 * i + 2][...])
            for k, val in enumerate((g, delta, mn, vn)):
                o_refs[4 * i + k][...] = val
        o_refs[4 * n][...] = functools.reduce(lambda a, b: a + b, [loss_ref[j] for j in range(N_DEV)])

    vmem = pl.BlockSpec(memory_space=pltpu.VMEM)
    flat_params = [a for p in params for a in p]
    outs = pl.pallas_call(
        body, name=name, in_specs=[pl.BlockSpec(memory_space=pltpu.SMEM)] + [vmem] * (n + 1 + 3 * n),
        out_specs=[vmem] * (4 * n + 1),
        out_shape=[jax.ShapeDtypeStruct(shp, F32) for shp in shapes for _ in range(4)] + [jax.ShapeDtypeStruct((1, 128), F32)],
        compiler_params=_cp(VMEM_BIG))(me, *gathered, loss_all, *flat_params)
    return [tuple(outs[4 * i:4 * i + 4]) for i in range(n)], outs[4 * n]


def _pack(arrays):
    pieces, layout, off = [], [], 0
    for a in arrays:
        n = a.size
        padded = -(-n // 1024) * 1024
        flat = a.reshape(-1).astype(F32)
        if padded != n:
            flat = jnp.pad(flat, (0, padded - n))
        pieces.append(flat.reshape(padded // 128, 128))
        layout.append((off, n, a.shape))
        off += padded // 128
    return jnp.concatenate(pieces, axis=0), layout


def kernel(x, mem, norm_mix, norm_ffn, mem_norm, w_kv, w_out, w_ffn1, w_ffn2, a_in, a_ln_g, a_ln_b, a_ws, a_bs, b_in, b_conv_w, b_conv_b, b_dt_bias, b_a_log, b_d, b_gnorm, final_norm, loss_target, m_norm_mix, m_norm_ffn, m_mem_norm, m_w_kv, m_w_out, m_w_ffn1, m_w_ffn2, m_a_in, m_a_ln_g, m_a_ln_b, m_a_ws, m_a_bs, m_b_in, m_b_conv_w, m_b_conv_b, m_b_dt_bias, m_b_a_log, m_b_d, m_b_gnorm, m_final_norm, v_norm_mix, v_norm_ffn, v_mem_norm, v_w_kv, v_w_out, v_w_ffn1, v_w_ffn2, v_a_in, v_a_ln_g, v_a_ln_b, v_a_ws, v_a_bs, v_b_in, v_b_conv_w, v_b_conv_b, v_b_dt_bias, v_b_a_log, v_b_d, v_b_gnorm, v_final_norm):
    s = x.shape[1]
    xs = x.reshape(s, D_MODEL)
    mems = mem.reshape(N_MEM, D_MODEL)
    target = loss_target.reshape(s, D_MODEL)
    ax, ay, ac = lax.axis_index("x"), lax.axis_index("y"), lax.axis_index("c")
    me = 4 * ax + 2 * ay + ac
    xyc = jnp.stack([ax, ay, ac]).astype(jnp.int32)
    me1 = me.astype(jnp.int32).reshape(1)

    b_cols = b_in.shape[2]
    act = lambda a: a.astype(_ACT)
    lay_f1, lay_f2 = ("col", 512, (1, D_MODEL, D_FF)), ("row", 512, (1, D_FF, D_MODEL))
    lay_out, lay_kv = ("row", 384, (1, 3 * D_MODEL, D_MODEL)), ("col", 256, (1, D_MODEL, 2 * X_WIDTH))
    small_w_pack = _pack([b_conv_w[0], b_conv_b[0], b_gnorm[0]])[0]
    (WA,) = _all_gather_seq([act(a_in)], [("col", 640, (1, D_MODEL, 5 * D_MODEL))], "ag_proj_a")
    wo0, wkv0 = _all_gather_seq([act(w_out[0:1]), act(w_kv[0:1])], [lay_out, lay_kv], "ag_out0")
    w1_0, w2_0 = _all_gather_seq([act(w_ffn1[0:1]), act(w_ffn2[0:1])], [lay_f1, lay_f2], "ag_ffn0")
    a0 = _rms_fwd(xs, norm_mix[0].reshape(1, -1), "mix_norm0")
    tr_b = lambda a: jnp.swapaxes(a, 1, 2)
    wbt_blk, small_w = _all_gather_seq(
        [act(tr_b(b_in)[0]), small_w_pack],
        [("blk", 0, (N_DEV, b_cols, D_MODEL)), ("blk", 0, (N_DEV, 32, 128))], "ag_proj_b", after=[a0])
    wo1, wkv1 = _all_gather_seq([act(w_out[1:2]), act(w_kv[1:2])], [lay_out, lay_kv], "ag_out1", after=[a0])
    w1_1, w2_1 = _all_gather_seq([act(w_ffn1[1:2]), act(w_ffn2[1:2])], [lay_f1, lay_f2], "ag_ffn1", after=[a0])
    W1, W2, WO, WKV = [w1_0, w1_1], [w2_0, w2_1], [wo0, wo1], [wkv0, wkv1]
    dt0 = D_INNER + CONV_DIM

    row = lambda a: a.reshape(1, -1)
    nmix = [row(norm_mix[0]), row(norm_mix[1])]
    nffn = [row(norm_ffn[0]), row(norm_ffn[1])]
    nmem = [row(mem_norm[0]), row(mem_norm[1])]
    fin = row(final_norm)
    lng, lnb = a_ln_g.reshape(1, D_INNER), a_ln_b.reshape(1, D_INNER)
    ws = a_ws[0]
    bs3 = a_bs[0].reshape(A_GROUPS, CHUNK, 1)
    pad_h = lambda a: jnp.pad(a.reshape(-1), (0, HPAD - SSM_HEADS))
    bias_row = pad_h(b_dt_bias).reshape(1, HPAD)
    alog_row = pad_h(b_a_log).reshape(1, HPAD)
    dfull = jnp.repeat(b_d.reshape(-1), SSM_P).reshape(1, D_INNER)

    kvs, mns = [None, None], [None, None]

    def mem_kv(i, after=None):
        gain = nmem[i] if after is None else _tie(nmem[i], after)
        mns[i] = _rms_fwd(mems, gain, f"mem_norm{i}")
        kvs[i] = _mm(mns[i], WKV[i], m=N_MEM, n=2 * X_WIDTH, k=D_MODEL, b_at=(0, 0, 0), out_dtype=_ACT, name=f"kv{i}")

    def ffn_fwd(h, i):
        f = _rms_fwd(h, nffn[i], f"ffn_norm{i}")
        p = _mm(f, W1[i], m=s, n=D_FF, k=D_MODEL, b_at=(0, 0, 0), out_dtype=_ACT, name=f"ffn_up{i}")
        hn = _mm(p, W2[i], m=s, n=D_MODEL, k=D_FF, b_at=(0, 0, 0), a_pro="relu2", add=h, name=f"ffn_down{i}")
        return f, p, hn

    def out_proj(h, cat, i):
        return _mm(cat, WO[i], m=s, n=D_MODEL, k=3 * D_MODEL, b_at=(0, 0, 0), add=h, name=f"out_proj{i}")

    proj_a = _mm(a0, WA, m=s, n=5 * D_MODEL, k=D_MODEL, b_at=(0, 0, 0), name="proj_a")
    mem_kv(0, after=[proj_a])
    cat_a = _gmlp_fwd(proj_a, lng, lnb, ws, bs3, "gmlp_fwd")
    cat_a = _attn_fwd(proj_a, 4, kvs[0], cat_a, "attn_fwd0")
    h1 = out_proj(xs, cat_a, 0)
    f0, p0, h2 = ffn_fwd(h1, 0)

    wbt_blk, small_w, _ = lax.optimization_barrier((wbt_blk, small_w, p0))
    jd, lo = divmod(dt0, b_cols)
    assert lo + SSM_HEADS <= b_cols
    wbt_full = wbt_blk.reshape(N_DEV * b_cols, D_MODEL)
    WBT = jnp.concatenate([wbt_full[:dt0], wbt_full[dt0 + SSM_HEADS:]], axis=0)
    WBDT = jnp.pad(wbt_full[dt0:dt0 + SSM_HEADS], ((0, HPAD - SSM_HEADS), (0, 0)))
    cw_sh, cb_sh, gn_sh = 4 * 384, 384, 256
    sw = small_w.reshape(N_DEV, 32 * 128)
    conv_w = jnp.transpose(sw[:, :cw_sh].reshape(N_DEV, CONV_K, 384), (1, 0, 2)).reshape(CONV_K, CONV_DIM)
    conv_b = sw[:, 2048:2048 + cb_sh].reshape(1, CONV_DIM)
    gnorm = sw[:, 3072:3072 + gn_sh].reshape(1, D_INNER)

    a1 = _rms_fwd(h2, nmix[1], "mix_norm1")
    proj_b = _mm(a1, WBT, m=s, n=6 * D_MODEL, k=D_MODEL, tb=True, name="proj_b")
    dt_raw = _mm(a1, WBDT, m=s, n=HPAD, k=D_MODEL, tb=True, name="proj_dt")
    xbc = _conv_fwd(proj_b, conv_w, conv_b, "conv_fwd")
    y_ssd, states, *ssd_expansions = _ssd_fwd(xbc, dt_raw, bias_row, alog_row, dfull, "ssd_fwd")
    cat_b = _gate_fwd(y_ssd, proj_b, gnorm, "gate_fwd")
    mem_kv(1, after=[cat_b])
    cat_b = _attn_fwd(proj_b, 5, kvs[1], cat_b, "attn_fwd1")
    h3 = out_proj(h2, cat_b, 1)
    f1, p1, h4 = ffn_fwd(h3, 1)

    loss_part, dh, dh_act, d_fin = _loss_head(h4, fin, target, "loss_head")

    g_f1, g_f2, g_out, g_kv = [None, None], [None, None], [None, None], [None, None]
    d_nffn, d_nmix, d_nmem = [None, None], [None, None], [None, None]

    def ffn_bwd(dh, dh_act, h_in, f, p, i, after=(), after_last=()):
        dp = _mm(dh_act, W2[i], m=s, n=D_FF, k=D_MODEL, tb=True, b_at=(0, 0, 0), epi_p=p, out_dtype=_ACT, name=f"ffn_down_dx{i}")
        g_f2[i] = _mm(p, dh_act, m=D_FF, n=D_MODEL, k=s, ta=True, a_pro="relu2", out_dtype=_ACT, name=f"ffn_down_dw{i}")
        g_f1[i] = _mm(f, dp, m=D_MODEL, n=D_FF, k=s, ta=True, out_dtype=_ACT, name=f"ffn_up_dw{i}")
        df = _mm(dp, W1[i], m=s, n=D_MODEL, k=D_FF, tb=True, b_at=(0, 0, 0), after=after, name=f"ffn_up_dx{i}")
        gain = _tie(nffn[i], after_last) if after_last else nffn[i]
        dh_in, dh_in_act, d_nffn[i] = _rms_bwd(h_in, gain, df, dh, f"ffn_norm_bwd{i}")
        return dh_in, dh_in_act

    def out_bwd(dh_act, cat, i):
        dcat = _mm(dh_act, WO[i], m=s, n=3 * D_MODEL, k=D_MODEL, tb=True, b_at=(0, 0, 0), out_dtype=_ACT, name=f"out_dx{i}")
        g_out[i] = _mm(cat, dh_act, m=3 * D_MODEL, n=D_MODEL, k=s, ta=True, out_dtype=_ACT, name=f"out_dw{i}")
        return dcat

    def mem_bwd(dkv, i):
        g_kv[i] = _mm(mns[i], dkv, m=D_MODEL, n=2 * X_WIDTH, k=N_MEM, ta=True, out_dtype=_ACT, name=f"kv_dw{i}")
        dmn = _mm(dkv, WKV[i], m=N_MEM, n=D_MODEL, k=2 * X_WIDTH, tb=True, b_at=(0, 0, 0), name=f"kv_dx{i}")
        _, _, d_nmem[i] = _rms_bwd(mems, nmem[i], dmn, None, f"mem_norm_bwd{i}")

    lay_g = {"f1": ("col", 512, (D_MODEL, 512)), "f2": ("row", 512, (512, D_MODEL)), "out": ("row", 384, (384, D_MODEL)),
             "kv": ("col", 256, (D_MODEL, 256)), "a": ("col", 640, (D_MODEL, 640)), "b": ("blk", 0, (b_cols, D_MODEL))}
    reduced = {}

    def reduce_scatter(group, tag, after=(), sums_after=()):
        grads3, lays3 = [], []
        for fam, _, g in group:
            kind, width, shape = lay_g[fam]
            grads3.append(g if kind == "blk" else g.reshape((1,) + g.shape))
            lays3.append((kind, width, shape if kind == "blk" else (1,) + shape))
        recv1 = _rs_to_sibling(grads3, lays3, f"rs_sibling_{tag}", after)
        if sums_after:
            recv1 = lax.optimization_barrier((tuple(recv1), tuple(sums_after)))[0]
        recv1 = [recv1[t].reshape((4,) + lay_g[fam][2]) for t, (fam, _, _) in enumerate(group)]
        parts = [_rs_chip_sum(g, r1, lay_g[fam], xyc, f"rs_chip_sum_{fam}{i}") for r1, (fam, i, g) in zip(recv1, group)]
        recv2 = _rs_across_chips(parts, f"rs_chips_{tag}")
        for (fam, i, g), r1, r2 in zip(group, recv1, recv2):
            reduced[fam, i] = (g, r1, r2)
        return parts, recv2

    dh3, dh3_act = ffn_bwd(dh, dh_act, h3, f1, p1, 1)
    dcat_b = out_bwd(dh3_act, cat_b, 1)
    sums, got_ffn1 = reduce_scatter([("f1", 1, g_f1[1]), ("f2", 1, g_f2[1]), ("out", 1, g_out[1])], "ffn1", sums_after=[dcat_b])
    dy_ssd, dproj_b, d_gnorm = _gate_bwd(y_ssd, proj_b, gnorm, dcat_b, "gate_bwd")
    dproj_b, dkv_b = _attn_bwd(proj_b, 5, kvs[1], dcat_b, dproj_b, "attn_bwd1")
    mem_bwd(dkv_b, 1)
    dxbc, ddt_raw, d_alog, d_dskip, d_dtbias = _ssd_bwd(
        xbc, dt_raw, _tie(bias_row, sums), alog_row, dfull, dy_ssd, states, ssd_expansions, "ssd_bwd")
    dproj_b, d_convw, d_convb = _conv_bwd(proj_b, conv_w, _tie(conv_b, got_ffn1), dxbc, dproj_b, "conv_bwd")
    gb = _mm(dproj_b, a1, m=6 * D_MODEL, n=D_MODEL, k=s, ta=True, out_dtype=_ACT, name="proj_b_dw")
    gb_dt = _mm(ddt_raw, a1, m=HPAD, n=D_MODEL, k=s, ta=True, out_dtype=_ACT, name="proj_b_dw_dt")
    blocks_b = [gb[j * b_cols:(j + 1) * b_cols] for j in range(jd)]
    blocks_b.append(jnp.concatenate([gb[jd * b_cols:dt0], gb_dt[:SSM_HEADS], gb[dt0:(jd + 1) * b_cols - SSM_HEADS]], axis=0))
    blocks_b += [gb[j * b_cols - SSM_HEADS:(j + 1) * b_cols - SSM_HEADS] for j in range(jd + 1, N_DEV)]
    gb_blk = jnp.stack(blocks_b)
    da1 = _mm(dproj_b, WBT, m=s, n=D_MODEL, k=6 * D_MODEL, name="proj_b_dx")
    sums, got_mix1 = reduce_scatter([("kv", 1, g_kv[1]), ("b", 0, gb_blk)], "mix1", sums_after=[da1])
    da1 = _mm(ddt_raw, WBDT, m=s, n=D_MODEL, k=HPAD, add=da1, name="proj_b_dx_dt")
    dh2, dh2_act, d_nmix[1] = _rms_bwd(h2, _tie(nmix[1], sums), da1, dh3, "mix_norm_bwd1")

    dh1, dh1_act = ffn_bwd(dh2, dh2_act, h1, f0, p0, 0, after=got_ffn1, after_last=got_mix1)
    dcat_a = out_bwd(dh1_act, cat_a, 0)
    sums, got_ffn0 = reduce_scatter([("f1", 0, g_f1[0]), ("f2", 0, g_f2[0]), ("out", 0, g_out[0])], "ffn0", sums_after=[dcat_a])
    dproj_a, d_ws, d_bs3, d_lng, d_lnb = _gmlp_bwd(proj_a, dcat_a, _tie(lng, sums), lnb, ws, bs3, "gmlp_bwd")
    dproj_a, dkv_a = _attn_bwd(proj_a, 4, kvs[0], dcat_a, dproj_a, "attn_bwd0")
    mem_bwd(dkv_a, 0)

    def big_update(w, m, v, fam, nlayer):
        res = None
        for i in range(nlayer):
            grad, recv1, recv2 = reduced[fam, i]
            plist = [(recv1, 0), (recv2, 0), (recv2, 1), (recv2, 2)]
            res = _adamw(w, m, v, (grad, lay_g[fam][0]), plist, me1, f"adamw_{fam}{i}", layer=i, prev=res)
        return res

    da0 = _mm(dproj_a, WA, m=s, n=D_MODEL, k=5 * D_MODEL, tb=True, b_at=(0, 0, 0), name="proj_a_dx")
    grad_x, _, d_nmix[0] = _rms_bwd(xs, nmix[0], da0, dh1, "mix_norm_bwd0")
    ga = _mm(a0, dproj_a, m=D_MODEL, n=5 * D_MODEL, k=s, ta=True, out_dtype=_ACT, after=[grad_x], name="proj_a_dw")
    r_b = big_update(tr_b(b_in), tr_b(m_b_in), tr_b(v_b_in), "b", 1)
    reduce_scatter([("kv", 0, g_kv[0]), ("a", 0, ga)], "mix0", after=got_ffn0, sums_after=r_b)
    r_b = [tr_b(o) for o in r_b]

    small_names = ["norm_mix", "norm_ffn", "mem_norm", "a_ln_g", "a_ln_b", "a_ws", "a_bs", "b_dt_bias", "b_a_log", "b_d",
                   "final_norm", "b_conv_w", "b_conv_b", "b_gnorm"]
    small_grads = [jnp.concatenate(d_nmix, axis=0), jnp.concatenate(d_nffn, axis=0), jnp.concatenate(d_nmem, axis=0),
                   d_lng, d_lnb, d_ws.reshape(A_GROUPS * CHUNK, CHUNK), d_bs3.reshape(A_GROUPS, CHUNK),
                   d_dtbias[:, :SSM_HEADS], d_alog[:, :SSM_HEADS], d_dskip[:, :SSM_HEADS], d_fin,
                   d_convw, d_convb, d_gnorm]
    small_2d = [(2, D_MODEL)] * 3 + [(1, D_INNER)] * 2 + [(A_GROUPS * CHUNK, CHUNK), (A_GROUPS, CHUNK)] + [(1, SSM_HEADS)] * 3 \
        + [(1, D_MODEL), (CONV_K, 384), (1, 384), (1, 256)]
    gathered = _all_gather_seq(
        small_grads + [loss_part], [("blk", 0, (N_DEV,) + g.shape) for g in small_grads + [loss_part]], "ag_small_grads")

    r_f1 = big_update(w_ffn1, m_w_ffn1, v_w_ffn1, "f1", 2)
    r_f2 = big_update(w_ffn2, m_w_ffn2, v_w_ffn2, "f2", 2)
    r_out = big_update(w_out, m_w_out, v_w_out, "out", 2)
    r_kv = big_update(w_kv, m_w_kv, v_w_kv, "kv", 2)
    r_a = big_update(a_in, m_a_in, v_a_in, "a", 1)

    small_w = [norm_mix, norm_ffn, mem_norm, a_ln_g, a_ln_b, a_ws, a_bs, b_dt_bias, b_a_log, b_d, final_norm,
               b_conv_w, b_conv_b, b_gnorm]
    small_m = [m_norm_mix, m_norm_ffn, m_mem_norm, m_a_ln_g, m_a_ln_b, m_a_ws, m_a_bs, m_b_dt_bias, m_b_a_log, m_b_d,
               m_final_norm, m_b_conv_w, m_b_conv_b, m_b_gnorm]
    small_v = [v_norm_mix, v_norm_ffn, v_mem_norm, v_a_ln_g, v_a_ln_b, v_a_ws, v_a_bs, v_b_dt_bias, v_b_a_log, v_b_d,
               v_final_norm, v_b_conv_w, v_b_conv_b, v_b_gnorm]
    params = [tuple(a.reshape(shp) for a in wmv) for shp, wmv in zip(small_2d, zip(small_w, small_m, small_v))]
    loss_all = _tie(gathered[-1], [r_a[0], r_kv[0]])
    small_res, loss_sum = _small_update(gathered[:-1], params, loss_all, me1, "adamw_small")
    loss = loss_sum[0, 0]

    names = ["norm_mix", "norm_ffn", "mem_norm", "w_kv", "w_out", "w_ffn1", "w_ffn2", "a_in", "a_ln_g", "a_ln_b", "a_ws",
             "a_bs", "b_in", "b_conv_w", "b_conv_b", "b_dt_bias", "b_a_log", "b_d", "b_gnorm", "final_norm"]
    big = {"w_kv": r_kv, "w_out": r_out, "w_ffn1": r_f1, "w_ffn2": r_f2, "a_in": r_a, "b_in": r_b}
    outs = [loss, grad_x.reshape(x.shape)]
    for kind in range(4):
        for nm in names:
            if nm in big:
                outs.append(big[nm][kind])
            else:
                i = small_names.index(nm)
                outs.append(small_res[i][kind].reshape(small_w[i].shape))
    return tuple(outs)
```

```python
import functools
import math

import jax
import jax.numpy as jnp
from jax import lax
from jax.experimental import pallas as pl
from jax.experimental.pallas import tpu as pltpu
from jax.experimental.pallas import tpu_sc as plsc

F32 = jnp.float32
_MXU = jnp.bfloat16
_ACT = jnp.bfloat16

D_MODEL = 1024
CHUNK = 128
N_MEM = 256
D_INNER = 2048
A_GROUPS = 8
A_GW = D_INNER // A_GROUPS
SSM_HEADS = 32
SSM_P = 64
SSM_GROUPS = 4
SSM_GW = D_INNER // SSM_GROUPS
SSM_N = 128
CONV_K = 4
CONV_DIM = 3072
X_HEADS = 4
X_HD = 256
X_WIDTH = 1024
D_FF = 4096
EPS = 1e-6
HPAD = 128
N_DEV = 8

ADAM_LR = 0.001
ADAM_B1 = 0.9
ADAM_B2 = 0.999
ADAM_EPS = 1e-08
ADAM_WD = 0.01
ADAM_STEP = 10

VMEM_BIG = 56 * 1024 * 1024
MESH = pl.DeviceIdType.MESH


def _cp(vmem=None):
    if vmem is None:
        return pltpu.CompilerParams()
    return pltpu.CompilerParams(vmem_limit_bytes=vmem)


def _dot(a, b, dims=((1,), (0,))):
    return lax.dot_general(a.astype(_MXU), b.astype(_MXU), (dims, ((), ())), preferred_element_type=F32)


def _dot_nt(a, b):
    return _dot(a, b, ((1,), (1,)))


def _dot_tn(a, b):
    return _dot(a, b, ((0,), (0,)))


def _split3(x):
    x1 = x.astype(jnp.bfloat16)
    r = x - x1.astype(F32)
    x2 = r.astype(jnp.bfloat16)
    x3 = (r - x2.astype(F32)).astype(jnp.bfloat16)
    return x1, x2, x3


def _dot_sel(x, sel, dims=((1,), (0,)), terms=2):
    sel = sel.astype(jnp.bfloat16)
    parts = [lax.dot_general(t, sel, (dims, ((), ())), preferred_element_type=F32) for t in _split3(x)[:terms]]
    return functools.reduce(lambda a, b: a + b, parts)


def _sel_dot(sel, x, dims=((1,), (0,))):
    sel = sel.astype(jnp.bfloat16)
    parts = [lax.dot_general(sel, t, (dims, ((), ())), preferred_element_type=F32) for t in _split3(x)]
    return (parts[0] + parts[1]) + parts[2]


def _sigmoid(x):
    return 1.0 / (1.0 + jnp.exp(-x))


def _gelu(x):
    return 0.5 * x * (1.0 + lax.erf(x * (1.0 / math.sqrt(2.0))))


def _gelu_with_grad(x):
    phi = 0.5 * (1.0 + lax.erf(x * (1.0 / math.sqrt(2.0))))
    return x * phi, phi + x * jnp.exp(-0.5 * x * x) * (1.0 / math.sqrt(2.0 * math.pi))


def _softplus(x):
    return jnp.maximum(x, 0.0) + jnp.log1p(jnp.exp(-jnp.abs(x)))


def _iota(shape, dim):
    return lax.broadcasted_iota(jnp.int32, shape, dim)


MM_VMEM_BUDGET = 40 * 1024 * 1024
HBM_BYTES_PER_S = 2.5e12
GRID_STEP_S = 0.35e-6
VMEM_ACC_BYTES_PER_S = 6e12


def _divisors(dim, unit):
    out = [d for d in range(unit, min(dim, 2048) + 1, unit) if dim % d == 0]
    return out if out else [dim]


def _mm_tiles(m, n, k, sa, sb, s_mn, a_pro, offsets):
    best = None
    (a_r0, a_c0, ta), (b_r0, b_c0, tb), (o_r0, o_c0) = offsets
    for tm in _divisors(m, 128):
        for tn in _divisors(n, 128):
            for tk in [k // d for d in (1, 2, 3, 4, 6, 8) if k % d == 0 and (k // d) % 128 == 0]:
                a_t = (tk, tm) if ta else (tm, tk)
                b_t = (tn, tk) if tb else (tk, tn)
                if a_r0 % a_t[0] or a_c0 % a_t[1] or b_r0 % b_t[0] or b_c0 % b_t[1] or o_r0 % tm or o_c0 % tn:
                    continue
                nk = k // tk
                vmem = 2 * (tm * tk * sa + tk * tn * sb + tm * tn * s_mn) + tm * tn * 4 * (2 if nk > 1 else 1)
                if a_pro or sa == 4:
                    vmem += tm * tk * 6
                if sb == 4:
                    vmem += tk * tn * 2
                if vmem > MM_VMEM_BUDGET:
                    continue
                gi, gj = m // tm, n // tn
                for j_inner in (True, False):
                    if nk > 1:
                        traffic = gj * m * k * sa + gi * k * n * sb
                    elif j_inner:
                        traffic = m * k * sa + gi * k * n * sb
                    else:
                        traffic = gj * m * k * sa + k * n * sb
                    traffic += m * n * s_mn + (tm * tk * sa + tk * tn * sb)
                    cost = traffic / HBM_BYTES_PER_S + gi * gj * nk * GRID_STEP_S
                    if nk > 1:
                        cost += m * n * 8 * nk / VMEM_ACC_BYTES_PER_S
                    if best is None or cost < best[0]:
                        best = (cost, tm, tn, tk, j_inner)
    assert best is not None, (m, n, k)
    return best[1:]


def _mm(a, b, *, m, n, k, name, ta=False, tb=False, a_at=(None, 0, 0), b_at=(None, 0, 0),
        out_dtype=F32, add=None, epi_p=None, epi_at=(None, 0, 0), out=None, out_at=(None, 0, 0),
        out_full=None, a_pro=None, after=()):
    s_mn =jnp.dtype(out.dtype if out is not None else out_dtype).itemsize
    s_mn += add.dtype.itemsize if add is not None else 0
    s_mn += epi_p.dtype.itemsize if epi_p is not None else 0
    tm, tn, tk, j_inner = _mm_tiles(m, n, k, a.dtype.itemsize, b.dtype.itemsize, s_mn, a_pro is not None,
                                    ((a_at[1], a_at[2], ta), (b_at[1], b_at[2], tb), (out_at[1], out_at[2])))
    nk = k // tk

    def spec(at, tr, tc, rsel, csel):
        lead, r0, c0 = at
        assert r0 % tr == 0 and c0 % tc == 0, (name, at, tr, tc)
        rb, cb = r0 // tr, c0 // tc
        if lead is None:
            return pl.BlockSpec((tr, tc), lambda g0, g1, kk: (rb + rsel(g0, g1, kk), cb + csel(g0, g1, kk)))
        return pl.BlockSpec((None, tr, tc), lambda g0, g1, kk: (lead, rb + rsel(g0, g1, kk), cb + csel(g0, g1, kk)))

    gi = (lambda g0, g1, kk: g0) if j_inner else (lambda g0, g1, kk: g1)
    gj = (lambda g0, g1, kk: g1) if j_inner else (lambda g0, g1, kk: g0)
    gk = lambda g0, g1, kk: kk
    a_spec = spec(a_at, tk, tm, gk, gi) if ta else spec(a_at, tm, tk, gi, gk)
    b_spec = spec(b_at, tn, tk, gj, gk) if tb else spec(b_at, tk, tn, gk, gj)
    dims = ((0,), (0,)) if ta else (((1,), (1,)) if tb else ((1,), (0,)))
    assert not (ta and tb)

    operands, in_specs = [a, b], [a_spec, b_spec]
    if add is not None:
        operands.append(add)
        in_specs.append(spec((None, 0, 0), tm, tn, gi, gj))
    if epi_p is not None:
        operands.append(epi_p)
        in_specs.append(spec(epi_at, tm, tn, gi, gj))
    aliases = {}
    if out is not None:
        aliases = {len(operands): 0}
        operands.append(out)
        in_specs.append(pl.BlockSpec(memory_space=pl.ANY))
        out_struct = jax.ShapeDtypeStruct(out.shape, out.dtype)
        out_dtype = out.dtype
    else:
        out_struct = jax.ShapeDtypeStruct(out_full if out_full is not None else (m, n), out_dtype)
    has_add, has_epi = add is not None, epi_p is not None
    n_skip = (1 if out is not None else 0) + len(after)
    operands += list(after)
    in_specs += [pl.BlockSpec(memory_space=pl.ANY)] * len(after)

    def body(*refs):
        a_ref, b_ref = refs[0], refs[1]
        pos = 2
        add_ref = epi_ref = None
        if has_add:
            add_ref = refs[pos]
            pos += 1
        if has_epi:
            epi_ref = refs[pos]
            pos += 1
        pos += n_skip
        o_ref = refs[pos]

        def finish(r):
            if has_add:
                r = r + add_ref[...].astype(F32)
            if has_epi:
                r = r * (2.0 * jnp.maximum(epi_ref[...].astype(F32), 0.0))
            o_ref[...] = r.astype(o_ref.dtype)

        av = a_ref[...]
        if a_pro == "relu2":
            av = jnp.square(jnp.maximum(av.astype(F32), 0.0))
        part = _dot(av, b_ref[...], dims)
        if nk == 1:
            finish(part)
        else:
            acc_ref = refs[pos + 1]
            kk = pl.program_id(2)

            @pl.when(kk == 0)
            def _():
                acc_ref[...] = part

            @pl.when(kk > 0)
            def _():
                acc_ref[...] += part

            @pl.when(kk == nk - 1)
            def _():
                finish(acc_ref[...])

    grid = (m // tm, n // tn, nk) if j_inner else (n // tn, m // tm, nk)
    return pl.pallas_call(
        body, name=name, grid=grid, in_specs=in_specs,
        out_specs=spec(out_at, tm, tn, gi, gj), out_shape=out_struct,
        scratch_shapes=[pltpu.VMEM((tm, tn), F32)] if nk > 1 else [], input_output_aliases=aliases,
        compiler_params=_cp(VMEM_BIG))(*operands)


def _rms_fwd(x, g, name, tm=1024):
    s, d = x.shape
    tm = min(tm, s)

    def body(x_ref, g_ref, o_ref):
        xv = x_ref[...]
        r = lax.rsqrt(jnp.mean(xv * xv, axis=-1, keepdims=True) + EPS)
        o_ref[...] = (xv * r * g_ref[...]).astype(o_ref.dtype)

    return pl.pallas_call(
        body, name=name, grid=(s // tm,),
        in_specs=[pl.BlockSpec((tm, d), lambda i: (i, 0)), pl.BlockSpec((1, d), lambda i: (0, 0))],
        out_specs=pl.BlockSpec((tm, d), lambda i: (i, 0)),
        out_shape=jax.ShapeDtypeStruct((s, d), _ACT), compiler_params=_cp(VMEM_BIG))(x, g)


def _rms_bwd(x, g, dy, dres, name, tm=512):
    s, d = x.shape
    tm = min(tm, s)
    has_res = dres is not None

    def body(*refs):
        if has_res:
            x_ref, g_ref, dy_ref, dres_ref, dx_ref, dxa_ref, dg_ref = refs
        else:
            x_ref, g_ref, dy_ref, dx_ref, dxa_ref, dg_ref = refs

        @pl.when(pl.program_id(0) == 0)
        def _():
            dg_ref[...] = jnp.zeros_like(dg_ref)

        xv = x_ref[...]
        dyv = dy_ref[...].astype(F32)
        r = lax.rsqrt(jnp.mean(xv * xv, axis=-1, keepdims=True) + EPS)
        xh = xv * r
        dyg = dyv * g_ref[...]
        dx = r * (dyg - xh * jnp.mean(dyg * xh, axis=-1, keepdims=True))
        if has_res:
            dx = dx + dres_ref[...]
        dx_ref[...] = dx
        dxa_ref[...] = dx.astype(dxa_ref.dtype)
        dg_ref[...] += jnp.sum(dyv * xh, axis=0, keepdims=True)

    row = pl.BlockSpec((tm, d), lambda i: (i, 0))
    vec = pl.BlockSpec((1, d), lambda i: (0, 0))
    in_specs = [row, vec, row] + ([row] if has_res else [])
    operands = [x, g, dy] + ([dres] if has_res else [])
    return pl.pallas_call(
        body, name=name, grid=(s // tm,), in_specs=in_specs, out_specs=[row, row, vec],
        out_shape=[jax.ShapeDtypeStruct((s, d), F32), jax.ShapeDtypeStruct((s, d), _ACT),
                   jax.ShapeDtypeStruct((1, d), F32)], compiler_params=_cp(VMEM_BIG))(*operands)


def _loss_head(h, g, target, name, tm=512):
    s, d = h.shape
    tm = min(tm, s)

    def body(h_ref, g_ref, t_ref, loss_ref, dh_ref, dha_ref, dg_ref):
        @pl.when(pl.program_id(0) == 0)
        def _():
            dg_ref[...] = jnp.zeros_like(dg_ref)
            loss_ref[...] = jnp.zeros_like(loss_ref)

        xv = h_ref[...]
        r = lax.rsqrt(jnp.mean(xv * xv, axis=-1, keepdims=True) + EPS)
        xh = xv * r
        err = xh * g_ref[...] - t_ref[...]
        loss_ref[...] += jnp.full(loss_ref.shape, 0.5 * jnp.sum(jnp.mean(err * err, axis=-1, keepdims=True)), F32)
        dyv = err * (1.0 / d)
        dyg = dyv * g_ref[...]
        dh = r * (dyg - xh * jnp.mean(dyg * xh, axis=-1, keepdims=True))
        dh_ref[...] = dh
        dha_ref[...] = dh.astype(dha_ref.dtype)
        dg_ref[...] += jnp.sum(dyv * xh, axis=0, keepdims=True)

    row = pl.BlockSpec((tm, d), lambda i: (i, 0))
    vec = pl.BlockSpec((1, d), lambda i: (0, 0))
    return pl.pallas_call(
        body, name=name, grid=(s // tm,), in_specs=[row, vec, row],
        out_specs=[pl.BlockSpec((1, 128), lambda i: (0, 0)), row, row, vec],
        out_shape=[jax.ShapeDtypeStruct((1, 128), F32), jax.ShapeDtypeStruct((s, d), F32),
                   jax.ShapeDtypeStruct((s, d), _ACT), jax.ShapeDtypeStruct((1, d), F32)],
        compiler_params=_cp(VMEM_BIG))(h, g, target)


def _gmlp_parts(u, v, lng, lnb):
    mu = jnp.mean(v, axis=-1, keepdims=True)
    vc = v - mu
    rstd = lax.rsqrt(jnp.mean(vc * vc, axis=-1, keepdims=True) + EPS)
    xhat = vc * rstd
    vn = xhat * lng + lnb
    return u, xhat, rstd, vn


def _gmlp_fwd(proj, lng, lnb, ws, bs3, name):
    s = proj.shape[0]

    def body(pu_ref, pv_ref, lng_ref, lnb_ref, ws_ref, bs_ref, o_ref):
        u, _, _, vn = _gmlp_parts(_gelu(pu_ref[...]), _gelu(pv_ref[...]), lng_ref[...], lnb_ref[...])
        causal = _iota((CHUNK, CHUNK), 0) >= _iota((CHUNK, CHUNK), 1)
        for g in range(A_GROUPS):
            sl = slice(g * A_GW, (g + 1) * A_GW)
            w = jnp.where(causal, ws_ref[g], 0.0)
            sv = _dot(w, vn[:, sl]) + bs_ref[g]
            o_ref[:, sl] = (u[:, sl] * sv).astype(o_ref.dtype)

    full = lambda shape: pl.BlockSpec(shape, lambda c: (0,) * len(shape))
    return pl.pallas_call(
        body, name=name, grid=(s // CHUNK,),
        in_specs=[pl.BlockSpec((CHUNK, D_INNER), lambda c: (c, 0)), pl.BlockSpec((CHUNK, D_INNER), lambda c: (c, 1)),
                  full((1, D_INNER)), full((1, D_INNER)), full((A_GROUPS, CHUNK, CHUNK)), full((A_GROUPS, CHUNK, 1))],
        out_specs=pl.BlockSpec((CHUNK, D_INNER), lambda c: (c, 0)),
        out_shape=jax.ShapeDtypeStruct((s, D_INNER + X_WIDTH), _ACT), compiler_params=_cp(VMEM_BIG))(proj, proj, lng, lnb, ws, bs3)


def _gmlp_bwd(proj, dcat, lng, lnb, ws, bs3, name):
    s = proj.shape[0]

    def body(pu_ref, pv_ref, dm_ref, lng_ref, lnb_ref, ws_ref, bs_ref, dp_ref, dws_ref, dbs_ref, dlng_ref, dlnb_ref, dvn_ref):
        @pl.when(pl.program_id(0) == 0)
        def _():
            dws_ref[...] = jnp.zeros_like(dws_ref)
            dbs_ref[...] = jnp.zeros_like(dbs_ref)
            dlng_ref[...] = jnp.zeros_like(dlng_ref)
            dlnb_ref[...] = jnp.zeros_like(dlnb_ref)

        lng = lng_ref[...]
        u, u_grad = _gelu_with_grad(pu_ref[...])
        v, v_grad = _gelu_with_grad(pv_ref[...])
        u, xhat, rstd, vn = _gmlp_parts(u, v, lng, lnb_ref[...])
        dm = dm_ref[...].astype(F32)
        causal = _iota((CHUNK, CHUNK), 0) >= _iota((CHUNK, CHUNK), 1)
        for g in range(A_GROUPS):
            sl = slice(g * A_GW, (g + 1) * A_GW)
            w = jnp.where(causal, ws_ref[g], 0.0)
            sv = _dot(w, vn[:, sl]) + bs_ref[g]
            dsv = dm[:, sl] * u[:, sl]
            dp_ref[:, sl] = (dm[:, sl] * sv * u_grad[:, sl]).astype(dp_ref.dtype)
            dvn_ref[:, sl] = _dot_tn(w, dsv)
            dws_ref[g] += jnp.where(causal, _dot_nt(dsv, vn[:, sl]), 0.0)
            dbs_ref[g] += jnp.sum(dsv, axis=-1, keepdims=True)
        dvn = dvn_ref[...]
        dlng_ref[...] += jnp.sum(dvn * xhat, axis=0, keepdims=True)
        dlnb_ref[...] += jnp.sum(dvn, axis=0, keepdims=True)
        dxh = dvn * lng
        dv = rstd * (dxh - jnp.mean(dxh, axis=-1, keepdims=True) - xhat * jnp.mean(dxh * xhat, axis=-1, keepdims=True))
        dp_ref[:, D_INNER:] = (dv * v_grad).astype(dp_ref.dtype)

    full = lambda shape: pl.BlockSpec(shape, lambda c: (0,) * len(shape))
    return pl.pallas_call(
        body, name=name, grid=(s // CHUNK,),
        in_specs=[pl.BlockSpec((CHUNK, D_INNER), lambda c: (c, 0)), pl.BlockSpec((CHUNK, D_INNER), lambda c: (c, 1)),
                  pl.BlockSpec((CHUNK, D_INNER), lambda c: (c, 0)),
                  full((1, D_INNER)), full((1, D_INNER)), full((A_GROUPS, CHUNK, CHUNK)), full((A_GROUPS, CHUNK, 1))],
        out_specs=[pl.BlockSpec((CHUNK, 2 * D_INNER), lambda c: (c, 0)), full((A_GROUPS, CHUNK, CHUNK)),
                   full((A_GROUPS, CHUNK, 1)), full((1, D_INNER)), full((1, D_INNER))],
        out_shape=[jax.ShapeDtypeStruct((s, 2 * D_INNER + X_WIDTH), _ACT), jax.ShapeDtypeStruct((A_GROUPS, CHUNK, CHUNK), F32),
                   jax.ShapeDtypeStruct((A_GROUPS, CHUNK, 1), F32), jax.ShapeDtypeStruct((1, D_INNER), F32),
                   jax.ShapeDtypeStruct((1, D_INNER), F32)],
        scratch_shapes=[pltpu.VMEM((CHUNK, D_INNER), F32)],
        compiler_params=_cp(VMEM_BIG))(proj, proj, dcat, lng, lnb, ws, bs3)


_X_SCALE = 1.0 / math.sqrt(X_HD)


def _attn_fwd(proj, qblk, kv, cat, name, tm=512):
    s = proj.shape[0]
    tm = min(tm, s)

    def body(q_ref, kv_ref, cat_ref, o_ref):
        for h in range(X_HEADS):
            sl = slice(h * X_HD, (h + 1) * X_HD)
            k = kv_ref[:, sl]
            v = kv_ref[:, X_WIDTH + h * X_HD:X_WIDTH + (h + 1) * X_HD]
            sc = _dot_nt(q_ref[:, sl], k) * _X_SCALE
            e = jnp.exp(sc - jnp.max(sc, axis=-1, keepdims=True))
            p = e / jnp.sum(e, axis=-1, keepdims=True)
            o_ref[:, sl] = _dot(p, v).astype(o_ref.dtype)

    return pl.pallas_call(
        body, name=name, grid=(s // tm,),
        in_specs=[pl.BlockSpec((tm, X_WIDTH), lambda i: (i, qblk)), pl.BlockSpec((N_MEM, 2 * X_WIDTH), lambda i: (0, 0)),
                  pl.BlockSpec(memory_space=pl.ANY)],
        out_specs=pl.BlockSpec((tm, X_WIDTH), lambda i: (i, D_INNER // X_WIDTH)),
        out_shape=jax.ShapeDtypeStruct(cat.shape, cat.dtype), input_output_aliases={2: 0},
        compiler_params=_cp(VMEM_BIG))(proj, kv, cat)


def _attn_bwd(proj, qblk, kv, dcat, dproj, name, tm=512):
    s = proj.shape[0]
    tm = min(tm, s)

    def body(q_ref, kv_ref, do_ref, dproj_ref, dq_ref, dkv_ref):
        @pl.when(pl.program_id(0) == 0)
        def _():
            dkv_ref[...] = jnp.zeros_like(dkv_ref)

        for h in range(X_HEADS):
            sl = slice(h * X_HD, (h + 1) * X_HD)
            slv = slice(X_WIDTH + h * X_HD, X_WIDTH + (h + 1) * X_HD)
            q = q_ref[:, sl]
            k = kv_ref[:, sl]
            v = kv_ref[:, slv]
            do = do_ref[:, sl].astype(F32)
            sc = _dot_nt(q, k) * _X_SCALE
            e = jnp.exp(sc - jnp.max(sc, axis=-1, keepdims=True))
            p = e / jnp.sum(e, axis=-1, keepdims=True)
            dp = _dot_nt(do, v)
            ds = p * (dp - jnp.sum(dp * p, axis=-1, keepdims=True)) * _X_SCALE
            dq_ref[:, sl] = _dot(ds, k).astype(dq_ref.dtype)
            dkv_ref[:, sl] += _dot_tn(ds, q)
            dkv_ref[:, slv] += _dot_tn(p, do)

    return pl.pallas_call(
        body, name=name, grid=(s // tm,),
        in_specs=[pl.BlockSpec((tm, X_WIDTH), lambda i: (i, qblk)), pl.BlockSpec((N_MEM, 2 * X_WIDTH), lambda i: (0, 0)),
                  pl.BlockSpec((tm, X_WIDTH), lambda i: (i, 2)), pl.BlockSpec(memory_space=pl.ANY)],
        out_specs=[pl.BlockSpec((tm, X_WIDTH), lambda i: (i, qblk)), pl.BlockSpec((N_MEM, 2 * X_WIDTH), lambda i: (0, 0))],
        out_shape=[jax.ShapeDtypeStruct(dproj.shape, dproj.dtype), jax.ShapeDtypeStruct((N_MEM, 2 * X_WIDTH), F32)],
        input_output_aliases={3: 0}, compiler_params=_cp(VMEM_BIG))(proj, kv, dcat, dproj)


CONV_TC = 256
_XBC_BLK0 = D_INNER // CONV_TC


CONV_RB = 64
SUBLANES = 8


def _rows_before(cur, prev_last, j):
    rolled = pltpu.roll(cur, j, 0)
    head = jnp.where(_iota((SUBLANES, cur.shape[1]), 0) < j, pltpu.roll(prev_last, j, 0), rolled[:SUBLANES])
    return jnp.concatenate([head, rolled[SUBLANES:]], axis=0)


def _rows_after(cur, next_first, j):
    n = cur.shape[0]
    rolled = pltpu.roll(cur, n - j, 0)
    tail = jnp.where(_iota((SUBLANES, cur.shape[1]), 0) >= SUBLANES - j, pltpu.roll(next_first, SUBLANES - j, 0),
                     rolled[n - SUBLANES:])
    return jnp.concatenate([rolled[:n - SUBLANES], tail], axis=0)


def _conv_pre(x_ref, w_ref, b_ref, r0, prev_last):
    cur = x_ref[pl.ds(r0, CONV_RB), :]
    shifts = [_rows_before(cur, prev_last, j) for j in range(1, CONV_K)]
    pre = b_ref[...] + w_ref[CONV_K - 1:CONV_K, :] * cur
    for j in range(1, CONV_K):
        pre = pre + w_ref[CONV_K - 1 - j:CONV_K - j, :] * shifts[j - 1]
    return pre, cur, shifts


def _conv_fwd(proj, w, b, name):
    s = proj.shape[0]

    def body(x_ref, w_ref, b_ref, o_ref):
        xv = x_ref[...]
        rows = _iota(xv.shape, 0)
        pre = b_ref[...] + w_ref[CONV_K - 1:CONV_K, :] * xv
        for j in range(1, CONV_K):
            pre = pre + w_ref[CONV_K - 1 - j:CONV_K - j, :] * jnp.where(rows >= j, pltpu.roll(xv, j, 0), 0.0)
        o_ref[...] = pre * _sigmoid(pre)

    return pl.pallas_call(
        body, name=name, grid=(CONV_DIM // CONV_TC,),
        in_specs=[pl.BlockSpec((s, CONV_TC), lambda j: (0, _XBC_BLK0 + j)), pl.BlockSpec((CONV_K, CONV_TC), lambda j: (0, j)),
                  pl.BlockSpec((1, CONV_TC), lambda j: (0, j))],
        out_specs=pl.BlockSpec((s, CONV_TC), lambda j: (0, j)),
        out_shape=jax.ShapeDtypeStruct((s, CONV_DIM), F32), compiler_params=_cp(VMEM_BIG))(proj, w, b)


def _conv_bwd(proj, w, b, dxbc, dproj, name):
    s = proj.shape[0]

    nb = s // CONV_RB

    def body(x_ref, w_ref, b_ref, d_ref, dproj_ref, dx_ref, dw_ref, db_ref, dpre_ref):
        def fold(v):
            out = v[:SUBLANES]
            for t in range(1, CONV_RB // SUBLANES):
                out = out + v[t * SUBLANES:(t + 1) * SUBLANES]
            return out

        def first(i, carry):
            prev_last, acc = carry
            r0 = pl.multiple_of(i * CONV_RB, CONV_RB)
            pre, cur, shifts = _conv_pre(x_ref, w_ref, b_ref, r0, prev_last)
            sig = _sigmoid(pre)
            dpre = d_ref[pl.ds(r0, CONV_RB), :] * (sig * (1.0 + pre * (1.0 - sig)))
            dpre_ref[pl.ds(r0, CONV_RB), :] = dpre
            taps = [cur] + shifts
            acc = tuple(a + fold(dpre * t) for a, t in zip(acc[:CONV_K], taps)) + (acc[CONV_K] + fold(dpre),)
            return cur[CONV_RB - SUBLANES:], acc

        zero8 = jnp.zeros((SUBLANES, CONV_TC), F32)
        _, acc = lax.fori_loop(0, nb, first, (zero8, (zero8,) * (CONV_K + 1)))
        for j in range(CONV_K):
            dw_ref[CONV_K - 1 - j:CONV_K - j, :] = jnp.sum(acc[j], axis=0, keepdims=True)
        db_ref[...] = jnp.sum(acc[CONV_K], axis=0, keepdims=True)

        def second(i, next_first):
            r0 = pl.multiple_of((nb - 1 - i) * CONV_RB, CONV_RB)
            cur = dpre_ref[pl.ds(r0, CONV_RB), :]
            dx = w_ref[CONV_K - 1:CONV_K, :] * cur
            for j in range(1, CONV_K):
                dx = dx + w_ref[CONV_K - 1 - j:CONV_K - j, :] * _rows_after(cur, next_first, j)
            dx_ref[pl.ds(r0, CONV_RB), :] = dx.astype(dx_ref.dtype)
            return cur[:SUBLANES]

        lax.fori_loop(0, nb, second, zero8)

    return pl.pallas_call(
        body, name=name, grid=(CONV_DIM // CONV_TC,),
        in_specs=[pl.BlockSpec((s, CONV_TC), lambda j: (0, _XBC_BLK0 + j)), pl.BlockSpec((CONV_K, CONV_TC), lambda j: (0, j)),
                  pl.BlockSpec((1, CONV_TC), lambda j: (0, j)), pl.BlockSpec((s, CONV_TC), lambda j: (0, j)),
                  pl.BlockSpec(memory_space=pl.ANY)],
        out_specs=[pl.BlockSpec((s, CONV_TC), lambda j: (0, _XBC_BLK0 + j)), pl.BlockSpec((CONV_K, CONV_TC), lambda j: (0, j)),
                   pl.BlockSpec((1, CONV_TC), lambda j: (0, j))],
        out_shape=[jax.ShapeDtypeStruct(dproj.shape, dproj.dtype), jax.ShapeDtypeStruct((CONV_K, CONV_DIM), F32),
                   jax.ShapeDtypeStruct((1, CONV_DIM), F32)], input_output_aliases={4: 0},
        scratch_shapes=[pltpu.VMEM((s, CONV_TC), F32)],
        compiler_params=_cp(VMEM_BIG))(proj, w, b, dxbc, dproj)


def _ssd_common(dtc_ref, br_ref, ar_ref, csb_ref, cst_ref, csf_ref, dtf_ref, expand):
    a_row = -jnp.exp(ar_ref[...])
    dt_c = _softplus(dtc_ref[...] + br_ref[...])
    tril = _iota((CHUNK, CHUNK), 0) >= _iota((CHUNK, CHUNK), 1)
    cs = _sel_dot(tril, dt_c * a_row)
    cst_ref[...] = cs.T
    e64 = (jnp.right_shift(_iota((HPAD, D_INNER), 1), 6) == _iota((HPAD, D_INNER), 0)).astype(jnp.bfloat16)
    if expand:
        e128 = jnp.right_shift(_iota((HPAD, SSM_HEADS * CHUNK), 1), 7) == _iota((HPAD, SSM_HEADS * CHUNK), 0)
        csb_ref[...] = _dot_sel(cs, e128)
        dtf_ref[...] = _dot_sel(dt_c, e64)
        csf_ref[...] = _dot_sel(cs, e64)
    dt_full = dtf_ref[...]
    cs_full = csf_ref[...]
    cs_last = csf_ref[CHUNK - 1:CHUNK, :]
    e_full = jnp.exp(cs_full)
    f_full = jnp.exp(cs_last - cs_full)
    gamma = jnp.exp(cs_last)
    return a_row, dt_c, cs, dt_full, e_full, f_full, gamma, e64


def _ssd_lambda(csb_ref, cst_ref, h, causal):
    diff = csb_ref[:, h * CHUNK:(h + 1) * CHUNK] - cst_ref[h:h + 1, :]
    return jnp.exp(jnp.where(causal, diff, -1e30))


_SSD_VEC_SPECS = lambda: [pl.BlockSpec((1, HPAD), lambda c: (0, 0)), pl.BlockSpec((1, HPAD), lambda c: (0, 0)),
                          pl.BlockSpec((1, D_INNER), lambda c: (0, 0))]


def _ssd_fwd(xbc, dtc, bias_row, alog_row, dfull, name):
    s = xbc.shape[0]
    nc = s // CHUNK

    def body(xbc_ref, dtc_ref, br_ref, ar_ref, df_ref, y_ref, st_ref, csb_ref, csf_ref, dtf_ref, ht_ref, cst_ref):
        @pl.when(pl.program_id(0) == 0)
        def _():
            ht_ref[...] = jnp.zeros_like(ht_ref)

        _, _, _, dt_full, e_full, f_full, gamma, _ = _ssd_common(
            dtc_ref, br_ref, ar_ref, csb_ref, cst_ref, csf_ref, dtf_ref, expand=True)
        x = xbc_ref[:, :D_INNER]
        xdt = x * dt_full
        st_ref[...] = ht_ref[...]
        causal = _iota((CHUNK, CHUNK), 0) >= _iota((CHUNK, CHUNK), 1)
        lo = _iota((CHUNK, CHUNK), 1) < SSM_P
        for g in range(SSM_GROUPS):
            gs = slice(g * SSM_GW, (g + 1) * SSM_GW)
            bg = xbc_ref[:, D_INNER + g * SSM_N:D_INNER + (g + 1) * SSM_N]
            cg = xbc_ref[:, D_INNER + SSM_GROUPS * SSM_N + g * SSM_N:D_INNER + SSM_GROUPS * SSM_N + (g + 1) * SSM_N]
            ht = ht_ref[:, gs]
            cb = _dot_nt(cg, bg)
            yoff = e_full[:, gs] * _dot(cg, ht)
            for jp in range(SSM_GW // CHUNK):
                j = g * (SSM_GW // CHUNK) + jp
                ps = slice(j * CHUNK, (j + 1) * CHUNK)
                x2 = xdt[:, ps]
                y0 = _dot(cb * _ssd_lambda(csb_ref, cst_ref, 2 * j, causal), x2)
                y1 = _dot(cb * _ssd_lambda(csb_ref, cst_ref, 2 * j + 1, causal), x2)
                y_ref[:, ps] = (jnp.where(lo, y0, y1) + yoff[:, jp * CHUNK:(jp + 1) * CHUNK]
                                + x[:, ps] * df_ref[:, ps])
            ht_ref[:, gs] = gamma[:, gs] * ht + _dot_tn(bg, xdt[:, gs] * f_full[:, gs])

    return pl.pallas_call(
        body, name=name, grid=(nc,),
        in_specs=[pl.BlockSpec((CHUNK, CONV_DIM), lambda c: (c, 0)), pl.BlockSpec((CHUNK, HPAD), lambda c: (c, 0))]
                 + _SSD_VEC_SPECS(),
        out_specs=[pl.BlockSpec((CHUNK, D_INNER), lambda c: (c, 0)), pl.BlockSpec((None, SSM_N, D_INNER), lambda c: (c, 0, 0)),
                   pl.BlockSpec((CHUNK, SSM_HEADS * CHUNK), lambda c: (c, 0)), pl.BlockSpec((CHUNK, D_INNER), lambda c: (c, 0)),
                   pl.BlockSpec((CHUNK, D_INNER), lambda c: (c, 0))],
        out_shape=[jax.ShapeDtypeStruct((s, D_INNER), F32), jax.ShapeDtypeStruct((nc, SSM_N, D_INNER), F32),
                   jax.ShapeDtypeStruct((s, SSM_HEADS * CHUNK), F32), jax.ShapeDtypeStruct((s, D_INNER), F32),
                   jax.ShapeDtypeStruct((s, D_INNER), F32)],
        scratch_shapes=[pltpu.VMEM((SSM_N, D_INNER), F32), pltpu.VMEM((HPAD, CHUNK), F32)],
        compiler_params=_cp(VMEM_BIG))(xbc, dtc, bias_row, alog_row, dfull)


def _ssd_bwd(xbc, dtc, bias_row, alog_row, dfull, dy, states, expansions, name):
    s = xbc.shape[0]
    nc = s // CHUNK
    rev = lambda c: nc - 1 - c

    def body(xbc_ref, dtc_ref, br_ref, ar_ref, df_ref, dy_ref, st_ref, csb_ref, csf_ref, dtf_ref,
             dxbc_ref, ddt_ref, dalog_ref, dd_ref, dbias_ref,
             dht_ref, cst_ref, ddf_ref, dxs_ref, dcsf_ref, dcsl_ref):
        step = pl.program_id(0)

        @pl.when(step == 0)
        def _():
            dht_ref[...] = jnp.zeros_like(dht_ref)
            ddf_ref[...] = jnp.zeros_like(ddf_ref)
            dalog_ref[...] = jnp.zeros_like(dalog_ref)
            dbias_ref[...] = jnp.zeros_like(dbias_ref)
            dd_ref[...] = jnp.zeros_like(dd_ref)

        a_row, dt_c, _, dt_full, e_full, f_full, gamma, e64 = _ssd_common(
            dtc_ref, br_ref, ar_ref, csb_ref, cst_ref, csf_ref, dtf_ref, expand=False)
        x = xbc_ref[:, :D_INNER]
        xdt = x * dt_full
        dy_all = dy_ref[...]
        ddf_ref[...] += jnp.broadcast_to(jnp.sum(dy_all * x, axis=0, keepdims=True), ddf_ref.shape)
        causal = _iota((CHUNK, CHUNK), 0) >= _iota((CHUNK, CHUNK), 1)
        lo = _iota((CHUNK, CHUNK), 1) < SSM_P
        head_lane = _iota((CHUNK, HPAD), 1)
        head_row = _iota((HPAD, CHUNK), 0)
        dcs_heads = jnp.zeros((CHUNK, HPAD), F32)
        dcs_cols = jnp.zeros((HPAD, CHUNK), F32)
        for g in range(SSM_GROUPS):
            gs = slice(g * SSM_GW, (g + 1) * SSM_GW)
            b0 = D_INNER + g * SSM_N
            c0 = D_INNER + SSM_GROUPS * SSM_N + g * SSM_N
            bg = xbc_ref[:, b0:b0 + SSM_N]
            cg = xbc_ref[:, c0:c0 + SSM_N]
            ht = st_ref[:, gs]
            dht = dht_ref[:, gs]
            dyg = dy_all[:, gs]
            eg, fg, gg = e_full[:, gs], f_full[:, gs], gamma[:, gs]
            z = _dot(cg, ht)
            dz = dyg * eg
            dcg = _dot_nt(dz, ht)
            dht_new = _dot_tn(cg, dz) + gg * dht
            xf = xdt[:, gs] * fg
            dxf = _dot(bg, dht)
            dbg = _dot_nt(xf, dht)
            dff = dxf * xf
            dcsf_ref[:, gs] = dyg * eg * z - dff
            dcsl_ref[:, gs] = jnp.broadcast_to(
                jnp.sum(dff, axis=0, keepdims=True) + jnp.sum(dht * ht, axis=0, keepdims=True) * gg, (8, SSM_GW))
            cb = _dot_nt(cg, bg)
            dcb = jnp.zeros((CHUNK, CHUNK), F32)
            for jp in range(SSM_GW // CHUNK):
                j = g * (SSM_GW // CHUNK) + jp
                ps = slice(j * CHUNK, (j + 1) * CHUNK)
                x2 = xdt[:, ps]
                dy2 = dy_all[:, ps]
                dxh = []
                for hh in range(2):
                    h = 2 * j + hh
                    lam = _ssd_lambda(csb_ref, cst_ref, h, causal)
                    mh = cb * lam
                    dyh = jnp.where(lo, dy2, 0.0) if hh == 0 else jnp.where(lo, 0.0, dy2)
                    dm = _dot_nt(dyh, x2)
                    dcb = dcb + dm * lam
                    gm = dm * mh
                    dcs_heads = dcs_heads + jnp.where(head_lane == h, jnp.sum(gm, axis=1, keepdims=True), 0.0)
                    dcs_cols = dcs_cols + jnp.where(head_row == h, jnp.sum(gm, axis=0, keepdims=True), 0.0)
                    dxh.append(_dot_tn(mh, dy2))
                dxs_ref[:, ps] = jnp.where(lo, dxh[0], dxh[1]) + dxf[:, jp * CHUNK:(jp + 1) * CHUNK] * fg[:, jp * CHUNK:(jp + 1) * CHUNK]
            dxbc_ref[:, b0:b0 + SSM_N] = (dbg + _dot_tn(dcb, cg)).astype(dxbc_ref.dtype)
            dxbc_ref[:, c0:c0 + SSM_N] = (dcg + _dot(dcb, bg)).astype(dxbc_ref.dtype)
            dht_ref[:, gs] = dht_new
        dxs = dxs_ref[...]
        dcs_heads = dcs_heads - dcs_cols.T + _dot_sel(dcsf_ref[...], e64, ((1,), (1,)))
        dcs_last = _dot_sel(dcsl_ref[...], e64, ((1,), (1,)))
        dcs_heads = dcs_heads + jnp.where(_iota((CHUNK, HPAD), 0) == CHUNK - 1, dcs_last[0:1, :], 0.0)
        triu = _iota((CHUNK, CHUNK), 0) <= _iota((CHUNK, CHUNK), 1)
        dda = _sel_dot(triu, dcs_heads)
        ddt = dda * a_row + _dot_sel(dxs * x, e64, ((1,), (1,)))
        dxbc_ref[:, :D_INNER] = (dxs * dt_full + dy_all * df_ref[...]).astype(dxbc_ref.dtype)
        dalog_ref[...] += jnp.sum(dda * dt_c, axis=0, keepdims=True) * a_row
        ddt_raw = ddt * _sigmoid(dtc_ref[...] + br_ref[...])
        ddt_ref[...] = ddt_raw.astype(ddt_ref.dtype)
        dbias_ref[...] += jnp.sum(ddt_raw, axis=0, keepdims=True)

        @pl.when(step == nc - 1)
        def _():
            dd_ref[...] = _dot_sel(ddf_ref[...], e64, ((1,), (1,)))[0:1, :]

    vec = pl.BlockSpec((1, HPAD), lambda c: (0, 0))
    return pl.pallas_call(
        body, name=name, grid=(nc,),
        in_specs=[pl.BlockSpec((CHUNK, CONV_DIM), lambda c: (rev(c), 0)), pl.BlockSpec((CHUNK, HPAD), lambda c: (rev(c), 0))]
                 + _SSD_VEC_SPECS()
                 + [pl.BlockSpec((CHUNK, D_INNER), lambda c: (rev(c), 0)),
                    pl.BlockSpec((None, SSM_N, D_INNER), lambda c: (rev(c), 0, 0)),
                    pl.BlockSpec((CHUNK, SSM_HEADS * CHUNK), lambda c: (rev(c), 0)),
                    pl.BlockSpec((CHUNK, D_INNER), lambda c: (rev(c), 0)), pl.BlockSpec((CHUNK, D_INNER), lambda c: (rev(c), 0))],
        out_specs=[pl.BlockSpec((CHUNK, CONV_DIM), lambda c: (rev(c), 0)), pl.BlockSpec((CHUNK, HPAD), lambda c: (rev(c), 0)),
                   vec, vec, vec],
        out_shape=[jax.ShapeDtypeStruct((s, CONV_DIM), F32), jax.ShapeDtypeStruct((s, HPAD), _ACT),
                   jax.ShapeDtypeStruct((1, HPAD), F32), jax.ShapeDtypeStruct((1, HPAD), F32),
                   jax.ShapeDtypeStruct((1, HPAD), F32)],
        scratch_shapes=[pltpu.VMEM((SSM_N, D_INNER), F32), pltpu.VMEM((HPAD, CHUNK), F32),
                        pltpu.VMEM((8, D_INNER), F32), pltpu.VMEM((CHUNK, D_INNER), F32),
                        pltpu.VMEM((CHUNK, D_INNER), F32), pltpu.VMEM((8, D_INNER), F32)],
        compiler_params=_cp(VMEM_BIG))(xbc, dtc, bias_row, alog_row, dfull, dy, states, *expansions)


def _gate_fwd(y, proj, gn, name, tm=512):
    s = y.shape[0]
    tm = min(tm, s)

    def body(y_ref, z_ref, gn_ref, o_ref):
        for g in range(SSM_GROUPS):
            gs = slice(g * SSM_GW, (g + 1) * SSM_GW)
            z = z_ref[:, gs]
            t = y_ref[:, gs] * (z * _sigmoid(z))
            r = lax.rsqrt(jnp.mean(t * t, axis=-1, keepdims=True) + EPS)
            o_ref[:, gs] = (t * r * gn_ref[:, gs]).astype(o_ref.dtype)

    row = pl.BlockSpec((tm, D_INNER), lambda i: (i, 0))
    return pl.pallas_call(
        body, name=name, grid=(s // tm,), in_specs=[row, row, pl.BlockSpec((1, D_INNER), lambda i: (0, 0))],
        out_specs=row, out_shape=jax.ShapeDtypeStruct((s, D_INNER + X_WIDTH), _ACT),
        compiler_params=_cp(VMEM_BIG))(y, proj, gn)


def _gate_bwd(y, proj, gn, dcat, name, tm=512):
    s = y.shape[0]
    tm = min(tm, s)

    def body(y_ref, z_ref, gn_ref, dm_ref, dy_ref, dz_ref, dgn_ref):
        @pl.when(pl.program_id(0) == 0)
        def _():
            dgn_ref[...] = jnp.zeros_like(dgn_ref)

        for g in range(SSM_GROUPS):
            gs = slice(g * SSM_GW, (g + 1) * SSM_GW)
            z = z_ref[:, gs]
            yv = y_ref[:, gs]
            sig = _sigmoid(z)
            sz = z * sig
            t = yv * sz
            r = lax.rsqrt(jnp.mean(t * t, axis=-1, keepdims=True) + EPS)
            th = t * r
            dm = dm_ref[:, gs].astype(F32)
            dmg = dm * gn_ref[:, gs]
            dt_ = r * (dmg - th * jnp.mean(dmg * th, axis=-1, keepdims=True))
            dgn_ref[:, gs] += jnp.sum(dm * th, axis=0, keepdims=True)
            dy_ref[:, gs] = dt_ * sz
            dz_ref[:, gs] = (dt_ * yv * (sig * (1.0 + z * (1.0 - sig)))).astype(dz_ref.dtype)

    row = pl.BlockSpec((tm, D_INNER), lambda i: (i, 0))
    vec = pl.BlockSpec((1, D_INNER), lambda i: (0, 0))
    return pl.pallas_call(
        body, name=name, grid=(s // tm,), in_specs=[row, row, vec, row], out_specs=[row, row, vec],
        out_shape=[jax.ShapeDtypeStruct((s, D_INNER), F32), jax.ShapeDtypeStruct((s, 6 * D_MODEL), _ACT),
                   jax.ShapeDtypeStruct((1, D_INNER), F32)], compiler_params=_cp(VMEM_BIG))(y, proj, gn, dcat)


def _block_of(kind, width):
    if kind == "col":
        return lambda ref, j: ref.at[:, :, pl.ds(pl.multiple_of(j * width, 128), width)]
    if kind == "row":
        return lambda ref, j: ref.at[:, pl.ds(pl.multiple_of(j * width, 8), width), :]
    return lambda ref, j: ref.at[j]


def _coords():
    return lax.axis_index("x"), lax.axis_index("y"), lax.axis_index("c")


def _rel_chip(x, y, k):
    return (1 - x if k & 1 else x), (1 - y if k & 2 else y)


def _all_gather_body(ins, outs, send_sems, recv_sems, local_sems, blocks):
    n = len(ins)
    x, y, c = _coords()
    sibling = (x, y, 1 - c)
    via = (x + (1 - c) * (1 - 2 * x), y + c * (1 - 2 * y))
    onto = (x + c * (1 - 2 * x), y + (1 - c) * (1 - 2 * y))

    def copy(t, k, chip, core, to, src=None):
        dst = blocks[t](outs[t], 4 * chip[0] + 2 * chip[1] + core)
        return pltpu.make_async_remote_copy(
            src_ref=dst if src is None else src, dst_ref=dst, send_sem=send_sems.at[t, k],
            recv_sem=recv_sems.at[t, k], device_id=to, device_id_type=MESH)

    started = []
    for t in range(n):
        mine = pltpu.make_async_copy(ins[t], blocks[t](outs[t], 4 * x + 2 * y + c), local_sems.at[t])
        mine.start()
        started.append(mine)
    sends = []
    for t in range(n):
        for k in range(3):
            px, py = _rel_chip(x, y, k)
            cp = copy(t, k, (x, y), c, (px, py, 1 - c if k == 0 else c), src=ins[t])
            cp.start()
            sends.append(cp)
    for t in range(n):
        for k in (1, 2):
            chip = _rel_chip(x, y, k)
            copy(t, k, chip, c, sibling).wait_recv()
            fwd = copy(t, 3 + k, chip, c, sibling)
            fwd.start()
            sends.append(fwd)
        hop = copy(t, 3, via, c, (*onto, c))
        hop.start()
        sends.append(hop)
    for t in range(n):
        diagonal = _rel_chip(x, y, 3)
        copy(t, 3, diagonal, c, sibling).wait_recv()
        fwd = copy(t, 6, diagonal, c, sibling)
        fwd.start()
        sends.append(fwd)
    for t in range(n):
        copy(t, 0, (x, y), 1 - c, sibling).wait_recv()
        for k in range(1, 4):
            copy(t, 3 + k, _rel_chip(x, y, k), 1 - c, sibling).wait_recv()
    for cp in sends:
        cp.wait_send()
    for mine in started:
        mine.wait()


def _handshake(peers):
    barrier = pltpu.get_barrier_semaphore()
    for peer in peers:
        pl.semaphore_signal(barrier, inc=1, device_id=peer, device_id_type=MESH)
    pl.semaphore_wait(barrier, len(peers))


def _gather_peers():
    x, y, c = _coords()
    return [(x, y, 1 - c)] + [(*_rel_chip(x, y, k), c) for k in (1, 2)]


SEQ_ID_GATHER, SEQ_ID_SIBLING, SEQ_ID_CHIPS = 1, 2, 3


def _sequencer_call(body, peers, operands, out_types, sems, name, collective_id, after=()):
    n_in, n_out, n_after = len(operands), len(out_types), len(after)

    def launch(*refs):
        _handshake(peers())
        body(refs[:n_in], refs[n_in + n_after:n_in + n_after + n_out], *refs[n_in + n_after + n_out:])

    return pl.kernel(
        launch, name=name, out_type=out_types, mesh=plsc.ScalarSubcoreMesh(axis_name="seq", num_cores=1),
        scratch_types=sems, compiler_params=pltpu.CompilerParams(collective_id=collective_id))(*operands, *after)


def _all_gather_seq(shards, layouts, name, after=()):
    n = len(shards)
    blocks = [_block_of(kind, width) for kind, width, _ in layouts]
    return _sequencer_call(
        lambda ins, outs, *sems: _all_gather_body(ins, outs, *sems, blocks), _gather_peers, shards,
        [jax.ShapeDtypeStruct(shape, sh.dtype) for sh, (_, _, shape) in zip(shards, layouts)],
        [pltpu.SemaphoreType.DMA((n, 7)), pltpu.SemaphoreType.DMA((n, 7)), pltpu.SemaphoreType.DMA((n,))],
        name, SEQ_ID_GATHER, after)


def _tie(small, after):
    return lax.optimization_barrier((small, *after))[0]


def _rs_to_sibling(grads, layouts, name, after=()):
    n = len(grads)
    blocks = [_block_of(kind, width) for kind, width, _ in layouts]

    def body(ins, outs, send_sems, recv_sems):
        x, y, c = _coords()
        sibling = (x, y, 1 - c)
        cps = []
        for t in range(n):
            for k in range(4):
                px, py = _rel_chip(x, y, k)
                cp = pltpu.make_async_remote_copy(
                    src_ref=blocks[t](ins[t], 4 * px + 2 * py + (1 - c)), dst_ref=outs[t].at[k],
                    send_sem=send_sems.at[t, k], recv_sem=recv_sems.at[t, k], device_id=sibling, device_id_type=MESH)
                cp.start()
                cps.append(cp)
        for cp in cps:
            cp.wait_recv()
        for cp in cps:
            cp.wait_send()

    def sibling_only():
        x, y, c = _coords()
        return [(x, y, 1 - c)]

    return _sequencer_call(
        body, sibling_only, grads,
        [jax.ShapeDtypeStruct((4,) + shape, g.dtype) for g, (_, _, shape) in zip(grads, layouts)],
        [pltpu.SemaphoreType.DMA((n, 4)), pltpu.SemaphoreType.DMA((n, 4))], name, SEQ_ID_SIBLING, after)


def _rs_chip_sum(grad, recv, layout, xyc, name):
    kind, width, shape = layout
    r, ccols = shape

    def src_index(step, xyc_ref):
        k = step + 1
        px = jnp.where(k % 2 == 1, 1 - xyc_ref[0], xyc_ref[0])
        py = jnp.where(k // 2 == 1, 1 - xyc_ref[1], xyc_ref[1])
        return 4 * px + 2 * py + xyc_ref[2]

    if kind == "col":
        g_spec = pl.BlockSpec((r, ccols), lambda k, s_: (0, src_index(k, s_)))
    elif kind == "row":
        g_spec = pl.BlockSpec((r, ccols), lambda k, s_: (src_index(k, s_), 0))
    else:
        g_spec = pl.BlockSpec((None, r, ccols), lambda k, s_: (src_index(k, s_), 0, 0))

    def body(xyc_ref, g_ref, r_ref, o_ref):
        o_ref[...] = (g_ref[...].astype(F32) + r_ref[...].astype(F32)).astype(o_ref.dtype)

    slot = pl.BlockSpec((None, r, ccols), lambda k, s_: (k + 1, 0, 0))
    return pl.pallas_call(
        body, name=name,
        grid_spec=pltpu.PrefetchScalarGridSpec(num_scalar_prefetch=1, grid=(3,), in_specs=[g_spec, slot], out_specs=slot),
        out_shape=jax.ShapeDtypeStruct((4, r, ccols), grad.dtype), compiler_params=_cp(VMEM_BIG))(xyc, grad, recv)


def _rs_across_chips(parts, name):
    n = len(parts)

    def body(ins, outs, send_sems, recv_sems):
        x, y, c = _coords()
        cps = []
        for t in range(n):
            for k in range(1, 4):
                px, py = _rel_chip(x, y, k)
                cp = pltpu.make_async_remote_copy(
                    src_ref=ins[t].at[k], dst_ref=outs[t].at[k - 1], send_sem=send_sems.at[t, k - 1],
                    recv_sem=recv_sems.at[t, k - 1], device_id=(px, py, c), device_id_type=MESH)
                cp.start()
                cps.append(cp)
        for cp in cps:
            cp.wait_recv()
        for cp in cps:
            cp.wait_send()

    def other_chips():
        x, y, c = _coords()
        return [(*_rel_chip(x, y, k), c) for k in range(1, 4)]

    return _sequencer_call(
        body, other_chips, parts, [jax.ShapeDtypeStruct((3,) + p.shape[1:], p.dtype) for p in parts],
        [pltpu.SemaphoreType.DMA((n, 3)), pltpu.SemaphoreType.DMA((n, 3))], name, SEQ_ID_CHIPS)


def _adamw_math(w, g, m, v):
    m = ADAM_B1 * m + (1.0 - ADAM_B1) * g
    v = ADAM_B2 * v + (1.0 - ADAM_B2) * jnp.square(g)
    m_hat = m / (1.0 - ADAM_B1 ** ADAM_STEP)
    v_hat = v / (1.0 - ADAM_B2 ** ADAM_STEP)
    delta = -ADAM_LR * (m_hat / (jnp.sqrt(v_hat) + ADAM_EPS) + ADAM_WD * w)
    return delta, m, v


def _row_tile(rows, cap):
    best = None
    for cand in range(8, min(rows, cap) + 1, 8):
        if rows % cand == 0:
            best = cand
    assert best is not None, rows
    return best


def _adamw(w, m, v, own, parts, me, name, layer, prev=None, tr=256):
    r, ccols = w.shape[-2:]
    npart = len(parts)
    if r % 8 == 0:
        tr, tc = _row_tile(r, tr), ccols
        steps, at = r // tr, (lambda i: (i, 0))
    else:
        tr, tc = r, 256
        assert ccols % tc == 0
        steps, at = ccols // tc, (lambda i: (0, i))

    def spec(lead):
        return pl.BlockSpec((None, tr, tc), lambda i, me_ref: (lead,) + at(i))

    grad, kind = own
    if kind == "col":
        own_spec = pl.BlockSpec((tr, tc), lambda i, me_ref: (at(i)[0], me_ref[0]))
    elif kind == "row":
        own_spec = pl.BlockSpec((tr, tc), lambda i, me_ref: (me_ref[0] * (r // tr) + at(i)[0], 0))
    else:
        own_spec = pl.BlockSpec((None, tr, tc), lambda i, me_ref: (me_ref[0],) + at(i))

    def body(me_ref, *refs):
        w_ref, m_ref, v_ref = refs[:3]
        p_refs = refs[3:4 + npart]
        outs = refs[len(refs) - 4:]
        g = p_refs[0][...].astype(F32)
        for p_ref in p_refs[1:]:
            g = g + p_ref[...].astype(F32)
        delta, mn, vn = _adamw_math(w_ref[...], g, m_ref[...], v_ref[...])
        outs[0][...] = g
        outs[1][...] = delta
        outs[2][...] = mn
        outs[3][...] = vn

    operands = [w, m, v, grad] + [p for p, _ in parts]
    in_specs = [spec(layer)] * 3 + [own_spec] + [spec(lead) for _, lead in parts]
    aliases = {}
    if prev is not None:
        for i, p in enumerate(prev):
            aliases[1 + len(operands)] = i
            operands.append(p)
            in_specs.append(pl.BlockSpec(memory_space=pl.ANY))
    return pl.pallas_call(
        body, name=name,
        grid_spec=pltpu.PrefetchScalarGridSpec(num_scalar_prefetch=1, grid=(steps,), in_specs=in_specs,
                                               out_specs=[spec(layer)] * 4),
        out_shape=[jax.ShapeDtypeStruct(w.shape, F32)] * 4, input_output_aliases=aliases,
        compiler_params=_cp(VMEM_BIG))(me, *operands)


def _small_update(gathered, params, loss_all, me, name):
    n = len(gathered)
    shapes = [w.shape for w, _, _ in params]

    def body(me_ref, *refs):
        g_refs, loss_ref = refs[:n], refs[n]
        p_refs = refs[n + 1:n + 1 + 3 * n]
        o_refs = refs[n + 1 + 3 * n:]
        for i in range(n):
            r, c = shapes[i]
            if gathered[i].shape[2] == c:
                parts = [g_refs[i][j] for j in range(N_DEV)]
            else:
                off = pl.multiple_of(me_ref[0] * c, 128)
                parts = [g_refs[i][j, :, pl.ds(off, c)] for j in range(N_DEV)]
            g = functools.reduce(lambda a, b: a + b, parts)
            delta, mn, vn = _adamw_math(p_refs[3 * i][...], g, p_refs[3 * i + 1][...], p_refs[3 * i + 2][...])
            for k, val in enumerate((g, delta, mn, vn)):
                o_refs[4 * i + k][...] = val
        o_refs[4 * n][...] = functools.reduce(lambda a, b: a + b, [loss_ref[j] for j in range(N_DEV)])

    vmem = pl.BlockSpec(memory_space=pltpu.VMEM)
    flat_params = [a for p in params for a in p]
    outs = pl.pallas_call(
        body, name=name, in_specs=[pl.BlockSpec(memory_space=pltpu.SMEM)] + [vmem] * (n + 1 + 3 * n),
        out_specs=[vmem] * (4 * n + 1),
        out_shape=[jax.ShapeDtypeStruct(shp, F32) for shp in shapes for _ in range(4)] + [jax.ShapeDtypeStruct((1, 128), F32)],
        compiler_params=_cp(VMEM_BIG))(me, *gathered, loss_all, *flat_params)
    return [tuple(outs[4 * i:4 * i + 4]) for i in range(n)], outs[4 * n]


def _pack(arrays):
    pieces, layout, off = [], [], 0
    for a in arrays:
        n = a.size
        padded = -(-n // 1024) * 1024
        flat = a.reshape(-1).astype(F32)
        if padded != n:
            flat = jnp.pad(flat, (0, padded - n))
        pieces.append(flat.reshape(padded // 128, 128))
        layout.append((off, n, a.shape))
        off += padded // 128
    return jnp.concatenate(pieces, axis=0), layout


def kernel(x, mem, norm_mix, norm_ffn, mem_norm, w_kv, w_out, w_ffn1, w_ffn2, a_in, a_ln_g, a_ln_b, a_ws, a_bs, b_in, b_conv_w, b_conv_b, b_dt_bias, b_a_log, b_d, b_gnorm, final_norm, loss_target, m_norm_mix, m_norm_ffn, m_mem_norm, m_w_kv, m_w_out, m_w_ffn1, m_w_ffn2, m_a_in, m_a_ln_g, m_a_ln_b, m_a_ws, m_a_bs, m_b_in, m_b_conv_w, m_b_conv_b, m_b_dt_bias, m_b_a_log, m_b_d, m_b_gnorm, m_final_norm, v_norm_mix, v_norm_ffn, v_mem_norm, v_w_kv, v_w_out, v_w_ffn1, v_w_ffn2, v_a_in, v_a_ln_g, v_a_ln_b, v_a_ws, v_a_bs, v_b_in, v_b_conv_w, v_b_conv_b, v_b_dt_bias, v_b_a_log, v_b_d, v_b_gnorm, v_final_norm):
    s = x.shape[1]
    xs = x.reshape(s, D_MODEL)
    mems = mem.reshape(N_MEM, D_MODEL)
    target = loss_target.reshape(s, D_MODEL)
    ax, ay, ac = lax.axis_index("x"), lax.axis_index("y"), lax.axis_index("c")
    me = 4 * ax + 2 * ay + ac
    xyc = jnp.stack([ax, ay, ac]).astype(jnp.int32)
    me1 = me.astype(jnp.int32).reshape(1)

    b_cols = b_in.shape[2]
    act = lambda a: a.astype(_ACT)
    lay_f1, lay_f2 = ("col", 512, (1, D_MODEL, D_FF)), ("row", 512, (1, D_FF, D_MODEL))
    lay_out, lay_kv = ("row", 384, (1, 3 * D_MODEL, D_MODEL)), ("col", 256, (1, D_MODEL, 2 * X_WIDTH))
    small_w_pack = _pack([b_conv_w[0], b_conv_b[0], b_gnorm[0]])[0]
    (WA,) = _all_gather_seq([act(a_in)], [("col", 640, (1, D_MODEL, 5 * D_MODEL))], "ag_proj_a")
    wo0, wkv0 = _all_gather_seq([act(w_out[0:1]), act(w_kv[0:1])], [lay_out, lay_kv], "ag_out0")
    w1_0, w2_0 = _all_gather_seq([act(w_ffn1[0:1]), act(w_ffn2[0:1])], [lay_f1, lay_f2], "ag_ffn0")
    a0 = _rms_fwd(xs, norm_mix[0].reshape(1, -1), "mix_norm0")
    tr_b = lambda a: jnp.swapaxes(a, 1, 2)
    wbt_blk, small_w = _all_gather_seq(
        [act(tr_b(b_in)[0]), small_w_pack],
        [("blk", 0, (N_DEV, b_cols, D_MODEL)), ("blk", 0, (N_DEV, 32, 128))], "ag_proj_b", after=[a0])
    wo1, wkv1 = _all_gather_seq([act(w_out[1:2]), act(w_kv[1:2])], [lay_out, lay_kv], "ag_out1", after=[a0])
    w1_1, w2_1 = _all_gather_seq([act(w_ffn1[1:2]), act(w_ffn2[1:2])], [lay_f1, lay_f2], "ag_ffn1", after=[a0])
    W1, W2, WO, WKV = [w1_0, w1_1], [w2_0, w2_1], [wo0, wo1], [wkv0, wkv1]
    dt0 = D_INNER + CONV_DIM

    row = lambda a: a.reshape(1, -1)
    nmix = [row(norm_mix[0]), row(norm_mix[1])]
    nffn = [row(norm_ffn[0]), row(norm_ffn[1])]
    nmem = [row(mem_norm[0]), row(mem_norm[1])]
    fin = row(final_norm)
    lng, lnb = a_ln_g.reshape(1, D_INNER), a_ln_b.reshape(1, D_INNER)
    ws = a_ws[0]
    bs3 = a_bs[0].reshape(A_GROUPS, CHUNK, 1)
    pad_h = lambda a: jnp.pad(a.reshape(-1), (0, HPAD - SSM_HEADS))
    bias_row = pad_h(b_dt_bias).reshape(1, HPAD)
    alog_row = pad_h(b_a_log).reshape(1, HPAD)
    dfull = jnp.repeat(b_d.reshape(-1), SSM_P).reshape(1, D_INNER)

    kvs, mns = [None, None], [None, None]

    def mem_kv(i, after=None):
        gain = nmem[i] if after is None else _tie(nmem[i], after)
        mns[i] = _rms_fwd(mems, gain, f"mem_norm{i}")
        kvs[i] = _mm(mns[i], WKV[i], m=N_MEM, n=2 * X_WIDTH, k=D_MODEL, b_at=(0, 0, 0), out_dtype=_ACT, name=f"kv{i}")

    def ffn_fwd(h, i, after=()):
        f = _rms_fwd(h, nffn[i], f"ffn_norm{i}")
        p = _mm(f, W1[i], m=s, n=D_FF, k=D_MODEL, b_at=(0, 0, 0), out_dtype=_ACT, after=after, name=f"ffn_up{i}")
        hn = _mm(p, W2[i], m=s, n=D_MODEL, k=D_FF, b_at=(0, 0, 0), a_pro="relu2", add=h, name=f"ffn_down{i}")
        return f, p, hn

    def out_proj(h, cat, i):
        return _mm(cat, WO[i], m=s, n=D_MODEL, k=3 * D_MODEL, b_at=(0, 0, 0), add=h, name=f"out_proj{i}")

    proj_a = _mm(a0, WA, m=s, n=5 * D_MODEL, k=D_MODEL, b_at=(0, 0, 0), name="proj_a")
    mem_kv(0, after=[proj_a])
    cat_a = _gmlp_fwd(proj_a, lng, lnb, ws, bs3, "gmlp_fwd")
    cat_a = _attn_fwd(proj_a, 4, kvs[0], cat_a, "attn_fwd0")
    h1 = out_proj(xs, cat_a, 0)

    wbt_blk, small_w, _ = lax.optimization_barrier((wbt_blk, small_w, h1))
    jd, lo = divmod(dt0, b_cols)
    assert lo + SSM_HEADS <= b_cols
    wbt_full = wbt_blk.reshape(N_DEV * b_cols, D_MODEL)
    WBT = jnp.concatenate([wbt_full[:dt0], wbt_full[dt0 + SSM_HEADS:]], axis=0)
    WBDT = jnp.pad(wbt_full[dt0:dt0 + SSM_HEADS], ((0, HPAD - SSM_HEADS), (0, 0)))
    cw_sh, cb_sh, gn_sh = 4 * 384, 384, 256
    sw = small_w.reshape(N_DEV, 32 * 128)
    conv_w = jnp.transpose(sw[:, :cw_sh].reshape(N_DEV, CONV_K, 384), (1, 0, 2)).reshape(CONV_K, CONV_DIM)
    conv_b = sw[:, 2048:2048 + cb_sh].reshape(1, CONV_DIM)
    gnorm = sw[:, 3072:3072 + gn_sh].reshape(1, D_INNER)

    f0, p0, h2 = ffn_fwd(h1, 0, after=[WBT, WBDT])
    a1 = _rms_fwd(h2, nmix[1], "mix_norm1")
    proj_b = _mm(a1, WBT, m=s, n=6 * D_MODEL, k=D_MODEL, tb=True, name="proj_b")
    dt_raw = _mm(a1, WBDT, m=s, n=HPAD, k=D_MODEL, tb=True, name="proj_dt")
    xbc = _conv_fwd(proj_b, conv_w, conv_b, "conv_fwd")
    y_ssd, states, *ssd_expansions = _ssd_fwd(xbc, dt_raw, bias_row, alog_row, dfull, "ssd_fwd")
    cat_b = _gate_fwd(y_ssd, proj_b, gnorm, "gate_fwd")
    mem_kv(1, after=[cat_b])
    cat_b = _attn_fwd(proj_b, 5, kvs[1], cat_b, "attn_fwd1")
    h3 = out_proj(h2, cat_b, 1)
    f1, p1, h4 = ffn_fwd(h3, 1)

    loss_part, dh, dh_act, d_fin = _loss_head(h4, fin, target, "loss_head")

    g_f1, g_f2, g_out, g_kv = [None, None], [None, None], [None, None], [None, None]
    d_nffn, d_nmix, d_nmem = [None, None], [None, None], [None, None]

    def ffn_bwd(dh, dh_act, h_in, f, p, i, after=(), after_last=()):
        dp = _mm(dh_act, W2[i], m=s, n=D_FF, k=D_MODEL, tb=True, b_at=(0, 0, 0), epi_p=p, out_dtype=_ACT, name=f"ffn_down_dx{i}")
        g_f2[i] = _mm(p, dh_act, m=D_FF, n=D_MODEL, k=s, ta=True, a_pro="relu2", out_dtype=_ACT, name=f"ffn_down_dw{i}")
        g_f1[i] = _mm(f, dp, m=D_MODEL, n=D_FF, k=s, ta=True, out_dtype=_ACT, name=f"ffn_up_dw{i}")
        df = _mm(dp, W1[i], m=s, n=D_MODEL, k=D_FF, tb=True, b_at=(0, 0, 0), after=after, name=f"ffn_up_dx{i}")
        gain = _tie(nffn[i], after_last) if after_last else nffn[i]
        dh_in, dh_in_act, d_nffn[i] = _rms_bwd(h_in, gain, df, dh, f"ffn_norm_bwd{i}")
        return dh_in, dh_in_act

    def out_bwd(dh_act, cat, i):
        dcat = _mm(dh_act, WO[i], m=s, n=3 * D_MODEL, k=D_MODEL, tb=True, b_at=(0, 0, 0), out_dtype=_ACT, name=f"out_dx{i}")
        g_out[i] = _mm(cat, dh_act, m=3 * D_MODEL, n=D_MODEL, k=s, ta=True, out_dtype=_ACT, name=f"out_dw{i}")
        return dcat

    def mem_bwd(dkv, i):
        g_kv[i] = _mm(mns[i], dkv, m=D_MODEL, n=2 * X_WIDTH, k=N_MEM, ta=True, out_dtype=_ACT, name=f"kv_dw{i}")
        dmn = _mm(dkv, WKV[i], m=N_MEM, n=D_MODEL, k=2 * X_WIDTH, tb=True, b_at=(0, 0, 0), name=f"kv_dx{i}")
        _, _, d_nmem[i] = _rms_bwd(mems, nmem[i], dmn, None, f"mem_norm_bwd{i}")

    lay_g = {"f1": ("col", 512, (D_MODEL, 512)), "f2": ("row", 512, (512, D_MODEL)), "out": ("row", 384, (384, D_MODEL)),
             "kv": ("col", 256, (D_MODEL, 256)), "a": ("col", 640, (D_MODEL, 640)), "b": ("blk", 0, (b_cols, D_MODEL))}
    reduced = {}

    def reduce_scatter(group, tag, after=(), sums_after=()):
        grads3, lays3 = [], []
        for fam, _, g in group:
            kind, width, shape = lay_g[fam]
            grads3.append(g if kind == "blk" else g.reshape((1,) + g.shape))
            lays3.append((kind, width, shape if kind == "blk" else (1,) + shape))
        recv1 = _rs_to_sibling(grads3, lays3, f"rs_sibling_{tag}", after)
        if sums_after:
            recv1 = lax.optimization_barrier((tuple(recv1), tuple(sums_after)))[0]
        recv1 = [recv1[t].reshape((4,) + lay_g[fam][2]) for t, (fam, _, _) in enumerate(group)]
        parts = [_rs_chip_sum(g, r1, lay_g[fam], xyc, f"rs_chip_sum_{fam}{i}") for r1, (fam, i, g) in zip(recv1, group)]
        recv2 = _rs_across_chips(parts, f"rs_chips_{tag}")
        for (fam, i, g), r1, r2 in zip(group, recv1, recv2):
            reduced[fam, i] = (g, r1, r2)
        return parts, recv2

    dh3, dh3_act = ffn_bwd(dh, dh_act, h3, f1, p1, 1)
    dcat_b = out_bwd(dh3_act, cat_b, 1)
    sums, got_ffn1 = reduce_scatter([("f1", 1, g_f1[1]), ("f2", 1, g_f2[1]), ("out", 1, g_out[1])], "ffn1", sums_after=[dcat_b])
    dy_ssd, dproj_b, d_gnorm = _gate_bwd(y_ssd, proj_b, gnorm, dcat_b, "gate_bwd")
    dproj_b, dkv_b = _attn_bwd(proj_b, 5, kvs[1], dcat_b, dproj_b, "attn_bwd1")
    mem_bwd(dkv_b, 1)
    dxbc, ddt_raw, d_alog, d_dskip, d_dtbias = _ssd_bwd(
        xbc, dt_raw, _tie(bias_row, sums), alog_row, dfull, dy_ssd, states, ssd_expansions, "ssd_bwd")
    dproj_b, d_convw, d_convb = _conv_bwd(proj_b, conv_w, _tie(conv_b, got_ffn1), dxbc, dproj_b, "conv_bwd")
    gb = _mm(dproj_b, a1, m=6 * D_MODEL, n=D_MODEL, k=s, ta=True, out_dtype=_ACT, name="proj_b_dw")
    gb_dt = _mm(ddt_raw, a1, m=HPAD, n=D_MODEL, k=s, ta=True, out_dtype=_ACT, name="proj_b_dw_dt")
    blocks_b = [gb[j * b_cols:(j + 1) * b_cols] for j in range(jd)]
    blocks_b.append(jnp.concatenate([gb[jd * b_cols:dt0], gb_dt[:SSM_HEADS], gb[dt0:(jd + 1) * b_cols - SSM_HEADS]], axis=0))
    blocks_b += [gb[j * b_cols - SSM_HEADS:(j + 1) * b_cols - SSM_HEADS] for j in range(jd + 1, N_DEV)]
    gb_blk = jnp.stack(blocks_b)
    da1 = _mm(dproj_b, WBT, m=s, n=D_MODEL, k=6 * D_MODEL, name="proj_b_dx")
    sums, got_mix1 = reduce_scatter([("kv", 1, g_kv[1]), ("b", 0, gb_blk)], "mix1", sums_after=[da1])
    da1 = _mm(ddt_raw, WBDT, m=s, n=D_MODEL, k=HPAD, add=da1, name="proj_b_dx_dt")
    dh2, dh2_act, d_nmix[1] = _rms_bwd(h2, _tie(nmix[1], sums), da1, dh3, "mix_norm_bwd1")

    dh1, dh1_act = ffn_bwd(dh2, dh2_act, h1, f0, p0, 0, after=got_ffn1, after_last=got_mix1)
    dcat_a = out_bwd(dh1_act, cat_a, 0)
    sums, got_ffn0 = reduce_scatter([("f1", 0, g_f1[0]), ("f2", 0, g_f2[0]), ("out", 0, g_out[0])], "ffn0", sums_after=[dcat_a])
    dproj_a, d_ws, d_bs3, d_lng, d_lnb = _gmlp_bwd(proj_a, dcat_a, _tie(lng, sums), lnb, ws, bs3, "gmlp_bwd")
    dproj_a, dkv_a = _attn_bwd(proj_a, 4, kvs[0], dcat_a, dproj_a, "attn_bwd0")
    mem_bwd(dkv_a, 0)

    def big_update(w, m, v, fam, nlayer):
        res = None
        for i in range(nlayer):
            grad, recv1, recv2 = reduced[fam, i]
            plist = [(recv1, 0), (recv2, 0), (recv2, 1), (recv2, 2)]
            res = _adamw(w, m, v, (grad, lay_g[fam][0]), plist, me1, f"adamw_{fam}{i}", layer=i, prev=res)
        return res

    da0 = _mm(dproj_a, WA, m=s, n=D_MODEL, k=5 * D_MODEL, tb=True, b_at=(0, 0, 0), name="proj_a_dx")
    grad_x, _, d_nmix[0] = _rms_bwd(xs, nmix[0], da0, dh1, "mix_norm_bwd0")
    ga = _mm(a0, dproj_a, m=D_MODEL, n=5 * D_MODEL, k=s, ta=True, out_dtype=_ACT, after=[grad_x], name="proj_a_dw")
    r_b = big_update(tr_b(b_in), tr_b(m_b_in), tr_b(v_b_in), "b", 1)
    reduce_scatter([("kv", 0, g_kv[0]), ("a", 0, ga)], "mix0", after=got_ffn0, sums_after=r_b)
    r_b = [tr_b(o) for o in r_b]

    small_names = ["norm_mix", "norm_ffn", "mem_norm", "a_ln_g", "a_ln_b", "a_ws", "a_bs", "b_dt_bias", "b_a_log", "b_d",
                   "final_norm", "b_conv_w", "b_conv_b", "b_gnorm"]
    small_grads = [jnp.concatenate(d_nmix, axis=0), jnp.concatenate(d_nffn, axis=0), jnp.concatenate(d_nmem, axis=0),
                   d_lng, d_lnb, d_ws.reshape(A_GROUPS * CHUNK, CHUNK), d_bs3.reshape(A_GROUPS, CHUNK),
                   d_dtbias[:, :SSM_HEADS], d_alog[:, :SSM_HEADS], d_dskip[:, :SSM_HEADS], d_fin,
                   d_convw, d_convb, d_gnorm]
    small_2d = [(2, D_MODEL)] * 3 + [(1, D_INNER)] * 2 + [(A_GROUPS * CHUNK, CHUNK), (A_GROUPS, CHUNK)] + [(1, SSM_HEADS)] * 3 \
        + [(1, D_MODEL), (CONV_K, 384), (1, 384), (1, 256)]
    gathered = _all_gather_seq(
        small_grads + [loss_part], [("blk", 0, (N_DEV,) + g.shape) for g in small_grads + [loss_part]], "ag_small_grads")

    r_f1 = big_update(w_ffn1, m_w_ffn1, v_w_ffn1, "f1", 2)
    r_f2 = big_update(w_ffn2, m_w_ffn2, v_w_ffn2, "f2", 2)
    r_out = big_update(w_out, m_w_out, v_w_out, "out", 2)
    r_kv = big_update(w_kv, m_w_kv, v_w_kv, "kv", 2)
    r_a = big_update(a_in, m_a_in, v_a_in, "a", 1)

    small_w = [norm_mix, norm_ffn, mem_norm, a_ln_g, a_ln_b, a_ws, a_bs, b_dt_bias, b_a_log, b_d, final_norm,
               b_conv_w, b_conv_b, b_gnorm]
    small_m = [m_norm_mix, m_norm_ffn, m_mem_norm, m_a_ln_g, m_a_ln_b, m_a_ws, m_a_bs, m_b_dt_bias, m_b_a_log, m_b_d,
               m_final_norm, m_b_conv_w, m_b_conv_b, m_b_gnorm]
    small_v = [v_norm_mix, v_norm_ffn, v_mem_norm, v_a_ln_g, v_a_ln_b, v_a_ws, v_a_bs, v_b_dt_bias, v_b_a_log, v_b_d,
               v_final_norm, v_b_conv_w, v_b_conv_b, v_b_gnorm]
    params = [tuple(a.reshape(shp) for a in wmv) for shp, wmv in zip(small_2d, zip(small_w, small_m, small_v))]
    loss_all = _tie(gathered[-1], [r_a[0], r_kv[0]])
    small_res, loss_sum = _small_update(gathered[:-1], params, loss_all, me1, "adamw_small")
    loss = loss_sum[0, 0]

    names = ["norm_mix", "norm_ffn", "mem_norm", "w_kv", "w_out", "w_ffn1", "w_ffn2", "a_in", "a_ln_g", "a_ln_b", "a_ws",
             "a_bs", "b_in", "b_conv_w", "b_conv_b", "b_dt_bias", "b_a_log", "b_d", "b_gnorm", "final_norm"]
    big = {"w_kv": r_kv, "w_out": r_out, "w_ffn1": r_f1, "w_ffn2": r_f2, "a_in": r_a, "b_in": r_b}
    outs = [loss, grad_x.reshape(x.shape)]
    for kind in range(4):
        for nm in names:
            if nm in big:
                outs.append(big[nm][kind])
            else:
                i = small_names.index(nm)
                outs.append(small_res[i][kind].reshape(small_w[i].shape))
    return tuple(outs)
```

```python
import functools
import math

import jax
import jax.numpy as jnp
from jax import lax
from jax.experimental import pallas as pl
from jax.experimental.pallas import tpu as pltpu
from jax.experimental.pallas import tpu_sc as plsc

F32 = jnp.float32
_MXU = jnp.bfloat16
_ACT = jnp.bfloat16

D_MODEL = 1024
CHUNK = 128
N_MEM = 256
D_INNER = 2048
A_GROUPS = 8
A_GW = D_INNER // A_GROUPS
SSM_HEADS = 32
SSM_P = 64
SSM_GROUPS = 4
SSM_GW = D_INNER // SSM_GROUPS
SSM_N = 128
CONV_K = 4
CONV_DIM = 3072
X_HEADS = 4
X_HD = 256
X_WIDTH = 1024
D_FF = 4096
EPS = 1e-6
HPAD = 128
N_DEV = 8

ADAM_LR = 0.001
ADAM_B1 = 0.9
ADAM_B2 = 0.999
ADAM_EPS = 1e-08
ADAM_WD = 0.01
ADAM_STEP = 10

VMEM_BIG = 56 * 1024 * 1024
MESH = pl.DeviceIdType.MESH


def _cp(vmem=None):
    if vmem is None:
        return pltpu.CompilerParams()
    return pltpu.CompilerParams(vmem_limit_bytes=vmem)


def _dot(a, b, dims=((1,), (0,))):
    return lax.dot_general(a.astype(_MXU), b.astype(_MXU), (dims, ((), ())), preferred_element_type=F32)


def _dot_nt(a, b):
    return _dot(a, b, ((1,), (1,)))


def _dot_tn(a, b):
    return _dot(a, b, ((0,), (0,)))


def _split3(x):
    x1 = x.astype(jnp.bfloat16)
    r = x - x1.astype(F32)
    x2 = r.astype(jnp.bfloat16)
    x3 = (r - x2.astype(F32)).astype(jnp.bfloat16)
    return x1, x2, x3


def _dot_sel(x, sel, dims=((1,), (0,)), terms=2):
    sel = sel.astype(jnp.bfloat16)
    parts = [lax.dot_general(t, sel, (dims, ((), ())), preferred_element_type=F32) for t in _split3(x)[:terms]]
    return functools.reduce(lambda a, b: a + b, parts)


def _sel_dot(sel, x, dims=((1,), (0,))):
    sel = sel.astype(jnp.bfloat16)
    parts = [lax.dot_general(sel, t, (dims, ((), ())), preferred_element_type=F32) for t in _split3(x)]
    return (parts[0] + parts[1]) + parts[2]


def _sigmoid(x):
    return 1.0 / (1.0 + jnp.exp(-x))


def _gelu(x):
    return 0.5 * x * (1.0 + lax.erf(x * (1.0 / math.sqrt(2.0))))


def _gelu_with_grad(x):
    phi = 0.5 * (1.0 + lax.erf(x * (1.0 / math.sqrt(2.0))))
    return x * phi, phi + x * jnp.exp(-0.5 * x * x) * (1.0 / math.sqrt(2.0 * math.pi))


def _softplus(x):
    return jnp.maximum(x, 0.0) + jnp.log1p(jnp.exp(-jnp.abs(x)))


def _iota(shape, dim):
    return lax.broadcasted_iota(jnp.int32, shape, dim)


MM_VMEM_BUDGET = 40 * 1024 * 1024
HBM_BYTES_PER_S = 2.5e12
GRID_STEP_S = 0.35e-6
VMEM_ACC_BYTES_PER_S = 6e12


def _divisors(dim, unit):
    out = [d for d in range(unit, min(dim, 2048) + 1, unit) if dim % d == 0]
    return out if out else [dim]


def _mm_tiles(m, n, k, sa, sb, s_mn, a_pro, offsets):
    best = None
    (a_r0, a_c0, ta), (b_r0, b_c0, tb), (o_r0, o_c0) = offsets
    for tm in _divisors(m, 128):
        for tn in _divisors(n, 128):
            for tk in [k // d for d in (1, 2, 3, 4, 6, 8) if k % d == 0 and (k // d) % 128 == 0]:
                a_t = (tk, tm) if ta else (tm, tk)
                b_t = (tn, tk) if tb else (tk, tn)
                if a_r0 % a_t[0] or a_c0 % a_t[1] or b_r0 % b_t[0] or b_c0 % b_t[1] or o_r0 % tm or o_c0 % tn:
                    continue
                nk = k // tk
                vmem = 2 * (tm * tk * sa + tk * tn * sb + tm * tn * s_mn) + tm * tn * 4 * (2 if nk > 1 else 1)
                if a_pro or sa == 4:
                    vmem += tm * tk * 6
                if sb == 4:
                    vmem += tk * tn * 2
                if vmem > MM_VMEM_BUDGET:
                    continue
                gi, gj = m // tm, n // tn
                for j_inner in (True, False):
                    if nk > 1:
                        traffic = gj * m * k * sa + gi * k * n * sb
                    elif j_inner:
                        traffic = m * k * sa + gi * k * n * sb
                    else:
                        traffic = gj * m * k * sa + k * n * sb
                    traffic += m * n * s_mn + (tm * tk * sa + tk * tn * sb)
                    cost = traffic / HBM_BYTES_PER_S + gi * gj * nk * GRID_STEP_S
                    if nk > 1:
                        cost += m * n * 8 * nk / VMEM_ACC_BYTES_PER_S
                    if best is None or cost < best[0]:
                        best = (cost, tm, tn, tk, j_inner)
    assert best is not None, (m, n, k)
    return best[1:]


def _mm(a, b, *, m, n, k, name, ta=False, tb=False, a_at=(None, 0, 0), b_at=(None, 0, 0),
        out_dtype=F32, add=None, epi_p=None, epi_at=(None, 0, 0), out=None, out_at=(None, 0, 0),
        out_full=None, a_pro=None, after=()):
    s_mn =jnp.dtype(out.dtype if out is not None else out_dtype).itemsize
    s_mn += add.dtype.itemsize if add is not None else 0
    s_mn += epi_p.dtype.itemsize if epi_p is not None else 0
    tm, tn, tk, j_inner = _mm_tiles(m, n, k, a.dtype.itemsize, b.dtype.itemsize, s_mn, a_pro is not None,
                                    ((a_at[1], a_at[2], ta), (b_at[1], b_at[2], tb), (out_at[1], out_at[2])))
    nk = k // tk

    def spec(at, tr, tc, rsel, csel):
        lead, r0, c0 = at
        assert r0 % tr == 0 and c0 % tc == 0, (name, at, tr, tc)
        rb, cb = r0 // tr, c0 // tc
        if lead is None:
            return pl.BlockSpec((tr, tc), lambda g0, g1, kk: (rb + rsel(g0, g1, kk), cb + csel(g0, g1, kk)))
        return pl.BlockSpec((None, tr, tc), lambda g0, g1, kk: (lead, rb + rsel(g0, g1, kk), cb + csel(g0, g1, kk)))

    gi = (lambda g0, g1, kk: g0) if j_inner else (lambda g0, g1, kk: g1)
    gj = (lambda g0, g1, kk: g1) if j_inner else (lambda g0, g1, kk: g0)
    gk = lambda g0, g1, kk: kk
    a_spec = spec(a_at, tk, tm, gk, gi) if ta else spec(a_at, tm, tk, gi, gk)
    b_spec = spec(b_at, tn, tk, gj, gk) if tb else spec(b_at, tk, tn, gk, gj)
    dims = ((0,), (0,)) if ta else (((1,), (1,)) if tb else ((1,), (0,)))
    assert not (ta and tb)

    operands, in_specs = [a, b], [a_spec, b_spec]
    if add is not None:
        operands.append(add)
        in_specs.append(spec((None, 0, 0), tm, tn, gi, gj))
    if epi_p is not None:
        operands.append(epi_p)
        in_specs.append(spec(epi_at, tm, tn, gi, gj))
    aliases = {}
    if out is not None:
        aliases = {len(operands): 0}
        operands.append(out)
        in_specs.append(pl.BlockSpec(memory_space=pl.ANY))
        out_struct = jax.ShapeDtypeStruct(out.shape, out.dtype)
        out_dtype = out.dtype
    else:
        out_struct = jax.ShapeDtypeStruct(out_full if out_full is not None else (m, n), out_dtype)
    has_add, has_epi = add is not None, epi_p is not None
    n_skip = (1 if out is not None else 0) + len(after)
    operands += list(after)
    in_specs += [pl.BlockSpec(memory_space=pl.ANY)] * len(after)

    def body(*refs):
        a_ref, b_ref = refs[0], refs[1]
        pos = 2
        add_ref = epi_ref = None
        if has_add:
            add_ref = refs[pos]
            pos += 1
        if has_epi:
            epi_ref = refs[pos]
            pos += 1
        pos += n_skip
        o_ref = refs[pos]

        def finish(r):
            if has_add:
                r = r + add_ref[...].astype(F32)
            if has_epi:
                r = r * (2.0 * jnp.maximum(epi_ref[...].astype(F32), 0.0))
            o_ref[...] = r.astype(o_ref.dtype)

        av = a_ref[...]
        if a_pro == "relu2":
            av = jnp.square(jnp.maximum(av.astype(F32), 0.0))
        part = _dot(av, b_ref[...], dims)
        if nk == 1:
            finish(part)
        else:
            acc_ref = refs[pos + 1]
            kk = pl.program_id(2)

            @pl.when(kk == 0)
            def _():
                acc_ref[...] = part

            @pl.when(kk > 0)
            def _():
                acc_ref[...] += part

            @pl.when(kk == nk - 1)
            def _():
                finish(acc_ref[...])

    grid = (m // tm, n // tn, nk) if j_inner else (n // tn, m // tm, nk)
    return pl.pallas_call(
        body, name=name, grid=grid, in_specs=in_specs,
        out_specs=spec(out_at, tm, tn, gi, gj), out_shape=out_struct,
        scratch_shapes=[pltpu.VMEM((tm, tn), F32)] if nk > 1 else [], input_output_aliases=aliases,
        compiler_params=_cp(VMEM_BIG))(*operands)


def _rms_fwd(x, g, name, tm=1024):
    s, d = x.shape
    tm = min(tm, s)

    def body(x_ref, g_ref, o_ref):
        xv = x_ref[...]
        r = lax.rsqrt(jnp.mean(xv * xv, axis=-1, keepdims=True) + EPS)
        o_ref[...] = (xv * r * g_ref[...]).astype(o_ref.dtype)

    return pl.pallas_call(
        body, name=name, grid=(s // tm,),
        in_specs=[pl.BlockSpec((tm, d), lambda i: (i, 0)), pl.BlockSpec((1, d), lambda i: (0, 0))],
        out_specs=pl.BlockSpec((tm, d), lambda i: (i, 0)),
        out_shape=jax.ShapeDtypeStruct((s, d), _ACT), compiler_params=_cp(VMEM_BIG))(x, g)


def _rms_bwd(x, g, dy, dres, name, tm=512):
    s, d = x.shape
    tm = min(tm, s)
    has_res = dres is not None

    def body(*refs):
        if has_res:
            x_ref, g_ref, dy_ref, dres_ref, dx_ref, dxa_ref, dg_ref = refs
        else:
            x_ref, g_ref, dy_ref, dx_ref, dxa_ref, dg_ref = refs

        @pl.when(pl.program_id(0) == 0)
        def _():
            dg_ref[...] = jnp.zeros_like(dg_ref)

        xv = x_ref[...]
        dyv = dy_ref[...].astype(F32)
        r = lax.rsqrt(jnp.mean(xv * xv, axis=-1, keepdims=True) + EPS)
        xh = xv * r
        dyg = dyv * g_ref[...]
        dx = r * (dyg - xh * jnp.mean(dyg * xh, axis=-1, keepdims=True))
        if has_res:
            dx = dx + dres_ref[...]
        dx_ref[...] = dx
        dxa_ref[...] = dx.astype(dxa_ref.dtype)
        dg_ref[...] += jnp.sum(dyv * xh, axis=0, keepdims=True)

    row = pl.BlockSpec((tm, d), lambda i: (i, 0))
    vec = pl.BlockSpec((1, d), lambda i: (0, 0))
    in_specs = [row, vec, row] + ([row] if has_res else [])
    operands = [x, g, dy] + ([dres] if has_res else [])
    return pl.pallas_call(
        body, name=name, grid=(s // tm,), in_specs=in_specs, out_specs=[row, row, vec],
        out_shape=[jax.ShapeDtypeStruct((s, d), F32), jax.ShapeDtypeStruct((s, d), _ACT),
                   jax.ShapeDtypeStruct((1, d), F32)], compiler_params=_cp(VMEM_BIG))(*operands)


def _loss_head(h, g, target, name, tm=512):
    s, d = h.shape
    tm = min(tm, s)

    def body(h_ref, g_ref, t_ref, loss_ref, dh_ref, dha_ref, dg_ref):
        @pl.when(pl.program_id(0) == 0)
        def _():
            dg_ref[...] = jnp.zeros_like(dg_ref)
            loss_ref[...] = jnp.zeros_like(loss_ref)

        xv = h_ref[...]
        r = lax.rsqrt(jnp.mean(xv * xv, axis=-1, keepdims=True) + EPS)
        xh = xv * r
        err = xh * g_ref[...] - t_ref[...]
        loss_ref[...] += jnp.full(loss_ref.shape, 0.5 * jnp.sum(jnp.mean(err * err, axis=-1, keepdims=True)), F32)
        dyv = err * (1.0 / d)
        dyg = dyv * g_ref[...]
        dh = r * (dyg - xh * jnp.mean(dyg * xh, axis=-1, keepdims=True))
        dh_ref[...] = dh
        dha_ref[...] = dh.astype(dha_ref.dtype)
        dg_ref[...] += jnp.sum(dyv * xh, axis=0, keepdims=True)

    row = pl.BlockSpec((tm, d), lambda i: (i, 0))
    vec = pl.BlockSpec((1, d), lambda i: (0, 0))
    return pl.pallas_call(
        body, name=name, grid=(s // tm,), in_specs=[row, vec, row],
        out_specs=[pl.BlockSpec((1, 128), lambda i: (0, 0)), row, row, vec],
        out_shape=[jax.ShapeDtypeStruct((1, 128), F32), jax.ShapeDtypeStruct((s, d), F32),
                   jax.ShapeDtypeStruct((s, d), _ACT), jax.ShapeDtypeStruct((1, d), F32)],
        compiler_params=_cp(VMEM_BIG))(h, g, target)


def _gmlp_parts(u, v, lng, lnb):
    mu = jnp.mean(v, axis=-1, keepdims=True)
    vc = v - mu
    rstd = lax.rsqrt(jnp.mean(vc * vc, axis=-1, keepdims=True) + EPS)
    xhat = vc * rstd
    vn = xhat * lng + lnb
    return u, xhat, rstd, vn


def _gmlp_fwd(proj, lng, lnb, ws, bs3, name):
    s = proj.shape[0]

    def body(pu_ref, pv_ref, lng_ref, lnb_ref, ws_ref, bs_ref, o_ref):
        u, _, _, vn = _gmlp_parts(_gelu(pu_ref[...]), _gelu(pv_ref[...]), lng_ref[...], lnb_ref[...])
        causal = _iota((CHUNK, CHUNK), 0) >= _iota((CHUNK, CHUNK), 1)
        for g in range(A_GROUPS):
            sl = slice(g * A_GW, (g + 1) * A_GW)
            w = jnp.where(causal, ws_ref[g], 0.0)
            sv = _dot(w, vn[:, sl]) + bs_ref[g]
            o_ref[:, sl] = (u[:, sl] * sv).astype(o_ref.dtype)

    full = lambda shape: pl.BlockSpec(shape, lambda c: (0,) * len(shape))
    return pl.pallas_call(
        body, name=name, grid=(s // CHUNK,),
        in_specs=[pl.BlockSpec((CHUNK, D_INNER), lambda c: (c, 0)), pl.BlockSpec((CHUNK, D_INNER), lambda c: (c, 1)),
                  full((1, D_INNER)), full((1, D_INNER)), full((A_GROUPS, CHUNK, CHUNK)), full((A_GROUPS, CHUNK, 1))],
        out_specs=pl.BlockSpec((CHUNK, D_INNER), lambda c: (c, 0)),
        out_shape=jax.ShapeDtypeStruct((s, D_INNER + X_WIDTH), _ACT), compiler_params=_cp(VMEM_BIG))(proj, proj, lng, lnb, ws, bs3)


def _gmlp_bwd(proj, dcat, lng, lnb, ws, bs3, name):
    s = proj.shape[0]

    def body(pu_ref, pv_ref, dm_ref, lng_ref, lnb_ref, ws_ref, bs_ref, dp_ref, dws_ref, dbs_ref, dlng_ref, dlnb_ref, dvn_ref):
        @pl.when(pl.program_id(0) == 0)
        def _():
            dws_ref[...] = jnp.zeros_like(dws_ref)
            dbs_ref[...] = jnp.zeros_like(dbs_ref)
            dlng_ref[...] = jnp.zeros_like(dlng_ref)
            dlnb_ref[...] = jnp.zeros_like(dlnb_ref)

        lng = lng_ref[...]
        u, u_grad = _gelu_with_grad(pu_ref[...])
        v, v_grad = _gelu_with_grad(pv_ref[...])
        u, xhat, rstd, vn = _gmlp_parts(u, v, lng, lnb_ref[...])
        dm = dm_ref[...].astype(F32)
        causal = _iota((CHUNK, CHUNK), 0) >= _iota((CHUNK, CHUNK), 1)
        for g in range(A_GROUPS):
            sl = slice(g * A_GW, (g + 1) * A_GW)
            w = jnp.where(causal, ws_ref[g], 0.0)
            sv = _dot(w, vn[:, sl]) + bs_ref[g]
            dsv = dm[:, sl] * u[:, sl]
            dp_ref[:, sl] = (dm[:, sl] * sv * u_grad[:, sl]).astype(dp_ref.dtype)
            dvn_ref[:, sl] = _dot_tn(w, dsv)
            dws_ref[g] += jnp.where(causal, _dot_nt(dsv, vn[:, sl]), 0.0)
            dbs_ref[g] += jnp.sum(dsv, axis=-1, keepdims=True)
        dvn = dvn_ref[...]
        dlng_ref[...] += jnp.sum(dvn * xhat, axis=0, keepdims=True)
        dlnb_ref[...] += jnp.sum(dvn, axis=0, keepdims=True)
        dxh = dvn * lng
        dv = rstd * (dxh - jnp.mean(dxh, axis=-1, keepdims=True) - xhat * jnp.mean(dxh * xhat, axis=-1, keepdims=True))
        dp_ref[:, D_INNER:] = (dv * v_grad).astype(dp_ref.dtype)

    full = lambda shape: pl.BlockSpec(shape, lambda c: (0,) * len(shape))
    return pl.pallas_call(
        body, name=name, grid=(s // CHUNK,),
        in_specs=[pl.BlockSpec((CHUNK, D_INNER), lambda c: (c, 0)), pl.BlockSpec((CHUNK, D_INNER), lambda c: (c, 1)),
                  pl.BlockSpec((CHUNK, D_INNER), lambda c: (c, 0)),
                  full((1, D_INNER)), full((1, D_INNER)), full((A_GROUPS, CHUNK, CHUNK)), full((A_GROUPS, CHUNK, 1))],
        out_specs=[pl.BlockSpec((CHUNK, 2 * D_INNER), lambda c: (c, 0)), full((A_GROUPS, CHUNK, CHUNK)),
                   full((A_GROUPS, CHUNK, 1)), full((1, D_INNER)), full((1, D_INNER))],
        out_shape=[jax.ShapeDtypeStruct((s, 2 * D_INNER + X_WIDTH), _ACT), jax.ShapeDtypeStruct((A_GROUPS, CHUNK, CHUNK), F32),
                   jax.ShapeDtypeStruct((A_GROUPS, CHUNK, 1), F32), jax.ShapeDtypeStruct((1, D_INNER), F32),
                   jax.ShapeDtypeStruct((1, D_INNER), F32)],
        scratch_shapes=[pltpu.VMEM((CHUNK, D_INNER), F32)],
        compiler_params=_cp(VMEM_BIG))(proj, proj, dcat, lng, lnb, ws, bs3)


_X_SCALE = 1.0 / math.sqrt(X_HD)


def _attn_fwd(proj, qblk, kv, cat, name, tm=512):
    s = proj.shape[0]
    tm = min(tm, s)

    def body(q_ref, kv_ref, cat_ref, o_ref):
        for h in range(X_HEADS):
            sl = slice(h * X_HD, (h + 1) * X_HD)
            k = kv_ref[:, sl]
            v = kv_ref[:, X_WIDTH + h * X_HD:X_WIDTH + (h + 1) * X_HD]
            sc = _dot_nt(q_ref[:, sl], k) * _X_SCALE
            e = jnp.exp(sc - jnp.max(sc, axis=-1, keepdims=True))
            p = e / jnp.sum(e, axis=-1, keepdims=True)
            o_ref[:, sl] = _dot(p, v).astype(o_ref.dtype)

    return pl.pallas_call(
        body, name=name, grid=(s // tm,),
        in_specs=[pl.BlockSpec((tm, X_WIDTH), lambda i: (i, qblk)), pl.BlockSpec((N_MEM, 2 * X_WIDTH), lambda i: (0, 0)),
                  pl.BlockSpec(memory_space=pl.ANY)],
        out_specs=pl.BlockSpec((tm, X_WIDTH), lambda i: (i, D_INNER // X_WIDTH)),
        out_shape=jax.ShapeDtypeStruct(cat.shape, cat.dtype), input_output_aliases={2: 0},
        compiler_params=_cp(VMEM_BIG))(proj, kv, cat)


def _attn_bwd(proj, qblk, kv, dcat, dproj, name, tm=512):
    s = proj.shape[0]
    tm = min(tm, s)

    def body(q_ref, kv_ref, do_ref, dproj_ref, dq_ref, dkv_ref):
        @pl.when(pl.program_id(0) == 0)
        def _():
            dkv_ref[...] = jnp.zeros_like(dkv_ref)

        for h in range(X_HEADS):
            sl = slice(h * X_HD, (h + 1) * X_HD)
            slv = slice(X_WIDTH + h * X_HD, X_WIDTH + (h + 1) * X_HD)
            q = q_ref[:, sl]
            k = kv_ref[:, sl]
            v = kv_ref[:, slv]
            do = do_ref[:, sl].astype(F32)
            sc = _dot_nt(q, k) * _X_SCALE
            e = jnp.exp(sc - jnp.max(sc, axis=-1, keepdims=True))
            p = e / jnp.sum(e, axis=-1, keepdims=True)
            dp = _dot_nt(do, v)
            ds = p * (dp - jnp.sum(dp * p, axis=-1, keepdims=True)) * _X_SCALE
            dq_ref[:, sl] = _dot(ds, k).astype(dq_ref.dtype)
            dkv_ref[:, sl] += _dot_tn(ds, q)
            dkv_ref[:, slv] += _dot_tn(p, do)

    return pl.pallas_call(
        body, name=name, grid=(s // tm,),
        in_specs=[pl.BlockSpec((tm, X_WIDTH), lambda i: (i, qblk)), pl.BlockSpec((N_MEM, 2 * X_WIDTH), lambda i: (0, 0)),
                  pl.BlockSpec((tm, X_WIDTH), lambda i: (i, 2)), pl.BlockSpec(memory_space=pl.ANY)],
        out_specs=[pl.BlockSpec((tm, X_WIDTH), lambda i: (i, qblk)), pl.BlockSpec((N_MEM, 2 * X_WIDTH), lambda i: (0, 0))],
        out_shape=[jax.ShapeDtypeStruct(dproj.shape, dproj.dtype), jax.ShapeDtypeStruct((N_MEM, 2 * X_WIDTH), F32)],
        input_output_aliases={3: 0}, compiler_params=_cp(VMEM_BIG))(proj, kv, dcat, dproj)


CONV_TC = 256
_XBC_BLK0 = D_INNER // CONV_TC


CONV_RB = 64
SUBLANES = 8


def _rows_before(cur, prev_last, j):
    rolled = pltpu.roll(cur, j, 0)
    head = jnp.where(_iota((SUBLANES, cur.shape[1]), 0) < j, pltpu.roll(prev_last, j, 0), rolled[:SUBLANES])
    return jnp.concatenate([head, rolled[SUBLANES:]], axis=0)


def _rows_after(cur, next_first, j):
    n = cur.shape[0]
    rolled = pltpu.roll(cur, n - j, 0)
    tail = jnp.where(_iota((SUBLANES, cur.shape[1]), 0) >= SUBLANES - j, pltpu.roll(next_first, SUBLANES - j, 0),
                     rolled[n - SUBLANES:])
    return jnp.concatenate([rolled[:n - SUBLANES], tail], axis=0)


def _conv_pre(x_ref, w_ref, b_ref, r0, prev_last):
    cur = x_ref[pl.ds(r0, CONV_RB), :]
    shifts = [_rows_before(cur, prev_last, j) for j in range(1, CONV_K)]
    pre = b_ref[...] + w_ref[CONV_K - 1:CONV_K, :] * cur
    for j in range(1, CONV_K):
        pre = pre + w_ref[CONV_K - 1 - j:CONV_K - j, :] * shifts[j - 1]
    return pre, cur, shifts


def _conv_fwd(proj, w, b, name):
    s = proj.shape[0]

    def body(x_ref, w_ref, b_ref, o_ref):
        xv = x_ref[...]
        rows = _iota(xv.shape, 0)
        pre = b_ref[...] + w_ref[CONV_K - 1:CONV_K, :] * xv
        for j in range(1, CONV_K):
            pre = pre + w_ref[CONV_K - 1 - j:CONV_K - j, :] * jnp.where(rows >= j, pltpu.roll(xv, j, 0), 0.0)
        o_ref[...] = pre * _sigmoid(pre)

    return pl.pallas_call(
        body, name=name, grid=(CONV_DIM // CONV_TC,),
        in_specs=[pl.BlockSpec((s, CONV_TC), lambda j: (0, _XBC_BLK0 + j)), pl.BlockSpec((CONV_K, CONV_TC), lambda j: (0, j)),
                  pl.BlockSpec((1, CONV_TC), lambda j: (0, j))],
        out_specs=pl.BlockSpec((s, CONV_TC), lambda j: (0, j)),
        out_shape=jax.ShapeDtypeStruct((s, CONV_DIM), F32), compiler_params=_cp(VMEM_BIG))(proj, w, b)


def _conv_bwd(proj, w, b, dxbc, dproj, name):
    s = proj.shape[0]

    nb = s // CONV_RB

    def body(x_ref, w_ref, b_ref, d_ref, dproj_ref, dx_ref, dw_ref, db_ref, dpre_ref):
        def fold(v):
            out = v[:SUBLANES]
            for t in range(1, CONV_RB // SUBLANES):
                out = out + v[t * SUBLANES:(t + 1) * SUBLANES]
            return out

        def first(i, carry):
            prev_last, acc = carry
            r0 = pl.multiple_of(i * CONV_RB, CONV_RB)
            pre, cur, shifts = _conv_pre(x_ref, w_ref, b_ref, r0, prev_last)
            sig = _sigmoid(pre)
            dpre = d_ref[pl.ds(r0, CONV_RB), :] * (sig * (1.0 + pre * (1.0 - sig)))
            dpre_ref[pl.ds(r0, CONV_RB), :] = dpre
            taps = [cur] + shifts
            acc = tuple(a + fold(dpre * t) for a, t in zip(acc[:CONV_K], taps)) + (acc[CONV_K] + fold(dpre),)
            return cur[CONV_RB - SUBLANES:], acc

        zero8 = jnp.zeros((SUBLANES, CONV_TC), F32)
        _, acc = lax.fori_loop(0, nb, first, (zero8, (zero8,) * (CONV_K + 1)))
        for j in range(CONV_K):
            dw_ref[CONV_K - 1 - j:CONV_K - j, :] = jnp.sum(acc[j], axis=0, keepdims=True)
        db_ref[...] = jnp.sum(acc[CONV_K], axis=0, keepdims=True)

        def second(i, next_first):
            r0 = pl.multiple_of((nb - 1 - i) * CONV_RB, CONV_RB)
            cur = dpre_ref[pl.ds(r0, CONV_RB), :]
            dx = w_ref[CONV_K - 1:CONV_K, :] * cur
            for j in range(1, CONV_K):
                dx = dx + w_ref[CONV_K - 1 - j:CONV_K - j, :] * _rows_after(cur, next_first, j)
            dx_ref[pl.ds(r0, CONV_RB), :] = dx.astype(dx_ref.dtype)
            return cur[:SUBLANES]

        lax.fori_loop(0, nb, second, zero8)

    return pl.pallas_call(
        body, name=name, grid=(CONV_DIM // CONV_TC,),
        in_specs=[pl.BlockSpec((s, CONV_TC), lambda j: (0, _XBC_BLK0 + j)), pl.BlockSpec((CONV_K, CONV_TC), lambda j: (0, j)),
                  pl.BlockSpec((1, CONV_TC), lambda j: (0, j)), pl.BlockSpec((s, CONV_TC), lambda j: (0, j)),
                  pl.BlockSpec(memory_space=pl.ANY)],
        out_specs=[pl.BlockSpec((s, CONV_TC), lambda j: (0, _XBC_BLK0 + j)), pl.BlockSpec((CONV_K, CONV_TC), lambda j: (0, j)),
                   pl.BlockSpec((1, CONV_TC), lambda j: (0, j))],
        out_shape=[jax.ShapeDtypeStruct(dproj.shape, dproj.dtype), jax.ShapeDtypeStruct((CONV_K, CONV_DIM), F32),
                   jax.ShapeDtypeStruct((1, CONV_DIM), F32)], input_output_aliases={4: 0},
        scratch_shapes=[pltpu.VMEM((s, CONV_TC), F32)],
        compiler_params=_cp(VMEM_BIG))(proj, w, b, dxbc, dproj)


def _ssd_common(dtc_ref, br_ref, ar_ref, csb_ref, cst_ref, csf_ref, dtf_ref, expand):
    a_row = -jnp.exp(ar_ref[...])
    dt_c = _softplus(dtc_ref[...] + br_ref[...])
    tril = _iota((CHUNK, CHUNK), 0) >= _iota((CHUNK, CHUNK), 1)
    cs = _sel_dot(tril, dt_c * a_row)
    cst_ref[...] = cs.T
    e64 = (jnp.right_shift(_iota((HPAD, D_INNER), 1), 6) == _iota((HPAD, D_INNER), 0)).astype(jnp.bfloat16)
    if expand:
        e128 = jnp.right_shift(_iota((HPAD, SSM_HEADS * CHUNK), 1), 7) == _iota((HPAD, SSM_HEADS * CHUNK), 0)
        csb_ref[...] = _dot_sel(cs, e128)
        dtf_ref[...] = _dot_sel(dt_c, e64)
        csf_ref[...] = _dot_sel(cs, e64)
    dt_full = dtf_ref[...]
    cs_full = csf_ref[...]
    cs_last = csf_ref[CHUNK - 1:CHUNK, :]
    e_full = jnp.exp(cs_full)
    f_full = jnp.exp(cs_last - cs_full)
    gamma = jnp.exp(cs_last)
    return a_row, dt_c, cs, dt_full, e_full, f_full, gamma, e64


def _ssd_lambda(csb_ref, cst_ref, h, causal):
    diff = csb_ref[:, h * CHUNK:(h + 1) * CHUNK] - cst_ref[h:h + 1, :]
    return jnp.exp(jnp.where(causal, diff, -1e30))


_SSD_VEC_SPECS = lambda: [pl.BlockSpec((1, HPAD), lambda c: (0, 0)), pl.BlockSpec((1, HPAD), lambda c: (0, 0)),
                          pl.BlockSpec((1, D_INNER), lambda c: (0, 0))]


def _ssd_fwd(xbc, dtc, bias_row, alog_row, dfull, name):
    s = xbc.shape[0]
    nc = s // CHUNK

    def body(xbc_ref, dtc_ref, br_ref, ar_ref, df_ref, y_ref, st_ref, csb_ref, csf_ref, dtf_ref, ht_ref, cst_ref):
        @pl.when(pl.program_id(0) == 0)
        def _():
            ht_ref[...] = jnp.zeros_like(ht_ref)

        _, _, _, dt_full, e_full, f_full, gamma, _ = _ssd_common(
            dtc_ref, br_ref, ar_ref, csb_ref, cst_ref, csf_ref, dtf_ref, expand=True)
        x = xbc_ref[:, :D_INNER]
        xdt = x * dt_full
        st_ref[...] = ht_ref[...]
        causal = _iota((CHUNK, CHUNK), 0) >= _iota((CHUNK, CHUNK), 1)
        lo = _iota((CHUNK, CHUNK), 1) < SSM_P
        for g in range(SSM_GROUPS):
            gs = slice(g * SSM_GW, (g + 1) * SSM_GW)
            bg = xbc_ref[:, D_INNER + g * SSM_N:D_INNER + (g + 1) * SSM_N]
            cg = xbc_ref[:, D_INNER + SSM_GROUPS * SSM_N + g * SSM_N:D_INNER + SSM_GROUPS * SSM_N + (g + 1) * SSM_N]
            ht = ht_ref[:, gs]
            cb = _dot_nt(cg, bg)
            yoff = e_full[:, gs] * _dot(cg, ht)
            for jp in range(SSM_GW // CHUNK):
                j = g * (SSM_GW // CHUNK) + jp
                ps = slice(j * CHUNK, (j + 1) * CHUNK)
                x2 = xdt[:, ps]
                y0 = _dot(cb * _ssd_lambda(csb_ref, cst_ref, 2 * j, causal), x2)
                y1 = _dot(cb * _ssd_lambda(csb_ref, cst_ref, 2 * j + 1, causal), x2)
                y_ref[:, ps] = (jnp.where(lo, y0, y1) + yoff[:, jp * CHUNK:(jp + 1) * CHUNK]
                                + x[:, ps] * df_ref[:, ps])
            ht_ref[:, gs] = gamma[:, gs] * ht + _dot_tn(bg, xdt[:, gs] * f_full[:, gs])

    return pl.pallas_call(
        body, name=name, grid=(nc,),
        in_specs=[pl.BlockSpec((CHUNK, CONV_DIM), lambda c: (c, 0)), pl.BlockSpec((CHUNK, HPAD), lambda c: (c, 0))]
                 + _SSD_VEC_SPECS(),
        out_specs=[pl.BlockSpec((CHUNK, D_INNER), lambda c: (c, 0)), pl.BlockSpec((None, SSM_N, D_INNER), lambda c: (c, 0, 0)),
                   pl.BlockSpec((CHUNK, SSM_HEADS * CHUNK), lambda c: (c, 0)), pl.BlockSpec((CHUNK, D_INNER), lambda c: (c, 0)),
                   pl.BlockSpec((CHUNK, D_INNER), lambda c: (c, 0))],
        out_shape=[jax.ShapeDtypeStruct((s, D_INNER), F32), jax.ShapeDtypeStruct((nc, SSM_N, D_INNER), F32),
                   jax.ShapeDtypeStruct((s, SSM_HEADS * CHUNK), F32), jax.ShapeDtypeStruct((s, D_INNER), F32),
                   jax.ShapeDtypeStruct((s, D_INNER), F32)],
        scratch_shapes=[pltpu.VMEM((SSM_N, D_INNER), F32), pltpu.VMEM((HPAD, CHUNK), F32)],
        compiler_params=_cp(VMEM_BIG))(xbc, dtc, bias_row, alog_row, dfull)


def _ssd_bwd(xbc, dtc, bias_row, alog_row, dfull, dy, states, expansions, name):
    s = xbc.shape[0]
    nc = s // CHUNK
    rev = lambda c: nc - 1 - c

    def body(xbc_ref, dtc_ref, br_ref, ar_ref, df_ref, dy_ref, st_ref, csb_ref, csf_ref, dtf_ref,
             dxbc_ref, ddt_ref, dalog_ref, dd_ref, dbias_ref,
             dht_ref, cst_ref, ddf_ref, dxs_ref, dcsf_ref, dcsl_ref):
        step = pl.program_id(0)

        @pl.when(step == 0)
        def _():
            dht_ref[...] = jnp.zeros_like(dht_ref)
            ddf_ref[...] = jnp.zeros_like(ddf_ref)
            dalog_ref[...] = jnp.zeros_like(dalog_ref)
            dbias_ref[...] = jnp.zeros_like(dbias_ref)
            dd_ref[...] = jnp.zeros_like(dd_ref)

        a_row, dt_c, _, dt_full, e_full, f_full, gamma, e64 = _ssd_common(
            dtc_ref, br_ref, ar_ref, csb_ref, cst_ref, csf_ref, dtf_ref, expand=False)
        x = xbc_ref[:, :D_INNER]
        xdt = x * dt_full
        dy_all = dy_ref[...]
        ddf_ref[...] += jnp.broadcast_to(jnp.sum(dy_all * x, axis=0, keepdims=True), ddf_ref.shape)
        causal = _iota((CHUNK, CHUNK), 0) >= _iota((CHUNK, CHUNK), 1)
        lo = _iota((CHUNK, CHUNK), 1) < SSM_P
        head_lane = _iota((CHUNK, HPAD), 1)
        head_row = _iota((HPAD, CHUNK), 0)
        dcs_heads = jnp.zeros((CHUNK, HPAD), F32)
        dcs_cols = jnp.zeros((HPAD, CHUNK), F32)
        for g in range(SSM_GROUPS):
            gs = slice(g * SSM_GW, (g + 1) * SSM_GW)
            b0 = D_INNER + g * SSM_N
            c0 = D_INNER + SSM_GROUPS * SSM_N + g * SSM_N
            bg = xbc_ref[:, b0:b0 + SSM_N]
            cg = xbc_ref[:, c0:c0 + SSM_N]
            ht = st_ref[:, gs]
            dht = dht_ref[:, gs]
            dyg = dy_all[:, gs]
            eg, fg, gg = e_full[:, gs], f_full[:, gs], gamma[:, gs]
            z = _dot(cg, ht)
            dz = dyg * eg
            dcg = _dot_nt(dz, ht)
            dht_new = _dot_tn(cg, dz) + gg * dht
            xf = xdt[:, gs] * fg
            dxf = _dot(bg, dht)
            dbg = _dot_nt(xf, dht)
            dff = dxf * xf
            dcsf_ref[:, gs] = dyg * eg * z - dff
            dcsl_ref[:, gs] = jnp.broadcast_to(
                jnp.sum(dff, axis=0, keepdims=True) + jnp.sum(dht * ht, axis=0, keepdims=True) * gg, (8, SSM_GW))
            cb = _dot_nt(cg, bg)
            dcb = jnp.zeros((CHUNK, CHUNK), F32)
            for jp in range(SSM_GW // CHUNK):
                j = g * (SSM_GW // CHUNK) + jp
                ps = slice(j * CHUNK, (j + 1) * CHUNK)
                x2 = xdt[:, ps]
                dy2 = dy_all[:, ps]
                dxh = []
                for hh in range(2):
                    h = 2 * j + hh
                    lam = _ssd_lambda(csb_ref, cst_ref, h, causal)
                    mh = cb * lam
                    dyh = jnp.where(lo, dy2, 0.0) if hh == 0 else jnp.where(lo, 0.0, dy2)
                    dm = _dot_nt(dyh, x2)
                    dcb = dcb + dm * lam
                    gm = dm * mh
                    dcs_heads = dcs_heads + jnp.where(head_lane == h, jnp.sum(gm, axis=1, keepdims=True), 0.0)
                    dcs_cols = dcs_cols + jnp.where(head_row == h, jnp.sum(gm, axis=0, keepdims=True), 0.0)
                    dxh.append(_dot_tn(mh, dy2))
                dxs_ref[:, ps] = jnp.where(lo, dxh[0], dxh[1]) + dxf[:, jp * CHUNK:(jp + 1) * CHUNK] * fg[:, jp * CHUNK:(jp + 1) * CHUNK]
            dxbc_ref[:, b0:b0 + SSM_N] = (dbg + _dot_tn(dcb, cg)).astype(dxbc_ref.dtype)
            dxbc_ref[:, c0:c0 + SSM_N] = (dcg + _dot(dcb, bg)).astype(dxbc_ref.dtype)
            dht_ref[:, gs] = dht_new
        dxs = dxs_ref[...]
        dcs_heads = dcs_heads - dcs_cols.T + _dot_sel(dcsf_ref[...], e64, ((1,), (1,)))
        dcs_last = _dot_sel(dcsl_ref[...], e64, ((1,), (1,)))
        dcs_heads = dcs_heads + jnp.where(_iota((CHUNK, HPAD), 0) == CHUNK - 1, dcs_last[0:1, :], 0.0)
        triu = _iota((CHUNK, CHUNK), 0) <= _iota((CHUNK, CHUNK), 1)
        dda = _sel_dot(triu, dcs_heads)
        ddt = dda * a_row + _dot_sel(dxs * x, e64, ((1,), (1,)))
        dxbc_ref[:, :D_INNER] = (dxs * dt_full + dy_all * df_ref[...]).astype(dxbc_ref.dtype)
        dalog_ref[...] += jnp.sum(dda * dt_c, axis=0, keepdims=True) * a_row
        ddt_raw = ddt * _sigmoid(dtc_ref[...] + br_ref[...])
        ddt_ref[...] = ddt_raw.astype(ddt_ref.dtype)
        dbias_ref[...] += jnp.sum(ddt_raw, axis=0, keepdims=True)

        @pl.when(step == nc - 1)
        def _():
            dd_ref[...] = _dot_sel(ddf_ref[...], e64, ((1,), (1,)))[0:1, :]

    vec = pl.BlockSpec((1, HPAD), lambda c: (0, 0))
    return pl.pallas_call(
        body, name=name, grid=(nc,),
        in_specs=[pl.BlockSpec((CHUNK, CONV_DIM), lambda c: (rev(c), 0)), pl.BlockSpec((CHUNK, HPAD), lambda c: (rev(c), 0))]
                 + _SSD_VEC_SPECS()
                 + [pl.BlockSpec((CHUNK, D_INNER), lambda c: (rev(c), 0)),
                    pl.BlockSpec((None, SSM_N, D_INNER), lambda c: (rev(c), 0, 0)),
                    pl.BlockSpec((CHUNK, SSM_HEADS * CHUNK), lambda c: (rev(c), 0)),
                    pl.BlockSpec((CHUNK, D_INNER), lambda c: (rev(c), 0)), pl.BlockSpec((CHUNK, D_INNER), lambda c: (rev(c), 0))],
        out_specs=[pl.BlockSpec((CHUNK, CONV_DIM), lambda c: (rev(c), 0)), pl.BlockSpec((CHUNK, HPAD), lambda c: (rev(c), 0)),
                   vec, vec, vec],
        out_shape=[jax.ShapeDtypeStruct((s, CONV_DIM), F32), jax.ShapeDtypeStruct((s, HPAD), _ACT),
                   jax.ShapeDtypeStruct((1, HPAD), F32), jax.ShapeDtypeStruct((1, HPAD), F32),
                   jax.ShapeDtypeStruct((1, HPAD), F32)],
        scratch_shapes=[pltpu.VMEM((SSM_N, D_INNER), F32), pltpu.VMEM((HPAD, CHUNK), F32),
                        pltpu.VMEM((8, D_INNER), F32), pltpu.VMEM((CHUNK, D_INNER), F32),
                        pltpu.VMEM((CHUNK, D_INNER), F32), pltpu.VMEM((8, D_INNER), F32)],
        compiler_params=_cp(VMEM_BIG))(xbc, dtc, bias_row, alog_row, dfull, dy, states, *expansions)


def _gate_fwd(y, proj, gn, name, tm=512):
    s = y.shape[0]
    tm = min(tm, s)

    def body(y_ref, z_ref, gn_ref, o_ref):
        for g in range(SSM_GROUPS):
            gs = slice(g * SSM_GW, (g + 1) * SSM_GW)
            z = z_ref[:, gs]
            t = y_ref[:, gs] * (z * _sigmoid(z))
            r = lax.rsqrt(jnp.mean(t * t, axis=-1, keepdims=True) + EPS)
            o_ref[:, gs] = (t * r * gn_ref[:, gs]).astype(o_ref.dtype)

    row = pl.BlockSpec((tm, D_INNER), lambda i: (i, 0))
    return pl.pallas_call(
        body, name=name, grid=(s // tm,), in_specs=[row, row, pl.BlockSpec((1, D_INNER), lambda i: (0, 0))],
        out_specs=row, out_shape=jax.ShapeDtypeStruct((s, D_INNER + X_WIDTH), _ACT),
        compiler_params=_cp(VMEM_BIG))(y, proj, gn)


def _gate_bwd(y, proj, gn, dcat, name, tm=512):
    s = y.shape[0]
    tm = min(tm, s)

    def body(y_ref, z_ref, gn_ref, dm_ref, dy_ref, dz_ref, dgn_ref):
        @pl.when(pl.program_id(0) == 0)
        def _():
            dgn_ref[...] = jnp.zeros_like(dgn_ref)

        for g in range(SSM_GROUPS):
            gs = slice(g * SSM_GW, (g + 1) * SSM_GW)
            z = z_ref[:, gs]
            yv = y_ref[:, gs]
            sig = _sigmoid(z)
            sz = z * sig
            t = yv * sz
            r = lax.rsqrt(jnp.mean(t * t, axis=-1, keepdims=True) + EPS)
            th = t * r
            dm = dm_ref[:, gs].astype(F32)
            dmg = dm * gn_ref[:, gs]
            dt_ = r * (dmg - th * jnp.mean(dmg * th, axis=-1, keepdims=True))
            dgn_ref[:, gs] += jnp.sum(dm * th, axis=0, keepdims=True)
            dy_ref[:, gs] = dt_ * sz
            dz_ref[:, gs] = (dt_ * yv * (sig * (1.0 + z * (1.0 - sig)))).astype(dz_ref.dtype)

    row = pl.BlockSpec((tm, D_INNER), lambda i: (i, 0))
    vec = pl.BlockSpec((1, D_INNER), lambda i: (0, 0))
    return pl.pallas_call(
        body, name=name, grid=(s // tm,), in_specs=[row, row, vec, row], out_specs=[row, row, vec],
        out_shape=[jax.ShapeDtypeStruct((s, D_INNER), F32), jax.ShapeDtypeStruct((s, 6 * D_MODEL), _ACT),
                   jax.ShapeDtypeStruct((1, D_INNER), F32)], compiler_params=_cp(VMEM_BIG))(y, proj, gn, dcat)


def _block_of(kind, width):
    if kind == "col":
        return lambda ref, j: ref.at[:, :, pl.ds(pl.multiple_of(j * width, 128), width)]
    if kind == "row":
        return lambda ref, j: ref.at[:, pl.ds(pl.multiple_of(j * width, 8), width), :]
    return lambda ref, j: ref.at[j]


def _coords():
    return lax.axis_index("x"), lax.axis_index("y"), lax.axis_index("c")


def _rel_chip(x, y, k):
    return (1 - x if k & 1 else x), (1 - y if k & 2 else y)


def _all_gather_body(ins, outs, send_sems, recv_sems, local_sems, blocks):
    n = len(ins)
    x, y, c = _coords()
    sibling = (x, y, 1 - c)
    via = (x + (1 - c) * (1 - 2 * x), y + c * (1 - 2 * y))
    onto = (x + c * (1 - 2 * x), y + (1 - c) * (1 - 2 * y))

    def copy(t, k, chip, core, to, src=None):
        dst = blocks[t](outs[t], 4 * chip[0] + 2 * chip[1] + core)
        return pltpu.make_async_remote_copy(
            src_ref=dst if src is None else src, dst_ref=dst, send_sem=send_sems.at[t, k],
            recv_sem=recv_sems.at[t, k], device_id=to, device_id_type=MESH)

    started = []
    for t in range(n):
        mine = pltpu.make_async_copy(ins[t], blocks[t](outs[t], 4 * x + 2 * y + c), local_sems.at[t])
        mine.start()
        started.append(mine)
    sends = []
    for t in range(n):
        for k in range(3):
            px, py = _rel_chip(x, y, k)
            cp = copy(t, k, (x, y), c, (px, py, 1 - c if k == 0 else c), src=ins[t])
            cp.start()
            sends.append(cp)
    for t in range(n):
        for k in (1, 2):
            chip = _rel_chip(x, y, k)
            copy(t, k, chip, c, sibling).wait_recv()
            fwd = copy(t, 3 + k, chip, c, sibling)
            fwd.start()
            sends.append(fwd)
        hop = copy(t, 3, via, c, (*onto, c))
        hop.start()
        sends.append(hop)
    for t in range(n):
        diagonal = _rel_chip(x, y, 3)
        copy(t, 3, diagonal, c, sibling).wait_recv()
        fwd = copy(t, 6, diagonal, c, sibling)
        fwd.start()
        sends.append(fwd)
    for t in range(n):
        copy(t, 0, (x, y), 1 - c, sibling).wait_recv()
        for k in range(1, 4):
            copy(t, 3 + k, _rel_chip(x, y, k), 1 - c, sibling).wait_recv()
    for cp in sends:
        cp.wait_send()
    for mine in started:
        mine.wait()


def _handshake(peers):
    barrier = pltpu.get_barrier_semaphore()
    for peer in peers:
        pl.semaphore_signal(barrier, inc=1, device_id=peer, device_id_type=MESH)
    pl.semaphore_wait(barrier, len(peers))


def _gather_peers():
    x, y, c = _coords()
    return [(x, y, 1 - c)] + [(*_rel_chip(x, y, k), c) for k in (1, 2)]


SEQ_ID_GATHER, SEQ_ID_SIBLING, SEQ_ID_CHIPS = 1, 2, 3


def _sequencer_call(body, peers, operands, out_types, sems, name, collective_id, after=()):
    n_in, n_out, n_after = len(operands), len(out_types), len(after)

    def launch(*refs):
        _handshake(peers())
        body(refs[:n_in], refs[n_in + n_after:n_in + n_after + n_out], *refs[n_in + n_after + n_out:])

    return pl.kernel(
        launch, name=name, out_type=out_types, mesh=plsc.ScalarSubcoreMesh(axis_name="seq", num_cores=1),
        scratch_types=sems, compiler_params=pltpu.CompilerParams(collective_id=collective_id))(*operands, *after)


def _all_gather_seq(shards, layouts, name, after=()):
    n = len(shards)
    blocks = [_block_of(kind, width) for kind, width, _ in layouts]
    return _sequencer_call(
        lambda ins, outs, *sems: _all_gather_body(ins, outs, *sems, blocks), _gather_peers, shards,
        [jax.ShapeDtypeStruct(shape, sh.dtype) for sh, (_, _, shape) in zip(shards, layouts)],
        [pltpu.SemaphoreType.DMA((n, 7)), pltpu.SemaphoreType.DMA((n, 7)), pltpu.SemaphoreType.DMA((n,))],
        name, SEQ_ID_GATHER, after)


def _tie(small, after):
    return lax.optimization_barrier((small, *after))[0]


def _rs_to_sibling(grads, layouts, name, after=()):
    n = len(grads)
    blocks = [_block_of(kind, width) for kind, width, _ in layouts]

    def body(ins, outs, send_sems, recv_sems):
        x, y, c = _coords()
        sibling = (x, y, 1 - c)
        cps = []
        for t in range(n):
            for k in range(4):
                px, py = _rel_chip(x, y, k)
                cp = pltpu.make_async_remote_copy(
                    src_ref=blocks[t](ins[t], 4 * px + 2 * py + (1 - c)), dst_ref=outs[t].at[k],
                    send_sem=send_sems.at[t, k], recv_sem=recv_sems.at[t, k], device_id=sibling, device_id_type=MESH)
                cp.start()
                cps.append(cp)
        for cp in cps:
            cp.wait_recv()
        for cp in cps:
            cp.wait_send()

    def sibling_only():
        x, y, c = _coords()
        return [(x, y, 1 - c)]

    return _sequencer_call(
        body, sibling_only, grads,
        [jax.ShapeDtypeStruct((4,) + shape, g.dtype) for g, (_, _, shape) in zip(grads, layouts)],
        [pltpu.SemaphoreType.DMA((n, 4)), pltpu.SemaphoreType.DMA((n, 4))], name, SEQ_ID_SIBLING, after)


def _rs_chip_sum(grad, recv, layout, xyc, name):
    kind, width, shape = layout
    r, ccols = shape

    def src_index(step, xyc_ref):
        k = step + 1
        px = jnp.where(k % 2 == 1, 1 - xyc_ref[0], xyc_ref[0])
        py = jnp.where(k // 2 == 1, 1 - xyc_ref[1], xyc_ref[1])
        return 4 * px + 2 * py + xyc_ref[2]

    if kind == "col":
        g_spec = pl.BlockSpec((r, ccols), lambda k, s_: (0, src_index(k, s_)))
    elif kind == "row":
        g_spec = pl.BlockSpec((r, ccols), lambda k, s_: (src_index(k, s_), 0))
    else:
        g_spec = pl.BlockSpec((None, r, ccols), lambda k, s_: (src_index(k, s_), 0, 0))

    def body(xyc_ref, g_ref, r_ref, o_ref):
        o_ref[...] = (g_ref[...].astype(F32) + r_ref[...].astype(F32)).astype(o_ref.dtype)

    slot = pl.BlockSpec((None, r, ccols), lambda k, s_: (k + 1, 0, 0))
    return pl.pallas_call(
        body, name=name,
        grid_spec=pltpu.PrefetchScalarGridSpec(num_scalar_prefetch=1, grid=(3,), in_specs=[g_spec, slot], out_specs=slot),
        out_shape=jax.ShapeDtypeStruct((4, r, ccols), grad.dtype), compiler_params=_cp(VMEM_BIG))(xyc, grad, recv)


def _rs_across_chips(parts, name):
    n = len(parts)

    def body(ins, outs, send_sems, recv_sems):
        x, y, c = _coords()
        cps = []
        for t in range(n):
            for k in range(1, 4):
                px, py = _rel_chip(x, y, k)
                cp = pltpu.make_async_remote_copy(
                    src_ref=ins[t].at[k], dst_ref=outs[t].at[k - 1], send_sem=send_sems.at[t, k - 1],
                    recv_sem=recv_sems.at[t, k - 1], device_id=(px, py, c), device_id_type=MESH)
                cp.start()
                cps.append(cp)
        for cp in cps:
            cp.wait_recv()
        for cp in cps:
            cp.wait_send()

    def other_chips():
        x, y, c = _coords()
        return [(*_rel_chip(x, y, k), c) for k in range(1, 4)]

    return _sequencer_call(
        body, other_chips, parts, [jax.ShapeDtypeStruct((3,) + p.shape[1:], p.dtype) for p in parts],
        [pltpu.SemaphoreType.DMA((n, 3)), pltpu.SemaphoreType.DMA((n, 3))], name, SEQ_ID_CHIPS)


def _adamw_math(w, g, m, v):
    m = ADAM_B1 * m + (1.0 - ADAM_B1) * g
    v = ADAM_B2 * v + (1.0 - ADAM_B2) * jnp.square(g)
    m_hat = m / (1.0 - ADAM_B1 ** ADAM_STEP)
    v_hat = v / (1.0 - ADAM_B2 ** ADAM_STEP)
    delta = -ADAM_LR * (m_hat / (jnp.sqrt(v_hat) + ADAM_EPS) + ADAM_WD * w)
    return delta, m, v


def _row_tile(rows, cap):
    best = None
    for cand in range(8, min(rows, cap) + 1, 8):
        if rows % cand == 0:
            best = cand
    assert best is not None, rows
    return best


def _adamw(w, m, v, own, parts, me, name, layer, prev=None, tr=256):
    r, ccols = w.shape[-2:]
    npart = len(parts)
    if r % 8 == 0:
        tr, tc = _row_tile(r, tr), ccols
        steps, at = r // tr, (lambda i: (i, 0))
    else:
        tr, tc = r, 256
        assert ccols % tc == 0
        steps, at = ccols // tc, (lambda i: (0, i))

    def spec(lead):
        return pl.BlockSpec((None, tr, tc), lambda i, me_ref: (lead,) + at(i))

    grad, kind = own
    if kind == "col":
        own_spec = pl.BlockSpec((tr, tc), lambda i, me_ref: (at(i)[0], me_ref[0]))
    elif kind == "row":
        own_spec = pl.BlockSpec((tr, tc), lambda i, me_ref: (me_ref[0] * (r // tr) + at(i)[0], 0))
    else:
        own_spec = pl.BlockSpec((None, tr, tc), lambda i, me_ref: (me_ref[0],) + at(i))

    def body(me_ref, *refs):
        w_ref, m_ref, v_ref = refs[:3]
        p_refs = refs[3:4 + npart]
        outs = refs[len(refs) - 4:]
        g = p_refs[0][...].astype(F32)
        for p_ref in p_refs[1:]:
            g = g + p_ref[...].astype(F32)
        delta, mn, vn = _adamw_math(w_ref[...], g, m_ref[...], v_ref[...])
        outs[0][...] = g
        outs[1][...] = delta
        outs[2][...] = mn
        outs[3][...] = vn

    operands = [w, m, v, grad] + [p for p, _ in parts]
    in_specs = [spec(layer)] * 3 + [own_spec] + [spec(lead) for _, lead in parts]
    aliases = {}
    if prev is not None:
        for i, p in enumerate(prev):
            aliases[1 + len(operands)] = i
            operands.append(p)
            in_specs.append(pl.BlockSpec(memory_space=pl.ANY))
    return pl.pallas_call(
        body, name=name,
        grid_spec=pltpu.PrefetchScalarGridSpec(num_scalar_prefetch=1, grid=(steps,), in_specs=in_specs,
                                               out_specs=[spec(layer)] * 4),
        out_shape=[jax.ShapeDtypeStruct(w.shape, F32)] * 4, input_output_aliases=aliases,
        compiler_params=_cp(VMEM_BIG))(me, *operands)


def _small_update(gathered, params, loss_all, me, name):
    n = len(gathered)
    shapes = [w.shape for w, _, _ in params]

    def body(me_ref, *refs):
        g_refs, loss_ref = refs[:n], refs[n]
        p_refs = refs[n + 1:n + 1 + 3 * n]
        o_refs = refs[n + 1 + 3 * n:]
        for i in range(n):
            r, c = shapes[i]
            if gathered[i].shape[2] == c:
                parts = [g_refs[i][j] for j in range(N_DEV)]
            else:
                off = pl.multiple_of(me_ref[0] * c, 128)
                parts = [g_refs[i][j, :, pl.ds(off, c)] for j in range(N_DEV)]
            g = functools.reduce(lambda a, b: a + b, parts)
            delta, mn, vn = _adamw_math(p_refs[3 * i][...], g, p_refs[3 * i + 1][...], p_refs[3 * i + 2][...])
            for k, val in enumerate((g, delta, mn, vn)):
                o_refs[4 * i + k][...] = val
        o_refs[4 * n][...] = functools.reduce(lambda a, b: a + b, [loss_ref[j] for j in range(N_DEV)])

    vmem = pl.BlockSpec(memory_space=pltpu.VMEM)
    flat_params = [a for p in params for a in p]
    outs = pl.pallas_call(
        body, name=name, in_specs=[pl.BlockSpec(memory_space=pltpu.SMEM)] + [vmem] * (n + 1 + 3 * n),
        out_specs=[vmem] * (4 * n + 1),
        out_shape=[jax.ShapeDtypeStruct(shp, F32) for shp in shapes for _ in range(4)] + [jax.ShapeDtypeStruct((1, 128), F32)],
        compiler_params=_cp(VMEM_BIG))(me, *gathered, loss_all, *flat_params)
    return [tuple(outs[4 * i:4 * i + 4]) for i in range(n)], outs[4 * n]


def _pack(arrays):
    pieces, layout, off = [], [], 0
    for a in arrays:
        n = a.size
        padded = -(-n // 1024) * 1024
        flat = a.reshape(-1).astype(F32)
        if padded != n:
            flat = jnp.pad(flat, (0, padded - n))
        pieces.append(flat.reshape(padded // 128, 128))
        layout.append((off, n, a.shape))
        off += padded // 128
    return jnp.concatenate(pieces, axis=0), layout


def kernel(x, mem, norm_mix, norm_ffn, mem_norm, w_kv, w_out, w_ffn1, w_ffn2, a_in, a_ln_g, a_ln_b, a_ws, a_bs, b_in, b_conv_w, b_conv_b, b_dt_bias, b_a_log, b_d, b_gnorm, final_norm, loss_target, m_norm_mix, m_norm_ffn, m_mem_norm, m_w_kv, m_w_out, m_w_ffn1, m_w_ffn2, m_a_in, m_a_ln_g, m_a_ln_b, m_a_ws, m_a_bs, m_b_in, m_b_conv_w, m_b_conv_b, m_b_dt_bias, m_b_a_log, m_b_d, m_b_gnorm, m_final_norm, v_norm_mix, v_norm_ffn, v_mem_norm, v_w_kv, v_w_out, v_w_ffn1, v_w_ffn2, v_a_in, v_a_ln_g, v_a_ln_b, v_a_ws, v_a_bs, v_b_in, v_b_conv_w, v_b_conv_b, v_b_dt_bias, v_b_a_log, v_b_d, v_b_gnorm, v_final_norm):
    s = x.shape[1]
    xs = x.reshape(s, D_MODEL)
    mems = mem.reshape(N_MEM, D_MODEL)
    target = loss_target.reshape(s, D_MODEL)
    ax, ay, ac = lax.axis_index("x"), lax.axis_index("y"), lax.axis_index("c")
    me = 4 * ax + 2 * ay + ac
    xyc = jnp.stack([ax, ay, ac]).astype(jnp.int32)
    me1 = me.astype(jnp.int32).reshape(1)

    b_cols = b_in.shape[2]
    act = lambda a: a.astype(_ACT)
    lay_f1, lay_f2 = ("col", 512, (1, D_MODEL, D_FF)), ("row", 512, (1, D_FF, D_MODEL))
    lay_out, lay_kv = ("row", 384, (1, 3 * D_MODEL, D_MODEL)), ("col", 256, (1, D_MODEL, 2 * X_WIDTH))
    small_w_pack = _pack([b_conv_w[0], b_conv_b[0], b_gnorm[0]])[0]
    (WA,) = _all_gather_seq([act(a_in)], [("col", 640, (1, D_MODEL, 5 * D_MODEL))], "ag_proj_a")
    wo0, wkv0 = _all_gather_seq([act(w_out[0:1]), act(w_kv[0:1])], [lay_out, lay_kv], "ag_out0")
    w1_0, w2_0 = _all_gather_seq([act(w_ffn1[0:1]), act(w_ffn2[0:1])], [lay_f1, lay_f2], "ag_ffn0")
    a0 = _rms_fwd(xs, norm_mix[0].reshape(1, -1), "mix_norm0")
    tr_b = lambda a: jnp.swapaxes(a, 1, 2)
    wbt_blk, small_w = _all_gather_seq(
        [act(tr_b(b_in)[0]), small_w_pack],
        [("blk", 0, (N_DEV, b_cols, D_MODEL)), ("blk", 0, (N_DEV, 32, 128))], "ag_proj_b", after=[a0])
    wo1, wkv1 = _all_gather_seq([act(w_out[1:2]), act(w_kv[1:2])], [lay_out, lay_kv], "ag_out1", after=[a0])
    w1_1, w2_1 = _all_gather_seq([act(w_ffn1[1:2]), act(w_ffn2[1:2])], [lay_f1, lay_f2], "ag_ffn1", after=[a0])
    W1, W2, WO, WKV = [w1_0, w1_1], [w2_0, w2_1], [wo0, wo1], [wkv0, wkv1]
    dt0 = D_INNER + CONV_DIM

    row = lambda a: a.reshape(1, -1)
    nmix = [row(norm_mix[0]), row(norm_mix[1])]
    nffn = [row(norm_ffn[0]), row(norm_ffn[1])]
    nmem = [row(mem_norm[0]), row(mem_norm[1])]
    fin = row(final_norm)
    lng, lnb = a_ln_g.reshape(1, D_INNER), a_ln_b.reshape(1, D_INNER)
    ws = a_ws[0]
    bs3 = a_bs[0].reshape(A_GROUPS, CHUNK, 1)
    pad_h = lambda a: jnp.pad(a.reshape(-1), (0, HPAD - SSM_HEADS))
    bias_row = pad_h(b_dt_bias).reshape(1, HPAD)
    alog_row = pad_h(b_a_log).reshape(1, HPAD)
    dfull = jnp.repeat(b_d.reshape(-1), SSM_P).reshape(1, D_INNER)

    kvs, mns = [None, None], [None, None]

    def mem_kv(i, after=None):
        gain = nmem[i] if after is None else _tie(nmem[i], after)
        mns[i] = _rms_fwd(mems, gain, f"mem_norm{i}")
        kvs[i] = _mm(mns[i], WKV[i], m=N_MEM, n=2 * X_WIDTH, k=D_MODEL, b_at=(0, 0, 0), out_dtype=_ACT, name=f"kv{i}")

    def ffn_fwd(h, i, after=()):
        f = _rms_fwd(h, nffn[i], f"ffn_norm{i}")
        p = _mm(f, W1[i], m=s, n=D_FF, k=D_MODEL, b_at=(0, 0, 0), out_dtype=_ACT, after=after, name=f"ffn_up{i}")
        hn = _mm(p, W2[i], m=s, n=D_MODEL, k=D_FF, b_at=(0, 0, 0), a_pro="relu2", add=h, name=f"ffn_down{i}")
        return f, p, hn

    def out_proj(h, cat, i):
        return _mm(cat, WO[i], m=s, n=D_MODEL, k=3 * D_MODEL, b_at=(0, 0, 0), add=h, name=f"out_proj{i}")

    proj_a = _mm(a0, WA, m=s, n=5 * D_MODEL, k=D_MODEL, b_at=(0, 0, 0), name="proj_a")
    cat_a = _gmlp_fwd(proj_a, lng, lnb, ws, bs3, "gmlp_fwd")
    mem_kv(0, after=[cat_a])
    cat_a = _attn_fwd(proj_a, 4, kvs[0], cat_a, "attn_fwd0")
    h1 = out_proj(xs, cat_a, 0)

    wbt_blk, small_w, _ = lax.optimization_barrier((wbt_blk, small_w, h1))
    jd, lo = divmod(dt0, b_cols)
    assert lo + SSM_HEADS <= b_cols
    wbt_full = wbt_blk.reshape(N_DEV * b_cols, D_MODEL)
    WBT = jnp.concatenate([wbt_full[:dt0], wbt_full[dt0 + SSM_HEADS:]], axis=0)
    WBDT = jnp.pad(wbt_full[dt0:dt0 + SSM_HEADS], ((0, HPAD - SSM_HEADS), (0, 0)))
    cw_sh, cb_sh, gn_sh = 4 * 384, 384, 256
    sw = small_w.reshape(N_DEV, 32 * 128)
    conv_w = jnp.transpose(sw[:, :cw_sh].reshape(N_DEV, CONV_K, 384), (1, 0, 2)).reshape(CONV_K, CONV_DIM)
    conv_b = sw[:, 2048:2048 + cb_sh].reshape(1, CONV_DIM)
    gnorm = sw[:, 3072:3072 + gn_sh].reshape(1, D_INNER)

    f0, p0, h2 = ffn_fwd(h1, 0, after=[WBT, WBDT])
    a1 = _rms_fwd(h2, nmix[1], "mix_norm1")
    proj_b = _mm(a1, WBT, m=s, n=6 * D_MODEL, k=D_MODEL, tb=True, name="proj_b")
    dt_raw = _mm(a1, WBDT, m=s, n=HPAD, k=D_MODEL, tb=True, name="proj_dt")
    xbc = _conv_fwd(proj_b, conv_w, conv_b, "conv_fwd")
    y_ssd, states, *ssd_expansions = _ssd_fwd(xbc, dt_raw, bias_row, alog_row, dfull, "ssd_fwd")
    cat_b = _gate_fwd(y_ssd, proj_b, gnorm, "gate_fwd")
    mem_kv(1, after=[cat_b])
    cat_b = _attn_fwd(proj_b, 5, kvs[1], cat_b, "attn_fwd1")
    h3 = out_proj(h2, cat_b, 1)
    f1, p1, h4 = ffn_fwd(h3, 1)

    loss_part, dh, dh_act, d_fin = _loss_head(h4, fin, target, "loss_head")

    g_f1, g_f2, g_out, g_kv = [None, None], [None, None], [None, None], [None, None]
    d_nffn, d_nmix, d_nmem = [None, None], [None, None], [None, None]

    def ffn_bwd(dh, dh_act, h_in, f, p, i, after=(), after_last=()):
        dp = _mm(dh_act, W2[i], m=s, n=D_FF, k=D_MODEL, tb=True, b_at=(0, 0, 0), epi_p=p, out_dtype=_ACT, name=f"ffn_down_dx{i}")
        g_f2[i] = _mm(p, dh_act, m=D_FF, n=D_MODEL, k=s, ta=True, a_pro="relu2", out_dtype=_ACT, name=f"ffn_down_dw{i}")
        g_f1[i] = _mm(f, dp, m=D_MODEL, n=D_FF, k=s, ta=True, out_dtype=_ACT, name=f"ffn_up_dw{i}")
        df = _mm(dp, W1[i], m=s, n=D_MODEL, k=D_FF, tb=True, b_at=(0, 0, 0), after=after, name=f"ffn_up_dx{i}")
        gain = _tie(nffn[i], after_last) if after_last else nffn[i]
        dh_in, dh_in_act, d_nffn[i] = _rms_bwd(h_in, gain, df, dh, f"ffn_norm_bwd{i}")
        return dh_in, dh_in_act

    def out_bwd(dh_act, cat, i):
        dcat = _mm(dh_act, WO[i], m=s, n=3 * D_MODEL, k=D_MODEL, tb=True, b_at=(0, 0, 0), out_dtype=_ACT, name=f"out_dx{i}")
        g_out[i] = _mm(cat, dh_act, m=3 * D_MODEL, n=D_MODEL, k=s, ta=True, out_dtype=_ACT, name=f"out_dw{i}")
        return dcat

    def mem_bwd(dkv, i):
        g_kv[i] = _mm(mns[i], dkv, m=D_MODEL, n=2 * X_WIDTH, k=N_MEM, ta=True, out_dtype=_ACT, name=f"kv_dw{i}")
        dmn = _mm(dkv, WKV[i], m=N_MEM, n=D_MODEL, k=2 * X_WIDTH, tb=True, b_at=(0, 0, 0), name=f"kv_dx{i}")
        _, _, d_nmem[i] = _rms_bwd(mems, nmem[i], dmn, None, f"mem_norm_bwd{i}")

    lay_g = {"f1": ("col", 512, (D_MODEL, 512)), "f2": ("row", 512, (512, D_MODEL)), "out": ("row", 384, (384, D_MODEL)),
             "kv": ("col", 256, (D_MODEL, 256)), "a": ("col", 640, (D_MODEL, 640)), "b": ("blk", 0, (b_cols, D_MODEL))}
    reduced = {}

    def reduce_scatter(group, tag, after=(), sums_after=()):
        grads3, lays3 = [], []
        for fam, _, g in group:
            kind, width, shape = lay_g[fam]
            grads3.append(g if kind == "blk" else g.reshape((1,) + g.shape))
            lays3.append((kind, width, shape if kind == "blk" else (1,) + shape))
        recv1 = _rs_to_sibling(grads3, lays3, f"rs_sibling_{tag}", after)
        if sums_after:
            recv1 = lax.optimization_barrier((tuple(recv1), tuple(sums_after)))[0]
        recv1 = [recv1[t].reshape((4,) + lay_g[fam][2]) for t, (fam, _, _) in enumerate(group)]
        parts = [_rs_chip_sum(g, r1, lay_g[fam], xyc, f"rs_chip_sum_{fam}{i}") for r1, (fam, i, g) in zip(recv1, group)]
        recv2 = _rs_across_chips(parts, f"rs_chips_{tag}")
        for (fam, i, g), r1, r2 in zip(group, recv1, recv2):
            reduced[fam, i] = (g, r1, r2)
        return parts, recv2

    dh3, dh3_act = ffn_bwd(dh, dh_act, h3, f1, p1, 1)
    dcat_b = out_bwd(dh3_act, cat_b, 1)
    sums, got_ffn1 = reduce_scatter([("f1", 1, g_f1[1]), ("f2", 1, g_f2[1]), ("out", 1, g_out[1])], "ffn1", sums_after=[dcat_b])
    dy_ssd, dproj_b, d_gnorm = _gate_bwd(y_ssd, proj_b, gnorm, dcat_b, "gate_bwd")
    dproj_b, dkv_b = _attn_bwd(proj_b, 5, kvs[1], dcat_b, dproj_b, "attn_bwd1")
    mem_bwd(dkv_b, 1)
    dxbc, ddt_raw, d_alog, d_dskip, d_dtbias = _ssd_bwd(
        xbc, dt_raw, _tie(bias_row, sums), alog_row, dfull, dy_ssd, states, ssd_expansions, "ssd_bwd")
    dproj_b, d_convw, d_convb = _conv_bwd(proj_b, conv_w, _tie(conv_b, got_ffn1), dxbc, dproj_b, "conv_bwd")
    gb = _mm(dproj_b, a1, m=6 * D_MODEL, n=D_MODEL, k=s, ta=True, out_dtype=_ACT, name="proj_b_dw")
    gb_dt = _mm(ddt_raw, a1, m=HPAD, n=D_MODEL, k=s, ta=True, out_dtype=_ACT, name="proj_b_dw_dt")
    blocks_b = [gb[j * b_cols:(j + 1) * b_cols] for j in range(jd)]
    blocks_b.append(jnp.concatenate([gb[jd * b_cols:dt0], gb_dt[:SSM_HEADS], gb[dt0:(jd + 1) * b_cols - SSM_HEADS]], axis=0))
    blocks_b += [gb[j * b_cols - SSM_HEADS:(j + 1) * b_cols - SSM_HEADS] for j in range(jd + 1, N_DEV)]
    gb_blk = jnp.stack(blocks_b)
    da1 = _mm(dproj_b, WBT, m=s, n=D_MODEL, k=6 * D_MODEL, name="proj_b_dx")
    sums, got_mix1 = reduce_scatter([("kv", 1, g_kv[1]), ("b", 0, gb_blk)], "mix1", sums_after=[da1])
    da1 = _mm(ddt_raw, WBDT, m=s, n=D_MODEL, k=HPAD, add=da1, name="proj_b_dx_dt")
    dh2, dh2_act, d_nmix[1] = _rms_bwd(h2, _tie(nmix[1], sums), da1, dh3, "mix_norm_bwd1")

    dh1, dh1_act = ffn_bwd(dh2, dh2_act, h1, f0, p0, 0, after=got_ffn1, after_last=got_mix1)
    dcat_a = out_bwd(dh1_act, cat_a, 0)
    sums, got_ffn0 = reduce_scatter([("f1", 0, g_f1[0]), ("f2", 0, g_f2[0]), ("out", 0, g_out[0])], "ffn0", sums_after=[dcat_a])
    dproj_a, d_ws, d_bs3, d_lng, d_lnb = _gmlp_bwd(proj_a, dcat_a, _tie(lng, sums), lnb, ws, bs3, "gmlp_bwd")
    dproj_a, dkv_a = _attn_bwd(proj_a, 4, kvs[0], dcat_a, dproj_a, "attn_bwd0")
    mem_bwd(dkv_a, 0)

    def big_update(w, m, v, fam, nlayer):
        res = None
        for i in range(nlayer):
            grad, recv1, recv2 = reduced[fam, i]
            plist = [(recv1, 0), (recv2, 0), (recv2, 1), (recv2, 2)]
            res = _adamw(w, m, v, (grad, lay_g[fam][0]), plist, me1, f"adamw_{fam}{i}", layer=i, prev=res)
        return res

    da0 = _mm(dproj_a, WA, m=s, n=D_MODEL, k=5 * D_MODEL, tb=True, b_at=(0, 0, 0), name="proj_a_dx")
    grad_x, _, d_nmix[0] = _rms_bwd(xs, nmix[0], da0, dh1, "mix_norm_bwd0")
    ga = _mm(a0, dproj_a, m=D_MODEL, n=5 * D_MODEL, k=s, ta=True, out_dtype=_ACT, after=[grad_x], name="proj_a_dw")
    r_b = big_update(tr_b(b_in), tr_b(m_b_in), tr_b(v_b_in), "b", 1)
    reduce_scatter([("kv", 0, g_kv[0]), ("a", 0, ga)], "mix0", after=got_ffn0, sums_after=r_b)
    r_b = [tr_b(o) for o in r_b]

    small_names = ["norm_mix", "norm_ffn", "mem_norm", "a_ln_g", "a_ln_b", "a_ws", "a_bs", "b_dt_bias", "b_a_log", "b_d",
                   "final_norm", "b_conv_w", "b_conv_b", "b_gnorm"]
    small_grads = [jnp.concatenate(d_nmix, axis=0), jnp.concatenate(d_nffn, axis=0), jnp.concatenate(d_nmem, axis=0),
                   d_lng, d_lnb, d_ws.reshape(A_GROUPS * CHUNK, CHUNK), d_bs3.reshape(A_GROUPS, CHUNK),
                   d_dtbias[:, :SSM_HEADS], d_alog[:, :SSM_HEADS], d_dskip[:, :SSM_HEADS], d_fin,
                   d_convw, d_convb, d_gnorm]
    small_2d = [(2, D_MODEL)] * 3 + [(1, D_INNER)] * 2 + [(A_GROUPS * CHUNK, CHUNK), (A_GROUPS, CHUNK)] + [(1, SSM_HEADS)] * 3 \
        + [(1, D_MODEL), (CONV_K, 384), (1, 384), (1, 256)]
    gathered = _all_gather_seq(
        small_grads + [loss_part], [("blk", 0, (N_DEV,) + g.shape) for g in small_grads + [loss_part]], "ag_small_grads")

    r_f1 = big_update(w_ffn1, m_w_ffn1, v_w_ffn1, "f1", 2)
    r_f2 = big_update(w_ffn2, m_w_ffn2, v_w_ffn2, "f2", 2)
    r_out = big_update(w_out, m_w_out, v_w_out, "out", 2)
    r_kv = big_update(w_kv, m_w_kv, v_w_kv, "kv", 2)
    r_a = big_update(a_in, m_a_in, v_a_in, "a", 1)

    small_w = [norm_mix, norm_ffn, mem_norm, a_ln_g, a_ln_b, a_ws, a_bs, b_dt_bias, b_a_log, b_d, final_norm,
               b_conv_w, b_conv_b, b_gnorm]
    small_m = [m_norm_mix, m_norm_ffn, m_mem_norm, m_a_ln_g, m_a_ln_b, m_a_ws, m_a_bs, m_b_dt_bias, m_b_a_log, m_b_d,
               m_final_norm, m_b_conv_w, m_b_conv_b, m_b_gnorm]
    small_v = [v_norm_mix, v_norm_ffn, v_mem_norm, v_a_ln_g, v_a_ln_b, v_a_ws, v_a_bs, v_b_dt_bias, v_b_a_log, v_b_d,
               v_final_norm, v_b_conv_w, v_b_conv_b, v_b_gnorm]
    params = [tuple(a.reshape(shp) for a in wmv) for shp, wmv in zip(small_2d, zip(small_w, small_m, small_v))]
    loss_all = _tie(gathered[-1], [r_a[0], r_kv[0]])
    small_res, loss_sum = _small_update(gathered[:-1], params, loss_all, me1, "adamw_small")
    loss = loss_sum[0, 0]

    names = ["norm_mix", "norm_ffn", "mem_norm", "w_kv", "w_out", "w_ffn1", "w_ffn2", "a_in", "a_ln_g", "a_ln_b", "a_ws",
             "a_bs", "b_in", "b_conv_w", "b_conv_b", "b_dt_bias", "b_a_log", "b_d", "b_gnorm", "final_norm"]
    big = {"w_kv": r_kv, "w_out": r_out, "w_ffn1": r_f1, "w_ffn2": r_f2, "a_in": r_a, "b_in": r_b}
    outs = [loss, grad_x.reshape(x.shape)]
    for kind in range(4):
        for nm in names:
            if nm in big:
                outs.append(big[nm][kind])
            else:
                i = small_names.index(nm)
                outs.append(small_res[i][kind].reshape(small_w[i].shape))
    return tuple(outs)
```

```python
import functools
import math

import jax
import jax.numpy as jnp
from jax import lax
from jax.experimental import pallas as pl
from jax.experimental.pallas import tpu as pltpu
from jax.experimental.pallas import tpu_sc as plsc

F32 = jnp.float32
_MXU = jnp.bfloat16
_ACT = jnp.bfloat16

D_MODEL = 1024
CHUNK = 128
N_MEM = 256
D_INNER = 2048
A_GROUPS = 8
A_GW = D_INNER // A_GROUPS
SSM_HEADS = 32
SSM_P = 64
SSM_GROUPS = 4
SSM_GW = D_INNER // SSM_GROUPS
SSM_N = 128
CONV_K = 4
CONV_DIM = 3072
X_HEADS = 4
X_HD = 256
X_WIDTH = 1024
D_FF = 4096
EPS = 1e-6
HPAD = 128
N_DEV = 8

ADAM_LR = 0.001
ADAM_B1 = 0.9
ADAM_B2 = 0.999
ADAM_EPS = 1e-08
ADAM_WD = 0.01
ADAM_STEP = 10

VMEM_BIG = 56 * 1024 * 1024
MESH = pl.DeviceIdType.MESH


def _cp(vmem=None):
    if vmem is None:
        return pltpu.CompilerParams()
    return pltpu.CompilerParams(vmem_limit_bytes=vmem)


def _dot(a, b, dims=((1,), (0,))):
    return lax.dot_general(a.astype(_MXU), b.astype(_MXU), (dims, ((), ())), preferred_element_type=F32)


def _dot_nt(a, b):
    return _dot(a, b, ((1,), (1,)))


def _dot_tn(a, b):
    return _dot(a, b, ((0,), (0,)))


def _split3(x):
    x1 = x.astype(jnp.bfloat16)
    r = x - x1.astype(F32)
    x2 = r.astype(jnp.bfloat16)
    x3 = (r - x2.astype(F32)).astype(jnp.bfloat16)
    return x1, x2, x3


def _dot_sel(x, sel, dims=((1,), (0,)), terms=2):
    sel = sel.astype(jnp.bfloat16)
    parts = [lax.dot_general(t, sel, (dims, ((), ())), preferred_element_type=F32) for t in _split3(x)[:terms]]
    return functools.reduce(lambda a, b: a + b, parts)


def _sel_dot(sel, x, dims=((1,), (0,))):
    sel = sel.astype(jnp.bfloat16)
    parts = [lax.dot_general(sel, t, (dims, ((), ())), preferred_element_type=F32) for t in _split3(x)]
    return (parts[0] + parts[1]) + parts[2]


def _sigmoid(x):
    return 1.0 / (1.0 + jnp.exp(-x))


def _gelu(x):
    return 0.5 * x * (1.0 + lax.erf(x * (1.0 / math.sqrt(2.0))))


def _gelu_with_grad(x):
    phi = 0.5 * (1.0 + lax.erf(x * (1.0 / math.sqrt(2.0))))
    return x * phi, phi + x * jnp.exp(-0.5 * x * x) * (1.0 / math.sqrt(2.0 * math.pi))


def _softplus(x):
    return jnp.maximum(x, 0.0) + jnp.log1p(jnp.exp(-jnp.abs(x)))


def _iota(shape, dim):
    return lax.broadcasted_iota(jnp.int32, shape, dim)


MM_VMEM_BUDGET = 40 * 1024 * 1024
HBM_BYTES_PER_S = 2.5e12
GRID_STEP_S = 0.35e-6
VMEM_ACC_BYTES_PER_S = 6e12


def _divisors(dim, unit):
    out = [d for d in range(unit, min(dim, 2048) + 1, unit) if dim % d == 0]
    return out if out else [dim]


def _mm_tiles(m, n, k, sa, sb, s_mn, a_pro, offsets):
    best = None
    (a_r0, a_c0, ta), (b_r0, b_c0, tb), (o_r0, o_c0) = offsets
    for tm in _divisors(m, 128):
        for tn in _divisors(n, 128):
            for tk in [k // d for d in (1, 2, 3, 4, 6, 8) if k % d == 0 and (k // d) % 128 == 0]:
                a_t = (tk, tm) if ta else (tm, tk)
                b_t = (tn, tk) if tb else (tk, tn)
                if a_r0 % a_t[0] or a_c0 % a_t[1] or b_r0 % b_t[0] or b_c0 % b_t[1] or o_r0 % tm or o_c0 % tn:
                    continue
                nk = k // tk
                vmem = 2 * (tm * tk * sa + tk * tn * sb + tm * tn * s_mn) + tm * tn * 4 * (2 if nk > 1 else 1)
                if a_pro or sa == 4:
                    vmem += tm * tk * 6
                if sb == 4:
                    vmem += tk * tn * 2
                if vmem > MM_VMEM_BUDGET:
                    continue
                gi, gj = m // tm, n // tn
                for j_inner in (True, False):
                    if nk > 1:
                        traffic = gj * m * k * sa + gi * k * n * sb
                    elif j_inner:
                        traffic = m * k * sa + gi * k * n * sb
                    else:
                        traffic = gj * m * k * sa + k * n * sb
                    traffic += m * n * s_mn + (tm * tk * sa + tk * tn * sb)
                    cost = traffic / HBM_BYTES_PER_S + gi * gj * nk * GRID_STEP_S
                    if nk > 1:
                        cost += m * n * 8 * nk / VMEM_ACC_BYTES_PER_S
                    if best is None or cost < best[0]:
                        best = (cost, tm, tn, tk, j_inner)
    assert best is not None, (m, n, k)
    return best[1:]


def _mm(a, b, *, m, n, k, name, ta=False, tb=False, a_at=(None, 0, 0), b_at=(None, 0, 0),
        out_dtype=F32, add=None, epi_p=None, epi_at=(None, 0, 0), out=None, out_at=(None, 0, 0),
        out_full=None, a_pro=None, after=()):
    s_mn =jnp.dtype(out.dtype if out is not None else out_dtype).itemsize
    s_mn += add.dtype.itemsize if add is not None else 0
    s_mn += epi_p.dtype.itemsize if epi_p is not None else 0
    tm, tn, tk, j_inner = _mm_tiles(m, n, k, a.dtype.itemsize, b.dtype.itemsize, s_mn, a_pro is not None,
                                    ((a_at[1], a_at[2], ta), (b_at[1], b_at[2], tb), (out_at[1], out_at[2])))
    nk = k // tk

    def spec(at, tr, tc, rsel, csel):
        lead, r0, c0 = at
        assert r0 % tr == 0 and c0 % tc == 0, (name, at, tr, tc)
        rb, cb = r0 // tr, c0 // tc
        if lead is None:
            return pl.BlockSpec((tr, tc), lambda g0, g1, kk: (rb + rsel(g0, g1, kk), cb + csel(g0, g1, kk)))
        return pl.BlockSpec((None, tr, tc), lambda g0, g1, kk: (lead, rb + rsel(g0, g1, kk), cb + csel(g0, g1, kk)))

    gi = (lambda g0, g1, kk: g0) if j_inner else (lambda g0, g1, kk: g1)
    gj = (lambda g0, g1, kk: g1) if j_inner else (lambda g0, g1, kk: g0)
    gk = lambda g0, g1, kk: kk
    a_spec = spec(a_at, tk, tm, gk, gi) if ta else spec(a_at, tm, tk, gi, gk)
    b_spec = spec(b_at, tn, tk, gj, gk) if tb else spec(b_at, tk, tn, gk, gj)
    dims = ((0,), (0,)) if ta else (((1,), (1,)) if tb else ((1,), (0,)))
    assert not (ta and tb)

    operands, in_specs = [a, b], [a_spec, b_spec]
    if add is not None:
        operands.append(add)
        in_specs.append(spec((None, 0, 0), tm, tn, gi, gj))
    if epi_p is not None:
        operands.append(epi_p)
        in_specs.append(spec(epi_at, tm, tn, gi, gj))
    aliases = {}
    if out is not None:
        aliases = {len(operands): 0}
        operands.append(out)
        in_specs.append(pl.BlockSpec(memory_space=pl.ANY))
        out_struct = jax.ShapeDtypeStruct(out.shape, out.dtype)
        out_dtype = out.dtype
    else:
        out_struct = jax.ShapeDtypeStruct(out_full if out_full is not None else (m, n), out_dtype)
    has_add, has_epi = add is not None, epi_p is not None
    n_skip = (1 if out is not None else 0) + len(after)
    operands += list(after)
    in_specs += [pl.BlockSpec(memory_space=pl.ANY)] * len(after)

    def body(*refs):
        a_ref, b_ref = refs[0], refs[1]
        pos = 2
        add_ref = epi_ref = None
        if has_add:
            add_ref = refs[pos]
            pos += 1
        if has_epi:
            epi_ref = refs[pos]
            pos += 1
        pos += n_skip
        o_ref = refs[pos]

        def finish(r):
            if has_add:
                r = r + add_ref[...].astype(F32)
            if has_epi:
                r = r * (2.0 * jnp.maximum(epi_ref[...].astype(F32), 0.0))
            o_ref[...] = r.astype(o_ref.dtype)

        av = a_ref[...]
        if a_pro == "relu2":
            av = jnp.square(jnp.maximum(av.astype(F32), 0.0))
        part = _dot(av, b_ref[...], dims)
        if nk == 1:
            finish(part)
        else:
            acc_ref = refs[pos + 1]
            kk = pl.program_id(2)

            @pl.when(kk == 0)
            def _():
                acc_ref[...] = part

            @pl.when(kk > 0)
            def _():
                acc_ref[...] += part

            @pl.when(kk == nk - 1)
            def _():
                finish(acc_ref[...])

    grid = (m // tm, n // tn, nk) if j_inner else (n // tn, m // tm, nk)
    return pl.pallas_call(
        body, name=name, grid=grid, in_specs=in_specs,
        out_specs=spec(out_at, tm, tn, gi, gj), out_shape=out_struct,
        scratch_shapes=[pltpu.VMEM((tm, tn), F32)] if nk > 1 else [], input_output_aliases=aliases,
        compiler_params=_cp(VMEM_BIG))(*operands)


def _rms_fwd(x, g, name, tm=1024):
    s, d = x.shape
    tm = min(tm, s)

    def body(x_ref, g_ref, o_ref):
        xv = x_ref[...]
        r = lax.rsqrt(jnp.mean(xv * xv, axis=-1, keepdims=True) + EPS)
        o_ref[...] = (xv * r * g_ref[...]).astype(o_ref.dtype)

    return pl.pallas_call(
        body, name=name, grid=(s // tm,),
        in_specs=[pl.BlockSpec((tm, d), lambda i: (i, 0)), pl.BlockSpec((1, d), lambda i: (0, 0))],
        out_specs=pl.BlockSpec((tm, d), lambda i: (i, 0)),
        out_shape=jax.ShapeDtypeStruct((s, d), _ACT), compiler_params=_cp(VMEM_BIG))(x, g)


def _rms_bwd(x, g, dy, dres, name, tm=512):
    s, d = x.shape
    tm = min(tm, s)
    has_res = dres is not None

    def body(*refs):
        if has_res:
            x_ref, g_ref, dy_ref, dres_ref, dx_ref, dxa_ref, dg_ref = refs
        else:
            x_ref, g_ref, dy_ref, dx_ref, dxa_ref, dg_ref = refs

        @pl.when(pl.program_id(0) == 0)
        def _():
            dg_ref[...] = jnp.zeros_like(dg_ref)

        xv = x_ref[...]
        dyv = dy_ref[...].astype(F32)
        r = lax.rsqrt(jnp.mean(xv * xv, axis=-1, keepdims=True) + EPS)
        xh = xv * r
        dyg = dyv * g_ref[...]
        dx = r * (dyg - xh * jnp.mean(dyg * xh, axis=-1, keepdims=True))
        if has_res:
            dx = dx + dres_ref[...]
        dx_ref[...] = dx
        dxa_ref[...] = dx.astype(dxa_ref.dtype)
        dg_ref[...] += jnp.sum(dyv * xh, axis=0, keepdims=True)

    row = pl.BlockSpec((tm, d), lambda i: (i, 0))
    vec = pl.BlockSpec((1, d), lambda i: (0, 0))
    in_specs = [row, vec, row] + ([row] if has_res else [])
    operands = [x, g, dy] + ([dres] if has_res else [])
    return pl.pallas_call(
        body, name=name, grid=(s // tm,), in_specs=in_specs, out_specs=[row, row, vec],
        out_shape=[jax.ShapeDtypeStruct((s, d), F32), jax.ShapeDtypeStruct((s, d), _ACT),
                   jax.ShapeDtypeStruct((1, d), F32)], compiler_params=_cp(VMEM_BIG))(*operands)


def _loss_head(h, g, target, name, tm=512):
    s, d = h.shape
    tm = min(tm, s)

    def body(h_ref, g_ref, t_ref, loss_ref, dh_ref, dha_ref, dg_ref):
        @pl.when(pl.program_id(0) == 0)
        def _():
            dg_ref[...] = jnp.zeros_like(dg_ref)
            loss_ref[...] = jnp.zeros_like(loss_ref)

        xv = h_ref[...]
        r = lax.rsqrt(jnp.mean(xv * xv, axis=-1, keepdims=True) + EPS)
        xh = xv * r
        err = xh * g_ref[...] - t_ref[...]
        loss_ref[...] += jnp.full(loss_ref.shape, 0.5 * jnp.sum(jnp.mean(err * err, axis=-1, keepdims=True)), F32)
        dyv = err * (1.0 / d)
        dyg = dyv * g_ref[...]
        dh = r * (dyg - xh * jnp.mean(dyg * xh, axis=-1, keepdims=True))
        dh_ref[...] = dh
        dha_ref[...] = dh.astype(dha_ref.dtype)
        dg_ref[...] += jnp.sum(dyv * xh, axis=0, keepdims=True)

    row = pl.BlockSpec((tm, d), lambda i: (i, 0))
    vec = pl.BlockSpec((1, d), lambda i: (0, 0))
    return pl.pallas_call(
        body, name=name, grid=(s // tm,), in_specs=[row, vec, row],
        out_specs=[pl.BlockSpec((1, 128), lambda i: (0, 0)), row, row, vec],
        out_shape=[jax.ShapeDtypeStruct((1, 128), F32), jax.ShapeDtypeStruct((s, d), F32),
                   jax.ShapeDtypeStruct((s, d), _ACT), jax.ShapeDtypeStruct((1, d), F32)],
        compiler_params=_cp(VMEM_BIG))(h, g, target)


def _gmlp_parts(u, v, lng, lnb):
    mu = jnp.mean(v, axis=-1, keepdims=True)
    vc = v - mu
    rstd = lax.rsqrt(jnp.mean(vc * vc, axis=-1, keepdims=True) + EPS)
    xhat = vc * rstd
    vn = xhat * lng + lnb
    return u, xhat, rstd, vn


def _gmlp_fwd(proj, lng, lnb, ws, bs3, name):
    s = proj.shape[0]

    def body(pu_ref, pv_ref, lng_ref, lnb_ref, ws_ref, bs_ref, o_ref):
        u, _, _, vn = _gmlp_parts(_gelu(pu_ref[...]), _gelu(pv_ref[...]), lng_ref[...], lnb_ref[...])
        causal = _iota((CHUNK, CHUNK), 0) >= _iota((CHUNK, CHUNK), 1)
        for g in range(A_GROUPS):
            sl = slice(g * A_GW, (g + 1) * A_GW)
            w = jnp.where(causal, ws_ref[g], 0.0)
            sv = _dot(w, vn[:, sl]) + bs_ref[g]
            o_ref[:, sl] = (u[:, sl] * sv).astype(o_ref.dtype)

    full = lambda shape: pl.BlockSpec(shape, lambda c: (0,) * len(shape))
    return pl.pallas_call(
        body, name=name, grid=(s // CHUNK,),
        in_specs=[pl.BlockSpec((CHUNK, D_INNER), lambda c: (c, 0)), pl.BlockSpec((CHUNK, D_INNER), lambda c: (c, 1)),
                  full((1, D_INNER)), full((1, D_INNER)), full((A_GROUPS, CHUNK, CHUNK)), full((A_GROUPS, CHUNK, 1))],
        out_specs=pl.BlockSpec((CHUNK, D_INNER), lambda c: (c, 0)),
        out_shape=jax.ShapeDtypeStruct((s, D_INNER + X_WIDTH), _ACT), compiler_params=_cp(VMEM_BIG))(proj, proj, lng, lnb, ws, bs3)


def _gmlp_bwd(proj, dcat, lng, lnb, ws, bs3, name):
    s = proj.shape[0]

    def body(pu_ref, pv_ref, dm_ref, lng_ref, lnb_ref, ws_ref, bs_ref, dp_ref, dws_ref, dbs_ref, dlng_ref, dlnb_ref, dvn_ref):
        @pl.when(pl.program_id(0) == 0)
        def _():
            dws_ref[...] = jnp.zeros_like(dws_ref)
            dbs_ref[...] = jnp.zeros_like(dbs_ref)
            dlng_ref[...] = jnp.zeros_like(dlng_ref)
            dlnb_ref[...] = jnp.zeros_like(dlnb_ref)

        lng = lng_ref[...]
        u, u_grad = _gelu_with_grad(pu_ref[...])
        v, v_grad = _gelu_with_grad(pv_ref[...])
        u, xhat, rstd, vn = _gmlp_parts(u, v, lng, lnb_ref[...])
        dm = dm_ref[...].astype(F32)
        causal = _iota((CHUNK, CHUNK), 0) >= _iota((CHUNK, CHUNK), 1)
        for g in range(A_GROUPS):
            sl = slice(g * A_GW, (g + 1) * A_GW)
            w = jnp.where(causal, ws_ref[g], 0.0)
            sv = _dot(w, vn[:, sl]) + bs_ref[g]
            dsv = dm[:, sl] * u[:, sl]
            dp_ref[:, sl] = (dm[:, sl] * sv * u_grad[:, sl]).astype(dp_ref.dtype)
            dvn_ref[:, sl] = _dot_tn(w, dsv)
            dws_ref[g] += jnp.where(causal, _dot_nt(dsv, vn[:, sl]), 0.0)
            dbs_ref[g] += jnp.sum(dsv, axis=-1, keepdims=True)
        dvn = dvn_ref[...]
        dlng_ref[...] += jnp.sum(dvn * xhat, axis=0, keepdims=True)
        dlnb_ref[...] += jnp.sum(dvn, axis=0, keepdims=True)
        dxh = dvn * lng
        dv = rstd * (dxh - jnp.mean(dxh, axis=-1, keepdims=True) - xhat * jnp.mean(dxh * xhat, axis=-1, keepdims=True))
        dp_ref[:, D_INNER:] = (dv * v_grad).astype(dp_ref.dtype)

    full = lambda shape: pl.BlockSpec(shape, lambda c: (0,) * len(shape))
    return pl.pallas_call(
        body, name=name, grid=(s // CHUNK,),
        in_specs=[pl.BlockSpec((CHUNK, D_INNER), lambda c: (c, 0)), pl.BlockSpec((CHUNK, D_INNER), lambda c: (c, 1)),
                  pl.BlockSpec((CHUNK, D_INNER), lambda c: (c, 0)),
                  full((1, D_INNER)), full((1, D_INNER)), full((A_GROUPS, CHUNK, CHUNK)), full((A_GROUPS, CHUNK, 1))],
        out_specs=[pl.BlockSpec((CHUNK, 2 * D_INNER), lambda c: (c, 0)), full((A_GROUPS, CHUNK, CHUNK)),
                   full((A_GROUPS, CHUNK, 1)), full((1, D_INNER)), full((1, D_INNER))],
        out_shape=[jax.ShapeDtypeStruct((s, 2 * D_INNER + X_WIDTH), _ACT), jax.ShapeDtypeStruct((A_GROUPS, CHUNK, CHUNK), F32),
                   jax.ShapeDtypeStruct((A_GROUPS, CHUNK, 1), F32), jax.ShapeDtypeStruct((1, D_INNER), F32),
                   jax.ShapeDtypeStruct((1, D_INNER), F32)],
        scratch_shapes=[pltpu.VMEM((CHUNK, D_INNER), F32)],
        compiler_params=_cp(VMEM_BIG))(proj, proj, dcat, lng, lnb, ws, bs3)


_X_SCALE = 1.0 / math.sqrt(X_HD)


def _attn_fwd(proj, qblk, kv, cat, name, tm=512):
    s = proj.shape[0]
    tm = min(tm, s)

    def body(q_ref, kv_ref, cat_ref, o_ref):
        for h in range(X_HEADS):
            sl = slice(h * X_HD, (h + 1) * X_HD)
            k = kv_ref[:, sl]
            v = kv_ref[:, X_WIDTH + h * X_HD:X_WIDTH + (h + 1) * X_HD]
            sc = _dot_nt(q_ref[:, sl], k) * _X_SCALE
            e = jnp.exp(sc - jnp.max(sc, axis=-1, keepdims=True))
            p = e / jnp.sum(e, axis=-1, keepdims=True)
            o_ref[:, sl] = _dot(p, v).astype(o_ref.dtype)

    return pl.pallas_call(
        body, name=name, grid=(s // tm,),
        in_specs=[pl.BlockSpec((tm, X_WIDTH), lambda i: (i, qblk)), pl.BlockSpec((N_MEM, 2 * X_WIDTH), lambda i: (0, 0)),
                  pl.BlockSpec(memory_space=pl.ANY)],
        out_specs=pl.BlockSpec((tm, X_WIDTH), lambda i: (i, D_INNER // X_WIDTH)),
        out_shape=jax.ShapeDtypeStruct(cat.shape, cat.dtype), input_output_aliases={2: 0},
        compiler_params=_cp(VMEM_BIG))(proj, kv, cat)


def _attn_bwd(proj, qblk, kv, dcat, dproj, name, tm=512):
    s = proj.shape[0]
    tm = min(tm, s)

    def body(q_ref, kv_ref, do_ref, dproj_ref, dq_ref, dkv_ref):
        @pl.when(pl.program_id(0) == 0)
        def _():
            dkv_ref[...] = jnp.zeros_like(dkv_ref)

        for h in range(X_HEADS):
            sl = slice(h * X_HD, (h + 1) * X_HD)
            slv = slice(X_WIDTH + h * X_HD, X_WIDTH + (h + 1) * X_HD)
            q = q_ref[:, sl]
            k = kv_ref[:, sl]
            v = kv_ref[:, slv]
            do = do_ref[:, sl].astype(F32)
            sc = _dot_nt(q, k) * _X_SCALE
            e = jnp.exp(sc - jnp.max(sc, axis=-1, keepdims=True))
            p = e / jnp.sum(e, axis=-1, keepdims=True)
            dp = _dot_nt(do, v)
            ds = p * (dp - jnp.sum(dp * p, axis=-1, keepdims=True)) * _X_SCALE
            dq_ref[:, sl] = _dot(ds, k).astype(dq_ref.dtype)
            dkv_ref[:, sl] += _dot_tn(ds, q)
            dkv_ref[:, slv] += _dot_tn(p, do)

    return pl.pallas_call(
        body, name=name, grid=(s // tm,),
        in_specs=[pl.BlockSpec((tm, X_WIDTH), lambda i: (i, qblk)), pl.BlockSpec((N_MEM, 2 * X_WIDTH), lambda i: (0, 0)),
                  pl.BlockSpec((tm, X_WIDTH), lambda i: (i, 2)), pl.BlockSpec(memory_space=pl.ANY)],
        out_specs=[pl.BlockSpec((tm, X_WIDTH), lambda i: (i, qblk)), pl.BlockSpec((N_MEM, 2 * X_WIDTH), lambda i: (0, 0))],
        out_shape=[jax.ShapeDtypeStruct(dproj.shape, dproj.dtype), jax.ShapeDtypeStruct((N_MEM, 2 * X_WIDTH), F32)],
        input_output_aliases={3: 0}, compiler_params=_cp(VMEM_BIG))(proj, kv, dcat, dproj)


CONV_TC = 256
_XBC_BLK0 = D_INNER // CONV_TC


CONV_RB = 64
SUBLANES = 8


def _rows_before(cur, prev_last, j):
    rolled = pltpu.roll(cur, j, 0)
    head = jnp.where(_iota((SUBLANES, cur.shape[1]), 0) < j, pltpu.roll(prev_last, j, 0), rolled[:SUBLANES])
    return jnp.concatenate([head, rolled[SUBLANES:]], axis=0)


def _rows_after(cur, next_first, j):
    n = cur.shape[0]
    rolled = pltpu.roll(cur, n - j, 0)
    tail = jnp.where(_iota((SUBLANES, cur.shape[1]), 0) >= SUBLANES - j, pltpu.roll(next_first, SUBLANES - j, 0),
                     rolled[n - SUBLANES:])
    return jnp.concatenate([rolled[:n - SUBLANES], tail], axis=0)


def _conv_pre(x_ref, w_ref, b_ref, r0, prev_last):
    cur = x_ref[pl.ds(r0, CONV_RB), :]
    shifts = [_rows_before(cur, prev_last, j) for j in range(1, CONV_K)]
    pre = b_ref[...] + w_ref[CONV_K - 1:CONV_K, :] * cur
    for j in range(1, CONV_K):
        pre = pre + w_ref[CONV_K - 1 - j:CONV_K - j, :] * shifts[j - 1]
    return pre, cur, shifts


def _conv_fwd(proj, w, b, name):
    s = proj.shape[0]

    def body(x_ref, w_ref, b_ref, o_ref):
        xv = x_ref[...]
        rows = _iota(xv.shape, 0)
        pre = b_ref[...] + w_ref[CONV_K - 1:CONV_K, :] * xv
        for j in range(1, CONV_K):
            pre = pre + w_ref[CONV_K - 1 - j:CONV_K - j, :] * jnp.where(rows >= j, pltpu.roll(xv, j, 0), 0.0)
        o_ref[...] = pre * _sigmoid(pre)

    return pl.pallas_call(
        body, name=name, grid=(CONV_DIM // CONV_TC,),
        in_specs=[pl.BlockSpec((s, CONV_TC), lambda j: (0, _XBC_BLK0 + j)), pl.BlockSpec((CONV_K, CONV_TC), lambda j: (0, j)),
                  pl.BlockSpec((1, CONV_TC), lambda j: (0, j))],
        out_specs=pl.BlockSpec((s, CONV_TC), lambda j: (0, j)),
        out_shape=jax.ShapeDtypeStruct((s, CONV_DIM), F32), compiler_params=_cp(VMEM_BIG))(proj, w, b)


def _conv_bwd(proj, w, b, dxbc, dproj, name):
    s = proj.shape[0]

    nb = s // CONV_RB

    def body(x_ref, w_ref, b_ref, d_ref, dproj_ref, dx_ref, dw_ref, db_ref, dpre_ref):
        def fold(v):
            out = v[:SUBLANES]
            for t in range(1, CONV_RB // SUBLANES):
                out = out + v[t * SUBLANES:(t + 1) * SUBLANES]
            return out

        def first(i, carry):
            prev_last, acc = carry
            r0 = pl.multiple_of(i * CONV_RB, CONV_RB)
            pre, cur, shifts = _conv_pre(x_ref, w_ref, b_ref, r0, prev_last)
            sig = _sigmoid(pre)
            dpre = d_ref[pl.ds(r0, CONV_RB), :] * (sig * (1.0 + pre * (1.0 - sig)))
            dpre_ref[pl.ds(r0, CONV_RB), :] = dpre
            taps = [cur] + shifts
            acc = tuple(a + fold(dpre * t) for a, t in zip(acc[:CONV_K], taps)) + (acc[CONV_K] + fold(dpre),)
            return cur[CONV_RB - SUBLANES:], acc

        zero8 = jnp.zeros((SUBLANES, CONV_TC), F32)
        _, acc = lax.fori_loop(0, nb, first, (zero8, (zero8,) * (CONV_K + 1)))
        for j in range(CONV_K):
            dw_ref[CONV_K - 1 - j:CONV_K - j, :] = jnp.sum(acc[j], axis=0, keepdims=True)
        db_ref[...] = jnp.sum(acc[CONV_K], axis=0, keepdims=True)

        def second(i, next_first):
            r0 = pl.multiple_of((nb - 1 - i) * CONV_RB, CONV_RB)
            cur = dpre_ref[pl.ds(r0, CONV_RB), :]
            dx = w_ref[CONV_K - 1:CONV_K, :] * cur
            for j in range(1, CONV_K):
                dx = dx + w_ref[CONV_K - 1 - j:CONV_K - j, :] * _rows_after(cur, next_first, j)
            dx_ref[pl.ds(r0, CONV_RB), :] = dx.astype(dx_ref.dtype)
            return cur[:SUBLANES]

        lax.fori_loop(0, nb, second, zero8)

    return pl.pallas_call(
        body, name=name, grid=(CONV_DIM // CONV_TC,),
        in_specs=[pl.BlockSpec((s, CONV_TC), lambda j: (0, _XBC_BLK0 + j)), pl.BlockSpec((CONV_K, CONV_TC), lambda j: (0, j)),
                  pl.BlockSpec((1, CONV_TC), lambda j: (0, j)), pl.BlockSpec((s, CONV_TC), lambda j: (0, j)),
                  pl.BlockSpec(memory_space=pl.ANY)],
        out_specs=[pl.BlockSpec((s, CONV_TC), lambda j: (0, _XBC_BLK0 + j)), pl.BlockSpec((CONV_K, CONV_TC), lambda j: (0, j)),
                   pl.BlockSpec((1, CONV_TC), lambda j: (0, j))],
        out_shape=[jax.ShapeDtypeStruct(dproj.shape, dproj.dtype), jax.ShapeDtypeStruct((CONV_K, CONV_DIM), F32),
                   jax.ShapeDtypeStruct((1, CONV_DIM), F32)], input_output_aliases={4: 0},
        scratch_shapes=[pltpu.VMEM((s, CONV_TC), F32)],
        compiler_params=_cp(VMEM_BIG))(proj, w, b, dxbc, dproj)


def _ssd_common(dtc_ref, br_ref, ar_ref, csb_ref, cst_ref, csf_ref, dtf_ref, expand):
    a_row = -jnp.exp(ar_ref[...])
    dt_c = _softplus(dtc_ref[...] + br_ref[...])
    tril = _iota((CHUNK, CHUNK), 0) >= _iota((CHUNK, CHUNK), 1)
    cs = _sel_dot(tril, dt_c * a_row)
    cst_ref[...] = cs.T
    e64 = (jnp.right_shift(_iota((HPAD, D_INNER), 1), 6) == _iota((HPAD, D_INNER), 0)).astype(jnp.bfloat16)
    if expand:
        e128 = jnp.right_shift(_iota((HPAD, SSM_HEADS * CHUNK), 1), 7) == _iota((HPAD, SSM_HEADS * CHUNK), 0)
        csb_ref[...] = _dot_sel(cs, e128)
        dtf_ref[...] = _dot_sel(dt_c, e64)
        csf_ref[...] = _dot_sel(cs, e64)
    dt_full = dtf_ref[...]
    cs_full = csf_ref[...]
    cs_last = csf_ref[CHUNK - 1:CHUNK, :]
    e_full = jnp.exp(cs_full)
    f_full = jnp.exp(cs_last - cs_full)
    gamma = jnp.exp(cs_last)
    return a_row, dt_c, cs, dt_full, e_full, f_full, gamma, e64


def _ssd_lambda(csb_ref, cst_ref, h, causal):
    diff = csb_ref[:, h * CHUNK:(h + 1) * CHUNK] - cst_ref[h:h + 1, :]
    return jnp.exp(jnp.where(causal, diff, -1e30))


_SSD_VEC_SPECS = lambda: [pl.BlockSpec((1, HPAD), lambda c: (0, 0)), pl.BlockSpec((1, HPAD), lambda c: (0, 0)),
                          pl.BlockSpec((1, D_INNER), lambda c: (0, 0))]


def _ssd_fwd(xbc, dtc, bias_row, alog_row, dfull, name):
    s = xbc.shape[0]
    nc = s // CHUNK

    def body(xbc_ref, dtc_ref, br_ref, ar_ref, df_ref, y_ref, st_ref, csb_ref, csf_ref, dtf_ref, ht_ref, cst_ref):
        @pl.when(pl.program_id(0) == 0)
        def _():
            ht_ref[...] = jnp.zeros_like(ht_ref)

        _, _, _, dt_full, e_full, f_full, gamma, _ = _ssd_common(
            dtc_ref, br_ref, ar_ref, csb_ref, cst_ref, csf_ref, dtf_ref, expand=True)
        x = xbc_ref[:, :D_INNER]
        xdt = x * dt_full
        st_ref[...] = ht_ref[...]
        causal = _iota((CHUNK, CHUNK), 0) >= _iota((CHUNK, CHUNK), 1)
        lo = _iota((CHUNK, CHUNK), 1) < SSM_P
        for g in range(SSM_GROUPS):
            gs = slice(g * SSM_GW, (g + 1) * SSM_GW)
            bg = xbc_ref[:, D_INNER + g * SSM_N:D_INNER + (g + 1) * SSM_N]
            cg = xbc_ref[:, D_INNER + SSM_GROUPS * SSM_N + g * SSM_N:D_INNER + SSM_GROUPS * SSM_N + (g + 1) * SSM_N]
            ht = ht_ref[:, gs]
            cb = _dot_nt(cg, bg)
            yoff = e_full[:, gs] * _dot(cg, ht)
            for jp in range(SSM_GW // CHUNK):
                j = g * (SSM_GW // CHUNK) + jp
                ps = slice(j * CHUNK, (j + 1) * CHUNK)
                x2 = xdt[:, ps]
                y0 = _dot(cb * _ssd_lambda(csb_ref, cst_ref, 2 * j, causal), x2)
                y1 = _dot(cb * _ssd_lambda(csb_ref, cst_ref, 2 * j + 1, causal), x2)
                y_ref[:, ps] = (jnp.where(lo, y0, y1) + yoff[:, jp * CHUNK:(jp + 1) * CHUNK]
                                + x[:, ps] * df_ref[:, ps])
            ht_ref[:, gs] = gamma[:, gs] * ht + _dot_tn(bg, xdt[:, gs] * f_full[:, gs])

    return pl.pallas_call(
        body, name=name, grid=(nc,),
        in_specs=[pl.BlockSpec((CHUNK, CONV_DIM), lambda c: (c, 0)), pl.BlockSpec((CHUNK, HPAD), lambda c: (c, 0))]
                 + _SSD_VEC_SPECS(),
        out_specs=[pl.BlockSpec((CHUNK, D_INNER), lambda c: (c, 0)), pl.BlockSpec((None, SSM_N, D_INNER), lambda c: (c, 0, 0)),
                   pl.BlockSpec((CHUNK, SSM_HEADS * CHUNK), lambda c: (c, 0)), pl.BlockSpec((CHUNK, D_INNER), lambda c: (c, 0)),
                   pl.BlockSpec((CHUNK, D_INNER), lambda c: (c, 0))],
        out_shape=[jax.ShapeDtypeStruct((s, D_INNER), F32), jax.ShapeDtypeStruct((nc, SSM_N, D_INNER), F32),
                   jax.ShapeDtypeStruct((s, SSM_HEADS * CHUNK), F32), jax.ShapeDtypeStruct((s, D_INNER), F32),
                   jax.ShapeDtypeStruct((s, D_INNER), F32)],
        scratch_shapes=[pltpu.VMEM((SSM_N, D_INNER), F32), pltpu.VMEM((HPAD, CHUNK), F32)],
        compiler_params=_cp(VMEM_BIG))(xbc, dtc, bias_row, alog_row, dfull)


def _ssd_bwd(xbc, dtc, bias_row, alog_row, dfull, dy, states, expansions, name):
    s = xbc.shape[0]
    nc = s // CHUNK
    rev = lambda c: nc - 1 - c

    def body(xbc_ref, dtc_ref, br_ref, ar_ref, df_ref, dy_ref, st_ref, csb_ref, csf_ref, dtf_ref,
             dxbc_ref, ddt_ref, dalog_ref, dd_ref, dbias_ref,
             dht_ref, cst_ref, ddf_ref, dxs_ref, dcsf_ref, dcsl_ref):
        step = pl.program_id(0)

        @pl.when(step == 0)
        def _():
            dht_ref[...] = jnp.zeros_like(dht_ref)
            ddf_ref[...] = jnp.zeros_like(ddf_ref)
            dalog_ref[...] = jnp.zeros_like(dalog_ref)
            dbias_ref[...] = jnp.zeros_like(dbias_ref)
            dd_ref[...] = jnp.zeros_like(dd_ref)

        a_row, dt_c, _, dt_full, e_full, f_full, gamma, e64 = _ssd_common(
            dtc_ref, br_ref, ar_ref, csb_ref, cst_ref, csf_ref, dtf_ref, expand=False)
        x = xbc_ref[:, :D_INNER]
        xdt = x * dt_full
        dy_all = dy_ref[...]
        ddf_ref[...] += jnp.broadcast_to(jnp.sum(dy_all * x, axis=0, keepdims=True), ddf_ref.shape)
        causal = _iota((CHUNK, CHUNK), 0) >= _iota((CHUNK, CHUNK), 1)
        lo = _iota((CHUNK, CHUNK), 1) < SSM_P
        head_lane = _iota((CHUNK, HPAD), 1)
        head_row = _iota((HPAD, CHUNK), 0)
        dcs_heads = jnp.zeros((CHUNK, HPAD), F32)
        dcs_cols = jnp.zeros((HPAD, CHUNK), F32)
        for g in range(SSM_GROUPS):
            gs = slice(g * SSM_GW, (g + 1) * SSM_GW)
            b0 = D_INNER + g * SSM_N
            c0 = D_INNER + SSM_GROUPS * SSM_N + g * SSM_N
            bg = xbc_ref[:, b0:b0 + SSM_N]
            cg = xbc_ref[:, c0:c0 + SSM_N]
            ht = st_ref[:, gs]
            dht = dht_ref[:, gs]
            dyg = dy_all[:, gs]
            eg, fg, gg = e_full[:, gs], f_full[:, gs], gamma[:, gs]
            z = _dot(cg, ht)
            dz = dyg * eg
            dcg = _dot_nt(dz, ht)
            dht_new = _dot_tn(cg, dz) + gg * dht
            xf = xdt[:, gs] * fg
            dxf = _dot(bg, dht)
            dbg = _dot_nt(xf, dht)
            dff = dxf * xf
            dcsf_ref[:, gs] = dyg * eg * z - dff
            dcsl_ref[:, gs] = jnp.broadcast_to(
                jnp.sum(dff, axis=0, keepdims=True) + jnp.sum(dht * ht, axis=0, keepdims=True) * gg, (8, SSM_GW))
            cb = _dot_nt(cg, bg)
            dcb = jnp.zeros((CHUNK, CHUNK), F32)
            for jp in range(SSM_GW // CHUNK):
                j = g * (SSM_GW // CHUNK) + jp
                ps = slice(j * CHUNK, (j + 1) * CHUNK)
                x2 = xdt[:, ps]
                dy2 = dy_all[:, ps]
                dxh = []
                for hh in range(2):
                    h = 2 * j + hh
                    lam = _ssd_lambda(csb_ref, cst_ref, h, causal)
                    mh = cb * lam
                    dyh = jnp.where(lo, dy2, 0.0) if hh == 0 else jnp.where(lo, 0.0, dy2)
                    dm = _dot_nt(dyh, x2)
                    dcb = dcb + dm * lam
                    gm = dm * mh
                    dcs_heads = dcs_heads + jnp.where(head_lane == h, jnp.sum(gm, axis=1, keepdims=True), 0.0)
                    dcs_cols = dcs_cols + jnp.where(head_row == h, jnp.sum(gm, axis=0, keepdims=True), 0.0)
                    dxh.append(_dot_tn(mh, dy2))
                dxs_ref[:, ps] = jnp.where(lo, dxh[0], dxh[1]) + dxf[:, jp * CHUNK:(jp + 1) * CHUNK] * fg[:, jp * CHUNK:(jp + 1) * CHUNK]
            dxbc_ref[:, b0:b0 + SSM_N] = (dbg + _dot_tn(dcb, cg)).astype(dxbc_ref.dtype)
            dxbc_ref[:, c0:c0 + SSM_N] = (dcg + _dot(dcb, bg)).astype(dxbc_ref.dtype)
            dht_ref[:, gs] = dht_new
        dxs = dxs_ref[...]
        dcs_heads = dcs_heads - dcs_cols.T + _dot_sel(dcsf_ref[...], e64, ((1,), (1,)))
        dcs_last = _dot_sel(dcsl_ref[...], e64, ((1,), (1,)))
        dcs_heads = dcs_heads + jnp.where(_iota((CHUNK, HPAD), 0) == CHUNK - 1, dcs_last[0:1, :], 0.0)
        triu = _iota((CHUNK, CHUNK), 0) <= _iota((CHUNK, CHUNK), 1)
        dda = _sel_dot(triu, dcs_heads)
        ddt = dda * a_row + _dot_sel(dxs * x, e64, ((1,), (1,)))
        dxbc_ref[:, :D_INNER] = (dxs * dt_full + dy_all * df_ref[...]).astype(dxbc_ref.dtype)
        dalog_ref[...] += jnp.sum(dda * dt_c, axis=0, keepdims=True) * a_row
        ddt_raw = ddt * _sigmoid(dtc_ref[...] + br_ref[...])
        ddt_ref[...] = ddt_raw.astype(ddt_ref.dtype)
        dbias_ref[...] += jnp.sum(ddt_raw, axis=0, keepdims=True)

        @pl.when(step == nc - 1)
        def _():
            dd_ref[...] = _dot_sel(ddf_ref[...], e64, ((1,), (1,)))[0:1, :]

    vec = pl.BlockSpec((1, HPAD), lambda c: (0, 0))
    return pl.pallas_call(
        body, name=name, grid=(nc,),
        in_specs=[pl.BlockSpec((CHUNK, CONV_DIM), lambda c: (rev(c), 0)), pl.BlockSpec((CHUNK, HPAD), lambda c: (rev(c), 0))]
                 + _SSD_VEC_SPECS()
                 + [pl.BlockSpec((CHUNK, D_INNER), lambda c: (rev(c), 0)),
                    pl.BlockSpec((None, SSM_N, D_INNER), lambda c: (rev(c), 0, 0)),
                    pl.BlockSpec((CHUNK, SSM_HEADS * CHUNK), lambda c: (rev(c), 0)),
                    pl.BlockSpec((CHUNK, D_INNER), lambda c: (rev(c), 0)), pl.BlockSpec((CHUNK, D_INNER), lambda c: (rev(c), 0))],
        out_specs=[pl.BlockSpec((CHUNK, CONV_DIM), lambda c: (rev(c), 0)), pl.BlockSpec((CHUNK, HPAD), lambda c: (rev(c), 0)),
                   vec, vec, vec],
        out_shape=[jax.ShapeDtypeStruct((s, CONV_DIM), F32), jax.ShapeDtypeStruct((s, HPAD), _ACT),
                   jax.ShapeDtypeStruct((1, HPAD), F32), jax.ShapeDtypeStruct((1, HPAD), F32),
                   jax.ShapeDtypeStruct((1, HPAD), F32)],
        scratch_shapes=[pltpu.VMEM((SSM_N, D_INNER), F32), pltpu.VMEM((HPAD, CHUNK), F32),
                        pltpu.VMEM((8, D_INNER), F32), pltpu.VMEM((CHUNK, D_INNER), F32),
                        pltpu.VMEM((CHUNK, D_INNER), F32), pltpu.VMEM((8, D_INNER), F32)],
        compiler_params=_cp(VMEM_BIG))(xbc, dtc, bias_row, alog_row, dfull, dy, states, *expansions)


def _gate_fwd(y, proj, gn, name, tm=512):
    s = y.shape[0]
    tm = min(tm, s)

    def body(y_ref, z_ref, gn_ref, o_ref):
        for g in range(SSM_GROUPS):
            gs = slice(g * SSM_GW, (g + 1) * SSM_GW)
            z = z_ref[:, gs]
            t = y_ref[:, gs] * (z * _sigmoid(z))
            r = lax.rsqrt(jnp.mean(t * t, axis=-1, keepdims=True) + EPS)
            o_ref[:, gs] = (t * r * gn_ref[:, gs]).astype(o_ref.dtype)

    row = pl.BlockSpec((tm, D_INNER), lambda i: (i, 0))
    return pl.pallas_call(
        body, name=name, grid=(s // tm,), in_specs=[row, row, pl.BlockSpec((1, D_INNER), lambda i: (0, 0))],
        out_specs=row, out_shape=jax.ShapeDtypeStruct((s, D_INNER + X_WIDTH), _ACT),
        compiler_params=_cp(VMEM_BIG))(y, proj, gn)


def _gate_bwd(y, proj, gn, dcat, name, tm=512):
    s = y.shape[0]
    tm = min(tm, s)

    def body(y_ref, z_ref, gn_ref, dm_ref, dy_ref, dz_ref, dgn_ref):
        @pl.when(pl.program_id(0) == 0)
        def _():
            dgn_ref[...] = jnp.zeros_like(dgn_ref)

        for g in range(SSM_GROUPS):
            gs = slice(g * SSM_GW, (g + 1) * SSM_GW)
            z = z_ref[:, gs]
            yv = y_ref[:, gs]
            sig = _sigmoid(z)
            sz = z * sig
            t = yv * sz
            r = lax.rsqrt(jnp.mean(t * t, axis=-1, keepdims=True) + EPS)
            th = t * r
            dm = dm_ref[:, gs].astype(F32)
            dmg = dm * gn_ref[:, gs]
            dt_ = r * (dmg - th * jnp.mean(dmg * th, axis=-1, keepdims=True))
            dgn_ref[:, gs] += jnp.sum(dm * th, axis=0, keepdims=True)
            dy_ref[:, gs] = dt_ * sz
            dz_ref[:, gs] = (dt_ * yv * (sig * (1.0 + z * (1.0 - sig)))).astype(dz_ref.dtype)

    row = pl.BlockSpec((tm, D_INNER), lambda i: (i, 0))
    vec = pl.BlockSpec((1, D_INNER), lambda i: (0, 0))
    return pl.pallas_call(
        body, name=name, grid=(s // tm,), in_specs=[row, row, vec, row], out_specs=[row, row, vec],
        out_shape=[jax.ShapeDtypeStruct((s, D_INNER), F32), jax.ShapeDtypeStruct((s, 6 * D_MODEL), _ACT),
                   jax.ShapeDtypeStruct((1, D_INNER), F32)], compiler_params=_cp(VMEM_BIG))(y, proj, gn, dcat)


def _block_of(kind, width):
    if kind == "col":
        return lambda ref, j: ref.at[:, :, pl.ds(pl.multiple_of(j * width, 128), width)]
    if kind == "row":
        return lambda ref, j: ref.at[:, pl.ds(pl.multiple_of(j * width, 8), width), :]
    return lambda ref, j: ref.at[j]


def _coords():
    return lax.axis_index("x"), lax.axis_index("y"), lax.axis_index("c")


def _rel_chip(x, y, k):
    return (1 - x if k & 1 else x), (1 - y if k & 2 else y)


def _all_gather_body(ins, outs, send_sems, recv_sems, local_sems, blocks):
    n = len(ins)
    x, y, c = _coords()
    sibling = (x, y, 1 - c)
    via = (x + (1 - c) * (1 - 2 * x), y + c * (1 - 2 * y))
    onto = (x + c * (1 - 2 * x), y + (1 - c) * (1 - 2 * y))

    def copy(t, k, chip, core, to, src=None):
        dst = blocks[t](outs[t], 4 * chip[0] + 2 * chip[1] + core)
        return pltpu.make_async_remote_copy(
            src_ref=dst if src is None else src, dst_ref=dst, send_sem=send_sems.at[t, k],
            recv_sem=recv_sems.at[t, k], device_id=to, device_id_type=MESH)

    started = []
    for t in range(n):
        mine = pltpu.make_async_copy(ins[t], blocks[t](outs[t], 4 * x + 2 * y + c), local_sems.at[t])
        mine.start()
        started.append(mine)
    sends = []
    for t in range(n):
        for k in range(3):
            px, py = _rel_chip(x, y, k)
            cp = copy(t, k, (x, y), c, (px, py, 1 - c if k == 0 else c), src=ins[t])
            cp.start()
            sends.append(cp)
    for t in range(n):
        for k in (1, 2):
            chip = _rel_chip(x, y, k)
            copy(t, k, chip, c, sibling).wait_recv()
            fwd = copy(t, 3 + k, chip, c, sibling)
            fwd.start()
            sends.append(fwd)
        hop = copy(t, 3, via, c, (*onto, c))
        hop.start()
        sends.append(hop)
    for t in range(n):
        diagonal = _rel_chip(x, y, 3)
        copy(t, 3, diagonal, c, sibling).wait_recv()
        fwd = copy(t, 6, diagonal, c, sibling)
        fwd.start()
        sends.append(fwd)
    for t in range(n):
        copy(t, 0, (x, y), 1 - c, sibling).wait_recv()
        for k in range(1, 4):
            copy(t, 3 + k, _rel_chip(x, y, k), 1 - c, sibling).wait_recv()
    for cp in sends:
        cp.wait_send()
    for mine in started:
        mine.wait()


def _handshake(peers):
    barrier = pltpu.get_barrier_semaphore()
    for peer in peers:
        pl.semaphore_signal(barrier, inc=1, device_id=peer, device_id_type=MESH)
    pl.semaphore_wait(barrier, len(peers))


def _gather_peers():
    x, y, c = _coords()
    return [(x, y, 1 - c)] + [(*_rel_chip(x, y, k), c) for k in (1, 2)]


SEQ_ID_GATHER, SEQ_ID_SIBLING, SEQ_ID_CHIPS = 1, 2, 3


def _sequencer_call(body, peers, operands, out_types, sems, name, collective_id, after=()):
    n_in, n_out, n_after = len(operands), len(out_types), len(after)

    def launch(*refs):
        _handshake(peers())
        body(refs[:n_in], refs[n_in + n_after:n_in + n_after + n_out], *refs[n_in + n_after + n_out:])

    return pl.kernel(
        launch, name=name, out_type=out_types, mesh=plsc.ScalarSubcoreMesh(axis_name="seq", num_cores=1),
        scratch_types=sems, compiler_params=pltpu.CompilerParams(collective_id=collective_id))(*operands, *after)


def _all_gather_seq(shards, layouts, name, after=()):
    n = len(shards)
    blocks = [_block_of(kind, width) for kind, width, _ in layouts]
    return _sequencer_call(
        lambda ins, outs, *sems: _all_gather_body(ins, outs, *sems, blocks), _gather_peers, shards,
        [jax.ShapeDtypeStruct(shape, sh.dtype) for sh, (_, _, shape) in zip(shards, layouts)],
        [pltpu.SemaphoreType.DMA((n, 7)), pltpu.SemaphoreType.DMA((n, 7)), pltpu.SemaphoreType.DMA((n,))],
        name, SEQ_ID_GATHER, after)


def _tie(small, after):
    return lax.optimization_barrier((small, *after))[0]


def _rs_to_sibling(grads, layouts, name, after=()):
    n = len(grads)
    blocks = [_block_of(kind, width) for kind, width, _ in layouts]

    def body(ins, outs, send_sems, recv_sems):
        x, y, c = _coords()
        sibling = (x, y, 1 - c)
        cps = []
        for t in range(n):
            for k in range(4):
                px, py = _rel_chip(x, y, k)
                cp = pltpu.make_async_remote_copy(
                    src_ref=blocks[t](ins[t], 4 * px + 2 * py + (1 - c)), dst_ref=outs[t].at[k],
                    send_sem=send_sems.at[t, k], recv_sem=recv_sems.at[t, k], device_id=sibling, device_id_type=MESH)
                cp.start()
                cps.append(cp)
        for cp in cps:
            cp.wait_recv()
        for cp in cps:
            cp.wait_send()

    def sibling_only():
        x, y, c = _coords()
        return [(x, y, 1 - c)]

    return _sequencer_call(
        body, sibling_only, grads,
        [jax.ShapeDtypeStruct((4,) + shape, g.dtype) for g, (_, _, shape) in zip(grads, layouts)],
        [pltpu.SemaphoreType.DMA((n, 4)), pltpu.SemaphoreType.DMA((n, 4))], name, SEQ_ID_SIBLING, after)


def _rs_chip_sum(grad, recv, layout, xyc, name):
    kind, width, shape = layout
    r, ccols = shape

    def src_index(step, xyc_ref):
        k = step + 1
        px = jnp.where(k % 2 == 1, 1 - xyc_ref[0], xyc_ref[0])
        py = jnp.where(k // 2 == 1, 1 - xyc_ref[1], xyc_ref[1])
        return 4 * px + 2 * py + xyc_ref[2]

    if kind == "col":
        g_spec = pl.BlockSpec((r, ccols), lambda k, s_: (0, src_index(k, s_)))
    elif kind == "row":
        g_spec = pl.BlockSpec((r, ccols), lambda k, s_: (src_index(k, s_), 0))
    else:
        g_spec = pl.BlockSpec((None, r, ccols), lambda k, s_: (src_index(k, s_), 0, 0))

    def body(xyc_ref, g_ref, r_ref, o_ref):
        o_ref[...] = (g_ref[...].astype(F32) + r_ref[...].astype(F32)).astype(o_ref.dtype)

    slot = pl.BlockSpec((None, r, ccols), lambda k, s_: (k + 1, 0, 0))
    return pl.pallas_call(
        body, name=name,
        grid_spec=pltpu.PrefetchScalarGridSpec(num_scalar_prefetch=1, grid=(3,), in_specs=[g_spec, slot], out_specs=slot),
        out_shape=jax.ShapeDtypeStruct((4, r, ccols), grad.dtype), compiler_params=_cp(VMEM_BIG))(xyc, grad, recv)


def _rs_across_chips(parts, name):
    n = len(parts)

    def body(ins, outs, send_sems, recv_sems):
        x, y, c = _coords()
        cps = []
        for t in range(n):
            for k in range(1, 4):
                px, py = _rel_chip(x, y, k)
                cp = pltpu.make_async_remote_copy(
                    src_ref=ins[t].at[k], dst_ref=outs[t].at[k - 1], send_sem=send_sems.at[t, k - 1],
                    recv_sem=recv_sems.at[t, k - 1], device_id=(px, py, c), device_id_type=MESH)
                cp.start()
                cps.append(cp)
        for cp in cps:
            cp.wait_recv()
        for cp in cps:
            cp.wait_send()

    def other_chips():
        x, y, c = _coords()
        return [(*_rel_chip(x, y, k), c) for k in range(1, 4)]

    return _sequencer_call(
        body, other_chips, parts, [jax.ShapeDtypeStruct((3,) + p.shape[1:], p.dtype) for p in parts],
        [pltpu.SemaphoreType.DMA((n, 3)), pltpu.SemaphoreType.DMA((n, 3))], name, SEQ_ID_CHIPS)


def _adamw_math(w, g, m, v):
    m = ADAM_B1 * m + (1.0 - ADAM_B1) * g
    v = ADAM_B2 * v + (1.0 - ADAM_B2) * jnp.square(g)
    m_hat = m / (1.0 - ADAM_B1 ** ADAM_STEP)
    v_hat = v / (1.0 - ADAM_B2 ** ADAM_STEP)
    delta = -ADAM_LR * (m_hat / (jnp.sqrt(v_hat) + ADAM_EPS) + ADAM_WD * w)
    return delta, m, v


def _row_tile(rows, cap):
    best = None
    for cand in range(8, min(rows, cap) + 1, 8):
        if rows % cand == 0:
            best = cand
    assert best is not None, rows
    return best


def _adamw(w, m, v, own, parts, me, name, layer, prev=None, tr=256):
    r, ccols = w.shape[-2:]
    npart = len(parts)
    if r % 8 == 0:
        tr, tc = _row_tile(r, tr), ccols
        steps, at = r // tr, (lambda i: (i, 0))
    else:
        tr, tc = r, 256
        assert ccols % tc == 0
        steps, at = ccols // tc, (lambda i: (0, i))

    def spec(lead):
        return pl.BlockSpec((None, tr, tc), lambda i, me_ref: (lead,) + at(i))

    grad, kind = own
    if kind == "col":
        own_spec = pl.BlockSpec((tr, tc), lambda i, me_ref: (at(i)[0], me_ref[0]))
    elif kind == "row":
        own_spec = pl.BlockSpec((tr, tc), lambda i, me_ref: (me_ref[0] * (r // tr) + at(i)[0], 0))
    else:
        own_spec = pl.BlockSpec((None, tr, tc), lambda i, me_ref: (me_ref[0],) + at(i))

    def body(me_ref, *refs):
        w_ref, m_ref, v_ref = refs[:3]
        p_refs = refs[3:4 + npart]
        outs = refs[len(refs) - 4:]
        g = p_refs[0][...].astype(F32)
        for p_ref in p_refs[1:]:
            g = g + p_ref[...].astype(F32)
        delta, mn, vn = _adamw_math(w_ref[...], g, m_ref[...], v_ref[...])
        outs[0][...] = g
        outs[1][...] = delta
        outs[2][...] = mn
        outs[3][...] = vn

    operands = [w, m, v, grad] + [p for p, _ in parts]
    in_specs = [spec(layer)] * 3 + [own_spec] + [spec(lead) for _, lead in parts]
    aliases = {}
    if prev is not None:
        for i, p in enumerate(prev):
            aliases[1 + len(operands)] = i
            operands.append(p)
            in_specs.append(pl.BlockSpec(memory_space=pl.ANY))
    return pl.pallas_call(
        body, name=name,
        grid_spec=pltpu.PrefetchScalarGridSpec(num_scalar_prefetch=1, grid=(steps,), in_specs=in_specs,
                                               out_specs=[spec(layer)] * 4),
        out_shape=[jax.ShapeDtypeStruct(w.shape, F32)] * 4, input_output_aliases=aliases,
        compiler_params=_cp(VMEM_BIG))(me, *operands)


def _small_update(gathered, params, loss_all, me, name):
    n = len(gathered)
    shapes = [w.shape for w, _, _ in params]

    def body(me_ref, *refs):
        g_refs, loss_ref = refs[:n], refs[n]
        p_refs = refs[n + 1:n + 1 + 3 * n]
        o_refs = refs[n + 1 + 3 * n:]
        for i in range(n):
            r, c = shapes[i]
            if gathered[i].shape[2] == c:
                parts = [g_refs[i][j] for j in range(N_DEV)]
            else:
                off = pl.multiple_of(me_ref[0] * c, 128)
                parts = [g_refs[i][j, :, pl.ds(off, c)] for j in range(N_DEV)]
            g = functools.reduce(lambda a, b: a + b, parts)
            delta, mn, vn = _adamw_math(p_refs[3 * i][...], g, p_refs[3 * i + 1][...], p_refs[3 * i + 2][...])
            for k, val in enumerate((g, delta, mn, vn)):
                o_refs[4 * i + k][...] = val
        o_refs[4 * n][...] = functools.reduce(lambda a, b: a + b, [loss_ref[j] for j in range(N_DEV)])

    vmem = pl.BlockSpec(memory_space=pltpu.VMEM)
    flat_params = [a for p in params for a in p]
    outs = pl.pallas_call(
        body, name=name, in_specs=[pl.BlockSpec(memory_space=pltpu.SMEM)] + [vmem] * (n + 1 + 3 * n),
        out_specs=[vmem] * (4 * n + 1),
        out_shape=[jax.ShapeDtypeStruct(shp, F32) for shp in shapes for _ in range(4)] + [jax.ShapeDtypeStruct((1, 128), F32)],
        compiler_params=_cp(VMEM_BIG))(me, *gathered, loss_all, *flat_params)
    return [tuple(outs[4 * i:4 * i + 4]) for i in range(n)], outs[4 * n]


def _pack(arrays):
    pieces, layout, off = [], [], 0
    for a in arrays:
        n = a.size
        padded = -(-n // 1024) * 1024
        flat = a.reshape(-1).astype(F32)
        if padded != n:
            flat = jnp.pad(flat, (0, padded - n))
        pieces.append(flat.reshape(padded // 128, 128))
        layout.append((off, n, a.shape))
        off += padded // 128
    return jnp.concatenate(pieces, axis=0), layout


def kernel(x, mem, norm_mix, norm_ffn, mem_norm, w_kv, w_out, w_ffn1, w_ffn2, a_in, a_ln_g, a_ln_b, a_ws, a_bs, b_in, b_conv_w, b_conv_b, b_dt_bias, b_a_log, b_d, b_gnorm, final_norm, loss_target, m_norm_mix, m_norm_ffn, m_mem_norm, m_w_kv, m_w_out, m_w_ffn1, m_w_ffn2, m_a_in, m_a_ln_g, m_a_ln_b, m_a_ws, m_a_bs, m_b_in, m_b_conv_w, m_b_conv_b, m_b_dt_bias, m_b_a_log, m_b_d, m_b_gnorm, m_final_norm, v_norm_mix, v_norm_ffn, v_mem_norm, v_w_kv, v_w_out, v_w_ffn1, v_w_ffn2, v_a_in, v_a_ln_g, v_a_ln_b, v_a_ws, v_a_bs, v_b_in, v_b_conv_w, v_b_conv_b, v_b_dt_bias, v_b_a_log, v_b_d, v_b_gnorm, v_final_norm):
    s = x.shape[1]
    xs = x.reshape(s, D_MODEL)
    mems = mem.reshape(N_MEM, D_MODEL)
    target = loss_target.reshape(s, D_MODEL)
    ax, ay, ac = lax.axis_index("x"), lax.axis_index("y"), lax.axis_index("c")
    me = 4 * ax + 2 * ay + ac
    xyc = jnp.stack([ax, ay, ac]).astype(jnp.int32)
    me1 = me.astype(jnp.int32).reshape(1)

    b_cols = b_in.shape[2]
    act = lambda a: a.astype(_ACT)
    lay_f1, lay_f2 = ("col", 512, (1, D_MODEL, D_FF)), ("row", 512, (1, D_FF, D_MODEL))
    lay_out, lay_kv = ("row", 384, (1, 3 * D_MODEL, D_MODEL)), ("col", 256, (1, D_MODEL, 2 * X_WIDTH))
    small_w_pack = _pack([b_conv_w[0], b_conv_b[0], b_gnorm[0]])[0]
    (WA,) = _all_gather_seq([act(a_in)], [("col", 640, (1, D_MODEL, 5 * D_MODEL))], "ag_proj_a")
    wo0, wkv0 = _all_gather_seq([act(w_out[0:1]), act(w_kv[0:1])], [lay_out, lay_kv], "ag_out0")
    (w1_0,) = _all_gather_seq([act(w_ffn1[0:1])], [lay_f1], "ag_ffn0_up")
    (w2_0,) = _all_gather_seq([act(w_ffn2[0:1])], [lay_f2], "ag_ffn0_down")
    a0 = _rms_fwd(xs, norm_mix[0].reshape(1, -1), "mix_norm0")
    tr_b = lambda a: jnp.swapaxes(a, 1, 2)
    wbt_blk, small_w = _all_gather_seq(
        [act(tr_b(b_in)[0]), small_w_pack],
        [("blk", 0, (N_DEV, b_cols, D_MODEL)), ("blk", 0, (N_DEV, 32, 128))], "ag_proj_b", after=[a0])
    wo1, wkv1 = _all_gather_seq([act(w_out[1:2]), act(w_kv[1:2])], [lay_out, lay_kv], "ag_out1", after=[a0])
    w1_1, w2_1 = _all_gather_seq([act(w_ffn1[1:2]), act(w_ffn2[1:2])], [lay_f1, lay_f2], "ag_ffn1", after=[a0])
    W1, W2, WO, WKV = [w1_0, w1_1], [w2_0, w2_1], [wo0, wo1], [wkv0, wkv1]
    dt0 = D_INNER + CONV_DIM

    row = lambda a: a.reshape(1, -1)
    nmix = [row(norm_mix[0]), row(norm_mix[1])]
    nffn = [row(norm_ffn[0]), row(norm_ffn[1])]
    nmem = [row(mem_norm[0]), row(mem_norm[1])]
    fin = row(final_norm)
    lng, lnb = a_ln_g.reshape(1, D_INNER), a_ln_b.reshape(1, D_INNER)
    ws = a_ws[0]
    bs3 = a_bs[0].reshape(A_GROUPS, CHUNK, 1)
    pad_h = lambda a: jnp.pad(a.reshape(-1), (0, HPAD - SSM_HEADS))
    bias_row = pad_h(b_dt_bias).reshape(1, HPAD)
    alog_row = pad_h(b_a_log).reshape(1, HPAD)
    dfull = jnp.repeat(b_d.reshape(-1), SSM_P).reshape(1, D_INNER)

    kvs, mns = [None, None], [None, None]

    def mem_kv(i, after=None):
        gain = nmem[i] if after is None else _tie(nmem[i], after)
        mns[i] = _rms_fwd(mems, gain, f"mem_norm{i}")
        kvs[i] = _mm(mns[i], WKV[i], m=N_MEM, n=2 * X_WIDTH, k=D_MODEL, b_at=(0, 0, 0), out_dtype=_ACT, name=f"kv{i}")

    def ffn_fwd(h, i, after=()):
        f = _rms_fwd(h, nffn[i], f"ffn_norm{i}")
        p = _mm(f, W1[i], m=s, n=D_FF, k=D_MODEL, b_at=(0, 0, 0), out_dtype=_ACT, after=after, name=f"ffn_up{i}")
        hn = _mm(p, W2[i], m=s, n=D_MODEL, k=D_FF, b_at=(0, 0, 0), a_pro="relu2", add=h, name=f"ffn_down{i}")
        return f, p, hn

    def out_proj(h, cat, i):
        return _mm(cat, WO[i], m=s, n=D_MODEL, k=3 * D_MODEL, b_at=(0, 0, 0), add=h, name=f"out_proj{i}")

    proj_a = _mm(a0, WA, m=s, n=5 * D_MODEL, k=D_MODEL, b_at=(0, 0, 0), name="proj_a")
    cat_a = _gmlp_fwd(proj_a, lng, lnb, ws, bs3, "gmlp_fwd")
    mem_kv(0, after=[cat_a])
    cat_a = _attn_fwd(proj_a, 4, kvs[0], cat_a, "attn_fwd0")
    h1 = out_proj(xs, cat_a, 0)

    wbt_blk, small_w, _ = lax.optimization_barrier((wbt_blk, small_w, h1))
    jd, lo = divmod(dt0, b_cols)
    assert lo + SSM_HEADS <= b_cols
    wbt_full = wbt_blk.reshape(N_DEV * b_cols, D_MODEL)
    WBT = jnp.concatenate([wbt_full[:dt0], wbt_full[dt0 + SSM_HEADS:]], axis=0)
    WBDT = jnp.pad(wbt_full[dt0:dt0 + SSM_HEADS], ((0, HPAD - SSM_HEADS), (0, 0)))
    cw_sh, cb_sh, gn_sh = 4 * 384, 384, 256
    sw = small_w.reshape(N_DEV, 32 * 128)
    conv_w = jnp.transpose(sw[:, :cw_sh].reshape(N_DEV, CONV_K, 384), (1, 0, 2)).reshape(CONV_K, CONV_DIM)
    conv_b = sw[:, 2048:2048 + cb_sh].reshape(1, CONV_DIM)
    gnorm = sw[:, 3072:3072 + gn_sh].reshape(1, D_INNER)

    f0, p0, h2 = ffn_fwd(h1, 0, after=[WBT, WBDT])
    a1 = _rms_fwd(h2, nmix[1], "mix_norm1")
    proj_b = _mm(a1, WBT, m=s, n=6 * D_MODEL, k=D_MODEL, tb=True, name="proj_b")
    dt_raw = _mm(a1, WBDT, m=s, n=HPAD, k=D_MODEL, tb=True, name="proj_dt")
    xbc = _conv_fwd(proj_b, conv_w, conv_b, "conv_fwd")
    y_ssd, states, *ssd_expansions = _ssd_fwd(xbc, dt_raw, bias_row, alog_row, dfull, "ssd_fwd")
    cat_b = _gate_fwd(y_ssd, proj_b, gnorm, "gate_fwd")
    mem_kv(1, after=[cat_b])
    cat_b = _attn_fwd(proj_b, 5, kvs[1], cat_b, "attn_fwd1")
    h3 = out_proj(h2, cat_b, 1)
    f1, p1, h4 = ffn_fwd(h3, 1)

    loss_part, dh, dh_act, d_fin = _loss_head(h4, fin, target, "loss_head")

    g_f1, g_f2, g_out, g_kv = [None, None], [None, None], [None, None], [None, None]
    d_nffn, d_nmix, d_nmem = [None, None], [None, None], [None, None]

    def ffn_bwd(dh, dh_act, h_in, f, p, i, after=(), after_last=()):
        dp = _mm(dh_act, W2[i], m=s, n=D_FF, k=D_MODEL, tb=True, b_at=(0, 0, 0), epi_p=p, out_dtype=_ACT, name=f"ffn_down_dx{i}")
        g_f2[i] = _mm(p, dh_act, m=D_FF, n=D_MODEL, k=s, ta=True, a_pro="relu2", out_dtype=_ACT, name=f"ffn_down_dw{i}")
        g_f1[i] = _mm(f, dp, m=D_MODEL, n=D_FF, k=s, ta=True, out_dtype=_ACT, name=f"ffn_up_dw{i}")
        df = _mm(dp, W1[i], m=s, n=D_MODEL, k=D_FF, tb=True, b_at=(0, 0, 0), after=after, name=f"ffn_up_dx{i}")
        gain = _tie(nffn[i], after_last) if after_last else nffn[i]
        dh_in, dh_in_act, d_nffn[i] = _rms_bwd(h_in, gain, df, dh, f"ffn_norm_bwd{i}")
        return dh_in, dh_in_act

    def out_bwd(dh_act, cat, i):
        dcat = _mm(dh_act, WO[i], m=s, n=3 * D_MODEL, k=D_MODEL, tb=True, b_at=(0, 0, 0), out_dtype=_ACT, name=f"out_dx{i}")
        g_out[i] = _mm(cat, dh_act, m=3 * D_MODEL, n=D_MODEL, k=s, ta=True, out_dtype=_ACT, name=f"out_dw{i}")
        return dcat

    def mem_bwd(dkv, i):
        g_kv[i] = _mm(mns[i], dkv, m=D_MODEL, n=2 * X_WIDTH, k=N_MEM, ta=True, out_dtype=_ACT, name=f"kv_dw{i}")
        dmn = _mm(dkv, WKV[i], m=N_MEM, n=D_MODEL, k=2 * X_WIDTH, tb=True, b_at=(0, 0, 0), name=f"kv_dx{i}")
        _, _, d_nmem[i] = _rms_bwd(mems, nmem[i], dmn, None, f"mem_norm_bwd{i}")

    lay_g = {"f1": ("col", 512, (D_MODEL, 512)), "f2": ("row", 512, (512, D_MODEL)), "out": ("row", 384, (384, D_MODEL)),
             "kv": ("col", 256, (D_MODEL, 256)), "a": ("col", 640, (D_MODEL, 640)), "b": ("blk", 0, (b_cols, D_MODEL))}
    reduced = {}

    def reduce_scatter(group, tag, after=(), sums_after=()):
        grads3, lays3 = [], []
        for fam, _, g in group:
            kind, width, shape = lay_g[fam]
            grads3.append(g if kind == "blk" else g.reshape((1,) + g.shape))
            lays3.append((kind, width, shape if kind == "blk" else (1,) + shape))
        recv1 = _rs_to_sibling(grads3, lays3, f"rs_sibling_{tag}", after)
        if sums_after:
            recv1 = lax.optimization_barrier((tuple(recv1), tuple(sums_after)))[0]
        recv1 = [recv1[t].reshape((4,) + lay_g[fam][2]) for t, (fam, _, _) in enumerate(group)]
        parts = [_rs_chip_sum(g, r1, lay_g[fam], xyc, f"rs_chip_sum_{fam}{i}") for r1, (fam, i, g) in zip(recv1, group)]
        recv2 = _rs_across_chips(parts, f"rs_chips_{tag}")
        for (fam, i, g), r1, r2 in zip(group, recv1, recv2):
            reduced[fam, i] = (g, r1, r2)
        return parts, recv2

    dh3, dh3_act = ffn_bwd(dh, dh_act, h3, f1, p1, 1)
    dcat_b = out_bwd(dh3_act, cat_b, 1)
    sums, got_ffn1 = reduce_scatter([("f1", 1, g_f1[1]), ("f2", 1, g_f2[1]), ("out", 1, g_out[1])], "ffn1", sums_after=[dcat_b])
    dy_ssd, dproj_b, d_gnorm = _gate_bwd(y_ssd, proj_b, gnorm, dcat_b, "gate_bwd")
    dproj_b, dkv_b = _attn_bwd(proj_b, 5, kvs[1], dcat_b, dproj_b, "attn_bwd1")
    mem_bwd(dkv_b, 1)
    dxbc, ddt_raw, d_alog, d_dskip, d_dtbias = _ssd_bwd(
        xbc, dt_raw, _tie(bias_row, sums), alog_row, dfull, dy_ssd, states, ssd_expansions, "ssd_bwd")
    dproj_b, d_convw, d_convb = _conv_bwd(proj_b, conv_w, _tie(conv_b, got_ffn1), dxbc, dproj_b, "conv_bwd")
    gb = _mm(dproj_b, a1, m=6 * D_MODEL, n=D_MODEL, k=s, ta=True, out_dtype=_ACT, name="proj_b_dw")
    gb_dt = _mm(ddt_raw, a1, m=HPAD, n=D_MODEL, k=s, ta=True, out_dtype=_ACT, name="proj_b_dw_dt")
    blocks_b = [gb[j * b_cols:(j + 1) * b_cols] for j in range(jd)]
    blocks_b.append(jnp.concatenate([gb[jd * b_cols:dt0], gb_dt[:SSM_HEADS], gb[dt0:(jd + 1) * b_cols - SSM_HEADS]], axis=0))
    blocks_b += [gb[j * b_cols - SSM_HEADS:(j + 1) * b_cols - SSM_HEADS] for j in range(jd + 1, N_DEV)]
    gb_blk = jnp.stack(blocks_b)
    da1 = _mm(dproj_b, WBT, m=s, n=D_MODEL, k=6 * D_MODEL, name="proj_b_dx")
    sums, got_mix1 = reduce_scatter([("kv", 1, g_kv[1]), ("b", 0, gb_blk)], "mix1", sums_after=[da1])
    da1 = _mm(ddt_raw, WBDT, m=s, n=D_MODEL, k=HPAD, add=da1, name="proj_b_dx_dt")
    dh2, dh2_act, d_nmix[1] = _rms_bwd(h2, _tie(nmix[1], sums), da1, dh3, "mix_norm_bwd1")

    dh1, dh1_act = ffn_bwd(dh2, dh2_act, h1, f0, p0, 0, after=got_ffn1, after_last=got_mix1)
    dcat_a = out_bwd(dh1_act, cat_a, 0)
    sums, got_ffn0 = reduce_scatter([("f1", 0, g_f1[0]), ("f2", 0, g_f2[0]), ("out", 0, g_out[0])], "ffn0", sums_after=[dcat_a])
    dproj_a, d_ws, d_bs3, d_lng, d_lnb = _gmlp_bwd(proj_a, dcat_a, _tie(lng, sums), lnb, ws, bs3, "gmlp_bwd")
    dproj_a, dkv_a = _attn_bwd(proj_a, 4, kvs[0], dcat_a, dproj_a, "attn_bwd0")
    mem_bwd(dkv_a, 0)

    def big_update(w, m, v, fam, nlayer):
        res = None
        for i in range(nlayer):
            grad, recv1, recv2 = reduced[fam, i]
            plist = [(recv1, 0), (recv2, 0), (recv2, 1), (recv2, 2)]
            res = _adamw(w, m, v, (grad, lay_g[fam][0]), plist, me1, f"adamw_{fam}{i}", layer=i, prev=res)
        return res

    da0 = _mm(dproj_a, WA, m=s, n=D_MODEL, k=5 * D_MODEL, tb=True, b_at=(0, 0, 0), name="proj_a_dx")
    grad_x, _, d_nmix[0] = _rms_bwd(xs, nmix[0], da0, dh1, "mix_norm_bwd0")
    ga = _mm(a0, dproj_a, m=D_MODEL, n=5 * D_MODEL, k=s, ta=True, out_dtype=_ACT, after=[grad_x], name="proj_a_dw")
    r_b = big_update(tr_b(b_in), tr_b(m_b_in), tr_b(v_b_in), "b", 1)
    reduce_scatter([("kv", 0, g_kv[0]), ("a", 0, ga)], "mix0", after=got_ffn0, sums_after=r_b)
    r_b = [tr_b(o) for o in r_b]

    small_names = ["norm_mix", "norm_ffn", "mem_norm", "a_ln_g", "a_ln_b", "a_ws", "a_bs", "b_dt_bias", "b_a_log", "b_d",
                   "final_norm", "b_conv_w", "b_conv_b", "b_gnorm"]
    small_grads = [jnp.concatenate(d_nmix, axis=0), jnp.concatenate(d_nffn, axis=0), jnp.concatenate(d_nmem, axis=0),
                   d_lng, d_lnb, d_ws.reshape(A_GROUPS * CHUNK, CHUNK), d_bs3.reshape(A_GROUPS, CHUNK),
                   d_dtbias[:, :SSM_HEADS], d_alog[:, :SSM_HEADS], d_dskip[:, :SSM_HEADS], d_fin,
                   d_convw, d_convb, d_gnorm]
    small_2d = [(2, D_MODEL)] * 3 + [(1, D_INNER)] * 2 + [(A_GROUPS * CHUNK, CHUNK), (A_GROUPS, CHUNK)] + [(1, SSM_HEADS)] * 3 \
        + [(1, D_MODEL), (CONV_K, 384), (1, 384), (1, 256)]
    gathered = _all_gather_seq(
        small_grads + [loss_part], [("blk", 0, (N_DEV,) + g.shape) for g in small_grads + [loss_part]], "ag_small_grads")

    r_f1 = big_update(w_ffn1, m_w_ffn1, v_w_ffn1, "f1", 2)
    r_f2 = big_update(w_ffn2, m_w_ffn2, v_w_ffn2, "f2", 2)
    r_out = big_update(w_out, m_w_out, v_w_out, "out", 2)
    r_kv = big_update(w_kv, m_w_kv, v_w_kv, "kv", 2)
    r_a = big_update(a_in, m_a_in, v_a_in, "a", 1)

    small_w = [norm_mix, norm_ffn, mem_norm, a_ln_g, a_ln_b, a_ws, a_bs, b_dt_bias, b_a_log, b_d, final_norm,
               b_conv_w, b_conv_b, b_gnorm]
    small_m = [m_norm_mix, m_norm_ffn, m_mem_norm, m_a_ln_g, m_a_ln_b, m_a_ws, m_a_bs, m_b_dt_bias, m_b_a_log, m_b_d,
               m_final_norm, m_b_conv_w, m_b_conv_b, m_b_gnorm]
    small_v = [v_norm_mix, v_norm_ffn, v_mem_norm, v_a_ln_g, v_a_ln_b, v_a_ws, v_a_bs, v_b_dt_bias, v_b_a_log, v_b_d,
               v_final_norm, v_b_conv_w, v_b_conv_b, v_b_gnorm]
    params = [tuple(a.reshape(shp) for a in wmv) for shp, wmv in zip(small_2d, zip(small_w, small_m, small_v))]
    loss_all = _tie(gathered[-1], [r_a[0], r_kv[0]])
    small_res, loss_sum = _small_update(gathered[:-1], params, loss_all, me1, "adamw_small")
    loss = loss_sum[0, 0]

    names = ["norm_mix", "norm_ffn", "mem_norm", "w_kv", "w_out", "w_ffn1", "w_ffn2", "a_in", "a_ln_g", "a_ln_b", "a_ws",
             "a_bs", "b_in", "b_conv_w", "b_conv_b", "b_dt_bias", "b_a_log", "b_d", "b_gnorm", "final_norm"]
    big = {"w_kv": r_kv, "w_out": r_out, "w_ffn1": r_f1, "w_ffn2": r_f2, "a_in": r_a, "b_in": r_b}
    outs = [loss, grad_x.reshape(x.shape)]
    for kind in range(4):
        for nm in names:
            if nm in big:
                outs.append(big[nm][kind])
            else:
                i = small_names.index(nm)
                outs.append(small_res[i][kind].reshape(small_w[i].shape))
    return tuple(outs)
```

```python
import functools
import math

import jax
import jax.numpy as jnp
from jax import lax
from jax.experimental import pallas as pl
from jax.experimental.pallas import tpu as pltpu
from jax.experimental.pallas import tpu_sc as plsc

F32 = jnp.float32
_MXU = jnp.bfloat16
_ACT = jnp.bfloat16

D_MODEL = 1024
CHUNK = 128
N_MEM = 256
D_INNER = 2048
A_GROUPS = 8
A_GW = D_INNER // A_GROUPS
SSM_HEADS = 32
SSM_P = 64
SSM_GROUPS = 4
SSM_GW = D_INNER // SSM_GROUPS
SSM_N = 128
CONV_K = 4
CONV_DIM = 3072
X_HEADS = 4
X_HD = 256
X_WIDTH = 1024
D_FF = 4096
EPS = 1e-6
HPAD = 128
N_DEV = 8

ADAM_LR = 0.001
ADAM_B1 = 0.9
ADAM_B2 = 0.999
ADAM_EPS = 1e-08
ADAM_WD = 0.01
ADAM_STEP = 10

VMEM_BIG = 56 * 1024 * 1024
MESH = pl.DeviceIdType.MESH


def _cp(vmem=None):
    if vmem is None:
        return pltpu.CompilerParams()
    return pltpu.CompilerParams(vmem_limit_bytes=vmem)


def _dot(a, b, dims=((1,), (0,))):
    return lax.dot_general(a.astype(_MXU), b.astype(_MXU), (dims, ((), ())), preferred_element_type=F32)


def _dot_nt(a, b):
    return _dot(a, b, ((1,), (1,)))


def _dot_tn(a, b):
    return _dot(a, b, ((0,), (0,)))


def _split3(x):
    x1 = x.astype(jnp.bfloat16)
    r = x - x1.astype(F32)
    x2 = r.astype(jnp.bfloat16)
    x3 = (r - x2.astype(F32)).astype(jnp.bfloat16)
    return x1, x2, x3


def _dot_sel(x, sel, dims=((1,), (0,)), terms=2):
    sel = sel.astype(jnp.bfloat16)
    parts = [lax.dot_general(t, sel, (dims, ((), ())), preferred_element_type=F32) for t in _split3(x)[:terms]]
    return functools.reduce(lambda a, b: a + b, parts)


def _sel_dot(sel, x, dims=((1,), (0,))):
    sel = sel.astype(jnp.bfloat16)
    parts = [lax.dot_general(sel, t, (dims, ((), ())), preferred_element_type=F32) for t in _split3(x)]
    return (parts[0] + parts[1]) + parts[2]


def _sigmoid(x):
    return 1.0 / (1.0 + jnp.exp(-x))


def _gelu(x):
    return 0.5 * x * (1.0 + lax.erf(x * (1.0 / math.sqrt(2.0))))


def _gelu_with_grad(x):
    phi = 0.5 * (1.0 + lax.erf(x * (1.0 / math.sqrt(2.0))))
    return x * phi, phi + x * jnp.exp(-0.5 * x * x) * (1.0 / math.sqrt(2.0 * math.pi))


def _softplus(x):
    return jnp.maximum(x, 0.0) + jnp.log1p(jnp.exp(-jnp.abs(x)))


def _iota(shape, dim):
    return lax.broadcasted_iota(jnp.int32, shape, dim)


MM_VMEM_BUDGET = 40 * 1024 * 1024
HBM_BYTES_PER_S = 2.5e12
GRID_STEP_S = 0.35e-6
VMEM_ACC_BYTES_PER_S = 6e12


def _divisors(dim, unit):
    out = [d for d in range(unit, min(dim, 2048) + 1, unit) if dim % d == 0]
    return out if out else [dim]


def _mm_tiles(m, n, k, sa, sb, s_mn, a_pro, offsets):
    best = None
    (a_r0, a_c0, ta), (b_r0, b_c0, tb), (o_r0, o_c0) = offsets
    for tm in _divisors(m, 128):
        for tn in _divisors(n, 128):
            for tk in [k // d for d in (1, 2, 3, 4, 6, 8) if k % d == 0 and (k // d) % 128 == 0]:
                a_t = (tk, tm) if ta else (tm, tk)
                b_t = (tn, tk) if tb else (tk, tn)
                if a_r0 % a_t[0] or a_c0 % a_t[1] or b_r0 % b_t[0] or b_c0 % b_t[1] or o_r0 % tm or o_c0 % tn:
                    continue
                nk = k // tk
                vmem = 2 * (tm * tk * sa + tk * tn * sb + tm * tn * s_mn) + tm * tn * 4 * (2 if nk > 1 else 1)
                if a_pro or sa == 4:
                    vmem += tm * tk * 6
                if sb == 4:
                    vmem += tk * tn * 2
                if vmem > MM_VMEM_BUDGET:
                    continue
                gi, gj = m // tm, n // tn
                for j_inner in (True, False):
                    if nk > 1:
                        traffic = gj * m * k * sa + gi * k * n * sb
                    elif j_inner:
                        traffic = m * k * sa + gi * k * n * sb
                    else:
                        traffic = gj * m * k * sa + k * n * sb
                    traffic += m * n * s_mn + (tm * tk * sa + tk * tn * sb)
                    cost = traffic / HBM_BYTES_PER_S + gi * gj * nk * GRID_STEP_S
                    if nk > 1:
                        cost += m * n * 8 * nk / VMEM_ACC_BYTES_PER_S
                    if best is None or cost < best[0]:
                        best = (cost, tm, tn, tk, j_inner)
    assert best is not None, (m, n, k)
    return best[1:]


def _mm(a, b, *, m, n, k, name, ta=False, tb=False, a_at=(None, 0, 0), b_at=(None, 0, 0),
        out_dtype=F32, add=None, epi_p=None, epi_at=(None, 0, 0), out=None, out_at=(None, 0, 0),
        out_full=None, a_pro=None, after=()):
    s_mn =jnp.dtype(out.dtype if out is not None else out_dtype).itemsize
    s_mn += add.dtype.itemsize if add is not None else 0
    s_mn += epi_p.dtype.itemsize if epi_p is not None else 0
    tm, tn, tk, j_inner = _mm_tiles(m, n, k, a.dtype.itemsize, b.dtype.itemsize, s_mn, a_pro is not None,
                                    ((a_at[1], a_at[2], ta), (b_at[1], b_at[2], tb), (out_at[1], out_at[2])))
    nk = k // tk

    def spec(at, tr, tc, rsel, csel):
        lead, r0, c0 = at
        assert r0 % tr == 0 and c0 % tc == 0, (name, at, tr, tc)
        rb, cb = r0 // tr, c0 // tc
        if lead is None:
            return pl.BlockSpec((tr, tc), lambda g0, g1, kk: (rb + rsel(g0, g1, kk), cb + csel(g0, g1, kk)))
        return pl.BlockSpec((None, tr, tc), lambda g0, g1, kk: (lead, rb + rsel(g0, g1, kk), cb + csel(g0, g1, kk)))

    gi = (lambda g0, g1, kk: g0) if j_inner else (lambda g0, g1, kk: g1)
    gj = (lambda g0, g1, kk: g1) if j_inner else (lambda g0, g1, kk: g0)
    gk = lambda g0, g1, kk: kk
    a_spec = spec(a_at, tk, tm, gk, gi) if ta else spec(a_at, tm, tk, gi, gk)
    b_spec = spec(b_at, tn, tk, gj, gk) if tb else spec(b_at, tk, tn, gk, gj)
    dims = ((0,), (0,)) if ta else (((1,), (1,)) if tb else ((1,), (0,)))
    assert not (ta and tb)

    operands, in_specs = [a, b], [a_spec, b_spec]
    if add is not None:
        operands.append(add)
        in_specs.append(spec((None, 0, 0), tm, tn, gi, gj))
    if epi_p is not None:
        operands.append(epi_p)
        in_specs.append(spec(epi_at, tm, tn, gi, gj))
    aliases = {}
    if out is not None:
        aliases = {len(operands): 0}
        operands.append(out)
        in_specs.append(pl.BlockSpec(memory_space=pl.ANY))
        out_struct = jax.ShapeDtypeStruct(out.shape, out.dtype)
        out_dtype = out.dtype
    else:
        out_struct = jax.ShapeDtypeStruct(out_full if out_full is not None else (m, n), out_dtype)
    has_add, has_epi = add is not None, epi_p is not None
    n_skip = (1 if out is not None else 0) + len(after)
    operands += list(after)
    in_specs += [pl.BlockSpec(memory_space=pl.ANY)] * len(after)

    def body(*refs):
        a_ref, b_ref = refs[0], refs[1]
        pos = 2
        add_ref = epi_ref = None
        if has_add:
            add_ref = refs[pos]
            pos += 1
        if has_epi:
            epi_ref = refs[pos]
            pos += 1
        pos += n_skip
        o_ref = refs[pos]

        def finish(r):
            if has_add:
                r = r + add_ref[...].astype(F32)
            if has_epi:
                r = r * (2.0 * jnp.maximum(epi_ref[...].astype(F32), 0.0))
            o_ref[...] = r.astype(o_ref.dtype)

        av = a_ref[...]
        if a_pro == "relu2":
            av = jnp.square(jnp.maximum(av.astype(F32), 0.0))
        part = _dot(av, b_ref[...], dims)
        if nk == 1:
            finish(part)
        else:
            acc_ref = refs[pos + 1]
            kk = pl.program_id(2)

            @pl.when(kk == 0)
            def _():
                acc_ref[...] = part

            @pl.when(kk > 0)
            def _():
                acc_ref[...] += part

            @pl.when(kk == nk - 1)
            def _():
                finish(acc_ref[...])

    grid = (m // tm, n // tn, nk) if j_inner else (n // tn, m // tm, nk)
    return pl.pallas_call(
        body, name=name, grid=grid, in_specs=in_specs,
        out_specs=spec(out_at, tm, tn, gi, gj), out_shape=out_struct,
        scratch_shapes=[pltpu.VMEM((tm, tn), F32)] if nk > 1 else [], input_output_aliases=aliases,
        compiler_params=_cp(VMEM_BIG))(*operands)


def _rms_fwd(x, g, name, tm=1024):
    s, d = x.shape
    tm = min(tm, s)

    def body(x_ref, g_ref, o_ref):
        xv = x_ref[...]
        r = lax.rsqrt(jnp.mean(xv * xv, axis=-1, keepdims=True) + EPS)
        o_ref[...] = (xv * r * g_ref[...]).astype(o_ref.dtype)

    return pl.pallas_call(
        body, name=name, grid=(s // tm,),
        in_specs=[pl.BlockSpec((tm, d), lambda i: (i, 0)), pl.BlockSpec((1, d), lambda i: (0, 0))],
        out_specs=pl.BlockSpec((tm, d), lambda i: (i, 0)),
        out_shape=jax.ShapeDtypeStruct((s, d), _ACT), compiler_params=_cp(VMEM_BIG))(x, g)


def _rms_bwd(x, g, dy, dres, name, tm=512):
    s, d = x.shape
    tm = min(tm, s)
    has_res = dres is not None

    def body(*refs):
        if has_res:
            x_ref, g_ref, dy_ref, dres_ref, dx_ref, dxa_ref, dg_ref = refs
        else:
            x_ref, g_ref, dy_ref, dx_ref, dxa_ref, dg_ref = refs

        @pl.when(pl.program_id(0) == 0)
        def _():
            dg_ref[...] = jnp.zeros_like(dg_ref)

        xv = x_ref[...]
        dyv = dy_ref[...].astype(F32)
        r = lax.rsqrt(jnp.mean(xv * xv, axis=-1, keepdims=True) + EPS)
        xh = xv * r
        dyg = dyv * g_ref[...]
        dx = r * (dyg - xh * jnp.mean(dyg * xh, axis=-1, keepdims=True))
        if has_res:
            dx = dx + dres_ref[...]
        dx_ref[...] = dx
        dxa_ref[...] = dx.astype(dxa_ref.dtype)
        dg_ref[...] += jnp.sum(dyv * xh, axis=0, keepdims=True)

    row = pl.BlockSpec((tm, d), lambda i: (i, 0))
    vec = pl.BlockSpec((1, d), lambda i: (0, 0))
    in_specs = [row, vec, row] + ([row] if has_res else [])
    operands = [x, g, dy] + ([dres] if has_res else [])
    return pl.pallas_call(
        body, name=name, grid=(s // tm,), in_specs=in_specs, out_specs=[row, row, vec],
        out_shape=[jax.ShapeDtypeStruct((s, d), F32), jax.ShapeDtypeStruct((s, d), _ACT),
                   jax.ShapeDtypeStruct((1, d), F32)], compiler_params=_cp(VMEM_BIG))(*operands)


def _loss_head(h, g, target, name, tm=512):
    s, d = h.shape
    tm = min(tm, s)

    def body(h_ref, g_ref, t_ref, loss_ref, dh_ref, dha_ref, dg_ref):
        @pl.when(pl.program_id(0) == 0)
        def _():
            dg_ref[...] = jnp.zeros_like(dg_ref)
            loss_ref[...] = jnp.zeros_like(loss_ref)

        xv = h_ref[...]
        r = lax.rsqrt(jnp.mean(xv * xv, axis=-1, keepdims=True) + EPS)
        xh = xv * r
        err = xh * g_ref[...] - t_ref[...]
        loss_ref[...] += jnp.full(loss_ref.shape, 0.5 * jnp.sum(jnp.mean(err * err, axis=-1, keepdims=True)), F32)
        dyv = err * (1.0 / d)
        dyg = dyv * g_ref[...]
        dh = r * (dyg - xh * jnp.mean(dyg * xh, axis=-1, keepdims=True))
        dh_ref[...] = dh
        dha_ref[...] = dh.astype(dha_ref.dtype)
        dg_ref[...] += jnp.sum(dyv * xh, axis=0, keepdims=True)

    row = pl.BlockSpec((tm, d), lambda i: (i, 0))
    vec = pl.BlockSpec((1, d), lambda i: (0, 0))
    return pl.pallas_call(
        body, name=name, grid=(s // tm,), in_specs=[row, vec, row],
        out_specs=[pl.BlockSpec((1, 128), lambda i: (0, 0)), row, row, vec],
        out_shape=[jax.ShapeDtypeStruct((1, 128), F32), jax.ShapeDtypeStruct((s, d), F32),
                   jax.ShapeDtypeStruct((s, d), _ACT), jax.ShapeDtypeStruct((1, d), F32)],
        compiler_params=_cp(VMEM_BIG))(h, g, target)


def _gmlp_parts(u, v, lng, lnb):
    mu = jnp.mean(v, axis=-1, keepdims=True)
    vc = v - mu
    rstd = lax.rsqrt(jnp.mean(vc * vc, axis=-1, keepdims=True) + EPS)
    xhat = vc * rstd
    vn = xhat * lng + lnb
    return u, xhat, rstd, vn


def _gmlp_fwd(proj, lng, lnb, ws, bs3, name):
    s = proj.shape[0]

    def body(pu_ref, pv_ref, lng_ref, lnb_ref, ws_ref, bs_ref, o_ref):
        u, _, _, vn = _gmlp_parts(_gelu(pu_ref[...]), _gelu(pv_ref[...]), lng_ref[...], lnb_ref[...])
        causal = _iota((CHUNK, CHUNK), 0) >= _iota((CHUNK, CHUNK), 1)
        for g in range(A_GROUPS):
            sl = slice(g * A_GW, (g + 1) * A_GW)
            w = jnp.where(causal, ws_ref[g], 0.0)
            sv = _dot(w, vn[:, sl]) + bs_ref[g]
            o_ref[:, sl] = (u[:, sl] * sv).astype(o_ref.dtype)

    full = lambda shape: pl.BlockSpec(shape, lambda c: (0,) * len(shape))
    return pl.pallas_call(
        body, name=name, grid=(s // CHUNK,),
        in_specs=[pl.BlockSpec((CHUNK, D_INNER), lambda c: (c, 0)), pl.BlockSpec((CHUNK, D_INNER), lambda c: (c, 1)),
                  full((1, D_INNER)), full((1, D_INNER)), full((A_GROUPS, CHUNK, CHUNK)), full((A_GROUPS, CHUNK, 1))],
        out_specs=pl.BlockSpec((CHUNK, D_INNER), lambda c: (c, 0)),
        out_shape=jax.ShapeDtypeStruct((s, D_INNER + X_WIDTH), _ACT), compiler_params=_cp(VMEM_BIG))(proj, proj, lng, lnb, ws, bs3)


def _gmlp_bwd(proj, dcat, lng, lnb, ws, bs3, name):
    s = proj.shape[0]

    def body(pu_ref, pv_ref, dm_ref, lng_ref, lnb_ref, ws_ref, bs_ref, dp_ref, dws_ref, dbs_ref, dlng_ref, dlnb_ref, dvn_ref):
        @pl.when(pl.program_id(0) == 0)
        def _():
            dws_ref[...] = jnp.zeros_like(dws_ref)
            dbs_ref[...] = jnp.zeros_like(dbs_ref)
            dlng_ref[...] = jnp.zeros_like(dlng_ref)
            dlnb_ref[...] = jnp.zeros_like(dlnb_ref)

        lng = lng_ref[...]
        u, u_grad = _gelu_with_grad(pu_ref[...])
        v, v_grad = _gelu_with_grad(pv_ref[...])
        u, xhat, rstd, vn = _gmlp_parts(u, v, lng, lnb_ref[...])
        dm = dm_ref[...].astype(F32)
        causal = _iota((CHUNK, CHUNK), 0) >= _iota((CHUNK, CHUNK), 1)
        for g in range(A_GROUPS):
            sl = slice(g * A_GW, (g + 1) * A_GW)
            w = jnp.where(causal, ws_ref[g], 0.0)
            sv = _dot(w, vn[:, sl]) + bs_ref[g]
            dsv = dm[:, sl] * u[:, sl]
            dp_ref[:, sl] = (dm[:, sl] * sv * u_grad[:, sl]).astype(dp_ref.dtype)
            dvn_ref[:, sl] = _dot_tn(w, dsv)
            dws_ref[g] += jnp.where(causal, _dot_nt(dsv, vn[:, sl]), 0.0)
            dbs_ref[g] += jnp.sum(dsv, axis=-1, keepdims=True)
        dvn = dvn_ref[...]
        dlng_ref[...] += jnp.sum(dvn * xhat, axis=0, keepdims=True)
        dlnb_ref[...] += jnp.sum(dvn, axis=0, keepdims=True)
        dxh = dvn * lng
        dv = rstd * (dxh - jnp.mean(dxh, axis=-1, keepdims=True) - xhat * jnp.mean(dxh * xhat, axis=-1, keepdims=True))
        dp_ref[:, D_INNER:] = (dv * v_grad).astype(dp_ref.dtype)

    full = lambda shape: pl.BlockSpec(shape, lambda c: (0,) * len(shape))
    return pl.pallas_call(
        body, name=name, grid=(s // CHUNK,),
        in_specs=[pl.BlockSpec((CHUNK, D_INNER), lambda c: (c, 0)), pl.BlockSpec((CHUNK, D_INNER), lambda c: (c, 1)),
                  pl.BlockSpec((CHUNK, D_INNER), lambda c: (c, 0)),
                  full((1, D_INNER)), full((1, D_INNER)), full((A_GROUPS, CHUNK, CHUNK)), full((A_GROUPS, CHUNK, 1))],
        out_specs=[pl.BlockSpec((CHUNK, 2 * D_INNER), lambda c: (c, 0)), full((A_GROUPS, CHUNK, CHUNK)),
                   full((A_GROUPS, CHUNK, 1)), full((1, D_INNER)), full((1, D_INNER))],
        out_shape=[jax.ShapeDtypeStruct((s, 2 * D_INNER + X_WIDTH), _ACT), jax.ShapeDtypeStruct((A_GROUPS, CHUNK, CHUNK), F32),
                   jax.ShapeDtypeStruct((A_GROUPS, CHUNK, 1), F32), jax.ShapeDtypeStruct((1, D_INNER), F32),
                   jax.ShapeDtypeStruct((1, D_INNER), F32)],
        scratch_shapes=[pltpu.VMEM((CHUNK, D_INNER), F32)],
        compiler_params=_cp(VMEM_BIG))(proj, proj, dcat, lng, lnb, ws, bs3)


_X_SCALE = 1.0 / math.sqrt(X_HD)


def _attn_fwd(proj, qblk, kv, cat, name, tm=512):
    s = proj.shape[0]
    tm = min(tm, s)

    def body(q_ref, kv_ref, cat_ref, o_ref):
        for h in range(X_HEADS):
            sl = slice(h * X_HD, (h + 1) * X_HD)
            k = kv_ref[:, sl]
            v = kv_ref[:, X_WIDTH + h * X_HD:X_WIDTH + (h + 1) * X_HD]
            sc = _dot_nt(q_ref[:, sl], k) * _X_SCALE
            e = jnp.exp(sc - jnp.max(sc, axis=-1, keepdims=True))
            p = e / jnp.sum(e, axis=-1, keepdims=True)
            o_ref[:, sl] = _dot(p, v).astype(o_ref.dtype)

    return pl.pallas_call(
        body, name=name, grid=(s // tm,),
        in_specs=[pl.BlockSpec((tm, X_WIDTH), lambda i: (i, qblk)), pl.BlockSpec((N_MEM, 2 * X_WIDTH), lambda i: (0, 0)),
                  pl.BlockSpec(memory_space=pl.ANY)],
        out_specs=pl.BlockSpec((tm, X_WIDTH), lambda i: (i, D_INNER // X_WIDTH)),
        out_shape=jax.ShapeDtypeStruct(cat.shape, cat.dtype), input_output_aliases={2: 0},
        compiler_params=_cp(VMEM_BIG))(proj, kv, cat)


def _attn_bwd(proj, qblk, kv, dcat, dproj, name, tm=512):
    s = proj.shape[0]
    tm = min(tm, s)

    def body(q_ref, kv_ref, do_ref, dproj_ref, dq_ref, dkv_ref):
        @pl.when(pl.program_id(0) == 0)
        def _():
            dkv_ref[...] = jnp.zeros_like(dkv_ref)

        for h in range(X_HEADS):
            sl = slice(h * X_HD, (h + 1) * X_HD)
            slv = slice(X_WIDTH + h * X_HD, X_WIDTH + (h + 1) * X_HD)
            q = q_ref[:, sl]
            k = kv_ref[:, sl]
            v = kv_ref[:, slv]
            do = do_ref[:, sl].astype(F32)
            sc = _dot_nt(q, k) * _X_SCALE
            e = jnp.exp(sc - jnp.max(sc, axis=-1, keepdims=True))
            p = e / jnp.sum(e, axis=-1, keepdims=True)
            dp = _dot_nt(do, v)
            ds = p * (dp - jnp.sum(dp * p, axis=-1, keepdims=True)) * _X_SCALE
            dq_ref[:, sl] = _dot(ds, k).astype(dq_ref.dtype)
            dkv_ref[:, sl] += _dot_tn(ds, q)
            dkv_ref[:, slv] += _dot_tn(p, do)

    return pl.pallas_call(
        body, name=name, grid=(s // tm,),
        in_specs=[pl.BlockSpec((tm, X_WIDTH), lambda i: (i, qblk)), pl.BlockSpec((N_MEM, 2 * X_WIDTH), lambda i: (0, 0)),
                  pl.BlockSpec((tm, X_WIDTH), lambda i: (i, 2)), pl.BlockSpec(memory_space=pl.ANY)],
        out_specs=[pl.BlockSpec((tm, X_WIDTH), lambda i: (i, qblk)), pl.BlockSpec((N_MEM, 2 * X_WIDTH), lambda i: (0, 0))],
        out_shape=[jax.ShapeDtypeStruct(dproj.shape, dproj.dtype), jax.ShapeDtypeStruct((N_MEM, 2 * X_WIDTH), F32)],
        input_output_aliases={3: 0}, compiler_params=_cp(VMEM_BIG))(proj, kv, dcat, dproj)


CONV_TC = 256
_XBC_BLK0 = D_INNER // CONV_TC


CONV_RB = 64
SUBLANES = 8


def _rows_before(cur, prev_last, j):
    rolled = pltpu.roll(cur, j, 0)
    head = jnp.where(_iota((SUBLANES, cur.shape[1]), 0) < j, pltpu.roll(prev_last, j, 0), rolled[:SUBLANES])
    return jnp.concatenate([head, rolled[SUBLANES:]], axis=0)


def _rows_after(cur, next_first, j):
    n = cur.shape[0]
    rolled = pltpu.roll(cur, n - j, 0)
    tail = jnp.where(_iota((SUBLANES, cur.shape[1]), 0) >= SUBLANES - j, pltpu.roll(next_first, SUBLANES - j, 0),
                     rolled[n - SUBLANES:])
    return jnp.concatenate([rolled[:n - SUBLANES], tail], axis=0)


def _conv_pre(x_ref, w_ref, b_ref, r0, prev_last):
    cur = x_ref[pl.ds(r0, CONV_RB), :]
    shifts = [_rows_before(cur, prev_last, j) for j in range(1, CONV_K)]
    pre = b_ref[...] + w_ref[CONV_K - 1:CONV_K, :] * cur
    for j in range(1, CONV_K):
        pre = pre + w_ref[CONV_K - 1 - j:CONV_K - j, :] * shifts[j - 1]
    return pre, cur, shifts


def _conv_fwd(proj, w, b, name):
    s = proj.shape[0]

    def body(x_ref, w_ref, b_ref, o_ref):
        xv = x_ref[...]
        rows = _iota(xv.shape, 0)
        pre = b_ref[...] + w_ref[CONV_K - 1:CONV_K, :] * xv
        for j in range(1, CONV_K):
            pre = pre + w_ref[CONV_K - 1 - j:CONV_K - j, :] * jnp.where(rows >= j, pltpu.roll(xv, j, 0), 0.0)
        o_ref[...] = pre * _sigmoid(pre)

    return pl.pallas_call(
        body, name=name, grid=(CONV_DIM // CONV_TC,),
        in_specs=[pl.BlockSpec((s, CONV_TC), lambda j: (0, _XBC_BLK0 + j)), pl.BlockSpec((CONV_K, CONV_TC), lambda j: (0, j)),
                  pl.BlockSpec((1, CONV_TC), lambda j: (0, j))],
        out_specs=pl.BlockSpec((s, CONV_TC), lambda j: (0, j)),
        out_shape=jax.ShapeDtypeStruct((s, CONV_DIM), F32), compiler_params=_cp(VMEM_BIG))(proj, w, b)


def _conv_bwd(proj, w, b, dxbc, dproj, name):
    s = proj.shape[0]

    nb = s // CONV_RB

    def body(x_ref, w_ref, b_ref, d_ref, dproj_ref, dx_ref, dw_ref, db_ref, dpre_ref):
        def fold(v):
            out = v[:SUBLANES]
            for t in range(1, CONV_RB // SUBLANES):
                out = out + v[t * SUBLANES:(t + 1) * SUBLANES]
            return out

        def first(i, carry):
            prev_last, acc = carry
            r0 = pl.multiple_of(i * CONV_RB, CONV_RB)
            pre, cur, shifts = _conv_pre(x_ref, w_ref, b_ref, r0, prev_last)
            sig = _sigmoid(pre)
            dpre = d_ref[pl.ds(r0, CONV_RB), :] * (sig * (1.0 + pre * (1.0 - sig)))
            dpre_ref[pl.ds(r0, CONV_RB), :] = dpre
            taps = [cur] + shifts
            acc = tuple(a + fold(dpre * t) for a, t in zip(acc[:CONV_K], taps)) + (acc[CONV_K] + fold(dpre),)
            return cur[CONV_RB - SUBLANES:], acc

        zero8 = jnp.zeros((SUBLANES, CONV_TC), F32)
        _, acc = lax.fori_loop(0, nb, first, (zero8, (zero8,) * (CONV_K + 1)))
        for j in range(CONV_K):
            dw_ref[CONV_K - 1 - j:CONV_K - j, :] = jnp.sum(acc[j], axis=0, keepdims=True)
        db_ref[...] = jnp.sum(acc[CONV_K], axis=0, keepdims=True)

        def second(i, next_first):
            r0 = pl.multiple_of((nb - 1 - i) * CONV_RB, CONV_RB)
            cur = dpre_ref[pl.ds(r0, CONV_RB), :]
            dx = w_ref[CONV_K - 1:CONV_K, :] * cur
            for j in range(1, CONV_K):
                dx = dx + w_ref[CONV_K - 1 - j:CONV_K - j, :] * _rows_after(cur, next_first, j)
            dx_ref[pl.ds(r0, CONV_RB), :] = dx.astype(dx_ref.dtype)
            return cur[:SUBLANES]

        lax.fori_loop(0, nb, second, zero8)

    return pl.pallas_call(
        body, name=name, grid=(CONV_DIM // CONV_TC,),
        in_specs=[pl.BlockSpec((s, CONV_TC), lambda j: (0, _XBC_BLK0 + j)), pl.BlockSpec((CONV_K, CONV_TC), lambda j: (0, j)),
                  pl.BlockSpec((1, CONV_TC), lambda j: (0, j)), pl.BlockSpec((s, CONV_TC), lambda j: (0, j)),
                  pl.BlockSpec(memory_space=pl.ANY)],
        out_specs=[pl.BlockSpec((s, CONV_TC), lambda j: (0, _XBC_BLK0 + j)), pl.BlockSpec((CONV_K, CONV_TC), lambda j: (0, j)),
                   pl.BlockSpec((1, CONV_TC), lambda j: (0, j))],
        out_shape=[jax.ShapeDtypeStruct(dproj.shape, dproj.dtype), jax.ShapeDtypeStruct((CONV_K, CONV_DIM), F32),
                   jax.ShapeDtypeStruct((1, CONV_DIM), F32)], input_output_aliases={4: 0},
        scratch_shapes=[pltpu.VMEM((s, CONV_TC), F32)],
        compiler_params=_cp(VMEM_BIG))(proj, w, b, dxbc, dproj)


def _ssd_common(dtc_ref, br_ref, ar_ref, csb_ref, cst_ref, csf_ref, dtf_ref, expand):
    a_row = -jnp.exp(ar_ref[...])
    dt_c = _softplus(dtc_ref[...] + br_ref[...])
    tril = _iota((CHUNK, CHUNK), 0) >= _iota((CHUNK, CHUNK), 1)
    cs = _sel_dot(tril, dt_c * a_row)
    cst_ref[...] = cs.T
    e64 = (jnp.right_shift(_iota((HPAD, D_INNER), 1), 6) == _iota((HPAD, D_INNER), 0)).astype(jnp.bfloat16)
    if expand:
        e128 = jnp.right_shift(_iota((HPAD, SSM_HEADS * CHUNK), 1), 7) == _iota((HPAD, SSM_HEADS * CHUNK), 0)
        csb_ref[...] = _dot_sel(cs, e128)
        dtf_ref[...] = _dot_sel(dt_c, e64)
        csf_ref[...] = _dot_sel(cs, e64)
    dt_full = dtf_ref[...]
    cs_full = csf_ref[...]
    cs_last = csf_ref[CHUNK - 1:CHUNK, :]
    e_full = jnp.exp(cs_full)
    f_full = jnp.exp(cs_last - cs_full)
    gamma = jnp.exp(cs_last)
    return a_row, dt_c, cs, dt_full, e_full, f_full, gamma, e64


def _ssd_lambda(csb_ref, cst_ref, h, causal):
    diff = csb_ref[:, h * CHUNK:(h + 1) * CHUNK] - cst_ref[h:h + 1, :]
    return jnp.exp(jnp.where(causal, diff, -1e30))


_SSD_VEC_SPECS = lambda: [pl.BlockSpec((1, HPAD), lambda c: (0, 0)), pl.BlockSpec((1, HPAD), lambda c: (0, 0)),
                          pl.BlockSpec((1, D_INNER), lambda c: (0, 0))]


def _ssd_fwd(xbc, dtc, bias_row, alog_row, dfull, name):
    s = xbc.shape[0]
    nc = s // CHUNK

    def body(xbc_ref, dtc_ref, br_ref, ar_ref, df_ref, y_ref, st_ref, csb_ref, csf_ref, dtf_ref, ht_ref, cst_ref):
        @pl.when(pl.program_id(0) == 0)
        def _():
            ht_ref[...] = jnp.zeros_like(ht_ref)

        _, _, _, dt_full, e_full, f_full, gamma, _ = _ssd_common(
            dtc_ref, br_ref, ar_ref, csb_ref, cst_ref, csf_ref, dtf_ref, expand=True)
        x = xbc_ref[:, :D_INNER]
        xdt = x * dt_full
        st_ref[...] = ht_ref[...]
        causal = _iota((CHUNK, CHUNK), 0) >= _iota((CHUNK, CHUNK), 1)
        lo = _iota((CHUNK, CHUNK), 1) < SSM_P
        for g in range(SSM_GROUPS):
            gs = slice(g * SSM_GW, (g + 1) * SSM_GW)
            bg = xbc_ref[:, D_INNER + g * SSM_N:D_INNER + (g + 1) * SSM_N]
            cg = xbc_ref[:, D_INNER + SSM_GROUPS * SSM_N + g * SSM_N:D_INNER + SSM_GROUPS * SSM_N + (g + 1) * SSM_N]
            ht = ht_ref[:, gs]
            cb = _dot_nt(cg, bg)
            yoff = e_full[:, gs] * _dot(cg, ht)
            for jp in range(SSM_GW // CHUNK):
                j = g * (SSM_GW // CHUNK) + jp
                ps = slice(j * CHUNK, (j + 1) * CHUNK)
                x2 = xdt[:, ps]
                y0 = _dot(cb * _ssd_lambda(csb_ref, cst_ref, 2 * j, causal), x2)
                y1 = _dot(cb * _ssd_lambda(csb_ref, cst_ref, 2 * j + 1, causal), x2)
                y_ref[:, ps] = (jnp.where(lo, y0, y1) + yoff[:, jp * CHUNK:(jp + 1) * CHUNK]
                                + x[:, ps] * df_ref[:, ps])
            ht_ref[:, gs] = gamma[:, gs] * ht + _dot_tn(bg, xdt[:, gs] * f_full[:, gs])

    return pl.pallas_call(
        body, name=name, grid=(nc,),
        in_specs=[pl.BlockSpec((CHUNK, CONV_DIM), lambda c: (c, 0)), pl.BlockSpec((CHUNK, HPAD), lambda c: (c, 0))]
                 + _SSD_VEC_SPECS(),
        out_specs=[pl.BlockSpec((CHUNK, D_INNER), lambda c: (c, 0)), pl.BlockSpec((None, SSM_N, D_INNER), lambda c: (c, 0, 0)),
                   pl.BlockSpec((CHUNK, SSM_HEADS * CHUNK), lambda c: (c, 0)), pl.BlockSpec((CHUNK, D_INNER), lambda c: (c, 0)),
                   pl.BlockSpec((CHUNK, D_INNER), lambda c: (c, 0))],
        out_shape=[jax.ShapeDtypeStruct((s, D_INNER), F32), jax.ShapeDtypeStruct((nc, SSM_N, D_INNER), F32),
                   jax.ShapeDtypeStruct((s, SSM_HEADS * CHUNK), F32), jax.ShapeDtypeStruct((s, D_INNER), F32),
                   jax.ShapeDtypeStruct((s, D_INNER), F32)],
        scratch_shapes=[pltpu.VMEM((SSM_N, D_INNER), F32), pltpu.VMEM((HPAD, CHUNK), F32)],
        compiler_params=_cp(VMEM_BIG))(xbc, dtc, bias_row, alog_row, dfull)


def _ssd_bwd(xbc, dtc, bias_row, alog_row, dfull, dy, states, expansions, name):
    s = xbc.shape[0]
    nc = s // CHUNK
    rev = lambda c: nc - 1 - c

    def body(xbc_ref, dtc_ref, br_ref, ar_ref, df_ref, dy_ref, st_ref, csb_ref, csf_ref, dtf_ref,
             dxbc_ref, ddt_ref, dalog_ref, dd_ref, dbias_ref,
             dht_ref, cst_ref, ddf_ref, dxs_ref, dcsf_ref, dcsl_ref):
        step = pl.program_id(0)

        @pl.when(step == 0)
        def _():
            dht_ref[...] = jnp.zeros_like(dht_ref)
            ddf_ref[...] = jnp.zeros_like(ddf_ref)
            dalog_ref[...] = jnp.zeros_like(dalog_ref)
            dbias_ref[...] = jnp.zeros_like(dbias_ref)
            dd_ref[...] = jnp.zeros_like(dd_ref)

        a_row, dt_c, _, dt_full, e_full, f_full, gamma, e64 = _ssd_common(
            dtc_ref, br_ref, ar_ref, csb_ref, cst_ref, csf_ref, dtf_ref, expand=False)
        x = xbc_ref[:, :D_INNER]
        xdt = x * dt_full
        dy_all = dy_ref[...]
        ddf_ref[...] += jnp.broadcast_to(jnp.sum(dy_all * x, axis=0, keepdims=True), ddf_ref.shape)
        causal = _iota((CHUNK, CHUNK), 0) >= _iota((CHUNK, CHUNK), 1)
        lo = _iota((CHUNK, CHUNK), 1) < SSM_P
        head_lane = _iota((CHUNK, HPAD), 1)
        head_row = _iota((HPAD, CHUNK), 0)
        dcs_heads = jnp.zeros((CHUNK, HPAD), F32)
        dcs_cols = jnp.zeros((HPAD, CHUNK), F32)
        for g in range(SSM_GROUPS):
            gs = slice(g * SSM_GW, (g + 1) * SSM_GW)
            b0 = D_INNER + g * SSM_N
            c0 = D_INNER + SSM_GROUPS * SSM_N + g * SSM_N
            bg = xbc_ref[:, b0:b0 + SSM_N]
            cg = xbc_ref[:, c0:c0 + SSM_N]
            ht = st_ref[:, gs]
            dht = dht_ref[:, gs]
            dyg = dy_all[:, gs]
            eg, fg, gg = e_full[:, gs], f_full[:, gs], gamma[:, gs]
            z = _dot(cg, ht)
            dz = dyg * eg
            dcg = _dot_nt(dz, ht)
            dht_new = _dot_tn(cg, dz) + gg * dht
            xf = xdt[:, gs] * fg
            dxf = _dot(bg, dht)
            dbg = _dot_nt(xf, dht)
            dff = dxf * xf
            dcsf_ref[:, gs] = dyg * eg * z - dff
            dcsl_ref[:, gs] = jnp.broadcast_to(
                jnp.sum(dff, axis=0, keepdims=True) + jnp.sum(dht * ht, axis=0, keepdims=True) * gg, (8, SSM_GW))
            cb = _dot_nt(cg, bg)
            dcb = jnp.zeros((CHUNK, CHUNK), F32)
            for jp in range(SSM_GW // CHUNK):
                j = g * (SSM_GW // CHUNK) + jp
                ps = slice(j * CHUNK, (j + 1) * CHUNK)
                x2 = xdt[:, ps]
                dy2 = dy_all[:, ps]
                dxh = []
                for hh in range(2):
                    h = 2 * j + hh
                    lam = _ssd_lambda(csb_ref, cst_ref, h, causal)
                    mh = cb * lam
                    dyh = jnp.where(lo, dy2, 0.0) if hh == 0 else jnp.where(lo, 0.0, dy2)
                    dm = _dot_nt(dyh, x2)
                    dcb = dcb + dm * lam
                    gm = dm * mh
                    dcs_heads = dcs_heads + jnp.where(head_lane == h, jnp.sum(gm, axis=1, keepdims=True), 0.0)
                    dcs_cols = dcs_cols + jnp.where(head_row == h, jnp.sum(gm, axis=0, keepdims=True), 0.0)
                    dxh.append(_dot_tn(mh, dy2))
                dxs_ref[:, ps] = jnp.where(lo, dxh[0], dxh[1]) + dxf[:, jp * CHUNK:(jp + 1) * CHUNK] * fg[:, jp * CHUNK:(jp + 1) * CHUNK]
            dxbc_ref[:, b0:b0 + SSM_N] = (dbg + _dot_tn(dcb, cg)).astype(dxbc_ref.dtype)
            dxbc_ref[:, c0:c0 + SSM_N] = (dcg + _dot(dcb, bg)).astype(dxbc_ref.dtype)
            dht_ref[:, gs] = dht_new
        dxs = dxs_ref[...]
        dcs_heads = dcs_heads - dcs_cols.T + _dot_sel(dcsf_ref[...], e64, ((1,), (1,)))
        dcs_last = _dot_sel(dcsl_ref[...], e64, ((1,), (1,)))
        dcs_heads = dcs_heads + jnp.where(_iota((CHUNK, HPAD), 0) == CHUNK - 1, dcs_last[0:1, :], 0.0)
        triu = _iota((CHUNK, CHUNK), 0) <= _iota((CHUNK, CHUNK), 1)
        dda = _sel_dot(triu, dcs_heads)
        ddt = dda * a_row + _dot_sel(dxs * x, e64, ((1,), (1,)))
        dxbc_ref[:, :D_INNER] = (dxs * dt_full + dy_all * df_ref[...]).astype(dxbc_ref.dtype)
        dalog_ref[...] += jnp.sum(dda * dt_c, axis=0, keepdims=True) * a_row
        ddt_raw = ddt * _sigmoid(dtc_ref[...] + br_ref[...])
        ddt_ref[...] = ddt_raw.astype(ddt_ref.dtype)
        dbias_ref[...] += jnp.sum(ddt_raw, axis=0, keepdims=True)

        @pl.when(step == nc - 1)
        def _():
            dd_ref[...] = _dot_sel(ddf_ref[...], e64, ((1,), (1,)))[0:1, :]

    vec = pl.BlockSpec((1, HPAD), lambda c: (0, 0))
    return pl.pallas_call(
        body, name=name, grid=(nc,),
        in_specs=[pl.BlockSpec((CHUNK, CONV_DIM), lambda c: (rev(c), 0)), pl.BlockSpec((CHUNK, HPAD), lambda c: (rev(c), 0))]
                 + _SSD_VEC_SPECS()
                 + [pl.BlockSpec((CHUNK, D_INNER), lambda c: (rev(c), 0)),
                    pl.BlockSpec((None, SSM_N, D_INNER), lambda c: (rev(c), 0, 0)),
                    pl.BlockSpec((CHUNK, SSM_HEADS * CHUNK), lambda c: (rev(c), 0)),
                    pl.BlockSpec((CHUNK, D_INNER), lambda c: (rev(c), 0)), pl.BlockSpec((CHUNK, D_INNER), lambda c: (rev(c), 0))],
        out_specs=[pl.BlockSpec((CHUNK, CONV_DIM), lambda c: (rev(c), 0)), pl.BlockSpec((CHUNK, HPAD), lambda c: (rev(c), 0)),
                   vec, vec, vec],
        out_shape=[jax.ShapeDtypeStruct((s, CONV_DIM), F32), jax.ShapeDtypeStruct((s, HPAD), _ACT),
                   jax.ShapeDtypeStruct((1, HPAD), F32), jax.ShapeDtypeStruct((1, HPAD), F32),
                   jax.ShapeDtypeStruct((1, HPAD), F32)],
        scratch_shapes=[pltpu.VMEM((SSM_N, D_INNER), F32), pltpu.VMEM((HPAD, CHUNK), F32),
                        pltpu.VMEM((8, D_INNER), F32), pltpu.VMEM((CHUNK, D_INNER), F32),
                        pltpu.VMEM((CHUNK, D_INNER), F32), pltpu.VMEM((8, D_INNER), F32)],
        compiler_params=_cp(VMEM_BIG))(xbc, dtc, bias_row, alog_row, dfull, dy, states, *expansions)


def _gate_fwd(y, proj, gn, name, tm=512):
    s = y.shape[0]
    tm = min(tm, s)

    def body(y_ref, z_ref, gn_ref, o_ref):
        for g in range(SSM_GROUPS):
            gs = slice(g * SSM_GW, (g + 1) * SSM_GW)
            z = z_ref[:, gs]
            t = y_ref[:, gs] * (z * _sigmoid(z))
            r = lax.rsqrt(jnp.mean(t * t, axis=-1, keepdims=True) + EPS)
            o_ref[:, gs] = (t * r * gn_ref[:, gs]).astype(o_ref.dtype)

    row = pl.BlockSpec((tm, D_INNER), lambda i: (i, 0))
    return pl.pallas_call(
        body, name=name, grid=(s // tm,), in_specs=[row, row, pl.BlockSpec((1, D_INNER), lambda i: (0, 0))],
        out_specs=row, out_shape=jax.ShapeDtypeStruct((s, D_INNER + X_WIDTH), _ACT),
        compiler_params=_cp(VMEM_BIG))(y, proj, gn)


def _gate_bwd(y, proj, gn, dcat, name, tm=512):
    s = y.shape[0]
    tm = min(tm, s)

    def body(y_ref, z_ref, gn_ref, dm_ref, dy_ref, dz_ref, dgn_ref):
        @pl.when(pl.program_id(0) == 0)
        def _():
            dgn_ref[...] = jnp.zeros_like(dgn_ref)

        for g in range(SSM_GROUPS):
            gs = slice(g * SSM_GW, (g + 1) * SSM_GW)
            z = z_ref[:, gs]
            yv = y_ref[:, gs]
            sig = _sigmoid(z)
            sz = z * sig
            t = yv * sz
            r = lax.rsqrt(jnp.mean(t * t, axis=-1, keepdims=True) + EPS)
            th = t * r
            dm = dm_ref[:, gs].astype(F32)
            dmg = dm * gn_ref[:, gs]
            dt_ = r * (dmg - th * jnp.mean(dmg * th, axis=-1, keepdims=True))
            dgn_ref[:, gs] += jnp.sum(dm * th, axis=0, keepdims=True)
            dy_ref[:, gs] = dt_ * sz
            dz_ref[:, gs] = (dt_ * yv * (sig * (1.0 + z * (1.0 - sig)))).astype(dz_ref.dtype)

    row = pl.BlockSpec((tm, D_INNER), lambda i: (i, 0))
    vec = pl.BlockSpec((1, D_INNER), lambda i: (0, 0))
    return pl.pallas_call(
        body, name=name, grid=(s // tm,), in_specs=[row, row, vec, row], out_specs=[row, row, vec],
        out_shape=[jax.ShapeDtypeStruct((s, D_INNER), F32), jax.ShapeDtypeStruct((s, 6 * D_MODEL), _ACT),
                   jax.ShapeDtypeStruct((1, D_INNER), F32)], compiler_params=_cp(VMEM_BIG))(y, proj, gn, dcat)


def _block_of(kind, width):
    if kind == "col":
        return lambda ref, j: ref.at[:, :, pl.ds(pl.multiple_of(j * width, 128), width)]
    if kind == "row":
        return lambda ref, j: ref.at[:, pl.ds(pl.multiple_of(j * width, 8), width), :]
    return lambda ref, j: ref.at[j]


def _coords():
    return lax.axis_index("x"), lax.axis_index("y"), lax.axis_index("c")


def _rel_chip(x, y, k):
    return (1 - x if k & 1 else x), (1 - y if k & 2 else y)


def _all_gather_body(ins, outs, send_sems, recv_sems, local_sems, blocks):
    n = len(ins)
    x, y, c = _coords()
    sibling = (x, y, 1 - c)
    via = (x + (1 - c) * (1 - 2 * x), y + c * (1 - 2 * y))
    onto = (x + c * (1 - 2 * x), y + (1 - c) * (1 - 2 * y))

    def copy(t, k, chip, core, to, src=None):
        dst = blocks[t](outs[t], 4 * chip[0] + 2 * chip[1] + core)
        return pltpu.make_async_remote_copy(
            src_ref=dst if src is None else src, dst_ref=dst, send_sem=send_sems.at[t, k],
            recv_sem=recv_sems.at[t, k], device_id=to, device_id_type=MESH)

    started = []
    for t in range(n):
        mine = pltpu.make_async_copy(ins[t], blocks[t](outs[t], 4 * x + 2 * y + c), local_sems.at[t])
        mine.start()
        started.append(mine)
    sends = []
    for t in range(n):
        for k in range(3):
            px, py = _rel_chip(x, y, k)
            cp = copy(t, k, (x, y), c, (px, py, 1 - c if k == 0 else c), src=ins[t])
            cp.start()
            sends.append(cp)
    for t in range(n):
        for k in (1, 2):
            chip = _rel_chip(x, y, k)
            copy(t, k, chip, c, sibling).wait_recv()
            fwd = copy(t, 3 + k, chip, c, sibling)
            fwd.start()
            sends.append(fwd)
        hop = copy(t, 3, via, c, (*onto, c))
        hop.start()
        sends.append(hop)
    for t in range(n):
        diagonal = _rel_chip(x, y, 3)
        copy(t, 3, diagonal, c, sibling).wait_recv()
        fwd = copy(t, 6, diagonal, c, sibling)
        fwd.start()
        sends.append(fwd)
    for t in range(n):
        copy(t, 0, (x, y), 1 - c, sibling).wait_recv()
        for k in range(1, 4):
            copy(t, 3 + k, _rel_chip(x, y, k), 1 - c, sibling).wait_recv()
    for cp in sends:
        cp.wait_send()
    for mine in started:
        mine.wait()


def _handshake(peers):
    barrier = pltpu.get_barrier_semaphore()
    for peer in peers:
        pl.semaphore_signal(barrier, inc=1, device_id=peer, device_id_type=MESH)
    pl.semaphore_wait(barrier, len(peers))


def _gather_peers():
    x, y, c = _coords()
    return [(x, y, 1 - c)] + [(*_rel_chip(x, y, k), c) for k in (1, 2)]


SEQ_ID_GATHER, SEQ_ID_SIBLING, SEQ_ID_CHIPS = 1, 2, 3


def _sequencer_call(body, peers, operands, out_types, sems, name, collective_id, after=()):
    n_in, n_out, n_after = len(operands), len(out_types), len(after)

    def launch(*refs):
        _handshake(peers())
        body(refs[:n_in], refs[n_in + n_after:n_in + n_after + n_out], *refs[n_in + n_after + n_out:])

    return pl.kernel(
        launch, name=name, out_type=out_types, mesh=plsc.ScalarSubcoreMesh(axis_name="seq", num_cores=1),
        scratch_types=sems, compiler_params=pltpu.CompilerParams(collective_id=collective_id))(*operands, *after)


def _all_gather_seq(shards, layouts, name, after=()):
    n = len(shards)
    blocks = [_block_of(kind, width) for kind, width, _ in layouts]
    return _sequencer_call(
        lambda ins, outs, *sems: _all_gather_body(ins, outs, *sems, blocks), _gather_peers, shards,
        [jax.ShapeDtypeStruct(shape, sh.dtype) for sh, (_, _, shape) in zip(shards, layouts)],
        [pltpu.SemaphoreType.DMA((n, 7)), pltpu.SemaphoreType.DMA((n, 7)), pltpu.SemaphoreType.DMA((n,))],
        name, SEQ_ID_GATHER, after)


def _tie(small, after):
    return lax.optimization_barrier((small, *after))[0]


def _rs_to_sibling(grads, layouts, name, after=()):
    n = len(grads)
    blocks = [_block_of(kind, width) for kind, width, _ in layouts]

    def body(ins, outs, send_sems, recv_sems):
        x, y, c = _coords()
        sibling = (x, y, 1 - c)
        cps = []
        for t in range(n):
            for k in range(4):
                px, py = _rel_chip(x, y, k)
                cp = pltpu.make_async_remote_copy(
                    src_ref=blocks[t](ins[t], 4 * px + 2 * py + (1 - c)), dst_ref=outs[t].at[k],
                    send_sem=send_sems.at[t, k], recv_sem=recv_sems.at[t, k], device_id=sibling, device_id_type=MESH)
                cp.start()
                cps.append(cp)
        for cp in cps:
            cp.wait_recv()
        for cp in cps:
            cp.wait_send()

    def sibling_only():
        x, y, c = _coords()
        return [(x, y, 1 - c)]

    return _sequencer_call(
        body, sibling_only, grads,
        [jax.ShapeDtypeStruct((4,) + shape, g.dtype) for g, (_, _, shape) in zip(grads, layouts)],
        [pltpu.SemaphoreType.DMA((n, 4)), pltpu.SemaphoreType.DMA((n, 4))], name, SEQ_ID_SIBLING, after)


def _rs_chip_sum(grad, recv, layout, xyc, name):
    kind, width, shape = layout
    r, ccols = shape

    def src_index(step, xyc_ref):
        k = step + 1
        px = jnp.where(k % 2 == 1, 1 - xyc_ref[0], xyc_ref[0])
        py = jnp.where(k // 2 == 1, 1 - xyc_ref[1], xyc_ref[1])
        return 4 * px + 2 * py + xyc_ref[2]

    if kind == "col":
        g_spec = pl.BlockSpec((r, ccols), lambda k, s_: (0, src_index(k, s_)))
    elif kind == "row":
        g_spec = pl.BlockSpec((r, ccols), lambda k, s_: (src_index(k, s_), 0))
    else:
        g_spec = pl.BlockSpec((None, r, ccols), lambda k, s_: (src_index(k, s_), 0, 0))

    def body(xyc_ref, g_ref, r_ref, o_ref):
        o_ref[...] = (g_ref[...].astype(F32) + r_ref[...].astype(F32)).astype(o_ref.dtype)

    slot = pl.BlockSpec((None, r, ccols), lambda k, s_: (k + 1, 0, 0))
    return pl.pallas_call(
        body, name=name,
        grid_spec=pltpu.PrefetchScalarGridSpec(num_scalar_prefetch=1, grid=(3,), in_specs=[g_spec, slot], out_specs=slot),
        out_shape=jax.ShapeDtypeStruct((4, r, ccols), grad.dtype), compiler_params=_cp(VMEM_BIG))(xyc, grad, recv)


def _rs_across_chips(parts, name):
    n = len(parts)

    def body(ins, outs, send_sems, recv_sems):
        x, y, c = _coords()
        cps = []
        for t in range(n):
            for k in range(1, 4):
                px, py = _rel_chip(x, y, k)
                cp = pltpu.make_async_remote_copy(
                    src_ref=ins[t].at[k], dst_ref=outs[t].at[k - 1], send_sem=send_sems.at[t, k - 1],
                    recv_sem=recv_sems.at[t, k - 1], device_id=(px, py, c), device_id_type=MESH)
                cp.start()
                cps.append(cp)
        for cp in cps:
            cp.wait_recv()
        for cp in cps:
            cp.wait_send()

    def other_chips():
        x, y, c = _coords()
        return [(*_rel_chip(x, y, k), c) for k in range(1, 4)]

    return _sequencer_call(
        body, other_chips, parts, [jax.ShapeDtypeStruct((3,) + p.shape[1:], p.dtype) for p in parts],
        [pltpu.SemaphoreType.DMA((n, 3)), pltpu.SemaphoreType.DMA((n, 3))], name, SEQ_ID_CHIPS)


def _adamw_math(w, g, m, v):
    m = ADAM_B1 * m + (1.0 - ADAM_B1) * g
    v = ADAM_B2 * v + (1.0 - ADAM_B2) * jnp.square(g)
    m_hat = m / (1.0 - ADAM_B1 ** ADAM_STEP)
    v_hat = v / (1.0 - ADAM_B2 ** ADAM_STEP)
    delta = -ADAM_LR * (m_hat / (jnp.sqrt(v_hat) + ADAM_EPS) + ADAM_WD * w)
    return delta, m, v


def _row_tile(rows, cap):
    best = None
    for cand in range(8, min(rows, cap) + 1, 8):
        if rows % cand == 0:
            best = cand
    assert best is not None, rows
    return best


def _adamw(w, m, v, own, parts, me, name, layer, prev=None, tr=256):
    r, ccols = w.shape[-2:]
    npart = len(parts)
    if r % 8 == 0:
        tr, tc = _row_tile(r, tr), ccols
        steps, at = r // tr, (lambda i: (i, 0))
    else:
        tr, tc = r, 256
        assert ccols % tc == 0
        steps, at = ccols // tc, (lambda i: (0, i))

    def spec(lead):
        return pl.BlockSpec((None, tr, tc), lambda i, me_ref: (lead,) + at(i))

    grad, kind = own
    if kind == "col":
        own_spec = pl.BlockSpec((tr, tc), lambda i, me_ref: (at(i)[0], me_ref[0]))
    elif kind == "row":
        own_spec = pl.BlockSpec((tr, tc), lambda i, me_ref: (me_ref[0] * (r // tr) + at(i)[0], 0))
    else:
        own_spec = pl.BlockSpec((None, tr, tc), lambda i, me_ref: (me_ref[0],) + at(i))

    def body(me_ref, *refs):
        w_ref, m_ref, v_ref = refs[:3]
        p_refs = refs[3:4 + npart]
        outs = refs[len(refs) - 4:]
        g = p_refs[0][...].astype(F32)
        for p_ref in p_refs[1:]:
            g = g + p_ref[...].astype(F32)
        delta, mn, vn = _adamw_math(w_ref[...], g, m_ref[...], v_ref[...])
        outs[0][...] = g
        outs[1][...] = delta
        outs[2][...] = mn
        outs[3][...] = vn

    operands = [w, m, v, grad] + [p for p, _ in parts]
    in_specs = [spec(layer)] * 3 + [own_spec] + [spec(lead) for _, lead in parts]
    aliases = {}
    if prev is not None:
        for i, p in enumerate(prev):
            aliases[1 + len(operands)] = i
            operands.append(p)
            in_specs.append(pl.BlockSpec(memory_space=pl.ANY))
    return pl.pallas_call(
        body, name=name,
        grid_spec=pltpu.PrefetchScalarGridSpec(num_scalar_prefetch=1, grid=(steps,), in_specs=in_specs,
                                               out_specs=[spec(layer)] * 4),
        out_shape=[jax.ShapeDtypeStruct(w.shape, F32)] * 4, input_output_aliases=aliases,
        compiler_params=_cp(VMEM_BIG))(me, *operands)


def _small_update(gathered, params, loss_all, me, name):
    n = len(gathered)
    shapes = [w.shape for w, _, _ in params]

    def body(me_ref, *refs):
        g_refs, loss_ref = refs[:n], refs[n]
        p_refs = refs[n + 1:n + 1 + 3 * n]
        o_refs = refs[n + 1 + 3 * n:]
        for i in range(n):
            r, c = shapes[i]
            if gathered[i].shape[2] == c:
                parts = [g_refs[i][j] for j in range(N_DEV)]
            else:
                off = pl.multiple_of(me_ref[0] * c, 128)
                parts = [g_refs[i][j, :, pl.ds(off, c)] for j in range(N_DEV)]
            g = functools.reduce(lambda a, b: a + b, parts)
            delta, mn, vn = _adamw_math(p_refs[3 * i][...], g, p_refs[3 * i + 1][...], p_refs[3 * i + 2][...])
            for k, val in enumerate((g, delta, mn, vn)):
                o_refs[4 * i + k][...] = val
        o_refs[4 * n][...] = functools.reduce(lambda a, b: a + b, [loss_ref[j] for j in range(N_DEV)])

    vmem = pl.BlockSpec(memory_space=pltpu.VMEM)
    flat_params = [a for p in params for a in p]
    outs = pl.pallas_call(
        body, name=name, in_specs=[pl.BlockSpec(memory_space=pltpu.SMEM)] + [vmem] * (n + 1 + 3 * n),
        out_specs=[vmem] * (4 * n + 1),
        out_shape=[jax.ShapeDtypeStruct(shp, F32) for shp in shapes for _ in range(4)] + [jax.ShapeDtypeStruct((1, 128), F32)],
        compiler_params=_cp(VMEM_BIG))(me, *gathered, loss_all, *flat_params)
    return [tuple(outs[4 * i:4 * i + 4]) for i in range(n)], outs[4 * n]


def _pack(arrays):
    pieces, layout, off = [], [], 0
    for a in arrays:
        n = a.size
        padded = -(-n // 1024) * 1024
        flat = a.reshape(-1).astype(F32)
        if padded != n:
            flat = jnp.pad(flat, (0, padded - n))
        pieces.append(flat.reshape(padded // 128, 128))
        layout.append((off, n, a.shape))
        off += padded // 128
    return jnp.concatenate(pieces, axis=0), layout


def kernel(x, mem, norm_mix, norm_ffn, mem_norm, w_kv, w_out, w_ffn1, w_ffn2, a_in, a_ln_g, a_ln_b, a_ws, a_bs, b_in, b_conv_w, b_conv_b, b_dt_bias, b_a_log, b_d, b_gnorm, final_norm, loss_target, m_norm_mix, m_norm_ffn, m_mem_norm, m_w_kv, m_w_out, m_w_ffn1, m_w_ffn2, m_a_in, m_a_ln_g, m_a_ln_b, m_a_ws, m_a_bs, m_b_in, m_b_conv_w, m_b_conv_b, m_b_dt_bias, m_b_a_log, m_b_d, m_b_gnorm, m_final_norm, v_norm_mix, v_norm_ffn, v_mem_norm, v_w_kv, v_w_out, v_w_ffn1, v_w_ffn2, v_a_in, v_a_ln_g, v_a_ln_b, v_a_ws, v_a_bs, v_b_in, v_b_conv_w, v_b_conv_b, v_b_dt_bias, v_b_a_log, v_b_d, v_b_gnorm, v_final_norm):
    s = x.shape[1]
    xs = x.reshape(s, D_MODEL)
    mems = mem.reshape(N_MEM, D_MODEL)
    target = loss_target.reshape(s, D_MODEL)
    ax, ay, ac = lax.axis_index("x"), lax.axis_index("y"), lax.axis_index("c")
    me = 4 * ax + 2 * ay + ac
    xyc = jnp.stack([ax, ay, ac]).astype(jnp.int32)
    me1 = me.astype(jnp.int32).reshape(1)

    b_cols = b_in.shape[2]
    act = lambda a: a.astype(_ACT)
    lay_f1, lay_f2 = ("col", 512, (1, D_MODEL, D_FF)), ("row", 512, (1, D_FF, D_MODEL))
    lay_out, lay_kv = ("row", 384, (1, 3 * D_MODEL, D_MODEL)), ("col", 256, (1, D_MODEL, 2 * X_WIDTH))
    small_w_pack = _pack([b_conv_w[0], b_conv_b[0], b_gnorm[0]])[0]
    (WA,) = _all_gather_seq([act(a_in)], [("col", 640, (1, D_MODEL, 5 * D_MODEL))], "ag_proj_a")
    wo0, wkv0 = _all_gather_seq([act(w_out[0:1]), act(w_kv[0:1])], [lay_out, lay_kv], "ag_out0")
    (w1_0,) = _all_gather_seq([act(w_ffn1[0:1])], [lay_f1], "ag_ffn0_up")
    (w2_0,) = _all_gather_seq([act(w_ffn2[0:1])], [lay_f2], "ag_ffn0_down")
    a0 = _rms_fwd(xs, norm_mix[0].reshape(1, -1), "mix_norm0")
    tr_b = lambda a: jnp.swapaxes(a, 1, 2)
    wbt_blk, small_w = _all_gather_seq(
        [act(tr_b(b_in)[0]), small_w_pack],
        [("blk", 0, (N_DEV, b_cols, D_MODEL)), ("blk", 0, (N_DEV, 32, 128))], "ag_proj_b", after=[a0])
    wo1, wkv1 = _all_gather_seq([act(w_out[1:2]), act(w_kv[1:2])], [lay_out, lay_kv], "ag_out1", after=[a0])
    w1_1, w2_1 = _all_gather_seq([act(w_ffn1[1:2]), act(w_ffn2[1:2])], [lay_f1, lay_f2], "ag_ffn1", after=[a0])
    W1, W2, WO, WKV = [w1_0, w1_1], [w2_0, w2_1], [wo0, wo1], [wkv0, wkv1]
    dt0 = D_INNER + CONV_DIM

    row = lambda a: a.reshape(1, -1)
    nmix = [row(norm_mix[0]), row(norm_mix[1])]
    nffn = [row(norm_ffn[0]), row(norm_ffn[1])]
    nmem = [row(mem_norm[0]), row(mem_norm[1])]
    fin = row(final_norm)
    lng, lnb = a_ln_g.reshape(1, D_INNER), a_ln_b.reshape(1, D_INNER)
    ws = a_ws[0]
    bs3 = a_bs[0].reshape(A_GROUPS, CHUNK, 1)
    pad_h = lambda a: jnp.pad(a.reshape(-1), (0, HPAD - SSM_HEADS))
    bias_row = pad_h(b_dt_bias).reshape(1, HPAD)
    alog_row = pad_h(b_a_log).reshape(1, HPAD)
    dfull = jnp.repeat(b_d.reshape(-1), SSM_P).reshape(1, D_INNER)

    kvs, mns = [None, None], [None, None]

    def mem_kv(i, after=None):
        gain = nmem[i] if after is None else _tie(nmem[i], after)
        mns[i] = _rms_fwd(mems, gain, f"mem_norm{i}")
        kvs[i] = _mm(mns[i], WKV[i], m=N_MEM, n=2 * X_WIDTH, k=D_MODEL, b_at=(0, 0, 0), out_dtype=_ACT, name=f"kv{i}")

    def ffn_up(h, i):
        f = _rms_fwd(h, nffn[i], f"ffn_norm{i}")
        return f, _mm(f, W1[i], m=s, n=D_FF, k=D_MODEL, b_at=(0, 0, 0), out_dtype=_ACT, name=f"ffn_up{i}")

    def ffn_down(h, p, i, after=()):
        return _mm(p, W2[i], m=s, n=D_MODEL, k=D_FF, b_at=(0, 0, 0), a_pro="relu2", add=h, after=after, name=f"ffn_down{i}")

    def out_proj(h, cat, i):
        return _mm(cat, WO[i], m=s, n=D_MODEL, k=3 * D_MODEL, b_at=(0, 0, 0), add=h, name=f"out_proj{i}")

    proj_a = _mm(a0, WA, m=s, n=5 * D_MODEL, k=D_MODEL, b_at=(0, 0, 0), name="proj_a")
    cat_a = _gmlp_fwd(proj_a, lng, lnb, ws, bs3, "gmlp_fwd")
    mem_kv(0, after=[cat_a])
    cat_a = _attn_fwd(proj_a, 4, kvs[0], cat_a, "attn_fwd0")
    h1 = out_proj(xs, cat_a, 0)

    f0, p0 = ffn_up(h1, 0)
    wbt_blk, small_w, _ = lax.optimization_barrier((wbt_blk, small_w, p0))
    jd, lo = divmod(dt0, b_cols)
    assert lo + SSM_HEADS <= b_cols
    wbt_full = wbt_blk.reshape(N_DEV * b_cols, D_MODEL)
    WBT = jnp.concatenate([wbt_full[:dt0], wbt_full[dt0 + SSM_HEADS:]], axis=0)
    WBDT = jnp.pad(wbt_full[dt0:dt0 + SSM_HEADS], ((0, HPAD - SSM_HEADS), (0, 0)))
    cw_sh, cb_sh, gn_sh = 4 * 384, 384, 256
    sw = small_w.reshape(N_DEV, 32 * 128)
    conv_w = jnp.transpose(sw[:, :cw_sh].reshape(N_DEV, CONV_K, 384), (1, 0, 2)).reshape(CONV_K, CONV_DIM)
    conv_b = sw[:, 2048:2048 + cb_sh].reshape(1, CONV_DIM)
    gnorm = sw[:, 3072:3072 + gn_sh].reshape(1, D_INNER)

    h2 = ffn_down(h1, p0, 0, after=[WBT, WBDT])
    a1 = _rms_fwd(h2, nmix[1], "mix_norm1")
    proj_b = _mm(a1, WBT, m=s, n=6 * D_MODEL, k=D_MODEL, tb=True, name="proj_b")
    dt_raw = _mm(a1, WBDT, m=s, n=HPAD, k=D_MODEL, tb=True, name="proj_dt")
    xbc = _conv_fwd(proj_b, conv_w, conv_b, "conv_fwd")
    y_ssd, states, *ssd_expansions = _ssd_fwd(xbc, dt_raw, bias_row, alog_row, dfull, "ssd_fwd")
    cat_b = _gate_fwd(y_ssd, proj_b, gnorm, "gate_fwd")
    mem_kv(1, after=[cat_b])
    cat_b = _attn_fwd(proj_b, 5, kvs[1], cat_b, "attn_fwd1")
    h3 = out_proj(h2, cat_b, 1)
    f1, p1 = ffn_up(h3, 1)
    h4 = ffn_down(h3, p1, 1)

    loss_part, dh, dh_act, d_fin = _loss_head(h4, fin, target, "loss_head")

    g_f1, g_f2, g_out, g_kv = [None, None], [None, None], [None, None], [None, None]
    d_nffn, d_nmix, d_nmem = [None, None], [None, None], [None, None]

    def ffn_bwd(dh, dh_act, h_in, f, p, i, after=(), after_last=()):
        dp = _mm(dh_act, W2[i], m=s, n=D_FF, k=D_MODEL, tb=True, b_at=(0, 0, 0), epi_p=p, out_dtype=_ACT, name=f"ffn_down_dx{i}")
        g_f2[i] = _mm(p, dh_act, m=D_FF, n=D_MODEL, k=s, ta=True, a_pro="relu2", out_dtype=_ACT, name=f"ffn_down_dw{i}")
        g_f1[i] = _mm(f, dp, m=D_MODEL, n=D_FF, k=s, ta=True, out_dtype=_ACT, name=f"ffn_up_dw{i}")
        df = _mm(dp, W1[i], m=s, n=D_MODEL, k=D_FF, tb=True, b_at=(0, 0, 0), after=after, name=f"ffn_up_dx{i}")
        gain = _tie(nffn[i], after_last) if after_last else nffn[i]
        dh_in, dh_in_act, d_nffn[i] = _rms_bwd(h_in, gain, df, dh, f"ffn_norm_bwd{i}")
        return dh_in, dh_in_act

    def out_bwd(dh_act, cat, i):
        dcat = _mm(dh_act, WO[i], m=s, n=3 * D_MODEL, k=D_MODEL, tb=True, b_at=(0, 0, 0), out_dtype=_ACT, name=f"out_dx{i}")
        g_out[i] = _mm(cat, dh_act, m=3 * D_MODEL, n=D_MODEL, k=s, ta=True, out_dtype=_ACT, name=f"out_dw{i}")
        return dcat

    def mem_bwd(dkv, i):
        g_kv[i] = _mm(mns[i], dkv, m=D_MODEL, n=2 * X_WIDTH, k=N_MEM, ta=True, out_dtype=_ACT, name=f"kv_dw{i}")
        dmn = _mm(dkv, WKV[i], m=N_MEM, n=D_MODEL, k=2 * X_WIDTH, tb=True, b_at=(0, 0, 0), name=f"kv_dx{i}")
        _, _, d_nmem[i] = _rms_bwd(mems, nmem[i], dmn, None, f"mem_norm_bwd{i}")

    lay_g = {"f1": ("col", 512, (D_MODEL, 512)), "f2": ("row", 512, (512, D_MODEL)), "out": ("row", 384, (384, D_MODEL)),
             "kv": ("col", 256, (D_MODEL, 256)), "a": ("col", 640, (D_MODEL, 640)), "b": ("blk", 0, (b_cols, D_MODEL))}
    reduced = {}

    def reduce_scatter(group, tag, after=(), sums_after=()):
        grads3, lays3 = [], []
        for fam, _, g in group:
            kind, width, shape = lay_g[fam]
            grads3.append(g if kind == "blk" else g.reshape((1,) + g.shape))
            lays3.append((kind, width, shape if kind == "blk" else (1,) + shape))
        recv1 = _rs_to_sibling(grads3, lays3, f"rs_sibling_{tag}", after)
        if sums_after:
            recv1 = lax.optimization_barrier((tuple(recv1), tuple(sums_after)))[0]
        recv1 = [recv1[t].reshape((4,) + lay_g[fam][2]) for t, (fam, _, _) in enumerate(group)]
        parts = [_rs_chip_sum(g, r1, lay_g[fam], xyc, f"rs_chip_sum_{fam}{i}") for r1, (fam, i, g) in zip(recv1, group)]
        recv2 = _rs_across_chips(parts, f"rs_chips_{tag}")
        for (fam, i, g), r1, r2 in zip(group, recv1, recv2):
            reduced[fam, i] = (g, r1, r2)
        return parts, recv2

    dh3, dh3_act = ffn_bwd(dh, dh_act, h3, f1, p1, 1)
    dcat_b = out_bwd(dh3_act, cat_b, 1)
    sums, got_ffn1 = reduce_scatter([("f1", 1, g_f1[1]), ("f2", 1, g_f2[1]), ("out", 1, g_out[1])], "ffn1", sums_after=[dcat_b])
    dy_ssd, dproj_b, d_gnorm = _gate_bwd(y_ssd, proj_b, gnorm, dcat_b, "gate_bwd")
    dproj_b, dkv_b = _attn_bwd(proj_b, 5, kvs[1], dcat_b, dproj_b, "attn_bwd1")
    mem_bwd(dkv_b, 1)
    dxbc, ddt_raw, d_alog, d_dskip, d_dtbias = _ssd_bwd(
        xbc, dt_raw, _tie(bias_row, sums), alog_row, dfull, dy_ssd, states, ssd_expansions, "ssd_bwd")
    dproj_b, d_convw, d_convb = _conv_bwd(proj_b, conv_w, _tie(conv_b, got_ffn1), dxbc, dproj_b, "conv_bwd")
    gb = _mm(dproj_b, a1, m=6 * D_MODEL, n=D_MODEL, k=s, ta=True, out_dtype=_ACT, name="proj_b_dw")
    gb_dt = _mm(ddt_raw, a1, m=HPAD, n=D_MODEL, k=s, ta=True, out_dtype=_ACT, name="proj_b_dw_dt")
    blocks_b = [gb[j * b_cols:(j + 1) * b_cols] for j in range(jd)]
    blocks_b.append(jnp.concatenate([gb[jd * b_cols:dt0], gb_dt[:SSM_HEADS], gb[dt0:(jd + 1) * b_cols - SSM_HEADS]], axis=0))
    blocks_b += [gb[j * b_cols - SSM_HEADS:(j + 1) * b_cols - SSM_HEADS] for j in range(jd + 1, N_DEV)]
    gb_blk = jnp.stack(blocks_b)
    da1 = _mm(dproj_b, WBT, m=s, n=D_MODEL, k=6 * D_MODEL, name="proj_b_dx")
    sums, got_mix1 = reduce_scatter([("kv", 1, g_kv[1]), ("b", 0, gb_blk)], "mix1", sums_after=[da1])
    da1 = _mm(ddt_raw, WBDT, m=s, n=D_MODEL, k=HPAD, add=da1, name="proj_b_dx_dt")
    dh2, dh2_act, d_nmix[1] = _rms_bwd(h2, _tie(nmix[1], sums), da1, dh3, "mix_norm_bwd1")

    dh1, dh1_act = ffn_bwd(dh2, dh2_act, h1, f0, p0, 0, after=got_ffn1, after_last=got_mix1)
    dcat_a = out_bwd(dh1_act, cat_a, 0)
    sums, got_ffn0 = reduce_scatter([("f1", 0, g_f1[0]), ("f2", 0, g_f2[0]), ("out", 0, g_out[0])], "ffn0", sums_after=[dcat_a])
    dproj_a, d_ws, d_bs3, d_lng, d_lnb = _gmlp_bwd(proj_a, dcat_a, _tie(lng, sums), lnb, ws, bs3, "gmlp_bwd")
    dproj_a, dkv_a = _attn_bwd(proj_a, 4, kvs[0], dcat_a, dproj_a, "attn_bwd0")
    mem_bwd(dkv_a, 0)

    def big_update(w, m, v, fam, nlayer):
        res = None
        for i in range(nlayer):
            grad, recv1, recv2 = reduced[fam, i]
            plist = [(recv1, 0), (recv2, 0), (recv2, 1), (recv2, 2)]
            res = _adamw(w, m, v, (grad, lay_g[fam][0]), plist, me1, f"adamw_{fam}{i}", layer=i, prev=res)
        return res

    da0 = _mm(dproj_a, WA, m=s, n=D_MODEL, k=5 * D_MODEL, tb=True, b_at=(0, 0, 0), name="proj_a_dx")
    grad_x, _, d_nmix[0] = _rms_bwd(xs, nmix[0], da0, dh1, "mix_norm_bwd0")
    ga = _mm(a0, dproj_a, m=D_MODEL, n=5 * D_MODEL, k=s, ta=True, out_dtype=_ACT, after=[grad_x], name="proj_a_dw")
    r_b = big_update(tr_b(b_in), tr_b(m_b_in), tr_b(v_b_in), "b", 1)
    reduce_scatter([("kv", 0, g_kv[0]), ("a", 0, ga)], "mix0", after=got_ffn0, sums_after=r_b)
    r_b = [tr_b(o) for o in r_b]

    small_names = ["norm_mix", "norm_ffn", "mem_norm", "a_ln_g", "a_ln_b", "a_ws", "a_bs", "b_dt_bias", "b_a_log", "b_d",
                   "final_norm", "b_conv_w", "b_conv_b", "b_gnorm"]
    small_grads = [jnp.concatenate(d_nmix, axis=0), jnp.concatenate(d_nffn, axis=0), jnp.concatenate(d_nmem, axis=0),
                   d_lng, d_lnb, d_ws.reshape(A_GROUPS * CHUNK, CHUNK), d_bs3.reshape(A_GROUPS, CHUNK),
                   d_dtbias[:, :SSM_HEADS], d_alog[:, :SSM_HEADS], d_dskip[:, :SSM_HEADS], d_fin,
                   d_convw, d_convb, d_gnorm]
    small_2d = [(2, D_MODEL)] * 3 + [(1, D_INNER)] * 2 + [(A_GROUPS * CHUNK, CHUNK), (A_GROUPS, CHUNK)] + [(1, SSM_HEADS)] * 3 \
        + [(1, D_MODEL), (CONV_K, 384), (1, 384), (1, 256)]
    gathered = _all_gather_seq(
        small_grads + [loss_part], [("blk", 0, (N_DEV,) + g.shape) for g in small_grads + [loss_part]], "ag_small_grads")

    r_f1 = big_update(w_ffn1, m_w_ffn1, v_w_ffn1, "f1", 2)
    r_f2 = big_update(w_ffn2, m_w_ffn2, v_w_ffn2, "f2", 2)
    r_out = big_update(w_out, m_w_out, v_w_out, "out", 2)
    r_kv = big_update(w_kv, m_w_kv, v_w_kv, "kv", 2)
    r_a = big_update(a_in, m_a_in, v_a_in, "a", 1)

    small_w = [norm_mix, norm_ffn, mem_norm, a_ln_g, a_ln_b, a_ws, a_bs, b_dt_bias, b_a_log, b_d, final_norm,
               b_conv_w, b_conv_b, b_gnorm]
    small_m = [m_norm_mix, m_norm_ffn, m_mem_norm, m_a_ln_g, m_a_ln_b, m_a_ws, m_a_bs, m_b_dt_bias, m_b_a_log, m_b_d,
               m_final_norm, m_b_conv_w, m_b_conv_b, m_b_gnorm]
    small_v = [v_norm_mix, v_norm_ffn, v_mem_norm, v_a_ln_g, v_a_ln_b, v_a_ws, v_a_bs, v_b_dt_bias, v_b_a_log, v_b_d,
               v_final_norm, v_b_conv_w, v_b_conv_b, v_b_gnorm]
    params = [tuple(a.reshape(shp) for a in wmv) for shp, wmv in zip(small_2d, zip(small_w, small_m, small_v))]
    loss_all = _tie(gathered[-1], [r_a[0], r_kv[0]])
    small_res, loss_sum = _small_update(gathered[:-1], params, loss_all, me1, "adamw_small")
    loss = loss_sum[0, 0]

    names = ["norm_mix", "norm_ffn", "mem_norm", "w_kv", "w_out", "w_ffn1", "w_ffn2", "a_in", "a_ln_g", "a_ln_b", "a_ws",
             "a_bs", "b_in", "b_conv_w", "b_conv_b", "b_dt_bias", "b_a_log", "b_d", "b_gnorm", "final_norm"]
    big = {"w_kv": r_kv, "w_out": r_out, "w_ffn1": r_f1, "w_ffn2": r_f2, "a_in": r_a, "b_in": r_b}
    outs = [loss, grad_x.reshape(x.shape)]
    for kind in range(4):
        for nm in names:
            if nm in big:
                outs.append(big[nm][kind])
            else:
                i = small_names.index(nm)
                outs.append(small_res[i][kind].reshape(small_w[i].shape))
    return tuple(outs)
```

```python
import functools
import math

import jax
import jax.numpy as jnp
from jax import lax
from jax.experimental import pallas as pl
from jax.experimental.pallas import tpu as pltpu
from jax.experimental.pallas import tpu_sc as plsc

F32 = jnp.float32
_MXU = jnp.bfloat16
_ACT = jnp.bfloat16

D_MODEL = 1024
CHUNK = 128
N_MEM = 256
D_INNER = 2048
A_GROUPS = 8
A_GW = D_INNER // A_GROUPS
SSM_HEADS = 32
SSM_P = 64
SSM_GROUPS = 4
SSM_GW = D_INNER // SSM_GROUPS
SSM_N = 128
CONV_K = 4
CONV_DIM = 3072
X_HEADS = 4
X_HD = 256
X_WIDTH = 1024
D_FF = 4096
EPS = 1e-6
HPAD = 128
N_DEV = 8

ADAM_LR = 0.001
ADAM_B1 = 0.9
ADAM_B2 = 0.999
ADAM_EPS = 1e-08
ADAM_WD = 0.01
ADAM_STEP = 10

VMEM_BIG = 56 * 1024 * 1024
MESH = pl.DeviceIdType.MESH


def _cp(vmem=None):
    if vmem is None:
        return pltpu.CompilerParams()
    return pltpu.CompilerParams(vmem_limit_bytes=vmem)


def _dot(a, b, dims=((1,), (0,))):
    return lax.dot_general(a.astype(_MXU), b.astype(_MXU), (dims, ((), ())), preferred_element_type=F32)


def _dot_nt(a, b):
    return _dot(a, b, ((1,), (1,)))


def _dot_tn(a, b):
    return _dot(a, b, ((0,), (0,)))


def _split3(x):
    x1 = x.astype(jnp.bfloat16)
    r = x - x1.astype(F32)
    x2 = r.astype(jnp.bfloat16)
    x3 = (r - x2.astype(F32)).astype(jnp.bfloat16)
    return x1, x2, x3


def _dot_sel(x, sel, dims=((1,), (0,)), terms=2):
    sel = sel.astype(jnp.bfloat16)
    parts = [lax.dot_general(t, sel, (dims, ((), ())), preferred_element_type=F32) for t in _split3(x)[:terms]]
    return functools.reduce(lambda a, b: a + b, parts)


def _sel_dot(sel, x, dims=((1,), (0,))):
    sel = sel.astype(jnp.bfloat16)
    parts = [lax.dot_general(sel, t, (dims, ((), ())), preferred_element_type=F32) for t in _split3(x)]
    return (parts[0] + parts[1]) + parts[2]


def _sigmoid(x):
    return 1.0 / (1.0 + jnp.exp(-x))


def _gelu(x):
    return 0.5 * x * (1.0 + lax.erf(x * (1.0 / math.sqrt(2.0))))


def _gelu_with_grad(x):
    phi = 0.5 * (1.0 + lax.erf(x * (1.0 / math.sqrt(2.0))))
    return x * phi, phi + x * jnp.exp(-0.5 * x * x) * (1.0 / math.sqrt(2.0 * math.pi))


def _softplus(x):
    return jnp.maximum(x, 0.0) + jnp.log1p(jnp.exp(-jnp.abs(x)))


def _iota(shape, dim):
    return lax.broadcasted_iota(jnp.int32, shape, dim)


MM_VMEM_BUDGET = 40 * 1024 * 1024
HBM_BYTES_PER_S = 2.5e12
GRID_STEP_S = 0.35e-6
VMEM_ACC_BYTES_PER_S = 6e12


def _divisors(dim, unit):
    out = [d for d in range(unit, min(dim, 2048) + 1, unit) if dim % d == 0]
    return out if out else [dim]


def _mm_tiles(m, n, k, sa, sb, s_mn, a_pro, offsets):
    best = None
    (a_r0, a_c0, ta), (b_r0, b_c0, tb), (o_r0, o_c0) = offsets
    for tm in _divisors(m, 128):
        for tn in _divisors(n, 128):
            for tk in [k // d for d in (1, 2, 3, 4, 6, 8) if k % d == 0 and (k // d) % 128 == 0]:
                a_t = (tk, tm) if ta else (tm, tk)
                b_t = (tn, tk) if tb else (tk, tn)
                if a_r0 % a_t[0] or a_c0 % a_t[1] or b_r0 % b_t[0] or b_c0 % b_t[1] or o_r0 % tm or o_c0 % tn:
                    continue
                nk = k // tk
                vmem = 2 * (tm * tk * sa + tk * tn * sb + tm * tn * s_mn) + tm * tn * 4 * (2 if nk > 1 else 1)
                if a_pro or sa == 4:
                    vmem += tm * tk * 6
                if sb == 4:
                    vmem += tk * tn * 2
                if vmem > MM_VMEM_BUDGET:
                    continue
                gi, gj = m // tm, n // tn
                for j_inner in (True, False):
                    if nk > 1:
                        traffic = gj * m * k * sa + gi * k * n * sb
                    elif j_inner:
                        traffic = m * k * sa + gi * k * n * sb
                    else:
                        traffic = gj * m * k * sa + k * n * sb
                    traffic += m * n * s_mn + (tm * tk * sa + tk * tn * sb)
                    cost = traffic / HBM_BYTES_PER_S + gi * gj * nk * GRID_STEP_S
                    if nk > 1:
                        cost += m * n * 8 * nk / VMEM_ACC_BYTES_PER_S
                    if best is None or cost < best[0]:
                        best = (cost, tm, tn, tk, j_inner)
    assert best is not None, (m, n, k)
    return best[1:]


def _mm(a, b, *, m, n, k, name, ta=False, tb=False, a_at=(None, 0, 0), b_at=(None, 0, 0),
        out_dtype=F32, add=None, epi_p=None, epi_at=(None, 0, 0), out=None, out_at=(None, 0, 0),
        out_full=None, a_pro=None, after=()):
    s_mn =jnp.dtype(out.dtype if out is not None else out_dtype).itemsize
    s_mn += add.dtype.itemsize if add is not None else 0
    s_mn += epi_p.dtype.itemsize if epi_p is not None else 0
    tm, tn, tk, j_inner = _mm_tiles(m, n, k, a.dtype.itemsize, b.dtype.itemsize, s_mn, a_pro is not None,
                                    ((a_at[1], a_at[2], ta), (b_at[1], b_at[2], tb), (out_at[1], out_at[2])))
    nk = k // tk

    def spec(at, tr, tc, rsel, csel):
        lead, r0, c0 = at
        assert r0 % tr == 0 and c0 % tc == 0, (name, at, tr, tc)
        rb, cb = r0 // tr, c0 // tc
        if lead is None:
            return pl.BlockSpec((tr, tc), lambda g0, g1, kk: (rb + rsel(g0, g1, kk), cb + csel(g0, g1, kk)))
        return pl.BlockSpec((None, tr, tc), lambda g0, g1, kk: (lead, rb + rsel(g0, g1, kk), cb + csel(g0, g1, kk)))

    gi = (lambda g0, g1, kk: g0) if j_inner else (lambda g0, g1, kk: g1)
    gj = (lambda g0, g1, kk: g1) if j_inner else (lambda g0, g1, kk: g0)
    gk = lambda g0, g1, kk: kk
    a_spec = spec(a_at, tk, tm, gk, gi) if ta else spec(a_at, tm, tk, gi, gk)
    b_spec = spec(b_at, tn, tk, gj, gk) if tb else spec(b_at, tk, tn, gk, gj)
    dims = ((0,), (0,)) if ta else (((1,), (1,)) if tb else ((1,), (0,)))
    assert not (ta and tb)

    operands, in_specs = [a, b], [a_spec, b_spec]
    if add is not None:
        operands.append(add)
        in_specs.append(spec((None, 0, 0), tm, tn, gi, gj))
    if epi_p is not None:
        operands.append(epi_p)
        in_specs.append(spec(epi_at, tm, tn, gi, gj))
    aliases = {}
    if out is not None:
        aliases = {len(operands): 0}
        operands.append(out)
        in_specs.append(pl.BlockSpec(memory_space=pl.ANY))
        out_struct = jax.ShapeDtypeStruct(out.shape, out.dtype)
        out_dtype = out.dtype
    else:
        out_struct = jax.ShapeDtypeStruct(out_full if out_full is not None else (m, n), out_dtype)
    has_add, has_epi = add is not None, epi_p is not None
    n_skip = (1 if out is not None else 0) + len(after)
    operands += list(after)
    in_specs += [pl.BlockSpec(memory_space=pl.ANY)] * len(after)

    def body(*refs):
        a_ref, b_ref = refs[0], refs[1]
        pos = 2
        add_ref = epi_ref = None
        if has_add:
            add_ref = refs[pos]
            pos += 1
        if has_epi:
            epi_ref = refs[pos]
            pos += 1
        pos += n_skip
        o_ref = refs[pos]

        def finish(r):
            if has_add:
                r = r + add_ref[...].astype(F32)
            if has_epi:
                r = r * (2.0 * jnp.maximum(epi_ref[...].astype(F32), 0.0))
            o_ref[...] = r.astype(o_ref.dtype)

        av = a_ref[...]
        if a_pro == "relu2":
            av = jnp.square(jnp.maximum(av.astype(F32), 0.0))
        part = _dot(av, b_ref[...], dims)
        if nk == 1:
            finish(part)
        else:
            acc_ref = refs[pos + 1]
            kk = pl.program_id(2)

            @pl.when(kk == 0)
            def _():
                acc_ref[...] = part

            @pl.when(kk > 0)
            def _():
                acc_ref[...] += part

            @pl.when(kk == nk - 1)
            def _():
                finish(acc_ref[...])

    grid = (m // tm, n // tn, nk) if j_inner else (n // tn, m // tm, nk)
    return pl.pallas_call(
        body, name=name, grid=grid, in_specs=in_specs,
        out_specs=spec(out_at, tm, tn, gi, gj), out_shape=out_struct,
        scratch_shapes=[pltpu.VMEM((tm, tn), F32)] if nk > 1 else [], input_output_aliases=aliases,
        compiler_params=_cp(VMEM_BIG))(*operands)


def _rms_fwd(x, g, name, tm=1024):
    s, d = x.shape
    tm = min(tm, s)

    def body(x_ref, g_ref, o_ref):
        xv = x_ref[...]
        r = lax.rsqrt(jnp.mean(xv * xv, axis=-1, keepdims=True) + EPS)
        o_ref[...] = (xv * r * g_ref[...]).astype(o_ref.dtype)

    return pl.pallas_call(
        body, name=name, grid=(s // tm,),
        in_specs=[pl.BlockSpec((tm, d), lambda i: (i, 0)), pl.BlockSpec((1, d), lambda i: (0, 0))],
        out_specs=pl.BlockSpec((tm, d), lambda i: (i, 0)),
        out_shape=jax.ShapeDtypeStruct((s, d), _ACT), compiler_params=_cp(VMEM_BIG))(x, g)


def _rms_bwd(x, g, dy, dres, name, tm=512):
    s, d = x.shape
    tm = min(tm, s)
    has_res = dres is not None

    def body(*refs):
        if has_res:
            x_ref, g_ref, dy_ref, dres_ref, dx_ref, dxa_ref, dg_ref = refs
        else:
            x_ref, g_ref, dy_ref, dx_ref, dxa_ref, dg_ref = refs

        @pl.when(pl.program_id(0) == 0)
        def _():
            dg_ref[...] = jnp.zeros_like(dg_ref)

        xv = x_ref[...]
        dyv = dy_ref[...].astype(F32)
        r = lax.rsqrt(jnp.mean(xv * xv, axis=-1, keepdims=True) + EPS)
        xh = xv * r
        dyg = dyv * g_ref[...]
        dx = r * (dyg - xh * jnp.mean(dyg * xh, axis=-1, keepdims=True))
        if has_res:
            dx = dx + dres_ref[...]
        dx_ref[...] = dx
        dxa_ref[...] = dx.astype(dxa_ref.dtype)
        dg_ref[...] += jnp.sum(dyv * xh, axis=0, keepdims=True)

    row = pl.BlockSpec((tm, d), lambda i: (i, 0))
    vec = pl.BlockSpec((1, d), lambda i: (0, 0))
    in_specs = [row, vec, row] + ([row] if has_res else [])
    operands = [x, g, dy] + ([dres] if has_res else [])
    return pl.pallas_call(
        body, name=name, grid=(s // tm,), in_specs=in_specs, out_specs=[row, row, vec],
        out_shape=[jax.ShapeDtypeStruct((s, d), F32), jax.ShapeDtypeStruct((s, d), _ACT),
                   jax.ShapeDtypeStruct((1, d), F32)], compiler_params=_cp(VMEM_BIG))(*operands)


def _mem_kv_fwd(mem, g, wkv, name):
    def body(mem_ref, g_ref, w_ref, mn_ref, kv_ref):
        xv = mem_ref[...]
        r = lax.rsqrt(jnp.mean(xv * xv, axis=-1, keepdims=True) + EPS)
        mn = (xv * r * g_ref[...]).astype(mn_ref.dtype)
        mn_ref[...] = mn
        kv_ref[...] = _dot(mn, w_ref[0]).astype(kv_ref.dtype)

    vmem = pl.BlockSpec(memory_space=pltpu.VMEM)
    return pl.pallas_call(
        body, name=name, in_specs=[vmem] * 3, out_specs=[vmem] * 2,
        out_shape=[jax.ShapeDtypeStruct(mem.shape, _ACT), jax.ShapeDtypeStruct((mem.shape[0], wkv.shape[2]), _ACT)],
        compiler_params=_cp(VMEM_BIG))(mem, g, wkv)


def _mem_kv_bwd(mem, g, mn, dkv, wkv, name):
    def body(mem_ref, g_ref, mn_ref, dkv_ref, w_ref, dw_ref, dg_ref):
        dkv_v = dkv_ref[...]
        dw_ref[...] = _dot_tn(mn_ref[...], dkv_v).astype(dw_ref.dtype)
        dmn = _dot_nt(dkv_v, w_ref[0])
        xv = mem_ref[...]
        r = lax.rsqrt(jnp.mean(xv * xv, axis=-1, keepdims=True) + EPS)
        dg_ref[...] = jnp.sum(dmn * (xv * r), axis=0, keepdims=True)

    vmem = pl.BlockSpec(memory_space=pltpu.VMEM)
    return pl.pallas_call(
        body, name=name, in_specs=[vmem] * 5, out_specs=[vmem] * 2,
        out_shape=[jax.ShapeDtypeStruct(wkv.shape[1:], _ACT), jax.ShapeDtypeStruct(g.shape, F32)],
        compiler_params=_cp(VMEM_BIG))(mem, g, mn, dkv, wkv)


def _loss_head(h, g, target, name, tm=512):
    s, d = h.shape
    tm = min(tm, s)

    def body(h_ref, g_ref, t_ref, loss_ref, dh_ref, dha_ref, dg_ref):
        @pl.when(pl.program_id(0) == 0)
        def _():
            dg_ref[...] = jnp.zeros_like(dg_ref)
            loss_ref[...] = jnp.zeros_like(loss_ref)

        xv = h_ref[...]
        r = lax.rsqrt(jnp.mean(xv * xv, axis=-1, keepdims=True) + EPS)
        xh = xv * r
        err = xh * g_ref[...] - t_ref[...]
        loss_ref[...] += jnp.full(loss_ref.shape, 0.5 * jnp.sum(jnp.mean(err * err, axis=-1, keepdims=True)), F32)
        dyv = err * (1.0 / d)
        dyg = dyv * g_ref[...]
        dh = r * (dyg - xh * jnp.mean(dyg * xh, axis=-1, keepdims=True))
        dh_ref[...] = dh
        dha_ref[...] = dh.astype(dha_ref.dtype)
        dg_ref[...] += jnp.sum(dyv * xh, axis=0, keepdims=True)

    row = pl.BlockSpec((tm, d), lambda i: (i, 0))
    vec = pl.BlockSpec((1, d), lambda i: (0, 0))
    return pl.pallas_call(
        body, name=name, grid=(s // tm,), in_specs=[row, vec, row],
        out_specs=[pl.BlockSpec((1, 128), lambda i: (0, 0)), row, row, vec],
        out_shape=[jax.ShapeDtypeStruct((1, 128), F32), jax.ShapeDtypeStruct((s, d), F32),
                   jax.ShapeDtypeStruct((s, d), _ACT), jax.ShapeDtypeStruct((1, d), F32)],
        compiler_params=_cp(VMEM_BIG))(h, g, target)


def _gmlp_parts(u, v, lng, lnb):
    mu = jnp.mean(v, axis=-1, keepdims=True)
    vc = v - mu
    rstd = lax.rsqrt(jnp.mean(vc * vc, axis=-1, keepdims=True) + EPS)
    xhat = vc * rstd
    vn = xhat * lng + lnb
    return u, xhat, rstd, vn


def _gmlp_fwd(proj, lng, lnb, ws, bs3, name):
    s = proj.shape[0]

    def body(pu_ref, pv_ref, lng_ref, lnb_ref, ws_ref, bs_ref, o_ref):
        u, _, _, vn = _gmlp_parts(_gelu(pu_ref[...]), _gelu(pv_ref[...]), lng_ref[...], lnb_ref[...])
        causal = _iota((CHUNK, CHUNK), 0) >= _iota((CHUNK, CHUNK), 1)
        for g in range(A_GROUPS):
            sl = slice(g * A_GW, (g + 1) * A_GW)
            w = jnp.where(causal, ws_ref[g], 0.0)
            sv = _dot(w, vn[:, sl]) + bs_ref[g]
            o_ref[:, sl] = (u[:, sl] * sv).astype(o_ref.dtype)

    full = lambda shape: pl.BlockSpec(shape, lambda c: (0,) * len(shape))
    return pl.pallas_call(
        body, name=name, grid=(s // CHUNK,),
        in_specs=[pl.BlockSpec((CHUNK, D_INNER), lambda c: (c, 0)), pl.BlockSpec((CHUNK, D_INNER), lambda c: (c, 1)),
                  full((1, D_INNER)), full((1, D_INNER)), full((A_GROUPS, CHUNK, CHUNK)), full((A_GROUPS, CHUNK, 1))],
        out_specs=pl.BlockSpec((CHUNK, D_INNER), lambda c: (c, 0)),
        out_shape=jax.ShapeDtypeStruct((s, D_INNER + X_WIDTH), _ACT), compiler_params=_cp(VMEM_BIG))(proj, proj, lng, lnb, ws, bs3)


def _gmlp_bwd(proj, dcat, lng, lnb, ws, bs3, name):
    s = proj.shape[0]

    def body(pu_ref, pv_ref, dm_ref, lng_ref, lnb_ref, ws_ref, bs_ref, dp_ref, dws_ref, dbs_ref, dlng_ref, dlnb_ref, dvn_ref):
        @pl.when(pl.program_id(0) == 0)
        def _():
            dws_ref[...] = jnp.zeros_like(dws_ref)
            dbs_ref[...] = jnp.zeros_like(dbs_ref)
            dlng_ref[...] = jnp.zeros_like(dlng_ref)
            dlnb_ref[...] = jnp.zeros_like(dlnb_ref)

        lng = lng_ref[...]
        u, u_grad = _gelu_with_grad(pu_ref[...])
        v, v_grad = _gelu_with_grad(pv_ref[...])
        u, xhat, rstd, vn = _gmlp_parts(u, v, lng, lnb_ref[...])
        dm = dm_ref[...].astype(F32)
        causal = _iota((CHUNK, CHUNK), 0) >= _iota((CHUNK, CHUNK), 1)
        for g in range(A_GROUPS):
            sl = slice(g * A_GW, (g + 1) * A_GW)
            w = jnp.where(causal, ws_ref[g], 0.0)
            sv = _dot(w, vn[:, sl]) + bs_ref[g]
            dsv = dm[:, sl] * u[:, sl]
            dp_ref[:, sl] = (dm[:, sl] * sv * u_grad[:, sl]).astype(dp_ref.dtype)
            dvn_ref[:, sl] = _dot_tn(w, dsv)
            dws_ref[g] += jnp.where(causal, _dot_nt(dsv, vn[:, sl]), 0.0)
            dbs_ref[g] += jnp.sum(dsv, axis=-1, keepdims=True)
        dvn = dvn_ref[...]
        dlng_ref[...] += jnp.sum(dvn * xhat, axis=0, keepdims=True)
        dlnb_ref[...] += jnp.sum(dvn, axis=0, keepdims=True)
        dxh = dvn * lng
        dv = rstd * (dxh - jnp.mean(dxh, axis=-1, keepdims=True) - xhat * jnp.mean(dxh * xhat, axis=-1, keepdims=True))
        dp_ref[:, D_INNER:] = (dv * v_grad).astype(dp_ref.dtype)

    full = lambda shape: pl.BlockSpec(shape, lambda c: (0,) * len(shape))
    return pl.pallas_call(
        body, name=name, grid=(s // CHUNK,),
        in_specs=[pl.BlockSpec((CHUNK, D_INNER), lambda c: (c, 0)), pl.BlockSpec((CHUNK, D_INNER), lambda c: (c, 1)),
                  pl.BlockSpec((CHUNK, D_INNER), lambda c: (c, 0)),
                  full((1, D_INNER)), full((1, D_INNER)), full((A_GROUPS, CHUNK, CHUNK)), full((A_GROUPS, CHUNK, 1))],
        out_specs=[pl.BlockSpec((CHUNK, 2 * D_INNER), lambda c: (c, 0)), full((A_GROUPS, CHUNK, CHUNK)),
                   full((A_GROUPS, CHUNK, 1)), full((1, D_INNER)), full((1, D_INNER))],
        out_shape=[jax.ShapeDtypeStruct((s, 2 * D_INNER + X_WIDTH), _ACT), jax.ShapeDtypeStruct((A_GROUPS, CHUNK, CHUNK), F32),
                   jax.ShapeDtypeStruct((A_GROUPS, CHUNK, 1), F32), jax.ShapeDtypeStruct((1, D_INNER), F32),
                   jax.ShapeDtypeStruct((1, D_INNER), F32)],
        scratch_shapes=[pltpu.VMEM((CHUNK, D_INNER), F32)],
        compiler_params=_cp(VMEM_BIG))(proj, proj, dcat, lng, lnb, ws, bs3)


_X_SCALE = 1.0 / math.sqrt(X_HD)


def _attn_fwd(proj, qblk, kv, cat, name, tm=512):
    s = proj.shape[0]
    tm = min(tm, s)

    def body(q_ref, kv_ref, cat_ref, o_ref):
        for h in range(X_HEADS):
            sl = slice(h * X_HD, (h + 1) * X_HD)
            k = kv_ref[:, sl]
            v = kv_ref[:, X_WIDTH + h * X_HD:X_WIDTH + (h + 1) * X_HD]
            sc = _dot_nt(q_ref[:, sl], k) * _X_SCALE
            e = jnp.exp(sc - jnp.max(sc, axis=-1, keepdims=True))
            p = e / jnp.sum(e, axis=-1, keepdims=True)
            o_ref[:, sl] = _dot(p, v).astype(o_ref.dtype)

    return pl.pallas_call(
        body, name=name, grid=(s // tm,),
        in_specs=[pl.BlockSpec((tm, X_WIDTH), lambda i: (i, qblk)), pl.BlockSpec((N_MEM, 2 * X_WIDTH), lambda i: (0, 0)),
                  pl.BlockSpec(memory_space=pl.ANY)],
        out_specs=pl.BlockSpec((tm, X_WIDTH), lambda i: (i, D_INNER // X_WIDTH)),
        out_shape=jax.ShapeDtypeStruct(cat.shape, cat.dtype), input_output_aliases={2: 0},
        compiler_params=_cp(VMEM_BIG))(proj, kv, cat)


def _attn_bwd(proj, qblk, kv, dcat, dproj, name, tm=512):
    s = proj.shape[0]
    tm = min(tm, s)

    def body(q_ref, kv_ref, do_ref, dproj_ref, dq_ref, dkv_ref):
        @pl.when(pl.program_id(0) == 0)
        def _():
            dkv_ref[...] = jnp.zeros_like(dkv_ref)

        for h in range(X_HEADS):
            sl = slice(h * X_HD, (h + 1) * X_HD)
            slv = slice(X_WIDTH + h * X_HD, X_WIDTH + (h + 1) * X_HD)
            q = q_ref[:, sl]
            k = kv_ref[:, sl]
            v = kv_ref[:, slv]
            do = do_ref[:, sl].astype(F32)
            sc = _dot_nt(q, k) * _X_SCALE
            e = jnp.exp(sc - jnp.max(sc, axis=-1, keepdims=True))
            p = e / jnp.sum(e, axis=-1, keepdims=True)
            dp = _dot_nt(do, v)
            ds = p * (dp - jnp.sum(dp * p, axis=-1, keepdims=True)) * _X_SCALE
            dq_ref[:, sl] = _dot(ds, k).astype(dq_ref.dtype)
            dkv_ref[:, sl] += _dot_tn(ds, q)
            dkv_ref[:, slv] += _dot_tn(p, do)

    return pl.pallas_call(
        body, name=name, grid=(s // tm,),
        in_specs=[pl.BlockSpec((tm, X_WIDTH), lambda i: (i, qblk)), pl.BlockSpec((N_MEM, 2 * X_WIDTH), lambda i: (0, 0)),
                  pl.BlockSpec((tm, X_WIDTH), lambda i: (i, 2)), pl.BlockSpec(memory_space=pl.ANY)],
        out_specs=[pl.BlockSpec((tm, X_WIDTH), lambda i: (i, qblk)), pl.BlockSpec((N_MEM, 2 * X_WIDTH), lambda i: (0, 0))],
        out_shape=[jax.ShapeDtypeStruct(dproj.shape, dproj.dtype), jax.ShapeDtypeStruct((N_MEM, 2 * X_WIDTH), F32)],
        input_output_aliases={3: 0}, compiler_params=_cp(VMEM_BIG))(proj, kv, dcat, dproj)


CONV_TC = 256
_XBC_BLK0 = D_INNER // CONV_TC


CONV_RB = 64
SUBLANES = 8


def _rows_before(cur, prev_last, j):
    rolled = pltpu.roll(cur, j, 0)
    head = jnp.where(_iota((SUBLANES, cur.shape[1]), 0) < j, pltpu.roll(prev_last, j, 0), rolled[:SUBLANES])
    return jnp.concatenate([head, rolled[SUBLANES:]], axis=0)


def _rows_after(cur, next_first, j):
    n = cur.shape[0]
    rolled = pltpu.roll(cur, n - j, 0)
    tail = jnp.where(_iota((SUBLANES, cur.shape[1]), 0) >= SUBLANES - j, pltpu.roll(next_first, SUBLANES - j, 0),
                     rolled[n - SUBLANES:])
    return jnp.concatenate([rolled[:n - SUBLANES], tail], axis=0)


def _conv_pre(x_ref, w_ref, b_ref, r0, prev_last):
    cur = x_ref[pl.ds(r0, CONV_RB), :]
    shifts = [_rows_before(cur, prev_last, j) for j in range(1, CONV_K)]
    pre = b_ref[...] + w_ref[CONV_K - 1:CONV_K, :] * cur
    for j in range(1, CONV_K):
        pre = pre + w_ref[CONV_K - 1 - j:CONV_K - j, :] * shifts[j - 1]
    return pre, cur, shifts


def _conv_fwd(proj, w, b, name):
    s = proj.shape[0]

    def body(x_ref, w_ref, b_ref, o_ref):
        xv = x_ref[...]
        rows = _iota(xv.shape, 0)
        pre = b_ref[...] + w_ref[CONV_K - 1:CONV_K, :] * xv
        for j in range(1, CONV_K):
            pre = pre + w_ref[CONV_K - 1 - j:CONV_K - j, :] * jnp.where(rows >= j, pltpu.roll(xv, j, 0), 0.0)
        o_ref[...] = pre * _sigmoid(pre)

    return pl.pallas_call(
        body, name=name, grid=(CONV_DIM // CONV_TC,),
        in_specs=[pl.BlockSpec((s, CONV_TC), lambda j: (0, _XBC_BLK0 + j)), pl.BlockSpec((CONV_K, CONV_TC), lambda j: (0, j)),
                  pl.BlockSpec((1, CONV_TC), lambda j: (0, j))],
        out_specs=pl.BlockSpec((s, CONV_TC), lambda j: (0, j)),
        out_shape=jax.ShapeDtypeStruct((s, CONV_DIM), F32), compiler_params=_cp(VMEM_BIG))(proj, w, b)


def _conv_bwd(proj, w, b, dxbc, dproj, name):
    s = proj.shape[0]

    nb = s // CONV_RB

    def body(x_ref, w_ref, b_ref, d_ref, dproj_ref, dx_ref, dw_ref, db_ref, dpre_ref):
        def fold(v):
            out = v[:SUBLANES]
            for t in range(1, CONV_RB // SUBLANES):
                out = out + v[t * SUBLANES:(t + 1) * SUBLANES]
            return out

        def first(i, carry):
            prev_last, acc = carry
            r0 = pl.multiple_of(i * CONV_RB, CONV_RB)
            pre, cur, shifts = _conv_pre(x_ref, w_ref, b_ref, r0, prev_last)
            sig = _sigmoid(pre)
            dpre = d_ref[pl.ds(r0, CONV_RB), :] * (sig * (1.0 + pre * (1.0 - sig)))
            dpre_ref[pl.ds(r0, CONV_RB), :] = dpre
            taps = [cur] + shifts
            acc = tuple(a + fold(dpre * t) for a, t in zip(acc[:CONV_K], taps)) + (acc[CONV_K] + fold(dpre),)
            return cur[CONV_RB - SUBLANES:], acc

        zero8 = jnp.zeros((SUBLANES, CONV_TC), F32)
        _, acc = lax.fori_loop(0, nb, first, (zero8, (zero8,) * (CONV_K + 1)))
        for j in range(CONV_K):
            dw_ref[CONV_K - 1 - j:CONV_K - j, :] = jnp.sum(acc[j], axis=0, keepdims=True)
        db_ref[...] = jnp.sum(acc[CONV_K], axis=0, keepdims=True)

        def second(i, next_first):
            r0 = pl.multiple_of((nb - 1 - i) * CONV_RB, CONV_RB)
            cur = dpre_ref[pl.ds(r0, CONV_RB), :]
            dx = w_ref[CONV_K - 1:CONV_K, :] * cur
            for j in range(1, CONV_K):
                dx = dx + w_ref[CONV_K - 1 - j:CONV_K - j, :] * _rows_after(cur, next_first, j)
            dx_ref[pl.ds(r0, CONV_RB), :] = dx.astype(dx_ref.dtype)
            return cur[:SUBLANES]

        lax.fori_loop(0, nb, second, zero8)

    return pl.pallas_call(
        body, name=name, grid=(CONV_DIM // CONV_TC,),
        in_specs=[pl.BlockSpec((s, CONV_TC), lambda j: (0, _XBC_BLK0 + j)), pl.BlockSpec((CONV_K, CONV_TC), lambda j: (0, j)),
                  pl.BlockSpec((1, CONV_TC), lambda j: (0, j)), pl.BlockSpec((s, CONV_TC), lambda j: (0, j)),
                  pl.BlockSpec(memory_space=pl.ANY)],
        out_specs=[pl.BlockSpec((s, CONV_TC), lambda j: (0, _XBC_BLK0 + j)), pl.BlockSpec((CONV_K, CONV_TC), lambda j: (0, j)),
                   pl.BlockSpec((1, CONV_TC), lambda j: (0, j))],
        out_shape=[jax.ShapeDtypeStruct(dproj.shape, dproj.dtype), jax.ShapeDtypeStruct((CONV_K, CONV_DIM), F32),
                   jax.ShapeDtypeStruct((1, CONV_DIM), F32)], input_output_aliases={4: 0},
        scratch_shapes=[pltpu.VMEM((s, CONV_TC), F32)],
        compiler_params=_cp(VMEM_BIG))(proj, w, b, dxbc, dproj)


def _ssd_common(dtc_ref, br_ref, ar_ref, csb_ref, cst_ref, csf_ref, dtf_ref, expand):
    a_row = -jnp.exp(ar_ref[...])
    dt_c = _softplus(dtc_ref[...] + br_ref[...])
    tril = _iota((CHUNK, CHUNK), 0) >= _iota((CHUNK, CHUNK), 1)
    cs = _sel_dot(tril, dt_c * a_row)
    cst_ref[...] = cs.T
    e64 = (jnp.right_shift(_iota((HPAD, D_INNER), 1), 6) == _iota((HPAD, D_INNER), 0)).astype(jnp.bfloat16)
    if expand:
        e128 = jnp.right_shift(_iota((HPAD, SSM_HEADS * CHUNK), 1), 7) == _iota((HPAD, SSM_HEADS * CHUNK), 0)
        csb_ref[...] = _dot_sel(cs, e128)
        dtf_ref[...] = _dot_sel(dt_c, e64)
        csf_ref[...] = _dot_sel(cs, e64)
    dt_full = dtf_ref[...]
    cs_full = csf_ref[...]
    cs_last = csf_ref[CHUNK - 1:CHUNK, :]
    e_full = jnp.exp(cs_full)
    f_full = jnp.exp(cs_last - cs_full)
    gamma = jnp.exp(cs_last)
    return a_row, dt_c, cs, dt_full, e_full, f_full, gamma, e64


def _ssd_lambda(csb_ref, cst_ref, h, causal):
    diff = csb_ref[:, h * CHUNK:(h + 1) * CHUNK] - cst_ref[h:h + 1, :]
    return jnp.exp(jnp.where(causal, diff, -1e30))


_SSD_VEC_SPECS = lambda: [pl.BlockSpec((1, HPAD), lambda c: (0, 0)), pl.BlockSpec((1, HPAD), lambda c: (0, 0)),
                          pl.BlockSpec((1, D_INNER), lambda c: (0, 0))]


def _ssd_fwd(xbc, dtc, bias_row, alog_row, dfull, name):
    s = xbc.shape[0]
    nc = s // CHUNK

    def body(xbc_ref, dtc_ref, br_ref, ar_ref, df_ref, y_ref, st_ref, csb_ref, csf_ref, dtf_ref, ht_ref, cst_ref):
        @pl.when(pl.program_id(0) == 0)
        def _():
            ht_ref[...] = jnp.zeros_like(ht_ref)

        _, _, _, dt_full, e_full, f_full, gamma, _ = _ssd_common(
            dtc_ref, br_ref, ar_ref, csb_ref, cst_ref, csf_ref, dtf_ref, expand=True)
        x = xbc_ref[:, :D_INNER]
        xdt = x * dt_full
        st_ref[...] = ht_ref[...]
        causal = _iota((CHUNK, CHUNK), 0) >= _iota((CHUNK, CHUNK), 1)
        lo = _iota((CHUNK, CHUNK), 1) < SSM_P
        for g in range(SSM_GROUPS):
            gs = slice(g * SSM_GW, (g + 1) * SSM_GW)
            bg = xbc_ref[:, D_INNER + g * SSM_N:D_INNER + (g + 1) * SSM_N]
            cg = xbc_ref[:, D_INNER + SSM_GROUPS * SSM_N + g * SSM_N:D_INNER + SSM_GROUPS * SSM_N + (g + 1) * SSM_N]
            ht = ht_ref[:, gs]
            cb = _dot_nt(cg, bg)
            yoff = e_full[:, gs] * _dot(cg, ht)
            for jp in range(SSM_GW // CHUNK):
                j = g * (SSM_GW // CHUNK) + jp
                ps = slice(j * CHUNK, (j + 1) * CHUNK)
                x2 = xdt[:, ps]
                y0 = _dot(cb * _ssd_lambda(csb_ref, cst_ref, 2 * j, causal), x2)
                y1 = _dot(cb * _ssd_lambda(csb_ref, cst_ref, 2 * j + 1, causal), x2)
                y_ref[:, ps] = (jnp.where(lo, y0, y1) + yoff[:, jp * CHUNK:(jp + 1) * CHUNK]
                                + x[:, ps] * df_ref[:, ps])
            ht_ref[:, gs] = gamma[:, gs] * ht + _dot_tn(bg, xdt[:, gs] * f_full[:, gs])

    return pl.pallas_call(
        body, name=name, grid=(nc,),
        in_specs=[pl.BlockSpec((CHUNK, CONV_DIM), lambda c: (c, 0)), pl.BlockSpec((CHUNK, HPAD), lambda c: (c, 0))]
                 + _SSD_VEC_SPECS(),
        out_specs=[pl.BlockSpec((CHUNK, D_INNER), lambda c: (c, 0)), pl.BlockSpec((None, SSM_N, D_INNER), lambda c: (c, 0, 0)),
                   pl.BlockSpec((CHUNK, SSM_HEADS * CHUNK), lambda c: (c, 0)), pl.BlockSpec((CHUNK, D_INNER), lambda c: (c, 0)),
                   pl.BlockSpec((CHUNK, D_INNER), lambda c: (c, 0))],
        out_shape=[jax.ShapeDtypeStruct((s, D_INNER), F32), jax.ShapeDtypeStruct((nc, SSM_N, D_INNER), F32),
                   jax.ShapeDtypeStruct((s, SSM_HEADS * CHUNK), F32), jax.ShapeDtypeStruct((s, D_INNER), F32),
                   jax.ShapeDtypeStruct((s, D_INNER), F32)],
        scratch_shapes=[pltpu.VMEM((SSM_N, D_INNER), F32), pltpu.VMEM((HPAD, CHUNK), F32)],
        compiler_params=_cp(VMEM_BIG))(xbc, dtc, bias_row, alog_row, dfull)


def _ssd_bwd(xbc, dtc, bias_row, alog_row, dfull, dy, states, expansions, name):
    s = xbc.shape[0]
    nc = s // CHUNK
    rev = lambda c: nc - 1 - c

    def body(xbc_ref, dtc_ref, br_ref, ar_ref, df_ref, dy_ref, st_ref, csb_ref, csf_ref, dtf_ref,
             dxbc_ref, ddt_ref, dalog_ref, dd_ref, dbias_ref,
             dht_ref, cst_ref, ddf_ref, dxs_ref, dcsf_ref, dcsl_ref):
        step = pl.program_id(0)

        @pl.when(step == 0)
        def _():
            dht_ref[...] = jnp.zeros_like(dht_ref)
            ddf_ref[...] = jnp.zeros_like(ddf_ref)
            dalog_ref[...] = jnp.zeros_like(dalog_ref)
            dbias_ref[...] = jnp.zeros_like(dbias_ref)
            dd_ref[...] = jnp.zeros_like(dd_ref)

        a_row, dt_c, _, dt_full, e_full, f_full, gamma, e64 = _ssd_common(
            dtc_ref, br_ref, ar_ref, csb_ref, cst_ref, csf_ref, dtf_ref, expand=False)
        x = xbc_ref[:, :D_INNER]
        xdt = x * dt_full
        dy_all = dy_ref[...]
        ddf_ref[...] += jnp.broadcast_to(jnp.sum(dy_all * x, axis=0, keepdims=True), ddf_ref.shape)
        causal = _iota((CHUNK, CHUNK), 0) >= _iota((CHUNK, CHUNK), 1)
        lo = _iota((CHUNK, CHUNK), 1) < SSM_P
        head_lane = _iota((CHUNK, HPAD), 1)
        head_row = _iota((HPAD, CHUNK), 0)
        dcs_heads = jnp.zeros((CHUNK, HPAD), F32)
        dcs_cols = jnp.zeros((HPAD, CHUNK), F32)
        for g in range(SSM_GROUPS):
            gs = slice(g * SSM_GW, (g + 1) * SSM_GW)
            b0 = D_INNER + g * SSM_N
            c0 = D_INNER + SSM_GROUPS * SSM_N + g * SSM_N
            bg = xbc_ref[:, b0:b0 + SSM_N]
            cg = xbc_ref[:, c0:c0 + SSM_N]
            ht = st_ref[:, gs]
            dht = dht_ref[:, gs]
            dyg = dy_all[:, gs]
            eg, fg, gg = e_full[:, gs], f_full[:, gs], gamma[:, gs]
            z = _dot(cg, ht)
            dz = dyg * eg
            dcg = _dot_nt(dz, ht)
            dht_new = _dot_tn(cg, dz) + gg * dht
            xf = xdt[:, gs] * fg
            dxf = _dot(bg, dht)
            dbg = _dot_nt(xf, dht)
            dff = dxf * xf
            dcsf_ref[:, gs] = dyg * eg * z - dff
            dcsl_ref[:, gs] = jnp.broadcast_to(
                jnp.sum(dff, axis=0, keepdims=True) + jnp.sum(dht * ht, axis=0, keepdims=True) * gg, (8, SSM_GW))
            cb = _dot_nt(cg, bg)
            dcb = jnp.zeros((CHUNK, CHUNK), F32)
            for jp in range(SSM_GW // CHUNK):
                j = g * (SSM_GW // CHUNK) + jp
                ps = slice(j * CHUNK, (j + 1) * CHUNK)
                x2 = xdt[:, ps]
                dy2 = dy_all[:, ps]
                dxh = []
                for hh in range(2):
                    h = 2 * j + hh
                    lam = _ssd_lambda(csb_ref, cst_ref, h, causal)
                    mh = cb * lam
                    dyh = jnp.where(lo, dy2, 0.0) if hh == 0 else jnp.where(lo, 0.0, dy2)
                    dm = _dot_nt(dyh, x2)
                    dcb = dcb + dm * lam
                    gm = dm * mh
                    dcs_heads = dcs_heads + jnp.where(head_lane == h, jnp.sum(gm, axis=1, keepdims=True), 0.0)
                    dcs_cols = dcs_cols + jnp.where(head_row == h, jnp.sum(gm, axis=0, keepdims=True), 0.0)
                    dxh.append(_dot_tn(mh, dy2))
                dxs_ref[:, ps] = jnp.where(lo, dxh[0], dxh[1]) + dxf[:, jp * CHUNK:(jp + 1) * CHUNK] * fg[:, jp * CHUNK:(jp + 1) * CHUNK]
            dxbc_ref[:, b0:b0 + SSM_N] = (dbg + _dot_tn(dcb, cg)).astype(dxbc_ref.dtype)
            dxbc_ref[:, c0:c0 + SSM_N] = (dcg + _dot(dcb, bg)).astype(dxbc_ref.dtype)
            dht_ref[:, gs] = dht_new
        dxs = dxs_ref[...]
        dcs_heads = dcs_heads - dcs_cols.T + _dot_sel(dcsf_ref[...], e64, ((1,), (1,)))
        dcs_last = _dot_sel(dcsl_ref[...], e64, ((1,), (1,)))
        dcs_heads = dcs_heads + jnp.where(_iota((CHUNK, HPAD), 0) == CHUNK - 1, dcs_last[0:1, :], 0.0)
        triu = _iota((CHUNK, CHUNK), 0) <= _iota((CHUNK, CHUNK), 1)
        dda = _sel_dot(triu, dcs_heads)
        ddt = dda * a_row + _dot_sel(dxs * x, e64, ((1,), (1,)))
        dxbc_ref[:, :D_INNER] = (dxs * dt_full + dy_all * df_ref[...]).astype(dxbc_ref.dtype)
        dalog_ref[...] += jnp.sum(dda * dt_c, axis=0, keepdims=True) * a_row
        ddt_raw = ddt * _sigmoid(dtc_ref[...] + br_ref[...])
        ddt_ref[...] = ddt_raw.astype(ddt_ref.dtype)
        dbias_ref[...] += jnp.sum(ddt_raw, axis=0, keepdims=True)

        @pl.when(step == nc - 1)
        def _():
            dd_ref[...] = _dot_sel(ddf_ref[...], e64, ((1,), (1,)))[0:1, :]

    vec = pl.BlockSpec((1, HPAD), lambda c: (0, 0))
    return pl.pallas_call(
        body, name=name, grid=(nc,),
        in_specs=[pl.BlockSpec((CHUNK, CONV_DIM), lambda c: (rev(c), 0)), pl.BlockSpec((CHUNK, HPAD), lambda c: (rev(c), 0))]
                 + _SSD_VEC_SPECS()
                 + [pl.BlockSpec((CHUNK, D_INNER), lambda c: (rev(c), 0)),
                    pl.BlockSpec((None, SSM_N, D_INNER), lambda c: (rev(c), 0, 0)),
                    pl.BlockSpec((CHUNK, SSM_HEADS * CHUNK), lambda c: (rev(c), 0)),
                    pl.BlockSpec((CHUNK, D_INNER), lambda c: (rev(c), 0)), pl.BlockSpec((CHUNK, D_INNER), lambda c: (rev(c), 0))],
        out_specs=[pl.BlockSpec((CHUNK, CONV_DIM), lambda c: (rev(c), 0)), pl.BlockSpec((CHUNK, HPAD), lambda c: (rev(c), 0)),
                   vec, vec, vec],
        out_shape=[jax.ShapeDtypeStruct((s, CONV_DIM), F32), jax.ShapeDtypeStruct((s, HPAD), _ACT),
                   jax.ShapeDtypeStruct((1, HPAD), F32), jax.ShapeDtypeStruct((1, HPAD), F32),
                   jax.ShapeDtypeStruct((1, HPAD), F32)],
        scratch_shapes=[pltpu.VMEM((SSM_N, D_INNER), F32), pltpu.VMEM((HPAD, CHUNK), F32),
                        pltpu.VMEM((8, D_INNER), F32), pltpu.VMEM((CHUNK, D_INNER), F32),
                        pltpu.VMEM((CHUNK, D_INNER), F32), pltpu.VMEM((8, D_INNER), F32)],
        compiler_params=_cp(VMEM_BIG))(xbc, dtc, bias_row, alog_row, dfull, dy, states, *expansions)


def _gate_fwd(y, proj, gn, name, tm=512):
    s = y.shape[0]
    tm = min(tm, s)

    def body(y_ref, z_ref, gn_ref, o_ref):
        for g in range(SSM_GROUPS):
            gs = slice(g * SSM_GW, (g + 1) * SSM_GW)
            z = z_ref[:, gs]
            t = y_ref[:, gs] * (z * _sigmoid(z))
            r = lax.rsqrt(jnp.mean(t * t, axis=-1, keepdims=True) + EPS)
            o_ref[:, gs] = (t * r * gn_ref[:, gs]).astype(o_ref.dtype)

    row = pl.BlockSpec((tm, D_INNER), lambda i: (i, 0))
    return pl.pallas_call(
        body, name=name, grid=(s // tm,), in_specs=[row, row, pl.BlockSpec((1, D_INNER), lambda i: (0, 0))],
        out_specs=row, out_shape=jax.ShapeDtypeStruct((s, D_INNER + X_WIDTH), _ACT),
        compiler_params=_cp(VMEM_BIG))(y, proj, gn)


def _gate_bwd(y, proj, gn, dcat, name, tm=512):
    s = y.shape[0]
    tm = min(tm, s)

    def body(y_ref, z_ref, gn_ref, dm_ref, dy_ref, dz_ref, dgn_ref):
        @pl.when(pl.program_id(0) == 0)
        def _():
            dgn_ref[...] = jnp.zeros_like(dgn_ref)

        for g in range(SSM_GROUPS):
            gs = slice(g * SSM_GW, (g + 1) * SSM_GW)
            z = z_ref[:, gs]
            yv = y_ref[:, gs]
            sig = _sigmoid(z)
            sz = z * sig
            t = yv * sz
            r = lax.rsqrt(jnp.mean(t * t, axis=-1, keepdims=True) + EPS)
            th = t * r
            dm = dm_ref[:, gs].astype(F32)
            dmg = dm * gn_ref[:, gs]
            dt_ = r * (dmg - th * jnp.mean(dmg * th, axis=-1, keepdims=True))
            dgn_ref[:, gs] += jnp.sum(dm * th, axis=0, keepdims=True)
            dy_ref[:, gs] = dt_ * sz
            dz_ref[:, gs] = (dt_ * yv * (sig * (1.0 + z * (1.0 - sig)))).astype(dz_ref.dtype)

    row = pl.BlockSpec((tm, D_INNER), lambda i: (i, 0))
    vec = pl.BlockSpec((1, D_INNER), lambda i: (0, 0))
    return pl.pallas_call(
        body, name=name, grid=(s // tm,), in_specs=[row, row, vec, row], out_specs=[row, row, vec],
        out_shape=[jax.ShapeDtypeStruct((s, D_INNER), F32), jax.ShapeDtypeStruct((s, 6 * D_MODEL), _ACT),
                   jax.ShapeDtypeStruct((1, D_INNER), F32)], compiler_params=_cp(VMEM_BIG))(y, proj, gn, dcat)


def _block_of(kind, width):
    if kind == "col":
        return lambda ref, j: ref.at[:, :, pl.ds(pl.multiple_of(j * width, 128), width)]
    if kind == "row":
        return lambda ref, j: ref.at[:, pl.ds(pl.multiple_of(j * width, 8), width), :]
    return lambda ref, j: ref.at[j]


def _coords():
    return lax.axis_index("x"), lax.axis_index("y"), lax.axis_index("c")


def _rel_chip(x, y, k):
    return (1 - x if k & 1 else x), (1 - y if k & 2 else y)


def _all_gather_body(ins, outs, send_sems, recv_sems, local_sems, blocks):
    n = len(ins)
    x, y, c = _coords()
    sibling = (x, y, 1 - c)
    via = (x + (1 - c) * (1 - 2 * x), y + c * (1 - 2 * y))
    onto = (x + c * (1 - 2 * x), y + (1 - c) * (1 - 2 * y))

    def copy(t, k, chip, core, to, src=None):
        dst = blocks[t](outs[t], 4 * chip[0] + 2 * chip[1] + core)
        return pltpu.make_async_remote_copy(
            src_ref=dst if src is None else src, dst_ref=dst, send_sem=send_sems.at[t, k],
            recv_sem=recv_sems.at[t, k], device_id=to, device_id_type=MESH)

    started = []
    for t in range(n):
        mine = pltpu.make_async_copy(ins[t], blocks[t](outs[t], 4 * x + 2 * y + c), local_sems.at[t])
        mine.start()
        started.append(mine)
    sends = []
    for t in range(n):
        for k in range(3):
            px, py = _rel_chip(x, y, k)
            cp = copy(t, k, (x, y), c, (px, py, 1 - c if k == 0 else c), src=ins[t])
            cp.start()
            sends.append(cp)
    for t in range(n):
        for k in (1, 2):
            chip = _rel_chip(x, y, k)
            copy(t, k, chip, c, sibling).wait_recv()
            fwd = copy(t, 3 + k, chip, c, sibling)
            fwd.start()
            sends.append(fwd)
        hop = copy(t, 3, via, c, (*onto, c))
        hop.start()
        sends.append(hop)
    for t in range(n):
        diagonal = _rel_chip(x, y, 3)
        copy(t, 3, diagonal, c, sibling).wait_recv()
        fwd = copy(t, 6, diagonal, c, sibling)
        fwd.start()
        sends.append(fwd)
    for t in range(n):
        copy(t, 0, (x, y), 1 - c, sibling).wait_recv()
        for k in range(1, 4):
            copy(t, 3 + k, _rel_chip(x, y, k), 1 - c, sibling).wait_recv()
    for cp in sends:
        cp.wait_send()
    for mine in started:
        mine.wait()


def _handshake(peers):
    barrier = pltpu.get_barrier_semaphore()
    for peer in peers:
        pl.semaphore_signal(barrier, inc=1, device_id=peer, device_id_type=MESH)
    pl.semaphore_wait(barrier, len(peers))


def _gather_peers():
    x, y, c = _coords()
    return [(x, y, 1 - c)] + [(*_rel_chip(x, y, k), c) for k in (1, 2)]


SEQ_ID_GATHER, SEQ_ID_SIBLING, SEQ_ID_CHIPS = 1, 2, 3


def _sequencer_call(body, peers, operands, out_types, sems, name, collective_id, after=()):
    n_in, n_out, n_after = len(operands), len(out_types), len(after)

    def launch(*refs):
        _handshake(peers())
        body(refs[:n_in], refs[n_in + n_after:n_in + n_after + n_out], *refs[n_in + n_after + n_out:])

    return pl.kernel(
        launch, name=name, out_type=out_types, mesh=plsc.ScalarSubcoreMesh(axis_name="seq", num_cores=1),
        scratch_types=sems, compiler_params=pltpu.CompilerParams(collective_id=collective_id))(*operands, *after)


def _all_gather_seq(shards, layouts, name, after=()):
    n = len(shards)
    blocks = [_block_of(kind, width) for kind, width, _ in layouts]
    return _sequencer_call(
        lambda ins, outs, *sems: _all_gather_body(ins, outs, *sems, blocks), _gather_peers, shards,
        [jax.ShapeDtypeStruct(shape, sh.dtype) for sh, (_, _, shape) in zip(shards, layouts)],
        [pltpu.SemaphoreType.DMA((n, 7)), pltpu.SemaphoreType.DMA((n, 7)), pltpu.SemaphoreType.DMA((n,))],
        name, SEQ_ID_GATHER, after)


def _tie(small, after):
    return lax.optimization_barrier((small, *after))[0]


def _rs_to_sibling(grads, layouts, name, after=()):
    n = len(grads)
    blocks = [_block_of(kind, width) for kind, width, _ in layouts]

    def body(ins, outs, send_sems, recv_sems):
        x, y, c = _coords()
        sibling = (x, y, 1 - c)
        cps = []
        for t in range(n):
            for k in range(4):
                px, py = _rel_chip(x, y, k)
                cp = pltpu.make_async_remote_copy(
                    src_ref=blocks[t](ins[t], 4 * px + 2 * py + (1 - c)), dst_ref=outs[t].at[k],
                    send_sem=send_sems.at[t, k], recv_sem=recv_sems.at[t, k], device_id=sibling, device_id_type=MESH)
                cp.start()
                cps.append(cp)
        for cp in cps:
            cp.wait_recv()
        for cp in cps:
            cp.wait_send()

    def sibling_only():
        x, y, c = _coords()
        return [(x, y, 1 - c)]

    return _sequencer_call(
        body, sibling_only, grads,
        [jax.ShapeDtypeStruct((4,) + shape, g.dtype) for g, (_, _, shape) in zip(grads, layouts)],
        [pltpu.SemaphoreType.DMA((n, 4)), pltpu.SemaphoreType.DMA((n, 4))], name, SEQ_ID_SIBLING, after)


def _rs_chip_sum(grad, recv, layout, xyc, name):
    kind, width, shape = layout
    r, ccols = shape

    def src_index(step, xyc_ref):
        k = step + 1
        px = jnp.where(k % 2 == 1, 1 - xyc_ref[0], xyc_ref[0])
        py = jnp.where(k // 2 == 1, 1 - xyc_ref[1], xyc_ref[1])
        return 4 * px + 2 * py + xyc_ref[2]

    if kind == "col":
        g_spec = pl.BlockSpec((r, ccols), lambda k, s_: (0, src_index(k, s_)))
    elif kind == "row":
        g_spec = pl.BlockSpec((r, ccols), lambda k, s_: (src_index(k, s_), 0))
    else:
        g_spec = pl.BlockSpec((None, r, ccols), lambda k, s_: (src_index(k, s_), 0, 0))

    def body(xyc_ref, g_ref, r_ref, o_ref):
        o_ref[...] = (g_ref[...].astype(F32) + r_ref[...].astype(F32)).astype(o_ref.dtype)

    slot = pl.BlockSpec((None, r, ccols), lambda k, s_: (k + 1, 0, 0))
    return pl.pallas_call(
        body, name=name,
        grid_spec=pltpu.PrefetchScalarGridSpec(num_scalar_prefetch=1, grid=(3,), in_specs=[g_spec, slot], out_specs=slot),
        out_shape=jax.ShapeDtypeStruct((4, r, ccols), grad.dtype), compiler_params=_cp(VMEM_BIG))(xyc, grad, recv)


def _rs_across_chips(parts, name):
    n = len(parts)

    def body(ins, outs, send_sems, recv_sems):
        x, y, c = _coords()
        cps = []
        for t in range(n):
            for k in range(1, 4):
                px, py = _rel_chip(x, y, k)
                cp = pltpu.make_async_remote_copy(
                    src_ref=ins[t].at[k], dst_ref=outs[t].at[k - 1], send_sem=send_sems.at[t, k - 1],
                    recv_sem=recv_sems.at[t, k - 1], device_id=(px, py, c), device_id_type=MESH)
                cp.start()
                cps.append(cp)
        for cp in cps:
            cp.wait_recv()
        for cp in cps:
            cp.wait_send()

    def other_chips():
        x, y, c = _coords()
        return [(*_rel_chip(x, y, k), c) for k in range(1, 4)]

    return _sequencer_call(
        body, other_chips, parts, [jax.ShapeDtypeStruct((3,) + p.shape[1:], p.dtype) for p in parts],
        [pltpu.SemaphoreType.DMA((n, 3)), pltpu.SemaphoreType.DMA((n, 3))], name, SEQ_ID_CHIPS)


def _adamw_math(w, g, m, v):
    m = ADAM_B1 * m + (1.0 - ADAM_B1) * g
    v = ADAM_B2 * v + (1.0 - ADAM_B2) * jnp.square(g)
    m_hat = m / (1.0 - ADAM_B1 ** ADAM_STEP)
    v_hat = v / (1.0 - ADAM_B2 ** ADAM_STEP)
    delta = -ADAM_LR * (m_hat / (jnp.sqrt(v_hat) + ADAM_EPS) + ADAM_WD * w)
    return delta, m, v


def _row_tile(rows, cap):
    best = None
    for cand in range(8, min(rows, cap) + 1, 8):
        if rows % cand == 0:
            best = cand
    assert best is not None, rows
    return best


def _adamw(w, m, v, own, parts, me, name, layer, prev=None, tr=256):
    r, ccols = w.shape[-2:]
    npart = len(parts)
    if r % 8 == 0:
        tr, tc = _row_tile(r, tr), ccols
        steps, at = r // tr, (lambda i: (i, 0))
    else:
        tr, tc = r, 256
        assert ccols % tc == 0
        steps, at = ccols // tc, (lambda i: (0, i))

    def spec(lead):
        return pl.BlockSpec((None, tr, tc), lambda i, me_ref: (lead,) + at(i))

    grad, kind = own
    if kind == "col":
        own_spec = pl.BlockSpec((tr, tc), lambda i, me_ref: (at(i)[0], me_ref[0]))
    elif kind == "row":
        own_spec = pl.BlockSpec((tr, tc), lambda i, me_ref: (me_ref[0] * (r // tr) + at(i)[0], 0))
    else:
        own_spec = pl.BlockSpec((None, tr, tc), lambda i, me_ref: (me_ref[0],) + at(i))

    def body(me_ref, *refs):
        w_ref, m_ref, v_ref = refs[:3]
        p_refs = refs[3:4 + npart]
        outs = refs[len(refs) - 4:]
        g = p_refs[0][...].astype(F32)
        for p_ref in p_refs[1:]:
            g = g + p_ref[...].astype(F32)
        delta, mn, vn = _adamw_math(w_ref[...], g, m_ref[...], v_ref[...])
        outs[0][...] = g
        outs[1][...] = delta
        outs[2][...] = mn
        outs[3][...] = vn

    operands = [w, m, v, grad] + [p for p, _ in parts]
    in_specs = [spec(layer)] * 3 + [own_spec] + [spec(lead) for _, lead in parts]
    aliases = {}
    if prev is not None:
        for i, p in enumerate(prev):
            aliases[1 + len(operands)] = i
            operands.append(p)
            in_specs.append(pl.BlockSpec(memory_space=pl.ANY))
    return pl.pallas_call(
        body, name=name,
        grid_spec=pltpu.PrefetchScalarGridSpec(num_scalar_prefetch=1, grid=(steps,), in_specs=in_specs,
                                               out_specs=[spec(layer)] * 4),
        out_shape=[jax.ShapeDtypeStruct(w.shape, F32)] * 4, input_output_aliases=aliases,
        compiler_params=_cp(VMEM_BIG))(me, *operands)


def _small_update(gathered, params, loss_all, me, name):
    n = len(gathered)
    shapes = [w.shape for w, _, _ in params]

    def body(me_ref, *refs):
        g_refs, loss_ref = refs[:n], refs[n]
        p_refs = refs[n + 1:n + 1 + 3 * n]
        o_refs = refs[n + 1 + 3 * n:]
        for i in range(n):
            r, c = shapes[i]
            if gathered[i].shape[2] == c:
                parts = [g_refs[i][j] for j in range(N_DEV)]
            else:
                off = pl.multiple_of(me_ref[0] * c, 128)
                parts = [g_refs[i][j, :, pl.ds(off, c)] for j in range(N_DEV)]
            g = functools.reduce(lambda a, b: a + b, parts)
            delta, mn, vn = _adamw_math(p_refs[3 * i][...], g, p_refs[3 * i + 1][...], p_refs[3 * i + 2][...])
            for k, val in enumerate((g, delta, mn, vn)):
                o_refs[4 * i + k][...] = val
        o_refs[4 * n][...] = functools.reduce(lambda a, b: a + b, [loss_ref[j] for j in range(N_DEV)])

    vmem = pl.BlockSpec(memory_space=pltpu.VMEM)
    flat_params = [a for p in params for a in p]
    outs = pl.pallas_call(
        body, name=name, in_specs=[pl.BlockSpec(memory_space=pltpu.SMEM)] + [vmem] * (n + 1 + 3 * n),
        out_specs=[vmem] * (4 * n + 1),
        out_shape=[jax.ShapeDtypeStruct(shp, F32) for shp in shapes for _ in range(4)] + [jax.ShapeDtypeStruct((1, 128), F32)],
        compiler_params=_cp(VMEM_BIG))(me, *gathered, loss_all, *flat_params)
    return [tuple(outs[4 * i:4 * i + 4]) for i in range(n)], outs[4 * n]


def _pack(arrays):
    pieces, layout, off = [], [], 0
    for a in arrays:
        n = a.size
        padded = -(-n // 1024) * 1024
        flat = a.reshape(-1).astype(F32)
        if padded != n:
            flat = jnp.pad(flat, (0, padded - n))
        pieces.append(flat.reshape(padded // 128, 128))
        layout.append((off, n, a.shape))
        off += padded // 128
    return jnp.concatenate(pieces, axis=0), layout


def kernel(x, mem, norm_mix, norm_ffn, mem_norm, w_kv, w_out, w_ffn1, w_ffn2, a_in, a_ln_g, a_ln_b, a_ws, a_bs, b_in, b_conv_w, b_conv_b, b_dt_bias, b_a_log, b_d, b_gnorm, final_norm, loss_target, m_norm_mix, m_norm_ffn, m_mem_norm, m_w_kv, m_w_out, m_w_ffn1, m_w_ffn2, m_a_in, m_a_ln_g, m_a_ln_b, m_a_ws, m_a_bs, m_b_in, m_b_conv_w, m_b_conv_b, m_b_dt_bias, m_b_a_log, m_b_d, m_b_gnorm, m_final_norm, v_norm_mix, v_norm_ffn, v_mem_norm, v_w_kv, v_w_out, v_w_ffn1, v_w_ffn2, v_a_in, v_a_ln_g, v_a_ln_b, v_a_ws, v_a_bs, v_b_in, v_b_conv_w, v_b_conv_b, v_b_dt_bias, v_b_a_log, v_b_d, v_b_gnorm, v_final_norm):
    s = x.shape[1]
    xs = x.reshape(s, D_MODEL)
    mems = mem.reshape(N_MEM, D_MODEL)
    target = loss_target.reshape(s, D_MODEL)
    ax, ay, ac = lax.axis_index("x"), lax.axis_index("y"), lax.axis_index("c")
    me = 4 * ax + 2 * ay + ac
    xyc = jnp.stack([ax, ay, ac]).astype(jnp.int32)
    me1 = me.astype(jnp.int32).reshape(1)

    b_cols = b_in.shape[2]
    act = lambda a: a.astype(_ACT)
    lay_f1, lay_f2 = ("col", 512, (1, D_MODEL, D_FF)), ("row", 512, (1, D_FF, D_MODEL))
    lay_out, lay_kv = ("row", 384, (1, 3 * D_MODEL, D_MODEL)), ("col", 256, (1, D_MODEL, 2 * X_WIDTH))
    small_w_pack = _pack([b_conv_w[0], b_conv_b[0], b_gnorm[0]])[0]
    (WA,) = _all_gather_seq([act(a_in)], [("col", 640, (1, D_MODEL, 5 * D_MODEL))], "ag_proj_a")
    wo0, wkv0 = _all_gather_seq([act(w_out[0:1]), act(w_kv[0:1])], [lay_out, lay_kv], "ag_out0")
    (w1_0,) = _all_gather_seq([act(w_ffn1[0:1])], [lay_f1], "ag_ffn0_up")
    (w2_0,) = _all_gather_seq([act(w_ffn2[0:1])], [lay_f2], "ag_ffn0_down")
    a0 = _rms_fwd(xs, norm_mix[0].reshape(1, -1), "mix_norm0")
    tr_b = lambda a: jnp.swapaxes(a, 1, 2)
    wbt_blk, small_w = _all_gather_seq(
        [act(tr_b(b_in)[0]), small_w_pack],
        [("blk", 0, (N_DEV, b_cols, D_MODEL)), ("blk", 0, (N_DEV, 32, 128))], "ag_proj_b", after=[a0])
    wo1, wkv1 = _all_gather_seq([act(w_out[1:2]), act(w_kv[1:2])], [lay_out, lay_kv], "ag_out1", after=[a0])
    w1_1, w2_1 = _all_gather_seq([act(w_ffn1[1:2]), act(w_ffn2[1:2])], [lay_f1, lay_f2], "ag_ffn1", after=[a0])
    W1, W2, WO, WKV = [w1_0, w1_1], [w2_0, w2_1], [wo0, wo1], [wkv0, wkv1]
    dt0 = D_INNER + CONV_DIM

    row = lambda a: a.reshape(1, -1)
    nmix = [row(norm_mix[0]), row(norm_mix[1])]
    nffn = [row(norm_ffn[0]), row(norm_ffn[1])]
    nmem = [row(mem_norm[0]), row(mem_norm[1])]
    fin = row(final_norm)
    lng, lnb = a_ln_g.reshape(1, D_INNER), a_ln_b.reshape(1, D_INNER)
    ws = a_ws[0]
    bs3 = a_bs[0].reshape(A_GROUPS, CHUNK, 1)
    pad_h = lambda a: jnp.pad(a.reshape(-1), (0, HPAD - SSM_HEADS))
    bias_row = pad_h(b_dt_bias).reshape(1, HPAD)
    alog_row = pad_h(b_a_log).reshape(1, HPAD)
    dfull = jnp.repeat(b_d.reshape(-1), SSM_P).reshape(1, D_INNER)

    kvs, mns = [None, None], [None, None]

    def mem_kv(i, after=None):
        gain = nmem[i] if after is None else _tie(nmem[i], after)
        mns[i], kvs[i] = _mem_kv_fwd(mems, gain, WKV[i], f"mem_kv{i}")

    def ffn_up(h, i):
        f = _rms_fwd(h, nffn[i], f"ffn_norm{i}")
        return f, _mm(f, W1[i], m=s, n=D_FF, k=D_MODEL, b_at=(0, 0, 0), out_dtype=_ACT, name=f"ffn_up{i}")

    def ffn_down(h, p, i, after=()):
        return _mm(p, W2[i], m=s, n=D_MODEL, k=D_FF, b_at=(0, 0, 0), a_pro="relu2", add=h, after=after, name=f"ffn_down{i}")

    def out_proj(h, cat, i):
        return _mm(cat, WO[i], m=s, n=D_MODEL, k=3 * D_MODEL, b_at=(0, 0, 0), add=h, name=f"out_proj{i}")

    proj_a = _mm(a0, WA, m=s, n=5 * D_MODEL, k=D_MODEL, b_at=(0, 0, 0), name="proj_a")
    cat_a = _gmlp_fwd(proj_a, lng, lnb, ws, bs3, "gmlp_fwd")
    mem_kv(0, after=[cat_a])
    cat_a = _attn_fwd(proj_a, 4, kvs[0], cat_a, "attn_fwd0")
    h1 = out_proj(xs, cat_a, 0)

    f0, p0 = ffn_up(h1, 0)
    wbt_blk, small_w, _ = lax.optimization_barrier((wbt_blk, small_w, p0))
    jd, lo = divmod(dt0, b_cols)
    assert lo + SSM_HEADS <= b_cols
    wbt_full = wbt_blk.reshape(N_DEV * b_cols, D_MODEL)
    WBT = jnp.concatenate([wbt_full[:dt0], wbt_full[dt0 + SSM_HEADS:]], axis=0)
    WBDT = jnp.pad(wbt_full[dt0:dt0 + SSM_HEADS], ((0, HPAD - SSM_HEADS), (0, 0)))
    cw_sh, cb_sh, gn_sh = 4 * 384, 384, 256
    sw = small_w.reshape(N_DEV, 32 * 128)
    conv_w = jnp.transpose(sw[:, :cw_sh].reshape(N_DEV, CONV_K, 384), (1, 0, 2)).reshape(CONV_K, CONV_DIM)
    conv_b = sw[:, 2048:2048 + cb_sh].reshape(1, CONV_DIM)
    gnorm = sw[:, 3072:3072 + gn_sh].reshape(1, D_INNER)

    h2 = ffn_down(h1, p0, 0, after=[WBT, WBDT])
    a1 = _rms_fwd(h2, nmix[1], "mix_norm1")
    proj_b = _mm(a1, WBT, m=s, n=6 * D_MODEL, k=D_MODEL, tb=True, name="proj_b")
    dt_raw = _mm(a1, WBDT, m=s, n=HPAD, k=D_MODEL, tb=True, name="proj_dt")
    xbc = _conv_fwd(proj_b, conv_w, conv_b, "conv_fwd")
    y_ssd, states, *ssd_expansions = _ssd_fwd(xbc, dt_raw, bias_row, alog_row, dfull, "ssd_fwd")
    cat_b = _gate_fwd(y_ssd, proj_b, gnorm, "gate_fwd")
    mem_kv(1, after=[cat_b])
    cat_b = _attn_fwd(proj_b, 5, kvs[1], cat_b, "attn_fwd1")
    h3 = out_proj(h2, cat_b, 1)
    f1, p1 = ffn_up(h3, 1)
    h4 = ffn_down(h3, p1, 1)

    loss_part, dh, dh_act, d_fin = _loss_head(h4, fin, target, "loss_head")

    g_f1, g_f2, g_out, g_kv = [None, None], [None, None], [None, None], [None, None]
    d_nffn, d_nmix, d_nmem = [None, None], [None, None], [None, None]

    def ffn_bwd(dh, dh_act, h_in, f, p, i, after=(), after_last=()):
        dp = _mm(dh_act, W2[i], m=s, n=D_FF, k=D_MODEL, tb=True, b_at=(0, 0, 0), epi_p=p, out_dtype=_ACT, name=f"ffn_down_dx{i}")
        g_f2[i] = _mm(p, dh_act, m=D_FF, n=D_MODEL, k=s, ta=True, a_pro="relu2", out_dtype=_ACT, name=f"ffn_down_dw{i}")
        g_f1[i] = _mm(f, dp, m=D_MODEL, n=D_FF, k=s, ta=True, out_dtype=_ACT, name=f"ffn_up_dw{i}")
        df = _mm(dp, W1[i], m=s, n=D_MODEL, k=D_FF, tb=True, b_at=(0, 0, 0), after=after, name=f"ffn_up_dx{i}")
        gain = _tie(nffn[i], after_last) if after_last else nffn[i]
        dh_in, dh_in_act, d_nffn[i] = _rms_bwd(h_in, gain, df, dh, f"ffn_norm_bwd{i}")
        return dh_in, dh_in_act

    def out_bwd(dh_act, cat, i):
        dcat = _mm(dh_act, WO[i], m=s, n=3 * D_MODEL, k=D_MODEL, tb=True, b_at=(0, 0, 0), out_dtype=_ACT, name=f"out_dx{i}")
        g_out[i] = _mm(cat, dh_act, m=3 * D_MODEL, n=D_MODEL, k=s, ta=True, out_dtype=_ACT, name=f"out_dw{i}")
        return dcat

    def mem_bwd(dkv, i):
        g_kv[i], d_nmem[i] = _mem_kv_bwd(mems, nmem[i], mns[i], dkv, WKV[i], f"mem_kv_bwd{i}")

    lay_g = {"f1": ("col", 512, (D_MODEL, 512)), "f2": ("row", 512, (512, D_MODEL)), "out": ("row", 384, (384, D_MODEL)),
             "kv": ("col", 256, (D_MODEL, 256)), "a": ("col", 640, (D_MODEL, 640)), "b": ("blk", 0, (b_cols, D_MODEL))}
    reduced = {}

    def reduce_scatter(group, tag, after=(), sums_after=()):
        grads3, lays3 = [], []
        for fam, _, g in group:
            kind, width, shape = lay_g[fam]
            grads3.append(g if kind == "blk" else g.reshape((1,) + g.shape))
            lays3.append((kind, width, shape if kind == "blk" else (1,) + shape))
        recv1 = _rs_to_sibling(grads3, lays3, f"rs_sibling_{tag}", after)
        if sums_after:
            recv1 = lax.optimization_barrier((tuple(recv1), tuple(sums_after)))[0]
        recv1 = [recv1[t].reshape((4,) + lay_g[fam][2]) for t, (fam, _, _) in enumerate(group)]
        parts = [_rs_chip_sum(g, r1, lay_g[fam], xyc, f"rs_chip_sum_{fam}{i}") for r1, (fam, i, g) in zip(recv1, group)]
        recv2 = _rs_across_chips(parts, f"rs_chips_{tag}")
        for (fam, i, g), r1, r2 in zip(group, recv1, recv2):
            reduced[fam, i] = (g, r1, r2)
        return parts, recv2

    dh3, dh3_act = ffn_bwd(dh, dh_act, h3, f1, p1, 1)
    dcat_b = out_bwd(dh3_act, cat_b, 1)
    sums, got_ffn1 = reduce_scatter([("f1", 1, g_f1[1]), ("f2", 1, g_f2[1]), ("out", 1, g_out[1])], "ffn1", sums_after=[dcat_b])
    dy_ssd, dproj_b, d_gnorm = _gate_bwd(y_ssd, proj_b, gnorm, dcat_b, "gate_bwd")
    dproj_b, dkv_b = _attn_bwd(proj_b, 5, kvs[1], dcat_b, dproj_b, "attn_bwd1")
    mem_bwd(dkv_b, 1)
    dxbc, ddt_raw, d_alog, d_dskip, d_dtbias = _ssd_bwd(
        xbc, dt_raw, _tie(bias_row, sums), alog_row, dfull, dy_ssd, states, ssd_expansions, "ssd_bwd")
    dproj_b, d_convw, d_convb = _conv_bwd(proj_b, conv_w, _tie(conv_b, got_ffn1), dxbc, dproj_b, "conv_bwd")
    gb = _mm(dproj_b, a1, m=6 * D_MODEL, n=D_MODEL, k=s, ta=True, out_dtype=_ACT, name="proj_b_dw")
    gb_dt = _mm(ddt_raw, a1, m=HPAD, n=D_MODEL, k=s, ta=True, out_dtype=_ACT, name="proj_b_dw_dt")
    blocks_b = [gb[j * b_cols:(j + 1) * b_cols] for j in range(jd)]
    blocks_b.append(jnp.concatenate([gb[jd * b_cols:dt0], gb_dt[:SSM_HEADS], gb[dt0:(jd + 1) * b_cols - SSM_HEADS]], axis=0))
    blocks_b += [gb[j * b_cols - SSM_HEADS:(j + 1) * b_cols - SSM_HEADS] for j in range(jd + 1, N_DEV)]
    gb_blk = jnp.stack(blocks_b)
    da1 = _mm(dproj_b, WBT, m=s, n=D_MODEL, k=6 * D_MODEL, name="proj_b_dx")
    sums, got_mix1 = reduce_scatter([("kv", 1, g_kv[1]), ("b", 0, gb_blk)], "mix1", sums_after=[da1])
    da1 = _mm(ddt_raw, WBDT, m=s, n=D_MODEL, k=HPAD, add=da1, name="proj_b_dx_dt")
    dh2, dh2_act, d_nmix[1] = _rms_bwd(h2, _tie(nmix[1], sums), da1, dh3, "mix_norm_bwd1")

    dh1, dh1_act = ffn_bwd(dh2, dh2_act, h1, f0, p0, 0, after=got_ffn1, after_last=got_mix1)
    dcat_a = out_bwd(dh1_act, cat_a, 0)
    sums, got_ffn0 = reduce_scatter([("f1", 0, g_f1[0]), ("f2", 0, g_f2[0]), ("out", 0, g_out[0])], "ffn0", sums_after=[dcat_a])
    dproj_a, d_ws, d_bs3, d_lng, d_lnb = _gmlp_bwd(proj_a, dcat_a, _tie(lng, sums), lnb, ws, bs3, "gmlp_bwd")
    dproj_a, dkv_a = _attn_bwd(proj_a, 4, kvs[0], dcat_a, dproj_a, "attn_bwd0")
    mem_bwd(dkv_a, 0)

    def big_update(w, m, v, fam, nlayer):
        res = None
        for i in range(nlayer):
            grad, recv1, recv2 = reduced[fam, i]
            plist = [(recv1, 0), (recv2, 0), (recv2, 1), (recv2, 2)]
            res = _adamw(w, m, v, (grad, lay_g[fam][0]), plist, me1, f"adamw_{fam}{i}", layer=i, prev=res)
        return res

    da0 = _mm(dproj_a, WA, m=s, n=D_MODEL, k=5 * D_MODEL, tb=True, b_at=(0, 0, 0), name="proj_a_dx")
    grad_x, _, d_nmix[0] = _rms_bwd(xs, nmix[0], da0, dh1, "mix_norm_bwd0")
    ga = _mm(a0, dproj_a, m=D_MODEL, n=5 * D_MODEL, k=s, ta=True, out_dtype=_ACT, after=[grad_x], name="proj_a_dw")
    r_b = big_update(tr_b(b_in), tr_b(m_b_in), tr_b(v_b_in), "b", 1)
    reduce_scatter([("kv", 0, g_kv[0]), ("a", 0, ga)], "mix0", after=got_ffn0, sums_after=r_b)
    r_b = [tr_b(o) for o in r_b]

    small_names = ["norm_mix", "norm_ffn", "mem_norm", "a_ln_g", "a_ln_b", "a_ws", "a_bs", "b_dt_bias", "b_a_log", "b_d",
                   "final_norm", "b_conv_w", "b_conv_b", "b_gnorm"]
    small_grads = [jnp.concatenate(d_nmix, axis=0), jnp.concatenate(d_nffn, axis=0), jnp.concatenate(d_nmem, axis=0),
                   d_lng, d_lnb, d_ws.reshape(A_GROUPS * CHUNK, CHUNK), d_bs3.reshape(A_GROUPS, CHUNK),
                   d_dtbias[:, :SSM_HEADS], d_alog[:, :SSM_HEADS], d_dskip[:, :SSM_HEADS], d_fin,
                   d_convw, d_convb, d_gnorm]
    small_2d = [(2, D_MODEL)] * 3 + [(1, D_INNER)] * 2 + [(A_GROUPS * CHUNK, CHUNK), (A_GROUPS, CHUNK)] + [(1, SSM_HEADS)] * 3 \
        + [(1, D_MODEL), (CONV_K, 384), (1, 384), (1, 256)]
    gathered = _all_gather_seq(
        small_grads + [loss_part], [("blk", 0, (N_DEV,) + g.shape) for g in small_grads + [loss_part]], "ag_small_grads")

    r_f1 = big_update(w_ffn1, m_w_ffn1, v_w_ffn1, "f1", 2)
    r_f2 = big_update(w_ffn2, m_w_ffn2, v_w_ffn2, "f2", 2)
    r_out = big_update(w_out, m_w_out, v_w_out, "out", 2)
    r_kv = big_update(w_kv, m_w_kv, v_w_kv, "kv", 2)
    r_a = big_update(a_in, m_a_in, v_a_in, "a", 1)

    small_w = [norm_mix, norm_ffn, mem_norm, a_ln_g, a_ln_b, a_ws, a_bs, b_dt_bias, b_a_log, b_d, final_norm,
               b_conv_w, b_conv_b, b_gnorm]
    small_m = [m_norm_mix, m_norm_ffn, m_mem_norm, m_a_ln_g, m_a_ln_b, m_a_ws, m_a_bs, m_b_dt_bias, m_b_a_log, m_b_d,
               m_final_norm, m_b_conv_w, m_b_conv_b, m_b_gnorm]
    small_v = [v_norm_mix, v_norm_ffn, v_mem_norm, v_a_ln_g, v_a_ln_b, v_a_ws, v_a_bs, v_b_dt_bias, v_b_a_log, v_b_d,
               v_final_norm, v_b_conv_w, v_b_conv_b, v_b_gnorm]
    params = [tuple(a.reshape(shp) for a in wmv) for shp, wmv in zip(small_2d, zip(small_w, small_m, small_v))]
    loss_all = _tie(gathered[-1], [r_a[0], r_kv[0]])
    small_res, loss_sum = _small_update(gathered[:-1], params, loss_all, me1, "adamw_small")
    loss = loss_sum[0, 0]

    names = ["norm_mix", "norm_ffn", "mem_norm", "w_kv", "w_out", "w_ffn1", "w_ffn2", "a_in", "a_ln_g", "a_ln_b", "a_ws",
             "a_bs", "b_in", "b_conv_w", "b_conv_b", "b_dt_bias", "b_a_log", "b_d", "b_gnorm", "final_norm"]
    big = {"w_kv": r_kv, "w_out": r_out, "w_ffn1": r_f1, "w_ffn2": r_f2, "a_in": r_a, "b_in": r_b}
    outs = [loss, grad_x.reshape(x.shape)]
    for kind in range(4):
        for nm in names:
            if nm in big:
                outs.append(big[nm][kind])
            else:
                i = small_names.index(nm)
                outs.append(small_res[i][kind].reshape(small_w[i].shape))
    return tuple(outs)
```

```python
import functools
import math

import jax
import jax.numpy as jnp
from jax import lax
from jax.experimental import pallas as pl
from jax.experimental.pallas import tpu as pltpu
from jax.experimental.pallas import tpu_sc as plsc

F32 = jnp.float32
_MXU = jnp.bfloat16
_ACT = jnp.bfloat16

D_MODEL = 1024
CHUNK = 128
N_MEM = 256
D_INNER = 2048
A_GROUPS = 8
A_GW = D_INNER // A_GROUPS
SSM_HEADS = 32
SSM_P = 64
SSM_GROUPS = 4
SSM_GW = D_INNER // SSM_GROUPS
SSM_N = 128
CONV_K = 4
CONV_DIM = 3072
X_HEADS = 4
X_HD = 256
X_WIDTH = 1024
D_FF = 4096
EPS = 1e-6
HPAD = 128
N_DEV = 8

ADAM_LR = 0.001
ADAM_B1 = 0.9
ADAM_B2 = 0.999
ADAM_EPS = 1e-08
ADAM_WD = 0.01
ADAM_STEP = 10

VMEM_BIG = 56 * 1024 * 1024
MESH = pl.DeviceIdType.MESH


def _cp(vmem=None):
    if vmem is None:
        return pltpu.CompilerParams()
    return pltpu.CompilerParams(vmem_limit_bytes=vmem)


def _dot(a, b, dims=((1,), (0,))):
    return lax.dot_general(a.astype(_MXU), b.astype(_MXU), (dims, ((), ())), preferred_element_type=F32)


def _dot_nt(a, b):
    return _dot(a, b, ((1,), (1,)))


def _dot_tn(a, b):
    return _dot(a, b, ((0,), (0,)))


def _split3(x):
    x1 = x.astype(jnp.bfloat16)
    r = x - x1.astype(F32)
    x2 = r.astype(jnp.bfloat16)
    x3 = (r - x2.astype(F32)).astype(jnp.bfloat16)
    return x1, x2, x3


def _dot_sel(x, sel, dims=((1,), (0,)), terms=2):
    sel = sel.astype(jnp.bfloat16)
    parts = [lax.dot_general(t, sel, (dims, ((), ())), preferred_element_type=F32) for t in _split3(x)[:terms]]
    return functools.reduce(lambda a, b: a + b, parts)


def _sel_dot(sel, x, dims=((1,), (0,))):
    sel = sel.astype(jnp.bfloat16)
    parts = [lax.dot_general(sel, t, (dims, ((), ())), preferred_element_type=F32) for t in _split3(x)]
    return (parts[0] + parts[1]) + parts[2]


def _sigmoid(x):
    return 1.0 / (1.0 + jnp.exp(-x))


def _gelu(x):
    return 0.5 * x * (1.0 + lax.erf(x * (1.0 / math.sqrt(2.0))))


def _gelu_with_grad(x):
    phi = 0.5 * (1.0 + lax.erf(x * (1.0 / math.sqrt(2.0))))
    return x * phi, phi + x * jnp.exp(-0.5 * x * x) * (1.0 / math.sqrt(2.0 * math.pi))


def _softplus(x):
    return jnp.maximum(x, 0.0) + jnp.log1p(jnp.exp(-jnp.abs(x)))


def _iota(shape, dim):
    return lax.broadcasted_iota(jnp.int32, shape, dim)


MM_VMEM_BUDGET = 40 * 1024 * 1024
HBM_BYTES_PER_S = 2.5e12
GRID_STEP_S = 0.35e-6
VMEM_ACC_BYTES_PER_S = 6e12


def _divisors(dim, unit):
    out = [d for d in range(unit, min(dim, 2048) + 1, unit) if dim % d == 0]
    return out if out else [dim]


def _mm_tiles(m, n, k, sa, sb, s_mn, a_pro, offsets):
    best = None
    (a_r0, a_c0, ta), (b_r0, b_c0, tb), (o_r0, o_c0) = offsets
    for tm in _divisors(m, 128):
        for tn in _divisors(n, 128):
            for tk in [k // d for d in (1, 2, 3, 4, 6, 8) if k % d == 0 and (k // d) % 128 == 0]:
                a_t = (tk, tm) if ta else (tm, tk)
                b_t = (tn, tk) if tb else (tk, tn)
                if a_r0 % a_t[0] or a_c0 % a_t[1] or b_r0 % b_t[0] or b_c0 % b_t[1] or o_r0 % tm or o_c0 % tn:
                    continue
                nk = k // tk
                vmem = 2 * (tm * tk * sa + tk * tn * sb + tm * tn * s_mn) + tm * tn * 4 * (2 if nk > 1 else 1)
                if a_pro or sa == 4:
                    vmem += tm * tk * 6
                if sb == 4:
                    vmem += tk * tn * 2
                if vmem > MM_VMEM_BUDGET:
                    continue
                gi, gj = m // tm, n // tn
                for j_inner in (True, False):
                    if nk > 1:
                        traffic = gj * m * k * sa + gi * k * n * sb
                    elif j_inner:
                        traffic = m * k * sa + gi * k * n * sb
                    else:
                        traffic = gj * m * k * sa + k * n * sb
                    traffic += m * n * s_mn + (tm * tk * sa + tk * tn * sb)
                    cost = traffic / HBM_BYTES_PER_S + gi * gj * nk * GRID_STEP_S
                    if nk > 1:
                        cost += m * n * 8 * nk / VMEM_ACC_BYTES_PER_S
                    if best is None or cost < best[0]:
                        best = (cost, tm, tn, tk, j_inner)
    assert best is not None, (m, n, k)
    return best[1:]


def _mm(a, b, *, m, n, k, name, ta=False, tb=False, a_at=(None, 0, 0), b_at=(None, 0, 0),
        out_dtype=F32, add=None, epi_p=None, epi_at=(None, 0, 0), out=None, out_at=(None, 0, 0),
        out_full=None, a_pro=None, after=()):
    s_mn =jnp.dtype(out.dtype if out is not None else out_dtype).itemsize
    s_mn += add.dtype.itemsize if add is not None else 0
    s_mn += epi_p.dtype.itemsize if epi_p is not None else 0
    tm, tn, tk, j_inner = _mm_tiles(m, n, k, a.dtype.itemsize, b.dtype.itemsize, s_mn, a_pro is not None,
                                    ((a_at[1], a_at[2], ta), (b_at[1], b_at[2], tb), (out_at[1], out_at[2])))
    nk = k // tk

    def spec(at, tr, tc, rsel, csel):
        lead, r0, c0 = at
        assert r0 % tr == 0 and c0 % tc == 0, (name, at, tr, tc)
        rb, cb = r0 // tr, c0 // tc
        if lead is None:
            return pl.BlockSpec((tr, tc), lambda g0, g1, kk: (rb + rsel(g0, g1, kk), cb + csel(g0, g1, kk)))
        return pl.BlockSpec((None, tr, tc), lambda g0, g1, kk: (lead, rb + rsel(g0, g1, kk), cb + csel(g0, g1, kk)))

    gi = (lambda g0, g1, kk: g0) if j_inner else (lambda g0, g1, kk: g1)
    gj = (lambda g0, g1, kk: g1) if j_inner else (lambda g0, g1, kk: g0)
    gk = lambda g0, g1, kk: kk
    a_spec = spec(a_at, tk, tm, gk, gi) if ta else spec(a_at, tm, tk, gi, gk)
    b_spec = spec(b_at, tn, tk, gj, gk) if tb else spec(b_at, tk, tn, gk, gj)
    dims = ((0,), (0,)) if ta else (((1,), (1,)) if tb else ((1,), (0,)))
    assert not (ta and tb)

    operands, in_specs = [a, b], [a_spec, b_spec]
    if add is not None:
        operands.append(add)
        in_specs.append(spec((None, 0, 0), tm, tn, gi, gj))
    if epi_p is not None:
        operands.append(epi_p)
        in_specs.append(spec(epi_at, tm, tn, gi, gj))
    aliases = {}
    if out is not None:
        aliases = {len(operands): 0}
        operands.append(out)
        in_specs.append(pl.BlockSpec(memory_space=pl.ANY))
        out_struct = jax.ShapeDtypeStruct(out.shape, out.dtype)
        out_dtype = out.dtype
    else:
        out_struct = jax.ShapeDtypeStruct(out_full if out_full is not None else (m, n), out_dtype)
    has_add, has_epi = add is not None, epi_p is not None
    n_skip = (1 if out is not None else 0) + len(after)
    operands += list(after)
    in_specs += [pl.BlockSpec(memory_space=pl.ANY)] * len(after)

    def body(*refs):
        a_ref, b_ref = refs[0], refs[1]
        pos = 2
        add_ref = epi_ref = None
        if has_add:
            add_ref = refs[pos]
            pos += 1
        if has_epi:
            epi_ref = refs[pos]
            pos += 1
        pos += n_skip
        o_ref = refs[pos]

        def finish(r):
            if has_add:
                r = r + add_ref[...].astype(F32)
            if has_epi:
                r = r * (2.0 * jnp.maximum(epi_ref[...].astype(F32), 0.0))
            o_ref[...] = r.astype(o_ref.dtype)

        av = a_ref[...]
        if a_pro == "relu2":
            av = jnp.square(jnp.maximum(av.astype(F32), 0.0))
        part = _dot(av, b_ref[...], dims)
        if nk == 1:
            finish(part)
        else:
            acc_ref = refs[pos + 1]
            kk = pl.program_id(2)

            @pl.when(kk == 0)
            def _():
                acc_ref[...] = part

            @pl.when(kk > 0)
            def _():
                acc_ref[...] += part

            @pl.when(kk == nk - 1)
            def _():
                finish(acc_ref[...])

    grid = (m // tm, n // tn, nk) if j_inner else (n // tn, m // tm, nk)
    return pl.pallas_call(
        body, name=name, grid=grid, in_specs=in_specs,
        out_specs=spec(out_at, tm, tn, gi, gj), out_shape=out_struct,
        scratch_shapes=[pltpu.VMEM((tm, tn), F32)] if nk > 1 else [], input_output_aliases=aliases,
        compiler_params=_cp(VMEM_BIG))(*operands)


def _rms_fwd(x, g, name, tm=1024):
    s, d = x.shape
    tm = min(tm, s)

    def body(x_ref, g_ref, o_ref):
        xv = x_ref[...]
        r = lax.rsqrt(jnp.mean(xv * xv, axis=-1, keepdims=True) + EPS)
        o_ref[...] = (xv * r * g_ref[...]).astype(o_ref.dtype)

    return pl.pallas_call(
        body, name=name, grid=(s // tm,),
        in_specs=[pl.BlockSpec((tm, d), lambda i: (i, 0)), pl.BlockSpec((1, d), lambda i: (0, 0))],
        out_specs=pl.BlockSpec((tm, d), lambda i: (i, 0)),
        out_shape=jax.ShapeDtypeStruct((s, d), _ACT), compiler_params=_cp(VMEM_BIG))(x, g)


def _rms_bwd(x, g, dy, dres, name, tm=512):
    s, d = x.shape
    tm = min(tm, s)
    has_res = dres is not None

    def body(*refs):
        if has_res:
            x_ref, g_ref, dy_ref, dres_ref, dx_ref, dxa_ref, dg_ref = refs
        else:
            x_ref, g_ref, dy_ref, dx_ref, dxa_ref, dg_ref = refs

        @pl.when(pl.program_id(0) == 0)
        def _():
            dg_ref[...] = jnp.zeros_like(dg_ref)

        xv = x_ref[...]
        dyv = dy_ref[...].astype(F32)
        r = lax.rsqrt(jnp.mean(xv * xv, axis=-1, keepdims=True) + EPS)
        xh = xv * r
        dyg = dyv * g_ref[...]
        dx = r * (dyg - xh * jnp.mean(dyg * xh, axis=-1, keepdims=True))
        if has_res:
            dx = dx + dres_ref[...]
        dx_ref[...] = dx
        dxa_ref[...] = dx.astype(dxa_ref.dtype)
        dg_ref[...] += jnp.sum(dyv * xh, axis=0, keepdims=True)

    row = pl.BlockSpec((tm, d), lambda i: (i, 0))
    vec = pl.BlockSpec((1, d), lambda i: (0, 0))
    in_specs = [row, vec, row] + ([row] if has_res else [])
    operands = [x, g, dy] + ([dres] if has_res else [])
    return pl.pallas_call(
        body, name=name, grid=(s // tm,), in_specs=in_specs, out_specs=[row, row, vec],
        out_shape=[jax.ShapeDtypeStruct((s, d), F32), jax.ShapeDtypeStruct((s, d), _ACT),
                   jax.ShapeDtypeStruct((1, d), F32)], compiler_params=_cp(VMEM_BIG))(*operands)


def _loss_head(h, g, target, name, tm=512):
    s, d = h.shape
    tm = min(tm, s)

    def body(h_ref, g_ref, t_ref, loss_ref, dh_ref, dha_ref, dg_ref):
        @pl.when(pl.program_id(0) == 0)
        def _():
            dg_ref[...] = jnp.zeros_like(dg_ref)
            loss_ref[...] = jnp.zeros_like(loss_ref)

        xv = h_ref[...]
        r = lax.rsqrt(jnp.mean(xv * xv, axis=-1, keepdims=True) + EPS)
        xh = xv * r
        err = xh * g_ref[...] - t_ref[...]
        loss_ref[...] += jnp.full(loss_ref.shape, 0.5 * jnp.sum(jnp.mean(err * err, axis=-1, keepdims=True)), F32)
        dyv = err * (1.0 / d)
        dyg = dyv * g_ref[...]
        dh = r * (dyg - xh * jnp.mean(dyg * xh, axis=-1, keepdims=True))
        dh_ref[...] = dh
        dha_ref[...] = dh.astype(dha_ref.dtype)
        dg_ref[...] += jnp.sum(dyv * xh, axis=0, keepdims=True)

    row = pl.BlockSpec((tm, d), lambda i: (i, 0))
    vec = pl.BlockSpec((1, d), lambda i: (0, 0))
    return pl.pallas_call(
        body, name=name, grid=(s // tm,), in_specs=[row, vec, row],
        out_specs=[pl.BlockSpec((1, 128), lambda i: (0, 0)), row, row, vec],
        out_shape=[jax.ShapeDtypeStruct((1, 128), F32), jax.ShapeDtypeStruct((s, d), F32),
                   jax.ShapeDtypeStruct((s, d), _ACT), jax.ShapeDtypeStruct((1, d), F32)],
        compiler_params=_cp(VMEM_BIG))(h, g, target)


def _gmlp_parts(u, v, lng, lnb):
    mu = jnp.mean(v, axis=-1, keepdims=True)
    vc = v - mu
    rstd = lax.rsqrt(jnp.mean(vc * vc, axis=-1, keepdims=True) + EPS)
    xhat = vc * rstd
    vn = xhat * lng + lnb
    return u, xhat, rstd, vn


def _gmlp_fwd(proj, lng, lnb, ws, bs3, name):
    s = proj.shape[0]
    per_step = 2 if s % (2 * CHUNK) == 0 else 1
    rows = per_step * CHUNK

    def body(pu_ref, pv_ref, lng_ref, lnb_ref, ws_ref, bs_ref, o_ref):
        causal = _iota((CHUNK, CHUNK), 0) >= _iota((CHUNK, CHUNK), 1)
        for c in range(per_step):
            rs = slice(c * CHUNK, (c + 1) * CHUNK)
            u, _, _, vn = _gmlp_parts(_gelu(pu_ref[rs, :]), _gelu(pv_ref[rs, :]), lng_ref[...], lnb_ref[...])
            for g in range(A_GROUPS):
                sl = slice(g * A_GW, (g + 1) * A_GW)
                w = jnp.where(causal, ws_ref[g], 0.0)
                sv = _dot(w, vn[:, sl]) + bs_ref[g]
                o_ref[rs, sl] = (u[:, sl] * sv).astype(o_ref.dtype)

    full = lambda shape: pl.BlockSpec(shape, lambda c: (0,) * len(shape))
    return pl.pallas_call(
        body, name=name, grid=(s // rows,),
        in_specs=[pl.BlockSpec((rows, D_INNER), lambda c: (c, 0)), pl.BlockSpec((rows, D_INNER), lambda c: (c, 1)),
                  full((1, D_INNER)), full((1, D_INNER)), full((A_GROUPS, CHUNK, CHUNK)), full((A_GROUPS, CHUNK, 1))],
        out_specs=pl.BlockSpec((rows, D_INNER), lambda c: (c, 0)),
        out_shape=jax.ShapeDtypeStruct((s, D_INNER + X_WIDTH), _ACT), compiler_params=_cp(VMEM_BIG))(proj, proj, lng, lnb, ws, bs3)


def _gmlp_bwd(proj, dcat, lng, lnb, ws, bs3, name):
    s = proj.shape[0]

    def body(pu_ref, pv_ref, dm_ref, lng_ref, lnb_ref, ws_ref, bs_ref, dp_ref, dws_ref, dbs_ref, dlng_ref, dlnb_ref, dvn_ref):
        @pl.when(pl.program_id(0) == 0)
        def _():
            dws_ref[...] = jnp.zeros_like(dws_ref)
            dbs_ref[...] = jnp.zeros_like(dbs_ref)
            dlng_ref[...] = jnp.zeros_like(dlng_ref)
            dlnb_ref[...] = jnp.zeros_like(dlnb_ref)

        lng = lng_ref[...]
        u, u_grad = _gelu_with_grad(pu_ref[...])
        v, v_grad = _gelu_with_grad(pv_ref[...])
        u, xhat, rstd, vn = _gmlp_parts(u, v, lng, lnb_ref[...])
        dm = dm_ref[...].astype(F32)
        causal = _iota((CHUNK, CHUNK), 0) >= _iota((CHUNK, CHUNK), 1)
        for g in range(A_GROUPS):
            sl = slice(g * A_GW, (g + 1) * A_GW)
            w = jnp.where(causal, ws_ref[g], 0.0)
            sv = _dot(w, vn[:, sl]) + bs_ref[g]
            dsv = dm[:, sl] * u[:, sl]
            dp_ref[:, sl] = (dm[:, sl] * sv * u_grad[:, sl]).astype(dp_ref.dtype)
            dvn_ref[:, sl] = _dot_tn(w, dsv)
            dws_ref[g] += jnp.where(causal, _dot_nt(dsv, vn[:, sl]), 0.0)
            dbs_ref[g] += jnp.sum(dsv, axis=-1, keepdims=True)
        dvn = dvn_ref[...]
        dlng_ref[...] += jnp.sum(dvn * xhat, axis=0, keepdims=True)
        dlnb_ref[...] += jnp.sum(dvn, axis=0, keepdims=True)
        dxh = dvn * lng
        dv = rstd * (dxh - jnp.mean(dxh, axis=-1, keepdims=True) - xhat * jnp.mean(dxh * xhat, axis=-1, keepdims=True))
        dp_ref[:, D_INNER:] = (dv * v_grad).astype(dp_ref.dtype)

    full = lambda shape: pl.BlockSpec(shape, lambda c: (0,) * len(shape))
    return pl.pallas_call(
        body, name=name, grid=(s // CHUNK,),
        in_specs=[pl.BlockSpec((CHUNK, D_INNER), lambda c: (c, 0)), pl.BlockSpec((CHUNK, D_INNER), lambda c: (c, 1)),
                  pl.BlockSpec((CHUNK, D_INNER), lambda c: (c, 0)),
                  full((1, D_INNER)), full((1, D_INNER)), full((A_GROUPS, CHUNK, CHUNK)), full((A_GROUPS, CHUNK, 1))],
        out_specs=[pl.BlockSpec((CHUNK, 2 * D_INNER), lambda c: (c, 0)), full((A_GROUPS, CHUNK, CHUNK)),
                   full((A_GROUPS, CHUNK, 1)), full((1, D_INNER)), full((1, D_INNER))],
        out_shape=[jax.ShapeDtypeStruct((s, 2 * D_INNER + X_WIDTH), _ACT), jax.ShapeDtypeStruct((A_GROUPS, CHUNK, CHUNK), F32),
                   jax.ShapeDtypeStruct((A_GROUPS, CHUNK, 1), F32), jax.ShapeDtypeStruct((1, D_INNER), F32),
                   jax.ShapeDtypeStruct((1, D_INNER), F32)],
        scratch_shapes=[pltpu.VMEM((CHUNK, D_INNER), F32)],
        compiler_params=_cp(VMEM_BIG))(proj, proj, dcat, lng, lnb, ws, bs3)


_X_SCALE = 1.0 / math.sqrt(X_HD)


def _attn_fwd(proj, qblk, kv, cat, name, tm=512):
    s = proj.shape[0]
    tm = min(tm, s)

    def body(q_ref, kv_ref, cat_ref, o_ref):
        for h in range(X_HEADS):
            sl = slice(h * X_HD, (h + 1) * X_HD)
            k = kv_ref[:, sl]
            v = kv_ref[:, X_WIDTH + h * X_HD:X_WIDTH + (h + 1) * X_HD]
            sc = _dot_nt(q_ref[:, sl], k) * _X_SCALE
            e = jnp.exp(sc - jnp.max(sc, axis=-1, keepdims=True))
            p = e / jnp.sum(e, axis=-1, keepdims=True)
            o_ref[:, sl] = _dot(p, v).astype(o_ref.dtype)

    return pl.pallas_call(
        body, name=name, grid=(s // tm,),
        in_specs=[pl.BlockSpec((tm, X_WIDTH), lambda i: (i, qblk)), pl.BlockSpec((N_MEM, 2 * X_WIDTH), lambda i: (0, 0)),
                  pl.BlockSpec(memory_space=pl.ANY)],
        out_specs=pl.BlockSpec((tm, X_WIDTH), lambda i: (i, D_INNER // X_WIDTH)),
        out_shape=jax.ShapeDtypeStruct(cat.shape, cat.dtype), input_output_aliases={2: 0},
        compiler_params=_cp(VMEM_BIG))(proj, kv, cat)


def _attn_bwd(proj, qblk, kv, dcat, dproj, name, tm=512):
    s = proj.shape[0]
    tm = min(tm, s)

    def body(q_ref, kv_ref, do_ref, dproj_ref, dq_ref, dkv_ref):
        @pl.when(pl.program_id(0) == 0)
        def _():
            dkv_ref[...] = jnp.zeros_like(dkv_ref)

        for h in range(X_HEADS):
            sl = slice(h * X_HD, (h + 1) * X_HD)
            slv = slice(X_WIDTH + h * X_HD, X_WIDTH + (h + 1) * X_HD)
            q = q_ref[:, sl]
            k = kv_ref[:, sl]
            v = kv_ref[:, slv]
            do = do_ref[:, sl].astype(F32)
            sc = _dot_nt(q, k) * _X_SCALE
            e = jnp.exp(sc - jnp.max(sc, axis=-1, keepdims=True))
            p = e / jnp.sum(e, axis=-1, keepdims=True)
            dp = _dot_nt(do, v)
            ds = p * (dp - jnp.sum(dp * p, axis=-1, keepdims=True)) * _X_SCALE
            dq_ref[:, sl] = _dot(ds, k).astype(dq_ref.dtype)
            dkv_ref[:, sl] += _dot_tn(ds, q)
            dkv_ref[:, slv] += _dot_tn(p, do)

    return pl.pallas_call(
        body, name=name, grid=(s // tm,),
        in_specs=[pl.BlockSpec((tm, X_WIDTH), lambda i: (i, qblk)), pl.BlockSpec((N_MEM, 2 * X_WIDTH), lambda i: (0, 0)),
                  pl.BlockSpec((tm, X_WIDTH), lambda i: (i, 2)), pl.BlockSpec(memory_space=pl.ANY)],
        out_specs=[pl.BlockSpec((tm, X_WIDTH), lambda i: (i, qblk)), pl.BlockSpec((N_MEM, 2 * X_WIDTH), lambda i: (0, 0))],
        out_shape=[jax.ShapeDtypeStruct(dproj.shape, dproj.dtype), jax.ShapeDtypeStruct((N_MEM, 2 * X_WIDTH), F32)],
        input_output_aliases={3: 0}, compiler_params=_cp(VMEM_BIG))(proj, kv, dcat, dproj)


CONV_TC = 256
_XBC_BLK0 = D_INNER // CONV_TC


CONV_RB = 64
SUBLANES = 8


def _rows_before(cur, prev_last, j):
    rolled = pltpu.roll(cur, j, 0)
    head = jnp.where(_iota((SUBLANES, cur.shape[1]), 0) < j, pltpu.roll(prev_last, j, 0), rolled[:SUBLANES])
    return jnp.concatenate([head, rolled[SUBLANES:]], axis=0)


def _rows_after(cur, next_first, j):
    n = cur.shape[0]
    rolled = pltpu.roll(cur, n - j, 0)
    tail = jnp.where(_iota((SUBLANES, cur.shape[1]), 0) >= SUBLANES - j, pltpu.roll(next_first, SUBLANES - j, 0),
                     rolled[n - SUBLANES:])
    return jnp.concatenate([rolled[:n - SUBLANES], tail], axis=0)


def _conv_pre(x_ref, w_ref, b_ref, r0, prev_last):
    cur = x_ref[pl.ds(r0, CONV_RB), :]
    shifts = [_rows_before(cur, prev_last, j) for j in range(1, CONV_K)]
    pre = b_ref[...] + w_ref[CONV_K - 1:CONV_K, :] * cur
    for j in range(1, CONV_K):
        pre = pre + w_ref[CONV_K - 1 - j:CONV_K - j, :] * shifts[j - 1]
    return pre, cur, shifts


def _conv_fwd(proj, w, b, name):
    s = proj.shape[0]

    def body(x_ref, w_ref, b_ref, o_ref):
        xv = x_ref[...]
        rows = _iota(xv.shape, 0)
        pre = b_ref[...] + w_ref[CONV_K - 1:CONV_K, :] * xv
        for j in range(1, CONV_K):
            pre = pre + w_ref[CONV_K - 1 - j:CONV_K - j, :] * jnp.where(rows >= j, pltpu.roll(xv, j, 0), 0.0)
        o_ref[...] = pre * _sigmoid(pre)

    return pl.pallas_call(
        body, name=name, grid=(CONV_DIM // CONV_TC,),
        in_specs=[pl.BlockSpec((s, CONV_TC), lambda j: (0, _XBC_BLK0 + j)), pl.BlockSpec((CONV_K, CONV_TC), lambda j: (0, j)),
                  pl.BlockSpec((1, CONV_TC), lambda j: (0, j))],
        out_specs=pl.BlockSpec((s, CONV_TC), lambda j: (0, j)),
        out_shape=jax.ShapeDtypeStruct((s, CONV_DIM), F32), compiler_params=_cp(VMEM_BIG))(proj, w, b)


def _conv_bwd(proj, w, b, dxbc, dproj, name):
    s = proj.shape[0]

    nb = s // CONV_RB

    def body(x_ref, w_ref, b_ref, d_ref, dproj_ref, dx_ref, dw_ref, db_ref, dpre_ref):
        def fold(v):
            out = v[:SUBLANES]
            for t in range(1, CONV_RB // SUBLANES):
                out = out + v[t * SUBLANES:(t + 1) * SUBLANES]
            return out

        def first(i, carry):
            prev_last, acc = carry
            r0 = pl.multiple_of(i * CONV_RB, CONV_RB)
            pre, cur, shifts = _conv_pre(x_ref, w_ref, b_ref, r0, prev_last)
            sig = _sigmoid(pre)
            dpre = d_ref[pl.ds(r0, CONV_RB), :] * (sig * (1.0 + pre * (1.0 - sig)))
            dpre_ref[pl.ds(r0, CONV_RB), :] = dpre
            taps = [cur] + shifts
            acc = tuple(a + fold(dpre * t) for a, t in zip(acc[:CONV_K], taps)) + (acc[CONV_K] + fold(dpre),)
            return cur[CONV_RB - SUBLANES:], acc

        zero8 = jnp.zeros((SUBLANES, CONV_TC), F32)
        _, acc = lax.fori_loop(0, nb, first, (zero8, (zero8,) * (CONV_K + 1)))
        for j in range(CONV_K):
            dw_ref[CONV_K - 1 - j:CONV_K - j, :] = jnp.sum(acc[j], axis=0, keepdims=True)
        db_ref[...] = jnp.sum(acc[CONV_K], axis=0, keepdims=True)

        def second(i, next_first):
            r0 = pl.multiple_of((nb - 1 - i) * CONV_RB, CONV_RB)
            cur = dpre_ref[pl.ds(r0, CONV_RB), :]
            dx = w_ref[CONV_K - 1:CONV_K, :] * cur
            for j in range(1, CONV_K):
                dx = dx + w_ref[CONV_K - 1 - j:CONV_K - j, :] * _rows_after(cur, next_first, j)
            dx_ref[pl.ds(r0, CONV_RB), :] = dx.astype(dx_ref.dtype)
            return cur[:SUBLANES]

        lax.fori_loop(0, nb, second, zero8)

    return pl.pallas_call(
        body, name=name, grid=(CONV_DIM // CONV_TC,),
        in_specs=[pl.BlockSpec((s, CONV_TC), lambda j: (0, _XBC_BLK0 + j)), pl.BlockSpec((CONV_K, CONV_TC), lambda j: (0, j)),
                  pl.BlockSpec((1, CONV_TC), lambda j: (0, j)), pl.BlockSpec((s, CONV_TC), lambda j: (0, j)),
                  pl.BlockSpec(memory_space=pl.ANY)],
        out_specs=[pl.BlockSpec((s, CONV_TC), lambda j: (0, _XBC_BLK0 + j)), pl.BlockSpec((CONV_K, CONV_TC), lambda j: (0, j)),
                   pl.BlockSpec((1, CONV_TC), lambda j: (0, j))],
        out_shape=[jax.ShapeDtypeStruct(dproj.shape, dproj.dtype), jax.ShapeDtypeStruct((CONV_K, CONV_DIM), F32),
                   jax.ShapeDtypeStruct((1, CONV_DIM), F32)], input_output_aliases={4: 0},
        scratch_shapes=[pltpu.VMEM((s, CONV_TC), F32)],
        compiler_params=_cp(VMEM_BIG))(proj, w, b, dxbc, dproj)


def _ssd_common(dtc_ref, br_ref, ar_ref, csb_ref, cst_ref, csf_ref, dtf_ref, expand):
    a_row = -jnp.exp(ar_ref[...])
    dt_c = _softplus(dtc_ref[...] + br_ref[...])
    tril = _iota((CHUNK, CHUNK), 0) >= _iota((CHUNK, CHUNK), 1)
    cs = _sel_dot(tril, dt_c * a_row)
    cst_ref[...] = cs.T
    e64 = (jnp.right_shift(_iota((HPAD, D_INNER), 1), 6) == _iota((HPAD, D_INNER), 0)).astype(jnp.bfloat16)
    if expand:
        e128 = jnp.right_shift(_iota((HPAD, SSM_HEADS * CHUNK), 1), 7) == _iota((HPAD, SSM_HEADS * CHUNK), 0)
        csb_ref[...] = _dot_sel(cs, e128)
        dtf_ref[...] = _dot_sel(dt_c, e64)
        csf_ref[...] = _dot_sel(cs, e64)
    dt_full = dtf_ref[...]
    cs_full = csf_ref[...]
    cs_last = csf_ref[CHUNK - 1:CHUNK, :]
    e_full = jnp.exp(cs_full)
    f_full = jnp.exp(cs_last - cs_full)
    gamma = jnp.exp(cs_last)
    return a_row, dt_c, cs, dt_full, e_full, f_full, gamma, e64


def _ssd_lambda(csb_ref, cst_ref, h, causal):
    diff = csb_ref[:, h * CHUNK:(h + 1) * CHUNK] - cst_ref[h:h + 1, :]
    return jnp.exp(jnp.where(causal, diff, -1e30))


_SSD_VEC_SPECS = lambda: [pl.BlockSpec((1, HPAD), lambda c: (0, 0)), pl.BlockSpec((1, HPAD), lambda c: (0, 0)),
                          pl.BlockSpec((1, D_INNER), lambda c: (0, 0))]


def _ssd_fwd(xbc, dtc, bias_row, alog_row, dfull, name):
    s = xbc.shape[0]
    nc = s // CHUNK

    def body(xbc_ref, dtc_ref, br_ref, ar_ref, df_ref, y_ref, st_ref, csb_ref, csf_ref, dtf_ref, ht_ref, cst_ref):
        @pl.when(pl.program_id(0) == 0)
        def _():
            ht_ref[...] = jnp.zeros_like(ht_ref)

        _, _, _, dt_full, e_full, f_full, gamma, _ = _ssd_common(
            dtc_ref, br_ref, ar_ref, csb_ref, cst_ref, csf_ref, dtf_ref, expand=True)
        x = xbc_ref[:, :D_INNER]
        xdt = x * dt_full
        st_ref[...] = ht_ref[...]
        causal = _iota((CHUNK, CHUNK), 0) >= _iota((CHUNK, CHUNK), 1)
        lo = _iota((CHUNK, CHUNK), 1) < SSM_P
        for g in range(SSM_GROUPS):
            gs = slice(g * SSM_GW, (g + 1) * SSM_GW)
            bg = xbc_ref[:, D_INNER + g * SSM_N:D_INNER + (g + 1) * SSM_N]
            cg = xbc_ref[:, D_INNER + SSM_GROUPS * SSM_N + g * SSM_N:D_INNER + SSM_GROUPS * SSM_N + (g + 1) * SSM_N]
            ht = ht_ref[:, gs]
            cb = _dot_nt(cg, bg)
            yoff = e_full[:, gs] * _dot(cg, ht)
            for jp in range(SSM_GW // CHUNK):
                j = g * (SSM_GW // CHUNK) + jp
                ps = slice(j * CHUNK, (j + 1) * CHUNK)
                x2 = xdt[:, ps]
                y0 = _dot(cb * _ssd_lambda(csb_ref, cst_ref, 2 * j, causal), x2)
                y1 = _dot(cb * _ssd_lambda(csb_ref, cst_ref, 2 * j + 1, causal), x2)
                y_ref[:, ps] = (jnp.where(lo, y0, y1) + yoff[:, jp * CHUNK:(jp + 1) * CHUNK]
                                + x[:, ps] * df_ref[:, ps])
            ht_ref[:, gs] = gamma[:, gs] * ht + _dot_tn(bg, xdt[:, gs] * f_full[:, gs])

    return pl.pallas_call(
        body, name=name, grid=(nc,),
        in_specs=[pl.BlockSpec((CHUNK, CONV_DIM), lambda c: (c, 0)), pl.BlockSpec((CHUNK, HPAD), lambda c: (c, 0))]
                 + _SSD_VEC_SPECS(),
        out_specs=[pl.BlockSpec((CHUNK, D_INNER), lambda c: (c, 0)), pl.BlockSpec((None, SSM_N, D_INNER), lambda c: (c, 0, 0)),
                   pl.BlockSpec((CHUNK, SSM_HEADS * CHUNK), lambda c: (c, 0)), pl.BlockSpec((CHUNK, D_INNER), lambda c: (c, 0)),
                   pl.BlockSpec((CHUNK, D_INNER), lambda c: (c, 0))],
        out_shape=[jax.ShapeDtypeStruct((s, D_INNER), F32), jax.ShapeDtypeStruct((nc, SSM_N, D_INNER), F32),
                   jax.ShapeDtypeStruct((s, SSM_HEADS * CHUNK), F32), jax.ShapeDtypeStruct((s, D_INNER), F32),
                   jax.ShapeDtypeStruct((s, D_INNER), F32)],
        scratch_shapes=[pltpu.VMEM((SSM_N, D_INNER), F32), pltpu.VMEM((HPAD, CHUNK), F32)],
        compiler_params=_cp(VMEM_BIG))(xbc, dtc, bias_row, alog_row, dfull)


def _ssd_bwd(xbc, dtc, bias_row, alog_row, dfull, dy, states, expansions, name):
    s = xbc.shape[0]
    nc = s // CHUNK
    rev = lambda c: nc - 1 - c

    def body(xbc_ref, dtc_ref, br_ref, ar_ref, df_ref, dy_ref, st_ref, csb_ref, csf_ref, dtf_ref,
             dxbc_ref, ddt_ref, dalog_ref, dd_ref, dbias_ref,
             dht_ref, cst_ref, ddf_ref, dxs_ref, dcsf_ref, dcsl_ref):
        step = pl.program_id(0)

        @pl.when(step == 0)
        def _():
            dht_ref[...] = jnp.zeros_like(dht_ref)
            ddf_ref[...] = jnp.zeros_like(ddf_ref)
            dalog_ref[...] = jnp.zeros_like(dalog_ref)
            dbias_ref[...] = jnp.zeros_like(dbias_ref)
            dd_ref[...] = jnp.zeros_like(dd_ref)

        a_row, dt_c, _, dt_full, e_full, f_full, gamma, e64 = _ssd_common(
            dtc_ref, br_ref, ar_ref, csb_ref, cst_ref, csf_ref, dtf_ref, expand=False)
        x = xbc_ref[:, :D_INNER]
        xdt = x * dt_full
        dy_all = dy_ref[...]
        ddf_ref[...] += jnp.broadcast_to(jnp.sum(dy_all * x, axis=0, keepdims=True), ddf_ref.shape)
        causal = _iota((CHUNK, CHUNK), 0) >= _iota((CHUNK, CHUNK), 1)
        lo = _iota((CHUNK, CHUNK), 1) < SSM_P
        head_lane = _iota((CHUNK, HPAD), 1)
        head_row = _iota((HPAD, CHUNK), 0)
        dcs_heads = jnp.zeros((CHUNK, HPAD), F32)
        dcs_cols = jnp.zeros((HPAD, CHUNK), F32)
        for g in range(SSM_GROUPS):
            gs = slice(g * SSM_GW, (g + 1) * SSM_GW)
            b0 = D_INNER + g * SSM_N
            c0 = D_INNER + SSM_GROUPS * SSM_N + g * SSM_N
            bg = xbc_ref[:, b0:b0 + SSM_N]
            cg = xbc_ref[:, c0:c0 + SSM_N]
            ht = st_ref[:, gs]
            dht = dht_ref[:, gs]
            dyg = dy_all[:, gs]
            eg, fg, gg = e_full[:, gs], f_full[:, gs], gamma[:, gs]
            z = _dot(cg, ht)
            dz = dyg * eg
            dcg = _dot_nt(dz, ht)
            dht_new = _dot_tn(cg, dz) + gg * dht
            xf = xdt[:, gs] * fg
            dxf = _dot(bg, dht)
            dbg = _dot_nt(xf, dht)
            dff = dxf * xf
            dcsf_ref[:, gs] = dyg * eg * z - dff
            dcsl_ref[:, gs] = jnp.broadcast_to(
                jnp.sum(dff, axis=0, keepdims=True) + jnp.sum(dht * ht, axis=0, keepdims=True) * gg, (8, SSM_GW))
            cb = _dot_nt(cg, bg)
            dcb = jnp.zeros((CHUNK, CHUNK), F32)
            for jp in range(SSM_GW // CHUNK):
                j = g * (SSM_GW // CHUNK) + jp
                ps = slice(j * CHUNK, (j + 1) * CHUNK)
                x2 = xdt[:, ps]
                dy2 = dy_all[:, ps]
                dxh = []
                for hh in range(2):
                    h = 2 * j + hh
                    lam = _ssd_lambda(csb_ref, cst_ref, h, causal)
                    mh = cb * lam
                    dyh = jnp.where(lo, dy2, 0.0) if hh == 0 else jnp.where(lo, 0.0, dy2)
                    dm = _dot_nt(dyh, x2)
                    dcb = dcb + dm * lam
                    gm = dm * mh
                    dcs_heads = dcs_heads + jnp.where(head_lane == h, jnp.sum(gm, axis=1, keepdims=True), 0.0)
                    dcs_cols = dcs_cols + jnp.where(head_row == h, jnp.sum(gm, axis=0, keepdims=True), 0.0)
                    dxh.append(_dot_tn(mh, dy2))
                dxs_ref[:, ps] = jnp.where(lo, dxh[0], dxh[1]) + dxf[:, jp * CHUNK:(jp + 1) * CHUNK] * fg[:, jp * CHUNK:(jp + 1) * CHUNK]
            dxbc_ref[:, b0:b0 + SSM_N] = (dbg + _dot_tn(dcb, cg)).astype(dxbc_ref.dtype)
            dxbc_ref[:, c0:c0 + SSM_N] = (dcg + _dot(dcb, bg)).astype(dxbc_ref.dtype)
            dht_ref[:, gs] = dht_new
        dxs = dxs_ref[...]
        dcs_heads = dcs_heads - dcs_cols.T + _dot_sel(dcsf_ref[...], e64, ((1,), (1,)))
        dcs_last = _dot_sel(dcsl_ref[...], e64, ((1,), (1,)))
        dcs_heads = dcs_heads + jnp.where(_iota((CHUNK, HPAD), 0) == CHUNK - 1, dcs_last[0:1, :], 0.0)
        triu = _iota((CHUNK, CHUNK), 0) <= _iota((CHUNK, CHUNK), 1)
        dda = _sel_dot(triu, dcs_heads)
        ddt = dda * a_row + _dot_sel(dxs * x, e64, ((1,), (1,)))
        dxbc_ref[:, :D_INNER] = (dxs * dt_full + dy_all * df_ref[...]).astype(dxbc_ref.dtype)
        dalog_ref[...] += jnp.sum(dda * dt_c, axis=0, keepdims=True) * a_row
        ddt_raw = ddt * _sigmoid(dtc_ref[...] + br_ref[...])
        ddt_ref[...] = ddt_raw.astype(ddt_ref.dtype)
        dbias_ref[...] += jnp.sum(ddt_raw, axis=0, keepdims=True)

        @pl.when(step == nc - 1)
        def _():
            dd_ref[...] = _dot_sel(ddf_ref[...], e64, ((1,), (1,)))[0:1, :]

    vec = pl.BlockSpec((1, HPAD), lambda c: (0, 0))
    return pl.pallas_call(
        body, name=name, grid=(nc,),
        in_specs=[pl.BlockSpec((CHUNK, CONV_DIM), lambda c: (rev(c), 0)), pl.BlockSpec((CHUNK, HPAD), lambda c: (rev(c), 0))]
                 + _SSD_VEC_SPECS()
                 + [pl.BlockSpec((CHUNK, D_INNER), lambda c: (rev(c), 0)),
                    pl.BlockSpec((None, SSM_N, D_INNER), lambda c: (rev(c), 0, 0)),
                    pl.BlockSpec((CHUNK, SSM_HEADS * CHUNK), lambda c: (rev(c), 0)),
                    pl.BlockSpec((CHUNK, D_INNER), lambda c: (rev(c), 0)), pl.BlockSpec((CHUNK, D_INNER), lambda c: (rev(c), 0))],
        out_specs=[pl.BlockSpec((CHUNK, CONV_DIM), lambda c: (rev(c), 0)), pl.BlockSpec((CHUNK, HPAD), lambda c: (rev(c), 0)),
                   vec, vec, vec],
        out_shape=[jax.ShapeDtypeStruct((s, CONV_DIM), F32), jax.ShapeDtypeStruct((s, HPAD), _ACT),
                   jax.ShapeDtypeStruct((1, HPAD), F32), jax.ShapeDtypeStruct((1, HPAD), F32),
                   jax.ShapeDtypeStruct((1, HPAD), F32)],
        scratch_shapes=[pltpu.VMEM((SSM_N, D_INNER), F32), pltpu.VMEM((HPAD, CHUNK), F32),
                        pltpu.VMEM((8, D_INNER), F32), pltpu.VMEM((CHUNK, D_INNER), F32),
                        pltpu.VMEM((CHUNK, D_INNER), F32), pltpu.VMEM((8, D_INNER), F32)],
        compiler_params=_cp(VMEM_BIG))(xbc, dtc, bias_row, alog_row, dfull, dy, states, *expansions)


def _gate_fwd(y, proj, gn, name, tm=512):
    s = y.shape[0]
    tm = min(tm, s)

    def body(y_ref, z_ref, gn_ref, o_ref):
        for g in range(SSM_GROUPS):
            gs = slice(g * SSM_GW, (g + 1) * SSM_GW)
            z = z_ref[:, gs]
            t = y_ref[:, gs] * (z * _sigmoid(z))
            r = lax.rsqrt(jnp.mean(t * t, axis=-1, keepdims=True) + EPS)
            o_ref[:, gs] = (t * r * gn_ref[:, gs]).astype(o_ref.dtype)

    row = pl.BlockSpec((tm, D_INNER), lambda i: (i, 0))
    return pl.pallas_call(
        body, name=name, grid=(s // tm,), in_specs=[row, row, pl.BlockSpec((1, D_INNER), lambda i: (0, 0))],
        out_specs=row, out_shape=jax.ShapeDtypeStruct((s, D_INNER + X_WIDTH), _ACT),
        compiler_params=_cp(VMEM_BIG))(y, proj, gn)


def _gate_bwd(y, proj, gn, dcat, name, tm=512):
    s = y.shape[0]
    tm = min(tm, s)

    def body(y_ref, z_ref, gn_ref, dm_ref, dy_ref, dz_ref, dgn_ref):
        @pl.when(pl.program_id(0) == 0)
        def _():
            dgn_ref[...] = jnp.zeros_like(dgn_ref)

        for g in range(SSM_GROUPS):
            gs = slice(g * SSM_GW, (g + 1) * SSM_GW)
            z = z_ref[:, gs]
            yv = y_ref[:, gs]
            sig = _sigmoid(z)
            sz = z * sig
            t = yv * sz
            r = lax.rsqrt(jnp.mean(t * t, axis=-1, keepdims=True) + EPS)
            th = t * r
            dm = dm_ref[:, gs].astype(F32)
            dmg = dm * gn_ref[:, gs]
            dt_ = r * (dmg - th * jnp.mean(dmg * th, axis=-1, keepdims=True))
            dgn_ref[:, gs] += jnp.sum(dm * th, axis=0, keepdims=True)
            dy_ref[:, gs] = dt_ * sz
            dz_ref[:, gs] = (dt_ * yv * (sig * (1.0 + z * (1.0 - sig)))).astype(dz_ref.dtype)

    row = pl.BlockSpec((tm, D_INNER), lambda i: (i, 0))
    vec = pl.BlockSpec((1, D_INNER), lambda i: (0, 0))
    return pl.pallas_call(
        body, name=name, grid=(s // tm,), in_specs=[row, row, vec, row], out_specs=[row, row, vec],
        out_shape=[jax.ShapeDtypeStruct((s, D_INNER), F32), jax.ShapeDtypeStruct((s, 6 * D_MODEL), _ACT),
                   jax.ShapeDtypeStruct((1, D_INNER), F32)], compiler_params=_cp(VMEM_BIG))(y, proj, gn, dcat)


def _block_of(kind, width):
    if kind == "col":
        return lambda ref, j: ref.at[:, :, pl.ds(pl.multiple_of(j * width, 128), width)]
    if kind == "row":
        return lambda ref, j: ref.at[:, pl.ds(pl.multiple_of(j * width, 8), width), :]
    return lambda ref, j: ref.at[j]


def _coords():
    return lax.axis_index("x"), lax.axis_index("y"), lax.axis_index("c")


def _rel_chip(x, y, k):
    return (1 - x if k & 1 else x), (1 - y if k & 2 else y)


def _all_gather_body(ins, outs, send_sems, recv_sems, local_sems, blocks):
    n = len(ins)
    x, y, c = _coords()
    sibling = (x, y, 1 - c)
    via = (x + (1 - c) * (1 - 2 * x), y + c * (1 - 2 * y))
    onto = (x + c * (1 - 2 * x), y + (1 - c) * (1 - 2 * y))

    def copy(t, k, chip, core, to, src=None):
        dst = blocks[t](outs[t], 4 * chip[0] + 2 * chip[1] + core)
        return pltpu.make_async_remote_copy(
            src_ref=dst if src is None else src, dst_ref=dst, send_sem=send_sems.at[t, k],
            recv_sem=recv_sems.at[t, k], device_id=to, device_id_type=MESH)

    started = []
    for t in range(n):
        mine = pltpu.make_async_copy(ins[t], blocks[t](outs[t], 4 * x + 2 * y + c), local_sems.at[t])
        mine.start()
        started.append(mine)
    sends = []
    for t in range(n):
        for k in range(3):
            px, py = _rel_chip(x, y, k)
            cp = copy(t, k, (x, y), c, (px, py, 1 - c if k == 0 else c), src=ins[t])
            cp.start()
            sends.append(cp)
    for t in range(n):
        for k in (1, 2):
            chip = _rel_chip(x, y, k)
            copy(t, k, chip, c, sibling).wait_recv()
            fwd = copy(t, 3 + k, chip, c, sibling)
            fwd.start()
            sends.append(fwd)
        hop = copy(t, 3, via, c, (*onto, c))
        hop.start()
        sends.append(hop)
    for t in range(n):
        diagonal = _rel_chip(x, y, 3)
        copy(t, 3, diagonal, c, sibling).wait_recv()
        fwd = copy(t, 6, diagonal, c, sibling)
        fwd.start()
        sends.append(fwd)
    for t in range(n):
        copy(t, 0, (x, y), 1 - c, sibling).wait_recv()
        for k in range(1, 4):
            copy(t, 3 + k, _rel_chip(x, y, k), 1 - c, sibling).wait_recv()
    for cp in sends:
        cp.wait_send()
    for mine in started:
        mine.wait()


def _handshake(peers):
    barrier = pltpu.get_barrier_semaphore()
    for peer in peers:
        pl.semaphore_signal(barrier, inc=1, device_id=peer, device_id_type=MESH)
    pl.semaphore_wait(barrier, len(peers))


def _gather_peers():
    x, y, c = _coords()
    return [(x, y, 1 - c)] + [(*_rel_chip(x, y, k), c) for k in (1, 2)]


SEQ_ID_GATHER, SEQ_ID_SIBLING, SEQ_ID_CHIPS = 1, 2, 3


def _sequencer_call(body, peers, operands, out_types, sems, name, collective_id, after=()):
    n_in, n_out, n_after = len(operands), len(out_types), len(after)

    def launch(*refs):
        _handshake(peers())
        body(refs[:n_in], refs[n_in + n_after:n_in + n_after + n_out], *refs[n_in + n_after + n_out:])

    return pl.kernel(
        launch, name=name, out_type=out_types, mesh=plsc.ScalarSubcoreMesh(axis_name="seq", num_cores=1),
        scratch_types=sems, compiler_params=pltpu.CompilerParams(collective_id=collective_id))(*operands, *after)


def _all_gather_seq(shards, layouts, name, after=()):
    n = len(shards)
    blocks = [_block_of(kind, width) for kind, width, _ in layouts]
    return _sequencer_call(
        lambda ins, outs, *sems: _all_gather_body(ins, outs, *sems, blocks), _gather_peers, shards,
        [jax.ShapeDtypeStruct(shape, sh.dtype) for sh, (_, _, shape) in zip(shards, layouts)],
        [pltpu.SemaphoreType.DMA((n, 7)), pltpu.SemaphoreType.DMA((n, 7)), pltpu.SemaphoreType.DMA((n,))],
        name, SEQ_ID_GATHER, after)


def _tie(small, after):
    return lax.optimization_barrier((small, *after))[0]


def _rs_to_sibling(grads, layouts, name, after=()):
    n = len(grads)
    blocks = [_block_of(kind, width) for kind, width, _ in layouts]

    def body(ins, outs, send_sems, recv_sems):
        x, y, c = _coords()
        sibling = (x, y, 1 - c)
        cps = []
        for t in range(n):
            for k in range(4):
                px, py = _rel_chip(x, y, k)
                cp = pltpu.make_async_remote_copy(
                    src_ref=blocks[t](ins[t], 4 * px + 2 * py + (1 - c)), dst_ref=outs[t].at[k],
                    send_sem=send_sems.at[t, k], recv_sem=recv_sems.at[t, k], device_id=sibling, device_id_type=MESH)
                cp.start()
                cps.append(cp)
        for cp in cps:
            cp.wait_recv()
        for cp in cps:
            cp.wait_send()

    def sibling_only():
        x, y, c = _coords()
        return [(x, y, 1 - c)]

    return _sequencer_call(
        body, sibling_only, grads,
        [jax.ShapeDtypeStruct((4,) + shape, g.dtype) for g, (_, _, shape) in zip(grads, layouts)],
        [pltpu.SemaphoreType.DMA((n, 4)), pltpu.SemaphoreType.DMA((n, 4))], name, SEQ_ID_SIBLING, after)


def _rs_chip_sum(grad, recv, layout, xyc, name):
    kind, width, shape = layout
    r, ccols = shape

    def src_index(step, xyc_ref):
        k = step + 1
        px = jnp.where(k % 2 == 1, 1 - xyc_ref[0], xyc_ref[0])
        py = jnp.where(k // 2 == 1, 1 - xyc_ref[1], xyc_ref[1])
        return 4 * px + 2 * py + xyc_ref[2]

    if kind == "col":
        g_spec = pl.BlockSpec((r, ccols), lambda k, s_: (0, src_index(k, s_)))
    elif kind == "row":
        g_spec = pl.BlockSpec((r, ccols), lambda k, s_: (src_index(k, s_), 0))
    else:
        g_spec = pl.BlockSpec((None, r, ccols), lambda k, s_: (src_index(k, s_), 0, 0))

    def body(xyc_ref, g_ref, r_ref, o_ref):
        o_ref[...] = (g_ref[...].astype(F32) + r_ref[...].astype(F32)).astype(o_ref.dtype)

    slot = pl.BlockSpec((None, r, ccols), lambda k, s_: (k + 1, 0, 0))
    return pl.pallas_call(
        body, name=name,
        grid_spec=pltpu.PrefetchScalarGridSpec(num_scalar_prefetch=1, grid=(3,), in_specs=[g_spec, slot], out_specs=slot),
        out_shape=jax.ShapeDtypeStruct((4, r, ccols), grad.dtype), compiler_params=_cp(VMEM_BIG))(xyc, grad, recv)


def _rs_across_chips(parts, name):
    n = len(parts)

    def body(ins, outs, send_sems, recv_sems):
        x, y, c = _coords()
        cps = []
        for t in range(n):
            for k in range(1, 4):
                px, py = _rel_chip(x, y, k)
                cp = pltpu.make_async_remote_copy(
                    src_ref=ins[t].at[k], dst_ref=outs[t].at[k - 1], send_sem=send_sems.at[t, k - 1],
                    recv_sem=recv_sems.at[t, k - 1], device_id=(px, py, c), device_id_type=MESH)
                cp.start()
                cps.append(cp)
        for cp in cps:
            cp.wait_recv()
        for cp in cps:
            cp.wait_send()

    def other_chips():
        x, y, c = _coords()
        return [(*_rel_chip(x, y, k), c) for k in range(1, 4)]

    return _sequencer_call(
        body, other_chips, parts, [jax.ShapeDtypeStruct((3,) + p.shape[1:], p.dtype) for p in parts],
        [pltpu.SemaphoreType.DMA((n, 3)), pltpu.SemaphoreType.DMA((n, 3))], name, SEQ_ID_CHIPS)


def _adamw_math(w, g, m, v):
    m = ADAM_B1 * m + (1.0 - ADAM_B1) * g
    v = ADAM_B2 * v + (1.0 - ADAM_B2) * jnp.square(g)
    m_hat = m / (1.0 - ADAM_B1 ** ADAM_STEP)
    v_hat = v / (1.0 - ADAM_B2 ** ADAM_STEP)
    delta = -ADAM_LR * (m_hat / (jnp.sqrt(v_hat) + ADAM_EPS) + ADAM_WD * w)
    return delta, m, v


def _row_tile(rows, cap):
    best = None
    for cand in range(8, min(rows, cap) + 1, 8):
        if rows % cand == 0:
            best = cand
    assert best is not None, rows
    return best


def _adamw(w, m, v, own, parts, me, name, layer, prev=None, tr=256):
    r, ccols = w.shape[-2:]
    npart = len(parts)
    if r % 8 == 0:
        tr, tc = _row_tile(r, tr), ccols
        steps, at = r // tr, (lambda i: (i, 0))
    else:
        tr, tc = r, 256
        assert ccols % tc == 0
        steps, at = ccols // tc, (lambda i: (0, i))

    def spec(lead):
        return pl.BlockSpec((None, tr, tc), lambda i, me_ref: (lead,) + at(i))

    grad, kind = own
    if kind == "col":
        own_spec = pl.BlockSpec((tr, tc), lambda i, me_ref: (at(i)[0], me_ref[0]))
    elif kind == "row":
        own_spec = pl.BlockSpec((tr, tc), lambda i, me_ref: (me_ref[0] * (r // tr) + at(i)[0], 0))
    else:
        own_spec = pl.BlockSpec((None, tr, tc), lambda i, me_ref: (me_ref[0],) + at(i))

    def body(me_ref, *refs):
        w_ref, m_ref, v_ref = refs[:3]
        p_refs = refs[3:4 + npart]
        outs = refs[len(refs) - 4:]
        g = p_refs[0][...].astype(F32)
        for p_ref in p_refs[1:]:
            g = g + p_ref[...].astype(F32)
        delta, mn, vn = _adamw_math(w_ref[...], g, m_ref[...], v_ref[...])
        outs[0][...] = g
        outs[1][...] = delta
        outs[2][...] = mn
        outs[3][...] = vn

    operands = [w, m, v, grad] + [p for p, _ in parts]
    in_specs = [spec(layer)] * 3 + [own_spec] + [spec(lead) for _, lead in parts]
    aliases = {}
    if prev is not None:
        for i, p in enumerate(prev):
            aliases[1 + len(operands)] = i
            operands.append(p)
            in_specs.append(pl.BlockSpec(memory_space=pl.ANY))
    return pl.pallas_call(
        body, name=name,
        grid_spec=pltpu.PrefetchScalarGridSpec(num_scalar_prefetch=1, grid=(steps,), in_specs=in_specs,
                                               out_specs=[spec(layer)] * 4),
        out_shape=[jax.ShapeDtypeStruct(w.shape, F32)] * 4, input_output_aliases=aliases,
        compiler_params=_cp(VMEM_BIG))(me, *operands)


def _small_update(gathered, params, loss_all, me, name):
    n = len(gathered)
    shapes = [w.shape for w, _, _ in params]

    def body(me_ref, *refs):
        g_refs, loss_ref = refs[:n], refs[n]
        p_refs = refs[n + 1:n + 1 + 3 * n]
        o_refs = refs[n + 1 + 3 * n:]
        for i in range(n):
            r, c = shapes[i]
            if gathered[i].shape[2] == c:
                parts = [g_refs[i][j] for j in range(N_DEV)]
            else:
                off = pl.multiple_of(me_ref[0] * c, 128)
                parts = [g_refs[i][j, :, pl.ds(off, c)] for j in range(N_DEV)]
            g = functools.reduce(lambda a, b: a + b, parts)
            delta, mn, vn = _adamw_math(p_refs[3 * i][...], g, p_refs[3 * i + 1][...], p_refs[3 * i + 2][...])
            for k, val in enumerate((g, delta, mn, vn)):
                o_refs[4 * i + k][...] = val
        o_refs[4 * n][...] = functools.reduce(lambda a, b: a + b, [loss_ref[j] for j in range(N_DEV)])

    vmem = pl.BlockSpec(memory_space=pltpu.VMEM)
    flat_params = [a for p in params for a in p]
    outs = pl.pallas_call(
        body, name=name, in_specs=[pl.BlockSpec(memory_space=pltpu.SMEM)] + [vmem] * (n + 1 + 3 * n),
        out_specs=[vmem] * (4 * n + 1),
        out_shape=[jax.ShapeDtypeStruct(shp, F32) for shp in shapes for _ in range(4)] + [jax.ShapeDtypeStruct((1, 128), F32)],
        compiler_params=_cp(VMEM_BIG))(me, *gathered, loss_all, *flat_params)
    return [tuple(outs[4 * i:4 * i + 4]) for i in range(n)], outs[4 * n]


def _pack(arrays):
    pieces, layout, off = [], [], 0
    for a in arrays:
        n = a.size
        padded = -(-n // 1024) * 1024
        flat = a.reshape(-1).astype(F32)
        if padded != n:
            flat = jnp.pad(flat, (0, padded - n))
        pieces.append(flat.reshape(padded // 128, 128))
        layout.append((off, n, a.shape))
        off += padded // 128
    return jnp.concatenate(pieces, axis=0), layout


def kernel(x, mem, norm_mix, norm_ffn, mem_norm, w_kv, w_out, w_ffn1, w_ffn2, a_in, a_ln_g, a_ln_b, a_ws, a_bs, b_in, b_conv_w, b_conv_b, b_dt_bias, b_a_log, b_d, b_gnorm, final_norm, loss_target, m_norm_mix, m_norm_ffn, m_mem_norm, m_w_kv, m_w_out, m_w_ffn1, m_w_ffn2, m_a_in, m_a_ln_g, m_a_ln_b, m_a_ws, m_a_bs, m_b_in, m_b_conv_w, m_b_conv_b, m_b_dt_bias, m_b_a_log, m_b_d, m_b_gnorm, m_final_norm, v_norm_mix, v_norm_ffn, v_mem_norm, v_w_kv, v_w_out, v_w_ffn1, v_w_ffn2, v_a_in, v_a_ln_g, v_a_ln_b, v_a_ws, v_a_bs, v_b_in, v_b_conv_w, v_b_conv_b, v_b_dt_bias, v_b_a_log, v_b_d, v_b_gnorm, v_final_norm):
    s = x.shape[1]
    xs = x.reshape(s, D_MODEL)
    mems = mem.reshape(N_MEM, D_MODEL)
    target = loss_target.reshape(s, D_MODEL)
    ax, ay, ac = lax.axis_index("x"), lax.axis_index("y"), lax.axis_index("c")
    me = 4 * ax + 2 * ay + ac
    xyc = jnp.stack([ax, ay, ac]).astype(jnp.int32)
    me1 = me.astype(jnp.int32).reshape(1)

    b_cols = b_in.shape[2]
    act = lambda a: a.astype(_ACT)
    lay_f1, lay_f2 = ("col", 512, (1, D_MODEL, D_FF)), ("row", 512, (1, D_FF, D_MODEL))
    lay_out, lay_kv = ("row", 384, (1, 3 * D_MODEL, D_MODEL)), ("col", 256, (1, D_MODEL, 2 * X_WIDTH))
    small_w_pack = _pack([b_conv_w[0], b_conv_b[0], b_gnorm[0]])[0]
    (WA,) = _all_gather_seq([act(a_in)], [("col", 640, (1, D_MODEL, 5 * D_MODEL))], "ag_proj_a")
    wo0, wkv0 = _all_gather_seq([act(w_out[0:1]), act(w_kv[0:1])], [lay_out, lay_kv], "ag_out0")
    (w1_0,) = _all_gather_seq([act(w_ffn1[0:1])], [lay_f1], "ag_ffn0_up")
    (w2_0,) = _all_gather_seq([act(w_ffn2[0:1])], [lay_f2], "ag_ffn0_down")
    a0 = _rms_fwd(xs, norm_mix[0].reshape(1, -1), "mix_norm0")
    tr_b = lambda a: jnp.swapaxes(a, 1, 2)
    wbt_blk, small_w = _all_gather_seq(
        [act(tr_b(b_in)[0]), small_w_pack],
        [("blk", 0, (N_DEV, b_cols, D_MODEL)), ("blk", 0, (N_DEV, 32, 128))], "ag_proj_b", after=[a0])
    wo1, wkv1 = _all_gather_seq([act(w_out[1:2]), act(w_kv[1:2])], [lay_out, lay_kv], "ag_out1", after=[a0])
    w1_1, w2_1 = _all_gather_seq([act(w_ffn1[1:2]), act(w_ffn2[1:2])], [lay_f1, lay_f2], "ag_ffn1", after=[a0])
    W1, W2, WO, WKV = [w1_0, w1_1], [w2_0, w2_1], [wo0, wo1], [wkv0, wkv1]
    dt0 = D_INNER + CONV_DIM

    row = lambda a: a.reshape(1, -1)
    nmix = [row(norm_mix[0]), row(norm_mix[1])]
    nffn = [row(norm_ffn[0]), row(norm_ffn[1])]
    nmem = [row(mem_norm[0]), row(mem_norm[1])]
    fin = row(final_norm)
    lng, lnb = a_ln_g.reshape(1, D_INNER), a_ln_b.reshape(1, D_INNER)
    ws = a_ws[0]
    bs3 = a_bs[0].reshape(A_GROUPS, CHUNK, 1)
    pad_h = lambda a: jnp.pad(a.reshape(-1), (0, HPAD - SSM_HEADS))
    bias_row = pad_h(b_dt_bias).reshape(1, HPAD)
    alog_row = pad_h(b_a_log).reshape(1, HPAD)
    dfull = jnp.repeat(b_d.reshape(-1), SSM_P).reshape(1, D_INNER)

    kvs, mns = [None, None], [None, None]

    def mem_kv(i, after=None):
        gain = nmem[i] if after is None else _tie(nmem[i], after)
        mns[i] = _rms_fwd(mems, gain, f"mem_norm{i}")
        kvs[i] = _mm(mns[i], WKV[i], m=N_MEM, n=2 * X_WIDTH, k=D_MODEL, b_at=(0, 0, 0), out_dtype=_ACT, name=f"kv{i}")

    def ffn_up(h, i):
        f = _rms_fwd(h, nffn[i], f"ffn_norm{i}")
        return f, _mm(f, W1[i], m=s, n=D_FF, k=D_MODEL, b_at=(0, 0, 0), out_dtype=_ACT, name=f"ffn_up{i}")

    def ffn_down(h, p, i, after=()):
        return _mm(p, W2[i], m=s, n=D_MODEL, k=D_FF, b_at=(0, 0, 0), a_pro="relu2", add=h, after=after, name=f"ffn_down{i}")

    def out_proj(h, cat, i):
        return _mm(cat, WO[i], m=s, n=D_MODEL, k=3 * D_MODEL, b_at=(0, 0, 0), add=h, name=f"out_proj{i}")

    proj_a = _mm(a0, WA, m=s, n=5 * D_MODEL, k=D_MODEL, b_at=(0, 0, 0), name="proj_a")
    cat_a = _gmlp_fwd(proj_a, lng, lnb, ws, bs3, "gmlp_fwd")
    mem_kv(0, after=[cat_a])
    cat_a = _attn_fwd(proj_a, 4, kvs[0], cat_a, "attn_fwd0")
    h1 = out_proj(xs, cat_a, 0)

    f0, p0 = ffn_up(h1, 0)
    wbt_blk, small_w, _ = lax.optimization_barrier((wbt_blk, small_w, p0))
    jd, lo = divmod(dt0, b_cols)
    assert lo + SSM_HEADS <= b_cols
    wbt_full = wbt_blk.reshape(N_DEV * b_cols, D_MODEL)
    WBT = jnp.concatenate([wbt_full[:dt0], wbt_full[dt0 + SSM_HEADS:]], axis=0)
    WBDT = jnp.pad(wbt_full[dt0:dt0 + SSM_HEADS], ((0, HPAD - SSM_HEADS), (0, 0)))
    cw_sh, cb_sh, gn_sh = 4 * 384, 384, 256
    sw = small_w.reshape(N_DEV, 32 * 128)
    conv_w = jnp.transpose(sw[:, :cw_sh].reshape(N_DEV, CONV_K, 384), (1, 0, 2)).reshape(CONV_K, CONV_DIM)
    conv_b = sw[:, 2048:2048 + cb_sh].reshape(1, CONV_DIM)
    gnorm = sw[:, 3072:3072 + gn_sh].reshape(1, D_INNER)

    h2 = ffn_down(h1, p0, 0, after=[WBT, WBDT])
    a1 = _rms_fwd(h2, nmix[1], "mix_norm1")
    proj_b = _mm(a1, WBT, m=s, n=6 * D_MODEL, k=D_MODEL, tb=True, name="proj_b")
    dt_raw = _mm(a1, WBDT, m=s, n=HPAD, k=D_MODEL, tb=True, name="proj_dt")
    xbc = _conv_fwd(proj_b, conv_w, conv_b, "conv_fwd")
    y_ssd, states, *ssd_expansions = _ssd_fwd(xbc, dt_raw, bias_row, alog_row, dfull, "ssd_fwd")
    cat_b = _gate_fwd(y_ssd, proj_b, gnorm, "gate_fwd")
    mem_kv(1, after=[cat_b])
    cat_b = _attn_fwd(proj_b, 5, kvs[1], cat_b, "attn_fwd1")
    h3 = out_proj(h2, cat_b, 1)
    f1, p1 = ffn_up(h3, 1)
    h4 = ffn_down(h3, p1, 1)

    loss_part, dh, dh_act, d_fin = _loss_head(h4, fin, target, "loss_head")

    g_f1, g_f2, g_out, g_kv = [None, None], [None, None], [None, None], [None, None]
    d_nffn, d_nmix, d_nmem = [None, None], [None, None], [None, None]

    def ffn_bwd(dh, dh_act, h_in, f, p, i, after=(), after_last=()):
        dp = _mm(dh_act, W2[i], m=s, n=D_FF, k=D_MODEL, tb=True, b_at=(0, 0, 0), epi_p=p, out_dtype=_ACT, name=f"ffn_down_dx{i}")
        g_f2[i] = _mm(p, dh_act, m=D_FF, n=D_MODEL, k=s, ta=True, a_pro="relu2", out_dtype=_ACT, name=f"ffn_down_dw{i}")
        g_f1[i] = _mm(f, dp, m=D_MODEL, n=D_FF, k=s, ta=True, out_dtype=_ACT, name=f"ffn_up_dw{i}")
        df = _mm(dp, W1[i], m=s, n=D_MODEL, k=D_FF, tb=True, b_at=(0, 0, 0), after=after, name=f"ffn_up_dx{i}")
        gain = _tie(nffn[i], after_last) if after_last else nffn[i]
        dh_in, dh_in_act, d_nffn[i] = _rms_bwd(h_in, gain, df, dh, f"ffn_norm_bwd{i}")
        return dh_in, dh_in_act

    def out_bwd(dh_act, cat, i):
        dcat = _mm(dh_act, WO[i], m=s, n=3 * D_MODEL, k=D_MODEL, tb=True, b_at=(0, 0, 0), out_dtype=_ACT, name=f"out_dx{i}")
        g_out[i] = _mm(cat, dh_act, m=3 * D_MODEL, n=D_MODEL, k=s, ta=True, out_dtype=_ACT, name=f"out_dw{i}")
        return dcat

    def mem_bwd(dkv, i):
        g_kv[i] = _mm(mns[i], dkv, m=D_MODEL, n=2 * X_WIDTH, k=N_MEM, ta=True, out_dtype=_ACT, name=f"kv_dw{i}")
        dmn = _mm(dkv, WKV[i], m=N_MEM, n=D_MODEL, k=2 * X_WIDTH, tb=True, b_at=(0, 0, 0), name=f"kv_dx{i}")
        _, _, d_nmem[i] = _rms_bwd(mems, nmem[i], dmn, None, f"mem_norm_bwd{i}")

    lay_g = {"f1": ("col", 512, (D_MODEL, 512)), "f2": ("row", 512, (512, D_MODEL)), "out": ("row", 384, (384, D_MODEL)),
             "kv": ("col", 256, (D_MODEL, 256)), "a": ("col", 640, (D_MODEL, 640)), "b": ("blk", 0, (b_cols, D_MODEL))}
    reduced = {}

    def reduce_scatter(group, tag, after=(), sums_after=()):
        grads3, lays3 = [], []
        for fam, _, g in group:
            kind, width, shape = lay_g[fam]
            grads3.append(g if kind == "blk" else g.reshape((1,) + g.shape))
            lays3.append((kind, width, shape if kind == "blk" else (1,) + shape))
        recv1 = _rs_to_sibling(grads3, lays3, f"rs_sibling_{tag}", after)
        if sums_after:
            recv1 = lax.optimization_barrier((tuple(recv1), tuple(sums_after)))[0]
        recv1 = [recv1[t].reshape((4,) + lay_g[fam][2]) for t, (fam, _, _) in enumerate(group)]
        parts = [_rs_chip_sum(g, r1, lay_g[fam], xyc, f"rs_chip_sum_{fam}{i}") for r1, (fam, i, g) in zip(recv1, group)]
        recv2 = _rs_across_chips(parts, f"rs_chips_{tag}")
        for (fam, i, g), r1, r2 in zip(group, recv1, recv2):
            reduced[fam, i] = (g, r1, r2)
        return parts, recv2

    dh3, dh3_act = ffn_bwd(dh, dh_act, h3, f1, p1, 1)
    dcat_b = out_bwd(dh3_act, cat_b, 1)
    sums, got_ffn1 = reduce_scatter([("f1", 1, g_f1[1]), ("f2", 1, g_f2[1]), ("out", 1, g_out[1])], "ffn1", sums_after=[dcat_b])
    dy_ssd, dproj_b, d_gnorm = _gate_bwd(y_ssd, proj_b, gnorm, dcat_b, "gate_bwd")
    dproj_b, dkv_b = _attn_bwd(proj_b, 5, kvs[1], dcat_b, dproj_b, "attn_bwd1")
    mem_bwd(dkv_b, 1)
    dxbc, ddt_raw, d_alog, d_dskip, d_dtbias = _ssd_bwd(
        xbc, dt_raw, _tie(bias_row, sums), alog_row, dfull, dy_ssd, states, ssd_expansions, "ssd_bwd")
    dproj_b, d_convw, d_convb = _conv_bwd(proj_b, conv_w, _tie(conv_b, got_ffn1), dxbc, dproj_b, "conv_bwd")
    gb = _mm(dproj_b, a1, m=6 * D_MODEL, n=D_MODEL, k=s, ta=True, out_dtype=_ACT, name="proj_b_dw")
    gb_dt = _mm(ddt_raw, a1, m=HPAD, n=D_MODEL, k=s, ta=True, out_dtype=_ACT, name="proj_b_dw_dt")
    blocks_b = [gb[j * b_cols:(j + 1) * b_cols] for j in range(jd)]
    blocks_b.append(jnp.concatenate([gb[jd * b_cols:dt0], gb_dt[:SSM_HEADS], gb[dt0:(jd + 1) * b_cols - SSM_HEADS]], axis=0))
    blocks_b += [gb[j * b_cols - SSM_HEADS:(j + 1) * b_cols - SSM_HEADS] for j in range(jd + 1, N_DEV)]
    gb_blk = jnp.stack(blocks_b)
    da1 = _mm(dproj_b, WBT, m=s, n=D_MODEL, k=6 * D_MODEL, name="proj_b_dx")
    sums, got_mix1 = reduce_scatter([("kv", 1, g_kv[1]), ("b", 0, gb_blk)], "mix1", sums_after=[da1])
    da1 = _mm(ddt_raw, WBDT, m=s, n=D_MODEL, k=HPAD, add=da1, name="proj_b_dx_dt")
    dh2, dh2_act, d_nmix[1] = _rms_bwd(h2, _tie(nmix[1], sums), da1, dh3, "mix_norm_bwd1")

    dh1, dh1_act = ffn_bwd(dh2, dh2_act, h1, f0, p0, 0, after=got_ffn1, after_last=got_mix1)
    dcat_a = out_bwd(dh1_act, cat_a, 0)
    sums, got_ffn0 = reduce_scatter([("f1", 0, g_f1[0]), ("f2", 0, g_f2[0]), ("out", 0, g_out[0])], "ffn0", sums_after=[dcat_a])
    dproj_a, d_ws, d_bs3, d_lng, d_lnb = _gmlp_bwd(proj_a, dcat_a, _tie(lng, sums), lnb, ws, bs3, "gmlp_bwd")
    dproj_a, dkv_a = _attn_bwd(proj_a, 4, kvs[0], dcat_a, dproj_a, "attn_bwd0")
    mem_bwd(dkv_a, 0)

    def big_update(w, m, v, fam, nlayer):
        res = None
        for i in range(nlayer):
            grad, recv1, recv2 = reduced[fam, i]
            plist = [(recv1, 0), (recv2, 0), (recv2, 1), (recv2, 2)]
            res = _adamw(w, m, v, (grad, lay_g[fam][0]), plist, me1, f"adamw_{fam}{i}", layer=i, prev=res)
        return res

    da0 = _mm(dproj_a, WA, m=s, n=D_MODEL, k=5 * D_MODEL, tb=True, b_at=(0, 0, 0), name="proj_a_dx")
    grad_x, _, d_nmix[0] = _rms_bwd(xs, nmix[0], da0, dh1, "mix_norm_bwd0")
    ga = _mm(a0, dproj_a, m=D_MODEL, n=5 * D_MODEL, k=s, ta=True, out_dtype=_ACT, after=[grad_x], name="proj_a_dw")
    r_b = big_update(tr_b(b_in), tr_b(m_b_in), tr_b(v_b_in), "b", 1)
    reduce_scatter([("kv", 0, g_kv[0]), ("a", 0, ga)], "mix0", after=got_ffn0, sums_after=r_b)
    r_b = [tr_b(o) for o in r_b]

    small_names = ["norm_mix", "norm_ffn", "mem_norm", "a_ln_g", "a_ln_b", "a_ws", "a_bs", "b_dt_bias", "b_a_log", "b_d",
                   "final_norm", "b_conv_w", "b_conv_b", "b_gnorm"]
    small_grads = [jnp.concatenate(d_nmix, axis=0), jnp.concatenate(d_nffn, axis=0), jnp.concatenate(d_nmem, axis=0),
                   d_lng, d_lnb, d_ws.reshape(A_GROUPS * CHUNK, CHUNK), d_bs3.reshape(A_GROUPS, CHUNK),
                   d_dtbias[:, :SSM_HEADS], d_alog[:, :SSM_HEADS], d_dskip[:, :SSM_HEADS], d_fin,
                   d_convw, d_convb, d_gnorm]
    small_2d = [(2, D_MODEL)] * 3 + [(1, D_INNER)] * 2 + [(A_GROUPS * CHUNK, CHUNK), (A_GROUPS, CHUNK)] + [(1, SSM_HEADS)] * 3 \
        + [(1, D_MODEL), (CONV_K, 384), (1, 384), (1, 256)]
    gathered = _all_gather_seq(
        small_grads + [loss_part], [("blk", 0, (N_DEV,) + g.shape) for g in small_grads + [loss_part]], "ag_small_grads")

    r_f1 = big_update(w_ffn1, m_w_ffn1, v_w_ffn1, "f1", 2)
    r_f2 = big_update(w_ffn2, m_w_ffn2, v_w_ffn2, "f2", 2)
    r_out = big_update(w_out, m_w_out, v_w_out, "out", 2)
    r_kv = big_update(w_kv, m_w_kv, v_w_kv, "kv", 2)
    r_a = big_update(a_in, m_a_in, v_a_in, "a", 1)

    small_w = [norm_mix, norm_ffn, mem_norm, a_ln_g, a_ln_b, a_ws, a_bs, b_dt_bias, b_a_log, b_d, final_norm,
               b_conv_w, b_conv_b, b_gnorm]
    small_m = [m_norm_mix, m_norm_ffn, m_mem_norm, m_a_ln_g, m_a_ln_b, m_a_ws, m_a_bs, m_b_dt_bias, m_b_a_log, m_b_d,
               m_final_norm, m_b_conv_w, m_b_conv_b, m_b_gnorm]
    small_v = [v_norm_mix, v_norm_ffn, v_mem_norm, v_a_ln_g, v_a_ln_b, v_a_ws, v_a_bs, v_b_dt_bias, v_b_a_log, v_b_d,
               v_final_norm, v_b_conv_w, v_b_conv_b, v_b_gnorm]
    params = [tuple(a.reshape(shp) for a in wmv) for shp, wmv in zip(small_2d, zip(small_w, small_m, small_v))]
    loss_all = _tie(gathered[-1], [r_a[0], r_kv[0]])
    small_res, loss_sum = _small_update(gathered[:-1], params, loss_all, me1, "adamw_small")
    loss = loss_sum[0, 0]

    names = ["norm_mix", "norm_ffn", "mem_norm", "w_kv", "w_out", "w_ffn1", "w_ffn2", "a_in", "a_ln_g", "a_ln_b", "a_ws",
             "a_bs", "b_in", "b_conv_w", "b_conv_b", "b_dt_bias", "b_a_log", "b_d", "b_gnorm", "final_norm"]
    big = {"w_kv": r_kv, "w_out": r_out, "w_ffn1": r_f1, "w_ffn2": r_f2, "a_in": r_a, "b_in": r_b}
    outs = [loss, grad_x.reshape(x.shape)]
    for kind in range(4):
        for nm in names:
            if nm in big:
                outs.append(big[nm][kind])
            else:
                i = small_names.index(nm)
                outs.append(small_res[i][kind].reshape(small_w[i].shape))
    return tuple(outs)
```
